```python
import jax, jax.numpy as jnp
from jax import lax
import numpy as np

D_MODEL = 2048
BATCH = 8
SEQ = 4096
DEPTH = 1

CONV_WIDTH = 3
CONV_GROUPS = 8
D_CONV = D_MODEL // 2
HGRN_HEADS = 8
HGRN_DK = 128
HGRN_DV = (D_MODEL // 2) // HGRN_HEADS
D_HGRN_K = HGRN_HEADS * HGRN_DK
D_HGRN_V = HGRN_HEADS * HGRN_DV
CHUNK = 64
D_FF = ((8 * D_MODEL // 3 + 255) // 256) * 256
N_MOD = 6
EPS = 1e-6
SPLITS = (D_CONV, D_CONV, D_CONV, D_HGRN_K, D_HGRN_K, D_HGRN_V, D_HGRN_V, D_MODEL, D_MODEL)
D_IN = sum(SPLITS)

kernel_name = "hybrid_conv_hgrn2_gated_merge_block"


def rmsnorm(x, g):
    xf = x.astype(jnp.float32)
    y = xf * lax.rsqrt(jnp.mean(xf * xf, axis=-1, keepdims=True) + EPS)
    return (y * g.astype(jnp.float32)).astype(x.dtype)


def causal_depthwise_conv(u, w):
    s = u.shape[1]
    upad = jnp.pad(u, ((0, 0), (CONV_WIDTH - 1, 0), (0, 0)))
    y = w[0] * upad[:, 0:s]
    for k in range(1, CONV_WIDTH):
        y = y + w[k] * upad[:, k:k + s]
    return y


def to_chunks(t):
    b, s, h, d = t.shape
    return t.reshape(b, s // CHUNK, CHUNK, h, d).transpose(1, 0, 3, 2, 4)


def from_chunks(t):
    n, b, h, c, d = t.shape
    return t.transpose(1, 0, 3, 2, 4).reshape(b, n * c, h, d)


def hgrn2_chunked(q, k, v, log_f):
    qc, kc, vc = to_chunks(q), to_chunks(k), to_chunks(v)
    bc = jnp.cumsum(to_chunks(log_f), axis=3)
    causal = jnp.tril(jnp.ones((CHUNK, CHUNK), dtype=bool))[:, :, None]
    b_, h_ = q.shape[0], q.shape[2]
    s0 = jnp.zeros((b_, h_, HGRN_DK, HGRN_DV), jnp.float32)

    def step(state, xs):
        qi, ki, vi, bi = xs
        b_last = bi[:, :, -1:, :]
        o_inter = jnp.einsum('bhtk,bhkv->bhtv', qi * jnp.exp(bi), state)
        diff = bi[:, :, :, None, :] - bi[:, :, None, :, :]
        decay = jnp.exp(jnp.where(causal, diff, -jnp.inf))
        scores = jnp.einsum('bhtk,bhtsk,bhsk->bhts', qi, decay, ki)
        o_intra = jnp.einsum('bhts,bhsv->bhtv', scores, vi)
        new_state = (jnp.swapaxes(jnp.exp(b_last), -1, -2) * state
                     + jnp.einsum('bhsk,bhsv->bhkv', ki * jnp.exp(b_last - bi), vi))
        return new_state, o_inter + o_intra

    _, o = lax.scan(step, s0, (qc, kc, vc, bc))
    return from_chunks(o)


def _fwd_setup_inputs(seed: int = 0) -> dict:
    key = jax.random.key(seed)
    ks = jax.random.split(key, 20)

    def nrm(k, shape, fan_in):
        return jax.random.normal(k, shape, jnp.float32) * (fan_in ** -0.5)

    def gain(k, shape):
        return 1.0 + 0.02 * jax.random.normal(k, shape, jnp.float32)

    return {
        "x": jax.random.normal(ks[0], (BATCH, SEQ, D_MODEL), jnp.float32),
        "c": jax.random.normal(ks[1], (BATCH, D_MODEL), jnp.float32),
        "w_ada": nrm(ks[2], (DEPTH, D_MODEL, N_MOD * D_MODEL), D_MODEL) * 0.5,
        "b_ada": 0.02 * jax.random.normal(ks[3], (DEPTH, N_MOD * D_MODEL), jnp.float32),
        "norm_mix_g": gain(ks[4], (DEPTH, D_MODEL)),
        "w_in": nrm(ks[5], (DEPTH, D_MODEL, D_IN), D_MODEL),
        "conv_w": nrm(ks[6], (DEPTH, CONV_WIDTH, D_CONV), CONV_WIDTH),
        "lb_param": jax.random.normal(ks[7], (DEPTH + 1, D_HGRN_K), jnp.float32),
        "gnorm_g": gain(ks[8], (DEPTH, HGRN_DV)),
        "w_conv_out": nrm(ks[9], (DEPTH, D_CONV, D_MODEL), D_CONV),
        "w_hgrn_out": nrm(ks[10], (DEPTH, D_HGRN_V, D_MODEL), D_HGRN_V),
        "w_o": nrm(ks[11], (DEPTH, D_MODEL, D_MODEL), D_MODEL),
        "norm_ffn_g": gain(ks[12], (DEPTH, D_MODEL)),
        "w_ffn_gate": nrm(ks[13], (DEPTH, D_MODEL, D_FF), D_MODEL),
        "w_ffn_up": nrm(ks[14], (DEPTH, D_MODEL, D_FF), D_MODEL),
        "w_ffn_down": nrm(ks[15], (DEPTH, D_FF, D_MODEL), D_FF),
        "norm_final_g": gain(ks[16], (D_MODEL,)),
    }


def _fwd_reference(x, c, w_ada, b_ada, norm_mix_g, w_in, conv_w, lb_param, gnorm_g,
              w_conv_out, w_hgrn_out, w_o, norm_ffn_g, w_ffn_gate, w_ffn_up, w_ffn_down,
              norm_final_g):
    b, s, _ = x.shape
    lb_all = jnp.cumsum(jax.nn.softmax(lb_param.astype(jnp.float32), axis=0), axis=0)
    split_idx = list(np.cumsum(SPLITS)[:-1])
    c_act = jax.nn.silu(c)

    for l in range(DEPTH):
        mod = c_act @ w_ada[l] + b_ada[l]
        sh_m, sc_m, gt_m, sh_f, sc_f, gt_f = [m[:, None, :] for m in jnp.split(mod, N_MOD, axis=-1)]

        h = rmsnorm(x, norm_mix_g[l]) * (1.0 + sc_m) + sh_m
        proj = h @ w_in[l]
        a_b, a_c, a_x, q, f_logit, i_in, g_out, gate_a, gate_b = jnp.split(proj, split_idx, axis=-1)

        y_a = (a_b * causal_depthwise_conv(a_c * a_x, conv_w[l])) @ w_conv_out[l]

        lb = lb_all[l]
        f = lb + (1.0 - lb) * jax.nn.sigmoid(f_logit.astype(jnp.float32))
        qh = jax.nn.silu(q.astype(jnp.float32)).reshape(b, s, HGRN_HEADS, HGRN_DK)
        kh = (1.0 - f).reshape(b, s, HGRN_HEADS, HGRN_DK)
        log_f = jnp.log(f).reshape(b, s, HGRN_HEADS, HGRN_DK)
        vh = i_in.astype(jnp.float32).reshape(b, s, HGRN_HEADS, HGRN_DV)
        o = hgrn2_chunked(qh, kh, vh, log_f)
        o = rmsnorm(o, gnorm_g[l]) * jax.nn.silu(g_out.astype(jnp.float32)).reshape(b, s, HGRN_HEADS, HGRN_DV)
        y_b = o.reshape(b, s, D_HGRN_V).astype(x.dtype) @ w_hgrn_out[l]

        merged = jax.nn.sigmoid(gate_a) * y_a + jax.nn.sigmoid(gate_b) * y_b
        x = x + gt_m * (merged @ w_o[l])

        h2 = rmsnorm(x, norm_ffn_g[l]) * (1.0 + sc_f) + sh_f
        ff = (jax.nn.silu(h2 @ w_ffn_gate[l]) * (h2 @ w_ffn_up[l])) @ w_ffn_down[l]
        x = x + gt_f * ff

    return rmsnorm(x, norm_final_g)


import jax as _jax
import jax.numpy as _jnp

TWIN_FORMAT = 'train_step'
FWD_PARAMS = ['x', 'c', 'w_ada', 'b_ada', 'norm_mix_g', 'w_in', 'conv_w', 'lb_param', 'gnorm_g', 'w_conv_out', 'w_hgrn_out', 'w_o', 'norm_ffn_g', 'w_ffn_gate', 'w_ffn_up', 'w_ffn_down', 'norm_final_g']
TWIN_WEIGHTS = ['w_ada', 'b_ada', 'norm_mix_g', 'w_in', 'conv_w', 'lb_param', 'gnorm_g', 'w_conv_out', 'w_hgrn_out', 'w_o', 'norm_ffn_g', 'w_ffn_gate', 'w_ffn_up', 'w_ffn_down', 'norm_final_g']
TWIN_DIFF_INPUT = 'x'
TWIN_INPUTS = ['x', 'c', 'w_ada', 'b_ada', 'norm_mix_g', 'w_in', 'conv_w', 'lb_param', 'gnorm_g', 'w_conv_out', 'w_hgrn_out', 'w_o', 'norm_ffn_g', 'w_ffn_gate', 'w_ffn_up', 'w_ffn_down', 'norm_final_g', 'loss_target', 'm_w_ada', 'm_b_ada', 'm_norm_mix_g', 'm_w_in', 'm_conv_w', 'm_lb_param', 'm_gnorm_g', 'm_w_conv_out', 'm_w_hgrn_out', 'm_w_o', 'm_norm_ffn_g', 'm_w_ffn_gate', 'm_w_ffn_up', 'm_w_ffn_down', 'm_norm_final_g', 'v_w_ada', 'v_b_ada', 'v_norm_mix_g', 'v_w_in', 'v_conv_w', 'v_lb_param', 'v_gnorm_g', 'v_w_conv_out', 'v_w_hgrn_out', 'v_w_o', 'v_norm_ffn_g', 'v_w_ffn_gate', 'v_w_ffn_up', 'v_w_ffn_down', 'v_norm_final_g']
TWIN_OUTPUTS = ['loss', 'grad_x', 'grad_w_ada', 'grad_b_ada', 'grad_norm_mix_g', 'grad_w_in', 'grad_conv_w', 'grad_lb_param', 'grad_gnorm_g', 'grad_w_conv_out', 'grad_w_hgrn_out', 'grad_w_o', 'grad_norm_ffn_g', 'grad_w_ffn_gate', 'grad_w_ffn_up', 'grad_w_ffn_down', 'grad_norm_final_g', 'delta_w_ada', 'delta_b_ada', 'delta_norm_mix_g', 'delta_w_in', 'delta_conv_w', 'delta_lb_param', 'delta_gnorm_g', 'delta_w_conv_out', 'delta_w_hgrn_out', 'delta_w_o', 'delta_norm_ffn_g', 'delta_w_ffn_gate', 'delta_w_ffn_up', 'delta_w_ffn_down', 'delta_norm_final_g', 'new_m_w_ada', 'new_m_b_ada', 'new_m_norm_mix_g', 'new_m_w_in', 'new_m_conv_w', 'new_m_lb_param', 'new_m_gnorm_g', 'new_m_w_conv_out', 'new_m_w_hgrn_out', 'new_m_w_o', 'new_m_norm_ffn_g', 'new_m_w_ffn_gate', 'new_m_w_ffn_up', 'new_m_w_ffn_down', 'new_m_norm_final_g', 'new_v_w_ada', 'new_v_b_ada', 'new_v_norm_mix_g', 'new_v_w_in', 'new_v_conv_w', 'new_v_lb_param', 'new_v_gnorm_g', 'new_v_w_conv_out', 'new_v_w_hgrn_out', 'new_v_w_o', 'new_v_norm_ffn_g', 'new_v_w_ffn_gate', 'new_v_w_ffn_up', 'new_v_w_ffn_down', 'new_v_norm_final_g']
TWIN_LEAF_KINDS = {'loss': 'loss', 'grad_x': 'grad_x', 'grad_w_ada': 'grad_w', 'grad_b_ada': 'grad_w', 'grad_norm_mix_g': 'grad_w', 'grad_w_in': 'grad_w', 'grad_conv_w': 'grad_w', 'grad_lb_param': 'grad_w', 'grad_gnorm_g': 'grad_w', 'grad_w_conv_out': 'grad_w', 'grad_w_hgrn_out': 'grad_w', 'grad_w_o': 'grad_w', 'grad_norm_ffn_g': 'grad_w', 'grad_w_ffn_gate': 'grad_w', 'grad_w_ffn_up': 'grad_w', 'grad_w_ffn_down': 'grad_w', 'grad_norm_final_g': 'grad_w', 'delta_w_ada': 'delta_w', 'delta_b_ada': 'delta_w', 'delta_norm_mix_g': 'delta_w', 'delta_w_in': 'delta_w', 'delta_conv_w': 'delta_w', 'delta_lb_param': 'delta_w', 'delta_gnorm_g': 'delta_w', 'delta_w_conv_out': 'delta_w', 'delta_w_hgrn_out': 'delta_w', 'delta_w_o': 'delta_w', 'delta_norm_ffn_g': 'delta_w', 'delta_w_ffn_gate': 'delta_w', 'delta_w_ffn_up': 'delta_w', 'delta_w_ffn_down': 'delta_w', 'delta_norm_final_g': 'delta_w', 'new_m_w_ada': 'new_m', 'new_m_b_ada': 'new_m', 'new_m_norm_mix_g': 'new_m', 'new_m_w_in': 'new_m', 'new_m_conv_w': 'new_m', 'new_m_lb_param': 'new_m', 'new_m_gnorm_g': 'new_m', 'new_m_w_conv_out': 'new_m', 'new_m_w_hgrn_out': 'new_m', 'new_m_w_o': 'new_m', 'new_m_norm_ffn_g': 'new_m', 'new_m_w_ffn_gate': 'new_m', 'new_m_w_ffn_up': 'new_m', 'new_m_w_ffn_down': 'new_m', 'new_m_norm_final_g': 'new_m', 'new_v_w_ada': 'new_v', 'new_v_b_ada': 'new_v', 'new_v_norm_mix_g': 'new_v', 'new_v_w_in': 'new_v', 'new_v_conv_w': 'new_v', 'new_v_lb_param': 'new_v', 'new_v_gnorm_g': 'new_v', 'new_v_w_conv_out': 'new_v', 'new_v_w_hgrn_out': 'new_v', 'new_v_w_o': 'new_v', 'new_v_norm_ffn_g': 'new_v', 'new_v_w_ffn_gate': 'new_v', 'new_v_w_ffn_up': 'new_v', 'new_v_w_ffn_down': 'new_v', 'new_v_norm_final_g': 'new_v'}


def _forward(args):
    return _fwd_reference(*[args[k] for k in FWD_PARAMS])


def _output_shape():
    def fwd():
        inp = _fwd_setup_inputs(0)
        return _fwd_reference(*[inp[k] for k in FWD_PARAMS])
    out = _jax.eval_shape(fwd)
    return out.shape, out.dtype

N_MICROBATCH = 1
ADAM_LR = 0.001
ADAM_B1 = 0.9
ADAM_B2 = 0.999
ADAM_EPS = 1e-08
ADAM_WD = 0.01
ADAM_STEP = 10
PER_EXAMPLE_BATCH_AXIS = {'x': 0, 'c': 0, 'loss_target': 0}
SHARED_INPUTS = []
_WEIGHT_DTYPES = {'w_ada': _jnp.float32, 'b_ada': _jnp.float32, 'norm_mix_g': _jnp.float32, 'w_in': _jnp.float32, 'conv_w': _jnp.float32, 'lb_param': _jnp.float32, 'gnorm_g': _jnp.float32, 'w_conv_out': _jnp.float32, 'w_hgrn_out': _jnp.float32, 'w_o': _jnp.float32, 'norm_ffn_g': _jnp.float32, 'w_ffn_gate': _jnp.float32, 'w_ffn_up': _jnp.float32, 'w_ffn_down': _jnp.float32, 'norm_final_g': _jnp.float32}
MOMENT_SCALE = {'w_ada': 2.807578e-02, 'b_ada': 4.772431e-02, 'norm_mix_g': 3.383325e-02, 'w_in': 1.522132e-02, 'conv_w': 2.640609e-02, 'lb_param': 8.882683e-04, 'gnorm_g': 5.062107e-02, 'w_conv_out': 1.840453e-02, 'w_hgrn_out': 9.494003e-03, 'w_o': 2.069565e-02, 'norm_ffn_g': 2.510185e-02, 'w_ffn_gate': 1.146622e-02, 'w_ffn_up': 1.109955e-02, 'w_ffn_down': 1.841579e-02, 'norm_final_g': 1.600155e+01}


def _to_microbatches(a, axis):
    t = _jnp.moveaxis(a, axis, 0)
    t = t.reshape((N_MICROBATCH, t.shape[0] // N_MICROBATCH) + t.shape[1:])
    return _jnp.moveaxis(t, 1, axis + 1)


def setup_inputs(seed: int = 0) -> dict:
    inp = _fwd_setup_inputs(seed)
    key = _jax.random.fold_in(_jax.random.key(seed), 7919)
    shape, _ = _output_shape()
    out = dict(inp)
    out["loss_target"] = _jax.random.normal(_jax.random.fold_in(key, 0), shape, _jnp.float32)
    for i, name in enumerate(TWIN_WEIGHTS):
        w = inp[name].astype(_jnp.float32)
        if MOMENT_SCALE is None:
            s = _jnp.sqrt(_jnp.mean(_jnp.square(w)) + 1e-30)
        else:
            s = MOMENT_SCALE[name]
        km, kv = _jax.random.split(_jax.random.fold_in(key, i + 1))
        out[name] = w
        out["m_" + name] = s * _jax.random.normal(km, w.shape, _jnp.float32)
        out["v_" + name] = (s * s) * _jax.random.uniform(kv, w.shape, _jnp.float32, 0.5, 1.5)
    if N_MICROBATCH > 1:
        for name, axis in PER_EXAMPLE_BATCH_AXIS.items():
            out[name] = _to_microbatches(out[name], axis)
    return {'x': out['x'], 'c': out['c'], 'w_ada': out['w_ada'], 'b_ada': out['b_ada'], 'norm_mix_g': out['norm_mix_g'], 'w_in': out['w_in'], 'conv_w': out['conv_w'], 'lb_param': out['lb_param'], 'gnorm_g': out['gnorm_g'], 'w_conv_out': out['w_conv_out'], 'w_hgrn_out': out['w_hgrn_out'], 'w_o': out['w_o'], 'norm_ffn_g': out['norm_ffn_g'], 'w_ffn_gate': out['w_ffn_gate'], 'w_ffn_up': out['w_ffn_up'], 'w_ffn_down': out['w_ffn_down'], 'norm_final_g': out['norm_final_g'], 'loss_target': out['loss_target'], 'm_w_ada': out['m_w_ada'], 'm_b_ada': out['m_b_ada'], 'm_norm_mix_g': out['m_norm_mix_g'], 'm_w_in': out['m_w_in'], 'm_conv_w': out['m_conv_w'], 'm_lb_param': out['m_lb_param'], 'm_gnorm_g': out['m_gnorm_g'], 'm_w_conv_out': out['m_w_conv_out'], 'm_w_hgrn_out': out['m_w_hgrn_out'], 'm_w_o': out['m_w_o'], 'm_norm_ffn_g': out['m_norm_ffn_g'], 'm_w_ffn_gate': out['m_w_ffn_gate'], 'm_w_ffn_up': out['m_w_ffn_up'], 'm_w_ffn_down': out['m_w_ffn_down'], 'm_norm_final_g': out['m_norm_final_g'], 'v_w_ada': out['v_w_ada'], 'v_b_ada': out['v_b_ada'], 'v_norm_mix_g': out['v_norm_mix_g'], 'v_w_in': out['v_w_in'], 'v_conv_w': out['v_conv_w'], 'v_lb_param': out['v_lb_param'], 'v_gnorm_g': out['v_gnorm_g'], 'v_w_conv_out': out['v_w_conv_out'], 'v_w_hgrn_out': out['v_w_hgrn_out'], 'v_w_o': out['v_w_o'], 'v_norm_ffn_g': out['v_norm_ffn_g'], 'v_w_ffn_gate': out['v_w_ffn_gate'], 'v_w_ffn_up': out['v_w_ffn_up'], 'v_w_ffn_down': out['v_w_ffn_down'], 'v_norm_final_g': out['v_norm_final_g']}


def _loss(weights, diff, rest, loss_target):
    with _jax.named_scope("forward"):
        args = {**rest, TWIN_DIFF_INPUT: diff, **{k: w.astype(_WEIGHT_DTYPES[k]) for k, w in weights.items()}}
        y = _forward(args)
    with _jax.named_scope("loss_head"):
        err = _jnp.square(y.astype(_jnp.float32) - loss_target)
        return 0.5 * _jnp.sum(_jnp.mean(err, axis=-1)) if err.ndim else 0.5 * err


def _adamw(w, g, m, v):
    m = ADAM_B1 * m + (1.0 - ADAM_B1) * g
    v = ADAM_B2 * v + (1.0 - ADAM_B2) * _jnp.square(g)
    m_hat = m / (1.0 - ADAM_B1 ** ADAM_STEP)
    v_hat = v / (1.0 - ADAM_B2 ** ADAM_STEP)
    delta = -ADAM_LR * (m_hat / (_jnp.sqrt(v_hat) + ADAM_EPS) + ADAM_WD * w)
    return delta, m, v


def reference(x, c, w_ada, b_ada, norm_mix_g, w_in, conv_w, lb_param, gnorm_g, w_conv_out, w_hgrn_out, w_o, norm_ffn_g, w_ffn_gate, w_ffn_up, w_ffn_down, norm_final_g, loss_target, m_w_ada, m_b_ada, m_norm_mix_g, m_w_in, m_conv_w, m_lb_param, m_gnorm_g, m_w_conv_out, m_w_hgrn_out, m_w_o, m_norm_ffn_g, m_w_ffn_gate, m_w_ffn_up, m_w_ffn_down, m_norm_final_g, v_w_ada, v_b_ada, v_norm_mix_g, v_w_in, v_conv_w, v_lb_param, v_gnorm_g, v_w_conv_out, v_w_hgrn_out, v_w_o, v_norm_ffn_g, v_w_ffn_gate, v_w_ffn_up, v_w_ffn_down, v_norm_final_g):
    given = dict(x=x, c=c, w_ada=w_ada, b_ada=b_ada, norm_mix_g=norm_mix_g, w_in=w_in, conv_w=conv_w, lb_param=lb_param, gnorm_g=gnorm_g, w_conv_out=w_conv_out, w_hgrn_out=w_hgrn_out, w_o=w_o, norm_ffn_g=norm_ffn_g, w_ffn_gate=w_ffn_gate, w_ffn_up=w_ffn_up, w_ffn_down=w_ffn_down, norm_final_g=norm_final_g, loss_target=loss_target, m_w_ada=m_w_ada, m_b_ada=m_b_ada, m_norm_mix_g=m_norm_mix_g, m_w_in=m_w_in, m_conv_w=m_conv_w, m_lb_param=m_lb_param, m_gnorm_g=m_gnorm_g, m_w_conv_out=m_w_conv_out, m_w_hgrn_out=m_w_hgrn_out, m_w_o=m_w_o, m_norm_ffn_g=m_norm_ffn_g, m_w_ffn_gate=m_w_ffn_gate, m_w_ffn_up=m_w_ffn_up, m_w_ffn_down=m_w_ffn_down, m_norm_final_g=m_norm_final_g, v_w_ada=v_w_ada, v_b_ada=v_b_ada, v_norm_mix_g=v_norm_mix_g, v_w_in=v_w_in, v_conv_w=v_conv_w, v_lb_param=v_lb_param, v_gnorm_g=v_gnorm_g, v_w_conv_out=v_w_conv_out, v_w_hgrn_out=v_w_hgrn_out, v_w_o=v_w_o, v_norm_ffn_g=v_norm_ffn_g, v_w_ffn_gate=v_w_ffn_gate, v_w_ffn_up=v_w_ffn_up, v_w_ffn_down=v_w_ffn_down, v_norm_final_g=v_norm_final_g)
    weights = {n: given[n] for n in TWIN_WEIGHTS}
    shared = {n: given[n] for n in SHARED_INPUTS}
    per_example = {n: given[n] for n in ['x', 'c']}
    grad_fn = _jax.value_and_grad(_loss, argnums=(0, 1))

    def one_microbatch(ex, loss_target):
        ex = dict(ex)
        diff = ex.pop(TWIN_DIFF_INPUT)
        return grad_fn(weights, diff, {**shared, **ex}, loss_target)

    if N_MICROBATCH == 1:
        loss, (grad_w, grad_x) = one_microbatch(per_example, given["loss_target"])
    else:
        def body(carry, xs):
            loss_sum, grad_sum = carry
            l_k, (gw_k, gx_k) = one_microbatch(xs[0], xs[1])
            with _jax.named_scope("update"):
                return (loss_sum + l_k, _jax.tree.map(_jnp.add, grad_sum, gw_k)), gx_k

        init = (_jnp.zeros((), _jnp.float32), _jax.tree.map(_jnp.zeros_like, weights))
        (loss, grad_w), grad_x = _jax.lax.scan(body, init, (per_example, given["loss_target"]))
    with _jax.named_scope("update"):
        delta_w, new_m, new_v = {}, {}, {}
        for n in TWIN_WEIGHTS:
            delta_w[n], new_m[n], new_v[n] = _adamw(weights[n], grad_w[n], given["m_" + n], given["v_" + n])
    return (loss, grad_x, *[grad_w[n] for n in TWIN_WEIGHTS], *[delta_w[n] for n in TWIN_WEIGHTS],
            *[new_m[n] for n in TWIN_WEIGHTS], *[new_v[n] for n in TWIN_WEIGHTS])
```

```python
import functools

import jax
import jax.numpy as jnp
from jax import lax
from jax.experimental import pallas as pl
from jax.experimental.pallas import tpu as pltpu

F32 = jnp.float32
BF16 = jnp.bfloat16
MESH = pl.DeviceIdType.MESH

EPS = 1e-6
HEAD = 128
BLK = 16
N_CHIPS = 4
N_DEV = 8
VMEM_LIMIT_BYTES = 56 * 1024 * 1024

ADAM_LR = 0.001
ADAM_B1 = 0.9
ADAM_B2 = 0.999
ADAM_EPS = 1e-08
ADAM_WD = 0.01
ADAM_STEP = 10


def _pick(dim, pref, mult):
    if dim <= pref:
        return dim
    t = (pref // mult) * mult
    while t >= mult:
        if dim % t == 0:
            return t
        t -= mult
    return dim


def _sig(x):
    return 1.0 / (1.0 + jnp.exp(-x))


def _params(sem):
    return pltpu.CompilerParams(dimension_semantics=sem, vmem_limit_bytes=VMEM_LIMIT_BYTES)


def _gj(j, i, k):
    return j


def _gk(j, i, k):
    return k


def _wspec(arr, rt, ct, r_of, c_of):
    if arr.ndim == 2:
        return pl.BlockSpec((rt, ct), lambda j, i, k: (r_of(j, i, k), c_of(j, i, k)))
    per = arr.shape[2] // ct
    assert arr.shape[2] % ct == 0
    return pl.BlockSpec((None, rt, ct),
                        lambda j, i, k: (c_of(j, i, k) // per, r_of(j, i, k), c_of(j, i, k) % per))


def _matmul(name, pairs, acc_of, M, N, K, tm, tn, tk, extras, outs, epilogue):
    assert M % tm == 0 and N % tn == 0 and K % tk == 0, (name, M, N, K, tm, tn, tk)
    nj, ni, nk = N // tn, M // tm, K // tk
    n_pairs, n_extra, n_out = len(pairs), len(extras), len(outs)
    n_acc = max(acc_of) + 1
    in_specs, args, dims = [], [], []
    for a, am, b, bm in pairs:
        if am == 'n':
            in_specs.append(pl.BlockSpec((tm, tk), lambda j, i, k: (i, k)))
        else:
            in_specs.append(pl.BlockSpec((tk, tm), lambda j, i, k: (k, i)))
        if bm == 'n':
            in_specs.append(_wspec(b, tk, tn, _gk, _gj))
        else:
            in_specs.append(_wspec(b, tn, tk, _gj, _gk))
        dims.append((((1 if am == 'n' else 0,), (0 if bm == 'n' else 1,)), ((), ())))
        args += [a, b]
    for e, kind, off in extras:
        assert off % tn == 0, (name, off, tn)
        o = off // tn
        if kind == 'tile':
            in_specs.append(pl.BlockSpec((tm, tn), lambda j, i, k, o=o: (i, j + o)))
        else:
            in_specs.append(pl.BlockSpec((1, tn), lambda j, i, k, o=o: (0, j + o)))
        args.append(e)
    out_shape, out_specs = [], []
    for dt, nb in outs:
        if nb is None:
            out_shape.append(jax.ShapeDtypeStruct((M, N), dt))
            out_specs.append(pl.BlockSpec((tm, tn), lambda j, i, k: (i, j)))
        else:
            assert nb % tn == 0 and N % nb == 0
            per = nb // tn
            out_shape.append(jax.ShapeDtypeStruct((N // nb, M, nb), dt))
            out_specs.append(pl.BlockSpec((None, tm, tn), lambda j, i, k, per=per: (j // per, i, j % per)))

    def body(*refs):
        ab = refs[:2 * n_pairs]
        e_refs = refs[2 * n_pairs:2 * n_pairs + n_extra]
        o_refs = refs[2 * n_pairs + n_extra:2 * n_pairs + n_extra + n_out]
        acc_refs = refs[2 * n_pairs + n_extra + n_out:]
        sums = [None] * n_acc
        for p in range(n_pairs):
            prod = lax.dot_general(ab[2 * p][...], ab[2 * p + 1][...], dims[p], preferred_element_type=F32)
            a = acc_of[p]
            sums[a] = prod if sums[a] is None else sums[a] + prod

        def finish(accs):
            res = epilogue(accs, [e[...] for e in e_refs])
            for o_ref, r in zip(o_refs, res):
                o_ref[...] = r.astype(o_ref.dtype)

        if nk == 1:
            finish(sums)
        else:
            k = pl.program_id(2)

            @pl.when(k == 0)
            def _():
                for a in range(n_acc):
                    acc_refs[a][...] = sums[a]

            @pl.when(k > 0)
            def _():
                for a in range(n_acc):
                    acc_refs[a][...] += sums[a]

            @pl.when(k == nk - 1)
            def _():
                finish([acc_refs[a][...] for a in range(n_acc)])

    scratch = [pltpu.VMEM((tm, tn), F32) for _ in range(n_acc)] if nk > 1 else []
    res = pl.pallas_call(
        body, name=name, grid=(nj, ni, nk), in_specs=in_specs, out_specs=out_specs, out_shape=out_shape,
        scratch_shapes=scratch, compiler_params=_params(("parallel", "parallel", "arbitrary")),
    )(*args)
    return res


def _row_spec(tr, d):
    return pl.BlockSpec((tr, d), lambda i: (i, 0))


def _vec_spec(d):
    return pl.BlockSpec((1, d), lambda i: (0, 0))


def _norm_mod(name, x, g, sc, sh):
    S, D = x.shape
    tr = _pick(S, 256, 8)

    def body(x_ref, g_ref, sc_ref, sh_ref, h_ref):
        xv = x_ref[...]
        r = lax.rsqrt(jnp.mean(xv * xv, axis=-1, keepdims=True) + EPS)
        h_ref[...] = ((xv * r) * g_ref[...] * (1.0 + sc_ref[...]) + sh_ref[...]).astype(BF16)

    return pl.pallas_call(
        body, name=name, grid=(S // tr,),
        in_specs=[_row_spec(tr, D), _vec_spec(D), _vec_spec(D), _vec_spec(D)],
        out_specs=_row_spec(tr, D), out_shape=jax.ShapeDtypeStruct((S, D), BF16),
        compiler_params=_params(("parallel",)),
    )(x, g, sc, sh)


def _loss_head(x2, target, g, ff, gt):
    S, D = x2.shape
    tr = _pick(S, 256, 8)

    def body(x_ref, t_ref, g_ref, ff_ref, gt_ref, dx_ref, dffb_ref, loss_ref, dgt_ref, dg_ref):
        i = pl.program_id(0)

        @pl.when(i == 0)
        def _():
            loss_ref[...] = jnp.zeros_like(loss_ref)
            dgt_ref[...] = jnp.zeros_like(dgt_ref)
            dg_ref[...] = jnp.zeros_like(dg_ref)

        xv = x_ref[...]
        gv = g_ref[...]
        r = lax.rsqrt(jnp.mean(xv * xv, axis=-1, keepdims=True) + EPS)
        xh = xv * r
        err = xh * gv - t_ref[...]
        loss_ref[...] += 0.5 * jnp.sum(jnp.mean(err * err, axis=-1, keepdims=True))
        dy = err * (1.0 / D)
        dg_ref[...] += jnp.sum(dy * xh, axis=0, keepdims=True)
        dxh = dy * gv
        dx = r * (dxh - xh * jnp.mean(dxh * xh, axis=-1, keepdims=True))
        dx_ref[...] = dx
        dffb_ref[...] = (dx * gt_ref[...]).astype(BF16)
        dgt_ref[...] += jnp.sum(dx * ff_ref[...], axis=0, keepdims=True)

    return pl.pallas_call(
        body, name="loss_head", grid=(S // tr,),
        in_specs=[_row_spec(tr, D), _row_spec(tr, D), _vec_spec(D), _row_spec(tr, D), _vec_spec(D)],
        out_specs=[_row_spec(tr, D), _row_spec(tr, D), pl.BlockSpec((1, 128), lambda i: (0, 0)),
                   _vec_spec(D), _vec_spec(D)],
        out_shape=[jax.ShapeDtypeStruct((S, D), F32), jax.ShapeDtypeStruct((S, D), BF16),
                   jax.ShapeDtypeStruct((1, 128), F32), jax.ShapeDtypeStruct((1, D), F32),
                   jax.ShapeDtypeStruct((1, D), F32)],
        compiler_params=_params(("arbitrary",)),
    )(x2, target, g, ff, gt)


def _norm_mod_bwd(name, dh, x, g, sc, dres, gated=None):
    S, D = x.shape
    tr = _pick(S, 256, 8)
    n = S // tr
    with_gate = gated is not None

    def body(*refs):
        if with_gate:
            dh_ref, x_ref, g_ref, sc_ref, dres_ref, mo_ref, gt_ref, dx_ref, dsh_ref, dsc_ref, dg_ref, dmob_ref, dgt_ref = refs
        else:
            dh_ref, x_ref, g_ref, sc_ref, dres_ref, dx_ref, dsh_ref, dsc_ref, dg_ref = refs
        i = pl.program_id(0)

        @pl.when(i == 0)
        def _():
            dsh_ref[...] = jnp.zeros_like(dsh_ref)
            dsc_ref[...] = jnp.zeros_like(dsc_ref)
            if with_gate:
                dgt_ref[...] = jnp.zeros_like(dgt_ref)

        xv = x_ref[...]
        dhv = dh_ref[...]
        gv = g_ref[...]
        scale = 1.0 + sc_ref[...]
        r = lax.rsqrt(jnp.mean(xv * xv, axis=-1, keepdims=True) + EPS)
        xh = xv * r
        dsh_ref[...] += jnp.sum(dhv, axis=0, keepdims=True)
        dsc_ref[...] += jnp.sum(dhv * xh, axis=0, keepdims=True)
        dxh = dhv * (gv * scale)
        dx = dres_ref[...] + r * (dxh - xh * jnp.mean(dxh * xh, axis=-1, keepdims=True))
        dx_ref[...] = dx
        if with_gate:
            dmob_ref[...] = (dx * gt_ref[...]).astype(BF16)
            dgt_ref[...] += jnp.sum(dx * mo_ref[...], axis=0, keepdims=True)

        @pl.when(i == n - 1)
        def _():
            t = dsc_ref[...]
            dg_ref[...] = t * scale
            dsc_ref[...] = t * gv

    in_specs = [_row_spec(tr, D), _row_spec(tr, D), _vec_spec(D), _vec_spec(D), _row_spec(tr, D)]
    args = [dh, x, g, sc, dres]
    out_specs = [_row_spec(tr, D), _vec_spec(D), _vec_spec(D), _vec_spec(D)]
    out_shape = [jax.ShapeDtypeStruct((S, D), F32)] + [jax.ShapeDtypeStruct((1, D), F32)] * 3
    if with_gate:
        in_specs += [_row_spec(tr, D), _vec_spec(D)]
        args += list(gated)
        out_specs += [_row_spec(tr, D), _vec_spec(D)]
        out_shape += [jax.ShapeDtypeStruct((S, D), BF16), jax.ShapeDtypeStruct((1, D), F32)]
    return pl.pallas_call(
        body, name=name, grid=(n,), in_specs=in_specs, out_specs=out_specs, out_shape=out_shape,
        compiler_params=_params(("arbitrary",)),
    )(*args)


CONV_ROWS = 256


def _col_spec(S, off_cols):
    o = off_cols // HEAD
    return pl.BlockSpec((S, HEAD), lambda c, o=o: (0, c + o))


def _conv_taps(u, u_prev, rid):
    s1 = jnp.where(rid >= 1, pltpu.roll(u, 1, 0), pltpu.roll(u_prev, 1, 0))
    s2 = jnp.where(rid >= 2, pltpu.roll(u, 2, 0), pltpu.roll(u_prev, 2, 0))
    return s1, s2


def _conv_fwd(proj, conv_w, d_conv):
    S = proj.shape[0]
    R = min(CONV_ROWS, S)
    nr = S // R

    def body(ab_ref, ac_ref, ax_ref, w_ref, ya_ref):
        w0, w1, w2 = w_ref[0:1, :], w_ref[1:2, :], w_ref[2:3, :]
        rid = lax.broadcasted_iota(jnp.int32, (R, HEAD), 0)

        def step(c, carry):
            rows = pl.ds(pl.multiple_of(c * R, R), R)
            prev = pl.ds(pl.multiple_of(jnp.maximum(c - 1, 0) * R, R), R)
            u = ac_ref[rows, :] * ax_ref[rows, :]
            u_prev = jnp.where(c > 0, ac_ref[prev, :] * ax_ref[prev, :], 0.0)
            s1, s2 = _conv_taps(u, u_prev, rid)
            y = w0 * s2 + w1 * s1 + w2 * u
            ya_ref[rows, :] = (ab_ref[rows, :] * y).astype(BF16)
            return carry

        lax.fori_loop(0, nr, step, 0)

    return pl.pallas_call(
        body, name="conv_fwd", grid=(d_conv // HEAD,),
        in_specs=[_col_spec(S, 0), _col_spec(S, d_conv), _col_spec(S, 2 * d_conv),
                  pl.BlockSpec((3, HEAD), lambda c: (0, c))],
        out_specs=pl.BlockSpec((S, HEAD), lambda c: (0, c)),
        out_shape=jax.ShapeDtypeStruct((S, d_conv), BF16),
        compiler_params=_params(("parallel",)),
    )(proj, proj, proj, conv_w)


def _conv_bwd(dya, proj, conv_w, d_conv):
    S = proj.shape[0]
    R = min(CONV_ROWS, S)
    nr = S // R

    def body(dya_ref, ab_ref, ac_ref, ax_ref, w_ref, dab_ref, dac_ref, dax_ref, dw_ref):
        w0, w1, w2 = w_ref[0:1, :], w_ref[1:2, :], w_ref[2:3, :]
        rid = lax.broadcasted_iota(jnp.int32, (R, HEAD), 0)

        def step(c, carry):
            dw0, dw1, dw2 = carry
            rows = pl.ds(pl.multiple_of(c * R, R), R)
            prev = pl.ds(pl.multiple_of(jnp.maximum(c - 1, 0) * R, R), R)
            nxt = pl.ds(pl.multiple_of(jnp.minimum(c + 1, nr - 1) * R, R), R)
            a_c, a_x, a_b = ac_ref[rows, :], ax_ref[rows, :], ab_ref[rows, :]
            u = a_c * a_x
            u_prev = jnp.where(c > 0, ac_ref[prev, :] * ax_ref[prev, :], 0.0)
            s1, s2 = _conv_taps(u, u_prev, rid)
            y = w0 * s2 + w1 * s1 + w2 * u
            dy = dya_ref[rows, :]
            dab_ref[rows, :] = (dy * y).astype(BF16)
            dyc = dy * a_b
            dyc_next = jnp.where(c < nr - 1, dya_ref[nxt, :] * ab_ref[nxt, :], 0.0)
            t1 = jnp.where(rid < R - 1, pltpu.roll(dyc, R - 1, 0), pltpu.roll(dyc_next, R - 1, 0))
            t2 = jnp.where(rid < R - 2, pltpu.roll(dyc, R - 2, 0), pltpu.roll(dyc_next, R - 2, 0))
            du = w2 * dyc + w1 * t1 + w0 * t2
            dac_ref[rows, :] = (du * a_x).astype(BF16)
            dax_ref[rows, :] = (du * a_c).astype(BF16)
            dw0 = dw0 + jnp.sum(dyc * s2, axis=0, keepdims=True)
            dw1 = dw1 + jnp.sum(dyc * s1, axis=0, keepdims=True)
            dw2 = dw2 + jnp.sum(dyc * u, axis=0, keepdims=True)
            return dw0, dw1, dw2

        z = jnp.zeros((1, HEAD), F32)
        dw0, dw1, dw2 = lax.fori_loop(0, nr, step, (z, z, z))
        dw_ref[0:1, :] = dw0
        dw_ref[1:2, :] = dw1
        dw_ref[2:3, :] = dw2

    col = pl.BlockSpec((S, HEAD), lambda c: (0, c))
    return pl.pallas_call(
        body, name="conv_bwd", grid=(d_conv // HEAD,),
        in_specs=[col, _col_spec(S, 0), _col_spec(S, d_conv), _col_spec(S, 2 * d_conv),
                  pl.BlockSpec((3, HEAD), lambda c: (0, c))],
        out_specs=[col, col, col, pl.BlockSpec((3, HEAD), lambda c: (0, c))],
        out_shape=[jax.ShapeDtypeStruct((S, d_conv), BF16)] * 3 + [jax.ShapeDtypeStruct((3, d_conv), F32)],
        compiler_params=_params(("parallel",)),
    )(dya, proj, proj, proj, conv_w)


HGRN_ROWS = 256


def _block_cumsum(x, pos):
    for s in (1, 2, 4, 8):
        x = x + jnp.where(pos >= s, pltpu.roll(x, s, 0), 0.0)
    return x


def _block_rev_cumsum(x, pos, rows):
    for s in (1, 2, 4, 8):
        x = x + jnp.where(pos < BLK - s, pltpu.roll(x, rows - s, 0), 0.0)
    return x


def _block_last(x, pos, rows):
    m = jnp.where(pos == BLK - 1, x, 0.0)
    for s in (1, 2, 4, 8):
        m = m + pltpu.roll(m, rows - s, 0)
    return m


def _hgrn_gates(q, z, lb):
    sgq = _sig(q)
    qs = q * sgq
    sg = _sig(z)
    f = lb + (1.0 - lb) * sg
    kk = (1.0 - lb) * (1.0 - sg)
    return sgq, qs, sg, f, kk


def _hgrn_specs(T, d_conv, n_t, rev):
    def t_of(i):
        return (n_t - 1 - i) if rev else i

    def pcol(off):
        o = off // HEAD
        return pl.BlockSpec((T, HEAD), lambda h, i, o=o: (t_of(i), h + o))

    q_spec, z_spec, v_spec, g_spec = pcol(3 * d_conv), pcol(4 * d_conv), pcol(5 * d_conv), pcol(6 * d_conv)
    lb_spec = pl.BlockSpec((1, HEAD), lambda h, i: (0, h))
    gn_spec = pl.BlockSpec((1, HEAD), lambda h, i: (0, 0))
    act_spec = pl.BlockSpec((T, HEAD), lambda h, i: (t_of(i), h))
    st_spec = pl.BlockSpec((None, T // BLK, HEAD, HEAD), lambda h, i: (h, t_of(i), 0, 0))
    return q_spec, z_spec, v_spec, g_spec, lb_spec, gn_spec, act_spec, st_spec


def _hgrn_fwd(proj, lb, gn, d_conv):
    S = proj.shape[0]
    H = d_conv // HEAD
    T = min(HGRN_ROWS, S)
    n_t, nb = S // T, T // BLK
    q_spec, z_spec, v_spec, g_spec, lb_spec, gn_spec, act_spec, st_spec = _hgrn_specs(T, d_conv, n_t, False)

    def body(q_ref, z_ref, v_ref, g_ref, lb_ref, gn_ref, ob_ref, o_ref, st_ref, state, qe_s, ke_s, v_s, dec_s, o_s):
        @pl.when(pl.program_id(1) == 0)
        def _():
            state[...] = jnp.zeros_like(state)

        pos = lax.broadcasted_iota(jnp.int32, (T, HEAD), 0) % BLK
        v = v_ref[...]
        _, qs, _, f, kk = _hgrn_gates(q_ref[...], z_ref[...], lb_ref[...])
        b = _block_cumsum(jnp.log(f), pos)
        b_tot = _block_last(b, pos, T)
        qe_s[...] = (qs * jnp.exp(b)).astype(BF16)
        ke_s[...] = (kk * jnp.exp(b_tot - b)).astype(BF16)
        v_s[...] = v.astype(BF16)
        dec_s[...] = jnp.exp(b_tot)
        o = jnp.sum(qs * kk, axis=-1, keepdims=True) * v
        for d in range(1, BLK):
            e = jnp.exp(b - pltpu.roll(b, d, 0))
            a = jnp.sum(qs * pltpu.roll(kk, d, 0) * e, axis=-1, keepdims=True)
            o = o + jnp.where(pos >= d, a * pltpu.roll(v, d, 0), 0.0)
        o_s[...] = o

        def step(j, st):
            rows = pl.ds(pl.multiple_of(j * BLK, BLK), BLK)
            stb = st.astype(BF16)
            st_ref[j] = stb
            o_s[rows, :] += lax.dot_general(qe_s[rows, :], stb, (((1,), (1,)), ((), ())),
                                            preferred_element_type=F32)
            upd = lax.dot_general(v_s[rows, :], ke_s[rows, :], (((0,), (0,)), ((), ())),
                                  preferred_element_type=F32)
            return st * dec_s[pl.ds(pl.multiple_of(j * BLK, BLK), 1), :] + upd

        state[...] = lax.fori_loop(0, nb, step, state[...])
        o = o_s[...]
        o_ref[...] = o
        r = lax.rsqrt(jnp.mean(o * o, axis=-1, keepdims=True) + EPS)
        g = g_ref[...]
        ob_ref[...] = ((o * r) * gn_ref[...] * (g * _sig(g))).astype(BF16)

    return pl.pallas_call(
        body, name="hgrn_fwd", grid=(H, n_t),
        in_specs=[q_spec, z_spec, v_spec, g_spec, lb_spec, gn_spec],
        out_specs=[act_spec, act_spec, st_spec],
        out_shape=[jax.ShapeDtypeStruct((S, d_conv), BF16), jax.ShapeDtypeStruct((S, d_conv), F32),
                   jax.ShapeDtypeStruct((H, S // BLK, HEAD, HEAD), BF16)],
        scratch_shapes=[pltpu.VMEM((HEAD, HEAD), F32), pltpu.VMEM((T, HEAD), BF16), pltpu.VMEM((T, HEAD), BF16),
                        pltpu.VMEM((T, HEAD), BF16), pltpu.VMEM((T, HEAD), F32), pltpu.VMEM((T, HEAD), F32)],
        compiler_params=_params(("parallel", "arbitrary")),
    )(proj, proj, proj, proj, lb, gn)


def _hgrn_bwd(dob, o_raw, states, proj, lb, gn, d_conv):
    S = proj.shape[0]
    H = d_conv // HEAD
    T = min(HGRN_ROWS, S)
    n_t, nb = S // T, T // BLK
    q_spec, z_spec, v_spec, g_spec, lb_spec, gn_spec, act_spec, st_spec = _hgrn_specs(T, d_conv, n_t, True)

    def body(dob_ref, o_ref, st_ref, q_ref, z_ref, v_ref, g_ref, lb_ref, gn_ref,
             dq_ref, dz_ref, dv_ref, dg_ref, dlb_ref, dgn_ref,
             dstate, do_b, qe_b, ke_b, v_b, dec_s, dqe_s, dke_s, dvi_s, ddec_s):
        @pl.when(pl.program_id(1) == 0)
        def _():
            dstate[...] = jnp.zeros_like(dstate)
            dlb_ref[...] = jnp.zeros_like(dlb_ref)
            dgn_ref[...] = jnp.zeros_like(dgn_ref)

        pos = lax.broadcasted_iota(jnp.int32, (T, HEAD), 0) % BLK
        lbv, gnv = lb_ref[...], gn_ref[...]
        q, v, g = q_ref[...], v_ref[...], g_ref[...]
        sgq, qs, sg, f, kk = _hgrn_gates(q, z_ref[...], lbv)
        b = _block_cumsum(jnp.log(f), pos)
        b_tot = _block_last(b, pos, T)
        eb = jnp.exp(b)
        ek = jnp.exp(b_tot - b)
        dec = jnp.exp(b_tot)
        qe, ke = qs * eb, kk * ek
        o = o_ref[...]
        r = lax.rsqrt(jnp.mean(o * o, axis=-1, keepdims=True) + EPS)
        on = o * r
        sgg = _sig(g)
        dobv = dob_ref[...]
        dog = dobv * (g * sgg)
        dgn_ref[...] += jnp.sum(dog * on, axis=0, keepdims=True)
        don = dog * gnv
        do = r * (don - on * jnp.mean(don * on, axis=-1, keepdims=True))
        dg_ref[...] = (dobv * (on * gnv) * (sgg * (1.0 + g * (1.0 - sgg)))).astype(BF16)

        do_b[...] = do.astype(BF16)
        qe_b[...] = qe.astype(BF16)
        ke_b[...] = ke.astype(BF16)
        v_b[...] = v.astype(BF16)
        dec_s[...] = dec

        def step(jj, dst):
            j = nb - 1 - jj
            rows = pl.ds(pl.multiple_of(j * BLK, BLK), BLK)
            stj = st_ref[j]
            dstb = dst.astype(BF16)
            dob_j = do_b[rows, :]
            dqe_s[rows, :] = lax.dot_general(dob_j, stj, (((1,), (0,)), ((), ())), preferred_element_type=F32)
            dke_s[rows, :] = lax.dot_general(v_b[rows, :], dstb, (((1,), (0,)), ((), ())), preferred_element_type=F32)
            dvi_s[rows, :] = lax.dot_general(ke_b[rows, :], dstb, (((1,), (1,)), ((), ())), preferred_element_type=F32)
            ddec = jnp.sum(stj.astype(F32) * dst, axis=0, keepdims=True)
            ddec_s[rows, :] = jnp.broadcast_to(ddec, (BLK, HEAD))
            upd = lax.dot_general(dob_j, qe_b[rows, :], (((0,), (0,)), ((), ())), preferred_element_type=F32)
            return dst * dec_s[pl.ds(pl.multiple_of(j * BLK, BLK), 1), :] + upd

        dstate[...] = lax.fori_loop(0, nb, step, dstate[...])

        dqe, dke = dqe_s[...], dke_s[...]
        dqs = dqe * eb
        dkk = dke * ek
        tke = dke * ke
        db = dqe * qe - tke
        db = db + jnp.where(pos == BLK - 1, _block_cumsum(tke, pos) + ddec_s[...] * dec, 0.0)
        dv = dvi_s[...]
        da = jnp.sum(do * v, axis=-1, keepdims=True)
        dqs = dqs + da * kk
        dkk = dkk + da * qs
        dv = dv + jnp.sum(qs * kk, axis=-1, keepdims=True) * do
        for d in range(1, BLK):
            kd = pltpu.roll(kk, d, 0)
            e = jnp.where(pos >= d, jnp.exp(b - pltpu.roll(b, d, 0)), 0.0)
            a = jnp.sum(qs * kd * e, axis=-1, keepdims=True)
            da = jnp.sum(do * pltpu.roll(v, d, 0), axis=-1, keepdims=True)
            gq = da * e
            dqs = dqs + gq * kd
            pk = gq * qs
            dkk = dkk + pltpu.roll(pk, T - d, 0)
            pb = pk * kd
            db = db + pb - pltpu.roll(pb, T - d, 0)
            dv = dv + pltpu.roll(jnp.where(pos >= d, a * do, 0.0), T - d, 0)
        dlf = _block_rev_cumsum(db, pos, T)
        df = dlf / f - dkk
        dlb_ref[...] += jnp.sum(df * (1.0 - sg), axis=0, keepdims=True)
        dz_ref[...] = (df * (1.0 - lbv) * sg * (1.0 - sg)).astype(BF16)
        dq_ref[...] = (dqs * (sgq * (1.0 + q * (1.0 - sgq)))).astype(BF16)
        dv_ref[...] = dv.astype(BF16)

    tb = lambda dt: pltpu.VMEM((T, HEAD), dt)
    return pl.pallas_call(
        body, name="hgrn_bwd", grid=(H, n_t),
        in_specs=[act_spec, act_spec, st_spec, q_spec, z_spec, v_spec, g_spec, lb_spec, gn_spec],
        out_specs=[act_spec, act_spec, act_spec, act_spec, lb_spec,
                   pl.BlockSpec((None, 1, HEAD), lambda h, i: (h, 0, 0))],
        out_shape=[jax.ShapeDtypeStruct((S, d_conv), BF16)] * 4
        + [jax.ShapeDtypeStruct((1, d_conv), F32), jax.ShapeDtypeStruct((H, 1, HEAD), F32)],
        scratch_shapes=[pltpu.VMEM((HEAD, HEAD), F32), tb(BF16), tb(BF16), tb(BF16), tb(BF16), tb(F32),
                        tb(F32), tb(F32), tb(F32), tb(F32)],
        compiler_params=_params(("parallel", "arbitrary")),
    )(dob, o_raw, states, proj, proj, proj, proj, lb, gn)


def _first(accs, extras):
    return [accs[0]]


def _local_step(x, target, mod, norm_mix_g, lb, gnorm_g, norm_ffn_g, norm_final_g, conv_w,
                w_in, w_conv_out, w_hgrn_out, w_o, w_ffn_gate, w_ffn_up, w_ffn_down):
    S, D = x.shape
    d_conv = D // 2
    d_in = 7 * d_conv + 2 * D
    d_ff = w_ffn_down.shape[0]
    nb_in, nb_co, nb_ff = w_in.shape[2], w_conv_out.shape[2], w_ffn_gate.shape[2]
    sh_m, sc_m, gt_m, sh_f, sc_f, gt_f = [mod[:, i * D:(i + 1) * D] for i in range(6)]
    tm = _pick(S, 512, 16)
    ts = _pick(S, 1024, 128)
    tn_in = _pick(nb_in, 1408, 128)
    tn_co = _pick(nb_co, 512, 128)
    tn_ff = _pick(nb_ff, 1408, 128)
    tn_d = _pick(D, 512, 128)

    h = _norm_mod("norm_mix", x, norm_mix_g, sc_m, sh_m)
    proj, = _matmul("proj", [(h, 'n', w_in, 'n')], [0], S, d_in, D, tm, tn_in, D, [], [(F32, None)], _first)
    ya = _conv_fwd(proj, conv_w, d_conv)
    ob, o_raw, states = _hgrn_fwd(proj, lb, gnorm_g, d_conv)

    def merge_epi(accs, ex):
        y_a, y_b = accs
        return [y_a, y_b, _sig(ex[0]) * y_a + _sig(ex[1]) * y_b]

    y_a, y_b, merged = _matmul(
        "branch_out", [(ya, 'n', w_conv_out, 'n'), (ob, 'n', w_hgrn_out, 'n')], [0, 1], S, D, d_conv, tm, tn_co, d_conv,
        [(proj, 'tile', 7 * d_conv), (proj, 'tile', 7 * d_conv + D)], [(F32, None), (F32, None), (BF16, None)], merge_epi)

    def resid_epi(accs, ex):
        return [accs[0], ex[0] + ex[1] * accs[0]]

    mo, x1 = _matmul("w_o", [(merged, 'n', w_o, 'n')], [0], S, D, D, tm, tn_d, D,
                     [(x, 'tile', 0), (gt_m, 'row', 0)], [(F32, None), (F32, None)], resid_epi)
    h2 = _norm_mod("norm_ffn", x1, norm_ffn_g, sc_f, sh_f)

    def swiglu_epi(accs, ex):
        gg, uu = accs
        return [gg, uu, gg * _sig(gg) * uu]

    G, U, act = _matmul("ffn_in", [(h2, 'n', w_ffn_gate, 'n'), (h2, 'n', w_ffn_up, 'n')], [0, 1], S, d_ff, D,
                        tm, tn_ff, _pick(D, 1024, 128), [], [(F32, None), (F32, None), (BF16, None)], swiglu_epi)
    ff, x2 = _matmul("ffn_out", [(act, 'n', w_ffn_down, 'n')], [0], S, D, d_ff, tm, tn_d, _pick(d_ff, 1408, 128),
                     [(x1, 'tile', 0), (gt_f, 'row', 0)], [(F32, None), (F32, None)], resid_epi)

    dx2, dffb, loss, dgt_f, dg_final = _loss_head(x2, target, norm_final_g, ff, gt_f)

    def swiglu_bwd_epi(accs, ex):
        dact, gg, uu = accs[0], ex[0], ex[1]
        s = _sig(gg)
        return [dact * uu * (s * (1.0 + gg * (1.0 - s))), dact * (gg * s)]

    dG, dU = _matmul("d_act", [(dffb, 'n', w_ffn_down, 't')], [0], S, d_ff, D, tm, tn_ff, D,
                     [(G, 'tile', 0), (U, 'tile', 0)], [(BF16, None), (BF16, None)], swiglu_bwd_epi)
    dw_down, = _matmul("dw_ffn_down", [(act, 't', dffb, 'n')], [0], d_ff, D, S, _pick(d_ff, 512, 128),
                       _pick(D, 1024, 128), ts, [], [(BF16, None)], _first)
    dh2, = _matmul("d_h2", [(dG, 'n', w_ffn_gate, 't'), (dU, 'n', w_ffn_up, 't')], [0, 0], S, D, d_ff,
                   tm, tn_d, tn_ff, [], [(F32, None)], _first)
    dw_gate, dw_up = _matmul("dw_ffn_in", [(h2, 't', dG, 'n'), (h2, 't', dU, 'n')], [0, 1], D, d_ff, S,
                             _pick(D, 512, 128), tn_ff, ts, [], [(BF16, nb_ff), (BF16, nb_ff)],
                             lambda accs, ex: accs)
    dx1, dsh_f, dsc_f, dg_ffn, dmob, dgt_m = _norm_mod_bwd("norm_ffn_bwd", dh2, x1, norm_ffn_g, sc_f, dx2, (mo, gt_m))

    def merge_bwd_epi(accs, ex):
        dm, ga, gb, ya_, yb_ = accs[0], ex[0], ex[1], ex[2], ex[3]
        sa, sb = _sig(ga), _sig(gb)
        return [dm * ya_ * sa * (1.0 - sa), dm * yb_ * sb * (1.0 - sb), dm * sa, dm * sb]

    dga, dgb, dy_a, dy_b = _matmul(
        "d_merged", [(dmob, 'n', w_o, 't')], [0], S, D, D, tm, tn_co, D,
        [(proj, 'tile', 7 * d_conv), (proj, 'tile', 7 * d_conv + D), (y_a, 'tile', 0), (y_b, 'tile', 0)],
        [(BF16, None)] * 4, merge_bwd_epi)
    dw_o, = _matmul("dw_o", [(merged, 't', dmob, 'n')], [0], D, D, S, _pick(D, 512, 128), _pick(D, 1024, 128), ts,
                    [], [(BF16, None)], _first)
    tn_c = _pick(d_conv, 512, 128)
    dya, = _matmul("d_ya", [(dy_a, 'n', w_conv_out, 't')], [0], S, d_conv, D, tm, tn_c, tn_co, [], [(F32, None)], _first)
    dob, = _matmul("d_ob", [(dy_b, 'n', w_hgrn_out, 't')], [0], S, d_conv, D, tm, tn_c, tn_co, [], [(F32, None)], _first)
    dw_co, = _matmul("dw_conv_out", [(ya, 't', dy_a, 'n')], [0], d_conv, D, S, _pick(d_conv, 512, 128), tn_co, ts,
                     [], [(BF16, nb_co)], _first)
    dw_ho, = _matmul("dw_hgrn_out", [(ob, 't', dy_b, 'n')], [0], d_conv, D, S, _pick(d_conv, 512, 128), tn_co, ts,
                     [], [(BF16, nb_co)], _first)
    dab, dac, dax, dconv_w = _conv_bwd(dya, proj, conv_w, d_conv)
    dq, dz, dv, dgo, dlb, dgn = _hgrn_bwd(dob, o_raw, states, proj, lb, gnorm_g, d_conv)
    dproj = jnp.concatenate([dab, dac, dax, dq, dz, dv, dgo, dga, dgb], axis=1)
    dh, = _matmul("d_h", [(dproj, 'n', w_in, 't')], [0], S, D, d_in, tm, tn_d, tn_in, [], [(F32, None)], _first)
    dw_in, = _matmul("dw_in", [(h, 't', dproj, 'n')], [0], D, d_in, S, _pick(D, 512, 128), tn_in, ts,
                     [], [(BF16, nb_in)], _first)
    dx, dsh_m, dsc_m, dg_mix = _norm_mod_bwd("norm_mix_bwd", dh, x, norm_mix_g, sc_m, dx1)
    dmod = jnp.concatenate([dsh_m, dsc_m, dgt_m, dsh_f, dsc_f, dgt_f], axis=1)
    small = dict(dmod=dmod, dg_mix=dg_mix, dg_ffn=dg_ffn, dg_final=dg_final, dlb=dlb,
                 dgn=jnp.sum(dgn, axis=0), dconv_w=dconv_w)
    big = dict(w_in=dw_in, w_conv_out=dw_co, w_hgrn_out=dw_ho, w_o=dw_o, w_ffn_gate=dw_gate, w_ffn_up=dw_up,
               w_ffn_down=dw_down)
    return loss, dx, small, big


def _adamw_math(w, g, m, v):
    m = ADAM_B1 * m + (1.0 - ADAM_B1) * g
    v = ADAM_B2 * v + (1.0 - ADAM_B2) * (g * g)
    m_hat = m / (1.0 - ADAM_B1 ** ADAM_STEP)
    v_hat = v / (1.0 - ADAM_B2 ** ADAM_STEP)
    delta = -ADAM_LR * (m_hat / (jnp.sqrt(v_hat) + ADAM_EPS) + ADAM_WD * w)
    return delta, m, v


def _adamw(name, w, g, m, v):
    R, C = w.shape
    tr = _pick(R, max(8, (256 * 1024) // C // 8 * 8), 8)
    spec = pl.BlockSpec((tr, C), lambda i: (i, 0))

    def body(w_ref, g_ref, m_ref, v_ref, d_ref, mo_ref, vo_ref):
        d, m2, v2 = _adamw_math(w_ref[...], g_ref[...], m_ref[...], v_ref[...])
        d_ref[...] = d
        mo_ref[...] = m2
        vo_ref[...] = v2

    return pl.pallas_call(
        body, name=name, grid=(R // tr,), in_specs=[spec] * 4, out_specs=[spec] * 3,
        out_shape=[jax.ShapeDtypeStruct((R, C), F32)] * 3, compiler_params=_params(("parallel",)),
    )(w, g, m, v)


def _ada_update(c_act_t, dmod, w, m, v):
    R, C = w.shape
    B = dmod.shape[0]
    tr = _pick(R, 256, 8)
    tc = _pick(C, 1536, 128)
    spec = pl.BlockSpec((tr, tc), lambda i, j: (i, j))

    def body(ct_ref, dm_ref, w_ref, m_ref, v_ref, g_ref, d_ref, mo_ref, vo_ref):
        ct = ct_ref[...]
        dm = dm_ref[...]
        g = ct[:, 0:1] * dm[0:1, :]
        for b in range(1, B):
            g = g + ct[:, b:b + 1] * dm[b:b + 1, :]
        d, m2, v2 = _adamw_math(w_ref[...], g, m_ref[...], v_ref[...])
        g_ref[...] = g
        d_ref[...] = d
        mo_ref[...] = m2
        vo_ref[...] = v2

    return pl.pallas_call(
        body, name="ada_update", grid=(R // tr, C // tc),
        in_specs=[pl.BlockSpec((tr, B), lambda i, j: (i, 0)), pl.BlockSpec((B, tc), lambda i, j: (0, j)),
                  spec, spec, spec],
        out_specs=[spec] * 4, out_shape=[jax.ShapeDtypeStruct((R, C), F32)] * 4,
        compiler_params=_params(("parallel", "parallel")),
    )(c_act_t, dmod, w, m, v)


def _prep(c_all, lb_param):
    def body(c_ref, p_ref, ca_ref, cab_ref, lb_ref):
        cv = c_ref[...]
        ca = cv * _sig(cv)
        ca_ref[...] = ca
        cab_ref[...] = ca.astype(BF16)
        lb_ref[...] = _sig(p_ref[0:1, :] - p_ref[1:2, :])

    return pl.pallas_call(
        body, name="prep",
        out_shape=[jax.ShapeDtypeStruct(c_all.shape, F32), jax.ShapeDtypeStruct(c_all.shape, BF16),
                   jax.ShapeDtypeStruct((1, lb_param.shape[1]), F32)],
    )(c_all, lb_param)


def _small_reduce(parts):
    W = parts.shape[1]

    def body(p_ref, o_ref):
        acc = p_ref[0:1, :]
        for d in range(1, N_DEV):
            acc = acc + p_ref[d:d + 1, :]
        o_ref[...] = acc

    return pl.pallas_call(body, name="small_reduce", out_shape=jax.ShapeDtypeStruct((1, W), F32))(parts)


def _lb_grad(dlb, lb):
    def body(d_ref, lb_ref, o_ref):
        t = d_ref[...] * lb_ref[...] * (1.0 - lb_ref[...])
        o_ref[0:1, :] = t
        o_ref[1:2, :] = -t

    return pl.pallas_call(body, name="lb_grad", out_shape=jax.ShapeDtypeStruct((2, dlb.shape[1]), F32))(dlb, lb)


_HBM = pl.BlockSpec(memory_space=pltpu.HBM)


def _place():
    x, y, c = lax.axis_index("x"), lax.axis_index("y"), lax.axis_index("c")
    chips = [(1 - x, y), (x, 1 - y), (1 - x, 1 - y)]
    return x, y, c, chips


def _allgather_rows(name, v):
    m_per, n = v.shape

    def body(x_ref, out_ref, send_sems, recv_sems, local_sem):
        x, y, c, chips = _place()
        me, sibling = (x, y, c), (x, y, 1 - c)

        def rows(px, py, pc):
            return out_ref.at[pl.ds((4 * px + 2 * py + pc) * m_per, m_per), :]

        def copy(k, block, to, src=None):
            return pltpu.make_async_remote_copy(
                src_ref=rows(*block) if src is None else src, dst_ref=rows(*block),
                send_sem=send_sems.at[k], recv_sem=recv_sems.at[k], device_id=to, device_id_type=MESH)

        mine = pltpu.make_async_copy(x_ref, rows(*me), local_sem)
        mine.start()
        first = [copy(0, me, sibling, src=x_ref)]
        first += [copy(1 + j, me, (*chip, c), src=x_ref) for j, chip in enumerate(chips)]
        for cp in first:
            cp.start()
        passed = [copy(4 + j, (*chip, c), sibling) for j, chip in enumerate(chips)]
        for j, chip in enumerate(chips):
            copy(1 + j, (*chip, c), me).wait_recv()
            passed[j].start()
        copy(0, sibling, me).wait_recv()
        for j, chip in enumerate(chips):
            copy(4 + j, (*chip, 1 - c), me).wait_recv()
        for cp in first + passed:
            cp.wait_send()
        mine.wait()

    return pl.pallas_call(
        body, name=name, out_shape=jax.ShapeDtypeStruct((N_DEV * m_per, n), v.dtype),
        in_specs=[pl.BlockSpec(memory_space=pltpu.VMEM)], out_specs=pl.BlockSpec(memory_space=pltpu.VMEM),
        scratch_shapes=[pltpu.SemaphoreType.DMA((7,)), pltpu.SemaphoreType.DMA((7,)), pltpu.SemaphoreType.DMA],
    )(v)


def _gather_weights(shards):
    n = len(shards)

    def body(*refs):
        in_refs, out_refs = refs[:n], refs[n:2 * n]
        send_sems, recv_sems, local_sems = refs[2 * n:]
        x, y, c, chips = _place()
        k_me = 2 * x + y
        sibling = (x, y, 1 - c)

        def region(a, k, hc):
            rh = shards[a].shape[0] // 2
            return out_refs[a].at[k, pl.ds(hc * rh, rh)]

        def copy(a, s, src, dst, to):
            return pltpu.make_async_remote_copy(src_ref=src, dst_ref=dst, send_sem=send_sems.at[6 * a + s],
                                                recv_sem=recv_sems.at[6 * a + s], device_id=to, device_id_type=MESH)

        local = [pltpu.make_async_copy(in_refs[a], out_refs[a].at[k_me], local_sems.at[a]) for a in range(n)]
        for cp in local:
            cp.start()
        started = []
        for a in range(n):
            rh = shards[a].shape[0] // 2
            for j, chip in enumerate(chips):
                cp = copy(a, j, in_refs[a].at[pl.ds(c * rh, rh)], region(a, k_me, c), (*chip, c))
                cp.start()
                started.append(cp)
        for a in range(n):
            for j, chip in enumerate(chips):
                got = region(a, 2 * chip[0] + chip[1], c)
                copy(a, j, got, got, (x, y, c)).wait_recv()
                cp = copy(a, 3 + j, got, got, sibling)
                cp.start()
                started.append(cp)
        for a in range(n):
            for j, chip in enumerate(chips):
                got = region(a, 2 * chip[0] + chip[1], 1 - c)
                copy(a, 3 + j, got, got, (x, y, c)).wait_recv()
        for cp in started:
            cp.wait_send()
        for cp in local:
            cp.wait()

    return pl.pallas_call(
        body, name="gather_weights",
        out_shape=[jax.ShapeDtypeStruct((N_CHIPS,) + s.shape, s.dtype) for s in shards],
        in_specs=[_HBM] * n, out_specs=[_HBM] * n,
        scratch_shapes=[pltpu.SemaphoreType.DMA((6 * n,)), pltpu.SemaphoreType.DMA((6 * n,)),
                        pltpu.SemaphoreType.DMA((n,))],
    )(*shards)


def _swap_halves(grads):
    n = len(grads)

    def body(*refs):
        in_refs, out_refs = refs[:n], refs[n:2 * n]
        send_sems, recv_sems = refs[2 * n:]
        x, y, c, _ = _place()
        cps = []
        for a in range(n):
            rh = grads[a].shape[1] // 2
            cp = pltpu.make_async_remote_copy(
                src_ref=in_refs[a].at[:, pl.ds((1 - c) * rh, rh)], dst_ref=out_refs[a],
                send_sem=send_sems.at[a], recv_sem=recv_sems.at[a], device_id=(x, y, 1 - c), device_id_type=MESH)
            cp.start()
            cps.append(cp)
        for cp in cps:
            cp.wait()

    return pl.pallas_call(
        body, name="swap_halves",
        out_shape=[jax.ShapeDtypeStruct((N_CHIPS, g.shape[1] // 2, g.shape[2]), g.dtype) for g in grads],
        in_specs=[_HBM] * n, out_specs=[_HBM] * n,
        scratch_shapes=[pltpu.SemaphoreType.DMA((n,)), pltpu.SemaphoreType.DMA((n,))],
    )(*grads)


def _pair_sum(name, g, q, c_idx):
    _, R, C = g.shape
    rh = R // 2
    tr = _pick(rh, max(16, (512 * 1024) // C // 16 * 16), 16)
    nh = rh // tr

    def body(c_ref, g_ref, q_ref, o_ref):
        o_ref[...] = (g_ref[...].astype(F32) + q_ref[...].astype(F32)).astype(BF16)

    return pl.pallas_call(
        body, name=name,
        grid_spec=pltpu.PrefetchScalarGridSpec(
            num_scalar_prefetch=1, grid=(N_CHIPS, nh),
            in_specs=[pl.BlockSpec((None, tr, C), lambda k, i, c_ref: (k, c_ref[0] * nh + i, 0)),
                      pl.BlockSpec((None, tr, C), lambda k, i, c_ref: (k, i, 0))],
            out_specs=pl.BlockSpec((None, tr, C), lambda k, i, c_ref: (k, i, 0))),
        out_shape=jax.ShapeDtypeStruct((N_CHIPS, rh, C), BF16),
        compiler_params=_params(("parallel", "parallel")),
    )(c_idx, g, q)


def _scatter_chips(parts):
    n = len(parts)

    def body(*refs):
        in_refs, out_refs = refs[:n], refs[n:2 * n]
        send_sems, recv_sems, local_sems = refs[2 * n:]
        x, y, c, chips = _place()
        k_me = 2 * x + y
        local = [pltpu.make_async_copy(in_refs[a].at[k_me], out_refs[a].at[k_me], local_sems.at[a]) for a in range(n)]
        for cp in local:
            cp.start()
        cps = []
        for a in range(n):
            for j, chip in enumerate(chips):
                cp = pltpu.make_async_remote_copy(
                    src_ref=in_refs[a].at[2 * chip[0] + chip[1]], dst_ref=out_refs[a].at[k_me],
                    send_sem=send_sems.at[3 * a + j], recv_sem=recv_sems.at[3 * a + j],
                    device_id=(*chip, c), device_id_type=MESH)
                cp.start()
                cps.append((cp, a, chip, j))
        for cp, a, chip, j in cps:
            got = out_refs[a].at[2 * chip[0] + chip[1]]
            pltpu.make_async_remote_copy(src_ref=got, dst_ref=got, send_sem=send_sems.at[3 * a + j],
                                         recv_sem=recv_sems.at[3 * a + j], device_id=(x, y, c),
                                         device_id_type=MESH).wait_recv()
        for cp, _, _, _ in cps:
            cp.wait_send()
        for cp in local:
            cp.wait()

    return pl.pallas_call(
        body, name="scatter_chips",
        out_shape=[jax.ShapeDtypeStruct(p.shape, p.dtype) for p in parts],
        in_specs=[_HBM] * n, out_specs=[_HBM] * n,
        scratch_shapes=[pltpu.SemaphoreType.DMA((3 * n,)), pltpu.SemaphoreType.DMA((3 * n,)),
                        pltpu.SemaphoreType.DMA((n,))],
    )(*parts)


def _sum_chips(name, r):
    _, rh, C = r.shape
    tr = _pick(rh, max(16, (256 * 1024) // C // 16 * 16), 16)

    def body(r_ref, o_ref):
        acc = r_ref[0].astype(F32)
        for k in range(1, N_CHIPS):
            acc = acc + r_ref[k].astype(F32)
        o_ref[...] = acc

    return pl.pallas_call(
        body, name=name, grid=(rh // tr,),
        in_specs=[pl.BlockSpec((N_CHIPS, tr, C), lambda i: (0, i, 0))],
        out_specs=pl.BlockSpec((tr, C), lambda i: (i, 0)),
        out_shape=jax.ShapeDtypeStruct((rh, C), F32), compiler_params=_params(("parallel",)),
    )(r)


def _share_halves(halves):
    n = len(halves)

    def body(*refs):
        in_refs, out_refs = refs[:n], refs[n:2 * n]
        send_sems, recv_sems, local_sems = refs[2 * n:]
        x, y, c, _ = _place()
        local = [pltpu.make_async_copy(in_refs[a], out_refs[a].at[c], local_sems.at[a]) for a in range(n)]
        for cp in local:
            cp.start()
        cps = []
        for a in range(n):
            cp = pltpu.make_async_remote_copy(
                src_ref=in_refs[a], dst_ref=out_refs[a].at[c], send_sem=send_sems.at[a], recv_sem=recv_sems.at[a],
                device_id=(x, y, 1 - c), device_id_type=MESH)
            cp.start()
            cps.append(cp)
        for a, cp in enumerate(cps):
            got = out_refs[a].at[1 - c]
            pltpu.make_async_remote_copy(src_ref=got, dst_ref=got, send_sem=send_sems.at[a], recv_sem=recv_sems.at[a],
                                         device_id=(x, y, c), device_id_type=MESH).wait_recv()
        for cp in cps:
            cp.wait_send()
        for cp in local:
            cp.wait()

    return pl.pallas_call(
        body, name="share_halves",
        out_shape=[jax.ShapeDtypeStruct((2,) + h.shape, h.dtype) for h in halves],
        in_specs=[_HBM] * n, out_specs=[_HBM] * n,
        scratch_shapes=[pltpu.SemaphoreType.DMA((n,)), pltpu.SemaphoreType.DMA((n,)), pltpu.SemaphoreType.DMA((n,))],
    )(*halves)


_BIG = ("w_in", "w_conv_out", "w_hgrn_out", "w_o", "w_ffn_gate", "w_ffn_up", "w_ffn_down")
_ROW_SHARDED = ("w_o", "w_ffn_down")
_WEIGHTS = ("w_ada", "b_ada", "norm_mix_g", "w_in", "conv_w", "lb_param", "gnorm_g", "w_conv_out", "w_hgrn_out",
            "w_o", "norm_ffn_g", "w_ffn_gate", "w_ffn_up", "w_ffn_down", "norm_final_g")


def _pack_rows(pieces):
    flat = jnp.concatenate([p.reshape(-1) for p in pieces])
    pad = (-flat.shape[0]) % 1024
    return jnp.pad(flat, (0, pad)).reshape(8, -1)


def kernel(x, c, w_ada, b_ada, norm_mix_g, w_in, conv_w, lb_param, gnorm_g, w_conv_out, w_hgrn_out, w_o, norm_ffn_g, w_ffn_gate, w_ffn_up, w_ffn_down, norm_final_g, loss_target, m_w_ada, m_b_ada, m_norm_mix_g, m_w_in, m_conv_w, m_lb_param, m_gnorm_g, m_w_conv_out, m_w_hgrn_out, m_w_o, m_norm_ffn_g, m_w_ffn_gate, m_w_ffn_up, m_w_ffn_down, m_norm_final_g, v_w_ada, v_b_ada, v_norm_mix_g, v_w_in, v_conv_w, v_lb_param, v_gnorm_g, v_w_conv_out, v_w_hgrn_out, v_w_o, v_norm_ffn_g, v_w_ffn_gate, v_w_ffn_up, v_w_ffn_down, v_norm_final_g):
    w = dict(w_ada=w_ada, b_ada=b_ada, norm_mix_g=norm_mix_g, w_in=w_in, conv_w=conv_w, lb_param=lb_param,
             gnorm_g=gnorm_g, w_conv_out=w_conv_out, w_hgrn_out=w_hgrn_out, w_o=w_o, norm_ffn_g=norm_ffn_g,
             w_ffn_gate=w_ffn_gate, w_ffn_up=w_ffn_up, w_ffn_down=w_ffn_down, norm_final_g=norm_final_g)
    m = dict(w_ada=m_w_ada, b_ada=m_b_ada, norm_mix_g=m_norm_mix_g, w_in=m_w_in, conv_w=m_conv_w, lb_param=m_lb_param,
             gnorm_g=m_gnorm_g, w_conv_out=m_w_conv_out, w_hgrn_out=m_w_hgrn_out, w_o=m_w_o, norm_ffn_g=m_norm_ffn_g,
             w_ffn_gate=m_w_ffn_gate, w_ffn_up=m_w_ffn_up, w_ffn_down=m_w_ffn_down, norm_final_g=m_norm_final_g)
    v = dict(w_ada=v_w_ada, b_ada=v_b_ada, norm_mix_g=v_norm_mix_g, w_in=v_w_in, conv_w=v_conv_w, lb_param=v_lb_param,
             gnorm_g=v_gnorm_g, w_conv_out=v_w_conv_out, w_hgrn_out=v_w_hgrn_out, w_o=v_w_o, norm_ffn_g=v_norm_ffn_g,
             w_ffn_gate=v_w_ffn_gate, w_ffn_up=v_w_ffn_up, w_ffn_down=v_w_ffn_down, norm_final_g=v_norm_final_g)
    two_d = lambda a: a.reshape((-1, a.shape[-1])) if a.ndim != 2 else a
    shapes = {k: a.shape for k, a in w.items()}
    w, m, v = ({k: two_d(a) for k, a in t.items()} for t in (w, m, v))
    w["lb_param"], m["lb_param"], v["lb_param"] = lb_param, m_lb_param, v_lb_param
    S, D = x.shape[1], x.shape[2]
    d_conv = D // 2
    ix, iy, ic = lax.axis_index("x"), lax.axis_index("y"), lax.axis_index("c")
    k_me = 2 * ix + iy
    dev = 2 * k_me + ic
    cw_shard = w["conv_w"].shape[1]
    ada_cols = w["w_ada"].shape[1]

    got = _allgather_rows("gather_cond", _pack_rows([c, w["conv_w"]])).reshape(N_DEV, -1)
    c_all = got[:, :D]
    conv_full = got[::2, D:D + 3 * cw_shard].reshape(N_CHIPS, 3, cw_shard).transpose(1, 0, 2).reshape(3, -1)
    c_act, c_act_b, lb = _prep(c_all, lb_param)
    b_cols = lax.dynamic_slice(w["b_ada"], (0, k_me * ada_cols), (1, ada_cols))
    mod_cols, = _matmul("ada_fwd", [(c_act_b, 'n', w["w_ada"].astype(BF16), 'n')], [0], N_DEV, ada_cols, D,
                        N_DEV, _pick(ada_cols, 1536, 128), D, [(b_cols, 'row', 0)], [(F32, None)],
                        lambda accs, ex: [accs[0] + ex[0]])
    mod_all = _allgather_rows("gather_mod", mod_cols).reshape(N_CHIPS, 2, N_DEV, ada_cols)[:, 0]
    mod_all = mod_all.transpose(1, 0, 2).reshape(N_DEV, -1)
    mod = lax.dynamic_slice(mod_all, (dev, 0), (1, mod_all.shape[1]))
    gathered = _gather_weights([w[k].astype(BF16) for k in _BIG])
    wg = {k: (g.reshape(-1, g.shape[2]) if k in _ROW_SHARDED else g) for k, g in zip(_BIG, gathered)}

    loss, dx, small, big = _local_step(
        x[0], loss_target[0], mod, w["norm_mix_g"], lb, w["gnorm_g"], w["norm_ffn_g"], w["norm_final_g"], conv_full,
        wg["w_in"], wg["w_conv_out"], wg["w_hgrn_out"], wg["w_o"], wg["w_ffn_gate"], wg["w_ffn_up"], wg["w_ffn_down"])

    pieces = [small["dmod"], small["dg_mix"], small["dg_ffn"], small["dg_final"], small["dlb"], small["dgn"],
              small["dconv_w"], loss]
    sizes = [p.size for p in pieces]
    parts = _allgather_rows("gather_small", _pack_rows(pieces)).reshape(N_DEV, -1)
    total = _small_reduce(parts)
    offs = [0]
    for s in sizes:
        offs.append(offs[-1] + s)
    tot = [total[:, offs[i]:offs[i + 1]] for i in range(len(sizes))]
    dmod_all = parts[:, :sizes[0]]
    grads = {
        "b_ada": tot[0], "norm_mix_g": tot[1], "norm_ffn_g": tot[2], "norm_final_g": tot[3],
        "lb_param": _lb_grad(tot[4], lb), "gnorm_g": tot[5],
        "conv_w": lax.dynamic_slice(tot[6].reshape(3, -1), (0, k_me * cw_shard), (3, cw_shard)),
    }
    loss_out = tot[7][0, 0]

    blocked = [big[k].reshape(N_CHIPS, -1, big[k].shape[-1]) if k in _ROW_SHARDED else big[k] for k in _BIG]
    from_sibling = _swap_halves(blocked)
    c_idx = jnp.reshape(ic, (1,)).astype(jnp.int32)
    pair = [_pair_sum("pair_sum_" + k, g, q, c_idx) for k, g, q in zip(_BIG, blocked, from_sibling)]
    arrived = _scatter_chips(pair)
    halves = [_sum_chips("sum_chips_" + k, r) for k, r in zip(_BIG, arrived)]
    whole = _share_halves(halves)
    for k, g in zip(_BIG, whole):
        grads[k] = g.reshape(-1, g.shape[-1])

    delta, new_m, new_v = {}, {}, {}
    dmod_cols = lax.dynamic_slice(dmod_all, (0, k_me * ada_cols), (N_DEV, ada_cols))
    grads["w_ada"], delta["w_ada"], new_m["w_ada"], new_v["w_ada"] = _ada_update(
        c_act.T, dmod_cols, w["w_ada"], m["w_ada"], v["w_ada"])
    for k in _WEIGHTS:
        if k != "w_ada":
            delta[k], new_m[k], new_v[k] = _adamw("adamw_" + k, w[k], grads[k], m[k], v[k])
    outs = [loss_out, dx.reshape(x.shape)]
    for t in (grads, delta, new_m, new_v):
        outs += [t[k].reshape(shapes[k]) for k in _WEIGHTS]
    return tuple(outs)
```

```python
import functools

import jax
import jax.numpy as jnp
from jax import lax
from jax.experimental import pallas as pl
from jax.experimental.pallas import tpu as pltpu

F32 = jnp.float32
BF16 = jnp.bfloat16
MESH = pl.DeviceIdType.MESH

EPS = 1e-6
HEAD = 128
BLK = 16
N_CHIPS = 4
N_DEV = 8
VMEM_LIMIT_BYTES = 56 * 1024 * 1024

ADAM_LR = 0.001
ADAM_B1 = 0.9
ADAM_B2 = 0.999
ADAM_EPS = 1e-08
ADAM_WD = 0.01
ADAM_STEP = 10


def _pick(dim, pref, mult):
    if dim <= pref:
        return dim
    t = (pref // mult) * mult
    while t >= mult:
        if dim % t == 0:
            return t
        t -= mult
    return dim


def _sig(x):
    return 1.0 / (1.0 + jnp.exp(-x))


def _params(sem):
    return pltpu.CompilerParams(dimension_semantics=sem, vmem_limit_bytes=VMEM_LIMIT_BYTES)


def _gj(j, i, k):
    return j


def _gk(j, i, k):
    return k


def _wspec(arr, rt, ct, r_of, c_of):
    if arr.ndim == 2:
        return pl.BlockSpec((rt, ct), lambda j, i, k: (r_of(j, i, k), c_of(j, i, k)))
    per = arr.shape[2] // ct
    assert arr.shape[2] % ct == 0
    return pl.BlockSpec((None, rt, ct),
                        lambda j, i, k: (c_of(j, i, k) // per, r_of(j, i, k), c_of(j, i, k) % per))


def _matmul(name, pairs, acc_of, M, N, K, tm, tn, tk, extras, outs, epilogue):
    assert M % tm == 0 and N % tn == 0 and K % tk == 0, (name, M, N, K, tm, tn, tk)
    nj, ni, nk = N // tn, M // tm, K // tk
    n_pairs, n_extra, n_out = len(pairs), len(extras), len(outs)
    n_acc = max(acc_of) + 1
    in_specs, args, dims = [], [], []
    for a, am, b, bm in pairs:
        if am == 'n':
            in_specs.append(pl.BlockSpec((tm, tk), lambda j, i, k: (i, k)))
        else:
            in_specs.append(pl.BlockSpec((tk, tm), lambda j, i, k: (k, i)))
        if bm == 'n':
            in_specs.append(_wspec(b, tk, tn, _gk, _gj))
        else:
            in_specs.append(_wspec(b, tn, tk, _gj, _gk))
        dims.append((((1 if am == 'n' else 0,), (0 if bm == 'n' else 1,)), ((), ())))
        args += [a, b]
    for e, kind, off in extras:
        assert off % tn == 0, (name, off, tn)
        o = off // tn
        if kind == 'tile':
            in_specs.append(pl.BlockSpec((tm, tn), lambda j, i, k, o=o: (i, j + o)))
        else:
            in_specs.append(pl.BlockSpec((1, tn), lambda j, i, k, o=o: (0, j + o)))
        args.append(e)
    out_shape, out_specs = [], []
    for dt, nb in outs:
        if nb is None:
            out_shape.append(jax.ShapeDtypeStruct((M, N), dt))
            out_specs.append(pl.BlockSpec((tm, tn), lambda j, i, k: (i, j)))
        else:
            assert nb % tn == 0 and N % nb == 0
            per = nb // tn
            out_shape.append(jax.ShapeDtypeStruct((N // nb, M, nb), dt))
            out_specs.append(pl.BlockSpec((None, tm, tn), lambda j, i, k, per=per: (j // per, i, j % per)))

    def body(*refs):
        ab = refs[:2 * n_pairs]
        e_refs = refs[2 * n_pairs:2 * n_pairs + n_extra]
        o_refs = refs[2 * n_pairs + n_extra:2 * n_pairs + n_extra + n_out]
        acc_refs = refs[2 * n_pairs + n_extra + n_out:]
        sums = [None] * n_acc
        for p in range(n_pairs):
            prod = lax.dot_general(ab[2 * p][...], ab[2 * p + 1][...], dims[p], preferred_element_type=F32)
            a = acc_of[p]
            sums[a] = prod if sums[a] is None else sums[a] + prod

        def finish(accs):
            res = epilogue(accs, [e[...] for e in e_refs])
            for o_ref, r in zip(o_refs, res):
                o_ref[...] = r.astype(o_ref.dtype)

        if nk == 1:
            finish(sums)
        else:
            k = pl.program_id(2)

            @pl.when(k == 0)
            def _():
                for a in range(n_acc):
                    acc_refs[a][...] = sums[a]

            @pl.when(k > 0)
            def _():
                for a in range(n_acc):
                    acc_refs[a][...] += sums[a]

            @pl.when(k == nk - 1)
            def _():
                finish([acc_refs[a][...] for a in range(n_acc)])

    scratch = [pltpu.VMEM((tm, tn), F32) for _ in range(n_acc)] if nk > 1 else []
    res = pl.pallas_call(
        body, name=name, grid=(nj, ni, nk), in_specs=in_specs, out_specs=out_specs, out_shape=out_shape,
        scratch_shapes=scratch, compiler_params=_params(("parallel", "parallel", "arbitrary")),
    )(*args)
    return res


def _row_spec(tr, d):
    return pl.BlockSpec((tr, d), lambda i: (i, 0))


def _vec_spec(d):
    return pl.BlockSpec((1, d), lambda i: (0, 0))


def _norm_mod(name, x, g, sc, sh):
    S, D = x.shape
    tr = _pick(S, 256, 8)

    def body(x_ref, g_ref, sc_ref, sh_ref, h_ref):
        xv = x_ref[...]
        r = lax.rsqrt(jnp.mean(xv * xv, axis=-1, keepdims=True) + EPS)
        h_ref[...] = ((xv * r) * g_ref[...] * (1.0 + sc_ref[...]) + sh_ref[...]).astype(BF16)

    return pl.pallas_call(
        body, name=name, grid=(S // tr,),
        in_specs=[_row_spec(tr, D), _vec_spec(D), _vec_spec(D), _vec_spec(D)],
        out_specs=_row_spec(tr, D), out_shape=jax.ShapeDtypeStruct((S, D), BF16),
        compiler_params=_params(("parallel",)),
    )(x, g, sc, sh)


def _loss_head(x2, target, g, ff, gt):
    S, D = x2.shape
    tr = _pick(S, 256, 8)

    def body(x_ref, t_ref, g_ref, ff_ref, gt_ref, dx_ref, dffb_ref, loss_ref, dgt_ref, dg_ref):
        i = pl.program_id(0)

        @pl.when(i == 0)
        def _():
            loss_ref[...] = jnp.zeros_like(loss_ref)
            dgt_ref[...] = jnp.zeros_like(dgt_ref)
            dg_ref[...] = jnp.zeros_like(dg_ref)

        xv = x_ref[...]
        gv = g_ref[...]
        r = lax.rsqrt(jnp.mean(xv * xv, axis=-1, keepdims=True) + EPS)
        xh = xv * r
        err = xh * gv - t_ref[...]
        loss_ref[...] += 0.5 * jnp.sum(jnp.mean(err * err, axis=-1, keepdims=True))
        dy = err * (1.0 / D)
        dg_ref[...] += jnp.sum(dy * xh, axis=0, keepdims=True)
        dxh = dy * gv
        dx = r * (dxh - xh * jnp.mean(dxh * xh, axis=-1, keepdims=True))
        dx_ref[...] = dx
        dffb_ref[...] = (dx * gt_ref[...]).astype(BF16)
        dgt_ref[...] += jnp.sum(dx * ff_ref[...], axis=0, keepdims=True)

    return pl.pallas_call(
        body, name="loss_head", grid=(S // tr,),
        in_specs=[_row_spec(tr, D), _row_spec(tr, D), _vec_spec(D), _row_spec(tr, D), _vec_spec(D)],
        out_specs=[_row_spec(tr, D), _row_spec(tr, D), pl.BlockSpec((1, 128), lambda i: (0, 0)),
                   _vec_spec(D), _vec_spec(D)],
        out_shape=[jax.ShapeDtypeStruct((S, D), F32), jax.ShapeDtypeStruct((S, D), BF16),
                   jax.ShapeDtypeStruct((1, 128), F32), jax.ShapeDtypeStruct((1, D), F32),
                   jax.ShapeDtypeStruct((1, D), F32)],
        compiler_params=_params(("arbitrary",)),
    )(x2, target, g, ff, gt)


def _norm_mod_bwd(name, dh, x, g, sc, dres, gated=None):
    S, D = x.shape
    tr = _pick(S, 256, 8)
    n = S // tr
    with_gate = gated is not None

    def body(*refs):
        if with_gate:
            dh_ref, x_ref, g_ref, sc_ref, dres_ref, mo_ref, gt_ref, dx_ref, dsh_ref, dsc_ref, dg_ref, dmob_ref, dgt_ref = refs
        else:
            dh_ref, x_ref, g_ref, sc_ref, dres_ref, dx_ref, dsh_ref, dsc_ref, dg_ref = refs
        i = pl.program_id(0)

        @pl.when(i == 0)
        def _():
            dsh_ref[...] = jnp.zeros_like(dsh_ref)
            dsc_ref[...] = jnp.zeros_like(dsc_ref)
            if with_gate:
                dgt_ref[...] = jnp.zeros_like(dgt_ref)

        xv = x_ref[...]
        dhv = dh_ref[...]
        gv = g_ref[...]
        scale = 1.0 + sc_ref[...]
        r = lax.rsqrt(jnp.mean(xv * xv, axis=-1, keepdims=True) + EPS)
        xh = xv * r
        dsh_ref[...] += jnp.sum(dhv, axis=0, keepdims=True)
        dsc_ref[...] += jnp.sum(dhv * xh, axis=0, keepdims=True)
        dxh = dhv * (gv * scale)
        dx = dres_ref[...] + r * (dxh - xh * jnp.mean(dxh * xh, axis=-1, keepdims=True))
        dx_ref[...] = dx
        if with_gate:
            dmob_ref[...] = (dx * gt_ref[...]).astype(BF16)
            dgt_ref[...] += jnp.sum(dx * mo_ref[...], axis=0, keepdims=True)

        @pl.when(i == n - 1)
        def _():
            t = dsc_ref[...]
            dg_ref[...] = t * scale
            dsc_ref[...] = t * gv

    in_specs = [_row_spec(tr, D), _row_spec(tr, D), _vec_spec(D), _vec_spec(D), _row_spec(tr, D)]
    args = [dh, x, g, sc, dres]
    out_specs = [_row_spec(tr, D), _vec_spec(D), _vec_spec(D), _vec_spec(D)]
    out_shape = [jax.ShapeDtypeStruct((S, D), F32)] + [jax.ShapeDtypeStruct((1, D), F32)] * 3
    if with_gate:
        in_specs += [_row_spec(tr, D), _vec_spec(D)]
        args += list(gated)
        out_specs += [_row_spec(tr, D), _vec_spec(D)]
        out_shape += [jax.ShapeDtypeStruct((S, D), BF16), jax.ShapeDtypeStruct((1, D), F32)]
    return pl.pallas_call(
        body, name=name, grid=(n,), in_specs=in_specs, out_specs=out_specs, out_shape=out_shape,
        compiler_params=_params(("arbitrary",)),
    )(*args)


CONV_ROWS = 256


def _col_spec(S, off_cols):
    o = off_cols // HEAD
    return pl.BlockSpec((S, HEAD), lambda c, o=o: (0, c + o))


def _conv_taps(u, u_prev, rid):
    s1 = jnp.where(rid >= 1, pltpu.roll(u, 1, 0), pltpu.roll(u_prev, 1, 0))
    s2 = jnp.where(rid >= 2, pltpu.roll(u, 2, 0), pltpu.roll(u_prev, 2, 0))
    return s1, s2


def _conv_fwd(proj, conv_w, d_conv):
    S = proj.shape[0]
    R = min(CONV_ROWS, S)
    nr = S // R

    def body(ab_ref, ac_ref, ax_ref, w_ref, ya_ref):
        w0, w1, w2 = w_ref[0:1, :], w_ref[1:2, :], w_ref[2:3, :]
        rid = lax.broadcasted_iota(jnp.int32, (R, HEAD), 0)

        def step(c, carry):
            rows = pl.ds(pl.multiple_of(c * R, R), R)
            prev = pl.ds(pl.multiple_of(jnp.maximum(c - 1, 0) * R, R), R)
            u = ac_ref[rows, :] * ax_ref[rows, :]
            u_prev = jnp.where(c > 0, ac_ref[prev, :] * ax_ref[prev, :], 0.0)
            s1, s2 = _conv_taps(u, u_prev, rid)
            y = w0 * s2 + w1 * s1 + w2 * u
            ya_ref[rows, :] = (ab_ref[rows, :] * y).astype(BF16)
            return carry

        lax.fori_loop(0, nr, step, 0)

    return pl.pallas_call(
        body, name="conv_fwd", grid=(d_conv // HEAD,),
        in_specs=[_col_spec(S, 0), _col_spec(S, d_conv), _col_spec(S, 2 * d_conv),
                  pl.BlockSpec((3, HEAD), lambda c: (0, c))],
        out_specs=pl.BlockSpec((S, HEAD), lambda c: (0, c)),
        out_shape=jax.ShapeDtypeStruct((S, d_conv), BF16),
        compiler_params=_params(("parallel",)),
    )(proj, proj, proj, conv_w)


def _conv_bwd(dya, proj, conv_w, d_conv):
    S = proj.shape[0]
    R = min(CONV_ROWS, S)
    nr = S // R

    def body(dya_ref, ab_ref, ac_ref, ax_ref, w_ref, dab_ref, dac_ref, dax_ref, dw_ref):
        w0, w1, w2 = w_ref[0:1, :], w_ref[1:2, :], w_ref[2:3, :]
        rid = lax.broadcasted_iota(jnp.int32, (R, HEAD), 0)

        def step(c, carry):
            dw0, dw1, dw2 = carry
            rows = pl.ds(pl.multiple_of(c * R, R), R)
            prev = pl.ds(pl.multiple_of(jnp.maximum(c - 1, 0) * R, R), R)
            nxt = pl.ds(pl.multiple_of(jnp.minimum(c + 1, nr - 1) * R, R), R)
            a_c, a_x, a_b = ac_ref[rows, :], ax_ref[rows, :], ab_ref[rows, :]
            u = a_c * a_x
            u_prev = jnp.where(c > 0, ac_ref[prev, :] * ax_ref[prev, :], 0.0)
            s1, s2 = _conv_taps(u, u_prev, rid)
            y = w0 * s2 + w1 * s1 + w2 * u
            dy = dya_ref[rows, :]
            dab_ref[rows, :] = (dy * y).astype(BF16)
            dyc = dy * a_b
            dyc_next = jnp.where(c < nr - 1, dya_ref[nxt, :] * ab_ref[nxt, :], 0.0)
            t1 = jnp.where(rid < R - 1, pltpu.roll(dyc, R - 1, 0), pltpu.roll(dyc_next, R - 1, 0))
            t2 = jnp.where(rid < R - 2, pltpu.roll(dyc, R - 2, 0), pltpu.roll(dyc_next, R - 2, 0))
            du = w2 * dyc + w1 * t1 + w0 * t2
            dac_ref[rows, :] = (du * a_x).astype(BF16)
            dax_ref[rows, :] = (du * a_c).astype(BF16)
            dw0 = dw0 + jnp.sum(dyc * s2, axis=0, keepdims=True)
            dw1 = dw1 + jnp.sum(dyc * s1, axis=0, keepdims=True)
            dw2 = dw2 + jnp.sum(dyc * u, axis=0, keepdims=True)
            return dw0, dw1, dw2

        z = jnp.zeros((1, HEAD), F32)
        dw0, dw1, dw2 = lax.fori_loop(0, nr, step, (z, z, z))
        dw_ref[0:1, :] = dw0
        dw_ref[1:2, :] = dw1
        dw_ref[2:3, :] = dw2

    col = pl.BlockSpec((S, HEAD), lambda c: (0, c))
    return pl.pallas_call(
        body, name="conv_bwd", grid=(d_conv // HEAD,),
        in_specs=[col, _col_spec(S, 0), _col_spec(S, d_conv), _col_spec(S, 2 * d_conv),
                  pl.BlockSpec((3, HEAD), lambda c: (0, c))],
        out_specs=[col, col, col, pl.BlockSpec((3, HEAD), lambda c: (0, c))],
        out_shape=[jax.ShapeDtypeStruct((S, d_conv), BF16)] * 3 + [jax.ShapeDtypeStruct((3, d_conv), F32)],
        compiler_params=_params(("parallel",)),
    )(dya, proj, proj, proj, conv_w)


HGRN_ROWS = 256


def _block_cumsum(x, pos):
    for s in (1, 2, 4, 8):
        x = x + jnp.where(pos >= s, pltpu.roll(x, s, 0), 0.0)
    return x


def _block_rev_cumsum(x, pos, rows):
    for s in (1, 2, 4, 8):
        x = x + jnp.where(pos < BLK - s, pltpu.roll(x, rows - s, 0), 0.0)
    return x


def _block_last(x, pos, rows):
    m = jnp.where(pos == BLK - 1, x, 0.0)
    for s in (1, 2, 4, 8):
        m = m + pltpu.roll(m, rows - s, 0)
    return m


def _hgrn_gates(q, z, lb):
    sgq = _sig(q)
    qs = q * sgq
    sg = _sig(z)
    f = lb + (1.0 - lb) * sg
    kk = (1.0 - lb) * (1.0 - sg)
    return sgq, qs, sg, f, kk


def _hgrn_specs(T, d_conv, n_t, rev):
    def t_of(i):
        return (n_t - 1 - i) if rev else i

    def pcol(off):
        o = off // HEAD
        return pl.BlockSpec((T, HEAD), lambda h, i, o=o: (t_of(i), h + o))

    q_spec, z_spec, v_spec, g_spec = pcol(3 * d_conv), pcol(4 * d_conv), pcol(5 * d_conv), pcol(6 * d_conv)
    lb_spec = pl.BlockSpec((1, HEAD), lambda h, i: (0, h))
    gn_spec = pl.BlockSpec((1, HEAD), lambda h, i: (0, 0))
    act_spec = pl.BlockSpec((T, HEAD), lambda h, i: (t_of(i), h))
    st_spec = pl.BlockSpec((None, T // BLK, HEAD, HEAD), lambda h, i: (h, t_of(i), 0, 0))
    return q_spec, z_spec, v_spec, g_spec, lb_spec, gn_spec, act_spec, st_spec


def _hgrn_fwd(proj, lb, gn, d_conv):
    S = proj.shape[0]
    H = d_conv // HEAD
    T = min(HGRN_ROWS, S)
    n_t, nb = S // T, T // BLK
    q_spec, z_spec, v_spec, g_spec, lb_spec, gn_spec, act_spec, st_spec = _hgrn_specs(T, d_conv, n_t, False)

    def body(q_ref, z_ref, v_ref, g_ref, lb_ref, gn_ref, ob_ref, o_ref, st_ref, state, qe_s, ke_s, v_s, dec_s, o_s):
        @pl.when(pl.program_id(1) == 0)
        def _():
            state[...] = jnp.zeros_like(state)

        pos = lax.broadcasted_iota(jnp.int32, (T, HEAD), 0) % BLK
        v = v_ref[...]
        _, qs, _, f, kk = _hgrn_gates(q_ref[...], z_ref[...], lb_ref[...])
        b = _block_cumsum(jnp.log(f), pos)
        b_tot = _block_last(b, pos, T)
        qe_s[...] = (qs * jnp.exp(b)).astype(BF16)
        ke_s[...] = (kk * jnp.exp(b_tot - b)).astype(BF16)
        v_s[...] = v.astype(BF16)
        dec_s[...] = jnp.exp(b_tot)
        o = jnp.sum(qs * kk, axis=-1, keepdims=True) * v
        for d in range(1, BLK):
            e = jnp.exp(b - pltpu.roll(b, d, 0))
            a = jnp.sum(qs * pltpu.roll(kk, d, 0) * e, axis=-1, keepdims=True)
            o = o + jnp.where(pos >= d, a * pltpu.roll(v, d, 0), 0.0)
        o_s[...] = o

        def step(j, st):
            rows = pl.ds(pl.multiple_of(j * BLK, BLK), BLK)
            stb = st.astype(BF16)
            st_ref[j] = stb
            o_s[rows, :] += lax.dot_general(qe_s[rows, :], stb, (((1,), (1,)), ((), ())),
                                            preferred_element_type=F32)
            upd = lax.dot_general(v_s[rows, :], ke_s[rows, :], (((0,), (0,)), ((), ())),
                                  preferred_element_type=F32)
            return st * dec_s[pl.ds(pl.multiple_of(j * BLK, BLK), 1), :] + upd

        state[...] = lax.fori_loop(0, nb, step, state[...])
        o = o_s[...]
        o_ref[...] = o
        r = lax.rsqrt(jnp.mean(o * o, axis=-1, keepdims=True) + EPS)
        g = g_ref[...]
        ob_ref[...] = ((o * r) * gn_ref[...] * (g * _sig(g))).astype(BF16)

    return pl.pallas_call(
        body, name="hgrn_fwd", grid=(H, n_t),
        in_specs=[q_spec, z_spec, v_spec, g_spec, lb_spec, gn_spec],
        out_specs=[act_spec, act_spec, st_spec],
        out_shape=[jax.ShapeDtypeStruct((S, d_conv), BF16), jax.ShapeDtypeStruct((S, d_conv), F32),
                   jax.ShapeDtypeStruct((H, S // BLK, HEAD, HEAD), BF16)],
        scratch_shapes=[pltpu.VMEM((HEAD, HEAD), F32), pltpu.VMEM((T, HEAD), BF16), pltpu.VMEM((T, HEAD), BF16),
                        pltpu.VMEM((T, HEAD), BF16), pltpu.VMEM((T, HEAD), F32), pltpu.VMEM((T, HEAD), F32)],
        compiler_params=_params(("parallel", "arbitrary")),
    )(proj, proj, proj, proj, lb, gn)


def _hgrn_bwd(dob, o_raw, states, proj, lb, gn, d_conv):
    S = proj.shape[0]
    H = d_conv // HEAD
    T = min(HGRN_ROWS, S)
    n_t, nb = S // T, T // BLK
    q_spec, z_spec, v_spec, g_spec, lb_spec, gn_spec, act_spec, st_spec = _hgrn_specs(T, d_conv, n_t, True)

    def body(dob_ref, o_ref, st_ref, q_ref, z_ref, v_ref, g_ref, lb_ref, gn_ref,
             dq_ref, dz_ref, dv_ref, dg_ref, dlb_ref, dgn_ref,
             dstate, do_b, qe_b, ke_b, v_b, dec_s, dqe_s, dke_s, dvi_s, ddec_s):
        @pl.when(pl.program_id(1) == 0)
        def _():
            dstate[...] = jnp.zeros_like(dstate)
            dlb_ref[...] = jnp.zeros_like(dlb_ref)
            dgn_ref[...] = jnp.zeros_like(dgn_ref)

        pos = lax.broadcasted_iota(jnp.int32, (T, HEAD), 0) % BLK
        lbv, gnv = lb_ref[...], gn_ref[...]
        q, v, g = q_ref[...], v_ref[...], g_ref[...]
        sgq, qs, sg, f, kk = _hgrn_gates(q, z_ref[...], lbv)
        b = _block_cumsum(jnp.log(f), pos)
        b_tot = _block_last(b, pos, T)
        eb = jnp.exp(b)
        ek = jnp.exp(b_tot - b)
        dec = jnp.exp(b_tot)
        qe, ke = qs * eb, kk * ek
        o = o_ref[...]
        r = lax.rsqrt(jnp.mean(o * o, axis=-1, keepdims=True) + EPS)
        on = o * r
        sgg = _sig(g)
        dobv = dob_ref[...]
        dog = dobv * (g * sgg)
        dgn_ref[...] += jnp.sum(dog * on, axis=0, keepdims=True)
        don = dog * gnv
        do = r * (don - on * jnp.mean(don * on, axis=-1, keepdims=True))
        dg_ref[...] = (dobv * (on * gnv) * (sgg * (1.0 + g * (1.0 - sgg)))).astype(BF16)

        do_b[...] = do.astype(BF16)
        qe_b[...] = qe.astype(BF16)
        ke_b[...] = ke.astype(BF16)
        v_b[...] = v.astype(BF16)
        dec_s[...] = dec

        def step(jj, dst):
            j = nb - 1 - jj
            rows = pl.ds(pl.multiple_of(j * BLK, BLK), BLK)
            stj = st_ref[j]
            dstb = dst.astype(BF16)
            dob_j = do_b[rows, :]
            dqe_s[rows, :] = lax.dot_general(dob_j, stj, (((1,), (0,)), ((), ())), preferred_element_type=F32)
            dke_s[rows, :] = lax.dot_general(v_b[rows, :], dstb, (((1,), (0,)), ((), ())), preferred_element_type=F32)
            dvi_s[rows, :] = lax.dot_general(ke_b[rows, :], dstb, (((1,), (1,)), ((), ())), preferred_element_type=F32)
            ddec = jnp.sum(stj.astype(F32) * dst, axis=0, keepdims=True)
            ddec_s[rows, :] = jnp.broadcast_to(ddec, (BLK, HEAD))
            upd = lax.dot_general(dob_j, qe_b[rows, :], (((0,), (0,)), ((), ())), preferred_element_type=F32)
            return dst * dec_s[pl.ds(pl.multiple_of(j * BLK, BLK), 1), :] + upd

        dstate[...] = lax.fori_loop(0, nb, step, dstate[...])

        dqe, dke = dqe_s[...], dke_s[...]
        dqs = dqe * eb
        dkk = dke * ek
        tke = dke * ke
        db = dqe * qe - tke
        db = db + jnp.where(pos == BLK - 1, _block_cumsum(tke, pos) + ddec_s[...] * dec, 0.0)
        dv = dvi_s[...]
        da = jnp.sum(do * v, axis=-1, keepdims=True)
        dqs = dqs + da * kk
        dkk = dkk + da * qs
        dv = dv + jnp.sum(qs * kk, axis=-1, keepdims=True) * do
        for d in range(1, BLK):
            kd = pltpu.roll(kk, d, 0)
            e = jnp.where(pos >= d, jnp.exp(b - pltpu.roll(b, d, 0)), 0.0)
            a = jnp.sum(qs * kd * e, axis=-1, keepdims=True)
            da = jnp.sum(do * pltpu.roll(v, d, 0), axis=-1, keepdims=True)
            gq = da * e
            dqs = dqs + gq * kd
            pk = gq * qs
            dkk = dkk + pltpu.roll(pk, T - d, 0)
            pb = pk * kd
            db = db + pb - pltpu.roll(pb, T - d, 0)
            dv = dv + pltpu.roll(jnp.where(pos >= d, a * do, 0.0), T - d, 0)
        dlf = _block_rev_cumsum(db, pos, T)
        df = dlf / f - dkk
        dlb_ref[...] += jnp.sum(df * (1.0 - sg), axis=0, keepdims=True)
        dz_ref[...] = (df * (1.0 - lbv) * sg * (1.0 - sg)).astype(BF16)
        dq_ref[...] = (dqs * (sgq * (1.0 + q * (1.0 - sgq)))).astype(BF16)
        dv_ref[...] = dv.astype(BF16)

    tb = lambda dt: pltpu.VMEM((T, HEAD), dt)
    return pl.pallas_call(
        body, name="hgrn_bwd", grid=(H, n_t),
        in_specs=[act_spec, act_spec, st_spec, q_spec, z_spec, v_spec, g_spec, lb_spec, gn_spec],
        out_specs=[act_spec, act_spec, act_spec, act_spec, lb_spec,
                   pl.BlockSpec((None, 1, HEAD), lambda h, i: (h, 0, 0))],
        out_shape=[jax.ShapeDtypeStruct((S, d_conv), BF16)] * 4
        + [jax.ShapeDtypeStruct((1, d_conv), F32), jax.ShapeDtypeStruct((H, 1, HEAD), F32)],
        scratch_shapes=[pltpu.VMEM((HEAD, HEAD), F32), tb(BF16), tb(BF16), tb(BF16), tb(BF16), tb(F32),
                        tb(F32), tb(F32), tb(F32), tb(F32)],
        compiler_params=_params(("parallel", "arbitrary")),
    )(dob, o_raw, states, proj, proj, proj, proj, lb, gn)


def _first(accs, extras):
    return [accs[0]]


def _local_step(x, target, mod, norm_mix_g, lb, gnorm_g, norm_ffn_g, norm_final_g, conv_w,
                w_in, w_conv_out, w_hgrn_out, w_o, w_ffn_gate, w_ffn_up, w_ffn_down):
    S, D = x.shape
    d_conv = D // 2
    d_in = 7 * d_conv + 2 * D
    d_ff = w_ffn_down.shape[0]
    nb_in, nb_co, nb_ff = w_in.shape[2], w_conv_out.shape[2], w_ffn_gate.shape[2]
    sh_m, sc_m, gt_m, sh_f, sc_f, gt_f = [mod[:, i * D:(i + 1) * D] for i in range(6)]
    tm = _pick(S, 512, 16)
    ts = _pick(S, 1024, 128)
    tn_in = _pick(nb_in, 1408, 128)
    tn_co = _pick(nb_co, 512, 128)
    tn_ff = _pick(nb_ff, 1408, 128)
    tn_d = _pick(D, 512, 128)
    tw = _pick(D, 1024, 128)
    ts2 = _pick(S, 2048, 128)

    h = _norm_mod("norm_mix", x, norm_mix_g, sc_m, sh_m)
    proj, = _matmul("proj", [(h, 'n', w_in, 'n')], [0], S, d_in, D, tm, tn_in, D, [], [(F32, None)], _first)
    ya = _conv_fwd(proj, conv_w, d_conv)
    ob, o_raw, states = _hgrn_fwd(proj, lb, gnorm_g, d_conv)

    def merge_epi(accs, ex):
        y_a, y_b = accs
        return [y_a, y_b, _sig(ex[0]) * y_a + _sig(ex[1]) * y_b]

    y_a, y_b, merged = _matmul(
        "branch_out", [(ya, 'n', w_conv_out, 'n'), (ob, 'n', w_hgrn_out, 'n')], [0, 1], S, D, d_conv, tm, tn_co, d_conv,
        [(proj, 'tile', 7 * d_conv), (proj, 'tile', 7 * d_conv + D)], [(F32, None), (F32, None), (BF16, None)], merge_epi)

    def resid_epi(accs, ex):
        return [accs[0], ex[0] + ex[1] * accs[0]]

    mo, x1 = _matmul("w_o", [(merged, 'n', w_o, 'n')], [0], S, D, D, tm, tw, D,
                     [(x, 'tile', 0), (gt_m, 'row', 0)], [(F32, None), (F32, None)], resid_epi)
    h2 = _norm_mod("norm_ffn", x1, norm_ffn_g, sc_f, sh_f)

    def swiglu_epi(accs, ex):
        gg, uu = accs
        return [gg, uu, gg * _sig(gg) * uu]

    G, U, act = _matmul("ffn_in", [(h2, 'n', w_ffn_gate, 'n'), (h2, 'n', w_ffn_up, 'n')], [0, 1], S, d_ff, D,
                        tm, tn_ff, D, [], [(F32, None), (F32, None), (BF16, None)], swiglu_epi)
    ff, x2 = _matmul("ffn_out", [(act, 'n', w_ffn_down, 'n')], [0], S, D, d_ff, tm, D, _pick(d_ff, 1408, 128),
                     [(x1, 'tile', 0), (gt_f, 'row', 0)], [(F32, None), (F32, None)], resid_epi)

    dx2, dffb, loss, dgt_f, dg_final = _loss_head(x2, target, norm_final_g, ff, gt_f)

    def swiglu_bwd_epi(accs, ex):
        dact, gg, uu = accs[0], ex[0], ex[1]
        s = _sig(gg)
        return [dact * uu * (s * (1.0 + gg * (1.0 - s))), dact * (gg * s)]

    dG, dU = _matmul("d_act", [(dffb, 'n', w_ffn_down, 't')], [0], S, d_ff, D, tm, tn_ff, D,
                     [(G, 'tile', 0), (U, 'tile', 0)], [(BF16, None), (BF16, None)], swiglu_bwd_epi)
    dw_down, = _matmul("dw_ffn_down", [(act, 't', dffb, 'n')], [0], d_ff, D, S, _pick(d_ff, 1408, 128),
                       D, ts, [], [(BF16, None)], _first)
    dh2, = _matmul("d_h2", [(dG, 'n', w_ffn_gate, 't'), (dU, 'n', w_ffn_up, 't')], [0, 0], S, D, d_ff,
                   tm, D, tn_ff, [], [(F32, None)], _first)
    dw_gate, dw_up = _matmul("dw_ffn_in", [(h2, 't', dG, 'n'), (h2, 't', dU, 'n')], [0, 1], D, d_ff, S,
                             tw, tn_ff, ts, [], [(BF16, nb_ff), (BF16, nb_ff)],
                             lambda accs, ex: accs)
    dx1, dsh_f, dsc_f, dg_ffn, dmob, dgt_m = _norm_mod_bwd("norm_ffn_bwd", dh2, x1, norm_ffn_g, sc_f, dx2, (mo, gt_m))

    def merge_bwd_epi(accs, ex):
        dm, ga, gb, ya_, yb_ = accs[0], ex[0], ex[1], ex[2], ex[3]
        sa, sb = _sig(ga), _sig(gb)
        return [dm * ya_ * sa * (1.0 - sa), dm * yb_ * sb * (1.0 - sb), dm * sa, dm * sb]

    dga, dgb, dy_a, dy_b = _matmul(
        "d_merged", [(dmob, 'n', w_o, 't')], [0], S, D, D, tm, tn_co, D,
        [(proj, 'tile', 7 * d_conv), (proj, 'tile', 7 * d_conv + D), (y_a, 'tile', 0), (y_b, 'tile', 0)],
        [(BF16, None)] * 4, merge_bwd_epi)
    dw_o, = _matmul("dw_o", [(merged, 't', dmob, 'n')], [0], D, D, S, _pick(D, 512, 128), _pick(D, 1024, 128), ts,
                    [], [(BF16, None)], _first)
    tn_c = _pick(d_conv, 512, 128)
    dya, = _matmul("d_ya", [(dy_a, 'n', w_conv_out, 't')], [0], S, d_conv, D, tm, tn_c, tn_co, [], [(F32, None)], _first)
    dob, = _matmul("d_ob", [(dy_b, 'n', w_hgrn_out, 't')], [0], S, d_conv, D, tm, tn_c, tn_co, [], [(F32, None)], _first)
    dw_co, = _matmul("dw_conv_out", [(ya, 't', dy_a, 'n')], [0], d_conv, D, S, _pick(d_conv, 512, 128), tn_co, ts,
                     [], [(BF16, nb_co)], _first)
    dw_ho, = _matmul("dw_hgrn_out", [(ob, 't', dy_b, 'n')], [0], d_conv, D, S, _pick(d_conv, 512, 128), tn_co, ts,
                     [], [(BF16, nb_co)], _first)
    dab, dac, dax, dconv_w = _conv_bwd(dya, proj, conv_w, d_conv)
    dq, dz, dv, dgo, dlb, dgn = _hgrn_bwd(dob, o_raw, states, proj, lb, gnorm_g, d_conv)
    dproj = jnp.concatenate([dab, dac, dax, dq, dz, dv, dgo, dga, dgb], axis=1)
    dh, = _matmul("d_h", [(dproj, 'n', w_in, 't')], [0], S, D, d_in, tm, D, tn_in, [], [(F32, None)], _first)
    dw_in, = _matmul("dw_in", [(h, 't', dproj, 'n')], [0], D, d_in, S, tw, tn_in, ts2,
                     [], [(BF16, nb_in)], _first)
    dx, dsh_m, dsc_m, dg_mix = _norm_mod_bwd("norm_mix_bwd", dh, x, norm_mix_g, sc_m, dx1)
    dmod = jnp.concatenate([dsh_m, dsc_m, dgt_m, dsh_f, dsc_f, dgt_f], axis=1)
    small = dict(dmod=dmod, dg_mix=dg_mix, dg_ffn=dg_ffn, dg_final=dg_final, dlb=dlb,
                 dgn=jnp.sum(dgn, axis=0), dconv_w=dconv_w)
    big = dict(w_in=dw_in, w_conv_out=dw_co, w_hgrn_out=dw_ho, w_o=dw_o, w_ffn_gate=dw_gate, w_ffn_up=dw_up,
               w_ffn_down=dw_down)
    return loss, dx, small, big


def _adamw_math(w, g, m, v):
    m = ADAM_B1 * m + (1.0 - ADAM_B1) * g
    v = ADAM_B2 * v + (1.0 - ADAM_B2) * (g * g)
    m_hat = m / (1.0 - ADAM_B1 ** ADAM_STEP)
    v_hat = v / (1.0 - ADAM_B2 ** ADAM_STEP)
    delta = -ADAM_LR * (m_hat / (jnp.sqrt(v_hat) + ADAM_EPS) + ADAM_WD * w)
    return delta, m, v


def _adamw(name, w, g, m, v):
    R, C = w.shape
    tr = _pick(R, max(8, (256 * 1024) // C // 8 * 8), 8)
    spec = pl.BlockSpec((tr, C), lambda i: (i, 0))

    def body(w_ref, g_ref, m_ref, v_ref, d_ref, mo_ref, vo_ref):
        d, m2, v2 = _adamw_math(w_ref[...], g_ref[...], m_ref[...], v_ref[...])
        d_ref[...] = d
        mo_ref[...] = m2
        vo_ref[...] = v2

    return pl.pallas_call(
        body, name=name, grid=(R // tr,), in_specs=[spec] * 4, out_specs=[spec] * 3,
        out_shape=[jax.ShapeDtypeStruct((R, C), F32)] * 3, compiler_params=_params(("parallel",)),
    )(w, g, m, v)


def _ada_update(c_act_t, dmod, w, m, v):
    R, C = w.shape
    B = dmod.shape[0]
    tr = _pick(R, 256, 8)
    tc = _pick(C, 1536, 128)
    spec = pl.BlockSpec((tr, tc), lambda i, j: (i, j))

    def body(ct_ref, dm_ref, w_ref, m_ref, v_ref, g_ref, d_ref, mo_ref, vo_ref):
        ct = ct_ref[...]
        dm = dm_ref[...]
        g = ct[:, 0:1] * dm[0:1, :]
        for b in range(1, B):
            g = g + ct[:, b:b + 1] * dm[b:b + 1, :]
        d, m2, v2 = _adamw_math(w_ref[...], g, m_ref[...], v_ref[...])
        g_ref[...] = g
        d_ref[...] = d
        mo_ref[...] = m2
        vo_ref[...] = v2

    return pl.pallas_call(
        body, name="ada_update", grid=(R // tr, C // tc),
        in_specs=[pl.BlockSpec((tr, B), lambda i, j: (i, 0)), pl.BlockSpec((B, tc), lambda i, j: (0, j)),
                  spec, spec, spec],
        out_specs=[spec] * 4, out_shape=[jax.ShapeDtypeStruct((R, C), F32)] * 4,
        compiler_params=_params(("parallel", "parallel")),
    )(c_act_t, dmod, w, m, v)


def _prep(c_all, lb_param):
    def body(c_ref, p_ref, ca_ref, cab_ref, lb_ref):
        cv = c_ref[...]
        ca = cv * _sig(cv)
        ca_ref[...] = ca
        cab_ref[...] = ca.astype(BF16)
        lb_ref[...] = _sig(p_ref[0:1, :] - p_ref[1:2, :])

    return pl.pallas_call(
        body, name="prep",
        out_shape=[jax.ShapeDtypeStruct(c_all.shape, F32), jax.ShapeDtypeStruct(c_all.shape, BF16),
                   jax.ShapeDtypeStruct((1, lb_param.shape[1]), F32)],
    )(c_all, lb_param)


def _small_reduce(parts):
    W = parts.shape[1]

    def body(p_ref, o_ref):
        acc = p_ref[0:1, :]
        for d in range(1, N_DEV):
            acc = acc + p_ref[d:d + 1, :]
        o_ref[...] = acc

    return pl.pallas_call(body, name="small_reduce", out_shape=jax.ShapeDtypeStruct((1, W), F32))(parts)


def _lb_grad(dlb, lb):
    def body(d_ref, lb_ref, o_ref):
        t = d_ref[...] * lb_ref[...] * (1.0 - lb_ref[...])
        o_ref[0:1, :] = t
        o_ref[1:2, :] = -t

    return pl.pallas_call(body, name="lb_grad", out_shape=jax.ShapeDtypeStruct((2, dlb.shape[1]), F32))(dlb, lb)


_HBM = pl.BlockSpec(memory_space=pltpu.HBM)


def _place():
    x, y, c = lax.axis_index("x"), lax.axis_index("y"), lax.axis_index("c")
    chips = [(1 - x, y), (x, 1 - y), (1 - x, 1 - y)]
    return x, y, c, chips


def _allgather_rows(name, v):
    m_per, n = v.shape

    def body(x_ref, out_ref, send_sems, recv_sems, local_sem):
        x, y, c, chips = _place()
        me, sibling = (x, y, c), (x, y, 1 - c)

        def rows(px, py, pc):
            return out_ref.at[pl.ds((4 * px + 2 * py + pc) * m_per, m_per), :]

        def copy(k, block, to, src=None):
            return pltpu.make_async_remote_copy(
                src_ref=rows(*block) if src is None else src, dst_ref=rows(*block),
                send_sem=send_sems.at[k], recv_sem=recv_sems.at[k], device_id=to, device_id_type=MESH)

        mine = pltpu.make_async_copy(x_ref, rows(*me), local_sem)
        mine.start()
        first = [copy(0, me, sibling, src=x_ref)]
        first += [copy(1 + j, me, (*chip, c), src=x_ref) for j, chip in enumerate(chips)]
        for cp in first:
            cp.start()
        passed = [copy(4 + j, (*chip, c), sibling) for j, chip in enumerate(chips)]
        for j, chip in enumerate(chips):
            copy(1 + j, (*chip, c), me).wait_recv()
            passed[j].start()
        copy(0, sibling, me).wait_recv()
        for j, chip in enumerate(chips):
            copy(4 + j, (*chip, 1 - c), me).wait_recv()
        for cp in first + passed:
            cp.wait_send()
        mine.wait()

    return pl.pallas_call(
        body, name=name, out_shape=jax.ShapeDtypeStruct((N_DEV * m_per, n), v.dtype),
        in_specs=[pl.BlockSpec(memory_space=pltpu.VMEM)], out_specs=pl.BlockSpec(memory_space=pltpu.VMEM),
        scratch_shapes=[pltpu.SemaphoreType.DMA((7,)), pltpu.SemaphoreType.DMA((7,)), pltpu.SemaphoreType.DMA],
    )(v)


def _cast_place(name, w, k_idx):
    R, C = w.shape
    tr = _pick(R, max(16, (512 * 1024) // C // 16 * 16), 16)

    def body(k_ref, w_ref, o_ref):
        o_ref[...] = w_ref[...].astype(BF16)

    return pl.pallas_call(
        body, name=name,
        grid_spec=pltpu.PrefetchScalarGridSpec(
            num_scalar_prefetch=1, grid=(R // tr,),
            in_specs=[pl.BlockSpec((tr, C), lambda i, k_ref: (i, 0))],
            out_specs=pl.BlockSpec((None, tr, C), lambda i, k_ref: (k_ref[0], i, 0))),
        out_shape=jax.ShapeDtypeStruct((N_CHIPS, R, C), BF16),
        compiler_params=_params(("parallel",)),
    )(k_idx, w)


def _gather_weights(bufs):
    n = len(bufs)

    def body(*refs):
        out_refs = refs[n:2 * n]
        send_sems, recv_sems = refs[2 * n:]
        x, y, c, chips = _place()
        k_me = 2 * x + y
        sibling = (x, y, 1 - c)

        def region(a, k, hc):
            rh = bufs[a].shape[1] // 2
            return out_refs[a].at[k, pl.ds(hc * rh, rh)]

        def copy(a, s, src, dst, to):
            return pltpu.make_async_remote_copy(src_ref=src, dst_ref=dst, send_sem=send_sems.at[6 * a + s],
                                                recv_sem=recv_sems.at[6 * a + s], device_id=to, device_id_type=MESH)

        started = []
        for a in range(n):
            mine = region(a, k_me, c)
            for j, chip in enumerate(chips):
                cp = copy(a, j, mine, mine, (*chip, c))
                cp.start()
                started.append(cp)
        for a in range(n):
            for j, chip in enumerate(chips):
                got = region(a, 2 * chip[0] + chip[1], c)
                copy(a, j, got, got, (x, y, c)).wait_recv()
                cp = copy(a, 3 + j, got, got, sibling)
                cp.start()
                started.append(cp)
        for a in range(n):
            for j, chip in enumerate(chips):
                got = region(a, 2 * chip[0] + chip[1], 1 - c)
                copy(a, 3 + j, got, got, (x, y, c)).wait_recv()
        for cp in started:
            cp.wait_send()

    return pl.pallas_call(
        body, name="gather_weights",
        out_shape=[jax.ShapeDtypeStruct(b.shape, b.dtype) for b in bufs],
        in_specs=[_HBM] * n, out_specs=[_HBM] * n, input_output_aliases={a: a for a in range(n)},
        scratch_shapes=[pltpu.SemaphoreType.DMA((6 * n,)), pltpu.SemaphoreType.DMA((6 * n,))],
    )(*bufs)


def _swap_halves(grads):
    n = len(grads)

    def body(*refs):
        in_refs, out_refs = refs[:n], refs[n:2 * n]
        send_sems, recv_sems = refs[2 * n:]
        x, y, c, _ = _place()
        cps = []
        for a in range(n):
            rh = grads[a].shape[1] // 2
            cp = pltpu.make_async_remote_copy(
                src_ref=in_refs[a].at[:, pl.ds((1 - c) * rh, rh)], dst_ref=out_refs[a],
                send_sem=send_sems.at[a], recv_sem=recv_sems.at[a], device_id=(x, y, 1 - c), device_id_type=MESH)
            cp.start()
            cps.append(cp)
        for cp in cps:
            cp.wait()

    return pl.pallas_call(
        body, name="swap_halves",
        out_shape=[jax.ShapeDtypeStruct((N_CHIPS, g.shape[1] // 2, g.shape[2]), g.dtype) for g in grads],
        in_specs=[_HBM] * n, out_specs=[_HBM] * n,
        scratch_shapes=[pltpu.SemaphoreType.DMA((n,)), pltpu.SemaphoreType.DMA((n,))],
    )(*grads)


def _pair_sum(name, g, q, c_idx):
    _, R, C = g.shape
    rh = R // 2
    tr = _pick(rh, max(16, (512 * 1024) // C // 16 * 16), 16)
    nh = rh // tr

    def body(c_ref, g_ref, q_ref, o_ref, o2_ref):
        s = (g_ref[...].astype(F32) + q_ref[...].astype(F32)).astype(BF16)
        o_ref[...] = s
        o2_ref[...] = s

    out_spec = pl.BlockSpec((None, tr, C), lambda k, i, c_ref: (k, i, 0))
    return pl.pallas_call(
        body, name=name,
        grid_spec=pltpu.PrefetchScalarGridSpec(
            num_scalar_prefetch=1, grid=(N_CHIPS, nh),
            in_specs=[pl.BlockSpec((None, tr, C), lambda k, i, c_ref: (k, c_ref[0] * nh + i, 0)),
                      pl.BlockSpec((None, tr, C), lambda k, i, c_ref: (k, i, 0))],
            out_specs=[out_spec, out_spec]),
        out_shape=[jax.ShapeDtypeStruct((N_CHIPS, rh, C), BF16)] * 2,
        compiler_params=_params(("parallel", "parallel")),
    )(c_idx, g, q)


def _scatter_chips(parts, slots):
    n = len(parts)

    def body(*refs):
        in_refs, out_refs = refs[:n], refs[2 * n:3 * n]
        send_sems, recv_sems = refs[3 * n:]
        x, y, c, chips = _place()
        k_me = 2 * x + y
        cps = []
        for a in range(n):
            for j, chip in enumerate(chips):
                cp = pltpu.make_async_remote_copy(
                    src_ref=in_refs[a].at[2 * chip[0] + chip[1]], dst_ref=out_refs[a].at[k_me],
                    send_sem=send_sems.at[3 * a + j], recv_sem=recv_sems.at[3 * a + j],
                    device_id=(*chip, c), device_id_type=MESH)
                cp.start()
                cps.append((cp, a, chip, j))
        for cp, a, chip, j in cps:
            got = out_refs[a].at[2 * chip[0] + chip[1]]
            pltpu.make_async_remote_copy(src_ref=got, dst_ref=got, send_sem=send_sems.at[3 * a + j],
                                         recv_sem=recv_sems.at[3 * a + j], device_id=(x, y, c),
                                         device_id_type=MESH).wait_recv()
        for cp, _, _, _ in cps:
            cp.wait_send()

    return pl.pallas_call(
        body, name="scatter_chips",
        out_shape=[jax.ShapeDtypeStruct(p.shape, p.dtype) for p in slots],
        in_specs=[_HBM] * (2 * n), out_specs=[_HBM] * n, input_output_aliases={n + a: a for a in range(n)},
        scratch_shapes=[pltpu.SemaphoreType.DMA((3 * n,)), pltpu.SemaphoreType.DMA((3 * n,))],
    )(*parts, *slots)


def _sum_chips(name, r, c_idx):
    _, rh, C = r.shape
    tr = _pick(rh, max(16, (256 * 1024) // C // 16 * 16), 16)

    def body(c_ref, r_ref, o_ref):
        acc = r_ref[0].astype(F32)
        for k in range(1, N_CHIPS):
            acc = acc + r_ref[k].astype(F32)
        o_ref[...] = acc

    return pl.pallas_call(
        body, name=name,
        grid_spec=pltpu.PrefetchScalarGridSpec(
            num_scalar_prefetch=1, grid=(rh // tr,),
            in_specs=[pl.BlockSpec((N_CHIPS, tr, C), lambda i, c_ref: (0, i, 0))],
            out_specs=pl.BlockSpec((None, tr, C), lambda i, c_ref: (c_ref[0], i, 0))),
        out_shape=jax.ShapeDtypeStruct((2, rh, C), F32), compiler_params=_params(("parallel",)),
    )(c_idx, r)


def _share_halves(bufs):
    n = len(bufs)

    def body(*refs):
        out_refs = refs[n:2 * n]
        send_sems, recv_sems = refs[2 * n:]
        x, y, c, _ = _place()
        cps = []
        for a in range(n):
            cp = pltpu.make_async_remote_copy(
                src_ref=out_refs[a].at[c], dst_ref=out_refs[a].at[c], send_sem=send_sems.at[a],
                recv_sem=recv_sems.at[a], device_id=(x, y, 1 - c), device_id_type=MESH)
            cp.start()
            cps.append(cp)
        for a, cp in enumerate(cps):
            got = out_refs[a].at[1 - c]
            pltpu.make_async_remote_copy(src_ref=got, dst_ref=got, send_sem=send_sems.at[a], recv_sem=recv_sems.at[a],
                                         device_id=(x, y, c), device_id_type=MESH).wait_recv()
        for cp in cps:
            cp.wait_send()

    return pl.pallas_call(
        body, name="share_halves",
        out_shape=[jax.ShapeDtypeStruct(b.shape, b.dtype) for b in bufs],
        in_specs=[_HBM] * n, out_specs=[_HBM] * n, input_output_aliases={a: a for a in range(n)},
        scratch_shapes=[pltpu.SemaphoreType.DMA((n,)), pltpu.SemaphoreType.DMA((n,))],
    )(*bufs)


_BIG = ("w_in", "w_conv_out", "w_hgrn_out", "w_o", "w_ffn_gate", "w_ffn_up", "w_ffn_down")
_ROW_SHARDED = ("w_o", "w_ffn_down")
_WEIGHTS = ("w_ada", "b_ada", "norm_mix_g", "w_in", "conv_w", "lb_param", "gnorm_g", "w_conv_out", "w_hgrn_out",
            "w_o", "norm_ffn_g", "w_ffn_gate", "w_ffn_up", "w_ffn_down", "norm_final_g")


def _pack_rows(pieces):
    flat = jnp.concatenate([p.reshape(-1) for p in pieces])
    pad = (-flat.shape[0]) % 1024
    return jnp.pad(flat, (0, pad)).reshape(8, -1)


def kernel(x, c, w_ada, b_ada, norm_mix_g, w_in, conv_w, lb_param, gnorm_g, w_conv_out, w_hgrn_out, w_o, norm_ffn_g, w_ffn_gate, w_ffn_up, w_ffn_down, norm_final_g, loss_target, m_w_ada, m_b_ada, m_norm_mix_g, m_w_in, m_conv_w, m_lb_param, m_gnorm_g, m_w_conv_out, m_w_hgrn_out, m_w_o, m_norm_ffn_g, m_w_ffn_gate, m_w_ffn_up, m_w_ffn_down, m_norm_final_g, v_w_ada, v_b_ada, v_norm_mix_g, v_w_in, v_conv_w, v_lb_param, v_gnorm_g, v_w_conv_out, v_w_hgrn_out, v_w_o, v_norm_ffn_g, v_w_ffn_gate, v_w_ffn_up, v_w_ffn_down, v_norm_final_g):
    w = dict(w_ada=w_ada, b_ada=b_ada, norm_mix_g=norm_mix_g, w_in=w_in, conv_w=conv_w, lb_param=lb_param,
             gnorm_g=gnorm_g, w_conv_out=w_conv_out, w_hgrn_out=w_hgrn_out, w_o=w_o, norm_ffn_g=norm_ffn_g,
             w_ffn_gate=w_ffn_gate, w_ffn_up=w_ffn_up, w_ffn_down=w_ffn_down, norm_final_g=norm_final_g)
    m = dict(w_ada=m_w_ada, b_ada=m_b_ada, norm_mix_g=m_norm_mix_g, w_in=m_w_in, conv_w=m_conv_w, lb_param=m_lb_param,
             gnorm_g=m_gnorm_g, w_conv_out=m_w_conv_out, w_hgrn_out=m_w_hgrn_out, w_o=m_w_o, norm_ffn_g=m_norm_ffn_g,
             w_ffn_gate=m_w_ffn_gate, w_ffn_up=m_w_ffn_up, w_ffn_down=m_w_ffn_down, norm_final_g=m_norm_final_g)
    v = dict(w_ada=v_w_ada, b_ada=v_b_ada, norm_mix_g=v_norm_mix_g, w_in=v_w_in, conv_w=v_conv_w, lb_param=v_lb_param,
             gnorm_g=v_gnorm_g, w_conv_out=v_w_conv_out, w_hgrn_out=v_w_hgrn_out, w_o=v_w_o, norm_ffn_g=v_norm_ffn_g,
             w_ffn_gate=v_w_ffn_gate, w_ffn_up=v_w_ffn_up, w_ffn_down=v_w_ffn_down, norm_final_g=v_norm_final_g)
    two_d = lambda a: a.reshape((-1, a.shape[-1])) if a.ndim != 2 else a
    shapes = {k: a.shape for k, a in w.items()}
    w, m, v = ({k: two_d(a) for k, a in t.items()} for t in (w, m, v))
    w["lb_param"], m["lb_param"], v["lb_param"] = lb_param, m_lb_param, v_lb_param
    S, D = x.shape[1], x.shape[2]
    d_conv = D // 2
    ix, iy, ic = lax.axis_index("x"), lax.axis_index("y"), lax.axis_index("c")
    k_me = 2 * ix + iy
    dev = 2 * k_me + ic
    cw_shard = w["conv_w"].shape[1]
    ada_cols = w["w_ada"].shape[1]

    got = _allgather_rows("gather_cond", _pack_rows([c, w["conv_w"]])).reshape(N_DEV, -1)
    c_all = got[:, :D]
    conv_full = got[::2, D:D + 3 * cw_shard].reshape(N_CHIPS, 3, cw_shard).transpose(1, 0, 2).reshape(3, -1)
    c_act, c_act_b, lb = _prep(c_all, lb_param)
    b_cols = lax.dynamic_slice(w["b_ada"], (0, k_me * ada_cols), (1, ada_cols))
    mod_cols, = _matmul("ada_fwd", [(c_act_b, 'n', w["w_ada"].astype(BF16), 'n')], [0], N_DEV, ada_cols, D,
                        N_DEV, _pick(ada_cols, 1536, 128), D, [(b_cols, 'row', 0)], [(F32, None)],
                        lambda accs, ex: [accs[0] + ex[0]])
    mod_all = _allgather_rows("gather_mod", mod_cols).reshape(N_CHIPS, 2, N_DEV, ada_cols)[:, 0]
    mod_all = mod_all.transpose(1, 0, 2).reshape(N_DEV, -1)
    mod = lax.dynamic_slice(mod_all, (dev, 0), (1, mod_all.shape[1]))
    k_idx = jnp.reshape(k_me, (1,)).astype(jnp.int32)
    c_idx = jnp.reshape(ic, (1,)).astype(jnp.int32)
    gathered = _gather_weights([_cast_place("cast_" + k, w[k], k_idx) for k in _BIG])
    wg = {k: (g.reshape(-1, g.shape[2]) if k in _ROW_SHARDED else g) for k, g in zip(_BIG, gathered)}

    loss, dx, small, big = _local_step(
        x[0], loss_target[0], mod, w["norm_mix_g"], lb, w["gnorm_g"], w["norm_ffn_g"], w["norm_final_g"], conv_full,
        wg["w_in"], wg["w_conv_out"], wg["w_hgrn_out"], wg["w_o"], wg["w_ffn_gate"], wg["w_ffn_up"], wg["w_ffn_down"])

    pieces = [small["dmod"], small["dg_mix"], small["dg_ffn"], small["dg_final"], small["dlb"], small["dgn"],
              small["dconv_w"], loss]
    sizes = [p.size for p in pieces]
    parts = _allgather_rows("gather_small", _pack_rows(pieces)).reshape(N_DEV, -1)
    total = _small_reduce(parts)
    offs = [0]
    for s in sizes:
        offs.append(offs[-1] + s)
    tot = [total[:, offs[i]:offs[i + 1]] for i in range(len(sizes))]
    dmod_all = parts[:, :sizes[0]]
    grads = {
        "b_ada": tot[0], "norm_mix_g": tot[1], "norm_ffn_g": tot[2], "norm_final_g": tot[3],
        "lb_param": _lb_grad(tot[4], lb), "gnorm_g": tot[5],
        "conv_w": lax.dynamic_slice(tot[6].reshape(3, -1), (0, k_me * cw_shard), (3, cw_shard)),
    }
    loss_out = tot[7][0, 0]

    blocked = [big[k].reshape(N_CHIPS, -1, big[k].shape[-1]) if k in _ROW_SHARDED else big[k] for k in _BIG]
    from_sibling = _swap_halves(blocked)
    pair = [_pair_sum("pair_sum_" + k, g, q, c_idx) for k, g, q in zip(_BIG, blocked, from_sibling)]
    arrived = _scatter_chips([p[0] for p in pair], [p[1] for p in pair])
    halves = [_sum_chips("sum_chips_" + k, r, c_idx) for k, r in zip(_BIG, arrived)]
    whole = _share_halves(halves)
    for k, g in zip(_BIG, whole):
        grads[k] = g.reshape(-1, g.shape[-1])

    delta, new_m, new_v = {}, {}, {}
    dmod_cols = lax.dynamic_slice(dmod_all, (0, k_me * ada_cols), (N_DEV, ada_cols))
    grads["w_ada"], delta["w_ada"], new_m["w_ada"], new_v["w_ada"] = _ada_update(
        c_act.T, dmod_cols, w["w_ada"], m["w_ada"], v["w_ada"])
    for k in _WEIGHTS:
        if k != "w_ada":
            delta[k], new_m[k], new_v[k] = _adamw("adamw_" + k, w[k], grads[k], m[k], v[k])
    outs = [loss_out, dx.reshape(x.shape)]
    for t in (grads, delta, new_m, new_v):
        outs += [t[k].reshape(shapes[k]) for k in _WEIGHTS]
    return tuple(outs)
```

```python
import functools

import jax
import jax.numpy as jnp
from jax import lax
from jax.experimental import pallas as pl
from jax.experimental.pallas import tpu as pltpu

F32 = jnp.float32
BF16 = jnp.bfloat16
MESH = pl.DeviceIdType.MESH

EPS = 1e-6
HEAD = 128
BLK = 16
N_CHIPS = 4
N_DEV = 8
VMEM_LIMIT_BYTES = 56 * 1024 * 1024

ADAM_LR = 0.001
ADAM_B1 = 0.9
ADAM_B2 = 0.999
ADAM_EPS = 1e-08
ADAM_WD = 0.01
ADAM_STEP = 10


def _pick(dim, pref, mult):
    if dim <= pref:
        return dim
    t = (pref // mult) * mult
    while t >= mult:
        if dim % t == 0:
            return t
        t -= mult
    return dim


def _sig(x):
    return 1.0 / (1.0 + jnp.exp(-x))


def _params(sem):
    return pltpu.CompilerParams(dimension_semantics=sem, vmem_limit_bytes=VMEM_LIMIT_BYTES)


def _gj(j, i, k):
    return j


def _gk(j, i, k):
    return k


def _wspec(arr, rt, ct, r_of, c_of):
    if arr.ndim == 2:
        return pl.BlockSpec((rt, ct), lambda j, i, k: (r_of(j, i, k), c_of(j, i, k)))
    per = arr.shape[2] // ct
    assert arr.shape[2] % ct == 0
    return pl.BlockSpec((None, rt, ct),
                        lambda j, i, k: (c_of(j, i, k) // per, r_of(j, i, k), c_of(j, i, k) % per))


_HBM = pl.BlockSpec(memory_space=pltpu.HBM)


def _host(job, grid, in_specs, args, out_specs, out_shape, scratch):
    if job is None:
        return {}, (lambda body: body)
    n_in, n_out, n_scr = len(args), len(out_shape), len(scratch)
    n_job, n_alias = len(job.inputs), job.n_alias
    in_specs += [_HBM] * n_job
    args += job.inputs
    out_specs += [_HBM] * n_alias
    out_shape += [jax.ShapeDtypeStruct(a.shape, a.dtype) for a in job.inputs[n_job - n_alias:]]
    scratch += job.sem_shapes
    aliases = {n_in + n_job - n_alias + t: n_out + t for t in range(n_alias)}

    def wrap(body):
        def hosted(*refs):
            ins, refs = refs[:n_in], refs[n_in:]
            j_in, refs = refs[:n_job], refs[n_job:]
            outs, refs = refs[:n_out], refs[n_out:]
            j_out, refs = refs[:n_alias], refs[n_alias:]
            scr, sems = refs[:n_scr], refs[n_scr:]
            first = functools.reduce(jnp.logical_and, [pl.program_id(d) == 0 for d in range(len(grid))])
            last = functools.reduce(jnp.logical_and, [pl.program_id(d) == grid[d] - 1 for d in range(len(grid))])

            @pl.when(first)
            def _():
                job.start(j_in, j_out, sems)

            body(*ins, *outs, *scr)

            @pl.when(last)
            def _():
                job.finish(j_in, j_out, sems)

        return hosted

    return aliases, wrap


def _matmul(name, pairs, acc_of, M, N, K, tm, tn, tk, extras, outs, epilogue, job=None):
    assert M % tm == 0 and N % tn == 0 and K % tk == 0, (name, M, N, K, tm, tn, tk)
    nj, ni, nk = N // tn, M // tm, K // tk
    n_pairs, n_extra, n_out = len(pairs), len(extras), len(outs)
    n_acc = max(acc_of) + 1
    in_specs, args, dims = [], [], []
    for a, am, b, bm in pairs:
        if am == 'n':
            in_specs.append(pl.BlockSpec((tm, tk), lambda j, i, k: (i, k)))
        else:
            in_specs.append(pl.BlockSpec((tk, tm), lambda j, i, k: (k, i)))
        if bm == 'n':
            in_specs.append(_wspec(b, tk, tn, _gk, _gj))
        else:
            in_specs.append(_wspec(b, tn, tk, _gj, _gk))
        dims.append((((1 if am == 'n' else 0,), (0 if bm == 'n' else 1,)), ((), ())))
        args += [a, b]
    for e, kind, off in extras:
        assert off % tn == 0, (name, off, tn)
        o = off // tn
        if kind == 'tile':
            in_specs.append(pl.BlockSpec((tm, tn), lambda j, i, k, o=o: (i, j + o)))
        else:
            in_specs.append(pl.BlockSpec((1, tn), lambda j, i, k, o=o: (0, j + o)))
        args.append(e)
    out_shape, out_specs = [], []
    for dt, nb in outs:
        if nb is None:
            out_shape.append(jax.ShapeDtypeStruct((M, N), dt))
            out_specs.append(pl.BlockSpec((tm, tn), lambda j, i, k: (i, j)))
        else:
            assert nb % tn == 0 and N % nb == 0
            per = nb // tn
            out_shape.append(jax.ShapeDtypeStruct((N // nb, M, nb), dt))
            out_specs.append(pl.BlockSpec((None, tm, tn), lambda j, i, k, per=per: (j // per, i, j % per)))

    def body(*refs):
        ab = refs[:2 * n_pairs]
        e_refs = refs[2 * n_pairs:2 * n_pairs + n_extra]
        o_refs = refs[2 * n_pairs + n_extra:2 * n_pairs + n_extra + n_out]
        acc_refs = refs[2 * n_pairs + n_extra + n_out:]
        sums = [None] * n_acc
        for p in range(n_pairs):
            prod = lax.dot_general(ab[2 * p][...], ab[2 * p + 1][...], dims[p], preferred_element_type=F32)
            a = acc_of[p]
            sums[a] = prod if sums[a] is None else sums[a] + prod

        def finish(accs):
            res = epilogue(accs, [e[...] for e in e_refs])
            for o_ref, r in zip(o_refs, res):
                o_ref[...] = r.astype(o_ref.dtype)

        if nk == 1:
            finish(sums)
        else:
            k = pl.program_id(2)

            @pl.when(k == 0)
            def _():
                for a in range(n_acc):
                    acc_refs[a][...] = sums[a]

            @pl.when(k > 0)
            def _():
                for a in range(n_acc):
                    acc_refs[a][...] += sums[a]

            @pl.when(k == nk - 1)
            def _():
                finish([acc_refs[a][...] for a in range(n_acc)])

    scratch = [pltpu.VMEM((tm, tn), F32) for _ in range(n_acc)] if nk > 1 else []
    grid = (nj, ni, nk)
    aliases, wrap = _host(job, grid, in_specs, args, out_specs, out_shape, scratch)
    sem = ("parallel", "parallel", "arbitrary") if job is None else ("arbitrary",) * 3
    return pl.pallas_call(
        wrap(body), name=name, grid=grid, in_specs=in_specs, out_specs=out_specs, out_shape=out_shape,
        scratch_shapes=scratch, input_output_aliases=aliases, compiler_params=_params(sem),
    )(*args)


def _row_spec(tr, d):
    return pl.BlockSpec((tr, d), lambda i: (i, 0))


def _vec_spec(d):
    return pl.BlockSpec((1, d), lambda i: (0, 0))


def _norm_mod(name, x, g, sc, sh):
    S, D = x.shape
    tr = _pick(S, 256, 8)

    def body(x_ref, g_ref, sc_ref, sh_ref, h_ref):
        xv = x_ref[...]
        r = lax.rsqrt(jnp.mean(xv * xv, axis=-1, keepdims=True) + EPS)
        h_ref[...] = ((xv * r) * g_ref[...] * (1.0 + sc_ref[...]) + sh_ref[...]).astype(BF16)

    return pl.pallas_call(
        body, name=name, grid=(S // tr,),
        in_specs=[_row_spec(tr, D), _vec_spec(D), _vec_spec(D), _vec_spec(D)],
        out_specs=_row_spec(tr, D), out_shape=jax.ShapeDtypeStruct((S, D), BF16),
        compiler_params=_params(("parallel",)),
    )(x, g, sc, sh)


def _loss_head(x2, target, g, ff, gt):
    S, D = x2.shape
    tr = _pick(S, 256, 8)

    def body(x_ref, t_ref, g_ref, ff_ref, gt_ref, dx_ref, dffb_ref, loss_ref, dgt_ref, dg_ref):
        i = pl.program_id(0)

        @pl.when(i == 0)
        def _():
            loss_ref[...] = jnp.zeros_like(loss_ref)
            dgt_ref[...] = jnp.zeros_like(dgt_ref)
            dg_ref[...] = jnp.zeros_like(dg_ref)

        xv = x_ref[...]
        gv = g_ref[...]
        r = lax.rsqrt(jnp.mean(xv * xv, axis=-1, keepdims=True) + EPS)
        xh = xv * r
        err = xh * gv - t_ref[...]
        loss_ref[...] += 0.5 * jnp.sum(jnp.mean(err * err, axis=-1, keepdims=True))
        dy = err * (1.0 / D)
        dg_ref[...] += jnp.sum(dy * xh, axis=0, keepdims=True)
        dxh = dy * gv
        dx = r * (dxh - xh * jnp.mean(dxh * xh, axis=-1, keepdims=True))
        dx_ref[...] = dx
        dffb_ref[...] = (dx * gt_ref[...]).astype(BF16)
        dgt_ref[...] += jnp.sum(dx * ff_ref[...], axis=0, keepdims=True)

    return pl.pallas_call(
        body, name="loss_head", grid=(S // tr,),
        in_specs=[_row_spec(tr, D), _row_spec(tr, D), _vec_spec(D), _row_spec(tr, D), _vec_spec(D)],
        out_specs=[_row_spec(tr, D), _row_spec(tr, D), pl.BlockSpec((1, 128), lambda i: (0, 0)),
                   _vec_spec(D), _vec_spec(D)],
        out_shape=[jax.ShapeDtypeStruct((S, D), F32), jax.ShapeDtypeStruct((S, D), BF16),
                   jax.ShapeDtypeStruct((1, 128), F32), jax.ShapeDtypeStruct((1, D), F32),
                   jax.ShapeDtypeStruct((1, D), F32)],
        compiler_params=_params(("arbitrary",)),
    )(x2, target, g, ff, gt)


def _norm_mod_bwd(name, dh, x, g, sc, dres, gated=None):
    S, D = x.shape
    tr = _pick(S, 256, 8)
    n = S // tr
    with_gate = gated is not None

    def body(*refs):
        if with_gate:
            dh_ref, x_ref, g_ref, sc_ref, dres_ref, mo_ref, gt_ref, dx_ref, dsh_ref, dsc_ref, dg_ref, dmob_ref, dgt_ref = refs
        else:
            dh_ref, x_ref, g_ref, sc_ref, dres_ref, dx_ref, dsh_ref, dsc_ref, dg_ref = refs
        i = pl.program_id(0)

        @pl.when(i == 0)
        def _():
            dsh_ref[...] = jnp.zeros_like(dsh_ref)
            dsc_ref[...] = jnp.zeros_like(dsc_ref)
            if with_gate:
                dgt_ref[...] = jnp.zeros_like(dgt_ref)

        xv = x_ref[...]
        dhv = dh_ref[...]
        gv = g_ref[...]
        scale = 1.0 + sc_ref[...]
        r = lax.rsqrt(jnp.mean(xv * xv, axis=-1, keepdims=True) + EPS)
        xh = xv * r
        dsh_ref[...] += jnp.sum(dhv, axis=0, keepdims=True)
        dsc_ref[...] += jnp.sum(dhv * xh, axis=0, keepdims=True)
        dxh = dhv * (gv * scale)
        dx = dres_ref[...] + r * (dxh - xh * jnp.mean(dxh * xh, axis=-1, keepdims=True))
        dx_ref[...] = dx
        if with_gate:
            dmob_ref[...] = (dx * gt_ref[...]).astype(BF16)
            dgt_ref[...] += jnp.sum(dx * mo_ref[...], axis=0, keepdims=True)

        @pl.when(i == n - 1)
        def _():
            t = dsc_ref[...]
            dg_ref[...] = t * scale
            dsc_ref[...] = t * gv

    in_specs = [_row_spec(tr, D), _row_spec(tr, D), _vec_spec(D), _vec_spec(D), _row_spec(tr, D)]
    args = [dh, x, g, sc, dres]
    out_specs = [_row_spec(tr, D), _vec_spec(D), _vec_spec(D), _vec_spec(D)]
    out_shape = [jax.ShapeDtypeStruct((S, D), F32)] + [jax.ShapeDtypeStruct((1, D), F32)] * 3
    if with_gate:
        in_specs += [_row_spec(tr, D), _vec_spec(D)]
        args += list(gated)
        out_specs += [_row_spec(tr, D), _vec_spec(D)]
        out_shape += [jax.ShapeDtypeStruct((S, D), BF16), jax.ShapeDtypeStruct((1, D), F32)]
    return pl.pallas_call(
        body, name=name, grid=(n,), in_specs=in_specs, out_specs=out_specs, out_shape=out_shape,
        compiler_params=_params(("arbitrary",)),
    )(*args)


CONV_ROWS = 256


def _col_spec(S, off_cols):
    o = off_cols // HEAD
    return pl.BlockSpec((S, HEAD), lambda c, o=o: (0, c + o))


def _conv_taps(u, u_prev, rid):
    s1 = jnp.where(rid >= 1, pltpu.roll(u, 1, 0), pltpu.roll(u_prev, 1, 0))
    s2 = jnp.where(rid >= 2, pltpu.roll(u, 2, 0), pltpu.roll(u_prev, 2, 0))
    return s1, s2


def _conv_fwd(proj, conv_w, d_conv):
    S = proj.shape[0]
    R = min(CONV_ROWS, S)
    nr = S // R

    def body(ab_ref, ac_ref, ax_ref, w_ref, ya_ref):
        w0, w1, w2 = w_ref[0:1, :], w_ref[1:2, :], w_ref[2:3, :]
        rid = lax.broadcasted_iota(jnp.int32, (R, HEAD), 0)

        def step(c, carry):
            rows = pl.ds(pl.multiple_of(c * R, R), R)
            prev = pl.ds(pl.multiple_of(jnp.maximum(c - 1, 0) * R, R), R)
            u = ac_ref[rows, :] * ax_ref[rows, :]
            u_prev = jnp.where(c > 0, ac_ref[prev, :] * ax_ref[prev, :], 0.0)
            s1, s2 = _conv_taps(u, u_prev, rid)
            y = w0 * s2 + w1 * s1 + w2 * u
            ya_ref[rows, :] = (ab_ref[rows, :] * y).astype(BF16)
            return carry

        lax.fori_loop(0, nr, step, 0)

    return pl.pallas_call(
        body, name="conv_fwd", grid=(d_conv // HEAD,),
        in_specs=[_col_spec(S, 0), _col_spec(S, d_conv), _col_spec(S, 2 * d_conv),
                  pl.BlockSpec((3, HEAD), lambda c: (0, c))],
        out_specs=pl.BlockSpec((S, HEAD), lambda c: (0, c)),
        out_shape=jax.ShapeDtypeStruct((S, d_conv), BF16),
        compiler_params=_params(("parallel",)),
    )(proj, proj, proj, conv_w)


def _conv_bwd(dya, proj, conv_w, d_conv):
    S = proj.shape[0]
    R = min(CONV_ROWS, S)
    nr = S // R

    def body(dya_ref, ab_ref, ac_ref, ax_ref, w_ref, dab_ref, dac_ref, dax_ref, dw_ref):
        w0, w1, w2 = w_ref[0:1, :], w_ref[1:2, :], w_ref[2:3, :]
        rid = lax.broadcasted_iota(jnp.int32, (R, HEAD), 0)

        def step(c, carry):
            dw0, dw1, dw2 = carry
            rows = pl.ds(pl.multiple_of(c * R, R), R)
            prev = pl.ds(pl.multiple_of(jnp.maximum(c - 1, 0) * R, R), R)
            nxt = pl.ds(pl.multiple_of(jnp.minimum(c + 1, nr - 1) * R, R), R)
            a_c, a_x, a_b = ac_ref[rows, :], ax_ref[rows, :], ab_ref[rows, :]
            u = a_c * a_x
            u_prev = jnp.where(c > 0, ac_ref[prev, :] * ax_ref[prev, :], 0.0)
            s1, s2 = _conv_taps(u, u_prev, rid)
            y = w0 * s2 + w1 * s1 + w2 * u
            dy = dya_ref[rows, :]
            dab_ref[rows, :] = (dy * y).astype(BF16)
            dyc = dy * a_b
            dyc_next = jnp.where(c < nr - 1, dya_ref[nxt, :] * ab_ref[nxt, :], 0.0)
            t1 = jnp.where(rid < R - 1, pltpu.roll(dyc, R - 1, 0), pltpu.roll(dyc_next, R - 1, 0))
            t2 = jnp.where(rid < R - 2, pltpu.roll(dyc, R - 2, 0), pltpu.roll(dyc_next, R - 2, 0))
            du = w2 * dyc + w1 * t1 + w0 * t2
            dac_ref[rows, :] = (du * a_x).astype(BF16)
            dax_ref[rows, :] = (du * a_c).astype(BF16)
            dw0 = dw0 + jnp.sum(dyc * s2, axis=0, keepdims=True)
            dw1 = dw1 + jnp.sum(dyc * s1, axis=0, keepdims=True)
            dw2 = dw2 + jnp.sum(dyc * u, axis=0, keepdims=True)
            return dw0, dw1, dw2

        z = jnp.zeros((1, HEAD), F32)
        dw0, dw1, dw2 = lax.fori_loop(0, nr, step, (z, z, z))
        dw_ref[0:1, :] = dw0
        dw_ref[1:2, :] = dw1
        dw_ref[2:3, :] = dw2

    col = pl.BlockSpec((S, HEAD), lambda c: (0, c))
    return pl.pallas_call(
        body, name="conv_bwd", grid=(d_conv // HEAD,),
        in_specs=[col, _col_spec(S, 0), _col_spec(S, d_conv), _col_spec(S, 2 * d_conv),
                  pl.BlockSpec((3, HEAD), lambda c: (0, c))],
        out_specs=[col, col, col, pl.BlockSpec((3, HEAD), lambda c: (0, c))],
        out_shape=[jax.ShapeDtypeStruct((S, d_conv), BF16)] * 3 + [jax.ShapeDtypeStruct((3, d_conv), F32)],
        compiler_params=_params(("parallel",)),
    )(dya, proj, proj, proj, conv_w)


HGRN_ROWS = 256


def _block_cumsum(x, pos):
    for s in (1, 2, 4, 8):
        x = x + jnp.where(pos >= s, pltpu.roll(x, s, 0), 0.0)
    return x


def _block_rev_cumsum(x, pos, rows):
    for s in (1, 2, 4, 8):
        x = x + jnp.where(pos < BLK - s, pltpu.roll(x, rows - s, 0), 0.0)
    return x


def _block_last(x, pos, rows):
    m = jnp.where(pos == BLK - 1, x, 0.0)
    for s in (1, 2, 4, 8):
        m = m + pltpu.roll(m, rows - s, 0)
    return m


def _hgrn_gates(q, z, lb):
    sgq = _sig(q)
    qs = q * sgq
    sg = _sig(z)
    f = lb + (1.0 - lb) * sg
    kk = (1.0 - lb) * (1.0 - sg)
    return sgq, qs, sg, f, kk


def _hgrn_specs(T, d_conv, n_t, rev):
    def t_of(i):
        return (n_t - 1 - i) if rev else i

    def pcol(off):
        o = off // HEAD
        return pl.BlockSpec((T, HEAD), lambda h, i, o=o: (t_of(i), h + o))

    q_spec, z_spec, v_spec, g_spec = pcol(3 * d_conv), pcol(4 * d_conv), pcol(5 * d_conv), pcol(6 * d_conv)
    lb_spec = pl.BlockSpec((1, HEAD), lambda h, i: (0, h))
    gn_spec = pl.BlockSpec((1, HEAD), lambda h, i: (0, 0))
    act_spec = pl.BlockSpec((T, HEAD), lambda h, i: (t_of(i), h))
    st_spec = pl.BlockSpec((None, T // BLK, HEAD, HEAD), lambda h, i: (h, t_of(i), 0, 0))
    return q_spec, z_spec, v_spec, g_spec, lb_spec, gn_spec, act_spec, st_spec


def _hgrn_fwd(proj, lb, gn, d_conv, job=None):
    S = proj.shape[0]
    H = d_conv // HEAD
    T = min(HGRN_ROWS, S)
    n_t, nb = S // T, T // BLK
    q_spec, z_spec, v_spec, g_spec, lb_spec, gn_spec, act_spec, st_spec = _hgrn_specs(T, d_conv, n_t, False)

    def body(q_ref, z_ref, v_ref, g_ref, lb_ref, gn_ref, ob_ref, o_ref, st_ref, state, qe_s, ke_s, v_s, dec_s, o_s):
        @pl.when(pl.program_id(1) == 0)
        def _():
            state[...] = jnp.zeros_like(state)

        pos = lax.broadcasted_iota(jnp.int32, (T, HEAD), 0) % BLK
        v = v_ref[...]
        _, qs, _, f, kk = _hgrn_gates(q_ref[...], z_ref[...], lb_ref[...])
        b = _block_cumsum(jnp.log(f), pos)
        b_tot = _block_last(b, pos, T)
        qe_s[...] = (qs * jnp.exp(b)).astype(BF16)
        ke_s[...] = (kk * jnp.exp(b_tot - b)).astype(BF16)
        v_s[...] = v.astype(BF16)
        dec_s[...] = jnp.exp(b_tot)
        o = jnp.sum(qs * kk, axis=-1, keepdims=True) * v
        for d in range(1, BLK):
            e = jnp.exp(b - pltpu.roll(b, d, 0))
            a = jnp.sum(qs * pltpu.roll(kk, d, 0) * e, axis=-1, keepdims=True)
            o = o + jnp.where(pos >= d, a * pltpu.roll(v, d, 0), 0.0)
        o_s[...] = o

        def step(j, st):
            rows = pl.ds(pl.multiple_of(j * BLK, BLK), BLK)
            stb = st.astype(BF16)
            st_ref[j] = stb
            o_s[rows, :] += lax.dot_general(qe_s[rows, :], stb, (((1,), (1,)), ((), ())),
                                            preferred_element_type=F32)
            upd = lax.dot_general(v_s[rows, :], ke_s[rows, :], (((0,), (0,)), ((), ())),
                                  preferred_element_type=F32)
            return st * dec_s[pl.ds(pl.multiple_of(j * BLK, BLK), 1), :] + upd

        state[...] = lax.fori_loop(0, nb, step, state[...])
        o = o_s[...]
        o_ref[...] = o
        r = lax.rsqrt(jnp.mean(o * o, axis=-1, keepdims=True) + EPS)
        g = g_ref[...]
        ob_ref[...] = ((o * r) * gn_ref[...] * (g * _sig(g))).astype(BF16)

    grid = (H, n_t)
    in_specs = [q_spec, z_spec, v_spec, g_spec, lb_spec, gn_spec]
    args = [proj, proj, proj, proj, lb, gn]
    out_specs = [act_spec, act_spec, st_spec]
    out_shape = [jax.ShapeDtypeStruct((S, d_conv), BF16), jax.ShapeDtypeStruct((S, d_conv), F32),
                 jax.ShapeDtypeStruct((H, S // BLK, HEAD, HEAD), BF16)]
    scratch = [pltpu.VMEM((HEAD, HEAD), F32), pltpu.VMEM((T, HEAD), BF16), pltpu.VMEM((T, HEAD), BF16),
               pltpu.VMEM((T, HEAD), BF16), pltpu.VMEM((T, HEAD), F32), pltpu.VMEM((T, HEAD), F32)]
    aliases, wrap = _host(job, grid, in_specs, args, out_specs, out_shape, scratch)
    return pl.pallas_call(
        wrap(body), name="hgrn_fwd", grid=grid, in_specs=in_specs, out_specs=out_specs, out_shape=out_shape,
        scratch_shapes=scratch, input_output_aliases=aliases,
        compiler_params=_params(("parallel" if job is None else "arbitrary", "arbitrary")),
    )(*args)


def _hgrn_bwd(dob, o_raw, states, proj, lb, gn, d_conv, job=None):
    S = proj.shape[0]
    H = d_conv // HEAD
    T = min(HGRN_ROWS, S)
    n_t, nb = S // T, T // BLK
    q_spec, z_spec, v_spec, g_spec, lb_spec, gn_spec, act_spec, st_spec = _hgrn_specs(T, d_conv, n_t, True)

    def body(dob_ref, o_ref, st_ref, q_ref, z_ref, v_ref, g_ref, lb_ref, gn_ref,
             dq_ref, dz_ref, dv_ref, dg_ref, dlb_ref, dgn_ref,
             dstate, do_b, qe_b, ke_b, v_b, dec_s, dqe_s, dke_s, dvi_s, ddec_s):
        @pl.when(pl.program_id(1) == 0)
        def _():
            dstate[...] = jnp.zeros_like(dstate)
            dlb_ref[...] = jnp.zeros_like(dlb_ref)
            dgn_ref[...] = jnp.zeros_like(dgn_ref)

        pos = lax.broadcasted_iota(jnp.int32, (T, HEAD), 0) % BLK
        lbv, gnv = lb_ref[...], gn_ref[...]
        q, v, g = q_ref[...], v_ref[...], g_ref[...]
        sgq, qs, sg, f, kk = _hgrn_gates(q, z_ref[...], lbv)
        b = _block_cumsum(jnp.log(f), pos)
        b_tot = _block_last(b, pos, T)
        eb = jnp.exp(b)
        ek = jnp.exp(b_tot - b)
        dec = jnp.exp(b_tot)
        qe, ke = qs * eb, kk * ek
        o = o_ref[...]
        r = lax.rsqrt(jnp.mean(o * o, axis=-1, keepdims=True) + EPS)
        on = o * r
        sgg = _sig(g)
        dobv = dob_ref[...]
        dog = dobv * (g * sgg)
        dgn_ref[...] += jnp.sum(dog * on, axis=0, keepdims=True)
        don = dog * gnv
        do = r * (don - on * jnp.mean(don * on, axis=-1, keepdims=True))
        dg_ref[...] = (dobv * (on * gnv) * (sgg * (1.0 + g * (1.0 - sgg)))).astype(BF16)

        do_b[...] = do.astype(BF16)
        qe_b[...] = qe.astype(BF16)
        ke_b[...] = ke.astype(BF16)
        v_b[...] = v.astype(BF16)
        dec_s[...] = dec

        def step(jj, dst):
            j = nb - 1 - jj
            rows = pl.ds(pl.multiple_of(j * BLK, BLK), BLK)
            stj = st_ref[j]
            dstb = dst.astype(BF16)
            dob_j = do_b[rows, :]
            dqe_s[rows, :] = lax.dot_general(dob_j, stj, (((1,), (0,)), ((), ())), preferred_element_type=F32)
            dke_s[rows, :] = lax.dot_general(v_b[rows, :], dstb, (((1,), (0,)), ((), ())), preferred_element_type=F32)
            dvi_s[rows, :] = lax.dot_general(ke_b[rows, :], dstb, (((1,), (1,)), ((), ())), preferred_element_type=F32)
            ddec = jnp.sum(stj.astype(F32) * dst, axis=0, keepdims=True)
            ddec_s[rows, :] = jnp.broadcast_to(ddec, (BLK, HEAD))
            upd = lax.dot_general(dob_j, qe_b[rows, :], (((0,), (0,)), ((), ())), preferred_element_type=F32)
            return dst * dec_s[pl.ds(pl.multiple_of(j * BLK, BLK), 1), :] + upd

        dstate[...] = lax.fori_loop(0, nb, step, dstate[...])

        dqe, dke = dqe_s[...], dke_s[...]
        dqs = dqe * eb
        dkk = dke * ek
        tke = dke * ke
        db = dqe * qe - tke
        db = db + jnp.where(pos == BLK - 1, _block_cumsum(tke, pos) + ddec_s[...] * dec, 0.0)
        dv = dvi_s[...]
        da = jnp.sum(do * v, axis=-1, keepdims=True)
        dqs = dqs + da * kk
        dkk = dkk + da * qs
        dv = dv + jnp.sum(qs * kk, axis=-1, keepdims=True) * do
        for d in range(1, BLK):
            kd = pltpu.roll(kk, d, 0)
            e = jnp.where(pos >= d, jnp.exp(b - pltpu.roll(b, d, 0)), 0.0)
            a = jnp.sum(qs * kd * e, axis=-1, keepdims=True)
            da = jnp.sum(do * pltpu.roll(v, d, 0), axis=-1, keepdims=True)
            gq = da * e
            dqs = dqs + gq * kd
            pk = gq * qs
            dkk = dkk + pltpu.roll(pk, T - d, 0)
            pb = pk * kd
            db = db + pb - pltpu.roll(pb, T - d, 0)
            dv = dv + pltpu.roll(jnp.where(pos >= d, a * do, 0.0), T - d, 0)
        dlf = _block_rev_cumsum(db, pos, T)
        df = dlf / f - dkk
        dlb_ref[...] += jnp.sum(df * (1.0 - sg), axis=0, keepdims=True)
        dz_ref[...] = (df * (1.0 - lbv) * sg * (1.0 - sg)).astype(BF16)
        dq_ref[...] = (dqs * (sgq * (1.0 + q * (1.0 - sgq)))).astype(BF16)
        dv_ref[...] = dv.astype(BF16)

    tb = lambda dt: pltpu.VMEM((T, HEAD), dt)
    grid = (H, n_t)
    in_specs = [act_spec, act_spec, st_spec, q_spec, z_spec, v_spec, g_spec, lb_spec, gn_spec]
    args = [dob, o_raw, states, proj, proj, proj, proj, lb, gn]
    out_specs = [act_spec, act_spec, act_spec, act_spec, lb_spec, pl.BlockSpec((None, 1, HEAD), lambda h, i: (h, 0, 0))]
    out_shape = ([jax.ShapeDtypeStruct((S, d_conv), BF16)] * 4
                 + [jax.ShapeDtypeStruct((1, d_conv), F32), jax.ShapeDtypeStruct((H, 1, HEAD), F32)])
    scratch = [pltpu.VMEM((HEAD, HEAD), F32), tb(BF16), tb(BF16), tb(BF16), tb(BF16), tb(F32),
               tb(F32), tb(F32), tb(F32), tb(F32)]
    aliases, wrap = _host(job, grid, in_specs, args, out_specs, out_shape, scratch)
    return pl.pallas_call(
        wrap(body), name="hgrn_bwd", grid=grid, in_specs=in_specs, out_specs=out_specs, out_shape=out_shape,
        scratch_shapes=scratch, input_output_aliases=aliases,
        compiler_params=_params(("parallel" if job is None else "arbitrary", "arbitrary")),
    )(*args)


def _first(accs, extras):
    return [accs[0]]


def _local_step(x, target, mod, norm_mix_g, lb, gnorm_g, norm_ffn_g, norm_final_g, conv_w, bufs, c_idx):
    S, D = x.shape
    d_conv = D // 2
    d_in = 7 * d_conv + 2 * D
    d_ff = N_CHIPS * bufs["w_ffn_down"].shape[1]
    nb_in, nb_co, nb_ff = bufs["w_in"].shape[2], bufs["w_conv_out"].shape[2], bufs["w_ffn_gate"].shape[2]
    rows = lambda g: g.reshape(-1, g.shape[2])
    w_in, = _gather_weights([bufs["w_in"]])
    sh_m, sc_m, gt_m, sh_f, sc_f, gt_f = [mod[:, i * D:(i + 1) * D] for i in range(6)]
    tm = _pick(S, 512, 16)
    ts = _pick(S, 1024, 128)
    tn_in = _pick(nb_in, 1408, 128)
    tn_co = _pick(nb_co, 512, 128)
    tn_ff = _pick(nb_ff, 1408, 128)
    tn_d = _pick(D, 512, 128)
    tw = _pick(D, 1024, 128)
    ts2 = _pick(S, 2048, 128)

    h = _norm_mod("norm_mix", x, norm_mix_g, sc_m, sh_m)
    proj, *got = _matmul("proj", [(h, 'n', w_in, 'n')], [0], S, d_in, D, tm, tn_in, D, [], [(F32, None)], _first,
                         job=_GatherJob([bufs[k] for k in ("w_conv_out", "w_hgrn_out", "w_o")]))
    w_conv_out, w_hgrn_out, w_o = _forward_halves("forward_branch", got)
    w_o = rows(w_o)
    ob, o_raw, states, *got = _hgrn_fwd(proj, lb, gnorm_g, d_conv,
                                        job=_GatherJob([bufs[k] for k in ("w_ffn_gate", "w_ffn_up", "w_ffn_down")]))
    ya = _conv_fwd(proj, conv_w, d_conv)
    w_ffn_gate, w_ffn_up, w_ffn_down = _forward_halves("forward_ffn", got)
    w_ffn_down = rows(w_ffn_down)

    def merge_epi(accs, ex):
        y_a, y_b = accs
        return [y_a, y_b, _sig(ex[0]) * y_a + _sig(ex[1]) * y_b]

    y_a, y_b, merged = _matmul(
        "branch_out", [(ya, 'n', w_conv_out, 'n'), (ob, 'n', w_hgrn_out, 'n')], [0, 1], S, D, d_conv, tm, tn_co, d_conv,
        [(proj, 'tile', 7 * d_conv), (proj, 'tile', 7 * d_conv + D)], [(F32, None), (F32, None), (BF16, None)], merge_epi)

    def resid_epi(accs, ex):
        return [accs[0], ex[0] + ex[1] * accs[0]]

    mo, x1 = _matmul("w_o", [(merged, 'n', w_o, 'n')], [0], S, D, D, tm, tw, D,
                     [(x, 'tile', 0), (gt_m, 'row', 0)], [(F32, None), (F32, None)], resid_epi)
    h2 = _norm_mod("norm_ffn", x1, norm_ffn_g, sc_f, sh_f)

    def swiglu_epi(accs, ex):
        gg, uu = accs
        return [gg, uu, gg * _sig(gg) * uu]

    G, U, act = _matmul("ffn_in", [(h2, 'n', w_ffn_gate, 'n'), (h2, 'n', w_ffn_up, 'n')], [0, 1], S, d_ff, D,
                        tm, tn_ff, D, [], [(F32, None), (F32, None), (BF16, None)], swiglu_epi)
    ff, x2 = _matmul("ffn_out", [(act, 'n', w_ffn_down, 'n')], [0], S, D, d_ff, tm, D, _pick(d_ff, 1408, 128),
                     [(x1, 'tile', 0), (gt_f, 'row', 0)], [(F32, None), (F32, None)], resid_epi)

    dx2, dffb, loss, dgt_f, dg_final = _loss_head(x2, target, norm_final_g, ff, gt_f)

    def swiglu_bwd_epi(accs, ex):
        dact, gg, uu = accs[0], ex[0], ex[1]
        s = _sig(gg)
        return [dact * uu * (s * (1.0 + gg * (1.0 - s))), dact * (gg * s)]

    dG, dU = _matmul("d_act", [(dffb, 'n', w_ffn_down, 't')], [0], S, d_ff, D, tm, tn_ff, D,
                     [(G, 'tile', 0), (U, 'tile', 0)], [(BF16, None), (BF16, None)], swiglu_bwd_epi)
    dw_down, = _matmul("dw_ffn_down", [(act, 't', dffb, 'n')], [0], d_ff, D, S, _pick(d_ff, 1408, 128),
                       D, ts, [], [(BF16, None)], _first)
    dh2, = _matmul("d_h2", [(dG, 'n', w_ffn_gate, 't'), (dU, 'n', w_ffn_up, 't')], [0, 0], S, D, d_ff,
                   tm, D, tn_ff, [], [(F32, None)], _first)
    dw_gate, dw_up = _matmul("dw_ffn_in", [(h2, 't', dG, 'n'), (h2, 't', dU, 'n')], [0, 1], D, d_ff, S,
                             tw, tn_ff, ts, [], [(BF16, nb_ff), (BF16, nb_ff)],
                             lambda accs, ex: accs)

    def pre_sum(tag, names, grads):
        from_sibling = _swap_halves("swap_" + tag, grads)
        pairs = [_pair_sum("pair_sum_" + k, g, q, c_idx) for k, g, q in zip(names, grads, from_sibling)]
        return [p[0] for p in pairs], [p[1] for p in pairs]

    ffn_names = ["w_ffn_down", "w_ffn_gate", "w_ffn_up"]
    parts_f, slots_f = pre_sum("ffn", ffn_names, [dw_down.reshape(N_CHIPS, -1, D), dw_gate, dw_up])
    dx1, dsh_f, dsc_f, dg_ffn, dmob, dgt_m = _norm_mod_bwd("norm_ffn_bwd", dh2, x1, norm_ffn_g, sc_f, dx2, (mo, gt_m))

    def merge_bwd_epi(accs, ex):
        dm, ga, gb, ya_, yb_ = accs[0], ex[0], ex[1], ex[2], ex[3]
        sa, sb = _sig(ga), _sig(gb)
        return [dm * ya_ * sa * (1.0 - sa), dm * yb_ * sb * (1.0 - sb), dm * sa, dm * sb]

    dga, dgb, dy_a, dy_b = _matmul(
        "d_merged", [(dmob, 'n', w_o, 't')], [0], S, D, D, tm, tn_co, D,
        [(proj, 'tile', 7 * d_conv), (proj, 'tile', 7 * d_conv + D), (y_a, 'tile', 0), (y_b, 'tile', 0)],
        [(BF16, None)] * 4, merge_bwd_epi)
    dw_o, = _matmul("dw_o", [(merged, 't', dmob, 'n')], [0], D, D, S, _pick(D, 512, 128), _pick(D, 1024, 128), ts,
                    [], [(BF16, None)], _first)
    tn_c = _pick(d_conv, 512, 128)
    dya, = _matmul("d_ya", [(dy_a, 'n', w_conv_out, 't')], [0], S, d_conv, D, tm, tn_c, tn_co, [], [(F32, None)], _first)
    dob, = _matmul("d_ob", [(dy_b, 'n', w_hgrn_out, 't')], [0], S, d_conv, D, tm, tn_c, tn_co, [], [(F32, None)], _first)
    dw_co, = _matmul("dw_conv_out", [(ya, 't', dy_a, 'n')], [0], d_conv, D, S, _pick(d_conv, 512, 128), tn_co, ts,
                     [], [(BF16, nb_co)], _first)
    dw_ho, = _matmul("dw_hgrn_out", [(ob, 't', dy_b, 'n')], [0], d_conv, D, S, _pick(d_conv, 512, 128), tn_co, ts,
                     [], [(BF16, nb_co)], _first)
    branch_names = ["w_conv_out", "w_hgrn_out", "w_o"]
    parts_b, slots_b = pre_sum("branch", branch_names, [dw_co, dw_ho, dw_o.reshape(N_CHIPS, -1, D)])
    dab, dac, dax, dconv_w = _conv_bwd(dya, proj, conv_w, d_conv)
    dq, dz, dv, dgo, dlb, dgn, *arrived = _hgrn_bwd(dob, o_raw, states, proj, lb, gnorm_g, d_conv,
                                                    job=_ScatterJob(parts_f + parts_b, slots_f + slots_b))
    dproj = jnp.concatenate([dab, dac, dax, dq, dz, dv, dgo, dga, dgb], axis=1)
    dw_in, = _matmul("dw_in", [(h, 't', dproj, 'n')], [0], D, d_in, S, tw, tn_in, ts2,
                     [], [(BF16, nb_in)], _first)
    parts_i, slots_i = pre_sum("in", ["w_in"], [dw_in])
    dh, arrived_in = _matmul("d_h", [(dproj, 'n', w_in, 't')], [0], S, D, d_in, tm, D, tn_in, [], [(F32, None)], _first,
                             job=_ScatterJob(parts_i, slots_i))
    dx, dsh_m, dsc_m, dg_mix = _norm_mod_bwd("norm_mix_bwd", dh, x, norm_mix_g, sc_m, dx1)
    dmod = jnp.concatenate([dsh_m, dsc_m, dgt_m, dsh_f, dsc_f, dgt_f], axis=1)
    small = dict(dmod=dmod, dg_mix=dg_mix, dg_ffn=dg_ffn, dg_final=dg_final, dlb=dlb,
                 dgn=jnp.sum(dgn, axis=0), dconv_w=dconv_w)
    big = dict(zip(ffn_names + branch_names, arrived), w_in=arrived_in)
    return loss, dx, small, big


def _adamw_math(w, g, m, v):
    m = ADAM_B1 * m + (1.0 - ADAM_B1) * g
    v = ADAM_B2 * v + (1.0 - ADAM_B2) * (g * g)
    m_hat = m / (1.0 - ADAM_B1 ** ADAM_STEP)
    v_hat = v / (1.0 - ADAM_B2 ** ADAM_STEP)
    delta = -ADAM_LR * (m_hat / (jnp.sqrt(v_hat) + ADAM_EPS) + ADAM_WD * w)
    return delta, m, v


def _adamw(name, w, g, m, v):
    R, C = w.shape
    tr = _pick(R, max(8, (256 * 1024) // C // 8 * 8), 8)
    spec = pl.BlockSpec((tr, C), lambda i: (i, 0))

    def body(w_ref, g_ref, m_ref, v_ref, d_ref, mo_ref, vo_ref):
        d, m2, v2 = _adamw_math(w_ref[...], g_ref[...], m_ref[...], v_ref[...])
        d_ref[...] = d
        mo_ref[...] = m2
        vo_ref[...] = v2

    return pl.pallas_call(
        body, name=name, grid=(R // tr,), in_specs=[spec] * 4, out_specs=[spec] * 3,
        out_shape=[jax.ShapeDtypeStruct((R, C), F32)] * 3, compiler_params=_params(("parallel",)),
    )(w, g, m, v)


def _ada_update(c_act_t, dmod, w, m, v):
    R, C = w.shape
    B = dmod.shape[0]
    tr = _pick(R, 256, 8)
    tc = _pick(C, 1536, 128)
    spec = pl.BlockSpec((tr, tc), lambda i, j: (i, j))

    def body(ct_ref, dm_ref, w_ref, m_ref, v_ref, g_ref, d_ref, mo_ref, vo_ref):
        ct = ct_ref[...]
        dm = dm_ref[...]
        g = ct[:, 0:1] * dm[0:1, :]
        for b in range(1, B):
            g = g + ct[:, b:b + 1] * dm[b:b + 1, :]
        d, m2, v2 = _adamw_math(w_ref[...], g, m_ref[...], v_ref[...])
        g_ref[...] = g
        d_ref[...] = d
        mo_ref[...] = m2
        vo_ref[...] = v2

    return pl.pallas_call(
        body, name="ada_update", grid=(R // tr, C // tc),
        in_specs=[pl.BlockSpec((tr, B), lambda i, j: (i, 0)), pl.BlockSpec((B, tc), lambda i, j: (0, j)),
                  spec, spec, spec],
        out_specs=[spec] * 4, out_shape=[jax.ShapeDtypeStruct((R, C), F32)] * 4,
        compiler_params=_params(("parallel", "parallel")),
    )(c_act_t, dmod, w, m, v)


def _prep(c_all, lb_param):
    def body(c_ref, p_ref, ca_ref, cab_ref, lb_ref):
        cv = c_ref[...]
        ca = cv * _sig(cv)
        ca_ref[...] = ca
        cab_ref[...] = ca.astype(BF16)
        lb_ref[...] = _sig(p_ref[0:1, :] - p_ref[1:2, :])

    return pl.pallas_call(
        body, name="prep",
        out_shape=[jax.ShapeDtypeStruct(c_all.shape, F32), jax.ShapeDtypeStruct(c_all.shape, BF16),
                   jax.ShapeDtypeStruct((1, lb_param.shape[1]), F32)],
    )(c_all, lb_param)


def _small_reduce(parts):
    W = parts.shape[1]

    def body(p_ref, o_ref):
        acc = p_ref[0:1, :]
        for d in range(1, N_DEV):
            acc = acc + p_ref[d:d + 1, :]
        o_ref[...] = acc

    return pl.pallas_call(body, name="small_reduce", out_shape=jax.ShapeDtypeStruct((1, W), F32))(parts)


def _lb_grad(dlb, lb):
    def body(d_ref, lb_ref, o_ref):
        t = d_ref[...] * lb_ref[...] * (1.0 - lb_ref[...])
        o_ref[0:1, :] = t
        o_ref[1:2, :] = -t

    return pl.pallas_call(body, name="lb_grad", out_shape=jax.ShapeDtypeStruct((2, dlb.shape[1]), F32))(dlb, lb)


def _place():
    x, y, c = lax.axis_index("x"), lax.axis_index("y"), lax.axis_index("c")
    chips = [(1 - x, y), (x, 1 - y), (1 - x, 1 - y)]
    return x, y, c, chips


def _allgather_rows(name, v):
    m_per, n = v.shape

    def body(x_ref, out_ref, send_sems, recv_sems, local_sem):
        x, y, c, chips = _place()
        me, sibling = (x, y, c), (x, y, 1 - c)

        def rows(px, py, pc):
            return out_ref.at[pl.ds((4 * px + 2 * py + pc) * m_per, m_per), :]

        def copy(k, block, to, src=None):
            return pltpu.make_async_remote_copy(
                src_ref=rows(*block) if src is None else src, dst_ref=rows(*block),
                send_sem=send_sems.at[k], recv_sem=recv_sems.at[k], device_id=to, device_id_type=MESH)

        mine = pltpu.make_async_copy(x_ref, rows(*me), local_sem)
        mine.start()
        first = [copy(0, me, sibling, src=x_ref)]
        first += [copy(1 + j, me, (*chip, c), src=x_ref) for j, chip in enumerate(chips)]
        for cp in first:
            cp.start()
        passed = [copy(4 + j, (*chip, c), sibling) for j, chip in enumerate(chips)]
        for j, chip in enumerate(chips):
            copy(1 + j, (*chip, c), me).wait_recv()
            passed[j].start()
        copy(0, sibling, me).wait_recv()
        for j, chip in enumerate(chips):
            copy(4 + j, (*chip, 1 - c), me).wait_recv()
        for cp in first + passed:
            cp.wait_send()
        mine.wait()

    return pl.pallas_call(
        body, name=name, out_shape=jax.ShapeDtypeStruct((N_DEV * m_per, n), v.dtype),
        in_specs=[pl.BlockSpec(memory_space=pltpu.VMEM)], out_specs=pl.BlockSpec(memory_space=pltpu.VMEM),
        scratch_shapes=[pltpu.SemaphoreType.DMA((7,)), pltpu.SemaphoreType.DMA((7,)), pltpu.SemaphoreType.DMA],
    )(v)


def _cast_place(name, w, k_idx):
    R, C = w.shape
    tr = _pick(R, max(16, (512 * 1024) // C // 16 * 16), 16)

    def body(k_ref, w_ref, o_ref):
        o_ref[...] = w_ref[...].astype(BF16)

    return pl.pallas_call(
        body, name=name,
        grid_spec=pltpu.PrefetchScalarGridSpec(
            num_scalar_prefetch=1, grid=(R // tr,),
            in_specs=[pl.BlockSpec((tr, C), lambda i, k_ref: (i, 0))],
            out_specs=pl.BlockSpec((None, tr, C), lambda i, k_ref: (k_ref[0], i, 0))),
        out_shape=jax.ShapeDtypeStruct((N_CHIPS, R, C), BF16),
        compiler_params=_params(("parallel",)),
    )(k_idx, w)


def _gather_weights(bufs):
    n = len(bufs)

    def body(*refs):
        out_refs = refs[n:2 * n]
        send_sems, recv_sems = refs[2 * n:]
        x, y, c, chips = _place()
        k_me = 2 * x + y
        sibling = (x, y, 1 - c)

        def region(a, k, hc):
            rh = bufs[a].shape[1] // 2
            return out_refs[a].at[k, pl.ds(hc * rh, rh)]

        def copy(a, s, src, dst, to):
            return pltpu.make_async_remote_copy(src_ref=src, dst_ref=dst, send_sem=send_sems.at[6 * a + s],
                                                recv_sem=recv_sems.at[6 * a + s], device_id=to, device_id_type=MESH)

        started = []
        for a in range(n):
            mine = region(a, k_me, c)
            for j, chip in enumerate(chips):
                cp = copy(a, j, mine, mine, (*chip, c))
                cp.start()
                started.append(cp)
        for a in range(n):
            for j, chip in enumerate(chips):
                got = region(a, 2 * chip[0] + chip[1], c)
                copy(a, j, got, got, (x, y, c)).wait_recv()
                cp = copy(a, 3 + j, got, got, sibling)
                cp.start()
                started.append(cp)
        for a in range(n):
            for j, chip in enumerate(chips):
                got = region(a, 2 * chip[0] + chip[1], 1 - c)
                copy(a, 3 + j, got, got, (x, y, c)).wait_recv()
        for cp in started:
            cp.wait_send()

    return pl.pallas_call(
        body, name="gather_weights",
        out_shape=[jax.ShapeDtypeStruct(b.shape, b.dtype) for b in bufs],
        in_specs=[_HBM] * n, out_specs=[_HBM] * n, input_output_aliases={a: a for a in range(n)},
        scratch_shapes=[pltpu.SemaphoreType.DMA((6 * n,)), pltpu.SemaphoreType.DMA((6 * n,))],
    )(*bufs)


def _chip_copies(src_of, dst_of, got_of, n, sems):
    x, y, c, chips = _place()
    k_me = 2 * x + y
    send_sems, recv_sems = sems
    out = []
    for a in range(n):
        for j, chip in enumerate(chips):
            k_chip = 2 * chip[0] + chip[1]
            send = pltpu.make_async_remote_copy(
                src_ref=src_of(a, k_chip, k_me, c), dst_ref=dst_of(a, k_me, c), send_sem=send_sems.at[3 * a + j],
                recv_sem=recv_sems.at[3 * a + j], device_id=(*chip, c), device_id_type=MESH)
            got = got_of(a, k_chip, c)
            recv = pltpu.make_async_remote_copy(
                src_ref=got, dst_ref=got, send_sem=send_sems.at[3 * a + j], recv_sem=recv_sems.at[3 * a + j],
                device_id=(x, y, c), device_id_type=MESH)
            out.append((send, recv))
    return out


class _ChipJob:
    def start(self, j_in, j_out, sems):
        for send, _ in self.copies(j_in, j_out, sems):
            send.start()

    def finish(self, j_in, j_out, sems):
        copies = self.copies(j_in, j_out, sems)
        for _, recv in copies:
            recv.wait_recv()
        for send, _ in copies:
            send.wait_send()


class _GatherJob(_ChipJob):
    def __init__(self, bufs):
        n = len(bufs)
        self.inputs, self.n_alias = list(bufs), n
        self.sem_shapes = [pltpu.SemaphoreType.DMA((3 * n,)), pltpu.SemaphoreType.DMA((3 * n,))]
        self.half = [b.shape[1] // 2 for b in bufs]

    def copies(self, j_in, j_out, sems):
        def half(a, k, c):
            return j_out[a].at[k, pl.ds(c * self.half[a], self.half[a])]

        return _chip_copies(lambda a, k_chip, k_me, c: half(a, k_me, c), half, half, len(self.half), sems)


class _ScatterJob(_ChipJob):
    def __init__(self, parts, slots):
        n = len(parts)
        self.n = n
        self.inputs, self.n_alias = list(parts) + list(slots), n
        self.sem_shapes = [pltpu.SemaphoreType.DMA((3 * n,)), pltpu.SemaphoreType.DMA((3 * n,))]

    def copies(self, j_in, j_out, sems):
        return _chip_copies(lambda a, k_chip, k_me, c: j_in[a].at[k_chip], lambda a, k_me, c: j_out[a].at[k_me],
                            lambda a, k_chip, c: j_out[a].at[k_chip], self.n, sems)


def _forward_halves(name, bufs):
    n = len(bufs)

    def body(*refs):
        out_refs = refs[n:2 * n]
        send_sems, recv_sems = refs[2 * n:]
        x, y, c, chips = _place()
        started, waits = [], []
        for a in range(n):
            rh = bufs[a].shape[1] // 2
            for j, chip in enumerate(chips):
                k_chip = 2 * chip[0] + chip[1]
                got = out_refs[a].at[k_chip, pl.ds(c * rh, rh)]
                cp = pltpu.make_async_remote_copy(src_ref=got, dst_ref=got, send_sem=send_sems.at[3 * a + j],
                                                  recv_sem=recv_sems.at[3 * a + j], device_id=(x, y, 1 - c),
                                                  device_id_type=MESH)
                cp.start()
                started.append(cp)
                other = out_refs[a].at[k_chip, pl.ds((1 - c) * rh, rh)]
                waits.append(pltpu.make_async_remote_copy(
                    src_ref=other, dst_ref=other, send_sem=send_sems.at[3 * a + j], recv_sem=recv_sems.at[3 * a + j],
                    device_id=(x, y, c), device_id_type=MESH))
        for cp in waits:
            cp.wait_recv()
        for cp in started:
            cp.wait_send()

    return pl.pallas_call(
        body, name=name, out_shape=[jax.ShapeDtypeStruct(b.shape, b.dtype) for b in bufs],
        in_specs=[_HBM] * n, out_specs=[_HBM] * n, input_output_aliases={a: a for a in range(n)},
        scratch_shapes=[pltpu.SemaphoreType.DMA((3 * n,)), pltpu.SemaphoreType.DMA((3 * n,))],
    )(*bufs)


def _swap_halves(name, grads):
    n = len(grads)

    def body(*refs):
        in_refs, out_refs = refs[:n], refs[n:2 * n]
        send_sems, recv_sems = refs[2 * n:]
        x, y, c, _ = _place()
        cps = []
        for a in range(n):
            rh = grads[a].shape[1] // 2
            cp = pltpu.make_async_remote_copy(
                src_ref=in_refs[a].at[:, pl.ds((1 - c) * rh, rh)], dst_ref=out_refs[a],
                send_sem=send_sems.at[a], recv_sem=recv_sems.at[a], device_id=(x, y, 1 - c), device_id_type=MESH)
            cp.start()
            cps.append(cp)
        for cp in cps:
            cp.wait()

    return pl.pallas_call(
        body, name=name,
        out_shape=[jax.ShapeDtypeStruct((N_CHIPS, g.shape[1] // 2, g.shape[2]), g.dtype) for g in grads],
        in_specs=[_HBM] * n, out_specs=[_HBM] * n,
        scratch_shapes=[pltpu.SemaphoreType.DMA((n,)), pltpu.SemaphoreType.DMA((n,))],
    )(*grads)


def _pair_sum(name, g, q, c_idx):
    _, R, C = g.shape
    rh = R // 2
    tr = _pick(rh, max(16, (512 * 1024) // C // 16 * 16), 16)
    nh = rh // tr

    def body(c_ref, g_ref, q_ref, o_ref, o2_ref):
        s = (g_ref[...].astype(F32) + q_ref[...].astype(F32)).astype(BF16)
        o_ref[...] = s
        o2_ref[...] = s

    out_spec = pl.BlockSpec((None, tr, C), lambda k, i, c_ref: (k, i, 0))
    return pl.pallas_call(
        body, name=name,
        grid_spec=pltpu.PrefetchScalarGridSpec(
            num_scalar_prefetch=1, grid=(N_CHIPS, nh),
            in_specs=[pl.BlockSpec((None, tr, C), lambda k, i, c_ref: (k, c_ref[0] * nh + i, 0)),
                      pl.BlockSpec((None, tr, C), lambda k, i, c_ref: (k, i, 0))],
            out_specs=[out_spec, out_spec]),
        out_shape=[jax.ShapeDtypeStruct((N_CHIPS, rh, C), BF16)] * 2,
        compiler_params=_params(("parallel", "parallel")),
    )(c_idx, g, q)


def _sum_chips(name, r, c_idx):
    _, rh, C = r.shape
    tr = _pick(rh, max(16, (256 * 1024) // C // 16 * 16), 16)

    def body(c_ref, r_ref, o_ref):
        acc = r_ref[0].astype(F32)
        for k in range(1, N_CHIPS):
            acc = acc + r_ref[k].astype(F32)
        o_ref[...] = acc

    return pl.pallas_call(
        body, name=name,
        grid_spec=pltpu.PrefetchScalarGridSpec(
            num_scalar_prefetch=1, grid=(rh // tr,),
            in_specs=[pl.BlockSpec((N_CHIPS, tr, C), lambda i, c_ref: (0, i, 0))],
            out_specs=pl.BlockSpec((None, tr, C), lambda i, c_ref: (c_ref[0], i, 0))),
        out_shape=jax.ShapeDtypeStruct((2, rh, C), F32), compiler_params=_params(("parallel",)),
    )(c_idx, r)


def _share_halves(bufs):
    n = len(bufs)

    def body(*refs):
        out_refs = refs[n:2 * n]
        send_sems, recv_sems = refs[2 * n:]
        x, y, c, _ = _place()
        cps = []
        for a in range(n):
            cp = pltpu.make_async_remote_copy(
                src_ref=out_refs[a].at[c], dst_ref=out_refs[a].at[c], send_sem=send_sems.at[a],
                recv_sem=recv_sems.at[a], device_id=(x, y, 1 - c), device_id_type=MESH)
            cp.start()
            cps.append(cp)
        for a, cp in enumerate(cps):
            got = out_refs[a].at[1 - c]
            pltpu.make_async_remote_copy(src_ref=got, dst_ref=got, send_sem=send_sems.at[a], recv_sem=recv_sems.at[a],
                                         device_id=(x, y, c), device_id_type=MESH).wait_recv()
        for cp in cps:
            cp.wait_send()

    return pl.pallas_call(
        body, name="share_halves",
        out_shape=[jax.ShapeDtypeStruct(b.shape, b.dtype) for b in bufs],
        in_specs=[_HBM] * n, out_specs=[_HBM] * n, input_output_aliases={a: a for a in range(n)},
        scratch_shapes=[pltpu.SemaphoreType.DMA((n,)), pltpu.SemaphoreType.DMA((n,))],
    )(*bufs)


_BIG = ("w_in", "w_conv_out", "w_hgrn_out", "w_o", "w_ffn_gate", "w_ffn_up", "w_ffn_down")
_ROW_SHARDED = ("w_o", "w_ffn_down")
_WEIGHTS = ("w_ada", "b_ada", "norm_mix_g", "w_in", "conv_w", "lb_param", "gnorm_g", "w_conv_out", "w_hgrn_out",
            "w_o", "norm_ffn_g", "w_ffn_gate", "w_ffn_up", "w_ffn_down", "norm_final_g")


def _pack_rows(pieces):
    flat = jnp.concatenate([p.reshape(-1) for p in pieces])
    pad = (-flat.shape[0]) % 1024
    return jnp.pad(flat, (0, pad)).reshape(8, -1)


def kernel(x, c, w_ada, b_ada, norm_mix_g, w_in, conv_w, lb_param, gnorm_g, w_conv_out, w_hgrn_out, w_o, norm_ffn_g, w_ffn_gate, w_ffn_up, w_ffn_down, norm_final_g, loss_target, m_w_ada, m_b_ada, m_norm_mix_g, m_w_in, m_conv_w, m_lb_param, m_gnorm_g, m_w_conv_out, m_w_hgrn_out, m_w_o, m_norm_ffn_g, m_w_ffn_gate, m_w_ffn_up, m_w_ffn_down, m_norm_final_g, v_w_ada, v_b_ada, v_norm_mix_g, v_w_in, v_conv_w, v_lb_param, v_gnorm_g, v_w_conv_out, v_w_hgrn_out, v_w_o, v_norm_ffn_g, v_w_ffn_gate, v_w_ffn_up, v_w_ffn_down, v_norm_final_g):
    w = dict(w_ada=w_ada, b_ada=b_ada, norm_mix_g=norm_mix_g, w_in=w_in, conv_w=conv_w, lb_param=lb_param,
             gnorm_g=gnorm_g, w_conv_out=w_conv_out, w_hgrn_out=w_hgrn_out, w_o=w_o, norm_ffn_g=norm_ffn_g,
             w_ffn_gate=w_ffn_gate, w_ffn_up=w_ffn_up, w_ffn_down=w_ffn_down, norm_final_g=norm_final_g)
    m = dict(w_ada=m_w_ada, b_ada=m_b_ada, norm_mix_g=m_norm_mix_g, w_in=m_w_in, conv_w=m_conv_w, lb_param=m_lb_param,
             gnorm_g=m_gnorm_g, w_conv_out=m_w_conv_out, w_hgrn_out=m_w_hgrn_out, w_o=m_w_o, norm_ffn_g=m_norm_ffn_g,
             w_ffn_gate=m_w_ffn_gate, w_ffn_up=m_w_ffn_up, w_ffn_down=m_w_ffn_down, norm_final_g=m_norm_final_g)
    v = dict(w_ada=v_w_ada, b_ada=v_b_ada, norm_mix_g=v_norm_mix_g, w_in=v_w_in, conv_w=v_conv_w, lb_param=v_lb_param,
             gnorm_g=v_gnorm_g, w_conv_out=v_w_conv_out, w_hgrn_out=v_w_hgrn_out, w_o=v_w_o, norm_ffn_g=v_norm_ffn_g,
             w_ffn_gate=v_w_ffn_gate, w_ffn_up=v_w_ffn_up, w_ffn_down=v_w_ffn_down, norm_final_g=v_norm_final_g)
    two_d = lambda a: a.reshape((-1, a.shape[-1])) if a.ndim != 2 else a
    shapes = {k: a.shape for k, a in w.items()}
    w, m, v = ({k: two_d(a) for k, a in t.items()} for t in (w, m, v))
    w["lb_param"], m["lb_param"], v["lb_param"] = lb_param, m_lb_param, v_lb_param
    S, D = x.shape[1], x.shape[2]
    d_conv = D // 2
    ix, iy, ic = lax.axis_index("x"), lax.axis_index("y"), lax.axis_index("c")
    k_me = 2 * ix + iy
    dev = 2 * k_me + ic
    cw_shard = w["conv_w"].shape[1]
    ada_cols = w["w_ada"].shape[1]

    got = _allgather_rows("gather_cond", _pack_rows([c, w["conv_w"]])).reshape(N_DEV, -1)
    c_all = got[:, :D]
    conv_full = got[::2, D:D + 3 * cw_shard].reshape(N_CHIPS, 3, cw_shard).transpose(1, 0, 2).reshape(3, -1)
    c_act, c_act_b, lb = _prep(c_all, lb_param)
    b_cols = lax.dynamic_slice(w["b_ada"], (0, k_me * ada_cols), (1, ada_cols))
    mod_cols, = _matmul("ada_fwd", [(c_act_b, 'n', w["w_ada"].astype(BF16), 'n')], [0], N_DEV, ada_cols, D,
                        N_DEV, _pick(ada_cols, 1536, 128), D, [(b_cols, 'row', 0)], [(F32, None)],
                        lambda accs, ex: [accs[0] + ex[0]])
    mod_all = _allgather_rows("gather_mod", mod_cols).reshape(N_CHIPS, 2, N_DEV, ada_cols)[:, 0]
    mod_all = mod_all.transpose(1, 0, 2).reshape(N_DEV, -1)
    mod = lax.dynamic_slice(mod_all, (dev, 0), (1, mod_all.shape[1]))
    k_idx = jnp.reshape(k_me, (1,)).astype(jnp.int32)
    c_idx = jnp.reshape(ic, (1,)).astype(jnp.int32)
    bufs = {k: _cast_place("cast_" + k, w[k], k_idx) for k in _BIG}

    loss, dx, small, arrived = _local_step(
        x[0], loss_target[0], mod, w["norm_mix_g"], lb, w["gnorm_g"], w["norm_ffn_g"], w["norm_final_g"], conv_full,
        bufs, c_idx)

    pieces = [small["dmod"], small["dg_mix"], small["dg_ffn"], small["dg_final"], small["dlb"], small["dgn"],
              small["dconv_w"], loss]
    sizes = [p.size for p in pieces]
    parts = _allgather_rows("gather_small", _pack_rows(pieces)).reshape(N_DEV, -1)
    total = _small_reduce(parts)
    offs = [0]
    for s in sizes:
        offs.append(offs[-1] + s)
    tot = [total[:, offs[i]:offs[i + 1]] for i in range(len(sizes))]
    dmod_all = parts[:, :sizes[0]]
    grads = {
        "b_ada": tot[0], "norm_mix_g": tot[1], "norm_ffn_g": tot[2], "norm_final_g": tot[3],
        "lb_param": _lb_grad(tot[4], lb), "gnorm_g": tot[5],
        "conv_w": lax.dynamic_slice(tot[6].reshape(3, -1), (0, k_me * cw_shard), (3, cw_shard)),
    }
    loss_out = tot[7][0, 0]

    halves = [_sum_chips("sum_chips_" + k, arrived[k], c_idx) for k in _BIG]
    whole = _share_halves(halves)
    for k, g in zip(_BIG, whole):
        grads[k] = g.reshape(-1, g.shape[-1])

    delta, new_m, new_v = {}, {}, {}
    dmod_cols = lax.dynamic_slice(dmod_all, (0, k_me * ada_cols), (N_DEV, ada_cols))
    grads["w_ada"], delta["w_ada"], new_m["w_ada"], new_v["w_ada"] = _ada_update(
        c_act.T, dmod_cols, w["w_ada"], m["w_ada"], v["w_ada"])
    for k in _WEIGHTS:
        if k != "w_ada":
            delta[k], new_m[k], new_v[k] = _adamw("adamw_" + k, w[k], grads[k], m[k], v[k])
    outs = [loss_out, dx.reshape(x.shape)]
    for t in (grads, delta, new_m, new_v):
        outs += [t[k].reshape(shapes[k]) for k in _WEIGHTS]
    return tuple(outs)
```

```python
import functools

import jax
import jax.numpy as jnp
from jax import lax
from jax.experimental import pallas as pl
from jax.experimental.pallas import tpu as pltpu

F32 = jnp.float32
BF16 = jnp.bfloat16
MESH = pl.DeviceIdType.MESH

EPS = 1e-6
HEAD = 128
BLK = 16
N_CHIPS = 4
N_DEV = 8
VMEM_LIMIT_BYTES = 56 * 1024 * 1024

ADAM_LR = 0.001
ADAM_B1 = 0.9
ADAM_B2 = 0.999
ADAM_EPS = 1e-08
ADAM_WD = 0.01
ADAM_STEP = 10


def _pick(dim, pref, mult):
    if dim <= pref:
        return dim
    t = (pref // mult) * mult
    while t >= mult:
        if dim % t == 0:
            return t
        t -= mult
    return dim


def _sig(x):
    return 1.0 / (1.0 + jnp.exp(-x))


def _params(sem):
    return pltpu.CompilerParams(dimension_semantics=sem, vmem_limit_bytes=VMEM_LIMIT_BYTES)


def _gj(j, i, k):
    return j


def _gk(j, i, k):
    return k


def _wspec(arr, rt, ct, r_of, c_of):
    if arr.ndim == 2:
        return pl.BlockSpec((rt, ct), lambda j, i, k: (r_of(j, i, k), c_of(j, i, k)))
    per = arr.shape[2] // ct
    assert arr.shape[2] % ct == 0
    return pl.BlockSpec((None, rt, ct),
                        lambda j, i, k: (c_of(j, i, k) // per, r_of(j, i, k), c_of(j, i, k) % per))


_HBM = pl.BlockSpec(memory_space=pltpu.HBM)


def _host(job, grid, in_specs, args, out_specs, out_shape, scratch):
    if job is None:
        return {}, (lambda body: body)
    n_in, n_out, n_scr = len(args), len(out_shape), len(scratch)
    n_job, n_alias = len(job.inputs), job.n_alias
    in_specs += [_HBM] * n_job
    args += job.inputs
    out_specs += [_HBM] * n_alias
    out_shape += [jax.ShapeDtypeStruct(a.shape, a.dtype) for a in job.inputs[n_job - n_alias:]]
    scratch += job.sem_shapes
    aliases = {n_in + n_job - n_alias + t: n_out + t for t in range(n_alias)}

    def wrap(body):
        def hosted(*refs):
            ins, refs = refs[:n_in], refs[n_in:]
            j_in, refs = refs[:n_job], refs[n_job:]
            outs, refs = refs[:n_out], refs[n_out:]
            j_out, refs = refs[:n_alias], refs[n_alias:]
            scr, sems = refs[:n_scr], refs[n_scr:]
            first = functools.reduce(jnp.logical_and, [pl.program_id(d) == 0 for d in range(len(grid))])
            last = functools.reduce(jnp.logical_and, [pl.program_id(d) == grid[d] - 1 for d in range(len(grid))])

            @pl.when(first)
            def _():
                job.start(j_in, j_out, sems)

            body(*ins, *outs, *scr)

            @pl.when(last)
            def _():
                job.finish(j_in, j_out, sems)

        return hosted

    return aliases, wrap


def _matmul(name, pairs, acc_of, M, N, K, tm, tn, tk, extras, outs, epilogue, job=None):
    assert M % tm == 0 and N % tn == 0 and K % tk == 0, (name, M, N, K, tm, tn, tk)
    nj, ni, nk = N // tn, M // tm, K // tk
    n_pairs, n_extra, n_out = len(pairs), len(extras), len(outs)
    n_acc = max(acc_of) + 1
    in_specs, args, dims = [], [], []
    for a, am, b, bm in pairs:
        if am == 'n':
            in_specs.append(pl.BlockSpec((tm, tk), lambda j, i, k: (i, k)))
        else:
            in_specs.append(pl.BlockSpec((tk, tm), lambda j, i, k: (k, i)))
        if bm == 'n':
            in_specs.append(_wspec(b, tk, tn, _gk, _gj))
        else:
            in_specs.append(_wspec(b, tn, tk, _gj, _gk))
        dims.append((((1 if am == 'n' else 0,), (0 if bm == 'n' else 1,)), ((), ())))
        args += [a, b]
    for e, kind, off in extras:
        assert off % tn == 0, (name, off, tn)
        o = off // tn
        if kind == 'tile':
            in_specs.append(pl.BlockSpec((tm, tn), lambda j, i, k, o=o: (i, j + o)))
        else:
            in_specs.append(pl.BlockSpec((1, tn), lambda j, i, k, o=o: (0, j + o)))
        args.append(e)
    out_shape, out_specs = [], []
    for dt, nb in outs:
        if nb is None:
            out_shape.append(jax.ShapeDtypeStruct((M, N), dt))
            out_specs.append(pl.BlockSpec((tm, tn), lambda j, i, k: (i, j)))
        else:
            assert nb % tn == 0 and N % nb == 0
            per = nb // tn
            out_shape.append(jax.ShapeDtypeStruct((N // nb, M, nb), dt))
            out_specs.append(pl.BlockSpec((None, tm, tn), lambda j, i, k, per=per: (j // per, i, j % per)))

    def body(*refs):
        ab = refs[:2 * n_pairs]
        e_refs = refs[2 * n_pairs:2 * n_pairs + n_extra]
        o_refs = refs[2 * n_pairs + n_extra:2 * n_pairs + n_extra + n_out]
        acc_refs = refs[2 * n_pairs + n_extra + n_out:]
        sums = [None] * n_acc
        for p in range(n_pairs):
            prod = lax.dot_general(ab[2 * p][...], ab[2 * p + 1][...], dims[p], preferred_element_type=F32)
            a = acc_of[p]
            sums[a] = prod if sums[a] is None else sums[a] + prod

        def finish(accs):
            res = epilogue(accs, [e[...] for e in e_refs])
            for o_ref, r in zip(o_refs, res):
                o_ref[...] = r.astype(o_ref.dtype)

        if nk == 1:
            finish(sums)
        else:
            k = pl.program_id(2)

            @pl.when(k == 0)
            def _():
                for a in range(n_acc):
                    acc_refs[a][...] = sums[a]

            @pl.when(k > 0)
            def _():
                for a in range(n_acc):
                    acc_refs[a][...] += sums[a]

            @pl.when(k == nk - 1)
            def _():
                finish([acc_refs[a][...] for a in range(n_acc)])

    scratch = [pltpu.VMEM((tm, tn), F32) for _ in range(n_acc)] if nk > 1 else []
    grid = (nj, ni, nk)
    aliases, wrap = _host(job, grid, in_specs, args, out_specs, out_shape, scratch)
    sem = ("parallel", "parallel", "arbitrary") if job is None else ("arbitrary",) * 3
    return pl.pallas_call(
        wrap(body), name=name, grid=grid, in_specs=in_specs, out_specs=out_specs, out_shape=out_shape,
        scratch_shapes=scratch, input_output_aliases=aliases, compiler_params=_params(sem),
    )(*args)


def _row_spec(tr, d):
    return pl.BlockSpec((tr, d), lambda i: (i, 0))


def _vec_spec(d):
    return pl.BlockSpec((1, d), lambda i: (0, 0))


def _norm_mod(name, x, g, sc, sh):
    S, D = x.shape
    tr = _pick(S, 256, 8)

    def body(x_ref, g_ref, sc_ref, sh_ref, h_ref):
        xv = x_ref[...]
        r = lax.rsqrt(jnp.mean(xv * xv, axis=-1, keepdims=True) + EPS)
        h_ref[...] = ((xv * r) * g_ref[...] * (1.0 + sc_ref[...]) + sh_ref[...]).astype(BF16)

    return pl.pallas_call(
        body, name=name, grid=(S // tr,),
        in_specs=[_row_spec(tr, D), _vec_spec(D), _vec_spec(D), _vec_spec(D)],
        out_specs=_row_spec(tr, D), out_shape=jax.ShapeDtypeStruct((S, D), BF16),
        compiler_params=_params(("parallel",)),
    )(x, g, sc, sh)


def _loss_head(x2, target, g, ff, gt):
    S, D = x2.shape
    tr = _pick(S, 256, 8)

    def body(x_ref, t_ref, g_ref, ff_ref, gt_ref, dx_ref, dffb_ref, loss_ref, dgt_ref, dg_ref):
        i = pl.program_id(0)

        @pl.when(i == 0)
        def _():
            loss_ref[...] = jnp.zeros_like(loss_ref)
            dgt_ref[...] = jnp.zeros_like(dgt_ref)
            dg_ref[...] = jnp.zeros_like(dg_ref)

        xv = x_ref[...]
        gv = g_ref[...]
        r = lax.rsqrt(jnp.mean(xv * xv, axis=-1, keepdims=True) + EPS)
        xh = xv * r
        err = xh * gv - t_ref[...]
        loss_ref[...] += 0.5 * jnp.sum(jnp.mean(err * err, axis=-1, keepdims=True))
        dy = err * (1.0 / D)
        dg_ref[...] += jnp.sum(dy * xh, axis=0, keepdims=True)
        dxh = dy * gv
        dx = r * (dxh - xh * jnp.mean(dxh * xh, axis=-1, keepdims=True))
        dx_ref[...] = dx
        dffb_ref[...] = (dx * gt_ref[...]).astype(BF16)
        dgt_ref[...] += jnp.sum(dx * ff_ref[...], axis=0, keepdims=True)

    return pl.pallas_call(
        body, name="loss_head", grid=(S // tr,),
        in_specs=[_row_spec(tr, D), _row_spec(tr, D), _vec_spec(D), _row_spec(tr, D), _vec_spec(D)],
        out_specs=[_row_spec(tr, D), _row_spec(tr, D), pl.BlockSpec((1, 128), lambda i: (0, 0)),
                   _vec_spec(D), _vec_spec(D)],
        out_shape=[jax.ShapeDtypeStruct((S, D), F32), jax.ShapeDtypeStruct((S, D), BF16),
                   jax.ShapeDtypeStruct((1, 128), F32), jax.ShapeDtypeStruct((1, D), F32),
                   jax.ShapeDtypeStruct((1, D), F32)],
        compiler_params=_params(("arbitrary",)),
    )(x2, target, g, ff, gt)


def _norm_mod_bwd(name, dh, x, g, sc, dres, gated=None):
    S, D = x.shape
    tr = _pick(S, 256, 8)
    n = S // tr
    with_gate = gated is not None

    def body(*refs):
        if with_gate:
            dh_ref, x_ref, g_ref, sc_ref, dres_ref, mo_ref, gt_ref, dx_ref, dsh_ref, dsc_ref, dg_ref, dmob_ref, dgt_ref = refs
        else:
            dh_ref, x_ref, g_ref, sc_ref, dres_ref, dx_ref, dsh_ref, dsc_ref, dg_ref = refs
        i = pl.program_id(0)

        @pl.when(i == 0)
        def _():
            dsh_ref[...] = jnp.zeros_like(dsh_ref)
            dsc_ref[...] = jnp.zeros_like(dsc_ref)
            if with_gate:
                dgt_ref[...] = jnp.zeros_like(dgt_ref)

        xv = x_ref[...]
        dhv = dh_ref[...]
        gv = g_ref[...]
        scale = 1.0 + sc_ref[...]
        r = lax.rsqrt(jnp.mean(xv * xv, axis=-1, keepdims=True) + EPS)
        xh = xv * r
        dsh_ref[...] += jnp.sum(dhv, axis=0, keepdims=True)
        dsc_ref[...] += jnp.sum(dhv * xh, axis=0, keepdims=True)
        dxh = dhv * (gv * scale)
        dx = dres_ref[...] + r * (dxh - xh * jnp.mean(dxh * xh, axis=-1, keepdims=True))
        dx_ref[...] = dx
        if with_gate:
            dmob_ref[...] = (dx * gt_ref[...]).astype(BF16)
            dgt_ref[...] += jnp.sum(dx * mo_ref[...], axis=0, keepdims=True)

        @pl.when(i == n - 1)
        def _():
            t = dsc_ref[...]
            dg_ref[...] = t * scale
            dsc_ref[...] = t * gv

    in_specs = [_row_spec(tr, D), _row_spec(tr, D), _vec_spec(D), _vec_spec(D), _row_spec(tr, D)]
    args = [dh, x, g, sc, dres]
    out_specs = [_row_spec(tr, D), _vec_spec(D), _vec_spec(D), _vec_spec(D)]
    out_shape = [jax.ShapeDtypeStruct((S, D), F32)] + [jax.ShapeDtypeStruct((1, D), F32)] * 3
    if with_gate:
        in_specs += [_row_spec(tr, D), _vec_spec(D)]
        args += list(gated)
        out_specs += [_row_spec(tr, D), _vec_spec(D)]
        out_shape += [jax.ShapeDtypeStruct((S, D), BF16), jax.ShapeDtypeStruct((1, D), F32)]
    return pl.pallas_call(
        body, name=name, grid=(n,), in_specs=in_specs, out_specs=out_specs, out_shape=out_shape,
        compiler_params=_params(("arbitrary",)),
    )(*args)


CONV_ROWS = 256


def _col_spec(S, off_cols):
    o = off_cols // HEAD
    return pl.BlockSpec((S, HEAD), lambda c, o=o: (0, c + o))


def _conv_taps(u, u_prev, rid):
    s1 = jnp.where(rid >= 1, pltpu.roll(u, 1, 0), pltpu.roll(u_prev, 1, 0))
    s2 = jnp.where(rid >= 2, pltpu.roll(u, 2, 0), pltpu.roll(u_prev, 2, 0))
    return s1, s2


def _conv_fwd(proj, conv_w, d_conv):
    S = proj.shape[0]
    R = min(CONV_ROWS, S)
    nr = S // R

    def body(ab_ref, ac_ref, ax_ref, w_ref, ya_ref):
        w0, w1, w2 = w_ref[0:1, :], w_ref[1:2, :], w_ref[2:3, :]
        rid = lax.broadcasted_iota(jnp.int32, (R, HEAD), 0)

        def step(c, carry):
            rows = pl.ds(pl.multiple_of(c * R, R), R)
            prev = pl.ds(pl.multiple_of(jnp.maximum(c - 1, 0) * R, R), R)
            u = ac_ref[rows, :] * ax_ref[rows, :]
            u_prev = jnp.where(c > 0, ac_ref[prev, :] * ax_ref[prev, :], 0.0)
            s1, s2 = _conv_taps(u, u_prev, rid)
            y = w0 * s2 + w1 * s1 + w2 * u
            ya_ref[rows, :] = (ab_ref[rows, :] * y).astype(BF16)
            return carry

        lax.fori_loop(0, nr, step, 0)

    return pl.pallas_call(
        body, name="conv_fwd", grid=(d_conv // HEAD,),
        in_specs=[_col_spec(S, 0), _col_spec(S, d_conv), _col_spec(S, 2 * d_conv),
                  pl.BlockSpec((3, HEAD), lambda c: (0, c))],
        out_specs=pl.BlockSpec((S, HEAD), lambda c: (0, c)),
        out_shape=jax.ShapeDtypeStruct((S, d_conv), BF16),
        compiler_params=_params(("parallel",)),
    )(proj, proj, proj, conv_w)


def _conv_bwd(dya, proj, conv_w, d_conv):
    S = proj.shape[0]
    R = min(CONV_ROWS, S)
    nr = S // R

    def body(dya_ref, ab_ref, ac_ref, ax_ref, w_ref, dab_ref, dac_ref, dax_ref, dw_ref):
        w0, w1, w2 = w_ref[0:1, :], w_ref[1:2, :], w_ref[2:3, :]
        rid = lax.broadcasted_iota(jnp.int32, (R, HEAD), 0)

        def step(c, carry):
            dw0, dw1, dw2 = carry
            rows = pl.ds(pl.multiple_of(c * R, R), R)
            prev = pl.ds(pl.multiple_of(jnp.maximum(c - 1, 0) * R, R), R)
            nxt = pl.ds(pl.multiple_of(jnp.minimum(c + 1, nr - 1) * R, R), R)
            a_c, a_x, a_b = ac_ref[rows, :], ax_ref[rows, :], ab_ref[rows, :]
            u = a_c * a_x
            u_prev = jnp.where(c > 0, ac_ref[prev, :] * ax_ref[prev, :], 0.0)
            s1, s2 = _conv_taps(u, u_prev, rid)
            y = w0 * s2 + w1 * s1 + w2 * u
            dy = dya_ref[rows, :]
            dab_ref[rows, :] = (dy * y).astype(BF16)
            dyc = dy * a_b
            dyc_next = jnp.where(c < nr - 1, dya_ref[nxt, :] * ab_ref[nxt, :], 0.0)
            t1 = jnp.where(rid < R - 1, pltpu.roll(dyc, R - 1, 0), pltpu.roll(dyc_next, R - 1, 0))
            t2 = jnp.where(rid < R - 2, pltpu.roll(dyc, R - 2, 0), pltpu.roll(dyc_next, R - 2, 0))
            du = w2 * dyc + w1 * t1 + w0 * t2
            dac_ref[rows, :] = (du * a_x).astype(BF16)
            dax_ref[rows, :] = (du * a_c).astype(BF16)
            dw0 = dw0 + jnp.sum(dyc * s2, axis=0, keepdims=True)
            dw1 = dw1 + jnp.sum(dyc * s1, axis=0, keepdims=True)
            dw2 = dw2 + jnp.sum(dyc * u, axis=0, keepdims=True)
            return dw0, dw1, dw2

        z = jnp.zeros((1, HEAD), F32)
        dw0, dw1, dw2 = lax.fori_loop(0, nr, step, (z, z, z))
        dw_ref[0:1, :] = dw0
        dw_ref[1:2, :] = dw1
        dw_ref[2:3, :] = dw2

    col = pl.BlockSpec((S, HEAD), lambda c: (0, c))
    return pl.pallas_call(
        body, name="conv_bwd", grid=(d_conv // HEAD,),
        in_specs=[col, _col_spec(S, 0), _col_spec(S, d_conv), _col_spec(S, 2 * d_conv),
                  pl.BlockSpec((3, HEAD), lambda c: (0, c))],
        out_specs=[col, col, col, pl.BlockSpec((3, HEAD), lambda c: (0, c))],
        out_shape=[jax.ShapeDtypeStruct((S, d_conv), BF16)] * 3 + [jax.ShapeDtypeStruct((3, d_conv), F32)],
        compiler_params=_params(("parallel",)),
    )(dya, proj, proj, proj, conv_w)


HGRN_ROWS = 256
HGRN_FWD_SUB = 64
HGRN_BWD_SUB = 32


def _block_cumsum(x, pos):
    for s in (1, 2, 4, 8):
        x = x + jnp.where(pos >= s, pltpu.roll(x, s, 0), 0.0)
    return x


def _block_rev_cumsum(x, pos, rows):
    for s in (1, 2, 4, 8):
        x = x + jnp.where(pos < BLK - s, pltpu.roll(x, rows - s, 0), 0.0)
    return x


def _block_last(x, pos, rows):
    m = jnp.where(pos == BLK - 1, x, 0.0)
    for s in (1, 2, 4, 8):
        m = m + pltpu.roll(m, rows - s, 0)
    return m


def _hgrn_gates(q, z, lb):
    sgq = _sig(q)
    qs = q * sgq
    sg = _sig(z)
    f = lb + (1.0 - lb) * sg
    kk = (1.0 - lb) * (1.0 - sg)
    return sgq, qs, sg, f, kk


def _hgrn_decays(q, z, lb, pos, rows):
    sgq, qs, sg, f, kk = _hgrn_gates(q, z, lb)
    b = _block_cumsum(jnp.log(f), pos)
    return sgq, qs, sg, f, kk, b, _block_last(b, pos, rows)


def _hgrn_specs(T, d_conv, n_t, rev):
    def t_of(i):
        return (n_t - 1 - i) if rev else i

    def pcol(off):
        o = off // HEAD
        return pl.BlockSpec((T, HEAD), lambda h, i, o=o: (t_of(i), h + o))

    q_spec, z_spec, v_spec, g_spec = pcol(3 * d_conv), pcol(4 * d_conv), pcol(5 * d_conv), pcol(6 * d_conv)
    lb_spec = pl.BlockSpec((1, HEAD), lambda h, i: (0, h))
    gn_spec = pl.BlockSpec((1, HEAD), lambda h, i: (0, 0))
    act_spec = pl.BlockSpec((T, HEAD), lambda h, i: (t_of(i), h))
    st_spec = pl.BlockSpec((None, T // BLK, HEAD, HEAD), lambda h, i: (h, t_of(i), 0, 0))
    return q_spec, z_spec, v_spec, g_spec, lb_spec, gn_spec, act_spec, st_spec


def _hgrn_fwd(proj, lb, gn, d_conv, job=None):
    S = proj.shape[0]
    H = d_conv // HEAD
    T = min(HGRN_ROWS, S)
    n_t, nb = S // T, T // BLK
    sub = min(HGRN_FWD_SUB, T)
    q_spec, z_spec, v_spec, g_spec, lb_spec, gn_spec, act_spec, st_spec = _hgrn_specs(T, d_conv, n_t, False)

    def body(q_ref, z_ref, v_ref, g_ref, lb_ref, gn_ref, ob_ref, o_ref, st_ref, state, qe_s, ke_s, v_s, dec_s, o_s, u_s):
        @pl.when(pl.program_id(1) == 0)
        def _():
            state[...] = jnp.zeros_like(state)

        pos = lax.broadcasted_iota(jnp.int32, (sub, HEAD), 0) % BLK
        lbv = lb_ref[...]

        def vector_pass(s, carry):
            r = pl.ds(pl.multiple_of(s * sub, sub), sub)
            v = v_ref[r, :]
            _, qs, _, _, kk, b, b_tot = _hgrn_decays(q_ref[r, :], z_ref[r, :], lbv, pos, sub)
            qe_s[r, :] = (qs * jnp.exp(b)).astype(BF16)
            ke_s[r, :] = (kk * jnp.exp(b_tot - b)).astype(BF16)
            v_s[r, :] = v.astype(BF16)
            dec_s[r, :] = jnp.exp(b_tot)
            o = jnp.sum(qs * kk, axis=-1, keepdims=True) * v
            for d in range(1, BLK):
                e = jnp.exp(b - pltpu.roll(b, d, 0))
                a = jnp.sum(qs * pltpu.roll(kk, d, 0) * e, axis=-1, keepdims=True)
                o = o + jnp.where(pos >= d, a * pltpu.roll(v, d, 0), 0.0)
            o_s[r, :] = o
            return carry

        lax.fori_loop(0, T // sub, vector_pass, 0)

        for j in range(nb):
            rows = pl.ds(j * BLK, BLK)
            u_s[j] = lax.dot_general(v_s[rows, :], ke_s[rows, :], (((0,), (0,)), ((), ())),
                                     preferred_element_type=F32)
        st = state[...]
        for j in range(nb):
            st_ref[j] = st.astype(BF16)
            st = st * dec_s[pl.ds(j * BLK, 1), :] + u_s[j]
        state[...] = st
        for j in range(nb):
            rows = pl.ds(j * BLK, BLK)
            o_s[rows, :] += lax.dot_general(qe_s[rows, :], st_ref[j], (((1,), (1,)), ((), ())),
                                            preferred_element_type=F32)
        o = o_s[...]
        o_ref[...] = o
        r = lax.rsqrt(jnp.mean(o * o, axis=-1, keepdims=True) + EPS)
        g = g_ref[...]
        ob_ref[...] = ((o * r) * gn_ref[...] * (g * _sig(g))).astype(BF16)

    grid = (H, n_t)
    in_specs = [q_spec, z_spec, v_spec, g_spec, lb_spec, gn_spec]
    args = [proj, proj, proj, proj, lb, gn]
    out_specs = [act_spec, act_spec, st_spec]
    out_shape = [jax.ShapeDtypeStruct((S, d_conv), BF16), jax.ShapeDtypeStruct((S, d_conv), F32),
                 jax.ShapeDtypeStruct((H, S // BLK, HEAD, HEAD), BF16)]
    scratch = [pltpu.VMEM((HEAD, HEAD), F32), pltpu.VMEM((T, HEAD), BF16), pltpu.VMEM((T, HEAD), BF16),
               pltpu.VMEM((T, HEAD), BF16), pltpu.VMEM((T, HEAD), F32), pltpu.VMEM((T, HEAD), F32),
               pltpu.VMEM((nb, HEAD, HEAD), F32)]
    aliases, wrap = _host(job, grid, in_specs, args, out_specs, out_shape, scratch)
    return pl.pallas_call(
        wrap(body), name="hgrn_fwd", grid=grid, in_specs=in_specs, out_specs=out_specs, out_shape=out_shape,
        scratch_shapes=scratch, input_output_aliases=aliases,
        compiler_params=_params(("parallel" if job is None else "arbitrary", "arbitrary")),
    )(*args)


def _hgrn_bwd(dob, o_raw, states, proj, lb, gn, d_conv, job=None):
    S = proj.shape[0]
    H = d_conv // HEAD
    T = min(HGRN_ROWS, S)
    n_t, nb = S // T, T // BLK
    sub = min(HGRN_BWD_SUB, T)
    q_spec, z_spec, v_spec, g_spec, lb_spec, gn_spec, act_spec, st_spec = _hgrn_specs(T, d_conv, n_t, True)

    def body(dob_ref, o_ref, st_ref, q_ref, z_ref, v_ref, g_ref, lb_ref, gn_ref,
             dq_ref, dz_ref, dv_ref, dg_ref, dlb_ref, dgn_ref,
             dstate, do_b, qe_b, ke_b, v_b, dec_s, dqe_s, dke_s, dvi_s, ddec_s, do_s, u_s, ds_s):
        @pl.when(pl.program_id(1) == 0)
        def _():
            dstate[...] = jnp.zeros_like(dstate)
            dlb_ref[...] = jnp.zeros_like(dlb_ref)
            dgn_ref[...] = jnp.zeros_like(dgn_ref)

        pos = lax.broadcasted_iota(jnp.int32, (sub, HEAD), 0) % BLK
        lbv, gnv = lb_ref[...], gn_ref[...]

        def before(s, carry):
            r = pl.ds(pl.multiple_of(s * sub, sub), sub)
            v, g = v_ref[r, :], g_ref[r, :]
            _, qs, _, _, kk, b, b_tot = _hgrn_decays(q_ref[r, :], z_ref[r, :], lbv, pos, sub)
            qe_b[r, :] = (qs * jnp.exp(b)).astype(BF16)
            ke_b[r, :] = (kk * jnp.exp(b_tot - b)).astype(BF16)
            v_b[r, :] = v.astype(BF16)
            dec_s[r, :] = jnp.exp(b_tot)
            o = o_ref[r, :]
            rs = lax.rsqrt(jnp.mean(o * o, axis=-1, keepdims=True) + EPS)
            on = o * rs
            sgg = _sig(g)
            dobv = dob_ref[r, :]
            dog = dobv * (g * sgg)
            dgn_ref[...] += jnp.sum(dog * on, axis=0, keepdims=True)
            don = dog * gnv
            do = rs * (don - on * jnp.mean(don * on, axis=-1, keepdims=True))
            dg_ref[r, :] = (dobv * (on * gnv) * (sgg * (1.0 + g * (1.0 - sgg)))).astype(BF16)
            do_s[r, :] = do
            do_b[r, :] = do.astype(BF16)
            return carry

        lax.fori_loop(0, T // sub, before, 0)

        for j in range(nb):
            rows = pl.ds(j * BLK, BLK)
            dob_j = do_b[rows, :]
            u_s[j] = lax.dot_general(dob_j, qe_b[rows, :], (((0,), (0,)), ((), ())), preferred_element_type=F32)
            dqe_s[rows, :] = lax.dot_general(dob_j, st_ref[j], (((1,), (0,)), ((), ())), preferred_element_type=F32)
        dst = dstate[...]
        for j in reversed(range(nb)):
            ds_s[j] = dst.astype(BF16)
            ddec = jnp.sum(st_ref[j].astype(F32) * dst, axis=0, keepdims=True)
            ddec_s[pl.ds(j * BLK, BLK), :] = jnp.broadcast_to(ddec, (BLK, HEAD))
            dst = dst * dec_s[pl.ds(j * BLK, 1), :] + u_s[j]
        dstate[...] = dst
        for j in range(nb):
            rows = pl.ds(j * BLK, BLK)
            dke_s[rows, :] = lax.dot_general(v_b[rows, :], ds_s[j], (((1,), (0,)), ((), ())), preferred_element_type=F32)
            dvi_s[rows, :] = lax.dot_general(ke_b[rows, :], ds_s[j], (((1,), (1,)), ((), ())), preferred_element_type=F32)

        def after(s, carry):
            r = pl.ds(pl.multiple_of(s * sub, sub), sub)
            q, v = q_ref[r, :], v_ref[r, :]
            sgq, qs, sg, f, kk, b, b_tot = _hgrn_decays(q, z_ref[r, :], lbv, pos, sub)
            do = do_s[r, :]
            dqs = dqe_s[r, :] * jnp.exp(b)
            dkk = dke_s[r, :] * jnp.exp(b_tot - b)
            d_tot = jnp.where(pos == BLK - 1, _block_cumsum(dkk * kk, pos) + ddec_s[r, :] * jnp.exp(b_tot), 0.0)
            dv = dvi_s[r, :]
            da = jnp.sum(do * v, axis=-1, keepdims=True)
            dv = dv + jnp.sum(qs * kk, axis=-1, keepdims=True) * do
            dqs = dqs + da * kk
            dkk = dkk + da * qs
            for d in range(1, BLK):
                kd = pltpu.roll(kk, d, 0)
                e = jnp.where(pos >= d, jnp.exp(b - pltpu.roll(b, d, 0)), 0.0)
                a = jnp.sum(qs * kd * e, axis=-1, keepdims=True)
                da = jnp.sum(do * pltpu.roll(v, d, 0), axis=-1, keepdims=True)
                gq = da * e
                dqs = dqs + gq * kd
                dkk = dkk + pltpu.roll(gq * qs, sub - d, 0)
                dv = dv + pltpu.roll(a * do, sub - d, 0)
            db = qs * dqs - kk * dkk + d_tot
            dlf = _block_rev_cumsum(db, pos, sub)
            df = dlf / f - dkk
            dlb_ref[...] += jnp.sum(df * (1.0 - sg), axis=0, keepdims=True)
            dz_ref[r, :] = (df * (1.0 - lbv) * sg * (1.0 - sg)).astype(BF16)
            dq_ref[r, :] = (dqs * (sgq * (1.0 + q * (1.0 - sgq)))).astype(BF16)
            dv_ref[r, :] = dv.astype(BF16)
            return carry

        lax.fori_loop(0, T // sub, after, 0)

    tb = lambda dt: pltpu.VMEM((T, HEAD), dt)
    grid = (H, n_t)
    in_specs = [act_spec, act_spec, st_spec, q_spec, z_spec, v_spec, g_spec, lb_spec, gn_spec]
    args = [dob, o_raw, states, proj, proj, proj, proj, lb, gn]
    out_specs = [act_spec, act_spec, act_spec, act_spec, lb_spec, pl.BlockSpec((None, 1, HEAD), lambda h, i: (h, 0, 0))]
    out_shape = ([jax.ShapeDtypeStruct((S, d_conv), BF16)] * 4
                 + [jax.ShapeDtypeStruct((1, d_conv), F32), jax.ShapeDtypeStruct((H, 1, HEAD), F32)])
    scratch = [pltpu.VMEM((HEAD, HEAD), F32), tb(BF16), tb(BF16), tb(BF16), tb(BF16), tb(F32),
               tb(F32), tb(F32), tb(F32), tb(F32), tb(F32),
               pltpu.VMEM((nb, HEAD, HEAD), F32), pltpu.VMEM((nb, HEAD, HEAD), BF16)]
    aliases, wrap = _host(job, grid, in_specs, args, out_specs, out_shape, scratch)
    return pl.pallas_call(
        wrap(body), name="hgrn_bwd", grid=grid, in_specs=in_specs, out_specs=out_specs, out_shape=out_shape,
        scratch_shapes=scratch, input_output_aliases=aliases,
        compiler_params=_params(("parallel" if job is None else "arbitrary", "arbitrary")),
    )(*args)


def _first(accs, extras):
    return [accs[0]]


def _local_step(x, target, mod, norm_mix_g, lb, gnorm_g, norm_ffn_g, norm_final_g, conv_w, bufs, c_idx):
    S, D = x.shape
    d_conv = D // 2
    d_in = 7 * d_conv + 2 * D
    d_ff = N_CHIPS * bufs["w_ffn_down"].shape[1]
    nb_in, nb_co, nb_ff = bufs["w_in"].shape[2], bufs["w_conv_out"].shape[2], bufs["w_ffn_gate"].shape[2]
    rows = lambda g: g.reshape(-1, g.shape[2])
    w_in, = _gather_weights([bufs["w_in"]])
    sh_m, sc_m, gt_m, sh_f, sc_f, gt_f = [mod[:, i * D:(i + 1) * D] for i in range(6)]
    tm = _pick(S, 512, 16)
    ts = _pick(S, 1024, 128)
    tn_in = _pick(nb_in, 1408, 128)
    tn_co = _pick(nb_co, 512, 128)
    tn_ff = _pick(nb_ff, 1408, 128)
    tn_d = _pick(D, 512, 128)
    tw = _pick(D, 1024, 128)
    ts2 = _pick(S, 2048, 128)

    h = _norm_mod("norm_mix", x, norm_mix_g, sc_m, sh_m)
    proj, *got = _matmul("proj", [(h, 'n', w_in, 'n')], [0], S, d_in, D, tm, tn_in, D, [], [(F32, None)], _first,
                         job=_GatherJob([bufs[k] for k in ("w_conv_out", "w_hgrn_out", "w_o")]))
    w_conv_out, w_hgrn_out, w_o = _forward_halves("forward_branch", got)
    w_o = rows(w_o)
    ob, o_raw, states, *got = _hgrn_fwd(proj, lb, gnorm_g, d_conv,
                                        job=_GatherJob([bufs[k] for k in ("w_ffn_gate", "w_ffn_up", "w_ffn_down")]))
    ya = _conv_fwd(proj, conv_w, d_conv)
    w_ffn_gate, w_ffn_up, w_ffn_down = _forward_halves("forward_ffn", got)
    w_ffn_down = rows(w_ffn_down)

    def merge_epi(accs, ex):
        y_a, y_b = accs
        return [y_a, y_b, _sig(ex[0]) * y_a + _sig(ex[1]) * y_b]

    y_a, y_b, merged = _matmul(
        "branch_out", [(ya, 'n', w_conv_out, 'n'), (ob, 'n', w_hgrn_out, 'n')], [0, 1], S, D, d_conv, tm, tn_co, d_conv,
        [(proj, 'tile', 7 * d_conv), (proj, 'tile', 7 * d_conv + D)], [(F32, None), (F32, None), (BF16, None)], merge_epi)

    def resid_epi(accs, ex):
        return [accs[0], ex[0] + ex[1] * accs[0]]

    mo, x1 = _matmul("w_o", [(merged, 'n', w_o, 'n')], [0], S, D, D, tm, tw, D,
                     [(x, 'tile', 0), (gt_m, 'row', 0)], [(F32, None), (F32, None)], resid_epi)
    h2 = _norm_mod("norm_ffn", x1, norm_ffn_g, sc_f, sh_f)

    def swiglu_epi(accs, ex):
        gg, uu = accs
        return [gg, uu, gg * _sig(gg) * uu]

    G, U, act = _matmul("ffn_in", [(h2, 'n', w_ffn_gate, 'n'), (h2, 'n', w_ffn_up, 'n')], [0, 1], S, d_ff, D,
                        tm, tn_ff, D, [], [(F32, None), (F32, None), (BF16, None)], swiglu_epi)
    ff, x2 = _matmul("ffn_out", [(act, 'n', w_ffn_down, 'n')], [0], S, D, d_ff, tm, D, _pick(d_ff, 1408, 128),
                     [(x1, 'tile', 0), (gt_f, 'row', 0)], [(F32, None), (F32, None)], resid_epi)

    dx2, dffb, loss, dgt_f, dg_final = _loss_head(x2, target, norm_final_g, ff, gt_f)

    def swiglu_bwd_epi(accs, ex):
        dact, gg, uu = accs[0], ex[0], ex[1]
        s = _sig(gg)
        return [dact * uu * (s * (1.0 + gg * (1.0 - s))), dact * (gg * s)]

    dG, dU = _matmul("d_act", [(dffb, 'n', w_ffn_down, 't')], [0], S, d_ff, D, tm, tn_ff, D,
                     [(G, 'tile', 0), (U, 'tile', 0)], [(BF16, None), (BF16, None)], swiglu_bwd_epi)
    dw_down, = _matmul("dw_ffn_down", [(act, 't', dffb, 'n')], [0], d_ff, D, S, _pick(d_ff, 1408, 128),
                       D, ts, [], [(BF16, None)], _first)
    dh2, = _matmul("d_h2", [(dG, 'n', w_ffn_gate, 't'), (dU, 'n', w_ffn_up, 't')], [0, 0], S, D, d_ff,
                   tm, D, tn_ff, [], [(F32, None)], _first)
    dw_gate, dw_up = _matmul("dw_ffn_in", [(h2, 't', dG, 'n'), (h2, 't', dU, 'n')], [0, 1], D, d_ff, S,
                             tw, tn_ff, ts, [], [(BF16, nb_ff), (BF16, nb_ff)],
                             lambda accs, ex: accs)

    def pre_sum(tag, names, grads):
        from_sibling = _swap_halves("swap_" + tag, grads)
        pairs = [_pair_sum("pair_sum_" + k, g, q, c_idx) for k, g, q in zip(names, grads, from_sibling)]
        return [p[0] for p in pairs], [p[1] for p in pairs]

    ffn_names = ["w_ffn_down", "w_ffn_gate", "w_ffn_up"]
    parts_f, slots_f = pre_sum("ffn", ffn_names, [dw_down.reshape(N_CHIPS, -1, D), dw_gate, dw_up])
    dx1, dsh_f, dsc_f, dg_ffn, dmob, dgt_m = _norm_mod_bwd("norm_ffn_bwd", dh2, x1, norm_ffn_g, sc_f, dx2, (mo, gt_m))

    def merge_bwd_epi(accs, ex):
        dm, ga, gb, ya_, yb_ = accs[0], ex[0], ex[1], ex[2], ex[3]
        sa, sb = _sig(ga), _sig(gb)
        return [dm * ya_ * sa * (1.0 - sa), dm * yb_ * sb * (1.0 - sb), dm * sa, dm * sb]

    dga, dgb, dy_a, dy_b = _matmul(
        "d_merged", [(dmob, 'n', w_o, 't')], [0], S, D, D, tm, tn_co, D,
        [(proj, 'tile', 7 * d_conv), (proj, 'tile', 7 * d_conv + D), (y_a, 'tile', 0), (y_b, 'tile', 0)],
        [(BF16, None)] * 4, merge_bwd_epi)
    dw_o, = _matmul("dw_o", [(merged, 't', dmob, 'n')], [0], D, D, S, _pick(D, 512, 128), _pick(D, 1024, 128), ts,
                    [], [(BF16, None)], _first)
    tn_c = _pick(d_conv, 512, 128)
    dya, = _matmul("d_ya", [(dy_a, 'n', w_conv_out, 't')], [0], S, d_conv, D, tm, tn_c, tn_co, [], [(F32, None)], _first)
    dob, = _matmul("d_ob", [(dy_b, 'n', w_hgrn_out, 't')], [0], S, d_conv, D, tm, tn_c, tn_co, [], [(F32, None)], _first)
    dw_co, = _matmul("dw_conv_out", [(ya, 't', dy_a, 'n')], [0], d_conv, D, S, _pick(d_conv, 512, 128), tn_co, ts,
                     [], [(BF16, nb_co)], _first)
    dw_ho, = _matmul("dw_hgrn_out", [(ob, 't', dy_b, 'n')], [0], d_conv, D, S, _pick(d_conv, 512, 128), tn_co, ts,
                     [], [(BF16, nb_co)], _first)
    branch_names = ["w_conv_out", "w_hgrn_out", "w_o"]
    parts_b, slots_b = pre_sum("branch", branch_names, [dw_co, dw_ho, dw_o.reshape(N_CHIPS, -1, D)])
    dab, dac, dax, dconv_w = _conv_bwd(dya, proj, conv_w, d_conv)
    dq, dz, dv, dgo, dlb, dgn, *arrived = _hgrn_bwd(dob, o_raw, states, proj, lb, gnorm_g, d_conv,
                                                    job=_ScatterJob(parts_f + parts_b, slots_f + slots_b))
    dproj = jnp.concatenate([dab, dac, dax, dq, dz, dv, dgo, dga, dgb], axis=1)
    dw_in, = _matmul("dw_in", [(h, 't', dproj, 'n')], [0], D, d_in, S, tw, tn_in, ts2,
                     [], [(BF16, nb_in)], _first)
    parts_i, slots_i = pre_sum("in", ["w_in"], [dw_in])
    dh, arrived_in = _matmul("d_h", [(dproj, 'n', w_in, 't')], [0], S, D, d_in, tm, D, tn_in, [], [(F32, None)], _first,
                             job=_ScatterJob(parts_i, slots_i))
    dx, dsh_m, dsc_m, dg_mix = _norm_mod_bwd("norm_mix_bwd", dh, x, norm_mix_g, sc_m, dx1)
    dmod = jnp.concatenate([dsh_m, dsc_m, dgt_m, dsh_f, dsc_f, dgt_f], axis=1)
    small = dict(dmod=dmod, dg_mix=dg_mix, dg_ffn=dg_ffn, dg_final=dg_final, dlb=dlb,
                 dgn=jnp.sum(dgn, axis=0), dconv_w=dconv_w)
    big = dict(zip(ffn_names + branch_names, arrived), w_in=arrived_in)
    return loss, dx, small, big


def _adamw_math(w, g, m, v):
    m = ADAM_B1 * m + (1.0 - ADAM_B1) * g
    v = ADAM_B2 * v + (1.0 - ADAM_B2) * (g * g)
    m_hat = m / (1.0 - ADAM_B1 ** ADAM_STEP)
    v_hat = v / (1.0 - ADAM_B2 ** ADAM_STEP)
    delta = -ADAM_LR * (m_hat / (jnp.sqrt(v_hat) + ADAM_EPS) + ADAM_WD * w)
    return delta, m, v


def _adamw(name, w, g, m, v):
    R, C = w.shape
    tr = _pick(R, max(8, (256 * 1024) // C // 8 * 8), 8)
    spec = pl.BlockSpec((tr, C), lambda i: (i, 0))

    def body(w_ref, g_ref, m_ref, v_ref, d_ref, mo_ref, vo_ref):
        d, m2, v2 = _adamw_math(w_ref[...], g_ref[...], m_ref[...], v_ref[...])
        d_ref[...] = d
        mo_ref[...] = m2
        vo_ref[...] = v2

    return pl.pallas_call(
        body, name=name, grid=(R // tr,), in_specs=[spec] * 4, out_specs=[spec] * 3,
        out_shape=[jax.ShapeDtypeStruct((R, C), F32)] * 3, compiler_params=_params(("parallel",)),
    )(w, g, m, v)


def _ada_update(c_act_t, dmod, w, m, v):
    R, C = w.shape
    B = dmod.shape[0]
    tr = _pick(R, 256, 8)
    tc = _pick(C, 1536, 128)
    spec = pl.BlockSpec((tr, tc), lambda i, j: (i, j))

    def body(ct_ref, dm_ref, w_ref, m_ref, v_ref, g_ref, d_ref, mo_ref, vo_ref):
        ct = ct_ref[...]
        dm = dm_ref[...]
        g = ct[:, 0:1] * dm[0:1, :]
        for b in range(1, B):
            g = g + ct[:, b:b + 1] * dm[b:b + 1, :]
        d, m2, v2 = _adamw_math(w_ref[...], g, m_ref[...], v_ref[...])
        g_ref[...] = g
        d_ref[...] = d
        mo_ref[...] = m2
        vo_ref[...] = v2

    return pl.pallas_call(
        body, name="ada_update", grid=(R // tr, C // tc),
        in_specs=[pl.BlockSpec((tr, B), lambda i, j: (i, 0)), pl.BlockSpec((B, tc), lambda i, j: (0, j)),
                  spec, spec, spec],
        out_specs=[spec] * 4, out_shape=[jax.ShapeDtypeStruct((R, C), F32)] * 4,
        compiler_params=_params(("parallel", "parallel")),
    )(c_act_t, dmod, w, m, v)


def _prep(c_all, lb_param):
    def body(c_ref, p_ref, ca_ref, cab_ref, lb_ref):
        cv = c_ref[...]
        ca = cv * _sig(cv)
        ca_ref[...] = ca
        cab_ref[...] = ca.astype(BF16)
        lb_ref[...] = _sig(p_ref[0:1, :] - p_ref[1:2, :])

    return pl.pallas_call(
        body, name="prep",
        out_shape=[jax.ShapeDtypeStruct(c_all.shape, F32), jax.ShapeDtypeStruct(c_all.shape, BF16),
                   jax.ShapeDtypeStruct((1, lb_param.shape[1]), F32)],
    )(c_all, lb_param)


def _small_reduce(parts):
    W = parts.shape[1]

    def body(p_ref, o_ref):
        acc = p_ref[0:1, :]
        for d in range(1, N_DEV):
            acc = acc + p_ref[d:d + 1, :]
        o_ref[...] = acc

    return pl.pallas_call(body, name="small_reduce", out_shape=jax.ShapeDtypeStruct((1, W), F32))(parts)


def _lb_grad(dlb, lb):
    def body(d_ref, lb_ref, o_ref):
        t = d_ref[...] * lb_ref[...] * (1.0 - lb_ref[...])
        o_ref[0:1, :] = t
        o_ref[1:2, :] = -t

    return pl.pallas_call(body, name="lb_grad", out_shape=jax.ShapeDtypeStruct((2, dlb.shape[1]), F32))(dlb, lb)


def _place():
    x, y, c = lax.axis_index("x"), lax.axis_index("y"), lax.axis_index("c")
    chips = [(1 - x, y), (x, 1 - y), (1 - x, 1 - y)]
    return x, y, c, chips


def _allgather_rows(name, v):
    m_per, n = v.shape

    def body(x_ref, out_ref, send_sems, recv_sems, local_sem):
        x, y, c, chips = _place()
        me, sibling = (x, y, c), (x, y, 1 - c)

        def rows(px, py, pc):
            return out_ref.at[pl.ds((4 * px + 2 * py + pc) * m_per, m_per), :]

        def copy(k, block, to, src=None):
            return pltpu.make_async_remote_copy(
                src_ref=rows(*block) if src is None else src, dst_ref=rows(*block),
                send_sem=send_sems.at[k], recv_sem=recv_sems.at[k], device_id=to, device_id_type=MESH)

        mine = pltpu.make_async_copy(x_ref, rows(*me), local_sem)
        mine.start()
        first = [copy(0, me, sibling, src=x_ref)]
        first += [copy(1 + j, me, (*chip, c), src=x_ref) for j, chip in enumerate(chips)]
        for cp in first:
            cp.start()
        passed = [copy(4 + j, (*chip, c), sibling) for j, chip in enumerate(chips)]
        for j, chip in enumerate(chips):
            copy(1 + j, (*chip, c), me).wait_recv()
            passed[j].start()
        copy(0, sibling, me).wait_recv()
        for j, chip in enumerate(chips):
            copy(4 + j, (*chip, 1 - c), me).wait_recv()
        for cp in first + passed:
            cp.wait_send()
        mine.wait()

    return pl.pallas_call(
        body, name=name, out_shape=jax.ShapeDtypeStruct((N_DEV * m_per, n), v.dtype),
        in_specs=[pl.BlockSpec(memory_space=pltpu.VMEM)], out_specs=pl.BlockSpec(memory_space=pltpu.VMEM),
        scratch_shapes=[pltpu.SemaphoreType.DMA((7,)), pltpu.SemaphoreType.DMA((7,)), pltpu.SemaphoreType.DMA],
    )(v)


def _cast_place(name, w, k_idx):
    R, C = w.shape
    tr = _pick(R, max(16, (512 * 1024) // C // 16 * 16), 16)

    def body(k_ref, w_ref, o_ref):
        o_ref[...] = w_ref[...].astype(BF16)

    return pl.pallas_call(
        body, name=name,
        grid_spec=pltpu.PrefetchScalarGridSpec(
            num_scalar_prefetch=1, grid=(R // tr,),
            in_specs=[pl.BlockSpec((tr, C), lambda i, k_ref: (i, 0))],
            out_specs=pl.BlockSpec((None, tr, C), lambda i, k_ref: (k_ref[0], i, 0))),
        out_shape=jax.ShapeDtypeStruct((N_CHIPS, R, C), BF16),
        compiler_params=_params(("parallel",)),
    )(k_idx, w)


def _gather_weights(bufs):
    n = len(bufs)

    def body(*refs):
        out_refs = refs[n:2 * n]
        send_sems, recv_sems = refs[2 * n:]
        x, y, c, chips = _place()
        k_me = 2 * x + y
        sibling = (x, y, 1 - c)

        def region(a, k, hc):
            rh = bufs[a].shape[1] // 2
            return out_refs[a].at[k, pl.ds(hc * rh, rh)]

        def copy(a, s, src, dst, to):
            return pltpu.make_async_remote_copy(src_ref=src, dst_ref=dst, send_sem=send_sems.at[6 * a + s],
                                                recv_sem=recv_sems.at[6 * a + s], device_id=to, device_id_type=MESH)

        started = []
        for a in range(n):
            mine = region(a, k_me, c)
            for j, chip in enumerate(chips):
                cp = copy(a, j, mine, mine, (*chip, c))
                cp.start()
                started.append(cp)
        for a in range(n):
            for j, chip in enumerate(chips):
                got = region(a, 2 * chip[0] + chip[1], c)
                copy(a, j, got, got, (x, y, c)).wait_recv()
                cp = copy(a, 3 + j, got, got, sibling)
                cp.start()
                started.append(cp)
        for a in range(n):
            for j, chip in enumerate(chips):
                got = region(a, 2 * chip[0] + chip[1], 1 - c)
                copy(a, 3 + j, got, got, (x, y, c)).wait_recv()
        for cp in started:
            cp.wait_send()

    return pl.pallas_call(
        body, name="gather_weights",
        out_shape=[jax.ShapeDtypeStruct(b.shape, b.dtype) for b in bufs],
        in_specs=[_HBM] * n, out_specs=[_HBM] * n, input_output_aliases={a: a for a in range(n)},
        scratch_shapes=[pltpu.SemaphoreType.DMA((6 * n,)), pltpu.SemaphoreType.DMA((6 * n,))],
    )(*bufs)


def _chip_copies(src_of, dst_of, got_of, n, sems, receiving):
    x, y, c, chips = _place()
    k_me = 2 * x + y
    send_sems, recv_sems = sems
    out = []
    for a in range(n):
        for j, chip in enumerate(chips):
            k_chip = 2 * chip[0] + chip[1]
            if receiving:
                src = dst = got_of(a, k_chip, c)
                to = (x, y, c)
            else:
                src, dst, to = src_of(a, k_chip, k_me, c), dst_of(a, k_me, c), (*chip, c)
            out.append(pltpu.make_async_remote_copy(
                src_ref=src, dst_ref=dst, send_sem=send_sems.at[3 * a + j], recv_sem=recv_sems.at[3 * a + j],
                device_id=to, device_id_type=MESH))
    return out


class _ChipJob:
    def start(self, j_in, j_out, sems):
        for send in self.copies(j_in, j_out, sems, False):
            send.start()

    def finish(self, j_in, j_out, sems):
        for recv in self.copies(j_in, j_out, sems, True):
            recv.wait_recv()
        for send in self.copies(j_in, j_out, sems, False):
            send.wait_send()


class _GatherJob(_ChipJob):
    def __init__(self, bufs):
        n = len(bufs)
        self.inputs, self.n_alias = list(bufs), n
        self.sem_shapes = [pltpu.SemaphoreType.DMA((3 * n,)), pltpu.SemaphoreType.DMA((3 * n,))]
        self.half = [b.shape[1] // 2 for b in bufs]

    def copies(self, j_in, j_out, sems, receiving):
        def half(a, k, c):
            return j_out[a].at[k, pl.ds(c * self.half[a], self.half[a])]

        return _chip_copies(lambda a, k_chip, k_me, c: half(a, k_me, c), half, half, len(self.half), sems, receiving)


class _ScatterJob(_ChipJob):
    def __init__(self, parts, slots):
        n = len(parts)
        self.n = n
        self.inputs, self.n_alias = list(parts) + list(slots), n
        self.sem_shapes = [pltpu.SemaphoreType.DMA((3 * n,)), pltpu.SemaphoreType.DMA((3 * n,))]

    def copies(self, j_in, j_out, sems, receiving):
        return _chip_copies(lambda a, k_chip, k_me, c: j_in[a].at[k_chip], lambda a, k_me, c: j_out[a].at[k_me],
                            lambda a, k_chip, c: j_out[a].at[k_chip], self.n, sems, receiving)


def _forward_halves(name, bufs):
    n = len(bufs)

    def body(*refs):
        out_refs = refs[n:2 * n]
        send_sems, recv_sems = refs[2 * n:]
        x, y, c, chips = _place()
        started, waits = [], []
        for a in range(n):
            rh = bufs[a].shape[1] // 2
            for j, chip in enumerate(chips):
                k_chip = 2 * chip[0] + chip[1]
                got = out_refs[a].at[k_chip, pl.ds(c * rh, rh)]
                cp = pltpu.make_async_remote_copy(src_ref=got, dst_ref=got, send_sem=send_sems.at[3 * a + j],
                                                  recv_sem=recv_sems.at[3 * a + j], device_id=(x, y, 1 - c),
                                                  device_id_type=MESH)
                cp.start()
                started.append(cp)
                other = out_refs[a].at[k_chip, pl.ds((1 - c) * rh, rh)]
                waits.append(pltpu.make_async_remote_copy(
                    src_ref=other, dst_ref=other, send_sem=send_sems.at[3 * a + j], recv_sem=recv_sems.at[3 * a + j],
                    device_id=(x, y, c), device_id_type=MESH))
        for cp in waits:
            cp.wait_recv()
        for cp in started:
            cp.wait_send()

    return pl.pallas_call(
        body, name=name, out_shape=[jax.ShapeDtypeStruct(b.shape, b.dtype) for b in bufs],
        in_specs=[_HBM] * n, out_specs=[_HBM] * n, input_output_aliases={a: a for a in range(n)},
        scratch_shapes=[pltpu.SemaphoreType.DMA((3 * n,)), pltpu.SemaphoreType.DMA((3 * n,))],
    )(*bufs)


def _swap_halves(name, grads):
    n = len(grads)

    def body(*refs):
        in_refs, out_refs = refs[:n], refs[n:2 * n]
        send_sems, recv_sems = refs[2 * n:]
        x, y, c, _ = _place()
        cps = []
        for a in range(n):
            rh = grads[a].shape[1] // 2
            cp = pltpu.make_async_remote_copy(
                src_ref=in_refs[a].at[:, pl.ds((1 - c) * rh, rh)], dst_ref=out_refs[a],
                send_sem=send_sems.at[a], recv_sem=recv_sems.at[a], device_id=(x, y, 1 - c), device_id_type=MESH)
            cp.start()
            cps.append(cp)
        for cp in cps:
            cp.wait()

    return pl.pallas_call(
        body, name=name,
        out_shape=[jax.ShapeDtypeStruct((N_CHIPS, g.shape[1] // 2, g.shape[2]), g.dtype) for g in grads],
        in_specs=[_HBM] * n, out_specs=[_HBM] * n,
        scratch_shapes=[pltpu.SemaphoreType.DMA((n,)), pltpu.SemaphoreType.DMA((n,))],
    )(*grads)


def _pair_sum(name, g, q, c_idx):
    _, R, C = g.shape
    rh = R // 2
    tr = _pick(rh, max(16, (512 * 1024) // C // 16 * 16), 16)
    nh = rh // tr

    def body(c_ref, g_ref, q_ref, o_ref, o2_ref):
        s = (g_ref[...].astype(F32) + q_ref[...].astype(F32)).astype(BF16)
        o_ref[...] = s
        o2_ref[...] = s

    out_spec = pl.BlockSpec((None, tr, C), lambda k, i, c_ref: (k, i, 0))
    return pl.pallas_call(
        body, name=name,
        grid_spec=pltpu.PrefetchScalarGridSpec(
            num_scalar_prefetch=1, grid=(N_CHIPS, nh),
            in_specs=[pl.BlockSpec((None, tr, C), lambda k, i, c_ref: (k, c_ref[0] * nh + i, 0)),
                      pl.BlockSpec((None, tr, C), lambda k, i, c_ref: (k, i, 0))],
            out_specs=[out_spec, out_spec]),
        out_shape=[jax.ShapeDtypeStruct((N_CHIPS, rh, C), BF16)] * 2,
        compiler_params=_params(("parallel", "parallel")),
    )(c_idx, g, q)


def _sum_chips(name, r, c_idx):
    _, rh, C = r.shape
    tr = _pick(rh, max(16, (256 * 1024) // C // 16 * 16), 16)

    def body(c_ref, r_ref, o_ref):
        acc = r_ref[0].astype(F32)
        for k in range(1, N_CHIPS):
            acc = acc + r_ref[k].astype(F32)
        o_ref[...] = acc

    return pl.pallas_call(
        body, name=name,
        grid_spec=pltpu.PrefetchScalarGridSpec(
            num_scalar_prefetch=1, grid=(rh // tr,),
            in_specs=[pl.BlockSpec((N_CHIPS, tr, C), lambda i, c_ref: (0, i, 0))],
            out_specs=pl.BlockSpec((None, tr, C), lambda i, c_ref: (c_ref[0], i, 0))),
        out_shape=jax.ShapeDtypeStruct((2, rh, C), F32), compiler_params=_params(("parallel",)),
    )(c_idx, r)


def _share_halves(bufs):
    n = len(bufs)

    def body(*refs):
        out_refs = refs[n:2 * n]
        send_sems, recv_sems = refs[2 * n:]
        x, y, c, _ = _place()
        cps = []
        for a in range(n):
            cp = pltpu.make_async_remote_copy(
                src_ref=out_refs[a].at[c], dst_ref=out_refs[a].at[c], send_sem=send_sems.at[a],
                recv_sem=recv_sems.at[a], device_id=(x, y, 1 - c), device_id_type=MESH)
            cp.start()
            cps.append(cp)
        for a, cp in enumerate(cps):
            got = out_refs[a].at[1 - c]
            pltpu.make_async_remote_copy(src_ref=got, dst_ref=got, send_sem=send_sems.at[a], recv_sem=recv_sems.at[a],
                                         device_id=(x, y, c), device_id_type=MESH).wait_recv()
        for cp in cps:
            cp.wait_send()

    return pl.pallas_call(
        body, name="share_halves",
        out_shape=[jax.ShapeDtypeStruct(b.shape, b.dtype) for b in bufs],
        in_specs=[_HBM] * n, out_specs=[_HBM] * n, input_output_aliases={a: a for a in range(n)},
        scratch_shapes=[pltpu.SemaphoreType.DMA((n,)), pltpu.SemaphoreType.DMA((n,))],
    )(*bufs)


_BIG = ("w_in", "w_conv_out", "w_hgrn_out", "w_o", "w_ffn_gate", "w_ffn_up", "w_ffn_down")
_ROW_SHARDED = ("w_o", "w_ffn_down")
_WEIGHTS = ("w_ada", "b_ada", "norm_mix_g", "w_in", "conv_w", "lb_param", "gnorm_g", "w_conv_out", "w_hgrn_out",
            "w_o", "norm_ffn_g", "w_ffn_gate", "w_ffn_up", "w_ffn_down", "norm_final_g")


def _pack_rows(pieces):
    flat = jnp.concatenate([p.reshape(-1) for p in pieces])
    pad = (-flat.shape[0]) % 1024
    return jnp.pad(flat, (0, pad)).reshape(8, -1)


def kernel(x, c, w_ada, b_ada, norm_mix_g, w_in, conv_w, lb_param, gnorm_g, w_conv_out, w_hgrn_out, w_o, norm_ffn_g, w_ffn_gate, w_ffn_up, w_ffn_down, norm_final_g, loss_target, m_w_ada, m_b_ada, m_norm_mix_g, m_w_in, m_conv_w, m_lb_param, m_gnorm_g, m_w_conv_out, m_w_hgrn_out, m_w_o, m_norm_ffn_g, m_w_ffn_gate, m_w_ffn_up, m_w_ffn_down, m_norm_final_g, v_w_ada, v_b_ada, v_norm_mix_g, v_w_in, v_conv_w, v_lb_param, v_gnorm_g, v_w_conv_out, v_w_hgrn_out, v_w_o, v_norm_ffn_g, v_w_ffn_gate, v_w_ffn_up, v_w_ffn_down, v_norm_final_g):
    w = dict(w_ada=w_ada, b_ada=b_ada, norm_mix_g=norm_mix_g, w_in=w_in, conv_w=conv_w, lb_param=lb_param,
             gnorm_g=gnorm_g, w_conv_out=w_conv_out, w_hgrn_out=w_hgrn_out, w_o=w_o, norm_ffn_g=norm_ffn_g,
             w_ffn_gate=w_ffn_gate, w_ffn_up=w_ffn_up, w_ffn_down=w_ffn_down, norm_final_g=norm_final_g)
    m = dict(w_ada=m_w_ada, b_ada=m_b_ada, norm_mix_g=m_norm_mix_g, w_in=m_w_in, conv_w=m_conv_w, lb_param=m_lb_param,
             gnorm_g=m_gnorm_g, w_conv_out=m_w_conv_out, w_hgrn_out=m_w_hgrn_out, w_o=m_w_o, norm_ffn_g=m_norm_ffn_g,
             w_ffn_gate=m_w_ffn_gate, w_ffn_up=m_w_ffn_up, w_ffn_down=m_w_ffn_down, norm_final_g=m_norm_final_g)
    v = dict(w_ada=v_w_ada, b_ada=v_b_ada, norm_mix_g=v_norm_mix_g, w_in=v_w_in, conv_w=v_conv_w, lb_param=v_lb_param,
             gnorm_g=v_gnorm_g, w_conv_out=v_w_conv_out, w_hgrn_out=v_w_hgrn_out, w_o=v_w_o, norm_ffn_g=v_norm_ffn_g,
             w_ffn_gate=v_w_ffn_gate, w_ffn_up=v_w_ffn_up, w_ffn_down=v_w_ffn_down, norm_final_g=v_norm_final_g)
    two_d = lambda a: a.reshape((-1, a.shape[-1])) if a.ndim != 2 else a
    shapes = {k: a.shape for k, a in w.items()}
    w, m, v = ({k: two_d(a) for k, a in t.items()} for t in (w, m, v))
    w["lb_param"], m["lb_param"], v["lb_param"] = lb_param, m_lb_param, v_lb_param
    S, D = x.shape[1], x.shape[2]
    d_conv = D // 2
    ix, iy, ic = lax.axis_index("x"), lax.axis_index("y"), lax.axis_index("c")
    k_me = 2 * ix + iy
    dev = 2 * k_me + ic
    cw_shard = w["conv_w"].shape[1]
    ada_cols = w["w_ada"].shape[1]

    got = _allgather_rows("gather_cond", _pack_rows([c, w["conv_w"]])).reshape(N_DEV, -1)
    c_all = got[:, :D]
    conv_full = got[::2, D:D + 3 * cw_shard].reshape(N_CHIPS, 3, cw_shard).transpose(1, 0, 2).reshape(3, -1)
    c_act, c_act_b, lb = _prep(c_all, lb_param)
    b_cols = lax.dynamic_slice(w["b_ada"], (0, k_me * ada_cols), (1, ada_cols))
    mod_cols, = _matmul("ada_fwd", [(c_act_b, 'n', w["w_ada"].astype(BF16), 'n')], [0], N_DEV, ada_cols, D,
                        N_DEV, _pick(ada_cols, 1536, 128), D, [(b_cols, 'row', 0)], [(F32, None)],
                        lambda accs, ex: [accs[0] + ex[0]])
    mod_all = _allgather_rows("gather_mod", mod_cols).reshape(N_CHIPS, 2, N_DEV, ada_cols)[:, 0]
    mod_all = mod_all.transpose(1, 0, 2).reshape(N_DEV, -1)
    mod = lax.dynamic_slice(mod_all, (dev, 0), (1, mod_all.shape[1]))
    k_idx = jnp.reshape(k_me, (1,)).astype(jnp.int32)
    c_idx = jnp.reshape(ic, (1,)).astype(jnp.int32)
    bufs = {k: _cast_place("cast_" + k, w[k], k_idx) for k in _BIG}

    loss, dx, small, arrived = _local_step(
        x[0], loss_target[0], mod, w["norm_mix_g"], lb, w["gnorm_g"], w["norm_ffn_g"], w["norm_final_g"], conv_full,
        bufs, c_idx)

    pieces = [small["dmod"], small["dg_mix"], small["dg_ffn"], small["dg_final"], small["dlb"], small["dgn"],
              small["dconv_w"], loss]
    sizes = [p.size for p in pieces]
    parts = _allgather_rows("gather_small", _pack_rows(pieces)).reshape(N_DEV, -1)
    total = _small_reduce(parts)
    offs = [0]
    for s in sizes:
        offs.append(offs[-1] + s)
    tot = [total[:, offs[i]:offs[i + 1]] for i in range(len(sizes))]
    dmod_all = parts[:, :sizes[0]]
    grads = {
        "b_ada": tot[0], "norm_mix_g": tot[1], "norm_ffn_g": tot[2], "norm_final_g": tot[3],
        "lb_param": _lb_grad(tot[4], lb), "gnorm_g": tot[5],
        "conv_w": lax.dynamic_slice(tot[6].reshape(3, -1), (0, k_me * cw_shard), (3, cw_shard)),
    }
    loss_out = tot[7][0, 0]

    halves = [_sum_chips("sum_chips_" + k, arrived[k], c_idx) for k in _BIG]
    whole = _share_halves(halves)
    for k, g in zip(_BIG, whole):
        grads[k] = g.reshape(-1, g.shape[-1])

    delta, new_m, new_v = {}, {}, {}
    dmod_cols = lax.dynamic_slice(dmod_all, (0, k_me * ada_cols), (N_DEV, ada_cols))
    grads["w_ada"], delta["w_ada"], new_m["w_ada"], new_v["w_ada"] = _ada_update(
        c_act.T, dmod_cols, w["w_ada"], m["w_ada"], v["w_ada"])
    for k in _WEIGHTS:
        if k != "w_ada":
            delta[k], new_m[k], new_v[k] = _adamw("adamw_" + k, w[k], grads[k], m[k], v[k])
    outs = [loss_out, dx.reshape(x.shape)]
    for t in (grads, delta, new_m, new_v):
        outs += [t[k].reshape(shapes[k]) for k in _WEIGHTS]
    return tuple(outs)
```

```python
import functools

import jax
import jax.numpy as jnp
from jax import lax
from jax.experimental import pallas as pl
from jax.experimental.pallas import tpu as pltpu

F32 = jnp.float32
BF16 = jnp.bfloat16
MESH = pl.DeviceIdType.MESH

EPS = 1e-6
HEAD = 128
BLK = 16
N_CHIPS = 4
N_DEV = 8
VMEM_LIMIT_BYTES = 56 * 1024 * 1024

ADAM_LR = 0.001
ADAM_B1 = 0.9
ADAM_B2 = 0.999
ADAM_EPS = 1e-08
ADAM_WD = 0.01
ADAM_STEP = 10


def _pick(dim, pref, mult):
    if dim <= pref:
        return dim
    t = (pref // mult) * mult
    while t >= mult:
        if dim % t == 0:
            return t
        t -= mult
    return dim


def _sig(x):
    return 1.0 / (1.0 + jnp.exp(-x))


def _params(sem):
    return pltpu.CompilerParams(dimension_semantics=sem, vmem_limit_bytes=VMEM_LIMIT_BYTES)


def _gj(j, i, k):
    return j


def _gk(j, i, k):
    return k


def _wspec(arr, rt, ct, r_of, c_of):
    if arr.ndim == 2:
        return pl.BlockSpec((rt, ct), lambda j, i, k: (r_of(j, i, k), c_of(j, i, k)))
    per = arr.shape[2] // ct
    assert arr.shape[2] % ct == 0
    return pl.BlockSpec((None, rt, ct),
                        lambda j, i, k: (c_of(j, i, k) // per, r_of(j, i, k), c_of(j, i, k) % per))


_HBM = pl.BlockSpec(memory_space=pltpu.HBM)


def _host(job, grid, in_specs, args, out_specs, out_shape, scratch):
    if job is None:
        return {}, (lambda body: body)
    n_in, n_out, n_scr = len(args), len(out_shape), len(scratch)
    n_job, n_alias = len(job.inputs), job.n_alias
    in_specs += [_HBM] * n_job
    args += job.inputs
    out_specs += [_HBM] * n_alias
    out_shape += [jax.ShapeDtypeStruct(a.shape, a.dtype) for a in job.inputs[n_job - n_alias:]]
    scratch += job.sem_shapes
    aliases = {n_in + n_job - n_alias + t: n_out + t for t in range(n_alias)}

    def wrap(body):
        def hosted(*refs):
            ins, refs = refs[:n_in], refs[n_in:]
            j_in, refs = refs[:n_job], refs[n_job:]
            outs, refs = refs[:n_out], refs[n_out:]
            j_out, refs = refs[:n_alias], refs[n_alias:]
            scr, sems = refs[:n_scr], refs[n_scr:]
            first = functools.reduce(jnp.logical_and, [pl.program_id(d) == 0 for d in range(len(grid))])
            last = functools.reduce(jnp.logical_and, [pl.program_id(d) == grid[d] - 1 for d in range(len(grid))])

            @pl.when(first)
            def _():
                job.start(j_in, j_out, sems)

            body(*ins, *outs, *scr)

            @pl.when(last)
            def _():
                job.finish(j_in, j_out, sems)

        return hosted

    return aliases, wrap


def _matmul(name, pairs, acc_of, M, N, K, tm, tn, tk, extras, outs, epilogue, job=None):
    assert M % tm == 0 and N % tn == 0 and K % tk == 0, (name, M, N, K, tm, tn, tk)
    nj, ni, nk = N // tn, M // tm, K // tk
    n_pairs, n_extra, n_out = len(pairs), len(extras), len(outs)
    n_acc = max(acc_of) + 1
    in_specs, args, dims = [], [], []
    lhs_of, seen = [], {}
    for a, am, b, bm in pairs:
        if (id(a), am) not in seen:
            seen[id(a), am] = len(args)
            args.append(a)
            if am == 'n':
                in_specs.append(pl.BlockSpec((tm, tk), lambda j, i, k: (i, k)))
            else:
                in_specs.append(pl.BlockSpec((tk, tm), lambda j, i, k: (k, i)))
        lhs_of.append(seen[id(a), am])
    n_lhs = len(args)
    for a, am, b, bm in pairs:
        if bm == 'n':
            in_specs.append(_wspec(b, tk, tn, _gk, _gj))
        else:
            in_specs.append(_wspec(b, tn, tk, _gj, _gk))
        dims.append((((1 if am == 'n' else 0,), (0 if bm == 'n' else 1,)), ((), ())))
        args.append(b)
    for e, kind, off in extras:
        assert off % tn == 0, (name, off, tn)
        o = off // tn
        if kind == 'tile':
            in_specs.append(pl.BlockSpec((tm, tn), lambda j, i, k, o=o: (i, j + o)))
        else:
            in_specs.append(pl.BlockSpec((1, tn), lambda j, i, k, o=o: (0, j + o)))
        args.append(e)
    out_shape, out_specs = [], []
    for dt, nb in outs:
        if nb is None:
            out_shape.append(jax.ShapeDtypeStruct((M, N), dt))
            out_specs.append(pl.BlockSpec((tm, tn), lambda j, i, k: (i, j)))
        else:
            assert nb % tn == 0 and N % nb == 0
            per = nb // tn
            out_shape.append(jax.ShapeDtypeStruct((N // nb, M, nb), dt))
            out_specs.append(pl.BlockSpec((None, tm, tn), lambda j, i, k, per=per: (j // per, i, j % per)))

    def body(*refs):
        n_ab = n_lhs + n_pairs
        e_refs = refs[n_ab:n_ab + n_extra]
        o_refs = refs[n_ab + n_extra:n_ab + n_extra + n_out]
        acc_refs = refs[n_ab + n_extra + n_out:]
        sums = [None] * n_acc
        for p in range(n_pairs):
            prod = lax.dot_general(refs[lhs_of[p]][...], refs[n_lhs + p][...], dims[p], preferred_element_type=F32)
            a = acc_of[p]
            sums[a] = prod if sums[a] is None else sums[a] + prod

        def finish(accs):
            res = epilogue(accs, [e[...] for e in e_refs])
            for o_ref, r in zip(o_refs, res):
                o_ref[...] = r.astype(o_ref.dtype)

        if nk == 1:
            finish(sums)
        else:
            k = pl.program_id(2)

            @pl.when(k == 0)
            def _():
                for a in range(n_acc):
                    acc_refs[a][...] = sums[a]

            @pl.when(k > 0)
            def _():
                for a in range(n_acc):
                    acc_refs[a][...] += sums[a]

            @pl.when(k == nk - 1)
            def _():
                finish([acc_refs[a][...] for a in range(n_acc)])

    scratch = [pltpu.VMEM((tm, tn), F32) for _ in range(n_acc)] if nk > 1 else []
    grid = (nj, ni, nk)
    aliases, wrap = _host(job, grid, in_specs, args, out_specs, out_shape, scratch)
    sem = ("parallel", "parallel", "arbitrary") if job is None else ("arbitrary",) * 3
    return pl.pallas_call(
        wrap(body), name=name, grid=grid, in_specs=in_specs, out_specs=out_specs, out_shape=out_shape,
        scratch_shapes=scratch, input_output_aliases=aliases, compiler_params=_params(sem),
    )(*args)


def _row_spec(tr, d):
    return pl.BlockSpec((tr, d), lambda i: (i, 0))


def _vec_spec(d):
    return pl.BlockSpec((1, d), lambda i: (0, 0))


def _norm_mod(name, x, g, sc, sh):
    S, D = x.shape
    tr = _pick(S, 256, 8)

    def body(x_ref, g_ref, sc_ref, sh_ref, h_ref):
        xv = x_ref[...]
        r = lax.rsqrt(jnp.mean(xv * xv, axis=-1, keepdims=True) + EPS)
        h_ref[...] = ((xv * r) * g_ref[...] * (1.0 + sc_ref[...]) + sh_ref[...]).astype(BF16)

    return pl.pallas_call(
        body, name=name, grid=(S // tr,),
        in_specs=[_row_spec(tr, D), _vec_spec(D), _vec_spec(D), _vec_spec(D)],
        out_specs=_row_spec(tr, D), out_shape=jax.ShapeDtypeStruct((S, D), BF16),
        compiler_params=_params(("parallel",)),
    )(x, g, sc, sh)


def _loss_head(x2, target, g, ff, gt):
    S, D = x2.shape
    tr = _pick(S, 256, 8)

    def body(x_ref, t_ref, g_ref, ff_ref, gt_ref, dx_ref, dffb_ref, loss_ref, dgt_ref, dg_ref):
        i = pl.program_id(0)

        @pl.when(i == 0)
        def _():
            loss_ref[...] = jnp.zeros_like(loss_ref)
            dgt_ref[...] = jnp.zeros_like(dgt_ref)
            dg_ref[...] = jnp.zeros_like(dg_ref)

        xv = x_ref[...]
        gv = g_ref[...]
        r = lax.rsqrt(jnp.mean(xv * xv, axis=-1, keepdims=True) + EPS)
        xh = xv * r
        err = xh * gv - t_ref[...]
        loss_ref[...] += 0.5 * jnp.sum(jnp.mean(err * err, axis=-1, keepdims=True))
        dy = err * (1.0 / D)
        dg_ref[...] += jnp.sum(dy * xh, axis=0, keepdims=True)
        dxh = dy * gv
        dx = r * (dxh - xh * jnp.mean(dxh * xh, axis=-1, keepdims=True))
        dx_ref[...] = dx
        dffb_ref[...] = (dx * gt_ref[...]).astype(BF16)
        dgt_ref[...] += jnp.sum(dx * ff_ref[...], axis=0, keepdims=True)

    return pl.pallas_call(
        body, name="loss_head", grid=(S // tr,),
        in_specs=[_row_spec(tr, D), _row_spec(tr, D), _vec_spec(D), _row_spec(tr, D), _vec_spec(D)],
        out_specs=[_row_spec(tr, D), _row_spec(tr, D), pl.BlockSpec((1, 128), lambda i: (0, 0)),
                   _vec_spec(D), _vec_spec(D)],
        out_shape=[jax.ShapeDtypeStruct((S, D), F32), jax.ShapeDtypeStruct((S, D), BF16),
                   jax.ShapeDtypeStruct((1, 128), F32), jax.ShapeDtypeStruct((1, D), F32),
                   jax.ShapeDtypeStruct((1, D), F32)],
        compiler_params=_params(("arbitrary",)),
    )(x2, target, g, ff, gt)


def _norm_mod_bwd(name, dh, x, g, sc, dres, gated=None):
    S, D = x.shape
    tr = _pick(S, 256, 8)
    n = S // tr
    with_gate = gated is not None

    def body(*refs):
        if with_gate:
            dh_ref, x_ref, g_ref, sc_ref, dres_ref, mo_ref, gt_ref, dx_ref, dsh_ref, dsc_ref, dg_ref, dmob_ref, dgt_ref = refs
        else:
            dh_ref, x_ref, g_ref, sc_ref, dres_ref, dx_ref, dsh_ref, dsc_ref, dg_ref = refs
        i = pl.program_id(0)

        @pl.when(i == 0)
        def _():
            dsh_ref[...] = jnp.zeros_like(dsh_ref)
            dsc_ref[...] = jnp.zeros_like(dsc_ref)
            if with_gate:
                dgt_ref[...] = jnp.zeros_like(dgt_ref)

        xv = x_ref[...]
        dhv = dh_ref[...]
        gv = g_ref[...]
        scale = 1.0 + sc_ref[...]
        r = lax.rsqrt(jnp.mean(xv * xv, axis=-1, keepdims=True) + EPS)
        xh = xv * r
        dsh_ref[...] += jnp.sum(dhv, axis=0, keepdims=True)
        dsc_ref[...] += jnp.sum(dhv * xh, axis=0, keepdims=True)
        dxh = dhv * (gv * scale)
        dx = dres_ref[...] + r * (dxh - xh * jnp.mean(dxh * xh, axis=-1, keepdims=True))
        dx_ref[...] = dx
        if with_gate:
            dmob_ref[...] = (dx * gt_ref[...]).astype(BF16)
            dgt_ref[...] += jnp.sum(dx * mo_ref[...], axis=0, keepdims=True)

        @pl.when(i == n - 1)
        def _():
            t = dsc_ref[...]
            dg_ref[...] = t * scale
            dsc_ref[...] = t * gv

    in_specs = [_row_spec(tr, D), _row_spec(tr, D), _vec_spec(D), _vec_spec(D), _row_spec(tr, D)]
    args = [dh, x, g, sc, dres]
    out_specs = [_row_spec(tr, D), _vec_spec(D), _vec_spec(D), _vec_spec(D)]
    out_shape = [jax.ShapeDtypeStruct((S, D), F32)] + [jax.ShapeDtypeStruct((1, D), F32)] * 3
    if with_gate:
        in_specs += [_row_spec(tr, D), _vec_spec(D)]
        args += list(gated)
        out_specs += [_row_spec(tr, D), _vec_spec(D)]
        out_shape += [jax.ShapeDtypeStruct((S, D), BF16), jax.ShapeDtypeStruct((1, D), F32)]
    return pl.pallas_call(
        body, name=name, grid=(n,), in_specs=in_specs, out_specs=out_specs, out_shape=out_shape,
        compiler_params=_params(("arbitrary",)),
    )(*args)


CONV_ROWS = 256


def _col_spec(S, off_cols):
    o = off_cols // HEAD
    return pl.BlockSpec((S, HEAD), lambda c, o=o: (0, c + o))


def _conv_taps(u, u_prev, rid):
    s1 = jnp.where(rid >= 1, pltpu.roll(u, 1, 0), pltpu.roll(u_prev, 1, 0))
    s2 = jnp.where(rid >= 2, pltpu.roll(u, 2, 0), pltpu.roll(u_prev, 2, 0))
    return s1, s2


def _conv_fwd(proj, conv_w, d_conv):
    S = proj.shape[0]
    R = min(CONV_ROWS, S)
    nr = S // R

    def body(ab_ref, ac_ref, ax_ref, w_ref, ya_ref):
        w0, w1, w2 = w_ref[0:1, :], w_ref[1:2, :], w_ref[2:3, :]
        rid = lax.broadcasted_iota(jnp.int32, (R, HEAD), 0)

        def step(c, carry):
            rows = pl.ds(pl.multiple_of(c * R, R), R)
            prev = pl.ds(pl.multiple_of(jnp.maximum(c - 1, 0) * R, R), R)
            u = ac_ref[rows, :] * ax_ref[rows, :]
            u_prev = jnp.where(c > 0, ac_ref[prev, :] * ax_ref[prev, :], 0.0)
            s1, s2 = _conv_taps(u, u_prev, rid)
            y = w0 * s2 + w1 * s1 + w2 * u
            ya_ref[rows, :] = (ab_ref[rows, :] * y).astype(BF16)
            return carry

        lax.fori_loop(0, nr, step, 0)

    return pl.pallas_call(
        body, name="conv_fwd", grid=(d_conv // HEAD,),
        in_specs=[_col_spec(S, 0), _col_spec(S, d_conv), _col_spec(S, 2 * d_conv),
                  pl.BlockSpec((3, HEAD), lambda c: (0, c))],
        out_specs=pl.BlockSpec((S, HEAD), lambda c: (0, c)),
        out_shape=jax.ShapeDtypeStruct((S, d_conv), BF16),
        compiler_params=_params(("parallel",)),
    )(proj, proj, proj, conv_w)


def _conv_bwd(dya, proj, conv_w, d_conv):
    S = proj.shape[0]
    R = min(CONV_ROWS, S)
    nr = S // R

    def body(dya_ref, ab_ref, ac_ref, ax_ref, w_ref, dab_ref, dac_ref, dax_ref, dw_ref):
        w0, w1, w2 = w_ref[0:1, :], w_ref[1:2, :], w_ref[2:3, :]
        rid = lax.broadcasted_iota(jnp.int32, (R, HEAD), 0)

        def step(c, carry):
            dw0, dw1, dw2 = carry
            rows = pl.ds(pl.multiple_of(c * R, R), R)
            prev = pl.ds(pl.multiple_of(jnp.maximum(c - 1, 0) * R, R), R)
            nxt = pl.ds(pl.multiple_of(jnp.minimum(c + 1, nr - 1) * R, R), R)
            a_c, a_x, a_b = ac_ref[rows, :], ax_ref[rows, :], ab_ref[rows, :]
            u = a_c * a_x
            u_prev = jnp.where(c > 0, ac_ref[prev, :] * ax_ref[prev, :], 0.0)
            s1, s2 = _conv_taps(u, u_prev, rid)
            y = w0 * s2 + w1 * s1 + w2 * u
            dy = dya_ref[rows, :]
            dab_ref[rows, :] = (dy * y).astype(BF16)
            dyc = dy * a_b
            dyc_next = jnp.where(c < nr - 1, dya_ref[nxt, :] * ab_ref[nxt, :], 0.0)
            t1 = jnp.where(rid < R - 1, pltpu.roll(dyc, R - 1, 0), pltpu.roll(dyc_next, R - 1, 0))
            t2 = jnp.where(rid < R - 2, pltpu.roll(dyc, R - 2, 0), pltpu.roll(dyc_next, R - 2, 0))
            du = w2 * dyc + w1 * t1 + w0 * t2
            dac_ref[rows, :] = (du * a_x).astype(BF16)
            dax_ref[rows, :] = (du * a_c).astype(BF16)
            dw0 = dw0 + jnp.sum(dyc * s2, axis=0, keepdims=True)
            dw1 = dw1 + jnp.sum(dyc * s1, axis=0, keepdims=True)
            dw2 = dw2 + jnp.sum(dyc * u, axis=0, keepdims=True)
            return dw0, dw1, dw2

        z = jnp.zeros((1, HEAD), F32)
        dw0, dw1, dw2 = lax.fori_loop(0, nr, step, (z, z, z))
        dw_ref[0:1, :] = dw0
        dw_ref[1:2, :] = dw1
        dw_ref[2:3, :] = dw2

    col = pl.BlockSpec((S, HEAD), lambda c: (0, c))
    return pl.pallas_call(
        body, name="conv_bwd", grid=(d_conv // HEAD,),
        in_specs=[col, _col_spec(S, 0), _col_spec(S, d_conv), _col_spec(S, 2 * d_conv),
                  pl.BlockSpec((3, HEAD), lambda c: (0, c))],
        out_specs=[col, col, col, pl.BlockSpec((3, HEAD), lambda c: (0, c))],
        out_shape=[jax.ShapeDtypeStruct((S, d_conv), BF16)] * 3 + [jax.ShapeDtypeStruct((3, d_conv), F32)],
        compiler_params=_params(("parallel",)),
    )(dya, proj, proj, proj, conv_w)


HGRN_ROWS = 256
HGRN_FWD_SUB = 64
HGRN_BWD_SUB = 32


def _block_cumsum(x, pos):
    for s in (1, 2, 4, 8):
        x = x + jnp.where(pos >= s, pltpu.roll(x, s, 0), 0.0)
    return x


def _block_rev_cumsum(x, pos, rows):
    for s in (1, 2, 4, 8):
        x = x + jnp.where(pos < BLK - s, pltpu.roll(x, rows - s, 0), 0.0)
    return x


def _block_last(x, pos, rows):
    m = jnp.where(pos == BLK - 1, x, 0.0)
    for s in (1, 2, 4, 8):
        m = m + pltpu.roll(m, rows - s, 0)
    return m


def _hgrn_gates(q, z, lb):
    sgq = _sig(q)
    qs = q * sgq
    sg = _sig(z)
    f = lb + (1.0 - lb) * sg
    kk = (1.0 - lb) * (1.0 - sg)
    return sgq, qs, sg, f, kk


def _hgrn_decays(q, z, lb, pos, rows):
    sgq, qs, sg, f, kk = _hgrn_gates(q, z, lb)
    b = _block_cumsum(jnp.log(f), pos)
    return sgq, qs, sg, f, kk, b, _block_last(b, pos, rows)


def _hgrn_specs(T, d_conv, n_t, rev):
    def t_of(i):
        return (n_t - 1 - i) if rev else i

    def pcol(off):
        o = off // HEAD
        return pl.BlockSpec((T, HEAD), lambda h, i, o=o: (t_of(i), h + o))

    q_spec, z_spec, v_spec, g_spec = pcol(3 * d_conv), pcol(4 * d_conv), pcol(5 * d_conv), pcol(6 * d_conv)
    lb_spec = pl.BlockSpec((1, HEAD), lambda h, i: (0, h))
    gn_spec = pl.BlockSpec((1, HEAD), lambda h, i: (0, 0))
    act_spec = pl.BlockSpec((T, HEAD), lambda h, i: (t_of(i), h))
    st_spec = pl.BlockSpec((None, T // BLK, HEAD, HEAD), lambda h, i: (h, t_of(i), 0, 0))
    return q_spec, z_spec, v_spec, g_spec, lb_spec, gn_spec, act_spec, st_spec


def _hgrn_fwd(proj, lb, gn, d_conv, job=None):
    S = proj.shape[0]
    H = d_conv // HEAD
    T = min(HGRN_ROWS, S)
    n_t, nb = S // T, T // BLK
    sub = min(HGRN_FWD_SUB, T)
    q_spec, z_spec, v_spec, g_spec, lb_spec, gn_spec, act_spec, st_spec = _hgrn_specs(T, d_conv, n_t, False)

    def body(q_ref, z_ref, v_ref, g_ref, lb_ref, gn_ref, ob_ref, o_ref, st_ref, state, qe_s, ke_s, v_s, dec_s, o_s, u_s):
        @pl.when(pl.program_id(1) == 0)
        def _():
            state[...] = jnp.zeros_like(state)

        pos = lax.broadcasted_iota(jnp.int32, (sub, HEAD), 0) % BLK
        lbv = lb_ref[...]

        def vector_pass(s, carry):
            r = pl.ds(pl.multiple_of(s * sub, sub), sub)
            v = v_ref[r, :]
            _, qs, _, _, kk, b, b_tot = _hgrn_decays(q_ref[r, :], z_ref[r, :], lbv, pos, sub)
            qe_s[r, :] = (qs * jnp.exp(b)).astype(BF16)
            ke_s[r, :] = (kk * jnp.exp(b_tot - b)).astype(BF16)
            v_s[r, :] = v.astype(BF16)
            dec_s[r, :] = jnp.exp(b_tot)
            o = jnp.sum(qs * kk, axis=-1, keepdims=True) * v
            for d in range(1, BLK):
                e = jnp.exp(b - pltpu.roll(b, d, 0))
                a = jnp.sum(qs * pltpu.roll(kk, d, 0) * e, axis=-1, keepdims=True)
                o = o + jnp.where(pos >= d, a * pltpu.roll(v, d, 0), 0.0)
            o_s[r, :] = o
            return carry

        lax.fori_loop(0, T // sub, vector_pass, 0)

        for j in range(nb):
            rows = pl.ds(j * BLK, BLK)
            u_s[j] = lax.dot_general(v_s[rows, :], ke_s[rows, :], (((0,), (0,)), ((), ())),
                                     preferred_element_type=F32)
        st = state[...]
        for j in range(nb):
            st_ref[j] = st.astype(BF16)
            st = st * dec_s[pl.ds(j * BLK, 1), :] + u_s[j]
        state[...] = st
        for j in range(nb):
            rows = pl.ds(j * BLK, BLK)
            o_s[rows, :] += lax.dot_general(qe_s[rows, :], st_ref[j], (((1,), (1,)), ((), ())),
                                            preferred_element_type=F32)
        o = o_s[...]
        o_ref[...] = o
        r = lax.rsqrt(jnp.mean(o * o, axis=-1, keepdims=True) + EPS)
        g = g_ref[...]
        ob_ref[...] = ((o * r) * gn_ref[...] * (g * _sig(g))).astype(BF16)

    grid = (H, n_t)
    in_specs = [q_spec, z_spec, v_spec, g_spec, lb_spec, gn_spec]
    args = [proj, proj, proj, proj, lb, gn]
    out_specs = [act_spec, act_spec, st_spec]
    out_shape = [jax.ShapeDtypeStruct((S, d_conv), BF16), jax.ShapeDtypeStruct((S, d_conv), F32),
                 jax.ShapeDtypeStruct((H, S // BLK, HEAD, HEAD), BF16)]
    scratch = [pltpu.VMEM((HEAD, HEAD), F32), pltpu.VMEM((T, HEAD), BF16), pltpu.VMEM((T, HEAD), BF16),
               pltpu.VMEM((T, HEAD), BF16), pltpu.VMEM((T, HEAD), F32), pltpu.VMEM((T, HEAD), F32),
               pltpu.VMEM((nb, HEAD, HEAD), F32)]
    aliases, wrap = _host(job, grid, in_specs, args, out_specs, out_shape, scratch)
    return pl.pallas_call(
        wrap(body), name="hgrn_fwd", grid=grid, in_specs=in_specs, out_specs=out_specs, out_shape=out_shape,
        scratch_shapes=scratch, input_output_aliases=aliases,
        compiler_params=_params(("parallel" if job is None else "arbitrary", "arbitrary")),
    )(*args)


def _hgrn_bwd(dob, o_raw, states, proj, lb, gn, d_conv, job=None):
    S = proj.shape[0]
    H = d_conv // HEAD
    T = min(HGRN_ROWS, S)
    n_t, nb = S // T, T // BLK
    sub = min(HGRN_BWD_SUB, T)
    q_spec, z_spec, v_spec, g_spec, lb_spec, gn_spec, act_spec, st_spec = _hgrn_specs(T, d_conv, n_t, True)

    def body(dob_ref, o_ref, st_ref, q_ref, z_ref, v_ref, g_ref, lb_ref, gn_ref,
             dq_ref, dz_ref, dv_ref, dg_ref, dlb_ref, dgn_ref,
             dstate, do_b, qe_b, ke_b, v_b, dec_s, dqe_s, dke_s, dvi_s, ddec_s, do_s, u_s, ds_s):
        @pl.when(pl.program_id(1) == 0)
        def _():
            dstate[...] = jnp.zeros_like(dstate)
            dlb_ref[...] = jnp.zeros_like(dlb_ref)
            dgn_ref[...] = jnp.zeros_like(dgn_ref)

        pos = lax.broadcasted_iota(jnp.int32, (sub, HEAD), 0) % BLK
        lbv, gnv = lb_ref[...], gn_ref[...]

        def before(s, carry):
            r = pl.ds(pl.multiple_of(s * sub, sub), sub)
            v, g = v_ref[r, :], g_ref[r, :]
            _, qs, _, _, kk, b, b_tot = _hgrn_decays(q_ref[r, :], z_ref[r, :], lbv, pos, sub)
            qe_b[r, :] = (qs * jnp.exp(b)).astype(BF16)
            ke_b[r, :] = (kk * jnp.exp(b_tot - b)).astype(BF16)
            v_b[r, :] = v.astype(BF16)
            dec_s[r, :] = jnp.exp(b_tot)
            o = o_ref[r, :]
            rs = lax.rsqrt(jnp.mean(o * o, axis=-1, keepdims=True) + EPS)
            on = o * rs
            sgg = _sig(g)
            dobv = dob_ref[r, :]
            dog = dobv * (g * sgg)
            dgn_ref[...] += jnp.sum(dog * on, axis=0, keepdims=True)
            don = dog * gnv
            do = rs * (don - on * jnp.mean(don * on, axis=-1, keepdims=True))
            dg_ref[r, :] = (dobv * (on * gnv) * (sgg * (1.0 + g * (1.0 - sgg)))).astype(BF16)
            do_s[r, :] = do
            do_b[r, :] = do.astype(BF16)
            return carry

        lax.fori_loop(0, T // sub, before, 0)

        for j in range(nb):
            rows = pl.ds(j * BLK, BLK)
            dob_j = do_b[rows, :]
            u_s[j] = lax.dot_general(dob_j, qe_b[rows, :], (((0,), (0,)), ((), ())), preferred_element_type=F32)
            dqe_s[rows, :] = lax.dot_general(dob_j, st_ref[j], (((1,), (0,)), ((), ())), preferred_element_type=F32)
        dst = dstate[...]
        for j in reversed(range(nb)):
            ds_s[j] = dst.astype(BF16)
            ddec = jnp.sum(st_ref[j].astype(F32) * dst, axis=0, keepdims=True)
            ddec_s[pl.ds(j * BLK, BLK), :] = jnp.broadcast_to(ddec, (BLK, HEAD))
            dst = dst * dec_s[pl.ds(j * BLK, 1), :] + u_s[j]
        dstate[...] = dst
        for j in range(nb):
            rows = pl.ds(j * BLK, BLK)
            dke_s[rows, :] = lax.dot_general(v_b[rows, :], ds_s[j], (((1,), (0,)), ((), ())), preferred_element_type=F32)
            dvi_s[rows, :] = lax.dot_general(ke_b[rows, :], ds_s[j], (((1,), (1,)), ((), ())), preferred_element_type=F32)

        def after(s, carry):
            r = pl.ds(pl.multiple_of(s * sub, sub), sub)
            q, v = q_ref[r, :], v_ref[r, :]
            sgq, qs, sg, f, kk, b, b_tot = _hgrn_decays(q, z_ref[r, :], lbv, pos, sub)
            do = do_s[r, :]
            dqs = dqe_s[r, :] * jnp.exp(b)
            dkk = dke_s[r, :] * jnp.exp(b_tot - b)
            d_tot = jnp.where(pos == BLK - 1, _block_cumsum(dkk * kk, pos) + ddec_s[r, :] * jnp.exp(b_tot), 0.0)
            dv = dvi_s[r, :]
            da = jnp.sum(do * v, axis=-1, keepdims=True)
            dv = dv + jnp.sum(qs * kk, axis=-1, keepdims=True) * do
            dqs = dqs + da * kk
            dkk = dkk + da * qs
            for d in range(1, BLK):
                kd = pltpu.roll(kk, d, 0)
                e = jnp.where(pos >= d, jnp.exp(b - pltpu.roll(b, d, 0)), 0.0)
                a = jnp.sum(qs * kd * e, axis=-1, keepdims=True)
                da = jnp.sum(do * pltpu.roll(v, d, 0), axis=-1, keepdims=True)
                gq = da * e
                dqs = dqs + gq * kd
                dkk = dkk + pltpu.roll(gq * qs, sub - d, 0)
                dv = dv + pltpu.roll(a * do, sub - d, 0)
            db = qs * dqs - kk * dkk + d_tot
            dlf = _block_rev_cumsum(db, pos, sub)
            df = dlf / f - dkk
            dlb_ref[...] += jnp.sum(df * (1.0 - sg), axis=0, keepdims=True)
            dz_ref[r, :] = (df * (1.0 - lbv) * sg * (1.0 - sg)).astype(BF16)
            dq_ref[r, :] = (dqs * (sgq * (1.0 + q * (1.0 - sgq)))).astype(BF16)
            dv_ref[r, :] = dv.astype(BF16)
            return carry

        lax.fori_loop(0, T // sub, after, 0)

    tb = lambda dt: pltpu.VMEM((T, HEAD), dt)
    grid = (H, n_t)
    in_specs = [act_spec, act_spec, st_spec, q_spec, z_spec, v_spec, g_spec, lb_spec, gn_spec]
    args = [dob, o_raw, states, proj, proj, proj, proj, lb, gn]
    out_specs = [act_spec, act_spec, act_spec, act_spec, lb_spec, pl.BlockSpec((None, 1, HEAD), lambda h, i: (h, 0, 0))]
    out_shape = ([jax.ShapeDtypeStruct((S, d_conv), BF16)] * 4
                 + [jax.ShapeDtypeStruct((1, d_conv), F32), jax.ShapeDtypeStruct((H, 1, HEAD), F32)])
    scratch = [pltpu.VMEM((HEAD, HEAD), F32), tb(BF16), tb(BF16), tb(BF16), tb(BF16), tb(F32),
               tb(F32), tb(F32), tb(F32), tb(F32), tb(F32),
               pltpu.VMEM((nb, HEAD, HEAD), F32), pltpu.VMEM((nb, HEAD, HEAD), BF16)]
    aliases, wrap = _host(job, grid, in_specs, args, out_specs, out_shape, scratch)
    return pl.pallas_call(
        wrap(body), name="hgrn_bwd", grid=grid, in_specs=in_specs, out_specs=out_specs, out_shape=out_shape,
        scratch_shapes=scratch, input_output_aliases=aliases,
        compiler_params=_params(("parallel" if job is None else "arbitrary", "arbitrary")),
    )(*args)


def _first(accs, extras):
    return [accs[0]]


def _local_step(x, target, mod, norm_mix_g, lb, gnorm_g, norm_ffn_g, norm_final_g, conv_w, bufs, c_idx):
    S, D = x.shape
    d_conv = D // 2
    d_in = 7 * d_conv + 2 * D
    d_ff = N_CHIPS * bufs["w_ffn_down"].shape[1]
    nb_in, nb_co, nb_ff = bufs["w_in"].shape[2], bufs["w_conv_out"].shape[2], bufs["w_ffn_gate"].shape[2]
    rows = lambda g: g.reshape(-1, g.shape[2])
    w_in, = _gather_weights([bufs["w_in"]])
    sh_m, sc_m, gt_m, sh_f, sc_f, gt_f = [mod[:, i * D:(i + 1) * D] for i in range(6)]
    tm = _pick(S, 512, 16)
    tm2 = _pick(S, 1024, 16)
    ts = _pick(S, 1024, 128)
    tn_in = _pick(nb_in, 1408, 128)
    tn_co = _pick(nb_co, 512, 128)
    tn_ff = _pick(nb_ff, 1408, 128)
    tn_d = _pick(D, 512, 128)
    tw = _pick(D, 1024, 128)
    ts2 = _pick(S, 2048, 128)

    h = _norm_mod("norm_mix", x, norm_mix_g, sc_m, sh_m)
    proj, *got = _matmul("proj", [(h, 'n', w_in, 'n')], [0], S, d_in, D, tm2, tn_in, D, [], [(F32, None)], _first,
                         job=_GatherJob([bufs[k] for k in ("w_conv_out", "w_hgrn_out", "w_o")]))
    w_conv_out, w_hgrn_out, w_o = _forward_halves("forward_branch", got)
    w_o = rows(w_o)
    ob, o_raw, states, *got = _hgrn_fwd(proj, lb, gnorm_g, d_conv,
                                        job=_GatherJob([bufs[k] for k in ("w_ffn_gate", "w_ffn_up")]))
    ya = _conv_fwd(proj, conv_w, d_conv)
    w_ffn_gate, w_ffn_up = _forward_halves("forward_ffn", got)

    def merge_epi(accs, ex):
        y_a, y_b = accs
        return [y_a, y_b, _sig(ex[0]) * y_a + _sig(ex[1]) * y_b]

    y_a, y_b, merged = _matmul(
        "branch_out", [(ya, 'n', w_conv_out, 'n'), (ob, 'n', w_hgrn_out, 'n')], [0, 1], S, D, d_conv, tm2, tn_co, d_conv,
        [(proj, 'tile', 7 * d_conv), (proj, 'tile', 7 * d_conv + D)], [(F32, None), (F32, None), (BF16, None)], merge_epi)

    def resid_epi(accs, ex):
        return [accs[0], ex[0] + ex[1] * accs[0]]

    mo, x1 = _matmul("w_o", [(merged, 'n', w_o, 'n')], [0], S, D, D, tm2, tw, D,
                     [(x, 'tile', 0), (gt_m, 'row', 0)], [(F32, None), (F32, None)], resid_epi)
    h2 = _norm_mod("norm_ffn", x1, norm_ffn_g, sc_f, sh_f)

    def swiglu_epi(accs, ex):
        gg, uu = accs
        return [gg, uu, gg * _sig(gg) * uu]

    G, U, act, w_ffn_down = _matmul(
        "ffn_in", [(h2, 'n', w_ffn_gate, 'n'), (h2, 'n', w_ffn_up, 'n')], [0, 1], S, d_ff, D,
        tm, tn_ff, D, [], [(BF16, None), (BF16, None), (BF16, None)], swiglu_epi, job=_GatherJob([bufs["w_ffn_down"]]))
    w_ffn_down = rows(_forward_halves("forward_down", [w_ffn_down])[0])
    ff, x2 = _matmul("ffn_out", [(act, 'n', w_ffn_down, 'n')], [0], S, D, d_ff, tm, D, _pick(d_ff, 1408, 128),
                     [(x1, 'tile', 0), (gt_f, 'row', 0)], [(F32, None), (F32, None)], resid_epi)

    dx2, dffb, loss, dgt_f, dg_final = _loss_head(x2, target, norm_final_g, ff, gt_f)

    def swiglu_bwd_epi(accs, ex):
        dact, gg, uu = accs[0], ex[0], ex[1]
        s = _sig(gg)
        return [dact * uu * (s * (1.0 + gg * (1.0 - s))), dact * (gg * s)]

    dG, dU = _matmul("d_act", [(dffb, 'n', w_ffn_down, 't')], [0], S, d_ff, D, tm2, tn_ff, D,
                     [(G, 'tile', 0), (U, 'tile', 0)], [(BF16, None), (BF16, None)], swiglu_bwd_epi)
    dw_down, = _matmul("dw_ffn_down", [(act, 't', dffb, 'n')], [0], d_ff, D, S, _pick(d_ff, 1408, 128),
                       D, ts, [], [(BF16, None)], _first)
    dh2, = _matmul("d_h2", [(dG, 'n', w_ffn_gate, 't'), (dU, 'n', w_ffn_up, 't')], [0, 0], S, D, d_ff,
                   tm, D, tn_ff, [], [(F32, None)], _first)
    dw_gate, dw_up = _matmul("dw_ffn_in", [(h2, 't', dG, 'n'), (h2, 't', dU, 'n')], [0, 1], D, d_ff, S,
                             tw, tn_ff, ts, [], [(BF16, nb_ff), (BF16, nb_ff)],
                             lambda accs, ex: accs)

    def pre_sum(tag, names, grads):
        from_sibling = _swap_halves("swap_" + tag, grads)
        pairs = [_pair_sum("pair_sum_" + k, g, q, c_idx) for k, g, q in zip(names, grads, from_sibling)]
        return [p[0] for p in pairs], [p[1] for p in pairs]

    ffn_names = ["w_ffn_down", "w_ffn_gate", "w_ffn_up"]
    parts_f, slots_f = pre_sum("ffn", ffn_names, [dw_down.reshape(N_CHIPS, -1, D), dw_gate, dw_up])
    dx1, dsh_f, dsc_f, dg_ffn, dmob, dgt_m = _norm_mod_bwd("norm_ffn_bwd", dh2, x1, norm_ffn_g, sc_f, dx2, (mo, gt_m))

    def merge_bwd_epi(accs, ex):
        dm, ga, gb, ya_, yb_ = accs[0], ex[0], ex[1], ex[2], ex[3]
        sa, sb = _sig(ga), _sig(gb)
        return [dm * ya_ * sa * (1.0 - sa), dm * yb_ * sb * (1.0 - sb), dm * sa, dm * sb]

    dga, dgb, dy_a, dy_b = _matmul(
        "d_merged", [(dmob, 'n', w_o, 't')], [0], S, D, D, tm2, tn_co, D,
        [(proj, 'tile', 7 * d_conv), (proj, 'tile', 7 * d_conv + D), (y_a, 'tile', 0), (y_b, 'tile', 0)],
        [(BF16, None)] * 4, merge_bwd_epi)
    dw_o, = _matmul("dw_o", [(merged, 't', dmob, 'n')], [0], D, D, S, _pick(D, 512, 128), _pick(D, 1024, 128), ts,
                    [], [(BF16, None)], _first)
    both = lambda accs, ex: accs
    dya, dob = _matmul("d_branch", [(dy_a, 'n', w_conv_out, 't'), (dy_b, 'n', w_hgrn_out, 't')], [0, 1], S, d_conv, D,
                       tm2, _pick(d_conv, 1024, 128), tn_co, [], [(F32, None), (F32, None)], both)
    dw_co, dw_ho = _matmul("dw_branch", [(ya, 't', dy_a, 'n'), (ob, 't', dy_b, 'n')], [0, 1], d_conv, D, S,
                           _pick(d_conv, 1024, 128), tn_co, ts, [], [(BF16, nb_co), (BF16, nb_co)], both)
    branch_names = ["w_conv_out", "w_hgrn_out", "w_o"]
    parts_b, slots_b = pre_sum("branch", branch_names, [dw_co, dw_ho, dw_o.reshape(N_CHIPS, -1, D)])
    dab, dac, dax, dconv_w = _conv_bwd(dya, proj, conv_w, d_conv)
    dq, dz, dv, dgo, dlb, dgn, *arrived = _hgrn_bwd(dob, o_raw, states, proj, lb, gnorm_g, d_conv,
                                                    job=_ScatterJob(parts_f + parts_b, slots_f + slots_b))
    dproj = jnp.concatenate([dab, dac, dax, dq, dz, dv, dgo, dga, dgb], axis=1)
    dw_in, = _matmul("dw_in", [(h, 't', dproj, 'n')], [0], D, d_in, S, tw, tn_in, ts2,
                     [], [(BF16, nb_in)], _first)
    parts_i, slots_i = pre_sum("in", ["w_in"], [dw_in])
    dh, arrived_in = _matmul("d_h", [(dproj, 'n', w_in, 't')], [0], S, D, d_in, tm2, D, tn_in, [], [(F32, None)], _first,
                             job=_ScatterJob(parts_i, slots_i))
    dx, dsh_m, dsc_m, dg_mix = _norm_mod_bwd("norm_mix_bwd", dh, x, norm_mix_g, sc_m, dx1)
    dmod = jnp.concatenate([dsh_m, dsc_m, dgt_m, dsh_f, dsc_f, dgt_f], axis=1)
    small = dict(dmod=dmod, dg_mix=dg_mix, dg_ffn=dg_ffn, dg_final=dg_final, dlb=dlb,
                 dgn=jnp.sum(dgn, axis=0), dconv_w=dconv_w)
    big = dict(zip(ffn_names + branch_names, arrived), w_in=arrived_in)
    return loss, dx, small, big


def _adamw_math(w, g, m, v):
    m = ADAM_B1 * m + (1.0 - ADAM_B1) * g
    v = ADAM_B2 * v + (1.0 - ADAM_B2) * (g * g)
    m_hat = m / (1.0 - ADAM_B1 ** ADAM_STEP)
    v_hat = v / (1.0 - ADAM_B2 ** ADAM_STEP)
    delta = -ADAM_LR * (m_hat / (jnp.sqrt(v_hat) + ADAM_EPS) + ADAM_WD * w)
    return delta, m, v


def _adamw(name, w, g, m, v):
    R, C = w.shape
    tr = _pick(R, max(8, (256 * 1024) // C // 8 * 8), 8)
    spec = pl.BlockSpec((tr, C), lambda i: (i, 0))

    def body(w_ref, g_ref, m_ref, v_ref, d_ref, mo_ref, vo_ref):
        d, m2, v2 = _adamw_math(w_ref[...], g_ref[...], m_ref[...], v_ref[...])
        d_ref[...] = d
        mo_ref[...] = m2
        vo_ref[...] = v2

    return pl.pallas_call(
        body, name=name, grid=(R // tr,), in_specs=[spec] * 4, out_specs=[spec] * 3,
        out_shape=[jax.ShapeDtypeStruct((R, C), F32)] * 3, compiler_params=_params(("parallel",)),
    )(w, g, m, v)


def _ada_update(c_act_t, dmod, w, m, v):
    R, C = w.shape
    B = dmod.shape[0]
    tr = _pick(R, 256, 8)
    tc = _pick(C, 1536, 128)
    spec = pl.BlockSpec((tr, tc), lambda i, j: (i, j))

    def body(ct_ref, dm_ref, w_ref, m_ref, v_ref, g_ref, d_ref, mo_ref, vo_ref):
        ct = ct_ref[...]
        dm = dm_ref[...]
        g = ct[:, 0:1] * dm[0:1, :]
        for b in range(1, B):
            g = g + ct[:, b:b + 1] * dm[b:b + 1, :]
        d, m2, v2 = _adamw_math(w_ref[...], g, m_ref[...], v_ref[...])
        g_ref[...] = g
        d_ref[...] = d
        mo_ref[...] = m2
        vo_ref[...] = v2

    return pl.pallas_call(
        body, name="ada_update", grid=(R // tr, C // tc),
        in_specs=[pl.BlockSpec((tr, B), lambda i, j: (i, 0)), pl.BlockSpec((B, tc), lambda i, j: (0, j)),
                  spec, spec, spec],
        out_specs=[spec] * 4, out_shape=[jax.ShapeDtypeStruct((R, C), F32)] * 4,
        compiler_params=_params(("parallel", "parallel")),
    )(c_act_t, dmod, w, m, v)


def _prep(c_all, lb_param):
    def body(c_ref, p_ref, ca_ref, cab_ref, lb_ref):
        cv = c_ref[...]
        ca = cv * _sig(cv)
        ca_ref[...] = ca
        cab_ref[...] = ca.astype(BF16)
        lb_ref[...] = _sig(p_ref[0:1, :] - p_ref[1:2, :])

    return pl.pallas_call(
        body, name="prep",
        out_shape=[jax.ShapeDtypeStruct(c_all.shape, F32), jax.ShapeDtypeStruct(c_all.shape, BF16),
                   jax.ShapeDtypeStruct((1, lb_param.shape[1]), F32)],
    )(c_all, lb_param)


def _small_reduce(parts):
    W = parts.shape[1]

    def body(p_ref, o_ref):
        acc = p_ref[0:1, :]
        for d in range(1, N_DEV):
            acc = acc + p_ref[d:d + 1, :]
        o_ref[...] = acc

    return pl.pallas_call(body, name="small_reduce", out_shape=jax.ShapeDtypeStruct((1, W), F32))(parts)


def _lb_grad(dlb, lb):
    def body(d_ref, lb_ref, o_ref):
        t = d_ref[...] * lb_ref[...] * (1.0 - lb_ref[...])
        o_ref[0:1, :] = t
        o_ref[1:2, :] = -t

    return pl.pallas_call(body, name="lb_grad", out_shape=jax.ShapeDtypeStruct((2, dlb.shape[1]), F32))(dlb, lb)


def _place():
    x, y, c = lax.axis_index("x"), lax.axis_index("y"), lax.axis_index("c")
    chips = [(1 - x, y), (x, 1 - y), (1 - x, 1 - y)]
    return x, y, c, chips


def _allgather_rows(name, v):
    m_per, n = v.shape

    def body(x_ref, out_ref, send_sems, recv_sems, local_sem):
        x, y, c, chips = _place()
        me, sibling = (x, y, c), (x, y, 1 - c)

        def rows(px, py, pc):
            return out_ref.at[pl.ds((4 * px + 2 * py + pc) * m_per, m_per), :]

        def copy(k, block, to, src=None):
            return pltpu.make_async_remote_copy(
                src_ref=rows(*block) if src is None else src, dst_ref=rows(*block),
                send_sem=send_sems.at[k], recv_sem=recv_sems.at[k], device_id=to, device_id_type=MESH)

        mine = pltpu.make_async_copy(x_ref, rows(*me), local_sem)
        mine.start()
        first = [copy(0, me, sibling, src=x_ref)]
        first += [copy(1 + j, me, (*chip, c), src=x_ref) for j, chip in enumerate(chips)]
        for cp in first:
            cp.start()
        passed = [copy(4 + j, (*chip, c), sibling) for j, chip in enumerate(chips)]
        for j, chip in enumerate(chips):
            copy(1 + j, (*chip, c), me).wait_recv()
            passed[j].start()
        copy(0, sibling, me).wait_recv()
        for j, chip in enumerate(chips):
            copy(4 + j, (*chip, 1 - c), me).wait_recv()
        for cp in first + passed:
            cp.wait_send()
        mine.wait()

    return pl.pallas_call(
        body, name=name, out_shape=jax.ShapeDtypeStruct((N_DEV * m_per, n), v.dtype),
        in_specs=[pl.BlockSpec(memory_space=pltpu.VMEM)], out_specs=pl.BlockSpec(memory_space=pltpu.VMEM),
        scratch_shapes=[pltpu.SemaphoreType.DMA((7,)), pltpu.SemaphoreType.DMA((7,)), pltpu.SemaphoreType.DMA],
    )(v)


def _cast_place(name, w, k_idx):
    R, C = w.shape
    tr = _pick(R, max(16, (512 * 1024) // C // 16 * 16), 16)

    def body(k_ref, w_ref, o_ref):
        o_ref[...] = w_ref[...].astype(BF16)

    return pl.pallas_call(
        body, name=name,
        grid_spec=pltpu.PrefetchScalarGridSpec(
            num_scalar_prefetch=1, grid=(R // tr,),
            in_specs=[pl.BlockSpec((tr, C), lambda i, k_ref: (i, 0))],
            out_specs=pl.BlockSpec((None, tr, C), lambda i, k_ref: (k_ref[0], i, 0))),
        out_shape=jax.ShapeDtypeStruct((N_CHIPS, R, C), BF16),
        compiler_params=_params(("parallel",)),
    )(k_idx, w)


def _gather_weights(bufs):
    n = len(bufs)

    def body(*refs):
        out_refs = refs[n:2 * n]
        send_sems, recv_sems = refs[2 * n:]
        x, y, c, chips = _place()
        k_me = 2 * x + y
        sibling = (x, y, 1 - c)

        def region(a, k, hc):
            rh = bufs[a].shape[1] // 2
            return out_refs[a].at[k, pl.ds(hc * rh, rh)]

        def copy(a, s, src, dst, to):
            return pltpu.make_async_remote_copy(src_ref=src, dst_ref=dst, send_sem=send_sems.at[6 * a + s],
                                                recv_sem=recv_sems.at[6 * a + s], device_id=to, device_id_type=MESH)

        started = []
        for a in range(n):
            mine = region(a, k_me, c)
            for j, chip in enumerate(chips):
                cp = copy(a, j, mine, mine, (*chip, c))
                cp.start()
                started.append(cp)
        for a in range(n):
            for j, chip in enumerate(chips):
                got = region(a, 2 * chip[0] + chip[1], c)
                copy(a, j, got, got, (x, y, c)).wait_recv()
                cp = copy(a, 3 + j, got, got, sibling)
                cp.start()
                started.append(cp)
        for a in range(n):
            for j, chip in enumerate(chips):
                got = region(a, 2 * chip[0] + chip[1], 1 - c)
                copy(a, 3 + j, got, got, (x, y, c)).wait_recv()
        for cp in started:
            cp.wait_send()

    return pl.pallas_call(
        body, name="gather_weights",
        out_shape=[jax.ShapeDtypeStruct(b.shape, b.dtype) for b in bufs],
        in_specs=[_HBM] * n, out_specs=[_HBM] * n, input_output_aliases={a: a for a in range(n)},
        scratch_shapes=[pltpu.SemaphoreType.DMA((6 * n,)), pltpu.SemaphoreType.DMA((6 * n,))],
    )(*bufs)


def _chip_copies(src_of, dst_of, got_of, n, sems, receiving):
    x, y, c, chips = _place()
    k_me = 2 * x + y
    send_sems, recv_sems = sems
    out = []
    for a in range(n):
        for j, chip in enumerate(chips):
            k_chip = 2 * chip[0] + chip[1]
            if receiving:
                src = dst = got_of(a, k_chip, c)
                to = (x, y, c)
            else:
                src, dst, to = src_of(a, k_chip, k_me, c), dst_of(a, k_me, c), (*chip, c)
            out.append(pltpu.make_async_remote_copy(
                src_ref=src, dst_ref=dst, send_sem=send_sems.at[3 * a + j], recv_sem=recv_sems.at[3 * a + j],
                device_id=to, device_id_type=MESH))
    return out


class _ChipJob:
    def start(self, j_in, j_out, sems):
        for send in self.copies(j_in, j_out, sems, False):
            send.start()

    def finish(self, j_in, j_out, sems):
        for recv in self.copies(j_in, j_out, sems, True):
            recv.wait_recv()
        for send in self.copies(j_in, j_out, sems, False):
            send.wait_send()


class _GatherJob(_ChipJob):
    def __init__(self, bufs):
        n = len(bufs)
        self.inputs, self.n_alias = list(bufs), n
        self.sem_shapes = [pltpu.SemaphoreType.DMA((3 * n,)), pltpu.SemaphoreType.DMA((3 * n,))]
        self.half = [b.shape[1] // 2 for b in bufs]

    def copies(self, j_in, j_out, sems, receiving):
        def half(a, k, c):
            return j_out[a].at[k, pl.ds(c * self.half[a], self.half[a])]

        return _chip_copies(lambda a, k_chip, k_me, c: half(a, k_me, c), half, half, len(self.half), sems, receiving)


class _ScatterJob(_ChipJob):
    def __init__(self, parts, slots):
        n = len(parts)
        self.n = n
        self.inputs, self.n_alias = list(parts) + list(slots), n
        self.sem_shapes = [pltpu.SemaphoreType.DMA((3 * n,)), pltpu.SemaphoreType.DMA((3 * n,))]

    def copies(self, j_in, j_out, sems, receiving):
        return _chip_copies(lambda a, k_chip, k_me, c: j_in[a].at[k_chip], lambda a, k_me, c: j_out[a].at[k_me],
                            lambda a, k_chip, c: j_out[a].at[k_chip], self.n, sems, receiving)


def _forward_halves(name, bufs):
    n = len(bufs)

    def body(*refs):
        out_refs = refs[n:2 * n]
        send_sems, recv_sems = refs[2 * n:]
        x, y, c, chips = _place()
        started, waits = [], []
        for a in range(n):
            rh = bufs[a].shape[1] // 2
            for j, chip in enumerate(chips):
                k_chip = 2 * chip[0] + chip[1]
                got = out_refs[a].at[k_chip, pl.ds(c * rh, rh)]
                cp = pltpu.make_async_remote_copy(src_ref=got, dst_ref=got, send_sem=send_sems.at[3 * a + j],
                                                  recv_sem=recv_sems.at[3 * a + j], device_id=(x, y, 1 - c),
                                                  device_id_type=MESH)
                cp.start()
                started.append(cp)
                other = out_refs[a].at[k_chip, pl.ds((1 - c) * rh, rh)]
                waits.append(pltpu.make_async_remote_copy(
                    src_ref=other, dst_ref=other, send_sem=send_sems.at[3 * a + j], recv_sem=recv_sems.at[3 * a + j],
                    device_id=(x, y, c), device_id_type=MESH))
        for cp in waits:
            cp.wait_recv()
        for cp in started:
            cp.wait_send()

    return pl.pallas_call(
        body, name=name, out_shape=[jax.ShapeDtypeStruct(b.shape, b.dtype) for b in bufs],
        in_specs=[_HBM] * n, out_specs=[_HBM] * n, input_output_aliases={a: a for a in range(n)},
        scratch_shapes=[pltpu.SemaphoreType.DMA((3 * n,)), pltpu.SemaphoreType.DMA((3 * n,))],
    )(*bufs)


def _swap_halves(name, grads):
    n = len(grads)

    def body(*refs):
        in_refs, out_refs = refs[:n], refs[n:2 * n]
        send_sems, recv_sems = refs[2 * n:]
        x, y, c, _ = _place()
        cps = []
        for a in range(n):
            rh = grads[a].shape[1] // 2
            cp = pltpu.make_async_remote_copy(
                src_ref=in_refs[a].at[:, pl.ds((1 - c) * rh, rh)], dst_ref=out_refs[a],
                send_sem=send_sems.at[a], recv_sem=recv_sems.at[a], device_id=(x, y, 1 - c), device_id_type=MESH)
            cp.start()
            cps.append(cp)
        for cp in cps:
            cp.wait()

    return pl.pallas_call(
        body, name=name,
        out_shape=[jax.ShapeDtypeStruct((N_CHIPS, g.shape[1] // 2, g.shape[2]), g.dtype) for g in grads],
        in_specs=[_HBM] * n, out_specs=[_HBM] * n,
        scratch_shapes=[pltpu.SemaphoreType.DMA((n,)), pltpu.SemaphoreType.DMA((n,))],
    )(*grads)


def _pair_sum(name, g, q, c_idx):
    _, R, C = g.shape
    rh = R // 2
    tr = _pick(rh, max(16, (512 * 1024) // C // 16 * 16), 16)
    nh = rh // tr

    def body(c_ref, g_ref, q_ref, o_ref, o2_ref):
        s = (g_ref[...].astype(F32) + q_ref[...].astype(F32)).astype(BF16)
        o_ref[...] = s
        o2_ref[...] = s

    out_spec = pl.BlockSpec((None, tr, C), lambda k, i, c_ref: (k, i, 0))
    return pl.pallas_call(
        body, name=name,
        grid_spec=pltpu.PrefetchScalarGridSpec(
            num_scalar_prefetch=1, grid=(N_CHIPS, nh),
            in_specs=[pl.BlockSpec((None, tr, C), lambda k, i, c_ref: (k, c_ref[0] * nh + i, 0)),
                      pl.BlockSpec((None, tr, C), lambda k, i, c_ref: (k, i, 0))],
            out_specs=[out_spec, out_spec]),
        out_shape=[jax.ShapeDtypeStruct((N_CHIPS, rh, C), BF16)] * 2,
        compiler_params=_params(("parallel", "parallel")),
    )(c_idx, g, q)


def _sum_chips(name, r, c_idx):
    _, rh, C = r.shape
    tr = _pick(rh, max(16, (256 * 1024) // C // 16 * 16), 16)

    def body(c_ref, r_ref, o_ref):
        acc = r_ref[0].astype(F32)
        for k in range(1, N_CHIPS):
            acc = acc + r_ref[k].astype(F32)
        o_ref[...] = acc

    return pl.pallas_call(
        body, name=name,
        grid_spec=pltpu.PrefetchScalarGridSpec(
            num_scalar_prefetch=1, grid=(rh // tr,),
            in_specs=[pl.BlockSpec((N_CHIPS, tr, C), lambda i, c_ref: (0, i, 0))],
            out_specs=pl.BlockSpec((None, tr, C), lambda i, c_ref: (c_ref[0], i, 0))),
        out_shape=jax.ShapeDtypeStruct((2, rh, C), F32), compiler_params=_params(("parallel",)),
    )(c_idx, r)


def _share_halves(bufs):
    n = len(bufs)

    def body(*refs):
        out_refs = refs[n:2 * n]
        send_sems, recv_sems = refs[2 * n:]
        x, y, c, _ = _place()
        cps = []
        for a in range(n):
            cp = pltpu.make_async_remote_copy(
                src_ref=out_refs[a].at[c], dst_ref=out_refs[a].at[c], send_sem=send_sems.at[a],
                recv_sem=recv_sems.at[a], device_id=(x, y, 1 - c), device_id_type=MESH)
            cp.start()
            cps.append(cp)
        for a, cp in enumerate(cps):
            got = out_refs[a].at[1 - c]
            pltpu.make_async_remote_copy(src_ref=got, dst_ref=got, send_sem=send_sems.at[a], recv_sem=recv_sems.at[a],
                                         device_id=(x, y, c), device_id_type=MESH).wait_recv()
        for cp in cps:
            cp.wait_send()

    return pl.pallas_call(
        body, name="share_halves",
        out_shape=[jax.ShapeDtypeStruct(b.shape, b.dtype) for b in bufs],
        in_specs=[_HBM] * n, out_specs=[_HBM] * n, input_output_aliases={a: a for a in range(n)},
        scratch_shapes=[pltpu.SemaphoreType.DMA((n,)), pltpu.SemaphoreType.DMA((n,))],
    )(*bufs)


_BIG = ("w_in", "w_conv_out", "w_hgrn_out", "w_o", "w_ffn_gate", "w_ffn_up", "w_ffn_down")
_ROW_SHARDED = ("w_o", "w_ffn_down")
_WEIGHTS = ("w_ada", "b_ada", "norm_mix_g", "w_in", "conv_w", "lb_param", "gnorm_g", "w_conv_out", "w_hgrn_out",
            "w_o", "norm_ffn_g", "w_ffn_gate", "w_ffn_up", "w_ffn_down", "norm_final_g")


def _pack_rows(pieces):
    flat = jnp.concatenate([p.reshape(-1) for p in pieces])
    pad = (-flat.shape[0]) % 1024
    return jnp.pad(flat, (0, pad)).reshape(8, -1)


def kernel(x, c, w_ada, b_ada, norm_mix_g, w_in, conv_w, lb_param, gnorm_g, w_conv_out, w_hgrn_out, w_o, norm_ffn_g, w_ffn_gate, w_ffn_up, w_ffn_down, norm_final_g, loss_target, m_w_ada, m_b_ada, m_norm_mix_g, m_w_in, m_conv_w, m_lb_param, m_gnorm_g, m_w_conv_out, m_w_hgrn_out, m_w_o, m_norm_ffn_g, m_w_ffn_gate, m_w_ffn_up, m_w_ffn_down, m_norm_final_g, v_w_ada, v_b_ada, v_norm_mix_g, v_w_in, v_conv_w, v_lb_param, v_gnorm_g, v_w_conv_out, v_w_hgrn_out, v_w_o, v_norm_ffn_g, v_w_ffn_gate, v_w_ffn_up, v_w_ffn_down, v_norm_final_g):
    w = dict(w_ada=w_ada, b_ada=b_ada, norm_mix_g=norm_mix_g, w_in=w_in, conv_w=conv_w, lb_param=lb_param,
             gnorm_g=gnorm_g, w_conv_out=w_conv_out, w_hgrn_out=w_hgrn_out, w_o=w_o, norm_ffn_g=norm_ffn_g,
             w_ffn_gate=w_ffn_gate, w_ffn_up=w_ffn_up, w_ffn_down=w_ffn_down, norm_final_g=norm_final_g)
    m = dict(w_ada=m_w_ada, b_ada=m_b_ada, norm_mix_g=m_norm_mix_g, w_in=m_w_in, conv_w=m_conv_w, lb_param=m_lb_param,
             gnorm_g=m_gnorm_g, w_conv_out=m_w_conv_out, w_hgrn_out=m_w_hgrn_out, w_o=m_w_o, norm_ffn_g=m_norm_ffn_g,
             w_ffn_gate=m_w_ffn_gate, w_ffn_up=m_w_ffn_up, w_ffn_down=m_w_ffn_down, norm_final_g=m_norm_final_g)
    v = dict(w_ada=v_w_ada, b_ada=v_b_ada, norm_mix_g=v_norm_mix_g, w_in=v_w_in, conv_w=v_conv_w, lb_param=v_lb_param,
             gnorm_g=v_gnorm_g, w_conv_out=v_w_conv_out, w_hgrn_out=v_w_hgrn_out, w_o=v_w_o, norm_ffn_g=v_norm_ffn_g,
             w_ffn_gate=v_w_ffn_gate, w_ffn_up=v_w_ffn_up, w_ffn_down=v_w_ffn_down, norm_final_g=v_norm_final_g)
    two_d = lambda a: a.reshape((-1, a.shape[-1])) if a.ndim != 2 else a
    shapes = {k: a.shape for k, a in w.items()}
    w, m, v = ({k: two_d(a) for k, a in t.items()} for t in (w, m, v))
    w["lb_param"], m["lb_param"], v["lb_param"] = lb_param, m_lb_param, v_lb_param
    S, D = x.shape[1], x.shape[2]
    d_conv = D // 2
    ix, iy, ic = lax.axis_index("x"), lax.axis_index("y"), lax.axis_index("c")
    k_me = 2 * ix + iy
    dev = 2 * k_me + ic
    cw_shard = w["conv_w"].shape[1]
    ada_cols = w["w_ada"].shape[1]

    got = _allgather_rows("gather_cond", _pack_rows([c, w["conv_w"]])).reshape(N_DEV, -1)
    c_all = got[:, :D]
    conv_full = got[::2, D:D + 3 * cw_shard].reshape(N_CHIPS, 3, cw_shard).transpose(1, 0, 2).reshape(3, -1)
    c_act, c_act_b, lb = _prep(c_all, lb_param)
    b_cols = lax.dynamic_slice(w["b_ada"], (0, k_me * ada_cols), (1, ada_cols))
    mod_cols, = _matmul("ada_fwd", [(c_act_b, 'n', w["w_ada"].astype(BF16), 'n')], [0], N_DEV, ada_cols, D,
                        N_DEV, _pick(ada_cols, 1536, 128), D, [(b_cols, 'row', 0)], [(F32, None)],
                        lambda accs, ex: [accs[0] + ex[0]])
    mod_all = _allgather_rows("gather_mod", mod_cols).reshape(N_CHIPS, 2, N_DEV, ada_cols)[:, 0]
    mod_all = mod_all.transpose(1, 0, 2).reshape(N_DEV, -1)
    mod = lax.dynamic_slice(mod_all, (dev, 0), (1, mod_all.shape[1]))
    k_idx = jnp.reshape(k_me, (1,)).astype(jnp.int32)
    c_idx = jnp.reshape(ic, (1,)).astype(jnp.int32)
    bufs = {k: _cast_place("cast_" + k, w[k], k_idx) for k in _BIG}

    loss, dx, small, arrived = _local_step(
        x[0], loss_target[0], mod, w["norm_mix_g"], lb, w["gnorm_g"], w["norm_ffn_g"], w["norm_final_g"], conv_full,
        bufs, c_idx)

    pieces = [small["dmod"], small["dg_mix"], small["dg_ffn"], small["dg_final"], small["dlb"], small["dgn"],
              small["dconv_w"], loss]
    sizes = [p.size for p in pieces]
    parts = _allgather_rows("gather_small", _pack_rows(pieces)).reshape(N_DEV, -1)
    total = _small_reduce(parts)
    offs = [0]
    for s in sizes:
        offs.append(offs[-1] + s)
    tot = [total[:, offs[i]:offs[i + 1]] for i in range(len(sizes))]
    dmod_all = parts[:, :sizes[0]]
    grads = {
        "b_ada": tot[0], "norm_mix_g": tot[1], "norm_ffn_g": tot[2], "norm_final_g": tot[3],
        "lb_param": _lb_grad(tot[4], lb), "gnorm_g": tot[5],
        "conv_w": lax.dynamic_slice(tot[6].reshape(3, -1), (0, k_me * cw_shard), (3, cw_shard)),
    }
    loss_out = tot[7][0, 0]

    halves = [_sum_chips("sum_chips_" + k, arrived[k], c_idx) for k in _BIG]
    whole = _share_halves(halves)
    for k, g in zip(_BIG, whole):
        grads[k] = g.reshape(-1, g.shape[-1])

    delta, new_m, new_v = {}, {}, {}
    dmod_cols = lax.dynamic_slice(dmod_all, (0, k_me * ada_cols), (N_DEV, ada_cols))
    grads["w_ada"], delta["w_ada"], new_m["w_ada"], new_v["w_ada"] = _ada_update(
        c_act.T, dmod_cols, w["w_ada"], m["w_ada"], v["w_ada"])
    for k in _WEIGHTS:
        if k != "w_ada":
            delta[k], new_m[k], new_v[k] = _adamw("adamw_" + k, w[k], grads[k], m[k], v[k])
    outs = [loss_out, dx.reshape(x.shape)]
    for t in (grads, delta, new_m, new_v):
        outs += [t[k].reshape(shapes[k]) for k in _WEIGHTS]
    return tuple(outs)
```

```python
import functools

import jax
import jax.numpy as jnp
from jax import lax
from jax.experimental import pallas as pl
from jax.experimental.pallas import tpu as pltpu

F32 = jnp.float32
BF16 = jnp.bfloat16
MESH = pl.DeviceIdType.MESH

EPS = 1e-6
HEAD = 128
BLK = 16
N_CHIPS = 4
N_DEV = 8
VMEM_LIMIT_BYTES = 56 * 1024 * 1024

ADAM_LR = 0.001
ADAM_B1 = 0.9
ADAM_B2 = 0.999
ADAM_EPS = 1e-08
ADAM_WD = 0.01
ADAM_STEP = 10


def _pick(dim, pref, mult):
    if dim <= pref:
        return dim
    t = (pref // mult) * mult
    while t >= mult:
        if dim % t == 0:
            return t
        t -= mult
    return dim


def _sig(x):
    return 1.0 / (1.0 + jnp.exp(-x))


def _params(sem):
    return pltpu.CompilerParams(dimension_semantics=sem, vmem_limit_bytes=VMEM_LIMIT_BYTES)


def _gj(j, i, k):
    return j


def _gk(j, i, k):
    return k


def _wspec(arr, rt, ct, r_of, c_of):
    if arr.ndim == 2:
        return pl.BlockSpec((rt, ct), lambda j, i, k: (r_of(j, i, k), c_of(j, i, k)))
    per = arr.shape[2] // ct
    assert arr.shape[2] % ct == 0
    return pl.BlockSpec((None, rt, ct),
                        lambda j, i, k: (c_of(j, i, k) // per, r_of(j, i, k), c_of(j, i, k) % per))


_HBM = pl.BlockSpec(memory_space=pltpu.HBM)


def _host(job, grid, in_specs, args, out_specs, out_shape, scratch, n_prefetch=0):
    if job is None:
        return {}, (lambda body: body)
    n_in, n_out, n_scr = len(args), len(out_shape), len(scratch)
    n_job, n_alias = len(job.inputs), job.n_alias
    in_specs += [_HBM] * n_job
    args += job.inputs
    out_specs += [_HBM] * n_alias
    out_shape += [jax.ShapeDtypeStruct(a.shape, a.dtype) for a in job.inputs[n_job - n_alias:]]
    scratch += job.sem_shapes
    aliases = {n_prefetch + n_in + n_job - n_alias + t: n_out + t for t in range(n_alias)}

    def wrap(body):
        def hosted(*refs):
            pre, refs = refs[:n_prefetch], refs[n_prefetch:]
            ins, refs = refs[:n_in], refs[n_in:]
            j_in, refs = refs[:n_job], refs[n_job:]
            outs, refs = refs[:n_out], refs[n_out:]
            j_out, refs = refs[:n_alias], refs[n_alias:]
            scr, sems = refs[:n_scr], refs[n_scr:]
            first = functools.reduce(jnp.logical_and, [pl.program_id(d) == 0 for d in range(len(grid))])
            last = functools.reduce(jnp.logical_and, [pl.program_id(d) == grid[d] - 1 for d in range(len(grid))])

            @pl.when(first)
            def _():
                job.start(j_in, j_out, sems)

            body(*pre, *ins, *outs, *scr)

            @pl.when(last)
            def _():
                job.finish(j_in, j_out, sems)

        return hosted

    return aliases, wrap


def _matmul(name, pairs, acc_of, M, N, K, tm, tn, tk, extras, outs, epilogue, job=None):
    assert M % tm == 0 and N % tn == 0 and K % tk == 0, (name, M, N, K, tm, tn, tk)
    nj, ni, nk = N // tn, M // tm, K // tk
    n_pairs, n_extra, n_out = len(pairs), len(extras), len(outs)
    n_acc = max(acc_of) + 1
    in_specs, args, dims = [], [], []
    lhs_of, seen = [], {}
    for a, am, b, bm in pairs:
        if (id(a), am) not in seen:
            seen[id(a), am] = len(args)
            args.append(a)
            if am == 'n':
                in_specs.append(pl.BlockSpec((tm, tk), lambda j, i, k: (i, k)))
            else:
                in_specs.append(pl.BlockSpec((tk, tm), lambda j, i, k: (k, i)))
        lhs_of.append(seen[id(a), am])
    n_lhs = len(args)
    for a, am, b, bm in pairs:
        if bm == 'n':
            in_specs.append(_wspec(b, tk, tn, _gk, _gj))
        else:
            in_specs.append(_wspec(b, tn, tk, _gj, _gk))
        dims.append((((1 if am == 'n' else 0,), (0 if bm == 'n' else 1,)), ((), ())))
        args.append(b)
    for e, kind, off in extras:
        assert off % tn == 0, (name, off, tn)
        o = off // tn
        if kind == 'tile':
            in_specs.append(pl.BlockSpec((tm, tn), lambda j, i, k, o=o: (i, j + o)))
        else:
            in_specs.append(pl.BlockSpec((1, tn), lambda j, i, k, o=o: (0, j + o)))
        args.append(e)
    out_shape, out_specs = [], []
    for dt, nb in outs:
        if nb is None:
            out_shape.append(jax.ShapeDtypeStruct((M, N), dt))
            out_specs.append(pl.BlockSpec((tm, tn), lambda j, i, k: (i, j)))
        else:
            assert nb % tn == 0 and N % nb == 0
            per = nb // tn
            out_shape.append(jax.ShapeDtypeStruct((N // nb, M, nb), dt))
            out_specs.append(pl.BlockSpec((None, tm, tn), lambda j, i, k, per=per: (j // per, i, j % per)))

    def body(*refs):
        n_ab = n_lhs + n_pairs
        e_refs = refs[n_ab:n_ab + n_extra]
        o_refs = refs[n_ab + n_extra:n_ab + n_extra + n_out]
        acc_refs = refs[n_ab + n_extra + n_out:]
        sums = [None] * n_acc
        for p in range(n_pairs):
            prod = lax.dot_general(refs[lhs_of[p]][...], refs[n_lhs + p][...], dims[p], preferred_element_type=F32)
            a = acc_of[p]
            sums[a] = prod if sums[a] is None else sums[a] + prod

        def finish(accs):
            res = epilogue(accs, [e[...] for e in e_refs])
            for o_ref, r in zip(o_refs, res):
                o_ref[...] = r.astype(o_ref.dtype)

        if nk == 1:
            finish(sums)
        else:
            k = pl.program_id(2)

            @pl.when(k == 0)
            def _():
                for a in range(n_acc):
                    acc_refs[a][...] = sums[a]

            @pl.when(k > 0)
            def _():
                for a in range(n_acc):
                    acc_refs[a][...] += sums[a]

            @pl.when(k == nk - 1)
            def _():
                finish([acc_refs[a][...] for a in range(n_acc)])

    scratch = [pltpu.VMEM((tm, tn), F32) for _ in range(n_acc)] if nk > 1 else []
    grid = (nj, ni, nk)
    aliases, wrap = _host(job, grid, in_specs, args, out_specs, out_shape, scratch)
    sem = ("parallel", "parallel", "arbitrary") if job is None else ("arbitrary",) * 3
    return pl.pallas_call(
        wrap(body), name=name, grid=grid, in_specs=in_specs, out_specs=out_specs, out_shape=out_shape,
        scratch_shapes=scratch, input_output_aliases=aliases, compiler_params=_params(sem),
    )(*args)


def _row_spec(tr, d):
    return pl.BlockSpec((tr, d), lambda i: (i, 0))


def _vec_spec(d):
    return pl.BlockSpec((1, d), lambda i: (0, 0))


def _norm_mod(name, x, g, sc, sh):
    S, D = x.shape
    tr = _pick(S, 256, 8)

    def body(x_ref, g_ref, sc_ref, sh_ref, h_ref):
        xv = x_ref[...]
        r = lax.rsqrt(jnp.mean(xv * xv, axis=-1, keepdims=True) + EPS)
        h_ref[...] = ((xv * r) * g_ref[...] * (1.0 + sc_ref[...]) + sh_ref[...]).astype(BF16)

    return pl.pallas_call(
        body, name=name, grid=(S // tr,),
        in_specs=[_row_spec(tr, D), _vec_spec(D), _vec_spec(D), _vec_spec(D)],
        out_specs=_row_spec(tr, D), out_shape=jax.ShapeDtypeStruct((S, D), BF16),
        compiler_params=_params(("parallel",)),
    )(x, g, sc, sh)


def _loss_head(x2, target, g, ff, gt):
    S, D = x2.shape
    tr = _pick(S, 256, 8)

    def body(x_ref, t_ref, g_ref, ff_ref, gt_ref, dx_ref, dffb_ref, loss_ref, dgt_ref, dg_ref):
        i = pl.program_id(0)

        @pl.when(i == 0)
        def _():
            loss_ref[...] = jnp.zeros_like(loss_ref)
            dgt_ref[...] = jnp.zeros_like(dgt_ref)
            dg_ref[...] = jnp.zeros_like(dg_ref)

        xv = x_ref[...]
        gv = g_ref[...]
        r = lax.rsqrt(jnp.mean(xv * xv, axis=-1, keepdims=True) + EPS)
        xh = xv * r
        err = xh * gv - t_ref[...]
        loss_ref[...] += 0.5 * jnp.sum(jnp.mean(err * err, axis=-1, keepdims=True))
        dy = err * (1.0 / D)
        dg_ref[...] += jnp.sum(dy * xh, axis=0, keepdims=True)
        dxh = dy * gv
        dx = r * (dxh - xh * jnp.mean(dxh * xh, axis=-1, keepdims=True))
        dx_ref[...] = dx
        dffb_ref[...] = (dx * gt_ref[...]).astype(BF16)
        dgt_ref[...] += jnp.sum(dx * ff_ref[...], axis=0, keepdims=True)

    return pl.pallas_call(
        body, name="loss_head", grid=(S // tr,),
        in_specs=[_row_spec(tr, D), _row_spec(tr, D), _vec_spec(D), _row_spec(tr, D), _vec_spec(D)],
        out_specs=[_row_spec(tr, D), _row_spec(tr, D), pl.BlockSpec((1, 128), lambda i: (0, 0)),
                   _vec_spec(D), _vec_spec(D)],
        out_shape=[jax.ShapeDtypeStruct((S, D), F32), jax.ShapeDtypeStruct((S, D), BF16),
                   jax.ShapeDtypeStruct((1, 128), F32), jax.ShapeDtypeStruct((1, D), F32),
                   jax.ShapeDtypeStruct((1, D), F32)],
        compiler_params=_params(("arbitrary",)),
    )(x2, target, g, ff, gt)


def _norm_mod_bwd(name, dh, x, g, sc, dres, gated=None):
    S, D = x.shape
    tr = _pick(S, 256, 8)
    n = S // tr
    with_gate = gated is not None

    def body(*refs):
        if with_gate:
            dh_ref, x_ref, g_ref, sc_ref, dres_ref, mo_ref, gt_ref, dx_ref, dsh_ref, dsc_ref, dg_ref, dmob_ref, dgt_ref = refs
        else:
            dh_ref, x_ref, g_ref, sc_ref, dres_ref, dx_ref, dsh_ref, dsc_ref, dg_ref = refs
        i = pl.program_id(0)

        @pl.when(i == 0)
        def _():
            dsh_ref[...] = jnp.zeros_like(dsh_ref)
            dsc_ref[...] = jnp.zeros_like(dsc_ref)
            if with_gate:
                dgt_ref[...] = jnp.zeros_like(dgt_ref)

        xv = x_ref[...]
        dhv = dh_ref[...]
        gv = g_ref[...]
        scale = 1.0 + sc_ref[...]
        r = lax.rsqrt(jnp.mean(xv * xv, axis=-1, keepdims=True) + EPS)
        xh = xv * r
        dsh_ref[...] += jnp.sum(dhv, axis=0, keepdims=True)
        dsc_ref[...] += jnp.sum(dhv * xh, axis=0, keepdims=True)
        dxh = dhv * (gv * scale)
        dx = dres_ref[...] + r * (dxh - xh * jnp.mean(dxh * xh, axis=-1, keepdims=True))
        dx_ref[...] = dx
        if with_gate:
            dmob_ref[...] = (dx * gt_ref[...]).astype(BF16)
            dgt_ref[...] += jnp.sum(dx * mo_ref[...], axis=0, keepdims=True)

        @pl.when(i == n - 1)
        def _():
            t = dsc_ref[...]
            dg_ref[...] = t * scale
            dsc_ref[...] = t * gv

    in_specs = [_row_spec(tr, D), _row_spec(tr, D), _vec_spec(D), _vec_spec(D), _row_spec(tr, D)]
    args = [dh, x, g, sc, dres]
    out_specs = [_row_spec(tr, D), _vec_spec(D), _vec_spec(D), _vec_spec(D)]
    out_shape = [jax.ShapeDtypeStruct((S, D), F32)] + [jax.ShapeDtypeStruct((1, D), F32)] * 3
    if with_gate:
        in_specs += [_row_spec(tr, D), _vec_spec(D)]
        args += list(gated)
        out_specs += [_row_spec(tr, D), _vec_spec(D)]
        out_shape += [jax.ShapeDtypeStruct((S, D), BF16), jax.ShapeDtypeStruct((1, D), F32)]
    return pl.pallas_call(
        body, name=name, grid=(n,), in_specs=in_specs, out_specs=out_specs, out_shape=out_shape,
        compiler_params=_params(("arbitrary",)),
    )(*args)


CONV_ROWS = 256


def _col_spec(S, off_cols):
    o = off_cols // HEAD
    return pl.BlockSpec((S, HEAD), lambda c, o=o: (0, c + o))


def _conv_taps(u, u_prev, rid):
    s1 = jnp.where(rid >= 1, pltpu.roll(u, 1, 0), pltpu.roll(u_prev, 1, 0))
    s2 = jnp.where(rid >= 2, pltpu.roll(u, 2, 0), pltpu.roll(u_prev, 2, 0))
    return s1, s2


def _conv_fwd(proj, conv_w, d_conv):
    S = proj.shape[0]
    R = min(CONV_ROWS, S)
    nr = S // R

    def body(ab_ref, ac_ref, ax_ref, w_ref, ya_ref):
        w0, w1, w2 = w_ref[0:1, :], w_ref[1:2, :], w_ref[2:3, :]
        rid = lax.broadcasted_iota(jnp.int32, (R, HEAD), 0)

        def step(c, carry):
            rows = pl.ds(pl.multiple_of(c * R, R), R)
            prev = pl.ds(pl.multiple_of(jnp.maximum(c - 1, 0) * R, R), R)
            u = ac_ref[rows, :] * ax_ref[rows, :]
            u_prev = jnp.where(c > 0, ac_ref[prev, :] * ax_ref[prev, :], 0.0)
            s1, s2 = _conv_taps(u, u_prev, rid)
            y = w0 * s2 + w1 * s1 + w2 * u
            ya_ref[rows, :] = (ab_ref[rows, :] * y).astype(BF16)
            return carry

        lax.fori_loop(0, nr, step, 0)

    return pl.pallas_call(
        body, name="conv_fwd", grid=(d_conv // HEAD,),
        in_specs=[_col_spec(S, 0), _col_spec(S, d_conv), _col_spec(S, 2 * d_conv),
                  pl.BlockSpec((3, HEAD), lambda c: (0, c))],
        out_specs=pl.BlockSpec((S, HEAD), lambda c: (0, c)),
        out_shape=jax.ShapeDtypeStruct((S, d_conv), BF16),
        compiler_params=_params(("parallel",)),
    )(proj, proj, proj, conv_w)


def _conv_bwd(dya, proj, conv_w, d_conv):
    S = proj.shape[0]
    R = min(CONV_ROWS, S)
    nr = S // R

    def body(dya_ref, ab_ref, ac_ref, ax_ref, w_ref, dab_ref, dac_ref, dax_ref, dw_ref):
        w0, w1, w2 = w_ref[0:1, :], w_ref[1:2, :], w_ref[2:3, :]
        rid = lax.broadcasted_iota(jnp.int32, (R, HEAD), 0)

        def step(c, carry):
            dw0, dw1, dw2 = carry
            rows = pl.ds(pl.multiple_of(c * R, R), R)
            prev = pl.ds(pl.multiple_of(jnp.maximum(c - 1, 0) * R, R), R)
            nxt = pl.ds(pl.multiple_of(jnp.minimum(c + 1, nr - 1) * R, R), R)
            a_c, a_x, a_b = ac_ref[rows, :], ax_ref[rows, :], ab_ref[rows, :]
            u = a_c * a_x
            u_prev = jnp.where(c > 0, ac_ref[prev, :] * ax_ref[prev, :], 0.0)
            s1, s2 = _conv_taps(u, u_prev, rid)
            y = w0 * s2 + w1 * s1 + w2 * u
            dy = dya_ref[rows, :]
            dab_ref[rows, :] = (dy * y).astype(BF16)
            dyc = dy * a_b
            dyc_next = jnp.where(c < nr - 1, dya_ref[nxt, :] * ab_ref[nxt, :], 0.0)
            t1 = jnp.where(rid < R - 1, pltpu.roll(dyc, R - 1, 0), pltpu.roll(dyc_next, R - 1, 0))
            t2 = jnp.where(rid < R - 2, pltpu.roll(dyc, R - 2, 0), pltpu.roll(dyc_next, R - 2, 0))
            du = w2 * dyc + w1 * t1 + w0 * t2
            dac_ref[rows, :] = (du * a_x).astype(BF16)
            dax_ref[rows, :] = (du * a_c).astype(BF16)
            dw0 = dw0 + jnp.sum(dyc * s2, axis=0, keepdims=True)
            dw1 = dw1 + jnp.sum(dyc * s1, axis=0, keepdims=True)
            dw2 = dw2 + jnp.sum(dyc * u, axis=0, keepdims=True)
            return dw0, dw1, dw2

        z = jnp.zeros((1, HEAD), F32)
        dw0, dw1, dw2 = lax.fori_loop(0, nr, step, (z, z, z))
        dw_ref[0:1, :] = dw0
        dw_ref[1:2, :] = dw1
        dw_ref[2:3, :] = dw2

    col = pl.BlockSpec((S, HEAD), lambda c: (0, c))
    return pl.pallas_call(
        body, name="conv_bwd", grid=(d_conv // HEAD,),
        in_specs=[col, _col_spec(S, 0), _col_spec(S, d_conv), _col_spec(S, 2 * d_conv),
                  pl.BlockSpec((3, HEAD), lambda c: (0, c))],
        out_specs=[col, col, col, pl.BlockSpec((3, HEAD), lambda c: (0, c))],
        out_shape=[jax.ShapeDtypeStruct((S, d_conv), BF16)] * 3 + [jax.ShapeDtypeStruct((3, d_conv), F32)],
        compiler_params=_params(("parallel",)),
    )(dya, proj, proj, proj, conv_w)


HGRN_ROWS = 256
HGRN_FWD_SUB = 64
HGRN_BWD_SUB = 32


def _block_cumsum(x, pos):
    for s in (1, 2, 4, 8):
        x = x + jnp.where(pos >= s, pltpu.roll(x, s, 0), 0.0)
    return x


def _block_rev_cumsum(x, pos, rows):
    for s in (1, 2, 4, 8):
        x = x + jnp.where(pos < BLK - s, pltpu.roll(x, rows - s, 0), 0.0)
    return x


def _block_last(x, pos, rows):
    m = jnp.where(pos == BLK - 1, x, 0.0)
    for s in (1, 2, 4, 8):
        m = m + pltpu.roll(m, rows - s, 0)
    return m


def _hgrn_gates(q, z, lb):
    sgq = _sig(q)
    qs = q * sgq
    sg = _sig(z)
    f = lb + (1.0 - lb) * sg
    kk = (1.0 - lb) * (1.0 - sg)
    return sgq, qs, sg, f, kk


def _hgrn_decays(q, z, lb, pos, rows):
    sgq, qs, sg, f, kk = _hgrn_gates(q, z, lb)
    b = _block_cumsum(jnp.log(f), pos)
    return sgq, qs, sg, f, kk, b, _block_last(b, pos, rows)


def _hgrn_specs(T, d_conv, n_t, rev):
    def t_of(i):
        return (n_t - 1 - i) if rev else i

    def pcol(off):
        o = off // HEAD
        return pl.BlockSpec((T, HEAD), lambda h, i, o=o: (t_of(i), h + o))

    q_spec, z_spec, v_spec, g_spec = pcol(3 * d_conv), pcol(4 * d_conv), pcol(5 * d_conv), pcol(6 * d_conv)
    lb_spec = pl.BlockSpec((1, HEAD), lambda h, i: (0, h))
    gn_spec = pl.BlockSpec((1, HEAD), lambda h, i: (0, 0))
    act_spec = pl.BlockSpec((T, HEAD), lambda h, i: (t_of(i), h))
    st_spec = pl.BlockSpec((None, T // BLK, HEAD, HEAD), lambda h, i: (h, t_of(i), 0, 0))
    return q_spec, z_spec, v_spec, g_spec, lb_spec, gn_spec, act_spec, st_spec


def _hgrn_fwd(proj, lb, gn, d_conv, job=None):
    S = proj.shape[0]
    H = d_conv // HEAD
    T = min(HGRN_ROWS, S)
    n_t, nb = S // T, T // BLK
    sub = min(HGRN_FWD_SUB, T)
    q_spec, z_spec, v_spec, g_spec, lb_spec, gn_spec, act_spec, st_spec = _hgrn_specs(T, d_conv, n_t, False)

    def body(q_ref, z_ref, v_ref, g_ref, lb_ref, gn_ref, ob_ref, o_ref, st_ref, state, qe_s, ke_s, v_s, dec_s, o_s, u_s):
        @pl.when(pl.program_id(1) == 0)
        def _():
            state[...] = jnp.zeros_like(state)

        pos = lax.broadcasted_iota(jnp.int32, (sub, HEAD), 0) % BLK
        lbv = lb_ref[...]

        def vector_pass(s, carry):
            r = pl.ds(pl.multiple_of(s * sub, sub), sub)
            v = v_ref[r, :]
            _, qs, _, _, kk, b, b_tot = _hgrn_decays(q_ref[r, :], z_ref[r, :], lbv, pos, sub)
            qe_s[r, :] = (qs * jnp.exp(b)).astype(BF16)
            ke_s[r, :] = (kk * jnp.exp(b_tot - b)).astype(BF16)
            v_s[r, :] = v.astype(BF16)
            dec_s[r, :] = jnp.exp(b_tot)
            o = jnp.sum(qs * kk, axis=-1, keepdims=True) * v
            for d in range(1, BLK):
                e = jnp.exp(b - pltpu.roll(b, d, 0))
                a = jnp.sum(qs * pltpu.roll(kk, d, 0) * e, axis=-1, keepdims=True)
                o = o + jnp.where(pos >= d, a * pltpu.roll(v, d, 0), 0.0)
            o_s[r, :] = o
            return carry

        lax.fori_loop(0, T // sub, vector_pass, 0)

        for j in range(nb):
            rows = pl.ds(j * BLK, BLK)
            u_s[j] = lax.dot_general(v_s[rows, :], ke_s[rows, :], (((0,), (0,)), ((), ())),
                                     preferred_element_type=F32)
        st = state[...]
        for j in range(nb):
            st_ref[j] = st.astype(BF16)
            st = st * dec_s[pl.ds(j * BLK, 1), :] + u_s[j]
        state[...] = st
        for j in range(nb):
            rows = pl.ds(j * BLK, BLK)
            o_s[rows, :] += lax.dot_general(qe_s[rows, :], st_ref[j], (((1,), (1,)), ((), ())),
                                            preferred_element_type=F32)
        o = o_s[...]
        o_ref[...] = o
        r = lax.rsqrt(jnp.mean(o * o, axis=-1, keepdims=True) + EPS)
        g = g_ref[...]
        ob_ref[...] = ((o * r) * gn_ref[...] * (g * _sig(g))).astype(BF16)

    grid = (H, n_t)
    in_specs = [q_spec, z_spec, v_spec, g_spec, lb_spec, gn_spec]
    args = [proj, proj, proj, proj, lb, gn]
    out_specs = [act_spec, act_spec, st_spec]
    out_shape = [jax.ShapeDtypeStruct((S, d_conv), BF16), jax.ShapeDtypeStruct((S, d_conv), F32),
                 jax.ShapeDtypeStruct((H, S // BLK, HEAD, HEAD), BF16)]
    scratch = [pltpu.VMEM((HEAD, HEAD), F32), pltpu.VMEM((T, HEAD), BF16), pltpu.VMEM((T, HEAD), BF16),
               pltpu.VMEM((T, HEAD), BF16), pltpu.VMEM((T, HEAD), F32), pltpu.VMEM((T, HEAD), F32),
               pltpu.VMEM((nb, HEAD, HEAD), F32)]
    aliases, wrap = _host(job, grid, in_specs, args, out_specs, out_shape, scratch)
    return pl.pallas_call(
        wrap(body), name="hgrn_fwd", grid=grid, in_specs=in_specs, out_specs=out_specs, out_shape=out_shape,
        scratch_shapes=scratch, input_output_aliases=aliases,
        compiler_params=_params(("parallel" if job is None else "arbitrary", "arbitrary")),
    )(*args)


def _hgrn_bwd(dob, o_raw, states, proj, lb, gn, d_conv, job=None):
    S = proj.shape[0]
    H = d_conv // HEAD
    T = min(HGRN_ROWS, S)
    n_t, nb = S // T, T // BLK
    sub = min(HGRN_BWD_SUB, T)
    q_spec, z_spec, v_spec, g_spec, lb_spec, gn_spec, act_spec, st_spec = _hgrn_specs(T, d_conv, n_t, True)

    def body(dob_ref, o_ref, st_ref, q_ref, z_ref, v_ref, g_ref, lb_ref, gn_ref,
             dq_ref, dz_ref, dv_ref, dg_ref, dlb_ref, dgn_ref,
             dstate, do_b, qe_b, ke_b, v_b, dec_s, dqe_s, dke_s, dvi_s, ddec_s, do_s, u_s, ds_s):
        @pl.when(pl.program_id(1) == 0)
        def _():
            dstate[...] = jnp.zeros_like(dstate)
            dlb_ref[...] = jnp.zeros_like(dlb_ref)
            dgn_ref[...] = jnp.zeros_like(dgn_ref)

        pos = lax.broadcasted_iota(jnp.int32, (sub, HEAD), 0) % BLK
        lbv, gnv = lb_ref[...], gn_ref[...]

        def before(s, carry):
            r = pl.ds(pl.multiple_of(s * sub, sub), sub)
            v, g = v_ref[r, :], g_ref[r, :]
            _, qs, _, _, kk, b, b_tot = _hgrn_decays(q_ref[r, :], z_ref[r, :], lbv, pos, sub)
            qe_b[r, :] = (qs * jnp.exp(b)).astype(BF16)
            ke_b[r, :] = (kk * jnp.exp(b_tot - b)).astype(BF16)
            v_b[r, :] = v.astype(BF16)
            dec_s[r, :] = jnp.exp(b_tot)
            o = o_ref[r, :]
            rs = lax.rsqrt(jnp.mean(o * o, axis=-1, keepdims=True) + EPS)
            on = o * rs
            sgg = _sig(g)
            dobv = dob_ref[r, :]
            dog = dobv * (g * sgg)
            dgn_ref[...] += jnp.sum(dog * on, axis=0, keepdims=True)
            don = dog * gnv
            do = rs * (don - on * jnp.mean(don * on, axis=-1, keepdims=True))
            dg_ref[r, :] = (dobv * (on * gnv) * (sgg * (1.0 + g * (1.0 - sgg)))).astype(BF16)
            do_s[r, :] = do
            do_b[r, :] = do.astype(BF16)
            return carry

        lax.fori_loop(0, T // sub, before, 0)

        for j in range(nb):
            rows = pl.ds(j * BLK, BLK)
            dob_j = do_b[rows, :]
            u_s[j] = lax.dot_general(dob_j, qe_b[rows, :], (((0,), (0,)), ((), ())), preferred_element_type=F32)
            dqe_s[rows, :] = lax.dot_general(dob_j, st_ref[j], (((1,), (0,)), ((), ())), preferred_element_type=F32)
        dst = dstate[...]
        for j in reversed(range(nb)):
            ds_s[j] = dst.astype(BF16)
            ddec = jnp.sum(st_ref[j].astype(F32) * dst, axis=0, keepdims=True)
            ddec_s[pl.ds(j * BLK, BLK), :] = jnp.broadcast_to(ddec, (BLK, HEAD))
            dst = dst * dec_s[pl.ds(j * BLK, 1), :] + u_s[j]
        dstate[...] = dst
        for j in range(nb):
            rows = pl.ds(j * BLK, BLK)
            dke_s[rows, :] = lax.dot_general(v_b[rows, :], ds_s[j], (((1,), (0,)), ((), ())), preferred_element_type=F32)
            dvi_s[rows, :] = lax.dot_general(ke_b[rows, :], ds_s[j], (((1,), (1,)), ((), ())), preferred_element_type=F32)

        def after(s, carry):
            r = pl.ds(pl.multiple_of(s * sub, sub), sub)
            q, v = q_ref[r, :], v_ref[r, :]
            sgq, qs, sg, f, kk, b, b_tot = _hgrn_decays(q, z_ref[r, :], lbv, pos, sub)
            do = do_s[r, :]
            dqs = dqe_s[r, :] * jnp.exp(b)
            dkk = dke_s[r, :] * jnp.exp(b_tot - b)
            d_tot = jnp.where(pos == BLK - 1, _block_cumsum(dkk * kk, pos) + ddec_s[r, :] * jnp.exp(b_tot), 0.0)
            dv = dvi_s[r, :]
            da = jnp.sum(do * v, axis=-1, keepdims=True)
            dv = dv + jnp.sum(qs * kk, axis=-1, keepdims=True) * do
            dqs = dqs + da * kk
            dkk = dkk + da * qs
            for d in range(1, BLK):
                kd = pltpu.roll(kk, d, 0)
                e = jnp.where(pos >= d, jnp.exp(b - pltpu.roll(b, d, 0)), 0.0)
                a = jnp.sum(qs * kd * e, axis=-1, keepdims=True)
                da = jnp.sum(do * pltpu.roll(v, d, 0), axis=-1, keepdims=True)
                gq = da * e
                dqs = dqs + gq * kd
                dkk = dkk + pltpu.roll(gq * qs, sub - d, 0)
                dv = dv + pltpu.roll(a * do, sub - d, 0)
            db = qs * dqs - kk * dkk + d_tot
            dlf = _block_rev_cumsum(db, pos, sub)
            df = dlf / f - dkk
            dlb_ref[...] += jnp.sum(df * (1.0 - sg), axis=0, keepdims=True)
            dz_ref[r, :] = (df * (1.0 - lbv) * sg * (1.0 - sg)).astype(BF16)
            dq_ref[r, :] = (dqs * (sgq * (1.0 + q * (1.0 - sgq)))).astype(BF16)
            dv_ref[r, :] = dv.astype(BF16)
            return carry

        lax.fori_loop(0, T // sub, after, 0)

    tb = lambda dt: pltpu.VMEM((T, HEAD), dt)
    grid = (H, n_t)
    in_specs = [act_spec, act_spec, st_spec, q_spec, z_spec, v_spec, g_spec, lb_spec, gn_spec]
    args = [dob, o_raw, states, proj, proj, proj, proj, lb, gn]
    out_specs = [act_spec, act_spec, act_spec, act_spec, lb_spec, pl.BlockSpec((None, 1, HEAD), lambda h, i: (h, 0, 0))]
    out_shape = ([jax.ShapeDtypeStruct((S, d_conv), BF16)] * 4
                 + [jax.ShapeDtypeStruct((1, d_conv), F32), jax.ShapeDtypeStruct((H, 1, HEAD), F32)])
    scratch = [pltpu.VMEM((HEAD, HEAD), F32), tb(BF16), tb(BF16), tb(BF16), tb(BF16), tb(F32),
               tb(F32), tb(F32), tb(F32), tb(F32), tb(F32),
               pltpu.VMEM((nb, HEAD, HEAD), F32), pltpu.VMEM((nb, HEAD, HEAD), BF16)]
    aliases, wrap = _host(job, grid, in_specs, args, out_specs, out_shape, scratch)
    return pl.pallas_call(
        wrap(body), name="hgrn_bwd", grid=grid, in_specs=in_specs, out_specs=out_specs, out_shape=out_shape,
        scratch_shapes=scratch, input_output_aliases=aliases,
        compiler_params=_params(("parallel" if job is None else "arbitrary", "arbitrary")),
    )(*args)


def _first(accs, extras):
    return [accs[0]]


def _local_step(x, target, mod, norm_mix_g, lb, gnorm_g, norm_ffn_g, norm_final_g, conv_w, bufs, c_idx, order):
    S, D = x.shape
    d_conv = D // 2
    d_in = 7 * d_conv + 2 * D
    d_ff = N_CHIPS * bufs["w_ffn_down"].shape[1]
    nb_in, nb_co, nb_ff = bufs["w_in"].shape[2], bufs["w_conv_out"].shape[2], bufs["w_ffn_gate"].shape[2]
    rows = lambda g: g.reshape(-1, g.shape[2])
    sh_m, sc_m, gt_m, sh_f, sc_f, gt_f = [mod[:, i * D:(i + 1) * D] for i in range(6)]
    tm = _pick(S, 512, 16)
    tm2 = _pick(S, 1024, 16)
    ts = _pick(S, 1024, 128)
    tn_in = _pick(nb_in, 1408, 128)
    tn_co = _pick(nb_co, 512, 128)
    tn_ff = _pick(nb_ff, 1408, 128)
    tn_d = _pick(D, 512, 128)
    tw = _pick(D, 1024, 128)
    ts2 = _pick(S, 2048, 128)

    h = _norm_mod("norm_mix", x, norm_mix_g, sc_m, sh_m)
    proj, w_in, *got = _proj_gather(h, bufs["w_in"], order,
                                    _GatherJob([bufs[k] for k in ("w_conv_out", "w_hgrn_out", "w_o")]))
    w_conv_out, w_hgrn_out, w_o = _forward_halves("forward_branch", got)
    w_o = rows(w_o)
    ob, o_raw, states, *got = _hgrn_fwd(proj, lb, gnorm_g, d_conv,
                                        job=_GatherJob([bufs[k] for k in ("w_ffn_gate", "w_ffn_up")]))
    ya = _conv_fwd(proj, conv_w, d_conv)
    w_ffn_gate, w_ffn_up = _forward_halves("forward_ffn", got)

    def merge_epi(accs, ex):
        y_a, y_b = accs
        return [y_a, y_b, _sig(ex[0]) * y_a + _sig(ex[1]) * y_b]

    y_a, y_b, merged = _matmul(
        "branch_out", [(ya, 'n', w_conv_out, 'n'), (ob, 'n', w_hgrn_out, 'n')], [0, 1], S, D, d_conv, tm2, tn_co, d_conv,
        [(proj, 'tile', 7 * d_conv), (proj, 'tile', 7 * d_conv + D)], [(F32, None), (F32, None), (BF16, None)], merge_epi)

    def resid_epi(accs, ex):
        return [accs[0], ex[0] + ex[1] * accs[0]]

    mo, x1 = _matmul("w_o", [(merged, 'n', w_o, 'n')], [0], S, D, D, tm2, tw, D,
                     [(x, 'tile', 0), (gt_m, 'row', 0)], [(F32, None), (F32, None)], resid_epi)
    h2 = _norm_mod("norm_ffn", x1, norm_ffn_g, sc_f, sh_f)

    def swiglu_epi(accs, ex):
        gg, uu = accs
        return [gg, uu, gg * _sig(gg) * uu]

    G, U, act, w_ffn_down = _matmul(
        "ffn_in", [(h2, 'n', w_ffn_gate, 'n'), (h2, 'n', w_ffn_up, 'n')], [0, 1], S, d_ff, D,
        tm, tn_ff, D, [], [(BF16, None), (BF16, None), (BF16, None)], swiglu_epi, job=_GatherJob([bufs["w_ffn_down"]]))
    w_ffn_down = rows(_forward_halves("forward_down", [w_ffn_down])[0])
    ff, x2 = _matmul("ffn_out", [(act, 'n', w_ffn_down, 'n')], [0], S, D, d_ff, tm, D, _pick(d_ff, 1408, 128),
                     [(x1, 'tile', 0), (gt_f, 'row', 0)], [(F32, None), (F32, None)], resid_epi)

    dx2, dffb, loss, dgt_f, dg_final = _loss_head(x2, target, norm_final_g, ff, gt_f)

    def swiglu_bwd_epi(accs, ex):
        dact, gg, uu = accs[0], ex[0], ex[1]
        s = _sig(gg)
        return [dact * uu * (s * (1.0 + gg * (1.0 - s))), dact * (gg * s)]

    dG, dU = _matmul("d_act", [(dffb, 'n', w_ffn_down, 't')], [0], S, d_ff, D, tm2, tn_ff, D,
                     [(G, 'tile', 0), (U, 'tile', 0)], [(BF16, None), (BF16, None)], swiglu_bwd_epi)
    dw_down, = _matmul("dw_ffn_down", [(act, 't', dffb, 'n')], [0], d_ff, D, S, _pick(d_ff, 1408, 128),
                       D, ts, [], [(BF16, None)], _first)
    dh2, = _matmul("d_h2", [(dG, 'n', w_ffn_gate, 't'), (dU, 'n', w_ffn_up, 't')], [0, 0], S, D, d_ff,
                   tm, D, tn_ff, [], [(F32, None)], _first)
    dw_gate, dw_up = _matmul("dw_ffn_in", [(h2, 't', dG, 'n'), (h2, 't', dU, 'n')], [0, 1], D, d_ff, S,
                             tw, tn_ff, ts, [], [(BF16, nb_ff), (BF16, nb_ff)],
                             lambda accs, ex: accs)

    def pre_sum(tag, names, grads):
        from_sibling = _swap_halves("swap_" + tag, grads)
        pairs = [_pair_sum("pair_sum_" + k, g, q, c_idx) for k, g, q in zip(names, grads, from_sibling)]
        return [p[0] for p in pairs], [p[1] for p in pairs]

    ffn_names = ["w_ffn_down", "w_ffn_gate", "w_ffn_up"]
    parts_f, slots_f = pre_sum("ffn", ffn_names, [dw_down.reshape(N_CHIPS, -1, D), dw_gate, dw_up])
    dx1, dsh_f, dsc_f, dg_ffn, dmob, dgt_m = _norm_mod_bwd("norm_ffn_bwd", dh2, x1, norm_ffn_g, sc_f, dx2, (mo, gt_m))

    def merge_bwd_epi(accs, ex):
        dm, ga, gb, ya_, yb_ = accs[0], ex[0], ex[1], ex[2], ex[3]
        sa, sb = _sig(ga), _sig(gb)
        return [dm * ya_ * sa * (1.0 - sa), dm * yb_ * sb * (1.0 - sb), dm * sa, dm * sb]

    dga, dgb, dy_a, dy_b = _matmul(
        "d_merged", [(dmob, 'n', w_o, 't')], [0], S, D, D, tm2, tn_co, D,
        [(proj, 'tile', 7 * d_conv), (proj, 'tile', 7 * d_conv + D), (y_a, 'tile', 0), (y_b, 'tile', 0)],
        [(BF16, None)] * 4, merge_bwd_epi)
    dw_o, = _matmul("dw_o", [(merged, 't', dmob, 'n')], [0], D, D, S, _pick(D, 512, 128), _pick(D, 1024, 128), ts,
                    [], [(BF16, None)], _first)
    both = lambda accs, ex: accs
    dya, dob = _matmul("d_branch", [(dy_a, 'n', w_conv_out, 't'), (dy_b, 'n', w_hgrn_out, 't')], [0, 1], S, d_conv, D,
                       tm2, _pick(d_conv, 1024, 128), tn_co, [], [(F32, None), (F32, None)], both)
    dw_co, dw_ho = _matmul("dw_branch", [(ya, 't', dy_a, 'n'), (ob, 't', dy_b, 'n')], [0, 1], d_conv, D, S,
                           _pick(d_conv, 1024, 128), tn_co, ts, [], [(BF16, nb_co), (BF16, nb_co)], both)
    branch_names = ["w_conv_out", "w_hgrn_out", "w_o"]
    parts_b, slots_b = pre_sum("branch", branch_names, [dw_co, dw_ho, dw_o.reshape(N_CHIPS, -1, D)])
    dab, dac, dax, dconv_w = _conv_bwd(dya, proj, conv_w, d_conv)
    dq, dz, dv, dgo, dlb, dgn, *arrived = _hgrn_bwd(dob, o_raw, states, proj, lb, gnorm_g, d_conv,
                                                    job=_ScatterJob(parts_f + parts_b, slots_f + slots_b))
    dproj = jnp.concatenate([dab, dac, dax, dq, dz, dv, dgo, dga, dgb], axis=1)
    dw_in, = _matmul("dw_in", [(h, 't', dproj, 'n')], [0], D, d_in, S, tw, tn_in, ts2,
                     [], [(BF16, nb_in)], _first)
    parts_i, slots_i = pre_sum("in", ["w_in"], [dw_in])
    dh, arrived_in = _matmul("d_h", [(dproj, 'n', w_in, 't')], [0], S, D, d_in, tm2, D, tn_in, [], [(F32, None)], _first,
                             job=_ScatterJob(parts_i, slots_i))
    dx, dsh_m, dsc_m, dg_mix = _norm_mod_bwd("norm_mix_bwd", dh, x, norm_mix_g, sc_m, dx1)
    dmod = jnp.concatenate([dsh_m, dsc_m, dgt_m, dsh_f, dsc_f, dgt_f], axis=1)
    small = dict(dmod=dmod, dg_mix=dg_mix, dg_ffn=dg_ffn, dg_final=dg_final, dlb=dlb,
                 dgn=jnp.sum(dgn, axis=0), dconv_w=dconv_w)
    big = dict(zip(ffn_names + branch_names, arrived), w_in=arrived_in)
    return loss, dx, small, big


def _adamw_math(w, g, m, v):
    m = ADAM_B1 * m + (1.0 - ADAM_B1) * g
    v = ADAM_B2 * v + (1.0 - ADAM_B2) * (g * g)
    m_hat = m / (1.0 - ADAM_B1 ** ADAM_STEP)
    v_hat = v / (1.0 - ADAM_B2 ** ADAM_STEP)
    delta = -ADAM_LR * (m_hat / (jnp.sqrt(v_hat) + ADAM_EPS) + ADAM_WD * w)
    return delta, m, v


def _adamw(name, w, g, m, v):
    R, C = w.shape
    tr = _pick(R, max(8, (256 * 1024) // C // 8 * 8), 8)
    spec = pl.BlockSpec((tr, C), lambda i: (i, 0))

    def body(w_ref, g_ref, m_ref, v_ref, d_ref, mo_ref, vo_ref):
        d, m2, v2 = _adamw_math(w_ref[...], g_ref[...], m_ref[...], v_ref[...])
        d_ref[...] = d
        mo_ref[...] = m2
        vo_ref[...] = v2

    return pl.pallas_call(
        body, name=name, grid=(R // tr,), in_specs=[spec] * 4, out_specs=[spec] * 3,
        out_shape=[jax.ShapeDtypeStruct((R, C), F32)] * 3, compiler_params=_params(("parallel",)),
    )(w, g, m, v)


def _ada_update(c_act_t, dmod, w, m, v):
    R, C = w.shape
    B = dmod.shape[0]
    tr = _pick(R, 256, 8)
    tc = _pick(C, 1536, 128)
    spec = pl.BlockSpec((tr, tc), lambda i, j: (i, j))

    def body(ct_ref, dm_ref, w_ref, m_ref, v_ref, g_ref, d_ref, mo_ref, vo_ref):
        ct = ct_ref[...]
        dm = dm_ref[...]
        g = ct[:, 0:1] * dm[0:1, :]
        for b in range(1, B):
            g = g + ct[:, b:b + 1] * dm[b:b + 1, :]
        d, m2, v2 = _adamw_math(w_ref[...], g, m_ref[...], v_ref[...])
        g_ref[...] = g
        d_ref[...] = d
        mo_ref[...] = m2
        vo_ref[...] = v2

    return pl.pallas_call(
        body, name="ada_update", grid=(R // tr, C // tc),
        in_specs=[pl.BlockSpec((tr, B), lambda i, j: (i, 0)), pl.BlockSpec((B, tc), lambda i, j: (0, j)),
                  spec, spec, spec],
        out_specs=[spec] * 4, out_shape=[jax.ShapeDtypeStruct((R, C), F32)] * 4,
        compiler_params=_params(("parallel", "parallel")),
    )(c_act_t, dmod, w, m, v)


def _prep(c_all, lb_param):
    def body(c_ref, p_ref, ca_ref, cab_ref, lb_ref):
        cv = c_ref[...]
        ca = cv * _sig(cv)
        ca_ref[...] = ca
        cab_ref[...] = ca.astype(BF16)
        lb_ref[...] = _sig(p_ref[0:1, :] - p_ref[1:2, :])

    return pl.pallas_call(
        body, name="prep",
        out_shape=[jax.ShapeDtypeStruct(c_all.shape, F32), jax.ShapeDtypeStruct(c_all.shape, BF16),
                   jax.ShapeDtypeStruct((1, lb_param.shape[1]), F32)],
    )(c_all, lb_param)


def _small_reduce(parts):
    W = parts.shape[1]

    def body(p_ref, o_ref):
        acc = p_ref[0:1, :]
        for d in range(1, N_DEV):
            acc = acc + p_ref[d:d + 1, :]
        o_ref[...] = acc

    return pl.pallas_call(body, name="small_reduce", out_shape=jax.ShapeDtypeStruct((1, W), F32))(parts)


def _lb_grad(dlb, lb):
    def body(d_ref, lb_ref, o_ref):
        t = d_ref[...] * lb_ref[...] * (1.0 - lb_ref[...])
        o_ref[0:1, :] = t
        o_ref[1:2, :] = -t

    return pl.pallas_call(body, name="lb_grad", out_shape=jax.ShapeDtypeStruct((2, dlb.shape[1]), F32))(dlb, lb)


def _place():
    x, y, c = lax.axis_index("x"), lax.axis_index("y"), lax.axis_index("c")
    chips = [(1 - x, y), (x, 1 - y), (1 - x, 1 - y)]
    return x, y, c, chips


def _allgather_rows(name, v):
    m_per, n = v.shape

    def body(x_ref, out_ref, send_sems, recv_sems, local_sem):
        x, y, c, chips = _place()
        me, sibling = (x, y, c), (x, y, 1 - c)

        def rows(px, py, pc):
            return out_ref.at[pl.ds((4 * px + 2 * py + pc) * m_per, m_per), :]

        def copy(k, block, to, src=None):
            return pltpu.make_async_remote_copy(
                src_ref=rows(*block) if src is None else src, dst_ref=rows(*block),
                send_sem=send_sems.at[k], recv_sem=recv_sems.at[k], device_id=to, device_id_type=MESH)

        mine = pltpu.make_async_copy(x_ref, rows(*me), local_sem)
        mine.start()
        first = [copy(0, me, sibling, src=x_ref)]
        first += [copy(1 + j, me, (*chip, c), src=x_ref) for j, chip in enumerate(chips)]
        for cp in first:
            cp.start()
        passed = [copy(4 + j, (*chip, c), sibling) for j, chip in enumerate(chips)]
        for j, chip in enumerate(chips):
            copy(1 + j, (*chip, c), me).wait_recv()
            passed[j].start()
        copy(0, sibling, me).wait_recv()
        for j, chip in enumerate(chips):
            copy(4 + j, (*chip, 1 - c), me).wait_recv()
        for cp in first + passed:
            cp.wait_send()
        mine.wait()

    return pl.pallas_call(
        body, name=name, out_shape=jax.ShapeDtypeStruct((N_DEV * m_per, n), v.dtype),
        in_specs=[pl.BlockSpec(memory_space=pltpu.VMEM)], out_specs=pl.BlockSpec(memory_space=pltpu.VMEM),
        scratch_shapes=[pltpu.SemaphoreType.DMA((7,)), pltpu.SemaphoreType.DMA((7,)), pltpu.SemaphoreType.DMA],
    )(v)


def _cast_place(name, w, k_idx):
    R, C = w.shape
    tr = _pick(R, max(16, (512 * 1024) // C // 16 * 16), 16)

    def body(k_ref, w_ref, o_ref):
        o_ref[...] = w_ref[...].astype(BF16)

    return pl.pallas_call(
        body, name=name,
        grid_spec=pltpu.PrefetchScalarGridSpec(
            num_scalar_prefetch=1, grid=(R // tr,),
            in_specs=[pl.BlockSpec((tr, C), lambda i, k_ref: (i, 0))],
            out_specs=pl.BlockSpec((None, tr, C), lambda i, k_ref: (k_ref[0], i, 0))),
        out_shape=jax.ShapeDtypeStruct((N_CHIPS, R, C), BF16),
        compiler_params=_params(("parallel",)),
    )(k_idx, w)


def _gather_weights(bufs):
    n = len(bufs)

    def body(*refs):
        out_refs = refs[n:2 * n]
        send_sems, recv_sems = refs[2 * n:]
        x, y, c, chips = _place()
        k_me = 2 * x + y
        sibling = (x, y, 1 - c)

        def region(a, k, hc):
            rh = bufs[a].shape[1] // 2
            return out_refs[a].at[k, pl.ds(hc * rh, rh)]

        def copy(a, s, src, dst, to):
            return pltpu.make_async_remote_copy(src_ref=src, dst_ref=dst, send_sem=send_sems.at[6 * a + s],
                                                recv_sem=recv_sems.at[6 * a + s], device_id=to, device_id_type=MESH)

        started = []
        for a in range(n):
            mine = region(a, k_me, c)
            for j, chip in enumerate(chips):
                cp = copy(a, j, mine, mine, (*chip, c))
                cp.start()
                started.append(cp)
        for a in range(n):
            for j, chip in enumerate(chips):
                got = region(a, 2 * chip[0] + chip[1], c)
                copy(a, j, got, got, (x, y, c)).wait_recv()
                cp = copy(a, 3 + j, got, got, sibling)
                cp.start()
                started.append(cp)
        for a in range(n):
            for j, chip in enumerate(chips):
                got = region(a, 2 * chip[0] + chip[1], 1 - c)
                copy(a, 3 + j, got, got, (x, y, c)).wait_recv()
        for cp in started:
            cp.wait_send()

    return pl.pallas_call(
        body, name="gather_weights",
        out_shape=[jax.ShapeDtypeStruct(b.shape, b.dtype) for b in bufs],
        in_specs=[_HBM] * n, out_specs=[_HBM] * n, input_output_aliases={a: a for a in range(n)},
        scratch_shapes=[pltpu.SemaphoreType.DMA((6 * n,)), pltpu.SemaphoreType.DMA((6 * n,))],
    )(*bufs)


def _chip_copies(src_of, dst_of, got_of, n, sems, receiving):
    x, y, c, chips = _place()
    k_me = 2 * x + y
    send_sems, recv_sems = sems
    out = []
    for a in range(n):
        for j, chip in enumerate(chips):
            k_chip = 2 * chip[0] + chip[1]
            if receiving:
                src = dst = got_of(a, k_chip, c)
                to = (x, y, c)
            else:
                src, dst, to = src_of(a, k_chip, k_me, c), dst_of(a, k_me, c), (*chip, c)
            out.append(pltpu.make_async_remote_copy(
                src_ref=src, dst_ref=dst, send_sem=send_sems.at[3 * a + j], recv_sem=recv_sems.at[3 * a + j],
                device_id=to, device_id_type=MESH))
    return out


class _ChipJob:
    def start(self, j_in, j_out, sems):
        for send in self.copies(j_in, j_out, sems, False):
            send.start()

    def finish(self, j_in, j_out, sems):
        for recv in self.copies(j_in, j_out, sems, True):
            recv.wait_recv()
        for send in self.copies(j_in, j_out, sems, False):
            send.wait_send()


class _GatherJob(_ChipJob):
    def __init__(self, bufs):
        n = len(bufs)
        self.inputs, self.n_alias = list(bufs), n
        self.sem_shapes = [pltpu.SemaphoreType.DMA((3 * n,)), pltpu.SemaphoreType.DMA((3 * n,))]
        self.half = [b.shape[1] // 2 for b in bufs]

    def copies(self, j_in, j_out, sems, receiving):
        def half(a, k, c):
            return j_out[a].at[k, pl.ds(c * self.half[a], self.half[a])]

        return _chip_copies(lambda a, k_chip, k_me, c: half(a, k_me, c), half, half, len(self.half), sems, receiving)


class _ScatterJob(_ChipJob):
    def __init__(self, parts, slots):
        n = len(parts)
        self.n = n
        self.inputs, self.n_alias = list(parts) + list(slots), n
        self.sem_shapes = [pltpu.SemaphoreType.DMA((3 * n,)), pltpu.SemaphoreType.DMA((3 * n,))]

    def copies(self, j_in, j_out, sems, receiving):
        return _chip_copies(lambda a, k_chip, k_me, c: j_in[a].at[k_chip], lambda a, k_me, c: j_out[a].at[k_me],
                            lambda a, k_chip, c: j_out[a].at[k_chip], self.n, sems, receiving)


def _forward_halves(name, bufs):
    n = len(bufs)

    def body(*refs):
        out_refs = refs[n:2 * n]
        send_sems, recv_sems = refs[2 * n:]
        x, y, c, chips = _place()
        started, waits = [], []
        for a in range(n):
            rh = bufs[a].shape[1] // 2
            for j, chip in enumerate(chips):
                k_chip = 2 * chip[0] + chip[1]
                got = out_refs[a].at[k_chip, pl.ds(c * rh, rh)]
                cp = pltpu.make_async_remote_copy(src_ref=got, dst_ref=got, send_sem=send_sems.at[3 * a + j],
                                                  recv_sem=recv_sems.at[3 * a + j], device_id=(x, y, 1 - c),
                                                  device_id_type=MESH)
                cp.start()
                started.append(cp)
                other = out_refs[a].at[k_chip, pl.ds((1 - c) * rh, rh)]
                waits.append(pltpu.make_async_remote_copy(
                    src_ref=other, dst_ref=other, send_sem=send_sems.at[3 * a + j], recv_sem=recv_sems.at[3 * a + j],
                    device_id=(x, y, c), device_id_type=MESH))
        for cp in waits:
            cp.wait_recv()
        for cp in started:
            cp.wait_send()

    return pl.pallas_call(
        body, name=name, out_shape=[jax.ShapeDtypeStruct(b.shape, b.dtype) for b in bufs],
        in_specs=[_HBM] * n, out_specs=[_HBM] * n, input_output_aliases={a: a for a in range(n)},
        scratch_shapes=[pltpu.SemaphoreType.DMA((3 * n,)), pltpu.SemaphoreType.DMA((3 * n,))],
    )(*bufs)


def _proj_gather(h, buf, order, job):
    S, D = h.shape
    nb = buf.shape[2]
    tm = _pick(S, 1024, 16)
    tn = _pick(nb, 1408, 128)
    per = nb // tn
    nj, ni = N_CHIPS * per, S // tm
    rh = D // 2

    def body(order_ref, h_ref, buf_in, proj_ref, buf_ref, wbuf, tile_sems, ici_send, ici_recv, d2d_send, d2d_recv):
        j, i = pl.program_id(0), pl.program_id(1)
        x, y, c, chips = _place()
        k_me = 2 * x + y

        def half(k, hc):
            return buf_ref.at[k, pl.ds(hc * rh, rh)]

        def ici(q, receiving):
            k_chip = 2 * chips[q][0] + chips[q][1]
            src, to = (half(k_chip, c), (x, y, c)) if receiving else (half(k_me, c), (*chips[q], c))
            return pltpu.make_async_remote_copy(src_ref=src, dst_ref=src, send_sem=ici_send.at[q],
                                                recv_sem=ici_recv.at[q], device_id=to, device_id_type=MESH)

        def d2d(q, receiving):
            k_chip = 2 * chips[q][0] + chips[q][1]
            src, to = (half(k_chip, 1 - c), (x, y, c)) if receiving else (half(k_chip, c), (x, y, 1 - c))
            return pltpu.make_async_remote_copy(src_ref=src, dst_ref=src, send_sem=d2d_send.at[q],
                                                recv_sem=d2d_recv.at[q], device_id=to, device_id_type=MESH)

        def tile_copy(t):
            col = pl.multiple_of((t % per) * tn, 128)
            return pltpu.make_async_copy(buf_ref.at[order_ref[t // per], :, pl.ds(col, tn)], wbuf.at[t % 2],
                                         tile_sems.at[t % 2])

        @pl.when(jnp.logical_and(j == 0, i == 0))
        def _():
            for q in range(3):
                ici(q, False).start()
            tile_copy(0).start()

        @pl.when(i == 0)
        def _():
            tile_copy(j).wait()
            for q in range(3):
                @pl.when(j + 1 == (q + 1) * per)
                def _():
                    ici(q, True).wait_recv()
                    d2d(q, False).start()
                    d2d(q, True).wait_recv()

            @pl.when(j + 1 < nj)
            def _():
                tile_copy(j + 1).start()

        proj_ref[...] = jnp.dot(h_ref[...], wbuf[j % 2], preferred_element_type=F32)

        @pl.when(jnp.logical_and(j == nj - 1, i == ni - 1))
        def _():
            for q in range(3):
                ici(q, False).wait_send()
                d2d(q, False).wait_send()

    grid = (nj, ni)
    in_specs = [pl.BlockSpec((tm, D), lambda j, i, o: (i, 0)), _HBM]
    args = [h, buf]
    out_specs = [pl.BlockSpec((tm, tn), lambda j, i, o: (i, o[j // per] * per + j % per)), _HBM]
    out_shape = [jax.ShapeDtypeStruct((S, N_CHIPS * nb), F32), jax.ShapeDtypeStruct(buf.shape, buf.dtype)]
    scratch = [pltpu.VMEM((2, D, tn), BF16), pltpu.SemaphoreType.DMA((2,))] + [pltpu.SemaphoreType.DMA((3,))] * 4
    aliases, wrap = _host(job, grid, in_specs, args, out_specs, out_shape, scratch, n_prefetch=1)
    aliases[2] = 1
    return pl.pallas_call(
        wrap(body), name="proj",
        grid_spec=pltpu.PrefetchScalarGridSpec(num_scalar_prefetch=1, grid=grid, in_specs=in_specs, out_specs=out_specs,
                                               scratch_shapes=scratch),
        out_shape=out_shape, input_output_aliases=aliases, compiler_params=_params(("arbitrary", "arbitrary")),
    )(order, *args)


def _swap_halves(name, grads):
    n = len(grads)

    def body(*refs):
        in_refs, out_refs = refs[:n], refs[n:2 * n]
        send_sems, recv_sems = refs[2 * n:]
        x, y, c, _ = _place()
        cps = []
        for a in range(n):
            rh = grads[a].shape[1] // 2
            cp = pltpu.make_async_remote_copy(
                src_ref=in_refs[a].at[:, pl.ds((1 - c) * rh, rh)], dst_ref=out_refs[a],
                send_sem=send_sems.at[a], recv_sem=recv_sems.at[a], device_id=(x, y, 1 - c), device_id_type=MESH)
            cp.start()
            cps.append(cp)
        for cp in cps:
            cp.wait()

    return pl.pallas_call(
        body, name=name,
        out_shape=[jax.ShapeDtypeStruct((N_CHIPS, g.shape[1] // 2, g.shape[2]), g.dtype) for g in grads],
        in_specs=[_HBM] * n, out_specs=[_HBM] * n,
        scratch_shapes=[pltpu.SemaphoreType.DMA((n,)), pltpu.SemaphoreType.DMA((n,))],
    )(*grads)


def _pair_sum(name, g, q, c_idx):
    _, R, C = g.shape
    rh = R // 2
    tr = _pick(rh, max(16, (512 * 1024) // C // 16 * 16), 16)
    nh = rh // tr

    def body(c_ref, g_ref, q_ref, o_ref, o2_ref):
        s = (g_ref[...].astype(F32) + q_ref[...].astype(F32)).astype(BF16)
        o_ref[...] = s
        o2_ref[...] = s

    out_spec = pl.BlockSpec((None, tr, C), lambda k, i, c_ref: (k, i, 0))
    return pl.pallas_call(
        body, name=name,
        grid_spec=pltpu.PrefetchScalarGridSpec(
            num_scalar_prefetch=1, grid=(N_CHIPS, nh),
            in_specs=[pl.BlockSpec((None, tr, C), lambda k, i, c_ref: (k, c_ref[0] * nh + i, 0)),
                      pl.BlockSpec((None, tr, C), lambda k, i, c_ref: (k, i, 0))],
            out_specs=[out_spec, out_spec]),
        out_shape=[jax.ShapeDtypeStruct((N_CHIPS, rh, C), BF16)] * 2,
        compiler_params=_params(("parallel", "parallel")),
    )(c_idx, g, q)


def _sum_chips(name, r, c_idx):
    _, rh, C = r.shape
    tr = _pick(rh, max(16, (256 * 1024) // C // 16 * 16), 16)

    def body(c_ref, r_ref, o_ref):
        acc = r_ref[0].astype(F32)
        for k in range(1, N_CHIPS):
            acc = acc + r_ref[k].astype(F32)
        o_ref[...] = acc

    return pl.pallas_call(
        body, name=name,
        grid_spec=pltpu.PrefetchScalarGridSpec(
            num_scalar_prefetch=1, grid=(rh // tr,),
            in_specs=[pl.BlockSpec((N_CHIPS, tr, C), lambda i, c_ref: (0, i, 0))],
            out_specs=pl.BlockSpec((None, tr, C), lambda i, c_ref: (c_ref[0], i, 0))),
        out_shape=jax.ShapeDtypeStruct((2, rh, C), F32), compiler_params=_params(("parallel",)),
    )(c_idx, r)


def _share_halves(bufs):
    n = len(bufs)

    def body(*refs):
        out_refs = refs[n:2 * n]
        send_sems, recv_sems = refs[2 * n:]
        x, y, c, _ = _place()
        cps = []
        for a in range(n):
            cp = pltpu.make_async_remote_copy(
                src_ref=out_refs[a].at[c], dst_ref=out_refs[a].at[c], send_sem=send_sems.at[a],
                recv_sem=recv_sems.at[a], device_id=(x, y, 1 - c), device_id_type=MESH)
            cp.start()
            cps.append(cp)
        for a, cp in enumerate(cps):
            got = out_refs[a].at[1 - c]
            pltpu.make_async_remote_copy(src_ref=got, dst_ref=got, send_sem=send_sems.at[a], recv_sem=recv_sems.at[a],
                                         device_id=(x, y, c), device_id_type=MESH).wait_recv()
        for cp in cps:
            cp.wait_send()

    return pl.pallas_call(
        body, name="share_halves",
        out_shape=[jax.ShapeDtypeStruct(b.shape, b.dtype) for b in bufs],
        in_specs=[_HBM] * n, out_specs=[_HBM] * n, input_output_aliases={a: a for a in range(n)},
        scratch_shapes=[pltpu.SemaphoreType.DMA((n,)), pltpu.SemaphoreType.DMA((n,))],
    )(*bufs)


_BIG = ("w_in", "w_conv_out", "w_hgrn_out", "w_o", "w_ffn_gate", "w_ffn_up", "w_ffn_down")
_ROW_SHARDED = ("w_o", "w_ffn_down")
_WEIGHTS = ("w_ada", "b_ada", "norm_mix_g", "w_in", "conv_w", "lb_param", "gnorm_g", "w_conv_out", "w_hgrn_out",
            "w_o", "norm_ffn_g", "w_ffn_gate", "w_ffn_up", "w_ffn_down", "norm_final_g")


def _pack_rows(pieces):
    flat = jnp.concatenate([p.reshape(-1) for p in pieces])
    pad = (-flat.shape[0]) % 1024
    return jnp.pad(flat, (0, pad)).reshape(8, -1)


def kernel(x, c, w_ada, b_ada, norm_mix_g, w_in, conv_w, lb_param, gnorm_g, w_conv_out, w_hgrn_out, w_o, norm_ffn_g, w_ffn_gate, w_ffn_up, w_ffn_down, norm_final_g, loss_target, m_w_ada, m_b_ada, m_norm_mix_g, m_w_in, m_conv_w, m_lb_param, m_gnorm_g, m_w_conv_out, m_w_hgrn_out, m_w_o, m_norm_ffn_g, m_w_ffn_gate, m_w_ffn_up, m_w_ffn_down, m_norm_final_g, v_w_ada, v_b_ada, v_norm_mix_g, v_w_in, v_conv_w, v_lb_param, v_gnorm_g, v_w_conv_out, v_w_hgrn_out, v_w_o, v_norm_ffn_g, v_w_ffn_gate, v_w_ffn_up, v_w_ffn_down, v_norm_final_g):
    w = dict(w_ada=w_ada, b_ada=b_ada, norm_mix_g=norm_mix_g, w_in=w_in, conv_w=conv_w, lb_param=lb_param,
             gnorm_g=gnorm_g, w_conv_out=w_conv_out, w_hgrn_out=w_hgrn_out, w_o=w_o, norm_ffn_g=norm_ffn_g,
             w_ffn_gate=w_ffn_gate, w_ffn_up=w_ffn_up, w_ffn_down=w_ffn_down, norm_final_g=norm_final_g)
    m = dict(w_ada=m_w_ada, b_ada=m_b_ada, norm_mix_g=m_norm_mix_g, w_in=m_w_in, conv_w=m_conv_w, lb_param=m_lb_param,
             gnorm_g=m_gnorm_g, w_conv_out=m_w_conv_out, w_hgrn_out=m_w_hgrn_out, w_o=m_w_o, norm_ffn_g=m_norm_ffn_g,
             w_ffn_gate=m_w_ffn_gate, w_ffn_up=m_w_ffn_up, w_ffn_down=m_w_ffn_down, norm_final_g=m_norm_final_g)
    v = dict(w_ada=v_w_ada, b_ada=v_b_ada, norm_mix_g=v_norm_mix_g, w_in=v_w_in, conv_w=v_conv_w, lb_param=v_lb_param,
             gnorm_g=v_gnorm_g, w_conv_out=v_w_conv_out, w_hgrn_out=v_w_hgrn_out, w_o=v_w_o, norm_ffn_g=v_norm_ffn_g,
             w_ffn_gate=v_w_ffn_gate, w_ffn_up=v_w_ffn_up, w_ffn_down=v_w_ffn_down, norm_final_g=v_norm_final_g)
    two_d = lambda a: a.reshape((-1, a.shape[-1])) if a.ndim != 2 else a
    shapes = {k: a.shape for k, a in w.items()}
    w, m, v = ({k: two_d(a) for k, a in t.items()} for t in (w, m, v))
    w["lb_param"], m["lb_param"], v["lb_param"] = lb_param, m_lb_param, v_lb_param
    S, D = x.shape[1], x.shape[2]
    d_conv = D // 2
    ix, iy, ic = lax.axis_index("x"), lax.axis_index("y"), lax.axis_index("c")
    k_me = 2 * ix + iy
    dev = 2 * k_me + ic
    cw_shard = w["conv_w"].shape[1]
    ada_cols = w["w_ada"].shape[1]

    got = _allgather_rows("gather_cond", _pack_rows([c, w["conv_w"]])).reshape(N_DEV, -1)
    c_all = got[:, :D]
    conv_full = got[::2, D:D + 3 * cw_shard].reshape(N_CHIPS, 3, cw_shard).transpose(1, 0, 2).reshape(3, -1)
    c_act, c_act_b, lb = _prep(c_all, lb_param)
    b_cols = lax.dynamic_slice(w["b_ada"], (0, k_me * ada_cols), (1, ada_cols))
    mod_cols, = _matmul("ada_fwd", [(c_act_b, 'n', w["w_ada"].astype(BF16), 'n')], [0], N_DEV, ada_cols, D,
                        N_DEV, _pick(ada_cols, 1536, 128), D, [(b_cols, 'row', 0)], [(F32, None)],
                        lambda accs, ex: [accs[0] + ex[0]])
    mod_all = _allgather_rows("gather_mod", mod_cols).reshape(N_CHIPS, 2, N_DEV, ada_cols)[:, 0]
    mod_all = mod_all.transpose(1, 0, 2).reshape(N_DEV, -1)
    mod = lax.dynamic_slice(mod_all, (dev, 0), (1, mod_all.shape[1]))
    k_idx = jnp.reshape(k_me, (1,)).astype(jnp.int32)
    c_idx = jnp.reshape(ic, (1,)).astype(jnp.int32)
    bufs = {k: _cast_place("cast_" + k, w[k], k_idx) for k in _BIG}
    order = jnp.stack([k_me, 2 * (1 - ix) + iy, 2 * ix + (1 - iy), 2 * (1 - ix) + (1 - iy)]).astype(jnp.int32)

    loss, dx, small, arrived = _local_step(
        x[0], loss_target[0], mod, w["norm_mix_g"], lb, w["gnorm_g"], w["norm_ffn_g"], w["norm_final_g"], conv_full,
        bufs, c_idx, order)

    pieces = [small["dmod"], small["dg_mix"], small["dg_ffn"], small["dg_final"], small["dlb"], small["dgn"],
              small["dconv_w"], loss]
    sizes = [p.size for p in pieces]
    parts = _allgather_rows("gather_small", _pack_rows(pieces)).reshape(N_DEV, -1)
    total = _small_reduce(parts)
    offs = [0]
    for s in sizes:
        offs.append(offs[-1] + s)
    tot = [total[:, offs[i]:offs[i + 1]] for i in range(len(sizes))]
    dmod_all = parts[:, :sizes[0]]
    grads = {
        "b_ada": tot[0], "norm_mix_g": tot[1], "norm_ffn_g": tot[2], "norm_final_g": tot[3],
        "lb_param": _lb_grad(tot[4], lb), "gnorm_g": tot[5],
        "conv_w": lax.dynamic_slice(tot[6].reshape(3, -1), (0, k_me * cw_shard), (3, cw_shard)),
    }
    loss_out = tot[7][0, 0]

    halves = [_sum_chips("sum_chips_" + k, arrived[k], c_idx) for k in _BIG]
    whole = _share_halves(halves)
    for k, g in zip(_BIG, whole):
        grads[k] = g.reshape(-1, g.shape[-1])

    delta, new_m, new_v = {}, {}, {}
    dmod_cols = lax.dynamic_slice(dmod_all, (0, k_me * ada_cols), (N_DEV, ada_cols))
    grads["w_ada"], delta["w_ada"], new_m["w_ada"], new_v["w_ada"] = _ada_update(
        c_act.T, dmod_cols, w["w_ada"], m["w_ada"], v["w_ada"])
    for k in _WEIGHTS:
        if k != "w_ada":
            delta[k], new_m[k], new_v[k] = _adamw("adamw_" + k, w[k], grads[k], m[k], v[k])
    outs = [loss_out, dx.reshape(x.shape)]
    for t in (grads, delta, new_m, new_v):
        outs += [t[k].reshape(shapes[k]) for k in _WEIGHTS]
    return tuple(outs)
```

```python
import functools

import jax
import jax.numpy as jnp
from jax import lax
from jax.experimental import pallas as pl
from jax.experimental.pallas import tpu as pltpu

F32 = jnp.float32
BF16 = jnp.bfloat16
MESH = pl.DeviceIdType.MESH

EPS = 1e-6
HEAD = 128
BLK = 16
N_CHIPS = 4
N_DEV = 8
VMEM_LIMIT_BYTES = 56 * 1024 * 1024

ADAM_LR = 0.001
ADAM_B1 = 0.9
ADAM_B2 = 0.999
ADAM_EPS = 1e-08
ADAM_WD = 0.01
ADAM_STEP = 10


def _pick(dim, pref, mult):
    if dim <= pref:
        return dim
    t = (pref // mult) * mult
    while t >= mult:
        if dim % t == 0:
            return t
        t -= mult
    return dim


def _sig(x):
    return 1.0 / (1.0 + jnp.exp(-x))


def _params(sem):
    return pltpu.CompilerParams(dimension_semantics=sem, vmem_limit_bytes=VMEM_LIMIT_BYTES)


def _gj(j, i, k):
    return j


def _gk(j, i, k):
    return k


def _wspec(arr, rt, ct, r_of, c_of):
    if arr.ndim == 2:
        return pl.BlockSpec((rt, ct), lambda j, i, k: (r_of(j, i, k), c_of(j, i, k)))
    per = arr.shape[2] // ct
    assert arr.shape[2] % ct == 0
    return pl.BlockSpec((None, rt, ct),
                        lambda j, i, k: (c_of(j, i, k) // per, r_of(j, i, k), c_of(j, i, k) % per))


_HBM = pl.BlockSpec(memory_space=pltpu.HBM)


def _host(job, grid, in_specs, args, out_specs, out_shape, scratch, n_prefetch=0):
    if job is None:
        return {}, (lambda body: body)
    n_in, n_out, n_scr = len(args), len(out_shape), len(scratch)
    n_job, n_alias = len(job.inputs), job.n_alias
    in_specs += [_HBM] * n_job
    args += job.inputs
    out_specs += [_HBM] * n_alias
    out_shape += [jax.ShapeDtypeStruct(a.shape, a.dtype) for a in job.inputs[n_job - n_alias:]]
    scratch += job.sem_shapes
    aliases = {n_prefetch + n_in + n_job - n_alias + t: n_out + t for t in range(n_alias)}

    def wrap(body):
        def hosted(*refs):
            pre, refs = refs[:n_prefetch], refs[n_prefetch:]
            ins, refs = refs[:n_in], refs[n_in:]
            j_in, refs = refs[:n_job], refs[n_job:]
            outs, refs = refs[:n_out], refs[n_out:]
            j_out, refs = refs[:n_alias], refs[n_alias:]
            scr, sems = refs[:n_scr], refs[n_scr:]
            first = functools.reduce(jnp.logical_and, [pl.program_id(d) == 0 for d in range(len(grid))])
            last = functools.reduce(jnp.logical_and, [pl.program_id(d) == grid[d] - 1 for d in range(len(grid))])

            @pl.when(first)
            def _():
                job.start(j_in, j_out, sems)

            body(*pre, *ins, *outs, *scr)

            @pl.when(last)
            def _():
                job.finish(j_in, j_out, sems)

        return hosted

    return aliases, wrap


def _matmul(name, pairs, acc_of, M, N, K, tm, tn, tk, extras, outs, epilogue, job=None):
    assert M % tm == 0 and N % tn == 0 and K % tk == 0, (name, M, N, K, tm, tn, tk)
    nj, ni, nk = N // tn, M // tm, K // tk
    n_pairs, n_extra, n_out = len(pairs), len(extras), len(outs)
    n_acc = max(acc_of) + 1
    in_specs, args, dims = [], [], []
    lhs_of, seen = [], {}
    for a, am, b, bm in pairs:
        if (id(a), am) not in seen:
            seen[id(a), am] = len(args)
            args.append(a)
            if am == 'n':
                in_specs.append(pl.BlockSpec((tm, tk), lambda j, i, k: (i, k)))
            else:
                in_specs.append(pl.BlockSpec((tk, tm), lambda j, i, k: (k, i)))
        lhs_of.append(seen[id(a), am])
    n_lhs = len(args)
    for a, am, b, bm in pairs:
        if bm == 'n':
            in_specs.append(_wspec(b, tk, tn, _gk, _gj))
        else:
            in_specs.append(_wspec(b, tn, tk, _gj, _gk))
        dims.append((((1 if am == 'n' else 0,), (0 if bm == 'n' else 1,)), ((), ())))
        args.append(b)
    for e, kind, off in extras:
        assert off % tn == 0, (name, off, tn)
        o = off // tn
        if kind == 'tile':
            in_specs.append(pl.BlockSpec((tm, tn), lambda j, i, k, o=o: (i, j + o)))
        else:
            in_specs.append(pl.BlockSpec((1, tn), lambda j, i, k, o=o: (0, j + o)))
        args.append(e)
    out_shape, out_specs = [], []
    for dt, nb in outs:
        if nb is None:
            out_shape.append(jax.ShapeDtypeStruct((M, N), dt))
            out_specs.append(pl.BlockSpec((tm, tn), lambda j, i, k: (i, j)))
        else:
            assert nb % tn == 0 and N % nb == 0
            per = nb // tn
            out_shape.append(jax.ShapeDtypeStruct((N // nb, M, nb), dt))
            out_specs.append(pl.BlockSpec((None, tm, tn), lambda j, i, k, per=per: (j // per, i, j % per)))

    def body(*refs):
        n_ab = n_lhs + n_pairs
        e_refs = refs[n_ab:n_ab + n_extra]
        o_refs = refs[n_ab + n_extra:n_ab + n_extra + n_out]
        acc_refs = refs[n_ab + n_extra + n_out:]
        sums = [None] * n_acc
        for p in range(n_pairs):
            prod = lax.dot_general(refs[lhs_of[p]][...], refs[n_lhs + p][...], dims[p], preferred_element_type=F32)
            a = acc_of[p]
            sums[a] = prod if sums[a] is None else sums[a] + prod

        def finish(accs):
            res = epilogue(accs, [e[...] for e in e_refs])
            for o_ref, r in zip(o_refs, res):
                o_ref[...] = r.astype(o_ref.dtype)

        if nk == 1:
            finish(sums)
        else:
            k = pl.program_id(2)

            @pl.when(k == 0)
            def _():
                for a in range(n_acc):
                    acc_refs[a][...] = sums[a]

            @pl.when(k > 0)
            def _():
                for a in range(n_acc):
                    acc_refs[a][...] += sums[a]

            @pl.when(k == nk - 1)
            def _():
                finish([acc_refs[a][...] for a in range(n_acc)])

    scratch = [pltpu.VMEM((tm, tn), F32) for _ in range(n_acc)] if nk > 1 else []
    grid = (nj, ni, nk)
    aliases, wrap = _host(job, grid, in_specs, args, out_specs, out_shape, scratch)
    sem = ("parallel", "parallel", "arbitrary") if job is None else ("arbitrary",) * 3
    return pl.pallas_call(
        wrap(body), name=name, grid=grid, in_specs=in_specs, out_specs=out_specs, out_shape=out_shape,
        scratch_shapes=scratch, input_output_aliases=aliases, compiler_params=_params(sem),
    )(*args)


def _row_spec(tr, d):
    return pl.BlockSpec((tr, d), lambda i: (i, 0))


def _vec_spec(d):
    return pl.BlockSpec((1, d), lambda i: (0, 0))


def _norm_mod(name, x, g, sc, sh):
    S, D = x.shape
    tr = _pick(S, 256, 8)

    def body(x_ref, g_ref, sc_ref, sh_ref, h_ref):
        xv = x_ref[...]
        r = lax.rsqrt(jnp.mean(xv * xv, axis=-1, keepdims=True) + EPS)
        h_ref[...] = ((xv * r) * g_ref[...] * (1.0 + sc_ref[...]) + sh_ref[...]).astype(BF16)

    return pl.pallas_call(
        body, name=name, grid=(S // tr,),
        in_specs=[_row_spec(tr, D), _vec_spec(D), _vec_spec(D), _vec_spec(D)],
        out_specs=_row_spec(tr, D), out_shape=jax.ShapeDtypeStruct((S, D), BF16),
        compiler_params=_params(("parallel",)),
    )(x, g, sc, sh)


def _loss_head(x2, target, g, ff, gt):
    S, D = x2.shape
    tr = _pick(S, 256, 8)

    def body(x_ref, t_ref, g_ref, ff_ref, gt_ref, dx_ref, dffb_ref, loss_ref, dgt_ref, dg_ref):
        i = pl.program_id(0)

        @pl.when(i == 0)
        def _():
            loss_ref[...] = jnp.zeros_like(loss_ref)
            dgt_ref[...] = jnp.zeros_like(dgt_ref)
            dg_ref[...] = jnp.zeros_like(dg_ref)

        xv = x_ref[...]
        gv = g_ref[...]
        r = lax.rsqrt(jnp.mean(xv * xv, axis=-1, keepdims=True) + EPS)
        xh = xv * r
        err = xh * gv - t_ref[...]
        loss_ref[...] += 0.5 * jnp.sum(jnp.mean(err * err, axis=-1, keepdims=True))
        dy = err * (1.0 / D)
        dg_ref[...] += jnp.sum(dy * xh, axis=0, keepdims=True)
        dxh = dy * gv
        dx = r * (dxh - xh * jnp.mean(dxh * xh, axis=-1, keepdims=True))
        dx_ref[...] = dx
        dffb_ref[...] = (dx * gt_ref[...]).astype(BF16)
        dgt_ref[...] += jnp.sum(dx * ff_ref[...], axis=0, keepdims=True)

    return pl.pallas_call(
        body, name="loss_head", grid=(S // tr,),
        in_specs=[_row_spec(tr, D), _row_spec(tr, D), _vec_spec(D), _row_spec(tr, D), _vec_spec(D)],
        out_specs=[_row_spec(tr, D), _row_spec(tr, D), pl.BlockSpec((1, 128), lambda i: (0, 0)),
                   _vec_spec(D), _vec_spec(D)],
        out_shape=[jax.ShapeDtypeStruct((S, D), F32), jax.ShapeDtypeStruct((S, D), BF16),
                   jax.ShapeDtypeStruct((1, 128), F32), jax.ShapeDtypeStruct((1, D), F32),
                   jax.ShapeDtypeStruct((1, D), F32)],
        compiler_params=_params(("arbitrary",)),
    )(x2, target, g, ff, gt)


def _norm_mod_bwd(name, dh, x, g, sc, dres, gated=None):
    S, D = x.shape
    tr = _pick(S, 256, 8)
    n = S // tr
    with_gate = gated is not None

    def body(*refs):
        if with_gate:
            dh_ref, x_ref, g_ref, sc_ref, dres_ref, mo_ref, gt_ref, dx_ref, dsh_ref, dsc_ref, dg_ref, dmob_ref, dgt_ref = refs
        else:
            dh_ref, x_ref, g_ref, sc_ref, dres_ref, dx_ref, dsh_ref, dsc_ref, dg_ref = refs
        i = pl.program_id(0)

        @pl.when(i == 0)
        def _():
            dsh_ref[...] = jnp.zeros_like(dsh_ref)
            dsc_ref[...] = jnp.zeros_like(dsc_ref)
            if with_gate:
                dgt_ref[...] = jnp.zeros_like(dgt_ref)

        xv = x_ref[...]
        dhv = dh_ref[...]
        gv = g_ref[...]
        scale = 1.0 + sc_ref[...]
        r = lax.rsqrt(jnp.mean(xv * xv, axis=-1, keepdims=True) + EPS)
        xh = xv * r
        dsh_ref[...] += jnp.sum(dhv, axis=0, keepdims=True)
        dsc_ref[...] += jnp.sum(dhv * xh, axis=0, keepdims=True)
        dxh = dhv * (gv * scale)
        dx = dres_ref[...] + r * (dxh - xh * jnp.mean(dxh * xh, axis=-1, keepdims=True))
        dx_ref[...] = dx
        if with_gate:
            dmob_ref[...] = (dx * gt_ref[...]).astype(BF16)
            dgt_ref[...] += jnp.sum(dx * mo_ref[...], axis=0, keepdims=True)

        @pl.when(i == n - 1)
        def _():
            t = dsc_ref[...]
            dg_ref[...] = t * scale
            dsc_ref[...] = t * gv

    in_specs = [_row_spec(tr, D), _row_spec(tr, D), _vec_spec(D), _vec_spec(D), _row_spec(tr, D)]
    args = [dh, x, g, sc, dres]
    out_specs = [_row_spec(tr, D), _vec_spec(D), _vec_spec(D), _vec_spec(D)]
    out_shape = [jax.ShapeDtypeStruct((S, D), F32)] + [jax.ShapeDtypeStruct((1, D), F32)] * 3
    if with_gate:
        in_specs += [_row_spec(tr, D), _vec_spec(D)]
        args += list(gated)
        out_specs += [_row_spec(tr, D), _vec_spec(D)]
        out_shape += [jax.ShapeDtypeStruct((S, D), BF16), jax.ShapeDtypeStruct((1, D), F32)]
    return pl.pallas_call(
        body, name=name, grid=(n,), in_specs=in_specs, out_specs=out_specs, out_shape=out_shape,
        compiler_params=_params(("arbitrary",)),
    )(*args)


CONV_ROWS = 256


def _col_spec(S, off_cols):
    o = off_cols // HEAD
    return pl.BlockSpec((S, HEAD), lambda c, o=o: (0, c + o))


def _conv_taps(u, u_prev, rid):
    s1 = jnp.where(rid >= 1, pltpu.roll(u, 1, 0), pltpu.roll(u_prev, 1, 0))
    s2 = jnp.where(rid >= 2, pltpu.roll(u, 2, 0), pltpu.roll(u_prev, 2, 0))
    return s1, s2


def _conv_fwd(proj, conv_w, d_conv):
    S = proj.shape[0]
    R = min(CONV_ROWS, S)
    nr = S // R

    def body(ab_ref, ac_ref, ax_ref, w_ref, ya_ref):
        w0, w1, w2 = w_ref[0:1, :], w_ref[1:2, :], w_ref[2:3, :]
        rid = lax.broadcasted_iota(jnp.int32, (R, HEAD), 0)

        def step(c, carry):
            rows = pl.ds(pl.multiple_of(c * R, R), R)
            prev = pl.ds(pl.multiple_of(jnp.maximum(c - 1, 0) * R, R), R)
            u = ac_ref[rows, :] * ax_ref[rows, :]
            u_prev = jnp.where(c > 0, ac_ref[prev, :] * ax_ref[prev, :], 0.0)
            s1, s2 = _conv_taps(u, u_prev, rid)
            y = w0 * s2 + w1 * s1 + w2 * u
            ya_ref[rows, :] = (ab_ref[rows, :] * y).astype(BF16)
            return carry

        lax.fori_loop(0, nr, step, 0)

    return pl.pallas_call(
        body, name="conv_fwd", grid=(d_conv // HEAD,),
        in_specs=[_col_spec(S, 0), _col_spec(S, d_conv), _col_spec(S, 2 * d_conv),
                  pl.BlockSpec((3, HEAD), lambda c: (0, c))],
        out_specs=pl.BlockSpec((S, HEAD), lambda c: (0, c)),
        out_shape=jax.ShapeDtypeStruct((S, d_conv), BF16),
        compiler_params=_params(("parallel",)),
    )(proj, proj, proj, conv_w)


def _conv_bwd(dya, proj, conv_w, d_conv):
    S = proj.shape[0]
    R = min(CONV_ROWS, S)
    nr = S // R

    def body(dya_ref, ab_ref, ac_ref, ax_ref, w_ref, dab_ref, dac_ref, dax_ref, dw_ref):
        w0, w1, w2 = w_ref[0:1, :], w_ref[1:2, :], w_ref[2:3, :]
        rid = lax.broadcasted_iota(jnp.int32, (R, HEAD), 0)

        def step(c, carry):
            dw0, dw1, dw2 = carry
            rows = pl.ds(pl.multiple_of(c * R, R), R)
            prev = pl.ds(pl.multiple_of(jnp.maximum(c - 1, 0) * R, R), R)
            nxt = pl.ds(pl.multiple_of(jnp.minimum(c + 1, nr - 1) * R, R), R)
            a_c, a_x, a_b = ac_ref[rows, :], ax_ref[rows, :], ab_ref[rows, :]
            u = a_c * a_x
            u_prev = jnp.where(c > 0, ac_ref[prev, :] * ax_ref[prev, :], 0.0)
            s1, s2 = _conv_taps(u, u_prev, rid)
            y = w0 * s2 + w1 * s1 + w2 * u
            dy = dya_ref[rows, :]
            dab_ref[rows, :] = (dy * y).astype(BF16)
            dyc = dy * a_b
            dyc_next = jnp.where(c < nr - 1, dya_ref[nxt, :] * ab_ref[nxt, :], 0.0)
            t1 = jnp.where(rid < R - 1, pltpu.roll(dyc, R - 1, 0), pltpu.roll(dyc_next, R - 1, 0))
            t2 = jnp.where(rid < R - 2, pltpu.roll(dyc, R - 2, 0), pltpu.roll(dyc_next, R - 2, 0))
            du = w2 * dyc + w1 * t1 + w0 * t2
            dac_ref[rows, :] = (du * a_x).astype(BF16)
            dax_ref[rows, :] = (du * a_c).astype(BF16)
            dw0 = dw0 + jnp.sum(dyc * s2, axis=0, keepdims=True)
            dw1 = dw1 + jnp.sum(dyc * s1, axis=0, keepdims=True)
            dw2 = dw2 + jnp.sum(dyc * u, axis=0, keepdims=True)
            return dw0, dw1, dw2

        z = jnp.zeros((1, HEAD), F32)
        dw0, dw1, dw2 = lax.fori_loop(0, nr, step, (z, z, z))
        dw_ref[0:1, :] = dw0
        dw_ref[1:2, :] = dw1
        dw_ref[2:3, :] = dw2

    col = pl.BlockSpec((S, HEAD), lambda c: (0, c))
    return pl.pallas_call(
        body, name="conv_bwd", grid=(d_conv // HEAD,),
        in_specs=[col, _col_spec(S, 0), _col_spec(S, d_conv), _col_spec(S, 2 * d_conv),
                  pl.BlockSpec((3, HEAD), lambda c: (0, c))],
        out_specs=[col, col, col, pl.BlockSpec((3, HEAD), lambda c: (0, c))],
        out_shape=[jax.ShapeDtypeStruct((S, d_conv), BF16)] * 3 + [jax.ShapeDtypeStruct((3, d_conv), F32)],
        compiler_params=_params(("parallel",)),
    )(dya, proj, proj, proj, conv_w)


HGRN_ROWS = 256
HGRN_FWD_SUB = 64
HGRN_BWD_SUB = 32


def _block_cumsum(x, pos):
    for s in (1, 2, 4, 8):
        x = x + jnp.where(pos >= s, pltpu.roll(x, s, 0), 0.0)
    return x


def _block_rev_cumsum(x, pos, rows):
    for s in (1, 2, 4, 8):
        x = x + jnp.where(pos < BLK - s, pltpu.roll(x, rows - s, 0), 0.0)
    return x


def _block_last(x, pos, rows):
    m = jnp.where(pos == BLK - 1, x, 0.0)
    for s in (1, 2, 4, 8):
        m = m + pltpu.roll(m, rows - s, 0)
    return m


def _hgrn_gates(q, z, lb):
    sgq = _sig(q)
    qs = q * sgq
    sg = _sig(z)
    f = lb + (1.0 - lb) * sg
    kk = (1.0 - lb) * (1.0 - sg)
    return sgq, qs, sg, f, kk


def _hgrn_decays(q, z, lb, pos, rows):
    sgq, qs, sg, f, kk = _hgrn_gates(q, z, lb)
    b = _block_cumsum(jnp.log(f), pos)
    return sgq, qs, sg, f, kk, b, _block_last(b, pos, rows)


def _hgrn_specs(T, d_conv, n_t, rev):
    def t_of(i):
        return (n_t - 1 - i) if rev else i

    def pcol(off):
        o = off // HEAD
        return pl.BlockSpec((T, HEAD), lambda h, i, o=o: (t_of(i), h + o))

    q_spec, z_spec, v_spec, g_spec = pcol(3 * d_conv), pcol(4 * d_conv), pcol(5 * d_conv), pcol(6 * d_conv)
    lb_spec = pl.BlockSpec((1, HEAD), lambda h, i: (0, h))
    gn_spec = pl.BlockSpec((1, HEAD), lambda h, i: (0, 0))
    act_spec = pl.BlockSpec((T, HEAD), lambda h, i: (t_of(i), h))
    st_spec = pl.BlockSpec((None, T // BLK, HEAD, HEAD), lambda h, i: (h, t_of(i), 0, 0))
    return q_spec, z_spec, v_spec, g_spec, lb_spec, gn_spec, act_spec, st_spec


def _hgrn_fwd(proj, lb, gn, d_conv, job=None):
    S = proj.shape[0]
    H = d_conv // HEAD
    T = min(HGRN_ROWS, S)
    n_t, nb = S // T, T // BLK
    sub = min(HGRN_FWD_SUB, T)
    q_spec, z_spec, v_spec, g_spec, lb_spec, gn_spec, act_spec, st_spec = _hgrn_specs(T, d_conv, n_t, False)

    def body(q_ref, z_ref, v_ref, g_ref, lb_ref, gn_ref, ob_ref, o_ref, st_ref, state, qe_s, ke_s, v_s, dec_s, o_s, u_s):
        @pl.when(pl.program_id(1) == 0)
        def _():
            state[...] = jnp.zeros_like(state)

        pos = lax.broadcasted_iota(jnp.int32, (sub, HEAD), 0) % BLK
        lbv = lb_ref[...]

        def vector_pass(s, carry):
            r = pl.ds(pl.multiple_of(s * sub, sub), sub)
            v = v_ref[r, :]
            _, qs, _, _, kk, b, b_tot = _hgrn_decays(q_ref[r, :], z_ref[r, :], lbv, pos, sub)
            qe_s[r, :] = (qs * jnp.exp(b)).astype(BF16)
            ke_s[r, :] = (kk * jnp.exp(b_tot - b)).astype(BF16)
            v_s[r, :] = v.astype(BF16)
            dec_s[r, :] = jnp.exp(b_tot)
            o = jnp.sum(qs * kk, axis=-1, keepdims=True) * v
            for d in range(1, BLK):
                e = jnp.exp(b - pltpu.roll(b, d, 0))
                a = jnp.sum(qs * pltpu.roll(kk, d, 0) * e, axis=-1, keepdims=True)
                o = o + jnp.where(pos >= d, a * pltpu.roll(v, d, 0), 0.0)
            o_s[r, :] = o
            return carry

        lax.fori_loop(0, T // sub, vector_pass, 0)

        for j in range(nb):
            rows = pl.ds(j * BLK, BLK)
            u_s[j] = lax.dot_general(v_s[rows, :], ke_s[rows, :], (((0,), (0,)), ((), ())),
                                     preferred_element_type=F32)
        st = state[...]
        for j in range(nb):
            st_ref[j] = st.astype(BF16)
            st = st * dec_s[pl.ds(j * BLK, 1), :] + u_s[j]
        state[...] = st
        for j in range(nb):
            rows = pl.ds(j * BLK, BLK)
            o_s[rows, :] += lax.dot_general(qe_s[rows, :], st_ref[j], (((1,), (1,)), ((), ())),
                                            preferred_element_type=F32)
        o = o_s[...]
        o_ref[...] = o
        r = lax.rsqrt(jnp.mean(o * o, axis=-1, keepdims=True) + EPS)
        g = g_ref[...]
        ob_ref[...] = ((o * r) * gn_ref[...] * (g * _sig(g))).astype(BF16)

    grid = (H, n_t)
    in_specs = [q_spec, z_spec, v_spec, g_spec, lb_spec, gn_spec]
    args = [proj, proj, proj, proj, lb, gn]
    out_specs = [act_spec, act_spec, st_spec]
    out_shape = [jax.ShapeDtypeStruct((S, d_conv), BF16), jax.ShapeDtypeStruct((S, d_conv), F32),
                 jax.ShapeDtypeStruct((H, S // BLK, HEAD, HEAD), BF16)]
    scratch = [pltpu.VMEM((HEAD, HEAD), F32), pltpu.VMEM((T, HEAD), BF16), pltpu.VMEM((T, HEAD), BF16),
               pltpu.VMEM((T, HEAD), BF16), pltpu.VMEM((T, HEAD), F32), pltpu.VMEM((T, HEAD), F32),
               pltpu.VMEM((nb, HEAD, HEAD), F32)]
    aliases, wrap = _host(job, grid, in_specs, args, out_specs, out_shape, scratch)
    return pl.pallas_call(
        wrap(body), name="hgrn_fwd", grid=grid, in_specs=in_specs, out_specs=out_specs, out_shape=out_shape,
        scratch_shapes=scratch, input_output_aliases=aliases,
        compiler_params=_params(("parallel" if job is None else "arbitrary", "arbitrary")),
    )(*args)


def _hgrn_bwd(dob, o_raw, states, proj, lb, gn, d_conv, job=None):
    S = proj.shape[0]
    H = d_conv // HEAD
    T = min(HGRN_ROWS, S)
    n_t, nb = S // T, T // BLK
    sub = min(HGRN_BWD_SUB, T)
    q_spec, z_spec, v_spec, g_spec, lb_spec, gn_spec, act_spec, st_spec = _hgrn_specs(T, d_conv, n_t, True)

    def body(dob_ref, o_ref, st_ref, q_ref, z_ref, v_ref, g_ref, lb_ref, gn_ref,
             dq_ref, dz_ref, dv_ref, dg_ref, dlb_ref, dgn_ref,
             dstate, do_b, qe_b, ke_b, v_b, dec_s, dqe_s, dke_s, dvi_s, ddec_s, do_s, u_s, ds_s):
        @pl.when(pl.program_id(1) == 0)
        def _():
            dstate[...] = jnp.zeros_like(dstate)
            dlb_ref[...] = jnp.zeros_like(dlb_ref)
            dgn_ref[...] = jnp.zeros_like(dgn_ref)

        pos = lax.broadcasted_iota(jnp.int32, (sub, HEAD), 0) % BLK
        lbv, gnv = lb_ref[...], gn_ref[...]

        def before(s, carry):
            r = pl.ds(pl.multiple_of(s * sub, sub), sub)
            v, g = v_ref[r, :], g_ref[r, :]
            _, qs, _, _, kk, b, b_tot = _hgrn_decays(q_ref[r, :], z_ref[r, :], lbv, pos, sub)
            qe_b[r, :] = (qs * jnp.exp(b)).astype(BF16)
            ke_b[r, :] = (kk * jnp.exp(b_tot - b)).astype(BF16)
            v_b[r, :] = v.astype(BF16)
            dec_s[r, :] = jnp.exp(b_tot)
            o = o_ref[r, :]
            rs = lax.rsqrt(jnp.mean(o * o, axis=-1, keepdims=True) + EPS)
            on = o * rs
            sgg = _sig(g)
            dobv = dob_ref[r, :]
            dog = dobv * (g * sgg)
            dgn_ref[...] += jnp.sum(dog * on, axis=0, keepdims=True)
            don = dog * gnv
            do = rs * (don - on * jnp.mean(don * on, axis=-1, keepdims=True))
            dg_ref[r, :] = (dobv * (on * gnv) * (sgg * (1.0 + g * (1.0 - sgg)))).astype(BF16)
            do_s[r, :] = do
            do_b[r, :] = do.astype(BF16)
            return carry

        lax.fori_loop(0, T // sub, before, 0)

        for j in range(nb):
            rows = pl.ds(j * BLK, BLK)
            dob_j = do_b[rows, :]
            u_s[j] = lax.dot_general(dob_j, qe_b[rows, :], (((0,), (0,)), ((), ())), preferred_element_type=F32)
            dqe_s[rows, :] = lax.dot_general(dob_j, st_ref[j], (((1,), (0,)), ((), ())), preferred_element_type=F32)
        dst = dstate[...]
        for j in reversed(range(nb)):
            ds_s[j] = dst.astype(BF16)
            ddec = jnp.sum(st_ref[j].astype(F32) * dst, axis=0, keepdims=True)
            ddec_s[pl.ds(j * BLK, BLK), :] = jnp.broadcast_to(ddec, (BLK, HEAD))
            dst = dst * dec_s[pl.ds(j * BLK, 1), :] + u_s[j]
        dstate[...] = dst
        for j in range(nb):
            rows = pl.ds(j * BLK, BLK)
            dke_s[rows, :] = lax.dot_general(v_b[rows, :], ds_s[j], (((1,), (0,)), ((), ())), preferred_element_type=F32)
            dvi_s[rows, :] = lax.dot_general(ke_b[rows, :], ds_s[j], (((1,), (1,)), ((), ())), preferred_element_type=F32)

        def after(s, carry):
            r = pl.ds(pl.multiple_of(s * sub, sub), sub)
            q, v = q_ref[r, :], v_ref[r, :]
            sgq, qs, sg, f, kk, b, b_tot = _hgrn_decays(q, z_ref[r, :], lbv, pos, sub)
            do = do_s[r, :]
            dqs = dqe_s[r, :] * jnp.exp(b)
            dkk = dke_s[r, :] * jnp.exp(b_tot - b)
            d_tot = jnp.where(pos == BLK - 1, _block_cumsum(dkk * kk, pos) + ddec_s[r, :] * jnp.exp(b_tot), 0.0)
            dv = dvi_s[r, :]
            da = jnp.sum(do * v, axis=-1, keepdims=True)
            dv = dv + jnp.sum(qs * kk, axis=-1, keepdims=True) * do
            dqs = dqs + da * kk
            dkk = dkk + da * qs
            for d in range(1, BLK):
                kd = pltpu.roll(kk, d, 0)
                e = jnp.where(pos >= d, jnp.exp(b - pltpu.roll(b, d, 0)), 0.0)
                a = jnp.sum(qs * kd * e, axis=-1, keepdims=True)
                da = jnp.sum(do * pltpu.roll(v, d, 0), axis=-1, keepdims=True)
                gq = da * e
                dqs = dqs + gq * kd
                dkk = dkk + pltpu.roll(gq * qs, sub - d, 0)
                dv = dv + pltpu.roll(a * do, sub - d, 0)
            db = qs * dqs - kk * dkk + d_tot
            dlf = _block_rev_cumsum(db, pos, sub)
            df = dlf / f - dkk
            dlb_ref[...] += jnp.sum(df * (1.0 - sg), axis=0, keepdims=True)
            dz_ref[r, :] = (df * (1.0 - lbv) * sg * (1.0 - sg)).astype(BF16)
            dq_ref[r, :] = (dqs * (sgq * (1.0 + q * (1.0 - sgq)))).astype(BF16)
            dv_ref[r, :] = dv.astype(BF16)
            return carry

        lax.fori_loop(0, T // sub, after, 0)

    tb = lambda dt: pltpu.VMEM((T, HEAD), dt)
    grid = (H, n_t)
    in_specs = [act_spec, act_spec, st_spec, q_spec, z_spec, v_spec, g_spec, lb_spec, gn_spec]
    args = [dob, o_raw, states, proj, proj, proj, proj, lb, gn]
    out_specs = [act_spec, act_spec, act_spec, act_spec, lb_spec, pl.BlockSpec((None, 1, HEAD), lambda h, i: (h, 0, 0))]
    out_shape = ([jax.ShapeDtypeStruct((S, d_conv), BF16)] * 4
                 + [jax.ShapeDtypeStruct((1, d_conv), F32), jax.ShapeDtypeStruct((H, 1, HEAD), F32)])
    scratch = [pltpu.VMEM((HEAD, HEAD), F32), tb(BF16), tb(BF16), tb(BF16), tb(BF16), tb(F32),
               tb(F32), tb(F32), tb(F32), tb(F32), tb(F32),
               pltpu.VMEM((nb, HEAD, HEAD), F32), pltpu.VMEM((nb, HEAD, HEAD), BF16)]
    aliases, wrap = _host(job, grid, in_specs, args, out_specs, out_shape, scratch)
    return pl.pallas_call(
        wrap(body), name="hgrn_bwd", grid=grid, in_specs=in_specs, out_specs=out_specs, out_shape=out_shape,
        scratch_shapes=scratch, input_output_aliases=aliases,
        compiler_params=_params(("parallel" if job is None else "arbitrary", "arbitrary")),
    )(*args)


def _first(accs, extras):
    return [accs[0]]


def _local_step(x, target, mod, norm_mix_g, lb, gnorm_g, norm_ffn_g, norm_final_g, conv_w, bufs, c_idx, order):
    S, D = x.shape
    d_conv = D // 2
    d_in = 7 * d_conv + 2 * D
    d_ff = N_CHIPS * bufs["w_ffn_down"].shape[1]
    nb_in, nb_co, nb_ff = bufs["w_in"].shape[2], bufs["w_conv_out"].shape[2], bufs["w_ffn_gate"].shape[2]
    rows = lambda g: g.reshape(-1, g.shape[2])
    sh_m, sc_m, gt_m, sh_f, sc_f, gt_f = [mod[:, i * D:(i + 1) * D] for i in range(6)]
    tm = _pick(S, 512, 16)
    tm2 = _pick(S, 1024, 16)
    ts = _pick(S, 1024, 128)
    tn_in = _pick(nb_in, 1408, 128)
    tn_co = _pick(nb_co, 512, 128)
    tn_ff = _pick(nb_ff, 1408, 128)
    tn_d = _pick(D, 512, 128)
    tw = _pick(D, 1024, 128)
    ts2 = _pick(S, 2048, 128)

    h = _norm_mod("norm_mix", x, norm_mix_g, sc_m, sh_m)
    proj, w_in = _proj_gather(h, bufs["w_in"], order, None)
    ob, o_raw, states, *got = _hgrn_fwd(
        proj, lb, gnorm_g, d_conv,
        job=_GatherJob([bufs[k] for k in ("w_conv_out", "w_hgrn_out", "w_o", "w_ffn_gate")]))
    ya = _conv_fwd(proj, conv_w, d_conv)
    w_conv_out, w_hgrn_out, w_o, w_ffn_gate = _forward_halves("forward_branch", got)
    w_o = rows(w_o)

    def merge_epi(accs, ex):
        y_a, y_b = accs
        return [y_a, y_b, _sig(ex[0]) * y_a + _sig(ex[1]) * y_b]

    y_a, y_b, merged, w_ffn_up = _matmul(
        "branch_out", [(ya, 'n', w_conv_out, 'n'), (ob, 'n', w_hgrn_out, 'n')], [0, 1], S, D, d_conv, tm2, tn_co, d_conv,
        [(proj, 'tile', 7 * d_conv), (proj, 'tile', 7 * d_conv + D)], [(F32, None), (F32, None), (BF16, None)], merge_epi,
        job=_GatherJob([bufs["w_ffn_up"]], 0, 2))

    def resid_epi(accs, ex):
        return [accs[0], ex[0] + ex[1] * accs[0]]

    mo, x1, w_ffn_up = _matmul("w_o", [(merged, 'n', w_o, 'n')], [0], S, D, D, tm2, tw, D,
                               [(x, 'tile', 0), (gt_m, 'row', 0)], [(F32, None), (F32, None)], resid_epi,
                               job=_GatherJob([w_ffn_up], 1, 2))
    h2 = _norm_mod("norm_ffn", x1, norm_ffn_g, sc_f, sh_f)
    w_ffn_up, = _forward_halves("forward_up", [w_ffn_up])

    def swiglu_epi(accs, ex):
        gg, uu = accs
        return [gg, uu, gg * _sig(gg) * uu]

    G, U, act, w_ffn_down = _matmul(
        "ffn_in", [(h2, 'n', w_ffn_gate, 'n'), (h2, 'n', w_ffn_up, 'n')], [0, 1], S, d_ff, D,
        tm, tn_ff, D, [], [(BF16, None), (BF16, None), (BF16, None)], swiglu_epi, job=_GatherJob([bufs["w_ffn_down"]]))
    w_ffn_down = rows(_forward_halves("forward_down", [w_ffn_down])[0])
    ff, x2 = _matmul("ffn_out", [(act, 'n', w_ffn_down, 'n')], [0], S, D, d_ff, tm, D, _pick(d_ff, 1408, 128),
                     [(x1, 'tile', 0), (gt_f, 'row', 0)], [(F32, None), (F32, None)], resid_epi)

    dx2, dffb, loss, dgt_f, dg_final = _loss_head(x2, target, norm_final_g, ff, gt_f)

    def swiglu_bwd_epi(accs, ex):
        dact, gg, uu = accs[0], ex[0], ex[1]
        s = _sig(gg)
        return [dact * uu * (s * (1.0 + gg * (1.0 - s))), dact * (gg * s)]

    dG, dU = _matmul("d_act", [(dffb, 'n', w_ffn_down, 't')], [0], S, d_ff, D, tm2, tn_ff, D,
                     [(G, 'tile', 0), (U, 'tile', 0)], [(BF16, None), (BF16, None)], swiglu_bwd_epi)
    dw_down, = _matmul("dw_ffn_down", [(act, 't', dffb, 'n')], [0], d_ff, D, S, _pick(d_ff, 1408, 128),
                       D, ts, [], [(BF16, None)], _first)
    dh2, = _matmul("d_h2", [(dG, 'n', w_ffn_gate, 't'), (dU, 'n', w_ffn_up, 't')], [0, 0], S, D, d_ff,
                   tm, D, tn_ff, [], [(F32, None)], _first)
    dw_gate, dw_up = _matmul("dw_ffn_in", [(h2, 't', dG, 'n'), (h2, 't', dU, 'n')], [0, 1], D, d_ff, S,
                             tw, tn_ff, ts, [], [(BF16, nb_ff), (BF16, nb_ff)],
                             lambda accs, ex: accs)

    def pre_sum(tag, names, grads):
        from_sibling = _swap_halves("swap_" + tag, grads)
        pairs = [_pair_sum("pair_sum_" + k, g, q, c_idx) for k, g, q in zip(names, grads, from_sibling)]
        return [p[0] for p in pairs], [p[1] for p in pairs]

    ffn_names = ["w_ffn_down", "w_ffn_gate", "w_ffn_up"]
    parts_f, slots_f = pre_sum("ffn", ffn_names, [dw_down.reshape(N_CHIPS, -1, D), dw_gate, dw_up])
    dx1, dsh_f, dsc_f, dg_ffn, dmob, dgt_m = _norm_mod_bwd("norm_ffn_bwd", dh2, x1, norm_ffn_g, sc_f, dx2, (mo, gt_m))

    def merge_bwd_epi(accs, ex):
        dm, ga, gb, ya_, yb_ = accs[0], ex[0], ex[1], ex[2], ex[3]
        sa, sb = _sig(ga), _sig(gb)
        return [dm * ya_ * sa * (1.0 - sa), dm * yb_ * sb * (1.0 - sb), dm * sa, dm * sb]

    dga, dgb, dy_a, dy_b = _matmul(
        "d_merged", [(dmob, 'n', w_o, 't')], [0], S, D, D, tm2, tn_co, D,
        [(proj, 'tile', 7 * d_conv), (proj, 'tile', 7 * d_conv + D), (y_a, 'tile', 0), (y_b, 'tile', 0)],
        [(BF16, None)] * 4, merge_bwd_epi)
    dw_o, = _matmul("dw_o", [(merged, 't', dmob, 'n')], [0], D, D, S, _pick(D, 512, 128), _pick(D, 1024, 128), ts,
                    [], [(BF16, None)], _first)
    both = lambda accs, ex: accs
    dya, dob = _matmul("d_branch", [(dy_a, 'n', w_conv_out, 't'), (dy_b, 'n', w_hgrn_out, 't')], [0, 1], S, d_conv, D,
                       tm2, _pick(d_conv, 1024, 128), tn_co, [], [(F32, None), (F32, None)], both)
    dw_co, dw_ho = _matmul("dw_branch", [(ya, 't', dy_a, 'n'), (ob, 't', dy_b, 'n')], [0, 1], d_conv, D, S,
                           _pick(d_conv, 1024, 128), tn_co, ts, [], [(BF16, nb_co), (BF16, nb_co)], both)
    branch_names = ["w_conv_out", "w_hgrn_out", "w_o"]
    parts_b, slots_b = pre_sum("branch", branch_names, [dw_co, dw_ho, dw_o.reshape(N_CHIPS, -1, D)])
    dab, dac, dax, dconv_w = _conv_bwd(dya, proj, conv_w, d_conv)
    dq, dz, dv, dgo, dlb, dgn, *arrived = _hgrn_bwd(dob, o_raw, states, proj, lb, gnorm_g, d_conv,
                                                    job=_ScatterJob(parts_f + parts_b, slots_f + slots_b))
    dproj = jnp.concatenate([dab, dac, dax, dq, dz, dv, dgo, dga, dgb], axis=1)
    dw_in, = _matmul("dw_in", [(h, 't', dproj, 'n')], [0], D, d_in, S, tw, tn_in, ts2,
                     [], [(BF16, nb_in)], _first)
    parts_i, slots_i = pre_sum("in", ["w_in"], [dw_in])
    dh, arrived_in = _matmul("d_h", [(dproj, 'n', w_in, 't')], [0], S, D, d_in, tm2, D, tn_in, [], [(F32, None)], _first,
                             job=_ScatterJob(parts_i, slots_i))
    dx, dsh_m, dsc_m, dg_mix = _norm_mod_bwd("norm_mix_bwd", dh, x, norm_mix_g, sc_m, dx1)
    dmod = jnp.concatenate([dsh_m, dsc_m, dgt_m, dsh_f, dsc_f, dgt_f], axis=1)
    small = dict(dmod=dmod, dg_mix=dg_mix, dg_ffn=dg_ffn, dg_final=dg_final, dlb=dlb,
                 dgn=jnp.sum(dgn, axis=0), dconv_w=dconv_w)
    big = dict(zip(ffn_names + branch_names, arrived), w_in=arrived_in)
    return loss, dx, small, big


def _adamw_math(w, g, m, v):
    m = ADAM_B1 * m + (1.0 - ADAM_B1) * g
    v = ADAM_B2 * v + (1.0 - ADAM_B2) * (g * g)
    m_hat = m / (1.0 - ADAM_B1 ** ADAM_STEP)
    v_hat = v / (1.0 - ADAM_B2 ** ADAM_STEP)
    delta = -ADAM_LR * (m_hat / (jnp.sqrt(v_hat) + ADAM_EPS) + ADAM_WD * w)
    return delta, m, v


def _adamw(name, w, g, m, v):
    R, C = w.shape
    tr = _pick(R, max(8, (256 * 1024) // C // 8 * 8), 8)
    spec = pl.BlockSpec((tr, C), lambda i: (i, 0))

    def body(w_ref, g_ref, m_ref, v_ref, d_ref, mo_ref, vo_ref):
        d, m2, v2 = _adamw_math(w_ref[...], g_ref[...], m_ref[...], v_ref[...])
        d_ref[...] = d
        mo_ref[...] = m2
        vo_ref[...] = v2

    return pl.pallas_call(
        body, name=name, grid=(R // tr,), in_specs=[spec] * 4, out_specs=[spec] * 3,
        out_shape=[jax.ShapeDtypeStruct((R, C), F32)] * 3, compiler_params=_params(("parallel",)),
    )(w, g, m, v)


def _ada_update(c_act_t, dmod, w, m, v):
    R, C = w.shape
    B = dmod.shape[0]
    tr = _pick(R, 256, 8)
    tc = _pick(C, 1536, 128)
    spec = pl.BlockSpec((tr, tc), lambda i, j: (i, j))

    def body(ct_ref, dm_ref, w_ref, m_ref, v_ref, g_ref, d_ref, mo_ref, vo_ref):
        ct = ct_ref[...]
        dm = dm_ref[...]
        g = ct[:, 0:1] * dm[0:1, :]
        for b in range(1, B):
            g = g + ct[:, b:b + 1] * dm[b:b + 1, :]
        d, m2, v2 = _adamw_math(w_ref[...], g, m_ref[...], v_ref[...])
        g_ref[...] = g
        d_ref[...] = d
        mo_ref[...] = m2
        vo_ref[...] = v2

    return pl.pallas_call(
        body, name="ada_update", grid=(R // tr, C // tc),
        in_specs=[pl.BlockSpec((tr, B), lambda i, j: (i, 0)), pl.BlockSpec((B, tc), lambda i, j: (0, j)),
                  spec, spec, spec],
        out_specs=[spec] * 4, out_shape=[jax.ShapeDtypeStruct((R, C), F32)] * 4,
        compiler_params=_params(("parallel", "parallel")),
    )(c_act_t, dmod, w, m, v)


def _prep(c_all, lb_param):
    def body(c_ref, p_ref, ca_ref, cab_ref, lb_ref):
        cv = c_ref[...]
        ca = cv * _sig(cv)
        ca_ref[...] = ca
        cab_ref[...] = ca.astype(BF16)
        lb_ref[...] = _sig(p_ref[0:1, :] - p_ref[1:2, :])

    return pl.pallas_call(
        body, name="prep",
        out_shape=[jax.ShapeDtypeStruct(c_all.shape, F32), jax.ShapeDtypeStruct(c_all.shape, BF16),
                   jax.ShapeDtypeStruct((1, lb_param.shape[1]), F32)],
    )(c_all, lb_param)


def _small_reduce(parts):
    W = parts.shape[1]

    def body(p_ref, o_ref):
        acc = p_ref[0:1, :]
        for d in range(1, N_DEV):
            acc = acc + p_ref[d:d + 1, :]
        o_ref[...] = acc

    return pl.pallas_call(body, name="small_reduce", out_shape=jax.ShapeDtypeStruct((1, W), F32))(parts)


def _lb_grad(dlb, lb):
    def body(d_ref, lb_ref, o_ref):
        t = d_ref[...] * lb_ref[...] * (1.0 - lb_ref[...])
        o_ref[0:1, :] = t
        o_ref[1:2, :] = -t

    return pl.pallas_call(body, name="lb_grad", out_shape=jax.ShapeDtypeStruct((2, dlb.shape[1]), F32))(dlb, lb)


def _place():
    x, y, c = lax.axis_index("x"), lax.axis_index("y"), lax.axis_index("c")
    chips = [(1 - x, y), (x, 1 - y), (1 - x, 1 - y)]
    return x, y, c, chips


def _allgather_rows(name, v):
    m_per, n = v.shape

    def body(x_ref, out_ref, send_sems, recv_sems, local_sem):
        x, y, c, chips = _place()
        me, sibling = (x, y, c), (x, y, 1 - c)

        def rows(px, py, pc):
            return out_ref.at[pl.ds((4 * px + 2 * py + pc) * m_per, m_per), :]

        def copy(k, block, to, src=None):
            return pltpu.make_async_remote_copy(
                src_ref=rows(*block) if src is None else src, dst_ref=rows(*block),
                send_sem=send_sems.at[k], recv_sem=recv_sems.at[k], device_id=to, device_id_type=MESH)

        mine = pltpu.make_async_copy(x_ref, rows(*me), local_sem)
        mine.start()
        first = [copy(0, me, sibling, src=x_ref)]
        first += [copy(1 + j, me, (*chip, c), src=x_ref) for j, chip in enumerate(chips)]
        for cp in first:
            cp.start()
        passed = [copy(4 + j, (*chip, c), sibling) for j, chip in enumerate(chips)]
        for j, chip in enumerate(chips):
            copy(1 + j, (*chip, c), me).wait_recv()
            passed[j].start()
        copy(0, sibling, me).wait_recv()
        for j, chip in enumerate(chips):
            copy(4 + j, (*chip, 1 - c), me).wait_recv()
        for cp in first + passed:
            cp.wait_send()
        mine.wait()

    return pl.pallas_call(
        body, name=name, out_shape=jax.ShapeDtypeStruct((N_DEV * m_per, n), v.dtype),
        in_specs=[pl.BlockSpec(memory_space=pltpu.VMEM)], out_specs=pl.BlockSpec(memory_space=pltpu.VMEM),
        scratch_shapes=[pltpu.SemaphoreType.DMA((7,)), pltpu.SemaphoreType.DMA((7,)), pltpu.SemaphoreType.DMA],
    )(v)


def _cast_place(name, w, k_idx):
    R, C = w.shape
    tr = _pick(R, max(16, (512 * 1024) // C // 16 * 16), 16)

    def body(k_ref, w_ref, o_ref):
        o_ref[...] = w_ref[...].astype(BF16)

    return pl.pallas_call(
        body, name=name,
        grid_spec=pltpu.PrefetchScalarGridSpec(
            num_scalar_prefetch=1, grid=(R // tr,),
            in_specs=[pl.BlockSpec((tr, C), lambda i, k_ref: (i, 0))],
            out_specs=pl.BlockSpec((None, tr, C), lambda i, k_ref: (k_ref[0], i, 0))),
        out_shape=jax.ShapeDtypeStruct((N_CHIPS, R, C), BF16),
        compiler_params=_params(("parallel",)),
    )(k_idx, w)


def _chip_copies(src_of, dst_of, got_of, n, sems, receiving):
    x, y, c, chips = _place()
    k_me = 2 * x + y
    send_sems, recv_sems = sems
    out = []
    for a in range(n):
        for j, chip in enumerate(chips):
            k_chip = 2 * chip[0] + chip[1]
            if receiving:
                src = dst = got_of(a, k_chip, c)
                to = (x, y, c)
            else:
                src, dst, to = src_of(a, k_chip, k_me, c), dst_of(a, k_me, c), (*chip, c)
            out.append(pltpu.make_async_remote_copy(
                src_ref=src, dst_ref=dst, send_sem=send_sems.at[3 * a + j], recv_sem=recv_sems.at[3 * a + j],
                device_id=to, device_id_type=MESH))
    return out


class _ChipJob:
    def start(self, j_in, j_out, sems):
        for send in self.copies(j_in, j_out, sems, False):
            send.start()

    def finish(self, j_in, j_out, sems):
        for recv in self.copies(j_in, j_out, sems, True):
            recv.wait_recv()
        for send in self.copies(j_in, j_out, sems, False):
            send.wait_send()


class _GatherJob(_ChipJob):
    def __init__(self, bufs, part=0, n_parts=1):
        n = len(bufs)
        self.inputs, self.n_alias = list(bufs), n
        self.sem_shapes = [pltpu.SemaphoreType.DMA((3 * n,)), pltpu.SemaphoreType.DMA((3 * n,))]
        self.half = [b.shape[1] // 2 for b in bufs]
        self.part, self.n_parts = part, n_parts

    def copies(self, j_in, j_out, sems, receiving):
        def half(a, k, c):
            rows = self.half[a] // self.n_parts
            return j_out[a].at[k, pl.ds(c * self.half[a] + self.part * rows, rows)]

        return _chip_copies(lambda a, k_chip, k_me, c: half(a, k_me, c), half, half, len(self.half), sems, receiving)


class _ScatterJob(_ChipJob):
    def __init__(self, parts, slots):
        n = len(parts)
        self.n = n
        self.inputs, self.n_alias = list(parts) + list(slots), n
        self.sem_shapes = [pltpu.SemaphoreType.DMA((3 * n,)), pltpu.SemaphoreType.DMA((3 * n,))]

    def copies(self, j_in, j_out, sems, receiving):
        return _chip_copies(lambda a, k_chip, k_me, c: j_in[a].at[k_chip], lambda a, k_me, c: j_out[a].at[k_me],
                            lambda a, k_chip, c: j_out[a].at[k_chip], self.n, sems, receiving)


def _forward_halves(name, bufs):
    n = len(bufs)

    def body(*refs):
        out_refs = refs[n:2 * n]
        send_sems, recv_sems = refs[2 * n:]
        x, y, c, chips = _place()
        started, waits = [], []
        for a in range(n):
            rh = bufs[a].shape[1] // 2
            for j, chip in enumerate(chips):
                k_chip = 2 * chip[0] + chip[1]
                got = out_refs[a].at[k_chip, pl.ds(c * rh, rh)]
                cp = pltpu.make_async_remote_copy(src_ref=got, dst_ref=got, send_sem=send_sems.at[3 * a + j],
                                                  recv_sem=recv_sems.at[3 * a + j], device_id=(x, y, 1 - c),
                                                  device_id_type=MESH)
                cp.start()
                started.append(cp)
                other = out_refs[a].at[k_chip, pl.ds((1 - c) * rh, rh)]
                waits.append(pltpu.make_async_remote_copy(
                    src_ref=other, dst_ref=other, send_sem=send_sems.at[3 * a + j], recv_sem=recv_sems.at[3 * a + j],
                    device_id=(x, y, c), device_id_type=MESH))
        for cp in waits:
            cp.wait_recv()
        for cp in started:
            cp.wait_send()

    return pl.pallas_call(
        body, name=name, out_shape=[jax.ShapeDtypeStruct(b.shape, b.dtype) for b in bufs],
        in_specs=[_HBM] * n, out_specs=[_HBM] * n, input_output_aliases={a: a for a in range(n)},
        scratch_shapes=[pltpu.SemaphoreType.DMA((3 * n,)), pltpu.SemaphoreType.DMA((3 * n,))],
    )(*bufs)


def _proj_gather(h, buf, order, job):
    S, D = h.shape
    nb = buf.shape[2]
    tm = _pick(S, 1024, 16)
    tn = _pick(nb, 1408, 128)
    per = nb // tn
    nj, ni = N_CHIPS * per, S // tm
    rh = D // 2

    def body(order_ref, h_ref, buf_in, proj_ref, buf_ref, wbuf, tile_sems, ici_send, ici_recv, d2d_send, d2d_recv):
        j, i = pl.program_id(0), pl.program_id(1)
        x, y, c, chips = _place()
        k_me = 2 * x + y

        def half(k, hc):
            return buf_ref.at[k, pl.ds(hc * rh, rh)]

        def ici(q, receiving):
            k_chip = 2 * chips[q][0] + chips[q][1]
            src, to = (half(k_chip, c), (x, y, c)) if receiving else (half(k_me, c), (*chips[q], c))
            return pltpu.make_async_remote_copy(src_ref=src, dst_ref=src, send_sem=ici_send.at[q],
                                                recv_sem=ici_recv.at[q], device_id=to, device_id_type=MESH)

        def d2d(q, receiving):
            k_chip = 2 * chips[q][0] + chips[q][1]
            src, to = (half(k_chip, 1 - c), (x, y, c)) if receiving else (half(k_chip, c), (x, y, 1 - c))
            return pltpu.make_async_remote_copy(src_ref=src, dst_ref=src, send_sem=d2d_send.at[q],
                                                recv_sem=d2d_recv.at[q], device_id=to, device_id_type=MESH)

        def tile_copy(t):
            col = pl.multiple_of((t % per) * tn, 128)
            return pltpu.make_async_copy(buf_ref.at[order_ref[t // per], :, pl.ds(col, tn)], wbuf.at[t % 2],
                                         tile_sems.at[t % 2])

        @pl.when(jnp.logical_and(j == 0, i == 0))
        def _():
            ici(0, False).start()
            ici(1, False).start()
            tile_copy(0).start()

        @pl.when(i == 0)
        def _():
            tile_copy(j).wait()
            for q in range(3):
                @pl.when(j + 1 == (q + 1) * per)
                def _():
                    ici(q, True).wait_recv()
                    d2d(q, False).start()
                    if q == 0:
                        ici(0, False).wait_send()
                        ici(1, False).wait_send()
                        ici(2, False).start()
                    d2d(q, True).wait_recv()

            @pl.when(j + 1 < nj)
            def _():
                tile_copy(j + 1).start()

        proj_ref[...] = jnp.dot(h_ref[...], wbuf[j % 2], preferred_element_type=F32)

        @pl.when(jnp.logical_and(j == nj - 1, i == ni - 1))
        def _():
            ici(2, False).wait_send()
            for q in range(3):
                d2d(q, False).wait_send()

    grid = (nj, ni)
    in_specs = [pl.BlockSpec((tm, D), lambda j, i, o: (i, 0)), _HBM]
    args = [h, buf]
    out_specs = [pl.BlockSpec((tm, tn), lambda j, i, o: (i, o[j // per] * per + j % per)), _HBM]
    out_shape = [jax.ShapeDtypeStruct((S, N_CHIPS * nb), F32), jax.ShapeDtypeStruct(buf.shape, buf.dtype)]
    scratch = [pltpu.VMEM((2, D, tn), BF16), pltpu.SemaphoreType.DMA((2,))] + [pltpu.SemaphoreType.DMA((3,))] * 4
    aliases, wrap = _host(job, grid, in_specs, args, out_specs, out_shape, scratch, n_prefetch=1)
    aliases[2] = 1
    return pl.pallas_call(
        wrap(body), name="proj",
        grid_spec=pltpu.PrefetchScalarGridSpec(num_scalar_prefetch=1, grid=grid, in_specs=in_specs, out_specs=out_specs,
                                               scratch_shapes=scratch),
        out_shape=out_shape, input_output_aliases=aliases, compiler_params=_params(("arbitrary", "arbitrary")),
    )(order, *args)


def _swap_halves(name, grads):
    n = len(grads)

    def body(*refs):
        in_refs, out_refs = refs[:n], refs[n:2 * n]
        send_sems, recv_sems = refs[2 * n:]
        x, y, c, _ = _place()
        cps = []
        for a in range(n):
            rh = grads[a].shape[1] // 2
            cp = pltpu.make_async_remote_copy(
                src_ref=in_refs[a].at[:, pl.ds((1 - c) * rh, rh)], dst_ref=out_refs[a],
                send_sem=send_sems.at[a], recv_sem=recv_sems.at[a], device_id=(x, y, 1 - c), device_id_type=MESH)
            cp.start()
            cps.append(cp)
        for cp in cps:
            cp.wait()

    return pl.pallas_call(
        body, name=name,
        out_shape=[jax.ShapeDtypeStruct((N_CHIPS, g.shape[1] // 2, g.shape[2]), g.dtype) for g in grads],
        in_specs=[_HBM] * n, out_specs=[_HBM] * n,
        scratch_shapes=[pltpu.SemaphoreType.DMA((n,)), pltpu.SemaphoreType.DMA((n,))],
    )(*grads)


def _pair_sum(name, g, q, c_idx):
    _, R, C = g.shape
    rh = R // 2
    tr = _pick(rh, max(16, (512 * 1024) // C // 16 * 16), 16)
    nh = rh // tr

    def body(c_ref, g_ref, q_ref, o_ref, o2_ref):
        s = (g_ref[...].astype(F32) + q_ref[...].astype(F32)).astype(BF16)
        o_ref[...] = s
        o2_ref[...] = s

    out_spec = pl.BlockSpec((None, tr, C), lambda k, i, c_ref: (k, i, 0))
    return pl.pallas_call(
        body, name=name,
        grid_spec=pltpu.PrefetchScalarGridSpec(
            num_scalar_prefetch=1, grid=(N_CHIPS, nh),
            in_specs=[pl.BlockSpec((None, tr, C), lambda k, i, c_ref: (k, c_ref[0] * nh + i, 0)),
                      pl.BlockSpec((None, tr, C), lambda k, i, c_ref: (k, i, 0))],
            out_specs=[out_spec, out_spec]),
        out_shape=[jax.ShapeDtypeStruct((N_CHIPS, rh, C), BF16)] * 2,
        compiler_params=_params(("parallel", "parallel")),
    )(c_idx, g, q)


def _sum_chips(name, r, c_idx):
    _, rh, C = r.shape
    tr = _pick(rh, max(16, (256 * 1024) // C // 16 * 16), 16)

    def body(c_ref, r_ref, o_ref):
        acc = r_ref[0].astype(F32)
        for k in range(1, N_CHIPS):
            acc = acc + r_ref[k].astype(F32)
        o_ref[...] = acc

    return pl.pallas_call(
        body, name=name,
        grid_spec=pltpu.PrefetchScalarGridSpec(
            num_scalar_prefetch=1, grid=(rh // tr,),
            in_specs=[pl.BlockSpec((N_CHIPS, tr, C), lambda i, c_ref: (0, i, 0))],
            out_specs=pl.BlockSpec((None, tr, C), lambda i, c_ref: (c_ref[0], i, 0))),
        out_shape=jax.ShapeDtypeStruct((2, rh, C), F32), compiler_params=_params(("parallel",)),
    )(c_idx, r)


def _share_halves(bufs):
    n = len(bufs)

    def body(*refs):
        out_refs = refs[n:2 * n]
        send_sems, recv_sems = refs[2 * n:]
        x, y, c, _ = _place()
        cps = []
        for a in range(n):
            cp = pltpu.make_async_remote_copy(
                src_ref=out_refs[a].at[c], dst_ref=out_refs[a].at[c], send_sem=send_sems.at[a],
                recv_sem=recv_sems.at[a], device_id=(x, y, 1 - c), device_id_type=MESH)
            cp.start()
            cps.append(cp)
        for a, cp in enumerate(cps):
            got = out_refs[a].at[1 - c]
            pltpu.make_async_remote_copy(src_ref=got, dst_ref=got, send_sem=send_sems.at[a], recv_sem=recv_sems.at[a],
                                         device_id=(x, y, c), device_id_type=MESH).wait_recv()
        for cp in cps:
            cp.wait_send()

    return pl.pallas_call(
        body, name="share_halves",
        out_shape=[jax.ShapeDtypeStruct(b.shape, b.dtype) for b in bufs],
        in_specs=[_HBM] * n, out_specs=[_HBM] * n, input_output_aliases={a: a for a in range(n)},
        scratch_shapes=[pltpu.SemaphoreType.DMA((n,)), pltpu.SemaphoreType.DMA((n,))],
    )(*bufs)


_BIG = ("w_in", "w_conv_out", "w_hgrn_out", "w_o", "w_ffn_gate", "w_ffn_up", "w_ffn_down")
_ROW_SHARDED = ("w_o", "w_ffn_down")
_WEIGHTS = ("w_ada", "b_ada", "norm_mix_g", "w_in", "conv_w", "lb_param", "gnorm_g", "w_conv_out", "w_hgrn_out",
            "w_o", "norm_ffn_g", "w_ffn_gate", "w_ffn_up", "w_ffn_down", "norm_final_g")


def _pack_rows(pieces):
    flat = jnp.concatenate([p.reshape(-1) for p in pieces])
    pad = (-flat.shape[0]) % 1024
    return jnp.pad(flat, (0, pad)).reshape(8, -1)


def kernel(x, c, w_ada, b_ada, norm_mix_g, w_in, conv_w, lb_param, gnorm_g, w_conv_out, w_hgrn_out, w_o, norm_ffn_g, w_ffn_gate, w_ffn_up, w_ffn_down, norm_final_g, loss_target, m_w_ada, m_b_ada, m_norm_mix_g, m_w_in, m_conv_w, m_lb_param, m_gnorm_g, m_w_conv_out, m_w_hgrn_out, m_w_o, m_norm_ffn_g, m_w_ffn_gate, m_w_ffn_up, m_w_ffn_down, m_norm_final_g, v_w_ada, v_b_ada, v_norm_mix_g, v_w_in, v_conv_w, v_lb_param, v_gnorm_g, v_w_conv_out, v_w_hgrn_out, v_w_o, v_norm_ffn_g, v_w_ffn_gate, v_w_ffn_up, v_w_ffn_down, v_norm_final_g):
    w = dict(w_ada=w_ada, b_ada=b_ada, norm_mix_g=norm_mix_g, w_in=w_in, conv_w=conv_w, lb_param=lb_param,
             gnorm_g=gnorm_g, w_conv_out=w_conv_out, w_hgrn_out=w_hgrn_out, w_o=w_o, norm_ffn_g=norm_ffn_g,
             w_ffn_gate=w_ffn_gate, w_ffn_up=w_ffn_up, w_ffn_down=w_ffn_down, norm_final_g=norm_final_g)
    m = dict(w_ada=m_w_ada, b_ada=m_b_ada, norm_mix_g=m_norm_mix_g, w_in=m_w_in, conv_w=m_conv_w, lb_param=m_lb_param,
             gnorm_g=m_gnorm_g, w_conv_out=m_w_conv_out, w_hgrn_out=m_w_hgrn_out, w_o=m_w_o, norm_ffn_g=m_norm_ffn_g,
             w_ffn_gate=m_w_ffn_gate, w_ffn_up=m_w_ffn_up, w_ffn_down=m_w_ffn_down, norm_final_g=m_norm_final_g)
    v = dict(w_ada=v_w_ada, b_ada=v_b_ada, norm_mix_g=v_norm_mix_g, w_in=v_w_in, conv_w=v_conv_w, lb_param=v_lb_param,
             gnorm_g=v_gnorm_g, w_conv_out=v_w_conv_out, w_hgrn_out=v_w_hgrn_out, w_o=v_w_o, norm_ffn_g=v_norm_ffn_g,
             w_ffn_gate=v_w_ffn_gate, w_ffn_up=v_w_ffn_up, w_ffn_down=v_w_ffn_down, norm_final_g=v_norm_final_g)
    two_d = lambda a: a.reshape((-1, a.shape[-1])) if a.ndim != 2 else a
    shapes = {k: a.shape for k, a in w.items()}
    w, m, v = ({k: two_d(a) for k, a in t.items()} for t in (w, m, v))
    w["lb_param"], m["lb_param"], v["lb_param"] = lb_param, m_lb_param, v_lb_param
    S, D = x.shape[1], x.shape[2]
    d_conv = D // 2
    ix, iy, ic = lax.axis_index("x"), lax.axis_index("y"), lax.axis_index("c")
    k_me = 2 * ix + iy
    dev = 2 * k_me + ic
    cw_shard = w["conv_w"].shape[1]
    ada_cols = w["w_ada"].shape[1]

    got = _allgather_rows("gather_cond", _pack_rows([c, w["conv_w"]])).reshape(N_DEV, -1)
    c_all = got[:, :D]
    conv_full = got[::2, D:D + 3 * cw_shard].reshape(N_CHIPS, 3, cw_shard).transpose(1, 0, 2).reshape(3, -1)
    c_act, c_act_b, lb = _prep(c_all, lb_param)
    b_cols = lax.dynamic_slice(w["b_ada"], (0, k_me * ada_cols), (1, ada_cols))
    mod_cols, = _matmul("ada_fwd", [(c_act_b, 'n', w["w_ada"].astype(BF16), 'n')], [0], N_DEV, ada_cols, D,
                        N_DEV, _pick(ada_cols, 1536, 128), D, [(b_cols, 'row', 0)], [(F32, None)],
                        lambda accs, ex: [accs[0] + ex[0]])
    mod_all = _allgather_rows("gather_mod", mod_cols).reshape(N_CHIPS, 2, N_DEV, ada_cols)[:, 0]
    mod_all = mod_all.transpose(1, 0, 2).reshape(N_DEV, -1)
    mod = lax.dynamic_slice(mod_all, (dev, 0), (1, mod_all.shape[1]))
    k_idx = jnp.reshape(k_me, (1,)).astype(jnp.int32)
    c_idx = jnp.reshape(ic, (1,)).astype(jnp.int32)
    bufs = {k: _cast_place("cast_" + k, w[k], k_idx) for k in _BIG}
    order = jnp.stack([k_me, 2 * (1 - ix) + iy, 2 * ix + (1 - iy), 2 * (1 - ix) + (1 - iy)]).astype(jnp.int32)

    loss, dx, small, arrived = _local_step(
        x[0], loss_target[0], mod, w["norm_mix_g"], lb, w["gnorm_g"], w["norm_ffn_g"], w["norm_final_g"], conv_full,
        bufs, c_idx, order)

    pieces = [small["dmod"], small["dg_mix"], small["dg_ffn"], small["dg_final"], small["dlb"], small["dgn"],
              small["dconv_w"], loss]
    sizes = [p.size for p in pieces]
    parts = _allgather_rows("gather_small", _pack_rows(pieces)).reshape(N_DEV, -1)
    total = _small_reduce(parts)
    offs = [0]
    for s in sizes:
        offs.append(offs[-1] + s)
    tot = [total[:, offs[i]:offs[i + 1]] for i in range(len(sizes))]
    dmod_all = parts[:, :sizes[0]]
    grads = {
        "b_ada": tot[0], "norm_mix_g": tot[1], "norm_ffn_g": tot[2], "norm_final_g": tot[3],
        "lb_param": _lb_grad(tot[4], lb), "gnorm_g": tot[5],
        "conv_w": lax.dynamic_slice(tot[6].reshape(3, -1), (0, k_me * cw_shard), (3, cw_shard)),
    }
    loss_out = tot[7][0, 0]

    halves = [_sum_chips("sum_chips_" + k, arrived[k], c_idx) for k in _BIG]
    whole = _share_halves(halves)
    for k, g in zip(_BIG, whole):
        grads[k] = g.reshape(-1, g.shape[-1])

    delta, new_m, new_v = {}, {}, {}
    dmod_cols = lax.dynamic_slice(dmod_all, (0, k_me * ada_cols), (N_DEV, ada_cols))
    grads["w_ada"], delta["w_ada"], new_m["w_ada"], new_v["w_ada"] = _ada_update(
        c_act.T, dmod_cols, w["w_ada"], m["w_ada"], v["w_ada"])
    for k in _WEIGHTS:
        if k != "w_ada":
            delta[k], new_m[k], new_v[k] = _adamw("adamw_" + k, w[k], grads[k], m[k], v[k])
    outs = [loss_out, dx.reshape(x.shape)]
    for t in (grads, delta, new_m, new_v):
        outs += [t[k].reshape(shapes[k]) for k in _WEIGHTS]
    return tuple(outs)
```

```python
import functools

import jax
import jax.numpy as jnp
from jax import lax
from jax.experimental import pallas as pl
from jax.experimental.pallas import tpu as pltpu

F32 = jnp.float32
BF16 = jnp.bfloat16
MESH = pl.DeviceIdType.MESH

EPS = 1e-6
HEAD = 128
BLK = 16
N_CHIPS = 4
N_DEV = 8
VMEM_LIMIT_BYTES = 56 * 1024 * 1024

ADAM_LR = 0.001
ADAM_B1 = 0.9
ADAM_B2 = 0.999
ADAM_EPS = 1e-08
ADAM_WD = 0.01
ADAM_STEP = 10


def _pick(dim, pref, mult):
    if dim <= pref:
        return dim
    t = (pref // mult) * mult
    while t >= mult:
        if dim % t == 0:
            return t
        t -= mult
    return dim


def _sig(x):
    return 1.0 / (1.0 + jnp.exp(-x))


def _params(sem):
    return pltpu.CompilerParams(dimension_semantics=sem, vmem_limit_bytes=VMEM_LIMIT_BYTES)


def _gj(j, i, k):
    return j


def _gk(j, i, k):
    return k


def _wspec(arr, rt, ct, r_of, c_of):
    if arr.ndim == 2:
        return pl.BlockSpec((rt, ct), lambda j, i, k: (r_of(j, i, k), c_of(j, i, k)))
    per = arr.shape[2] // ct
    assert arr.shape[2] % ct == 0
    return pl.BlockSpec((None, rt, ct),
                        lambda j, i, k: (c_of(j, i, k) // per, r_of(j, i, k), c_of(j, i, k) % per))


_HBM = pl.BlockSpec(memory_space=pltpu.HBM)


def _host(job, grid, in_specs, args, out_specs, out_shape, scratch, n_prefetch=0):
    if job is None:
        return {}, (lambda body: body)
    n_in, n_out, n_scr = len(args), len(out_shape), len(scratch)
    n_job, n_alias = len(job.inputs), job.n_alias
    in_specs += [_HBM] * n_job
    args += job.inputs
    out_specs += [_HBM] * n_alias
    out_shape += [jax.ShapeDtypeStruct(a.shape, a.dtype) for a in job.inputs[n_job - n_alias:]]
    scratch += job.sem_shapes
    aliases = {n_prefetch + n_in + n_job - n_alias + t: n_out + t for t in range(n_alias)}

    def wrap(body):
        def hosted(*refs):
            pre, refs = refs[:n_prefetch], refs[n_prefetch:]
            ins, refs = refs[:n_in], refs[n_in:]
            j_in, refs = refs[:n_job], refs[n_job:]
            outs, refs = refs[:n_out], refs[n_out:]
            j_out, refs = refs[:n_alias], refs[n_alias:]
            scr, sems = refs[:n_scr], refs[n_scr:]
            first = functools.reduce(jnp.logical_and, [pl.program_id(d) == 0 for d in range(len(grid))])
            last = functools.reduce(jnp.logical_and, [pl.program_id(d) == grid[d] - 1 for d in range(len(grid))])

            @pl.when(first)
            def _():
                job.start(j_in, j_out, sems)

            body(*pre, *ins, *outs, *scr)

            @pl.when(last)
            def _():
                job.finish(j_in, j_out, sems)

        return hosted

    return aliases, wrap


def _matmul(name, pairs, acc_of, M, N, K, tm, tn, tk, extras, outs, epilogue, job=None, rows_outer=False):
    assert M % tm == 0 and N % tn == 0 and K % tk == 0, (name, M, N, K, tm, tn, tk)
    nj, ni, nk = N // tn, M // tm, K // tk
    n_pairs, n_extra, n_out = len(pairs), len(extras), len(outs)
    n_acc = max(acc_of) + 1
    in_specs, args, dims = [], [], []
    lhs_of, seen = [], {}
    for a, am, b, bm in pairs:
        if (id(a), am) not in seen:
            seen[id(a), am] = len(args)
            args.append(a)
            if am == 'n':
                in_specs.append(pl.BlockSpec((tm, tk), lambda j, i, k: (i, k)))
            else:
                in_specs.append(pl.BlockSpec((tk, tm), lambda j, i, k: (k, i)))
        lhs_of.append(seen[id(a), am])
    n_lhs = len(args)
    for a, am, b, bm in pairs:
        if bm == 'n':
            in_specs.append(_wspec(b, tk, tn, _gk, _gj))
        else:
            in_specs.append(_wspec(b, tn, tk, _gj, _gk))
        dims.append((((1 if am == 'n' else 0,), (0 if bm == 'n' else 1,)), ((), ())))
        args.append(b)
    for e, kind, off in extras:
        assert off % tn == 0, (name, off, tn)
        o = off // tn
        if kind == 'tile':
            in_specs.append(pl.BlockSpec((tm, tn), lambda j, i, k, o=o: (i, j + o)))
        else:
            in_specs.append(pl.BlockSpec((1, tn), lambda j, i, k, o=o: (0, j + o)))
        args.append(e)
    out_shape, out_specs = [], []
    for dt, nb in outs:
        if nb is None:
            out_shape.append(jax.ShapeDtypeStruct((M, N), dt))
            out_specs.append(pl.BlockSpec((tm, tn), lambda j, i, k: (i, j)))
        else:
            assert nb % tn == 0 and N % nb == 0
            per = nb // tn
            out_shape.append(jax.ShapeDtypeStruct((N // nb, M, nb), dt))
            out_specs.append(pl.BlockSpec((None, tm, tn), lambda j, i, k, per=per: (j // per, i, j % per)))

    def body(*refs):
        n_ab = n_lhs + n_pairs
        e_refs = refs[n_ab:n_ab + n_extra]
        o_refs = refs[n_ab + n_extra:n_ab + n_extra + n_out]
        acc_refs = refs[n_ab + n_extra + n_out:]
        sums = [None] * n_acc
        for p in range(n_pairs):
            prod = lax.dot_general(refs[lhs_of[p]][...], refs[n_lhs + p][...], dims[p], preferred_element_type=F32)
            a = acc_of[p]
            sums[a] = prod if sums[a] is None else sums[a] + prod

        def finish(accs):
            res = epilogue(accs, [e[...] for e in e_refs])
            for o_ref, r in zip(o_refs, res):
                o_ref[...] = r.astype(o_ref.dtype)

        if nk == 1:
            finish(sums)
        else:
            k = pl.program_id(2)

            @pl.when(k == 0)
            def _():
                for a in range(n_acc):
                    acc_refs[a][...] = sums[a]

            @pl.when(k > 0)
            def _():
                for a in range(n_acc):
                    acc_refs[a][...] += sums[a]

            @pl.when(k == nk - 1)
            def _():
                finish([acc_refs[a][...] for a in range(n_acc)])

    scratch = [pltpu.VMEM((tm, tn), F32) for _ in range(n_acc)] if nk > 1 else []
    grid = (nj, ni, nk)
    if rows_outer:
        grid = (ni, nj, nk)
        swap = lambda spec: pl.BlockSpec(spec.block_shape, lambda i, j, k, f=spec.index_map: f(j, i, k))
        in_specs, out_specs = [swap(s) for s in in_specs], [swap(s) for s in out_specs]
    aliases, wrap = _host(job, grid, in_specs, args, out_specs, out_shape, scratch)
    sem = ("parallel", "parallel", "arbitrary") if job is None else ("arbitrary",) * 3
    return pl.pallas_call(
        wrap(body), name=name, grid=grid, in_specs=in_specs, out_specs=out_specs, out_shape=out_shape,
        scratch_shapes=scratch, input_output_aliases=aliases, compiler_params=_params(sem),
    )(*args)


def _row_spec(tr, d):
    return pl.BlockSpec((tr, d), lambda i: (i, 0))


def _vec_spec(d):
    return pl.BlockSpec((1, d), lambda i: (0, 0))


def _norm_mod(name, x, g, sc, sh):
    S, D = x.shape
    tr = _pick(S, 256, 8)

    def body(x_ref, g_ref, sc_ref, sh_ref, h_ref):
        xv = x_ref[...]
        r = lax.rsqrt(jnp.mean(xv * xv, axis=-1, keepdims=True) + EPS)
        h_ref[...] = ((xv * r) * g_ref[...] * (1.0 + sc_ref[...]) + sh_ref[...]).astype(BF16)

    return pl.pallas_call(
        body, name=name, grid=(S // tr,),
        in_specs=[_row_spec(tr, D), _vec_spec(D), _vec_spec(D), _vec_spec(D)],
        out_specs=_row_spec(tr, D), out_shape=jax.ShapeDtypeStruct((S, D), BF16),
        compiler_params=_params(("parallel",)),
    )(x, g, sc, sh)


def _loss_head(x2, target, g, ff, gt):
    S, D = x2.shape
    tr = _pick(S, 256, 8)

    def body(x_ref, t_ref, g_ref, ff_ref, gt_ref, dx_ref, dffb_ref, loss_ref, dgt_ref, dg_ref):
        i = pl.program_id(0)

        @pl.when(i == 0)
        def _():
            loss_ref[...] = jnp.zeros_like(loss_ref)
            dgt_ref[...] = jnp.zeros_like(dgt_ref)
            dg_ref[...] = jnp.zeros_like(dg_ref)

        xv = x_ref[...]
        gv = g_ref[...]
        r = lax.rsqrt(jnp.mean(xv * xv, axis=-1, keepdims=True) + EPS)
        xh = xv * r
        err = xh * gv - t_ref[...]
        loss_ref[...] += 0.5 * jnp.sum(jnp.mean(err * err, axis=-1, keepdims=True))
        dy = err * (1.0 / D)
        dg_ref[...] += jnp.sum(dy * xh, axis=0, keepdims=True)
        dxh = dy * gv
        dx = r * (dxh - xh * jnp.mean(dxh * xh, axis=-1, keepdims=True))
        dx_ref[...] = dx
        dffb_ref[...] = (dx * gt_ref[...]).astype(BF16)
        dgt_ref[...] += jnp.sum(dx * ff_ref[...], axis=0, keepdims=True)

    return pl.pallas_call(
        body, name="loss_head", grid=(S // tr,),
        in_specs=[_row_spec(tr, D), _row_spec(tr, D), _vec_spec(D), _row_spec(tr, D), _vec_spec(D)],
        out_specs=[_row_spec(tr, D), _row_spec(tr, D), pl.BlockSpec((1, 128), lambda i: (0, 0)),
                   _vec_spec(D), _vec_spec(D)],
        out_shape=[jax.ShapeDtypeStruct((S, D), F32), jax.ShapeDtypeStruct((S, D), BF16),
                   jax.ShapeDtypeStruct((1, 128), F32), jax.ShapeDtypeStruct((1, D), F32),
                   jax.ShapeDtypeStruct((1, D), F32)],
        compiler_params=_params(("arbitrary",)),
    )(x2, target, g, ff, gt)


def _norm_mod_bwd(name, dh, x, g, sc, dres, gated=None):
    S, D = x.shape
    tr = _pick(S, 256, 8)
    n = S // tr
    with_gate = gated is not None

    def body(*refs):
        if with_gate:
            dh_ref, x_ref, g_ref, sc_ref, dres_ref, mo_ref, gt_ref, dx_ref, dsh_ref, dsc_ref, dg_ref, dmob_ref, dgt_ref = refs
        else:
            dh_ref, x_ref, g_ref, sc_ref, dres_ref, dx_ref, dsh_ref, dsc_ref, dg_ref = refs
        i = pl.program_id(0)

        @pl.when(i == 0)
        def _():
            dsh_ref[...] = jnp.zeros_like(dsh_ref)
            dsc_ref[...] = jnp.zeros_like(dsc_ref)
            if with_gate:
                dgt_ref[...] = jnp.zeros_like(dgt_ref)

        xv = x_ref[...]
        dhv = dh_ref[...]
        gv = g_ref[...]
        scale = 1.0 + sc_ref[...]
        r = lax.rsqrt(jnp.mean(xv * xv, axis=-1, keepdims=True) + EPS)
        xh = xv * r
        dsh_ref[...] += jnp.sum(dhv, axis=0, keepdims=True)
        dsc_ref[...] += jnp.sum(dhv * xh, axis=0, keepdims=True)
        dxh = dhv * (gv * scale)
        dx = dres_ref[...] + r * (dxh - xh * jnp.mean(dxh * xh, axis=-1, keepdims=True))
        dx_ref[...] = dx
        if with_gate:
            dmob_ref[...] = (dx * gt_ref[...]).astype(BF16)
            dgt_ref[...] += jnp.sum(dx * mo_ref[...], axis=0, keepdims=True)

        @pl.when(i == n - 1)
        def _():
            t = dsc_ref[...]
            dg_ref[...] = t * scale
            dsc_ref[...] = t * gv

    in_specs = [_row_spec(tr, D), _row_spec(tr, D), _vec_spec(D), _vec_spec(D), _row_spec(tr, D)]
    args = [dh, x, g, sc, dres]
    out_specs = [_row_spec(tr, D), _vec_spec(D), _vec_spec(D), _vec_spec(D)]
    out_shape = [jax.ShapeDtypeStruct((S, D), F32)] + [jax.ShapeDtypeStruct((1, D), F32)] * 3
    if with_gate:
        in_specs += [_row_spec(tr, D), _vec_spec(D)]
        args += list(gated)
        out_specs += [_row_spec(tr, D), _vec_spec(D)]
        out_shape += [jax.ShapeDtypeStruct((S, D), BF16), jax.ShapeDtypeStruct((1, D), F32)]
    return pl.pallas_call(
        body, name=name, grid=(n,), in_specs=in_specs, out_specs=out_specs, out_shape=out_shape,
        compiler_params=_params(("arbitrary",)),
    )(*args)


CONV_ROWS = 256


def _col_spec(S, off_cols):
    o = off_cols // HEAD
    return pl.BlockSpec((S, HEAD), lambda c, o=o: (0, c + o))


def _conv_taps(u, u_prev, rid):
    s1 = jnp.where(rid >= 1, pltpu.roll(u, 1, 0), pltpu.roll(u_prev, 1, 0))
    s2 = jnp.where(rid >= 2, pltpu.roll(u, 2, 0), pltpu.roll(u_prev, 2, 0))
    return s1, s2


def _conv_fwd(proj, conv_w, d_conv):
    S = proj.shape[0]
    R = min(CONV_ROWS, S)
    nr = S // R

    def body(ab_ref, ac_ref, ax_ref, w_ref, ya_ref):
        w0, w1, w2 = w_ref[0:1, :], w_ref[1:2, :], w_ref[2:3, :]
        rid = lax.broadcasted_iota(jnp.int32, (R, HEAD), 0)

        def step(c, carry):
            rows = pl.ds(pl.multiple_of(c * R, R), R)
            prev = pl.ds(pl.multiple_of(jnp.maximum(c - 1, 0) * R, R), R)
            u = ac_ref[rows, :] * ax_ref[rows, :]
            u_prev = jnp.where(c > 0, ac_ref[prev, :] * ax_ref[prev, :], 0.0)
            s1, s2 = _conv_taps(u, u_prev, rid)
            y = w0 * s2 + w1 * s1 + w2 * u
            ya_ref[rows, :] = (ab_ref[rows, :] * y).astype(BF16)
            return carry

        lax.fori_loop(0, nr, step, 0)

    return pl.pallas_call(
        body, name="conv_fwd", grid=(d_conv // HEAD,),
        in_specs=[_col_spec(S, 0), _col_spec(S, d_conv), _col_spec(S, 2 * d_conv),
                  pl.BlockSpec((3, HEAD), lambda c: (0, c))],
        out_specs=pl.BlockSpec((S, HEAD), lambda c: (0, c)),
        out_shape=jax.ShapeDtypeStruct((S, d_conv), BF16),
        compiler_params=_params(("parallel",)),
    )(proj, proj, proj, conv_w)


def _conv_bwd(dya, proj, conv_w, d_conv):
    S = proj.shape[0]
    R = min(CONV_ROWS, S)
    nr = S // R

    def body(dya_ref, ab_ref, ac_ref, ax_ref, w_ref, dab_ref, dac_ref, dax_ref, dw_ref):
        w0, w1, w2 = w_ref[0:1, :], w_ref[1:2, :], w_ref[2:3, :]
        rid = lax.broadcasted_iota(jnp.int32, (R, HEAD), 0)

        def step(c, carry):
            dw0, dw1, dw2 = carry
            rows = pl.ds(pl.multiple_of(c * R, R), R)
            prev = pl.ds(pl.multiple_of(jnp.maximum(c - 1, 0) * R, R), R)
            nxt = pl.ds(pl.multiple_of(jnp.minimum(c + 1, nr - 1) * R, R), R)
            a_c, a_x, a_b = ac_ref[rows, :], ax_ref[rows, :], ab_ref[rows, :]
            u = a_c * a_x
            u_prev = jnp.where(c > 0, ac_ref[prev, :] * ax_ref[prev, :], 0.0)
            s1, s2 = _conv_taps(u, u_prev, rid)
            y = w0 * s2 + w1 * s1 + w2 * u
            dy = dya_ref[rows, :]
            dab_ref[rows, :] = (dy * y).astype(BF16)
            dyc = dy * a_b
            dyc_next = jnp.where(c < nr - 1, dya_ref[nxt, :] * ab_ref[nxt, :], 0.0)
            t1 = jnp.where(rid < R - 1, pltpu.roll(dyc, R - 1, 0), pltpu.roll(dyc_next, R - 1, 0))
            t2 = jnp.where(rid < R - 2, pltpu.roll(dyc, R - 2, 0), pltpu.roll(dyc_next, R - 2, 0))
            du = w2 * dyc + w1 * t1 + w0 * t2
            dac_ref[rows, :] = (du * a_x).astype(BF16)
            dax_ref[rows, :] = (du * a_c).astype(BF16)
            dw0 = dw0 + jnp.sum(dyc * s2, axis=0, keepdims=True)
            dw1 = dw1 + jnp.sum(dyc * s1, axis=0, keepdims=True)
            dw2 = dw2 + jnp.sum(dyc * u, axis=0, keepdims=True)
            return dw0, dw1, dw2

        z = jnp.zeros((1, HEAD), F32)
        dw0, dw1, dw2 = lax.fori_loop(0, nr, step, (z, z, z))
        dw_ref[0:1, :] = dw0
        dw_ref[1:2, :] = dw1
        dw_ref[2:3, :] = dw2

    col = pl.BlockSpec((S, HEAD), lambda c: (0, c))
    return pl.pallas_call(
        body, name="conv_bwd", grid=(d_conv // HEAD,),
        in_specs=[col, _col_spec(S, 0), _col_spec(S, d_conv), _col_spec(S, 2 * d_conv),
                  pl.BlockSpec((3, HEAD), lambda c: (0, c))],
        out_specs=[col, col, col, pl.BlockSpec((3, HEAD), lambda c: (0, c))],
        out_shape=[jax.ShapeDtypeStruct((S, d_conv), BF16)] * 3 + [jax.ShapeDtypeStruct((3, d_conv), F32)],
        compiler_params=_params(("parallel",)),
    )(dya, proj, proj, proj, conv_w)


HGRN_ROWS = 256
HGRN_FWD_SUB = 64
HGRN_BWD_SUB = 32


def _block_cumsum(x, pos):
    for s in (1, 2, 4, 8):
        x = x + jnp.where(pos >= s, pltpu.roll(x, s, 0), 0.0)
    return x


def _block_rev_cumsum(x, pos, rows):
    for s in (1, 2, 4, 8):
        x = x + jnp.where(pos < BLK - s, pltpu.roll(x, rows - s, 0), 0.0)
    return x


def _block_last(x, pos, rows):
    m = jnp.where(pos == BLK - 1, x, 0.0)
    for s in (1, 2, 4, 8):
        m = m + pltpu.roll(m, rows - s, 0)
    return m


def _hgrn_gates(q, z, lb):
    sgq = _sig(q)
    qs = q * sgq
    sg = _sig(z)
    f = lb + (1.0 - lb) * sg
    kk = (1.0 - lb) * (1.0 - sg)
    return sgq, qs, sg, f, kk


def _hgrn_decays(q, z, lb, pos, rows):
    sgq, qs, sg, f, kk = _hgrn_gates(q, z, lb)
    b = _block_cumsum(jnp.log(f), pos)
    return sgq, qs, sg, f, kk, b, _block_last(b, pos, rows)


def _hgrn_specs(T, d_conv, n_t, rev):
    def t_of(i):
        return (n_t - 1 - i) if rev else i

    def pcol(off):
        o = off // HEAD
        return pl.BlockSpec((T, HEAD), lambda h, i, o=o: (t_of(i), h + o))

    q_spec, z_spec, v_spec, g_spec = pcol(3 * d_conv), pcol(4 * d_conv), pcol(5 * d_conv), pcol(6 * d_conv)
    lb_spec = pl.BlockSpec((1, HEAD), lambda h, i: (0, h))
    gn_spec = pl.BlockSpec((1, HEAD), lambda h, i: (0, 0))
    act_spec = pl.BlockSpec((T, HEAD), lambda h, i: (t_of(i), h))
    st_spec = pl.BlockSpec((None, T // BLK, HEAD, HEAD), lambda h, i: (h, t_of(i), 0, 0))
    return q_spec, z_spec, v_spec, g_spec, lb_spec, gn_spec, act_spec, st_spec


def _hgrn_fwd(proj, lb, gn, d_conv, job=None):
    S = proj.shape[0]
    H = d_conv // HEAD
    T = min(HGRN_ROWS, S)
    n_t, nb = S // T, T // BLK
    sub = min(HGRN_FWD_SUB, T)
    q_spec, z_spec, v_spec, g_spec, lb_spec, gn_spec, act_spec, st_spec = _hgrn_specs(T, d_conv, n_t, False)

    def body(q_ref, z_ref, v_ref, g_ref, lb_ref, gn_ref, ob_ref, o_ref, st_ref, qe_s, ke_s, dec_s, state, v_s, o_s, u_s):
        @pl.when(pl.program_id(1) == 0)
        def _():
            state[...] = jnp.zeros_like(state)

        pos = lax.broadcasted_iota(jnp.int32, (sub, HEAD), 0) % BLK
        lbv = lb_ref[...]

        def vector_pass(s, carry):
            r = pl.ds(pl.multiple_of(s * sub, sub), sub)
            v = v_ref[r, :]
            _, qs, _, _, kk, b, b_tot = _hgrn_decays(q_ref[r, :], z_ref[r, :], lbv, pos, sub)
            qe_s[r, :] = (qs * jnp.exp(b)).astype(BF16)
            ke_s[r, :] = (kk * jnp.exp(b_tot - b)).astype(BF16)
            v_s[r, :] = v.astype(BF16)
            dec_s[r, :] = jnp.exp(b_tot)
            o = jnp.sum(qs * kk, axis=-1, keepdims=True) * v
            for d in range(1, BLK):
                e = jnp.exp(b - pltpu.roll(b, d, 0))
                a = jnp.sum(qs * pltpu.roll(kk, d, 0) * e, axis=-1, keepdims=True)
                o = o + jnp.where(pos >= d, a * pltpu.roll(v, d, 0), 0.0)
            o_s[r, :] = o
            return carry

        lax.fori_loop(0, T // sub, vector_pass, 0)

        for j in range(nb):
            rows = pl.ds(j * BLK, BLK)
            u_s[j] = lax.dot_general(v_s[rows, :], ke_s[rows, :], (((0,), (0,)), ((), ())),
                                     preferred_element_type=F32)
        st = state[...]
        for j in range(nb):
            st_ref[j] = st.astype(BF16)
            st = st * dec_s[pl.ds(j * BLK, 1), :] + u_s[j]
        state[...] = st
        for j in range(nb):
            rows = pl.ds(j * BLK, BLK)
            o_s[rows, :] += lax.dot_general(qe_s[rows, :], st_ref[j], (((1,), (1,)), ((), ())),
                                            preferred_element_type=F32)
        o = o_s[...]
        o_ref[...] = o
        r = lax.rsqrt(jnp.mean(o * o, axis=-1, keepdims=True) + EPS)
        g = g_ref[...]
        ob_ref[...] = ((o * r) * gn_ref[...] * (g * _sig(g))).astype(BF16)

    grid = (H, n_t)
    in_specs = [q_spec, z_spec, v_spec, g_spec, lb_spec, gn_spec]
    args = [proj, proj, proj, proj, lb, gn]
    out_specs = [act_spec, act_spec, st_spec, act_spec, act_spec, act_spec]
    out_shape = [jax.ShapeDtypeStruct((S, d_conv), BF16), jax.ShapeDtypeStruct((S, d_conv), F32),
                 jax.ShapeDtypeStruct((H, S // BLK, HEAD, HEAD), BF16),
                 jax.ShapeDtypeStruct((S, d_conv), BF16), jax.ShapeDtypeStruct((S, d_conv), BF16),
                 jax.ShapeDtypeStruct((S, d_conv), F32)]
    scratch = [pltpu.VMEM((HEAD, HEAD), F32), pltpu.VMEM((T, HEAD), BF16), pltpu.VMEM((T, HEAD), F32),
               pltpu.VMEM((nb, HEAD, HEAD), F32)]
    aliases, wrap = _host(job, grid, in_specs, args, out_specs, out_shape, scratch)
    return pl.pallas_call(
        wrap(body), name="hgrn_fwd", grid=grid, in_specs=in_specs, out_specs=out_specs, out_shape=out_shape,
        scratch_shapes=scratch, input_output_aliases=aliases,
        compiler_params=_params(("parallel" if job is None else "arbitrary", "arbitrary")),
    )(*args)


def _hgrn_bwd(dob, o_raw, states, qe, ke, dec, proj, lb, gn, d_conv, job=None):
    S = proj.shape[0]
    H = d_conv // HEAD
    T = min(HGRN_ROWS, S)
    n_t, nb = S // T, T // BLK
    sub = min(HGRN_BWD_SUB, T)
    q_spec, z_spec, v_spec, g_spec, lb_spec, gn_spec, act_spec, st_spec = _hgrn_specs(T, d_conv, n_t, True)

    def body(dob_ref, o_ref, st_ref, qe_b, ke_b, dec_s, q_ref, z_ref, v_ref, g_ref, lb_ref, gn_ref,
             dq_ref, dz_ref, dv_ref, dg_ref, dlb_ref, dgn_ref,
             dstate, do_b, v_b, dqe_s, dke_s, dvi_s, ddec_s, do_s, u_s, ds_s):
        @pl.when(pl.program_id(1) == 0)
        def _():
            dstate[...] = jnp.zeros_like(dstate)
            dlb_ref[...] = jnp.zeros_like(dlb_ref)
            dgn_ref[...] = jnp.zeros_like(dgn_ref)

        pos = lax.broadcasted_iota(jnp.int32, (sub, HEAD), 0) % BLK
        lbv, gnv = lb_ref[...], gn_ref[...]

        def before(s, carry):
            r = pl.ds(pl.multiple_of(s * sub, sub), sub)
            g = g_ref[r, :]
            v_b[r, :] = v_ref[r, :].astype(BF16)
            o = o_ref[r, :]
            rs = lax.rsqrt(jnp.mean(o * o, axis=-1, keepdims=True) + EPS)
            on = o * rs
            sgg = _sig(g)
            dobv = dob_ref[r, :]
            dog = dobv * (g * sgg)
            dgn_ref[...] += jnp.sum(dog * on, axis=0, keepdims=True)
            don = dog * gnv
            do = rs * (don - on * jnp.mean(don * on, axis=-1, keepdims=True))
            dg_ref[r, :] = (dobv * (on * gnv) * (sgg * (1.0 + g * (1.0 - sgg)))).astype(BF16)
            do_s[r, :] = do
            do_b[r, :] = do.astype(BF16)
            return carry

        lax.fori_loop(0, T // sub, before, 0)

        for j in range(nb):
            rows = pl.ds(j * BLK, BLK)
            dob_j = do_b[rows, :]
            u_s[j] = lax.dot_general(dob_j, qe_b[rows, :], (((0,), (0,)), ((), ())), preferred_element_type=F32)
            dqe_s[rows, :] = lax.dot_general(dob_j, st_ref[j], (((1,), (0,)), ((), ())), preferred_element_type=F32)
        dst = dstate[...]
        for j in reversed(range(nb)):
            ds_s[j] = dst.astype(BF16)
            ddec = jnp.sum(st_ref[j].astype(F32) * dst, axis=0, keepdims=True)
            ddec_s[pl.ds(j * BLK, BLK), :] = jnp.broadcast_to(ddec, (BLK, HEAD))
            dst = dst * dec_s[pl.ds(j * BLK, 1), :] + u_s[j]
        dstate[...] = dst
        for j in range(nb):
            rows = pl.ds(j * BLK, BLK)
            dke_s[rows, :] = lax.dot_general(v_b[rows, :], ds_s[j], (((1,), (0,)), ((), ())), preferred_element_type=F32)
            dvi_s[rows, :] = lax.dot_general(ke_b[rows, :], ds_s[j], (((1,), (1,)), ((), ())), preferred_element_type=F32)

        def after(s, carry):
            r = pl.ds(pl.multiple_of(s * sub, sub), sub)
            q, v = q_ref[r, :], v_ref[r, :]
            sgq, qs, sg, f, kk, b, b_tot = _hgrn_decays(q, z_ref[r, :], lbv, pos, sub)
            do = do_s[r, :]
            dqs = dqe_s[r, :] * jnp.exp(b)
            dkk = dke_s[r, :] * jnp.exp(b_tot - b)
            d_tot = jnp.where(pos == BLK - 1, _block_cumsum(dkk * kk, pos) + ddec_s[r, :] * jnp.exp(b_tot), 0.0)
            dv = dvi_s[r, :]
            da = jnp.sum(do * v, axis=-1, keepdims=True)
            dv = dv + jnp.sum(qs * kk, axis=-1, keepdims=True) * do
            dqs = dqs + da * kk
            dkk = dkk + da * qs
            for d in range(1, BLK):
                kd = pltpu.roll(kk, d, 0)
                e = jnp.where(pos >= d, jnp.exp(b - pltpu.roll(b, d, 0)), 0.0)
                a = jnp.sum(qs * kd * e, axis=-1, keepdims=True)
                da = jnp.sum(do * pltpu.roll(v, d, 0), axis=-1, keepdims=True)
                gq = da * e
                dqs = dqs + gq * kd
                dkk = dkk + pltpu.roll(gq * qs, sub - d, 0)
                dv = dv + pltpu.roll(a * do, sub - d, 0)
            db = qs * dqs - kk * dkk + d_tot
            dlf = _block_rev_cumsum(db, pos, sub)
            df = dlf / f - dkk
            dlb_ref[...] += jnp.sum(df * (1.0 - sg), axis=0, keepdims=True)
            dz_ref[r, :] = (df * (1.0 - lbv) * sg * (1.0 - sg)).astype(BF16)
            dq_ref[r, :] = (dqs * (sgq * (1.0 + q * (1.0 - sgq)))).astype(BF16)
            dv_ref[r, :] = dv.astype(BF16)
            return carry

        lax.fori_loop(0, T // sub, after, 0)

    tb = lambda dt: pltpu.VMEM((T, HEAD), dt)
    grid = (H, n_t)
    in_specs = [act_spec, act_spec, st_spec, act_spec, act_spec, act_spec, q_spec, z_spec, v_spec, g_spec, lb_spec,
                gn_spec]
    args = [dob, o_raw, states, qe, ke, dec, proj, proj, proj, proj, lb, gn]
    out_specs = [act_spec, act_spec, act_spec, act_spec, lb_spec, pl.BlockSpec((None, 1, HEAD), lambda h, i: (h, 0, 0))]
    out_shape = ([jax.ShapeDtypeStruct((S, d_conv), BF16)] * 4
                 + [jax.ShapeDtypeStruct((1, d_conv), F32), jax.ShapeDtypeStruct((H, 1, HEAD), F32)])
    scratch = [pltpu.VMEM((HEAD, HEAD), F32), tb(BF16), tb(BF16), tb(F32), tb(F32), tb(F32), tb(F32), tb(F32),
               pltpu.VMEM((nb, HEAD, HEAD), F32), pltpu.VMEM((nb, HEAD, HEAD), BF16)]
    aliases, wrap = _host(job, grid, in_specs, args, out_specs, out_shape, scratch)
    return pl.pallas_call(
        wrap(body), name="hgrn_bwd", grid=grid, in_specs=in_specs, out_specs=out_specs, out_shape=out_shape,
        scratch_shapes=scratch, input_output_aliases=aliases,
        compiler_params=_params(("parallel" if job is None else "arbitrary", "arbitrary")),
    )(*args)


def _first(accs, extras):
    return [accs[0]]


def _local_step(x, target, mod, norm_mix_g, lb, gnorm_g, norm_ffn_g, norm_final_g, conv_w, bufs, c_idx, order):
    S, D = x.shape
    d_conv = D // 2
    d_in = 7 * d_conv + 2 * D
    d_ff = N_CHIPS * bufs["w_ffn_down"].shape[1]
    nb_in, nb_co, nb_ff = bufs["w_in"].shape[2], bufs["w_conv_out"].shape[2], bufs["w_ffn_gate"].shape[2]
    rows = lambda g: g.reshape(-1, g.shape[2])
    sh_m, sc_m, gt_m, sh_f, sc_f, gt_f = [mod[:, i * D:(i + 1) * D] for i in range(6)]
    tm = _pick(S, 512, 16)
    tm2 = _pick(S, 1024, 16)
    tn_in = _pick(nb_in, 1408, 128)
    tn_co = _pick(nb_co, 512, 128)
    tn_ff = _pick(nb_ff, 1408, 128)
    tn_d = _pick(D, 512, 128)
    tw = _pick(D, 1024, 128)
    tg = _pick(D, 512, 128)

    h = _norm_mod("norm_mix", x, norm_mix_g, sc_m, sh_m)
    proj, w_in = _proj_gather(h, bufs["w_in"], order, None)
    ob, o_raw, states, qe, ke, dec, *got = _hgrn_fwd(
        proj, lb, gnorm_g, d_conv,
        job=_GatherJob([bufs[k] for k in ("w_conv_out", "w_hgrn_out", "w_o", "w_ffn_gate")]))
    ya = _conv_fwd(proj, conv_w, d_conv)
    w_conv_out, w_hgrn_out, w_o, w_ffn_gate = _forward_halves("forward_branch", got)
    w_o = rows(w_o)

    def merge_epi(accs, ex):
        y_a, y_b = accs
        return [y_a, y_b, _sig(ex[0]) * y_a + _sig(ex[1]) * y_b]

    y_a, y_b, merged, w_ffn_up = _matmul(
        "branch_out", [(ya, 'n', w_conv_out, 'n'), (ob, 'n', w_hgrn_out, 'n')], [0, 1], S, D, d_conv, tm2, tn_co, d_conv,
        [(proj, 'tile', 7 * d_conv), (proj, 'tile', 7 * d_conv + D)], [(F32, None), (F32, None), (BF16, None)], merge_epi,
        job=_GatherJob([bufs["w_ffn_up"]], 0, 2))

    def resid_epi(accs, ex):
        return [accs[0], ex[0] + ex[1] * accs[0]]

    mo, x1, w_ffn_up = _matmul("w_o", [(merged, 'n', w_o, 'n')], [0], S, D, D, tm2, tw, D,
                               [(x, 'tile', 0), (gt_m, 'row', 0)], [(F32, None), (F32, None)], resid_epi,
                               job=_GatherJob([w_ffn_up], 1, 2))
    h2 = _norm_mod("norm_ffn", x1, norm_ffn_g, sc_f, sh_f)
    w_ffn_up, = _forward_halves("forward_up", [w_ffn_up])

    def swiglu_epi(accs, ex):
        gg, uu = accs
        return [gg, uu, gg * _sig(gg) * uu]

    G, U, act, w_ffn_down = _matmul(
        "ffn_in", [(h2, 'n', w_ffn_gate, 'n'), (h2, 'n', w_ffn_up, 'n')], [0, 1], S, d_ff, D,
        tm, tn_ff, D, [], [(BF16, None), (BF16, None), (BF16, None)], swiglu_epi, job=_GatherJob([bufs["w_ffn_down"]]))
    w_ffn_down = rows(_forward_halves("forward_down", [w_ffn_down])[0])
    ff, x2 = _matmul("ffn_out", [(act, 'n', w_ffn_down, 'n')], [0], S, D, d_ff, tm2, tn_d, d_ff,
                     [(x1, 'tile', 0), (gt_f, 'row', 0)], [(F32, None), (F32, None)], resid_epi, rows_outer=True)

    dx2, dffb, loss, dgt_f, dg_final = _loss_head(x2, target, norm_final_g, ff, gt_f)

    def swiglu_bwd_epi(accs, ex):
        dact, gg, uu = accs[0], ex[0], ex[1]
        s = _sig(gg)
        return [dact * uu * (s * (1.0 + gg * (1.0 - s))), dact * (gg * s)]

    dG, dU = _matmul("d_act", [(dffb, 'n', w_ffn_down, 't')], [0], S, d_ff, D, tm2, tn_ff, D,
                     [(G, 'tile', 0), (U, 'tile', 0)], [(BF16, None), (BF16, None)], swiglu_bwd_epi)
    dw_down, = _matmul("dw_ffn_down", [(act, 't', dffb, 'n')], [0], d_ff, D, S, _pick(d_ff, 512, 128),
                       D, S, [], [(BF16, None)], _first)
    dh2, = _matmul("d_h2", [(dG, 'n', w_ffn_gate, 't'), (dU, 'n', w_ffn_up, 't')], [0, 0], S, D, d_ff,
                   tm, D, tn_ff, [], [(F32, None)], _first)
    dw_gate, = _matmul("dw_ffn_gate", [(h2, 't', dG, 'n')], [0], D, d_ff, S, tg, tn_ff, S, [], [(BF16, nb_ff)], _first)
    dw_up, = _matmul("dw_ffn_up", [(h2, 't', dU, 'n')], [0], D, d_ff, S, tg, tn_ff, S, [], [(BF16, nb_ff)], _first)

    def pre_sum(tag, names, grads):
        from_sibling = _swap_halves("swap_" + tag, grads)
        pairs = [_pair_sum("pair_sum_" + k, g, q, c_idx) for k, g, q in zip(names, grads, from_sibling)]
        return [p[0] for p in pairs], [p[1] for p in pairs]

    ffn_names = ["w_ffn_down", "w_ffn_gate", "w_ffn_up"]
    parts_f, slots_f = pre_sum("ffn", ffn_names, [dw_down.reshape(N_CHIPS, -1, D), dw_gate, dw_up])
    dx1, dsh_f, dsc_f, dg_ffn, dmob, dgt_m = _norm_mod_bwd("norm_ffn_bwd", dh2, x1, norm_ffn_g, sc_f, dx2, (mo, gt_m))

    def merge_bwd_epi(accs, ex):
        dm, ga, gb, ya_, yb_ = accs[0], ex[0], ex[1], ex[2], ex[3]
        sa, sb = _sig(ga), _sig(gb)
        return [dm * ya_ * sa * (1.0 - sa), dm * yb_ * sb * (1.0 - sb), dm * sa, dm * sb]

    dga, dgb, dy_a, dy_b = _matmul(
        "d_merged", [(dmob, 'n', w_o, 't')], [0], S, D, D, tm2, tn_co, D,
        [(proj, 'tile', 7 * d_conv), (proj, 'tile', 7 * d_conv + D), (y_a, 'tile', 0), (y_b, 'tile', 0)],
        [(BF16, None)] * 4, merge_bwd_epi)
    dw_o, = _matmul("dw_o", [(merged, 't', dmob, 'n')], [0], D, D, S, tg, D, S, [], [(BF16, None)], _first)
    both = lambda accs, ex: accs
    dya, dob = _matmul("d_branch", [(dy_a, 'n', w_conv_out, 't'), (dy_b, 'n', w_hgrn_out, 't')], [0, 1], S, d_conv, D,
                       tm2, _pick(d_conv, 1024, 128), tn_co, [], [(F32, None), (F32, None)], both)
    dw_co, dw_ho = _matmul("dw_branch", [(ya, 't', dy_a, 'n'), (ob, 't', dy_b, 'n')], [0, 1], d_conv, D, S,
                           _pick(d_conv, 512, 128), tn_co, S, [], [(BF16, nb_co), (BF16, nb_co)], both)
    branch_names = ["w_conv_out", "w_hgrn_out", "w_o"]
    parts_b, slots_b = pre_sum("branch", branch_names, [dw_co, dw_ho, dw_o.reshape(N_CHIPS, -1, D)])
    dab, dac, dax, dconv_w = _conv_bwd(dya, proj, conv_w, d_conv)
    dq, dz, dv, dgo, dlb, dgn, *arrived = _hgrn_bwd(dob, o_raw, states, qe, ke, dec, proj, lb, gnorm_g, d_conv,
                                                    job=_ScatterJob(parts_f + parts_b, slots_f + slots_b))
    dproj = jnp.concatenate([dab, dac, dax, dq, dz, dv, dgo, dga, dgb], axis=1)
    dw_in, = _matmul("dw_in", [(h, 't', dproj, 'n')], [0], D, d_in, S, tg, tn_in, S, [], [(BF16, nb_in)], _first)
    parts_i, slots_i = pre_sum("in", ["w_in"], [dw_in])
    dh, arrived_in = _matmul("d_h", [(dproj, 'n', w_in, 't')], [0], S, D, d_in, tm2, D, tn_in, [], [(F32, None)], _first,
                             job=_ScatterJob(parts_i, slots_i))
    dx, dsh_m, dsc_m, dg_mix = _norm_mod_bwd("norm_mix_bwd", dh, x, norm_mix_g, sc_m, dx1)
    dmod = jnp.concatenate([dsh_m, dsc_m, dgt_m, dsh_f, dsc_f, dgt_f], axis=1)
    small = dict(dmod=dmod, dg_mix=dg_mix, dg_ffn=dg_ffn, dg_final=dg_final, dlb=dlb,
                 dgn=jnp.sum(dgn, axis=0), dconv_w=dconv_w)
    big = dict(zip(ffn_names + branch_names, arrived), w_in=arrived_in)
    return loss, dx, small, big


def _adamw_math(w, g, m, v):
    m = ADAM_B1 * m + (1.0 - ADAM_B1) * g
    v = ADAM_B2 * v + (1.0 - ADAM_B2) * (g * g)
    m_hat = m / (1.0 - ADAM_B1 ** ADAM_STEP)
    v_hat = v / (1.0 - ADAM_B2 ** ADAM_STEP)
    delta = -ADAM_LR * (m_hat / (jnp.sqrt(v_hat) + ADAM_EPS) + ADAM_WD * w)
    return delta, m, v


def _adamw(name, w, g, m, v):
    R, C = w.shape
    tr = _pick(R, max(8, (256 * 1024) // C // 8 * 8), 8)
    spec = pl.BlockSpec((tr, C), lambda i: (i, 0))

    def body(w_ref, g_ref, m_ref, v_ref, d_ref, mo_ref, vo_ref):
        d, m2, v2 = _adamw_math(w_ref[...], g_ref[...], m_ref[...], v_ref[...])
        d_ref[...] = d
        mo_ref[...] = m2
        vo_ref[...] = v2

    return pl.pallas_call(
        body, name=name, grid=(R // tr,), in_specs=[spec] * 4, out_specs=[spec] * 3,
        out_shape=[jax.ShapeDtypeStruct((R, C), F32)] * 3, compiler_params=_params(("parallel",)),
    )(w, g, m, v)


def _ada_update(c_act_t, dmod, w, m, v):
    R, C = w.shape
    B = dmod.shape[0]
    tr = _pick(R, 256, 8)
    tc = _pick(C, 1536, 128)
    spec = pl.BlockSpec((tr, tc), lambda i, j: (i, j))

    def body(ct_ref, dm_ref, w_ref, m_ref, v_ref, g_ref, d_ref, mo_ref, vo_ref):
        ct = ct_ref[...]
        dm = dm_ref[...]
        g = ct[:, 0:1] * dm[0:1, :]
        for b in range(1, B):
            g = g + ct[:, b:b + 1] * dm[b:b + 1, :]
        d, m2, v2 = _adamw_math(w_ref[...], g, m_ref[...], v_ref[...])
        g_ref[...] = g
        d_ref[...] = d
        mo_ref[...] = m2
        vo_ref[...] = v2

    return pl.pallas_call(
        body, name="ada_update", grid=(R // tr, C // tc),
        in_specs=[pl.BlockSpec((tr, B), lambda i, j: (i, 0)), pl.BlockSpec((B, tc), lambda i, j: (0, j)),
                  spec, spec, spec],
        out_specs=[spec] * 4, out_shape=[jax.ShapeDtypeStruct((R, C), F32)] * 4,
        compiler_params=_params(("parallel", "parallel")),
    )(c_act_t, dmod, w, m, v)


def _prep(c_all, lb_param):
    def body(c_ref, p_ref, ca_ref, cab_ref, lb_ref):
        cv = c_ref[...]
        ca = cv * _sig(cv)
        ca_ref[...] = ca
        cab_ref[...] = ca.astype(BF16)
        lb_ref[...] = _sig(p_ref[0:1, :] - p_ref[1:2, :])

    return pl.pallas_call(
        body, name="prep",
        out_shape=[jax.ShapeDtypeStruct(c_all.shape, F32), jax.ShapeDtypeStruct(c_all.shape, BF16),
                   jax.ShapeDtypeStruct((1, lb_param.shape[1]), F32)],
    )(c_all, lb_param)


def _small_reduce(parts):
    W = parts.shape[1]

    def body(p_ref, o_ref):
        acc = p_ref[0:1, :]
        for d in range(1, N_DEV):
            acc = acc + p_ref[d:d + 1, :]
        o_ref[...] = acc

    return pl.pallas_call(body, name="small_reduce", out_shape=jax.ShapeDtypeStruct((1, W), F32))(parts)


def _lb_grad(dlb, lb):
    def body(d_ref, lb_ref, o_ref):
        t = d_ref[...] * lb_ref[...] * (1.0 - lb_ref[...])
        o_ref[0:1, :] = t
        o_ref[1:2, :] = -t

    return pl.pallas_call(body, name="lb_grad", out_shape=jax.ShapeDtypeStruct((2, dlb.shape[1]), F32))(dlb, lb)


def _place():
    x, y, c = lax.axis_index("x"), lax.axis_index("y"), lax.axis_index("c")
    chips = [(1 - x, y), (x, 1 - y), (1 - x, 1 - y)]
    return x, y, c, chips


def _allgather_rows(name, v):
    m_per, n = v.shape

    def body(x_ref, out_ref, send_sems, recv_sems, local_sem):
        x, y, c, chips = _place()
        me, sibling = (x, y, c), (x, y, 1 - c)

        def rows(px, py, pc):
            return out_ref.at[pl.ds((4 * px + 2 * py + pc) * m_per, m_per), :]

        def copy(k, block, to, src=None):
            return pltpu.make_async_remote_copy(
                src_ref=rows(*block) if src is None else src, dst_ref=rows(*block),
                send_sem=send_sems.at[k], recv_sem=recv_sems.at[k], device_id=to, device_id_type=MESH)

        mine = pltpu.make_async_copy(x_ref, rows(*me), local_sem)
        mine.start()
        first = [copy(0, me, sibling, src=x_ref)]
        first += [copy(1 + j, me, (*chip, c), src=x_ref) for j, chip in enumerate(chips)]
        for cp in first:
            cp.start()
        passed = [copy(4 + j, (*chip, c), sibling) for j, chip in enumerate(chips)]
        for j, chip in enumerate(chips):
            copy(1 + j, (*chip, c), me).wait_recv()
            passed[j].start()
        copy(0, sibling, me).wait_recv()
        for j, chip in enumerate(chips):
            copy(4 + j, (*chip, 1 - c), me).wait_recv()
        for cp in first + passed:
            cp.wait_send()
        mine.wait()

    return pl.pallas_call(
        body, name=name, out_shape=jax.ShapeDtypeStruct((N_DEV * m_per, n), v.dtype),
        in_specs=[pl.BlockSpec(memory_space=pltpu.VMEM)], out_specs=pl.BlockSpec(memory_space=pltpu.VMEM),
        scratch_shapes=[pltpu.SemaphoreType.DMA((7,)), pltpu.SemaphoreType.DMA((7,)), pltpu.SemaphoreType.DMA],
    )(v)


def _cast_place(name, w, k_idx):
    R, C = w.shape
    tr = _pick(R, max(16, (512 * 1024) // C // 16 * 16), 16)

    def body(k_ref, w_ref, o_ref):
        o_ref[...] = w_ref[...].astype(BF16)

    return pl.pallas_call(
        body, name=name,
        grid_spec=pltpu.PrefetchScalarGridSpec(
            num_scalar_prefetch=1, grid=(R // tr,),
            in_specs=[pl.BlockSpec((tr, C), lambda i, k_ref: (i, 0))],
            out_specs=pl.BlockSpec((None, tr, C), lambda i, k_ref: (k_ref[0], i, 0))),
        out_shape=jax.ShapeDtypeStruct((N_CHIPS, R, C), BF16),
        compiler_params=_params(("parallel",)),
    )(k_idx, w)


def _chip_copies(src_of, dst_of, got_of, n, sems, receiving):
    x, y, c, chips = _place()
    k_me = 2 * x + y
    send_sems, recv_sems = sems
    out = []
    for a in range(n):
        for j, chip in enumerate(chips):
            k_chip = 2 * chip[0] + chip[1]
            if receiving:
                src = dst = got_of(a, k_chip, c)
                to = (x, y, c)
            else:
                src, dst, to = src_of(a, k_chip, k_me, c), dst_of(a, k_me, c), (*chip, c)
            out.append(pltpu.make_async_remote_copy(
                src_ref=src, dst_ref=dst, send_sem=send_sems.at[3 * a + j], recv_sem=recv_sems.at[3 * a + j],
                device_id=to, device_id_type=MESH))
    return out


class _ChipJob:
    def start(self, j_in, j_out, sems):
        for send in self.copies(j_in, j_out, sems, False):
            send.start()

    def finish(self, j_in, j_out, sems):
        for recv in self.copies(j_in, j_out, sems, True):
            recv.wait_recv()
        for send in self.copies(j_in, j_out, sems, False):
            send.wait_send()


class _GatherJob(_ChipJob):
    def __init__(self, bufs, part=0, n_parts=1):
        n = len(bufs)
        self.inputs, self.n_alias = list(bufs), n
        self.sem_shapes = [pltpu.SemaphoreType.DMA((3 * n,)), pltpu.SemaphoreType.DMA((3 * n,))]
        self.half = [b.shape[1] // 2 for b in bufs]
        self.part, self.n_parts = part, n_parts

    def copies(self, j_in, j_out, sems, receiving):
        def half(a, k, c):
            rows = self.half[a] // self.n_parts
            return j_out[a].at[k, pl.ds(c * self.half[a] + self.part * rows, rows)]

        return _chip_copies(lambda a, k_chip, k_me, c: half(a, k_me, c), half, half, len(self.half), sems, receiving)


class _ScatterJob(_ChipJob):
    def __init__(self, parts, slots):
        n = len(parts)
        self.n = n
        self.inputs, self.n_alias = list(parts) + list(slots), n
        self.sem_shapes = [pltpu.SemaphoreType.DMA((3 * n,)), pltpu.SemaphoreType.DMA((3 * n,))]

    def copies(self, j_in, j_out, sems, receiving):
        return _chip_copies(lambda a, k_chip, k_me, c: j_in[a].at[k_chip], lambda a, k_me, c: j_out[a].at[k_me],
                            lambda a, k_chip, c: j_out[a].at[k_chip], self.n, sems, receiving)


def _forward_halves(name, bufs):
    n = len(bufs)

    def body(*refs):
        out_refs = refs[n:2 * n]
        send_sems, recv_sems = refs[2 * n:]
        x, y, c, chips = _place()
        started, waits = [], []
        for a in range(n):
            rh = bufs[a].shape[1] // 2
            for j, chip in enumerate(chips):
                k_chip = 2 * chip[0] + chip[1]
                got = out_refs[a].at[k_chip, pl.ds(c * rh, rh)]
                cp = pltpu.make_async_remote_copy(src_ref=got, dst_ref=got, send_sem=send_sems.at[3 * a + j],
                                                  recv_sem=recv_sems.at[3 * a + j], device_id=(x, y, 1 - c),
                                                  device_id_type=MESH)
                cp.start()
                started.append(cp)
                other = out_refs[a].at[k_chip, pl.ds((1 - c) * rh, rh)]
                waits.append(pltpu.make_async_remote_copy(
                    src_ref=other, dst_ref=other, send_sem=send_sems.at[3 * a + j], recv_sem=recv_sems.at[3 * a + j],
                    device_id=(x, y, c), device_id_type=MESH))
        for cp in waits:
            cp.wait_recv()
        for cp in started:
            cp.wait_send()

    return pl.pallas_call(
        body, name=name, out_shape=[jax.ShapeDtypeStruct(b.shape, b.dtype) for b in bufs],
        in_specs=[_HBM] * n, out_specs=[_HBM] * n, input_output_aliases={a: a for a in range(n)},
        scratch_shapes=[pltpu.SemaphoreType.DMA((3 * n,)), pltpu.SemaphoreType.DMA((3 * n,))],
    )(*bufs)


def _proj_gather(h, buf, order, job):
    S, D = h.shape
    nb = buf.shape[2]
    tm = _pick(S, 1024, 16)
    tn = _pick(nb, 1408, 128)
    per = nb // tn
    nj, ni = N_CHIPS * per, S // tm
    rh = D // 2

    def body(order_ref, h_ref, buf_in, proj_ref, buf_ref, wbuf, tile_sems, ici_send, ici_recv, d2d_send, d2d_recv):
        j, i = pl.program_id(0), pl.program_id(1)
        x, y, c, chips = _place()
        k_me = 2 * x + y

        def half(k, hc):
            return buf_ref.at[k, pl.ds(hc * rh, rh)]

        def ici(q, receiving):
            k_chip = 2 * chips[q][0] + chips[q][1]
            src, to = (half(k_chip, c), (x, y, c)) if receiving else (half(k_me, c), (*chips[q], c))
            return pltpu.make_async_remote_copy(src_ref=src, dst_ref=src, send_sem=ici_send.at[q],
                                                recv_sem=ici_recv.at[q], device_id=to, device_id_type=MESH)

        def d2d(q, receiving):
            k_chip = 2 * chips[q][0] + chips[q][1]
            src, to = (half(k_chip, 1 - c), (x, y, c)) if receiving else (half(k_chip, c), (x, y, 1 - c))
            return pltpu.make_async_remote_copy(src_ref=src, dst_ref=src, send_sem=d2d_send.at[q],
                                                recv_sem=d2d_recv.at[q], device_id=to, device_id_type=MESH)

        def tile_copy(t):
            col = pl.multiple_of((t % per) * tn, 128)
            return pltpu.make_async_copy(buf_ref.at[order_ref[t // per], :, pl.ds(col, tn)], wbuf.at[t % 2],
                                         tile_sems.at[t % 2])

        @pl.when(jnp.logical_and(j == 0, i == 0))
        def _():
            ici(0, False).start()
            ici(1, False).start()
            tile_copy(0).start()

        @pl.when(i == 0)
        def _():
            tile_copy(j).wait()
            for q in range(3):
                @pl.when(j + 1 == (q + 1) * per)
                def _():
                    ici(q, True).wait_recv()
                    d2d(q, False).start()
                    if q == 0:
                        ici(0, False).wait_send()
                        ici(1, False).wait_send()
                        ici(2, False).start()
                    d2d(q, True).wait_recv()

            @pl.when(j + 1 < nj)
            def _():
                tile_copy(j + 1).start()

        proj_ref[...] = jnp.dot(h_ref[...], wbuf[j % 2], preferred_element_type=F32)

        @pl.when(jnp.logical_and(j == nj - 1, i == ni - 1))
        def _():
            ici(2, False).wait_send()
            for q in range(3):
                d2d(q, False).wait_send()

    grid = (nj, ni)
    in_specs = [pl.BlockSpec((tm, D), lambda j, i, o: (i, 0)), _HBM]
    args = [h, buf]
    out_specs = [pl.BlockSpec((tm, tn), lambda j, i, o: (i, o[j // per] * per + j % per)), _HBM]
    out_shape = [jax.ShapeDtypeStruct((S, N_CHIPS * nb), F32), jax.ShapeDtypeStruct(buf.shape, buf.dtype)]
    scratch = [pltpu.VMEM((2, D, tn), BF16), pltpu.SemaphoreType.DMA((2,))] + [pltpu.SemaphoreType.DMA((3,))] * 4
    aliases, wrap = _host(job, grid, in_specs, args, out_specs, out_shape, scratch, n_prefetch=1)
    aliases[2] = 1
    return pl.pallas_call(
        wrap(body), name="proj",
        grid_spec=pltpu.PrefetchScalarGridSpec(num_scalar_prefetch=1, grid=grid, in_specs=in_specs, out_specs=out_specs,
                                               scratch_shapes=scratch),
        out_shape=out_shape, input_output_aliases=aliases, compiler_params=_params(("arbitrary", "arbitrary")),
    )(order, *args)


def _swap_halves(name, grads):
    n = len(grads)

    def body(*refs):
        in_refs, out_refs = refs[:n], refs[n:2 * n]
        send_sems, recv_sems = refs[2 * n:]
        x, y, c, _ = _place()
        cps = []
        for a in range(n):
            rh = grads[a].shape[1] // 2
            cp = pltpu.make_async_remote_copy(
                src_ref=in_refs[a].at[:, pl.ds((1 - c) * rh, rh)], dst_ref=out_refs[a],
                send_sem=send_sems.at[a], recv_sem=recv_sems.at[a], device_id=(x, y, 1 - c), device_id_type=MESH)
            cp.start()
            cps.append(cp)
        for cp in cps:
            cp.wait()

    return pl.pallas_call(
        body, name=name,
        out_shape=[jax.ShapeDtypeStruct((N_CHIPS, g.shape[1] // 2, g.shape[2]), g.dtype) for g in grads],
        in_specs=[_HBM] * n, out_specs=[_HBM] * n,
        scratch_shapes=[pltpu.SemaphoreType.DMA((n,)), pltpu.SemaphoreType.DMA((n,))],
    )(*grads)


def _pair_sum(name, g, q, c_idx):
    _, R, C = g.shape
    rh = R // 2
    tr = _pick(rh, max(16, (512 * 1024) // C // 16 * 16), 16)
    nh = rh // tr

    def body(c_ref, g_ref, q_ref, o_ref, o2_ref):
        s = (g_ref[...].astype(F32) + q_ref[...].astype(F32)).astype(BF16)
        o_ref[...] = s
        o2_ref[...] = s

    out_spec = pl.BlockSpec((None, tr, C), lambda k, i, c_ref: (k, i, 0))
    return pl.pallas_call(
        body, name=name,
        grid_spec=pltpu.PrefetchScalarGridSpec(
            num_scalar_prefetch=1, grid=(N_CHIPS, nh),
            in_specs=[pl.BlockSpec((None, tr, C), lambda k, i, c_ref: (k, c_ref[0] * nh + i, 0)),
                      pl.BlockSpec((None, tr, C), lambda k, i, c_ref: (k, i, 0))],
            out_specs=[out_spec, out_spec]),
        out_shape=[jax.ShapeDtypeStruct((N_CHIPS, rh, C), BF16)] * 2,
        compiler_params=_params(("parallel", "parallel")),
    )(c_idx, g, q)


def _sum_chips(name, r, c_idx):
    _, rh, C = r.shape
    tr = _pick(rh, max(16, (256 * 1024) // C // 16 * 16), 16)

    def body(c_ref, r_ref, o_ref):
        acc = r_ref[0].astype(F32)
        for k in range(1, N_CHIPS):
            acc = acc + r_ref[k].astype(F32)
        o_ref[...] = acc

    return pl.pallas_call(
        body, name=name,
        grid_spec=pltpu.PrefetchScalarGridSpec(
            num_scalar_prefetch=1, grid=(rh // tr,),
            in_specs=[pl.BlockSpec((N_CHIPS, tr, C), lambda i, c_ref: (0, i, 0))],
            out_specs=pl.BlockSpec((None, tr, C), lambda i, c_ref: (c_ref[0], i, 0))),
        out_shape=jax.ShapeDtypeStruct((2, rh, C), F32), compiler_params=_params(("parallel",)),
    )(c_idx, r)


def _share_halves(bufs):
    n = len(bufs)

    def body(*refs):
        out_refs = refs[n:2 * n]
        send_sems, recv_sems = refs[2 * n:]
        x, y, c, _ = _place()
        cps = []
        for a in range(n):
            cp = pltpu.make_async_remote_copy(
                src_ref=out_refs[a].at[c], dst_ref=out_refs[a].at[c], send_sem=send_sems.at[a],
                recv_sem=recv_sems.at[a], device_id=(x, y, 1 - c), device_id_type=MESH)
            cp.start()
            cps.append(cp)
        for a, cp in enumerate(cps):
            got = out_refs[a].at[1 - c]
            pltpu.make_async_remote_copy(src_ref=got, dst_ref=got, send_sem=send_sems.at[a], recv_sem=recv_sems.at[a],
                                         device_id=(x, y, c), device_id_type=MESH).wait_recv()
        for cp in cps:
            cp.wait_send()

    return pl.pallas_call(
        body, name="share_halves",
        out_shape=[jax.ShapeDtypeStruct(b.shape, b.dtype) for b in bufs],
        in_specs=[_HBM] * n, out_specs=[_HBM] * n, input_output_aliases={a: a for a in range(n)},
        scratch_shapes=[pltpu.SemaphoreType.DMA((n,)), pltpu.SemaphoreType.DMA((n,))],
    )(*bufs)


_BIG = ("w_in", "w_conv_out", "w_hgrn_out", "w_o", "w_ffn_gate", "w_ffn_up", "w_ffn_down")
_ROW_SHARDED = ("w_o", "w_ffn_down")
_WEIGHTS = ("w_ada", "b_ada", "norm_mix_g", "w_in", "conv_w", "lb_param", "gnorm_g", "w_conv_out", "w_hgrn_out",
            "w_o", "norm_ffn_g", "w_ffn_gate", "w_ffn_up", "w_ffn_down", "norm_final_g")


def _pack_rows(pieces):
    flat = jnp.concatenate([p.reshape(-1) for p in pieces])
    pad = (-flat.shape[0]) % 1024
    return jnp.pad(flat, (0, pad)).reshape(8, -1)


def kernel(x, c, w_ada, b_ada, norm_mix_g, w_in, conv_w, lb_param, gnorm_g, w_conv_out, w_hgrn_out, w_o, norm_ffn_g, w_ffn_gate, w_ffn_up, w_ffn_down, norm_final_g, loss_target, m_w_ada, m_b_ada, m_norm_mix_g, m_w_in, m_conv_w, m_lb_param, m_gnorm_g, m_w_conv_out, m_w_hgrn_out, m_w_o, m_norm_ffn_g, m_w_ffn_gate, m_w_ffn_up, m_w_ffn_down, m_norm_final_g, v_w_ada, v_b_ada, v_norm_mix_g, v_w_in, v_conv_w, v_lb_param, v_gnorm_g, v_w_conv_out, v_w_hgrn_out, v_w_o, v_norm_ffn_g, v_w_ffn_gate, v_w_ffn_up, v_w_ffn_down, v_norm_final_g):
    w = dict(w_ada=w_ada, b_ada=b_ada, norm_mix_g=norm_mix_g, w_in=w_in, conv_w=conv_w, lb_param=lb_param,
             gnorm_g=gnorm_g, w_conv_out=w_conv_out, w_hgrn_out=w_hgrn_out, w_o=w_o, norm_ffn_g=norm_ffn_g,
             w_ffn_gate=w_ffn_gate, w_ffn_up=w_ffn_up, w_ffn_down=w_ffn_down, norm_final_g=norm_final_g)
    m = dict(w_ada=m_w_ada, b_ada=m_b_ada, norm_mix_g=m_norm_mix_g, w_in=m_w_in, conv_w=m_conv_w, lb_param=m_lb_param,
             gnorm_g=m_gnorm_g, w_conv_out=m_w_conv_out, w_hgrn_out=m_w_hgrn_out, w_o=m_w_o, norm_ffn_g=m_norm_ffn_g,
             w_ffn_gate=m_w_ffn_gate, w_ffn_up=m_w_ffn_up, w_ffn_down=m_w_ffn_down, norm_final_g=m_norm_final_g)
    v = dict(w_ada=v_w_ada, b_ada=v_b_ada, norm_mix_g=v_norm_mix_g, w_in=v_w_in, conv_w=v_conv_w, lb_param=v_lb_param,
             gnorm_g=v_gnorm_g, w_conv_out=v_w_conv_out, w_hgrn_out=v_w_hgrn_out, w_o=v_w_o, norm_ffn_g=v_norm_ffn_g,
             w_ffn_gate=v_w_ffn_gate, w_ffn_up=v_w_ffn_up, w_ffn_down=v_w_ffn_down, norm_final_g=v_norm_final_g)
    two_d = lambda a: a.reshape((-1, a.shape[-1])) if a.ndim != 2 else a
    shapes = {k: a.shape for k, a in w.items()}
    w, m, v = ({k: two_d(a) for k, a in t.items()} for t in (w, m, v))
    w["lb_param"], m["lb_param"], v["lb_param"] = lb_param, m_lb_param, v_lb_param
    S, D = x.shape[1], x.shape[2]
    d_conv = D // 2
    ix, iy, ic = lax.axis_index("x"), lax.axis_index("y"), lax.axis_index("c")
    k_me = 2 * ix + iy
    dev = 2 * k_me + ic
    cw_shard = w["conv_w"].shape[1]
    ada_cols = w["w_ada"].shape[1]

    got = _allgather_rows("gather_cond", _pack_rows([c, w["conv_w"]])).reshape(N_DEV, -1)
    c_all = got[:, :D]
    conv_full = got[::2, D:D + 3 * cw_shard].reshape(N_CHIPS, 3, cw_shard).transpose(1, 0, 2).reshape(3, -1)
    c_act, c_act_b, lb = _prep(c_all, lb_param)
    b_cols = lax.dynamic_slice(w["b_ada"], (0, k_me * ada_cols), (1, ada_cols))
    mod_cols, = _matmul("ada_fwd", [(c_act_b, 'n', w["w_ada"].astype(BF16), 'n')], [0], N_DEV, ada_cols, D,
                        N_DEV, _pick(ada_cols, 1536, 128), D, [(b_cols, 'row', 0)], [(F32, None)],
                        lambda accs, ex: [accs[0] + ex[0]])
    mod_all = _allgather_rows("gather_mod", mod_cols).reshape(N_CHIPS, 2, N_DEV, ada_cols)[:, 0]
    mod_all = mod_all.transpose(1, 0, 2).reshape(N_DEV, -1)
    mod = lax.dynamic_slice(mod_all, (dev, 0), (1, mod_all.shape[1]))
    k_idx = jnp.reshape(k_me, (1,)).astype(jnp.int32)
    c_idx = jnp.reshape(ic, (1,)).astype(jnp.int32)
    bufs = {k: _cast_place("cast_" + k, w[k], k_idx) for k in _BIG}
    order = jnp.stack([k_me, 2 * (1 - ix) + iy, 2 * ix + (1 - iy), 2 * (1 - ix) + (1 - iy)]).astype(jnp.int32)

    loss, dx, small, arrived = _local_step(
        x[0], loss_target[0], mod, w["norm_mix_g"], lb, w["gnorm_g"], w["norm_ffn_g"], w["norm_final_g"], conv_full,
        bufs, c_idx, order)

    pieces = [small["dmod"], small["dg_mix"], small["dg_ffn"], small["dg_final"], small["dlb"], small["dgn"],
              small["dconv_w"], loss]
    sizes = [p.size for p in pieces]
    parts = _allgather_rows("gather_small", _pack_rows(pieces)).reshape(N_DEV, -1)
    total = _small_reduce(parts)
    offs = [0]
    for s in sizes:
        offs.append(offs[-1] + s)
    tot = [total[:, offs[i]:offs[i + 1]] for i in range(len(sizes))]
    dmod_all = parts[:, :sizes[0]]
    grads = {
        "b_ada": tot[0], "norm_mix_g": tot[1], "norm_ffn_g": tot[2], "norm_final_g": tot[3],
        "lb_param": _lb_grad(tot[4], lb), "gnorm_g": tot[5],
        "conv_w": lax.dynamic_slice(tot[6].reshape(3, -1), (0, k_me * cw_shard), (3, cw_shard)),
    }
    loss_out = tot[7][0, 0]

    halves = [_sum_chips("sum_chips_" + k, arrived[k], c_idx) for k in _BIG]
    whole = _share_halves(halves)
    for k, g in zip(_BIG, whole):
        grads[k] = g.reshape(-1, g.shape[-1])

    delta, new_m, new_v = {}, {}, {}
    dmod_cols = lax.dynamic_slice(dmod_all, (0, k_me * ada_cols), (N_DEV, ada_cols))
    grads["w_ada"], delta["w_ada"], new_m["w_ada"], new_v["w_ada"] = _ada_update(
        c_act.T, dmod_cols, w["w_ada"], m["w_ada"], v["w_ada"])
    for k in _WEIGHTS:
        if k != "w_ada":
            delta[k], new_m[k], new_v[k] = _adamw("adamw_" + k, w[k], grads[k], m[k], v[k])
    outs = [loss_out, dx.reshape(x.shape)]
    for t in (grads, delta, new_m, new_v):
        outs += [t[k].reshape(shapes[k]) for k in _WEIGHTS]
    return tuple(outs)
```

```python
import functools

import jax
import jax.numpy as jnp
from jax import lax
from jax.experimental import pallas as pl
from jax.experimental.pallas import tpu as pltpu

F32 = jnp.float32
BF16 = jnp.bfloat16
MESH = pl.DeviceIdType.MESH

EPS = 1e-6
HEAD = 128
BLK = 16
N_CHIPS = 4
N_DEV = 8
VMEM_LIMIT_BYTES = 56 * 1024 * 1024

ADAM_LR = 0.001
ADAM_B1 = 0.9
ADAM_B2 = 0.999
ADAM_EPS = 1e-08
ADAM_WD = 0.01
ADAM_STEP = 10


def _pick(dim, pref, mult):
    if dim <= pref:
        return dim
    t = (pref // mult) * mult
    while t >= mult:
        if dim % t == 0:
            return t
        t -= mult
    return dim


def _sig(x):
    return 1.0 / (1.0 + jnp.exp(-x))


def _params(sem):
    return pltpu.CompilerParams(dimension_semantics=sem, vmem_limit_bytes=VMEM_LIMIT_BYTES)


def _gj(j, i, k):
    return j


def _gk(j, i, k):
    return k


def _wspec(arr, rt, ct, r_of, c_of):
    if arr.ndim == 2:
        return pl.BlockSpec((rt, ct), lambda j, i, k: (r_of(j, i, k), c_of(j, i, k)))
    per = arr.shape[2] // ct
    assert arr.shape[2] % ct == 0
    return pl.BlockSpec((None, rt, ct),
                        lambda j, i, k: (c_of(j, i, k) // per, r_of(j, i, k), c_of(j, i, k) % per))


_HBM = pl.BlockSpec(memory_space=pltpu.HBM)


def _host(job, grid, in_specs, args, out_specs, out_shape, scratch, n_prefetch=0):
    if job is None:
        return {}, (lambda body: body)
    n_in, n_out, n_scr = len(args), len(out_shape), len(scratch)
    n_job, n_alias = len(job.inputs), job.n_alias
    in_specs += [_HBM] * n_job
    args += job.inputs
    out_specs += [_HBM] * n_alias
    out_shape += [jax.ShapeDtypeStruct(a.shape, a.dtype) for a in job.inputs[n_job - n_alias:]]
    scratch += job.sem_shapes
    aliases = {n_prefetch + n_in + n_job - n_alias + t: n_out + t for t in range(n_alias)}

    def wrap(body):
        def hosted(*refs):
            pre, refs = refs[:n_prefetch], refs[n_prefetch:]
            ins, refs = refs[:n_in], refs[n_in:]
            j_in, refs = refs[:n_job], refs[n_job:]
            outs, refs = refs[:n_out], refs[n_out:]
            j_out, refs = refs[:n_alias], refs[n_alias:]
            scr, sems = refs[:n_scr], refs[n_scr:]
            first = functools.reduce(jnp.logical_and, [pl.program_id(d) == 0 for d in range(len(grid))])
            last = functools.reduce(jnp.logical_and, [pl.program_id(d) == grid[d] - 1 for d in range(len(grid))])

            @pl.when(first)
            def _():
                job.start(j_in, j_out, sems)

            body(*pre, *ins, *outs, *scr)

            @pl.when(last)
            def _():
                job.finish(j_in, j_out, sems)

        return hosted

    return aliases, wrap


def _matmul(name, pairs, acc_of, M, N, K, tm, tn, tk, extras, outs, epilogue, job=None, rows_outer=False):
    assert M % tm == 0 and N % tn == 0 and K % tk == 0, (name, M, N, K, tm, tn, tk)
    nj, ni, nk = N // tn, M // tm, K // tk
    n_pairs, n_extra, n_out = len(pairs), len(extras), len(outs)
    n_acc = max(acc_of) + 1
    in_specs, args, dims = [], [], []
    lhs_of, seen = [], {}
    for a, am, b, bm in pairs:
        if (id(a), am) not in seen:
            seen[id(a), am] = len(args)
            args.append(a)
            if am == 'n':
                in_specs.append(pl.BlockSpec((tm, tk), lambda j, i, k: (i, k)))
            else:
                in_specs.append(pl.BlockSpec((tk, tm), lambda j, i, k: (k, i)))
        lhs_of.append(seen[id(a), am])
    n_lhs = len(args)
    for a, am, b, bm in pairs:
        if bm == 'n':
            in_specs.append(_wspec(b, tk, tn, _gk, _gj))
        else:
            in_specs.append(_wspec(b, tn, tk, _gj, _gk))
        dims.append((((1 if am == 'n' else 0,), (0 if bm == 'n' else 1,)), ((), ())))
        args.append(b)
    for e, kind, off in extras:
        assert off % tn == 0, (name, off, tn)
        o = off // tn
        if kind == 'tile':
            in_specs.append(pl.BlockSpec((tm, tn), lambda j, i, k, o=o: (i, j + o)))
        else:
            in_specs.append(pl.BlockSpec((1, tn), lambda j, i, k, o=o: (0, j + o)))
        args.append(e)
    out_shape, out_specs = [], []
    for dt, nb in outs:
        if nb is None:
            out_shape.append(jax.ShapeDtypeStruct((M, N), dt))
            out_specs.append(pl.BlockSpec((tm, tn), lambda j, i, k: (i, j)))
        else:
            assert nb % tn == 0 and N % nb == 0
            per = nb // tn
            out_shape.append(jax.ShapeDtypeStruct((N // nb, M, nb), dt))
            out_specs.append(pl.BlockSpec((None, tm, tn), lambda j, i, k, per=per: (j // per, i, j % per)))

    def body(*refs):
        n_ab = n_lhs + n_pairs
        e_refs = refs[n_ab:n_ab + n_extra]
        o_refs = refs[n_ab + n_extra:n_ab + n_extra + n_out]
        acc_refs = refs[n_ab + n_extra + n_out:]
        sums = [None] * n_acc
        for p in range(n_pairs):
            prod = lax.dot_general(refs[lhs_of[p]][...], refs[n_lhs + p][...], dims[p], preferred_element_type=F32)
            a = acc_of[p]
            sums[a] = prod if sums[a] is None else sums[a] + prod

        def finish(accs):
            res = epilogue(accs, [e[...] for e in e_refs])
            for o_ref, r in zip(o_refs, res):
                o_ref[...] = r.astype(o_ref.dtype)

        if nk == 1:
            finish(sums)
        else:
            k = pl.program_id(2)

            @pl.when(k == 0)
            def _():
                for a in range(n_acc):
                    acc_refs[a][...] = sums[a]

            @pl.when(k > 0)
            def _():
                for a in range(n_acc):
                    acc_refs[a][...] += sums[a]

            @pl.when(k == nk - 1)
            def _():
                finish([acc_refs[a][...] for a in range(n_acc)])

    scratch = [pltpu.VMEM((tm, tn), F32) for _ in range(n_acc)] if nk > 1 else []
    grid = (nj, ni, nk)
    if rows_outer:
        grid = (ni, nj, nk)
        swap = lambda spec: pl.BlockSpec(spec.block_shape, lambda i, j, k, f=spec.index_map: f(j, i, k))
        in_specs, out_specs = [swap(s) for s in in_specs], [swap(s) for s in out_specs]
    aliases, wrap = _host(job, grid, in_specs, args, out_specs, out_shape, scratch)
    sem = ("parallel", "parallel", "arbitrary") if job is None else ("arbitrary",) * 3
    return pl.pallas_call(
        wrap(body), name=name, grid=grid, in_specs=in_specs, out_specs=out_specs, out_shape=out_shape,
        scratch_shapes=scratch, input_output_aliases=aliases, compiler_params=_params(sem),
    )(*args)


def _row_spec(tr, d):
    return pl.BlockSpec((tr, d), lambda i: (i, 0))


def _vec_spec(d):
    return pl.BlockSpec((1, d), lambda i: (0, 0))


def _norm_mod(name, x, g, sc, sh):
    S, D = x.shape
    tr = _pick(S, 256, 8)

    def body(x_ref, g_ref, sc_ref, sh_ref, h_ref):
        xv = x_ref[...]
        r = lax.rsqrt(jnp.mean(xv * xv, axis=-1, keepdims=True) + EPS)
        h_ref[...] = ((xv * r) * g_ref[...] * (1.0 + sc_ref[...]) + sh_ref[...]).astype(BF16)

    return pl.pallas_call(
        body, name=name, grid=(S // tr,),
        in_specs=[_row_spec(tr, D), _vec_spec(D), _vec_spec(D), _vec_spec(D)],
        out_specs=_row_spec(tr, D), out_shape=jax.ShapeDtypeStruct((S, D), BF16),
        compiler_params=_params(("parallel",)),
    )(x, g, sc, sh)


def _loss_head(x2, target, g, ff, gt):
    S, D = x2.shape
    tr = _pick(S, 256, 8)

    def body(x_ref, t_ref, g_ref, ff_ref, gt_ref, dx_ref, dffb_ref, loss_ref, dgt_ref, dg_ref):
        i = pl.program_id(0)

        @pl.when(i == 0)
        def _():
            loss_ref[...] = jnp.zeros_like(loss_ref)
            dgt_ref[...] = jnp.zeros_like(dgt_ref)
            dg_ref[...] = jnp.zeros_like(dg_ref)

        xv = x_ref[...]
        gv = g_ref[...]
        r = lax.rsqrt(jnp.mean(xv * xv, axis=-1, keepdims=True) + EPS)
        xh = xv * r
        err = xh * gv - t_ref[...]
        loss_ref[...] += 0.5 * jnp.sum(jnp.mean(err * err, axis=-1, keepdims=True))
        dy = err * (1.0 / D)
        dg_ref[...] += jnp.sum(dy * xh, axis=0, keepdims=True)
        dxh = dy * gv
        dx = r * (dxh - xh * jnp.mean(dxh * xh, axis=-1, keepdims=True))
        dx_ref[...] = dx
        dffb_ref[...] = (dx * gt_ref[...]).astype(BF16)
        dgt_ref[...] += jnp.sum(dx * ff_ref[...], axis=0, keepdims=True)

    return pl.pallas_call(
        body, name="loss_head", grid=(S // tr,),
        in_specs=[_row_spec(tr, D), _row_spec(tr, D), _vec_spec(D), _row_spec(tr, D), _vec_spec(D)],
        out_specs=[_row_spec(tr, D), _row_spec(tr, D), pl.BlockSpec((1, 128), lambda i: (0, 0)),
                   _vec_spec(D), _vec_spec(D)],
        out_shape=[jax.ShapeDtypeStruct((S, D), F32), jax.ShapeDtypeStruct((S, D), BF16),
                   jax.ShapeDtypeStruct((1, 128), F32), jax.ShapeDtypeStruct((1, D), F32),
                   jax.ShapeDtypeStruct((1, D), F32)],
        compiler_params=_params(("arbitrary",)),
    )(x2, target, g, ff, gt)


def _norm_mod_bwd(name, dh, x, g, sc, dres, gated=None):
    S, D = x.shape
    tr = _pick(S, 256, 8)
    n = S // tr
    with_gate = gated is not None

    def body(*refs):
        if with_gate:
            dh_ref, x_ref, g_ref, sc_ref, dres_ref, mo_ref, gt_ref, dx_ref, dsh_ref, dsc_ref, dg_ref, dmob_ref, dgt_ref = refs
        else:
            dh_ref, x_ref, g_ref, sc_ref, dres_ref, dx_ref, dsh_ref, dsc_ref, dg_ref = refs
        i = pl.program_id(0)

        @pl.when(i == 0)
        def _():
            dsh_ref[...] = jnp.zeros_like(dsh_ref)
            dsc_ref[...] = jnp.zeros_like(dsc_ref)
            if with_gate:
                dgt_ref[...] = jnp.zeros_like(dgt_ref)

        xv = x_ref[...]
        dhv = dh_ref[...]
        gv = g_ref[...]
        scale = 1.0 + sc_ref[...]
        r = lax.rsqrt(jnp.mean(xv * xv, axis=-1, keepdims=True) + EPS)
        xh = xv * r
        dsh_ref[...] += jnp.sum(dhv, axis=0, keepdims=True)
        dsc_ref[...] += jnp.sum(dhv * xh, axis=0, keepdims=True)
        dxh = dhv * (gv * scale)
        dx = dres_ref[...] + r * (dxh - xh * jnp.mean(dxh * xh, axis=-1, keepdims=True))
        dx_ref[...] = dx
        if with_gate:
            dmob_ref[...] = (dx * gt_ref[...]).astype(BF16)
            dgt_ref[...] += jnp.sum(dx * mo_ref[...], axis=0, keepdims=True)

        @pl.when(i == n - 1)
        def _():
            t = dsc_ref[...]
            dg_ref[...] = t * scale
            dsc_ref[...] = t * gv

    in_specs = [_row_spec(tr, D), _row_spec(tr, D), _vec_spec(D), _vec_spec(D), _row_spec(tr, D)]
    args = [dh, x, g, sc, dres]
    out_specs = [_row_spec(tr, D), _vec_spec(D), _vec_spec(D), _vec_spec(D)]
    out_shape = [jax.ShapeDtypeStruct((S, D), F32)] + [jax.ShapeDtypeStruct((1, D), F32)] * 3
    if with_gate:
        in_specs += [_row_spec(tr, D), _vec_spec(D)]
        args += list(gated)
        out_specs += [_row_spec(tr, D), _vec_spec(D)]
        out_shape += [jax.ShapeDtypeStruct((S, D), BF16), jax.ShapeDtypeStruct((1, D), F32)]
    return pl.pallas_call(
        body, name=name, grid=(n,), in_specs=in_specs, out_specs=out_specs, out_shape=out_shape,
        compiler_params=_params(("arbitrary",)),
    )(*args)


CONV_ROWS = 256


def _col_spec(S, off_cols):
    o = off_cols // HEAD
    return pl.BlockSpec((S, HEAD), lambda c, o=o: (0, c + o))


def _conv_taps(u, u_prev, rid):
    s1 = jnp.where(rid >= 1, pltpu.roll(u, 1, 0), pltpu.roll(u_prev, 1, 0))
    s2 = jnp.where(rid >= 2, pltpu.roll(u, 2, 0), pltpu.roll(u_prev, 2, 0))
    return s1, s2


def _conv_fwd(proj, conv_w, d_conv):
    S = proj.shape[0]
    R = min(CONV_ROWS, S)
    nr = S // R

    def body(ab_ref, ac_ref, ax_ref, w_ref, ya_ref):
        w0, w1, w2 = w_ref[0:1, :], w_ref[1:2, :], w_ref[2:3, :]
        rid = lax.broadcasted_iota(jnp.int32, (R, HEAD), 0)

        def step(c, carry):
            rows = pl.ds(pl.multiple_of(c * R, R), R)
            prev = pl.ds(pl.multiple_of(jnp.maximum(c - 1, 0) * R, R), R)
            u = ac_ref[rows, :] * ax_ref[rows, :]
            u_prev = jnp.where(c > 0, ac_ref[prev, :] * ax_ref[prev, :], 0.0)
            s1, s2 = _conv_taps(u, u_prev, rid)
            y = w0 * s2 + w1 * s1 + w2 * u
            ya_ref[rows, :] = (ab_ref[rows, :] * y).astype(BF16)
            return carry

        lax.fori_loop(0, nr, step, 0)

    return pl.pallas_call(
        body, name="conv_fwd", grid=(d_conv // HEAD,),
        in_specs=[_col_spec(S, 0), _col_spec(S, d_conv), _col_spec(S, 2 * d_conv),
                  pl.BlockSpec((3, HEAD), lambda c: (0, c))],
        out_specs=pl.BlockSpec((S, HEAD), lambda c: (0, c)),
        out_shape=jax.ShapeDtypeStruct((S, d_conv), BF16),
        compiler_params=_params(("parallel",)),
    )(proj, proj, proj, conv_w)


def _conv_bwd(dya, proj, conv_w, d_conv):
    S = proj.shape[0]
    R = min(CONV_ROWS, S)
    nr = S // R

    def body(dya_ref, ab_ref, ac_ref, ax_ref, w_ref, dab_ref, dac_ref, dax_ref, dw_ref):
        w0, w1, w2 = w_ref[0:1, :], w_ref[1:2, :], w_ref[2:3, :]
        rid = lax.broadcasted_iota(jnp.int32, (R, HEAD), 0)

        def step(c, carry):
            dw0, dw1, dw2 = carry
            rows = pl.ds(pl.multiple_of(c * R, R), R)
            prev = pl.ds(pl.multiple_of(jnp.maximum(c - 1, 0) * R, R), R)
            nxt = pl.ds(pl.multiple_of(jnp.minimum(c + 1, nr - 1) * R, R), R)
            a_c, a_x, a_b = ac_ref[rows, :], ax_ref[rows, :], ab_ref[rows, :]
            u = a_c * a_x
            u_prev = jnp.where(c > 0, ac_ref[prev, :] * ax_ref[prev, :], 0.0)
            s1, s2 = _conv_taps(u, u_prev, rid)
            y = w0 * s2 + w1 * s1 + w2 * u
            dy = dya_ref[rows, :]
            dab_ref[rows, :] = (dy * y).astype(BF16)
            dyc = dy * a_b
            dyc_next = jnp.where(c < nr - 1, dya_ref[nxt, :] * ab_ref[nxt, :], 0.0)
            t1 = jnp.where(rid < R - 1, pltpu.roll(dyc, R - 1, 0), pltpu.roll(dyc_next, R - 1, 0))
            t2 = jnp.where(rid < R - 2, pltpu.roll(dyc, R - 2, 0), pltpu.roll(dyc_next, R - 2, 0))
            du = w2 * dyc + w1 * t1 + w0 * t2
            dac_ref[rows, :] = (du * a_x).astype(BF16)
            dax_ref[rows, :] = (du * a_c).astype(BF16)
            dw0 = dw0 + jnp.sum(dyc * s2, axis=0, keepdims=True)
            dw1 = dw1 + jnp.sum(dyc * s1, axis=0, keepdims=True)
            dw2 = dw2 + jnp.sum(dyc * u, axis=0, keepdims=True)
            return dw0, dw1, dw2

        z = jnp.zeros((1, HEAD), F32)
        dw0, dw1, dw2 = lax.fori_loop(0, nr, step, (z, z, z))
        dw_ref[0:1, :] = dw0
        dw_ref[1:2, :] = dw1
        dw_ref[2:3, :] = dw2

    col = pl.BlockSpec((S, HEAD), lambda c: (0, c))
    return pl.pallas_call(
        body, name="conv_bwd", grid=(d_conv // HEAD,),
        in_specs=[col, _col_spec(S, 0), _col_spec(S, d_conv), _col_spec(S, 2 * d_conv),
                  pl.BlockSpec((3, HEAD), lambda c: (0, c))],
        out_specs=[col, col, col, pl.BlockSpec((3, HEAD), lambda c: (0, c))],
        out_shape=[jax.ShapeDtypeStruct((S, d_conv), BF16)] * 3 + [jax.ShapeDtypeStruct((3, d_conv), F32)],
        compiler_params=_params(("parallel",)),
    )(dya, proj, proj, proj, conv_w)


HGRN_ROWS = 256
HGRN_FWD_SUB = 64
HGRN_BWD_SUB = 32
HGRN_GATE_ROWS = 128


def _block_cumsum(x, pos):
    for s in (1, 2, 4, 8):
        x = x + jnp.where(pos >= s, pltpu.roll(x, s, 0), 0.0)
    return x


def _block_rev_cumsum(x, pos, rows):
    for s in (1, 2, 4, 8):
        x = x + jnp.where(pos < BLK - s, pltpu.roll(x, rows - s, 0), 0.0)
    return x


def _block_last(x, pos, rows):
    m = jnp.where(pos == BLK - 1, x, 0.0)
    for s in (1, 2, 4, 8):
        m = m + pltpu.roll(m, rows - s, 0)
    return m


def _hgrn_gates(q, z, lb):
    sgq = _sig(q)
    qs = q * sgq
    sg = _sig(z)
    f = lb + (1.0 - lb) * sg
    kk = (1.0 - lb) * (1.0 - sg)
    return sgq, qs, sg, f, kk


def _hgrn_decays(q, z, lb, pos, rows):
    sgq, qs, sg, f, kk = _hgrn_gates(q, z, lb)
    b = _block_cumsum(jnp.log(f), pos)
    return sgq, qs, sg, f, kk, b, _block_last(b, pos, rows)


def _hgrn_specs(T, d_conv, n_t, rev):
    def t_of(i):
        return (n_t - 1 - i) if rev else i

    def pcol(off):
        o = off // HEAD
        return pl.BlockSpec((T, HEAD), lambda h, i, o=o: (t_of(i), h + o))

    q_spec, z_spec, v_spec, g_spec = pcol(3 * d_conv), pcol(4 * d_conv), pcol(5 * d_conv), pcol(6 * d_conv)
    lb_spec = pl.BlockSpec((1, HEAD), lambda h, i: (0, h))
    gn_spec = pl.BlockSpec((1, HEAD), lambda h, i: (0, 0))
    act_spec = pl.BlockSpec((T, HEAD), lambda h, i: (t_of(i), h))
    st_spec = pl.BlockSpec((None, T // BLK, HEAD, HEAD), lambda h, i: (h, t_of(i), 0, 0))
    return q_spec, z_spec, v_spec, g_spec, lb_spec, gn_spec, act_spec, st_spec


def _hgrn_fwd(proj, lb, gn, d_conv, job=None):
    S = proj.shape[0]
    H = d_conv // HEAD
    T = min(HGRN_ROWS, S)
    n_t, nb = S // T, T // BLK
    sub = min(HGRN_FWD_SUB, T)
    q_spec, z_spec, v_spec, g_spec, lb_spec, gn_spec, act_spec, st_spec = _hgrn_specs(T, d_conv, n_t, False)

    def body(q_ref, z_ref, v_ref, g_ref, lb_ref, gn_ref, ob_ref, o_ref, st_ref, qe_s, ke_s, dec_s, state, v_s, o_s, u_s):
        @pl.when(pl.program_id(1) == 0)
        def _():
            state[...] = jnp.zeros_like(state)

        pos = lax.broadcasted_iota(jnp.int32, (sub, HEAD), 0) % BLK
        lbv = lb_ref[...]

        def vector_pass(s, carry):
            r = pl.ds(pl.multiple_of(s * sub, sub), sub)
            v = v_ref[r, :]
            _, qs, _, _, kk, b, b_tot = _hgrn_decays(q_ref[r, :], z_ref[r, :], lbv, pos, sub)
            qe_s[r, :] = (qs * jnp.exp(b)).astype(BF16)
            ke_s[r, :] = (kk * jnp.exp(b_tot - b)).astype(BF16)
            v_s[r, :] = v.astype(BF16)
            dec_s[r, :] = jnp.exp(b_tot)
            o = jnp.sum(qs * kk, axis=-1, keepdims=True) * v
            for d in range(1, BLK):
                e = jnp.exp(b - pltpu.roll(b, d, 0))
                a = jnp.sum(qs * pltpu.roll(kk, d, 0) * e, axis=-1, keepdims=True)
                o = o + jnp.where(pos >= d, a * pltpu.roll(v, d, 0), 0.0)
            o_s[r, :] = o
            return carry

        lax.fori_loop(0, T // sub, vector_pass, 0)

        for j in range(nb):
            rows = pl.ds(j * BLK, BLK)
            u_s[j] = lax.dot_general(v_s[rows, :], ke_s[rows, :], (((0,), (0,)), ((), ())),
                                     preferred_element_type=F32)
        st = state[...]
        for j in range(nb):
            st_ref[j] = st.astype(BF16)
            st = st * dec_s[pl.ds(j * BLK, 1), :] + u_s[j]
        state[...] = st
        for j in range(nb):
            rows = pl.ds(j * BLK, BLK)
            o_s[rows, :] += lax.dot_general(qe_s[rows, :], st_ref[j], (((1,), (1,)), ((), ())),
                                            preferred_element_type=F32)
        o = o_s[...]
        o_ref[...] = o
        r = lax.rsqrt(jnp.mean(o * o, axis=-1, keepdims=True) + EPS)
        g = g_ref[...]
        ob_ref[...] = ((o * r) * gn_ref[...] * (g * _sig(g))).astype(BF16)

    grid = (H, n_t)
    in_specs = [q_spec, z_spec, v_spec, g_spec, lb_spec, gn_spec]
    args = [proj, proj, proj, proj, lb, gn]
    out_specs = [act_spec, act_spec, st_spec, act_spec, act_spec, act_spec]
    out_shape = [jax.ShapeDtypeStruct((S, d_conv), BF16), jax.ShapeDtypeStruct((S, d_conv), F32),
                 jax.ShapeDtypeStruct((H, S // BLK, HEAD, HEAD), BF16),
                 jax.ShapeDtypeStruct((S, d_conv), BF16), jax.ShapeDtypeStruct((S, d_conv), BF16),
                 jax.ShapeDtypeStruct((S, d_conv), F32)]
    scratch = [pltpu.VMEM((HEAD, HEAD), F32), pltpu.VMEM((T, HEAD), BF16), pltpu.VMEM((T, HEAD), F32),
               pltpu.VMEM((nb, HEAD, HEAD), F32)]
    aliases, wrap = _host(job, grid, in_specs, args, out_specs, out_shape, scratch)
    return pl.pallas_call(
        wrap(body), name="hgrn_fwd", grid=grid, in_specs=in_specs, out_specs=out_specs, out_shape=out_shape,
        scratch_shapes=scratch, input_output_aliases=aliases,
        compiler_params=_params(("parallel" if job is None else "arbitrary", "arbitrary")),
    )(*args)


def _hgrn_bwd(dob, o_raw, states, qe, ke, dec, proj, lb, gn, d_conv, job=None):
    S = proj.shape[0]
    H = d_conv // HEAD
    T = min(HGRN_ROWS, S)
    n_t, nb = S // T, T // BLK
    sub = min(HGRN_BWD_SUB, T)
    gate_rows = min(HGRN_GATE_ROWS, T)
    q_spec, z_spec, v_spec, g_spec, lb_spec, gn_spec, act_spec, st_spec = _hgrn_specs(T, d_conv, n_t, True)

    def body(dob_ref, o_ref, st_ref, qe_b, ke_b, dec_s, q_ref, z_ref, v_ref, g_ref, lb_ref, gn_ref,
             dq_ref, dz_ref, dv_ref, dg_ref, dlb_ref, dgn_ref,
             dstate, do_b, v_b, dqe_s, dke_s, dvi_s, ddec_s, do_s, u_s, ds_s):
        @pl.when(pl.program_id(1) == 0)
        def _():
            dstate[...] = jnp.zeros_like(dstate)
            dlb_ref[...] = jnp.zeros_like(dlb_ref)
            dgn_ref[...] = jnp.zeros_like(dgn_ref)

        pos = lax.broadcasted_iota(jnp.int32, (sub, HEAD), 0) % BLK
        lbv, gnv = lb_ref[...], gn_ref[...]

        def before(s, carry):
            r = pl.ds(pl.multiple_of(s * gate_rows, gate_rows), gate_rows)
            g = g_ref[r, :]
            v_b[r, :] = v_ref[r, :].astype(BF16)
            o = o_ref[r, :]
            rs = lax.rsqrt(jnp.mean(o * o, axis=-1, keepdims=True) + EPS)
            on = o * rs
            sgg = _sig(g)
            dobv = dob_ref[r, :]
            dog = dobv * (g * sgg)
            dgn_ref[...] += jnp.sum(dog * on, axis=0, keepdims=True)
            don = dog * gnv
            do = rs * (don - on * jnp.mean(don * on, axis=-1, keepdims=True))
            dg_ref[r, :] = (dobv * (on * gnv) * (sgg * (1.0 + g * (1.0 - sgg)))).astype(BF16)
            do_s[r, :] = do
            do_b[r, :] = do.astype(BF16)
            return carry

        lax.fori_loop(0, T // gate_rows, before, 0)

        for j in range(nb):
            rows = pl.ds(j * BLK, BLK)
            dob_j = do_b[rows, :]
            u_s[j] = lax.dot_general(dob_j, qe_b[rows, :], (((0,), (0,)), ((), ())), preferred_element_type=F32)
            dqe_s[rows, :] = lax.dot_general(dob_j, st_ref[j], (((1,), (0,)), ((), ())), preferred_element_type=F32)
        dst = dstate[...]
        for j in reversed(range(nb)):
            ds_s[j] = dst.astype(BF16)
            ddec = jnp.sum(st_ref[j].astype(F32) * dst, axis=0, keepdims=True)
            ddec_s[pl.ds(j * BLK, BLK), :] = jnp.broadcast_to(ddec, (BLK, HEAD))
            dst = dst * dec_s[pl.ds(j * BLK, 1), :] + u_s[j]
        dstate[...] = dst
        for j in range(nb):
            rows = pl.ds(j * BLK, BLK)
            dke_s[rows, :] = lax.dot_general(v_b[rows, :], ds_s[j], (((1,), (0,)), ((), ())), preferred_element_type=F32)
            dvi_s[rows, :] = lax.dot_general(ke_b[rows, :], ds_s[j], (((1,), (1,)), ((), ())), preferred_element_type=F32)

        def after(s, carry):
            r = pl.ds(pl.multiple_of(s * sub, sub), sub)
            q, v = q_ref[r, :], v_ref[r, :]
            sgq, qs, sg, f, kk, b, b_tot = _hgrn_decays(q, z_ref[r, :], lbv, pos, sub)
            do = do_s[r, :]
            dqs = dqe_s[r, :] * jnp.exp(b)
            dkk = dke_s[r, :] * jnp.exp(b_tot - b)
            d_tot = jnp.where(pos == BLK - 1, _block_cumsum(dkk * kk, pos) + ddec_s[r, :] * jnp.exp(b_tot), 0.0)
            dv = dvi_s[r, :]
            da = jnp.sum(do * v, axis=-1, keepdims=True)
            dv = dv + jnp.sum(qs * kk, axis=-1, keepdims=True) * do
            dqs = dqs + da * kk
            dkk = dkk + da * qs
            for d in range(1, BLK):
                kd = pltpu.roll(kk, d, 0)
                e = jnp.where(pos >= d, jnp.exp(b - pltpu.roll(b, d, 0)), 0.0)
                a = jnp.sum(qs * kd * e, axis=-1, keepdims=True)
                da = jnp.sum(do * pltpu.roll(v, d, 0), axis=-1, keepdims=True)
                gq = da * e
                dqs = dqs + gq * kd
                dkk = dkk + pltpu.roll(gq * qs, sub - d, 0)
                dv = dv + pltpu.roll(a * do, sub - d, 0)
            db = qs * dqs - kk * dkk + d_tot
            dlf = _block_rev_cumsum(db, pos, sub)
            df = dlf / f - dkk
            dlb_ref[...] += jnp.sum(df * (1.0 - sg), axis=0, keepdims=True)
            dz_ref[r, :] = (df * (1.0 - lbv) * sg * (1.0 - sg)).astype(BF16)
            dq_ref[r, :] = (dqs * (sgq * (1.0 + q * (1.0 - sgq)))).astype(BF16)
            dv_ref[r, :] = dv.astype(BF16)
            return carry

        lax.fori_loop(0, T // sub, after, 0)

    tb = lambda dt: pltpu.VMEM((T, HEAD), dt)
    grid = (H, n_t)
    in_specs = [act_spec, act_spec, st_spec, act_spec, act_spec, act_spec, q_spec, z_spec, v_spec, g_spec, lb_spec,
                gn_spec]
    args = [dob, o_raw, states, qe, ke, dec, proj, proj, proj, proj, lb, gn]
    out_specs = [act_spec, act_spec, act_spec, act_spec, lb_spec, pl.BlockSpec((None, 1, HEAD), lambda h, i: (h, 0, 0))]
    out_shape = ([jax.ShapeDtypeStruct((S, d_conv), BF16)] * 4
                 + [jax.ShapeDtypeStruct((1, d_conv), F32), jax.ShapeDtypeStruct((H, 1, HEAD), F32)])
    scratch = [pltpu.VMEM((HEAD, HEAD), F32), tb(BF16), tb(BF16), tb(F32), tb(F32), tb(F32), tb(F32), tb(F32),
               pltpu.VMEM((nb, HEAD, HEAD), F32), pltpu.VMEM((nb, HEAD, HEAD), BF16)]
    aliases, wrap = _host(job, grid, in_specs, args, out_specs, out_shape, scratch)
    return pl.pallas_call(
        wrap(body), name="hgrn_bwd", grid=grid, in_specs=in_specs, out_specs=out_specs, out_shape=out_shape,
        scratch_shapes=scratch, input_output_aliases=aliases,
        compiler_params=_params(("parallel" if job is None else "arbitrary", "arbitrary")),
    )(*args)


def _first(accs, extras):
    return [accs[0]]


def _local_step(x, target, mod, norm_mix_g, lb, gnorm_g, norm_ffn_g, norm_final_g, conv_w, bufs, c_idx, order):
    S, D = x.shape
    d_conv = D // 2
    d_in = 7 * d_conv + 2 * D
    d_ff = N_CHIPS * bufs["w_ffn_down"].shape[1]
    nb_in, nb_co, nb_ff = bufs["w_in"].shape[2], bufs["w_conv_out"].shape[2], bufs["w_ffn_gate"].shape[2]
    rows = lambda g: g.reshape(-1, g.shape[2])
    sh_m, sc_m, gt_m, sh_f, sc_f, gt_f = [mod[:, i * D:(i + 1) * D] for i in range(6)]
    tm = _pick(S, 512, 16)
    tm2 = _pick(S, 1024, 16)
    tn_in = _pick(nb_in, 1408, 128)
    tn_co = _pick(nb_co, 512, 128)
    tn_ff = _pick(nb_ff, 1408, 128)
    tn_d = _pick(D, 512, 128)
    tw = _pick(D, 1024, 128)
    tg = _pick(D, 512, 128)

    h = _norm_mod("norm_mix", x, norm_mix_g, sc_m, sh_m)
    proj, w_in = _proj_gather(h, bufs["w_in"], order, None)
    ob, o_raw, states, qe, ke, dec, *got = _hgrn_fwd(
        proj, lb, gnorm_g, d_conv,
        job=_GatherJob([bufs[k] for k in ("w_conv_out", "w_hgrn_out", "w_o", "w_ffn_gate")]))
    ya = _conv_fwd(proj, conv_w, d_conv)
    w_conv_out, w_hgrn_out, w_o, w_ffn_gate = _forward_halves("forward_branch", got)
    w_o = rows(w_o)

    def merge_epi(accs, ex):
        y_a, y_b = accs
        return [y_a, y_b, _sig(ex[0]) * y_a + _sig(ex[1]) * y_b]

    y_a, y_b, merged, w_ffn_up = _matmul(
        "branch_out", [(ya, 'n', w_conv_out, 'n'), (ob, 'n', w_hgrn_out, 'n')], [0, 1], S, D, d_conv, tm2, tn_co, d_conv,
        [(proj, 'tile', 7 * d_conv), (proj, 'tile', 7 * d_conv + D)], [(F32, None), (F32, None), (BF16, None)], merge_epi,
        job=_GatherJob([bufs["w_ffn_up"]], 0, 2))

    def resid_epi(accs, ex):
        return [accs[0], ex[0] + ex[1] * accs[0]]

    mo, x1, w_ffn_up = _matmul("w_o", [(merged, 'n', w_o, 'n')], [0], S, D, D, tm2, tw, D,
                               [(x, 'tile', 0), (gt_m, 'row', 0)], [(F32, None), (F32, None)], resid_epi,
                               job=_GatherJob([w_ffn_up], 1, 2))
    h2 = _norm_mod("norm_ffn", x1, norm_ffn_g, sc_f, sh_f)
    w_ffn_up, = _forward_halves("forward_up", [w_ffn_up])

    def swiglu_epi(accs, ex):
        gg, uu = accs
        return [gg, uu, gg * _sig(gg) * uu]

    G, U, act, w_ffn_down = _matmul(
        "ffn_in", [(h2, 'n', w_ffn_gate, 'n'), (h2, 'n', w_ffn_up, 'n')], [0, 1], S, d_ff, D,
        tm, tn_ff, D, [], [(BF16, None), (BF16, None), (BF16, None)], swiglu_epi, job=_GatherJob([bufs["w_ffn_down"]]))
    w_ffn_down = rows(_forward_halves("forward_down", [w_ffn_down])[0])
    ff, x2 = _matmul("ffn_out", [(act, 'n', w_ffn_down, 'n')], [0], S, D, d_ff, tm2, tn_d, d_ff,
                     [(x1, 'tile', 0), (gt_f, 'row', 0)], [(F32, None), (F32, None)], resid_epi, rows_outer=True)

    dx2, dffb, loss, dgt_f, dg_final = _loss_head(x2, target, norm_final_g, ff, gt_f)

    def swiglu_bwd_epi(accs, ex):
        dact, gg, uu = accs[0], ex[0], ex[1]
        s = _sig(gg)
        return [dact * uu * (s * (1.0 + gg * (1.0 - s))), dact * (gg * s)]

    dG, dU = _matmul("d_act", [(dffb, 'n', w_ffn_down, 't')], [0], S, d_ff, D, tm2, tn_ff, D,
                     [(G, 'tile', 0), (U, 'tile', 0)], [(BF16, None), (BF16, None)], swiglu_bwd_epi)
    dw_down, = _matmul("dw_ffn_down", [(act, 't', dffb, 'n')], [0], d_ff, D, S, _pick(d_ff, 512, 128),
                       D, S, [], [(BF16, None)], _first)
    dh2, = _matmul("d_h2", [(dG, 'n', w_ffn_gate, 't'), (dU, 'n', w_ffn_up, 't')], [0, 0], S, D, d_ff,
                   tm, D, tn_ff, [], [(F32, None)], _first)
    dw_gate, = _matmul("dw_ffn_gate", [(h2, 't', dG, 'n')], [0], D, d_ff, S, tg, tn_ff, S, [], [(BF16, nb_ff)], _first)
    dw_up, = _matmul("dw_ffn_up", [(h2, 't', dU, 'n')], [0], D, d_ff, S, tg, tn_ff, S, [], [(BF16, nb_ff)], _first)

    def pre_sum(tag, names, grads):
        from_sibling = _swap_halves("swap_" + tag, grads)
        pairs = [_pair_sum("pair_sum_" + k, g, q, c_idx) for k, g, q in zip(names, grads, from_sibling)]
        return [p[0] for p in pairs], [p[1] for p in pairs]

    ffn_names = ["w_ffn_down", "w_ffn_gate", "w_ffn_up"]
    parts_f, slots_f = pre_sum("ffn", ffn_names, [dw_down.reshape(N_CHIPS, -1, D), dw_gate, dw_up])
    dx1, dsh_f, dsc_f, dg_ffn, dmob, dgt_m = _norm_mod_bwd("norm_ffn_bwd", dh2, x1, norm_ffn_g, sc_f, dx2, (mo, gt_m))

    def merge_bwd_epi(accs, ex):
        dm, ga, gb, ya_, yb_ = accs[0], ex[0], ex[1], ex[2], ex[3]
        sa, sb = _sig(ga), _sig(gb)
        return [dm * ya_ * sa * (1.0 - sa), dm * yb_ * sb * (1.0 - sb), dm * sa, dm * sb]

    dga, dgb, dy_a, dy_b = _matmul(
        "d_merged", [(dmob, 'n', w_o, 't')], [0], S, D, D, tm2, tn_co, D,
        [(proj, 'tile', 7 * d_conv), (proj, 'tile', 7 * d_conv + D), (y_a, 'tile', 0), (y_b, 'tile', 0)],
        [(BF16, None)] * 4, merge_bwd_epi)
    dw_o, = _matmul("dw_o", [(merged, 't', dmob, 'n')], [0], D, D, S, tg, D, S, [], [(BF16, None)], _first)
    both = lambda accs, ex: accs
    dya, dob = _matmul("d_branch", [(dy_a, 'n', w_conv_out, 't'), (dy_b, 'n', w_hgrn_out, 't')], [0, 1], S, d_conv, D,
                       tm2, _pick(d_conv, 1024, 128), tn_co, [], [(F32, None), (F32, None)], both)
    dw_co, dw_ho = _matmul("dw_branch", [(ya, 't', dy_a, 'n'), (ob, 't', dy_b, 'n')], [0, 1], d_conv, D, S,
                           _pick(d_conv, 512, 128), tn_co, S, [], [(BF16, nb_co), (BF16, nb_co)], both)
    branch_names = ["w_conv_out", "w_hgrn_out", "w_o"]
    parts_b, slots_b = pre_sum("branch", branch_names, [dw_co, dw_ho, dw_o.reshape(N_CHIPS, -1, D)])
    dab, dac, dax, dconv_w = _conv_bwd(dya, proj, conv_w, d_conv)
    dq, dz, dv, dgo, dlb, dgn, *arrived = _hgrn_bwd(dob, o_raw, states, qe, ke, dec, proj, lb, gnorm_g, d_conv,
                                                    job=_ScatterJob(parts_f + parts_b, slots_f + slots_b))
    dproj = jnp.concatenate([dab, dac, dax, dq, dz, dv, dgo, dga, dgb], axis=1)
    dw_in, = _matmul("dw_in", [(h, 't', dproj, 'n')], [0], D, d_in, S, tg, tn_in, S, [], [(BF16, nb_in)], _first)
    parts_i, slots_i = pre_sum("in", ["w_in"], [dw_in])
    dh, arrived_in = _matmul("d_h", [(dproj, 'n', w_in, 't')], [0], S, D, d_in, tm2, D, tn_in, [], [(F32, None)], _first,
                             job=_ScatterJob(parts_i, slots_i))
    dx, dsh_m, dsc_m, dg_mix = _norm_mod_bwd("norm_mix_bwd", dh, x, norm_mix_g, sc_m, dx1)
    dmod = jnp.concatenate([dsh_m, dsc_m, dgt_m, dsh_f, dsc_f, dgt_f], axis=1)
    small = dict(dmod=dmod, dg_mix=dg_mix, dg_ffn=dg_ffn, dg_final=dg_final, dlb=dlb,
                 dgn=jnp.sum(dgn, axis=0), dconv_w=dconv_w)
    big = dict(zip(ffn_names + branch_names, arrived), w_in=arrived_in)
    return loss, dx, small, big


def _adamw_math(w, g, m, v):
    m = ADAM_B1 * m + (1.0 - ADAM_B1) * g
    v = ADAM_B2 * v + (1.0 - ADAM_B2) * (g * g)
    m_hat = m / (1.0 - ADAM_B1 ** ADAM_STEP)
    v_hat = v / (1.0 - ADAM_B2 ** ADAM_STEP)
    delta = -ADAM_LR * (m_hat / (jnp.sqrt(v_hat) + ADAM_EPS) + ADAM_WD * w)
    return delta, m, v


def _adamw(name, w, g, m, v):
    R, C = w.shape
    tr = _pick(R, max(8, (256 * 1024) // C // 8 * 8), 8)
    spec = pl.BlockSpec((tr, C), lambda i: (i, 0))

    def body(w_ref, g_ref, m_ref, v_ref, d_ref, mo_ref, vo_ref):
        d, m2, v2 = _adamw_math(w_ref[...], g_ref[...], m_ref[...], v_ref[...])
        d_ref[...] = d
        mo_ref[...] = m2
        vo_ref[...] = v2

    return pl.pallas_call(
        body, name=name, grid=(R // tr,), in_specs=[spec] * 4, out_specs=[spec] * 3,
        out_shape=[jax.ShapeDtypeStruct((R, C), F32)] * 3, compiler_params=_params(("parallel",)),
    )(w, g, m, v)


def _ada_update(c_act_t, dmod, w, m, v):
    R, C = w.shape
    B = dmod.shape[0]
    tr = _pick(R, 256, 8)
    tc = _pick(C, 1536, 128)
    spec = pl.BlockSpec((tr, tc), lambda i, j: (i, j))

    def body(ct_ref, dm_ref, w_ref, m_ref, v_ref, g_ref, d_ref, mo_ref, vo_ref):
        ct = ct_ref[...]
        dm = dm_ref[...]
        g = ct[:, 0:1] * dm[0:1, :]
        for b in range(1, B):
            g = g + ct[:, b:b + 1] * dm[b:b + 1, :]
        d, m2, v2 = _adamw_math(w_ref[...], g, m_ref[...], v_ref[...])
        g_ref[...] = g
        d_ref[...] = d
        mo_ref[...] = m2
        vo_ref[...] = v2

    return pl.pallas_call(
        body, name="ada_update", grid=(R // tr, C // tc),
        in_specs=[pl.BlockSpec((tr, B), lambda i, j: (i, 0)), pl.BlockSpec((B, tc), lambda i, j: (0, j)),
                  spec, spec, spec],
        out_specs=[spec] * 4, out_shape=[jax.ShapeDtypeStruct((R, C), F32)] * 4,
        compiler_params=_params(("parallel", "parallel")),
    )(c_act_t, dmod, w, m, v)


def _prep(c_all, lb_param):
    def body(c_ref, p_ref, ca_ref, cab_ref, lb_ref):
        cv = c_ref[...]
        ca = cv * _sig(cv)
        ca_ref[...] = ca
        cab_ref[...] = ca.astype(BF16)
        lb_ref[...] = _sig(p_ref[0:1, :] - p_ref[1:2, :])

    return pl.pallas_call(
        body, name="prep",
        out_shape=[jax.ShapeDtypeStruct(c_all.shape, F32), jax.ShapeDtypeStruct(c_all.shape, BF16),
                   jax.ShapeDtypeStruct((1, lb_param.shape[1]), F32)],
    )(c_all, lb_param)


def _small_reduce(parts):
    W = parts.shape[1]

    def body(p_ref, o_ref):
        acc = p_ref[0:1, :]
        for d in range(1, N_DEV):
            acc = acc + p_ref[d:d + 1, :]
        o_ref[...] = acc

    return pl.pallas_call(body, name="small_reduce", out_shape=jax.ShapeDtypeStruct((1, W), F32))(parts)


def _lb_grad(dlb, lb):
    def body(d_ref, lb_ref, o_ref):
        t = d_ref[...] * lb_ref[...] * (1.0 - lb_ref[...])
        o_ref[0:1, :] = t
        o_ref[1:2, :] = -t

    return pl.pallas_call(body, name="lb_grad", out_shape=jax.ShapeDtypeStruct((2, dlb.shape[1]), F32))(dlb, lb)


def _place():
    x, y, c = lax.axis_index("x"), lax.axis_index("y"), lax.axis_index("c")
    chips = [(1 - x, y), (x, 1 - y), (1 - x, 1 - y)]
    return x, y, c, chips


def _allgather_rows(name, v):
    m_per, n = v.shape

    def body(x_ref, out_ref, send_sems, recv_sems, local_sem):
        x, y, c, chips = _place()
        me, sibling = (x, y, c), (x, y, 1 - c)

        def rows(px, py, pc):
            return out_ref.at[pl.ds((4 * px + 2 * py + pc) * m_per, m_per), :]

        def copy(k, block, to, src=None):
            return pltpu.make_async_remote_copy(
                src_ref=rows(*block) if src is None else src, dst_ref=rows(*block),
                send_sem=send_sems.at[k], recv_sem=recv_sems.at[k], device_id=to, device_id_type=MESH)

        mine = pltpu.make_async_copy(x_ref, rows(*me), local_sem)
        mine.start()
        first = [copy(0, me, sibling, src=x_ref)]
        first += [copy(1 + j, me, (*chip, c), src=x_ref) for j, chip in enumerate(chips)]
        for cp in first:
            cp.start()
        passed = [copy(4 + j, (*chip, c), sibling) for j, chip in enumerate(chips)]
        for j, chip in enumerate(chips):
            copy(1 + j, (*chip, c), me).wait_recv()
            passed[j].start()
        copy(0, sibling, me).wait_recv()
        for j, chip in enumerate(chips):
            copy(4 + j, (*chip, 1 - c), me).wait_recv()
        for cp in first + passed:
            cp.wait_send()
        mine.wait()

    return pl.pallas_call(
        body, name=name, out_shape=jax.ShapeDtypeStruct((N_DEV * m_per, n), v.dtype),
        in_specs=[pl.BlockSpec(memory_space=pltpu.VMEM)], out_specs=pl.BlockSpec(memory_space=pltpu.VMEM),
        scratch_shapes=[pltpu.SemaphoreType.DMA((7,)), pltpu.SemaphoreType.DMA((7,)), pltpu.SemaphoreType.DMA],
    )(v)


def _cast_place(name, w, k_idx):
    R, C = w.shape
    tr = _pick(R, max(16, (512 * 1024) // C // 16 * 16), 16)

    def body(k_ref, w_ref, o_ref):
        o_ref[...] = w_ref[...].astype(BF16)

    return pl.pallas_call(
        body, name=name,
        grid_spec=pltpu.PrefetchScalarGridSpec(
            num_scalar_prefetch=1, grid=(R // tr,),
            in_specs=[pl.BlockSpec((tr, C), lambda i, k_ref: (i, 0))],
            out_specs=pl.BlockSpec((None, tr, C), lambda i, k_ref: (k_ref[0], i, 0))),
        out_shape=jax.ShapeDtypeStruct((N_CHIPS, R, C), BF16),
        compiler_params=_params(("parallel",)),
    )(k_idx, w)


def _chip_copies(src_of, dst_of, got_of, n, sems, receiving):
    x, y, c, chips = _place()
    k_me = 2 * x + y
    send_sems, recv_sems = sems
    out = []
    for a in range(n):
        for j, chip in enumerate(chips):
            k_chip = 2 * chip[0] + chip[1]
            if receiving:
                src = dst = got_of(a, k_chip, c)
                to = (x, y, c)
            else:
                src, dst, to = src_of(a, k_chip, k_me, c), dst_of(a, k_me, c), (*chip, c)
            out.append(pltpu.make_async_remote_copy(
                src_ref=src, dst_ref=dst, send_sem=send_sems.at[3 * a + j], recv_sem=recv_sems.at[3 * a + j],
                device_id=to, device_id_type=MESH))
    return out


class _ChipJob:
    def start(self, j_in, j_out, sems):
        for send in self.copies(j_in, j_out, sems, False):
            send.start()

    def finish(self, j_in, j_out, sems):
        for recv in self.copies(j_in, j_out, sems, True):
            recv.wait_recv()
        for send in self.copies(j_in, j_out, sems, False):
            send.wait_send()


class _GatherJob(_ChipJob):
    def __init__(self, bufs, part=0, n_parts=1):
        n = len(bufs)
        self.inputs, self.n_alias = list(bufs), n
        self.sem_shapes = [pltpu.SemaphoreType.DMA((3 * n,)), pltpu.SemaphoreType.DMA((3 * n,))]
        self.half = [b.shape[1] // 2 for b in bufs]
        self.part, self.n_parts = part, n_parts

    def copies(self, j_in, j_out, sems, receiving):
        def half(a, k, c):
            rows = self.half[a] // self.n_parts
            return j_out[a].at[k, pl.ds(c * self.half[a] + self.part * rows, rows)]

        return _chip_copies(lambda a, k_chip, k_me, c: half(a, k_me, c), half, half, len(self.half), sems, receiving)


class _ScatterJob(_ChipJob):
    def __init__(self, parts, slots):
        n = len(parts)
        self.n = n
        self.inputs, self.n_alias = list(parts) + list(slots), n
        self.sem_shapes = [pltpu.SemaphoreType.DMA((3 * n,)), pltpu.SemaphoreType.DMA((3 * n,))]

    def copies(self, j_in, j_out, sems, receiving):
        return _chip_copies(lambda a, k_chip, k_me, c: j_in[a].at[k_chip], lambda a, k_me, c: j_out[a].at[k_me],
                            lambda a, k_chip, c: j_out[a].at[k_chip], self.n, sems, receiving)


def _forward_halves(name, bufs):
    n = len(bufs)

    def body(*refs):
        out_refs = refs[n:2 * n]
        send_sems, recv_sems = refs[2 * n:]
        x, y, c, chips = _place()
        started, waits = [], []
        for a in range(n):
            rh = bufs[a].shape[1] // 2
            for j, chip in enumerate(chips):
                k_chip = 2 * chip[0] + chip[1]
                got = out_refs[a].at[k_chip, pl.ds(c * rh, rh)]
                cp = pltpu.make_async_remote_copy(src_ref=got, dst_ref=got, send_sem=send_sems.at[3 * a + j],
                                                  recv_sem=recv_sems.at[3 * a + j], device_id=(x, y, 1 - c),
                                                  device_id_type=MESH)
                cp.start()
                started.append(cp)
                other = out_refs[a].at[k_chip, pl.ds((1 - c) * rh, rh)]
                waits.append(pltpu.make_async_remote_copy(
                    src_ref=other, dst_ref=other, send_sem=send_sems.at[3 * a + j], recv_sem=recv_sems.at[3 * a + j],
                    device_id=(x, y, c), device_id_type=MESH))
        for cp in waits:
            cp.wait_recv()
        for cp in started:
            cp.wait_send()

    return pl.pallas_call(
        body, name=name, out_shape=[jax.ShapeDtypeStruct(b.shape, b.dtype) for b in bufs],
        in_specs=[_HBM] * n, out_specs=[_HBM] * n, input_output_aliases={a: a for a in range(n)},
        scratch_shapes=[pltpu.SemaphoreType.DMA((3 * n,)), pltpu.SemaphoreType.DMA((3 * n,))],
    )(*bufs)


def _proj_gather(h, buf, order, job):
    S, D = h.shape
    nb = buf.shape[2]
    tm = _pick(S, 1024, 16)
    tn = _pick(nb, 1408, 128)
    per = nb // tn
    nj, ni = N_CHIPS * per, S // tm
    rh = D // 2

    def body(order_ref, h_ref, buf_in, proj_ref, buf_ref, wbuf, tile_sems, ici_send, ici_recv, d2d_send, d2d_recv):
        j, i = pl.program_id(0), pl.program_id(1)
        x, y, c, chips = _place()
        k_me = 2 * x + y

        def half(k, hc):
            return buf_ref.at[k, pl.ds(hc * rh, rh)]

        def ici(q, receiving):
            k_chip = 2 * chips[q][0] + chips[q][1]
            src, to = (half(k_chip, c), (x, y, c)) if receiving else (half(k_me, c), (*chips[q], c))
            return pltpu.make_async_remote_copy(src_ref=src, dst_ref=src, send_sem=ici_send.at[q],
                                                recv_sem=ici_recv.at[q], device_id=to, device_id_type=MESH)

        def d2d(q, receiving):
            k_chip = 2 * chips[q][0] + chips[q][1]
            src, to = (half(k_chip, 1 - c), (x, y, c)) if receiving else (half(k_chip, c), (x, y, 1 - c))
            return pltpu.make_async_remote_copy(src_ref=src, dst_ref=src, send_sem=d2d_send.at[q],
                                                recv_sem=d2d_recv.at[q], device_id=to, device_id_type=MESH)

        def tile_copy(t):
            col = pl.multiple_of((t % per) * tn, 128)
            return pltpu.make_async_copy(buf_ref.at[order_ref[t // per], :, pl.ds(col, tn)], wbuf.at[t % 2],
                                         tile_sems.at[t % 2])

        @pl.when(jnp.logical_and(j == 0, i == 0))
        def _():
            ici(0, False).start()
            ici(1, False).start()
            tile_copy(0).start()

        @pl.when(i == 0)
        def _():
            tile_copy(j).wait()
            for q in range(3):
                @pl.when(j + 1 == (q + 1) * per)
                def _():
                    ici(q, True).wait_recv()
                    d2d(q, False).start()
                    if q == 0:
                        ici(0, False).wait_send()
                        ici(1, False).wait_send()
                        ici(2, False).start()
                    d2d(q, True).wait_recv()

            @pl.when(j + 1 < nj)
            def _():
                tile_copy(j + 1).start()

        proj_ref[...] = jnp.dot(h_ref[...], wbuf[j % 2], preferred_element_type=F32)

        @pl.when(jnp.logical_and(j == nj - 1, i == ni - 1))
        def _():
            ici(2, False).wait_send()
            for q in range(3):
                d2d(q, False).wait_send()

    grid = (nj, ni)
    in_specs = [pl.BlockSpec((tm, D), lambda j, i, o: (i, 0)), _HBM]
    args = [h, buf]
    out_specs = [pl.BlockSpec((tm, tn), lambda j, i, o: (i, o[j // per] * per + j % per)), _HBM]
    out_shape = [jax.ShapeDtypeStruct((S, N_CHIPS * nb), F32), jax.ShapeDtypeStruct(buf.shape, buf.dtype)]
    scratch = [pltpu.VMEM((2, D, tn), BF16), pltpu.SemaphoreType.DMA((2,))] + [pltpu.SemaphoreType.DMA((3,))] * 4
    aliases, wrap = _host(job, grid, in_specs, args, out_specs, out_shape, scratch, n_prefetch=1)
    aliases[2] = 1
    return pl.pallas_call(
        wrap(body), name="proj",
        grid_spec=pltpu.PrefetchScalarGridSpec(num_scalar_prefetch=1, grid=grid, in_specs=in_specs, out_specs=out_specs,
                                               scratch_shapes=scratch),
        out_shape=out_shape, input_output_aliases=aliases, compiler_params=_params(("arbitrary", "arbitrary")),
    )(order, *args)


def _swap_halves(name, grads):
    n = len(grads)

    def body(*refs):
        in_refs, out_refs = refs[:n], refs[n:2 * n]
        send_sems, recv_sems = refs[2 * n:]
        x, y, c, _ = _place()
        cps = []
        for a in range(n):
            rh = grads[a].shape[1] // 2
            cp = pltpu.make_async_remote_copy(
                src_ref=in_refs[a].at[:, pl.ds((1 - c) * rh, rh)], dst_ref=out_refs[a],
                send_sem=send_sems.at[a], recv_sem=recv_sems.at[a], device_id=(x, y, 1 - c), device_id_type=MESH)
            cp.start()
            cps.append(cp)
        for cp in cps:
            cp.wait()

    return pl.pallas_call(
        body, name=name,
        out_shape=[jax.ShapeDtypeStruct((N_CHIPS, g.shape[1] // 2, g.shape[2]), g.dtype) for g in grads],
        in_specs=[_HBM] * n, out_specs=[_HBM] * n,
        scratch_shapes=[pltpu.SemaphoreType.DMA((n,)), pltpu.SemaphoreType.DMA((n,))],
    )(*grads)


def _pair_sum(name, g, q, c_idx):
    _, R, C = g.shape
    rh = R // 2
    tr = _pick(rh, max(16, (512 * 1024) // C // 16 * 16), 16)
    nh = rh // tr

    def body(c_ref, g_ref, q_ref, o_ref, o2_ref):
        s = (g_ref[...].astype(F32) + q_ref[...].astype(F32)).astype(BF16)
        o_ref[...] = s
        o2_ref[...] = s

    out_spec = pl.BlockSpec((None, tr, C), lambda k, i, c_ref: (k, i, 0))
    return pl.pallas_call(
        body, name=name,
        grid_spec=pltpu.PrefetchScalarGridSpec(
            num_scalar_prefetch=1, grid=(N_CHIPS, nh),
            in_specs=[pl.BlockSpec((None, tr, C), lambda k, i, c_ref: (k, c_ref[0] * nh + i, 0)),
                      pl.BlockSpec((None, tr, C), lambda k, i, c_ref: (k, i, 0))],
            out_specs=[out_spec, out_spec]),
        out_shape=[jax.ShapeDtypeStruct((N_CHIPS, rh, C), BF16)] * 2,
        compiler_params=_params(("parallel", "parallel")),
    )(c_idx, g, q)


def _sum_chips(name, r, c_idx):
    _, rh, C = r.shape
    tr = _pick(rh, max(16, (256 * 1024) // C // 16 * 16), 16)

    def body(c_ref, r_ref, o_ref):
        acc = r_ref[0].astype(F32)
        for k in range(1, N_CHIPS):
            acc = acc + r_ref[k].astype(F32)
        o_ref[...] = acc

    return pl.pallas_call(
        body, name=name,
        grid_spec=pltpu.PrefetchScalarGridSpec(
            num_scalar_prefetch=1, grid=(rh // tr,),
            in_specs=[pl.BlockSpec((N_CHIPS, tr, C), lambda i, c_ref: (0, i, 0))],
            out_specs=pl.BlockSpec((None, tr, C), lambda i, c_ref: (c_ref[0], i, 0))),
        out_shape=jax.ShapeDtypeStruct((2, rh, C), F32), compiler_params=_params(("parallel",)),
    )(c_idx, r)


def _share_halves(bufs):
    n = len(bufs)

    def body(*refs):
        out_refs = refs[n:2 * n]
        send_sems, recv_sems = refs[2 * n:]
        x, y, c, _ = _place()
        cps = []
        for a in range(n):
            cp = pltpu.make_async_remote_copy(
                src_ref=out_refs[a].at[c], dst_ref=out_refs[a].at[c], send_sem=send_sems.at[a],
                recv_sem=recv_sems.at[a], device_id=(x, y, 1 - c), device_id_type=MESH)
            cp.start()
            cps.append(cp)
        for a, cp in enumerate(cps):
            got = out_refs[a].at[1 - c]
            pltpu.make_async_remote_copy(src_ref=got, dst_ref=got, send_sem=send_sems.at[a], recv_sem=recv_sems.at[a],
                                         device_id=(x, y, c), device_id_type=MESH).wait_recv()
        for cp in cps:
            cp.wait_send()

    return pl.pallas_call(
        body, name="share_halves",
        out_shape=[jax.ShapeDtypeStruct(b.shape, b.dtype) for b in bufs],
        in_specs=[_HBM] * n, out_specs=[_HBM] * n, input_output_aliases={a: a for a in range(n)},
        scratch_shapes=[pltpu.SemaphoreType.DMA((n,)), pltpu.SemaphoreType.DMA((n,))],
    )(*bufs)


_BIG = ("w_in", "w_conv_out", "w_hgrn_out", "w_o", "w_ffn_gate", "w_ffn_up", "w_ffn_down")
_ROW_SHARDED = ("w_o", "w_ffn_down")
_WEIGHTS = ("w_ada", "b_ada", "norm_mix_g", "w_in", "conv_w", "lb_param", "gnorm_g", "w_conv_out", "w_hgrn_out",
            "w_o", "norm_ffn_g", "w_ffn_gate", "w_ffn_up", "w_ffn_down", "norm_final_g")


def _pack_rows(pieces):
    flat = jnp.concatenate([p.reshape(-1) for p in pieces])
    pad = (-flat.shape[0]) % 1024
    return jnp.pad(flat, (0, pad)).reshape(8, -1)


def kernel(x, c, w_ada, b_ada, norm_mix_g, w_in, conv_w, lb_param, gnorm_g, w_conv_out, w_hgrn_out, w_o, norm_ffn_g, w_ffn_gate, w_ffn_up, w_ffn_down, norm_final_g, loss_target, m_w_ada, m_b_ada, m_norm_mix_g, m_w_in, m_conv_w, m_lb_param, m_gnorm_g, m_w_conv_out, m_w_hgrn_out, m_w_o, m_norm_ffn_g, m_w_ffn_gate, m_w_ffn_up, m_w_ffn_down, m_norm_final_g, v_w_ada, v_b_ada, v_norm_mix_g, v_w_in, v_conv_w, v_lb_param, v_gnorm_g, v_w_conv_out, v_w_hgrn_out, v_w_o, v_norm_ffn_g, v_w_ffn_gate, v_w_ffn_up, v_w_ffn_down, v_norm_final_g):
    w = dict(w_ada=w_ada, b_ada=b_ada, norm_mix_g=norm_mix_g, w_in=w_in, conv_w=conv_w, lb_param=lb_param,
             gnorm_g=gnorm_g, w_conv_out=w_conv_out, w_hgrn_out=w_hgrn_out, w_o=w_o, norm_ffn_g=norm_ffn_g,
             w_ffn_gate=w_ffn_gate, w_ffn_up=w_ffn_up, w_ffn_down=w_ffn_down, norm_final_g=norm_final_g)
    m = dict(w_ada=m_w_ada, b_ada=m_b_ada, norm_mix_g=m_norm_mix_g, w_in=m_w_in, conv_w=m_conv_w, lb_param=m_lb_param,
             gnorm_g=m_gnorm_g, w_conv_out=m_w_conv_out, w_hgrn_out=m_w_hgrn_out, w_o=m_w_o, norm_ffn_g=m_norm_ffn_g,
             w_ffn_gate=m_w_ffn_gate, w_ffn_up=m_w_ffn_up, w_ffn_down=m_w_ffn_down, norm_final_g=m_norm_final_g)
    v = dict(w_ada=v_w_ada, b_ada=v_b_ada, norm_mix_g=v_norm_mix_g, w_in=v_w_in, conv_w=v_conv_w, lb_param=v_lb_param,
             gnorm_g=v_gnorm_g, w_conv_out=v_w_conv_out, w_hgrn_out=v_w_hgrn_out, w_o=v_w_o, norm_ffn_g=v_norm_ffn_g,
             w_ffn_gate=v_w_ffn_gate, w_ffn_up=v_w_ffn_up, w_ffn_down=v_w_ffn_down, norm_final_g=v_norm_final_g)
    two_d = lambda a: a.reshape((-1, a.shape[-1])) if a.ndim != 2 else a
    shapes = {k: a.shape for k, a in w.items()}
    w, m, v = ({k: two_d(a) for k, a in t.items()} for t in (w, m, v))
    w["lb_param"], m["lb_param"], v["lb_param"] = lb_param, m_lb_param, v_lb_param
    S, D = x.shape[1], x.shape[2]
    d_conv = D // 2
    ix, iy, ic = lax.axis_index("x"), lax.axis_index("y"), lax.axis_index("c")
    k_me = 2 * ix + iy
    dev = 2 * k_me + ic
    cw_shard = w["conv_w"].shape[1]
    ada_cols = w["w_ada"].shape[1]

    got = _allgather_rows("gather_cond", _pack_rows([c, w["conv_w"]])).reshape(N_DEV, -1)
    c_all = got[:, :D]
    conv_full = got[::2, D:D + 3 * cw_shard].reshape(N_CHIPS, 3, cw_shard).transpose(1, 0, 2).reshape(3, -1)
    c_act, c_act_b, lb = _prep(c_all, lb_param)
    b_cols = lax.dynamic_slice(w["b_ada"], (0, k_me * ada_cols), (1, ada_cols))
    mod_cols, = _matmul("ada_fwd", [(c_act_b, 'n', w["w_ada"].astype(BF16), 'n')], [0], N_DEV, ada_cols, D,
                        N_DEV, _pick(ada_cols, 1536, 128), D, [(b_cols, 'row', 0)], [(F32, None)],
                        lambda accs, ex: [accs[0] + ex[0]])
    mod_all = _allgather_rows("gather_mod", mod_cols).reshape(N_CHIPS, 2, N_DEV, ada_cols)[:, 0]
    mod_all = mod_all.transpose(1, 0, 2).reshape(N_DEV, -1)
    mod = lax.dynamic_slice(mod_all, (dev, 0), (1, mod_all.shape[1]))
    k_idx = jnp.reshape(k_me, (1,)).astype(jnp.int32)
    c_idx = jnp.reshape(ic, (1,)).astype(jnp.int32)
    bufs = {k: _cast_place("cast_" + k, w[k], k_idx) for k in _BIG}
    order = jnp.stack([k_me, 2 * (1 - ix) + iy, 2 * ix + (1 - iy), 2 * (1 - ix) + (1 - iy)]).astype(jnp.int32)

    loss, dx, small, arrived = _local_step(
        x[0], loss_target[0], mod, w["norm_mix_g"], lb, w["gnorm_g"], w["norm_ffn_g"], w["norm_final_g"], conv_full,
        bufs, c_idx, order)

    pieces = [small["dmod"], small["dg_mix"], small["dg_ffn"], small["dg_final"], small["dlb"], small["dgn"],
              small["dconv_w"], loss]
    sizes = [p.size for p in pieces]
    parts = _allgather_rows("gather_small", _pack_rows(pieces)).reshape(N_DEV, -1)
    total = _small_reduce(parts)
    offs = [0]
    for s in sizes:
        offs.append(offs[-1] + s)
    tot = [total[:, offs[i]:offs[i + 1]] for i in range(len(sizes))]
    dmod_all = parts[:, :sizes[0]]
    grads = {
        "b_ada": tot[0], "norm_mix_g": tot[1], "norm_ffn_g": tot[2], "norm_final_g": tot[3],
        "lb_param": _lb_grad(tot[4], lb), "gnorm_g": tot[5],
        "conv_w": lax.dynamic_slice(tot[6].reshape(3, -1), (0, k_me * cw_shard), (3, cw_shard)),
    }
    loss_out = tot[7][0, 0]

    halves = [_sum_chips("sum_chips_" + k, arrived[k], c_idx) for k in _BIG]
    whole = _share_halves(halves)
    for k, g in zip(_BIG, whole):
        grads[k] = g.reshape(-1, g.shape[-1])

    delta, new_m, new_v = {}, {}, {}
    dmod_cols = lax.dynamic_slice(dmod_all, (0, k_me * ada_cols), (N_DEV, ada_cols))
    grads["w_ada"], delta["w_ada"], new_m["w_ada"], new_v["w_ada"] = _ada_update(
        c_act.T, dmod_cols, w["w_ada"], m["w_ada"], v["w_ada"])
    for k in _WEIGHTS:
        if k != "w_ada":
            delta[k], new_m[k], new_v[k] = _adamw("adamw_" + k, w[k], grads[k], m[k], v[k])
    outs = [loss_out, dx.reshape(x.shape)]
    for t in (grads, delta, new_m, new_v):
        outs += [t[k].reshape(shapes[k]) for k in _WEIGHTS]
    return tuple(outs)
```

```python
import functools

import jax
import jax.numpy as jnp
from jax import lax
from jax.experimental import pallas as pl
from jax.experimental.pallas import tpu as pltpu

F32 = jnp.float32
BF16 = jnp.bfloat16
MESH = pl.DeviceIdType.MESH

EPS = 1e-6
HEAD = 128
BLK = 16
N_CHIPS = 4
N_DEV = 8
VMEM_LIMIT_BYTES = 56 * 1024 * 1024

ADAM_LR = 0.001
ADAM_B1 = 0.9
ADAM_B2 = 0.999
ADAM_EPS = 1e-08
ADAM_WD = 0.01
ADAM_STEP = 10


def _pick(dim, pref, mult):
    if dim <= pref:
        return dim
    t = (pref // mult) * mult
    while t >= mult:
        if dim % t == 0:
            return t
        t -= mult
    return dim


def _sig(x):
    return 1.0 / (1.0 + jnp.exp(-x))


def _params(sem):
    return pltpu.CompilerParams(dimension_semantics=sem, vmem_limit_bytes=VMEM_LIMIT_BYTES)


def _gj(j, i, k):
    return j


def _gk(j, i, k):
    return k


def _wspec(arr, rt, ct, r_of, c_of):
    if arr.ndim == 2:
        return pl.BlockSpec((rt, ct), lambda j, i, k: (r_of(j, i, k), c_of(j, i, k)))
    per = arr.shape[2] // ct
    assert arr.shape[2] % ct == 0
    return pl.BlockSpec((None, rt, ct),
                        lambda j, i, k: (c_of(j, i, k) // per, r_of(j, i, k), c_of(j, i, k) % per))


_HBM = pl.BlockSpec(memory_space=pltpu.HBM)


def _host(job, grid, in_specs, args, out_specs, out_shape, scratch, n_prefetch=0):
    if job is None:
        return {}, (lambda body: body)
    n_in, n_out, n_scr = len(args), len(out_shape), len(scratch)
    n_job, n_alias = len(job.inputs), job.n_alias
    in_specs += [_HBM] * n_job
    args += job.inputs
    out_specs += [_HBM] * n_alias
    out_shape += [jax.ShapeDtypeStruct(a.shape, a.dtype) for a in job.inputs[n_job - n_alias:]]
    scratch += job.sem_shapes
    aliases = {n_prefetch + n_in + n_job - n_alias + t: n_out + t for t in range(n_alias)}

    def wrap(body):
        def hosted(*refs):
            pre, refs = refs[:n_prefetch], refs[n_prefetch:]
            ins, refs = refs[:n_in], refs[n_in:]
            j_in, refs = refs[:n_job], refs[n_job:]
            outs, refs = refs[:n_out], refs[n_out:]
            j_out, refs = refs[:n_alias], refs[n_alias:]
            scr, sems = refs[:n_scr], refs[n_scr:]
            first = functools.reduce(jnp.logical_and, [pl.program_id(d) == 0 for d in range(len(grid))])
            last = functools.reduce(jnp.logical_and, [pl.program_id(d) == grid[d] - 1 for d in range(len(grid))])

            @pl.when(first)
            def _():
                job.start(j_in, j_out, sems)

            body(*pre, *ins, *outs, *scr)

            @pl.when(last)
            def _():
                job.finish(j_in, j_out, sems)

        return hosted

    return aliases, wrap


def _matmul(name, pairs, acc_of, M, N, K, tm, tn, tk, extras, outs, epilogue, job=None, rows_outer=False):
    assert M % tm == 0 and N % tn == 0 and K % tk == 0, (name, M, N, K, tm, tn, tk)
    nj, ni, nk = N // tn, M // tm, K // tk
    n_pairs, n_extra, n_out = len(pairs), len(extras), len(outs)
    n_acc = max(acc_of) + 1
    in_specs, args, dims = [], [], []
    lhs_of, seen = [], {}
    for a, am, b, bm in pairs:
        if (id(a), am) not in seen:
            seen[id(a), am] = len(args)
            args.append(a)
            if am == 'n':
                in_specs.append(pl.BlockSpec((tm, tk), lambda j, i, k: (i, k)))
            else:
                in_specs.append(pl.BlockSpec((tk, tm), lambda j, i, k: (k, i)))
        lhs_of.append(seen[id(a), am])
    n_lhs = len(args)
    for a, am, b, bm in pairs:
        if bm == 'n':
            in_specs.append(_wspec(b, tk, tn, _gk, _gj))
        else:
            in_specs.append(_wspec(b, tn, tk, _gj, _gk))
        dims.append((((1 if am == 'n' else 0,), (0 if bm == 'n' else 1,)), ((), ())))
        args.append(b)
    for e, kind, off in extras:
        assert off % tn == 0, (name, off, tn)
        o = off // tn
        if kind == 'tile':
            in_specs.append(pl.BlockSpec((tm, tn), lambda j, i, k, o=o: (i, j + o)))
        else:
            in_specs.append(pl.BlockSpec((1, tn), lambda j, i, k, o=o: (0, j + o)))
        args.append(e)
    out_shape, out_specs = [], []
    for dt, nb in outs:
        if nb is None:
            out_shape.append(jax.ShapeDtypeStruct((M, N), dt))
            out_specs.append(pl.BlockSpec((tm, tn), lambda j, i, k: (i, j)))
        else:
            assert nb % tn == 0 and N % nb == 0
            per = nb // tn
            out_shape.append(jax.ShapeDtypeStruct((N // nb, M, nb), dt))
            out_specs.append(pl.BlockSpec((None, tm, tn), lambda j, i, k, per=per: (j // per, i, j % per)))

    def body(*refs):
        n_ab = n_lhs + n_pairs
        e_refs = refs[n_ab:n_ab + n_extra]
        o_refs = refs[n_ab + n_extra:n_ab + n_extra + n_out]
        acc_refs = refs[n_ab + n_extra + n_out:]
        sums = [None] * n_acc
        for p in range(n_pairs):
            prod = lax.dot_general(refs[lhs_of[p]][...], refs[n_lhs + p][...], dims[p], preferred_element_type=F32)
            a = acc_of[p]
            sums[a] = prod if sums[a] is None else sums[a] + prod

        def finish(accs):
            res = epilogue(accs, [e[...] for e in e_refs])
            for o_ref, r in zip(o_refs, res):
                o_ref[...] = r.astype(o_ref.dtype)

        if nk == 1:
            finish(sums)
        else:
            k = pl.program_id(2)

            @pl.when(k == 0)
            def _():
                for a in range(n_acc):
                    acc_refs[a][...] = sums[a]

            @pl.when(k > 0)
            def _():
                for a in range(n_acc):
                    acc_refs[a][...] += sums[a]

            @pl.when(k == nk - 1)
            def _():
                finish([acc_refs[a][...] for a in range(n_acc)])

    scratch = [pltpu.VMEM((tm, tn), F32) for _ in range(n_acc)] if nk > 1 else []
    grid = (nj, ni, nk)
    if rows_outer:
        grid = (ni, nj, nk)
        swap = lambda spec: pl.BlockSpec(spec.block_shape, lambda i, j, k, f=spec.index_map: f(j, i, k))
        in_specs, out_specs = [swap(s) for s in in_specs], [swap(s) for s in out_specs]
    aliases, wrap = _host(job, grid, in_specs, args, out_specs, out_shape, scratch)
    sem = ("parallel", "parallel", "arbitrary") if job is None else ("arbitrary",) * 3
    return pl.pallas_call(
        wrap(body), name=name, grid=grid, in_specs=in_specs, out_specs=out_specs, out_shape=out_shape,
        scratch_shapes=scratch, input_output_aliases=aliases, compiler_params=_params(sem),
    )(*args)


def _row_spec(tr, d):
    return pl.BlockSpec((tr, d), lambda i: (i, 0))


def _vec_spec(d):
    return pl.BlockSpec((1, d), lambda i: (0, 0))


def _norm_mod(name, x, g, sc, sh, job=None):
    S, D = x.shape
    tr = _pick(S, 256, 8)

    def body(x_ref, g_ref, sc_ref, sh_ref, h_ref):
        xv = x_ref[...]
        r = lax.rsqrt(jnp.mean(xv * xv, axis=-1, keepdims=True) + EPS)
        h_ref[...] = ((xv * r) * g_ref[...] * (1.0 + sc_ref[...]) + sh_ref[...]).astype(BF16)

    grid = (S // tr,)
    in_specs = [_row_spec(tr, D), _vec_spec(D), _vec_spec(D), _vec_spec(D)]
    args = [x, g, sc, sh]
    out_specs, out_shape, scratch = [_row_spec(tr, D)], [jax.ShapeDtypeStruct((S, D), BF16)], []
    aliases, wrap = _host(job, grid, in_specs, args, out_specs, out_shape, scratch)
    res = pl.pallas_call(
        wrap(body), name=name, grid=grid, in_specs=in_specs, out_specs=out_specs, out_shape=out_shape,
        scratch_shapes=scratch, input_output_aliases=aliases,
        compiler_params=_params(("parallel" if job is None else "arbitrary",)),
    )(*args)
    return res[0] if job is None else res


def _loss_head(x2, target, g, ff, gt):
    S, D = x2.shape
    tr = _pick(S, 256, 8)

    def body(x_ref, t_ref, g_ref, ff_ref, gt_ref, dx_ref, dffb_ref, loss_ref, dgt_ref, dg_ref):
        i = pl.program_id(0)

        @pl.when(i == 0)
        def _():
            loss_ref[...] = jnp.zeros_like(loss_ref)
            dgt_ref[...] = jnp.zeros_like(dgt_ref)
            dg_ref[...] = jnp.zeros_like(dg_ref)

        xv = x_ref[...]
        gv = g_ref[...]
        r = lax.rsqrt(jnp.mean(xv * xv, axis=-1, keepdims=True) + EPS)
        xh = xv * r
        err = xh * gv - t_ref[...]
        loss_ref[...] += 0.5 * jnp.sum(jnp.mean(err * err, axis=-1, keepdims=True))
        dy = err * (1.0 / D)
        dg_ref[...] += jnp.sum(dy * xh, axis=0, keepdims=True)
        dxh = dy * gv
        dx = r * (dxh - xh * jnp.mean(dxh * xh, axis=-1, keepdims=True))
        dx_ref[...] = dx
        dffb_ref[...] = (dx * gt_ref[...]).astype(BF16)
        dgt_ref[...] += jnp.sum(dx * ff_ref[...], axis=0, keepdims=True)

    return pl.pallas_call(
        body, name="loss_head", grid=(S // tr,),
        in_specs=[_row_spec(tr, D), _row_spec(tr, D), _vec_spec(D), _row_spec(tr, D), _vec_spec(D)],
        out_specs=[_row_spec(tr, D), _row_spec(tr, D), pl.BlockSpec((1, 128), lambda i: (0, 0)),
                   _vec_spec(D), _vec_spec(D)],
        out_shape=[jax.ShapeDtypeStruct((S, D), F32), jax.ShapeDtypeStruct((S, D), BF16),
                   jax.ShapeDtypeStruct((1, 128), F32), jax.ShapeDtypeStruct((1, D), F32),
                   jax.ShapeDtypeStruct((1, D), F32)],
        compiler_params=_params(("arbitrary",)),
    )(x2, target, g, ff, gt)


def _norm_mod_bwd(name, dh, x, g, sc, dres, gated=None, job=None):
    S, D = x.shape
    tr = _pick(S, 256, 8)
    n = S // tr
    with_gate = gated is not None

    def body(*refs):
        if with_gate:
            dh_ref, x_ref, g_ref, sc_ref, dres_ref, mo_ref, gt_ref, dx_ref, dsh_ref, dsc_ref, dg_ref, dmob_ref, dgt_ref = refs
        else:
            dh_ref, x_ref, g_ref, sc_ref, dres_ref, dx_ref, dsh_ref, dsc_ref, dg_ref = refs
        i = pl.program_id(0)

        @pl.when(i == 0)
        def _():
            dsh_ref[...] = jnp.zeros_like(dsh_ref)
            dsc_ref[...] = jnp.zeros_like(dsc_ref)
            if with_gate:
                dgt_ref[...] = jnp.zeros_like(dgt_ref)

        xv = x_ref[...]
        dhv = dh_ref[...]
        gv = g_ref[...]
        scale = 1.0 + sc_ref[...]
        r = lax.rsqrt(jnp.mean(xv * xv, axis=-1, keepdims=True) + EPS)
        xh = xv * r
        dsh_ref[...] += jnp.sum(dhv, axis=0, keepdims=True)
        dsc_ref[...] += jnp.sum(dhv * xh, axis=0, keepdims=True)
        dxh = dhv * (gv * scale)
        dx = dres_ref[...] + r * (dxh - xh * jnp.mean(dxh * xh, axis=-1, keepdims=True))
        dx_ref[...] = dx
        if with_gate:
            dmob_ref[...] = (dx * gt_ref[...]).astype(BF16)
            dgt_ref[...] += jnp.sum(dx * mo_ref[...], axis=0, keepdims=True)

        @pl.when(i == n - 1)
        def _():
            t = dsc_ref[...]
            dg_ref[...] = t * scale
            dsc_ref[...] = t * gv

    in_specs = [_row_spec(tr, D), _row_spec(tr, D), _vec_spec(D), _vec_spec(D), _row_spec(tr, D)]
    args = [dh, x, g, sc, dres]
    out_specs = [_row_spec(tr, D), _vec_spec(D), _vec_spec(D), _vec_spec(D)]
    out_shape = [jax.ShapeDtypeStruct((S, D), F32)] + [jax.ShapeDtypeStruct((1, D), F32)] * 3
    if with_gate:
        in_specs += [_row_spec(tr, D), _vec_spec(D)]
        args += list(gated)
        out_specs += [_row_spec(tr, D), _vec_spec(D)]
        out_shape += [jax.ShapeDtypeStruct((S, D), BF16), jax.ShapeDtypeStruct((1, D), F32)]
    scratch = []
    aliases, wrap = _host(job, (n,), in_specs, args, out_specs, out_shape, scratch)
    return pl.pallas_call(
        wrap(body), name=name, grid=(n,), in_specs=in_specs, out_specs=out_specs, out_shape=out_shape,
        scratch_shapes=scratch, input_output_aliases=aliases, compiler_params=_params(("arbitrary",)),
    )(*args)


CONV_ROWS = 256


def _col_spec(S, off_cols):
    o = off_cols // HEAD
    return pl.BlockSpec((S, HEAD), lambda c, o=o: (0, c + o))


def _conv_taps(u, u_prev, rid):
    s1 = jnp.where(rid >= 1, pltpu.roll(u, 1, 0), pltpu.roll(u_prev, 1, 0))
    s2 = jnp.where(rid >= 2, pltpu.roll(u, 2, 0), pltpu.roll(u_prev, 2, 0))
    return s1, s2


def _conv_fwd(proj, conv_w, d_conv, job=None):
    S = proj.shape[0]
    R = min(CONV_ROWS, S)
    nr = S // R

    def body(ab_ref, ac_ref, ax_ref, w_ref, ya_ref):
        w0, w1, w2 = w_ref[0:1, :], w_ref[1:2, :], w_ref[2:3, :]
        rid = lax.broadcasted_iota(jnp.int32, (R, HEAD), 0)

        def step(c, carry):
            rows = pl.ds(pl.multiple_of(c * R, R), R)
            prev = pl.ds(pl.multiple_of(jnp.maximum(c - 1, 0) * R, R), R)
            u = ac_ref[rows, :] * ax_ref[rows, :]
            u_prev = jnp.where(c > 0, ac_ref[prev, :] * ax_ref[prev, :], 0.0)
            s1, s2 = _conv_taps(u, u_prev, rid)
            y = w0 * s2 + w1 * s1 + w2 * u
            ya_ref[rows, :] = (ab_ref[rows, :] * y).astype(BF16)
            return carry

        lax.fori_loop(0, nr, step, 0)

    grid = (d_conv // HEAD,)
    in_specs = [_col_spec(S, 0), _col_spec(S, d_conv), _col_spec(S, 2 * d_conv), pl.BlockSpec((3, HEAD), lambda c: (0, c))]
    args = [proj, proj, proj, conv_w]
    out_specs, out_shape = [pl.BlockSpec((S, HEAD), lambda c: (0, c))], [jax.ShapeDtypeStruct((S, d_conv), BF16)]
    scratch = []
    aliases, wrap = _host(job, grid, in_specs, args, out_specs, out_shape, scratch)
    res = pl.pallas_call(
        wrap(body), name="conv_fwd", grid=grid, in_specs=in_specs, out_specs=out_specs, out_shape=out_shape,
        scratch_shapes=scratch, input_output_aliases=aliases,
        compiler_params=_params(("parallel" if job is None else "arbitrary",)),
    )(*args)
    return res[0] if job is None else res


def _conv_bwd(dya, proj, conv_w, d_conv, job=None):
    S = proj.shape[0]
    R = min(CONV_ROWS, S)
    nr = S // R

    def body(dya_ref, ab_ref, ac_ref, ax_ref, w_ref, dab_ref, dac_ref, dax_ref, dw_ref):
        w0, w1, w2 = w_ref[0:1, :], w_ref[1:2, :], w_ref[2:3, :]
        rid = lax.broadcasted_iota(jnp.int32, (R, HEAD), 0)

        def step(c, carry):
            dw0, dw1, dw2 = carry
            rows = pl.ds(pl.multiple_of(c * R, R), R)
            prev = pl.ds(pl.multiple_of(jnp.maximum(c - 1, 0) * R, R), R)
            nxt = pl.ds(pl.multiple_of(jnp.minimum(c + 1, nr - 1) * R, R), R)
            a_c, a_x, a_b = ac_ref[rows, :], ax_ref[rows, :], ab_ref[rows, :]
            u = a_c * a_x
            u_prev = jnp.where(c > 0, ac_ref[prev, :] * ax_ref[prev, :], 0.0)
            s1, s2 = _conv_taps(u, u_prev, rid)
            y = w0 * s2 + w1 * s1 + w2 * u
            dy = dya_ref[rows, :]
            dab_ref[rows, :] = (dy * y).astype(BF16)
            dyc = dy * a_b
            dyc_next = jnp.where(c < nr - 1, dya_ref[nxt, :] * ab_ref[nxt, :], 0.0)
            t1 = jnp.where(rid < R - 1, pltpu.roll(dyc, R - 1, 0), pltpu.roll(dyc_next, R - 1, 0))
            t2 = jnp.where(rid < R - 2, pltpu.roll(dyc, R - 2, 0), pltpu.roll(dyc_next, R - 2, 0))
            du = w2 * dyc + w1 * t1 + w0 * t2
            dac_ref[rows, :] = (du * a_x).astype(BF16)
            dax_ref[rows, :] = (du * a_c).astype(BF16)
            dw0 = dw0 + jnp.sum(dyc * s2, axis=0, keepdims=True)
            dw1 = dw1 + jnp.sum(dyc * s1, axis=0, keepdims=True)
            dw2 = dw2 + jnp.sum(dyc * u, axis=0, keepdims=True)
            return dw0, dw1, dw2

        z = jnp.zeros((1, HEAD), F32)
        dw0, dw1, dw2 = lax.fori_loop(0, nr, step, (z, z, z))
        dw_ref[0:1, :] = dw0
        dw_ref[1:2, :] = dw1
        dw_ref[2:3, :] = dw2

    col = pl.BlockSpec((S, HEAD), lambda c: (0, c))
    grid = (d_conv // HEAD,)
    in_specs = [col, _col_spec(S, 0), _col_spec(S, d_conv), _col_spec(S, 2 * d_conv),
                pl.BlockSpec((3, HEAD), lambda c: (0, c))]
    args = [dya, proj, proj, proj, conv_w]
    out_specs = [col, col, col, pl.BlockSpec((3, HEAD), lambda c: (0, c))]
    out_shape = [jax.ShapeDtypeStruct((S, d_conv), BF16)] * 3 + [jax.ShapeDtypeStruct((3, d_conv), F32)]
    scratch = []
    aliases, wrap = _host(job, grid, in_specs, args, out_specs, out_shape, scratch)
    return pl.pallas_call(
        wrap(body), name="conv_bwd", grid=grid, in_specs=in_specs, out_specs=out_specs, out_shape=out_shape,
        scratch_shapes=scratch, input_output_aliases=aliases,
        compiler_params=_params(("parallel" if job is None else "arbitrary",)),
    )(*args)


HGRN_ROWS = 256
HGRN_FWD_SUB = 64
HGRN_BWD_SUB = 32
HGRN_GATE_ROWS = 128


def _block_cumsum(x, pos):
    for s in (1, 2, 4, 8):
        x = x + jnp.where(pos >= s, pltpu.roll(x, s, 0), 0.0)
    return x


def _block_rev_cumsum(x, pos, rows):
    for s in (1, 2, 4, 8):
        x = x + jnp.where(pos < BLK - s, pltpu.roll(x, rows - s, 0), 0.0)
    return x


def _block_last(x, pos, rows):
    m = jnp.where(pos == BLK - 1, x, 0.0)
    for s in (1, 2, 4, 8):
        m = m + pltpu.roll(m, rows - s, 0)
    return m


def _hgrn_gates(q, z, lb):
    sgq = _sig(q)
    qs = q * sgq
    sg = _sig(z)
    f = lb + (1.0 - lb) * sg
    kk = (1.0 - lb) * (1.0 - sg)
    return sgq, qs, sg, f, kk


def _hgrn_decays(q, z, lb, pos, rows):
    sgq, qs, sg, f, kk = _hgrn_gates(q, z, lb)
    b = _block_cumsum(jnp.log(f), pos)
    return sgq, qs, sg, f, kk, b, _block_last(b, pos, rows)


def _hgrn_specs(T, d_conv, n_t, rev):
    def t_of(i):
        return (n_t - 1 - i) if rev else i

    def pcol(off):
        o = off // HEAD
        return pl.BlockSpec((T, HEAD), lambda h, i, o=o: (t_of(i), h + o))

    q_spec, z_spec, v_spec, g_spec = pcol(3 * d_conv), pcol(4 * d_conv), pcol(5 * d_conv), pcol(6 * d_conv)
    lb_spec = pl.BlockSpec((1, HEAD), lambda h, i: (0, h))
    gn_spec = pl.BlockSpec((1, HEAD), lambda h, i: (0, 0))
    act_spec = pl.BlockSpec((T, HEAD), lambda h, i: (t_of(i), h))
    st_spec = pl.BlockSpec((None, T // BLK, HEAD, HEAD), lambda h, i: (h, t_of(i), 0, 0))
    return q_spec, z_spec, v_spec, g_spec, lb_spec, gn_spec, act_spec, st_spec


def _hgrn_fwd(proj, lb, gn, d_conv, job=None):
    S = proj.shape[0]
    H = d_conv // HEAD
    T = min(HGRN_ROWS, S)
    n_t, nb = S // T, T // BLK
    sub = min(HGRN_FWD_SUB, T)
    q_spec, z_spec, v_spec, g_spec, lb_spec, gn_spec, act_spec, st_spec = _hgrn_specs(T, d_conv, n_t, False)

    def body(q_ref, z_ref, v_ref, g_ref, lb_ref, gn_ref, ob_ref, o_ref, st_ref, qe_s, ke_s, dec_s, state, v_s, o_s, u_s):
        @pl.when(pl.program_id(1) == 0)
        def _():
            state[...] = jnp.zeros_like(state)

        pos = lax.broadcasted_iota(jnp.int32, (sub, HEAD), 0) % BLK
        lbv = lb_ref[...]

        def vector_pass(s, carry):
            r = pl.ds(pl.multiple_of(s * sub, sub), sub)
            v = v_ref[r, :]
            _, qs, _, _, kk, b, b_tot = _hgrn_decays(q_ref[r, :], z_ref[r, :], lbv, pos, sub)
            qe_s[r, :] = (qs * jnp.exp(b)).astype(BF16)
            ke_s[r, :] = (kk * jnp.exp(b_tot - b)).astype(BF16)
            v_s[r, :] = v.astype(BF16)
            dec_s[r, :] = jnp.exp(b_tot)
            o = jnp.sum(qs * kk, axis=-1, keepdims=True) * v
            for d in range(1, BLK):
                e = jnp.exp(b - pltpu.roll(b, d, 0))
                a = jnp.sum(qs * pltpu.roll(kk, d, 0) * e, axis=-1, keepdims=True)
                o = o + jnp.where(pos >= d, a * pltpu.roll(v, d, 0), 0.0)
            o_s[r, :] = o
            return carry

        lax.fori_loop(0, T // sub, vector_pass, 0)

        for j in range(nb):
            rows = pl.ds(j * BLK, BLK)
            u_s[j] = lax.dot_general(v_s[rows, :], ke_s[rows, :], (((0,), (0,)), ((), ())),
                                     preferred_element_type=F32)
        st = state[...]
        for j in range(nb):
            st_ref[j] = st.astype(BF16)
            st = st * dec_s[pl.ds(j * BLK, 1), :] + u_s[j]
        state[...] = st
        for j in range(nb):
            rows = pl.ds(j * BLK, BLK)
            o_s[rows, :] += lax.dot_general(qe_s[rows, :], st_ref[j], (((1,), (1,)), ((), ())),
                                            preferred_element_type=F32)
        o = o_s[...]
        o_ref[...] = o
        r = lax.rsqrt(jnp.mean(o * o, axis=-1, keepdims=True) + EPS)
        g = g_ref[...]
        ob_ref[...] = ((o * r) * gn_ref[...] * (g * _sig(g))).astype(BF16)

    grid = (H, n_t)
    in_specs = [q_spec, z_spec, v_spec, g_spec, lb_spec, gn_spec]
    args = [proj, proj, proj, proj, lb, gn]
    out_specs = [act_spec, act_spec, st_spec, act_spec, act_spec, act_spec]
    out_shape = [jax.ShapeDtypeStruct((S, d_conv), BF16), jax.ShapeDtypeStruct((S, d_conv), F32),
                 jax.ShapeDtypeStruct((H, S // BLK, HEAD, HEAD), BF16),
                 jax.ShapeDtypeStruct((S, d_conv), BF16), jax.ShapeDtypeStruct((S, d_conv), BF16),
                 jax.ShapeDtypeStruct((S, d_conv), F32)]
    scratch = [pltpu.VMEM((HEAD, HEAD), F32), pltpu.VMEM((T, HEAD), BF16), pltpu.VMEM((T, HEAD), F32),
               pltpu.VMEM((nb, HEAD, HEAD), F32)]
    aliases, wrap = _host(job, grid, in_specs, args, out_specs, out_shape, scratch)
    return pl.pallas_call(
        wrap(body), name="hgrn_fwd", grid=grid, in_specs=in_specs, out_specs=out_specs, out_shape=out_shape,
        scratch_shapes=scratch, input_output_aliases=aliases,
        compiler_params=_params(("parallel" if job is None else "arbitrary", "arbitrary")),
    )(*args)


def _hgrn_bwd(dob, o_raw, states, qe, ke, dec, proj, lb, gn, d_conv, job=None):
    S = proj.shape[0]
    H = d_conv // HEAD
    T = min(HGRN_ROWS, S)
    n_t, nb = S // T, T // BLK
    sub = min(HGRN_BWD_SUB, T)
    gate_rows = min(HGRN_GATE_ROWS, T)
    q_spec, z_spec, v_spec, g_spec, lb_spec, gn_spec, act_spec, st_spec = _hgrn_specs(T, d_conv, n_t, True)

    def body(dob_ref, o_ref, st_ref, qe_b, ke_b, dec_s, q_ref, z_ref, v_ref, g_ref, lb_ref, gn_ref,
             dq_ref, dz_ref, dv_ref, dg_ref, dlb_ref, dgn_ref,
             dstate, do_b, v_b, dqe_s, dke_s, dvi_s, ddec_s, do_s, u_s, ds_s):
        @pl.when(pl.program_id(1) == 0)
        def _():
            dstate[...] = jnp.zeros_like(dstate)
            dlb_ref[...] = jnp.zeros_like(dlb_ref)
            dgn_ref[...] = jnp.zeros_like(dgn_ref)

        pos = lax.broadcasted_iota(jnp.int32, (sub, HEAD), 0) % BLK
        lbv, gnv = lb_ref[...], gn_ref[...]

        def before(s, carry):
            r = pl.ds(pl.multiple_of(s * gate_rows, gate_rows), gate_rows)
            g = g_ref[r, :]
            v_b[r, :] = v_ref[r, :].astype(BF16)
            o = o_ref[r, :]
            rs = lax.rsqrt(jnp.mean(o * o, axis=-1, keepdims=True) + EPS)
            on = o * rs
            sgg = _sig(g)
            dobv = dob_ref[r, :]
            dog = dobv * (g * sgg)
            dgn_ref[...] += jnp.sum(dog * on, axis=0, keepdims=True)
            don = dog * gnv
            do = rs * (don - on * jnp.mean(don * on, axis=-1, keepdims=True))
            dg_ref[r, :] = (dobv * (on * gnv) * (sgg * (1.0 + g * (1.0 - sgg)))).astype(BF16)
            do_s[r, :] = do
            do_b[r, :] = do.astype(BF16)
            return carry

        lax.fori_loop(0, T // gate_rows, before, 0)

        for j in range(nb):
            rows = pl.ds(j * BLK, BLK)
            dob_j = do_b[rows, :]
            u_s[j] = lax.dot_general(dob_j, qe_b[rows, :], (((0,), (0,)), ((), ())), preferred_element_type=F32)
            dqe_s[rows, :] = lax.dot_general(dob_j, st_ref[j], (((1,), (0,)), ((), ())), preferred_element_type=F32)
        dst = dstate[...]
        for j in reversed(range(nb)):
            ds_s[j] = dst.astype(BF16)
            ddec = jnp.sum(st_ref[j].astype(F32) * dst, axis=0, keepdims=True)
            ddec_s[pl.ds(j * BLK, BLK), :] = jnp.broadcast_to(ddec, (BLK, HEAD))
            dst = dst * dec_s[pl.ds(j * BLK, 1), :] + u_s[j]
        dstate[...] = dst
        for j in range(nb):
            rows = pl.ds(j * BLK, BLK)
            dke_s[rows, :] = lax.dot_general(v_b[rows, :], ds_s[j], (((1,), (0,)), ((), ())), preferred_element_type=F32)
            dvi_s[rows, :] = lax.dot_general(ke_b[rows, :], ds_s[j], (((1,), (1,)), ((), ())), preferred_element_type=F32)

        def after(s, carry):
            r = pl.ds(pl.multiple_of(s * sub, sub), sub)
            q, v = q_ref[r, :], v_ref[r, :]
            sgq, qs, sg, f, kk, b, b_tot = _hgrn_decays(q, z_ref[r, :], lbv, pos, sub)
            do = do_s[r, :]
            dqs = dqe_s[r, :] * jnp.exp(b)
            dkk = dke_s[r, :] * jnp.exp(b_tot - b)
            d_tot = jnp.where(pos == BLK - 1, _block_cumsum(dkk * kk, pos) + ddec_s[r, :] * jnp.exp(b_tot), 0.0)
            dv = dvi_s[r, :]
            da = jnp.sum(do * v, axis=-1, keepdims=True)
            dv = dv + jnp.sum(qs * kk, axis=-1, keepdims=True) * do
            dqs = dqs + da * kk
            dkk = dkk + da * qs
            for d in range(1, BLK):
                kd = pltpu.roll(kk, d, 0)
                e = jnp.where(pos >= d, jnp.exp(b - pltpu.roll(b, d, 0)), 0.0)
                a = jnp.sum(qs * kd * e, axis=-1, keepdims=True)
                da = jnp.sum(do * pltpu.roll(v, d, 0), axis=-1, keepdims=True)
                gq = da * e
                dqs = dqs + gq * kd
                dkk = dkk + pltpu.roll(gq * qs, sub - d, 0)
                dv = dv + pltpu.roll(a * do, sub - d, 0)
            db = qs * dqs - kk * dkk + d_tot
            dlf = _block_rev_cumsum(db, pos, sub)
            df = dlf / f - dkk
            dlb_ref[...] += jnp.sum(df * (1.0 - sg), axis=0, keepdims=True)
            dz_ref[r, :] = (df * (1.0 - lbv) * sg * (1.0 - sg)).astype(BF16)
            dq_ref[r, :] = (dqs * (sgq * (1.0 + q * (1.0 - sgq)))).astype(BF16)
            dv_ref[r, :] = dv.astype(BF16)
            return carry

        lax.fori_loop(0, T // sub, after, 0)

    tb = lambda dt: pltpu.VMEM((T, HEAD), dt)
    grid = (H, n_t)
    in_specs = [act_spec, act_spec, st_spec, act_spec, act_spec, act_spec, q_spec, z_spec, v_spec, g_spec, lb_spec,
                gn_spec]
    args = [dob, o_raw, states, qe, ke, dec, proj, proj, proj, proj, lb, gn]
    out_specs = [act_spec, act_spec, act_spec, act_spec, lb_spec, pl.BlockSpec((None, 1, HEAD), lambda h, i: (h, 0, 0))]
    out_shape = ([jax.ShapeDtypeStruct((S, d_conv), BF16)] * 4
                 + [jax.ShapeDtypeStruct((1, d_conv), F32), jax.ShapeDtypeStruct((H, 1, HEAD), F32)])
    scratch = [pltpu.VMEM((HEAD, HEAD), F32), tb(BF16), tb(BF16), tb(F32), tb(F32), tb(F32), tb(F32), tb(F32),
               pltpu.VMEM((nb, HEAD, HEAD), F32), pltpu.VMEM((nb, HEAD, HEAD), BF16)]
    aliases, wrap = _host(job, grid, in_specs, args, out_specs, out_shape, scratch)
    return pl.pallas_call(
        wrap(body), name="hgrn_bwd", grid=grid, in_specs=in_specs, out_specs=out_specs, out_shape=out_shape,
        scratch_shapes=scratch, input_output_aliases=aliases,
        compiler_params=_params(("parallel" if job is None else "arbitrary", "arbitrary")),
    )(*args)


def _first(accs, extras):
    return [accs[0]]


def _local_step(x, target, mod, norm_mix_g, lb, gnorm_g, norm_ffn_g, norm_final_g, conv_w, bufs, c_idx, order):
    S, D = x.shape
    d_conv = D // 2
    d_in = 7 * d_conv + 2 * D
    d_ff = N_CHIPS * bufs["w_ffn_down"].shape[1]
    nb_in, nb_co, nb_ff = bufs["w_in"].shape[2], bufs["w_conv_out"].shape[2], bufs["w_ffn_gate"].shape[2]
    rows = lambda g: g.reshape(-1, g.shape[2])
    sh_m, sc_m, gt_m, sh_f, sc_f, gt_f = [mod[:, i * D:(i + 1) * D] for i in range(6)]
    tm = _pick(S, 512, 16)
    tm2 = _pick(S, 1024, 16)
    tn_in = _pick(nb_in, 1408, 128)
    tn_co = _pick(nb_co, 512, 128)
    tn_ff = _pick(nb_ff, 1408, 128)
    tn_d = _pick(D, 512, 128)
    tw = _pick(D, 1024, 128)
    tg = _pick(D, 512, 128)

    h = _norm_mod("norm_mix", x, norm_mix_g, sc_m, sh_m)
    proj, w_in = _proj_gather(h, bufs["w_in"], order, None)
    ob, o_raw, states, qe, ke, dec, *got = _hgrn_fwd(
        proj, lb, gnorm_g, d_conv,
        job=_GatherJob([bufs[k] for k in ("w_conv_out", "w_hgrn_out", "w_o", "w_ffn_gate")]))
    ya, w_conv_out, w_hgrn_out, w_o, w_ffn_gate = _conv_fwd(proj, conv_w, d_conv, job=_ForwardJob(got))
    w_o = rows(w_o)

    def merge_epi(accs, ex):
        y_a, y_b = accs
        return [y_a, y_b, _sig(ex[0]) * y_a + _sig(ex[1]) * y_b]

    y_a, y_b, merged, w_ffn_up = _matmul(
        "branch_out", [(ya, 'n', w_conv_out, 'n'), (ob, 'n', w_hgrn_out, 'n')], [0, 1], S, D, d_conv, tm2, tn_co, d_conv,
        [(proj, 'tile', 7 * d_conv), (proj, 'tile', 7 * d_conv + D)], [(F32, None), (F32, None), (BF16, None)], merge_epi,
        job=_GatherJob([bufs["w_ffn_up"]], 0, 2))

    def resid_epi(accs, ex):
        return [accs[0], ex[0] + ex[1] * accs[0]]

    mo, x1, w_ffn_up = _matmul("w_o", [(merged, 'n', w_o, 'n')], [0], S, D, D, tm2, tw, D,
                               [(x, 'tile', 0), (gt_m, 'row', 0)], [(F32, None), (F32, None)], resid_epi,
                               job=_GatherJob([w_ffn_up], 1, 2))
    h2, w_ffn_up = _norm_mod("norm_ffn", x1, norm_ffn_g, sc_f, sh_f, job=_ForwardJob([w_ffn_up]))

    def swiglu_epi(accs, ex):
        gg, uu = accs
        return [gg, uu, gg * _sig(gg) * uu]

    G, U, act, w_ffn_down = _matmul(
        "ffn_in", [(h2, 'n', w_ffn_gate, 'n'), (h2, 'n', w_ffn_up, 'n')], [0, 1], S, d_ff, D,
        tm, tn_ff, D, [], [(BF16, None), (BF16, None), (BF16, None)], swiglu_epi, job=_GatherJob([bufs["w_ffn_down"]]))
    w_ffn_down = rows(_forward_halves("forward_down", [w_ffn_down])[0])
    ff, x2 = _matmul("ffn_out", [(act, 'n', w_ffn_down, 'n')], [0], S, D, d_ff, tm2, tn_d, d_ff,
                     [(x1, 'tile', 0), (gt_f, 'row', 0)], [(F32, None), (F32, None)], resid_epi, rows_outer=True)

    dx2, dffb, loss, dgt_f, dg_final = _loss_head(x2, target, norm_final_g, ff, gt_f)

    def swiglu_bwd_epi(accs, ex):
        dact, gg, uu = accs[0], ex[0], ex[1]
        s = _sig(gg)
        return [dact * uu * (s * (1.0 + gg * (1.0 - s))), dact * (gg * s)]

    dG, dU = _matmul("d_act", [(dffb, 'n', w_ffn_down, 't')], [0], S, d_ff, D, tm2, tn_ff, D,
                     [(G, 'tile', 0), (U, 'tile', 0)], [(BF16, None), (BF16, None)], swiglu_bwd_epi)
    dw_down, = _matmul("dw_ffn_down", [(act, 't', dffb, 'n')], [0], d_ff, D, S, _pick(d_ff, 512, 128),
                       D, S, [], [(BF16, None)], _first)
    dh2, = _matmul("d_h2", [(dG, 'n', w_ffn_gate, 't'), (dU, 'n', w_ffn_up, 't')], [0, 0], S, D, d_ff,
                   tm, D, tn_ff, [], [(F32, None)], _first)
    dw_gate, = _matmul("dw_ffn_gate", [(h2, 't', dG, 'n')], [0], D, d_ff, S, tg, tn_ff, S, [], [(BF16, nb_ff)], _first)
    dw_up, = _matmul("dw_ffn_up", [(h2, 't', dU, 'n')], [0], D, d_ff, S, tg, tn_ff, S, [], [(BF16, nb_ff)], _first)

    def pair_sums(names, grads, from_sibling):
        pairs = [_pair_sum("pair_sum_" + k, g, q, c_idx) for k, g, q in zip(names, grads, from_sibling)]
        return [p[0] for p in pairs], [p[1] for p in pairs]

    ffn_names = ["w_ffn_down", "w_ffn_gate", "w_ffn_up"]
    ffn_grads = [dw_down.reshape(N_CHIPS, -1, D), dw_gate, dw_up]
    dx1, dsh_f, dsc_f, dg_ffn, dmob, dgt_m, *from_sibling = _norm_mod_bwd(
        "norm_ffn_bwd", dh2, x1, norm_ffn_g, sc_f, dx2, (mo, gt_m), job=_SwapJob(ffn_grads))
    parts_f, slots_f = pair_sums(ffn_names, ffn_grads, from_sibling)

    def merge_bwd_epi(accs, ex):
        dm, ga, gb, ya_, yb_ = accs[0], ex[0], ex[1], ex[2], ex[3]
        sa, sb = _sig(ga), _sig(gb)
        return [dm * ya_ * sa * (1.0 - sa), dm * yb_ * sb * (1.0 - sb), dm * sa, dm * sb]

    dga, dgb, dy_a, dy_b = _matmul(
        "d_merged", [(dmob, 'n', w_o, 't')], [0], S, D, D, tm2, tn_co, D,
        [(proj, 'tile', 7 * d_conv), (proj, 'tile', 7 * d_conv + D), (y_a, 'tile', 0), (y_b, 'tile', 0)],
        [(BF16, None)] * 4, merge_bwd_epi)
    dw_o, = _matmul("dw_o", [(merged, 't', dmob, 'n')], [0], D, D, S, tg, D, S, [], [(BF16, None)], _first)
    both = lambda accs, ex: accs
    dya, dob = _matmul("d_branch", [(dy_a, 'n', w_conv_out, 't'), (dy_b, 'n', w_hgrn_out, 't')], [0, 1], S, d_conv, D,
                       tm2, _pick(d_conv, 1024, 128), tn_co, [], [(F32, None), (F32, None)], both)
    dw_co, dw_ho = _matmul("dw_branch", [(ya, 't', dy_a, 'n'), (ob, 't', dy_b, 'n')], [0, 1], d_conv, D, S,
                           _pick(d_conv, 512, 128), tn_co, S, [], [(BF16, nb_co), (BF16, nb_co)], both)
    branch_names = ["w_conv_out", "w_hgrn_out", "w_o"]
    branch_grads = [dw_co, dw_ho, dw_o.reshape(N_CHIPS, -1, D)]
    dab, dac, dax, dconv_w, *from_sibling = _conv_bwd(dya, proj, conv_w, d_conv, job=_SwapJob(branch_grads))
    parts_b, slots_b = pair_sums(branch_names, branch_grads, from_sibling)
    dq, dz, dv, dgo, dlb, dgn, *arrived = _hgrn_bwd(dob, o_raw, states, qe, ke, dec, proj, lb, gnorm_g, d_conv,
                                                    job=_ScatterJob(parts_f + parts_b, slots_f + slots_b))
    dproj = jnp.concatenate([dab, dac, dax, dq, dz, dv, dgo, dga, dgb], axis=1)
    dw_in, = _matmul("dw_in", [(h, 't', dproj, 'n')], [0], D, d_in, S, tg, tn_in, S, [], [(BF16, nb_in)], _first)
    parts_i, slots_i = pair_sums(["w_in"], [dw_in], _swap_halves("swap_in", [dw_in]))
    dh, arrived_in = _matmul("d_h", [(dproj, 'n', w_in, 't')], [0], S, D, d_in, tm2, D, tn_in, [], [(F32, None)], _first,
                             job=_ScatterJob(parts_i, slots_i))
    dx, dsh_m, dsc_m, dg_mix = _norm_mod_bwd("norm_mix_bwd", dh, x, norm_mix_g, sc_m, dx1)
    dmod = jnp.concatenate([dsh_m, dsc_m, dgt_m, dsh_f, dsc_f, dgt_f], axis=1)
    small = dict(dmod=dmod, dg_mix=dg_mix, dg_ffn=dg_ffn, dg_final=dg_final, dlb=dlb,
                 dgn=jnp.sum(dgn, axis=0), dconv_w=dconv_w)
    big = dict(zip(ffn_names + branch_names, arrived), w_in=arrived_in)
    return loss, dx, small, big


def _adamw_math(w, g, m, v):
    m = ADAM_B1 * m + (1.0 - ADAM_B1) * g
    v = ADAM_B2 * v + (1.0 - ADAM_B2) * (g * g)
    m_hat = m / (1.0 - ADAM_B1 ** ADAM_STEP)
    v_hat = v / (1.0 - ADAM_B2 ** ADAM_STEP)
    delta = -ADAM_LR * (m_hat / (jnp.sqrt(v_hat) + ADAM_EPS) + ADAM_WD * w)
    return delta, m, v


def _adamw(name, w, g, m, v):
    R, C = w.shape
    tr = _pick(R, max(8, (256 * 1024) // C // 8 * 8), 8)
    spec = pl.BlockSpec((tr, C), lambda i: (i, 0))

    def body(w_ref, g_ref, m_ref, v_ref, go_ref, d_ref, mo_ref, vo_ref):
        gv = g_ref[...]
        d, m2, v2 = _adamw_math(w_ref[...], gv, m_ref[...], v_ref[...])
        go_ref[...] = gv
        d_ref[...] = d
        mo_ref[...] = m2
        vo_ref[...] = v2

    return pl.pallas_call(
        body, name=name, grid=(R // tr,), in_specs=[spec] * 4, out_specs=[spec] * 4,
        out_shape=[jax.ShapeDtypeStruct((R, C), F32)] * 4, compiler_params=_params(("parallel",)),
    )(w, g, m, v)


def _ada_update(c_act_t, dmod, w, m, v):
    R, C = w.shape
    B = dmod.shape[0]
    tr = _pick(R, 256, 8)
    tc = _pick(C, 1536, 128)
    spec = pl.BlockSpec((tr, tc), lambda i, j: (i, j))

    def body(ct_ref, dm_ref, w_ref, m_ref, v_ref, g_ref, d_ref, mo_ref, vo_ref):
        ct = ct_ref[...]
        dm = dm_ref[...]
        g = ct[:, 0:1] * dm[0:1, :]
        for b in range(1, B):
            g = g + ct[:, b:b + 1] * dm[b:b + 1, :]
        d, m2, v2 = _adamw_math(w_ref[...], g, m_ref[...], v_ref[...])
        g_ref[...] = g
        d_ref[...] = d
        mo_ref[...] = m2
        vo_ref[...] = v2

    return pl.pallas_call(
        body, name="ada_update", grid=(R // tr, C // tc),
        in_specs=[pl.BlockSpec((tr, B), lambda i, j: (i, 0)), pl.BlockSpec((B, tc), lambda i, j: (0, j)),
                  spec, spec, spec],
        out_specs=[spec] * 4, out_shape=[jax.ShapeDtypeStruct((R, C), F32)] * 4,
        compiler_params=_params(("parallel", "parallel")),
    )(c_act_t, dmod, w, m, v)


def _prep(c_all, lb_param):
    def body(c_ref, p_ref, ca_ref, cab_ref, lb_ref):
        cv = c_ref[...]
        ca = cv * _sig(cv)
        ca_ref[...] = ca
        cab_ref[...] = ca.astype(BF16)
        lb_ref[...] = _sig(p_ref[0:1, :] - p_ref[1:2, :])

    return pl.pallas_call(
        body, name="prep",
        out_shape=[jax.ShapeDtypeStruct(c_all.shape, F32), jax.ShapeDtypeStruct(c_all.shape, BF16),
                   jax.ShapeDtypeStruct((1, lb_param.shape[1]), F32)],
    )(c_all, lb_param)


def _small_reduce(parts):
    W = parts.shape[1]

    def body(p_ref, o_ref):
        acc = p_ref[0:1, :]
        for d in range(1, N_DEV):
            acc = acc + p_ref[d:d + 1, :]
        o_ref[...] = acc

    return pl.pallas_call(body, name="small_reduce", out_shape=jax.ShapeDtypeStruct((1, W), F32))(parts)


def _lb_grad(dlb, lb):
    def body(d_ref, lb_ref, o_ref):
        t = d_ref[...] * lb_ref[...] * (1.0 - lb_ref[...])
        o_ref[0:1, :] = t
        o_ref[1:2, :] = -t

    return pl.pallas_call(body, name="lb_grad", out_shape=jax.ShapeDtypeStruct((2, dlb.shape[1]), F32))(dlb, lb)


def _place():
    x, y, c = lax.axis_index("x"), lax.axis_index("y"), lax.axis_index("c")
    chips = [(1 - x, y), (x, 1 - y), (1 - x, 1 - y)]
    return x, y, c, chips


def _allgather_rows(name, v):
    m_per, n = v.shape

    def body(x_ref, out_ref, send_sems, recv_sems, local_sem):
        x, y, c, chips = _place()
        me, sibling = (x, y, c), (x, y, 1 - c)

        def rows(px, py, pc):
            return out_ref.at[pl.ds((4 * px + 2 * py + pc) * m_per, m_per), :]

        def copy(k, block, to, src=None):
            return pltpu.make_async_remote_copy(
                src_ref=rows(*block) if src is None else src, dst_ref=rows(*block),
                send_sem=send_sems.at[k], recv_sem=recv_sems.at[k], device_id=to, device_id_type=MESH)

        mine = pltpu.make_async_copy(x_ref, rows(*me), local_sem)
        mine.start()
        first = [copy(0, me, sibling, src=x_ref)]
        first += [copy(1 + j, me, (*chip, c), src=x_ref) for j, chip in enumerate(chips)]
        for cp in first:
            cp.start()
        passed = [copy(4 + j, (*chip, c), sibling) for j, chip in enumerate(chips)]
        for j, chip in enumerate(chips):
            copy(1 + j, (*chip, c), me).wait_recv()
            passed[j].start()
        copy(0, sibling, me).wait_recv()
        for j, chip in enumerate(chips):
            copy(4 + j, (*chip, 1 - c), me).wait_recv()
        for cp in first + passed:
            cp.wait_send()
        mine.wait()

    return pl.pallas_call(
        body, name=name, out_shape=jax.ShapeDtypeStruct((N_DEV * m_per, n), v.dtype),
        in_specs=[pl.BlockSpec(memory_space=pltpu.VMEM)], out_specs=pl.BlockSpec(memory_space=pltpu.VMEM),
        scratch_shapes=[pltpu.SemaphoreType.DMA((7,)), pltpu.SemaphoreType.DMA((7,)), pltpu.SemaphoreType.DMA],
    )(v)


def _cast_place(name, w, k_idx):
    R, C = w.shape
    tr = _pick(R, max(16, (512 * 1024) // C // 16 * 16), 16)

    def body(k_ref, w_ref, o_ref):
        o_ref[...] = w_ref[...].astype(BF16)

    return pl.pallas_call(
        body, name=name,
        grid_spec=pltpu.PrefetchScalarGridSpec(
            num_scalar_prefetch=1, grid=(R // tr,),
            in_specs=[pl.BlockSpec((tr, C), lambda i, k_ref: (i, 0))],
            out_specs=pl.BlockSpec((None, tr, C), lambda i, k_ref: (k_ref[0], i, 0))),
        out_shape=jax.ShapeDtypeStruct((N_CHIPS, R, C), BF16),
        compiler_params=_params(("parallel",)),
    )(k_idx, w)


def _chip_copies(src_of, dst_of, got_of, n, sems, receiving):
    x, y, c, chips = _place()
    k_me = 2 * x + y
    send_sems, recv_sems = sems
    out = []
    for a in range(n):
        for j, chip in enumerate(chips):
            k_chip = 2 * chip[0] + chip[1]
            if receiving:
                src = dst = got_of(a, k_chip, c)
                to = (x, y, c)
            else:
                src, dst, to = src_of(a, k_chip, k_me, c), dst_of(a, k_me, c), (*chip, c)
            out.append(pltpu.make_async_remote_copy(
                src_ref=src, dst_ref=dst, send_sem=send_sems.at[3 * a + j], recv_sem=recv_sems.at[3 * a + j],
                device_id=to, device_id_type=MESH))
    return out


class _ChipJob:
    def start(self, j_in, j_out, sems):
        for send in self.copies(j_in, j_out, sems, False):
            send.start()

    def finish(self, j_in, j_out, sems):
        for recv in self.copies(j_in, j_out, sems, True):
            recv.wait_recv()
        for send in self.copies(j_in, j_out, sems, False):
            send.wait_send()


class _GatherJob(_ChipJob):
    def __init__(self, bufs, part=0, n_parts=1):
        n = len(bufs)
        self.inputs, self.n_alias = list(bufs), n
        self.sem_shapes = [pltpu.SemaphoreType.DMA((3 * n,)), pltpu.SemaphoreType.DMA((3 * n,))]
        self.half = [b.shape[1] // 2 for b in bufs]
        self.part, self.n_parts = part, n_parts

    def copies(self, j_in, j_out, sems, receiving):
        def half(a, k, c):
            rows = self.half[a] // self.n_parts
            return j_out[a].at[k, pl.ds(c * self.half[a] + self.part * rows, rows)]

        return _chip_copies(lambda a, k_chip, k_me, c: half(a, k_me, c), half, half, len(self.half), sems, receiving)


class _ScatterJob(_ChipJob):
    def __init__(self, parts, slots):
        n = len(parts)
        self.n = n
        self.inputs, self.n_alias = list(parts) + list(slots), n
        self.sem_shapes = [pltpu.SemaphoreType.DMA((3 * n,)), pltpu.SemaphoreType.DMA((3 * n,))]

    def copies(self, j_in, j_out, sems, receiving):
        return _chip_copies(lambda a, k_chip, k_me, c: j_in[a].at[k_chip], lambda a, k_me, c: j_out[a].at[k_me],
                            lambda a, k_chip, c: j_out[a].at[k_chip], self.n, sems, receiving)


class _SwapJob(_ChipJob):
    def __init__(self, grads):
        n = len(grads)
        self.n = n
        self.half = [g.shape[1] // 2 for g in grads]
        landing = [lax.empty((N_CHIPS, g.shape[1] // 2, g.shape[2]), g.dtype) for g in grads]
        self.inputs, self.n_alias = list(grads) + landing, n
        self.sem_shapes = [pltpu.SemaphoreType.DMA((n,)), pltpu.SemaphoreType.DMA((n,))]

    def copies(self, j_in, j_out, sems, receiving):
        x, y, c, _ = _place()
        out = []
        for a in range(self.n):
            if receiving:
                src, to = j_out[a], (x, y, c)
            else:
                src, to = j_in[a].at[:, pl.ds((1 - c) * self.half[a], self.half[a])], (x, y, 1 - c)
            out.append(pltpu.make_async_remote_copy(src_ref=src, dst_ref=j_out[a], send_sem=sems[0].at[a],
                                                    recv_sem=sems[1].at[a], device_id=to, device_id_type=MESH))
        return out


class _ForwardJob(_ChipJob):
    def __init__(self, bufs):
        n = len(bufs)
        self.n = n
        self.half = [b.shape[1] // 2 for b in bufs]
        self.inputs, self.n_alias = list(bufs), n
        self.sem_shapes = [pltpu.SemaphoreType.DMA((3 * n,)), pltpu.SemaphoreType.DMA((3 * n,))]

    def copies(self, j_in, j_out, sems, receiving):
        x, y, c, chips = _place()
        out = []
        for a in range(self.n):
            for q, chip in enumerate(chips):
                hc, to = (1 - c, (x, y, c)) if receiving else (c, (x, y, 1 - c))
                part = j_out[a].at[2 * chip[0] + chip[1], pl.ds(hc * self.half[a], self.half[a])]
                out.append(pltpu.make_async_remote_copy(
                    src_ref=part, dst_ref=part, send_sem=sems[0].at[3 * a + q], recv_sem=sems[1].at[3 * a + q],
                    device_id=to, device_id_type=MESH))
        return out


def _forward_halves(name, bufs):
    n = len(bufs)

    def body(*refs):
        out_refs = refs[n:2 * n]
        send_sems, recv_sems = refs[2 * n:]
        x, y, c, chips = _place()
        started, waits = [], []
        for a in range(n):
            rh = bufs[a].shape[1] // 2
            for j, chip in enumerate(chips):
                k_chip = 2 * chip[0] + chip[1]
                got = out_refs[a].at[k_chip, pl.ds(c * rh, rh)]
                cp = pltpu.make_async_remote_copy(src_ref=got, dst_ref=got, send_sem=send_sems.at[3 * a + j],
                                                  recv_sem=recv_sems.at[3 * a + j], device_id=(x, y, 1 - c),
                                                  device_id_type=MESH)
                cp.start()
                started.append(cp)
                other = out_refs[a].at[k_chip, pl.ds((1 - c) * rh, rh)]
                waits.append(pltpu.make_async_remote_copy(
                    src_ref=other, dst_ref=other, send_sem=send_sems.at[3 * a + j], recv_sem=recv_sems.at[3 * a + j],
                    device_id=(x, y, c), device_id_type=MESH))
        for cp in waits:
            cp.wait_recv()
        for cp in started:
            cp.wait_send()

    return pl.pallas_call(
        body, name=name, out_shape=[jax.ShapeDtypeStruct(b.shape, b.dtype) for b in bufs],
        in_specs=[_HBM] * n, out_specs=[_HBM] * n, input_output_aliases={a: a for a in range(n)},
        scratch_shapes=[pltpu.SemaphoreType.DMA((3 * n,)), pltpu.SemaphoreType.DMA((3 * n,))],
    )(*bufs)


def _proj_gather(h, buf, order, job):
    S, D = h.shape
    nb = buf.shape[2]
    tm = _pick(S, 1024, 16)
    tn = _pick(nb, 1408, 128)
    per = nb // tn
    nj, ni = N_CHIPS * per, S // tm
    rh = D // 2

    def body(order_ref, h_ref, buf_in, proj_ref, buf_ref, wbuf, tile_sems, ici_send, ici_recv, d2d_send, d2d_recv):
        j, i = pl.program_id(0), pl.program_id(1)
        x, y, c, chips = _place()
        k_me = 2 * x + y

        def half(k, hc):
            return buf_ref.at[k, pl.ds(hc * rh, rh)]

        def ici(q, receiving):
            k_chip = 2 * chips[q][0] + chips[q][1]
            src, to = (half(k_chip, c), (x, y, c)) if receiving else (half(k_me, c), (*chips[q], c))
            return pltpu.make_async_remote_copy(src_ref=src, dst_ref=src, send_sem=ici_send.at[q],
                                                recv_sem=ici_recv.at[q], device_id=to, device_id_type=MESH)

        def d2d(q, receiving):
            k_chip = 2 * chips[q][0] + chips[q][1]
            src, to = (half(k_chip, 1 - c), (x, y, c)) if receiving else (half(k_chip, c), (x, y, 1 - c))
            return pltpu.make_async_remote_copy(src_ref=src, dst_ref=src, send_sem=d2d_send.at[q],
                                                recv_sem=d2d_recv.at[q], device_id=to, device_id_type=MESH)

        def tile_copy(t):
            col = pl.multiple_of((t % per) * tn, 128)
            return pltpu.make_async_copy(buf_ref.at[order_ref[t // per], :, pl.ds(col, tn)], wbuf.at[t % 2],
                                         tile_sems.at[t % 2])

        @pl.when(jnp.logical_and(j == 0, i == 0))
        def _():
            ici(0, False).start()
            ici(1, False).start()
            tile_copy(0).start()

        @pl.when(i == 0)
        def _():
            tile_copy(j).wait()
            for q in range(3):
                @pl.when(j + 1 == (q + 1) * per)
                def _():
                    ici(q, True).wait_recv()
                    d2d(q, False).start()
                    if q == 0:
                        ici(0, False).wait_send()
                        ici(1, False).wait_send()
                        ici(2, False).start()
                    d2d(q, True).wait_recv()

            @pl.when(j + 1 < nj)
            def _():
                tile_copy(j + 1).start()

        proj_ref[...] = jnp.dot(h_ref[...], wbuf[j % 2], preferred_element_type=F32)

        @pl.when(jnp.logical_and(j == nj - 1, i == ni - 1))
        def _():
            ici(2, False).wait_send()
            for q in range(3):
                d2d(q, False).wait_send()

    grid = (nj, ni)
    in_specs = [pl.BlockSpec((tm, D), lambda j, i, o: (i, 0)), _HBM]
    args = [h, buf]
    out_specs = [pl.BlockSpec((tm, tn), lambda j, i, o: (i, o[j // per] * per + j % per)), _HBM]
    out_shape = [jax.ShapeDtypeStruct((S, N_CHIPS * nb), F32), jax.ShapeDtypeStruct(buf.shape, buf.dtype)]
    scratch = [pltpu.VMEM((2, D, tn), BF16), pltpu.SemaphoreType.DMA((2,))] + [pltpu.SemaphoreType.DMA((3,))] * 4
    aliases, wrap = _host(job, grid, in_specs, args, out_specs, out_shape, scratch, n_prefetch=1)
    aliases[2] = 1
    return pl.pallas_call(
        wrap(body), name="proj",
        grid_spec=pltpu.PrefetchScalarGridSpec(num_scalar_prefetch=1, grid=grid, in_specs=in_specs, out_specs=out_specs,
                                               scratch_shapes=scratch),
        out_shape=out_shape, input_output_aliases=aliases, compiler_params=_params(("arbitrary", "arbitrary")),
    )(order, *args)


def _swap_halves(name, grads):
    n = len(grads)

    def body(*refs):
        in_refs, out_refs = refs[:n], refs[n:2 * n]
        send_sems, recv_sems = refs[2 * n:]
        x, y, c, _ = _place()
        cps = []
        for a in range(n):
            rh = grads[a].shape[1] // 2
            cp = pltpu.make_async_remote_copy(
                src_ref=in_refs[a].at[:, pl.ds((1 - c) * rh, rh)], dst_ref=out_refs[a],
                send_sem=send_sems.at[a], recv_sem=recv_sems.at[a], device_id=(x, y, 1 - c), device_id_type=MESH)
            cp.start()
            cps.append(cp)
        for cp in cps:
            cp.wait()

    return pl.pallas_call(
        body, name=name,
        out_shape=[jax.ShapeDtypeStruct((N_CHIPS, g.shape[1] // 2, g.shape[2]), g.dtype) for g in grads],
        in_specs=[_HBM] * n, out_specs=[_HBM] * n,
        scratch_shapes=[pltpu.SemaphoreType.DMA((n,)), pltpu.SemaphoreType.DMA((n,))],
    )(*grads)


def _pair_sum(name, g, q, c_idx):
    _, R, C = g.shape
    rh = R // 2
    tr = _pick(rh, max(16, (512 * 1024) // C // 16 * 16), 16)
    nh = rh // tr

    def body(c_ref, g_ref, q_ref, o_ref, o2_ref):
        s = (g_ref[...].astype(F32) + q_ref[...].astype(F32)).astype(BF16)
        o_ref[...] = s
        o2_ref[...] = s

    out_spec = pl.BlockSpec((None, tr, C), lambda k, i, c_ref: (k, i, 0))
    return pl.pallas_call(
        body, name=name,
        grid_spec=pltpu.PrefetchScalarGridSpec(
            num_scalar_prefetch=1, grid=(N_CHIPS, nh),
            in_specs=[pl.BlockSpec((None, tr, C), lambda k, i, c_ref: (k, c_ref[0] * nh + i, 0)),
                      pl.BlockSpec((None, tr, C), lambda k, i, c_ref: (k, i, 0))],
            out_specs=[out_spec, out_spec]),
        out_shape=[jax.ShapeDtypeStruct((N_CHIPS, rh, C), BF16)] * 2,
        compiler_params=_params(("parallel", "parallel")),
    )(c_idx, g, q)


def _sum_chips(name, r, c_idx):
    _, rh, C = r.shape
    tr = _pick(rh, max(16, (256 * 1024) // C // 16 * 16), 16)

    def body(c_ref, r_ref, o_ref):
        acc = r_ref[0].astype(F32)
        for k in range(1, N_CHIPS):
            acc = acc + r_ref[k].astype(F32)
        o_ref[...] = acc

    return pl.pallas_call(
        body, name=name,
        grid_spec=pltpu.PrefetchScalarGridSpec(
            num_scalar_prefetch=1, grid=(rh // tr,),
            in_specs=[pl.BlockSpec((N_CHIPS, tr, C), lambda i, c_ref: (0, i, 0))],
            out_specs=pl.BlockSpec((None, tr, C), lambda i, c_ref: (c_ref[0], i, 0))),
        out_shape=jax.ShapeDtypeStruct((2, rh, C), F32), compiler_params=_params(("parallel",)),
    )(c_idx, r)


def _share_halves(bufs):
    n = len(bufs)

    def body(*refs):
        out_refs = refs[n:2 * n]
        send_sems, recv_sems = refs[2 * n:]
        x, y, c, _ = _place()
        cps = []
        for a in range(n):
            cp = pltpu.make_async_remote_copy(
                src_ref=out_refs[a].at[c], dst_ref=out_refs[a].at[c], send_sem=send_sems.at[a],
                recv_sem=recv_sems.at[a], device_id=(x, y, 1 - c), device_id_type=MESH)
            cp.start()
            cps.append(cp)
        for a, cp in enumerate(cps):
            got = out_refs[a].at[1 - c]
            pltpu.make_async_remote_copy(src_ref=got, dst_ref=got, send_sem=send_sems.at[a], recv_sem=recv_sems.at[a],
                                         device_id=(x, y, c), device_id_type=MESH).wait_recv()
        for cp in cps:
            cp.wait_send()

    return pl.pallas_call(
        body, name="share_halves",
        out_shape=[jax.ShapeDtypeStruct(b.shape, b.dtype) for b in bufs],
        in_specs=[_HBM] * n, out_specs=[_HBM] * n, input_output_aliases={a: a for a in range(n)},
        scratch_shapes=[pltpu.SemaphoreType.DMA((n,)), pltpu.SemaphoreType.DMA((n,))],
    )(*bufs)


_BIG = ("w_in", "w_conv_out", "w_hgrn_out", "w_o", "w_ffn_gate", "w_ffn_up", "w_ffn_down")
_ROW_SHARDED = ("w_o", "w_ffn_down")
_WEIGHTS = ("w_ada", "b_ada", "norm_mix_g", "w_in", "conv_w", "lb_param", "gnorm_g", "w_conv_out", "w_hgrn_out",
            "w_o", "norm_ffn_g", "w_ffn_gate", "w_ffn_up", "w_ffn_down", "norm_final_g")


def _pack_rows(pieces):
    flat = jnp.concatenate([p.reshape(-1) for p in pieces])
    pad = (-flat.shape[0]) % 1024
    return jnp.pad(flat, (0, pad)).reshape(8, -1)


def kernel(x, c, w_ada, b_ada, norm_mix_g, w_in, conv_w, lb_param, gnorm_g, w_conv_out, w_hgrn_out, w_o, norm_ffn_g, w_ffn_gate, w_ffn_up, w_ffn_down, norm_final_g, loss_target, m_w_ada, m_b_ada, m_norm_mix_g, m_w_in, m_conv_w, m_lb_param, m_gnorm_g, m_w_conv_out, m_w_hgrn_out, m_w_o, m_norm_ffn_g, m_w_ffn_gate, m_w_ffn_up, m_w_ffn_down, m_norm_final_g, v_w_ada, v_b_ada, v_norm_mix_g, v_w_in, v_conv_w, v_lb_param, v_gnorm_g, v_w_conv_out, v_w_hgrn_out, v_w_o, v_norm_ffn_g, v_w_ffn_gate, v_w_ffn_up, v_w_ffn_down, v_norm_final_g):
    w = dict(w_ada=w_ada, b_ada=b_ada, norm_mix_g=norm_mix_g, w_in=w_in, conv_w=conv_w, lb_param=lb_param,
             gnorm_g=gnorm_g, w_conv_out=w_conv_out, w_hgrn_out=w_hgrn_out, w_o=w_o, norm_ffn_g=norm_ffn_g,
             w_ffn_gate=w_ffn_gate, w_ffn_up=w_ffn_up, w_ffn_down=w_ffn_down, norm_final_g=norm_final_g)
    m = dict(w_ada=m_w_ada, b_ada=m_b_ada, norm_mix_g=m_norm_mix_g, w_in=m_w_in, conv_w=m_conv_w, lb_param=m_lb_param,
             gnorm_g=m_gnorm_g, w_conv_out=m_w_conv_out, w_hgrn_out=m_w_hgrn_out, w_o=m_w_o, norm_ffn_g=m_norm_ffn_g,
             w_ffn_gate=m_w_ffn_gate, w_ffn_up=m_w_ffn_up, w_ffn_down=m_w_ffn_down, norm_final_g=m_norm_final_g)
    v = dict(w_ada=v_w_ada, b_ada=v_b_ada, norm_mix_g=v_norm_mix_g, w_in=v_w_in, conv_w=v_conv_w, lb_param=v_lb_param,
             gnorm_g=v_gnorm_g, w_conv_out=v_w_conv_out, w_hgrn_out=v_w_hgrn_out, w_o=v_w_o, norm_ffn_g=v_norm_ffn_g,
             w_ffn_gate=v_w_ffn_gate, w_ffn_up=v_w_ffn_up, w_ffn_down=v_w_ffn_down, norm_final_g=v_norm_final_g)
    two_d = lambda a: a.reshape((-1, a.shape[-1])) if a.ndim != 2 else a
    shapes = {k: a.shape for k, a in w.items()}
    w, m, v = ({k: two_d(a) for k, a in t.items()} for t in (w, m, v))
    w["lb_param"], m["lb_param"], v["lb_param"] = lb_param, m_lb_param, v_lb_param
    S, D = x.shape[1], x.shape[2]
    d_conv = D // 2
    ix, iy, ic = lax.axis_index("x"), lax.axis_index("y"), lax.axis_index("c")
    k_me = 2 * ix + iy
    dev = 2 * k_me + ic
    cw_shard = w["conv_w"].shape[1]
    ada_cols = w["w_ada"].shape[1]

    got = _allgather_rows("gather_cond", _pack_rows([c, w["conv_w"]])).reshape(N_DEV, -1)
    c_all = got[:, :D]
    conv_full = got[::2, D:D + 3 * cw_shard].reshape(N_CHIPS, 3, cw_shard).transpose(1, 0, 2).reshape(3, -1)
    c_act, c_act_b, lb = _prep(c_all, lb_param)
    b_cols = lax.dynamic_slice(w["b_ada"], (0, k_me * ada_cols), (1, ada_cols))
    mod_cols, = _matmul("ada_fwd", [(c_act_b, 'n', w["w_ada"].astype(BF16), 'n')], [0], N_DEV, ada_cols, D,
                        N_DEV, _pick(ada_cols, 1536, 128), D, [(b_cols, 'row', 0)], [(F32, None)],
                        lambda accs, ex: [accs[0] + ex[0]])
    mod_all = _allgather_rows("gather_mod", mod_cols).reshape(N_CHIPS, 2, N_DEV, ada_cols)[:, 0]
    mod_all = mod_all.transpose(1, 0, 2).reshape(N_DEV, -1)
    mod = lax.dynamic_slice(mod_all, (dev, 0), (1, mod_all.shape[1]))
    k_idx = jnp.reshape(k_me, (1,)).astype(jnp.int32)
    c_idx = jnp.reshape(ic, (1,)).astype(jnp.int32)
    bufs = {k: _cast_place("cast_" + k, w[k], k_idx) for k in _BIG}
    order = jnp.stack([k_me, 2 * (1 - ix) + iy, 2 * ix + (1 - iy), 2 * (1 - ix) + (1 - iy)]).astype(jnp.int32)

    loss, dx, small, arrived = _local_step(
        x[0], loss_target[0], mod, w["norm_mix_g"], lb, w["gnorm_g"], w["norm_ffn_g"], w["norm_final_g"], conv_full,
        bufs, c_idx, order)

    pieces = [small["dmod"], small["dg_mix"], small["dg_ffn"], small["dg_final"], small["dlb"], small["dgn"],
              small["dconv_w"], loss]
    sizes = [p.size for p in pieces]
    parts = _allgather_rows("gather_small", _pack_rows(pieces)).reshape(N_DEV, -1)
    total = _small_reduce(parts)
    offs = [0]
    for s in sizes:
        offs.append(offs[-1] + s)
    tot = [total[:, offs[i]:offs[i + 1]] for i in range(len(sizes))]
    dmod_all = parts[:, :sizes[0]]
    grads = {
        "b_ada": tot[0], "norm_mix_g": tot[1], "norm_ffn_g": tot[2], "norm_final_g": tot[3],
        "lb_param": _lb_grad(tot[4], lb), "gnorm_g": tot[5],
        "conv_w": lax.dynamic_slice(tot[6].reshape(3, -1), (0, k_me * cw_shard), (3, cw_shard)),
    }
    loss_out = tot[7][0, 0]

    halves = [_sum_chips("sum_chips_" + k, arrived[k], c_idx) for k in _BIG]
    whole = _share_halves(halves)
    for k, g in zip(_BIG, whole):
        grads[k] = g.reshape(-1, g.shape[-1])

    delta, new_m, new_v = {}, {}, {}
    dmod_cols = lax.dynamic_slice(dmod_all, (0, k_me * ada_cols), (N_DEV, ada_cols))
    grads["w_ada"], delta["w_ada"], new_m["w_ada"], new_v["w_ada"] = _ada_update(
        c_act.T, dmod_cols, w["w_ada"], m["w_ada"], v["w_ada"])
    for k in _WEIGHTS:
        if k != "w_ada":
            grads[k], delta[k], new_m[k], new_v[k] = _adamw("adamw_" + k, w[k], grads[k], m[k], v[k])
    outs = [loss_out, dx.reshape(x.shape)]
    for t in (grads, delta, new_m, new_v):
        outs += [t[k].reshape(shapes[k]) for k in _WEIGHTS]
    return tuple(outs)
```

```python
import functools

import jax
import jax.numpy as jnp
from jax import lax
from jax.experimental import pallas as pl
from jax.experimental.pallas import tpu as pltpu

F32 = jnp.float32
BF16 = jnp.bfloat16
MESH = pl.DeviceIdType.MESH

EPS = 1e-6
HEAD = 128
BLK = 16
N_CHIPS = 4
N_DEV = 8
VMEM_LIMIT_BYTES = 56 * 1024 * 1024

ADAM_LR = 0.001
ADAM_B1 = 0.9
ADAM_B2 = 0.999
ADAM_EPS = 1e-08
ADAM_WD = 0.01
ADAM_STEP = 10


def _pick(dim, pref, mult):
    if dim <= pref:
        return dim
    t = (pref // mult) * mult
    while t >= mult:
        if dim % t == 0:
            return t
        t -= mult
    return dim


def _sig(x):
    return 1.0 / (1.0 + jnp.exp(-x))


def _params(sem):
    return pltpu.CompilerParams(dimension_semantics=sem, vmem_limit_bytes=VMEM_LIMIT_BYTES)


def _gj(j, i, k):
    return j


def _gk(j, i, k):
    return k


def _wspec(arr, rt, ct, r_of, c_of):
    if arr.ndim == 2:
        return pl.BlockSpec((rt, ct), lambda j, i, k: (r_of(j, i, k), c_of(j, i, k)))
    per = arr.shape[2] // ct
    assert arr.shape[2] % ct == 0
    return pl.BlockSpec((None, rt, ct),
                        lambda j, i, k: (c_of(j, i, k) // per, r_of(j, i, k), c_of(j, i, k) % per))


_HBM = pl.BlockSpec(memory_space=pltpu.HBM)


def _host(job, grid, in_specs, args, out_specs, out_shape, scratch, n_prefetch=0):
    if job is None:
        return {}, (lambda body: body)
    n_in, n_out, n_scr = len(args), len(out_shape), len(scratch)
    n_job, n_alias = len(job.inputs), job.n_alias
    in_specs += [_HBM] * n_job
    args += job.inputs
    out_specs += [_HBM] * n_alias
    out_shape += [jax.ShapeDtypeStruct(a.shape, a.dtype) for a in job.inputs[n_job - n_alias:]]
    scratch += job.sem_shapes
    aliases = {n_prefetch + n_in + n_job - n_alias + t: n_out + t for t in range(n_alias)}

    def wrap(body):
        def hosted(*refs):
            pre, refs = refs[:n_prefetch], refs[n_prefetch:]
            ins, refs = refs[:n_in], refs[n_in:]
            j_in, refs = refs[:n_job], refs[n_job:]
            outs, refs = refs[:n_out], refs[n_out:]
            j_out, refs = refs[:n_alias], refs[n_alias:]
            scr, sems = refs[:n_scr], refs[n_scr:]
            first = functools.reduce(jnp.logical_and, [pl.program_id(d) == 0 for d in range(len(grid))])
            last = functools.reduce(jnp.logical_and, [pl.program_id(d) == grid[d] - 1 for d in range(len(grid))])

            @pl.when(first)
            def _():
                job.start(j_in, j_out, sems)

            body(*pre, *ins, *outs, *scr)

            @pl.when(last)
            def _():
                job.finish(j_in, j_out, sems)

        return hosted

    return aliases, wrap


def _plain(accs, extras):
    return accs


def _matmul(name, pairs, acc_of, M, N, K, tm, tn, tk, extras, outs, epilogue, job=None, rows_outer=False):
    assert M % tm == 0 and N % tn == 0 and K % tk == 0, (name, M, N, K, tm, tn, tk)
    nj, ni, nk = N // tn, M // tm, K // tk
    n_pairs, n_extra, n_out = len(pairs), len(extras), len(outs)
    n_acc = max(acc_of) + 1
    in_specs, args, dims = [], [], []
    lhs_of, seen = [], {}
    for a, am, b, bm in pairs:
        if (id(a), am) not in seen:
            seen[id(a), am] = len(args)
            args.append(a)
            if am == 'n':
                in_specs.append(pl.BlockSpec((tm, tk), lambda j, i, k: (i, k)))
            else:
                in_specs.append(pl.BlockSpec((tk, tm), lambda j, i, k: (k, i)))
        lhs_of.append(seen[id(a), am])
    n_lhs = len(args)
    for a, am, b, bm in pairs:
        if bm == 'n':
            in_specs.append(_wspec(b, tk, tn, _gk, _gj))
        else:
            in_specs.append(_wspec(b, tn, tk, _gj, _gk))
        dims.append((((1 if am == 'n' else 0,), (0 if bm == 'n' else 1,)), ((), ())))
        args.append(b)
    for e, kind, off in extras:
        assert off % tn == 0, (name, off, tn)
        o = off // tn
        if kind == 'tile':
            in_specs.append(pl.BlockSpec((tm, tn), lambda j, i, k, o=o: (i, j + o)))
        else:
            in_specs.append(pl.BlockSpec((1, tn), lambda j, i, k, o=o: (0, j + o)))
        args.append(e)
    out_shape, out_specs = [], []
    for dt, nb in outs:
        if nb is None:
            out_shape.append(jax.ShapeDtypeStruct((M, N), dt))
            out_specs.append(pl.BlockSpec((tm, tn), lambda j, i, k: (i, j)))
        else:
            assert nb % tn == 0 and N % nb == 0
            per = nb // tn
            out_shape.append(jax.ShapeDtypeStruct((N // nb, M, nb), dt))
            out_specs.append(pl.BlockSpec((None, tm, tn), lambda j, i, k, per=per: (j // per, i, j % per)))

    def body(*refs):
        n_ab = n_lhs + n_pairs
        e_refs = refs[n_ab:n_ab + n_extra]
        o_refs = refs[n_ab + n_extra:n_ab + n_extra + n_out]
        acc_refs = refs[n_ab + n_extra + n_out:]
        sums = [None] * n_acc
        for p in range(n_pairs):
            prod = lax.dot_general(refs[lhs_of[p]][...], refs[n_lhs + p][...], dims[p], preferred_element_type=F32)
            a = acc_of[p]
            sums[a] = prod if sums[a] is None else sums[a] + prod

        def finish(accs):
            res = epilogue(accs, [e[...] for e in e_refs])
            for o_ref, r in zip(o_refs, res):
                o_ref[...] = r.astype(o_ref.dtype)

        if nk == 1:
            finish(sums)
        else:
            k = pl.program_id(2)
            targets = o_refs if sum_in_outs else acc_refs

            @pl.when(k == 0)
            def _():
                for a in range(n_acc):
                    targets[a][...] = sums[a]

            @pl.when(k > 0)
            def _():
                for a in range(n_acc):
                    targets[a][...] += sums[a]

            if not sum_in_outs:
                @pl.when(k == nk - 1)
                def _():
                    finish([acc_refs[a][...] for a in range(n_acc)])

    sum_in_outs = epilogue is _plain and nk > 1 and n_out == n_acc and all(dt == F32 for dt, _ in outs)
    scratch = [pltpu.VMEM((tm, tn), F32) for _ in range(n_acc)] if nk > 1 and not sum_in_outs else []
    grid = (nj, ni, nk)
    if rows_outer:
        grid = (ni, nj, nk)
        swap = lambda spec: pl.BlockSpec(spec.block_shape, lambda i, j, k, f=spec.index_map: f(j, i, k))
        in_specs, out_specs = [swap(s) for s in in_specs], [swap(s) for s in out_specs]
    aliases, wrap = _host(job, grid, in_specs, args, out_specs, out_shape, scratch)
    sem = ("parallel", "parallel", "arbitrary") if job is None else ("arbitrary",) * 3
    return pl.pallas_call(
        wrap(body), name=name, grid=grid, in_specs=in_specs, out_specs=out_specs, out_shape=out_shape,
        scratch_shapes=scratch, input_output_aliases=aliases, compiler_params=_params(sem),
    )(*args)


def _row_spec(tr, d):
    return pl.BlockSpec((tr, d), lambda i: (i, 0))


def _vec_spec(d):
    return pl.BlockSpec((1, d), lambda i: (0, 0))


def _norm_mod(name, x, g, sc, sh, job=None):
    S, D = x.shape
    tr = _pick(S, 256, 8)

    def body(x_ref, g_ref, sc_ref, sh_ref, h_ref):
        xv = x_ref[...]
        r = lax.rsqrt(jnp.mean(xv * xv, axis=-1, keepdims=True) + EPS)
        h_ref[...] = ((xv * r) * g_ref[...] * (1.0 + sc_ref[...]) + sh_ref[...]).astype(BF16)

    grid = (S // tr,)
    in_specs = [_row_spec(tr, D), _vec_spec(D), _vec_spec(D), _vec_spec(D)]
    args = [x, g, sc, sh]
    out_specs, out_shape, scratch = [_row_spec(tr, D)], [jax.ShapeDtypeStruct((S, D), BF16)], []
    aliases, wrap = _host(job, grid, in_specs, args, out_specs, out_shape, scratch)
    res = pl.pallas_call(
        wrap(body), name=name, grid=grid, in_specs=in_specs, out_specs=out_specs, out_shape=out_shape,
        scratch_shapes=scratch, input_output_aliases=aliases,
        compiler_params=_params(("parallel" if job is None else "arbitrary",)),
    )(*args)
    return res[0] if job is None else res


def _loss_head(x2, target, g, ff, gt):
    S, D = x2.shape
    tr = _pick(S, 256, 8)

    def body(x_ref, t_ref, g_ref, ff_ref, gt_ref, dx_ref, dffb_ref, loss_ref, dgt_ref, dg_ref):
        i = pl.program_id(0)

        @pl.when(i == 0)
        def _():
            loss_ref[...] = jnp.zeros_like(loss_ref)
            dgt_ref[...] = jnp.zeros_like(dgt_ref)
            dg_ref[...] = jnp.zeros_like(dg_ref)

        xv = x_ref[...]
        gv = g_ref[...]
        r = lax.rsqrt(jnp.mean(xv * xv, axis=-1, keepdims=True) + EPS)
        xh = xv * r
        err = xh * gv - t_ref[...]
        loss_ref[...] += 0.5 * jnp.sum(jnp.mean(err * err, axis=-1, keepdims=True))
        dy = err * (1.0 / D)
        dg_ref[...] += jnp.sum(dy * xh, axis=0, keepdims=True)
        dxh = dy * gv
        dx = r * (dxh - xh * jnp.mean(dxh * xh, axis=-1, keepdims=True))
        dx_ref[...] = dx
        dffb_ref[...] = (dx * gt_ref[...]).astype(BF16)
        dgt_ref[...] += jnp.sum(dx * ff_ref[...], axis=0, keepdims=True)

    return pl.pallas_call(
        body, name="loss_head", grid=(S // tr,),
        in_specs=[_row_spec(tr, D), _row_spec(tr, D), _vec_spec(D), _row_spec(tr, D), _vec_spec(D)],
        out_specs=[_row_spec(tr, D), _row_spec(tr, D), pl.BlockSpec((1, 128), lambda i: (0, 0)),
                   _vec_spec(D), _vec_spec(D)],
        out_shape=[jax.ShapeDtypeStruct((S, D), F32), jax.ShapeDtypeStruct((S, D), BF16),
                   jax.ShapeDtypeStruct((1, 128), F32), jax.ShapeDtypeStruct((1, D), F32),
                   jax.ShapeDtypeStruct((1, D), F32)],
        compiler_params=_params(("arbitrary",)),
    )(x2, target, g, ff, gt)


def _norm_mod_bwd(name, dh, x, g, sc, dres, gated=None, job=None):
    S, D = x.shape
    tr = _pick(S, 256, 8)
    n = S // tr
    with_gate = gated is not None

    def body(*refs):
        if with_gate:
            dh_ref, x_ref, g_ref, sc_ref, dres_ref, mo_ref, gt_ref, dx_ref, dsh_ref, dsc_ref, dg_ref, dmob_ref, dgt_ref = refs
        else:
            dh_ref, x_ref, g_ref, sc_ref, dres_ref, dx_ref, dsh_ref, dsc_ref, dg_ref = refs
        i = pl.program_id(0)

        @pl.when(i == 0)
        def _():
            dsh_ref[...] = jnp.zeros_like(dsh_ref)
            dsc_ref[...] = jnp.zeros_like(dsc_ref)
            if with_gate:
                dgt_ref[...] = jnp.zeros_like(dgt_ref)

        xv = x_ref[...]
        dhv = dh_ref[...]
        gv = g_ref[...]
        scale = 1.0 + sc_ref[...]
        r = lax.rsqrt(jnp.mean(xv * xv, axis=-1, keepdims=True) + EPS)
        xh = xv * r
        dsh_ref[...] += jnp.sum(dhv, axis=0, keepdims=True)
        dsc_ref[...] += jnp.sum(dhv * xh, axis=0, keepdims=True)
        dxh = dhv * (gv * scale)
        dx = dres_ref[...] + r * (dxh - xh * jnp.mean(dxh * xh, axis=-1, keepdims=True))
        dx_ref[...] = dx
        if with_gate:
            dmob_ref[...] = (dx * gt_ref[...]).astype(BF16)
            dgt_ref[...] += jnp.sum(dx * mo_ref[...], axis=0, keepdims=True)

        @pl.when(i == n - 1)
        def _():
            t = dsc_ref[...]
            dg_ref[...] = t * scale
            dsc_ref[...] = t * gv

    in_specs = [_row_spec(tr, D), _row_spec(tr, D), _vec_spec(D), _vec_spec(D), _row_spec(tr, D)]
    args = [dh, x, g, sc, dres]
    out_specs = [_row_spec(tr, D), _vec_spec(D), _vec_spec(D), _vec_spec(D)]
    out_shape = [jax.ShapeDtypeStruct((S, D), F32)] + [jax.ShapeDtypeStruct((1, D), F32)] * 3
    if with_gate:
        in_specs += [_row_spec(tr, D), _vec_spec(D)]
        args += list(gated)
        out_specs += [_row_spec(tr, D), _vec_spec(D)]
        out_shape += [jax.ShapeDtypeStruct((S, D), BF16), jax.ShapeDtypeStruct((1, D), F32)]
    scratch = []
    aliases, wrap = _host(job, (n,), in_specs, args, out_specs, out_shape, scratch)
    return pl.pallas_call(
        wrap(body), name=name, grid=(n,), in_specs=in_specs, out_specs=out_specs, out_shape=out_shape,
        scratch_shapes=scratch, input_output_aliases=aliases, compiler_params=_params(("arbitrary",)),
    )(*args)


CONV_ROWS = 256


def _col_spec(S, off_cols):
    o = off_cols // HEAD
    return pl.BlockSpec((S, HEAD), lambda c, o=o: (0, c + o))


def _conv_taps(u, u_prev, rid):
    s1 = jnp.where(rid >= 1, pltpu.roll(u, 1, 0), pltpu.roll(u_prev, 1, 0))
    s2 = jnp.where(rid >= 2, pltpu.roll(u, 2, 0), pltpu.roll(u_prev, 2, 0))
    return s1, s2


def _conv_fwd(proj, conv_w, d_conv, job=None):
    S = proj.shape[0]
    R = min(CONV_ROWS, S)
    nr = S // R

    def body(ab_ref, ac_ref, ax_ref, w_ref, ya_ref):
        w0, w1, w2 = w_ref[0:1, :], w_ref[1:2, :], w_ref[2:3, :]
        rid = lax.broadcasted_iota(jnp.int32, (R, HEAD), 0)

        def step(c, carry):
            rows = pl.ds(pl.multiple_of(c * R, R), R)
            prev = pl.ds(pl.multiple_of(jnp.maximum(c - 1, 0) * R, R), R)
            u = ac_ref[rows, :] * ax_ref[rows, :]
            u_prev = jnp.where(c > 0, ac_ref[prev, :] * ax_ref[prev, :], 0.0)
            s1, s2 = _conv_taps(u, u_prev, rid)
            y = w0 * s2 + w1 * s1 + w2 * u
            ya_ref[rows, :] = (ab_ref[rows, :] * y).astype(BF16)
            return carry

        lax.fori_loop(0, nr, step, 0)

    grid = (d_conv // HEAD,)
    in_specs = [_col_spec(S, 0), _col_spec(S, d_conv), _col_spec(S, 2 * d_conv), pl.BlockSpec((3, HEAD), lambda c: (0, c))]
    args = [proj, proj, proj, conv_w]
    out_specs, out_shape = [pl.BlockSpec((S, HEAD), lambda c: (0, c))], [jax.ShapeDtypeStruct((S, d_conv), BF16)]
    scratch = []
    aliases, wrap = _host(job, grid, in_specs, args, out_specs, out_shape, scratch)
    res = pl.pallas_call(
        wrap(body), name="conv_fwd", grid=grid, in_specs=in_specs, out_specs=out_specs, out_shape=out_shape,
        scratch_shapes=scratch, input_output_aliases=aliases,
        compiler_params=_params(("parallel" if job is None else "arbitrary",)),
    )(*args)
    return res[0] if job is None else res


def _conv_bwd(dya, proj, conv_w, d_conv, job=None):
    S = proj.shape[0]
    R = min(CONV_ROWS, S)
    nr = S // R

    def body(dya_ref, ab_ref, ac_ref, ax_ref, w_ref, dab_ref, dac_ref, dax_ref, dw_ref):
        w0, w1, w2 = w_ref[0:1, :], w_ref[1:2, :], w_ref[2:3, :]
        rid = lax.broadcasted_iota(jnp.int32, (R, HEAD), 0)

        def step(c, carry):
            dw0, dw1, dw2 = carry
            rows = pl.ds(pl.multiple_of(c * R, R), R)
            prev = pl.ds(pl.multiple_of(jnp.maximum(c - 1, 0) * R, R), R)
            nxt = pl.ds(pl.multiple_of(jnp.minimum(c + 1, nr - 1) * R, R), R)
            a_c, a_x, a_b = ac_ref[rows, :], ax_ref[rows, :], ab_ref[rows, :]
            u = a_c * a_x
            u_prev = jnp.where(c > 0, ac_ref[prev, :] * ax_ref[prev, :], 0.0)
            s1, s2 = _conv_taps(u, u_prev, rid)
            y = w0 * s2 + w1 * s1 + w2 * u
            dy = dya_ref[rows, :]
            dab_ref[rows, :] = (dy * y).astype(BF16)
            dyc = dy * a_b
            dyc_next = jnp.where(c < nr - 1, dya_ref[nxt, :] * ab_ref[nxt, :], 0.0)
            t1 = jnp.where(rid < R - 1, pltpu.roll(dyc, R - 1, 0), pltpu.roll(dyc_next, R - 1, 0))
            t2 = jnp.where(rid < R - 2, pltpu.roll(dyc, R - 2, 0), pltpu.roll(dyc_next, R - 2, 0))
            du = w2 * dyc + w1 * t1 + w0 * t2
            dac_ref[rows, :] = (du * a_x).astype(BF16)
            dax_ref[rows, :] = (du * a_c).astype(BF16)
            dw0 = dw0 + jnp.sum(dyc * s2, axis=0, keepdims=True)
            dw1 = dw1 + jnp.sum(dyc * s1, axis=0, keepdims=True)
            dw2 = dw2 + jnp.sum(dyc * u, axis=0, keepdims=True)
            return dw0, dw1, dw2

        z = jnp.zeros((1, HEAD), F32)
        dw0, dw1, dw2 = lax.fori_loop(0, nr, step, (z, z, z))
        dw_ref[0:1, :] = dw0
        dw_ref[1:2, :] = dw1
        dw_ref[2:3, :] = dw2

    col = pl.BlockSpec((S, HEAD), lambda c: (0, c))
    grid = (d_conv // HEAD,)
    in_specs = [col, _col_spec(S, 0), _col_spec(S, d_conv), _col_spec(S, 2 * d_conv),
                pl.BlockSpec((3, HEAD), lambda c: (0, c))]
    args = [dya, proj, proj, proj, conv_w]
    out_specs = [col, col, col, pl.BlockSpec((3, HEAD), lambda c: (0, c))]
    out_shape = [jax.ShapeDtypeStruct((S, d_conv), BF16)] * 3 + [jax.ShapeDtypeStruct((3, d_conv), F32)]
    scratch = []
    aliases, wrap = _host(job, grid, in_specs, args, out_specs, out_shape, scratch)
    return pl.pallas_call(
        wrap(body), name="conv_bwd", grid=grid, in_specs=in_specs, out_specs=out_specs, out_shape=out_shape,
        scratch_shapes=scratch, input_output_aliases=aliases,
        compiler_params=_params(("parallel" if job is None else "arbitrary",)),
    )(*args)


HGRN_ROWS = 256
HGRN_FWD_SUB = 64
HGRN_BWD_SUB = 32
HGRN_GATE_ROWS = 128


def _block_cumsum(x, pos):
    for s in (1, 2, 4, 8):
        x = x + jnp.where(pos >= s, pltpu.roll(x, s, 0), 0.0)
    return x


def _block_rev_cumsum(x, pos, rows):
    for s in (1, 2, 4, 8):
        x = x + jnp.where(pos < BLK - s, pltpu.roll(x, rows - s, 0), 0.0)
    return x


def _block_last(x, pos, rows):
    m = jnp.where(pos == BLK - 1, x, 0.0)
    for s in (1, 2, 4, 8):
        m = m + pltpu.roll(m, rows - s, 0)
    return m


def _hgrn_gates(q, z, lb):
    sgq = _sig(q)
    qs = q * sgq
    sg = _sig(z)
    f = lb + (1.0 - lb) * sg
    kk = (1.0 - lb) * (1.0 - sg)
    return sgq, qs, sg, f, kk


def _hgrn_decays(q, z, lb, pos, rows):
    sgq, qs, sg, f, kk = _hgrn_gates(q, z, lb)
    b = _block_cumsum(jnp.log(f), pos)
    return sgq, qs, sg, f, kk, b, _block_last(b, pos, rows)


def _hgrn_specs(T, d_conv, n_t, rev):
    def t_of(i):
        return (n_t - 1 - i) if rev else i

    def pcol(off):
        o = off // HEAD
        return pl.BlockSpec((T, HEAD), lambda h, i, o=o: (t_of(i), h + o))

    q_spec, z_spec, v_spec, g_spec = pcol(3 * d_conv), pcol(4 * d_conv), pcol(5 * d_conv), pcol(6 * d_conv)
    lb_spec = pl.BlockSpec((1, HEAD), lambda h, i: (0, h))
    gn_spec = pl.BlockSpec((1, HEAD), lambda h, i: (0, 0))
    act_spec = pl.BlockSpec((T, HEAD), lambda h, i: (t_of(i), h))
    st_spec = pl.BlockSpec((None, T // BLK, HEAD, HEAD), lambda h, i: (h, t_of(i), 0, 0))
    return q_spec, z_spec, v_spec, g_spec, lb_spec, gn_spec, act_spec, st_spec


def _hgrn_fwd(proj, lb, gn, d_conv, job=None):
    S = proj.shape[0]
    H = d_conv // HEAD
    T = min(HGRN_ROWS, S)
    n_t, nb = S // T, T // BLK
    sub = min(HGRN_FWD_SUB, T)
    q_spec, z_spec, v_spec, g_spec, lb_spec, gn_spec, act_spec, st_spec = _hgrn_specs(T, d_conv, n_t, False)

    def body(q_ref, z_ref, v_ref, g_ref, lb_ref, gn_ref, ob_ref, o_ref, st_ref, qe_s, ke_s, dec_s, state, v_s, o_s, u_s):
        @pl.when(pl.program_id(1) == 0)
        def _():
            state[...] = jnp.zeros_like(state)

        pos = lax.broadcasted_iota(jnp.int32, (sub, HEAD), 0) % BLK
        lbv = lb_ref[...]

        def vector_pass(s, carry):
            r = pl.ds(pl.multiple_of(s * sub, sub), sub)
            v = v_ref[r, :]
            _, qs, _, _, kk, b, b_tot = _hgrn_decays(q_ref[r, :], z_ref[r, :], lbv, pos, sub)
            qe_s[r, :] = (qs * jnp.exp(b)).astype(BF16)
            ke_s[r, :] = (kk * jnp.exp(b_tot - b)).astype(BF16)
            v_s[r, :] = v.astype(BF16)
            dec_s[r, :] = jnp.exp(b_tot)
            o = jnp.sum(qs * kk, axis=-1, keepdims=True) * v
            for d in range(1, BLK):
                e = jnp.exp(b - pltpu.roll(b, d, 0))
                a = jnp.sum(qs * pltpu.roll(kk, d, 0) * e, axis=-1, keepdims=True)
                o = o + jnp.where(pos >= d, a * pltpu.roll(v, d, 0), 0.0)
            o_s[r, :] = o
            return carry

        lax.fori_loop(0, T // sub, vector_pass, 0)

        for j in range(nb):
            rows = pl.ds(j * BLK, BLK)
            u_s[j] = lax.dot_general(v_s[rows, :], ke_s[rows, :], (((0,), (0,)), ((), ())),
                                     preferred_element_type=F32)
        st = state[...]
        for j in range(nb):
            st_ref[j] = st.astype(BF16)
            st = st * dec_s[pl.ds(j * BLK, 1), :] + u_s[j]
        state[...] = st
        for j in range(nb):
            rows = pl.ds(j * BLK, BLK)
            o_s[rows, :] += lax.dot_general(qe_s[rows, :], st_ref[j], (((1,), (1,)), ((), ())),
                                            preferred_element_type=F32)
        o = o_s[...]
        o_ref[...] = o
        r = lax.rsqrt(jnp.mean(o * o, axis=-1, keepdims=True) + EPS)
        g = g_ref[...]
        ob_ref[...] = ((o * r) * gn_ref[...] * (g * _sig(g))).astype(BF16)

    grid = (H, n_t)
    in_specs = [q_spec, z_spec, v_spec, g_spec, lb_spec, gn_spec]
    args = [proj, proj, proj, proj, lb, gn]
    out_specs = [act_spec, act_spec, st_spec, act_spec, act_spec, act_spec]
    out_shape = [jax.ShapeDtypeStruct((S, d_conv), BF16), jax.ShapeDtypeStruct((S, d_conv), F32),
                 jax.ShapeDtypeStruct((H, S // BLK, HEAD, HEAD), BF16),
                 jax.ShapeDtypeStruct((S, d_conv), BF16), jax.ShapeDtypeStruct((S, d_conv), BF16),
                 jax.ShapeDtypeStruct((S, d_conv), F32)]
    scratch = [pltpu.VMEM((HEAD, HEAD), F32), pltpu.VMEM((T, HEAD), BF16), pltpu.VMEM((T, HEAD), F32),
               pltpu.VMEM((nb, HEAD, HEAD), F32)]
    aliases, wrap = _host(job, grid, in_specs, args, out_specs, out_shape, scratch)
    return pl.pallas_call(
        wrap(body), name="hgrn_fwd", grid=grid, in_specs=in_specs, out_specs=out_specs, out_shape=out_shape,
        scratch_shapes=scratch, input_output_aliases=aliases,
        compiler_params=_params(("parallel" if job is None else "arbitrary", "arbitrary")),
    )(*args)


def _hgrn_bwd(dob, o_raw, states, qe, ke, dec, proj, lb, gn, d_conv, job=None):
    S = proj.shape[0]
    H = d_conv // HEAD
    T = min(HGRN_ROWS, S)
    n_t, nb = S // T, T // BLK
    sub = min(HGRN_BWD_SUB, T)
    gate_rows = min(HGRN_GATE_ROWS, T)
    q_spec, z_spec, v_spec, g_spec, lb_spec, gn_spec, act_spec, st_spec = _hgrn_specs(T, d_conv, n_t, True)

    def body(dob_ref, o_ref, st_ref, qe_b, ke_b, dec_s, q_ref, z_ref, v_ref, g_ref, lb_ref, gn_ref,
             dq_ref, dz_ref, dv_ref, dg_ref, dlb_ref, dgn_ref,
             dstate, do_b, v_b, dqe_s, dke_s, dvi_s, ddec_s, do_s, u_s, ds_s):
        @pl.when(pl.program_id(1) == 0)
        def _():
            dstate[...] = jnp.zeros_like(dstate)
            dlb_ref[...] = jnp.zeros_like(dlb_ref)
            dgn_ref[...] = jnp.zeros_like(dgn_ref)

        pos = lax.broadcasted_iota(jnp.int32, (sub, HEAD), 0) % BLK
        lbv, gnv = lb_ref[...], gn_ref[...]

        def before(s, carry):
            r = pl.ds(pl.multiple_of(s * gate_rows, gate_rows), gate_rows)
            g = g_ref[r, :]
            v_b[r, :] = v_ref[r, :].astype(BF16)
            o = o_ref[r, :]
            rs = lax.rsqrt(jnp.mean(o * o, axis=-1, keepdims=True) + EPS)
            on = o * rs
            sgg = _sig(g)
            dobv = dob_ref[r, :]
            dog = dobv * (g * sgg)
            dgn_ref[...] += jnp.sum(dog * on, axis=0, keepdims=True)
            don = dog * gnv
            do = rs * (don - on * jnp.mean(don * on, axis=-1, keepdims=True))
            dg_ref[r, :] = (dobv * (on * gnv) * (sgg * (1.0 + g * (1.0 - sgg)))).astype(BF16)
            do_s[r, :] = do
            do_b[r, :] = do.astype(BF16)
            return carry

        lax.fori_loop(0, T // gate_rows, before, 0)

        for j in range(nb):
            rows = pl.ds(j * BLK, BLK)
            dob_j = do_b[rows, :]
            u_s[j] = lax.dot_general(dob_j, qe_b[rows, :], (((0,), (0,)), ((), ())), preferred_element_type=F32)
            dqe_s[rows, :] = lax.dot_general(dob_j, st_ref[j], (((1,), (0,)), ((), ())), preferred_element_type=F32)
        dst = dstate[...]
        for j in reversed(range(nb)):
            ds_s[j] = dst.astype(BF16)
            ddec = jnp.sum(st_ref[j].astype(F32) * dst, axis=0, keepdims=True)
            ddec_s[pl.ds(j * BLK, BLK), :] = jnp.broadcast_to(ddec, (BLK, HEAD))
            dst = dst * dec_s[pl.ds(j * BLK, 1), :] + u_s[j]
        dstate[...] = dst
        for j in range(nb):
            rows = pl.ds(j * BLK, BLK)
            dke_s[rows, :] = lax.dot_general(v_b[rows, :], ds_s[j], (((1,), (0,)), ((), ())), preferred_element_type=F32)
            dvi_s[rows, :] = lax.dot_general(ke_b[rows, :], ds_s[j], (((1,), (1,)), ((), ())), preferred_element_type=F32)

        def after(s, carry):
            r = pl.ds(pl.multiple_of(s * sub, sub), sub)
            q, v = q_ref[r, :], v_ref[r, :]
            sgq, qs, sg, f, kk, b, b_tot = _hgrn_decays(q, z_ref[r, :], lbv, pos, sub)
            do = do_s[r, :]
            dqs = dqe_s[r, :] * jnp.exp(b)
            dkk = dke_s[r, :] * jnp.exp(b_tot - b)
            d_tot = jnp.where(pos == BLK - 1, _block_cumsum(dkk * kk, pos) + ddec_s[r, :] * jnp.exp(b_tot), 0.0)
            dv = dvi_s[r, :]
            da = jnp.sum(do * v, axis=-1, keepdims=True)
            dv = dv + jnp.sum(qs * kk, axis=-1, keepdims=True) * do
            dqs = dqs + da * kk
            dkk = dkk + da * qs
            for d in range(1, BLK):
                kd = pltpu.roll(kk, d, 0)
                e = jnp.where(pos >= d, jnp.exp(b - pltpu.roll(b, d, 0)), 0.0)
                a = jnp.sum(qs * kd * e, axis=-1, keepdims=True)
                da = jnp.sum(do * pltpu.roll(v, d, 0), axis=-1, keepdims=True)
                gq = da * e
                dqs = dqs + gq * kd
                dkk = dkk + pltpu.roll(gq * qs, sub - d, 0)
                dv = dv + pltpu.roll(a * do, sub - d, 0)
            db = qs * dqs - kk * dkk + d_tot
            dlf = _block_rev_cumsum(db, pos, sub)
            df = dlf / f - dkk
            dlb_ref[...] += jnp.sum(df * (1.0 - sg), axis=0, keepdims=True)
            dz_ref[r, :] = (df * (1.0 - lbv) * sg * (1.0 - sg)).astype(BF16)
            dq_ref[r, :] = (dqs * (sgq * (1.0 + q * (1.0 - sgq)))).astype(BF16)
            dv_ref[r, :] = dv.astype(BF16)
            return carry

        lax.fori_loop(0, T // sub, after, 0)

    tb = lambda dt: pltpu.VMEM((T, HEAD), dt)
    grid = (H, n_t)
    in_specs = [act_spec, act_spec, st_spec, act_spec, act_spec, act_spec, q_spec, z_spec, v_spec, g_spec, lb_spec,
                gn_spec]
    args = [dob, o_raw, states, qe, ke, dec, proj, proj, proj, proj, lb, gn]
    out_specs = [act_spec, act_spec, act_spec, act_spec, lb_spec, pl.BlockSpec((None, 1, HEAD), lambda h, i: (h, 0, 0))]
    out_shape = ([jax.ShapeDtypeStruct((S, d_conv), BF16)] * 4
                 + [jax.ShapeDtypeStruct((1, d_conv), F32), jax.ShapeDtypeStruct((H, 1, HEAD), F32)])
    scratch = [pltpu.VMEM((HEAD, HEAD), F32), tb(BF16), tb(BF16), tb(F32), tb(F32), tb(F32), tb(F32), tb(F32),
               pltpu.VMEM((nb, HEAD, HEAD), F32), pltpu.VMEM((nb, HEAD, HEAD), BF16)]
    aliases, wrap = _host(job, grid, in_specs, args, out_specs, out_shape, scratch)
    return pl.pallas_call(
        wrap(body), name="hgrn_bwd", grid=grid, in_specs=in_specs, out_specs=out_specs, out_shape=out_shape,
        scratch_shapes=scratch, input_output_aliases=aliases,
        compiler_params=_params(("parallel" if job is None else "arbitrary", "arbitrary")),
    )(*args)


def _local_step(x, target, mod, norm_mix_g, lb, gnorm_g, norm_ffn_g, norm_final_g, conv_w, bufs, c_idx, order):
    S, D = x.shape
    d_conv = D // 2
    d_in = 7 * d_conv + 2 * D
    d_ff = N_CHIPS * bufs["w_ffn_down"].shape[1]
    nb_in, nb_co, nb_ff = bufs["w_in"].shape[2], bufs["w_conv_out"].shape[2], bufs["w_ffn_gate"].shape[2]
    rows = lambda g: g.reshape(-1, g.shape[2])
    sh_m, sc_m, gt_m, sh_f, sc_f, gt_f = [mod[:, i * D:(i + 1) * D] for i in range(6)]
    tm = _pick(S, 512, 16)
    tm2 = _pick(S, 1024, 16)
    tn_in = _pick(nb_in, 1408, 128)
    tn_co = _pick(nb_co, 512, 128)
    tn_ff = _pick(nb_ff, 1408, 128)
    tn_d = _pick(D, 512, 128)
    tw = _pick(D, 1024, 128)
    tg = _pick(D, 512, 128)

    h = _norm_mod("norm_mix", x, norm_mix_g, sc_m, sh_m)
    proj, w_in = _proj_gather(h, bufs["w_in"], order, None)
    ob, o_raw, states, qe, ke, dec, *got = _hgrn_fwd(
        proj, lb, gnorm_g, d_conv,
        job=_GatherJob([bufs[k] for k in ("w_conv_out", "w_hgrn_out", "w_o", "w_ffn_gate")]))
    ya, w_conv_out, w_hgrn_out, w_o, w_ffn_gate = _conv_fwd(proj, conv_w, d_conv, job=_ForwardJob(got))
    w_o = rows(w_o)

    def merge_epi(accs, ex):
        y_a, y_b = accs
        return [y_a, y_b, _sig(ex[0]) * y_a + _sig(ex[1]) * y_b]

    y_a, y_b, merged, w_ffn_up = _matmul(
        "branch_out", [(ya, 'n', w_conv_out, 'n'), (ob, 'n', w_hgrn_out, 'n')], [0, 1], S, D, d_conv, tm2, tn_co, d_conv,
        [(proj, 'tile', 7 * d_conv), (proj, 'tile', 7 * d_conv + D)], [(BF16, None), (BF16, None), (BF16, None)], merge_epi,
        job=_GatherJob([bufs["w_ffn_up"]], 0, 2), rows_outer=True)

    def resid_epi(accs, ex):
        return [accs[0], ex[0] + ex[1] * accs[0]]

    mo, x1, w_ffn_up = _matmul("w_o", [(merged, 'n', w_o, 'n')], [0], S, D, D, tm2, tw, D,
                               [(x, 'tile', 0), (gt_m, 'row', 0)], [(BF16, None), (F32, None)], resid_epi,
                               job=_GatherJob([w_ffn_up], 1, 2))
    h2, w_ffn_up = _norm_mod("norm_ffn", x1, norm_ffn_g, sc_f, sh_f, job=_ForwardJob([w_ffn_up]))

    def swiglu_epi(accs, ex):
        gg, uu = accs
        return [gg, uu, gg * _sig(gg) * uu]

    G, U, act, w_ffn_down = _matmul(
        "ffn_in", [(h2, 'n', w_ffn_gate, 'n'), (h2, 'n', w_ffn_up, 'n')], [0, 1], S, d_ff, D,
        tm, tn_ff, D, [], [(BF16, None), (BF16, None), (BF16, None)], swiglu_epi, job=_GatherJob([bufs["w_ffn_down"]]))
    w_ffn_down = rows(_forward_halves("forward_down", [w_ffn_down])[0])
    ff, x2 = _matmul("ffn_out", [(act, 'n', w_ffn_down, 'n')], [0], S, D, d_ff, tm2, tn_d, d_ff,
                     [(x1, 'tile', 0), (gt_f, 'row', 0)], [(BF16, None), (F32, None)], resid_epi, rows_outer=True)

    dx2, dffb, loss, dgt_f, dg_final = _loss_head(x2, target, norm_final_g, ff, gt_f)

    def swiglu_bwd_epi(accs, ex):
        dact, gg, uu = accs[0], ex[0], ex[1]
        s = _sig(gg)
        return [dact * uu * (s * (1.0 + gg * (1.0 - s))), dact * (gg * s)]

    dG, dU = _matmul("d_act", [(dffb, 'n', w_ffn_down, 't')], [0], S, d_ff, D, tm2, tn_ff, D,
                     [(G, 'tile', 0), (U, 'tile', 0)], [(BF16, None), (BF16, None)], swiglu_bwd_epi)
    dw_down, = _matmul("dw_ffn_down", [(act, 't', dffb, 'n')], [0], d_ff, D, S, _pick(d_ff, 512, 128),
                       D, S, [], [(BF16, None)], _plain)
    dh2, = _matmul("d_h2", [(dG, 'n', w_ffn_gate, 't'), (dU, 'n', w_ffn_up, 't')], [0, 0], S, D, d_ff,
                   tm, D, tn_ff, [], [(F32, None)], _plain)
    dw_gate, = _matmul("dw_ffn_gate", [(h2, 't', dG, 'n')], [0], D, d_ff, S, tg, tn_ff, S, [], [(BF16, nb_ff)], _plain)
    dw_up, = _matmul("dw_ffn_up", [(h2, 't', dU, 'n')], [0], D, d_ff, S, tg, tn_ff, S, [], [(BF16, nb_ff)], _plain)

    def pair_sums(names, grads, from_sibling):
        pairs = [_pair_sum("pair_sum_" + k, g, q, c_idx) for k, g, q in zip(names, grads, from_sibling)]
        return [p[0] for p in pairs], [p[1] for p in pairs]

    ffn_names = ["w_ffn_down", "w_ffn_gate", "w_ffn_up"]
    ffn_grads = [dw_down.reshape(N_CHIPS, -1, D), dw_gate, dw_up]
    dx1, dsh_f, dsc_f, dg_ffn, dmob, dgt_m, *from_sibling = _norm_mod_bwd(
        "norm_ffn_bwd", dh2, x1, norm_ffn_g, sc_f, dx2, (mo, gt_m), job=_SwapJob(ffn_grads))
    parts_f, slots_f = pair_sums(ffn_names, ffn_grads, from_sibling)

    def merge_bwd_epi(accs, ex):
        dm, ga, gb, ya_, yb_ = accs[0], ex[0], ex[1], ex[2], ex[3]
        sa, sb = _sig(ga), _sig(gb)
        return [dm * ya_ * sa * (1.0 - sa), dm * yb_ * sb * (1.0 - sb), dm * sa, dm * sb]

    dga, dgb, dy_a, dy_b = _matmul(
        "d_merged", [(dmob, 'n', w_o, 't')], [0], S, D, D, tm2, tn_co, D,
        [(proj, 'tile', 7 * d_conv), (proj, 'tile', 7 * d_conv + D), (y_a, 'tile', 0), (y_b, 'tile', 0)],
        [(BF16, None)] * 4, merge_bwd_epi, rows_outer=True)
    dw_o, = _matmul("dw_o", [(merged, 't', dmob, 'n')], [0], D, D, S, tg, D, S, [], [(BF16, None)], _plain)
    dya, dob = _matmul("d_branch", [(dy_a, 'n', w_conv_out, 't'), (dy_b, 'n', w_hgrn_out, 't')], [0, 1], S, d_conv, D,
                       tm2, _pick(d_conv, 1024, 128), tn_co, [], [(F32, None), (F32, None)], _plain)
    dw_co, dw_ho = _matmul("dw_branch", [(ya, 't', dy_a, 'n'), (ob, 't', dy_b, 'n')], [0, 1], d_conv, D, S,
                           _pick(d_conv, 512, 128), tn_co, S, [], [(BF16, nb_co), (BF16, nb_co)], _plain)
    branch_names = ["w_conv_out", "w_hgrn_out", "w_o"]
    branch_grads = [dw_co, dw_ho, dw_o.reshape(N_CHIPS, -1, D)]
    dab, dac, dax, dconv_w, *from_sibling = _conv_bwd(dya, proj, conv_w, d_conv, job=_SwapJob(branch_grads))
    parts_b, slots_b = pair_sums(branch_names, branch_grads, from_sibling)
    dq, dz, dv, dgo, dlb, dgn, *arrived = _hgrn_bwd(dob, o_raw, states, qe, ke, dec, proj, lb, gnorm_g, d_conv,
                                                    job=_ScatterJob(parts_f + parts_b, slots_f + slots_b))
    dproj = jnp.concatenate([dab, dac, dax, dq, dz, dv, dgo, dga, dgb], axis=1)
    names_fb = ffn_names + branch_names
    halves = [_sum_chips("sum_chips_" + k, r, c_idx) for k, r in zip(names_fb, arrived)]
    dw_in, *whole = _matmul("dw_in", [(h, 't', dproj, 'n')], [0], D, d_in, S, tg, tn_in, S, [], [(BF16, nb_in)], _plain,
                            job=_ShareJob(halves))
    parts_i, slots_i = pair_sums(["w_in"], [dw_in], _swap_halves("swap_in", [dw_in]))
    dh, arrived_in = _matmul("d_h", [(dproj, 'n', w_in, 't')], [0], S, D, d_in, tm2, D, tn_in, [], [(F32, None)], _plain,
                             job=_ScatterJob(parts_i, slots_i))
    dx, dsh_m, dsc_m, dg_mix = _norm_mod_bwd("norm_mix_bwd", dh, x, norm_mix_g, sc_m, dx1)
    dmod = jnp.concatenate([dsh_m, dsc_m, dgt_m, dsh_f, dsc_f, dgt_f], axis=1)
    small = dict(dmod=dmod, dg_mix=dg_mix, dg_ffn=dg_ffn, dg_final=dg_final, dlb=dlb,
                 dgn=jnp.sum(dgn, axis=0), dconv_w=dconv_w)
    w_in_sum, = _share_halves([_sum_chips("sum_chips_w_in", arrived_in, c_idx)])
    big = dict(zip(names_fb, whole), w_in=w_in_sum)
    return loss, dx, small, big


def _adamw_math(w, g, m, v):
    m = ADAM_B1 * m + (1.0 - ADAM_B1) * g
    v = ADAM_B2 * v + (1.0 - ADAM_B2) * (g * g)
    m_hat = m / (1.0 - ADAM_B1 ** ADAM_STEP)
    v_hat = v / (1.0 - ADAM_B2 ** ADAM_STEP)
    delta = -ADAM_LR * (m_hat / (jnp.sqrt(v_hat) + ADAM_EPS) + ADAM_WD * w)
    return delta, m, v


def _adamw(name, w, g, m, v):
    R, C = w.shape
    tr = _pick(R, max(8, (256 * 1024) // C // 8 * 8), 8)
    spec = pl.BlockSpec((tr, C), lambda i: (i, 0))

    def body(w_ref, g_ref, m_ref, v_ref, go_ref, d_ref, mo_ref, vo_ref):
        gv = g_ref[...]
        d, m2, v2 = _adamw_math(w_ref[...], gv, m_ref[...], v_ref[...])
        go_ref[...] = gv
        d_ref[...] = d
        mo_ref[...] = m2
        vo_ref[...] = v2

    return pl.pallas_call(
        body, name=name, grid=(R // tr,), in_specs=[spec] * 4, out_specs=[spec] * 4,
        out_shape=[jax.ShapeDtypeStruct((R, C), F32)] * 4, compiler_params=_params(("parallel",)),
    )(w, g, m, v)


def _ada_update(c_act_t, dmod, w, m, v):
    R, C = w.shape
    B = dmod.shape[0]
    tr = _pick(R, 256, 8)
    tc = _pick(C, 1536, 128)
    spec = pl.BlockSpec((tr, tc), lambda i, j: (i, j))

    def body(ct_ref, dm_ref, w_ref, m_ref, v_ref, g_ref, d_ref, mo_ref, vo_ref):
        ct = ct_ref[...]
        dm = dm_ref[...]
        g = ct[:, 0:1] * dm[0:1, :]
        for b in range(1, B):
            g = g + ct[:, b:b + 1] * dm[b:b + 1, :]
        d, m2, v2 = _adamw_math(w_ref[...], g, m_ref[...], v_ref[...])
        g_ref[...] = g
        d_ref[...] = d
        mo_ref[...] = m2
        vo_ref[...] = v2

    return pl.pallas_call(
        body, name="ada_update", grid=(R // tr, C // tc),
        in_specs=[pl.BlockSpec((tr, B), lambda i, j: (i, 0)), pl.BlockSpec((B, tc), lambda i, j: (0, j)),
                  spec, spec, spec],
        out_specs=[spec] * 4, out_shape=[jax.ShapeDtypeStruct((R, C), F32)] * 4,
        compiler_params=_params(("parallel", "parallel")),
    )(c_act_t, dmod, w, m, v)


def _prep(c_all, lb_param):
    def body(c_ref, p_ref, ca_ref, cab_ref, lb_ref):
        cv = c_ref[...]
        ca = cv * _sig(cv)
        ca_ref[...] = ca
        cab_ref[...] = ca.astype(BF16)
        lb_ref[...] = _sig(p_ref[0:1, :] - p_ref[1:2, :])

    return pl.pallas_call(
        body, name="prep",
        out_shape=[jax.ShapeDtypeStruct(c_all.shape, F32), jax.ShapeDtypeStruct(c_all.shape, BF16),
                   jax.ShapeDtypeStruct((1, lb_param.shape[1]), F32)],
    )(c_all, lb_param)


def _small_reduce(parts):
    W = parts.shape[1]

    def body(p_ref, o_ref):
        acc = p_ref[0:1, :]
        for d in range(1, N_DEV):
            acc = acc + p_ref[d:d + 1, :]
        o_ref[...] = acc

    return pl.pallas_call(body, name="small_reduce", out_shape=jax.ShapeDtypeStruct((1, W), F32))(parts)


def _lb_grad(dlb, lb):
    def body(d_ref, lb_ref, o_ref):
        t = d_ref[...] * lb_ref[...] * (1.0 - lb_ref[...])
        o_ref[0:1, :] = t
        o_ref[1:2, :] = -t

    return pl.pallas_call(body, name="lb_grad", out_shape=jax.ShapeDtypeStruct((2, dlb.shape[1]), F32))(dlb, lb)


def _place():
    x, y, c = lax.axis_index("x"), lax.axis_index("y"), lax.axis_index("c")
    chips = [(1 - x, y), (x, 1 - y), (1 - x, 1 - y)]
    return x, y, c, chips


def _allgather_rows(name, v):
    m_per, n = v.shape

    def body(x_ref, out_ref, send_sems, recv_sems, local_sem):
        x, y, c, chips = _place()
        me, sibling = (x, y, c), (x, y, 1 - c)

        def rows(px, py, pc):
            return out_ref.at[pl.ds((4 * px + 2 * py + pc) * m_per, m_per), :]

        def copy(k, block, to, src=None):
            return pltpu.make_async_remote_copy(
                src_ref=rows(*block) if src is None else src, dst_ref=rows(*block),
                send_sem=send_sems.at[k], recv_sem=recv_sems.at[k], device_id=to, device_id_type=MESH)

        mine = pltpu.make_async_copy(x_ref, rows(*me), local_sem)
        mine.start()
        first = [copy(0, me, sibling, src=x_ref)]
        first += [copy(1 + j, me, (*chip, c), src=x_ref) for j, chip in enumerate(chips)]
        for cp in first:
            cp.start()
        passed = [copy(4 + j, (*chip, c), sibling) for j, chip in enumerate(chips)]
        for j, chip in enumerate(chips):
            copy(1 + j, (*chip, c), me).wait_recv()
            passed[j].start()
        copy(0, sibling, me).wait_recv()
        for j, chip in enumerate(chips):
            copy(4 + j, (*chip, 1 - c), me).wait_recv()
        for cp in first + passed:
            cp.wait_send()
        mine.wait()

    return pl.pallas_call(
        body, name=name, out_shape=jax.ShapeDtypeStruct((N_DEV * m_per, n), v.dtype),
        in_specs=[pl.BlockSpec(memory_space=pltpu.VMEM)], out_specs=pl.BlockSpec(memory_space=pltpu.VMEM),
        scratch_shapes=[pltpu.SemaphoreType.DMA((7,)), pltpu.SemaphoreType.DMA((7,)), pltpu.SemaphoreType.DMA],
    )(v)


def _cast_place(name, w, k_idx):
    R, C = w.shape
    tr = _pick(R, max(16, (512 * 1024) // C // 16 * 16), 16)

    def body(k_ref, w_ref, o_ref):
        o_ref[...] = w_ref[...].astype(BF16)

    return pl.pallas_call(
        body, name=name,
        grid_spec=pltpu.PrefetchScalarGridSpec(
            num_scalar_prefetch=1, grid=(R // tr,),
            in_specs=[pl.BlockSpec((tr, C), lambda i, k_ref: (i, 0))],
            out_specs=pl.BlockSpec((None, tr, C), lambda i, k_ref: (k_ref[0], i, 0))),
        out_shape=jax.ShapeDtypeStruct((N_CHIPS, R, C), BF16),
        compiler_params=_params(("parallel",)),
    )(k_idx, w)


def _chip_copies(src_of, dst_of, got_of, n, sems, receiving):
    x, y, c, chips = _place()
    k_me = 2 * x + y
    send_sems, recv_sems = sems
    out = []
    for a in range(n):
        for j, chip in enumerate(chips):
            k_chip = 2 * chip[0] + chip[1]
            if receiving:
                src = dst = got_of(a, k_chip, c)
                to = (x, y, c)
            else:
                src, dst, to = src_of(a, k_chip, k_me, c), dst_of(a, k_me, c), (*chip, c)
            out.append(pltpu.make_async_remote_copy(
                src_ref=src, dst_ref=dst, send_sem=send_sems.at[3 * a + j], recv_sem=recv_sems.at[3 * a + j],
                device_id=to, device_id_type=MESH))
    return out


class _ChipJob:
    def start(self, j_in, j_out, sems):
        for send in self.copies(j_in, j_out, sems, False):
            send.start()

    def finish(self, j_in, j_out, sems):
        for recv in self.copies(j_in, j_out, sems, True):
            recv.wait_recv()
        for send in self.copies(j_in, j_out, sems, False):
            send.wait_send()


class _GatherJob(_ChipJob):
    def __init__(self, bufs, part=0, n_parts=1):
        n = len(bufs)
        self.inputs, self.n_alias = list(bufs), n
        self.sem_shapes = [pltpu.SemaphoreType.DMA((3 * n,)), pltpu.SemaphoreType.DMA((3 * n,))]
        self.half = [b.shape[1] // 2 for b in bufs]
        self.part, self.n_parts = part, n_parts

    def copies(self, j_in, j_out, sems, receiving):
        def half(a, k, c):
            rows = self.half[a] // self.n_parts
            return j_out[a].at[k, pl.ds(c * self.half[a] + self.part * rows, rows)]

        return _chip_copies(lambda a, k_chip, k_me, c: half(a, k_me, c), half, half, len(self.half), sems, receiving)


class _ScatterJob(_ChipJob):
    def __init__(self, parts, slots):
        n = len(parts)
        self.n = n
        self.inputs, self.n_alias = list(parts) + list(slots), n
        self.sem_shapes = [pltpu.SemaphoreType.DMA((3 * n,)), pltpu.SemaphoreType.DMA((3 * n,))]

    def copies(self, j_in, j_out, sems, receiving):
        return _chip_copies(lambda a, k_chip, k_me, c: j_in[a].at[k_chip], lambda a, k_me, c: j_out[a].at[k_me],
                            lambda a, k_chip, c: j_out[a].at[k_chip], self.n, sems, receiving)


class _SwapJob(_ChipJob):
    def __init__(self, grads):
        n = len(grads)
        self.n = n
        self.half = [g.shape[1] // 2 for g in grads]
        landing = [lax.empty((N_CHIPS, g.shape[1] // 2, g.shape[2]), g.dtype) for g in grads]
        self.inputs, self.n_alias = list(grads) + landing, n
        self.sem_shapes = [pltpu.SemaphoreType.DMA((n,)), pltpu.SemaphoreType.DMA((n,))]

    def copies(self, j_in, j_out, sems, receiving):
        x, y, c, _ = _place()
        out = []
        for a in range(self.n):
            if receiving:
                src, to = j_out[a], (x, y, c)
            else:
                src, to = j_in[a].at[:, pl.ds((1 - c) * self.half[a], self.half[a])], (x, y, 1 - c)
            out.append(pltpu.make_async_remote_copy(src_ref=src, dst_ref=j_out[a], send_sem=sems[0].at[a],
                                                    recv_sem=sems[1].at[a], device_id=to, device_id_type=MESH))
        return out


class _ShareJob(_ChipJob):
    def __init__(self, bufs):
        n = len(bufs)
        self.n = n
        self.inputs, self.n_alias = list(bufs), n
        self.sem_shapes = [pltpu.SemaphoreType.DMA((n,)), pltpu.SemaphoreType.DMA((n,))]

    def copies(self, j_in, j_out, sems, receiving):
        x, y, c, _ = _place()
        out = []
        for a in range(self.n):
            hc, to = (1 - c, (x, y, c)) if receiving else (c, (x, y, 1 - c))
            out.append(pltpu.make_async_remote_copy(
                src_ref=j_out[a].at[hc], dst_ref=j_out[a].at[hc], send_sem=sems[0].at[a], recv_sem=sems[1].at[a],
                device_id=to, device_id_type=MESH))
        return out


class _ForwardJob(_ChipJob):
    def __init__(self, bufs):
        n = len(bufs)
        self.n = n
        self.half = [b.shape[1] // 2 for b in bufs]
        self.inputs, self.n_alias = list(bufs), n
        self.sem_shapes = [pltpu.SemaphoreType.DMA((3 * n,)), pltpu.SemaphoreType.DMA((3 * n,))]

    def copies(self, j_in, j_out, sems, receiving):
        x, y, c, chips = _place()
        out = []
        for a in range(self.n):
            for q, chip in enumerate(chips):
                hc, to = (1 - c, (x, y, c)) if receiving else (c, (x, y, 1 - c))
                part = j_out[a].at[2 * chip[0] + chip[1], pl.ds(hc * self.half[a], self.half[a])]
                out.append(pltpu.make_async_remote_copy(
                    src_ref=part, dst_ref=part, send_sem=sems[0].at[3 * a + q], recv_sem=sems[1].at[3 * a + q],
                    device_id=to, device_id_type=MESH))
        return out


def _forward_halves(name, bufs):
    n = len(bufs)

    def body(*refs):
        out_refs = refs[n:2 * n]
        send_sems, recv_sems = refs[2 * n:]
        x, y, c, chips = _place()
        started, waits = [], []
        for a in range(n):
            rh = bufs[a].shape[1] // 2
            for j, chip in enumerate(chips):
                k_chip = 2 * chip[0] + chip[1]
                got = out_refs[a].at[k_chip, pl.ds(c * rh, rh)]
                cp = pltpu.make_async_remote_copy(src_ref=got, dst_ref=got, send_sem=send_sems.at[3 * a + j],
                                                  recv_sem=recv_sems.at[3 * a + j], device_id=(x, y, 1 - c),
                                                  device_id_type=MESH)
                cp.start()
                started.append(cp)
                other = out_refs[a].at[k_chip, pl.ds((1 - c) * rh, rh)]
                waits.append(pltpu.make_async_remote_copy(
                    src_ref=other, dst_ref=other, send_sem=send_sems.at[3 * a + j], recv_sem=recv_sems.at[3 * a + j],
                    device_id=(x, y, c), device_id_type=MESH))
        for cp in waits:
            cp.wait_recv()
        for cp in started:
            cp.wait_send()

    return pl.pallas_call(
        body, name=name, out_shape=[jax.ShapeDtypeStruct(b.shape, b.dtype) for b in bufs],
        in_specs=[_HBM] * n, out_specs=[_HBM] * n, input_output_aliases={a: a for a in range(n)},
        scratch_shapes=[pltpu.SemaphoreType.DMA((3 * n,)), pltpu.SemaphoreType.DMA((3 * n,))],
    )(*bufs)


def _proj_gather(h, buf, order, job):
    S, D = h.shape
    nb = buf.shape[2]
    tm = _pick(S, 1024, 16)
    tn = _pick(nb, 1408, 128)
    per = nb // tn
    nj, ni = N_CHIPS * per, S // tm
    rh = D // 2

    def body(order_ref, h_ref, buf_in, proj_ref, buf_ref, wbuf, tile_sems, ici_send, ici_recv, d2d_send, d2d_recv):
        j, i = pl.program_id(0), pl.program_id(1)
        x, y, c, chips = _place()
        k_me = 2 * x + y

        def half(k, hc):
            return buf_ref.at[k, pl.ds(hc * rh, rh)]

        def ici(q, receiving):
            k_chip = 2 * chips[q][0] + chips[q][1]
            src, to = (half(k_chip, c), (x, y, c)) if receiving else (half(k_me, c), (*chips[q], c))
            return pltpu.make_async_remote_copy(src_ref=src, dst_ref=src, send_sem=ici_send.at[q],
                                                recv_sem=ici_recv.at[q], device_id=to, device_id_type=MESH)

        def d2d(q, receiving):
            k_chip = 2 * chips[q][0] + chips[q][1]
            src, to = (half(k_chip, 1 - c), (x, y, c)) if receiving else (half(k_chip, c), (x, y, 1 - c))
            return pltpu.make_async_remote_copy(src_ref=src, dst_ref=src, send_sem=d2d_send.at[q],
                                                recv_sem=d2d_recv.at[q], device_id=to, device_id_type=MESH)

        def tile_copy(t):
            col = pl.multiple_of((t % per) * tn, 128)
            return pltpu.make_async_copy(buf_ref.at[order_ref[t // per], :, pl.ds(col, tn)], wbuf.at[t % 2],
                                         tile_sems.at[t % 2])

        @pl.when(jnp.logical_and(j == 0, i == 0))
        def _():
            ici(0, False).start()
            ici(1, False).start()
            tile_copy(0).start()

        @pl.when(i == 0)
        def _():
            tile_copy(j).wait()
            for q in range(3):
                @pl.when(j + 1 == (q + 1) * per)
                def _():
                    ici(q, True).wait_recv()
                    d2d(q, False).start()
                    if q == 0:
                        ici(0, False).wait_send()
                        ici(1, False).wait_send()
                        ici(2, False).start()
                    d2d(q, True).wait_recv()

            @pl.when(j + 1 < nj)
            def _():
                tile_copy(j + 1).start()

        proj_ref[...] = jnp.dot(h_ref[...], wbuf[j % 2], preferred_element_type=F32)

        @pl.when(jnp.logical_and(j == nj - 1, i == ni - 1))
        def _():
            ici(2, False).wait_send()
            for q in range(3):
                d2d(q, False).wait_send()

    grid = (nj, ni)
    in_specs = [pl.BlockSpec((tm, D), lambda j, i, o: (i, 0)), _HBM]
    args = [h, buf]
    out_specs = [pl.BlockSpec((tm, tn), lambda j, i, o: (i, o[j // per] * per + j % per)), _HBM]
    out_shape = [jax.ShapeDtypeStruct((S, N_CHIPS * nb), F32), jax.ShapeDtypeStruct(buf.shape, buf.dtype)]
    scratch = [pltpu.VMEM((2, D, tn), BF16), pltpu.SemaphoreType.DMA((2,))] + [pltpu.SemaphoreType.DMA((3,))] * 4
    aliases, wrap = _host(job, grid, in_specs, args, out_specs, out_shape, scratch, n_prefetch=1)
    aliases[2] = 1
    return pl.pallas_call(
        wrap(body), name="proj",
        grid_spec=pltpu.PrefetchScalarGridSpec(num_scalar_prefetch=1, grid=grid, in_specs=in_specs, out_specs=out_specs,
                                               scratch_shapes=scratch),
        out_shape=out_shape, input_output_aliases=aliases, compiler_params=_params(("arbitrary", "arbitrary")),
    )(order, *args)


def _swap_halves(name, grads):
    n = len(grads)

    def body(*refs):
        in_refs, out_refs = refs[:n], refs[n:2 * n]
        send_sems, recv_sems = refs[2 * n:]
        x, y, c, _ = _place()
        cps = []
        for a in range(n):
            rh = grads[a].shape[1] // 2
            cp = pltpu.make_async_remote_copy(
                src_ref=in_refs[a].at[:, pl.ds((1 - c) * rh, rh)], dst_ref=out_refs[a],
                send_sem=send_sems.at[a], recv_sem=recv_sems.at[a], device_id=(x, y, 1 - c), device_id_type=MESH)
            cp.start()
            cps.append(cp)
        for cp in cps:
            cp.wait()

    return pl.pallas_call(
        body, name=name,
        out_shape=[jax.ShapeDtypeStruct((N_CHIPS, g.shape[1] // 2, g.shape[2]), g.dtype) for g in grads],
        in_specs=[_HBM] * n, out_specs=[_HBM] * n,
        scratch_shapes=[pltpu.SemaphoreType.DMA((n,)), pltpu.SemaphoreType.DMA((n,))],
    )(*grads)


def _pair_sum(name, g, q, c_idx):
    _, R, C = g.shape
    rh = R // 2
    tr = _pick(rh, max(16, (512 * 1024) // C // 16 * 16), 16)
    nh = rh // tr

    def body(c_ref, g_ref, q_ref, o_ref, o2_ref):
        s = (g_ref[...].astype(F32) + q_ref[...].astype(F32)).astype(BF16)
        o_ref[...] = s
        o2_ref[...] = s

    out_spec = pl.BlockSpec((None, tr, C), lambda k, i, c_ref: (k, i, 0))
    return pl.pallas_call(
        body, name=name,
        grid_spec=pltpu.PrefetchScalarGridSpec(
            num_scalar_prefetch=1, grid=(N_CHIPS, nh),
            in_specs=[pl.BlockSpec((None, tr, C), lambda k, i, c_ref: (k, c_ref[0] * nh + i, 0)),
                      pl.BlockSpec((None, tr, C), lambda k, i, c_ref: (k, i, 0))],
            out_specs=[out_spec, out_spec]),
        out_shape=[jax.ShapeDtypeStruct((N_CHIPS, rh, C), BF16)] * 2,
        compiler_params=_params(("parallel", "parallel")),
    )(c_idx, g, q)


def _sum_chips(name, r, c_idx):
    _, rh, C = r.shape
    tr = _pick(rh, max(16, (256 * 1024) // C // 16 * 16), 16)

    def body(c_ref, r_ref, o_ref):
        acc = r_ref[0].astype(F32)
        for k in range(1, N_CHIPS):
            acc = acc + r_ref[k].astype(F32)
        o_ref[...] = acc

    return pl.pallas_call(
        body, name=name,
        grid_spec=pltpu.PrefetchScalarGridSpec(
            num_scalar_prefetch=1, grid=(rh // tr,),
            in_specs=[pl.BlockSpec((N_CHIPS, tr, C), lambda i, c_ref: (0, i, 0))],
            out_specs=pl.BlockSpec((None, tr, C), lambda i, c_ref: (c_ref[0], i, 0))),
        out_shape=jax.ShapeDtypeStruct((2, rh, C), F32), compiler_params=_params(("parallel",)),
    )(c_idx, r)


def _share_halves(bufs):
    n = len(bufs)

    def body(*refs):
        out_refs = refs[n:2 * n]
        send_sems, recv_sems = refs[2 * n:]
        x, y, c, _ = _place()
        cps = []
        for a in range(n):
            cp = pltpu.make_async_remote_copy(
                src_ref=out_refs[a].at[c], dst_ref=out_refs[a].at[c], send_sem=send_sems.at[a],
                recv_sem=recv_sems.at[a], device_id=(x, y, 1 - c), device_id_type=MESH)
            cp.start()
            cps.append(cp)
        for a, cp in enumerate(cps):
            got = out_refs[a].at[1 - c]
            pltpu.make_async_remote_copy(src_ref=got, dst_ref=got, send_sem=send_sems.at[a], recv_sem=recv_sems.at[a],
                                         device_id=(x, y, c), device_id_type=MESH).wait_recv()
        for cp in cps:
            cp.wait_send()

    return pl.pallas_call(
        body, name="share_halves",
        out_shape=[jax.ShapeDtypeStruct(b.shape, b.dtype) for b in bufs],
        in_specs=[_HBM] * n, out_specs=[_HBM] * n, input_output_aliases={a: a for a in range(n)},
        scratch_shapes=[pltpu.SemaphoreType.DMA((n,)), pltpu.SemaphoreType.DMA((n,))],
    )(*bufs)


_BIG = ("w_in", "w_conv_out", "w_hgrn_out", "w_o", "w_ffn_gate", "w_ffn_up", "w_ffn_down")
_ROW_SHARDED = ("w_o", "w_ffn_down")
_WEIGHTS = ("w_ada", "b_ada", "norm_mix_g", "w_in", "conv_w", "lb_param", "gnorm_g", "w_conv_out", "w_hgrn_out",
            "w_o", "norm_ffn_g", "w_ffn_gate", "w_ffn_up", "w_ffn_down", "norm_final_g")


def _pack_rows(pieces):
    flat = jnp.concatenate([p.reshape(-1) for p in pieces])
    pad = (-flat.shape[0]) % 1024
    return jnp.pad(flat, (0, pad)).reshape(8, -1)


def kernel(x, c, w_ada, b_ada, norm_mix_g, w_in, conv_w, lb_param, gnorm_g, w_conv_out, w_hgrn_out, w_o, norm_ffn_g, w_ffn_gate, w_ffn_up, w_ffn_down, norm_final_g, loss_target, m_w_ada, m_b_ada, m_norm_mix_g, m_w_in, m_conv_w, m_lb_param, m_gnorm_g, m_w_conv_out, m_w_hgrn_out, m_w_o, m_norm_ffn_g, m_w_ffn_gate, m_w_ffn_up, m_w_ffn_down, m_norm_final_g, v_w_ada, v_b_ada, v_norm_mix_g, v_w_in, v_conv_w, v_lb_param, v_gnorm_g, v_w_conv_out, v_w_hgrn_out, v_w_o, v_norm_ffn_g, v_w_ffn_gate, v_w_ffn_up, v_w_ffn_down, v_norm_final_g):
    w = dict(w_ada=w_ada, b_ada=b_ada, norm_mix_g=norm_mix_g, w_in=w_in, conv_w=conv_w, lb_param=lb_param,
             gnorm_g=gnorm_g, w_conv_out=w_conv_out, w_hgrn_out=w_hgrn_out, w_o=w_o, norm_ffn_g=norm_ffn_g,
             w_ffn_gate=w_ffn_gate, w_ffn_up=w_ffn_up, w_ffn_down=w_ffn_down, norm_final_g=norm_final_g)
    m = dict(w_ada=m_w_ada, b_ada=m_b_ada, norm_mix_g=m_norm_mix_g, w_in=m_w_in, conv_w=m_conv_w, lb_param=m_lb_param,
             gnorm_g=m_gnorm_g, w_conv_out=m_w_conv_out, w_hgrn_out=m_w_hgrn_out, w_o=m_w_o, norm_ffn_g=m_norm_ffn_g,
             w_ffn_gate=m_w_ffn_gate, w_ffn_up=m_w_ffn_up, w_ffn_down=m_w_ffn_down, norm_final_g=m_norm_final_g)
    v = dict(w_ada=v_w_ada, b_ada=v_b_ada, norm_mix_g=v_norm_mix_g, w_in=v_w_in, conv_w=v_conv_w, lb_param=v_lb_param,
             gnorm_g=v_gnorm_g, w_conv_out=v_w_conv_out, w_hgrn_out=v_w_hgrn_out, w_o=v_w_o, norm_ffn_g=v_norm_ffn_g,
             w_ffn_gate=v_w_ffn_gate, w_ffn_up=v_w_ffn_up, w_ffn_down=v_w_ffn_down, norm_final_g=v_norm_final_g)
    two_d = lambda a: a.reshape((-1, a.shape[-1])) if a.ndim != 2 else a
    shapes = {k: a.shape for k, a in w.items()}
    w, m, v = ({k: two_d(a) for k, a in t.items()} for t in (w, m, v))
    w["lb_param"], m["lb_param"], v["lb_param"] = lb_param, m_lb_param, v_lb_param
    S, D = x.shape[1], x.shape[2]
    d_conv = D // 2
    ix, iy, ic = lax.axis_index("x"), lax.axis_index("y"), lax.axis_index("c")
    k_me = 2 * ix + iy
    dev = 2 * k_me + ic
    cw_shard = w["conv_w"].shape[1]
    ada_cols = w["w_ada"].shape[1]

    got = _allgather_rows("gather_cond", _pack_rows([c, w["conv_w"]])).reshape(N_DEV, -1)
    c_all = got[:, :D]
    conv_full = got[::2, D:D + 3 * cw_shard].reshape(N_CHIPS, 3, cw_shard).transpose(1, 0, 2).reshape(3, -1)
    c_act, c_act_b, lb = _prep(c_all, lb_param)
    b_cols = lax.dynamic_slice(w["b_ada"], (0, k_me * ada_cols), (1, ada_cols))
    mod_cols, = _matmul("ada_fwd", [(c_act_b, 'n', w["w_ada"].astype(BF16), 'n')], [0], N_DEV, ada_cols, D,
                        N_DEV, _pick(ada_cols, 1536, 128), D, [(b_cols, 'row', 0)], [(F32, None)],
                        lambda accs, ex: [accs[0] + ex[0]])
    mod_all = _allgather_rows("gather_mod", mod_cols).reshape(N_CHIPS, 2, N_DEV, ada_cols)[:, 0]
    mod_all = mod_all.transpose(1, 0, 2).reshape(N_DEV, -1)
    mod = lax.dynamic_slice(mod_all, (dev, 0), (1, mod_all.shape[1]))
    k_idx = jnp.reshape(k_me, (1,)).astype(jnp.int32)
    c_idx = jnp.reshape(ic, (1,)).astype(jnp.int32)
    bufs = {k: _cast_place("cast_" + k, w[k], k_idx) for k in _BIG}
    order = jnp.stack([k_me, 2 * (1 - ix) + iy, 2 * ix + (1 - iy), 2 * (1 - ix) + (1 - iy)]).astype(jnp.int32)

    loss, dx, small, arrived = _local_step(
        x[0], loss_target[0], mod, w["norm_mix_g"], lb, w["gnorm_g"], w["norm_ffn_g"], w["norm_final_g"], conv_full,
        bufs, c_idx, order)

    pieces = [small["dmod"], small["dg_mix"], small["dg_ffn"], small["dg_final"], small["dlb"], small["dgn"],
              small["dconv_w"], loss]
    sizes = [p.size for p in pieces]
    parts = _allgather_rows("gather_small", _pack_rows(pieces)).reshape(N_DEV, -1)
    total = _small_reduce(parts)
    offs = [0]
    for s in sizes:
        offs.append(offs[-1] + s)
    tot = [total[:, offs[i]:offs[i + 1]] for i in range(len(sizes))]
    dmod_all = parts[:, :sizes[0]]
    grads = {
        "b_ada": tot[0], "norm_mix_g": tot[1], "norm_ffn_g": tot[2], "norm_final_g": tot[3],
        "lb_param": _lb_grad(tot[4], lb), "gnorm_g": tot[5],
        "conv_w": lax.dynamic_slice(tot[6].reshape(3, -1), (0, k_me * cw_shard), (3, cw_shard)),
    }
    loss_out = tot[7][0, 0]

    for k in _BIG:
        grads[k] = arrived[k].reshape(-1, arrived[k].shape[-1])

    delta, new_m, new_v = {}, {}, {}
    dmod_cols = lax.dynamic_slice(dmod_all, (0, k_me * ada_cols), (N_DEV, ada_cols))
    grads["w_ada"], delta["w_ada"], new_m["w_ada"], new_v["w_ada"] = _ada_update(
        c_act.T, dmod_cols, w["w_ada"], m["w_ada"], v["w_ada"])
    for k in _WEIGHTS:
        if k != "w_ada":
            grads[k], delta[k], new_m[k], new_v[k] = _adamw("adamw_" + k, w[k], grads[k], m[k], v[k])
    outs = [loss_out, dx.reshape(x.shape)]
    for t in (grads, delta, new_m, new_v):
        outs += [t[k].reshape(shapes[k]) for k in _WEIGHTS]
    return tuple(outs)
```

```python
import functools

import jax
import jax.numpy as jnp
from jax import lax
from jax.experimental import pallas as pl
from jax.experimental.pallas import tpu as pltpu

F32 = jnp.float32
BF16 = jnp.bfloat16
MESH = pl.DeviceIdType.MESH

EPS = 1e-6
HEAD = 128
BLK = 16
N_CHIPS = 4
N_DEV = 8
VMEM_LIMIT_BYTES = 56 * 1024 * 1024

ADAM_LR = 0.001
ADAM_B1 = 0.9
ADAM_B2 = 0.999
ADAM_EPS = 1e-08
ADAM_WD = 0.01
ADAM_STEP = 10


def _pick(dim, pref, mult):
    if dim <= pref:
        return dim
    t = (pref // mult) * mult
    while t >= mult:
        if dim % t == 0:
            return t
        t -= mult
    return dim


def _sig(x):
    return 1.0 / (1.0 + jnp.exp(-x))


def _params(sem):
    return pltpu.CompilerParams(dimension_semantics=sem, vmem_limit_bytes=VMEM_LIMIT_BYTES)


def _gj(j, i, k):
    return j


def _gk(j, i, k):
    return k


def _wspec(arr, rt, ct, r_of, c_of):
    if arr.ndim == 2:
        return pl.BlockSpec((rt, ct), lambda j, i, k: (r_of(j, i, k), c_of(j, i, k)))
    per = arr.shape[2] // ct
    assert arr.shape[2] % ct == 0
    return pl.BlockSpec((None, rt, ct),
                        lambda j, i, k: (c_of(j, i, k) // per, r_of(j, i, k), c_of(j, i, k) % per))


_HBM = pl.BlockSpec(memory_space=pltpu.HBM)


def _host(job, grid, in_specs, args, out_specs, out_shape, scratch, n_prefetch=0):
    if job is None:
        return {}, (lambda body: body)
    n_in, n_out, n_scr = len(args), len(out_shape), len(scratch)
    n_job, n_alias = len(job.inputs), job.n_alias
    in_specs += [_HBM] * n_job
    args += job.inputs
    out_specs += [_HBM] * n_alias
    out_shape += [jax.ShapeDtypeStruct(a.shape, a.dtype) for a in job.inputs[n_job - n_alias:]]
    scratch += job.sem_shapes
    aliases = {n_prefetch + n_in + n_job - n_alias + t: n_out + t for t in range(n_alias)}

    def wrap(body):
        def hosted(*refs):
            pre, refs = refs[:n_prefetch], refs[n_prefetch:]
            ins, refs = refs[:n_in], refs[n_in:]
            j_in, refs = refs[:n_job], refs[n_job:]
            outs, refs = refs[:n_out], refs[n_out:]
            j_out, refs = refs[:n_alias], refs[n_alias:]
            scr, sems = refs[:n_scr], refs[n_scr:]
            first = functools.reduce(jnp.logical_and, [pl.program_id(d) == 0 for d in range(len(grid))])
            last = functools.reduce(jnp.logical_and, [pl.program_id(d) == grid[d] - 1 for d in range(len(grid))])

            @pl.when(first)
            def _():
                job.start(j_in, j_out, sems)

            body(*pre, *ins, *outs, *scr)

            @pl.when(last)
            def _():
                job.finish(j_in, j_out, sems)

        return hosted

    return aliases, wrap


EPILOGUE_COLS = 768


def _plain(accs, extras):
    return accs


def _matmul(name, pairs, acc_of, M, N, K, tm, tn, tk, extras, outs, epilogue, job=None, rows_outer=False):
    assert M % tm == 0 and N % tn == 0 and K % tk == 0, (name, M, N, K, tm, tn, tk)
    nj, ni, nk = N // tn, M // tm, K // tk
    n_pairs, n_extra, n_out = len(pairs), len(extras), len(outs)
    n_acc = max(acc_of) + 1
    in_specs, args, dims = [], [], []
    lhs_of, seen = [], {}
    for a, am, b, bm in pairs:
        if (id(a), am) not in seen:
            seen[id(a), am] = len(args)
            args.append(a)
            if am == 'n':
                in_specs.append(pl.BlockSpec((tm, tk), lambda j, i, k: (i, k)))
            else:
                in_specs.append(pl.BlockSpec((tk, tm), lambda j, i, k: (k, i)))
        lhs_of.append(seen[id(a), am])
    n_lhs = len(args)
    for a, am, b, bm in pairs:
        if bm == 'n':
            in_specs.append(_wspec(b, tk, tn, _gk, _gj))
        else:
            in_specs.append(_wspec(b, tn, tk, _gj, _gk))
        dims.append((((1 if am == 'n' else 0,), (0 if bm == 'n' else 1,)), ((), ())))
        args.append(b)
    for e, kind, off in extras:
        assert off % tn == 0, (name, off, tn)
        o = off // tn
        if kind == 'tile':
            in_specs.append(pl.BlockSpec((tm, tn), lambda j, i, k, o=o: (i, j + o)))
        else:
            in_specs.append(pl.BlockSpec((1, tn), lambda j, i, k, o=o: (0, j + o)))
        args.append(e)
    out_shape, out_specs = [], []
    for dt, nb in outs:
        if nb is None:
            out_shape.append(jax.ShapeDtypeStruct((M, N), dt))
            out_specs.append(pl.BlockSpec((tm, tn), lambda j, i, k: (i, j)))
        else:
            assert nb % tn == 0 and N % nb == 0
            per = nb // tn
            out_shape.append(jax.ShapeDtypeStruct((N // nb, M, nb), dt))
            out_specs.append(pl.BlockSpec((None, tm, tn), lambda j, i, k, per=per: (j // per, i, j % per)))

    def body(*refs):
        n_ab = n_lhs + n_pairs
        e_refs = refs[n_ab:n_ab + n_extra]
        o_refs = refs[n_ab + n_extra:n_ab + n_extra + n_out]
        acc_refs = refs[n_ab + n_extra + n_out:]
        def products(cols):
            sums = [None] * n_acc
            for p in range(n_pairs):
                b_ref = refs[n_lhs + p]
                b = b_ref[:, cols] if pairs[p][3] == 'n' else b_ref[cols, :]
                prod = lax.dot_general(refs[lhs_of[p]][...], b, dims[p], preferred_element_type=F32)
                a = acc_of[p]
                sums[a] = prod if sums[a] is None else sums[a] + prod
            return sums

        def finish(accs, cols):
            res = epilogue(accs, [e[:, cols] for e in e_refs])
            for o_ref, r in zip(o_refs, res):
                o_ref[:, cols] = r.astype(o_ref.dtype)

        if nk == 1 and epilogue is not _plain:
            for c0 in range(0, tn, EPILOGUE_COLS):
                cols = slice(c0, min(c0 + EPILOGUE_COLS, tn))
                finish(products(cols), cols)
            return
        whole = slice(0, tn)
        sums = products(whole)
        if nk == 1:
            finish(sums, whole)
        else:
            k = pl.program_id(2)
            targets = o_refs if sum_in_outs else acc_refs

            @pl.when(k == 0)
            def _():
                for a in range(n_acc):
                    targets[a][...] = sums[a]

            @pl.when(k > 0)
            def _():
                for a in range(n_acc):
                    targets[a][...] += sums[a]

            if not sum_in_outs:
                @pl.when(k == nk - 1)
                def _():
                    finish([acc_refs[a][...] for a in range(n_acc)], whole)

    sum_in_outs = epilogue is _plain and nk > 1 and n_out == n_acc and all(dt == F32 for dt, _ in outs)
    scratch = [pltpu.VMEM((tm, tn), F32) for _ in range(n_acc)] if nk > 1 and not sum_in_outs else []
    grid = (nj, ni, nk)
    if rows_outer:
        grid = (ni, nj, nk)
        swap = lambda spec: pl.BlockSpec(spec.block_shape, lambda i, j, k, f=spec.index_map: f(j, i, k))
        in_specs, out_specs = [swap(s) for s in in_specs], [swap(s) for s in out_specs]
    aliases, wrap = _host(job, grid, in_specs, args, out_specs, out_shape, scratch)
    sem = ("parallel", "parallel", "arbitrary") if job is None else ("arbitrary",) * 3
    return pl.pallas_call(
        wrap(body), name=name, grid=grid, in_specs=in_specs, out_specs=out_specs, out_shape=out_shape,
        scratch_shapes=scratch, input_output_aliases=aliases, compiler_params=_params(sem),
    )(*args)


def _row_spec(tr, d):
    return pl.BlockSpec((tr, d), lambda i: (i, 0))


def _vec_spec(d):
    return pl.BlockSpec((1, d), lambda i: (0, 0))


def _norm_mod(name, x, g, sc, sh, job=None):
    S, D = x.shape
    tr = _pick(S, 256, 8)

    def body(x_ref, g_ref, sc_ref, sh_ref, h_ref):
        xv = x_ref[...]
        r = lax.rsqrt(jnp.mean(xv * xv, axis=-1, keepdims=True) + EPS)
        h_ref[...] = ((xv * r) * g_ref[...] * (1.0 + sc_ref[...]) + sh_ref[...]).astype(BF16)

    grid = (S // tr,)
    in_specs = [_row_spec(tr, D), _vec_spec(D), _vec_spec(D), _vec_spec(D)]
    args = [x, g, sc, sh]
    out_specs, out_shape, scratch = [_row_spec(tr, D)], [jax.ShapeDtypeStruct((S, D), BF16)], []
    aliases, wrap = _host(job, grid, in_specs, args, out_specs, out_shape, scratch)
    res = pl.pallas_call(
        wrap(body), name=name, grid=grid, in_specs=in_specs, out_specs=out_specs, out_shape=out_shape,
        scratch_shapes=scratch, input_output_aliases=aliases,
        compiler_params=_params(("parallel" if job is None else "arbitrary",)),
    )(*args)
    return res[0] if job is None else res


def _loss_head(x2, target, g, ff, gt):
    S, D = x2.shape
    tr = _pick(S, 256, 8)

    def body(x_ref, t_ref, g_ref, ff_ref, gt_ref, dx_ref, dffb_ref, loss_ref, dgt_ref, dg_ref):
        i = pl.program_id(0)

        @pl.when(i == 0)
        def _():
            loss_ref[...] = jnp.zeros_like(loss_ref)
            dgt_ref[...] = jnp.zeros_like(dgt_ref)
            dg_ref[...] = jnp.zeros_like(dg_ref)

        xv = x_ref[...]
        gv = g_ref[...]
        r = lax.rsqrt(jnp.mean(xv * xv, axis=-1, keepdims=True) + EPS)
        xh = xv * r
        err = xh * gv - t_ref[...]
        loss_ref[...] += 0.5 * jnp.sum(jnp.mean(err * err, axis=-1, keepdims=True))
        dy = err * (1.0 / D)
        dg_ref[...] += jnp.sum(dy * xh, axis=0, keepdims=True)
        dxh = dy * gv
        dx = r * (dxh - xh * jnp.mean(dxh * xh, axis=-1, keepdims=True))
        dx_ref[...] = dx
        dffb_ref[...] = (dx * gt_ref[...]).astype(BF16)
        dgt_ref[...] += jnp.sum(dx * ff_ref[...], axis=0, keepdims=True)

    return pl.pallas_call(
        body, name="loss_head", grid=(S // tr,),
        in_specs=[_row_spec(tr, D), _row_spec(tr, D), _vec_spec(D), _row_spec(tr, D), _vec_spec(D)],
        out_specs=[_row_spec(tr, D), _row_spec(tr, D), pl.BlockSpec((1, 128), lambda i: (0, 0)),
                   _vec_spec(D), _vec_spec(D)],
        out_shape=[jax.ShapeDtypeStruct((S, D), F32), jax.ShapeDtypeStruct((S, D), BF16),
                   jax.ShapeDtypeStruct((1, 128), F32), jax.ShapeDtypeStruct((1, D), F32),
                   jax.ShapeDtypeStruct((1, D), F32)],
        compiler_params=_params(("arbitrary",)),
    )(x2, target, g, ff, gt)


def _norm_mod_bwd(name, dh, x, g, sc, dres, gated=None, job=None):
    S, D = x.shape
    tr = _pick(S, 256, 8)
    n = S // tr
    with_gate = gated is not None

    def body(*refs):
        if with_gate:
            dh_ref, x_ref, g_ref, sc_ref, dres_ref, mo_ref, gt_ref, dx_ref, dsh_ref, dsc_ref, dg_ref, dmob_ref, dgt_ref = refs
        else:
            dh_ref, x_ref, g_ref, sc_ref, dres_ref, dx_ref, dsh_ref, dsc_ref, dg_ref = refs
        i = pl.program_id(0)

        @pl.when(i == 0)
        def _():
            dsh_ref[...] = jnp.zeros_like(dsh_ref)
            dsc_ref[...] = jnp.zeros_like(dsc_ref)
            if with_gate:
                dgt_ref[...] = jnp.zeros_like(dgt_ref)

        xv = x_ref[...]
        dhv = dh_ref[...]
        gv = g_ref[...]
        scale = 1.0 + sc_ref[...]
        r = lax.rsqrt(jnp.mean(xv * xv, axis=-1, keepdims=True) + EPS)
        xh = xv * r
        dsh_ref[...] += jnp.sum(dhv, axis=0, keepdims=True)
        dsc_ref[...] += jnp.sum(dhv * xh, axis=0, keepdims=True)
        dxh = dhv * (gv * scale)
        dx = dres_ref[...] + r * (dxh - xh * jnp.mean(dxh * xh, axis=-1, keepdims=True))
        dx_ref[...] = dx
        if with_gate:
            dmob_ref[...] = (dx * gt_ref[...]).astype(BF16)
            dgt_ref[...] += jnp.sum(dx * mo_ref[...], axis=0, keepdims=True)

        @pl.when(i == n - 1)
        def _():
            t = dsc_ref[...]
            dg_ref[...] = t * scale
            dsc_ref[...] = t * gv

    in_specs = [_row_spec(tr, D), _row_spec(tr, D), _vec_spec(D), _vec_spec(D), _row_spec(tr, D)]
    args = [dh, x, g, sc, dres]
    out_specs = [_row_spec(tr, D), _vec_spec(D), _vec_spec(D), _vec_spec(D)]
    out_shape = [jax.ShapeDtypeStruct((S, D), F32)] + [jax.ShapeDtypeStruct((1, D), F32)] * 3
    if with_gate:
        in_specs += [_row_spec(tr, D), _vec_spec(D)]
        args += list(gated)
        out_specs += [_row_spec(tr, D), _vec_spec(D)]
        out_shape += [jax.ShapeDtypeStruct((S, D), BF16), jax.ShapeDtypeStruct((1, D), F32)]
    scratch = []
    aliases, wrap = _host(job, (n,), in_specs, args, out_specs, out_shape, scratch)
    return pl.pallas_call(
        wrap(body), name=name, grid=(n,), in_specs=in_specs, out_specs=out_specs, out_shape=out_shape,
        scratch_shapes=scratch, input_output_aliases=aliases, compiler_params=_params(("arbitrary",)),
    )(*args)


CONV_ROWS = 256


def _col_spec(S, off_cols):
    o = off_cols // HEAD
    return pl.BlockSpec((S, HEAD), lambda c, o=o: (0, c + o))


def _conv_taps(u, u_prev, rid):
    s1 = jnp.where(rid >= 1, pltpu.roll(u, 1, 0), pltpu.roll(u_prev, 1, 0))
    s2 = jnp.where(rid >= 2, pltpu.roll(u, 2, 0), pltpu.roll(u_prev, 2, 0))
    return s1, s2


def _conv_fwd(proj, conv_w, d_conv, job=None):
    S = proj.shape[0]
    R = min(CONV_ROWS, S)
    nr = S // R

    def body(ab_ref, ac_ref, ax_ref, w_ref, ya_ref):
        w0, w1, w2 = w_ref[0:1, :], w_ref[1:2, :], w_ref[2:3, :]
        rid = lax.broadcasted_iota(jnp.int32, (R, HEAD), 0)

        def step(c, carry):
            rows = pl.ds(pl.multiple_of(c * R, R), R)
            prev = pl.ds(pl.multiple_of(jnp.maximum(c - 1, 0) * R, R), R)
            u = ac_ref[rows, :] * ax_ref[rows, :]
            u_prev = jnp.where(c > 0, ac_ref[prev, :] * ax_ref[prev, :], 0.0)
            s1, s2 = _conv_taps(u, u_prev, rid)
            y = w0 * s2 + w1 * s1 + w2 * u
            ya_ref[rows, :] = (ab_ref[rows, :] * y).astype(BF16)
            return carry

        lax.fori_loop(0, nr, step, 0)

    grid = (d_conv // HEAD,)
    in_specs = [_col_spec(S, 0), _col_spec(S, d_conv), _col_spec(S, 2 * d_conv), pl.BlockSpec((3, HEAD), lambda c: (0, c))]
    args = [proj, proj, proj, conv_w]
    out_specs, out_shape = [pl.BlockSpec((S, HEAD), lambda c: (0, c))], [jax.ShapeDtypeStruct((S, d_conv), BF16)]
    scratch = []
    aliases, wrap = _host(job, grid, in_specs, args, out_specs, out_shape, scratch)
    res = pl.pallas_call(
        wrap(body), name="conv_fwd", grid=grid, in_specs=in_specs, out_specs=out_specs, out_shape=out_shape,
        scratch_shapes=scratch, input_output_aliases=aliases,
        compiler_params=_params(("parallel" if job is None else "arbitrary",)),
    )(*args)
    return res[0] if job is None else res


def _conv_bwd(dya, proj, conv_w, d_conv, job=None):
    S = proj.shape[0]
    R = min(CONV_ROWS, S)
    nr = S // R

    def body(dya_ref, ab_ref, ac_ref, ax_ref, w_ref, dab_ref, dac_ref, dax_ref, dw_ref):
        w0, w1, w2 = w_ref[0:1, :], w_ref[1:2, :], w_ref[2:3, :]
        rid = lax.broadcasted_iota(jnp.int32, (R, HEAD), 0)

        def step(c, carry):
            dw0, dw1, dw2 = carry
            rows = pl.ds(pl.multiple_of(c * R, R), R)
            prev = pl.ds(pl.multiple_of(jnp.maximum(c - 1, 0) * R, R), R)
            nxt = pl.ds(pl.multiple_of(jnp.minimum(c + 1, nr - 1) * R, R), R)
            a_c, a_x, a_b = ac_ref[rows, :], ax_ref[rows, :], ab_ref[rows, :]
            u = a_c * a_x
            u_prev = jnp.where(c > 0, ac_ref[prev, :] * ax_ref[prev, :], 0.0)
            s1, s2 = _conv_taps(u, u_prev, rid)
            y = w0 * s2 + w1 * s1 + w2 * u
            dy = dya_ref[rows, :]
            dab_ref[rows, :] = (dy * y).astype(BF16)
            dyc = dy * a_b
            dyc_next = jnp.where(c < nr - 1, dya_ref[nxt, :] * ab_ref[nxt, :], 0.0)
            t1 = jnp.where(rid < R - 1, pltpu.roll(dyc, R - 1, 0), pltpu.roll(dyc_next, R - 1, 0))
            t2 = jnp.where(rid < R - 2, pltpu.roll(dyc, R - 2, 0), pltpu.roll(dyc_next, R - 2, 0))
            du = w2 * dyc + w1 * t1 + w0 * t2
            dac_ref[rows, :] = (du * a_x).astype(BF16)
            dax_ref[rows, :] = (du * a_c).astype(BF16)
            dw0 = dw0 + jnp.sum(dyc * s2, axis=0, keepdims=True)
            dw1 = dw1 + jnp.sum(dyc * s1, axis=0, keepdims=True)
            dw2 = dw2 + jnp.sum(dyc * u, axis=0, keepdims=True)
            return dw0, dw1, dw2

        z = jnp.zeros((1, HEAD), F32)
        dw0, dw1, dw2 = lax.fori_loop(0, nr, step, (z, z, z))
        dw_ref[0:1, :] = dw0
        dw_ref[1:2, :] = dw1
        dw_ref[2:3, :] = dw2

    col = pl.BlockSpec((S, HEAD), lambda c: (0, c))
    grid = (d_conv // HEAD,)
    in_specs = [col, _col_spec(S, 0), _col_spec(S, d_conv), _col_spec(S, 2 * d_conv),
                pl.BlockSpec((3, HEAD), lambda c: (0, c))]
    args = [dya, proj, proj, proj, conv_w]
    out_specs = [col, col, col, pl.BlockSpec((3, HEAD), lambda c: (0, c))]
    out_shape = [jax.ShapeDtypeStruct((S, d_conv), BF16)] * 3 + [jax.ShapeDtypeStruct((3, d_conv), F32)]
    scratch = []
    aliases, wrap = _host(job, grid, in_specs, args, out_specs, out_shape, scratch)
    return pl.pallas_call(
        wrap(body), name="conv_bwd", grid=grid, in_specs=in_specs, out_specs=out_specs, out_shape=out_shape,
        scratch_shapes=scratch, input_output_aliases=aliases,
        compiler_params=_params(("parallel" if job is None else "arbitrary",)),
    )(*args)


HGRN_ROWS = 256
HGRN_FWD_SUB = 64
HGRN_BWD_SUB = 32
HGRN_GATE_ROWS = 128


def _block_cumsum(x, pos):
    for s in (1, 2, 4, 8):
        x = x + jnp.where(pos >= s, pltpu.roll(x, s, 0), 0.0)
    return x


def _block_rev_cumsum(x, pos, rows):
    for s in (1, 2, 4, 8):
        x = x + jnp.where(pos < BLK - s, pltpu.roll(x, rows - s, 0), 0.0)
    return x


def _block_last(x, pos, rows):
    m = jnp.where(pos == BLK - 1, x, 0.0)
    for s in (1, 2, 4, 8):
        m = m + pltpu.roll(m, rows - s, 0)
    return m


def _hgrn_gates(q, z, lb):
    sgq = _sig(q)
    qs = q * sgq
    sg = _sig(z)
    f = lb + (1.0 - lb) * sg
    kk = (1.0 - lb) * (1.0 - sg)
    return sgq, qs, sg, f, kk


def _hgrn_decays(q, z, lb, pos, rows):
    sgq, qs, sg, f, kk = _hgrn_gates(q, z, lb)
    b = _block_cumsum(jnp.log(f), pos)
    return sgq, qs, sg, f, kk, b, _block_last(b, pos, rows)


def _hgrn_specs(T, d_conv, n_t, rev):
    def t_of(i):
        return (n_t - 1 - i) if rev else i

    def pcol(off):
        o = off // HEAD
        return pl.BlockSpec((T, HEAD), lambda h, i, o=o: (t_of(i), h + o))

    q_spec, z_spec, v_spec, g_spec = pcol(3 * d_conv), pcol(4 * d_conv), pcol(5 * d_conv), pcol(6 * d_conv)
    lb_spec = pl.BlockSpec((1, HEAD), lambda h, i: (0, h))
    gn_spec = pl.BlockSpec((1, HEAD), lambda h, i: (0, 0))
    act_spec = pl.BlockSpec((T, HEAD), lambda h, i: (t_of(i), h))
    st_spec = pl.BlockSpec((None, T // BLK, HEAD, HEAD), lambda h, i: (h, t_of(i), 0, 0))
    return q_spec, z_spec, v_spec, g_spec, lb_spec, gn_spec, act_spec, st_spec


def _hgrn_fwd(proj, lb, gn, d_conv, job=None):
    S = proj.shape[0]
    H = d_conv // HEAD
    T = min(HGRN_ROWS, S)
    n_t, nb = S // T, T // BLK
    sub = min(HGRN_FWD_SUB, T)
    q_spec, z_spec, v_spec, g_spec, lb_spec, gn_spec, act_spec, st_spec = _hgrn_specs(T, d_conv, n_t, False)

    def body(q_ref, z_ref, v_ref, g_ref, lb_ref, gn_ref, ob_ref, o_ref, st_ref, qe_s, ke_s, dec_s, state, v_s, o_s, u_s):
        @pl.when(pl.program_id(1) == 0)
        def _():
            state[...] = jnp.zeros_like(state)

        pos = lax.broadcasted_iota(jnp.int32, (sub, HEAD), 0) % BLK
        lbv = lb_ref[...]

        def vector_pass(s, carry):
            r = pl.ds(pl.multiple_of(s * sub, sub), sub)
            v = v_ref[r, :]
            _, qs, _, _, kk, b, b_tot = _hgrn_decays(q_ref[r, :], z_ref[r, :], lbv, pos, sub)
            qe_s[r, :] = (qs * jnp.exp(b)).astype(BF16)
            ke_s[r, :] = (kk * jnp.exp(b_tot - b)).astype(BF16)
            v_s[r, :] = v.astype(BF16)
            dec_s[r, :] = jnp.exp(b_tot)
            o = jnp.sum(qs * kk, axis=-1, keepdims=True) * v
            for d in range(1, BLK):
                e = jnp.exp(b - pltpu.roll(b, d, 0))
                a = jnp.sum(qs * pltpu.roll(kk, d, 0) * e, axis=-1, keepdims=True)
                o = o + jnp.where(pos >= d, a * pltpu.roll(v, d, 0), 0.0)
            o_s[r, :] = o
            return carry

        lax.fori_loop(0, T // sub, vector_pass, 0)

        for j in range(nb):
            rows = pl.ds(j * BLK, BLK)
            u_s[j] = lax.dot_general(v_s[rows, :], ke_s[rows, :], (((0,), (0,)), ((), ())),
                                     preferred_element_type=F32)
        st = state[...]
        for j in range(nb):
            st_ref[j] = st.astype(BF16)
            st = st * dec_s[pl.ds(j * BLK, 1), :] + u_s[j]
        state[...] = st
        for j in range(nb):
            rows = pl.ds(j * BLK, BLK)
            o_s[rows, :] += lax.dot_general(qe_s[rows, :], st_ref[j], (((1,), (1,)), ((), ())),
                                            preferred_element_type=F32)
        o = o_s[...]
        o_ref[...] = o
        r = lax.rsqrt(jnp.mean(o * o, axis=-1, keepdims=True) + EPS)
        g = g_ref[...]
        ob_ref[...] = ((o * r) * gn_ref[...] * (g * _sig(g))).astype(BF16)

    grid = (H, n_t)
    in_specs = [q_spec, z_spec, v_spec, g_spec, lb_spec, gn_spec]
    args = [proj, proj, proj, proj, lb, gn]
    out_specs = [act_spec, act_spec, st_spec, act_spec, act_spec, act_spec]
    out_shape = [jax.ShapeDtypeStruct((S, d_conv), BF16), jax.ShapeDtypeStruct((S, d_conv), F32),
                 jax.ShapeDtypeStruct((H, S // BLK, HEAD, HEAD), BF16),
                 jax.ShapeDtypeStruct((S, d_conv), BF16), jax.ShapeDtypeStruct((S, d_conv), BF16),
                 jax.ShapeDtypeStruct((S, d_conv), F32)]
    scratch = [pltpu.VMEM((HEAD, HEAD), F32), pltpu.VMEM((T, HEAD), BF16), pltpu.VMEM((T, HEAD), F32),
               pltpu.VMEM((nb, HEAD, HEAD), F32)]
    aliases, wrap = _host(job, grid, in_specs, args, out_specs, out_shape, scratch)
    return pl.pallas_call(
        wrap(body), name="hgrn_fwd", grid=grid, in_specs=in_specs, out_specs=out_specs, out_shape=out_shape,
        scratch_shapes=scratch, input_output_aliases=aliases,
        compiler_params=_params(("parallel" if job is None else "arbitrary", "arbitrary")),
    )(*args)


def _hgrn_bwd(dob, o_raw, states, qe, ke, dec, proj, lb, gn, d_conv, job=None):
    S = proj.shape[0]
    H = d_conv // HEAD
    T = min(HGRN_ROWS, S)
    n_t, nb = S // T, T // BLK
    sub = min(HGRN_BWD_SUB, T)
    gate_rows = min(HGRN_GATE_ROWS, T)
    q_spec, z_spec, v_spec, g_spec, lb_spec, gn_spec, act_spec, st_spec = _hgrn_specs(T, d_conv, n_t, True)

    def body(dob_ref, o_ref, st_ref, qe_b, ke_b, dec_s, q_ref, z_ref, v_ref, g_ref, lb_ref, gn_ref,
             dq_ref, dz_ref, dv_ref, dg_ref, dlb_ref, dgn_ref,
             dstate, do_b, v_b, dqe_s, dke_s, dvi_s, ddec_s, do_s, u_s, ds_s):
        @pl.when(pl.program_id(1) == 0)
        def _():
            dstate[...] = jnp.zeros_like(dstate)
            dlb_ref[...] = jnp.zeros_like(dlb_ref)
            dgn_ref[...] = jnp.zeros_like(dgn_ref)

        pos = lax.broadcasted_iota(jnp.int32, (sub, HEAD), 0) % BLK
        lbv, gnv = lb_ref[...], gn_ref[...]

        def before(s, carry):
            r = pl.ds(pl.multiple_of(s * gate_rows, gate_rows), gate_rows)
            g = g_ref[r, :]
            v_b[r, :] = v_ref[r, :].astype(BF16)
            o = o_ref[r, :]
            rs = lax.rsqrt(jnp.mean(o * o, axis=-1, keepdims=True) + EPS)
            on = o * rs
            sgg = _sig(g)
            dobv = dob_ref[r, :]
            dog = dobv * (g * sgg)
            dgn_ref[...] += jnp.sum(dog * on, axis=0, keepdims=True)
            don = dog * gnv
            do = rs * (don - on * jnp.mean(don * on, axis=-1, keepdims=True))
            dg_ref[r, :] = (dobv * (on * gnv) * (sgg * (1.0 + g * (1.0 - sgg)))).astype(BF16)
            do_s[r, :] = do
            do_b[r, :] = do.astype(BF16)
            return carry

        lax.fori_loop(0, T // gate_rows, before, 0)

        for j in range(nb):
            rows = pl.ds(j * BLK, BLK)
            dob_j = do_b[rows, :]
            u_s[j] = lax.dot_general(dob_j, qe_b[rows, :], (((0,), (0,)), ((), ())), preferred_element_type=F32)
            dqe_s[rows, :] = lax.dot_general(dob_j, st_ref[j], (((1,), (0,)), ((), ())), preferred_element_type=F32)
        dst = dstate[...]
        for j in reversed(range(nb)):
            ds_s[j] = dst.astype(BF16)
            ddec = jnp.sum(st_ref[j].astype(F32) * dst, axis=0, keepdims=True)
            ddec_s[pl.ds(j * BLK, BLK), :] = jnp.broadcast_to(ddec, (BLK, HEAD))
            dst = dst * dec_s[pl.ds(j * BLK, 1), :] + u_s[j]
        dstate[...] = dst
        for j in range(nb):
            rows = pl.ds(j * BLK, BLK)
            dke_s[rows, :] = lax.dot_general(v_b[rows, :], ds_s[j], (((1,), (0,)), ((), ())), preferred_element_type=F32)
            dvi_s[rows, :] = lax.dot_general(ke_b[rows, :], ds_s[j], (((1,), (1,)), ((), ())), preferred_element_type=F32)

        def after(s, carry):
            r = pl.ds(pl.multiple_of(s * sub, sub), sub)
            q, v = q_ref[r, :], v_ref[r, :]
            sgq, qs, sg, f, kk, b, b_tot = _hgrn_decays(q, z_ref[r, :], lbv, pos, sub)
            do = do_s[r, :]
            dqs = dqe_s[r, :] * jnp.exp(b)
            dkk = dke_s[r, :] * jnp.exp(b_tot - b)
            d_tot = jnp.where(pos == BLK - 1, _block_cumsum(dkk * kk, pos) + ddec_s[r, :] * jnp.exp(b_tot), 0.0)
            dv = dvi_s[r, :]
            da = jnp.sum(do * v, axis=-1, keepdims=True)
            dv = dv + jnp.sum(qs * kk, axis=-1, keepdims=True) * do
            dqs = dqs + da * kk
            dkk = dkk + da * qs
            for d in range(1, BLK):
                kd = pltpu.roll(kk, d, 0)
                e = jnp.where(pos >= d, jnp.exp(b - pltpu.roll(b, d, 0)), 0.0)
                a = jnp.sum(qs * kd * e, axis=-1, keepdims=True)
                da = jnp.sum(do * pltpu.roll(v, d, 0), axis=-1, keepdims=True)
                gq = da * e
                dqs = dqs + gq * kd
                dkk = dkk + pltpu.roll(gq * qs, sub - d, 0)
                dv = dv + pltpu.roll(a * do, sub - d, 0)
            db = qs * dqs - kk * dkk + d_tot
            dlf = _block_rev_cumsum(db, pos, sub)
            df = dlf / f - dkk
            dlb_ref[...] += jnp.sum(df * (1.0 - sg), axis=0, keepdims=True)
            dz_ref[r, :] = (df * (1.0 - lbv) * sg * (1.0 - sg)).astype(BF16)
            dq_ref[r, :] = (dqs * (sgq * (1.0 + q * (1.0 - sgq)))).astype(BF16)
            dv_ref[r, :] = dv.astype(BF16)
            return carry

        lax.fori_loop(0, T // sub, after, 0)

    tb = lambda dt: pltpu.VMEM((T, HEAD), dt)
    grid = (H, n_t)
    in_specs = [act_spec, act_spec, st_spec, act_spec, act_spec, act_spec, q_spec, z_spec, v_spec, g_spec, lb_spec,
                gn_spec]
    args = [dob, o_raw, states, qe, ke, dec, proj, proj, proj, proj, lb, gn]
    out_specs = [act_spec, act_spec, act_spec, act_spec, lb_spec, pl.BlockSpec((None, 1, HEAD), lambda h, i: (h, 0, 0))]
    out_shape = ([jax.ShapeDtypeStruct((S, d_conv), BF16)] * 4
                 + [jax.ShapeDtypeStruct((1, d_conv), F32), jax.ShapeDtypeStruct((H, 1, HEAD), F32)])
    scratch = [pltpu.VMEM((HEAD, HEAD), F32), tb(BF16), tb(BF16), tb(F32), tb(F32), tb(F32), tb(F32), tb(F32),
               pltpu.VMEM((nb, HEAD, HEAD), F32), pltpu.VMEM((nb, HEAD, HEAD), BF16)]
    aliases, wrap = _host(job, grid, in_specs, args, out_specs, out_shape, scratch)
    return pl.pallas_call(
        wrap(body), name="hgrn_bwd", grid=grid, in_specs=in_specs, out_specs=out_specs, out_shape=out_shape,
        scratch_shapes=scratch, input_output_aliases=aliases,
        compiler_params=_params(("parallel" if job is None else "arbitrary", "arbitrary")),
    )(*args)


def _local_step(x, target, mod, norm_mix_g, lb, gnorm_g, norm_ffn_g, norm_final_g, conv_w, bufs, c_idx, order):
    S, D = x.shape
    d_conv = D // 2
    d_in = 7 * d_conv + 2 * D
    d_ff = N_CHIPS * bufs["w_ffn_down"].shape[1]
    nb_in, nb_co, nb_ff = bufs["w_in"].shape[2], bufs["w_conv_out"].shape[2], bufs["w_ffn_gate"].shape[2]
    rows = lambda g: g.reshape(-1, g.shape[2])
    sh_m, sc_m, gt_m, sh_f, sc_f, gt_f = [mod[:, i * D:(i + 1) * D] for i in range(6)]
    tm = _pick(S, 512, 16)
    tm2 = _pick(S, 1024, 16)
    tn_in = _pick(nb_in, 1408, 128)
    tn_co = _pick(nb_co, 512, 128)
    tn_ff = _pick(nb_ff, 1408, 128)
    tn_d = _pick(D, 512, 128)
    tw = _pick(D, 1024, 128)
    tg = _pick(D, 512, 128)

    h = _norm_mod("norm_mix", x, norm_mix_g, sc_m, sh_m)
    proj, w_in = _proj_gather(h, bufs["w_in"], order, None)
    ob, o_raw, states, qe, ke, dec, *got = _hgrn_fwd(
        proj, lb, gnorm_g, d_conv,
        job=_GatherJob([bufs[k] for k in ("w_conv_out", "w_hgrn_out", "w_o", "w_ffn_gate")]))
    ya, w_conv_out, w_hgrn_out, w_o, w_ffn_gate = _conv_fwd(proj, conv_w, d_conv, job=_ForwardJob(got))
    w_o = rows(w_o)

    def merge_epi(accs, ex):
        y_a, y_b = accs
        return [y_a, y_b, _sig(ex[0]) * y_a + _sig(ex[1]) * y_b]

    y_a, y_b, merged, w_ffn_up = _matmul(
        "branch_out", [(ya, 'n', w_conv_out, 'n'), (ob, 'n', w_hgrn_out, 'n')], [0, 1], S, D, d_conv, tm2, tn_co, d_conv,
        [(proj, 'tile', 7 * d_conv), (proj, 'tile', 7 * d_conv + D)], [(BF16, None), (BF16, None), (BF16, None)], merge_epi,
        job=_GatherJob([bufs["w_ffn_up"]], 0, 2), rows_outer=True)

    def resid_epi(accs, ex):
        return [accs[0], ex[0] + ex[1] * accs[0]]

    mo, x1, w_ffn_up = _matmul("w_o", [(merged, 'n', w_o, 'n')], [0], S, D, D, tm2, tw, D,
                               [(x, 'tile', 0), (gt_m, 'row', 0)], [(BF16, None), (F32, None)], resid_epi,
                               job=_GatherJob([w_ffn_up], 1, 2))
    h2, w_ffn_up = _norm_mod("norm_ffn", x1, norm_ffn_g, sc_f, sh_f, job=_ForwardJob([w_ffn_up]))

    def swiglu_epi(accs, ex):
        gg, uu = accs
        return [gg, uu, gg * _sig(gg) * uu]

    G, U, act, w_ffn_down = _matmul(
        "ffn_in", [(h2, 'n', w_ffn_gate, 'n'), (h2, 'n', w_ffn_up, 'n')], [0, 1], S, d_ff, D,
        tm2, tn_ff, D, [], [(BF16, None), (BF16, None), (BF16, None)], swiglu_epi, job=_GatherJob([bufs["w_ffn_down"]]))
    w_ffn_down = rows(_forward_halves("forward_down", [w_ffn_down])[0])
    ff, x2 = _matmul("ffn_out", [(act, 'n', w_ffn_down, 'n')], [0], S, D, d_ff, tm2, tn_d, d_ff,
                     [(x1, 'tile', 0), (gt_f, 'row', 0)], [(BF16, None), (F32, None)], resid_epi, rows_outer=True)

    dx2, dffb, loss, dgt_f, dg_final = _loss_head(x2, target, norm_final_g, ff, gt_f)

    def swiglu_bwd_epi(accs, ex):
        dact, gg, uu = accs[0], ex[0], ex[1]
        s = _sig(gg)
        return [dact * uu * (s * (1.0 + gg * (1.0 - s))), dact * (gg * s)]

    dG, dU = _matmul("d_act", [(dffb, 'n', w_ffn_down, 't')], [0], S, d_ff, D, tm2, tn_ff, D,
                     [(G, 'tile', 0), (U, 'tile', 0)], [(BF16, None), (BF16, None)], swiglu_bwd_epi)
    dw_down, = _matmul("dw_ffn_down", [(act, 't', dffb, 'n')], [0], d_ff, D, S, _pick(d_ff, 512, 128),
                       D, S, [], [(BF16, None)], _plain)
    dh2, = _matmul("d_h2", [(dG, 'n', w_ffn_gate, 't'), (dU, 'n', w_ffn_up, 't')], [0, 0], S, D, d_ff,
                   tm, D, tn_ff, [], [(F32, None)], _plain)
    dw_gate, = _matmul("dw_ffn_gate", [(h2, 't', dG, 'n')], [0], D, d_ff, S, tg, tn_ff, S, [], [(BF16, nb_ff)], _plain)
    dw_up, = _matmul("dw_ffn_up", [(h2, 't', dU, 'n')], [0], D, d_ff, S, tg, tn_ff, S, [], [(BF16, nb_ff)], _plain)

    def pair_sums(names, grads, from_sibling):
        pairs = [_pair_sum("pair_sum_" + k, g, q, c_idx) for k, g, q in zip(names, grads, from_sibling)]
        return [p[0] for p in pairs], [p[1] for p in pairs]

    ffn_names = ["w_ffn_down", "w_ffn_gate", "w_ffn_up"]
    ffn_grads = [dw_down.reshape(N_CHIPS, -1, D), dw_gate, dw_up]
    dx1, dsh_f, dsc_f, dg_ffn, dmob, dgt_m, *from_sibling = _norm_mod_bwd(
        "norm_ffn_bwd", dh2, x1, norm_ffn_g, sc_f, dx2, (mo, gt_m), job=_SwapJob(ffn_grads))
    parts_f, slots_f = pair_sums(ffn_names, ffn_grads, from_sibling)

    def merge_bwd_epi(accs, ex):
        dm, ga, gb, ya_, yb_ = accs[0], ex[0], ex[1], ex[2], ex[3]
        sa, sb = _sig(ga), _sig(gb)
        return [dm * ya_ * sa * (1.0 - sa), dm * yb_ * sb * (1.0 - sb), dm * sa, dm * sb]

    dga, dgb, dy_a, dy_b = _matmul(
        "d_merged", [(dmob, 'n', w_o, 't')], [0], S, D, D, tm2, tn_co, D,
        [(proj, 'tile', 7 * d_conv), (proj, 'tile', 7 * d_conv + D), (y_a, 'tile', 0), (y_b, 'tile', 0)],
        [(BF16, None)] * 4, merge_bwd_epi, rows_outer=True)
    dw_o, = _matmul("dw_o", [(merged, 't', dmob, 'n')], [0], D, D, S, tg, D, S, [], [(BF16, None)], _plain)
    dya, dob = _matmul("d_branch", [(dy_a, 'n', w_conv_out, 't'), (dy_b, 'n', w_hgrn_out, 't')], [0, 1], S, d_conv, D,
                       tm2, _pick(d_conv, 1024, 128), tn_co, [], [(F32, None), (F32, None)], _plain)
    dw_co, dw_ho = _matmul("dw_branch", [(ya, 't', dy_a, 'n'), (ob, 't', dy_b, 'n')], [0, 1], d_conv, D, S,
                           _pick(d_conv, 512, 128), tn_co, S, [], [(BF16, nb_co), (BF16, nb_co)], _plain)
    branch_names = ["w_conv_out", "w_hgrn_out", "w_o"]
    branch_grads = [dw_co, dw_ho, dw_o.reshape(N_CHIPS, -1, D)]
    dab, dac, dax, dconv_w, *from_sibling = _conv_bwd(dya, proj, conv_w, d_conv, job=_SwapJob(branch_grads))
    parts_b, slots_b = pair_sums(branch_names, branch_grads, from_sibling)
    dq, dz, dv, dgo, dlb, dgn, *arrived = _hgrn_bwd(dob, o_raw, states, qe, ke, dec, proj, lb, gnorm_g, d_conv,
                                                    job=_ScatterJob(parts_f + parts_b, slots_f + slots_b))
    dproj = jnp.concatenate([dab, dac, dax, dq, dz, dv, dgo, dga, dgb], axis=1)
    names_fb = ffn_names + branch_names
    halves = [_sum_chips("sum_chips_" + k, r, c_idx) for k, r in zip(names_fb, arrived)]
    dw_in, *whole = _matmul("dw_in", [(h, 't', dproj, 'n')], [0], D, d_in, S, tg, tn_in, S, [], [(BF16, nb_in)], _plain,
                            job=_ShareJob(halves))
    parts_i, slots_i = pair_sums(["w_in"], [dw_in], _swap_halves("swap_in", [dw_in]))
    dh, arrived_in = _matmul("d_h", [(dproj, 'n', w_in, 't')], [0], S, D, d_in, tm2, D, tn_in, [], [(F32, None)], _plain,
                             job=_ScatterJob(parts_i, slots_i))
    dx, dsh_m, dsc_m, dg_mix = _norm_mod_bwd("norm_mix_bwd", dh, x, norm_mix_g, sc_m, dx1)
    dmod = jnp.concatenate([dsh_m, dsc_m, dgt_m, dsh_f, dsc_f, dgt_f], axis=1)
    small = dict(dmod=dmod, dg_mix=dg_mix, dg_ffn=dg_ffn, dg_final=dg_final, dlb=dlb,
                 dgn=jnp.sum(dgn, axis=0), dconv_w=dconv_w)
    w_in_sum, = _share_halves([_sum_chips("sum_chips_w_in", arrived_in, c_idx)])
    big = dict(zip(names_fb, whole), w_in=w_in_sum)
    return loss, dx, small, big


def _adamw_math(w, g, m, v):
    m = ADAM_B1 * m + (1.0 - ADAM_B1) * g
    v = ADAM_B2 * v + (1.0 - ADAM_B2) * (g * g)
    m_hat = m / (1.0 - ADAM_B1 ** ADAM_STEP)
    v_hat = v / (1.0 - ADAM_B2 ** ADAM_STEP)
    delta = -ADAM_LR * (m_hat / (jnp.sqrt(v_hat) + ADAM_EPS) + ADAM_WD * w)
    return delta, m, v


def _adamw(name, w, g, m, v):
    R, C = w.shape
    tr = _pick(R, max(8, (256 * 1024) // C // 8 * 8), 8)
    spec = pl.BlockSpec((tr, C), lambda i: (i, 0))

    def body(w_ref, g_ref, m_ref, v_ref, go_ref, d_ref, mo_ref, vo_ref):
        gv = g_ref[...]
        d, m2, v2 = _adamw_math(w_ref[...], gv, m_ref[...], v_ref[...])
        go_ref[...] = gv
        d_ref[...] = d
        mo_ref[...] = m2
        vo_ref[...] = v2

    return pl.pallas_call(
        body, name=name, grid=(R // tr,), in_specs=[spec] * 4, out_specs=[spec] * 4,
        out_shape=[jax.ShapeDtypeStruct((R, C), F32)] * 4, compiler_params=_params(("parallel",)),
    )(w, g, m, v)


def _ada_update(c_act_t, dmod, w, m, v):
    R, C = w.shape
    B = dmod.shape[0]
    tr = _pick(R, 256, 8)
    tc = _pick(C, 1536, 128)
    spec = pl.BlockSpec((tr, tc), lambda i, j: (i, j))

    def body(ct_ref, dm_ref, w_ref, m_ref, v_ref, g_ref, d_ref, mo_ref, vo_ref):
        ct = ct_ref[...]
        dm = dm_ref[...]
        g = ct[:, 0:1] * dm[0:1, :]
        for b in range(1, B):
            g = g + ct[:, b:b + 1] * dm[b:b + 1, :]
        d, m2, v2 = _adamw_math(w_ref[...], g, m_ref[...], v_ref[...])
        g_ref[...] = g
        d_ref[...] = d
        mo_ref[...] = m2
        vo_ref[...] = v2

    return pl.pallas_call(
        body, name="ada_update", grid=(R // tr, C // tc),
        in_specs=[pl.BlockSpec((tr, B), lambda i, j: (i, 0)), pl.BlockSpec((B, tc), lambda i, j: (0, j)),
                  spec, spec, spec],
        out_specs=[spec] * 4, out_shape=[jax.ShapeDtypeStruct((R, C), F32)] * 4,
        compiler_params=_params(("parallel", "parallel")),
    )(c_act_t, dmod, w, m, v)


def _prep(c_all, lb_param):
    def body(c_ref, p_ref, ca_ref, cab_ref, lb_ref):
        cv = c_ref[...]
        ca = cv * _sig(cv)
        ca_ref[...] = ca
        cab_ref[...] = ca.astype(BF16)
        lb_ref[...] = _sig(p_ref[0:1, :] - p_ref[1:2, :])

    return pl.pallas_call(
        body, name="prep",
        out_shape=[jax.ShapeDtypeStruct(c_all.shape, F32), jax.ShapeDtypeStruct(c_all.shape, BF16),
                   jax.ShapeDtypeStruct((1, lb_param.shape[1]), F32)],
    )(c_all, lb_param)


def _small_reduce(parts):
    W = parts.shape[1]

    def body(p_ref, o_ref):
        acc = p_ref[0:1, :]
        for d in range(1, N_DEV):
            acc = acc + p_ref[d:d + 1, :]
        o_ref[...] = acc

    return pl.pallas_call(body, name="small_reduce", out_shape=jax.ShapeDtypeStruct((1, W), F32))(parts)


def _lb_grad(dlb, lb):
    def body(d_ref, lb_ref, o_ref):
        t = d_ref[...] * lb_ref[...] * (1.0 - lb_ref[...])
        o_ref[0:1, :] = t
        o_ref[1:2, :] = -t

    return pl.pallas_call(body, name="lb_grad", out_shape=jax.ShapeDtypeStruct((2, dlb.shape[1]), F32))(dlb, lb)


def _place():
    x, y, c = lax.axis_index("x"), lax.axis_index("y"), lax.axis_index("c")
    chips = [(1 - x, y), (x, 1 - y), (1 - x, 1 - y)]
    return x, y, c, chips


def _allgather_rows(name, v):
    m_per, n = v.shape

    def body(x_ref, out_ref, send_sems, recv_sems, local_sem):
        x, y, c, chips = _place()
        me, sibling = (x, y, c), (x, y, 1 - c)

        def rows(px, py, pc):
            return out_ref.at[pl.ds((4 * px + 2 * py + pc) * m_per, m_per), :]

        def copy(k, block, to, src=None):
            return pltpu.make_async_remote_copy(
                src_ref=rows(*block) if src is None else src, dst_ref=rows(*block),
                send_sem=send_sems.at[k], recv_sem=recv_sems.at[k], device_id=to, device_id_type=MESH)

        mine = pltpu.make_async_copy(x_ref, rows(*me), local_sem)
        mine.start()
        first = [copy(0, me, sibling, src=x_ref)]
        first += [copy(1 + j, me, (*chip, c), src=x_ref) for j, chip in enumerate(chips)]
        for cp in first:
            cp.start()
        passed = [copy(4 + j, (*chip, c), sibling) for j, chip in enumerate(chips)]
        for j, chip in enumerate(chips):
            copy(1 + j, (*chip, c), me).wait_recv()
            passed[j].start()
        copy(0, sibling, me).wait_recv()
        for j, chip in enumerate(chips):
            copy(4 + j, (*chip, 1 - c), me).wait_recv()
        for cp in first + passed:
            cp.wait_send()
        mine.wait()

    return pl.pallas_call(
        body, name=name, out_shape=jax.ShapeDtypeStruct((N_DEV * m_per, n), v.dtype),
        in_specs=[pl.BlockSpec(memory_space=pltpu.VMEM)], out_specs=pl.BlockSpec(memory_space=pltpu.VMEM),
        scratch_shapes=[pltpu.SemaphoreType.DMA((7,)), pltpu.SemaphoreType.DMA((7,)), pltpu.SemaphoreType.DMA],
    )(v)


def _cast_place(name, w, k_idx):
    R, C = w.shape
    tr = _pick(R, max(16, (512 * 1024) // C // 16 * 16), 16)

    def body(k_ref, w_ref, o_ref):
        o_ref[...] = w_ref[...].astype(BF16)

    return pl.pallas_call(
        body, name=name,
        grid_spec=pltpu.PrefetchScalarGridSpec(
            num_scalar_prefetch=1, grid=(R // tr,),
            in_specs=[pl.BlockSpec((tr, C), lambda i, k_ref: (i, 0))],
            out_specs=pl.BlockSpec((None, tr, C), lambda i, k_ref: (k_ref[0], i, 0))),
        out_shape=jax.ShapeDtypeStruct((N_CHIPS, R, C), BF16),
        compiler_params=_params(("parallel",)),
    )(k_idx, w)


def _chip_copies(src_of, dst_of, got_of, n, sems, receiving):
    x, y, c, chips = _place()
    k_me = 2 * x + y
    send_sems, recv_sems = sems
    out = []
    for a in range(n):
        for j, chip in enumerate(chips):
            k_chip = 2 * chip[0] + chip[1]
            if receiving:
                src = dst = got_of(a, k_chip, c)
                to = (x, y, c)
            else:
                src, dst, to = src_of(a, k_chip, k_me, c), dst_of(a, k_me, c), (*chip, c)
            out.append(pltpu.make_async_remote_copy(
                src_ref=src, dst_ref=dst, send_sem=send_sems.at[3 * a + j], recv_sem=recv_sems.at[3 * a + j],
                device_id=to, device_id_type=MESH))
    return out


class _ChipJob:
    def start(self, j_in, j_out, sems):
        for send in self.copies(j_in, j_out, sems, False):
            send.start()

    def finish(self, j_in, j_out, sems):
        for recv in self.copies(j_in, j_out, sems, True):
            recv.wait_recv()
        for send in self.copies(j_in, j_out, sems, False):
            send.wait_send()


class _GatherJob(_ChipJob):
    def __init__(self, bufs, part=0, n_parts=1):
        n = len(bufs)
        self.inputs, self.n_alias = list(bufs), n
        self.sem_shapes = [pltpu.SemaphoreType.DMA((3 * n,)), pltpu.SemaphoreType.DMA((3 * n,))]
        self.half = [b.shape[1] // 2 for b in bufs]
        self.part, self.n_parts = part, n_parts

    def copies(self, j_in, j_out, sems, receiving):
        def half(a, k, c):
            rows = self.half[a] // self.n_parts
            return j_out[a].at[k, pl.ds(c * self.half[a] + self.part * rows, rows)]

        return _chip_copies(lambda a, k_chip, k_me, c: half(a, k_me, c), half, half, len(self.half), sems, receiving)


class _ScatterJob(_ChipJob):
    def __init__(self, parts, slots):
        n = len(parts)
        self.n = n
        self.inputs, self.n_alias = list(parts) + list(slots), n
        self.sem_shapes = [pltpu.SemaphoreType.DMA((3 * n,)), pltpu.SemaphoreType.DMA((3 * n,))]

    def copies(self, j_in, j_out, sems, receiving):
        return _chip_copies(lambda a, k_chip, k_me, c: j_in[a].at[k_chip], lambda a, k_me, c: j_out[a].at[k_me],
                            lambda a, k_chip, c: j_out[a].at[k_chip], self.n, sems, receiving)


class _SwapJob(_ChipJob):
    def __init__(self, grads):
        n = len(grads)
        self.n = n
        self.half = [g.shape[1] // 2 for g in grads]
        landing = [lax.empty((N_CHIPS, g.shape[1] // 2, g.shape[2]), g.dtype) for g in grads]
        self.inputs, self.n_alias = list(grads) + landing, n
        self.sem_shapes = [pltpu.SemaphoreType.DMA((n,)), pltpu.SemaphoreType.DMA((n,))]

    def copies(self, j_in, j_out, sems, receiving):
        x, y, c, _ = _place()
        out = []
        for a in range(self.n):
            if receiving:
                src, to = j_out[a], (x, y, c)
            else:
                src, to = j_in[a].at[:, pl.ds((1 - c) * self.half[a], self.half[a])], (x, y, 1 - c)
            out.append(pltpu.make_async_remote_copy(src_ref=src, dst_ref=j_out[a], send_sem=sems[0].at[a],
                                                    recv_sem=sems[1].at[a], device_id=to, device_id_type=MESH))
        return out


class _ShareJob(_ChipJob):
    def __init__(self, bufs):
        n = len(bufs)
        self.n = n
        self.inputs, self.n_alias = list(bufs), n
        self.sem_shapes = [pltpu.SemaphoreType.DMA((n,)), pltpu.SemaphoreType.DMA((n,))]

    def copies(self, j_in, j_out, sems, receiving):
        x, y, c, _ = _place()
        out = []
        for a in range(self.n):
            hc, to = (1 - c, (x, y, c)) if receiving else (c, (x, y, 1 - c))
            out.append(pltpu.make_async_remote_copy(
                src_ref=j_out[a].at[hc], dst_ref=j_out[a].at[hc], send_sem=sems[0].at[a], recv_sem=sems[1].at[a],
                device_id=to, device_id_type=MESH))
        return out


class _ForwardJob(_ChipJob):
    def __init__(self, bufs):
        n = len(bufs)
        self.n = n
        self.half = [b.shape[1] // 2 for b in bufs]
        self.inputs, self.n_alias = list(bufs), n
        self.sem_shapes = [pltpu.SemaphoreType.DMA((3 * n,)), pltpu.SemaphoreType.DMA((3 * n,))]

    def copies(self, j_in, j_out, sems, receiving):
        x, y, c, chips = _place()
        out = []
        for a in range(self.n):
            for q, chip in enumerate(chips):
                hc, to = (1 - c, (x, y, c)) if receiving else (c, (x, y, 1 - c))
                part = j_out[a].at[2 * chip[0] + chip[1], pl.ds(hc * self.half[a], self.half[a])]
                out.append(pltpu.make_async_remote_copy(
                    src_ref=part, dst_ref=part, send_sem=sems[0].at[3 * a + q], recv_sem=sems[1].at[3 * a + q],
                    device_id=to, device_id_type=MESH))
        return out


def _forward_halves(name, bufs):
    n = len(bufs)

    def body(*refs):
        out_refs = refs[n:2 * n]
        send_sems, recv_sems = refs[2 * n:]
        x, y, c, chips = _place()
        started, waits = [], []
        for a in range(n):
            rh = bufs[a].shape[1] // 2
            for j, chip in enumerate(chips):
                k_chip = 2 * chip[0] + chip[1]
                got = out_refs[a].at[k_chip, pl.ds(c * rh, rh)]
                cp = pltpu.make_async_remote_copy(src_ref=got, dst_ref=got, send_sem=send_sems.at[3 * a + j],
                                                  recv_sem=recv_sems.at[3 * a + j], device_id=(x, y, 1 - c),
                                                  device_id_type=MESH)
                cp.start()
                started.append(cp)
                other = out_refs[a].at[k_chip, pl.ds((1 - c) * rh, rh)]
                waits.append(pltpu.make_async_remote_copy(
                    src_ref=other, dst_ref=other, send_sem=send_sems.at[3 * a + j], recv_sem=recv_sems.at[3 * a + j],
                    device_id=(x, y, c), device_id_type=MESH))
        for cp in waits:
            cp.wait_recv()
        for cp in started:
            cp.wait_send()

    return pl.pallas_call(
        body, name=name, out_shape=[jax.ShapeDtypeStruct(b.shape, b.dtype) for b in bufs],
        in_specs=[_HBM] * n, out_specs=[_HBM] * n, input_output_aliases={a: a for a in range(n)},
        scratch_shapes=[pltpu.SemaphoreType.DMA((3 * n,)), pltpu.SemaphoreType.DMA((3 * n,))],
    )(*bufs)


def _proj_gather(h, buf, order, job):
    S, D = h.shape
    nb = buf.shape[2]
    tm = _pick(S, 1024, 16)
    tn = _pick(nb, 1408, 128)
    per = nb // tn
    nj, ni = N_CHIPS * per, S // tm
    rh = D // 2

    def body(order_ref, h_ref, buf_in, proj_ref, buf_ref, wbuf, tile_sems, ici_send, ici_recv, d2d_send, d2d_recv):
        j, i = pl.program_id(0), pl.program_id(1)
        x, y, c, chips = _place()
        k_me = 2 * x + y

        def half(k, hc):
            return buf_ref.at[k, pl.ds(hc * rh, rh)]

        def ici(q, receiving):
            k_chip = 2 * chips[q][0] + chips[q][1]
            src, to = (half(k_chip, c), (x, y, c)) if receiving else (half(k_me, c), (*chips[q], c))
            return pltpu.make_async_remote_copy(src_ref=src, dst_ref=src, send_sem=ici_send.at[q],
                                                recv_sem=ici_recv.at[q], device_id=to, device_id_type=MESH)

        def d2d(q, receiving):
            k_chip = 2 * chips[q][0] + chips[q][1]
            src, to = (half(k_chip, 1 - c), (x, y, c)) if receiving else (half(k_chip, c), (x, y, 1 - c))
            return pltpu.make_async_remote_copy(src_ref=src, dst_ref=src, send_sem=d2d_send.at[q],
                                                recv_sem=d2d_recv.at[q], device_id=to, device_id_type=MESH)

        def tile_copy(t):
            col = pl.multiple_of((t % per) * tn, 128)
            return pltpu.make_async_copy(buf_ref.at[order_ref[t // per], :, pl.ds(col, tn)], wbuf.at[t % 2],
                                         tile_sems.at[t % 2])

        @pl.when(jnp.logical_and(j == 0, i == 0))
        def _():
            ici(0, False).start()
            ici(1, False).start()
            tile_copy(0).start()

        @pl.when(i == 0)
        def _():
            tile_copy(j).wait()
            for q in range(3):
                @pl.when(j + 1 == (q + 1) * per)
                def _():
                    ici(q, True).wait_recv()
                    d2d(q, False).start()
                    if q == 0:
                        ici(0, False).wait_send()
                        ici(1, False).wait_send()
                        ici(2, False).start()
                    d2d(q, True).wait_recv()

            @pl.when(j + 1 < nj)
            def _():
                tile_copy(j + 1).start()

        proj_ref[...] = jnp.dot(h_ref[...], wbuf[j % 2], preferred_element_type=F32)

        @pl.when(jnp.logical_and(j == nj - 1, i == ni - 1))
        def _():
            ici(2, False).wait_send()
            for q in range(3):
                d2d(q, False).wait_send()

    grid = (nj, ni)
    in_specs = [pl.BlockSpec((tm, D), lambda j, i, o: (i, 0)), _HBM]
    args = [h, buf]
    out_specs = [pl.BlockSpec((tm, tn), lambda j, i, o: (i, o[j // per] * per + j % per)), _HBM]
    out_shape = [jax.ShapeDtypeStruct((S, N_CHIPS * nb), F32), jax.ShapeDtypeStruct(buf.shape, buf.dtype)]
    scratch = [pltpu.VMEM((2, D, tn), BF16), pltpu.SemaphoreType.DMA((2,))] + [pltpu.SemaphoreType.DMA((3,))] * 4
    aliases, wrap = _host(job, grid, in_specs, args, out_specs, out_shape, scratch, n_prefetch=1)
    aliases[2] = 1
    return pl.pallas_call(
        wrap(body), name="proj",
        grid_spec=pltpu.PrefetchScalarGridSpec(num_scalar_prefetch=1, grid=grid, in_specs=in_specs, out_specs=out_specs,
                                               scratch_shapes=scratch),
        out_shape=out_shape, input_output_aliases=aliases, compiler_params=_params(("arbitrary", "arbitrary")),
    )(order, *args)


def _swap_halves(name, grads):
    n = len(grads)

    def body(*refs):
        in_refs, out_refs = refs[:n], refs[n:2 * n]
        send_sems, recv_sems = refs[2 * n:]
        x, y, c, _ = _place()
        cps = []
        for a in range(n):
            rh = grads[a].shape[1] // 2
            cp = pltpu.make_async_remote_copy(
                src_ref=in_refs[a].at[:, pl.ds((1 - c) * rh, rh)], dst_ref=out_refs[a],
                send_sem=send_sems.at[a], recv_sem=recv_sems.at[a], device_id=(x, y, 1 - c), device_id_type=MESH)
            cp.start()
            cps.append(cp)
        for cp in cps:
            cp.wait()

    return pl.pallas_call(
        body, name=name,
        out_shape=[jax.ShapeDtypeStruct((N_CHIPS, g.shape[1] // 2, g.shape[2]), g.dtype) for g in grads],
        in_specs=[_HBM] * n, out_specs=[_HBM] * n,
        scratch_shapes=[pltpu.SemaphoreType.DMA((n,)), pltpu.SemaphoreType.DMA((n,))],
    )(*grads)


def _pair_sum(name, g, q, c_idx):
    _, R, C = g.shape
    rh = R // 2
    tr = _pick(rh, max(16, (512 * 1024) // C // 16 * 16), 16)
    nh = rh // tr

    def body(c_ref, g_ref, q_ref, o_ref, o2_ref):
        s = (g_ref[...].astype(F32) + q_ref[...].astype(F32)).astype(BF16)
        o_ref[...] = s
        o2_ref[...] = s

    out_spec = pl.BlockSpec((None, tr, C), lambda k, i, c_ref: (k, i, 0))
    return pl.pallas_call(
        body, name=name,
        grid_spec=pltpu.PrefetchScalarGridSpec(
            num_scalar_prefetch=1, grid=(N_CHIPS, nh),
            in_specs=[pl.BlockSpec((None, tr, C), lambda k, i, c_ref: (k, c_ref[0] * nh + i, 0)),
                      pl.BlockSpec((None, tr, C), lambda k, i, c_ref: (k, i, 0))],
            out_specs=[out_spec, out_spec]),
        out_shape=[jax.ShapeDtypeStruct((N_CHIPS, rh, C), BF16)] * 2,
        compiler_params=_params(("parallel", "parallel")),
    )(c_idx, g, q)


def _sum_chips(name, r, c_idx):
    _, rh, C = r.shape
    tr = _pick(rh, max(16, (256 * 1024) // C // 16 * 16), 16)

    def body(c_ref, r_ref, o_ref):
        acc = r_ref[0].astype(F32)
        for k in range(1, N_CHIPS):
            acc = acc + r_ref[k].astype(F32)
        o_ref[...] = acc

    return pl.pallas_call(
        body, name=name,
        grid_spec=pltpu.PrefetchScalarGridSpec(
            num_scalar_prefetch=1, grid=(rh // tr,),
            in_specs=[pl.BlockSpec((N_CHIPS, tr, C), lambda i, c_ref: (0, i, 0))],
            out_specs=pl.BlockSpec((None, tr, C), lambda i, c_ref: (c_ref[0], i, 0))),
        out_shape=jax.ShapeDtypeStruct((2, rh, C), F32), compiler_params=_params(("parallel",)),
    )(c_idx, r)


def _share_halves(bufs):
    n = len(bufs)

    def body(*refs):
        out_refs = refs[n:2 * n]
        send_sems, recv_sems = refs[2 * n:]
        x, y, c, _ = _place()
        cps = []
        for a in range(n):
            cp = pltpu.make_async_remote_copy(
                src_ref=out_refs[a].at[c], dst_ref=out_refs[a].at[c], send_sem=send_sems.at[a],
                recv_sem=recv_sems.at[a], device_id=(x, y, 1 - c), device_id_type=MESH)
            cp.start()
            cps.append(cp)
        for a, cp in enumerate(cps):
            got = out_refs[a].at[1 - c]
            pltpu.make_async_remote_copy(src_ref=got, dst_ref=got, send_sem=send_sems.at[a], recv_sem=recv_sems.at[a],
                                         device_id=(x, y, c), device_id_type=MESH).wait_recv()
        for cp in cps:
            cp.wait_send()

    return pl.pallas_call(
        body, name="share_halves",
        out_shape=[jax.ShapeDtypeStruct(b.shape, b.dtype) for b in bufs],
        in_specs=[_HBM] * n, out_specs=[_HBM] * n, input_output_aliases={a: a for a in range(n)},
        scratch_shapes=[pltpu.SemaphoreType.DMA((n,)), pltpu.SemaphoreType.DMA((n,))],
    )(*bufs)


_BIG = ("w_in", "w_conv_out", "w_hgrn_out", "w_o", "w_ffn_gate", "w_ffn_up", "w_ffn_down")
_ROW_SHARDED = ("w_o", "w_ffn_down")
_WEIGHTS = ("w_ada", "b_ada", "norm_mix_g", "w_in", "conv_w", "lb_param", "gnorm_g", "w_conv_out", "w_hgrn_out",
            "w_o", "norm_ffn_g", "w_ffn_gate", "w_ffn_up", "w_ffn_down", "norm_final_g")


def _pack_rows(pieces):
    flat = jnp.concatenate([p.reshape(-1) for p in pieces])
    pad = (-flat.shape[0]) % 1024
    return jnp.pad(flat, (0, pad)).reshape(8, -1)


def kernel(x, c, w_ada, b_ada, norm_mix_g, w_in, conv_w, lb_param, gnorm_g, w_conv_out, w_hgrn_out, w_o, norm_ffn_g, w_ffn_gate, w_ffn_up, w_ffn_down, norm_final_g, loss_target, m_w_ada, m_b_ada, m_norm_mix_g, m_w_in, m_conv_w, m_lb_param, m_gnorm_g, m_w_conv_out, m_w_hgrn_out, m_w_o, m_norm_ffn_g, m_w_ffn_gate, m_w_ffn_up, m_w_ffn_down, m_norm_final_g, v_w_ada, v_b_ada, v_norm_mix_g, v_w_in, v_conv_w, v_lb_param, v_gnorm_g, v_w_conv_out, v_w_hgrn_out, v_w_o, v_norm_ffn_g, v_w_ffn_gate, v_w_ffn_up, v_w_ffn_down, v_norm_final_g):
    w = dict(w_ada=w_ada, b_ada=b_ada, norm_mix_g=norm_mix_g, w_in=w_in, conv_w=conv_w, lb_param=lb_param,
             gnorm_g=gnorm_g, w_conv_out=w_conv_out, w_hgrn_out=w_hgrn_out, w_o=w_o, norm_ffn_g=norm_ffn_g,
             w_ffn_gate=w_ffn_gate, w_ffn_up=w_ffn_up, w_ffn_down=w_ffn_down, norm_final_g=norm_final_g)
    m = dict(w_ada=m_w_ada, b_ada=m_b_ada, norm_mix_g=m_norm_mix_g, w_in=m_w_in, conv_w=m_conv_w, lb_param=m_lb_param,
             gnorm_g=m_gnorm_g, w_conv_out=m_w_conv_out, w_hgrn_out=m_w_hgrn_out, w_o=m_w_o, norm_ffn_g=m_norm_ffn_g,
             w_ffn_gate=m_w_ffn_gate, w_ffn_up=m_w_ffn_up, w_ffn_down=m_w_ffn_down, norm_final_g=m_norm_final_g)
    v = dict(w_ada=v_w_ada, b_ada=v_b_ada, norm_mix_g=v_norm_mix_g, w_in=v_w_in, conv_w=v_conv_w, lb_param=v_lb_param,
             gnorm_g=v_gnorm_g, w_conv_out=v_w_conv_out, w_hgrn_out=v_w_hgrn_out, w_o=v_w_o, norm_ffn_g=v_norm_ffn_g,
             w_ffn_gate=v_w_ffn_gate, w_ffn_up=v_w_ffn_up, w_ffn_down=v_w_ffn_down, norm_final_g=v_norm_final_g)
    two_d = lambda a: a.reshape((-1, a.shape[-1])) if a.ndim != 2 else a
    shapes = {k: a.shape for k, a in w.items()}
    w, m, v = ({k: two_d(a) for k, a in t.items()} for t in (w, m, v))
    w["lb_param"], m["lb_param"], v["lb_param"] = lb_param, m_lb_param, v_lb_param
    S, D = x.shape[1], x.shape[2]
    d_conv = D // 2
    ix, iy, ic = lax.axis_index("x"), lax.axis_index("y"), lax.axis_index("c")
    k_me = 2 * ix + iy
    dev = 2 * k_me + ic
    cw_shard = w["conv_w"].shape[1]
    ada_cols = w["w_ada"].shape[1]

    got = _allgather_rows("gather_cond", _pack_rows([c, w["conv_w"]])).reshape(N_DEV, -1)
    c_all = got[:, :D]
    conv_full = got[::2, D:D + 3 * cw_shard].reshape(N_CHIPS, 3, cw_shard).transpose(1, 0, 2).reshape(3, -1)
    c_act, c_act_b, lb = _prep(c_all, lb_param)
    b_cols = lax.dynamic_slice(w["b_ada"], (0, k_me * ada_cols), (1, ada_cols))
    mod_cols, = _matmul("ada_fwd", [(c_act_b, 'n', w["w_ada"].astype(BF16), 'n')], [0], N_DEV, ada_cols, D,
                        N_DEV, _pick(ada_cols, 1536, 128), D, [(b_cols, 'row', 0)], [(F32, None)],
                        lambda accs, ex: [accs[0] + ex[0]])
    mod_all = _allgather_rows("gather_mod", mod_cols).reshape(N_CHIPS, 2, N_DEV, ada_cols)[:, 0]
    mod_all = mod_all.transpose(1, 0, 2).reshape(N_DEV, -1)
    mod = lax.dynamic_slice(mod_all, (dev, 0), (1, mod_all.shape[1]))
    k_idx = jnp.reshape(k_me, (1,)).astype(jnp.int32)
    c_idx = jnp.reshape(ic, (1,)).astype(jnp.int32)
    bufs = {k: _cast_place("cast_" + k, w[k], k_idx) for k in _BIG}
    order = jnp.stack([k_me, 2 * (1 - ix) + iy, 2 * ix + (1 - iy), 2 * (1 - ix) + (1 - iy)]).astype(jnp.int32)

    loss, dx, small, arrived = _local_step(
        x[0], loss_target[0], mod, w["norm_mix_g"], lb, w["gnorm_g"], w["norm_ffn_g"], w["norm_final_g"], conv_full,
        bufs, c_idx, order)

    pieces = [small["dmod"], small["dg_mix"], small["dg_ffn"], small["dg_final"], small["dlb"], small["dgn"],
              small["dconv_w"], loss]
    sizes = [p.size for p in pieces]
    parts = _allgather_rows("gather_small", _pack_rows(pieces)).reshape(N_DEV, -1)
    total = _small_reduce(parts)
    offs = [0]
    for s in sizes:
        offs.append(offs[-1] + s)
    tot = [total[:, offs[i]:offs[i + 1]] for i in range(len(sizes))]
    dmod_all = parts[:, :sizes[0]]
    grads = {
        "b_ada": tot[0], "norm_mix_g": tot[1], "norm_ffn_g": tot[2], "norm_final_g": tot[3],
        "lb_param": _lb_grad(tot[4], lb), "gnorm_g": tot[5],
        "conv_w": lax.dynamic_slice(tot[6].reshape(3, -1), (0, k_me * cw_shard), (3, cw_shard)),
    }
    loss_out = tot[7][0, 0]

    for k in _BIG:
        grads[k] = arrived[k].reshape(-1, arrived[k].shape[-1])

    delta, new_m, new_v = {}, {}, {}
    dmod_cols = lax.dynamic_slice(dmod_all, (0, k_me * ada_cols), (N_DEV, ada_cols))
    grads["w_ada"], delta["w_ada"], new_m["w_ada"], new_v["w_ada"] = _ada_update(
        c_act.T, dmod_cols, w["w_ada"], m["w_ada"], v["w_ada"])
    for k in _WEIGHTS:
        if k != "w_ada":
            grads[k], delta[k], new_m[k], new_v[k] = _adamw("adamw_" + k, w[k], grads[k], m[k], v[k])
    outs = [loss_out, dx.reshape(x.shape)]
    for t in (grads, delta, new_m, new_v):
        outs += [t[k].reshape(shapes[k]) for k in _WEIGHTS]
    return tuple(outs)
```

```python
import functools

import jax
import jax.numpy as jnp
from jax import lax
from jax.experimental import pallas as pl
from jax.experimental.pallas import tpu as pltpu

F32 = jnp.float32
BF16 = jnp.bfloat16
MESH = pl.DeviceIdType.MESH

EPS = 1e-6
HEAD = 128
BLK = 16
N_CHIPS = 4
N_DEV = 8
VMEM_LIMIT_BYTES = 56 * 1024 * 1024

ADAM_LR = 0.001
ADAM_B1 = 0.9
ADAM_B2 = 0.999
ADAM_EPS = 1e-08
ADAM_WD = 0.01
ADAM_STEP = 10


def _pick(dim, pref, mult):
    if dim <= pref:
        return dim
    t = (pref // mult) * mult
    while t >= mult:
        if dim % t == 0:
            return t
        t -= mult
    return dim


def _sig(x):
    return 1.0 / (1.0 + jnp.exp(-x))


def _params(sem):
    return pltpu.CompilerParams(dimension_semantics=sem, vmem_limit_bytes=VMEM_LIMIT_BYTES)


def _gj(j, i, k):
    return j


def _gk(j, i, k):
    return k


def _wspec(arr, rt, ct, r_of, c_of):
    if arr.ndim == 2:
        return pl.BlockSpec((rt, ct), lambda j, i, k: (r_of(j, i, k), c_of(j, i, k)))
    per = arr.shape[2] // ct
    assert arr.shape[2] % ct == 0
    return pl.BlockSpec((None, rt, ct),
                        lambda j, i, k: (c_of(j, i, k) // per, r_of(j, i, k), c_of(j, i, k) % per))


_HBM = pl.BlockSpec(memory_space=pltpu.HBM)


def _host(job, grid, in_specs, args, out_specs, out_shape, scratch, n_prefetch=0):
    if job is None:
        return {}, (lambda body: body)
    n_in, n_out, n_scr = len(args), len(out_shape), len(scratch)
    n_job, n_alias = len(job.inputs), job.n_alias
    in_specs += [_HBM] * n_job
    args += job.inputs
    out_specs += [_HBM] * n_alias
    out_shape += [jax.ShapeDtypeStruct(a.shape, a.dtype) for a in job.inputs[n_job - n_alias:]]
    scratch += job.sem_shapes
    aliases = {n_prefetch + n_in + n_job - n_alias + t: n_out + t for t in range(n_alias)}

    def wrap(body):
        def hosted(*refs):
            pre, refs = refs[:n_prefetch], refs[n_prefetch:]
            ins, refs = refs[:n_in], refs[n_in:]
            j_in, refs = refs[:n_job], refs[n_job:]
            outs, refs = refs[:n_out], refs[n_out:]
            j_out, refs = refs[:n_alias], refs[n_alias:]
            scr, sems = refs[:n_scr], refs[n_scr:]
            first = functools.reduce(jnp.logical_and, [pl.program_id(d) == 0 for d in range(len(grid))])
            last = functools.reduce(jnp.logical_and, [pl.program_id(d) == grid[d] - 1 for d in range(len(grid))])

            @pl.when(first)
            def _():
                job.start(j_in, j_out, sems)

            body(*pre, *ins, *outs, *scr)

            @pl.when(last)
            def _():
                job.finish(j_in, j_out, sems)

        return hosted

    return aliases, wrap


EPILOGUE_COLS = 768


def _plain(accs, extras):
    return accs


def _matmul(name, pairs, acc_of, M, N, K, tm, tn, tk, extras, outs, epilogue, job=None, rows_outer=False):
    assert M % tm == 0 and N % tn == 0 and K % tk == 0, (name, M, N, K, tm, tn, tk)
    nj, ni, nk = N // tn, M // tm, K // tk
    n_pairs, n_extra, n_out = len(pairs), len(extras), len(outs)
    n_acc = max(acc_of) + 1
    in_specs, args, dims = [], [], []
    lhs_of, seen = [], {}
    for a, am, b, bm in pairs:
        if (id(a), am) not in seen:
            seen[id(a), am] = len(args)
            args.append(a)
            if am == 'n':
                in_specs.append(pl.BlockSpec((tm, tk), lambda j, i, k: (i, k)))
            else:
                in_specs.append(pl.BlockSpec((tk, tm), lambda j, i, k: (k, i)))
        lhs_of.append(seen[id(a), am])
    n_lhs = len(args)
    for a, am, b, bm in pairs:
        if bm == 'n':
            in_specs.append(_wspec(b, tk, tn, _gk, _gj))
        else:
            in_specs.append(_wspec(b, tn, tk, _gj, _gk))
        dims.append((((1 if am == 'n' else 0,), (0 if bm == 'n' else 1,)), ((), ())))
        args.append(b)
    for e, kind, off in extras:
        assert off % tn == 0, (name, off, tn)
        o = off // tn
        if kind == 'tile':
            in_specs.append(pl.BlockSpec((tm, tn), lambda j, i, k, o=o: (i, j + o)))
        else:
            in_specs.append(pl.BlockSpec((1, tn), lambda j, i, k, o=o: (0, j + o)))
        args.append(e)
    out_shape, out_specs = [], []
    for dt, nb in outs:
        if nb is None:
            out_shape.append(jax.ShapeDtypeStruct((M, N), dt))
            out_specs.append(pl.BlockSpec((tm, tn), lambda j, i, k: (i, j)))
        else:
            assert nb % tn == 0 and N % nb == 0
            per = nb // tn
            out_shape.append(jax.ShapeDtypeStruct((N // nb, M, nb), dt))
            out_specs.append(pl.BlockSpec((None, tm, tn), lambda j, i, k, per=per: (j // per, i, j % per)))

    def body(*refs):
        n_ab = n_lhs + n_pairs
        e_refs = refs[n_ab:n_ab + n_extra]
        o_refs = refs[n_ab + n_extra:n_ab + n_extra + n_out]
        acc_refs = refs[n_ab + n_extra + n_out:]
        def products(cols):
            sums = [None] * n_acc
            for p in range(n_pairs):
                b_ref = refs[n_lhs + p]
                b = b_ref[:, cols] if pairs[p][3] == 'n' else b_ref[cols, :]
                prod = lax.dot_general(refs[lhs_of[p]][...], b, dims[p], preferred_element_type=F32)
                a = acc_of[p]
                sums[a] = prod if sums[a] is None else sums[a] + prod
            return sums

        def finish(accs, cols):
            res = epilogue(accs, [e[:, cols] for e in e_refs])
            for o_ref, r in zip(o_refs, res):
                o_ref[:, cols] = r.astype(o_ref.dtype)

        if nk == 1 and epilogue is not _plain:
            for c0 in range(0, tn, EPILOGUE_COLS):
                cols = slice(c0, min(c0 + EPILOGUE_COLS, tn))
                finish(products(cols), cols)
            return
        whole = slice(0, tn)
        sums = products(whole)
        if nk == 1:
            finish(sums, whole)
        else:
            k = pl.program_id(2)
            targets = o_refs if sum_in_outs else acc_refs

            @pl.when(k == 0)
            def _():
                for a in range(n_acc):
                    targets[a][...] = sums[a]

            @pl.when(k > 0)
            def _():
                for a in range(n_acc):
                    targets[a][...] += sums[a]

            if not sum_in_outs:
                @pl.when(k == nk - 1)
                def _():
                    finish([acc_refs[a][...] for a in range(n_acc)], whole)

    sum_in_outs = epilogue is _plain and nk > 1 and n_out == n_acc and all(dt == F32 for dt, _ in outs)
    scratch = [pltpu.VMEM((tm, tn), F32) for _ in range(n_acc)] if nk > 1 and not sum_in_outs else []
    grid = (nj, ni, nk)
    if rows_outer:
        grid = (ni, nj, nk)
        swap = lambda spec: pl.BlockSpec(spec.block_shape, lambda i, j, k, f=spec.index_map: f(j, i, k))
        in_specs, out_specs = [swap(s) for s in in_specs], [swap(s) for s in out_specs]
    aliases, wrap = _host(job, grid, in_specs, args, out_specs, out_shape, scratch)
    sem = ("parallel", "parallel", "arbitrary") if job is None else ("arbitrary",) * 3
    return pl.pallas_call(
        wrap(body), name=name, grid=grid, in_specs=in_specs, out_specs=out_specs, out_shape=out_shape,
        scratch_shapes=scratch, input_output_aliases=aliases, compiler_params=_params(sem),
    )(*args)


def _row_spec(tr, d):
    return pl.BlockSpec((tr, d), lambda i: (i, 0))


def _vec_spec(d):
    return pl.BlockSpec((1, d), lambda i: (0, 0))


def _norm_mod(name, x, g, sc, sh, job=None):
    S, D = x.shape
    tr = _pick(S, 256, 8)

    def body(x_ref, g_ref, sc_ref, sh_ref, h_ref):
        xv = x_ref[...]
        r = lax.rsqrt(jnp.mean(xv * xv, axis=-1, keepdims=True) + EPS)
        h_ref[...] = ((xv * r) * g_ref[...] * (1.0 + sc_ref[...]) + sh_ref[...]).astype(BF16)

    grid = (S // tr,)
    in_specs = [_row_spec(tr, D), _vec_spec(D), _vec_spec(D), _vec_spec(D)]
    args = [x, g, sc, sh]
    out_specs, out_shape, scratch = [_row_spec(tr, D)], [jax.ShapeDtypeStruct((S, D), BF16)], []
    aliases, wrap = _host(job, grid, in_specs, args, out_specs, out_shape, scratch)
    res = pl.pallas_call(
        wrap(body), name=name, grid=grid, in_specs=in_specs, out_specs=out_specs, out_shape=out_shape,
        scratch_shapes=scratch, input_output_aliases=aliases,
        compiler_params=_params(("parallel" if job is None else "arbitrary",)),
    )(*args)
    return res[0] if job is None else res


def _loss_head(x2, target, g, ff, gt):
    S, D = x2.shape
    tr = _pick(S, 256, 8)

    def body(x_ref, t_ref, g_ref, ff_ref, gt_ref, dx_ref, dffb_ref, loss_ref, dgt_ref, dg_ref):
        i = pl.program_id(0)

        @pl.when(i == 0)
        def _():
            loss_ref[...] = jnp.zeros_like(loss_ref)
            dgt_ref[...] = jnp.zeros_like(dgt_ref)
            dg_ref[...] = jnp.zeros_like(dg_ref)

        xv = x_ref[...]
        gv = g_ref[...]
        r = lax.rsqrt(jnp.mean(xv * xv, axis=-1, keepdims=True) + EPS)
        xh = xv * r
        err = xh * gv - t_ref[...]
        loss_ref[...] += 0.5 * jnp.sum(jnp.mean(err * err, axis=-1, keepdims=True))
        dy = err * (1.0 / D)
        dg_ref[...] += jnp.sum(dy * xh, axis=0, keepdims=True)
        dxh = dy * gv
        dx = r * (dxh - xh * jnp.mean(dxh * xh, axis=-1, keepdims=True))
        dx_ref[...] = dx
        dffb_ref[...] = (dx * gt_ref[...]).astype(BF16)
        dgt_ref[...] += jnp.sum(dx * ff_ref[...], axis=0, keepdims=True)

    return pl.pallas_call(
        body, name="loss_head", grid=(S // tr,),
        in_specs=[_row_spec(tr, D), _row_spec(tr, D), _vec_spec(D), _row_spec(tr, D), _vec_spec(D)],
        out_specs=[_row_spec(tr, D), _row_spec(tr, D), pl.BlockSpec((1, 128), lambda i: (0, 0)),
                   _vec_spec(D), _vec_spec(D)],
        out_shape=[jax.ShapeDtypeStruct((S, D), F32), jax.ShapeDtypeStruct((S, D), BF16),
                   jax.ShapeDtypeStruct((1, 128), F32), jax.ShapeDtypeStruct((1, D), F32),
                   jax.ShapeDtypeStruct((1, D), F32)],
        compiler_params=_params(("arbitrary",)),
    )(x2, target, g, ff, gt)


def _norm_mod_bwd(name, dh, x, g, sc, dres, gated=None, job=None):
    S, D = x.shape
    tr = _pick(S, 256, 8)
    n = S // tr
    with_gate = gated is not None

    def body(*refs):
        if with_gate:
            dh_ref, x_ref, g_ref, sc_ref, dres_ref, mo_ref, gt_ref, dx_ref, dsh_ref, dsc_ref, dg_ref, dmob_ref, dgt_ref = refs
        else:
            dh_ref, x_ref, g_ref, sc_ref, dres_ref, dx_ref, dsh_ref, dsc_ref, dg_ref = refs
        i = pl.program_id(0)

        @pl.when(i == 0)
        def _():
            dsh_ref[...] = jnp.zeros_like(dsh_ref)
            dsc_ref[...] = jnp.zeros_like(dsc_ref)
            if with_gate:
                dgt_ref[...] = jnp.zeros_like(dgt_ref)

        xv = x_ref[...]
        dhv = dh_ref[...]
        gv = g_ref[...]
        scale = 1.0 + sc_ref[...]
        r = lax.rsqrt(jnp.mean(xv * xv, axis=-1, keepdims=True) + EPS)
        xh = xv * r
        dsh_ref[...] += jnp.sum(dhv, axis=0, keepdims=True)
        dsc_ref[...] += jnp.sum(dhv * xh, axis=0, keepdims=True)
        dxh = dhv * (gv * scale)
        dx = dres_ref[...] + r * (dxh - xh * jnp.mean(dxh * xh, axis=-1, keepdims=True))
        dx_ref[...] = dx
        if with_gate:
            dmob_ref[...] = (dx * gt_ref[...]).astype(BF16)
            dgt_ref[...] += jnp.sum(dx * mo_ref[...], axis=0, keepdims=True)

        @pl.when(i == n - 1)
        def _():
            t = dsc_ref[...]
            dg_ref[...] = t * scale
            dsc_ref[...] = t * gv

    in_specs = [_row_spec(tr, D), _row_spec(tr, D), _vec_spec(D), _vec_spec(D), _row_spec(tr, D)]
    args = [dh, x, g, sc, dres]
    out_specs = [_row_spec(tr, D), _vec_spec(D), _vec_spec(D), _vec_spec(D)]
    out_shape = [jax.ShapeDtypeStruct((S, D), F32)] + [jax.ShapeDtypeStruct((1, D), F32)] * 3
    if with_gate:
        in_specs += [_row_spec(tr, D), _vec_spec(D)]
        args += list(gated)
        out_specs += [_row_spec(tr, D), _vec_spec(D)]
        out_shape += [jax.ShapeDtypeStruct((S, D), BF16), jax.ShapeDtypeStruct((1, D), F32)]
    scratch = []
    aliases, wrap = _host(job, (n,), in_specs, args, out_specs, out_shape, scratch)
    return pl.pallas_call(
        wrap(body), name=name, grid=(n,), in_specs=in_specs, out_specs=out_specs, out_shape=out_shape,
        scratch_shapes=scratch, input_output_aliases=aliases, compiler_params=_params(("arbitrary",)),
    )(*args)


CONV_ROWS = 256


def _col_spec(S, off_cols):
    o = off_cols // HEAD
    return pl.BlockSpec((S, HEAD), lambda c, o=o: (0, c + o))


def _conv_taps(u, u_prev, rid):
    s1 = jnp.where(rid >= 1, pltpu.roll(u, 1, 0), pltpu.roll(u_prev, 1, 0))
    s2 = jnp.where(rid >= 2, pltpu.roll(u, 2, 0), pltpu.roll(u_prev, 2, 0))
    return s1, s2


def _conv_fwd(proj, conv_w, d_conv, job=None):
    S = proj.shape[0]
    R = min(CONV_ROWS, S)
    nr = S // R

    def body(ab_ref, ac_ref, ax_ref, w_ref, ya_ref):
        w0, w1, w2 = w_ref[0:1, :], w_ref[1:2, :], w_ref[2:3, :]
        rid = lax.broadcasted_iota(jnp.int32, (R, HEAD), 0)

        def step(c, carry):
            rows = pl.ds(pl.multiple_of(c * R, R), R)
            prev = pl.ds(pl.multiple_of(jnp.maximum(c - 1, 0) * R, R), R)
            u = ac_ref[rows, :] * ax_ref[rows, :]
            u_prev = jnp.where(c > 0, ac_ref[prev, :] * ax_ref[prev, :], 0.0)
            s1, s2 = _conv_taps(u, u_prev, rid)
            y = w0 * s2 + w1 * s1 + w2 * u
            ya_ref[rows, :] = (ab_ref[rows, :] * y).astype(BF16)
            return carry

        lax.fori_loop(0, nr, step, 0)

    grid = (d_conv // HEAD,)
    in_specs = [_col_spec(S, 0), _col_spec(S, d_conv), _col_spec(S, 2 * d_conv), pl.BlockSpec((3, HEAD), lambda c: (0, c))]
    args = [proj, proj, proj, conv_w]
    out_specs, out_shape = [pl.BlockSpec((S, HEAD), lambda c: (0, c))], [jax.ShapeDtypeStruct((S, d_conv), BF16)]
    scratch = []
    aliases, wrap = _host(job, grid, in_specs, args, out_specs, out_shape, scratch)
    res = pl.pallas_call(
        wrap(body), name="conv_fwd", grid=grid, in_specs=in_specs, out_specs=out_specs, out_shape=out_shape,
        scratch_shapes=scratch, input_output_aliases=aliases,
        compiler_params=_params(("parallel" if job is None else "arbitrary",)),
    )(*args)
    return res[0] if job is None else res


def _conv_bwd(dya, proj, conv_w, d_conv, job=None):
    S = proj.shape[0]
    R = min(CONV_ROWS, S)
    nr = S // R

    def body(dya_ref, ab_ref, ac_ref, ax_ref, w_ref, dab_ref, dac_ref, dax_ref, dw_ref):
        w0, w1, w2 = w_ref[0:1, :], w_ref[1:2, :], w_ref[2:3, :]
        rid = lax.broadcasted_iota(jnp.int32, (R, HEAD), 0)

        def step(c, carry):
            dw0, dw1, dw2 = carry
            rows = pl.ds(pl.multiple_of(c * R, R), R)
            prev = pl.ds(pl.multiple_of(jnp.maximum(c - 1, 0) * R, R), R)
            nxt = pl.ds(pl.multiple_of(jnp.minimum(c + 1, nr - 1) * R, R), R)
            a_c, a_x, a_b = ac_ref[rows, :], ax_ref[rows, :], ab_ref[rows, :]
            u = a_c * a_x
            u_prev = jnp.where(c > 0, ac_ref[prev, :] * ax_ref[prev, :], 0.0)
            s1, s2 = _conv_taps(u, u_prev, rid)
            y = w0 * s2 + w1 * s1 + w2 * u
            dy = dya_ref[rows, :]
            dab_ref[rows, :] = (dy * y).astype(BF16)
            dyc = dy * a_b
            dyc_next = jnp.where(c < nr - 1, dya_ref[nxt, :] * ab_ref[nxt, :], 0.0)
            t1 = jnp.where(rid < R - 1, pltpu.roll(dyc, R - 1, 0), pltpu.roll(dyc_next, R - 1, 0))
            t2 = jnp.where(rid < R - 2, pltpu.roll(dyc, R - 2, 0), pltpu.roll(dyc_next, R - 2, 0))
            du = w2 * dyc + w1 * t1 + w0 * t2
            dac_ref[rows, :] = (du * a_x).astype(BF16)
            dax_ref[rows, :] = (du * a_c).astype(BF16)
            dw0 = dw0 + jnp.sum(dyc * s2, axis=0, keepdims=True)
            dw1 = dw1 + jnp.sum(dyc * s1, axis=0, keepdims=True)
            dw2 = dw2 + jnp.sum(dyc * u, axis=0, keepdims=True)
            return dw0, dw1, dw2

        z = jnp.zeros((1, HEAD), F32)
        dw0, dw1, dw2 = lax.fori_loop(0, nr, step, (z, z, z))
        dw_ref[0:1, :] = dw0
        dw_ref[1:2, :] = dw1
        dw_ref[2:3, :] = dw2

    col = pl.BlockSpec((S, HEAD), lambda c: (0, c))
    grid = (d_conv // HEAD,)
    in_specs = [col, _col_spec(S, 0), _col_spec(S, d_conv), _col_spec(S, 2 * d_conv),
                pl.BlockSpec((3, HEAD), lambda c: (0, c))]
    args = [dya, proj, proj, proj, conv_w]
    out_specs = [col, col, col, pl.BlockSpec((3, HEAD), lambda c: (0, c))]
    out_shape = [jax.ShapeDtypeStruct((S, d_conv), BF16)] * 3 + [jax.ShapeDtypeStruct((3, d_conv), F32)]
    scratch = []
    aliases, wrap = _host(job, grid, in_specs, args, out_specs, out_shape, scratch)
    return pl.pallas_call(
        wrap(body), name="conv_bwd", grid=grid, in_specs=in_specs, out_specs=out_specs, out_shape=out_shape,
        scratch_shapes=scratch, input_output_aliases=aliases,
        compiler_params=_params(("parallel" if job is None else "arbitrary",)),
    )(*args)


HGRN_ROWS = 512
HGRN_FWD_SUB = 64
HGRN_BWD_SUB = 32
HGRN_GATE_ROWS = 128


def _block_cumsum(x, pos):
    for s in (1, 2, 4, 8):
        x = x + jnp.where(pos >= s, pltpu.roll(x, s, 0), 0.0)
    return x


def _block_rev_cumsum(x, pos, rows):
    for s in (1, 2, 4, 8):
        x = x + jnp.where(pos < BLK - s, pltpu.roll(x, rows - s, 0), 0.0)
    return x


def _block_last(x, pos, rows):
    m = jnp.where(pos == BLK - 1, x, 0.0)
    for s in (1, 2, 4, 8):
        m = m + pltpu.roll(m, rows - s, 0)
    return m


def _hgrn_gates(q, z, lb):
    sgq = _sig(q)
    qs = q * sgq
    sg = _sig(z)
    f = lb + (1.0 - lb) * sg
    kk = (1.0 - lb) * (1.0 - sg)
    return sgq, qs, sg, f, kk


def _hgrn_decays(q, z, lb, pos, rows):
    sgq, qs, sg, f, kk = _hgrn_gates(q, z, lb)
    b = _block_cumsum(jnp.log(f), pos)
    return sgq, qs, sg, f, kk, b, _block_last(b, pos, rows)


def _hgrn_specs(T, d_conv, n_t, rev):
    def t_of(i):
        return (n_t - 1 - i) if rev else i

    def pcol(off):
        o = off // HEAD
        return pl.BlockSpec((T, HEAD), lambda h, i, o=o: (t_of(i), h + o))

    q_spec, z_spec, v_spec, g_spec = pcol(3 * d_conv), pcol(4 * d_conv), pcol(5 * d_conv), pcol(6 * d_conv)
    lb_spec = pl.BlockSpec((1, HEAD), lambda h, i: (0, h))
    gn_spec = pl.BlockSpec((1, HEAD), lambda h, i: (0, 0))
    act_spec = pl.BlockSpec((T, HEAD), lambda h, i: (t_of(i), h))
    st_spec = pl.BlockSpec((None, T // BLK, HEAD, HEAD), lambda h, i: (h, t_of(i), 0, 0))
    return q_spec, z_spec, v_spec, g_spec, lb_spec, gn_spec, act_spec, st_spec


def _hgrn_fwd(proj, lb, gn, d_conv, job=None):
    S = proj.shape[0]
    H = d_conv // HEAD
    T = min(HGRN_ROWS, S)
    n_t, nb = S // T, T // BLK
    sub = min(HGRN_FWD_SUB, T)
    q_spec, z_spec, v_spec, g_spec, lb_spec, gn_spec, act_spec, st_spec = _hgrn_specs(T, d_conv, n_t, False)

    def body(q_ref, z_ref, v_ref, g_ref, lb_ref, gn_ref, ob_ref, o_ref, st_ref, qe_s, ke_s, dec_s, state, v_s, o_s, u_s):
        @pl.when(pl.program_id(1) == 0)
        def _():
            state[...] = jnp.zeros_like(state)

        pos = lax.broadcasted_iota(jnp.int32, (sub, HEAD), 0) % BLK
        lbv = lb_ref[...]

        def vector_pass(s, carry):
            r = pl.ds(pl.multiple_of(s * sub, sub), sub)
            v = v_ref[r, :]
            _, qs, _, _, kk, b, b_tot = _hgrn_decays(q_ref[r, :], z_ref[r, :], lbv, pos, sub)
            qe_s[r, :] = (qs * jnp.exp(b)).astype(BF16)
            ke_s[r, :] = (kk * jnp.exp(b_tot - b)).astype(BF16)
            v_s[r, :] = v.astype(BF16)
            dec_s[r, :] = jnp.exp(b_tot)
            o = jnp.sum(qs * kk, axis=-1, keepdims=True) * v
            for d in range(1, BLK):
                e = jnp.exp(b - pltpu.roll(b, d, 0))
                a = jnp.sum(qs * pltpu.roll(kk, d, 0) * e, axis=-1, keepdims=True)
                o = o + jnp.where(pos >= d, a * pltpu.roll(v, d, 0), 0.0)
            o_s[r, :] = o
            return carry

        lax.fori_loop(0, T // sub, vector_pass, 0)

        for j in range(nb):
            rows = pl.ds(j * BLK, BLK)
            u_s[j] = lax.dot_general(v_s[rows, :], ke_s[rows, :], (((0,), (0,)), ((), ())),
                                     preferred_element_type=F32)
        st = state[...]
        for j in range(nb):
            st_ref[j] = st.astype(BF16)
            st = st * dec_s[pl.ds(j * BLK, 1), :] + u_s[j]
        state[...] = st
        for j in range(nb):
            rows = pl.ds(j * BLK, BLK)
            o_s[rows, :] += lax.dot_general(qe_s[rows, :], st_ref[j], (((1,), (1,)), ((), ())),
                                            preferred_element_type=F32)
        o = o_s[...]
        o_ref[...] = o
        r = lax.rsqrt(jnp.mean(o * o, axis=-1, keepdims=True) + EPS)
        g = g_ref[...]
        ob_ref[...] = ((o * r) * gn_ref[...] * (g * _sig(g))).astype(BF16)

    grid = (H, n_t)
    in_specs = [q_spec, z_spec, v_spec, g_spec, lb_spec, gn_spec]
    args = [proj, proj, proj, proj, lb, gn]
    out_specs = [act_spec, act_spec, st_spec, act_spec, act_spec, act_spec]
    out_shape = [jax.ShapeDtypeStruct((S, d_conv), BF16), jax.ShapeDtypeStruct((S, d_conv), F32),
                 jax.ShapeDtypeStruct((H, S // BLK, HEAD, HEAD), BF16),
                 jax.ShapeDtypeStruct((S, d_conv), BF16), jax.ShapeDtypeStruct((S, d_conv), BF16),
                 jax.ShapeDtypeStruct((S, d_conv), F32)]
    scratch = [pltpu.VMEM((HEAD, HEAD), F32), pltpu.VMEM((T, HEAD), BF16), pltpu.VMEM((T, HEAD), F32),
               pltpu.VMEM((nb, HEAD, HEAD), F32)]
    aliases, wrap = _host(job, grid, in_specs, args, out_specs, out_shape, scratch)
    return pl.pallas_call(
        wrap(body), name="hgrn_fwd", grid=grid, in_specs=in_specs, out_specs=out_specs, out_shape=out_shape,
        scratch_shapes=scratch, input_output_aliases=aliases,
        compiler_params=_params(("parallel" if job is None else "arbitrary", "arbitrary")),
    )(*args)


def _hgrn_bwd(dob, o_raw, states, qe, ke, dec, proj, lb, gn, d_conv, job=None):
    S = proj.shape[0]
    H = d_conv // HEAD
    T = min(HGRN_ROWS, S)
    n_t, nb = S // T, T // BLK
    sub = min(HGRN_BWD_SUB, T)
    gate_rows = min(HGRN_GATE_ROWS, T)
    q_spec, z_spec, v_spec, g_spec, lb_spec, gn_spec, act_spec, st_spec = _hgrn_specs(T, d_conv, n_t, True)

    def body(dob_ref, o_ref, st_ref, qe_b, ke_b, dec_s, q_ref, z_ref, v_ref, g_ref, lb_ref, gn_ref,
             dq_ref, dz_ref, dv_ref, dg_ref, dlb_ref, dgn_ref,
             dstate, do_b, v_b, dqe_s, dke_s, dvi_s, ddec_s, do_s, u_s, ds_s):
        @pl.when(pl.program_id(1) == 0)
        def _():
            dstate[...] = jnp.zeros_like(dstate)
            dlb_ref[...] = jnp.zeros_like(dlb_ref)
            dgn_ref[...] = jnp.zeros_like(dgn_ref)

        pos = lax.broadcasted_iota(jnp.int32, (sub, HEAD), 0) % BLK
        lbv, gnv = lb_ref[...], gn_ref[...]

        def before(s, carry):
            r = pl.ds(pl.multiple_of(s * gate_rows, gate_rows), gate_rows)
            g = g_ref[r, :]
            v_b[r, :] = v_ref[r, :].astype(BF16)
            o = o_ref[r, :]
            rs = lax.rsqrt(jnp.mean(o * o, axis=-1, keepdims=True) + EPS)
            on = o * rs
            sgg = _sig(g)
            dobv = dob_ref[r, :]
            dog = dobv * (g * sgg)
            dgn_ref[...] += jnp.sum(dog * on, axis=0, keepdims=True)
            don = dog * gnv
            do = rs * (don - on * jnp.mean(don * on, axis=-1, keepdims=True))
            dg_ref[r, :] = (dobv * (on * gnv) * (sgg * (1.0 + g * (1.0 - sgg)))).astype(BF16)
            do_s[r, :] = do
            do_b[r, :] = do.astype(BF16)
            return carry

        lax.fori_loop(0, T // gate_rows, before, 0)

        for j in range(nb):
            rows = pl.ds(j * BLK, BLK)
            dob_j = do_b[rows, :]
            u_s[j] = lax.dot_general(dob_j, qe_b[rows, :], (((0,), (0,)), ((), ())), preferred_element_type=F32)
            dqe_s[rows, :] = lax.dot_general(dob_j, st_ref[j], (((1,), (0,)), ((), ())), preferred_element_type=F32)
        dst = dstate[...]
        for j in reversed(range(nb)):
            ds_s[j] = dst.astype(BF16)
            ddec = jnp.sum(st_ref[j].astype(F32) * dst, axis=0, keepdims=True)
            ddec_s[pl.ds(j * BLK, BLK), :] = jnp.broadcast_to(ddec, (BLK, HEAD))
            dst = dst * dec_s[pl.ds(j * BLK, 1), :] + u_s[j]
        dstate[...] = dst
        for j in range(nb):
            rows = pl.ds(j * BLK, BLK)
            dke_s[rows, :] = lax.dot_general(v_b[rows, :], ds_s[j], (((1,), (0,)), ((), ())), preferred_element_type=F32)
            dvi_s[rows, :] = lax.dot_general(ke_b[rows, :], ds_s[j], (((1,), (1,)), ((), ())), preferred_element_type=F32)

        def after(s, carry):
            r = pl.ds(pl.multiple_of(s * sub, sub), sub)
            q, v = q_ref[r, :], v_ref[r, :]
            sgq, qs, sg, f, kk, b, b_tot = _hgrn_decays(q, z_ref[r, :], lbv, pos, sub)
            do = do_s[r, :]
            dqs = dqe_s[r, :] * jnp.exp(b)
            dkk = dke_s[r, :] * jnp.exp(b_tot - b)
            d_tot = jnp.where(pos == BLK - 1, _block_cumsum(dkk * kk, pos) + ddec_s[r, :] * jnp.exp(b_tot), 0.0)
            dv = dvi_s[r, :]
            da = jnp.sum(do * v, axis=-1, keepdims=True)
            dv = dv + jnp.sum(qs * kk, axis=-1, keepdims=True) * do
            dqs = dqs + da * kk
            dkk = dkk + da * qs
            for d in range(1, BLK):
                kd = pltpu.roll(kk, d, 0)
                e = jnp.where(pos >= d, jnp.exp(b - pltpu.roll(b, d, 0)), 0.0)
                a = jnp.sum(qs * kd * e, axis=-1, keepdims=True)
                da = jnp.sum(do * pltpu.roll(v, d, 0), axis=-1, keepdims=True)
                gq = da * e
                dqs = dqs + gq * kd
                dkk = dkk + pltpu.roll(gq * qs, sub - d, 0)
                dv = dv + pltpu.roll(a * do, sub - d, 0)
            db = qs * dqs - kk * dkk + d_tot
            dlf = _block_rev_cumsum(db, pos, sub)
            df = dlf / f - dkk
            dlb_ref[...] += jnp.sum(df * (1.0 - sg), axis=0, keepdims=True)
            dz_ref[r, :] = (df * (1.0 - lbv) * sg * (1.0 - sg)).astype(BF16)
            dq_ref[r, :] = (dqs * (sgq * (1.0 + q * (1.0 - sgq)))).astype(BF16)
            dv_ref[r, :] = dv.astype(BF16)
            return carry

        lax.fori_loop(0, T // sub, after, 0)

    tb = lambda dt: pltpu.VMEM((T, HEAD), dt)
    grid = (H, n_t)
    in_specs = [act_spec, act_spec, st_spec, act_spec, act_spec, act_spec, q_spec, z_spec, v_spec, g_spec, lb_spec,
                gn_spec]
    args = [dob, o_raw, states, qe, ke, dec, proj, proj, proj, proj, lb, gn]
    out_specs = [act_spec, act_spec, act_spec, act_spec, lb_spec, pl.BlockSpec((None, 1, HEAD), lambda h, i: (h, 0, 0))]
    out_shape = ([jax.ShapeDtypeStruct((S, d_conv), BF16)] * 4
                 + [jax.ShapeDtypeStruct((1, d_conv), F32), jax.ShapeDtypeStruct((H, 1, HEAD), F32)])
    scratch = [pltpu.VMEM((HEAD, HEAD), F32), tb(BF16), tb(BF16), tb(F32), tb(F32), tb(F32), tb(F32), tb(F32),
               pltpu.VMEM((nb, HEAD, HEAD), F32), pltpu.VMEM((nb, HEAD, HEAD), BF16)]
    aliases, wrap = _host(job, grid, in_specs, args, out_specs, out_shape, scratch)
    return pl.pallas_call(
        wrap(body), name="hgrn_bwd", grid=grid, in_specs=in_specs, out_specs=out_specs, out_shape=out_shape,
        scratch_shapes=scratch, input_output_aliases=aliases,
        compiler_params=_params(("parallel" if job is None else "arbitrary", "arbitrary")),
    )(*args)


def _local_step(x, target, mod, norm_mix_g, lb, gnorm_g, norm_ffn_g, norm_final_g, conv_w, bufs, c_idx, order):
    S, D = x.shape
    d_conv = D // 2
    d_in = 7 * d_conv + 2 * D
    d_ff = N_CHIPS * bufs["w_ffn_down"].shape[1]
    nb_in, nb_co, nb_ff = bufs["w_in"].shape[2], bufs["w_conv_out"].shape[2], bufs["w_ffn_gate"].shape[2]
    rows = lambda g: g.reshape(-1, g.shape[2])
    sh_m, sc_m, gt_m, sh_f, sc_f, gt_f = [mod[:, i * D:(i + 1) * D] for i in range(6)]
    tm = _pick(S, 512, 16)
    tm2 = _pick(S, 1024, 16)
    tn_in = _pick(nb_in, 1408, 128)
    tn_co = _pick(nb_co, 512, 128)
    tn_ff = _pick(nb_ff, 1408, 128)
    tn_d = _pick(D, 512, 128)
    tw = _pick(D, 1024, 128)
    tg = _pick(D, 512, 128)

    h = _norm_mod("norm_mix", x, norm_mix_g, sc_m, sh_m)
    proj, w_in = _proj_gather(h, bufs["w_in"], order, None)
    ob, o_raw, states, qe, ke, dec, *got = _hgrn_fwd(
        proj, lb, gnorm_g, d_conv,
        job=_GatherJob([bufs[k] for k in ("w_conv_out", "w_hgrn_out", "w_o", "w_ffn_gate")]))
    ya, w_conv_out, w_hgrn_out, w_o, w_ffn_gate = _conv_fwd(proj, conv_w, d_conv, job=_ForwardJob(got))
    w_o = rows(w_o)

    def merge_epi(accs, ex):
        y_a, y_b = accs
        return [y_a, y_b, _sig(ex[0]) * y_a + _sig(ex[1]) * y_b]

    y_a, y_b, merged, w_ffn_up = _matmul(
        "branch_out", [(ya, 'n', w_conv_out, 'n'), (ob, 'n', w_hgrn_out, 'n')], [0, 1], S, D, d_conv, tm2, tn_co, d_conv,
        [(proj, 'tile', 7 * d_conv), (proj, 'tile', 7 * d_conv + D)], [(BF16, None), (BF16, None), (BF16, None)], merge_epi,
        job=_GatherJob([bufs["w_ffn_up"]], 0, 2), rows_outer=True)

    def resid_epi(accs, ex):
        return [accs[0], ex[0] + ex[1] * accs[0]]

    mo, x1, w_ffn_up = _matmul("w_o", [(merged, 'n', w_o, 'n')], [0], S, D, D, tm2, tw, D,
                               [(x, 'tile', 0), (gt_m, 'row', 0)], [(BF16, None), (F32, None)], resid_epi,
                               job=_GatherJob([w_ffn_up], 1, 2))
    h2, w_ffn_up = _norm_mod("norm_ffn", x1, norm_ffn_g, sc_f, sh_f, job=_ForwardJob([w_ffn_up]))

    def swiglu_epi(accs, ex):
        gg, uu = accs
        return [gg, uu, gg * _sig(gg) * uu]

    G, U, act, w_ffn_down = _matmul(
        "ffn_in", [(h2, 'n', w_ffn_gate, 'n'), (h2, 'n', w_ffn_up, 'n')], [0, 1], S, d_ff, D,
        tm2, tn_ff, D, [], [(BF16, None), (BF16, None), (BF16, None)], swiglu_epi, job=_GatherJob([bufs["w_ffn_down"]]))
    w_ffn_down = rows(_forward_halves("forward_down", [w_ffn_down])[0])
    ff, x2 = _matmul("ffn_out", [(act, 'n', w_ffn_down, 'n')], [0], S, D, d_ff, tm2, tn_d, d_ff,
                     [(x1, 'tile', 0), (gt_f, 'row', 0)], [(BF16, None), (F32, None)], resid_epi, rows_outer=True)

    dx2, dffb, loss, dgt_f, dg_final = _loss_head(x2, target, norm_final_g, ff, gt_f)

    def swiglu_bwd_epi(accs, ex):
        dact, gg, uu = accs[0], ex[0], ex[1]
        s = _sig(gg)
        return [dact * uu * (s * (1.0 + gg * (1.0 - s))), dact * (gg * s)]

    dG, dU = _matmul("d_act", [(dffb, 'n', w_ffn_down, 't')], [0], S, d_ff, D, tm2, tn_ff, D,
                     [(G, 'tile', 0), (U, 'tile', 0)], [(BF16, None), (BF16, None)], swiglu_bwd_epi)
    dw_down, = _matmul("dw_ffn_down", [(act, 't', dffb, 'n')], [0], d_ff, D, S, _pick(d_ff, 512, 128),
                       D, S, [], [(BF16, None)], _plain)
    dh2, = _matmul("d_h2", [(dG, 'n', w_ffn_gate, 't'), (dU, 'n', w_ffn_up, 't')], [0, 0], S, D, d_ff,
                   tm, D, tn_ff, [], [(F32, None)], _plain)
    dw_gate, = _matmul("dw_ffn_gate", [(h2, 't', dG, 'n')], [0], D, d_ff, S, tg, tn_ff, S, [], [(BF16, nb_ff)], _plain)
    dw_up, = _matmul("dw_ffn_up", [(h2, 't', dU, 'n')], [0], D, d_ff, S, tg, tn_ff, S, [], [(BF16, nb_ff)], _plain)

    def pair_sums(names, grads, from_sibling):
        pairs = [_pair_sum("pair_sum_" + k, g, q, c_idx) for k, g, q in zip(names, grads, from_sibling)]
        return [p[0] for p in pairs], [p[1] for p in pairs]

    ffn_names = ["w_ffn_down", "w_ffn_gate", "w_ffn_up"]
    ffn_grads = [dw_down.reshape(N_CHIPS, -1, D), dw_gate, dw_up]
    dx1, dsh_f, dsc_f, dg_ffn, dmob, dgt_m, *from_sibling = _norm_mod_bwd(
        "norm_ffn_bwd", dh2, x1, norm_ffn_g, sc_f, dx2, (mo, gt_m), job=_SwapJob(ffn_grads))
    parts_f, slots_f = pair_sums(ffn_names, ffn_grads, from_sibling)

    def merge_bwd_epi(accs, ex):
        dm, ga, gb, ya_, yb_ = accs[0], ex[0], ex[1], ex[2], ex[3]
        sa, sb = _sig(ga), _sig(gb)
        return [dm * ya_ * sa * (1.0 - sa), dm * yb_ * sb * (1.0 - sb), dm * sa, dm * sb]

    dga, dgb, dy_a, dy_b = _matmul(
        "d_merged", [(dmob, 'n', w_o, 't')], [0], S, D, D, tm2, tn_co, D,
        [(proj, 'tile', 7 * d_conv), (proj, 'tile', 7 * d_conv + D), (y_a, 'tile', 0), (y_b, 'tile', 0)],
        [(BF16, None)] * 4, merge_bwd_epi, rows_outer=True)
    dw_o, = _matmul("dw_o", [(merged, 't', dmob, 'n')], [0], D, D, S, tg, D, S, [], [(BF16, None)], _plain)
    dya, dob = _matmul("d_branch", [(dy_a, 'n', w_conv_out, 't'), (dy_b, 'n', w_hgrn_out, 't')], [0, 1], S, d_conv, D,
                       tm2, _pick(d_conv, 1024, 128), tn_co, [], [(F32, None), (F32, None)], _plain)
    dw_co, dw_ho = _matmul("dw_branch", [(ya, 't', dy_a, 'n'), (ob, 't', dy_b, 'n')], [0, 1], d_conv, D, S,
                           _pick(d_conv, 512, 128), tn_co, S, [], [(BF16, nb_co), (BF16, nb_co)], _plain)
    branch_names = ["w_conv_out", "w_hgrn_out", "w_o"]
    branch_grads = [dw_co, dw_ho, dw_o.reshape(N_CHIPS, -1, D)]
    dab, dac, dax, dconv_w, *from_sibling = _conv_bwd(dya, proj, conv_w, d_conv, job=_SwapJob(branch_grads))
    parts_b, slots_b = pair_sums(branch_names, branch_grads, from_sibling)
    dq, dz, dv, dgo, dlb, dgn, *arrived_f = _hgrn_bwd(dob, o_raw, states, qe, ke, dec, proj, lb, gnorm_g, d_conv,
                                                      job=_ScatterJob(parts_f, slots_f))
    dproj = jnp.concatenate([dab, dac, dax, dq, dz, dv, dgo, dga, dgb], axis=1)
    halves_f = [_sum_chips("sum_chips_" + k, r, c_idx) for k, r in zip(ffn_names, arrived_f)]
    dw_in, *moved = _matmul("dw_in", [(h, 't', dproj, 'n')], [0], D, d_in, S, tg, tn_in, S, [], [(BF16, nb_in)], _plain,
                            job=_Jobs([_ShareJob(halves_f), _ScatterJob(parts_b, slots_b)]))
    whole_f, arrived_b = moved[:len(ffn_names)], moved[len(ffn_names):]
    parts_i, slots_i = pair_sums(["w_in"], [dw_in], _swap_halves("swap_in", [dw_in]))
    dh, arrived_in = _matmul("d_h", [(dproj, 'n', w_in, 't')], [0], S, D, d_in, tm2, D, tn_in, [], [(F32, None)], _plain,
                             job=_ScatterJob(parts_i, slots_i))
    dx, dsh_m, dsc_m, dg_mix = _norm_mod_bwd("norm_mix_bwd", dh, x, norm_mix_g, sc_m, dx1)
    dmod = jnp.concatenate([dsh_m, dsc_m, dgt_m, dsh_f, dsc_f, dgt_f], axis=1)
    small = dict(dmod=dmod, dg_mix=dg_mix, dg_ffn=dg_ffn, dg_final=dg_final, dlb=dlb,
                 dgn=jnp.sum(dgn, axis=0), dconv_w=dconv_w)
    late_names = branch_names + ["w_in"]
    late = _share_halves([_sum_chips("sum_chips_" + k, r, c_idx) for k, r in zip(late_names, arrived_b + [arrived_in])])
    big = dict(zip(ffn_names + late_names, whole_f + list(late)))
    return loss, dx, small, big


def _adamw_math(w, g, m, v):
    m = ADAM_B1 * m + (1.0 - ADAM_B1) * g
    v = ADAM_B2 * v + (1.0 - ADAM_B2) * (g * g)
    m_hat = m / (1.0 - ADAM_B1 ** ADAM_STEP)
    v_hat = v / (1.0 - ADAM_B2 ** ADAM_STEP)
    delta = -ADAM_LR * (m_hat / (jnp.sqrt(v_hat) + ADAM_EPS) + ADAM_WD * w)
    return delta, m, v


def _adamw(name, w, g, m, v):
    R, C = w.shape
    tr = _pick(R, max(8, (256 * 1024) // C // 8 * 8), 8)
    spec = pl.BlockSpec((tr, C), lambda i: (i, 0))

    def body(w_ref, g_ref, m_ref, v_ref, go_ref, d_ref, mo_ref, vo_ref):
        gv = g_ref[...]
        d, m2, v2 = _adamw_math(w_ref[...], gv, m_ref[...], v_ref[...])
        go_ref[...] = gv
        d_ref[...] = d
        mo_ref[...] = m2
        vo_ref[...] = v2

    return pl.pallas_call(
        body, name=name, grid=(R // tr,), in_specs=[spec] * 4, out_specs=[spec] * 4,
        out_shape=[jax.ShapeDtypeStruct((R, C), F32)] * 4, compiler_params=_params(("parallel",)),
    )(w, g, m, v)


def _ada_update(c_act_t, dmod, w, m, v):
    R, C = w.shape
    B = dmod.shape[0]
    tr = _pick(R, 256, 8)
    tc = _pick(C, 1536, 128)
    spec = pl.BlockSpec((tr, tc), lambda i, j: (i, j))

    def body(ct_ref, dm_ref, w_ref, m_ref, v_ref, g_ref, d_ref, mo_ref, vo_ref):
        ct = ct_ref[...]
        dm = dm_ref[...]
        g = ct[:, 0:1] * dm[0:1, :]
        for b in range(1, B):
            g = g + ct[:, b:b + 1] * dm[b:b + 1, :]
        d, m2, v2 = _adamw_math(w_ref[...], g, m_ref[...], v_ref[...])
        g_ref[...] = g
        d_ref[...] = d
        mo_ref[...] = m2
        vo_ref[...] = v2

    return pl.pallas_call(
        body, name="ada_update", grid=(R // tr, C // tc),
        in_specs=[pl.BlockSpec((tr, B), lambda i, j: (i, 0)), pl.BlockSpec((B, tc), lambda i, j: (0, j)),
                  spec, spec, spec],
        out_specs=[spec] * 4, out_shape=[jax.ShapeDtypeStruct((R, C), F32)] * 4,
        compiler_params=_params(("parallel", "parallel")),
    )(c_act_t, dmod, w, m, v)


def _prep(c_all, lb_param):
    def body(c_ref, p_ref, ca_ref, cab_ref, lb_ref):
        cv = c_ref[...]
        ca = cv * _sig(cv)
        ca_ref[...] = ca
        cab_ref[...] = ca.astype(BF16)
        lb_ref[...] = _sig(p_ref[0:1, :] - p_ref[1:2, :])

    return pl.pallas_call(
        body, name="prep",
        out_shape=[jax.ShapeDtypeStruct(c_all.shape, F32), jax.ShapeDtypeStruct(c_all.shape, BF16),
                   jax.ShapeDtypeStruct((1, lb_param.shape[1]), F32)],
    )(c_all, lb_param)


def _small_reduce(parts):
    W = parts.shape[1]

    def body(p_ref, o_ref):
        acc = p_ref[0:1, :]
        for d in range(1, N_DEV):
            acc = acc + p_ref[d:d + 1, :]
        o_ref[...] = acc

    return pl.pallas_call(body, name="small_reduce", out_shape=jax.ShapeDtypeStruct((1, W), F32))(parts)


def _lb_grad(dlb, lb):
    def body(d_ref, lb_ref, o_ref):
        t = d_ref[...] * lb_ref[...] * (1.0 - lb_ref[...])
        o_ref[0:1, :] = t
        o_ref[1:2, :] = -t

    return pl.pallas_call(body, name="lb_grad", out_shape=jax.ShapeDtypeStruct((2, dlb.shape[1]), F32))(dlb, lb)


def _place():
    x, y, c = lax.axis_index("x"), lax.axis_index("y"), lax.axis_index("c")
    chips = [(1 - x, y), (x, 1 - y), (1 - x, 1 - y)]
    return x, y, c, chips


def _allgather_rows(name, v):
    m_per, n = v.shape

    def body(x_ref, out_ref, send_sems, recv_sems, local_sem):
        x, y, c, chips = _place()
        me, sibling = (x, y, c), (x, y, 1 - c)

        def rows(px, py, pc):
            return out_ref.at[pl.ds((4 * px + 2 * py + pc) * m_per, m_per), :]

        def copy(k, block, to, src=None):
            return pltpu.make_async_remote_copy(
                src_ref=rows(*block) if src is None else src, dst_ref=rows(*block),
                send_sem=send_sems.at[k], recv_sem=recv_sems.at[k], device_id=to, device_id_type=MESH)

        mine = pltpu.make_async_copy(x_ref, rows(*me), local_sem)
        mine.start()
        first = [copy(0, me, sibling, src=x_ref)]
        first += [copy(1 + j, me, (*chip, c), src=x_ref) for j, chip in enumerate(chips)]
        for cp in first:
            cp.start()
        passed = [copy(4 + j, (*chip, c), sibling) for j, chip in enumerate(chips)]
        for j, chip in enumerate(chips):
            copy(1 + j, (*chip, c), me).wait_recv()
            passed[j].start()
        copy(0, sibling, me).wait_recv()
        for j, chip in enumerate(chips):
            copy(4 + j, (*chip, 1 - c), me).wait_recv()
        for cp in first + passed:
            cp.wait_send()
        mine.wait()

    return pl.pallas_call(
        body, name=name, out_shape=jax.ShapeDtypeStruct((N_DEV * m_per, n), v.dtype),
        in_specs=[pl.BlockSpec(memory_space=pltpu.VMEM)], out_specs=pl.BlockSpec(memory_space=pltpu.VMEM),
        scratch_shapes=[pltpu.SemaphoreType.DMA((7,)), pltpu.SemaphoreType.DMA((7,)), pltpu.SemaphoreType.DMA],
    )(v)


def _cast_place(name, w, k_idx):
    R, C = w.shape
    tr = _pick(R, max(16, (512 * 1024) // C // 16 * 16), 16)

    def body(k_ref, w_ref, o_ref):
        o_ref[...] = w_ref[...].astype(BF16)

    return pl.pallas_call(
        body, name=name,
        grid_spec=pltpu.PrefetchScalarGridSpec(
            num_scalar_prefetch=1, grid=(R // tr,),
            in_specs=[pl.BlockSpec((tr, C), lambda i, k_ref: (i, 0))],
            out_specs=pl.BlockSpec((None, tr, C), lambda i, k_ref: (k_ref[0], i, 0))),
        out_shape=jax.ShapeDtypeStruct((N_CHIPS, R, C), BF16),
        compiler_params=_params(("parallel",)),
    )(k_idx, w)


def _chip_copies(src_of, dst_of, got_of, n, sems, receiving):
    x, y, c, chips = _place()
    k_me = 2 * x + y
    send_sems, recv_sems = sems
    out = []
    for a in range(n):
        for j, chip in enumerate(chips):
            k_chip = 2 * chip[0] + chip[1]
            if receiving:
                src = dst = got_of(a, k_chip, c)
                to = (x, y, c)
            else:
                src, dst, to = src_of(a, k_chip, k_me, c), dst_of(a, k_me, c), (*chip, c)
            out.append(pltpu.make_async_remote_copy(
                src_ref=src, dst_ref=dst, send_sem=send_sems.at[3 * a + j], recv_sem=recv_sems.at[3 * a + j],
                device_id=to, device_id_type=MESH))
    return out


class _ChipJob:
    def start(self, j_in, j_out, sems):
        for send in self.copies(j_in, j_out, sems, False):
            send.start()

    def finish(self, j_in, j_out, sems):
        for recv in self.copies(j_in, j_out, sems, True):
            recv.wait_recv()
        for send in self.copies(j_in, j_out, sems, False):
            send.wait_send()


class _GatherJob(_ChipJob):
    def __init__(self, bufs, part=0, n_parts=1):
        n = len(bufs)
        self.inputs, self.n_alias = list(bufs), n
        self.sem_shapes = [pltpu.SemaphoreType.DMA((3 * n,)), pltpu.SemaphoreType.DMA((3 * n,))]
        self.half = [b.shape[1] // 2 for b in bufs]
        self.part, self.n_parts = part, n_parts

    def copies(self, j_in, j_out, sems, receiving):
        def half(a, k, c):
            rows = self.half[a] // self.n_parts
            return j_out[a].at[k, pl.ds(c * self.half[a] + self.part * rows, rows)]

        return _chip_copies(lambda a, k_chip, k_me, c: half(a, k_me, c), half, half, len(self.half), sems, receiving)


class _ScatterJob(_ChipJob):
    def __init__(self, parts, slots):
        n = len(parts)
        self.n = n
        self.inputs, self.n_alias = list(parts) + list(slots), n
        self.sem_shapes = [pltpu.SemaphoreType.DMA((3 * n,)), pltpu.SemaphoreType.DMA((3 * n,))]

    def copies(self, j_in, j_out, sems, receiving):
        return _chip_copies(lambda a, k_chip, k_me, c: j_in[a].at[k_chip], lambda a, k_me, c: j_out[a].at[k_me],
                            lambda a, k_chip, c: j_out[a].at[k_chip], self.n, sems, receiving)


class _SwapJob(_ChipJob):
    def __init__(self, grads):
        n = len(grads)
        self.n = n
        self.half = [g.shape[1] // 2 for g in grads]
        landing = [lax.empty((N_CHIPS, g.shape[1] // 2, g.shape[2]), g.dtype) for g in grads]
        self.inputs, self.n_alias = list(grads) + landing, n
        self.sem_shapes = [pltpu.SemaphoreType.DMA((n,)), pltpu.SemaphoreType.DMA((n,))]

    def copies(self, j_in, j_out, sems, receiving):
        x, y, c, _ = _place()
        out = []
        for a in range(self.n):
            if receiving:
                src, to = j_out[a], (x, y, c)
            else:
                src, to = j_in[a].at[:, pl.ds((1 - c) * self.half[a], self.half[a])], (x, y, 1 - c)
            out.append(pltpu.make_async_remote_copy(src_ref=src, dst_ref=j_out[a], send_sem=sems[0].at[a],
                                                    recv_sem=sems[1].at[a], device_id=to, device_id_type=MESH))
        return out


class _Jobs:
    def __init__(self, jobs):
        self.jobs = jobs
        split = [(j.inputs[:len(j.inputs) - j.n_alias], j.inputs[len(j.inputs) - j.n_alias:]) for j in jobs]
        self.inputs = [a for plain, _ in split for a in plain] + [a for _, aliased in split for a in aliased]
        self.n_alias = sum(j.n_alias for j in jobs)
        self.sem_shapes = [s for j in jobs for s in j.sem_shapes]

    def _each(self, j_in, j_out, sems):
        n_plain = len(j_in) - self.n_alias
        p0 = a0 = s0 = 0
        for j in self.jobs:
            n_p, n_s = len(j.inputs) - j.n_alias, len(j.sem_shapes)
            ins = list(j_in[p0:p0 + n_p]) + list(j_in[n_plain + a0:n_plain + a0 + j.n_alias])
            yield j, ins, j_out[a0:a0 + j.n_alias], sems[s0:s0 + n_s]
            p0, a0, s0 = p0 + n_p, a0 + j.n_alias, s0 + n_s

    def start(self, j_in, j_out, sems):
        for j, ins, outs, s in self._each(j_in, j_out, sems):
            j.start(ins, outs, s)

    def finish(self, j_in, j_out, sems):
        for j, ins, outs, s in self._each(j_in, j_out, sems):
            j.finish(ins, outs, s)


class _ShareJob(_ChipJob):
    def __init__(self, bufs):
        n = len(bufs)
        self.n = n
        self.inputs, self.n_alias = list(bufs), n
        self.sem_shapes = [pltpu.SemaphoreType.DMA((n,)), pltpu.SemaphoreType.DMA((n,))]

    def copies(self, j_in, j_out, sems, receiving):
        x, y, c, _ = _place()
        out = []
        for a in range(self.n):
            hc, to = (1 - c, (x, y, c)) if receiving else (c, (x, y, 1 - c))
            out.append(pltpu.make_async_remote_copy(
                src_ref=j_out[a].at[hc], dst_ref=j_out[a].at[hc], send_sem=sems[0].at[a], recv_sem=sems[1].at[a],
                device_id=to, device_id_type=MESH))
        return out


class _ForwardJob(_ChipJob):
    def __init__(self, bufs):
        n = len(bufs)
        self.n = n
        self.half = [b.shape[1] // 2 for b in bufs]
        self.inputs, self.n_alias = list(bufs), n
        self.sem_shapes = [pltpu.SemaphoreType.DMA((3 * n,)), pltpu.SemaphoreType.DMA((3 * n,))]

    def copies(self, j_in, j_out, sems, receiving):
        x, y, c, chips = _place()
        out = []
        for a in range(self.n):
            for q, chip in enumerate(chips):
                hc, to = (1 - c, (x, y, c)) if receiving else (c, (x, y, 1 - c))
                part = j_out[a].at[2 * chip[0] + chip[1], pl.ds(hc * self.half[a], self.half[a])]
                out.append(pltpu.make_async_remote_copy(
                    src_ref=part, dst_ref=part, send_sem=sems[0].at[3 * a + q], recv_sem=sems[1].at[3 * a + q],
                    device_id=to, device_id_type=MESH))
        return out


def _forward_halves(name, bufs):
    n = len(bufs)

    def body(*refs):
        out_refs = refs[n:2 * n]
        send_sems, recv_sems = refs[2 * n:]
        x, y, c, chips = _place()
        started, waits = [], []
        for a in range(n):
            rh = bufs[a].shape[1] // 2
            for j, chip in enumerate(chips):
                k_chip = 2 * chip[0] + chip[1]
                got = out_refs[a].at[k_chip, pl.ds(c * rh, rh)]
                cp = pltpu.make_async_remote_copy(src_ref=got, dst_ref=got, send_sem=send_sems.at[3 * a + j],
                                                  recv_sem=recv_sems.at[3 * a + j], device_id=(x, y, 1 - c),
                                                  device_id_type=MESH)
                cp.start()
                started.append(cp)
                other = out_refs[a].at[k_chip, pl.ds((1 - c) * rh, rh)]
                waits.append(pltpu.make_async_remote_copy(
                    src_ref=other, dst_ref=other, send_sem=send_sems.at[3 * a + j], recv_sem=recv_sems.at[3 * a + j],
                    device_id=(x, y, c), device_id_type=MESH))
        for cp in waits:
            cp.wait_recv()
        for cp in started:
            cp.wait_send()

    return pl.pallas_call(
        body, name=name, out_shape=[jax.ShapeDtypeStruct(b.shape, b.dtype) for b in bufs],
        in_specs=[_HBM] * n, out_specs=[_HBM] * n, input_output_aliases={a: a for a in range(n)},
        scratch_shapes=[pltpu.SemaphoreType.DMA((3 * n,)), pltpu.SemaphoreType.DMA((3 * n,))],
    )(*bufs)


def _proj_gather(h, buf, order, job):
    S, D = h.shape
    nb = buf.shape[2]
    tm = _pick(S, 1024, 16)
    tn = _pick(nb, 1408, 128)
    per = nb // tn
    nj, ni = N_CHIPS * per, S // tm
    rh = D // 2

    def body(order_ref, h_ref, buf_in, proj_ref, buf_ref, wbuf, tile_sems, ici_send, ici_recv, d2d_send, d2d_recv):
        j, i = pl.program_id(0), pl.program_id(1)
        x, y, c, chips = _place()
        k_me = 2 * x + y

        def half(k, hc):
            return buf_ref.at[k, pl.ds(hc * rh, rh)]

        def ici(q, receiving):
            k_chip = 2 * chips[q][0] + chips[q][1]
            src, to = (half(k_chip, c), (x, y, c)) if receiving else (half(k_me, c), (*chips[q], c))
            return pltpu.make_async_remote_copy(src_ref=src, dst_ref=src, send_sem=ici_send.at[q],
                                                recv_sem=ici_recv.at[q], device_id=to, device_id_type=MESH)

        def d2d(q, receiving):
            k_chip = 2 * chips[q][0] + chips[q][1]
            src, to = (half(k_chip, 1 - c), (x, y, c)) if receiving else (half(k_chip, c), (x, y, 1 - c))
            return pltpu.make_async_remote_copy(src_ref=src, dst_ref=src, send_sem=d2d_send.at[q],
                                                recv_sem=d2d_recv.at[q], device_id=to, device_id_type=MESH)

        def tile_copy(t):
            col = pl.multiple_of((t % per) * tn, 128)
            return pltpu.make_async_copy(buf_ref.at[order_ref[t // per], :, pl.ds(col, tn)], wbuf.at[t % 2],
                                         tile_sems.at[t % 2])

        @pl.when(jnp.logical_and(j == 0, i == 0))
        def _():
            ici(0, False).start()
            ici(1, False).start()
            tile_copy(0).start()

        @pl.when(i == 0)
        def _():
            tile_copy(j).wait()
            for q in range(3):
                @pl.when(j + 1 == (q + 1) * per)
                def _():
                    ici(q, True).wait_recv()
                    d2d(q, False).start()
                    if q == 0:
                        ici(0, False).wait_send()
                        ici(1, False).wait_send()
                        ici(2, False).start()
                    d2d(q, True).wait_recv()

            @pl.when(j + 1 < nj)
            def _():
                tile_copy(j + 1).start()

        proj_ref[...] = jnp.dot(h_ref[...], wbuf[j % 2], preferred_element_type=F32)

        @pl.when(jnp.logical_and(j == nj - 1, i == ni - 1))
        def _():
            ici(2, False).wait_send()
            for q in range(3):
                d2d(q, False).wait_send()

    grid = (nj, ni)
    in_specs = [pl.BlockSpec((tm, D), lambda j, i, o: (i, 0)), _HBM]
    args = [h, buf]
    out_specs = [pl.BlockSpec((tm, tn), lambda j, i, o: (i, o[j // per] * per + j % per)), _HBM]
    out_shape = [jax.ShapeDtypeStruct((S, N_CHIPS * nb), F32), jax.ShapeDtypeStruct(buf.shape, buf.dtype)]
    scratch = [pltpu.VMEM((2, D, tn), BF16), pltpu.SemaphoreType.DMA((2,))] + [pltpu.SemaphoreType.DMA((3,))] * 4
    aliases, wrap = _host(job, grid, in_specs, args, out_specs, out_shape, scratch, n_prefetch=1)
    aliases[2] = 1
    return pl.pallas_call(
        wrap(body), name="proj",
        grid_spec=pltpu.PrefetchScalarGridSpec(num_scalar_prefetch=1, grid=grid, in_specs=in_specs, out_specs=out_specs,
                                               scratch_shapes=scratch),
        out_shape=out_shape, input_output_aliases=aliases, compiler_params=_params(("arbitrary", "arbitrary")),
    )(order, *args)


def _swap_halves(name, grads):
    n = len(grads)

    def body(*refs):
        in_refs, out_refs = refs[:n], refs[n:2 * n]
        send_sems, recv_sems = refs[2 * n:]
        x, y, c, _ = _place()
        cps = []
        for a in range(n):
            rh = grads[a].shape[1] // 2
            cp = pltpu.make_async_remote_copy(
                src_ref=in_refs[a].at[:, pl.ds((1 - c) * rh, rh)], dst_ref=out_refs[a],
                send_sem=send_sems.at[a], recv_sem=recv_sems.at[a], device_id=(x, y, 1 - c), device_id_type=MESH)
            cp.start()
            cps.append(cp)
        for cp in cps:
            cp.wait()

    return pl.pallas_call(
        body, name=name,
        out_shape=[jax.ShapeDtypeStruct((N_CHIPS, g.shape[1] // 2, g.shape[2]), g.dtype) for g in grads],
        in_specs=[_HBM] * n, out_specs=[_HBM] * n,
        scratch_shapes=[pltpu.SemaphoreType.DMA((n,)), pltpu.SemaphoreType.DMA((n,))],
    )(*grads)


def _pair_sum(name, g, q, c_idx):
    _, R, C = g.shape
    rh = R // 2
    tr = _pick(rh, max(16, (512 * 1024) // C // 16 * 16), 16)
    nh = rh // tr

    def body(c_ref, g_ref, q_ref, o_ref, o2_ref):
        s = (g_ref[...].astype(F32) + q_ref[...].astype(F32)).astype(BF16)
        o_ref[...] = s
        o2_ref[...] = s

    out_spec = pl.BlockSpec((None, tr, C), lambda k, i, c_ref: (k, i, 0))
    return pl.pallas_call(
        body, name=name,
        grid_spec=pltpu.PrefetchScalarGridSpec(
            num_scalar_prefetch=1, grid=(N_CHIPS, nh),
            in_specs=[pl.BlockSpec((None, tr, C), lambda k, i, c_ref: (k, c_ref[0] * nh + i, 0)),
                      pl.BlockSpec((None, tr, C), lambda k, i, c_ref: (k, i, 0))],
            out_specs=[out_spec, out_spec]),
        out_shape=[jax.ShapeDtypeStruct((N_CHIPS, rh, C), BF16)] * 2,
        compiler_params=_params(("parallel", "parallel")),
    )(c_idx, g, q)


def _sum_chips(name, r, c_idx):
    _, rh, C = r.shape
    tr = _pick(rh, max(16, (256 * 1024) // C // 16 * 16), 16)

    def body(c_ref, r_ref, o_ref):
        acc = r_ref[0].astype(F32)
        for k in range(1, N_CHIPS):
            acc = acc + r_ref[k].astype(F32)
        o_ref[...] = acc

    return pl.pallas_call(
        body, name=name,
        grid_spec=pltpu.PrefetchScalarGridSpec(
            num_scalar_prefetch=1, grid=(rh // tr,),
            in_specs=[pl.BlockSpec((N_CHIPS, tr, C), lambda i, c_ref: (0, i, 0))],
            out_specs=pl.BlockSpec((None, tr, C), lambda i, c_ref: (c_ref[0], i, 0))),
        out_shape=jax.ShapeDtypeStruct((2, rh, C), F32), compiler_params=_params(("parallel",)),
    )(c_idx, r)


def _share_halves(bufs):
    n = len(bufs)

    def body(*refs):
        out_refs = refs[n:2 * n]
        send_sems, recv_sems = refs[2 * n:]
        x, y, c, _ = _place()
        cps = []
        for a in range(n):
            cp = pltpu.make_async_remote_copy(
                src_ref=out_refs[a].at[c], dst_ref=out_refs[a].at[c], send_sem=send_sems.at[a],
                recv_sem=recv_sems.at[a], device_id=(x, y, 1 - c), device_id_type=MESH)
            cp.start()
            cps.append(cp)
        for a, cp in enumerate(cps):
            got = out_refs[a].at[1 - c]
            pltpu.make_async_remote_copy(src_ref=got, dst_ref=got, send_sem=send_sems.at[a], recv_sem=recv_sems.at[a],
                                         device_id=(x, y, c), device_id_type=MESH).wait_recv()
        for cp in cps:
            cp.wait_send()

    return pl.pallas_call(
        body, name="share_halves",
        out_shape=[jax.ShapeDtypeStruct(b.shape, b.dtype) for b in bufs],
        in_specs=[_HBM] * n, out_specs=[_HBM] * n, input_output_aliases={a: a for a in range(n)},
        scratch_shapes=[pltpu.SemaphoreType.DMA((n,)), pltpu.SemaphoreType.DMA((n,))],
    )(*bufs)


_BIG = ("w_in", "w_conv_out", "w_hgrn_out", "w_o", "w_ffn_gate", "w_ffn_up", "w_ffn_down")
_ROW_SHARDED = ("w_o", "w_ffn_down")
_WEIGHTS = ("w_ada", "b_ada", "norm_mix_g", "w_in", "conv_w", "lb_param", "gnorm_g", "w_conv_out", "w_hgrn_out",
            "w_o", "norm_ffn_g", "w_ffn_gate", "w_ffn_up", "w_ffn_down", "norm_final_g")


def _pack_rows(pieces):
    flat = jnp.concatenate([p.reshape(-1) for p in pieces])
    pad = (-flat.shape[0]) % 1024
    return jnp.pad(flat, (0, pad)).reshape(8, -1)


def kernel(x, c, w_ada, b_ada, norm_mix_g, w_in, conv_w, lb_param, gnorm_g, w_conv_out, w_hgrn_out, w_o, norm_ffn_g, w_ffn_gate, w_ffn_up, w_ffn_down, norm_final_g, loss_target, m_w_ada, m_b_ada, m_norm_mix_g, m_w_in, m_conv_w, m_lb_param, m_gnorm_g, m_w_conv_out, m_w_hgrn_out, m_w_o, m_norm_ffn_g, m_w_ffn_gate, m_w_ffn_up, m_w_ffn_down, m_norm_final_g, v_w_ada, v_b_ada, v_norm_mix_g, v_w_in, v_conv_w, v_lb_param, v_gnorm_g, v_w_conv_out, v_w_hgrn_out, v_w_o, v_norm_ffn_g, v_w_ffn_gate, v_w_ffn_up, v_w_ffn_down, v_norm_final_g):
    w = dict(w_ada=w_ada, b_ada=b_ada, norm_mix_g=norm_mix_g, w_in=w_in, conv_w=conv_w, lb_param=lb_param,
             gnorm_g=gnorm_g, w_conv_out=w_conv_out, w_hgrn_out=w_hgrn_out, w_o=w_o, norm_ffn_g=norm_ffn_g,
             w_ffn_gate=w_ffn_gate, w_ffn_up=w_ffn_up, w_ffn_down=w_ffn_down, norm_final_g=norm_final_g)
    m = dict(w_ada=m_w_ada, b_ada=m_b_ada, norm_mix_g=m_norm_mix_g, w_in=m_w_in, conv_w=m_conv_w, lb_param=m_lb_param,
             gnorm_g=m_gnorm_g, w_conv_out=m_w_conv_out, w_hgrn_out=m_w_hgrn_out, w_o=m_w_o, norm_ffn_g=m_norm_ffn_g,
             w_ffn_gate=m_w_ffn_gate, w_ffn_up=m_w_ffn_up, w_ffn_down=m_w_ffn_down, norm_final_g=m_norm_final_g)
    v = dict(w_ada=v_w_ada, b_ada=v_b_ada, norm_mix_g=v_norm_mix_g, w_in=v_w_in, conv_w=v_conv_w, lb_param=v_lb_param,
             gnorm_g=v_gnorm_g, w_conv_out=v_w_conv_out, w_hgrn_out=v_w_hgrn_out, w_o=v_w_o, norm_ffn_g=v_norm_ffn_g,
             w_ffn_gate=v_w_ffn_gate, w_ffn_up=v_w_ffn_up, w_ffn_down=v_w_ffn_down, norm_final_g=v_norm_final_g)
    two_d = lambda a: a.reshape((-1, a.shape[-1])) if a.ndim != 2 else a
    shapes = {k: a.shape for k, a in w.items()}
    w, m, v = ({k: two_d(a) for k, a in t.items()} for t in (w, m, v))
    w["lb_param"], m["lb_param"], v["lb_param"] = lb_param, m_lb_param, v_lb_param
    S, D = x.shape[1], x.shape[2]
    d_conv = D // 2
    ix, iy, ic = lax.axis_index("x"), lax.axis_index("y"), lax.axis_index("c")
    k_me = 2 * ix + iy
    dev = 2 * k_me + ic
    cw_shard = w["conv_w"].shape[1]
    ada_cols = w["w_ada"].shape[1]

    got = _allgather_rows("gather_cond", _pack_rows([c, w["conv_w"]])).reshape(N_DEV, -1)
    c_all = got[:, :D]
    conv_full = got[::2, D:D + 3 * cw_shard].reshape(N_CHIPS, 3, cw_shard).transpose(1, 0, 2).reshape(3, -1)
    c_act, c_act_b, lb = _prep(c_all, lb_param)
    b_cols = lax.dynamic_slice(w["b_ada"], (0, k_me * ada_cols), (1, ada_cols))
    mod_cols, = _matmul("ada_fwd", [(c_act_b, 'n', w["w_ada"].astype(BF16), 'n')], [0], N_DEV, ada_cols, D,
                        N_DEV, _pick(ada_cols, 1536, 128), D, [(b_cols, 'row', 0)], [(F32, None)],
                        lambda accs, ex: [accs[0] + ex[0]])
    mod_all = _allgather_rows("gather_mod", mod_cols).reshape(N_CHIPS, 2, N_DEV, ada_cols)[:, 0]
    mod_all = mod_all.transpose(1, 0, 2).reshape(N_DEV, -1)
    mod = lax.dynamic_slice(mod_all, (dev, 0), (1, mod_all.shape[1]))
    k_idx = jnp.reshape(k_me, (1,)).astype(jnp.int32)
    c_idx = jnp.reshape(ic, (1,)).astype(jnp.int32)
    bufs = {k: _cast_place("cast_" + k, w[k], k_idx) for k in _BIG}
    order = jnp.stack([k_me, 2 * (1 - ix) + iy, 2 * ix + (1 - iy), 2 * (1 - ix) + (1 - iy)]).astype(jnp.int32)

    loss, dx, small, arrived = _local_step(
        x[0], loss_target[0], mod, w["norm_mix_g"], lb, w["gnorm_g"], w["norm_ffn_g"], w["norm_final_g"], conv_full,
        bufs, c_idx, order)

    pieces = [small["dmod"], small["dg_mix"], small["dg_ffn"], small["dg_final"], small["dlb"], small["dgn"],
              small["dconv_w"], loss]
    sizes = [p.size for p in pieces]
    parts = _allgather_rows("gather_small", _pack_rows(pieces)).reshape(N_DEV, -1)
    total = _small_reduce(parts)
    offs = [0]
    for s in sizes:
        offs.append(offs[-1] + s)
    tot = [total[:, offs[i]:offs[i + 1]] for i in range(len(sizes))]
    dmod_all = parts[:, :sizes[0]]
    grads = {
        "b_ada": tot[0], "norm_mix_g": tot[1], "norm_ffn_g": tot[2], "norm_final_g": tot[3],
        "lb_param": _lb_grad(tot[4], lb), "gnorm_g": tot[5],
        "conv_w": lax.dynamic_slice(tot[6].reshape(3, -1), (0, k_me * cw_shard), (3, cw_shard)),
    }
    loss_out = tot[7][0, 0]

    for k in _BIG:
        grads[k] = arrived[k].reshape(-1, arrived[k].shape[-1])

    delta, new_m, new_v = {}, {}, {}
    dmod_cols = lax.dynamic_slice(dmod_all, (0, k_me * ada_cols), (N_DEV, ada_cols))
    grads["w_ada"], delta["w_ada"], new_m["w_ada"], new_v["w_ada"] = _ada_update(
        c_act.T, dmod_cols, w["w_ada"], m["w_ada"], v["w_ada"])
    for k in _WEIGHTS:
        if k != "w_ada":
            grads[k], delta[k], new_m[k], new_v[k] = _adamw("adamw_" + k, w[k], grads[k], m[k], v[k])
    outs = [loss_out, dx.reshape(x.shape)]
    for t in (grads, delta, new_m, new_v):
        outs += [t[k].reshape(shapes[k]) for k in _WEIGHTS]
    return tuple(outs)
```

```python
import functools

import jax
import jax.numpy as jnp
from jax import lax
from jax.experimental import pallas as pl
from jax.experimental.pallas import tpu as pltpu

F32 = jnp.float32
BF16 = jnp.bfloat16
MESH = pl.DeviceIdType.MESH

EPS = 1e-6
HEAD = 128
BLK = 16
N_CHIPS = 4
N_DEV = 8
VMEM_LIMIT_BYTES = 56 * 1024 * 1024

ADAM_LR = 0.001
ADAM_B1 = 0.9
ADAM_B2 = 0.999
ADAM_EPS = 1e-08
ADAM_WD = 0.01
ADAM_STEP = 10


def _pick(dim, pref, mult):
    if dim <= pref:
        return dim
    t = (pref // mult) * mult
    while t >= mult:
        if dim % t == 0:
            return t
        t -= mult
    return dim


def _sig(x):
    return 1.0 / (1.0 + jnp.exp(-x))


def _params(sem):
    return pltpu.CompilerParams(dimension_semantics=sem, vmem_limit_bytes=VMEM_LIMIT_BYTES)


def _gj(j, i, k):
    return j


def _gk(j, i, k):
    return k


def _wspec(arr, rt, ct, r_of, c_of):
    if arr.ndim == 2:
        return pl.BlockSpec((rt, ct), lambda j, i, k: (r_of(j, i, k), c_of(j, i, k)))
    per = arr.shape[2] // ct
    assert arr.shape[2] % ct == 0
    return pl.BlockSpec((None, rt, ct),
                        lambda j, i, k: (c_of(j, i, k) // per, r_of(j, i, k), c_of(j, i, k) % per))


_HBM = pl.BlockSpec(memory_space=pltpu.HBM)


def _host(job, grid, in_specs, args, out_specs, out_shape, scratch, n_prefetch=0):
    if job is None:
        return {}, (lambda body: body)
    n_in, n_out, n_scr = len(args), len(out_shape), len(scratch)
    n_job, n_alias = len(job.inputs), job.n_alias
    in_specs += [_HBM] * n_job
    args += job.inputs
    out_specs += [_HBM] * n_alias
    out_shape += [jax.ShapeDtypeStruct(a.shape, a.dtype) for a in job.inputs[n_job - n_alias:]]
    scratch += job.sem_shapes
    aliases = {n_prefetch + n_in + n_job - n_alias + t: n_out + t for t in range(n_alias)}

    def wrap(body):
        def hosted(*refs):
            pre, refs = refs[:n_prefetch], refs[n_prefetch:]
            ins, refs = refs[:n_in], refs[n_in:]
            j_in, refs = refs[:n_job], refs[n_job:]
            outs, refs = refs[:n_out], refs[n_out:]
            j_out, refs = refs[:n_alias], refs[n_alias:]
            scr, sems = refs[:n_scr], refs[n_scr:]
            first = functools.reduce(jnp.logical_and, [pl.program_id(d) == 0 for d in range(len(grid))])
            last = functools.reduce(jnp.logical_and, [pl.program_id(d) == grid[d] - 1 for d in range(len(grid))])

            @pl.when(first)
            def _():
                job.start(j_in, j_out, sems)

            body(*pre, *ins, *outs, *scr)

            @pl.when(last)
            def _():
                job.finish(j_in, j_out, sems)

        return hosted

    return aliases, wrap


EPILOGUE_COLS = 768


def _plain(accs, extras):
    return accs


def _matmul(name, pairs, acc_of, M, N, K, tm, tn, tk, extras, outs, epilogue, job=None, rows_outer=False):
    assert M % tm == 0 and N % tn == 0 and K % tk == 0, (name, M, N, K, tm, tn, tk)
    nj, ni, nk = N // tn, M // tm, K // tk
    n_pairs, n_extra, n_out = len(pairs), len(extras), len(outs)
    n_acc = max(acc_of) + 1
    in_specs, args, dims = [], [], []
    lhs_of, seen = [], {}
    for a, am, b, bm in pairs:
        if (id(a), am) not in seen:
            seen[id(a), am] = len(args)
            args.append(a)
            if am == 'n':
                in_specs.append(pl.BlockSpec((tm, tk), lambda j, i, k: (i, k)))
            else:
                in_specs.append(pl.BlockSpec((tk, tm), lambda j, i, k: (k, i)))
        lhs_of.append(seen[id(a), am])
    n_lhs = len(args)
    for a, am, b, bm in pairs:
        if bm == 'n':
            in_specs.append(_wspec(b, tk, tn, _gk, _gj))
        else:
            in_specs.append(_wspec(b, tn, tk, _gj, _gk))
        dims.append((((1 if am == 'n' else 0,), (0 if bm == 'n' else 1,)), ((), ())))
        args.append(b)
    for e, kind, off in extras:
        assert off % tn == 0, (name, off, tn)
        o = off // tn
        if kind == 'tile':
            in_specs.append(pl.BlockSpec((tm, tn), lambda j, i, k, o=o: (i, j + o)))
        else:
            in_specs.append(pl.BlockSpec((1, tn), lambda j, i, k, o=o: (0, j + o)))
        args.append(e)
    out_shape, out_specs = [], []
    for dt, nb in outs:
        if nb is None:
            out_shape.append(jax.ShapeDtypeStruct((M, N), dt))
            out_specs.append(pl.BlockSpec((tm, tn), lambda j, i, k: (i, j)))
        else:
            assert nb % tn == 0 and N % nb == 0
            per = nb // tn
            out_shape.append(jax.ShapeDtypeStruct((N // nb, M, nb), dt))
            out_specs.append(pl.BlockSpec((None, tm, tn), lambda j, i, k, per=per: (j // per, i, j % per)))

    def body(*refs):
        n_ab = n_lhs + n_pairs
        e_refs = refs[n_ab:n_ab + n_extra]
        o_refs = refs[n_ab + n_extra:n_ab + n_extra + n_out]
        acc_refs = refs[n_ab + n_extra + n_out:]
        def products(cols):
            sums = [None] * n_acc
            for p in range(n_pairs):
                b_ref = refs[n_lhs + p]
                b = b_ref[:, cols] if pairs[p][3] == 'n' else b_ref[cols, :]
                prod = lax.dot_general(refs[lhs_of[p]][...], b, dims[p], preferred_element_type=F32)
                a = acc_of[p]
                sums[a] = prod if sums[a] is None else sums[a] + prod
            return sums

        def finish(accs, cols):
            res = epilogue(accs, [e[:, cols] for e in e_refs])
            for o_ref, r in zip(o_refs, res):
                o_ref[:, cols] = r.astype(o_ref.dtype)

        if nk == 1 and epilogue is not _plain:
            for c0 in range(0, tn, EPILOGUE_COLS):
                cols = slice(c0, min(c0 + EPILOGUE_COLS, tn))
                finish(products(cols), cols)
            return
        whole = slice(0, tn)
        sums = products(whole)
        if nk == 1:
            finish(sums, whole)
        else:
            k = pl.program_id(2)
            targets = o_refs if sum_in_outs else acc_refs

            @pl.when(k == 0)
            def _():
                for a in range(n_acc):
                    targets[a][...] = sums[a]

            @pl.when(k > 0)
            def _():
                for a in range(n_acc):
                    targets[a][...] += sums[a]

            if not sum_in_outs:
                @pl.when(k == nk - 1)
                def _():
                    finish([acc_refs[a][...] for a in range(n_acc)], whole)

    sum_in_outs = epilogue is _plain and nk > 1 and n_out == n_acc and all(dt == F32 for dt, _ in outs)
    scratch = [pltpu.VMEM((tm, tn), F32) for _ in range(n_acc)] if nk > 1 and not sum_in_outs else []
    grid = (nj, ni, nk)
    if rows_outer:
        grid = (ni, nj, nk)
        swap = lambda spec: pl.BlockSpec(spec.block_shape, lambda i, j, k, f=spec.index_map: f(j, i, k))
        in_specs, out_specs = [swap(s) for s in in_specs], [swap(s) for s in out_specs]
    aliases, wrap = _host(job, grid, in_specs, args, out_specs, out_shape, scratch)
    sem = ("parallel", "parallel", "arbitrary") if job is None else ("arbitrary",) * 3
    return pl.pallas_call(
        wrap(body), name=name, grid=grid, in_specs=in_specs, out_specs=out_specs, out_shape=out_shape,
        scratch_shapes=scratch, input_output_aliases=aliases, compiler_params=_params(sem),
    )(*args)


def _row_spec(tr, d):
    return pl.BlockSpec((tr, d), lambda i: (i, 0))


def _vec_spec(d):
    return pl.BlockSpec((1, d), lambda i: (0, 0))


def _norm_mod(name, x, g, sc, sh, job=None):
    S, D = x.shape
    tr = _pick(S, 256, 8)

    def body(x_ref, g_ref, sc_ref, sh_ref, h_ref):
        xv = x_ref[...]
        r = lax.rsqrt(jnp.mean(xv * xv, axis=-1, keepdims=True) + EPS)
        h_ref[...] = ((xv * r) * g_ref[...] * (1.0 + sc_ref[...]) + sh_ref[...]).astype(BF16)

    grid = (S // tr,)
    in_specs = [_row_spec(tr, D), _vec_spec(D), _vec_spec(D), _vec_spec(D)]
    args = [x, g, sc, sh]
    out_specs, out_shape, scratch = [_row_spec(tr, D)], [jax.ShapeDtypeStruct((S, D), BF16)], []
    aliases, wrap = _host(job, grid, in_specs, args, out_specs, out_shape, scratch)
    res = pl.pallas_call(
        wrap(body), name=name, grid=grid, in_specs=in_specs, out_specs=out_specs, out_shape=out_shape,
        scratch_shapes=scratch, input_output_aliases=aliases,
        compiler_params=_params(("parallel" if job is None else "arbitrary",)),
    )(*args)
    return res[0] if job is None else res


def _loss_head(x2, target, g, ff, gt):
    S, D = x2.shape
    tr = _pick(S, 256, 8)

    def body(x_ref, t_ref, g_ref, ff_ref, gt_ref, dx_ref, dffb_ref, loss_ref, dgt_ref, dg_ref):
        i = pl.program_id(0)

        @pl.when(i == 0)
        def _():
            loss_ref[...] = jnp.zeros_like(loss_ref)
            dgt_ref[...] = jnp.zeros_like(dgt_ref)
            dg_ref[...] = jnp.zeros_like(dg_ref)

        xv = x_ref[...]
        gv = g_ref[...]
        r = lax.rsqrt(jnp.mean(xv * xv, axis=-1, keepdims=True) + EPS)
        xh = xv * r
        err = xh * gv - t_ref[...]
        loss_ref[...] += 0.5 * jnp.sum(jnp.mean(err * err, axis=-1, keepdims=True))
        dy = err * (1.0 / D)
        dg_ref[...] += jnp.sum(dy * xh, axis=0, keepdims=True)
        dxh = dy * gv
        dx = r * (dxh - xh * jnp.mean(dxh * xh, axis=-1, keepdims=True))
        dx_ref[...] = dx
        dffb_ref[...] = (dx * gt_ref[...]).astype(BF16)
        dgt_ref[...] += jnp.sum(dx * ff_ref[...], axis=0, keepdims=True)

    return pl.pallas_call(
        body, name="loss_head", grid=(S // tr,),
        in_specs=[_row_spec(tr, D), _row_spec(tr, D), _vec_spec(D), _row_spec(tr, D), _vec_spec(D)],
        out_specs=[_row_spec(tr, D), _row_spec(tr, D), pl.BlockSpec((1, 128), lambda i: (0, 0)),
                   _vec_spec(D), _vec_spec(D)],
        out_shape=[jax.ShapeDtypeStruct((S, D), F32), jax.ShapeDtypeStruct((S, D), BF16),
                   jax.ShapeDtypeStruct((1, 128), F32), jax.ShapeDtypeStruct((1, D), F32),
                   jax.ShapeDtypeStruct((1, D), F32)],
        compiler_params=_params(("arbitrary",)),
    )(x2, target, g, ff, gt)


def _norm_mod_bwd(name, dh, x, g, sc, dres, gated=None, job=None):
    S, D = x.shape
    tr = _pick(S, 256, 8)
    n = S // tr
    with_gate = gated is not None

    def body(*refs):
        if with_gate:
            dh_ref, x_ref, g_ref, sc_ref, dres_ref, mo_ref, gt_ref, dx_ref, dsh_ref, dsc_ref, dg_ref, dmob_ref, dgt_ref = refs
        else:
            dh_ref, x_ref, g_ref, sc_ref, dres_ref, dx_ref, dsh_ref, dsc_ref, dg_ref = refs
        i = pl.program_id(0)

        @pl.when(i == 0)
        def _():
            dsh_ref[...] = jnp.zeros_like(dsh_ref)
            dsc_ref[...] = jnp.zeros_like(dsc_ref)
            if with_gate:
                dgt_ref[...] = jnp.zeros_like(dgt_ref)

        xv = x_ref[...]
        dhv = dh_ref[...]
        gv = g_ref[...]
        scale = 1.0 + sc_ref[...]
        r = lax.rsqrt(jnp.mean(xv * xv, axis=-1, keepdims=True) + EPS)
        xh = xv * r
        dsh_ref[...] += jnp.sum(dhv, axis=0, keepdims=True)
        dsc_ref[...] += jnp.sum(dhv * xh, axis=0, keepdims=True)
        dxh = dhv * (gv * scale)
        dx = dres_ref[...] + r * (dxh - xh * jnp.mean(dxh * xh, axis=-1, keepdims=True))
        dx_ref[...] = dx
        if with_gate:
            dmob_ref[...] = (dx * gt_ref[...]).astype(BF16)
            dgt_ref[...] += jnp.sum(dx * mo_ref[...], axis=0, keepdims=True)

        @pl.when(i == n - 1)
        def _():
            t = dsc_ref[...]
            dg_ref[...] = t * scale
            dsc_ref[...] = t * gv

    in_specs = [_row_spec(tr, D), _row_spec(tr, D), _vec_spec(D), _vec_spec(D), _row_spec(tr, D)]
    args = [dh, x, g, sc, dres]
    out_specs = [_row_spec(tr, D), _vec_spec(D), _vec_spec(D), _vec_spec(D)]
    out_shape = [jax.ShapeDtypeStruct((S, D), F32)] + [jax.ShapeDtypeStruct((1, D), F32)] * 3
    if with_gate:
        in_specs += [_row_spec(tr, D), _vec_spec(D)]
        args += list(gated)
        out_specs += [_row_spec(tr, D), _vec_spec(D)]
        out_shape += [jax.ShapeDtypeStruct((S, D), BF16), jax.ShapeDtypeStruct((1, D), F32)]
    scratch = []
    aliases, wrap = _host(job, (n,), in_specs, args, out_specs, out_shape, scratch)
    return pl.pallas_call(
        wrap(body), name=name, grid=(n,), in_specs=in_specs, out_specs=out_specs, out_shape=out_shape,
        scratch_shapes=scratch, input_output_aliases=aliases, compiler_params=_params(("arbitrary",)),
    )(*args)


CONV_ROWS = 256


def _col_spec(S, off_cols):
    o = off_cols // HEAD
    return pl.BlockSpec((S, HEAD), lambda c, o=o: (0, c + o))


def _conv_taps(u, u_prev, rid):
    s1 = jnp.where(rid >= 1, pltpu.roll(u, 1, 0), pltpu.roll(u_prev, 1, 0))
    s2 = jnp.where(rid >= 2, pltpu.roll(u, 2, 0), pltpu.roll(u_prev, 2, 0))
    return s1, s2


def _conv_fwd(proj, conv_w, d_conv, job=None):
    S = proj.shape[0]
    R = min(CONV_ROWS, S)
    nr = S // R

    def body(ab_ref, ac_ref, ax_ref, w_ref, ya_ref):
        w0, w1, w2 = w_ref[0:1, :], w_ref[1:2, :], w_ref[2:3, :]
        rid = lax.broadcasted_iota(jnp.int32, (R, HEAD), 0)

        def step(c, carry):
            rows = pl.ds(pl.multiple_of(c * R, R), R)
            prev = pl.ds(pl.multiple_of(jnp.maximum(c - 1, 0) * R, R), R)
            u = ac_ref[rows, :] * ax_ref[rows, :]
            u_prev = jnp.where(c > 0, ac_ref[prev, :] * ax_ref[prev, :], 0.0)
            s1, s2 = _conv_taps(u, u_prev, rid)
            y = w0 * s2 + w1 * s1 + w2 * u
            ya_ref[rows, :] = (ab_ref[rows, :] * y).astype(BF16)
            return carry

        lax.fori_loop(0, nr, step, 0)

    grid = (d_conv // HEAD,)
    in_specs = [_col_spec(S, 0), _col_spec(S, d_conv), _col_spec(S, 2 * d_conv), pl.BlockSpec((3, HEAD), lambda c: (0, c))]
    args = [proj, proj, proj, conv_w]
    out_specs, out_shape = [pl.BlockSpec((S, HEAD), lambda c: (0, c))], [jax.ShapeDtypeStruct((S, d_conv), BF16)]
    scratch = []
    aliases, wrap = _host(job, grid, in_specs, args, out_specs, out_shape, scratch)
    res = pl.pallas_call(
        wrap(body), name="conv_fwd", grid=grid, in_specs=in_specs, out_specs=out_specs, out_shape=out_shape,
        scratch_shapes=scratch, input_output_aliases=aliases,
        compiler_params=_params(("parallel" if job is None else "arbitrary",)),
    )(*args)
    return res[0] if job is None else res


def _conv_bwd(dya, proj, conv_w, d_conv, job=None):
    S = proj.shape[0]
    R = min(CONV_ROWS, S)
    nr = S // R

    def body(dya_ref, ab_ref, ac_ref, ax_ref, w_ref, dab_ref, dac_ref, dax_ref, dw_ref):
        w0, w1, w2 = w_ref[0:1, :], w_ref[1:2, :], w_ref[2:3, :]
        rid = lax.broadcasted_iota(jnp.int32, (R, HEAD), 0)

        def step(c, carry):
            dw0, dw1, dw2 = carry
            rows = pl.ds(pl.multiple_of(c * R, R), R)
            prev = pl.ds(pl.multiple_of(jnp.maximum(c - 1, 0) * R, R), R)
            nxt = pl.ds(pl.multiple_of(jnp.minimum(c + 1, nr - 1) * R, R), R)
            a_c, a_x, a_b = ac_ref[rows, :], ax_ref[rows, :], ab_ref[rows, :]
            u = a_c * a_x
            u_prev = jnp.where(c > 0, ac_ref[prev, :] * ax_ref[prev, :], 0.0)
            s1, s2 = _conv_taps(u, u_prev, rid)
            y = w0 * s2 + w1 * s1 + w2 * u
            dy = dya_ref[rows, :]
            dab_ref[rows, :] = (dy * y).astype(BF16)
            dyc = dy * a_b
            dyc_next = jnp.where(c < nr - 1, dya_ref[nxt, :] * ab_ref[nxt, :], 0.0)
            t1 = jnp.where(rid < R - 1, pltpu.roll(dyc, R - 1, 0), pltpu.roll(dyc_next, R - 1, 0))
            t2 = jnp.where(rid < R - 2, pltpu.roll(dyc, R - 2, 0), pltpu.roll(dyc_next, R - 2, 0))
            du = w2 * dyc + w1 * t1 + w0 * t2
            dac_ref[rows, :] = (du * a_x).astype(BF16)
            dax_ref[rows, :] = (du * a_c).astype(BF16)
            dw0 = dw0 + jnp.sum(dyc * s2, axis=0, keepdims=True)
            dw1 = dw1 + jnp.sum(dyc * s1, axis=0, keepdims=True)
            dw2 = dw2 + jnp.sum(dyc * u, axis=0, keepdims=True)
            return dw0, dw1, dw2

        z = jnp.zeros((1, HEAD), F32)
        dw0, dw1, dw2 = lax.fori_loop(0, nr, step, (z, z, z))
        dw_ref[0:1, :] = dw0
        dw_ref[1:2, :] = dw1
        dw_ref[2:3, :] = dw2

    col = pl.BlockSpec((S, HEAD), lambda c: (0, c))
    grid = (d_conv // HEAD,)
    in_specs = [col, _col_spec(S, 0), _col_spec(S, d_conv), _col_spec(S, 2 * d_conv),
                pl.BlockSpec((3, HEAD), lambda c: (0, c))]
    args = [dya, proj, proj, proj, conv_w]
    out_specs = [col, col, col, pl.BlockSpec((3, HEAD), lambda c: (0, c))]
    out_shape = [jax.ShapeDtypeStruct((S, d_conv), BF16)] * 3 + [jax.ShapeDtypeStruct((3, d_conv), F32)]
    scratch = []
    aliases, wrap = _host(job, grid, in_specs, args, out_specs, out_shape, scratch)
    return pl.pallas_call(
        wrap(body), name="conv_bwd", grid=grid, in_specs=in_specs, out_specs=out_specs, out_shape=out_shape,
        scratch_shapes=scratch, input_output_aliases=aliases,
        compiler_params=_params(("parallel" if job is None else "arbitrary",)),
    )(*args)


HGRN_ROWS = 512
HGRN_FWD_SUB = 64
HGRN_BWD_SUB = 32
HGRN_GATE_ROWS = 128


def _block_cumsum(x, pos):
    for s in (1, 2, 4, 8):
        x = x + jnp.where(pos >= s, pltpu.roll(x, s, 0), 0.0)
    return x


def _block_rev_cumsum(x, pos, rows):
    for s in (1, 2, 4, 8):
        x = x + jnp.where(pos < BLK - s, pltpu.roll(x, rows - s, 0), 0.0)
    return x


def _block_last(x, pos, rows):
    m = jnp.where(pos == BLK - 1, x, 0.0)
    for s in (1, 2, 4, 8):
        m = m + pltpu.roll(m, rows - s, 0)
    return m


def _hgrn_gates(q, z, lb):
    sgq = _sig(q)
    qs = q * sgq
    sg = _sig(z)
    f = lb + (1.0 - lb) * sg
    kk = (1.0 - lb) * (1.0 - sg)
    return sgq, qs, sg, f, kk


def _hgrn_decays(q, z, lb, pos, rows):
    sgq, qs, sg, f, kk = _hgrn_gates(q, z, lb)
    b = _block_cumsum(jnp.log(f), pos)
    return sgq, qs, sg, f, kk, b, _block_last(b, pos, rows)


def _hgrn_specs(T, d_conv, n_t, rev):
    def t_of(i):
        return (n_t - 1 - i) if rev else i

    def pcol(off):
        o = off // HEAD
        return pl.BlockSpec((T, HEAD), lambda h, i, o=o: (t_of(i), h + o))

    q_spec, z_spec, v_spec, g_spec = pcol(3 * d_conv), pcol(4 * d_conv), pcol(5 * d_conv), pcol(6 * d_conv)
    lb_spec = pl.BlockSpec((1, HEAD), lambda h, i: (0, h))
    gn_spec = pl.BlockSpec((1, HEAD), lambda h, i: (0, 0))
    act_spec = pl.BlockSpec((T, HEAD), lambda h, i: (t_of(i), h))
    st_spec = pl.BlockSpec((None, T // BLK, HEAD, HEAD), lambda h, i: (h, t_of(i), 0, 0))
    return q_spec, z_spec, v_spec, g_spec, lb_spec, gn_spec, act_spec, st_spec


def _hgrn_fwd(proj, lb, gn, d_conv, job=None):
    S = proj.shape[0]
    H = d_conv // HEAD
    T = min(HGRN_ROWS, S)
    n_t, nb = S // T, T // BLK
    sub = min(HGRN_FWD_SUB, T)
    q_spec, z_spec, v_spec, g_spec, lb_spec, gn_spec, act_spec, st_spec = _hgrn_specs(T, d_conv, n_t, False)

    def body(q_ref, z_ref, v_ref, g_ref, lb_ref, gn_ref, ob_ref, o_ref, st_ref, qe_s, ke_s, dec_s, state, v_s, o_s, u_s):
        @pl.when(pl.program_id(1) == 0)
        def _():
            state[...] = jnp.zeros_like(state)

        pos = lax.broadcasted_iota(jnp.int32, (sub, HEAD), 0) % BLK
        lbv = lb_ref[...]

        def vector_pass(s, carry):
            r = pl.ds(pl.multiple_of(s * sub, sub), sub)
            v = v_ref[r, :]
            _, qs, _, _, kk, b, b_tot = _hgrn_decays(q_ref[r, :], z_ref[r, :], lbv, pos, sub)
            qe_s[r, :] = (qs * jnp.exp(b)).astype(BF16)
            ke_s[r, :] = (kk * jnp.exp(b_tot - b)).astype(BF16)
            v_s[r, :] = v.astype(BF16)
            dec_s[r, :] = jnp.exp(b_tot)
            o = jnp.sum(qs * kk, axis=-1, keepdims=True) * v
            for d in range(1, BLK):
                e = jnp.exp(b - pltpu.roll(b, d, 0))
                a = jnp.sum(qs * pltpu.roll(kk, d, 0) * e, axis=-1, keepdims=True)
                o = o + jnp.where(pos >= d, a * pltpu.roll(v, d, 0), 0.0)
            o_s[r, :] = o
            return carry

        lax.fori_loop(0, T // sub, vector_pass, 0)

        for j in range(nb):
            rows = pl.ds(j * BLK, BLK)
            u_s[j] = lax.dot_general(v_s[rows, :], ke_s[rows, :], (((0,), (0,)), ((), ())),
                                     preferred_element_type=F32)
        st = state[...]
        for j in range(nb):
            st_ref[j] = st.astype(BF16)
            st = st * dec_s[pl.ds(j * BLK, 1), :] + u_s[j]
        state[...] = st
        for j in range(nb):
            rows = pl.ds(j * BLK, BLK)
            o_s[rows, :] += lax.dot_general(qe_s[rows, :], st_ref[j], (((1,), (1,)), ((), ())),
                                            preferred_element_type=F32)
        o = o_s[...]
        o_ref[...] = o
        r = lax.rsqrt(jnp.mean(o * o, axis=-1, keepdims=True) + EPS)
        g = g_ref[...]
        ob_ref[...] = ((o * r) * gn_ref[...] * (g * _sig(g))).astype(BF16)

    grid = (H, n_t)
    in_specs = [q_spec, z_spec, v_spec, g_spec, lb_spec, gn_spec]
    args = [proj, proj, proj, proj, lb, gn]
    out_specs = [act_spec, act_spec, st_spec, act_spec, act_spec, act_spec]
    out_shape = [jax.ShapeDtypeStruct((S, d_conv), BF16), jax.ShapeDtypeStruct((S, d_conv), F32),
                 jax.ShapeDtypeStruct((H, S // BLK, HEAD, HEAD), BF16),
                 jax.ShapeDtypeStruct((S, d_conv), BF16), jax.ShapeDtypeStruct((S, d_conv), BF16),
                 jax.ShapeDtypeStruct((S, d_conv), F32)]
    scratch = [pltpu.VMEM((HEAD, HEAD), F32), pltpu.VMEM((T, HEAD), BF16), pltpu.VMEM((T, HEAD), F32),
               pltpu.VMEM((nb, HEAD, HEAD), F32)]
    aliases, wrap = _host(job, grid, in_specs, args, out_specs, out_shape, scratch)
    return pl.pallas_call(
        wrap(body), name="hgrn_fwd", grid=grid, in_specs=in_specs, out_specs=out_specs, out_shape=out_shape,
        scratch_shapes=scratch, input_output_aliases=aliases,
        compiler_params=_params(("parallel" if job is None else "arbitrary", "arbitrary")),
    )(*args)


def _hgrn_bwd(dob, o_raw, states, qe, ke, dec, proj, lb, gn, d_conv, job=None):
    S = proj.shape[0]
    H = d_conv // HEAD
    T = min(HGRN_ROWS, S)
    n_t, nb = S // T, T // BLK
    sub = min(HGRN_BWD_SUB, T)
    gate_rows = min(HGRN_GATE_ROWS, T)
    q_spec, z_spec, v_spec, g_spec, lb_spec, gn_spec, act_spec, st_spec = _hgrn_specs(T, d_conv, n_t, True)

    def body(dob_ref, o_ref, st_ref, qe_b, ke_b, dec_s, q_ref, z_ref, v_ref, g_ref, lb_ref, gn_ref,
             dq_ref, dz_ref, dv_ref, dg_ref, dlb_ref, dgn_ref,
             dstate, do_b, v_b, dqe_s, dke_s, dvi_s, ddec_s, do_s, u_s, ds_s):
        @pl.when(pl.program_id(1) == 0)
        def _():
            dstate[...] = jnp.zeros_like(dstate)
            dlb_ref[...] = jnp.zeros_like(dlb_ref)
            dgn_ref[...] = jnp.zeros_like(dgn_ref)

        pos = lax.broadcasted_iota(jnp.int32, (sub, HEAD), 0) % BLK
        lbv, gnv = lb_ref[...], gn_ref[...]

        def before(s, carry):
            r = pl.ds(pl.multiple_of(s * gate_rows, gate_rows), gate_rows)
            g = g_ref[r, :]
            v_b[r, :] = v_ref[r, :].astype(BF16)
            o = o_ref[r, :]
            rs = lax.rsqrt(jnp.mean(o * o, axis=-1, keepdims=True) + EPS)
            on = o * rs
            sgg = _sig(g)
            dobv = dob_ref[r, :]
            dog = dobv * (g * sgg)
            dgn_ref[...] += jnp.sum(dog * on, axis=0, keepdims=True)
            don = dog * gnv
            do = rs * (don - on * jnp.mean(don * on, axis=-1, keepdims=True))
            dg_ref[r, :] = (dobv * (on * gnv) * (sgg * (1.0 + g * (1.0 - sgg)))).astype(BF16)
            do_s[r, :] = do
            do_b[r, :] = do.astype(BF16)
            return carry

        lax.fori_loop(0, T // gate_rows, before, 0)

        for j in range(nb):
            rows = pl.ds(j * BLK, BLK)
            dob_j = do_b[rows, :]
            u_s[j] = lax.dot_general(dob_j, qe_b[rows, :], (((0,), (0,)), ((), ())), preferred_element_type=F32)
            dqe_s[rows, :] = lax.dot_general(dob_j, st_ref[j], (((1,), (0,)), ((), ())), preferred_element_type=F32)
        dst = dstate[...]
        for j in reversed(range(nb)):
            ds_s[j] = dst.astype(BF16)
            ddec = jnp.sum(st_ref[j].astype(F32) * dst, axis=0, keepdims=True)
            ddec_s[pl.ds(j * BLK, BLK), :] = jnp.broadcast_to(ddec, (BLK, HEAD))
            dst = dst * dec_s[pl.ds(j * BLK, 1), :] + u_s[j]
        dstate[...] = dst
        for j in range(nb):
            rows = pl.ds(j * BLK, BLK)
            dke_s[rows, :] = lax.dot_general(v_b[rows, :], ds_s[j], (((1,), (0,)), ((), ())), preferred_element_type=F32)
            dvi_s[rows, :] = lax.dot_general(ke_b[rows, :], ds_s[j], (((1,), (1,)), ((), ())), preferred_element_type=F32)

        def after(s, carry):
            r = pl.ds(pl.multiple_of(s * sub, sub), sub)
            q, v = q_ref[r, :], v_ref[r, :]
            sgq, qs, sg, f, kk, b, b_tot = _hgrn_decays(q, z_ref[r, :], lbv, pos, sub)
            do = do_s[r, :]
            dqs = dqe_s[r, :] * jnp.exp(b)
            dkk = dke_s[r, :] * jnp.exp(b_tot - b)
            d_tot = jnp.where(pos == BLK - 1, _block_cumsum(dkk * kk, pos) + ddec_s[r, :] * jnp.exp(b_tot), 0.0)
            dv = dvi_s[r, :]
            da = jnp.sum(do * v, axis=-1, keepdims=True)
            dv = dv + jnp.sum(qs * kk, axis=-1, keepdims=True) * do
            dqs = dqs + da * kk
            dkk = dkk + da * qs
            for d in range(1, BLK):
                kd = pltpu.roll(kk, d, 0)
                e = jnp.where(pos >= d, jnp.exp(b - pltpu.roll(b, d, 0)), 0.0)
                a = jnp.sum(qs * kd * e, axis=-1, keepdims=True)
                da = jnp.sum(do * pltpu.roll(v, d, 0), axis=-1, keepdims=True)
                gq = da * e
                dqs = dqs + gq * kd
                dkk = dkk + pltpu.roll(gq * qs, sub - d, 0)
                dv = dv + pltpu.roll(a * do, sub - d, 0)
            db = qs * dqs - kk * dkk + d_tot
            dlf = _block_rev_cumsum(db, pos, sub)
            df = dlf / f - dkk
            dlb_ref[...] += jnp.sum(df * (1.0 - sg), axis=0, keepdims=True)
            dz_ref[r, :] = (df * (1.0 - lbv) * sg * (1.0 - sg)).astype(BF16)
            dq_ref[r, :] = (dqs * (sgq * (1.0 + q * (1.0 - sgq)))).astype(BF16)
            dv_ref[r, :] = dv.astype(BF16)
            return carry

        lax.fori_loop(0, T // sub, after, 0)

    tb = lambda dt: pltpu.VMEM((T, HEAD), dt)
    grid = (H, n_t)
    in_specs = [act_spec, act_spec, st_spec, act_spec, act_spec, act_spec, q_spec, z_spec, v_spec, g_spec, lb_spec,
                gn_spec]
    args = [dob, o_raw, states, qe, ke, dec, proj, proj, proj, proj, lb, gn]
    out_specs = [act_spec, act_spec, act_spec, act_spec, lb_spec, pl.BlockSpec((None, 1, HEAD), lambda h, i: (h, 0, 0))]
    out_shape = ([jax.ShapeDtypeStruct((S, d_conv), BF16)] * 4
                 + [jax.ShapeDtypeStruct((1, d_conv), F32), jax.ShapeDtypeStruct((H, 1, HEAD), F32)])
    scratch = [pltpu.VMEM((HEAD, HEAD), F32), tb(BF16), tb(BF16), tb(F32), tb(F32), tb(F32), tb(F32), tb(F32),
               pltpu.VMEM((nb, HEAD, HEAD), F32), pltpu.VMEM((nb, HEAD, HEAD), BF16)]
    aliases, wrap = _host(job, grid, in_specs, args, out_specs, out_shape, scratch)
    return pl.pallas_call(
        wrap(body), name="hgrn_bwd", grid=grid, in_specs=in_specs, out_specs=out_specs, out_shape=out_shape,
        scratch_shapes=scratch, input_output_aliases=aliases,
        compiler_params=_params(("parallel" if job is None else "arbitrary", "arbitrary")),
    )(*args)


def _local_step(x, target, mod, norm_mix_g, lb, gnorm_g, norm_ffn_g, norm_final_g, conv_w, bufs, c_idx, order):
    S, D = x.shape
    d_conv = D // 2
    d_in = 7 * d_conv + 2 * D
    d_ff = N_CHIPS * bufs["w_ffn_down"].shape[1]
    nb_in, nb_co, nb_ff = bufs["w_in"].shape[2], bufs["w_conv_out"].shape[2], bufs["w_ffn_gate"].shape[2]
    rows = lambda g: g.reshape(-1, g.shape[2])
    sh_m, sc_m, gt_m, sh_f, sc_f, gt_f = [mod[:, i * D:(i + 1) * D] for i in range(6)]
    tm = _pick(S, 512, 16)
    tm2 = _pick(S, 1024, 16)
    tn_in = _pick(nb_in, 1408, 128)
    tn_co = _pick(nb_co, 512, 128)
    tn_ff = _pick(nb_ff, 1408, 128)
    tn_d = _pick(D, 512, 128)
    tw = _pick(D, 1024, 128)
    tg = _pick(D, 512, 128)

    h = _norm_mod("norm_mix", x, norm_mix_g, sc_m, sh_m)
    proj, w_in = _proj_gather(h, bufs["w_in"], order)
    ob, o_raw, states, qe, ke, dec, *got = _hgrn_fwd(
        proj, lb, gnorm_g, d_conv,
        job=_GatherJob([bufs[k] for k in ("w_conv_out", "w_hgrn_out", "w_o", "w_ffn_gate")]))
    ya, w_conv_out, w_hgrn_out, w_o, w_ffn_gate = _conv_fwd(proj, conv_w, d_conv, job=_ForwardJob(got))
    w_o = rows(w_o)

    def merge_epi(accs, ex):
        y_a, y_b = accs
        return [y_a, y_b, _sig(ex[0]) * y_a + _sig(ex[1]) * y_b]

    y_a, y_b, merged, w_ffn_up = _matmul(
        "branch_out", [(ya, 'n', w_conv_out, 'n'), (ob, 'n', w_hgrn_out, 'n')], [0, 1], S, D, d_conv, tm2, tn_co, d_conv,
        [(proj, 'tile', 7 * d_conv), (proj, 'tile', 7 * d_conv + D)], [(BF16, None), (BF16, None), (BF16, None)], merge_epi,
        job=_GatherJob([bufs["w_ffn_up"]], 0, 2), rows_outer=True)

    def resid_epi(accs, ex):
        return [accs[0], ex[0] + ex[1] * accs[0]]

    mo, x1, w_ffn_up = _matmul("w_o", [(merged, 'n', w_o, 'n')], [0], S, D, D, tm2, tw, D,
                               [(x, 'tile', 0), (gt_m, 'row', 0)], [(BF16, None), (F32, None)], resid_epi,
                               job=_GatherJob([w_ffn_up], 1, 2))
    h2, w_ffn_up = _norm_mod("norm_ffn", x1, norm_ffn_g, sc_f, sh_f, job=_ForwardJob([w_ffn_up]))

    def swiglu_epi(accs, ex):
        gg, uu = accs
        return [gg, uu, gg * _sig(gg) * uu]

    G, U, act, w_ffn_down = _matmul(
        "ffn_in", [(h2, 'n', w_ffn_gate, 'n'), (h2, 'n', w_ffn_up, 'n')], [0, 1], S, d_ff, D,
        tm2, tn_ff, D, [], [(BF16, None), (BF16, None), (BF16, None)], swiglu_epi, job=_GatherJob([bufs["w_ffn_down"]]))
    w_ffn_down = rows(_forward_halves("forward_down", [w_ffn_down])[0])
    ff, x2 = _matmul("ffn_out", [(act, 'n', w_ffn_down, 'n')], [0], S, D, d_ff, tm2, tn_d, d_ff,
                     [(x1, 'tile', 0), (gt_f, 'row', 0)], [(BF16, None), (F32, None)], resid_epi, rows_outer=True)

    dx2, dffb, loss, dgt_f, dg_final = _loss_head(x2, target, norm_final_g, ff, gt_f)

    def swiglu_bwd_epi(accs, ex):
        dact, gg, uu = accs[0], ex[0], ex[1]
        s = _sig(gg)
        return [dact * uu * (s * (1.0 + gg * (1.0 - s))), dact * (gg * s)]

    dG, dU = _matmul("d_act", [(dffb, 'n', w_ffn_down, 't')], [0], S, d_ff, D, tm2, tn_ff, D,
                     [(G, 'tile', 0), (U, 'tile', 0)], [(BF16, None), (BF16, None)], swiglu_bwd_epi)
    dw_down, = _matmul("dw_ffn_down", [(act, 't', dffb, 'n')], [0], d_ff, D, S, _pick(d_ff, 512, 128),
                       D, S, [], [(BF16, None)], _plain)
    dh2, = _matmul("d_h2", [(dG, 'n', w_ffn_gate, 't'), (dU, 'n', w_ffn_up, 't')], [0, 0], S, D, d_ff,
                   tm, D, tn_ff, [], [(F32, None)], _plain)
    dw_gate, = _matmul("dw_ffn_gate", [(h2, 't', dG, 'n')], [0], D, d_ff, S, tg, tn_ff, S, [], [(BF16, nb_ff)], _plain)
    dw_up, = _matmul("dw_ffn_up", [(h2, 't', dU, 'n')], [0], D, d_ff, S, tg, tn_ff, S, [], [(BF16, nb_ff)], _plain)

    def pair_sums(names, grads, from_sibling):
        pairs = [_pair_sum("pair_sum_" + k, g, q, c_idx) for k, g, q in zip(names, grads, from_sibling)]
        return [p[0] for p in pairs], [p[1] for p in pairs]

    ffn_names = ["w_ffn_down", "w_ffn_gate", "w_ffn_up"]
    ffn_grads = [dw_down.reshape(N_CHIPS, -1, D), dw_gate, dw_up]
    dx1, dsh_f, dsc_f, dg_ffn, dmob, dgt_m, *from_sibling = _norm_mod_bwd(
        "norm_ffn_bwd", dh2, x1, norm_ffn_g, sc_f, dx2, (mo, gt_m), job=_SwapJob(ffn_grads))
    parts_f, slots_f = pair_sums(ffn_names, ffn_grads, from_sibling)

    def merge_bwd_epi(accs, ex):
        dm, ga, gb, ya_, yb_ = accs[0], ex[0], ex[1], ex[2], ex[3]
        sa, sb = _sig(ga), _sig(gb)
        return [dm * ya_ * sa * (1.0 - sa), dm * yb_ * sb * (1.0 - sb), dm * sa, dm * sb]

    dga, dgb, dy_a, dy_b = _matmul(
        "d_merged", [(dmob, 'n', w_o, 't')], [0], S, D, D, tm2, tn_co, D,
        [(proj, 'tile', 7 * d_conv), (proj, 'tile', 7 * d_conv + D), (y_a, 'tile', 0), (y_b, 'tile', 0)],
        [(BF16, None)] * 4, merge_bwd_epi, rows_outer=True)
    dw_o, = _matmul("dw_o", [(merged, 't', dmob, 'n')], [0], D, D, S, tg, D, S, [], [(BF16, None)], _plain)
    dya, dob = _matmul("d_branch", [(dy_a, 'n', w_conv_out, 't'), (dy_b, 'n', w_hgrn_out, 't')], [0, 1], S, d_conv, D,
                       tm2, _pick(d_conv, 1024, 128), tn_co, [], [(F32, None), (F32, None)], _plain)
    dw_co, dw_ho = _matmul("dw_branch", [(ya, 't', dy_a, 'n'), (ob, 't', dy_b, 'n')], [0, 1], d_conv, D, S,
                           _pick(d_conv, 512, 128), tn_co, S, [], [(BF16, nb_co), (BF16, nb_co)], _plain)
    branch_names = ["w_conv_out", "w_hgrn_out", "w_o"]
    branch_grads = [dw_co, dw_ho, dw_o.reshape(N_CHIPS, -1, D)]
    dab, dac, dax, dconv_w, *from_sibling = _conv_bwd(dya, proj, conv_w, d_conv, job=_SwapJob(branch_grads))
    parts_b, slots_b = pair_sums(branch_names, branch_grads, from_sibling)
    dq, dz, dv, dgo, dlb, dgn, *arrived_f = _hgrn_bwd(dob, o_raw, states, qe, ke, dec, proj, lb, gnorm_g, d_conv,
                                                      job=_ScatterJob(parts_f, slots_f))
    dproj = jnp.concatenate([dab, dac, dax, dq, dz, dv, dgo, dga, dgb], axis=1)
    halves_f = [_sum_chips("sum_chips_" + k, r, c_idx) for k, r in zip(ffn_names, arrived_f)]
    dw_in, *moved = _matmul("dw_in", [(h, 't', dproj, 'n')], [0], D, d_in, S, tg, tn_in, S, [], [(BF16, nb_in)], _plain,
                            job=_Jobs([_ShareJob(halves_f), _ScatterJob(parts_b, slots_b)]))
    whole_f, arrived_b = moved[:len(ffn_names)], moved[len(ffn_names):]
    parts_i, slots_i = pair_sums(["w_in"], [dw_in], _swap_halves("swap_in", [dw_in]))
    dh, arrived_in = _matmul("d_h", [(dproj, 'n', w_in, 't')], [0], S, D, d_in, tm2, D, tn_in, [], [(F32, None)], _plain,
                             job=_ScatterJob(parts_i, slots_i))
    dx, dsh_m, dsc_m, dg_mix = _norm_mod_bwd("norm_mix_bwd", dh, x, norm_mix_g, sc_m, dx1)
    dmod = jnp.concatenate([dsh_m, dsc_m, dgt_m, dsh_f, dsc_f, dgt_f], axis=1)
    small = dict(dmod=dmod, dg_mix=dg_mix, dg_ffn=dg_ffn, dg_final=dg_final, dlb=dlb,
                 dgn=jnp.sum(dgn, axis=0), dconv_w=dconv_w)
    late_names = branch_names + ["w_in"]
    late = _share_halves([_sum_chips("sum_chips_" + k, r, c_idx) for k, r in zip(late_names, arrived_b + [arrived_in])])
    big = dict(zip(ffn_names + late_names, whole_f + list(late)))
    return loss, dx, small, big


def _adamw_math(w, g, m, v):
    m = ADAM_B1 * m + (1.0 - ADAM_B1) * g
    v = ADAM_B2 * v + (1.0 - ADAM_B2) * (g * g)
    m_hat = m / (1.0 - ADAM_B1 ** ADAM_STEP)
    v_hat = v / (1.0 - ADAM_B2 ** ADAM_STEP)
    delta = -ADAM_LR * (m_hat / (jnp.sqrt(v_hat) + ADAM_EPS) + ADAM_WD * w)
    return delta, m, v


def _adamw(name, w, g, m, v):
    R, C = w.shape
    tr = _pick(R, max(8, (256 * 1024) // C // 8 * 8), 8)
    spec = pl.BlockSpec((tr, C), lambda i: (i, 0))

    def body(w_ref, g_ref, m_ref, v_ref, go_ref, d_ref, mo_ref, vo_ref):
        gv = g_ref[...]
        d, m2, v2 = _adamw_math(w_ref[...], gv, m_ref[...], v_ref[...])
        go_ref[...] = gv
        d_ref[...] = d
        mo_ref[...] = m2
        vo_ref[...] = v2

    return pl.pallas_call(
        body, name=name, grid=(R // tr,), in_specs=[spec] * 4, out_specs=[spec] * 4,
        out_shape=[jax.ShapeDtypeStruct((R, C), F32)] * 4, compiler_params=_params(("parallel",)),
    )(w, g, m, v)


def _ada_update(c_act_t, dmod, w, m, v):
    R, C = w.shape
    B = dmod.shape[0]
    tr = _pick(R, 256, 8)
    tc = _pick(C, 1536, 128)
    spec = pl.BlockSpec((tr, tc), lambda i, j: (i, j))

    def body(ct_ref, dm_ref, w_ref, m_ref, v_ref, g_ref, d_ref, mo_ref, vo_ref):
        ct = ct_ref[...]
        dm = dm_ref[...]
        g = ct[:, 0:1] * dm[0:1, :]
        for b in range(1, B):
            g = g + ct[:, b:b + 1] * dm[b:b + 1, :]
        d, m2, v2 = _adamw_math(w_ref[...], g, m_ref[...], v_ref[...])
        g_ref[...] = g
        d_ref[...] = d
        mo_ref[...] = m2
        vo_ref[...] = v2

    return pl.pallas_call(
        body, name="ada_update", grid=(R // tr, C // tc),
        in_specs=[pl.BlockSpec((tr, B), lambda i, j: (i, 0)), pl.BlockSpec((B, tc), lambda i, j: (0, j)),
                  spec, spec, spec],
        out_specs=[spec] * 4, out_shape=[jax.ShapeDtypeStruct((R, C), F32)] * 4,
        compiler_params=_params(("parallel", "parallel")),
    )(c_act_t, dmod, w, m, v)


def _prep(c_all, lb_param):
    def body(c_ref, p_ref, ca_ref, cab_ref, lb_ref):
        cv = c_ref[...]
        ca = cv * _sig(cv)
        ca_ref[...] = ca
        cab_ref[...] = ca.astype(BF16)
        lb_ref[...] = _sig(p_ref[0:1, :] - p_ref[1:2, :])

    return pl.pallas_call(
        body, name="prep",
        out_shape=[jax.ShapeDtypeStruct(c_all.shape, F32), jax.ShapeDtypeStruct(c_all.shape, BF16),
                   jax.ShapeDtypeStruct((1, lb_param.shape[1]), F32)],
    )(c_all, lb_param)


def _small_reduce(parts):
    W = parts.shape[1]

    def body(p_ref, o_ref):
        acc = p_ref[0:1, :]
        for d in range(1, N_DEV):
            acc = acc + p_ref[d:d + 1, :]
        o_ref[...] = acc

    return pl.pallas_call(body, name="small_reduce", out_shape=jax.ShapeDtypeStruct((1, W), F32))(parts)


def _lb_grad(dlb, lb):
    def body(d_ref, lb_ref, o_ref):
        t = d_ref[...] * lb_ref[...] * (1.0 - lb_ref[...])
        o_ref[0:1, :] = t
        o_ref[1:2, :] = -t

    return pl.pallas_call(body, name="lb_grad", out_shape=jax.ShapeDtypeStruct((2, dlb.shape[1]), F32))(dlb, lb)


def _place():
    x, y, c = lax.axis_index("x"), lax.axis_index("y"), lax.axis_index("c")
    chips = [(1 - x, y), (x, 1 - y), (1 - x, 1 - y)]
    return x, y, c, chips


def _allgather_rows(name, v):
    m_per, n = v.shape

    def body(x_ref, out_ref, send_sems, recv_sems, local_sem):
        x, y, c, chips = _place()
        me, sibling = (x, y, c), (x, y, 1 - c)

        def rows(px, py, pc):
            return out_ref.at[pl.ds((4 * px + 2 * py + pc) * m_per, m_per), :]

        def copy(k, block, to, src=None):
            return pltpu.make_async_remote_copy(
                src_ref=rows(*block) if src is None else src, dst_ref=rows(*block),
                send_sem=send_sems.at[k], recv_sem=recv_sems.at[k], device_id=to, device_id_type=MESH)

        mine = pltpu.make_async_copy(x_ref, rows(*me), local_sem)
        mine.start()
        first = [copy(0, me, sibling, src=x_ref)]
        first += [copy(1 + j, me, (*chip, c), src=x_ref) for j, chip in enumerate(chips)]
        for cp in first:
            cp.start()
        passed = [copy(4 + j, (*chip, c), sibling) for j, chip in enumerate(chips)]
        for j, chip in enumerate(chips):
            copy(1 + j, (*chip, c), me).wait_recv()
            passed[j].start()
        copy(0, sibling, me).wait_recv()
        for j, chip in enumerate(chips):
            copy(4 + j, (*chip, 1 - c), me).wait_recv()
        for cp in first + passed:
            cp.wait_send()
        mine.wait()

    return pl.pallas_call(
        body, name=name, out_shape=jax.ShapeDtypeStruct((N_DEV * m_per, n), v.dtype),
        in_specs=[pl.BlockSpec(memory_space=pltpu.VMEM)], out_specs=pl.BlockSpec(memory_space=pltpu.VMEM),
        scratch_shapes=[pltpu.SemaphoreType.DMA((7,)), pltpu.SemaphoreType.DMA((7,)), pltpu.SemaphoreType.DMA],
    )(v)


def _cast_place(name, w, k_idx):
    R, C = w.shape
    tr = _pick(R, max(16, (512 * 1024) // C // 16 * 16), 16)

    def body(k_ref, w_ref, o_ref):
        o_ref[...] = w_ref[...].astype(BF16)

    return pl.pallas_call(
        body, name=name,
        grid_spec=pltpu.PrefetchScalarGridSpec(
            num_scalar_prefetch=1, grid=(R // tr,),
            in_specs=[pl.BlockSpec((tr, C), lambda i, k_ref: (i, 0))],
            out_specs=pl.BlockSpec((None, tr, C), lambda i, k_ref: (k_ref[0], i, 0))),
        out_shape=jax.ShapeDtypeStruct((N_CHIPS, R, C), BF16),
        compiler_params=_params(("parallel",)),
    )(k_idx, w)


def _chip_copies(src_of, dst_of, got_of, n, sems, receiving):
    x, y, c, chips = _place()
    k_me = 2 * x + y
    send_sems, recv_sems = sems
    out = []
    for a in range(n):
        for j, chip in enumerate(chips):
            k_chip = 2 * chip[0] + chip[1]
            if receiving:
                src = dst = got_of(a, k_chip, c)
                to = (x, y, c)
            else:
                src, dst, to = src_of(a, k_chip, k_me, c), dst_of(a, k_me, c), (*chip, c)
            out.append(pltpu.make_async_remote_copy(
                src_ref=src, dst_ref=dst, send_sem=send_sems.at[3 * a + j], recv_sem=recv_sems.at[3 * a + j],
                device_id=to, device_id_type=MESH))
    return out


class _ChipJob:
    def start(self, j_in, j_out, sems):
        for send in self.copies(j_in, j_out, sems, False):
            send.start()

    def finish(self, j_in, j_out, sems):
        for recv in self.copies(j_in, j_out, sems, True):
            recv.wait_recv()
        for send in self.copies(j_in, j_out, sems, False):
            send.wait_send()


class _GatherJob(_ChipJob):
    def __init__(self, bufs, part=0, n_parts=1):
        n = len(bufs)
        self.inputs, self.n_alias = list(bufs), n
        self.sem_shapes = [pltpu.SemaphoreType.DMA((3 * n,)), pltpu.SemaphoreType.DMA((3 * n,))]
        self.half = [b.shape[1] // 2 for b in bufs]
        self.part, self.n_parts = part, n_parts

    def copies(self, j_in, j_out, sems, receiving):
        def half(a, k, c):
            rows = self.half[a] // self.n_parts
            return j_out[a].at[k, pl.ds(c * self.half[a] + self.part * rows, rows)]

        return _chip_copies(lambda a, k_chip, k_me, c: half(a, k_me, c), half, half, len(self.half), sems, receiving)


class _ScatterJob(_ChipJob):
    def __init__(self, parts, slots):
        n = len(parts)
        self.n = n
        self.inputs, self.n_alias = list(parts) + list(slots), n
        self.sem_shapes = [pltpu.SemaphoreType.DMA((3 * n,)), pltpu.SemaphoreType.DMA((3 * n,))]

    def copies(self, j_in, j_out, sems, receiving):
        return _chip_copies(lambda a, k_chip, k_me, c: j_in[a].at[k_chip], lambda a, k_me, c: j_out[a].at[k_me],
                            lambda a, k_chip, c: j_out[a].at[k_chip], self.n, sems, receiving)


class _SwapJob(_ChipJob):
    def __init__(self, grads):
        n = len(grads)
        self.n = n
        self.half = [g.shape[1] // 2 for g in grads]
        landing = [lax.empty((N_CHIPS, g.shape[1] // 2, g.shape[2]), g.dtype) for g in grads]
        self.inputs, self.n_alias = list(grads) + landing, n
        self.sem_shapes = [pltpu.SemaphoreType.DMA((n,)), pltpu.SemaphoreType.DMA((n,))]

    def copies(self, j_in, j_out, sems, receiving):
        x, y, c, _ = _place()
        out = []
        for a in range(self.n):
            if receiving:
                src, to = j_out[a], (x, y, c)
            else:
                src, to = j_in[a].at[:, pl.ds((1 - c) * self.half[a], self.half[a])], (x, y, 1 - c)
            out.append(pltpu.make_async_remote_copy(src_ref=src, dst_ref=j_out[a], send_sem=sems[0].at[a],
                                                    recv_sem=sems[1].at[a], device_id=to, device_id_type=MESH))
        return out


class _Jobs:
    def __init__(self, jobs):
        self.jobs = jobs
        split = [(j.inputs[:len(j.inputs) - j.n_alias], j.inputs[len(j.inputs) - j.n_alias:]) for j in jobs]
        self.inputs = [a for plain, _ in split for a in plain] + [a for _, aliased in split for a in aliased]
        self.n_alias = sum(j.n_alias for j in jobs)
        self.sem_shapes = [s for j in jobs for s in j.sem_shapes]

    def _each(self, j_in, j_out, sems):
        n_plain = len(j_in) - self.n_alias
        p0 = a0 = s0 = 0
        for j in self.jobs:
            n_p, n_s = len(j.inputs) - j.n_alias, len(j.sem_shapes)
            ins = list(j_in[p0:p0 + n_p]) + list(j_in[n_plain + a0:n_plain + a0 + j.n_alias])
            yield j, ins, j_out[a0:a0 + j.n_alias], sems[s0:s0 + n_s]
            p0, a0, s0 = p0 + n_p, a0 + j.n_alias, s0 + n_s

    def start(self, j_in, j_out, sems):
        for j, ins, outs, s in self._each(j_in, j_out, sems):
            j.start(ins, outs, s)

    def finish(self, j_in, j_out, sems):
        for j, ins, outs, s in self._each(j_in, j_out, sems):
            j.finish(ins, outs, s)


class _ShareJob(_ChipJob):
    def __init__(self, bufs):
        n = len(bufs)
        self.n = n
        self.inputs, self.n_alias = list(bufs), n
        self.sem_shapes = [pltpu.SemaphoreType.DMA((n,)), pltpu.SemaphoreType.DMA((n,))]

    def copies(self, j_in, j_out, sems, receiving):
        x, y, c, _ = _place()
        out = []
        for a in range(self.n):
            hc, to = (1 - c, (x, y, c)) if receiving else (c, (x, y, 1 - c))
            out.append(pltpu.make_async_remote_copy(
                src_ref=j_out[a].at[hc], dst_ref=j_out[a].at[hc], send_sem=sems[0].at[a], recv_sem=sems[1].at[a],
                device_id=to, device_id_type=MESH))
        return out


class _ForwardJob(_ChipJob):
    def __init__(self, bufs):
        n = len(bufs)
        self.n = n
        self.half = [b.shape[1] // 2 for b in bufs]
        self.inputs, self.n_alias = list(bufs), n
        self.sem_shapes = [pltpu.SemaphoreType.DMA((3 * n,)), pltpu.SemaphoreType.DMA((3 * n,))]

    def copies(self, j_in, j_out, sems, receiving):
        x, y, c, chips = _place()
        out = []
        for a in range(self.n):
            for q, chip in enumerate(chips):
                hc, to = (1 - c, (x, y, c)) if receiving else (c, (x, y, 1 - c))
                part = j_out[a].at[2 * chip[0] + chip[1], pl.ds(hc * self.half[a], self.half[a])]
                out.append(pltpu.make_async_remote_copy(
                    src_ref=part, dst_ref=part, send_sem=sems[0].at[3 * a + q], recv_sem=sems[1].at[3 * a + q],
                    device_id=to, device_id_type=MESH))
        return out


def _forward_halves(name, bufs):
    n = len(bufs)

    def body(*refs):
        out_refs = refs[n:2 * n]
        send_sems, recv_sems = refs[2 * n:]
        x, y, c, chips = _place()
        started, waits = [], []
        for a in range(n):
            rh = bufs[a].shape[1] // 2
            for j, chip in enumerate(chips):
                k_chip = 2 * chip[0] + chip[1]
                got = out_refs[a].at[k_chip, pl.ds(c * rh, rh)]
                cp = pltpu.make_async_remote_copy(src_ref=got, dst_ref=got, send_sem=send_sems.at[3 * a + j],
                                                  recv_sem=recv_sems.at[3 * a + j], device_id=(x, y, 1 - c),
                                                  device_id_type=MESH)
                cp.start()
                started.append(cp)
                other = out_refs[a].at[k_chip, pl.ds((1 - c) * rh, rh)]
                waits.append(pltpu.make_async_remote_copy(
                    src_ref=other, dst_ref=other, send_sem=send_sems.at[3 * a + j], recv_sem=recv_sems.at[3 * a + j],
                    device_id=(x, y, c), device_id_type=MESH))
        for cp in waits:
            cp.wait_recv()
        for cp in started:
            cp.wait_send()

    return pl.pallas_call(
        body, name=name, out_shape=[jax.ShapeDtypeStruct(b.shape, b.dtype) for b in bufs],
        in_specs=[_HBM] * n, out_specs=[_HBM] * n, input_output_aliases={a: a for a in range(n)},
        scratch_shapes=[pltpu.SemaphoreType.DMA((3 * n,)), pltpu.SemaphoreType.DMA((3 * n,))],
    )(*bufs)


def _proj_gather(h, buf, order):
    S, D = h.shape
    nb = buf.shape[2]
    tm = _pick(S, 1024, 16)
    tn = _pick(nb, 1408, 128)
    per = nb // tn
    nj, ni = N_CHIPS * per, S // tm
    rh = D // 2
    seq = [(0, t) for t in range(per)] + [(p, t) for t in range(per) for p in (1, 2)] + [(3, t) for t in range(per)]
    tile_chip = jnp.stack([order[p] for p, _ in seq])
    tile_of = jnp.array([t for _, t in seq], jnp.int32)

    def body(chip_ref, t_ref, h_ref, buf_in, proj_ref, buf_ref, wbuf, tile_sems, ici_send, ici_recv, d2d_send, d2d_recv):
        j, i = pl.program_id(0), pl.program_id(1)
        x, y, c, chips = _place()
        k_me = 2 * x + y

        def part(k, hc, t):
            return buf_ref.at[k, pl.ds(hc * rh, rh), pl.ds(t * tn, tn)]

        def ici(q, t, receiving):
            k_chip = 2 * chips[q][0] + chips[q][1]
            src, to = (part(k_chip, c, t), (x, y, c)) if receiving else (part(k_me, c, t), (*chips[q], c))
            return pltpu.make_async_remote_copy(src_ref=src, dst_ref=src, send_sem=ici_send.at[q * per + t],
                                                recv_sem=ici_recv.at[q * per + t], device_id=to, device_id_type=MESH)

        def d2d(q, t, receiving):
            k_chip = 2 * chips[q][0] + chips[q][1]
            src, to = (part(k_chip, 1 - c, t), (x, y, c)) if receiving else (part(k_chip, c, t), (x, y, 1 - c))
            return pltpu.make_async_remote_copy(src_ref=src, dst_ref=src, send_sem=d2d_send.at[q * per + t],
                                                recv_sem=d2d_recv.at[q * per + t], device_id=to, device_id_type=MESH)

        def tile_copy(n):
            col = pl.multiple_of(t_ref[n] * tn, 128)
            return pltpu.make_async_copy(buf_ref.at[chip_ref[n], :, pl.ds(col, tn)], wbuf.at[n % 2], tile_sems.at[n % 2])

        @pl.when(jnp.logical_and(j == 0, i == 0))
        def _():
            for t in range(per):
                ici(0, t, False).start()
                ici(1, t, False).start()
            tile_copy(0).start()

        @pl.when(i == 0)
        def _():
            tile_copy(j).wait()
            for n in range(per, nj):
                p, t = seq[n]

                @pl.when(j + 1 == n)
                def _():
                    ici(p - 1, t, True).wait_recv()
                    d2d(p - 1, t, False).start()
                    if (p, t) == (1, per - 1):
                        for q in range(2):
                            for tt in range(per):
                                ici(q, tt, False).wait_send()
                        for tt in range(per):
                            ici(2, tt, False).start()
                    d2d(p - 1, t, True).wait_recv()

            @pl.when(j + 1 < nj)
            def _():
                tile_copy(j + 1).start()

        proj_ref[...] = jnp.dot(h_ref[...], wbuf[j % 2], preferred_element_type=F32)

        @pl.when(jnp.logical_and(j == nj - 1, i == ni - 1))
        def _():
            for t in range(per):
                ici(2, t, False).wait_send()
                for q in range(3):
                    d2d(q, t, False).wait_send()

    n_sems = 3 * per
    return pl.pallas_call(
        body, name="proj",
        grid_spec=pltpu.PrefetchScalarGridSpec(
            num_scalar_prefetch=2, grid=(nj, ni),
            in_specs=[pl.BlockSpec((tm, D), lambda j, i, kc, kt: (i, 0)), _HBM],
            out_specs=[pl.BlockSpec((tm, tn), lambda j, i, kc, kt: (i, kc[j] * per + kt[j])), _HBM],
            scratch_shapes=[pltpu.VMEM((2, D, tn), BF16), pltpu.SemaphoreType.DMA((2,))]
            + [pltpu.SemaphoreType.DMA((n_sems,))] * 4),
        out_shape=[jax.ShapeDtypeStruct((S, N_CHIPS * nb), F32), jax.ShapeDtypeStruct(buf.shape, buf.dtype)],
        input_output_aliases={3: 1}, compiler_params=_params(("arbitrary", "arbitrary")),
    )(tile_chip, tile_of, h, buf)


def _swap_halves(name, grads):
    n = len(grads)

    def body(*refs):
        in_refs, out_refs = refs[:n], refs[n:2 * n]
        send_sems, recv_sems = refs[2 * n:]
        x, y, c, _ = _place()
        cps = []
        for a in range(n):
            rh = grads[a].shape[1] // 2
            cp = pltpu.make_async_remote_copy(
                src_ref=in_refs[a].at[:, pl.ds((1 - c) * rh, rh)], dst_ref=out_refs[a],
                send_sem=send_sems.at[a], recv_sem=recv_sems.at[a], device_id=(x, y, 1 - c), device_id_type=MESH)
            cp.start()
            cps.append(cp)
        for cp in cps:
            cp.wait()

    return pl.pallas_call(
        body, name=name,
        out_shape=[jax.ShapeDtypeStruct((N_CHIPS, g.shape[1] // 2, g.shape[2]), g.dtype) for g in grads],
        in_specs=[_HBM] * n, out_specs=[_HBM] * n,
        scratch_shapes=[pltpu.SemaphoreType.DMA((n,)), pltpu.SemaphoreType.DMA((n,))],
    )(*grads)


def _pair_sum(name, g, q, c_idx):
    _, R, C = g.shape
    rh = R // 2
    tr = _pick(rh, max(16, (512 * 1024) // C // 16 * 16), 16)
    nh = rh // tr

    def body(c_ref, g_ref, q_ref, o_ref, o2_ref):
        s = (g_ref[...].astype(F32) + q_ref[...].astype(F32)).astype(BF16)
        o_ref[...] = s
        o2_ref[...] = s

    out_spec = pl.BlockSpec((None, tr, C), lambda k, i, c_ref: (k, i, 0))
    return pl.pallas_call(
        body, name=name,
        grid_spec=pltpu.PrefetchScalarGridSpec(
            num_scalar_prefetch=1, grid=(N_CHIPS, nh),
            in_specs=[pl.BlockSpec((None, tr, C), lambda k, i, c_ref: (k, c_ref[0] * nh + i, 0)),
                      pl.BlockSpec((None, tr, C), lambda k, i, c_ref: (k, i, 0))],
            out_specs=[out_spec, out_spec]),
        out_shape=[jax.ShapeDtypeStruct((N_CHIPS, rh, C), BF16)] * 2,
        compiler_params=_params(("parallel", "parallel")),
    )(c_idx, g, q)


def _sum_chips(name, r, c_idx):
    _, rh, C = r.shape
    tr = _pick(rh, max(16, (256 * 1024) // C // 16 * 16), 16)

    def body(c_ref, r_ref, o_ref):
        acc = r_ref[0].astype(F32)
        for k in range(1, N_CHIPS):
            acc = acc + r_ref[k].astype(F32)
        o_ref[...] = acc

    return pl.pallas_call(
        body, name=name,
        grid_spec=pltpu.PrefetchScalarGridSpec(
            num_scalar_prefetch=1, grid=(rh // tr,),
            in_specs=[pl.BlockSpec((N_CHIPS, tr, C), lambda i, c_ref: (0, i, 0))],
            out_specs=pl.BlockSpec((None, tr, C), lambda i, c_ref: (c_ref[0], i, 0))),
        out_shape=jax.ShapeDtypeStruct((2, rh, C), F32), compiler_params=_params(("parallel",)),
    )(c_idx, r)


def _share_halves(bufs):
    n = len(bufs)

    def body(*refs):
        out_refs = refs[n:2 * n]
        send_sems, recv_sems = refs[2 * n:]
        x, y, c, _ = _place()
        cps = []
        for a in range(n):
            cp = pltpu.make_async_remote_copy(
                src_ref=out_refs[a].at[c], dst_ref=out_refs[a].at[c], send_sem=send_sems.at[a],
                recv_sem=recv_sems.at[a], device_id=(x, y, 1 - c), device_id_type=MESH)
            cp.start()
            cps.append(cp)
        for a, cp in enumerate(cps):
            got = out_refs[a].at[1 - c]
            pltpu.make_async_remote_copy(src_ref=got, dst_ref=got, send_sem=send_sems.at[a], recv_sem=recv_sems.at[a],
                                         device_id=(x, y, c), device_id_type=MESH).wait_recv()
        for cp in cps:
            cp.wait_send()

    return pl.pallas_call(
        body, name="share_halves",
        out_shape=[jax.ShapeDtypeStruct(b.shape, b.dtype) for b in bufs],
        in_specs=[_HBM] * n, out_specs=[_HBM] * n, input_output_aliases={a: a for a in range(n)},
        scratch_shapes=[pltpu.SemaphoreType.DMA((n,)), pltpu.SemaphoreType.DMA((n,))],
    )(*bufs)


_BIG = ("w_in", "w_conv_out", "w_hgrn_out", "w_o", "w_ffn_gate", "w_ffn_up", "w_ffn_down")
_ROW_SHARDED = ("w_o", "w_ffn_down")
_WEIGHTS = ("w_ada", "b_ada", "norm_mix_g", "w_in", "conv_w", "lb_param", "gnorm_g", "w_conv_out", "w_hgrn_out",
            "w_o", "norm_ffn_g", "w_ffn_gate", "w_ffn_up", "w_ffn_down", "norm_final_g")


def _pack_rows(pieces):
    flat = jnp.concatenate([p.reshape(-1) for p in pieces])
    pad = (-flat.shape[0]) % 1024
    return jnp.pad(flat, (0, pad)).reshape(8, -1)


def kernel(x, c, w_ada, b_ada, norm_mix_g, w_in, conv_w, lb_param, gnorm_g, w_conv_out, w_hgrn_out, w_o, norm_ffn_g, w_ffn_gate, w_ffn_up, w_ffn_down, norm_final_g, loss_target, m_w_ada, m_b_ada, m_norm_mix_g, m_w_in, m_conv_w, m_lb_param, m_gnorm_g, m_w_conv_out, m_w_hgrn_out, m_w_o, m_norm_ffn_g, m_w_ffn_gate, m_w_ffn_up, m_w_ffn_down, m_norm_final_g, v_w_ada, v_b_ada, v_norm_mix_g, v_w_in, v_conv_w, v_lb_param, v_gnorm_g, v_w_conv_out, v_w_hgrn_out, v_w_o, v_norm_ffn_g, v_w_ffn_gate, v_w_ffn_up, v_w_ffn_down, v_norm_final_g):
    w = dict(w_ada=w_ada, b_ada=b_ada, norm_mix_g=norm_mix_g, w_in=w_in, conv_w=conv_w, lb_param=lb_param,
             gnorm_g=gnorm_g, w_conv_out=w_conv_out, w_hgrn_out=w_hgrn_out, w_o=w_o, norm_ffn_g=norm_ffn_g,
             w_ffn_gate=w_ffn_gate, w_ffn_up=w_ffn_up, w_ffn_down=w_ffn_down, norm_final_g=norm_final_g)
    m = dict(w_ada=m_w_ada, b_ada=m_b_ada, norm_mix_g=m_norm_mix_g, w_in=m_w_in, conv_w=m_conv_w, lb_param=m_lb_param,
             gnorm_g=m_gnorm_g, w_conv_out=m_w_conv_out, w_hgrn_out=m_w_hgrn_out, w_o=m_w_o, norm_ffn_g=m_norm_ffn_g,
             w_ffn_gate=m_w_ffn_gate, w_ffn_up=m_w_ffn_up, w_ffn_down=m_w_ffn_down, norm_final_g=m_norm_final_g)
    v = dict(w_ada=v_w_ada, b_ada=v_b_ada, norm_mix_g=v_norm_mix_g, w_in=v_w_in, conv_w=v_conv_w, lb_param=v_lb_param,
             gnorm_g=v_gnorm_g, w_conv_out=v_w_conv_out, w_hgrn_out=v_w_hgrn_out, w_o=v_w_o, norm_ffn_g=v_norm_ffn_g,
             w_ffn_gate=v_w_ffn_gate, w_ffn_up=v_w_ffn_up, w_ffn_down=v_w_ffn_down, norm_final_g=v_norm_final_g)
    two_d = lambda a: a.reshape((-1, a.shape[-1])) if a.ndim != 2 else a
    shapes = {k: a.shape for k, a in w.items()}
    w, m, v = ({k: two_d(a) for k, a in t.items()} for t in (w, m, v))
    w["lb_param"], m["lb_param"], v["lb_param"] = lb_param, m_lb_param, v_lb_param
    S, D = x.shape[1], x.shape[2]
    d_conv = D // 2
    ix, iy, ic = lax.axis_index("x"), lax.axis_index("y"), lax.axis_index("c")
    k_me = 2 * ix + iy
    dev = 2 * k_me + ic
    cw_shard = w["conv_w"].shape[1]
    ada_cols = w["w_ada"].shape[1]

    got = _allgather_rows("gather_cond", _pack_rows([c, w["conv_w"]])).reshape(N_DEV, -1)
    c_all = got[:, :D]
    conv_full = got[::2, D:D + 3 * cw_shard].reshape(N_CHIPS, 3, cw_shard).transpose(1, 0, 2).reshape(3, -1)
    c_act, c_act_b, lb = _prep(c_all, lb_param)
    b_cols = lax.dynamic_slice(w["b_ada"], (0, k_me * ada_cols), (1, ada_cols))
    mod_cols, = _matmul("ada_fwd", [(c_act_b, 'n', w["w_ada"].astype(BF16), 'n')], [0], N_DEV, ada_cols, D,
                        N_DEV, _pick(ada_cols, 1536, 128), D, [(b_cols, 'row', 0)], [(F32, None)],
                        lambda accs, ex: [accs[0] + ex[0]])
    mod_all = _allgather_rows("gather_mod", mod_cols).reshape(N_CHIPS, 2, N_DEV, ada_cols)[:, 0]
    mod_all = mod_all.transpose(1, 0, 2).reshape(N_DEV, -1)
    mod = lax.dynamic_slice(mod_all, (dev, 0), (1, mod_all.shape[1]))
    k_idx = jnp.reshape(k_me, (1,)).astype(jnp.int32)
    c_idx = jnp.reshape(ic, (1,)).astype(jnp.int32)
    bufs = {k: _cast_place("cast_" + k, w[k], k_idx) for k in _BIG}
    order = jnp.stack([k_me, 2 * (1 - ix) + iy, 2 * ix + (1 - iy), 2 * (1 - ix) + (1 - iy)]).astype(jnp.int32)

    loss, dx, small, arrived = _local_step(
        x[0], loss_target[0], mod, w["norm_mix_g"], lb, w["gnorm_g"], w["norm_ffn_g"], w["norm_final_g"], conv_full,
        bufs, c_idx, order)

    pieces = [small["dmod"], small["dg_mix"], small["dg_ffn"], small["dg_final"], small["dlb"], small["dgn"],
              small["dconv_w"], loss]
    sizes = [p.size for p in pieces]
    parts = _allgather_rows("gather_small", _pack_rows(pieces)).reshape(N_DEV, -1)
    total = _small_reduce(parts)
    offs = [0]
    for s in sizes:
        offs.append(offs[-1] + s)
    tot = [total[:, offs[i]:offs[i + 1]] for i in range(len(sizes))]
    dmod_all = parts[:, :sizes[0]]
    grads = {
        "b_ada": tot[0], "norm_mix_g": tot[1], "norm_ffn_g": tot[2], "norm_final_g": tot[3],
        "lb_param": _lb_grad(tot[4], lb), "gnorm_g": tot[5],
        "conv_w": lax.dynamic_slice(tot[6].reshape(3, -1), (0, k_me * cw_shard), (3, cw_shard)),
    }
    loss_out = tot[7][0, 0]

    for k in _BIG:
        grads[k] = arrived[k].reshape(-1, arrived[k].shape[-1])

    delta, new_m, new_v = {}, {}, {}
    dmod_cols = lax.dynamic_slice(dmod_all, (0, k_me * ada_cols), (N_DEV, ada_cols))
    grads["w_ada"], delta["w_ada"], new_m["w_ada"], new_v["w_ada"] = _ada_update(
        c_act.T, dmod_cols, w["w_ada"], m["w_ada"], v["w_ada"])
    for k in _WEIGHTS:
        if k != "w_ada":
            grads[k], delta[k], new_m[k], new_v[k] = _adamw("adamw_" + k, w[k], grads[k], m[k], v[k])
    outs = [loss_out, dx.reshape(x.shape)]
    for t in (grads, delta, new_m, new_v):
        outs += [t[k].reshape(shapes[k]) for k in _WEIGHTS]
    return tuple(outs)
```

```python
import functools

import jax
import jax.numpy as jnp
from jax import lax
from jax.experimental import pallas as pl
from jax.experimental.pallas import tpu as pltpu

F32 = jnp.float32
BF16 = jnp.bfloat16
MESH = pl.DeviceIdType.MESH

EPS = 1e-6
HEAD = 128
BLK = 16
N_CHIPS = 4
N_DEV = 8
SUBLANES, LANES = 8, 128
VMEM_LIMIT_BYTES = 56 * 1024 * 1024

ADAM_LR = 0.001
ADAM_B1 = 0.9
ADAM_B2 = 0.999
ADAM_EPS = 1e-08
ADAM_WD = 0.01
ADAM_STEP = 10


def _pick(dim, pref, mult):
    if dim <= pref:
        return dim
    t = (pref // mult) * mult
    while t >= mult:
        if dim % t == 0:
            return t
        t -= mult
    return dim


def _sig(x):
    return 1.0 / (1.0 + jnp.exp(-x))


def _params(sem):
    return pltpu.CompilerParams(dimension_semantics=sem, vmem_limit_bytes=VMEM_LIMIT_BYTES)


def _gj(j, i, k):
    return j


def _gk(j, i, k):
    return k


def _wspec(arr, rt, ct, r_of, c_of):
    if arr.ndim == 2:
        return pl.BlockSpec((rt, ct), lambda j, i, k: (r_of(j, i, k), c_of(j, i, k)))
    per = arr.shape[2] // ct
    assert arr.shape[2] % ct == 0
    return pl.BlockSpec((None, rt, ct),
                        lambda j, i, k: (c_of(j, i, k) // per, r_of(j, i, k), c_of(j, i, k) % per))


_HBM = pl.BlockSpec(memory_space=pltpu.HBM)


def _host(job, grid, in_specs, args, out_specs, out_shape, scratch):
    if job is None:
        return {}, (lambda body: body)
    n_in, n_out, n_scr = len(args), len(out_shape), len(scratch)
    n_job, n_alias = len(job.inputs), job.n_alias
    in_specs += [_HBM] * n_job
    args += job.inputs
    out_specs += [_HBM] * n_alias
    out_shape += [jax.ShapeDtypeStruct(a.shape, a.dtype) for a in job.inputs[n_job - n_alias:]]
    scratch += job.sem_shapes
    aliases = {n_in + n_job - n_alias + t: n_out + t for t in range(n_alias)}

    def wrap(body):
        def hosted(*refs):
            ins, refs = refs[:n_in], refs[n_in:]
            j_in, refs = refs[:n_job], refs[n_job:]
            outs, refs = refs[:n_out], refs[n_out:]
            j_out, refs = refs[:n_alias], refs[n_alias:]
            scr, sems = refs[:n_scr], refs[n_scr:]
            first = functools.reduce(jnp.logical_and, [pl.program_id(d) == 0 for d in range(len(grid))])
            last = functools.reduce(jnp.logical_and, [pl.program_id(d) == grid[d] - 1 for d in range(len(grid))])

            @pl.when(first)
            def _():
                job.start(j_in, j_out, sems)

            body(*ins, *outs, *scr)

            @pl.when(last)
            def _():
                job.finish(j_in, j_out, sems)

        return hosted

    return aliases, wrap


EPILOGUE_COLS = 768


def _plain(accs, extras):
    return accs


def _matmul(name, pairs, acc_of, M, N, K, tm, tn, tk, extras, outs, epilogue, job=None, rows_outer=False):
    assert M % tm == 0 and N % tn == 0 and K % tk == 0, (name, M, N, K, tm, tn, tk)
    nj, ni, nk = N // tn, M // tm, K // tk
    n_pairs, n_extra, n_out = len(pairs), len(extras), len(outs)
    n_acc = max(acc_of) + 1
    in_specs, args, dims = [], [], []
    lhs_of, seen = [], {}
    for a, am, b, bm in pairs:
        if (id(a), am) not in seen:
            seen[id(a), am] = len(args)
            args.append(a)
            if am == 'n':
                in_specs.append(pl.BlockSpec((tm, tk), lambda j, i, k: (i, k)))
            else:
                in_specs.append(pl.BlockSpec((tk, tm), lambda j, i, k: (k, i)))
        lhs_of.append(seen[id(a), am])
    n_lhs = len(args)
    for a, am, b, bm in pairs:
        if bm == 'n':
            in_specs.append(_wspec(b, tk, tn, _gk, _gj))
        else:
            in_specs.append(_wspec(b, tn, tk, _gj, _gk))
        dims.append((((1 if am == 'n' else 0,), (0 if bm == 'n' else 1,)), ((), ())))
        args.append(b)
    for e, kind, off in extras:
        assert off % tn == 0, (name, off, tn)
        o = off // tn
        if kind == 'tile':
            in_specs.append(pl.BlockSpec((tm, tn), lambda j, i, k, o=o: (i, j + o)))
        else:
            in_specs.append(pl.BlockSpec((1, tn), lambda j, i, k, o=o: (0, j + o)))
        args.append(e)
    out_shape, out_specs = [], []
    for dt, nb in outs:
        if nb is None:
            out_shape.append(jax.ShapeDtypeStruct((M, N), dt))
            out_specs.append(pl.BlockSpec((tm, tn), lambda j, i, k: (i, j)))
        else:
            assert nb % tn == 0 and N % nb == 0
            per = nb // tn
            out_shape.append(jax.ShapeDtypeStruct((N // nb, M, nb), dt))
            out_specs.append(pl.BlockSpec((None, tm, tn), lambda j, i, k, per=per: (j // per, i, j % per)))

    def body(*refs):
        n_ab = n_lhs + n_pairs
        e_refs = refs[n_ab:n_ab + n_extra]
        o_refs = refs[n_ab + n_extra:n_ab + n_extra + n_out]
        acc_refs = refs[n_ab + n_extra + n_out:]
        def products(cols):
            sums = [None] * n_acc
            for p in range(n_pairs):
                b_ref = refs[n_lhs + p]
                b = b_ref[:, cols] if pairs[p][3] == 'n' else b_ref[cols, :]
                prod = lax.dot_general(refs[lhs_of[p]][...], b, dims[p], preferred_element_type=F32)
                a = acc_of[p]
                sums[a] = prod if sums[a] is None else sums[a] + prod
            return sums

        def finish(accs, cols):
            res = epilogue(accs, [e[:, cols] for e in e_refs])
            for o_ref, r in zip(o_refs, res):
                o_ref[:, cols] = r.astype(o_ref.dtype)

        if nk == 1 and epilogue is not _plain:
            for c0 in range(0, tn, EPILOGUE_COLS):
                cols = slice(c0, min(c0 + EPILOGUE_COLS, tn))
                finish(products(cols), cols)
            return
        whole = slice(0, tn)
        sums = products(whole)
        if nk == 1:
            finish(sums, whole)
        else:
            k = pl.program_id(2)
            targets = o_refs if sum_in_outs else acc_refs

            @pl.when(k == 0)
            def _():
                for a in range(n_acc):
                    targets[a][...] = sums[a]

            @pl.when(k > 0)
            def _():
                for a in range(n_acc):
                    targets[a][...] += sums[a]

            if not sum_in_outs:
                @pl.when(k == nk - 1)
                def _():
                    finish([acc_refs[a][...] for a in range(n_acc)], whole)

    sum_in_outs = epilogue is _plain and nk > 1 and n_out == n_acc and all(dt == F32 for dt, _ in outs)
    scratch = [pltpu.VMEM((tm, tn), F32) for _ in range(n_acc)] if nk > 1 and not sum_in_outs else []
    grid = (nj, ni, nk)
    if rows_outer:
        grid = (ni, nj, nk)
        swap = lambda spec: pl.BlockSpec(spec.block_shape, lambda i, j, k, f=spec.index_map: f(j, i, k))
        in_specs, out_specs = [swap(s) for s in in_specs], [swap(s) for s in out_specs]
    aliases, wrap = _host(job, grid, in_specs, args, out_specs, out_shape, scratch)
    sem = ("parallel", "parallel", "arbitrary") if job is None else ("arbitrary",) * 3
    return pl.pallas_call(
        wrap(body), name=name, grid=grid, in_specs=in_specs, out_specs=out_specs, out_shape=out_shape,
        scratch_shapes=scratch, input_output_aliases=aliases, compiler_params=_params(sem),
    )(*args)


def _row_spec(tr, d):
    return pl.BlockSpec((tr, d), lambda i: (i, 0))


def _vec_spec(d):
    return pl.BlockSpec((1, d), lambda i: (0, 0))


def _norm_mod(name, x, g, sc, sh, job=None):
    S, D = x.shape
    tr = _pick(S, 256, 8)

    def body(x_ref, g_ref, sc_ref, sh_ref, h_ref):
        xv = x_ref[...]
        r = lax.rsqrt(jnp.mean(xv * xv, axis=-1, keepdims=True) + EPS)
        h_ref[...] = ((xv * r) * g_ref[...] * (1.0 + sc_ref[...]) + sh_ref[...]).astype(BF16)

    grid = (S // tr,)
    in_specs = [_row_spec(tr, D), _vec_spec(D), _vec_spec(D), _vec_spec(D)]
    args = [x, g, sc, sh]
    out_specs, out_shape, scratch = [_row_spec(tr, D)], [jax.ShapeDtypeStruct((S, D), BF16)], []
    aliases, wrap = _host(job, grid, in_specs, args, out_specs, out_shape, scratch)
    res = pl.pallas_call(
        wrap(body), name=name, grid=grid, in_specs=in_specs, out_specs=out_specs, out_shape=out_shape,
        scratch_shapes=scratch, input_output_aliases=aliases,
        compiler_params=_params(("parallel" if job is None else "arbitrary",)),
    )(*args)
    return res[0] if job is None else res


def _loss_head(x2, target, g, ff, gt):
    S, D = x2.shape
    tr = _pick(S, 256, 8)

    def body(x_ref, t_ref, g_ref, ff_ref, gt_ref, dx_ref, dffb_ref, loss_ref, dgt_ref, dg_ref):
        i = pl.program_id(0)

        @pl.when(i == 0)
        def _():
            loss_ref[...] = jnp.zeros_like(loss_ref)
            dgt_ref[...] = jnp.zeros_like(dgt_ref)
            dg_ref[...] = jnp.zeros_like(dg_ref)

        xv = x_ref[...]
        gv = g_ref[...]
        r = lax.rsqrt(jnp.mean(xv * xv, axis=-1, keepdims=True) + EPS)
        xh = xv * r
        err = xh * gv - t_ref[...]
        loss_ref[...] += 0.5 * jnp.sum(jnp.mean(err * err, axis=-1, keepdims=True))
        dy = err * (1.0 / D)
        dg_ref[...] += jnp.sum(dy * xh, axis=0, keepdims=True)
        dxh = dy * gv
        dx = r * (dxh - xh * jnp.mean(dxh * xh, axis=-1, keepdims=True))
        dx_ref[...] = dx
        dffb_ref[...] = (dx * gt_ref[...]).astype(BF16)
        dgt_ref[...] += jnp.sum(dx * ff_ref[...], axis=0, keepdims=True)

    return pl.pallas_call(
        body, name="loss_head", grid=(S // tr,),
        in_specs=[_row_spec(tr, D), _row_spec(tr, D), _vec_spec(D), _row_spec(tr, D), _vec_spec(D)],
        out_specs=[_row_spec(tr, D), _row_spec(tr, D), pl.BlockSpec((1, 128), lambda i: (0, 0)),
                   _vec_spec(D), _vec_spec(D)],
        out_shape=[jax.ShapeDtypeStruct((S, D), F32), jax.ShapeDtypeStruct((S, D), BF16),
                   jax.ShapeDtypeStruct((1, 128), F32), jax.ShapeDtypeStruct((1, D), F32),
                   jax.ShapeDtypeStruct((1, D), F32)],
        compiler_params=_params(("arbitrary",)),
    )(x2, target, g, ff, gt)


def _norm_mod_bwd(name, dh, x, g, sc, dres, gated=None, job=None):
    S, D = x.shape
    tr = _pick(S, 256, 8)
    n = S // tr
    with_gate = gated is not None

    def body(*refs):
        if with_gate:
            dh_ref, x_ref, g_ref, sc_ref, dres_ref, mo_ref, gt_ref, dx_ref, dsh_ref, dsc_ref, dg_ref, dmob_ref, dgt_ref = refs
        else:
            dh_ref, x_ref, g_ref, sc_ref, dres_ref, dx_ref, dsh_ref, dsc_ref, dg_ref = refs
        i = pl.program_id(0)

        @pl.when(i == 0)
        def _():
            dsh_ref[...] = jnp.zeros_like(dsh_ref)
            dsc_ref[...] = jnp.zeros_like(dsc_ref)
            if with_gate:
                dgt_ref[...] = jnp.zeros_like(dgt_ref)

        xv = x_ref[...]
        dhv = dh_ref[...]
        gv = g_ref[...]
        scale = 1.0 + sc_ref[...]
        r = lax.rsqrt(jnp.mean(xv * xv, axis=-1, keepdims=True) + EPS)
        xh = xv * r
        dsh_ref[...] += jnp.sum(dhv, axis=0, keepdims=True)
        dsc_ref[...] += jnp.sum(dhv * xh, axis=0, keepdims=True)
        dxh = dhv * (gv * scale)
        dx = dres_ref[...] + r * (dxh - xh * jnp.mean(dxh * xh, axis=-1, keepdims=True))
        dx_ref[...] = dx
        if with_gate:
            dmob_ref[...] = (dx * gt_ref[...]).astype(BF16)
            dgt_ref[...] += jnp.sum(dx * mo_ref[...], axis=0, keepdims=True)

        @pl.when(i == n - 1)
        def _():
            t = dsc_ref[...]
            dg_ref[...] = t * scale
            dsc_ref[...] = t * gv

    in_specs = [_row_spec(tr, D), _row_spec(tr, D), _vec_spec(D), _vec_spec(D), _row_spec(tr, D)]
    args = [dh, x, g, sc, dres]
    out_specs = [_row_spec(tr, D), _vec_spec(D), _vec_spec(D), _vec_spec(D)]
    out_shape = [jax.ShapeDtypeStruct((S, D), F32)] + [jax.ShapeDtypeStruct((1, D), F32)] * 3
    if with_gate:
        in_specs += [_row_spec(tr, D), _vec_spec(D)]
        args += list(gated)
        out_specs += [_row_spec(tr, D), _vec_spec(D)]
        out_shape += [jax.ShapeDtypeStruct((S, D), BF16), jax.ShapeDtypeStruct((1, D), F32)]
    scratch = []
    aliases, wrap = _host(job, (n,), in_specs, args, out_specs, out_shape, scratch)
    return pl.pallas_call(
        wrap(body), name=name, grid=(n,), in_specs=in_specs, out_specs=out_specs, out_shape=out_shape,
        scratch_shapes=scratch, input_output_aliases=aliases, compiler_params=_params(("arbitrary",)),
    )(*args)


CONV_ROWS = 256


def _col_spec(S, off_cols):
    o = off_cols // HEAD
    return pl.BlockSpec((S, HEAD), lambda c, o=o: (0, c + o))


def _conv_taps(u, u_prev, rid):
    s1 = jnp.where(rid >= 1, pltpu.roll(u, 1, 0), pltpu.roll(u_prev, 1, 0))
    s2 = jnp.where(rid >= 2, pltpu.roll(u, 2, 0), pltpu.roll(u_prev, 2, 0))
    return s1, s2


def _conv_fwd(proj, conv_w, d_conv, job=None):
    S = proj.shape[0]
    R = min(CONV_ROWS, S)
    nr = S // R

    def body(ab_ref, ac_ref, ax_ref, w_ref, ya_ref):
        w0, w1, w2 = w_ref[0:1, :], w_ref[1:2, :], w_ref[2:3, :]
        rid = lax.broadcasted_iota(jnp.int32, (R, HEAD), 0)

        def step(c, carry):
            rows = pl.ds(pl.multiple_of(c * R, R), R)
            prev = pl.ds(pl.multiple_of(jnp.maximum(c - 1, 0) * R, R), R)
            u = ac_ref[rows, :] * ax_ref[rows, :]
            u_prev = jnp.where(c > 0, ac_ref[prev, :] * ax_ref[prev, :], 0.0)
            s1, s2 = _conv_taps(u, u_prev, rid)
            y = w0 * s2 + w1 * s1 + w2 * u
            ya_ref[rows, :] = (ab_ref[rows, :] * y).astype(BF16)
            return carry

        lax.fori_loop(0, nr, step, 0)

    grid = (d_conv // HEAD,)
    in_specs = [_col_spec(S, 0), _col_spec(S, d_conv), _col_spec(S, 2 * d_conv), pl.BlockSpec((3, HEAD), lambda c: (0, c))]
    args = [proj, proj, proj, conv_w]
    out_specs, out_shape = [pl.BlockSpec((S, HEAD), lambda c: (0, c))], [jax.ShapeDtypeStruct((S, d_conv), BF16)]
    scratch = []
    aliases, wrap = _host(job, grid, in_specs, args, out_specs, out_shape, scratch)
    res = pl.pallas_call(
        wrap(body), name="conv_fwd", grid=grid, in_specs=in_specs, out_specs=out_specs, out_shape=out_shape,
        scratch_shapes=scratch, input_output_aliases=aliases,
        compiler_params=_params(("parallel" if job is None else "arbitrary",)),
    )(*args)
    return res[0] if job is None else res


def _conv_bwd(dya, proj, conv_w, d_conv, job=None):
    S = proj.shape[0]
    R = min(CONV_ROWS, S)
    nr = S // R

    def body(dya_ref, ab_ref, ac_ref, ax_ref, w_ref, dab_ref, dac_ref, dax_ref, dw_ref):
        w0, w1, w2 = w_ref[0:1, :], w_ref[1:2, :], w_ref[2:3, :]
        rid = lax.broadcasted_iota(jnp.int32, (R, HEAD), 0)

        def step(c, carry):
            dw0, dw1, dw2 = carry
            rows = pl.ds(pl.multiple_of(c * R, R), R)
            prev = pl.ds(pl.multiple_of(jnp.maximum(c - 1, 0) * R, R), R)
            nxt = pl.ds(pl.multiple_of(jnp.minimum(c + 1, nr - 1) * R, R), R)
            a_c, a_x, a_b = ac_ref[rows, :], ax_ref[rows, :], ab_ref[rows, :]
            u = a_c * a_x
            u_prev = jnp.where(c > 0, ac_ref[prev, :] * ax_ref[prev, :], 0.0)
            s1, s2 = _conv_taps(u, u_prev, rid)
            y = w0 * s2 + w1 * s1 + w2 * u
            dy = dya_ref[rows, :]
            dab_ref[rows, :] = (dy * y).astype(BF16)
            dyc = dy * a_b
            dyc_next = jnp.where(c < nr - 1, dya_ref[nxt, :] * ab_ref[nxt, :], 0.0)
            t1 = jnp.where(rid < R - 1, pltpu.roll(dyc, R - 1, 0), pltpu.roll(dyc_next, R - 1, 0))
            t2 = jnp.where(rid < R - 2, pltpu.roll(dyc, R - 2, 0), pltpu.roll(dyc_next, R - 2, 0))
            du = w2 * dyc + w1 * t1 + w0 * t2
            dac_ref[rows, :] = (du * a_x).astype(BF16)
            dax_ref[rows, :] = (du * a_c).astype(BF16)
            dw0 = dw0 + jnp.sum(dyc * s2, axis=0, keepdims=True)
            dw1 = dw1 + jnp.sum(dyc * s1, axis=0, keepdims=True)
            dw2 = dw2 + jnp.sum(dyc * u, axis=0, keepdims=True)
            return dw0, dw1, dw2

        z = jnp.zeros((1, HEAD), F32)
        dw0, dw1, dw2 = lax.fori_loop(0, nr, step, (z, z, z))
        dw_ref[0:1, :] = dw0
        dw_ref[1:2, :] = dw1
        dw_ref[2:3, :] = dw2

    col = pl.BlockSpec((S, HEAD), lambda c: (0, c))
    grid = (d_conv // HEAD,)
    in_specs = [col, _col_spec(S, 0), _col_spec(S, d_conv), _col_spec(S, 2 * d_conv),
                pl.BlockSpec((3, HEAD), lambda c: (0, c))]
    args = [dya, proj, proj, proj, conv_w]
    out_specs = [col, col, col, pl.BlockSpec((3, HEAD), lambda c: (0, c))]
    out_shape = [jax.ShapeDtypeStruct((S, d_conv), BF16)] * 3 + [jax.ShapeDtypeStruct((3, d_conv), F32)]
    scratch = []
    aliases, wrap = _host(job, grid, in_specs, args, out_specs, out_shape, scratch)
    return pl.pallas_call(
        wrap(body), name="conv_bwd", grid=grid, in_specs=in_specs, out_specs=out_specs, out_shape=out_shape,
        scratch_shapes=scratch, input_output_aliases=aliases,
        compiler_params=_params(("parallel" if job is None else "arbitrary",)),
    )(*args)


HGRN_ROWS = 1024
HGRN_FWD_SUB = 64
HGRN_BWD_SUB = 32
HGRN_GATE_ROWS = 128


def _block_cumsum(x, pos):
    for s in (1, 2, 4, 8):
        x = x + jnp.where(pos >= s, pltpu.roll(x, s, 0), 0.0)
    return x


def _block_rev_cumsum(x, pos, rows):
    for s in (1, 2, 4, 8):
        x = x + jnp.where(pos < BLK - s, pltpu.roll(x, rows - s, 0), 0.0)
    return x


def _block_last(x, pos, rows):
    m = jnp.where(pos == BLK - 1, x, 0.0)
    for s in (1, 2, 4, 8):
        m = m + pltpu.roll(m, rows - s, 0)
    return m


def _hgrn_gates(q, z, lb):
    sgq = _sig(q)
    qs = q * sgq
    sg = _sig(z)
    f = lb + (1.0 - lb) * sg
    kk = (1.0 - lb) * (1.0 - sg)
    return sgq, qs, sg, f, kk


def _hgrn_decays(q, z, lb, pos, rows):
    sgq, qs, sg, f, kk = _hgrn_gates(q, z, lb)
    b = _block_cumsum(jnp.log(f), pos)
    return sgq, qs, sg, f, kk, b, _block_last(b, pos, rows)


def _hgrn_specs(T, d_conv, n_t, rev):
    def t_of(i):
        return (n_t - 1 - i) if rev else i

    def pcol(off):
        o = off // HEAD
        return pl.BlockSpec((T, HEAD), lambda h, i, o=o: (t_of(i), h + o))

    q_spec, z_spec, v_spec, g_spec = pcol(3 * d_conv), pcol(4 * d_conv), pcol(5 * d_conv), pcol(6 * d_conv)
    lb_spec = pl.BlockSpec((1, HEAD), lambda h, i: (0, h))
    gn_spec = pl.BlockSpec((1, HEAD), lambda h, i: (0, 0))
    act_spec = pl.BlockSpec((T, HEAD), lambda h, i: (t_of(i), h))
    st_spec = pl.BlockSpec((None, T // BLK, HEAD, HEAD), lambda h, i: (h, t_of(i), 0, 0))
    return q_spec, z_spec, v_spec, g_spec, lb_spec, gn_spec, act_spec, st_spec


def _hgrn_fwd(proj, lb, gn, d_conv, job=None):
    S = proj.shape[0]
    H = d_conv // HEAD
    T = min(HGRN_ROWS, S)
    n_t, nb = S // T, T // BLK
    sub = min(HGRN_FWD_SUB, T)
    q_spec, z_spec, v_spec, g_spec, lb_spec, gn_spec, act_spec, st_spec = _hgrn_specs(T, d_conv, n_t, False)

    def body(q_ref, z_ref, v_ref, g_ref, lb_ref, gn_ref, ob_ref, o_ref, st_ref, qe_s, ke_s, dec_s, state, v_s, o_s, u_s):
        @pl.when(pl.program_id(1) == 0)
        def _():
            state[...] = jnp.zeros_like(state)

        pos = lax.broadcasted_iota(jnp.int32, (sub, HEAD), 0) % BLK
        lbv = lb_ref[...]

        def vector_pass(s, carry):
            r = pl.ds(pl.multiple_of(s * sub, sub), sub)
            v = v_ref[r, :]
            _, qs, _, _, kk, b, b_tot = _hgrn_decays(q_ref[r, :], z_ref[r, :], lbv, pos, sub)
            qe_s[r, :] = (qs * jnp.exp(b)).astype(BF16)
            ke_s[r, :] = (kk * jnp.exp(b_tot - b)).astype(BF16)
            v_s[r, :] = v.astype(BF16)
            dec_s[r, :] = jnp.exp(b_tot)
            o = jnp.sum(qs * kk, axis=-1, keepdims=True) * v
            for d in range(1, BLK):
                e = jnp.exp(b - pltpu.roll(b, d, 0))
                a = jnp.sum(qs * pltpu.roll(kk, d, 0) * e, axis=-1, keepdims=True)
                o = o + jnp.where(pos >= d, a * pltpu.roll(v, d, 0), 0.0)
            o_s[r, :] = o
            return carry

        lax.fori_loop(0, T // sub, vector_pass, 0)

        for j in range(nb):
            rows = pl.ds(j * BLK, BLK)
            u_s[j] = lax.dot_general(v_s[rows, :], ke_s[rows, :], (((0,), (0,)), ((), ())),
                                     preferred_element_type=F32)
        st = state[...]
        for j in range(nb):
            st_ref[j] = st.astype(BF16)
            st = st * dec_s[pl.ds(j * BLK, 1), :] + u_s[j]
        state[...] = st
        for j in range(nb):
            rows = pl.ds(j * BLK, BLK)
            o_s[rows, :] += lax.dot_general(qe_s[rows, :], st_ref[j], (((1,), (1,)), ((), ())),
                                            preferred_element_type=F32)
        o = o_s[...]
        o_ref[...] = o
        r = lax.rsqrt(jnp.mean(o * o, axis=-1, keepdims=True) + EPS)
        g = g_ref[...]
        ob_ref[...] = ((o * r) * gn_ref[...] * (g * _sig(g))).astype(BF16)

    grid = (H, n_t)
    in_specs = [q_spec, z_spec, v_spec, g_spec, lb_spec, gn_spec]
    args = [proj, proj, proj, proj, lb, gn]
    out_specs = [act_spec, act_spec, st_spec, act_spec, act_spec, act_spec]
    out_shape = [jax.ShapeDtypeStruct((S, d_conv), BF16), jax.ShapeDtypeStruct((S, d_conv), F32),
                 jax.ShapeDtypeStruct((H, S // BLK, HEAD, HEAD), BF16),
                 jax.ShapeDtypeStruct((S, d_conv), BF16), jax.ShapeDtypeStruct((S, d_conv), BF16),
                 jax.ShapeDtypeStruct((S, d_conv), F32)]
    scratch = [pltpu.VMEM((HEAD, HEAD), F32), pltpu.VMEM((T, HEAD), BF16), pltpu.VMEM((T, HEAD), F32),
               pltpu.VMEM((nb, HEAD, HEAD), F32)]
    aliases, wrap = _host(job, grid, in_specs, args, out_specs, out_shape, scratch)
    return pl.pallas_call(
        wrap(body), name="hgrn_fwd", grid=grid, in_specs=in_specs, out_specs=out_specs, out_shape=out_shape,
        scratch_shapes=scratch, input_output_aliases=aliases,
        compiler_params=_params(("parallel" if job is None else "arbitrary", "arbitrary")),
    )(*args)


def _hgrn_bwd(dob, o_raw, states, qe, ke, dec, proj, lb, gn, d_conv, job=None):
    S = proj.shape[0]
    H = d_conv // HEAD
    T = min(HGRN_ROWS, S)
    n_t, nb = S // T, T // BLK
    sub = min(HGRN_BWD_SUB, T)
    gate_rows = min(HGRN_GATE_ROWS, T)
    q_spec, z_spec, v_spec, g_spec, lb_spec, gn_spec, act_spec, st_spec = _hgrn_specs(T, d_conv, n_t, True)

    def body(dob_ref, o_ref, st_ref, qe_b, ke_b, dec_s, q_ref, z_ref, v_ref, g_ref, lb_ref, gn_ref,
             dq_ref, dz_ref, dv_ref, dg_ref, dlb_ref, dgn_ref,
             dstate, do_b, v_b, dqe_s, dke_s, dvi_s, ddec_s, do_s, u_s, ds_s):
        @pl.when(pl.program_id(1) == 0)
        def _():
            dstate[...] = jnp.zeros_like(dstate)
            dlb_ref[...] = jnp.zeros_like(dlb_ref)
            dgn_ref[...] = jnp.zeros_like(dgn_ref)

        pos = lax.broadcasted_iota(jnp.int32, (sub, HEAD), 0) % BLK
        lbv, gnv = lb_ref[...], gn_ref[...]

        def before(s, carry):
            r = pl.ds(pl.multiple_of(s * gate_rows, gate_rows), gate_rows)
            g = g_ref[r, :]
            v_b[r, :] = v_ref[r, :].astype(BF16)
            o = o_ref[r, :]
            rs = lax.rsqrt(jnp.mean(o * o, axis=-1, keepdims=True) + EPS)
            on = o * rs
            sgg = _sig(g)
            dobv = dob_ref[r, :]
            dog = dobv * (g * sgg)
            dgn_ref[...] += jnp.sum(dog * on, axis=0, keepdims=True)
            don = dog * gnv
            do = rs * (don - on * jnp.mean(don * on, axis=-1, keepdims=True))
            dg_ref[r, :] = (dobv * (on * gnv) * (sgg * (1.0 + g * (1.0 - sgg)))).astype(BF16)
            do_s[r, :] = do
            do_b[r, :] = do.astype(BF16)
            return carry

        lax.fori_loop(0, T // gate_rows, before, 0)

        for j in range(nb):
            rows = pl.ds(j * BLK, BLK)
            dob_j = do_b[rows, :]
            u_s[j] = lax.dot_general(dob_j, qe_b[rows, :], (((0,), (0,)), ((), ())), preferred_element_type=F32)
            dqe_s[rows, :] = lax.dot_general(dob_j, st_ref[j], (((1,), (0,)), ((), ())), preferred_element_type=F32)
        dst = dstate[...]
        for j in reversed(range(nb)):
            ds_s[j] = dst.astype(BF16)
            ddec = jnp.sum(st_ref[j].astype(F32) * dst, axis=0, keepdims=True)
            ddec_s[pl.ds(j * BLK, BLK), :] = jnp.broadcast_to(ddec, (BLK, HEAD))
            dst = dst * dec_s[pl.ds(j * BLK, 1), :] + u_s[j]
        dstate[...] = dst
        for j in range(nb):
            rows = pl.ds(j * BLK, BLK)
            dke_s[rows, :] = lax.dot_general(v_b[rows, :], ds_s[j], (((1,), (0,)), ((), ())), preferred_element_type=F32)
            dvi_s[rows, :] = lax.dot_general(ke_b[rows, :], ds_s[j], (((1,), (1,)), ((), ())), preferred_element_type=F32)

        def after(s, carry):
            r = pl.ds(pl.multiple_of(s * sub, sub), sub)
            q, v = q_ref[r, :], v_ref[r, :]
            sgq, qs, sg, f, kk, b, b_tot = _hgrn_decays(q, z_ref[r, :], lbv, pos, sub)
            do = do_s[r, :]
            dqs = dqe_s[r, :] * jnp.exp(b)
            dkk = dke_s[r, :] * jnp.exp(b_tot - b)
            d_tot = jnp.where(pos == BLK - 1, _block_cumsum(dkk * kk, pos) + ddec_s[r, :] * jnp.exp(b_tot), 0.0)
            dv = dvi_s[r, :]
            da = jnp.sum(do * v, axis=-1, keepdims=True)
            dv = dv + jnp.sum(qs * kk, axis=-1, keepdims=True) * do
            dqs = dqs + da * kk
            dkk = dkk + da * qs
            for d in range(1, BLK):
                kd = pltpu.roll(kk, d, 0)
                e = jnp.where(pos >= d, jnp.exp(b - pltpu.roll(b, d, 0)), 0.0)
                a = jnp.sum(qs * kd * e, axis=-1, keepdims=True)
                da = jnp.sum(do * pltpu.roll(v, d, 0), axis=-1, keepdims=True)
                gq = da * e
                dqs = dqs + gq * kd
                dkk = dkk + pltpu.roll(gq * qs, sub - d, 0)
                dv = dv + pltpu.roll(a * do, sub - d, 0)
            db = qs * dqs - kk * dkk + d_tot
            dlf = _block_rev_cumsum(db, pos, sub)
            df = dlf / f - dkk
            dlb_ref[...] += jnp.sum(df * (1.0 - sg), axis=0, keepdims=True)
            dz_ref[r, :] = (df * (1.0 - lbv) * sg * (1.0 - sg)).astype(BF16)
            dq_ref[r, :] = (dqs * (sgq * (1.0 + q * (1.0 - sgq)))).astype(BF16)
            dv_ref[r, :] = dv.astype(BF16)
            return carry

        lax.fori_loop(0, T // sub, after, 0)

    tb = lambda dt: pltpu.VMEM((T, HEAD), dt)
    grid = (H, n_t)
    in_specs = [act_spec, act_spec, st_spec, act_spec, act_spec, act_spec, q_spec, z_spec, v_spec, g_spec, lb_spec,
                gn_spec]
    args = [dob, o_raw, states, qe, ke, dec, proj, proj, proj, proj, lb, gn]
    out_specs = [act_spec, act_spec, act_spec, act_spec, lb_spec, pl.BlockSpec((None, 1, HEAD), lambda h, i: (h, 0, 0))]
    out_shape = ([jax.ShapeDtypeStruct((S, d_conv), BF16)] * 4
                 + [jax.ShapeDtypeStruct((1, d_conv), F32), jax.ShapeDtypeStruct((H, 1, HEAD), F32)])
    scratch = [pltpu.VMEM((HEAD, HEAD), F32), tb(BF16), tb(BF16), tb(F32), tb(F32), tb(F32), tb(F32), tb(F32),
               pltpu.VMEM((nb, HEAD, HEAD), F32), pltpu.VMEM((nb, HEAD, HEAD), BF16)]
    aliases, wrap = _host(job, grid, in_specs, args, out_specs, out_shape, scratch)
    return pl.pallas_call(
        wrap(body), name="hgrn_bwd", grid=grid, in_specs=in_specs, out_specs=out_specs, out_shape=out_shape,
        scratch_shapes=scratch, input_output_aliases=aliases,
        compiler_params=_params(("parallel" if job is None else "arbitrary", "arbitrary")),
    )(*args)


def _local_step(x, target, mod, norm_mix_g, lb, gnorm_g, norm_ffn_g, norm_final_g, conv_w, bufs, c_idx, order):
    S, D = x.shape
    d_conv = D // 2
    d_in = 7 * d_conv + 2 * D
    d_ff = N_CHIPS * bufs["w_ffn_down"].shape[1]
    nb_in, nb_co, nb_ff = bufs["w_in"].shape[2], bufs["w_conv_out"].shape[2], bufs["w_ffn_gate"].shape[2]
    rows = lambda g: g.reshape(-1, g.shape[2])
    sh_m, sc_m, gt_m, sh_f, sc_f, gt_f = [mod[:, i * D:(i + 1) * D] for i in range(6)]
    tm = _pick(S, 512, 16)
    tm2 = _pick(S, 1024, 16)
    tn_in = _pick(nb_in, 1408, 128)
    tn_co = _pick(nb_co, 512, 128)
    tn_ff = _pick(nb_ff, 1408, 128)
    tn_d = _pick(D, 512, 128)
    tw = _pick(D, 1024, 128)
    tg = _pick(D, 512, 128)

    h = _norm_mod("norm_mix", x, norm_mix_g, sc_m, sh_m)
    proj, w_in = _proj_gather(h, bufs["w_in"], order)
    ob, o_raw, states, qe, ke, dec, *got = _hgrn_fwd(
        proj, lb, gnorm_g, d_conv,
        job=_GatherJob([bufs[k] for k in ("w_conv_out", "w_hgrn_out", "w_o", "w_ffn_gate")]))
    ya, w_conv_out, w_hgrn_out, w_o, w_ffn_gate = _conv_fwd(proj, conv_w, d_conv, job=_ForwardJob(got))
    w_o = rows(w_o)

    def merge_epi(accs, ex):
        y_a, y_b = accs
        return [y_a, y_b, _sig(ex[0]) * y_a + _sig(ex[1]) * y_b]

    y_a, y_b, merged, w_ffn_up = _matmul(
        "branch_out", [(ya, 'n', w_conv_out, 'n'), (ob, 'n', w_hgrn_out, 'n')], [0, 1], S, D, d_conv, tm2, tn_co, d_conv,
        [(proj, 'tile', 7 * d_conv), (proj, 'tile', 7 * d_conv + D)], [(BF16, None), (BF16, None), (BF16, None)], merge_epi,
        job=_GatherJob([bufs["w_ffn_up"]], 0, 2), rows_outer=True)

    def resid_epi(accs, ex):
        return [accs[0], ex[0] + ex[1] * accs[0]]

    mo, x1, w_ffn_up = _matmul("w_o", [(merged, 'n', w_o, 'n')], [0], S, D, D, tm2, tw, D,
                               [(x, 'tile', 0), (gt_m, 'row', 0)], [(BF16, None), (F32, None)], resid_epi,
                               job=_GatherJob([w_ffn_up], 1, 2))
    h2, w_ffn_up = _norm_mod("norm_ffn", x1, norm_ffn_g, sc_f, sh_f, job=_ForwardJob([w_ffn_up]))

    def swiglu_epi(accs, ex):
        gg, uu = accs
        return [gg, uu, gg * _sig(gg) * uu]

    G, U, act, w_ffn_down = _matmul(
        "ffn_in", [(h2, 'n', w_ffn_gate, 'n'), (h2, 'n', w_ffn_up, 'n')], [0, 1], S, d_ff, D,
        tm2, tn_ff, D, [], [(BF16, None), (BF16, None), (BF16, None)], swiglu_epi, job=_GatherJob([bufs["w_ffn_down"]]))
    w_ffn_down = rows(_forward_halves("forward_down", [w_ffn_down])[0])
    ff, x2 = _matmul("ffn_out", [(act, 'n', w_ffn_down, 'n')], [0], S, D, d_ff, tm2, tn_d, d_ff,
                     [(x1, 'tile', 0), (gt_f, 'row', 0)], [(BF16, None), (F32, None)], resid_epi, rows_outer=True)

    dx2, dffb, loss, dgt_f, dg_final = _loss_head(x2, target, norm_final_g, ff, gt_f)

    def swiglu_bwd_epi(accs, ex):
        dact, gg, uu = accs[0], ex[0], ex[1]
        s = _sig(gg)
        return [dact * uu * (s * (1.0 + gg * (1.0 - s))), dact * (gg * s)]

    dG, dU = _matmul("d_act", [(dffb, 'n', w_ffn_down, 't')], [0], S, d_ff, D, tm2, tn_ff, D,
                     [(G, 'tile', 0), (U, 'tile', 0)], [(BF16, None), (BF16, None)], swiglu_bwd_epi)
    dw_down, = _matmul("dw_ffn_down", [(act, 't', dffb, 'n')], [0], d_ff, D, S, _pick(d_ff, 512, 128),
                       D, S, [], [(BF16, None)], _plain)
    dh2, = _matmul("d_h2", [(dG, 'n', w_ffn_gate, 't'), (dU, 'n', w_ffn_up, 't')], [0, 0], S, D, d_ff,
                   tm, D, tn_ff, [], [(F32, None)], _plain)
    dw_gate, = _matmul("dw_ffn_gate", [(h2, 't', dG, 'n')], [0], D, d_ff, S, tg, tn_ff, S, [], [(BF16, nb_ff)], _plain)
    dw_up, = _matmul("dw_ffn_up", [(h2, 't', dU, 'n')], [0], D, d_ff, S, tg, tn_ff, S, [], [(BF16, nb_ff)], _plain)

    def pair_sums(names, grads, from_sibling):
        pairs = [_pair_sum("pair_sum_" + k, g, q, c_idx) for k, g, q in zip(names, grads, from_sibling)]
        return [p[0] for p in pairs], [p[1] for p in pairs]

    ffn_names = ["w_ffn_down", "w_ffn_gate", "w_ffn_up"]
    ffn_grads = [dw_down.reshape(N_CHIPS, -1, D), dw_gate, dw_up]
    dx1, dsh_f, dsc_f, dg_ffn, dmob, dgt_m, *from_sibling = _norm_mod_bwd(
        "norm_ffn_bwd", dh2, x1, norm_ffn_g, sc_f, dx2, (mo, gt_m), job=_SwapJob(ffn_grads))
    parts_f, slots_f = pair_sums(ffn_names, ffn_grads, from_sibling)

    def merge_bwd_epi(accs, ex):
        dm, ga, gb, ya_, yb_ = accs[0], ex[0], ex[1], ex[2], ex[3]
        sa, sb = _sig(ga), _sig(gb)
        return [dm * ya_ * sa * (1.0 - sa), dm * yb_ * sb * (1.0 - sb), dm * sa, dm * sb]

    dga, dgb, dy_a, dy_b = _matmul(
        "d_merged", [(dmob, 'n', w_o, 't')], [0], S, D, D, tm2, tn_co, D,
        [(proj, 'tile', 7 * d_conv), (proj, 'tile', 7 * d_conv + D), (y_a, 'tile', 0), (y_b, 'tile', 0)],
        [(BF16, None)] * 4, merge_bwd_epi, rows_outer=True)
    dw_o, = _matmul("dw_o", [(merged, 't', dmob, 'n')], [0], D, D, S, tg, D, S, [], [(BF16, None)], _plain)
    dya, dob = _matmul("d_branch", [(dy_a, 'n', w_conv_out, 't'), (dy_b, 'n', w_hgrn_out, 't')], [0, 1], S, d_conv, D,
                       tm2, _pick(d_conv, 1024, 128), tn_co, [], [(F32, None), (F32, None)], _plain)
    dw_co, dw_ho = _matmul("dw_branch", [(ya, 't', dy_a, 'n'), (ob, 't', dy_b, 'n')], [0, 1], d_conv, D, S,
                           _pick(d_conv, 512, 128), tn_co, S, [], [(BF16, nb_co), (BF16, nb_co)], _plain)
    branch_names = ["w_conv_out", "w_hgrn_out", "w_o"]
    branch_grads = [dw_co, dw_ho, dw_o.reshape(N_CHIPS, -1, D)]
    dab, dac, dax, dconv_w, *from_sibling = _conv_bwd(dya, proj, conv_w, d_conv, job=_SwapJob(branch_grads))
    parts_b, slots_b = pair_sums(branch_names, branch_grads, from_sibling)
    dq, dz, dv, dgo, dlb, dgn, *arrived_f = _hgrn_bwd(dob, o_raw, states, qe, ke, dec, proj, lb, gnorm_g, d_conv,
                                                      job=_ScatterJob(parts_f, slots_f))
    dproj = jnp.concatenate([dab, dac, dax, dq, dz, dv, dgo, dga, dgb], axis=1)
    halves_f = [_sum_chips("sum_chips_" + k, r, c_idx) for k, r in zip(ffn_names, arrived_f)]
    dw_in, *moved = _matmul("dw_in", [(h, 't', dproj, 'n')], [0], D, d_in, S, tg, tn_in, S, [], [(BF16, nb_in)], _plain,
                            job=_Jobs([_ShareJob(halves_f), _ScatterJob(parts_b, slots_b)]))
    whole_f, arrived_b = moved[:len(ffn_names)], moved[len(ffn_names):]
    parts_i, slots_i = pair_sums(["w_in"], [dw_in], _swap_halves("swap_in", [dw_in]))
    dh, arrived_in = _matmul("d_h", [(dproj, 'n', w_in, 't')], [0], S, D, d_in, tm2, D, tn_in, [], [(F32, None)], _plain,
                             job=_ScatterJob(parts_i, slots_i))
    dx, dsh_m, dsc_m, dg_mix = _norm_mod_bwd("norm_mix_bwd", dh, x, norm_mix_g, sc_m, dx1)
    dmod = jnp.concatenate([dsh_m, dsc_m, dgt_m, dsh_f, dsc_f, dgt_f], axis=1)
    small = dict(dmod=dmod, dg_mix=dg_mix, dg_ffn=dg_ffn, dg_final=dg_final, dlb=dlb,
                 dgn=jnp.sum(dgn, axis=0), dconv_w=dconv_w)
    late_names = branch_names + ["w_in"]
    late = _share_halves([_sum_chips("sum_chips_" + k, r, c_idx) for k, r in zip(late_names, arrived_b + [arrived_in])])
    big = dict(zip(ffn_names + late_names, whole_f + list(late)))
    return loss, dx, small, big


def _adamw_math(w, g, m, v):
    m = ADAM_B1 * m + (1.0 - ADAM_B1) * g
    v = ADAM_B2 * v + (1.0 - ADAM_B2) * (g * g)
    m_hat = m / (1.0 - ADAM_B1 ** ADAM_STEP)
    v_hat = v / (1.0 - ADAM_B2 ** ADAM_STEP)
    delta = -ADAM_LR * (m_hat / (jnp.sqrt(v_hat) + ADAM_EPS) + ADAM_WD * w)
    return delta, m, v


def _adamw(name, w, g, m, v):
    R, C = w.shape
    tr = _pick(R, max(8, (256 * 1024) // C // 8 * 8), 8)
    spec = pl.BlockSpec((tr, C), lambda i: (i, 0))

    def body(w_ref, g_ref, m_ref, v_ref, go_ref, d_ref, mo_ref, vo_ref):
        gv = g_ref[...]
        d, m2, v2 = _adamw_math(w_ref[...], gv, m_ref[...], v_ref[...])
        go_ref[...] = gv
        d_ref[...] = d
        mo_ref[...] = m2
        vo_ref[...] = v2

    return pl.pallas_call(
        body, name=name, grid=(R // tr,), in_specs=[spec] * 4, out_specs=[spec] * 4,
        out_shape=[jax.ShapeDtypeStruct((R, C), F32)] * 4, compiler_params=_params(("parallel",)),
    )(w, g, m, v)


def _ada_update(c_act_t, dmod, w, m, v):
    R, C = w.shape
    B = dmod.shape[0]
    tr = _pick(R, 256, 8)
    tc = _pick(C, 1536, 128)
    spec = pl.BlockSpec((tr, tc), lambda i, j: (i, j))

    def body(ct_ref, dm_ref, w_ref, m_ref, v_ref, g_ref, d_ref, mo_ref, vo_ref):
        ct = ct_ref[...]
        dm = dm_ref[...]
        g = ct[:, 0:1] * dm[0:1, :]
        for b in range(1, B):
            g = g + ct[:, b:b + 1] * dm[b:b + 1, :]
        d, m2, v2 = _adamw_math(w_ref[...], g, m_ref[...], v_ref[...])
        g_ref[...] = g
        d_ref[...] = d
        mo_ref[...] = m2
        vo_ref[...] = v2

    return pl.pallas_call(
        body, name="ada_update", grid=(R // tr, C // tc),
        in_specs=[pl.BlockSpec((tr, B), lambda i, j: (i, 0)), pl.BlockSpec((B, tc), lambda i, j: (0, j)),
                  spec, spec, spec],
        out_specs=[spec] * 4, out_shape=[jax.ShapeDtypeStruct((R, C), F32)] * 4,
        compiler_params=_params(("parallel", "parallel")),
    )(c_act_t, dmod, w, m, v)


def _prep(c_all, lb_param):
    def body(c_ref, p_ref, ca_ref, cab_ref, lb_ref):
        cv = c_ref[...]
        ca = cv * _sig(cv)
        ca_ref[...] = ca
        cab_ref[...] = ca.astype(BF16)
        lb_ref[...] = _sig(p_ref[0:1, :] - p_ref[1:2, :])

    return pl.pallas_call(
        body, name="prep",
        out_shape=[jax.ShapeDtypeStruct(c_all.shape, F32), jax.ShapeDtypeStruct(c_all.shape, BF16),
                   jax.ShapeDtypeStruct((1, lb_param.shape[1]), F32)],
    )(c_all, lb_param)


def _small_reduce(parts):
    W = parts.shape[1]

    def body(p_ref, o_ref):
        acc = p_ref[0:1, :]
        for d in range(1, N_DEV):
            acc = acc + p_ref[d:d + 1, :]
        o_ref[...] = acc

    return pl.pallas_call(body, name="small_reduce", out_shape=jax.ShapeDtypeStruct((1, W), F32))(parts)


def _lb_grad(dlb, lb):
    def body(d_ref, lb_ref, o_ref):
        t = d_ref[...] * lb_ref[...] * (1.0 - lb_ref[...])
        o_ref[0:1, :] = t
        o_ref[1:2, :] = -t

    return pl.pallas_call(body, name="lb_grad", out_shape=jax.ShapeDtypeStruct((2, dlb.shape[1]), F32))(dlb, lb)


def _place():
    x, y, c = lax.axis_index("x"), lax.axis_index("y"), lax.axis_index("c")
    chips = [(1 - x, y), (x, 1 - y), (1 - x, 1 - y)]
    return x, y, c, chips


def _allgather_rows(name, v):
    m_per, n = v.shape

    def body(x_ref, out_ref, send_sems, recv_sems, local_sem):
        x, y, c, chips = _place()
        me, sibling = (x, y, c), (x, y, 1 - c)

        def rows(px, py, pc):
            return out_ref.at[pl.ds((4 * px + 2 * py + pc) * m_per, m_per), :]

        def copy(k, block, to, src=None):
            return pltpu.make_async_remote_copy(
                src_ref=rows(*block) if src is None else src, dst_ref=rows(*block),
                send_sem=send_sems.at[k], recv_sem=recv_sems.at[k], device_id=to, device_id_type=MESH)

        mine = pltpu.make_async_copy(x_ref, rows(*me), local_sem)
        mine.start()
        first = [copy(0, me, sibling, src=x_ref)]
        first += [copy(1 + j, me, (*chip, c), src=x_ref) for j, chip in enumerate(chips)]
        for cp in first:
            cp.start()
        passed = [copy(4 + j, (*chip, c), sibling) for j, chip in enumerate(chips)]
        for j, chip in enumerate(chips):
            copy(1 + j, (*chip, c), me).wait_recv()
            passed[j].start()
        copy(0, sibling, me).wait_recv()
        for j, chip in enumerate(chips):
            copy(4 + j, (*chip, 1 - c), me).wait_recv()
        for cp in first + passed:
            cp.wait_send()
        mine.wait()

    return pl.pallas_call(
        body, name=name, out_shape=jax.ShapeDtypeStruct((N_DEV * m_per, n), v.dtype),
        in_specs=[pl.BlockSpec(memory_space=pltpu.VMEM)], out_specs=pl.BlockSpec(memory_space=pltpu.VMEM),
        scratch_shapes=[pltpu.SemaphoreType.DMA((7,)), pltpu.SemaphoreType.DMA((7,)), pltpu.SemaphoreType.DMA],
    )(v)


def _cast_place(name, w, k_idx):
    R, C = w.shape
    tr = _pick(R, max(16, (512 * 1024) // C // 16 * 16), 16)

    def body(k_ref, w_ref, o_ref):
        o_ref[...] = w_ref[...].astype(BF16)

    return pl.pallas_call(
        body, name=name,
        grid_spec=pltpu.PrefetchScalarGridSpec(
            num_scalar_prefetch=1, grid=(R // tr,),
            in_specs=[pl.BlockSpec((tr, C), lambda i, k_ref: (i, 0))],
            out_specs=pl.BlockSpec((None, tr, C), lambda i, k_ref: (k_ref[0], i, 0))),
        out_shape=jax.ShapeDtypeStruct((N_CHIPS, R, C), BF16),
        compiler_params=_params(("parallel",)),
    )(k_idx, w)


def _chip_copies(src_of, dst_of, got_of, n, sems, receiving):
    x, y, c, chips = _place()
    k_me = 2 * x + y
    send_sems, recv_sems = sems
    out = []
    for a in range(n):
        for j, chip in enumerate(chips):
            k_chip = 2 * chip[0] + chip[1]
            if receiving:
                src = dst = got_of(a, k_chip, c)
                to = (x, y, c)
            else:
                src, dst, to = src_of(a, k_chip, k_me, c), dst_of(a, k_me, c), (*chip, c)
            out.append(pltpu.make_async_remote_copy(
                src_ref=src, dst_ref=dst, send_sem=send_sems.at[3 * a + j], recv_sem=recv_sems.at[3 * a + j],
                device_id=to, device_id_type=MESH))
    return out


class _ChipJob:
    def start(self, j_in, j_out, sems):
        for send in self.copies(j_in, j_out, sems, False):
            send.start()

    def finish(self, j_in, j_out, sems):
        for recv in self.copies(j_in, j_out, sems, True):
            recv.wait_recv()
        for send in self.copies(j_in, j_out, sems, False):
            send.wait_send()


class _GatherJob(_ChipJob):
    def __init__(self, bufs, part=0, n_parts=1):
        n = len(bufs)
        self.inputs, self.n_alias = list(bufs), n
        self.sem_shapes = [pltpu.SemaphoreType.DMA((3 * n,)), pltpu.SemaphoreType.DMA((3 * n,))]
        self.half = [b.shape[1] // 2 for b in bufs]
        self.part, self.n_parts = part, n_parts

    def copies(self, j_in, j_out, sems, receiving):
        def half(a, k, c):
            rows = self.half[a] // self.n_parts
            return j_out[a].at[k, pl.ds(c * self.half[a] + self.part * rows, rows)]

        return _chip_copies(lambda a, k_chip, k_me, c: half(a, k_me, c), half, half, len(self.half), sems, receiving)


class _ScatterJob(_ChipJob):
    def __init__(self, parts, slots):
        n = len(parts)
        self.n = n
        self.inputs, self.n_alias = list(parts) + list(slots), n
        self.sem_shapes = [pltpu.SemaphoreType.DMA((3 * n,)), pltpu.SemaphoreType.DMA((3 * n,))]

    def copies(self, j_in, j_out, sems, receiving):
        return _chip_copies(lambda a, k_chip, k_me, c: j_in[a].at[k_chip], lambda a, k_me, c: j_out[a].at[k_me],
                            lambda a, k_chip, c: j_out[a].at[k_chip], self.n, sems, receiving)


class _SwapJob(_ChipJob):
    def __init__(self, grads):
        n = len(grads)
        self.n = n
        self.half = [g.shape[1] // 2 for g in grads]
        landing = [lax.empty((N_CHIPS, g.shape[1] // 2, g.shape[2]), g.dtype) for g in grads]
        self.inputs, self.n_alias = list(grads) + landing, n
        self.sem_shapes = [pltpu.SemaphoreType.DMA((n,)), pltpu.SemaphoreType.DMA((n,))]

    def copies(self, j_in, j_out, sems, receiving):
        x, y, c, _ = _place()
        out = []
        for a in range(self.n):
            if receiving:
                src, to = j_out[a], (x, y, c)
            else:
                src, to = j_in[a].at[:, pl.ds((1 - c) * self.half[a], self.half[a])], (x, y, 1 - c)
            out.append(pltpu.make_async_remote_copy(src_ref=src, dst_ref=j_out[a], send_sem=sems[0].at[a],
                                                    recv_sem=sems[1].at[a], device_id=to, device_id_type=MESH))
        return out


class _Jobs:
    def __init__(self, jobs):
        self.jobs = jobs
        split = [(j.inputs[:len(j.inputs) - j.n_alias], j.inputs[len(j.inputs) - j.n_alias:]) for j in jobs]
        self.inputs = [a for plain, _ in split for a in plain] + [a for _, aliased in split for a in aliased]
        self.n_alias = sum(j.n_alias for j in jobs)
        self.sem_shapes = [s for j in jobs for s in j.sem_shapes]

    def _each(self, j_in, j_out, sems):
        n_plain = len(j_in) - self.n_alias
        p0 = a0 = s0 = 0
        for j in self.jobs:
            n_p, n_s = len(j.inputs) - j.n_alias, len(j.sem_shapes)
            ins = list(j_in[p0:p0 + n_p]) + list(j_in[n_plain + a0:n_plain + a0 + j.n_alias])
            yield j, ins, j_out[a0:a0 + j.n_alias], sems[s0:s0 + n_s]
            p0, a0, s0 = p0 + n_p, a0 + j.n_alias, s0 + n_s

    def start(self, j_in, j_out, sems):
        for j, ins, outs, s in self._each(j_in, j_out, sems):
            j.start(ins, outs, s)

    def finish(self, j_in, j_out, sems):
        for j, ins, outs, s in self._each(j_in, j_out, sems):
            j.finish(ins, outs, s)


class _ShareJob(_ChipJob):
    def __init__(self, bufs):
        n = len(bufs)
        self.n = n
        self.inputs, self.n_alias = list(bufs), n
        self.sem_shapes = [pltpu.SemaphoreType.DMA((n,)), pltpu.SemaphoreType.DMA((n,))]

    def copies(self, j_in, j_out, sems, receiving):
        x, y, c, _ = _place()
        out = []
        for a in range(self.n):
            hc, to = (1 - c, (x, y, c)) if receiving else (c, (x, y, 1 - c))
            out.append(pltpu.make_async_remote_copy(
                src_ref=j_out[a].at[hc], dst_ref=j_out[a].at[hc], send_sem=sems[0].at[a], recv_sem=sems[1].at[a],
                device_id=to, device_id_type=MESH))
        return out


class _ForwardJob(_ChipJob):
    def __init__(self, bufs):
        n = len(bufs)
        self.n = n
        self.half = [b.shape[1] // 2 for b in bufs]
        self.inputs, self.n_alias = list(bufs), n
        self.sem_shapes = [pltpu.SemaphoreType.DMA((3 * n,)), pltpu.SemaphoreType.DMA((3 * n,))]

    def copies(self, j_in, j_out, sems, receiving):
        x, y, c, chips = _place()
        out = []
        for a in range(self.n):
            for q, chip in enumerate(chips):
                hc, to = (1 - c, (x, y, c)) if receiving else (c, (x, y, 1 - c))
                part = j_out[a].at[2 * chip[0] + chip[1], pl.ds(hc * self.half[a], self.half[a])]
                out.append(pltpu.make_async_remote_copy(
                    src_ref=part, dst_ref=part, send_sem=sems[0].at[3 * a + q], recv_sem=sems[1].at[3 * a + q],
                    device_id=to, device_id_type=MESH))
        return out


def _forward_halves(name, bufs):
    n = len(bufs)

    def body(*refs):
        out_refs = refs[n:2 * n]
        send_sems, recv_sems = refs[2 * n:]
        x, y, c, chips = _place()
        started, waits = [], []
        for a in range(n):
            rh = bufs[a].shape[1] // 2
            for j, chip in enumerate(chips):
                k_chip = 2 * chip[0] + chip[1]
                got = out_refs[a].at[k_chip, pl.ds(c * rh, rh)]
                cp = pltpu.make_async_remote_copy(src_ref=got, dst_ref=got, send_sem=send_sems.at[3 * a + j],
                                                  recv_sem=recv_sems.at[3 * a + j], device_id=(x, y, 1 - c),
                                                  device_id_type=MESH)
                cp.start()
                started.append(cp)
                other = out_refs[a].at[k_chip, pl.ds((1 - c) * rh, rh)]
                waits.append(pltpu.make_async_remote_copy(
                    src_ref=other, dst_ref=other, send_sem=send_sems.at[3 * a + j], recv_sem=recv_sems.at[3 * a + j],
                    device_id=(x, y, c), device_id_type=MESH))
        for cp in waits:
            cp.wait_recv()
        for cp in started:
            cp.wait_send()

    return pl.pallas_call(
        body, name=name, out_shape=[jax.ShapeDtypeStruct(b.shape, b.dtype) for b in bufs],
        in_specs=[_HBM] * n, out_specs=[_HBM] * n, input_output_aliases={a: a for a in range(n)},
        scratch_shapes=[pltpu.SemaphoreType.DMA((3 * n,)), pltpu.SemaphoreType.DMA((3 * n,))],
    )(*bufs)


def _proj_gather(h, buf, order):
    S, D = h.shape
    nb = buf.shape[2]
    tm = _pick(S, 1024, 16)
    tn = _pick(nb, 1408, 128)
    per = nb // tn
    nj, ni = N_CHIPS * per, S // tm
    rh = D // 2
    seq = [(0, t) for t in range(per)] + [(p, t) for t in range(per) for p in (1, 2)] + [(3, t) for t in range(per)]
    tile_chip = jnp.stack([order[p] for p, _ in seq])
    tile_of = jnp.array([t for _, t in seq], jnp.int32)

    def body(chip_ref, t_ref, h_ref, buf_in, proj_ref, buf_ref, wbuf, tile_sems, ici_send, ici_recv, d2d_send, d2d_recv):
        j, i = pl.program_id(0), pl.program_id(1)
        x, y, c, chips = _place()
        k_me = 2 * x + y

        def part(k, hc, t):
            return buf_ref.at[k, pl.ds(hc * rh, rh), pl.ds(t * tn, tn)]

        def ici(q, t, receiving):
            k_chip = 2 * chips[q][0] + chips[q][1]
            src, to = (part(k_chip, c, t), (x, y, c)) if receiving else (part(k_me, c, t), (*chips[q], c))
            return pltpu.make_async_remote_copy(src_ref=src, dst_ref=src, send_sem=ici_send.at[q * per + t],
                                                recv_sem=ici_recv.at[q * per + t], device_id=to, device_id_type=MESH)

        def d2d(q, t, receiving):
            k_chip = 2 * chips[q][0] + chips[q][1]
            src, to = (part(k_chip, 1 - c, t), (x, y, c)) if receiving else (part(k_chip, c, t), (x, y, 1 - c))
            return pltpu.make_async_remote_copy(src_ref=src, dst_ref=src, send_sem=d2d_send.at[q * per + t],
                                                recv_sem=d2d_recv.at[q * per + t], device_id=to, device_id_type=MESH)

        def tile_copy(n):
            col = pl.multiple_of(t_ref[n] * tn, 128)
            return pltpu.make_async_copy(buf_ref.at[chip_ref[n], :, pl.ds(col, tn)], wbuf.at[n % 2], tile_sems.at[n % 2])

        @pl.when(jnp.logical_and(j == 0, i == 0))
        def _():
            for t in range(per):
                ici(0, t, False).start()
                ici(1, t, False).start()
            tile_copy(0).start()

        @pl.when(i == 0)
        def _():
            tile_copy(j).wait()
            for n in range(per, nj):
                p, t = seq[n]

                @pl.when(j + 1 == n)
                def _():
                    ici(p - 1, t, True).wait_recv()
                    d2d(p - 1, t, False).start()
                    if (p, t) == (1, per - 1):
                        for q in range(2):
                            for tt in range(per):
                                ici(q, tt, False).wait_send()
                        for tt in range(per):
                            ici(2, tt, False).start()
                    d2d(p - 1, t, True).wait_recv()

            @pl.when(j + 1 < nj)
            def _():
                tile_copy(j + 1).start()

        proj_ref[...] = jnp.dot(h_ref[...], wbuf[j % 2], preferred_element_type=F32)

        @pl.when(jnp.logical_and(j == nj - 1, i == ni - 1))
        def _():
            for t in range(per):
                ici(2, t, False).wait_send()
                for q in range(3):
                    d2d(q, t, False).wait_send()

    n_sems = 3 * per
    return pl.pallas_call(
        body, name="proj",
        grid_spec=pltpu.PrefetchScalarGridSpec(
            num_scalar_prefetch=2, grid=(nj, ni),
            in_specs=[pl.BlockSpec((tm, D), lambda j, i, kc, kt: (i, 0)), _HBM],
            out_specs=[pl.BlockSpec((tm, tn), lambda j, i, kc, kt: (i, kc[j] * per + kt[j])), _HBM],
            scratch_shapes=[pltpu.VMEM((2, D, tn), BF16), pltpu.SemaphoreType.DMA((2,))]
            + [pltpu.SemaphoreType.DMA((n_sems,))] * 4),
        out_shape=[jax.ShapeDtypeStruct((S, N_CHIPS * nb), F32), jax.ShapeDtypeStruct(buf.shape, buf.dtype)],
        input_output_aliases={3: 1}, compiler_params=_params(("arbitrary", "arbitrary")),
    )(tile_chip, tile_of, h, buf)


def _swap_halves(name, grads):
    n = len(grads)

    def body(*refs):
        in_refs, out_refs = refs[:n], refs[n:2 * n]
        send_sems, recv_sems = refs[2 * n:]
        x, y, c, _ = _place()
        cps = []
        for a in range(n):
            rh = grads[a].shape[1] // 2
            cp = pltpu.make_async_remote_copy(
                src_ref=in_refs[a].at[:, pl.ds((1 - c) * rh, rh)], dst_ref=out_refs[a],
                send_sem=send_sems.at[a], recv_sem=recv_sems.at[a], device_id=(x, y, 1 - c), device_id_type=MESH)
            cp.start()
            cps.append(cp)
        for cp in cps:
            cp.wait()

    return pl.pallas_call(
        body, name=name,
        out_shape=[jax.ShapeDtypeStruct((N_CHIPS, g.shape[1] // 2, g.shape[2]), g.dtype) for g in grads],
        in_specs=[_HBM] * n, out_specs=[_HBM] * n,
        scratch_shapes=[pltpu.SemaphoreType.DMA((n,)), pltpu.SemaphoreType.DMA((n,))],
    )(*grads)


def _pair_sum(name, g, q, c_idx):
    _, R, C = g.shape
    rh = R // 2
    tr = _pick(rh, max(16, (512 * 1024) // C // 16 * 16), 16)
    nh = rh // tr

    def body(c_ref, g_ref, q_ref, o_ref, o2_ref):
        s = (g_ref[...].astype(F32) + q_ref[...].astype(F32)).astype(BF16)
        o_ref[...] = s
        o2_ref[...] = s

    out_spec = pl.BlockSpec((None, tr, C), lambda k, i, c_ref: (k, i, 0))
    return pl.pallas_call(
        body, name=name,
        grid_spec=pltpu.PrefetchScalarGridSpec(
            num_scalar_prefetch=1, grid=(N_CHIPS, nh),
            in_specs=[pl.BlockSpec((None, tr, C), lambda k, i, c_ref: (k, c_ref[0] * nh + i, 0)),
                      pl.BlockSpec((None, tr, C), lambda k, i, c_ref: (k, i, 0))],
            out_specs=[out_spec, out_spec]),
        out_shape=[jax.ShapeDtypeStruct((N_CHIPS, rh, C), BF16)] * 2,
        compiler_params=_params(("parallel", "parallel")),
    )(c_idx, g, q)


def _sum_chips(name, r, c_idx):
    _, rh, C = r.shape
    tr = _pick(rh, max(16, (256 * 1024) // C // 16 * 16), 16)

    def body(c_ref, r_ref, o_ref):
        acc = r_ref[0].astype(F32)
        for k in range(1, N_CHIPS):
            acc = acc + r_ref[k].astype(F32)
        o_ref[...] = acc

    return pl.pallas_call(
        body, name=name,
        grid_spec=pltpu.PrefetchScalarGridSpec(
            num_scalar_prefetch=1, grid=(rh // tr,),
            in_specs=[pl.BlockSpec((N_CHIPS, tr, C), lambda i, c_ref: (0, i, 0))],
            out_specs=pl.BlockSpec((None, tr, C), lambda i, c_ref: (c_ref[0], i, 0))),
        out_shape=jax.ShapeDtypeStruct((2, rh, C), F32), compiler_params=_params(("parallel",)),
    )(c_idx, r)


def _share_halves(bufs):
    n = len(bufs)

    def body(*refs):
        out_refs = refs[n:2 * n]
        send_sems, recv_sems = refs[2 * n:]
        x, y, c, _ = _place()
        cps = []
        for a in range(n):
            cp = pltpu.make_async_remote_copy(
                src_ref=out_refs[a].at[c], dst_ref=out_refs[a].at[c], send_sem=send_sems.at[a],
                recv_sem=recv_sems.at[a], device_id=(x, y, 1 - c), device_id_type=MESH)
            cp.start()
            cps.append(cp)
        for a, cp in enumerate(cps):
            got = out_refs[a].at[1 - c]
            pltpu.make_async_remote_copy(src_ref=got, dst_ref=got, send_sem=send_sems.at[a], recv_sem=recv_sems.at[a],
                                         device_id=(x, y, c), device_id_type=MESH).wait_recv()
        for cp in cps:
            cp.wait_send()

    return pl.pallas_call(
        body, name="share_halves",
        out_shape=[jax.ShapeDtypeStruct(b.shape, b.dtype) for b in bufs],
        in_specs=[_HBM] * n, out_specs=[_HBM] * n, input_output_aliases={a: a for a in range(n)},
        scratch_shapes=[pltpu.SemaphoreType.DMA((n,)), pltpu.SemaphoreType.DMA((n,))],
    )(*bufs)


_BIG = ("w_in", "w_conv_out", "w_hgrn_out", "w_o", "w_ffn_gate", "w_ffn_up", "w_ffn_down")
_WEIGHTS = ("w_ada", "b_ada", "norm_mix_g", "w_in", "conv_w", "lb_param", "gnorm_g", "w_conv_out", "w_hgrn_out",
            "w_o", "norm_ffn_g", "w_ffn_gate", "w_ffn_up", "w_ffn_down", "norm_final_g")


def _pack_rows(pieces):
    flat = jnp.concatenate([p.reshape(-1) for p in pieces])
    pad = (-flat.shape[0]) % (SUBLANES * LANES)
    return jnp.pad(flat, (0, pad)).reshape(SUBLANES, -1)


def kernel(x, c, w_ada, b_ada, norm_mix_g, w_in, conv_w, lb_param, gnorm_g, w_conv_out, w_hgrn_out, w_o, norm_ffn_g, w_ffn_gate, w_ffn_up, w_ffn_down, norm_final_g, loss_target, m_w_ada, m_b_ada, m_norm_mix_g, m_w_in, m_conv_w, m_lb_param, m_gnorm_g, m_w_conv_out, m_w_hgrn_out, m_w_o, m_norm_ffn_g, m_w_ffn_gate, m_w_ffn_up, m_w_ffn_down, m_norm_final_g, v_w_ada, v_b_ada, v_norm_mix_g, v_w_in, v_conv_w, v_lb_param, v_gnorm_g, v_w_conv_out, v_w_hgrn_out, v_w_o, v_norm_ffn_g, v_w_ffn_gate, v_w_ffn_up, v_w_ffn_down, v_norm_final_g):
    w = dict(w_ada=w_ada, b_ada=b_ada, norm_mix_g=norm_mix_g, w_in=w_in, conv_w=conv_w, lb_param=lb_param,
             gnorm_g=gnorm_g, w_conv_out=w_conv_out, w_hgrn_out=w_hgrn_out, w_o=w_o, norm_ffn_g=norm_ffn_g,
             w_ffn_gate=w_ffn_gate, w_ffn_up=w_ffn_up, w_ffn_down=w_ffn_down, norm_final_g=norm_final_g)
    m = dict(w_ada=m_w_ada, b_ada=m_b_ada, norm_mix_g=m_norm_mix_g, w_in=m_w_in, conv_w=m_conv_w, lb_param=m_lb_param,
             gnorm_g=m_gnorm_g, w_conv_out=m_w_conv_out, w_hgrn_out=m_w_hgrn_out, w_o=m_w_o, norm_ffn_g=m_norm_ffn_g,
             w_ffn_gate=m_w_ffn_gate, w_ffn_up=m_w_ffn_up, w_ffn_down=m_w_ffn_down, norm_final_g=m_norm_final_g)
    v = dict(w_ada=v_w_ada, b_ada=v_b_ada, norm_mix_g=v_norm_mix_g, w_in=v_w_in, conv_w=v_conv_w, lb_param=v_lb_param,
             gnorm_g=v_gnorm_g, w_conv_out=v_w_conv_out, w_hgrn_out=v_w_hgrn_out, w_o=v_w_o, norm_ffn_g=v_norm_ffn_g,
             w_ffn_gate=v_w_ffn_gate, w_ffn_up=v_w_ffn_up, w_ffn_down=v_w_ffn_down, norm_final_g=v_norm_final_g)
    two_d = lambda a: a.reshape((-1, a.shape[-1])) if a.ndim != 2 else a
    shapes = {k: a.shape for k, a in w.items()}
    w, m, v = ({k: two_d(a) for k, a in t.items()} for t in (w, m, v))
    w["lb_param"], m["lb_param"], v["lb_param"] = lb_param, m_lb_param, v_lb_param
    S, D = x.shape[1], x.shape[2]
    d_conv = D // 2
    ix, iy, ic = lax.axis_index("x"), lax.axis_index("y"), lax.axis_index("c")
    k_me = 2 * ix + iy
    dev = 2 * k_me + ic
    cw_shard = w["conv_w"].shape[1]
    ada_cols = w["w_ada"].shape[1]

    got = _allgather_rows("gather_cond", _pack_rows([c, w["conv_w"]])).reshape(N_DEV, -1)
    c_all = got[:, :D]
    conv_full = got[::2, D:D + 3 * cw_shard].reshape(N_CHIPS, 3, cw_shard).transpose(1, 0, 2).reshape(3, -1)
    c_act, c_act_b, lb = _prep(c_all, lb_param)
    b_cols = lax.dynamic_slice(w["b_ada"], (0, k_me * ada_cols), (1, ada_cols))
    mod_cols, = _matmul("ada_fwd", [(c_act_b, 'n', w["w_ada"].astype(BF16), 'n')], [0], N_DEV, ada_cols, D,
                        N_DEV, _pick(ada_cols, 1536, 128), D, [(b_cols, 'row', 0)], [(F32, None)],
                        lambda accs, ex: [accs[0] + ex[0]])
    mod_all = _allgather_rows("gather_mod", mod_cols).reshape(N_CHIPS, 2, N_DEV, ada_cols)[:, 0]
    mod_all = mod_all.transpose(1, 0, 2).reshape(N_DEV, -1)
    mod = lax.dynamic_slice(mod_all, (dev, 0), (1, mod_all.shape[1]))
    k_idx = jnp.reshape(k_me, (1,)).astype(jnp.int32)
    c_idx = jnp.reshape(ic, (1,)).astype(jnp.int32)
    bufs = {k: _cast_place("cast_" + k, w[k], k_idx) for k in _BIG}
    order = jnp.stack([k_me, 2 * (1 - ix) + iy, 2 * ix + (1 - iy), 2 * (1 - ix) + (1 - iy)]).astype(jnp.int32)

    loss, dx, small, arrived = _local_step(
        x[0], loss_target[0], mod, w["norm_mix_g"], lb, w["gnorm_g"], w["norm_ffn_g"], w["norm_final_g"], conv_full,
        bufs, c_idx, order)

    pieces = [small["dmod"], small["dg_mix"], small["dg_ffn"], small["dg_final"], small["dlb"], small["dgn"],
              small["dconv_w"], loss]
    sizes = [p.size for p in pieces]
    parts = _allgather_rows("gather_small", _pack_rows(pieces)).reshape(N_DEV, -1)
    total = _small_reduce(parts)
    offs = [0]
    for s in sizes:
        offs.append(offs[-1] + s)
    tot = [total[:, offs[i]:offs[i + 1]] for i in range(len(sizes))]
    dmod_all = parts[:, :sizes[0]]
    grads = {
        "b_ada": tot[0], "norm_mix_g": tot[1], "norm_ffn_g": tot[2], "norm_final_g": tot[3],
        "lb_param": _lb_grad(tot[4], lb), "gnorm_g": tot[5],
        "conv_w": lax.dynamic_slice(tot[6].reshape(3, -1), (0, k_me * cw_shard), (3, cw_shard)),
    }
    loss_out = tot[7][0, 0]

    for k in _BIG:
        grads[k] = arrived[k].reshape(-1, arrived[k].shape[-1])

    delta, new_m, new_v = {}, {}, {}
    dmod_cols = lax.dynamic_slice(dmod_all, (0, k_me * ada_cols), (N_DEV, ada_cols))
    grads["w_ada"], delta["w_ada"], new_m["w_ada"], new_v["w_ada"] = _ada_update(
        c_act.T, dmod_cols, w["w_ada"], m["w_ada"], v["w_ada"])
    for k in _WEIGHTS:
        if k != "w_ada":
            grads[k], delta[k], new_m[k], new_v[k] = _adamw("adamw_" + k, w[k], grads[k], m[k], v[k])
    outs = [loss_out, dx.reshape(x.shape)]
    for t in (grads, delta, new_m, new_v):
        outs += [t[k].reshape(shapes[k]) for k in _WEIGHTS]
    return tuple(outs)
```

```python
import functools

import jax
import jax.numpy as jnp
from jax import lax
from jax.experimental import pallas as pl
from jax.experimental.pallas import tpu as pltpu

F32 = jnp.float32
BF16 = jnp.bfloat16
MESH = pl.DeviceIdType.MESH

EPS = 1e-6
HEAD = 128
BLK = 16
N_CHIPS = 4
N_DEV = 8
SUBLANES, LANES = 8, 128
VMEM_LIMIT_BYTES = 56 * 1024 * 1024

ADAM_LR = 0.001
ADAM_B1 = 0.9
ADAM_B2 = 0.999
ADAM_EPS = 1e-08
ADAM_WD = 0.01
ADAM_STEP = 10


def _pick(dim, pref, mult):
    if dim <= pref:
        return dim
    t = (pref // mult) * mult
    while t >= mult:
        if dim % t == 0:
            return t
        t -= mult
    return dim


def _sig(x):
    return 1.0 / (1.0 + jnp.exp(-x))


def _params(sem):
    return pltpu.CompilerParams(dimension_semantics=sem, vmem_limit_bytes=VMEM_LIMIT_BYTES)


def _gj(j, i, k):
    return j


def _gk(j, i, k):
    return k


def _wspec(arr, rt, ct, r_of, c_of):
    if arr.ndim == 2:
        return pl.BlockSpec((rt, ct), lambda j, i, k: (r_of(j, i, k), c_of(j, i, k)))
    per = arr.shape[2] // ct
    assert arr.shape[2] % ct == 0
    return pl.BlockSpec((None, rt, ct),
                        lambda j, i, k: (c_of(j, i, k) // per, r_of(j, i, k), c_of(j, i, k) % per))


_HBM = pl.BlockSpec(memory_space=pltpu.HBM)


def _host(job, grid, in_specs, args, out_specs, out_shape, scratch):
    if job is None:
        return {}, (lambda body: body)
    n_in, n_out, n_scr = len(args), len(out_shape), len(scratch)
    n_job, n_alias = len(job.inputs), job.n_alias
    in_specs += [_HBM] * n_job
    args += job.inputs
    out_specs += [_HBM] * n_alias
    out_shape += [jax.ShapeDtypeStruct(a.shape, a.dtype) for a in job.inputs[n_job - n_alias:]]
    scratch += job.sem_shapes
    aliases = {n_in + n_job - n_alias + t: n_out + t for t in range(n_alias)}

    def wrap(body):
        def hosted(*refs):
            ins, refs = refs[:n_in], refs[n_in:]
            j_in, refs = refs[:n_job], refs[n_job:]
            outs, refs = refs[:n_out], refs[n_out:]
            j_out, refs = refs[:n_alias], refs[n_alias:]
            scr, sems = refs[:n_scr], refs[n_scr:]
            first = functools.reduce(jnp.logical_and, [pl.program_id(d) == 0 for d in range(len(grid))])
            last = functools.reduce(jnp.logical_and, [pl.program_id(d) == grid[d] - 1 for d in range(len(grid))])

            @pl.when(first)
            def _():
                job.start(j_in, j_out, sems)

            body(*ins, *outs, *scr)

            @pl.when(last)
            def _():
                job.finish(j_in, j_out, sems)

        return hosted

    return aliases, wrap


EPILOGUE_COLS = 768


def _plain(accs, extras):
    return accs


def _matmul(name, pairs, acc_of, M, N, K, tm, tn, tk, extras, outs, epilogue, job=None, rows_outer=False):
    assert M % tm == 0 and N % tn == 0 and K % tk == 0, (name, M, N, K, tm, tn, tk)
    nj, ni, nk = N // tn, M // tm, K // tk
    n_pairs, n_extra, n_out = len(pairs), len(extras), len(outs)
    n_acc = max(acc_of) + 1
    in_specs, args, dims = [], [], []
    lhs_of, seen = [], {}
    for a, am, b, bm in pairs:
        if (id(a), am) not in seen:
            seen[id(a), am] = len(args)
            args.append(a)
            if am == 'n':
                in_specs.append(pl.BlockSpec((tm, tk), lambda j, i, k: (i, k)))
            else:
                in_specs.append(pl.BlockSpec((tk, tm), lambda j, i, k: (k, i)))
        lhs_of.append(seen[id(a), am])
    n_lhs = len(args)
    for a, am, b, bm in pairs:
        if bm == 'n':
            in_specs.append(_wspec(b, tk, tn, _gk, _gj))
        else:
            in_specs.append(_wspec(b, tn, tk, _gj, _gk))
        dims.append((((1 if am == 'n' else 0,), (0 if bm == 'n' else 1,)), ((), ())))
        args.append(b)
    for e, kind, off in extras:
        assert off % tn == 0, (name, off, tn)
        o = off // tn
        if kind == 'tile':
            in_specs.append(pl.BlockSpec((tm, tn), lambda j, i, k, o=o: (i, j + o)))
        else:
            in_specs.append(pl.BlockSpec((1, tn), lambda j, i, k, o=o: (0, j + o)))
        args.append(e)
    out_shape, out_specs = [], []
    for dt, nb in outs:
        if nb is None:
            out_shape.append(jax.ShapeDtypeStruct((M, N), dt))
            out_specs.append(pl.BlockSpec((tm, tn), lambda j, i, k: (i, j)))
        else:
            assert nb % tn == 0 and N % nb == 0
            per = nb // tn
            out_shape.append(jax.ShapeDtypeStruct((N // nb, M, nb), dt))
            out_specs.append(pl.BlockSpec((None, tm, tn), lambda j, i, k, per=per: (j // per, i, j % per)))

    def body(*refs):
        n_ab = n_lhs + n_pairs
        e_refs = refs[n_ab:n_ab + n_extra]
        o_refs = refs[n_ab + n_extra:n_ab + n_extra + n_out]
        acc_refs = refs[n_ab + n_extra + n_out:]
        def products(cols):
            sums = [None] * n_acc
            for p in range(n_pairs):
                b_ref = refs[n_lhs + p]
                b = b_ref[:, cols] if pairs[p][3] == 'n' else b_ref[cols, :]
                prod = lax.dot_general(refs[lhs_of[p]][...], b, dims[p], preferred_element_type=F32)
                a = acc_of[p]
                sums[a] = prod if sums[a] is None else sums[a] + prod
            return sums

        def finish(accs, cols):
            res = epilogue(accs, [e[:, cols] for e in e_refs])
            for o_ref, r in zip(o_refs, res):
                o_ref[:, cols] = r.astype(o_ref.dtype)

        if nk == 1 and epilogue is not _plain:
            for c0 in range(0, tn, EPILOGUE_COLS):
                cols = slice(c0, min(c0 + EPILOGUE_COLS, tn))
                finish(products(cols), cols)
            return
        whole = slice(0, tn)
        sums = products(whole)
        if nk == 1:
            finish(sums, whole)
        else:
            k = pl.program_id(2)
            targets = o_refs if sum_in_outs else acc_refs

            @pl.when(k == 0)
            def _():
                for a in range(n_acc):
                    targets[a][...] = sums[a]

            @pl.when(k > 0)
            def _():
                for a in range(n_acc):
                    targets[a][...] += sums[a]

            if not sum_in_outs:
                @pl.when(k == nk - 1)
                def _():
                    finish([acc_refs[a][...] for a in range(n_acc)], whole)

    sum_in_outs = epilogue is _plain and nk > 1 and n_out == n_acc and all(dt == F32 for dt, _ in outs)
    scratch = [pltpu.VMEM((tm, tn), F32) for _ in range(n_acc)] if nk > 1 and not sum_in_outs else []
    grid = (nj, ni, nk)
    if rows_outer:
        grid = (ni, nj, nk)
        swap = lambda spec: pl.BlockSpec(spec.block_shape, lambda i, j, k, f=spec.index_map: f(j, i, k))
        in_specs, out_specs = [swap(s) for s in in_specs], [swap(s) for s in out_specs]
    aliases, wrap = _host(job, grid, in_specs, args, out_specs, out_shape, scratch)
    sem = ("parallel", "parallel", "arbitrary") if job is None else ("arbitrary",) * 3
    return pl.pallas_call(
        wrap(body), name=name, grid=grid, in_specs=in_specs, out_specs=out_specs, out_shape=out_shape,
        scratch_shapes=scratch, input_output_aliases=aliases, compiler_params=_params(sem),
    )(*args)


def _row_spec(tr, d):
    return pl.BlockSpec((tr, d), lambda i: (i, 0))


def _vec_spec(d):
    return pl.BlockSpec((1, d), lambda i: (0, 0))


def _norm_mod(name, x, g, sc, sh, job=None):
    S, D = x.shape
    tr = _pick(S, 256, 8)

    def body(x_ref, g_ref, sc_ref, sh_ref, h_ref):
        xv = x_ref[...]
        r = lax.rsqrt(jnp.mean(xv * xv, axis=-1, keepdims=True) + EPS)
        h_ref[...] = ((xv * r) * g_ref[...] * (1.0 + sc_ref[...]) + sh_ref[...]).astype(BF16)

    grid = (S // tr,)
    in_specs = [_row_spec(tr, D), _vec_spec(D), _vec_spec(D), _vec_spec(D)]
    args = [x, g, sc, sh]
    out_specs, out_shape, scratch = [_row_spec(tr, D)], [jax.ShapeDtypeStruct((S, D), BF16)], []
    aliases, wrap = _host(job, grid, in_specs, args, out_specs, out_shape, scratch)
    res = pl.pallas_call(
        wrap(body), name=name, grid=grid, in_specs=in_specs, out_specs=out_specs, out_shape=out_shape,
        scratch_shapes=scratch, input_output_aliases=aliases,
        compiler_params=_params(("parallel" if job is None else "arbitrary",)),
    )(*args)
    return res[0] if job is None else res


def _loss_head(x2, target, g, ff, gt):
    S, D = x2.shape
    tr = _pick(S, 256, 8)

    def body(x_ref, t_ref, g_ref, ff_ref, gt_ref, dx_ref, dffb_ref, loss_ref, dgt_ref, dg_ref):
        i = pl.program_id(0)

        @pl.when(i == 0)
        def _():
            loss_ref[...] = jnp.zeros_like(loss_ref)
            dgt_ref[...] = jnp.zeros_like(dgt_ref)
            dg_ref[...] = jnp.zeros_like(dg_ref)

        xv = x_ref[...]
        gv = g_ref[...]
        r = lax.rsqrt(jnp.mean(xv * xv, axis=-1, keepdims=True) + EPS)
        xh = xv * r
        err = xh * gv - t_ref[...]
        loss_ref[...] += 0.5 * jnp.sum(jnp.mean(err * err, axis=-1, keepdims=True))
        dy = err * (1.0 / D)
        dg_ref[...] += jnp.sum(dy * xh, axis=0, keepdims=True)
        dxh = dy * gv
        dx = r * (dxh - xh * jnp.mean(dxh * xh, axis=-1, keepdims=True))
        dx_ref[...] = dx
        dffb_ref[...] = (dx * gt_ref[...]).astype(BF16)
        dgt_ref[...] += jnp.sum(dx * ff_ref[...], axis=0, keepdims=True)

    return pl.pallas_call(
        body, name="loss_head", grid=(S // tr,),
        in_specs=[_row_spec(tr, D), _row_spec(tr, D), _vec_spec(D), _row_spec(tr, D), _vec_spec(D)],
        out_specs=[_row_spec(tr, D), _row_spec(tr, D), pl.BlockSpec((1, 128), lambda i: (0, 0)),
                   _vec_spec(D), _vec_spec(D)],
        out_shape=[jax.ShapeDtypeStruct((S, D), F32), jax.ShapeDtypeStruct((S, D), BF16),
                   jax.ShapeDtypeStruct((1, 128), F32), jax.ShapeDtypeStruct((1, D), F32),
                   jax.ShapeDtypeStruct((1, D), F32)],
        compiler_params=_params(("arbitrary",)),
    )(x2, target, g, ff, gt)


def _norm_mod_bwd(name, dh, x, g, sc, dres, gated=None, job=None):
    S, D = x.shape
    tr = _pick(S, 256, 8)
    n = S // tr
    with_gate = gated is not None

    def body(*refs):
        if with_gate:
            dh_ref, x_ref, g_ref, sc_ref, dres_ref, mo_ref, gt_ref, dx_ref, dsh_ref, dsc_ref, dg_ref, dmob_ref, dgt_ref = refs
        else:
            dh_ref, x_ref, g_ref, sc_ref, dres_ref, dx_ref, dsh_ref, dsc_ref, dg_ref = refs
        i = pl.program_id(0)

        @pl.when(i == 0)
        def _():
            dsh_ref[...] = jnp.zeros_like(dsh_ref)
            dsc_ref[...] = jnp.zeros_like(dsc_ref)
            if with_gate:
                dgt_ref[...] = jnp.zeros_like(dgt_ref)

        xv = x_ref[...]
        dhv = dh_ref[...]
        gv = g_ref[...]
        scale = 1.0 + sc_ref[...]
        r = lax.rsqrt(jnp.mean(xv * xv, axis=-1, keepdims=True) + EPS)
        xh = xv * r
        dsh_ref[...] += jnp.sum(dhv, axis=0, keepdims=True)
        dsc_ref[...] += jnp.sum(dhv * xh, axis=0, keepdims=True)
        dxh = dhv * (gv * scale)
        dx = dres_ref[...] + r * (dxh - xh * jnp.mean(dxh * xh, axis=-1, keepdims=True))
        dx_ref[...] = dx
        if with_gate:
            dmob_ref[...] = (dx * gt_ref[...]).astype(BF16)
            dgt_ref[...] += jnp.sum(dx * mo_ref[...], axis=0, keepdims=True)

        @pl.when(i == n - 1)
        def _():
            t = dsc_ref[...]
            dg_ref[...] = t * scale
            dsc_ref[...] = t * gv

    in_specs = [_row_spec(tr, D), _row_spec(tr, D), _vec_spec(D), _vec_spec(D), _row_spec(tr, D)]
    args = [dh, x, g, sc, dres]
    out_specs = [_row_spec(tr, D), _vec_spec(D), _vec_spec(D), _vec_spec(D)]
    out_shape = [jax.ShapeDtypeStruct((S, D), F32)] + [jax.ShapeDtypeStruct((1, D), F32)] * 3
    if with_gate:
        in_specs += [_row_spec(tr, D), _vec_spec(D)]
        args += list(gated)
        out_specs += [_row_spec(tr, D), _vec_spec(D)]
        out_shape += [jax.ShapeDtypeStruct((S, D), BF16), jax.ShapeDtypeStruct((1, D), F32)]
    scratch = []
    aliases, wrap = _host(job, (n,), in_specs, args, out_specs, out_shape, scratch)
    return pl.pallas_call(
        wrap(body), name=name, grid=(n,), in_specs=in_specs, out_specs=out_specs, out_shape=out_shape,
        scratch_shapes=scratch, input_output_aliases=aliases, compiler_params=_params(("arbitrary",)),
    )(*args)


CONV_ROWS = 256


def _col_spec(S, off_cols):
    o = off_cols // HEAD
    return pl.BlockSpec((S, HEAD), lambda c, o=o: (0, c + o))


def _conv_taps(u, u_prev, rid):
    s1 = jnp.where(rid >= 1, pltpu.roll(u, 1, 0), pltpu.roll(u_prev, 1, 0))
    s2 = jnp.where(rid >= 2, pltpu.roll(u, 2, 0), pltpu.roll(u_prev, 2, 0))
    return s1, s2


def _conv_fwd(proj, conv_w, d_conv, job=None):
    S = proj.shape[0]
    R = min(CONV_ROWS, S)
    nr = S // R

    def body(ab_ref, ac_ref, ax_ref, w_ref, ya_ref):
        w0, w1, w2 = w_ref[0:1, :], w_ref[1:2, :], w_ref[2:3, :]
        rid = lax.broadcasted_iota(jnp.int32, (R, HEAD), 0)

        def step(c, carry):
            rows = pl.ds(pl.multiple_of(c * R, R), R)
            prev = pl.ds(pl.multiple_of(jnp.maximum(c - 1, 0) * R, R), R)
            u = ac_ref[rows, :] * ax_ref[rows, :]
            u_prev = jnp.where(c > 0, ac_ref[prev, :] * ax_ref[prev, :], 0.0)
            s1, s2 = _conv_taps(u, u_prev, rid)
            y = w0 * s2 + w1 * s1 + w2 * u
            ya_ref[rows, :] = (ab_ref[rows, :] * y).astype(BF16)
            return carry

        lax.fori_loop(0, nr, step, 0)

    grid = (d_conv // HEAD,)
    in_specs = [_col_spec(S, 0), _col_spec(S, d_conv), _col_spec(S, 2 * d_conv), pl.BlockSpec((3, HEAD), lambda c: (0, c))]
    args = [proj, proj, proj, conv_w]
    out_specs, out_shape = [pl.BlockSpec((S, HEAD), lambda c: (0, c))], [jax.ShapeDtypeStruct((S, d_conv), BF16)]
    scratch = []
    aliases, wrap = _host(job, grid, in_specs, args, out_specs, out_shape, scratch)
    res = pl.pallas_call(
        wrap(body), name="conv_fwd", grid=grid, in_specs=in_specs, out_specs=out_specs, out_shape=out_shape,
        scratch_shapes=scratch, input_output_aliases=aliases,
        compiler_params=_params(("parallel" if job is None else "arbitrary",)),
    )(*args)
    return res[0] if job is None else res


def _conv_bwd(dya, proj, conv_w, d_conv, job=None):
    S = proj.shape[0]
    R = min(CONV_ROWS, S)
    nr = S // R

    def body(dya_ref, ab_ref, ac_ref, ax_ref, w_ref, dab_ref, dac_ref, dax_ref, dw_ref):
        w0, w1, w2 = w_ref[0:1, :], w_ref[1:2, :], w_ref[2:3, :]
        rid = lax.broadcasted_iota(jnp.int32, (R, HEAD), 0)

        def step(c, carry):
            dw0, dw1, dw2 = carry
            rows = pl.ds(pl.multiple_of(c * R, R), R)
            prev = pl.ds(pl.multiple_of(jnp.maximum(c - 1, 0) * R, R), R)
            nxt = pl.ds(pl.multiple_of(jnp.minimum(c + 1, nr - 1) * R, R), R)
            a_c, a_x, a_b = ac_ref[rows, :], ax_ref[rows, :], ab_ref[rows, :]
            u = a_c * a_x
            u_prev = jnp.where(c > 0, ac_ref[prev, :] * ax_ref[prev, :], 0.0)
            s1, s2 = _conv_taps(u, u_prev, rid)
            y = w0 * s2 + w1 * s1 + w2 * u
            dy = dya_ref[rows, :]
            dab_ref[rows, :] = (dy * y).astype(BF16)
            dyc = dy * a_b
            dyc_next = jnp.where(c < nr - 1, dya_ref[nxt, :] * ab_ref[nxt, :], 0.0)
            t1 = jnp.where(rid < R - 1, pltpu.roll(dyc, R - 1, 0), pltpu.roll(dyc_next, R - 1, 0))
            t2 = jnp.where(rid < R - 2, pltpu.roll(dyc, R - 2, 0), pltpu.roll(dyc_next, R - 2, 0))
            du = w2 * dyc + w1 * t1 + w0 * t2
            dac_ref[rows, :] = (du * a_x).astype(BF16)
            dax_ref[rows, :] = (du * a_c).astype(BF16)
            dw0 = dw0 + jnp.sum(dyc * s2, axis=0, keepdims=True)
            dw1 = dw1 + jnp.sum(dyc * s1, axis=0, keepdims=True)
            dw2 = dw2 + jnp.sum(dyc * u, axis=0, keepdims=True)
            return dw0, dw1, dw2

        z = jnp.zeros((1, HEAD), F32)
        dw0, dw1, dw2 = lax.fori_loop(0, nr, step, (z, z, z))
        dw_ref[0:1, :] = dw0
        dw_ref[1:2, :] = dw1
        dw_ref[2:3, :] = dw2

    col = pl.BlockSpec((S, HEAD), lambda c: (0, c))
    grid = (d_conv // HEAD,)
    in_specs = [col, _col_spec(S, 0), _col_spec(S, d_conv), _col_spec(S, 2 * d_conv),
                pl.BlockSpec((3, HEAD), lambda c: (0, c))]
    args = [dya, proj, proj, proj, conv_w]
    out_specs = [col, col, col, pl.BlockSpec((3, HEAD), lambda c: (0, c))]
    out_shape = [jax.ShapeDtypeStruct((S, d_conv), BF16)] * 3 + [jax.ShapeDtypeStruct((3, d_conv), F32)]
    scratch = []
    aliases, wrap = _host(job, grid, in_specs, args, out_specs, out_shape, scratch)
    return pl.pallas_call(
        wrap(body), name="conv_bwd", grid=grid, in_specs=in_specs, out_specs=out_specs, out_shape=out_shape,
        scratch_shapes=scratch, input_output_aliases=aliases,
        compiler_params=_params(("parallel" if job is None else "arbitrary",)),
    )(*args)


HGRN_FWD_ROWS = 512
HGRN_BWD_ROWS = 1024
HGRN_FWD_SUB = 64
HGRN_BWD_SUB = 32
HGRN_GATE_ROWS = 128


def _block_cumsum(x, pos):
    for s in (1, 2, 4, 8):
        x = x + jnp.where(pos >= s, pltpu.roll(x, s, 0), 0.0)
    return x


def _block_rev_cumsum(x, pos, rows):
    for s in (1, 2, 4, 8):
        x = x + jnp.where(pos < BLK - s, pltpu.roll(x, rows - s, 0), 0.0)
    return x


def _block_last(x, pos, rows):
    m = jnp.where(pos == BLK - 1, x, 0.0)
    for s in (1, 2, 4, 8):
        m = m + pltpu.roll(m, rows - s, 0)
    return m


def _hgrn_gates(q, z, lb):
    sgq = _sig(q)
    qs = q * sgq
    sg = _sig(z)
    f = lb + (1.0 - lb) * sg
    kk = (1.0 - lb) * (1.0 - sg)
    return sgq, qs, sg, f, kk


def _hgrn_decays(q, z, lb, pos, rows):
    sgq, qs, sg, f, kk = _hgrn_gates(q, z, lb)
    b = _block_cumsum(jnp.log(f), pos)
    return sgq, qs, sg, f, kk, b, _block_last(b, pos, rows)


def _hgrn_specs(T, d_conv, n_t, rev):
    def t_of(i):
        return (n_t - 1 - i) if rev else i

    def pcol(off):
        o = off // HEAD
        return pl.BlockSpec((T, HEAD), lambda h, i, o=o: (t_of(i), h + o))

    q_spec, z_spec, v_spec, g_spec = pcol(3 * d_conv), pcol(4 * d_conv), pcol(5 * d_conv), pcol(6 * d_conv)
    lb_spec = pl.BlockSpec((1, HEAD), lambda h, i: (0, h))
    gn_spec = pl.BlockSpec((1, HEAD), lambda h, i: (0, 0))
    act_spec = pl.BlockSpec((T, HEAD), lambda h, i: (t_of(i), h))
    st_spec = pl.BlockSpec((None, T // BLK, HEAD, HEAD), lambda h, i: (h, t_of(i), 0, 0))
    return q_spec, z_spec, v_spec, g_spec, lb_spec, gn_spec, act_spec, st_spec


def _hgrn_fwd(proj, lb, gn, d_conv, job=None):
    S = proj.shape[0]
    H = d_conv // HEAD
    T = min(HGRN_FWD_ROWS, S)
    n_t, nb = S // T, T // BLK
    sub = min(HGRN_FWD_SUB, T)
    q_spec, z_spec, v_spec, g_spec, lb_spec, gn_spec, act_spec, st_spec = _hgrn_specs(T, d_conv, n_t, False)

    def body(q_ref, z_ref, v_ref, g_ref, lb_ref, gn_ref, ob_ref, o_ref, st_ref, qe_s, ke_s, dec_s, state, v_s, o_s, u_s):
        @pl.when(pl.program_id(1) == 0)
        def _():
            state[...] = jnp.zeros_like(state)

        pos = lax.broadcasted_iota(jnp.int32, (sub, HEAD), 0) % BLK
        lbv = lb_ref[...]

        def vector_pass(s, carry):
            r = pl.ds(pl.multiple_of(s * sub, sub), sub)
            v = v_ref[r, :]
            _, qs, _, _, kk, b, b_tot = _hgrn_decays(q_ref[r, :], z_ref[r, :], lbv, pos, sub)
            qe_s[r, :] = (qs * jnp.exp(b)).astype(BF16)
            ke_s[r, :] = (kk * jnp.exp(b_tot - b)).astype(BF16)
            v_s[r, :] = v.astype(BF16)
            dec_s[r, :] = jnp.exp(b_tot)
            o = jnp.sum(qs * kk, axis=-1, keepdims=True) * v
            for d in range(1, BLK):
                e = jnp.exp(b - pltpu.roll(b, d, 0))
                a = jnp.sum(qs * pltpu.roll(kk, d, 0) * e, axis=-1, keepdims=True)
                o = o + jnp.where(pos >= d, a * pltpu.roll(v, d, 0), 0.0)
            o_s[r, :] = o
            return carry

        lax.fori_loop(0, T // sub, vector_pass, 0)

        for j in range(nb):
            rows = pl.ds(j * BLK, BLK)
            u_s[j] = lax.dot_general(v_s[rows, :], ke_s[rows, :], (((0,), (0,)), ((), ())),
                                     preferred_element_type=F32)
        st = state[...]
        for j in range(nb):
            st_ref[j] = st.astype(BF16)
            st = st * dec_s[pl.ds(j * BLK, 1), :] + u_s[j]
        state[...] = st
        for j in range(nb):
            rows = pl.ds(j * BLK, BLK)
            o_s[rows, :] += lax.dot_general(qe_s[rows, :], st_ref[j], (((1,), (1,)), ((), ())),
                                            preferred_element_type=F32)
        o = o_s[...]
        o_ref[...] = o
        r = lax.rsqrt(jnp.mean(o * o, axis=-1, keepdims=True) + EPS)
        g = g_ref[...]
        ob_ref[...] = ((o * r) * gn_ref[...] * (g * _sig(g))).astype(BF16)

    grid = (H, n_t)
    in_specs = [q_spec, z_spec, v_spec, g_spec, lb_spec, gn_spec]
    args = [proj, proj, proj, proj, lb, gn]
    out_specs = [act_spec, act_spec, st_spec, act_spec, act_spec, act_spec]
    out_shape = [jax.ShapeDtypeStruct((S, d_conv), BF16), jax.ShapeDtypeStruct((S, d_conv), F32),
                 jax.ShapeDtypeStruct((H, S // BLK, HEAD, HEAD), BF16),
                 jax.ShapeDtypeStruct((S, d_conv), BF16), jax.ShapeDtypeStruct((S, d_conv), BF16),
                 jax.ShapeDtypeStruct((S, d_conv), F32)]
    scratch = [pltpu.VMEM((HEAD, HEAD), F32), pltpu.VMEM((T, HEAD), BF16), pltpu.VMEM((T, HEAD), F32),
               pltpu.VMEM((nb, HEAD, HEAD), F32)]
    aliases, wrap = _host(job, grid, in_specs, args, out_specs, out_shape, scratch)
    return pl.pallas_call(
        wrap(body), name="hgrn_fwd", grid=grid, in_specs=in_specs, out_specs=out_specs, out_shape=out_shape,
        scratch_shapes=scratch, input_output_aliases=aliases,
        compiler_params=_params(("parallel" if job is None else "arbitrary", "arbitrary")),
    )(*args)


def _hgrn_bwd(dob, o_raw, states, qe, ke, dec, proj, lb, gn, d_conv, job=None):
    S = proj.shape[0]
    H = d_conv // HEAD
    T = min(HGRN_BWD_ROWS, S)
    n_t, nb = S // T, T // BLK
    sub = min(HGRN_BWD_SUB, T)
    gate_rows = min(HGRN_GATE_ROWS, T)
    q_spec, z_spec, v_spec, g_spec, lb_spec, gn_spec, act_spec, st_spec = _hgrn_specs(T, d_conv, n_t, True)

    def body(dob_ref, o_ref, st_ref, qe_b, ke_b, dec_s, q_ref, z_ref, v_ref, g_ref, lb_ref, gn_ref,
             dq_ref, dz_ref, dv_ref, dg_ref, dlb_ref, dgn_ref,
             dstate, do_b, v_b, dqe_s, dke_s, dvi_s, ddec_s, do_s, u_s, ds_s):
        @pl.when(pl.program_id(1) == 0)
        def _():
            dstate[...] = jnp.zeros_like(dstate)
            dlb_ref[...] = jnp.zeros_like(dlb_ref)
            dgn_ref[...] = jnp.zeros_like(dgn_ref)

        pos = lax.broadcasted_iota(jnp.int32, (sub, HEAD), 0) % BLK
        lbv, gnv = lb_ref[...], gn_ref[...]

        def before(s, carry):
            r = pl.ds(pl.multiple_of(s * gate_rows, gate_rows), gate_rows)
            g = g_ref[r, :]
            v_b[r, :] = v_ref[r, :].astype(BF16)
            o = o_ref[r, :]
            rs = lax.rsqrt(jnp.mean(o * o, axis=-1, keepdims=True) + EPS)
            on = o * rs
            sgg = _sig(g)
            dobv = dob_ref[r, :]
            dog = dobv * (g * sgg)
            dgn_ref[...] += jnp.sum(dog * on, axis=0, keepdims=True)
            don = dog * gnv
            do = rs * (don - on * jnp.mean(don * on, axis=-1, keepdims=True))
            dg_ref[r, :] = (dobv * (on * gnv) * (sgg * (1.0 + g * (1.0 - sgg)))).astype(BF16)
            do_s[r, :] = do
            do_b[r, :] = do.astype(BF16)
            return carry

        lax.fori_loop(0, T // gate_rows, before, 0)

        for j in range(nb):
            rows = pl.ds(j * BLK, BLK)
            dob_j = do_b[rows, :]
            u_s[j] = lax.dot_general(dob_j, qe_b[rows, :], (((0,), (0,)), ((), ())), preferred_element_type=F32)
            dqe_s[rows, :] = lax.dot_general(dob_j, st_ref[j], (((1,), (0,)), ((), ())), preferred_element_type=F32)
        dst = dstate[...]
        for j in reversed(range(nb)):
            ds_s[j] = dst.astype(BF16)
            ddec = jnp.sum(st_ref[j].astype(F32) * dst, axis=0, keepdims=True)
            ddec_s[pl.ds(j * BLK, BLK), :] = jnp.broadcast_to(ddec, (BLK, HEAD))
            dst = dst * dec_s[pl.ds(j * BLK, 1), :] + u_s[j]
        dstate[...] = dst
        for j in range(nb):
            rows = pl.ds(j * BLK, BLK)
            dke_s[rows, :] = lax.dot_general(v_b[rows, :], ds_s[j], (((1,), (0,)), ((), ())), preferred_element_type=F32)
            dvi_s[rows, :] = lax.dot_general(ke_b[rows, :], ds_s[j], (((1,), (1,)), ((), ())), preferred_element_type=F32)

        def after(s, carry):
            r = pl.ds(pl.multiple_of(s * sub, sub), sub)
            q, v = q_ref[r, :], v_ref[r, :]
            sgq, qs, sg, f, kk, b, b_tot = _hgrn_decays(q, z_ref[r, :], lbv, pos, sub)
            do = do_s[r, :]
            dqs = dqe_s[r, :] * jnp.exp(b)
            dkk = dke_s[r, :] * jnp.exp(b_tot - b)
            d_tot = jnp.where(pos == BLK - 1, _block_cumsum(dkk * kk, pos) + ddec_s[r, :] * jnp.exp(b_tot), 0.0)
            dv = dvi_s[r, :]
            da = jnp.sum(do * v, axis=-1, keepdims=True)
            dv = dv + jnp.sum(qs * kk, axis=-1, keepdims=True) * do
            dqs = dqs + da * kk
            dkk = dkk + da * qs
            for d in range(1, BLK):
                kd = pltpu.roll(kk, d, 0)
                e = jnp.where(pos >= d, jnp.exp(b - pltpu.roll(b, d, 0)), 0.0)
                a = jnp.sum(qs * kd * e, axis=-1, keepdims=True)
                da = jnp.sum(do * pltpu.roll(v, d, 0), axis=-1, keepdims=True)
                gq = da * e
                dqs = dqs + gq * kd
                dkk = dkk + pltpu.roll(gq * qs, sub - d, 0)
                dv = dv + pltpu.roll(a * do, sub - d, 0)
            db = qs * dqs - kk * dkk + d_tot
            dlf = _block_rev_cumsum(db, pos, sub)
            df = dlf / f - dkk
            dlb_ref[...] += jnp.sum(df * (1.0 - sg), axis=0, keepdims=True)
            dz_ref[r, :] = (df * (1.0 - lbv) * sg * (1.0 - sg)).astype(BF16)
            dq_ref[r, :] = (dqs * (sgq * (1.0 + q * (1.0 - sgq)))).astype(BF16)
            dv_ref[r, :] = dv.astype(BF16)
            return carry

        lax.fori_loop(0, T // sub, after, 0)

    tb = lambda dt: pltpu.VMEM((T, HEAD), dt)
    grid = (H, n_t)
    in_specs = [act_spec, act_spec, st_spec, act_spec, act_spec, act_spec, q_spec, z_spec, v_spec, g_spec, lb_spec,
                gn_spec]
    args = [dob, o_raw, states, qe, ke, dec, proj, proj, proj, proj, lb, gn]
    out_specs = [act_spec, act_spec, act_spec, act_spec, lb_spec, pl.BlockSpec((None, 1, HEAD), lambda h, i: (h, 0, 0))]
    out_shape = ([jax.ShapeDtypeStruct((S, d_conv), BF16)] * 4
                 + [jax.ShapeDtypeStruct((1, d_conv), F32), jax.ShapeDtypeStruct((H, 1, HEAD), F32)])
    scratch = [pltpu.VMEM((HEAD, HEAD), F32), tb(BF16), tb(BF16), tb(F32), tb(F32), tb(F32), tb(F32), tb(F32),
               pltpu.VMEM((nb, HEAD, HEAD), F32), pltpu.VMEM((nb, HEAD, HEAD), BF16)]
    aliases, wrap = _host(job, grid, in_specs, args, out_specs, out_shape, scratch)
    return pl.pallas_call(
        wrap(body), name="hgrn_bwd", grid=grid, in_specs=in_specs, out_specs=out_specs, out_shape=out_shape,
        scratch_shapes=scratch, input_output_aliases=aliases,
        compiler_params=_params(("parallel" if job is None else "arbitrary", "arbitrary")),
    )(*args)


def _local_step(x, target, mod, norm_mix_g, lb, gnorm_g, norm_ffn_g, norm_final_g, conv_w, bufs, c_idx, order):
    S, D = x.shape
    d_conv = D // 2
    d_in = 7 * d_conv + 2 * D
    d_ff = N_CHIPS * bufs["w_ffn_down"].shape[1]
    nb_in, nb_co, nb_ff = bufs["w_in"].shape[2], bufs["w_conv_out"].shape[2], bufs["w_ffn_gate"].shape[2]
    rows = lambda g: g.reshape(-1, g.shape[2])
    sh_m, sc_m, gt_m, sh_f, sc_f, gt_f = [mod[:, i * D:(i + 1) * D] for i in range(6)]
    tm = _pick(S, 512, 16)
    tm2 = _pick(S, 1024, 16)
    tn_in = _pick(nb_in, 1408, 128)
    tn_co = _pick(nb_co, 512, 128)
    tn_ff = _pick(nb_ff, 1408, 128)
    tn_d = _pick(D, 512, 128)
    tw = _pick(D, 1024, 128)
    tg = _pick(D, 512, 128)

    h = _norm_mod("norm_mix", x, norm_mix_g, sc_m, sh_m)
    proj, w_in = _proj_gather(h, bufs["w_in"], order)
    ob, o_raw, states, qe, ke, dec, *got = _hgrn_fwd(
        proj, lb, gnorm_g, d_conv,
        job=_GatherJob([bufs[k] for k in ("w_conv_out", "w_hgrn_out", "w_o", "w_ffn_gate")]))
    ya, w_conv_out, w_hgrn_out, w_o, w_ffn_gate = _conv_fwd(proj, conv_w, d_conv, job=_ForwardJob(got))
    w_o = rows(w_o)

    def merge_epi(accs, ex):
        y_a, y_b = accs
        return [y_a, y_b, _sig(ex[0]) * y_a + _sig(ex[1]) * y_b]

    y_a, y_b, merged, w_ffn_up = _matmul(
        "branch_out", [(ya, 'n', w_conv_out, 'n'), (ob, 'n', w_hgrn_out, 'n')], [0, 1], S, D, d_conv, tm2, tn_co, d_conv,
        [(proj, 'tile', 7 * d_conv), (proj, 'tile', 7 * d_conv + D)], [(BF16, None), (BF16, None), (BF16, None)], merge_epi,
        job=_GatherJob([bufs["w_ffn_up"]], 0, 2), rows_outer=True)

    def resid_epi(accs, ex):
        return [accs[0], ex[0] + ex[1] * accs[0]]

    mo, x1, w_ffn_up = _matmul("w_o", [(merged, 'n', w_o, 'n')], [0], S, D, D, tm2, tw, D,
                               [(x, 'tile', 0), (gt_m, 'row', 0)], [(BF16, None), (F32, None)], resid_epi,
                               job=_GatherJob([w_ffn_up], 1, 2))
    h2, w_ffn_up = _norm_mod("norm_ffn", x1, norm_ffn_g, sc_f, sh_f, job=_ForwardJob([w_ffn_up]))

    def swiglu_epi(accs, ex):
        gg, uu = accs
        return [gg, uu, gg * _sig(gg) * uu]

    G, U, act, w_ffn_down = _matmul(
        "ffn_in", [(h2, 'n', w_ffn_gate, 'n'), (h2, 'n', w_ffn_up, 'n')], [0, 1], S, d_ff, D,
        tm2, tn_ff, D, [], [(BF16, None), (BF16, None), (BF16, None)], swiglu_epi, job=_GatherJob([bufs["w_ffn_down"]]))
    w_ffn_down = rows(_forward_halves("forward_down", [w_ffn_down])[0])
    ff, x2 = _matmul("ffn_out", [(act, 'n', w_ffn_down, 'n')], [0], S, D, d_ff, tm2, tn_d, d_ff,
                     [(x1, 'tile', 0), (gt_f, 'row', 0)], [(BF16, None), (F32, None)], resid_epi, rows_outer=True)

    dx2, dffb, loss, dgt_f, dg_final = _loss_head(x2, target, norm_final_g, ff, gt_f)

    def swiglu_bwd_epi(accs, ex):
        dact, gg, uu = accs[0], ex[0], ex[1]
        s = _sig(gg)
        return [dact * uu * (s * (1.0 + gg * (1.0 - s))), dact * (gg * s)]

    dG, dU = _matmul("d_act", [(dffb, 'n', w_ffn_down, 't')], [0], S, d_ff, D, tm2, tn_ff, D,
                     [(G, 'tile', 0), (U, 'tile', 0)], [(BF16, None), (BF16, None)], swiglu_bwd_epi)
    dw_down, = _matmul("dw_ffn_down", [(act, 't', dffb, 'n')], [0], d_ff, D, S, _pick(d_ff, 512, 128),
                       D, S, [], [(BF16, None)], _plain)
    dh2, = _matmul("d_h2", [(dG, 'n', w_ffn_gate, 't'), (dU, 'n', w_ffn_up, 't')], [0, 0], S, D, d_ff,
                   tm, D, tn_ff, [], [(F32, None)], _plain)
    dw_gate, = _matmul("dw_ffn_gate", [(h2, 't', dG, 'n')], [0], D, d_ff, S, tg, tn_ff, S, [], [(BF16, nb_ff)], _plain)
    dw_up, = _matmul("dw_ffn_up", [(h2, 't', dU, 'n')], [0], D, d_ff, S, tg, tn_ff, S, [], [(BF16, nb_ff)], _plain)

    ck_idx = jnp.stack([c_idx[0], order[0]])

    def pair_sums(names, grads, from_sibling):
        pairs = [_pair_sum("pair_sum_" + k, g, q, ck_idx) for k, g, q in zip(names, grads, from_sibling)]
        return [p[0] for p in pairs], [p[1] for p in pairs]

    ffn_names = ["w_ffn_down", "w_ffn_gate", "w_ffn_up"]
    ffn_grads = [dw_down.reshape(N_CHIPS, -1, D), dw_gate, dw_up]
    dx1, dsh_f, dsc_f, dg_ffn, dmob, dgt_m, *from_sibling = _norm_mod_bwd(
        "norm_ffn_bwd", dh2, x1, norm_ffn_g, sc_f, dx2, (mo, gt_m), job=_SwapJob(ffn_grads))
    parts_f, slots_f = pair_sums(ffn_names, ffn_grads, from_sibling)

    def merge_bwd_epi(accs, ex):
        dm, ga, gb, ya_, yb_ = accs[0], ex[0], ex[1], ex[2], ex[3]
        sa, sb = _sig(ga), _sig(gb)
        return [dm * ya_ * sa * (1.0 - sa), dm * yb_ * sb * (1.0 - sb), dm * sa, dm * sb]

    dga, dgb, dy_a, dy_b = _matmul(
        "d_merged", [(dmob, 'n', w_o, 't')], [0], S, D, D, tm2, tn_co, D,
        [(proj, 'tile', 7 * d_conv), (proj, 'tile', 7 * d_conv + D), (y_a, 'tile', 0), (y_b, 'tile', 0)],
        [(BF16, None)] * 4, merge_bwd_epi, rows_outer=True)
    dw_o, = _matmul("dw_o", [(merged, 't', dmob, 'n')], [0], D, D, S, tg, D, S, [], [(BF16, None)], _plain)
    dya, dob = _matmul("d_branch", [(dy_a, 'n', w_conv_out, 't'), (dy_b, 'n', w_hgrn_out, 't')], [0, 1], S, d_conv, D,
                       tm2, _pick(d_conv, 1024, 128), tn_co, [], [(F32, None), (F32, None)], _plain)
    dw_co, dw_ho = _matmul("dw_branch", [(ya, 't', dy_a, 'n'), (ob, 't', dy_b, 'n')], [0, 1], d_conv, D, S,
                           _pick(d_conv, 512, 128), tn_co, S, [], [(BF16, nb_co), (BF16, nb_co)], _plain)
    branch_names = ["w_conv_out", "w_hgrn_out", "w_o"]
    branch_grads = [dw_co, dw_ho, dw_o.reshape(N_CHIPS, -1, D)]
    dab, dac, dax, dconv_w, *from_sibling = _conv_bwd(dya, proj, conv_w, d_conv, job=_SwapJob(branch_grads))
    parts_b, slots_b = pair_sums(branch_names, branch_grads, from_sibling)
    dq, dz, dv, dgo, dlb, dgn, *arrived_f = _hgrn_bwd(dob, o_raw, states, qe, ke, dec, proj, lb, gnorm_g, d_conv,
                                                      job=_ScatterJob(parts_f, slots_f))
    dproj = jnp.concatenate([dab, dac, dax, dq, dz, dv, dgo, dga, dgb], axis=1)
    halves_f = [_sum_chips("sum_chips_" + k, r, c_idx) for k, r in zip(ffn_names, arrived_f)]
    dw_in, *moved = _matmul("dw_in", [(h, 't', dproj, 'n')], [0], D, d_in, S, tg, tn_in, S, [], [(BF16, nb_in)], _plain,
                            job=_Jobs([_ShareJob(halves_f), _ScatterJob(parts_b, slots_b)]))
    whole_f, arrived_b = moved[:len(ffn_names)], moved[len(ffn_names):]
    parts_i, slots_i = pair_sums(["w_in"], [dw_in], _swap_halves("swap_in", [dw_in]))
    dh, arrived_in = _matmul("d_h", [(dproj, 'n', w_in, 't')], [0], S, D, d_in, tm2, D, tn_in, [], [(F32, None)], _plain,
                             job=_ScatterJob(parts_i, slots_i))
    dx, dsh_m, dsc_m, dg_mix = _norm_mod_bwd("norm_mix_bwd", dh, x, norm_mix_g, sc_m, dx1)
    dmod = jnp.concatenate([dsh_m, dsc_m, dgt_m, dsh_f, dsc_f, dgt_f], axis=1)
    small = dict(dmod=dmod, dg_mix=dg_mix, dg_ffn=dg_ffn, dg_final=dg_final, dlb=dlb,
                 dgn=jnp.sum(dgn, axis=0), dconv_w=dconv_w)
    late_names = branch_names + ["w_in"]
    late = _share_halves([_sum_chips("sum_chips_" + k, r, c_idx) for k, r in zip(late_names, arrived_b + [arrived_in])])
    big = dict(zip(ffn_names + late_names, whole_f + list(late)))
    return loss, dx, small, big


def _adamw_math(w, g, m, v):
    m = ADAM_B1 * m + (1.0 - ADAM_B1) * g
    v = ADAM_B2 * v + (1.0 - ADAM_B2) * (g * g)
    m_hat = m / (1.0 - ADAM_B1 ** ADAM_STEP)
    v_hat = v / (1.0 - ADAM_B2 ** ADAM_STEP)
    delta = -ADAM_LR * (m_hat / (jnp.sqrt(v_hat) + ADAM_EPS) + ADAM_WD * w)
    return delta, m, v


def _adamw(name, w, g, m, v):
    R, C = w.shape
    tr = _pick(R, max(8, (256 * 1024) // C // 8 * 8), 8)
    spec = pl.BlockSpec((tr, C), lambda i: (i, 0))

    def body(w_ref, g_ref, m_ref, v_ref, go_ref, d_ref, mo_ref, vo_ref):
        gv = g_ref[...]
        d, m2, v2 = _adamw_math(w_ref[...], gv, m_ref[...], v_ref[...])
        go_ref[...] = gv
        d_ref[...] = d
        mo_ref[...] = m2
        vo_ref[...] = v2

    return pl.pallas_call(
        body, name=name, grid=(R // tr,), in_specs=[spec] * 4, out_specs=[spec] * 4,
        out_shape=[jax.ShapeDtypeStruct((R, C), F32)] * 4, compiler_params=_params(("parallel",)),
    )(w, g, m, v)


def _ada_update(c_act_t, dmod, w, m, v):
    R, C = w.shape
    B = dmod.shape[0]
    tr = _pick(R, 256, 8)
    tc = _pick(C, 1536, 128)
    spec = pl.BlockSpec((tr, tc), lambda i, j: (i, j))

    def body(ct_ref, dm_ref, w_ref, m_ref, v_ref, g_ref, d_ref, mo_ref, vo_ref):
        ct = ct_ref[...]
        dm = dm_ref[...]
        g = ct[:, 0:1] * dm[0:1, :]
        for b in range(1, B):
            g = g + ct[:, b:b + 1] * dm[b:b + 1, :]
        d, m2, v2 = _adamw_math(w_ref[...], g, m_ref[...], v_ref[...])
        g_ref[...] = g
        d_ref[...] = d
        mo_ref[...] = m2
        vo_ref[...] = v2

    return pl.pallas_call(
        body, name="ada_update", grid=(R // tr, C // tc),
        in_specs=[pl.BlockSpec((tr, B), lambda i, j: (i, 0)), pl.BlockSpec((B, tc), lambda i, j: (0, j)),
                  spec, spec, spec],
        out_specs=[spec] * 4, out_shape=[jax.ShapeDtypeStruct((R, C), F32)] * 4,
        compiler_params=_params(("parallel", "parallel")),
    )(c_act_t, dmod, w, m, v)


def _prep(c_all, lb_param):
    def body(c_ref, p_ref, ca_ref, cab_ref, lb_ref):
        cv = c_ref[...]
        ca = cv * _sig(cv)
        ca_ref[...] = ca
        cab_ref[...] = ca.astype(BF16)
        lb_ref[...] = _sig(p_ref[0:1, :] - p_ref[1:2, :])

    return pl.pallas_call(
        body, name="prep",
        out_shape=[jax.ShapeDtypeStruct(c_all.shape, F32), jax.ShapeDtypeStruct(c_all.shape, BF16),
                   jax.ShapeDtypeStruct((1, lb_param.shape[1]), F32)],
    )(c_all, lb_param)


def _small_reduce(parts):
    W = parts.shape[1]

    def body(p_ref, o_ref):
        acc = p_ref[0:1, :]
        for d in range(1, N_DEV):
            acc = acc + p_ref[d:d + 1, :]
        o_ref[...] = acc

    return pl.pallas_call(body, name="small_reduce", out_shape=jax.ShapeDtypeStruct((1, W), F32))(parts)


def _lb_grad(dlb, lb):
    def body(d_ref, lb_ref, o_ref):
        t = d_ref[...] * lb_ref[...] * (1.0 - lb_ref[...])
        o_ref[0:1, :] = t
        o_ref[1:2, :] = -t

    return pl.pallas_call(body, name="lb_grad", out_shape=jax.ShapeDtypeStruct((2, dlb.shape[1]), F32))(dlb, lb)


def _place():
    x, y, c = lax.axis_index("x"), lax.axis_index("y"), lax.axis_index("c")
    chips = [(1 - x, y), (x, 1 - y), (1 - x, 1 - y)]
    return x, y, c, chips


def _allgather_rows(name, v):
    m_per, n = v.shape

    def body(x_ref, out_ref, send_sems, recv_sems, local_sem):
        x, y, c, chips = _place()
        me, sibling = (x, y, c), (x, y, 1 - c)

        def rows(px, py, pc):
            return out_ref.at[pl.ds((4 * px + 2 * py + pc) * m_per, m_per), :]

        def copy(k, block, to, src=None):
            return pltpu.make_async_remote_copy(
                src_ref=rows(*block) if src is None else src, dst_ref=rows(*block),
                send_sem=send_sems.at[k], recv_sem=recv_sems.at[k], device_id=to, device_id_type=MESH)

        mine = pltpu.make_async_copy(x_ref, rows(*me), local_sem)
        mine.start()
        first = [copy(0, me, sibling, src=x_ref)]
        first += [copy(1 + j, me, (*chip, c), src=x_ref) for j, chip in enumerate(chips)]
        for cp in first:
            cp.start()
        passed = [copy(4 + j, (*chip, c), sibling) for j, chip in enumerate(chips)]
        for j, chip in enumerate(chips):
            copy(1 + j, (*chip, c), me).wait_recv()
            passed[j].start()
        copy(0, sibling, me).wait_recv()
        for j, chip in enumerate(chips):
            copy(4 + j, (*chip, 1 - c), me).wait_recv()
        for cp in first + passed:
            cp.wait_send()
        mine.wait()

    return pl.pallas_call(
        body, name=name, out_shape=jax.ShapeDtypeStruct((N_DEV * m_per, n), v.dtype),
        in_specs=[pl.BlockSpec(memory_space=pltpu.VMEM)], out_specs=pl.BlockSpec(memory_space=pltpu.VMEM),
        scratch_shapes=[pltpu.SemaphoreType.DMA((7,)), pltpu.SemaphoreType.DMA((7,)), pltpu.SemaphoreType.DMA],
    )(v)


def _cast_place(name, w, k_idx):
    R, C = w.shape
    tr = _pick(R, max(16, (512 * 1024) // C // 16 * 16), 16)

    def body(k_ref, w_ref, o_ref):
        o_ref[...] = w_ref[...].astype(BF16)

    return pl.pallas_call(
        body, name=name,
        grid_spec=pltpu.PrefetchScalarGridSpec(
            num_scalar_prefetch=1, grid=(R // tr,),
            in_specs=[pl.BlockSpec((tr, C), lambda i, k_ref: (i, 0))],
            out_specs=pl.BlockSpec((None, tr, C), lambda i, k_ref: (k_ref[0], i, 0))),
        out_shape=jax.ShapeDtypeStruct((N_CHIPS, R, C), BF16),
        compiler_params=_params(("parallel",)),
    )(k_idx, w)


def _chip_copies(src_of, dst_of, got_of, n, sems, receiving):
    x, y, c, chips = _place()
    k_me = 2 * x + y
    send_sems, recv_sems = sems
    out = []
    for a in range(n):
        for j, chip in enumerate(chips):
            k_chip = 2 * chip[0] + chip[1]
            if receiving:
                src = dst = got_of(a, k_chip, c)
                to = (x, y, c)
            else:
                src, dst, to = src_of(a, k_chip, k_me, c), dst_of(a, k_me, c), (*chip, c)
            out.append(pltpu.make_async_remote_copy(
                src_ref=src, dst_ref=dst, send_sem=send_sems.at[3 * a + j], recv_sem=recv_sems.at[3 * a + j],
                device_id=to, device_id_type=MESH))
    return out


class _ChipJob:
    def start(self, j_in, j_out, sems):
        for send in self.copies(j_in, j_out, sems, False):
            send.start()

    def finish(self, j_in, j_out, sems):
        for recv in self.copies(j_in, j_out, sems, True):
            recv.wait_recv()
        for send in self.copies(j_in, j_out, sems, False):
            send.wait_send()


class _GatherJob(_ChipJob):
    def __init__(self, bufs, part=0, n_parts=1):
        n = len(bufs)
        self.inputs, self.n_alias = list(bufs), n
        self.sem_shapes = [pltpu.SemaphoreType.DMA((3 * n,)), pltpu.SemaphoreType.DMA((3 * n,))]
        self.half = [b.shape[1] // 2 for b in bufs]
        self.part, self.n_parts = part, n_parts

    def copies(self, j_in, j_out, sems, receiving):
        def half(a, k, c):
            rows = self.half[a] // self.n_parts
            return j_out[a].at[k, pl.ds(c * self.half[a] + self.part * rows, rows)]

        return _chip_copies(lambda a, k_chip, k_me, c: half(a, k_me, c), half, half, len(self.half), sems, receiving)


class _ScatterJob(_ChipJob):
    def __init__(self, parts, slots):
        n = len(parts)
        self.n = n
        self.inputs, self.n_alias = list(parts) + list(slots), n
        self.sem_shapes = [pltpu.SemaphoreType.DMA((3 * n,)), pltpu.SemaphoreType.DMA((3 * n,))]

    def copies(self, j_in, j_out, sems, receiving):
        return _chip_copies(lambda a, k_chip, k_me, c: j_in[a].at[k_chip], lambda a, k_me, c: j_out[a].at[k_me],
                            lambda a, k_chip, c: j_out[a].at[k_chip], self.n, sems, receiving)


class _SwapJob(_ChipJob):
    def __init__(self, grads):
        n = len(grads)
        self.n = n
        self.half = [g.shape[1] // 2 for g in grads]
        landing = [lax.empty((N_CHIPS, g.shape[1] // 2, g.shape[2]), g.dtype) for g in grads]
        self.inputs, self.n_alias = list(grads) + landing, n
        self.sem_shapes = [pltpu.SemaphoreType.DMA((n,)), pltpu.SemaphoreType.DMA((n,))]

    def copies(self, j_in, j_out, sems, receiving):
        x, y, c, _ = _place()
        out = []
        for a in range(self.n):
            if receiving:
                src, to = j_out[a], (x, y, c)
            else:
                src, to = j_in[a].at[:, pl.ds((1 - c) * self.half[a], self.half[a])], (x, y, 1 - c)
            out.append(pltpu.make_async_remote_copy(src_ref=src, dst_ref=j_out[a], send_sem=sems[0].at[a],
                                                    recv_sem=sems[1].at[a], device_id=to, device_id_type=MESH))
        return out


class _Jobs:
    def __init__(self, jobs):
        self.jobs = jobs
        split = [(j.inputs[:len(j.inputs) - j.n_alias], j.inputs[len(j.inputs) - j.n_alias:]) for j in jobs]
        self.inputs = [a for plain, _ in split for a in plain] + [a for _, aliased in split for a in aliased]
        self.n_alias = sum(j.n_alias for j in jobs)
        self.sem_shapes = [s for j in jobs for s in j.sem_shapes]

    def _each(self, j_in, j_out, sems):
        n_plain = len(j_in) - self.n_alias
        p0 = a0 = s0 = 0
        for j in self.jobs:
            n_p, n_s = len(j.inputs) - j.n_alias, len(j.sem_shapes)
            ins = list(j_in[p0:p0 + n_p]) + list(j_in[n_plain + a0:n_plain + a0 + j.n_alias])
            yield j, ins, j_out[a0:a0 + j.n_alias], sems[s0:s0 + n_s]
            p0, a0, s0 = p0 + n_p, a0 + j.n_alias, s0 + n_s

    def start(self, j_in, j_out, sems):
        for j, ins, outs, s in self._each(j_in, j_out, sems):
            j.start(ins, outs, s)

    def finish(self, j_in, j_out, sems):
        for j, ins, outs, s in self._each(j_in, j_out, sems):
            j.finish(ins, outs, s)


class _ShareJob(_ChipJob):
    def __init__(self, bufs):
        n = len(bufs)
        self.n = n
        self.inputs, self.n_alias = list(bufs), n
        self.sem_shapes = [pltpu.SemaphoreType.DMA((n,)), pltpu.SemaphoreType.DMA((n,))]

    def copies(self, j_in, j_out, sems, receiving):
        x, y, c, _ = _place()
        out = []
        for a in range(self.n):
            hc, to = (1 - c, (x, y, c)) if receiving else (c, (x, y, 1 - c))
            out.append(pltpu.make_async_remote_copy(
                src_ref=j_out[a].at[hc], dst_ref=j_out[a].at[hc], send_sem=sems[0].at[a], recv_sem=sems[1].at[a],
                device_id=to, device_id_type=MESH))
        return out


class _ForwardJob(_ChipJob):
    def __init__(self, bufs):
        n = len(bufs)
        self.n = n
        self.half = [b.shape[1] // 2 for b in bufs]
        self.inputs, self.n_alias = list(bufs), n
        self.sem_shapes = [pltpu.SemaphoreType.DMA((3 * n,)), pltpu.SemaphoreType.DMA((3 * n,))]

    def copies(self, j_in, j_out, sems, receiving):
        x, y, c, chips = _place()
        out = []
        for a in range(self.n):
            for q, chip in enumerate(chips):
                hc, to = (1 - c, (x, y, c)) if receiving else (c, (x, y, 1 - c))
                part = j_out[a].at[2 * chip[0] + chip[1], pl.ds(hc * self.half[a], self.half[a])]
                out.append(pltpu.make_async_remote_copy(
                    src_ref=part, dst_ref=part, send_sem=sems[0].at[3 * a + q], recv_sem=sems[1].at[3 * a + q],
                    device_id=to, device_id_type=MESH))
        return out


def _forward_halves(name, bufs):
    n = len(bufs)

    def body(*refs):
        out_refs = refs[n:2 * n]
        send_sems, recv_sems = refs[2 * n:]
        x, y, c, chips = _place()
        started, waits = [], []
        for a in range(n):
            rh = bufs[a].shape[1] // 2
            for j, chip in enumerate(chips):
                k_chip = 2 * chip[0] + chip[1]
                got = out_refs[a].at[k_chip, pl.ds(c * rh, rh)]
                cp = pltpu.make_async_remote_copy(src_ref=got, dst_ref=got, send_sem=send_sems.at[3 * a + j],
                                                  recv_sem=recv_sems.at[3 * a + j], device_id=(x, y, 1 - c),
                                                  device_id_type=MESH)
                cp.start()
                started.append(cp)
                other = out_refs[a].at[k_chip, pl.ds((1 - c) * rh, rh)]
                waits.append(pltpu.make_async_remote_copy(
                    src_ref=other, dst_ref=other, send_sem=send_sems.at[3 * a + j], recv_sem=recv_sems.at[3 * a + j],
                    device_id=(x, y, c), device_id_type=MESH))
        for cp in waits:
            cp.wait_recv()
        for cp in started:
            cp.wait_send()

    return pl.pallas_call(
        body, name=name, out_shape=[jax.ShapeDtypeStruct(b.shape, b.dtype) for b in bufs],
        in_specs=[_HBM] * n, out_specs=[_HBM] * n, input_output_aliases={a: a for a in range(n)},
        scratch_shapes=[pltpu.SemaphoreType.DMA((3 * n,)), pltpu.SemaphoreType.DMA((3 * n,))],
    )(*bufs)


def _proj_gather(h, buf, order):
    S, D = h.shape
    nb = buf.shape[2]
    tm = _pick(S, 1024, 16)
    tn = _pick(nb, 1408, 128)
    per = nb // tn
    nj, ni = N_CHIPS * per, S // tm
    rh = D // 2
    seq = [(0, t) for t in range(per)] + [(p, t) for t in range(per) for p in (1, 2)] + [(3, t) for t in range(per)]
    tile_chip = jnp.stack([order[p] for p, _ in seq])
    tile_of = jnp.array([t for _, t in seq], jnp.int32)

    def body(chip_ref, t_ref, h_ref, buf_in, proj_ref, buf_ref, wbuf, tile_sems, ici_send, ici_recv, d2d_send, d2d_recv):
        j, i = pl.program_id(0), pl.program_id(1)
        x, y, c, chips = _place()
        k_me = 2 * x + y

        def part(k, hc, t):
            return buf_ref.at[k, pl.ds(hc * rh, rh), pl.ds(t * tn, tn)]

        def ici(q, t, receiving):
            k_chip = 2 * chips[q][0] + chips[q][1]
            src, to = (part(k_chip, c, t), (x, y, c)) if receiving else (part(k_me, c, t), (*chips[q], c))
            return pltpu.make_async_remote_copy(src_ref=src, dst_ref=src, send_sem=ici_send.at[q * per + t],
                                                recv_sem=ici_recv.at[q * per + t], device_id=to, device_id_type=MESH)

        def d2d(q, t, receiving):
            k_chip = 2 * chips[q][0] + chips[q][1]
            src, to = (part(k_chip, 1 - c, t), (x, y, c)) if receiving else (part(k_chip, c, t), (x, y, 1 - c))
            return pltpu.make_async_remote_copy(src_ref=src, dst_ref=src, send_sem=d2d_send.at[q * per + t],
                                                recv_sem=d2d_recv.at[q * per + t], device_id=to, device_id_type=MESH)

        def tile_copy(n):
            col = pl.multiple_of(t_ref[n] * tn, 128)
            return pltpu.make_async_copy(buf_ref.at[chip_ref[n], :, pl.ds(col, tn)], wbuf.at[n % 2], tile_sems.at[n % 2])

        @pl.when(jnp.logical_and(j == 0, i == 0))
        def _():
            for t in range(per):
                ici(0, t, False).start()
                ici(1, t, False).start()
            tile_copy(0).start()

        @pl.when(i == 0)
        def _():
            tile_copy(j).wait()
            for n in range(per, nj):
                p, t = seq[n]

                @pl.when(j + 1 == n)
                def _():
                    ici(p - 1, t, True).wait_recv()
                    d2d(p - 1, t, False).start()
                    if (p, t) == (1, per - 1):
                        for q in range(2):
                            for tt in range(per):
                                ici(q, tt, False).wait_send()
                        for tt in range(per):
                            ici(2, tt, False).start()
                    d2d(p - 1, t, True).wait_recv()

            @pl.when(j + 1 < nj)
            def _():
                tile_copy(j + 1).start()

        proj_ref[...] = jnp.dot(h_ref[...], wbuf[j % 2], preferred_element_type=F32)

        @pl.when(jnp.logical_and(j == nj - 1, i == ni - 1))
        def _():
            for t in range(per):
                ici(2, t, False).wait_send()
                for q in range(3):
                    d2d(q, t, False).wait_send()

    n_sems = 3 * per
    return pl.pallas_call(
        body, name="proj",
        grid_spec=pltpu.PrefetchScalarGridSpec(
            num_scalar_prefetch=2, grid=(nj, ni),
            in_specs=[pl.BlockSpec((tm, D), lambda j, i, kc, kt: (i, 0)), _HBM],
            out_specs=[pl.BlockSpec((tm, tn), lambda j, i, kc, kt: (i, kc[j] * per + kt[j])), _HBM],
            scratch_shapes=[pltpu.VMEM((2, D, tn), BF16), pltpu.SemaphoreType.DMA((2,))]
            + [pltpu.SemaphoreType.DMA((n_sems,))] * 4),
        out_shape=[jax.ShapeDtypeStruct((S, N_CHIPS * nb), F32), jax.ShapeDtypeStruct(buf.shape, buf.dtype)],
        input_output_aliases={3: 1}, compiler_params=_params(("arbitrary", "arbitrary")),
    )(tile_chip, tile_of, h, buf)


def _swap_halves(name, grads):
    n = len(grads)

    def body(*refs):
        in_refs, out_refs = refs[:n], refs[n:2 * n]
        send_sems, recv_sems = refs[2 * n:]
        x, y, c, _ = _place()
        cps = []
        for a in range(n):
            rh = grads[a].shape[1] // 2
            cp = pltpu.make_async_remote_copy(
                src_ref=in_refs[a].at[:, pl.ds((1 - c) * rh, rh)], dst_ref=out_refs[a],
                send_sem=send_sems.at[a], recv_sem=recv_sems.at[a], device_id=(x, y, 1 - c), device_id_type=MESH)
            cp.start()
            cps.append(cp)
        for cp in cps:
            cp.wait()

    return pl.pallas_call(
        body, name=name,
        out_shape=[jax.ShapeDtypeStruct((N_CHIPS, g.shape[1] // 2, g.shape[2]), g.dtype) for g in grads],
        in_specs=[_HBM] * n, out_specs=[_HBM] * n,
        scratch_shapes=[pltpu.SemaphoreType.DMA((n,)), pltpu.SemaphoreType.DMA((n,))],
    )(*grads)


def _pair_sum(name, g, q, ck_idx):
    _, R, C = g.shape
    rh = R // 2
    tr = _pick(rh, max(16, (512 * 1024) // C // 16 * 16), 16)
    nh = rh // tr

    def body(ck_ref, g_ref, q_ref, o_ref, own_ref):
        s = (g_ref[...].astype(F32) + q_ref[...].astype(F32)).astype(BF16)
        o_ref[...] = s

        @pl.when(pl.program_id(1) == ck_ref[1])
        def _():
            own_ref[...] = s

    return pl.pallas_call(
        body, name=name,
        grid_spec=pltpu.PrefetchScalarGridSpec(
            num_scalar_prefetch=1, grid=(nh, N_CHIPS),
            in_specs=[pl.BlockSpec((None, tr, C), lambda i, k, ck: (k, ck[0] * nh + i, 0)),
                      pl.BlockSpec((None, tr, C), lambda i, k, ck: (k, i, 0))],
            out_specs=[pl.BlockSpec((None, tr, C), lambda i, k, ck: (k, i, 0)),
                       pl.BlockSpec((None, tr, C), lambda i, k, ck: (ck[1], i, 0))]),
        out_shape=[jax.ShapeDtypeStruct((N_CHIPS, rh, C), BF16)] * 2,
        compiler_params=_params(("parallel", "arbitrary")),
    )(ck_idx, g, q)


def _sum_chips(name, r, c_idx):
    _, rh, C = r.shape
    tr = _pick(rh, max(16, (256 * 1024) // C // 16 * 16), 16)

    def body(c_ref, r_ref, o_ref):
        acc = r_ref[0].astype(F32)
        for k in range(1, N_CHIPS):
            acc = acc + r_ref[k].astype(F32)
        o_ref[...] = acc

    return pl.pallas_call(
        body, name=name,
        grid_spec=pltpu.PrefetchScalarGridSpec(
            num_scalar_prefetch=1, grid=(rh // tr,),
            in_specs=[pl.BlockSpec((N_CHIPS, tr, C), lambda i, c_ref: (0, i, 0))],
            out_specs=pl.BlockSpec((None, tr, C), lambda i, c_ref: (c_ref[0], i, 0))),
        out_shape=jax.ShapeDtypeStruct((2, rh, C), F32), compiler_params=_params(("parallel",)),
    )(c_idx, r)


def _share_halves(bufs):
    n = len(bufs)

    def body(*refs):
        out_refs = refs[n:2 * n]
        send_sems, recv_sems = refs[2 * n:]
        x, y, c, _ = _place()
        cps = []
        for a in range(n):
            cp = pltpu.make_async_remote_copy(
                src_ref=out_refs[a].at[c], dst_ref=out_refs[a].at[c], send_sem=send_sems.at[a],
                recv_sem=recv_sems.at[a], device_id=(x, y, 1 - c), device_id_type=MESH)
            cp.start()
            cps.append(cp)
        for a, cp in enumerate(cps):
            got = out_refs[a].at[1 - c]
            pltpu.make_async_remote_copy(src_ref=got, dst_ref=got, send_sem=send_sems.at[a], recv_sem=recv_sems.at[a],
                                         device_id=(x, y, c), device_id_type=MESH).wait_recv()
        for cp in cps:
            cp.wait_send()

    return pl.pallas_call(
        body, name="share_halves",
        out_shape=[jax.ShapeDtypeStruct(b.shape, b.dtype) for b in bufs],
        in_specs=[_HBM] * n, out_specs=[_HBM] * n, input_output_aliases={a: a for a in range(n)},
        scratch_shapes=[pltpu.SemaphoreType.DMA((n,)), pltpu.SemaphoreType.DMA((n,))],
    )(*bufs)


_BIG = ("w_in", "w_conv_out", "w_hgrn_out", "w_o", "w_ffn_gate", "w_ffn_up", "w_ffn_down")
_WEIGHTS = ("w_ada", "b_ada", "norm_mix_g", "w_in", "conv_w", "lb_param", "gnorm_g", "w_conv_out", "w_hgrn_out",
            "w_o", "norm_ffn_g", "w_ffn_gate", "w_ffn_up", "w_ffn_down", "norm_final_g")


def _pack_rows(pieces):
    flat = jnp.concatenate([p.reshape(-1) for p in pieces])
    pad = (-flat.shape[0]) % (SUBLANES * LANES)
    return jnp.pad(flat, (0, pad)).reshape(SUBLANES, -1)


def kernel(x, c, w_ada, b_ada, norm_mix_g, w_in, conv_w, lb_param, gnorm_g, w_conv_out, w_hgrn_out, w_o, norm_ffn_g, w_ffn_gate, w_ffn_up, w_ffn_down, norm_final_g, loss_target, m_w_ada, m_b_ada, m_norm_mix_g, m_w_in, m_conv_w, m_lb_param, m_gnorm_g, m_w_conv_out, m_w_hgrn_out, m_w_o, m_norm_ffn_g, m_w_ffn_gate, m_w_ffn_up, m_w_ffn_down, m_norm_final_g, v_w_ada, v_b_ada, v_norm_mix_g, v_w_in, v_conv_w, v_lb_param, v_gnorm_g, v_w_conv_out, v_w_hgrn_out, v_w_o, v_norm_ffn_g, v_w_ffn_gate, v_w_ffn_up, v_w_ffn_down, v_norm_final_g):
    w = dict(w_ada=w_ada, b_ada=b_ada, norm_mix_g=norm_mix_g, w_in=w_in, conv_w=conv_w, lb_param=lb_param,
             gnorm_g=gnorm_g, w_conv_out=w_conv_out, w_hgrn_out=w_hgrn_out, w_o=w_o, norm_ffn_g=norm_ffn_g,
             w_ffn_gate=w_ffn_gate, w_ffn_up=w_ffn_up, w_ffn_down=w_ffn_down, norm_final_g=norm_final_g)
    m = dict(w_ada=m_w_ada, b_ada=m_b_ada, norm_mix_g=m_norm_mix_g, w_in=m_w_in, conv_w=m_conv_w, lb_param=m_lb_param,
             gnorm_g=m_gnorm_g, w_conv_out=m_w_conv_out, w_hgrn_out=m_w_hgrn_out, w_o=m_w_o, norm_ffn_g=m_norm_ffn_g,
             w_ffn_gate=m_w_ffn_gate, w_ffn_up=m_w_ffn_up, w_ffn_down=m_w_ffn_down, norm_final_g=m_norm_final_g)
    v = dict(w_ada=v_w_ada, b_ada=v_b_ada, norm_mix_g=v_norm_mix_g, w_in=v_w_in, conv_w=v_conv_w, lb_param=v_lb_param,
             gnorm_g=v_gnorm_g, w_conv_out=v_w_conv_out, w_hgrn_out=v_w_hgrn_out, w_o=v_w_o, norm_ffn_g=v_norm_ffn_g,
             w_ffn_gate=v_w_ffn_gate, w_ffn_up=v_w_ffn_up, w_ffn_down=v_w_ffn_down, norm_final_g=v_norm_final_g)
    two_d = lambda a: a.reshape((-1, a.shape[-1])) if a.ndim != 2 else a
    shapes = {k: a.shape for k, a in w.items()}
    w, m, v = ({k: two_d(a) for k, a in t.items()} for t in (w, m, v))
    w["lb_param"], m["lb_param"], v["lb_param"] = lb_param, m_lb_param, v_lb_param
    S, D = x.shape[1], x.shape[2]
    d_conv = D // 2
    ix, iy, ic = lax.axis_index("x"), lax.axis_index("y"), lax.axis_index("c")
    k_me = 2 * ix + iy
    dev = 2 * k_me + ic
    cw_shard = w["conv_w"].shape[1]
    ada_cols = w["w_ada"].shape[1]

    got = _allgather_rows("gather_cond", _pack_rows([c, w["conv_w"]])).reshape(N_DEV, -1)
    c_all = got[:, :D]
    conv_full = got[::2, D:D + 3 * cw_shard].reshape(N_CHIPS, 3, cw_shard).transpose(1, 0, 2).reshape(3, -1)
    c_act, c_act_b, lb = _prep(c_all, lb_param)
    b_cols = lax.dynamic_slice(w["b_ada"], (0, k_me * ada_cols), (1, ada_cols))
    mod_cols, = _matmul("ada_fwd", [(c_act_b, 'n', w["w_ada"].astype(BF16), 'n')], [0], N_DEV, ada_cols, D,
                        N_DEV, _pick(ada_cols, 1536, 128), D, [(b_cols, 'row', 0)], [(F32, None)],
                        lambda accs, ex: [accs[0] + ex[0]])
    mod_all = _allgather_rows("gather_mod", mod_cols).reshape(N_CHIPS, 2, N_DEV, ada_cols)[:, 0]
    mod_all = mod_all.transpose(1, 0, 2).reshape(N_DEV, -1)
    mod = lax.dynamic_slice(mod_all, (dev, 0), (1, mod_all.shape[1]))
    k_idx = jnp.reshape(k_me, (1,)).astype(jnp.int32)
    c_idx = jnp.reshape(ic, (1,)).astype(jnp.int32)
    bufs = {k: _cast_place("cast_" + k, w[k], k_idx) for k in _BIG}
    order = jnp.stack([k_me, 2 * (1 - ix) + iy, 2 * ix + (1 - iy), 2 * (1 - ix) + (1 - iy)]).astype(jnp.int32)

    loss, dx, small, arrived = _local_step(
        x[0], loss_target[0], mod, w["norm_mix_g"], lb, w["gnorm_g"], w["norm_ffn_g"], w["norm_final_g"], conv_full,
        bufs, c_idx, order)

    pieces = [small["dmod"], small["dg_mix"], small["dg_ffn"], small["dg_final"], small["dlb"], small["dgn"],
              small["dconv_w"], loss]
    sizes = [p.size for p in pieces]
    parts = _allgather_rows("gather_small", _pack_rows(pieces)).reshape(N_DEV, -1)
    total = _small_reduce(parts)
    offs = [0]
    for s in sizes:
        offs.append(offs[-1] + s)
    tot = [total[:, offs[i]:offs[i + 1]] for i in range(len(sizes))]
    dmod_all = parts[:, :sizes[0]]
    grads = {
        "b_ada": tot[0], "norm_mix_g": tot[1], "norm_ffn_g": tot[2], "norm_final_g": tot[3],
        "lb_param": _lb_grad(tot[4], lb), "gnorm_g": tot[5],
        "conv_w": lax.dynamic_slice(tot[6].reshape(3, -1), (0, k_me * cw_shard), (3, cw_shard)),
    }
    loss_out = tot[7][0, 0]

    for k in _BIG:
        grads[k] = arrived[k].reshape(-1, arrived[k].shape[-1])

    delta, new_m, new_v = {}, {}, {}
    dmod_cols = lax.dynamic_slice(dmod_all, (0, k_me * ada_cols), (N_DEV, ada_cols))
    grads["w_ada"], delta["w_ada"], new_m["w_ada"], new_v["w_ada"] = _ada_update(
        c_act.T, dmod_cols, w["w_ada"], m["w_ada"], v["w_ada"])
    for k in _WEIGHTS:
        if k != "w_ada":
            grads[k], delta[k], new_m[k], new_v[k] = _adamw("adamw_" + k, w[k], grads[k], m[k], v[k])
    outs = [loss_out, dx.reshape(x.shape)]
    for t in (grads, delta, new_m, new_v):
        outs += [t[k].reshape(shapes[k]) for k in _WEIGHTS]
    return tuple(outs)
```

```python
import functools

import jax
import jax.numpy as jnp
from jax import lax
from jax.experimental import pallas as pl
from jax.experimental.pallas import tpu as pltpu

F32 = jnp.float32
BF16 = jnp.bfloat16
MESH = pl.DeviceIdType.MESH

EPS = 1e-6
HEAD = 128
BLK = 16
N_CHIPS = 4
N_DEV = 8
SUBLANES, LANES = 8, 128
VMEM_LIMIT_BYTES = 56 * 1024 * 1024

ADAM_LR = 0.001
ADAM_B1 = 0.9
ADAM_B2 = 0.999
ADAM_EPS = 1e-08
ADAM_WD = 0.01
ADAM_STEP = 10


def _pick(dim, pref, mult):
    if dim <= pref:
        return dim
    t = (pref // mult) * mult
    while t >= mult:
        if dim % t == 0:
            return t
        t -= mult
    return dim


def _sig(x):
    return 1.0 / (1.0 + jnp.exp(-x))


def _params(sem):
    return pltpu.CompilerParams(dimension_semantics=sem, vmem_limit_bytes=VMEM_LIMIT_BYTES)


def _gj(j, i, k):
    return j


def _gk(j, i, k):
    return k


def _wspec(arr, rt, ct, r_of, c_of):
    if arr.ndim == 2:
        return pl.BlockSpec((rt, ct), lambda j, i, k: (r_of(j, i, k), c_of(j, i, k)))
    per = arr.shape[2] // ct
    assert arr.shape[2] % ct == 0
    return pl.BlockSpec((None, rt, ct),
                        lambda j, i, k: (c_of(j, i, k) // per, r_of(j, i, k), c_of(j, i, k) % per))


_HBM = pl.BlockSpec(memory_space=pltpu.HBM)


def _host(job, grid, in_specs, args, out_specs, out_shape, scratch):
    if job is None:
        return {}, (lambda body: body)
    n_in, n_out, n_scr = len(args), len(out_shape), len(scratch)
    n_job, n_alias = len(job.inputs), job.n_alias
    in_specs += [_HBM] * n_job
    args += job.inputs
    out_specs += [_HBM] * n_alias
    out_shape += [jax.ShapeDtypeStruct(a.shape, a.dtype) for a in job.inputs[n_job - n_alias:]]
    scratch += job.sem_shapes
    aliases = {n_in + n_job - n_alias + t: n_out + t for t in range(n_alias)}

    def wrap(body):
        def hosted(*refs):
            ins, refs = refs[:n_in], refs[n_in:]
            j_in, refs = refs[:n_job], refs[n_job:]
            outs, refs = refs[:n_out], refs[n_out:]
            j_out, refs = refs[:n_alias], refs[n_alias:]
            scr, sems = refs[:n_scr], refs[n_scr:]
            first = functools.reduce(jnp.logical_and, [pl.program_id(d) == 0 for d in range(len(grid))])
            last = functools.reduce(jnp.logical_and, [pl.program_id(d) == grid[d] - 1 for d in range(len(grid))])

            @pl.when(first)
            def _():
                job.start(j_in, j_out, sems)

            body(*ins, *outs, *scr)

            @pl.when(last)
            def _():
                job.finish(j_in, j_out, sems)

        return hosted

    return aliases, wrap


EPILOGUE_COLS = 768


def _plain(accs, extras):
    return accs


def _matmul(name, pairs, acc_of, M, N, K, tm, tn, tk, extras, outs, epilogue, job=None, rows_outer=False):
    assert M % tm == 0 and N % tn == 0 and K % tk == 0, (name, M, N, K, tm, tn, tk)
    nj, ni, nk = N // tn, M // tm, K // tk
    n_pairs, n_extra, n_out = len(pairs), len(extras), len(outs)
    n_acc = max(acc_of) + 1
    in_specs, args, dims = [], [], []
    lhs_of, seen = [], {}
    for a, am, b, bm in pairs:
        if (id(a), am) not in seen:
            seen[id(a), am] = len(args)
            args.append(a)
            if am == 'n':
                in_specs.append(pl.BlockSpec((tm, tk), lambda j, i, k: (i, k)))
            else:
                in_specs.append(pl.BlockSpec((tk, tm), lambda j, i, k: (k, i)))
        lhs_of.append(seen[id(a), am])
    n_lhs = len(args)
    for a, am, b, bm in pairs:
        if bm == 'n':
            in_specs.append(_wspec(b, tk, tn, _gk, _gj))
        else:
            in_specs.append(_wspec(b, tn, tk, _gj, _gk))
        dims.append((((1 if am == 'n' else 0,), (0 if bm == 'n' else 1,)), ((), ())))
        args.append(b)
    for e, kind, off in extras:
        assert off % tn == 0, (name, off, tn)
        o = off // tn
        if kind == 'tile':
            in_specs.append(pl.BlockSpec((tm, tn), lambda j, i, k, o=o: (i, j + o)))
        else:
            in_specs.append(pl.BlockSpec((1, tn), lambda j, i, k, o=o: (0, j + o)))
        args.append(e)
    out_shape, out_specs = [], []
    for dt, nb in outs:
        if nb is None:
            out_shape.append(jax.ShapeDtypeStruct((M, N), dt))
            out_specs.append(pl.BlockSpec((tm, tn), lambda j, i, k: (i, j)))
        else:
            assert nb % tn == 0 and N % nb == 0
            per = nb // tn
            out_shape.append(jax.ShapeDtypeStruct((N // nb, M, nb), dt))
            out_specs.append(pl.BlockSpec((None, tm, tn), lambda j, i, k, per=per: (j // per, i, j % per)))

    def body(*refs):
        n_ab = n_lhs + n_pairs
        e_refs = refs[n_ab:n_ab + n_extra]
        o_refs = refs[n_ab + n_extra:n_ab + n_extra + n_out]
        acc_refs = refs[n_ab + n_extra + n_out:]

        def products(cols):
            sums = [None] * n_acc
            for p in range(n_pairs):
                b_ref = refs[n_lhs + p]
                b = b_ref[:, cols] if pairs[p][3] == 'n' else b_ref[cols, :]
                prod = lax.dot_general(refs[lhs_of[p]][...], b, dims[p], preferred_element_type=F32)
                a = acc_of[p]
                sums[a] = prod if sums[a] is None else sums[a] + prod
            return sums

        def finish(accs, cols):
            res = epilogue(accs, [e[:, cols] for e in e_refs])
            for o_ref, r in zip(o_refs, res):
                o_ref[:, cols] = r.astype(o_ref.dtype)

        whole = slice(0, tn)
        if nk == 1:
            step = tn if epilogue is _plain else EPILOGUE_COLS
            for c0 in range(0, tn, step):
                cols = slice(c0, min(c0 + step, tn))
                finish(products(cols), cols)
            return
        sums = products(whole)
        k = pl.program_id(2)
        targets = o_refs if sum_in_outs else acc_refs

        @pl.when(k == 0)
        def _():
            for a in range(n_acc):
                targets[a][...] = sums[a]

        @pl.when(k > 0)
        def _():
            for a in range(n_acc):
                targets[a][...] += sums[a]

        if not sum_in_outs:
            @pl.when(k == nk - 1)
            def _():
                finish([acc_refs[a][...] for a in range(n_acc)], whole)

    sum_in_outs = epilogue is _plain and nk > 1 and n_out == n_acc and all(dt == F32 for dt, _ in outs)
    scratch = [pltpu.VMEM((tm, tn), F32) for _ in range(n_acc)] if nk > 1 and not sum_in_outs else []
    grid = (nj, ni, nk)
    if rows_outer:
        grid = (ni, nj, nk)
        swap = lambda spec: pl.BlockSpec(spec.block_shape, lambda i, j, k, f=spec.index_map: f(j, i, k))
        in_specs, out_specs = [swap(s) for s in in_specs], [swap(s) for s in out_specs]
    aliases, wrap = _host(job, grid, in_specs, args, out_specs, out_shape, scratch)
    sem = ("parallel", "parallel", "arbitrary") if job is None else ("arbitrary",) * 3
    return pl.pallas_call(
        wrap(body), name=name, grid=grid, in_specs=in_specs, out_specs=out_specs, out_shape=out_shape,
        scratch_shapes=scratch, input_output_aliases=aliases, compiler_params=_params(sem),
    )(*args)


def _row_spec(tr, d):
    return pl.BlockSpec((tr, d), lambda i: (i, 0))


def _vec_spec(d):
    return pl.BlockSpec((1, d), lambda i: (0, 0))


def _norm_mod(name, x, g, sc, sh, job=None):
    S, D = x.shape
    tr = _pick(S, 256, 8)

    def body(x_ref, g_ref, sc_ref, sh_ref, h_ref):
        xv = x_ref[...]
        r = lax.rsqrt(jnp.mean(xv * xv, axis=-1, keepdims=True) + EPS)
        h_ref[...] = ((xv * r) * g_ref[...] * (1.0 + sc_ref[...]) + sh_ref[...]).astype(BF16)

    grid = (S // tr,)
    in_specs = [_row_spec(tr, D), _vec_spec(D), _vec_spec(D), _vec_spec(D)]
    args = [x, g, sc, sh]
    out_specs, out_shape, scratch = [_row_spec(tr, D)], [jax.ShapeDtypeStruct((S, D), BF16)], []
    aliases, wrap = _host(job, grid, in_specs, args, out_specs, out_shape, scratch)
    res = pl.pallas_call(
        wrap(body), name=name, grid=grid, in_specs=in_specs, out_specs=out_specs, out_shape=out_shape,
        scratch_shapes=scratch, input_output_aliases=aliases,
        compiler_params=_params(("parallel" if job is None else "arbitrary",)),
    )(*args)
    return res[0] if job is None else res


def _loss_head(x2, target, g, ff, gt):
    S, D = x2.shape
    tr = _pick(S, 256, 8)

    def body(x_ref, t_ref, g_ref, ff_ref, gt_ref, dx_ref, dffb_ref, loss_ref, dgt_ref, dg_ref):
        i = pl.program_id(0)

        @pl.when(i == 0)
        def _():
            loss_ref[...] = jnp.zeros_like(loss_ref)
            dgt_ref[...] = jnp.zeros_like(dgt_ref)
            dg_ref[...] = jnp.zeros_like(dg_ref)

        xv = x_ref[...]
        gv = g_ref[...]
        r = lax.rsqrt(jnp.mean(xv * xv, axis=-1, keepdims=True) + EPS)
        xh = xv * r
        err = xh * gv - t_ref[...]
        loss_ref[...] += 0.5 * jnp.sum(jnp.mean(err * err, axis=-1, keepdims=True))
        dy = err * (1.0 / D)
        dg_ref[...] += jnp.sum(dy * xh, axis=0, keepdims=True)
        dxh = dy * gv
        dx = r * (dxh - xh * jnp.mean(dxh * xh, axis=-1, keepdims=True))
        dx_ref[...] = dx
        dffb_ref[...] = (dx * gt_ref[...]).astype(BF16)
        dgt_ref[...] += jnp.sum(dx * ff_ref[...], axis=0, keepdims=True)

    return pl.pallas_call(
        body, name="loss_head", grid=(S // tr,),
        in_specs=[_row_spec(tr, D), _row_spec(tr, D), _vec_spec(D), _row_spec(tr, D), _vec_spec(D)],
        out_specs=[_row_spec(tr, D), _row_spec(tr, D), pl.BlockSpec((1, 128), lambda i: (0, 0)),
                   _vec_spec(D), _vec_spec(D)],
        out_shape=[jax.ShapeDtypeStruct((S, D), F32), jax.ShapeDtypeStruct((S, D), BF16),
                   jax.ShapeDtypeStruct((1, 128), F32), jax.ShapeDtypeStruct((1, D), F32),
                   jax.ShapeDtypeStruct((1, D), F32)],
        compiler_params=_params(("arbitrary",)),
    )(x2, target, g, ff, gt)


def _norm_mod_bwd(name, dh, x, g, sc, dres, gated=None, job=None):
    S, D = x.shape
    tr = _pick(S, 256, 8)
    n = S // tr
    with_gate = gated is not None

    def body(*refs):
        if with_gate:
            dh_ref, x_ref, g_ref, sc_ref, dres_ref, mo_ref, gt_ref, dx_ref, dsh_ref, dsc_ref, dg_ref, dmob_ref, dgt_ref = refs
        else:
            dh_ref, x_ref, g_ref, sc_ref, dres_ref, dx_ref, dsh_ref, dsc_ref, dg_ref = refs
        i = pl.program_id(0)

        @pl.when(i == 0)
        def _():
            dsh_ref[...] = jnp.zeros_like(dsh_ref)
            dsc_ref[...] = jnp.zeros_like(dsc_ref)
            if with_gate:
                dgt_ref[...] = jnp.zeros_like(dgt_ref)

        xv = x_ref[...]
        dhv = dh_ref[...]
        gv = g_ref[...]
        scale = 1.0 + sc_ref[...]
        r = lax.rsqrt(jnp.mean(xv * xv, axis=-1, keepdims=True) + EPS)
        xh = xv * r
        dsh_ref[...] += jnp.sum(dhv, axis=0, keepdims=True)
        dsc_ref[...] += jnp.sum(dhv * xh, axis=0, keepdims=True)
        dxh = dhv * (gv * scale)
        dx = dres_ref[...] + r * (dxh - xh * jnp.mean(dxh * xh, axis=-1, keepdims=True))
        dx_ref[...] = dx
        if with_gate:
            dmob_ref[...] = (dx * gt_ref[...]).astype(BF16)
            dgt_ref[...] += jnp.sum(dx * mo_ref[...], axis=0, keepdims=True)

        @pl.when(i == n - 1)
        def _():
            t = dsc_ref[...]
            dg_ref[...] = t * scale
            dsc_ref[...] = t * gv

    in_specs = [_row_spec(tr, D), _row_spec(tr, D), _vec_spec(D), _vec_spec(D), _row_spec(tr, D)]
    args = [dh, x, g, sc, dres]
    out_specs = [_row_spec(tr, D), _vec_spec(D), _vec_spec(D), _vec_spec(D)]
    out_shape = [jax.ShapeDtypeStruct((S, D), F32)] + [jax.ShapeDtypeStruct((1, D), F32)] * 3
    if with_gate:
        in_specs += [_row_spec(tr, D), _vec_spec(D)]
        args += list(gated)
        out_specs += [_row_spec(tr, D), _vec_spec(D)]
        out_shape += [jax.ShapeDtypeStruct((S, D), BF16), jax.ShapeDtypeStruct((1, D), F32)]
    scratch = []
    aliases, wrap = _host(job, (n,), in_specs, args, out_specs, out_shape, scratch)
    return pl.pallas_call(
        wrap(body), name=name, grid=(n,), in_specs=in_specs, out_specs=out_specs, out_shape=out_shape,
        scratch_shapes=scratch, input_output_aliases=aliases, compiler_params=_params(("arbitrary",)),
    )(*args)


CONV_ROWS = 256


def _col_spec(S, off_cols):
    o = off_cols // HEAD
    return pl.BlockSpec((S, HEAD), lambda c, o=o: (0, c + o))


def _conv_taps(u, u_prev, rid):
    s1 = jnp.where(rid >= 1, pltpu.roll(u, 1, 0), pltpu.roll(u_prev, 1, 0))
    s2 = jnp.where(rid >= 2, pltpu.roll(u, 2, 0), pltpu.roll(u_prev, 2, 0))
    return s1, s2


def _conv_fwd(proj, conv_w, d_conv, job=None):
    S = proj.shape[0]
    R = min(CONV_ROWS, S)
    nr = S // R

    def body(ab_ref, ac_ref, ax_ref, w_ref, ya_ref):
        w0, w1, w2 = w_ref[0:1, :], w_ref[1:2, :], w_ref[2:3, :]
        rid = lax.broadcasted_iota(jnp.int32, (R, HEAD), 0)

        def step(c, carry):
            rows = pl.ds(pl.multiple_of(c * R, R), R)
            prev = pl.ds(pl.multiple_of(jnp.maximum(c - 1, 0) * R, R), R)
            u = ac_ref[rows, :] * ax_ref[rows, :]
            u_prev = jnp.where(c > 0, ac_ref[prev, :] * ax_ref[prev, :], 0.0)
            s1, s2 = _conv_taps(u, u_prev, rid)
            y = w0 * s2 + w1 * s1 + w2 * u
            ya_ref[rows, :] = (ab_ref[rows, :] * y).astype(BF16)
            return carry

        lax.fori_loop(0, nr, step, 0)

    grid = (d_conv // HEAD,)
    in_specs = [_col_spec(S, 0), _col_spec(S, d_conv), _col_spec(S, 2 * d_conv), pl.BlockSpec((3, HEAD), lambda c: (0, c))]
    args = [proj, proj, proj, conv_w]
    out_specs, out_shape = [pl.BlockSpec((S, HEAD), lambda c: (0, c))], [jax.ShapeDtypeStruct((S, d_conv), BF16)]
    scratch = []
    aliases, wrap = _host(job, grid, in_specs, args, out_specs, out_shape, scratch)
    res = pl.pallas_call(
        wrap(body), name="conv_fwd", grid=grid, in_specs=in_specs, out_specs=out_specs, out_shape=out_shape,
        scratch_shapes=scratch, input_output_aliases=aliases,
        compiler_params=_params(("parallel" if job is None else "arbitrary",)),
    )(*args)
    return res[0] if job is None else res


def _conv_bwd(dya, proj, conv_w, d_conv, job=None):
    S = proj.shape[0]
    R = min(CONV_ROWS, S)
    nr = S // R

    def body(dya_ref, ab_ref, ac_ref, ax_ref, w_ref, dab_ref, dac_ref, dax_ref, dw_ref):
        w0, w1, w2 = w_ref[0:1, :], w_ref[1:2, :], w_ref[2:3, :]
        rid = lax.broadcasted_iota(jnp.int32, (R, HEAD), 0)

        def step(c, carry):
            dw0, dw1, dw2 = carry
            rows = pl.ds(pl.multiple_of(c * R, R), R)
            prev = pl.ds(pl.multiple_of(jnp.maximum(c - 1, 0) * R, R), R)
            nxt = pl.ds(pl.multiple_of(jnp.minimum(c + 1, nr - 1) * R, R), R)
            a_c, a_x, a_b = ac_ref[rows, :], ax_ref[rows, :], ab_ref[rows, :]
            u = a_c * a_x
            u_prev = jnp.where(c > 0, ac_ref[prev, :] * ax_ref[prev, :], 0.0)
            s1, s2 = _conv_taps(u, u_prev, rid)
            y = w0 * s2 + w1 * s1 + w2 * u
            dy = dya_ref[rows, :]
            dab_ref[rows, :] = (dy * y).astype(BF16)
            dyc = dy * a_b
            dyc_next = jnp.where(c < nr - 1, dya_ref[nxt, :] * ab_ref[nxt, :], 0.0)
            t1 = jnp.where(rid < R - 1, pltpu.roll(dyc, R - 1, 0), pltpu.roll(dyc_next, R - 1, 0))
            t2 = jnp.where(rid < R - 2, pltpu.roll(dyc, R - 2, 0), pltpu.roll(dyc_next, R - 2, 0))
            du = w2 * dyc + w1 * t1 + w0 * t2
            dac_ref[rows, :] = (du * a_x).astype(BF16)
            dax_ref[rows, :] = (du * a_c).astype(BF16)
            dw0 = dw0 + jnp.sum(dyc * s2, axis=0, keepdims=True)
            dw1 = dw1 + jnp.sum(dyc * s1, axis=0, keepdims=True)
            dw2 = dw2 + jnp.sum(dyc * u, axis=0, keepdims=True)
            return dw0, dw1, dw2

        z = jnp.zeros((1, HEAD), F32)
        dw0, dw1, dw2 = lax.fori_loop(0, nr, step, (z, z, z))
        dw_ref[0:1, :] = dw0
        dw_ref[1:2, :] = dw1
        dw_ref[2:3, :] = dw2

    col = pl.BlockSpec((S, HEAD), lambda c: (0, c))
    grid = (d_conv // HEAD,)
    in_specs = [col, _col_spec(S, 0), _col_spec(S, d_conv), _col_spec(S, 2 * d_conv),
                pl.BlockSpec((3, HEAD), lambda c: (0, c))]
    args = [dya, proj, proj, proj, conv_w]
    out_specs = [col, col, col, pl.BlockSpec((3, HEAD), lambda c: (0, c))]
    out_shape = [jax.ShapeDtypeStruct((S, d_conv), BF16)] * 3 + [jax.ShapeDtypeStruct((3, d_conv), F32)]
    scratch = []
    aliases, wrap = _host(job, grid, in_specs, args, out_specs, out_shape, scratch)
    return pl.pallas_call(
        wrap(body), name="conv_bwd", grid=grid, in_specs=in_specs, out_specs=out_specs, out_shape=out_shape,
        scratch_shapes=scratch, input_output_aliases=aliases,
        compiler_params=_params(("parallel" if job is None else "arbitrary",)),
    )(*args)


HGRN_FWD_ROWS = 512
HGRN_BWD_ROWS = 1024
HGRN_FWD_SUB = 64
HGRN_BWD_SUB = 32
HGRN_GATE_ROWS = 128


def _block_cumsum(x, pos):
    for s in (1, 2, 4, 8):
        x = x + jnp.where(pos >= s, pltpu.roll(x, s, 0), 0.0)
    return x


def _block_rev_cumsum(x, pos, rows):
    for s in (1, 2, 4, 8):
        x = x + jnp.where(pos < BLK - s, pltpu.roll(x, rows - s, 0), 0.0)
    return x


def _block_last(x, pos, rows):
    m = jnp.where(pos == BLK - 1, x, 0.0)
    for s in (1, 2, 4, 8):
        m = m + pltpu.roll(m, rows - s, 0)
    return m


def _hgrn_gates(q, z, lb):
    sgq = _sig(q)
    qs = q * sgq
    sg = _sig(z)
    f = lb + (1.0 - lb) * sg
    kk = (1.0 - lb) * (1.0 - sg)
    return sgq, qs, sg, f, kk


def _hgrn_decays(q, z, lb, pos, rows):
    sgq, qs, sg, f, kk = _hgrn_gates(q, z, lb)
    b = _block_cumsum(jnp.log(f), pos)
    return sgq, qs, sg, f, kk, b, _block_last(b, pos, rows)


def _hgrn_specs(T, d_conv, n_t, rev):
    def t_of(i):
        return (n_t - 1 - i) if rev else i

    def pcol(off):
        o = off // HEAD
        return pl.BlockSpec((T, HEAD), lambda h, i, o=o: (t_of(i), h + o))

    q_spec, z_spec, v_spec, g_spec = pcol(3 * d_conv), pcol(4 * d_conv), pcol(5 * d_conv), pcol(6 * d_conv)
    lb_spec = pl.BlockSpec((1, HEAD), lambda h, i: (0, h))
    gn_spec = pl.BlockSpec((1, HEAD), lambda h, i: (0, 0))
    act_spec = pl.BlockSpec((T, HEAD), lambda h, i: (t_of(i), h))
    st_spec = pl.BlockSpec((None, T // BLK, HEAD, HEAD), lambda h, i: (h, t_of(i), 0, 0))
    return q_spec, z_spec, v_spec, g_spec, lb_spec, gn_spec, act_spec, st_spec


def _hgrn_fwd(proj, lb, gn, d_conv, job=None):
    S = proj.shape[0]
    H = d_conv // HEAD
    T = min(HGRN_FWD_ROWS, S)
    n_t, nb = S // T, T // BLK
    sub = min(HGRN_FWD_SUB, T)
    q_spec, z_spec, v_spec, g_spec, lb_spec, gn_spec, act_spec, st_spec = _hgrn_specs(T, d_conv, n_t, False)

    def body(q_ref, z_ref, v_ref, g_ref, lb_ref, gn_ref, ob_ref, o_ref, st_ref, qe_s, ke_s, dec_s, state, v_s, o_s, u_s):
        @pl.when(pl.program_id(1) == 0)
        def _():
            state[...] = jnp.zeros_like(state)

        pos = lax.broadcasted_iota(jnp.int32, (sub, HEAD), 0) % BLK
        lbv = lb_ref[...]

        def vector_pass(s, carry):
            r = pl.ds(pl.multiple_of(s * sub, sub), sub)
            v = v_ref[r, :]
            _, qs, _, _, kk, b, b_tot = _hgrn_decays(q_ref[r, :], z_ref[r, :], lbv, pos, sub)
            qe_s[r, :] = (qs * jnp.exp(b)).astype(BF16)
            ke_s[r, :] = (kk * jnp.exp(b_tot - b)).astype(BF16)
            v_s[r, :] = v.astype(BF16)
            dec_s[r, :] = jnp.exp(b_tot)
            o = jnp.sum(qs * kk, axis=-1, keepdims=True) * v
            for d in range(1, BLK):
                e = jnp.exp(b - pltpu.roll(b, d, 0))
                a = jnp.sum(qs * pltpu.roll(kk, d, 0) * e, axis=-1, keepdims=True)
                o = o + jnp.where(pos >= d, a * pltpu.roll(v, d, 0), 0.0)
            o_s[r, :] = o
            return carry

        lax.fori_loop(0, T // sub, vector_pass, 0)

        for j in range(nb):
            rows = pl.ds(j * BLK, BLK)
            u_s[j] = lax.dot_general(v_s[rows, :], ke_s[rows, :], (((0,), (0,)), ((), ())),
                                     preferred_element_type=F32)
        st = state[...]
        for j in range(nb):
            st_ref[j] = st.astype(BF16)
            st = st * dec_s[pl.ds(j * BLK, 1), :] + u_s[j]
        state[...] = st
        for j in range(nb):
            rows = pl.ds(j * BLK, BLK)
            o_s[rows, :] += lax.dot_general(qe_s[rows, :], st_ref[j], (((1,), (1,)), ((), ())),
                                            preferred_element_type=F32)
        o = o_s[...]
        o_ref[...] = o
        r = lax.rsqrt(jnp.mean(o * o, axis=-1, keepdims=True) + EPS)
        g = g_ref[...]
        ob_ref[...] = ((o * r) * gn_ref[...] * (g * _sig(g))).astype(BF16)

    grid = (H, n_t)
    in_specs = [q_spec, z_spec, v_spec, g_spec, lb_spec, gn_spec]
    args = [proj, proj, proj, proj, lb, gn]
    out_specs = [act_spec, act_spec, st_spec, act_spec, act_spec, act_spec]
    out_shape = [jax.ShapeDtypeStruct((S, d_conv), BF16), jax.ShapeDtypeStruct((S, d_conv), F32),
                 jax.ShapeDtypeStruct((H, S // BLK, HEAD, HEAD), BF16),
                 jax.ShapeDtypeStruct((S, d_conv), BF16), jax.ShapeDtypeStruct((S, d_conv), BF16),
                 jax.ShapeDtypeStruct((S, d_conv), F32)]
    scratch = [pltpu.VMEM((HEAD, HEAD), F32), pltpu.VMEM((T, HEAD), BF16), pltpu.VMEM((T, HEAD), F32),
               pltpu.VMEM((nb, HEAD, HEAD), F32)]
    aliases, wrap = _host(job, grid, in_specs, args, out_specs, out_shape, scratch)
    return pl.pallas_call(
        wrap(body), name="hgrn_fwd", grid=grid, in_specs=in_specs, out_specs=out_specs, out_shape=out_shape,
        scratch_shapes=scratch, input_output_aliases=aliases,
        compiler_params=_params(("parallel" if job is None else "arbitrary", "arbitrary")),
    )(*args)


def _hgrn_bwd(dob, o_raw, states, qe, ke, dec, proj, lb, gn, d_conv, job=None):
    S = proj.shape[0]
    H = d_conv // HEAD
    T = min(HGRN_BWD_ROWS, S)
    n_t, nb = S // T, T // BLK
    sub = min(HGRN_BWD_SUB, T)
    gate_rows = min(HGRN_GATE_ROWS, T)
    q_spec, z_spec, v_spec, g_spec, lb_spec, gn_spec, act_spec, st_spec = _hgrn_specs(T, d_conv, n_t, True)

    def body(dob_ref, o_ref, st_ref, qe_b, ke_b, dec_s, q_ref, z_ref, v_ref, g_ref, lb_ref, gn_ref,
             dq_ref, dz_ref, dv_ref, dg_ref, dlb_ref, dgn_ref,
             dstate, do_b, v_b, dqe_s, dke_s, dvi_s, ddec_s, do_s, u_s, ds_s):
        @pl.when(pl.program_id(1) == 0)
        def _():
            dstate[...] = jnp.zeros_like(dstate)
            dlb_ref[...] = jnp.zeros_like(dlb_ref)
            dgn_ref[...] = jnp.zeros_like(dgn_ref)

        pos = lax.broadcasted_iota(jnp.int32, (sub, HEAD), 0) % BLK
        lbv, gnv = lb_ref[...], gn_ref[...]

        def before(s, carry):
            r = pl.ds(pl.multiple_of(s * gate_rows, gate_rows), gate_rows)
            g = g_ref[r, :]
            v_b[r, :] = v_ref[r, :].astype(BF16)
            o = o_ref[r, :]
            rs = lax.rsqrt(jnp.mean(o * o, axis=-1, keepdims=True) + EPS)
            on = o * rs
            sgg = _sig(g)
            dobv = dob_ref[r, :]
            dog = dobv * (g * sgg)
            dgn_ref[...] += jnp.sum(dog * on, axis=0, keepdims=True)
            don = dog * gnv
            do = rs * (don - on * jnp.mean(don * on, axis=-1, keepdims=True))
            dg_ref[r, :] = (dobv * (on * gnv) * (sgg * (1.0 + g * (1.0 - sgg)))).astype(BF16)
            do_s[r, :] = do
            do_b[r, :] = do.astype(BF16)
            return carry

        lax.fori_loop(0, T // gate_rows, before, 0)

        for j in range(nb):
            rows = pl.ds(j * BLK, BLK)
            dob_j = do_b[rows, :]
            u_s[j] = lax.dot_general(dob_j, qe_b[rows, :], (((0,), (0,)), ((), ())), preferred_element_type=F32)
            dqe_s[rows, :] = lax.dot_general(dob_j, st_ref[j], (((1,), (0,)), ((), ())), preferred_element_type=F32)
        dst = dstate[...]
        for j in reversed(range(nb)):
            ds_s[j] = dst.astype(BF16)
            ddec = jnp.sum(st_ref[j].astype(F32) * dst, axis=0, keepdims=True)
            ddec_s[pl.ds(j * BLK, BLK), :] = jnp.broadcast_to(ddec, (BLK, HEAD))
            dst = dst * dec_s[pl.ds(j * BLK, 1), :] + u_s[j]
        dstate[...] = dst
        for j in range(nb):
            rows = pl.ds(j * BLK, BLK)
            dke_s[rows, :] = lax.dot_general(v_b[rows, :], ds_s[j], (((1,), (0,)), ((), ())), preferred_element_type=F32)
            dvi_s[rows, :] = lax.dot_general(ke_b[rows, :], ds_s[j], (((1,), (1,)), ((), ())), preferred_element_type=F32)

        def after(s, carry):
            r = pl.ds(pl.multiple_of(s * sub, sub), sub)
            q, v = q_ref[r, :], v_ref[r, :]
            sgq, qs, sg, f, kk, b, b_tot = _hgrn_decays(q, z_ref[r, :], lbv, pos, sub)
            do = do_s[r, :]
            dqs = dqe_s[r, :] * jnp.exp(b)
            dkk = dke_s[r, :] * jnp.exp(b_tot - b)
            d_tot = jnp.where(pos == BLK - 1, _block_cumsum(dkk * kk, pos) + ddec_s[r, :] * jnp.exp(b_tot), 0.0)
            dv = dvi_s[r, :]
            da = jnp.sum(do * v, axis=-1, keepdims=True)
            dv = dv + jnp.sum(qs * kk, axis=-1, keepdims=True) * do
            dqs = dqs + da * kk
            dkk = dkk + da * qs
            for d in range(1, BLK):
                kd = pltpu.roll(kk, d, 0)
                e = jnp.where(pos >= d, jnp.exp(b - pltpu.roll(b, d, 0)), 0.0)
                a = jnp.sum(qs * kd * e, axis=-1, keepdims=True)
                da = jnp.sum(do * pltpu.roll(v, d, 0), axis=-1, keepdims=True)
                gq = da * e
                dqs = dqs + gq * kd
                dkk = dkk + pltpu.roll(gq * qs, sub - d, 0)
                dv = dv + pltpu.roll(a * do, sub - d, 0)
            db = qs * dqs - kk * dkk + d_tot
            dlf = _block_rev_cumsum(db, pos, sub)
            df = dlf / f - dkk
            dlb_ref[...] += jnp.sum(df * (1.0 - sg), axis=0, keepdims=True)
            dz_ref[r, :] = (df * (1.0 - lbv) * sg * (1.0 - sg)).astype(BF16)
            dq_ref[r, :] = (dqs * (sgq * (1.0 + q * (1.0 - sgq)))).astype(BF16)
            dv_ref[r, :] = dv.astype(BF16)
            return carry

        lax.fori_loop(0, T // sub, after, 0)

    tb = lambda dt: pltpu.VMEM((T, HEAD), dt)
    grid = (H, n_t)
    in_specs = [act_spec, act_spec, st_spec, act_spec, act_spec, act_spec, q_spec, z_spec, v_spec, g_spec, lb_spec,
                gn_spec]
    args = [dob, o_raw, states, qe, ke, dec, proj, proj, proj, proj, lb, gn]
    out_specs = [act_spec, act_spec, act_spec, act_spec, lb_spec, pl.BlockSpec((None, 1, HEAD), lambda h, i: (h, 0, 0))]
    out_shape = ([jax.ShapeDtypeStruct((S, d_conv), BF16)] * 4
                 + [jax.ShapeDtypeStruct((1, d_conv), F32), jax.ShapeDtypeStruct((H, 1, HEAD), F32)])
    scratch = [pltpu.VMEM((HEAD, HEAD), F32), tb(BF16), tb(BF16), tb(F32), tb(F32), tb(F32), tb(F32), tb(F32),
               pltpu.VMEM((nb, HEAD, HEAD), F32), pltpu.VMEM((nb, HEAD, HEAD), BF16)]
    aliases, wrap = _host(job, grid, in_specs, args, out_specs, out_shape, scratch)
    return pl.pallas_call(
        wrap(body), name="hgrn_bwd", grid=grid, in_specs=in_specs, out_specs=out_specs, out_shape=out_shape,
        scratch_shapes=scratch, input_output_aliases=aliases,
        compiler_params=_params(("parallel" if job is None else "arbitrary", "arbitrary")),
    )(*args)


def _local_step(x, target, mod, norm_mix_g, lb, gnorm_g, norm_ffn_g, norm_final_g, conv_w, bufs, c_idx, order):
    S, D = x.shape
    d_conv = D // 2
    d_in = 7 * d_conv + 2 * D
    d_ff = N_CHIPS * bufs["w_ffn_down"].shape[1]
    nb_in, nb_co, nb_ff = bufs["w_in"].shape[2], bufs["w_conv_out"].shape[2], bufs["w_ffn_gate"].shape[2]
    rows = lambda g: g.reshape(-1, g.shape[2])
    sh_m, sc_m, gt_m, sh_f, sc_f, gt_f = [mod[:, i * D:(i + 1) * D] for i in range(6)]
    tm = _pick(S, 512, 16)
    tm2 = _pick(S, 1024, 16)
    tn_in = _pick(nb_in, 1408, 128)
    tn_co = _pick(nb_co, 512, 128)
    tn_ff = _pick(nb_ff, 1408, 128)
    tn_d = _pick(D, 512, 128)
    tw = _pick(D, 1024, 128)
    tg = _pick(D, 512, 128)

    h = _norm_mod("norm_mix", x, norm_mix_g, sc_m, sh_m)
    proj, w_in = _proj_gather(h, bufs["w_in"], order)
    ob, o_raw, states, qe, ke, dec, *got = _hgrn_fwd(
        proj, lb, gnorm_g, d_conv,
        job=_GatherJob([bufs[k] for k in ("w_conv_out", "w_hgrn_out", "w_o", "w_ffn_gate")]))
    ya, w_conv_out, w_hgrn_out, w_o, w_ffn_gate, w_ffn_up = _conv_fwd(
        proj, conv_w, d_conv, job=_Jobs([_ForwardJob(got), _GatherJob([bufs["w_ffn_up"]], 0, 2)]))
    w_o = rows(w_o)

    def merge_epi(accs, ex):
        y_a, y_b = accs
        return [y_a, y_b, _sig(ex[0]) * y_a + _sig(ex[1]) * y_b]

    y_a, y_b, merged, w_ffn_up = _matmul(
        "branch_out", [(ya, 'n', w_conv_out, 'n'), (ob, 'n', w_hgrn_out, 'n')], [0, 1], S, D, d_conv, tm2, tn_co, d_conv,
        [(proj, 'tile', 7 * d_conv), (proj, 'tile', 7 * d_conv + D)], [(BF16, None), (BF16, None), (BF16, None)], merge_epi,
        job=_GatherJob([w_ffn_up], 2, 5), rows_outer=True)

    def resid_epi(accs, ex):
        return [accs[0], ex[0] + ex[1] * accs[0]]

    mo, x1, w_ffn_up = _matmul("w_o", [(merged, 'n', w_o, 'n')], [0], S, D, D, tm2, tw, D,
                               [(x, 'tile', 0), (gt_m, 'row', 0)], [(BF16, None), (F32, None)], resid_epi,
                               job=_GatherJob([w_ffn_up], 5, 8))
    h2, w_ffn_up = _norm_mod("norm_ffn", x1, norm_ffn_g, sc_f, sh_f, job=_ForwardJob([w_ffn_up]))

    def swiglu_epi(accs, ex):
        gg, uu = accs
        return [gg, uu, gg * _sig(gg) * uu]

    G, U, act, w_ffn_down = _matmul(
        "ffn_in", [(h2, 'n', w_ffn_gate, 'n'), (h2, 'n', w_ffn_up, 'n')], [0, 1], S, d_ff, D,
        tm2, tn_ff, D, [], [(BF16, None), (BF16, None), (BF16, None)], swiglu_epi, job=_GatherJob([bufs["w_ffn_down"]]))
    w_ffn_down = rows(_forward_halves("forward_down", [w_ffn_down])[0])
    ff, x2 = _matmul("ffn_out", [(act, 'n', w_ffn_down, 'n')], [0], S, D, d_ff, tm2, tn_d, d_ff,
                     [(x1, 'tile', 0), (gt_f, 'row', 0)], [(BF16, None), (F32, None)], resid_epi, rows_outer=True)

    dx2, dffb, loss, dgt_f, dg_final = _loss_head(x2, target, norm_final_g, ff, gt_f)

    def swiglu_bwd_epi(accs, ex):
        dact, gg, uu = accs[0], ex[0], ex[1]
        s = _sig(gg)
        return [dact * uu * (s * (1.0 + gg * (1.0 - s))), dact * (gg * s)]

    dG, dU = _matmul("d_act", [(dffb, 'n', w_ffn_down, 't')], [0], S, d_ff, D, tm2, tn_ff, D,
                     [(G, 'tile', 0), (U, 'tile', 0)], [(BF16, None), (BF16, None)], swiglu_bwd_epi)
    dw_down, = _matmul("dw_ffn_down", [(act, 't', dffb, 'n')], [0], d_ff, D, S, _pick(d_ff, 512, 128),
                       D, S, [], [(BF16, None)], _plain)
    dh2, = _matmul("d_h2", [(dG, 'n', w_ffn_gate, 't'), (dU, 'n', w_ffn_up, 't')], [0, 0], S, D, d_ff,
                   tm, D, tn_ff, [], [(F32, None)], _plain)
    dw_gate, = _matmul("dw_ffn_gate", [(h2, 't', dG, 'n')], [0], D, d_ff, S, tg, tn_ff, S, [], [(BF16, nb_ff)], _plain)
    dw_up, = _matmul("dw_ffn_up", [(h2, 't', dU, 'n')], [0], D, d_ff, S, tg, tn_ff, S, [], [(BF16, nb_ff)], _plain)

    ck_idx = jnp.stack([c_idx[0], order[0]])

    def pair_sums(names, grads, from_sibling):
        pairs = [_pair_sum("pair_sum_" + k, g, q, ck_idx) for k, g, q in zip(names, grads, from_sibling)]
        return [p[0] for p in pairs], [p[1] for p in pairs]

    ffn_names = ["w_ffn_down", "w_ffn_gate", "w_ffn_up"]
    ffn_grads = [dw_down.reshape(N_CHIPS, -1, D), dw_gate, dw_up]
    dx1, dsh_f, dsc_f, dg_ffn, dmob, dgt_m, *from_sibling = _norm_mod_bwd(
        "norm_ffn_bwd", dh2, x1, norm_ffn_g, sc_f, dx2, (mo, gt_m), job=_SwapJob(ffn_grads))
    parts_f, slots_f = pair_sums(ffn_names, ffn_grads, from_sibling)

    def merge_bwd_epi(accs, ex):
        dm, ga, gb, ya_, yb_ = accs[0], ex[0], ex[1], ex[2], ex[3]
        sa, sb = _sig(ga), _sig(gb)
        return [dm * ya_ * sa * (1.0 - sa), dm * yb_ * sb * (1.0 - sb), dm * sa, dm * sb]

    dga, dgb, dy_a, dy_b = _matmul(
        "d_merged", [(dmob, 'n', w_o, 't')], [0], S, D, D, tm2, tn_co, D,
        [(proj, 'tile', 7 * d_conv), (proj, 'tile', 7 * d_conv + D), (y_a, 'tile', 0), (y_b, 'tile', 0)],
        [(BF16, None)] * 4, merge_bwd_epi, rows_outer=True)
    dw_o, = _matmul("dw_o", [(merged, 't', dmob, 'n')], [0], D, D, S, tg, D, S, [], [(BF16, None)], _plain)
    dya, dob = _matmul("d_branch", [(dy_a, 'n', w_conv_out, 't'), (dy_b, 'n', w_hgrn_out, 't')], [0, 1], S, d_conv, D,
                       tm2, _pick(d_conv, 1024, 128), tn_co, [], [(F32, None), (F32, None)], _plain)
    dw_co, dw_ho = _matmul("dw_branch", [(ya, 't', dy_a, 'n'), (ob, 't', dy_b, 'n')], [0, 1], d_conv, D, S,
                           _pick(d_conv, 512, 128), tn_co, S, [], [(BF16, nb_co), (BF16, nb_co)], _plain)
    branch_names = ["w_conv_out", "w_hgrn_out", "w_o"]
    branch_grads = [dw_co, dw_ho, dw_o.reshape(N_CHIPS, -1, D)]
    dab, dac, dax, dconv_w, *from_sibling = _conv_bwd(dya, proj, conv_w, d_conv, job=_SwapJob(branch_grads))
    parts_b, slots_b = pair_sums(branch_names, branch_grads, from_sibling)
    dq, dz, dv, dgo, dlb, dgn, *arrived_f = _hgrn_bwd(dob, o_raw, states, qe, ke, dec, proj, lb, gnorm_g, d_conv,
                                                      job=_ScatterJob(parts_f, slots_f))
    dproj = jnp.concatenate([dab, dac, dax, dq, dz, dv, dgo, dga, dgb], axis=1)
    halves_f = [_sum_chips("sum_chips_" + k, r, c_idx) for k, r in zip(ffn_names, arrived_f)]
    dw_in, *moved = _matmul("dw_in", [(h, 't', dproj, 'n')], [0], D, d_in, S, tg, tn_in, S, [], [(BF16, nb_in)], _plain,
                            job=_Jobs([_ShareJob(halves_f), _ScatterJob(parts_b, slots_b)]))
    whole_f, arrived_b = moved[:len(ffn_names)], moved[len(ffn_names):]
    parts_i, slots_i = pair_sums(["w_in"], [dw_in], _swap_halves("swap_in", [dw_in]))
    dh, arrived_in = _matmul("d_h", [(dproj, 'n', w_in, 't')], [0], S, D, d_in, tm2, D, tn_in, [], [(F32, None)], _plain,
                             job=_ScatterJob(parts_i, slots_i))
    dx, dsh_m, dsc_m, dg_mix = _norm_mod_bwd("norm_mix_bwd", dh, x, norm_mix_g, sc_m, dx1)
    dmod = jnp.concatenate([dsh_m, dsc_m, dgt_m, dsh_f, dsc_f, dgt_f], axis=1)
    small = dict(dmod=dmod, dg_mix=dg_mix, dg_ffn=dg_ffn, dg_final=dg_final, dlb=dlb,
                 dgn=jnp.sum(dgn, axis=0), dconv_w=dconv_w)
    late_names = branch_names + ["w_in"]
    late = _share_halves([_sum_chips("sum_chips_" + k, r, c_idx) for k, r in zip(late_names, arrived_b + [arrived_in])])
    big = dict(zip(ffn_names + late_names, whole_f + list(late)))
    return loss, dx, small, big


def _adamw_math(w, g, m, v):
    m = ADAM_B1 * m + (1.0 - ADAM_B1) * g
    v = ADAM_B2 * v + (1.0 - ADAM_B2) * (g * g)
    m_hat = m / (1.0 - ADAM_B1 ** ADAM_STEP)
    v_hat = v / (1.0 - ADAM_B2 ** ADAM_STEP)
    delta = -ADAM_LR * (m_hat / (jnp.sqrt(v_hat) + ADAM_EPS) + ADAM_WD * w)
    return delta, m, v


def _adamw(name, w, g, m, v):
    R, C = w.shape
    tr = _pick(R, max(8, (256 * 1024) // C // 8 * 8), 8)
    spec = pl.BlockSpec((tr, C), lambda i: (i, 0))

    def body(w_ref, g_ref, m_ref, v_ref, go_ref, d_ref, mo_ref, vo_ref):
        gv = g_ref[...]
        d, m2, v2 = _adamw_math(w_ref[...], gv, m_ref[...], v_ref[...])
        go_ref[...] = gv
        d_ref[...] = d
        mo_ref[...] = m2
        vo_ref[...] = v2

    return pl.pallas_call(
        body, name=name, grid=(R // tr,), in_specs=[spec] * 4, out_specs=[spec] * 4,
        out_shape=[jax.ShapeDtypeStruct((R, C), F32)] * 4, compiler_params=_params(("parallel",)),
    )(w, g, m, v)


def _ada_update(c_act_t, dmod, w, m, v):
    R, C = w.shape
    B = dmod.shape[0]
    tr = _pick(R, 256, 8)
    tc = _pick(C, 1536, 128)
    spec = pl.BlockSpec((tr, tc), lambda i, j: (i, j))

    def body(ct_ref, dm_ref, w_ref, m_ref, v_ref, g_ref, d_ref, mo_ref, vo_ref):
        ct = ct_ref[...]
        dm = dm_ref[...]
        g = ct[:, 0:1] * dm[0:1, :]
        for b in range(1, B):
            g = g + ct[:, b:b + 1] * dm[b:b + 1, :]
        d, m2, v2 = _adamw_math(w_ref[...], g, m_ref[...], v_ref[...])
        g_ref[...] = g
        d_ref[...] = d
        mo_ref[...] = m2
        vo_ref[...] = v2

    return pl.pallas_call(
        body, name="ada_update", grid=(R // tr, C // tc),
        in_specs=[pl.BlockSpec((tr, B), lambda i, j: (i, 0)), pl.BlockSpec((B, tc), lambda i, j: (0, j)),
                  spec, spec, spec],
        out_specs=[spec] * 4, out_shape=[jax.ShapeDtypeStruct((R, C), F32)] * 4,
        compiler_params=_params(("parallel", "parallel")),
    )(c_act_t, dmod, w, m, v)


def _prep(c_all, lb_param):
    def body(c_ref, p_ref, ca_ref, cab_ref, lb_ref):
        cv = c_ref[...]
        ca = cv * _sig(cv)
        ca_ref[...] = ca
        cab_ref[...] = ca.astype(BF16)
        lb_ref[...] = _sig(p_ref[0:1, :] - p_ref[1:2, :])

    return pl.pallas_call(
        body, name="prep",
        out_shape=[jax.ShapeDtypeStruct(c_all.shape, F32), jax.ShapeDtypeStruct(c_all.shape, BF16),
                   jax.ShapeDtypeStruct((1, lb_param.shape[1]), F32)],
    )(c_all, lb_param)


def _small_reduce(parts):
    W = parts.shape[1]

    def body(p_ref, o_ref):
        acc = p_ref[0:1, :]
        for d in range(1, N_DEV):
            acc = acc + p_ref[d:d + 1, :]
        o_ref[...] = acc

    return pl.pallas_call(body, name="small_reduce", out_shape=jax.ShapeDtypeStruct((1, W), F32))(parts)


def _lb_grad(dlb, lb):
    def body(d_ref, lb_ref, o_ref):
        t = d_ref[...] * lb_ref[...] * (1.0 - lb_ref[...])
        o_ref[0:1, :] = t
        o_ref[1:2, :] = -t

    return pl.pallas_call(body, name="lb_grad", out_shape=jax.ShapeDtypeStruct((2, dlb.shape[1]), F32))(dlb, lb)


def _place():
    x, y, c = lax.axis_index("x"), lax.axis_index("y"), lax.axis_index("c")
    chips = [(1 - x, y), (x, 1 - y), (1 - x, 1 - y)]
    return x, y, c, chips


def _allgather_rows(name, v):
    m_per, n = v.shape

    def body(x_ref, out_ref, send_sems, recv_sems, local_sem):
        x, y, c, chips = _place()
        me, sibling = (x, y, c), (x, y, 1 - c)

        def rows(px, py, pc):
            return out_ref.at[pl.ds((4 * px + 2 * py + pc) * m_per, m_per), :]

        def copy(k, block, to, src=None):
            return pltpu.make_async_remote_copy(
                src_ref=rows(*block) if src is None else src, dst_ref=rows(*block),
                send_sem=send_sems.at[k], recv_sem=recv_sems.at[k], device_id=to, device_id_type=MESH)

        mine = pltpu.make_async_copy(x_ref, rows(*me), local_sem)
        mine.start()
        first = [copy(0, me, sibling, src=x_ref)]
        first += [copy(1 + j, me, (*chip, c), src=x_ref) for j, chip in enumerate(chips)]
        for cp in first:
            cp.start()
        passed = [copy(4 + j, (*chip, c), sibling) for j, chip in enumerate(chips)]
        for j, chip in enumerate(chips):
            copy(1 + j, (*chip, c), me).wait_recv()
            passed[j].start()
        copy(0, sibling, me).wait_recv()
        for j, chip in enumerate(chips):
            copy(4 + j, (*chip, 1 - c), me).wait_recv()
        for cp in first + passed:
            cp.wait_send()
        mine.wait()

    return pl.pallas_call(
        body, name=name, out_shape=jax.ShapeDtypeStruct((N_DEV * m_per, n), v.dtype),
        in_specs=[pl.BlockSpec(memory_space=pltpu.VMEM)], out_specs=pl.BlockSpec(memory_space=pltpu.VMEM),
        scratch_shapes=[pltpu.SemaphoreType.DMA((7,)), pltpu.SemaphoreType.DMA((7,)), pltpu.SemaphoreType.DMA],
    )(v)


def _cast_place(name, w, k_idx):
    R, C = w.shape
    tr = _pick(R, max(16, (512 * 1024) // C // 16 * 16), 16)

    def body(k_ref, w_ref, o_ref):
        o_ref[...] = w_ref[...].astype(BF16)

    return pl.pallas_call(
        body, name=name,
        grid_spec=pltpu.PrefetchScalarGridSpec(
            num_scalar_prefetch=1, grid=(R // tr,),
            in_specs=[pl.BlockSpec((tr, C), lambda i, k_ref: (i, 0))],
            out_specs=pl.BlockSpec((None, tr, C), lambda i, k_ref: (k_ref[0], i, 0))),
        out_shape=jax.ShapeDtypeStruct((N_CHIPS, R, C), BF16),
        compiler_params=_params(("parallel",)),
    )(k_idx, w)


def _chip_copies(src_of, dst_of, got_of, n, sems, receiving):
    x, y, c, chips = _place()
    k_me = 2 * x + y
    send_sems, recv_sems = sems
    out = []
    for a in range(n):
        for j, chip in enumerate(chips):
            k_chip = 2 * chip[0] + chip[1]
            if receiving:
                src = dst = got_of(a, k_chip, c)
                to = (x, y, c)
            else:
                src, dst, to = src_of(a, k_chip, k_me, c), dst_of(a, k_me, c), (*chip, c)
            out.append(pltpu.make_async_remote_copy(
                src_ref=src, dst_ref=dst, send_sem=send_sems.at[3 * a + j], recv_sem=recv_sems.at[3 * a + j],
                device_id=to, device_id_type=MESH))
    return out


class _ChipJob:
    def start(self, j_in, j_out, sems):
        for send in self.copies(j_in, j_out, sems, False):
            send.start()

    def finish(self, j_in, j_out, sems):
        for recv in self.copies(j_in, j_out, sems, True):
            recv.wait_recv()
        for send in self.copies(j_in, j_out, sems, False):
            send.wait_send()


class _GatherJob(_ChipJob):
    def __init__(self, bufs, lo=0, hi=8):
        n = len(bufs)
        self.inputs, self.n_alias = list(bufs), n
        self.sem_shapes = [pltpu.SemaphoreType.DMA((3 * n,)), pltpu.SemaphoreType.DMA((3 * n,))]
        self.half = [b.shape[1] // 2 for b in bufs]
        self.lo, self.hi = lo, hi

    def copies(self, j_in, j_out, sems, receiving):
        def half(a, k, c):
            eighth = self.half[a] // 8
            return j_out[a].at[k, pl.ds(c * self.half[a] + self.lo * eighth, (self.hi - self.lo) * eighth)]

        return _chip_copies(lambda a, k_chip, k_me, c: half(a, k_me, c), half, half, len(self.half), sems, receiving)


class _ScatterJob(_ChipJob):
    def __init__(self, parts, slots):
        n = len(parts)
        self.n = n
        self.inputs, self.n_alias = list(parts) + list(slots), n
        self.sem_shapes = [pltpu.SemaphoreType.DMA((3 * n,)), pltpu.SemaphoreType.DMA((3 * n,))]

    def copies(self, j_in, j_out, sems, receiving):
        return _chip_copies(lambda a, k_chip, k_me, c: j_in[a].at[k_chip], lambda a, k_me, c: j_out[a].at[k_me],
                            lambda a, k_chip, c: j_out[a].at[k_chip], self.n, sems, receiving)


class _SwapJob(_ChipJob):
    def __init__(self, grads):
        n = len(grads)
        self.n = n
        self.half = [g.shape[1] // 2 for g in grads]
        landing = [lax.empty((N_CHIPS, g.shape[1] // 2, g.shape[2]), g.dtype) for g in grads]
        self.inputs, self.n_alias = list(grads) + landing, n
        self.sem_shapes = [pltpu.SemaphoreType.DMA((n,)), pltpu.SemaphoreType.DMA((n,))]

    def copies(self, j_in, j_out, sems, receiving):
        x, y, c, _ = _place()
        out = []
        for a in range(self.n):
            if receiving:
                src, to = j_out[a], (x, y, c)
            else:
                src, to = j_in[a].at[:, pl.ds((1 - c) * self.half[a], self.half[a])], (x, y, 1 - c)
            out.append(pltpu.make_async_remote_copy(src_ref=src, dst_ref=j_out[a], send_sem=sems[0].at[a],
                                                    recv_sem=sems[1].at[a], device_id=to, device_id_type=MESH))
        return out


class _Jobs:
    def __init__(self, jobs):
        self.jobs = jobs
        split = [(j.inputs[:len(j.inputs) - j.n_alias], j.inputs[len(j.inputs) - j.n_alias:]) for j in jobs]
        self.inputs = [a for plain, _ in split for a in plain] + [a for _, aliased in split for a in aliased]
        self.n_alias = sum(j.n_alias for j in jobs)
        self.sem_shapes = [s for j in jobs for s in j.sem_shapes]

    def _each(self, j_in, j_out, sems):
        n_plain = len(j_in) - self.n_alias
        p0 = a0 = s0 = 0
        for j in self.jobs:
            n_p, n_s = len(j.inputs) - j.n_alias, len(j.sem_shapes)
            ins = list(j_in[p0:p0 + n_p]) + list(j_in[n_plain + a0:n_plain + a0 + j.n_alias])
            yield j, ins, j_out[a0:a0 + j.n_alias], sems[s0:s0 + n_s]
            p0, a0, s0 = p0 + n_p, a0 + j.n_alias, s0 + n_s

    def start(self, j_in, j_out, sems):
        for j, ins, outs, s in self._each(j_in, j_out, sems):
            j.start(ins, outs, s)

    def finish(self, j_in, j_out, sems):
        for j, ins, outs, s in self._each(j_in, j_out, sems):
            j.finish(ins, outs, s)


class _ShareJob(_ChipJob):
    def __init__(self, bufs):
        n = len(bufs)
        self.n = n
        self.inputs, self.n_alias = list(bufs), n
        self.sem_shapes = [pltpu.SemaphoreType.DMA((n,)), pltpu.SemaphoreType.DMA((n,))]

    def copies(self, j_in, j_out, sems, receiving):
        x, y, c, _ = _place()
        out = []
        for a in range(self.n):
            hc, to = (1 - c, (x, y, c)) if receiving else (c, (x, y, 1 - c))
            out.append(pltpu.make_async_remote_copy(
                src_ref=j_out[a].at[hc], dst_ref=j_out[a].at[hc], send_sem=sems[0].at[a], recv_sem=sems[1].at[a],
                device_id=to, device_id_type=MESH))
        return out


class _ForwardJob(_ChipJob):
    def __init__(self, bufs):
        n = len(bufs)
        self.n = n
        self.half = [b.shape[1] // 2 for b in bufs]
        self.inputs, self.n_alias = list(bufs), n
        self.sem_shapes = [pltpu.SemaphoreType.DMA((3 * n,)), pltpu.SemaphoreType.DMA((3 * n,))]

    def copies(self, j_in, j_out, sems, receiving):
        x, y, c, chips = _place()
        out = []
        for a in range(self.n):
            for q, chip in enumerate(chips):
                hc, to = (1 - c, (x, y, c)) if receiving else (c, (x, y, 1 - c))
                part = j_out[a].at[2 * chip[0] + chip[1], pl.ds(hc * self.half[a], self.half[a])]
                out.append(pltpu.make_async_remote_copy(
                    src_ref=part, dst_ref=part, send_sem=sems[0].at[3 * a + q], recv_sem=sems[1].at[3 * a + q],
                    device_id=to, device_id_type=MESH))
        return out


def _forward_halves(name, bufs):
    n = len(bufs)

    def body(*refs):
        out_refs = refs[n:2 * n]
        send_sems, recv_sems = refs[2 * n:]
        x, y, c, chips = _place()
        started, waits = [], []
        for a in range(n):
            rh = bufs[a].shape[1] // 2
            for j, chip in enumerate(chips):
                k_chip = 2 * chip[0] + chip[1]
                got = out_refs[a].at[k_chip, pl.ds(c * rh, rh)]
                cp = pltpu.make_async_remote_copy(src_ref=got, dst_ref=got, send_sem=send_sems.at[3 * a + j],
                                                  recv_sem=recv_sems.at[3 * a + j], device_id=(x, y, 1 - c),
                                                  device_id_type=MESH)
                cp.start()
                started.append(cp)
                other = out_refs[a].at[k_chip, pl.ds((1 - c) * rh, rh)]
                waits.append(pltpu.make_async_remote_copy(
                    src_ref=other, dst_ref=other, send_sem=send_sems.at[3 * a + j], recv_sem=recv_sems.at[3 * a + j],
                    device_id=(x, y, c), device_id_type=MESH))
        for cp in waits:
            cp.wait_recv()
        for cp in started:
            cp.wait_send()

    return pl.pallas_call(
        body, name=name, out_shape=[jax.ShapeDtypeStruct(b.shape, b.dtype) for b in bufs],
        in_specs=[_HBM] * n, out_specs=[_HBM] * n, input_output_aliases={a: a for a in range(n)},
        scratch_shapes=[pltpu.SemaphoreType.DMA((3 * n,)), pltpu.SemaphoreType.DMA((3 * n,))],
    )(*bufs)


def _proj_gather(h, buf, order):
    S, D = h.shape
    nb = buf.shape[2]
    tm = _pick(S, 1024, 16)
    tn = _pick(nb, 1408, 128)
    per = nb // tn
    nj, ni = N_CHIPS * per, S // tm
    rh = D // 2
    seq = [(0, t) for t in range(per)] + [(p, t) for t in range(per) for p in (1, 2)] + [(3, t) for t in range(per)]
    tile_chip = jnp.stack([order[p] for p, _ in seq])
    tile_of = jnp.array([t for _, t in seq], jnp.int32)

    def body(chip_ref, t_ref, h_ref, buf_in, proj_ref, buf_ref, wbuf, tile_sems, ici_send, ici_recv, d2d_send, d2d_recv):
        j, i = pl.program_id(0), pl.program_id(1)
        x, y, c, chips = _place()
        k_me = 2 * x + y

        def part(k, hc, t):
            return buf_ref.at[k, pl.ds(hc * rh, rh), pl.ds(t * tn, tn)]

        def ici(q, t, receiving):
            k_chip = 2 * chips[q][0] + chips[q][1]
            src, to = (part(k_chip, c, t), (x, y, c)) if receiving else (part(k_me, c, t), (*chips[q], c))
            return pltpu.make_async_remote_copy(src_ref=src, dst_ref=src, send_sem=ici_send.at[q * per + t],
                                                recv_sem=ici_recv.at[q * per + t], device_id=to, device_id_type=MESH)

        def d2d(q, t, receiving):
            k_chip = 2 * chips[q][0] + chips[q][1]
            src, to = (part(k_chip, 1 - c, t), (x, y, c)) if receiving else (part(k_chip, c, t), (x, y, 1 - c))
            return pltpu.make_async_remote_copy(src_ref=src, dst_ref=src, send_sem=d2d_send.at[q * per + t],
                                                recv_sem=d2d_recv.at[q * per + t], device_id=to, device_id_type=MESH)

        def tile_copy(n):
            col = pl.multiple_of(t_ref[n] * tn, 128)
            return pltpu.make_async_copy(buf_ref.at[chip_ref[n], :, pl.ds(col, tn)], wbuf.at[n % 2], tile_sems.at[n % 2])

        @pl.when(jnp.logical_and(j == 0, i == 0))
        def _():
            for t in range(per):
                ici(0, t, False).start()
                ici(1, t, False).start()
            tile_copy(0).start()

        @pl.when(i == 0)
        def _():
            tile_copy(j).wait()
            for n in range(per, nj):
                p, t = seq[n]

                @pl.when(j + 1 == n)
                def _():
                    ici(p - 1, t, True).wait_recv()
                    d2d(p - 1, t, False).start()
                    if (p, t) == (1, per - 1):
                        for q in range(2):
                            for tt in range(per):
                                ici(q, tt, False).wait_send()
                        for tt in range(per):
                            ici(2, tt, False).start()
                    d2d(p - 1, t, True).wait_recv()

            @pl.when(j + 1 < nj)
            def _():
                tile_copy(j + 1).start()

        proj_ref[...] = jnp.dot(h_ref[...], wbuf[j % 2], preferred_element_type=F32)

        @pl.when(jnp.logical_and(j == nj - 1, i == ni - 1))
        def _():
            for t in range(per):
                ici(2, t, False).wait_send()
                for q in range(3):
                    d2d(q, t, False).wait_send()

    n_sems = 3 * per
    return pl.pallas_call(
        body, name="proj",
        grid_spec=pltpu.PrefetchScalarGridSpec(
            num_scalar_prefetch=2, grid=(nj, ni),
            in_specs=[pl.BlockSpec((tm, D), lambda j, i, kc, kt: (i, 0)), _HBM],
            out_specs=[pl.BlockSpec((tm, tn), lambda j, i, kc, kt: (i, kc[j] * per + kt[j])), _HBM],
            scratch_shapes=[pltpu.VMEM((2, D, tn), BF16), pltpu.SemaphoreType.DMA((2,))]
            + [pltpu.SemaphoreType.DMA((n_sems,))] * 4),
        out_shape=[jax.ShapeDtypeStruct((S, N_CHIPS * nb), F32), jax.ShapeDtypeStruct(buf.shape, buf.dtype)],
        input_output_aliases={3: 1}, compiler_params=_params(("arbitrary", "arbitrary")),
    )(tile_chip, tile_of, h, buf)


def _swap_halves(name, grads):
    n = len(grads)

    def body(*refs):
        in_refs, out_refs = refs[:n], refs[n:2 * n]
        send_sems, recv_sems = refs[2 * n:]
        x, y, c, _ = _place()
        cps = []
        for a in range(n):
            rh = grads[a].shape[1] // 2
            cp = pltpu.make_async_remote_copy(
                src_ref=in_refs[a].at[:, pl.ds((1 - c) * rh, rh)], dst_ref=out_refs[a],
                send_sem=send_sems.at[a], recv_sem=recv_sems.at[a], device_id=(x, y, 1 - c), device_id_type=MESH)
            cp.start()
            cps.append(cp)
        for cp in cps:
            cp.wait()

    return pl.pallas_call(
        body, name=name,
        out_shape=[jax.ShapeDtypeStruct((N_CHIPS, g.shape[1] // 2, g.shape[2]), g.dtype) for g in grads],
        in_specs=[_HBM] * n, out_specs=[_HBM] * n,
        scratch_shapes=[pltpu.SemaphoreType.DMA((n,)), pltpu.SemaphoreType.DMA((n,))],
    )(*grads)


def _pair_sum(name, g, q, ck_idx):
    _, R, C = g.shape
    rh = R // 2
    tr = _pick(rh, max(16, (512 * 1024) // C // 16 * 16), 16)
    nh = rh // tr

    def body(ck_ref, g_ref, q_ref, o_ref, own_ref):
        s = (g_ref[...].astype(F32) + q_ref[...].astype(F32)).astype(BF16)
        o_ref[...] = s

        @pl.when(pl.program_id(1) == ck_ref[1])
        def _():
            own_ref[...] = s

    return pl.pallas_call(
        body, name=name,
        grid_spec=pltpu.PrefetchScalarGridSpec(
            num_scalar_prefetch=1, grid=(nh, N_CHIPS),
            in_specs=[pl.BlockSpec((None, tr, C), lambda i, k, ck: (k, ck[0] * nh + i, 0)),
                      pl.BlockSpec((None, tr, C), lambda i, k, ck: (k, i, 0))],
            out_specs=[pl.BlockSpec((None, tr, C), lambda i, k, ck: (k, i, 0)),
                       pl.BlockSpec((None, tr, C), lambda i, k, ck: (ck[1], i, 0))]),
        out_shape=[jax.ShapeDtypeStruct((N_CHIPS, rh, C), BF16)] * 2,
        compiler_params=_params(("parallel", "arbitrary")),
    )(ck_idx, g, q)


def _sum_chips(name, r, c_idx):
    _, rh, C = r.shape
    tr = _pick(rh, max(16, (256 * 1024) // C // 16 * 16), 16)

    def body(c_ref, r_ref, o_ref):
        acc = r_ref[0].astype(F32)
        for k in range(1, N_CHIPS):
            acc = acc + r_ref[k].astype(F32)
        o_ref[...] = acc

    return pl.pallas_call(
        body, name=name,
        grid_spec=pltpu.PrefetchScalarGridSpec(
            num_scalar_prefetch=1, grid=(rh // tr,),
            in_specs=[pl.BlockSpec((N_CHIPS, tr, C), lambda i, c_ref: (0, i, 0))],
            out_specs=pl.BlockSpec((None, tr, C), lambda i, c_ref: (c_ref[0], i, 0))),
        out_shape=jax.ShapeDtypeStruct((2, rh, C), F32), compiler_params=_params(("parallel",)),
    )(c_idx, r)


def _share_halves(bufs):
    n = len(bufs)

    def body(*refs):
        out_refs = refs[n:2 * n]
        send_sems, recv_sems = refs[2 * n:]
        x, y, c, _ = _place()
        cps = []
        for a in range(n):
            cp = pltpu.make_async_remote_copy(
                src_ref=out_refs[a].at[c], dst_ref=out_refs[a].at[c], send_sem=send_sems.at[a],
                recv_sem=recv_sems.at[a], device_id=(x, y, 1 - c), device_id_type=MESH)
            cp.start()
            cps.append(cp)
        for a, cp in enumerate(cps):
            got = out_refs[a].at[1 - c]
            pltpu.make_async_remote_copy(src_ref=got, dst_ref=got, send_sem=send_sems.at[a], recv_sem=recv_sems.at[a],
                                         device_id=(x, y, c), device_id_type=MESH).wait_recv()
        for cp in cps:
            cp.wait_send()

    return pl.pallas_call(
        body, name="share_halves",
        out_shape=[jax.ShapeDtypeStruct(b.shape, b.dtype) for b in bufs],
        in_specs=[_HBM] * n, out_specs=[_HBM] * n, input_output_aliases={a: a for a in range(n)},
        scratch_shapes=[pltpu.SemaphoreType.DMA((n,)), pltpu.SemaphoreType.DMA((n,))],
    )(*bufs)


_BIG = ("w_in", "w_conv_out", "w_hgrn_out", "w_o", "w_ffn_gate", "w_ffn_up", "w_ffn_down")
_WEIGHTS = ("w_ada", "b_ada", "norm_mix_g", "w_in", "conv_w", "lb_param", "gnorm_g", "w_conv_out", "w_hgrn_out",
            "w_o", "norm_ffn_g", "w_ffn_gate", "w_ffn_up", "w_ffn_down", "norm_final_g")


def _pack_rows(pieces):
    flat = jnp.concatenate([p.reshape(-1) for p in pieces])
    pad = (-flat.shape[0]) % (SUBLANES * LANES)
    return jnp.pad(flat, (0, pad)).reshape(SUBLANES, -1)


def kernel(x, c, w_ada, b_ada, norm_mix_g, w_in, conv_w, lb_param, gnorm_g, w_conv_out, w_hgrn_out, w_o, norm_ffn_g, w_ffn_gate, w_ffn_up, w_ffn_down, norm_final_g, loss_target, m_w_ada, m_b_ada, m_norm_mix_g, m_w_in, m_conv_w, m_lb_param, m_gnorm_g, m_w_conv_out, m_w_hgrn_out, m_w_o, m_norm_ffn_g, m_w_ffn_gate, m_w_ffn_up, m_w_ffn_down, m_norm_final_g, v_w_ada, v_b_ada, v_norm_mix_g, v_w_in, v_conv_w, v_lb_param, v_gnorm_g, v_w_conv_out, v_w_hgrn_out, v_w_o, v_norm_ffn_g, v_w_ffn_gate, v_w_ffn_up, v_w_ffn_down, v_norm_final_g):
    w = dict(w_ada=w_ada, b_ada=b_ada, norm_mix_g=norm_mix_g, w_in=w_in, conv_w=conv_w, lb_param=lb_param,
             gnorm_g=gnorm_g, w_conv_out=w_conv_out, w_hgrn_out=w_hgrn_out, w_o=w_o, norm_ffn_g=norm_ffn_g,
             w_ffn_gate=w_ffn_gate, w_ffn_up=w_ffn_up, w_ffn_down=w_ffn_down, norm_final_g=norm_final_g)
    m = dict(w_ada=m_w_ada, b_ada=m_b_ada, norm_mix_g=m_norm_mix_g, w_in=m_w_in, conv_w=m_conv_w, lb_param=m_lb_param,
             gnorm_g=m_gnorm_g, w_conv_out=m_w_conv_out, w_hgrn_out=m_w_hgrn_out, w_o=m_w_o, norm_ffn_g=m_norm_ffn_g,
             w_ffn_gate=m_w_ffn_gate, w_ffn_up=m_w_ffn_up, w_ffn_down=m_w_ffn_down, norm_final_g=m_norm_final_g)
    v = dict(w_ada=v_w_ada, b_ada=v_b_ada, norm_mix_g=v_norm_mix_g, w_in=v_w_in, conv_w=v_conv_w, lb_param=v_lb_param,
             gnorm_g=v_gnorm_g, w_conv_out=v_w_conv_out, w_hgrn_out=v_w_hgrn_out, w_o=v_w_o, norm_ffn_g=v_norm_ffn_g,
             w_ffn_gate=v_w_ffn_gate, w_ffn_up=v_w_ffn_up, w_ffn_down=v_w_ffn_down, norm_final_g=v_norm_final_g)
    two_d = lambda a: a.reshape((-1, a.shape[-1])) if a.ndim != 2 else a
    shapes = {k: a.shape for k, a in w.items()}
    w, m, v = ({k: two_d(a) for k, a in t.items()} for t in (w, m, v))
    w["lb_param"], m["lb_param"], v["lb_param"] = lb_param, m_lb_param, v_lb_param
    S, D = x.shape[1], x.shape[2]
    d_conv = D // 2
    ix, iy, ic = lax.axis_index("x"), lax.axis_index("y"), lax.axis_index("c")
    k_me = 2 * ix + iy
    dev = 2 * k_me + ic
    cw_shard = w["conv_w"].shape[1]
    ada_cols = w["w_ada"].shape[1]

    got = _allgather_rows("gather_cond", _pack_rows([c, w["conv_w"]])).reshape(N_DEV, -1)
    c_all = got[:, :D]
    conv_full = got[::2, D:D + 3 * cw_shard].reshape(N_CHIPS, 3, cw_shard).transpose(1, 0, 2).reshape(3, -1)
    c_act, c_act_b, lb = _prep(c_all, lb_param)
    b_cols = lax.dynamic_slice(w["b_ada"], (0, k_me * ada_cols), (1, ada_cols))
    mod_cols, = _matmul("ada_fwd", [(c_act_b, 'n', w["w_ada"].astype(BF16), 'n')], [0], N_DEV, ada_cols, D,
                        N_DEV, _pick(ada_cols, 1536, 128), D, [(b_cols, 'row', 0)], [(F32, None)],
                        lambda accs, ex: [accs[0] + ex[0]])
    mod_all = _allgather_rows("gather_mod", mod_cols).reshape(N_CHIPS, 2, N_DEV, ada_cols)[:, 0]
    mod_all = mod_all.transpose(1, 0, 2).reshape(N_DEV, -1)
    mod = lax.dynamic_slice(mod_all, (dev, 0), (1, mod_all.shape[1]))
    k_idx = jnp.reshape(k_me, (1,)).astype(jnp.int32)
    c_idx = jnp.reshape(ic, (1,)).astype(jnp.int32)
    bufs = {k: _cast_place("cast_" + k, w[k], k_idx) for k in _BIG}
    order = jnp.stack([k_me, 2 * (1 - ix) + iy, 2 * ix + (1 - iy), 2 * (1 - ix) + (1 - iy)]).astype(jnp.int32)

    loss, dx, small, arrived = _local_step(
        x[0], loss_target[0], mod, w["norm_mix_g"], lb, w["gnorm_g"], w["norm_ffn_g"], w["norm_final_g"], conv_full,
        bufs, c_idx, order)

    pieces = [small["dmod"], small["dg_mix"], small["dg_ffn"], small["dg_final"], small["dlb"], small["dgn"],
              small["dconv_w"], loss]
    sizes = [p.size for p in pieces]
    parts = _allgather_rows("gather_small", _pack_rows(pieces)).reshape(N_DEV, -1)
    total = _small_reduce(parts)
    offs = [0]
    for s in sizes:
        offs.append(offs[-1] + s)
    tot = [total[:, offs[i]:offs[i + 1]] for i in range(len(sizes))]
    dmod_all = parts[:, :sizes[0]]
    grads = {
        "b_ada": tot[0], "norm_mix_g": tot[1], "norm_ffn_g": tot[2], "norm_final_g": tot[3],
        "lb_param": _lb_grad(tot[4], lb), "gnorm_g": tot[5],
        "conv_w": lax.dynamic_slice(tot[6].reshape(3, -1), (0, k_me * cw_shard), (3, cw_shard)),
    }
    loss_out = tot[7][0, 0]

    for k in _BIG:
        grads[k] = arrived[k].reshape(-1, arrived[k].shape[-1])

    delta, new_m, new_v = {}, {}, {}
    dmod_cols = lax.dynamic_slice(dmod_all, (0, k_me * ada_cols), (N_DEV, ada_cols))
    grads["w_ada"], delta["w_ada"], new_m["w_ada"], new_v["w_ada"] = _ada_update(
        c_act.T, dmod_cols, w["w_ada"], m["w_ada"], v["w_ada"])
    for k in _WEIGHTS:
        if k != "w_ada":
            grads[k], delta[k], new_m[k], new_v[k] = _adamw("adamw_" + k, w[k], grads[k], m[k], v[k])
    outs = [loss_out, dx.reshape(x.shape)]
    for t in (grads, delta, new_m, new_v):
        outs += [t[k].reshape(shapes[k]) for k in _WEIGHTS]
    return tuple(outs)
```

```python
import functools

import jax
import jax.numpy as jnp
from jax import lax
from jax.experimental import pallas as pl
from jax.experimental.pallas import tpu as pltpu

F32 = jnp.float32
BF16 = jnp.bfloat16
MESH = pl.DeviceIdType.MESH

EPS = 1e-6
HEAD = 128
BLK = 16
N_CHIPS = 4
N_DEV = 8
SUBLANES, LANES = 8, 128
VMEM_LIMIT_BYTES = 56 * 1024 * 1024

ADAM_LR = 0.001
ADAM_B1 = 0.9
ADAM_B2 = 0.999
ADAM_EPS = 1e-08
ADAM_WD = 0.01
ADAM_STEP = 10


def _pick(dim, pref, mult):
    if dim <= pref:
        return dim
    t = (pref // mult) * mult
    while t >= mult:
        if dim % t == 0:
            return t
        t -= mult
    return dim


def _sig(x):
    return 1.0 / (1.0 + jnp.exp(-x))


def _params(sem):
    return pltpu.CompilerParams(dimension_semantics=sem, vmem_limit_bytes=VMEM_LIMIT_BYTES)


def _gj(j, i, k):
    return j


def _gk(j, i, k):
    return k


def _wspec(arr, rt, ct, r_of, c_of):
    if arr.ndim == 2:
        return pl.BlockSpec((rt, ct), lambda j, i, k: (r_of(j, i, k), c_of(j, i, k)))
    per = arr.shape[2] // ct
    assert arr.shape[2] % ct == 0
    return pl.BlockSpec((None, rt, ct),
                        lambda j, i, k: (c_of(j, i, k) // per, r_of(j, i, k), c_of(j, i, k) % per))


_HBM = pl.BlockSpec(memory_space=pltpu.HBM)


def _host(job, grid, in_specs, args, out_specs, out_shape, scratch):
    if job is None:
        return {}, (lambda body: body)
    n_in, n_out, n_scr = len(args), len(out_shape), len(scratch)
    n_job, n_alias = len(job.inputs), job.n_alias
    in_specs += [_HBM] * n_job
    args += job.inputs
    out_specs += [_HBM] * n_alias
    out_shape += [jax.ShapeDtypeStruct(a.shape, a.dtype) for a in job.inputs[n_job - n_alias:]]
    scratch += job.sem_shapes
    aliases = {n_in + n_job - n_alias + t: n_out + t for t in range(n_alias)}

    def wrap(body):
        def hosted(*refs):
            ins, refs = refs[:n_in], refs[n_in:]
            j_in, refs = refs[:n_job], refs[n_job:]
            outs, refs = refs[:n_out], refs[n_out:]
            j_out, refs = refs[:n_alias], refs[n_alias:]
            scr, sems = refs[:n_scr], refs[n_scr:]
            first = functools.reduce(jnp.logical_and, [pl.program_id(d) == 0 for d in range(len(grid))])
            last = functools.reduce(jnp.logical_and, [pl.program_id(d) == grid[d] - 1 for d in range(len(grid))])

            @pl.when(first)
            def _():
                job.start(j_in, j_out, sems)

            body(*ins, *outs, *scr)

            @pl.when(last)
            def _():
                job.finish(j_in, j_out, sems)

        return hosted

    return aliases, wrap


EPILOGUE_COLS = 768


def _plain(accs, extras):
    return accs


def _matmul(name, pairs, acc_of, M, N, K, tm, tn, tk, extras, outs, epilogue, job=None, rows_outer=False):
    assert M % tm == 0 and N % tn == 0 and K % tk == 0, (name, M, N, K, tm, tn, tk)
    nj, ni, nk = N // tn, M // tm, K // tk
    n_pairs, n_extra, n_out = len(pairs), len(extras), len(outs)
    n_acc = max(acc_of) + 1
    in_specs, args, dims = [], [], []
    lhs_of, seen = [], {}
    for a, am, b, bm in pairs:
        if (id(a), am) not in seen:
            seen[id(a), am] = len(args)
            args.append(a)
            if am == 'n':
                in_specs.append(pl.BlockSpec((tm, tk), lambda j, i, k: (i, k)))
            else:
                in_specs.append(pl.BlockSpec((tk, tm), lambda j, i, k: (k, i)))
        lhs_of.append(seen[id(a), am])
    n_lhs = len(args)
    for a, am, b, bm in pairs:
        if bm == 'n':
            in_specs.append(_wspec(b, tk, tn, _gk, _gj))
        else:
            in_specs.append(_wspec(b, tn, tk, _gj, _gk))
        dims.append((((1 if am == 'n' else 0,), (0 if bm == 'n' else 1,)), ((), ())))
        args.append(b)
    for e, kind, off in extras:
        assert off % tn == 0, (name, off, tn)
        o = off // tn
        if kind == 'tile':
            in_specs.append(pl.BlockSpec((tm, tn), lambda j, i, k, o=o: (i, j + o)))
        else:
            in_specs.append(pl.BlockSpec((1, tn), lambda j, i, k, o=o: (0, j + o)))
        args.append(e)
    out_shape, out_specs = [], []
    for dt, nb in outs:
        if nb is None:
            out_shape.append(jax.ShapeDtypeStruct((M, N), dt))
            out_specs.append(pl.BlockSpec((tm, tn), lambda j, i, k: (i, j)))
        else:
            assert nb % tn == 0 and N % nb == 0
            per = nb // tn
            out_shape.append(jax.ShapeDtypeStruct((N // nb, M, nb), dt))
            out_specs.append(pl.BlockSpec((None, tm, tn), lambda j, i, k, per=per: (j // per, i, j % per)))

    def body(*refs):
        n_ab = n_lhs + n_pairs
        e_refs = refs[n_ab:n_ab + n_extra]
        o_refs = refs[n_ab + n_extra:n_ab + n_extra + n_out]
        acc_refs = refs[n_ab + n_extra + n_out:]

        def products(cols):
            sums = [None] * n_acc
            for p in range(n_pairs):
                b_ref = refs[n_lhs + p]
                b = b_ref[:, cols] if pairs[p][3] == 'n' else b_ref[cols, :]
                prod = lax.dot_general(refs[lhs_of[p]][...], b, dims[p], preferred_element_type=F32)
                a = acc_of[p]
                sums[a] = prod if sums[a] is None else sums[a] + prod
            return sums

        def finish(accs, cols):
            res = epilogue(accs, [e[:, cols] for e in e_refs])
            for o_ref, r in zip(o_refs, res):
                o_ref[:, cols] = r.astype(o_ref.dtype)

        whole = slice(0, tn)
        if nk == 1:
            step = tn if epilogue is _plain else EPILOGUE_COLS
            for c0 in range(0, tn, step):
                cols = slice(c0, min(c0 + step, tn))
                finish(products(cols), cols)
            return
        sums = products(whole)
        k = pl.program_id(2)
        targets = o_refs if sum_in_outs else acc_refs

        @pl.when(k == 0)
        def _():
            for a in range(n_acc):
                targets[a][...] = sums[a]

        @pl.when(k > 0)
        def _():
            for a in range(n_acc):
                targets[a][...] += sums[a]

        if not sum_in_outs:
            @pl.when(k == nk - 1)
            def _():
                finish([acc_refs[a][...] for a in range(n_acc)], whole)

    sum_in_outs = epilogue is _plain and nk > 1 and n_out == n_acc and all(dt == F32 for dt, _ in outs)
    scratch = [pltpu.VMEM((tm, tn), F32) for _ in range(n_acc)] if nk > 1 and not sum_in_outs else []
    grid = (nj, ni, nk)
    if rows_outer:
        grid = (ni, nj, nk)
        swap = lambda spec: pl.BlockSpec(spec.block_shape, lambda i, j, k, f=spec.index_map: f(j, i, k))
        in_specs, out_specs = [swap(s) for s in in_specs], [swap(s) for s in out_specs]
    aliases, wrap = _host(job, grid, in_specs, args, out_specs, out_shape, scratch)
    sem = ("parallel", "parallel", "arbitrary") if job is None else ("arbitrary",) * 3
    return pl.pallas_call(
        wrap(body), name=name, grid=grid, in_specs=in_specs, out_specs=out_specs, out_shape=out_shape,
        scratch_shapes=scratch, input_output_aliases=aliases, compiler_params=_params(sem),
    )(*args)


def _row_spec(tr, d):
    return pl.BlockSpec((tr, d), lambda i: (i, 0))


def _vec_spec(d):
    return pl.BlockSpec((1, d), lambda i: (0, 0))


def _norm_mod(name, x, g, sc, sh, job=None):
    S, D = x.shape
    tr = _pick(S, 256, 8)

    def body(x_ref, g_ref, sc_ref, sh_ref, h_ref):
        xv = x_ref[...]
        r = lax.rsqrt(jnp.mean(xv * xv, axis=-1, keepdims=True) + EPS)
        h_ref[...] = ((xv * r) * g_ref[...] * (1.0 + sc_ref[...]) + sh_ref[...]).astype(BF16)

    grid = (S // tr,)
    in_specs = [_row_spec(tr, D), _vec_spec(D), _vec_spec(D), _vec_spec(D)]
    args = [x, g, sc, sh]
    out_specs, out_shape, scratch = [_row_spec(tr, D)], [jax.ShapeDtypeStruct((S, D), BF16)], []
    aliases, wrap = _host(job, grid, in_specs, args, out_specs, out_shape, scratch)
    res = pl.pallas_call(
        wrap(body), name=name, grid=grid, in_specs=in_specs, out_specs=out_specs, out_shape=out_shape,
        scratch_shapes=scratch, input_output_aliases=aliases,
        compiler_params=_params(("parallel" if job is None else "arbitrary",)),
    )(*args)
    return res[0] if job is None else res


def _loss_head(x2, target, g, ff, gt):
    S, D = x2.shape
    tr = _pick(S, 256, 8)

    def body(x_ref, t_ref, g_ref, ff_ref, gt_ref, dx_ref, dffb_ref, loss_ref, dgt_ref, dg_ref):
        i = pl.program_id(0)

        @pl.when(i == 0)
        def _():
            loss_ref[...] = jnp.zeros_like(loss_ref)
            dgt_ref[...] = jnp.zeros_like(dgt_ref)
            dg_ref[...] = jnp.zeros_like(dg_ref)

        xv = x_ref[...]
        gv = g_ref[...]
        r = lax.rsqrt(jnp.mean(xv * xv, axis=-1, keepdims=True) + EPS)
        xh = xv * r
        err = xh * gv - t_ref[...]
        loss_ref[...] += 0.5 * jnp.sum(jnp.mean(err * err, axis=-1, keepdims=True))
        dy = err * (1.0 / D)
        dg_ref[...] += jnp.sum(dy * xh, axis=0, keepdims=True)
        dxh = dy * gv
        dx = r * (dxh - xh * jnp.mean(dxh * xh, axis=-1, keepdims=True))
        dx_ref[...] = dx
        dffb_ref[...] = (dx * gt_ref[...]).astype(BF16)
        dgt_ref[...] += jnp.sum(dx * ff_ref[...], axis=0, keepdims=True)

    return pl.pallas_call(
        body, name="loss_head", grid=(S // tr,),
        in_specs=[_row_spec(tr, D), _row_spec(tr, D), _vec_spec(D), _row_spec(tr, D), _vec_spec(D)],
        out_specs=[_row_spec(tr, D), _row_spec(tr, D), pl.BlockSpec((1, 128), lambda i: (0, 0)),
                   _vec_spec(D), _vec_spec(D)],
        out_shape=[jax.ShapeDtypeStruct((S, D), F32), jax.ShapeDtypeStruct((S, D), BF16),
                   jax.ShapeDtypeStruct((1, 128), F32), jax.ShapeDtypeStruct((1, D), F32),
                   jax.ShapeDtypeStruct((1, D), F32)],
        compiler_params=_params(("arbitrary",)),
    )(x2, target, g, ff, gt)


def _norm_mod_bwd(name, dh, x, g, sc, dres, gated=None, job=None):
    S, D = x.shape
    tr = _pick(S, 256, 8)
    n = S // tr
    with_gate = gated is not None

    def body(*refs):
        if with_gate:
            dh_ref, x_ref, g_ref, sc_ref, dres_ref, mo_ref, gt_ref, dx_ref, dsh_ref, dsc_ref, dg_ref, dmob_ref, dgt_ref = refs
        else:
            dh_ref, x_ref, g_ref, sc_ref, dres_ref, dx_ref, dsh_ref, dsc_ref, dg_ref = refs
        i = pl.program_id(0)

        @pl.when(i == 0)
        def _():
            dsh_ref[...] = jnp.zeros_like(dsh_ref)
            dsc_ref[...] = jnp.zeros_like(dsc_ref)
            if with_gate:
                dgt_ref[...] = jnp.zeros_like(dgt_ref)

        xv = x_ref[...]
        dhv = dh_ref[...]
        gv = g_ref[...]
        scale = 1.0 + sc_ref[...]
        r = lax.rsqrt(jnp.mean(xv * xv, axis=-1, keepdims=True) + EPS)
        xh = xv * r
        dsh_ref[...] += jnp.sum(dhv, axis=0, keepdims=True)
        dsc_ref[...] += jnp.sum(dhv * xh, axis=0, keepdims=True)
        dxh = dhv * (gv * scale)
        dx = dres_ref[...] + r * (dxh - xh * jnp.mean(dxh * xh, axis=-1, keepdims=True))
        dx_ref[...] = dx
        if with_gate:
            dmob_ref[...] = (dx * gt_ref[...]).astype(BF16)
            dgt_ref[...] += jnp.sum(dx * mo_ref[...], axis=0, keepdims=True)

        @pl.when(i == n - 1)
        def _():
            t = dsc_ref[...]
            dg_ref[...] = t * scale
            dsc_ref[...] = t * gv

    in_specs = [_row_spec(tr, D), _row_spec(tr, D), _vec_spec(D), _vec_spec(D), _row_spec(tr, D)]
    args = [dh, x, g, sc, dres]
    out_specs = [_row_spec(tr, D), _vec_spec(D), _vec_spec(D), _vec_spec(D)]
    out_shape = [jax.ShapeDtypeStruct((S, D), F32)] + [jax.ShapeDtypeStruct((1, D), F32)] * 3
    if with_gate:
        in_specs += [_row_spec(tr, D), _vec_spec(D)]
        args += list(gated)
        out_specs += [_row_spec(tr, D), _vec_spec(D)]
        out_shape += [jax.ShapeDtypeStruct((S, D), BF16), jax.ShapeDtypeStruct((1, D), F32)]
    scratch = []
    aliases, wrap = _host(job, (n,), in_specs, args, out_specs, out_shape, scratch)
    return pl.pallas_call(
        wrap(body), name=name, grid=(n,), in_specs=in_specs, out_specs=out_specs, out_shape=out_shape,
        scratch_shapes=scratch, input_output_aliases=aliases, compiler_params=_params(("arbitrary",)),
    )(*args)


CONV_ROWS = 256


def _col_spec(S, off_cols):
    o = off_cols // HEAD
    return pl.BlockSpec((S, HEAD), lambda c, o=o: (0, c + o))


def _conv_taps(u, u_prev, rid):
    s1 = jnp.where(rid >= 1, pltpu.roll(u, 1, 0), pltpu.roll(u_prev, 1, 0))
    s2 = jnp.where(rid >= 2, pltpu.roll(u, 2, 0), pltpu.roll(u_prev, 2, 0))
    return s1, s2


def _conv_fwd(proj, conv_w, d_conv, job=None):
    S = proj.shape[0]
    R = min(CONV_ROWS, S)
    nr = S // R

    def body(ab_ref, ac_ref, ax_ref, w_ref, ya_ref):
        w0, w1, w2 = w_ref[0:1, :], w_ref[1:2, :], w_ref[2:3, :]
        rid = lax.broadcasted_iota(jnp.int32, (R, HEAD), 0)

        def step(c, carry):
            rows = pl.ds(pl.multiple_of(c * R, R), R)
            prev = pl.ds(pl.multiple_of(jnp.maximum(c - 1, 0) * R, R), R)
            u = ac_ref[rows, :] * ax_ref[rows, :]
            u_prev = jnp.where(c > 0, ac_ref[prev, :] * ax_ref[prev, :], 0.0)
            s1, s2 = _conv_taps(u, u_prev, rid)
            y = w0 * s2 + w1 * s1 + w2 * u
            ya_ref[rows, :] = (ab_ref[rows, :] * y).astype(BF16)
            return carry

        lax.fori_loop(0, nr, step, 0)

    grid = (d_conv // HEAD,)
    in_specs = [_col_spec(S, 0), _col_spec(S, d_conv), _col_spec(S, 2 * d_conv), pl.BlockSpec((3, HEAD), lambda c: (0, c))]
    args = [proj, proj, proj, conv_w]
    out_specs, out_shape = [pl.BlockSpec((S, HEAD), lambda c: (0, c))], [jax.ShapeDtypeStruct((S, d_conv), BF16)]
    scratch = []
    aliases, wrap = _host(job, grid, in_specs, args, out_specs, out_shape, scratch)
    res = pl.pallas_call(
        wrap(body), name="conv_fwd", grid=grid, in_specs=in_specs, out_specs=out_specs, out_shape=out_shape,
        scratch_shapes=scratch, input_output_aliases=aliases,
        compiler_params=_params(("parallel" if job is None else "arbitrary",)),
    )(*args)
    return res[0] if job is None else res


def _conv_bwd(dya, proj, conv_w, d_conv, job=None):
    S = proj.shape[0]
    R = min(CONV_ROWS, S)
    nr = S // R

    def body(dya_ref, ab_ref, ac_ref, ax_ref, w_ref, dab_ref, dac_ref, dax_ref, dw_ref):
        w0, w1, w2 = w_ref[0:1, :], w_ref[1:2, :], w_ref[2:3, :]
        rid = lax.broadcasted_iota(jnp.int32, (R, HEAD), 0)

        def step(c, carry):
            dw0, dw1, dw2 = carry
            rows = pl.ds(pl.multiple_of(c * R, R), R)
            prev = pl.ds(pl.multiple_of(jnp.maximum(c - 1, 0) * R, R), R)
            nxt = pl.ds(pl.multiple_of(jnp.minimum(c + 1, nr - 1) * R, R), R)
            a_c, a_x, a_b = ac_ref[rows, :], ax_ref[rows, :], ab_ref[rows, :]
            u = a_c * a_x
            u_prev = jnp.where(c > 0, ac_ref[prev, :] * ax_ref[prev, :], 0.0)
            s1, s2 = _conv_taps(u, u_prev, rid)
            y = w0 * s2 + w1 * s1 + w2 * u
            dy = dya_ref[rows, :]
            dab_ref[rows, :] = (dy * y).astype(BF16)
            dyc = dy * a_b
            dyc_next = jnp.where(c < nr - 1, dya_ref[nxt, :] * ab_ref[nxt, :], 0.0)
            t1 = jnp.where(rid < R - 1, pltpu.roll(dyc, R - 1, 0), pltpu.roll(dyc_next, R - 1, 0))
            t2 = jnp.where(rid < R - 2, pltpu.roll(dyc, R - 2, 0), pltpu.roll(dyc_next, R - 2, 0))
            du = w2 * dyc + w1 * t1 + w0 * t2
            dac_ref[rows, :] = (du * a_x).astype(BF16)
            dax_ref[rows, :] = (du * a_c).astype(BF16)
            dw0 = dw0 + jnp.sum(dyc * s2, axis=0, keepdims=True)
            dw1 = dw1 + jnp.sum(dyc * s1, axis=0, keepdims=True)
            dw2 = dw2 + jnp.sum(dyc * u, axis=0, keepdims=True)
            return dw0, dw1, dw2

        z = jnp.zeros((1, HEAD), F32)
        dw0, dw1, dw2 = lax.fori_loop(0, nr, step, (z, z, z))
        dw_ref[0:1, :] = dw0
        dw_ref[1:2, :] = dw1
        dw_ref[2:3, :] = dw2

    col = pl.BlockSpec((S, HEAD), lambda c: (0, c))
    grid = (d_conv // HEAD,)
    in_specs = [col, _col_spec(S, 0), _col_spec(S, d_conv), _col_spec(S, 2 * d_conv),
                pl.BlockSpec((3, HEAD), lambda c: (0, c))]
    args = [dya, proj, proj, proj, conv_w]
    out_specs = [col, col, col, pl.BlockSpec((3, HEAD), lambda c: (0, c))]
    out_shape = [jax.ShapeDtypeStruct((S, d_conv), BF16)] * 3 + [jax.ShapeDtypeStruct((3, d_conv), F32)]
    scratch = []
    aliases, wrap = _host(job, grid, in_specs, args, out_specs, out_shape, scratch)
    return pl.pallas_call(
        wrap(body), name="conv_bwd", grid=grid, in_specs=in_specs, out_specs=out_specs, out_shape=out_shape,
        scratch_shapes=scratch, input_output_aliases=aliases,
        compiler_params=_params(("parallel" if job is None else "arbitrary",)),
    )(*args)


HGRN_FWD_ROWS = 512
HGRN_BWD_ROWS = 1024
HGRN_FWD_SUB = 64
HGRN_BWD_SUB = 32
HGRN_GATE_ROWS = 128


def _block_cumsum(x, pos):
    for s in (1, 2, 4, 8):
        x = x + jnp.where(pos >= s, pltpu.roll(x, s, 0), 0.0)
    return x


def _block_rev_cumsum(x, pos, rows):
    for s in (1, 2, 4, 8):
        x = x + jnp.where(pos < BLK - s, pltpu.roll(x, rows - s, 0), 0.0)
    return x


def _block_last(x, pos, rows):
    m = jnp.where(pos == BLK - 1, x, 0.0)
    for s in (1, 2, 4, 8):
        m = m + pltpu.roll(m, rows - s, 0)
    return m


def _hgrn_gates(q, z, lb):
    sgq = _sig(q)
    qs = q * sgq
    sg = _sig(z)
    f = lb + (1.0 - lb) * sg
    kk = (1.0 - lb) * (1.0 - sg)
    return sgq, qs, sg, f, kk


def _hgrn_decays(q, z, lb, pos, rows):
    sgq, qs, sg, f, kk = _hgrn_gates(q, z, lb)
    b = _block_cumsum(jnp.log(f), pos)
    return sgq, qs, sg, f, kk, b, _block_last(b, pos, rows)


def _hgrn_specs(T, d_conv, n_t, rev):
    def t_of(i):
        return (n_t - 1 - i) if rev else i

    def pcol(off):
        o = off // HEAD
        return pl.BlockSpec((T, HEAD), lambda h, i, o=o: (t_of(i), h + o))

    q_spec, z_spec, v_spec, g_spec = pcol(3 * d_conv), pcol(4 * d_conv), pcol(5 * d_conv), pcol(6 * d_conv)
    lb_spec = pl.BlockSpec((1, HEAD), lambda h, i: (0, h))
    gn_spec = pl.BlockSpec((1, HEAD), lambda h, i: (0, 0))
    act_spec = pl.BlockSpec((T, HEAD), lambda h, i: (t_of(i), h))
    st_spec = pl.BlockSpec((None, T // BLK, HEAD, HEAD), lambda h, i: (h, t_of(i), 0, 0))
    return q_spec, z_spec, v_spec, g_spec, lb_spec, gn_spec, act_spec, st_spec


def _hgrn_fwd(proj, lb, gn, d_conv, job=None):
    S = proj.shape[0]
    H = d_conv // HEAD
    T = min(HGRN_FWD_ROWS, S)
    n_t, nb = S // T, T // BLK
    sub = min(HGRN_FWD_SUB, T)
    q_spec, z_spec, v_spec, g_spec, lb_spec, gn_spec, act_spec, st_spec = _hgrn_specs(T, d_conv, n_t, False)

    def body(q_ref, z_ref, v_ref, g_ref, lb_ref, gn_ref, ob_ref, o_ref, st_ref, qe_s, ke_s, dec_s, state, v_s, o_s, u_s):
        @pl.when(pl.program_id(1) == 0)
        def _():
            state[...] = jnp.zeros_like(state)

        pos = lax.broadcasted_iota(jnp.int32, (sub, HEAD), 0) % BLK
        lbv = lb_ref[...]

        def vector_pass(s, carry):
            r = pl.ds(pl.multiple_of(s * sub, sub), sub)
            v = v_ref[r, :]
            _, qs, _, _, kk, b, b_tot = _hgrn_decays(q_ref[r, :], z_ref[r, :], lbv, pos, sub)
            qe_s[r, :] = (qs * jnp.exp(b)).astype(BF16)
            ke_s[r, :] = (kk * jnp.exp(b_tot - b)).astype(BF16)
            v_s[r, :] = v.astype(BF16)
            dec_s[r, :] = jnp.exp(b_tot)
            o = jnp.sum(qs * kk, axis=-1, keepdims=True) * v
            for d in range(1, BLK):
                e = jnp.exp(b - pltpu.roll(b, d, 0))
                a = jnp.sum(qs * pltpu.roll(kk, d, 0) * e, axis=-1, keepdims=True)
                o = o + jnp.where(pos >= d, a * pltpu.roll(v, d, 0), 0.0)
            o_s[r, :] = o
            return carry

        lax.fori_loop(0, T // sub, vector_pass, 0)

        for j in range(nb):
            rows = pl.ds(j * BLK, BLK)
            u_s[j] = lax.dot_general(v_s[rows, :], ke_s[rows, :], (((0,), (0,)), ((), ())),
                                     preferred_element_type=F32)
        st = state[...]
        for j in range(nb):
            st_ref[j] = st.astype(BF16)
            st = st * dec_s[pl.ds(j * BLK, 1), :] + u_s[j]
        state[...] = st
        for j in range(nb):
            rows = pl.ds(j * BLK, BLK)
            o_s[rows, :] += lax.dot_general(qe_s[rows, :], st_ref[j], (((1,), (1,)), ((), ())),
                                            preferred_element_type=F32)
        o = o_s[...]
        o_ref[...] = o
        r = lax.rsqrt(jnp.mean(o * o, axis=-1, keepdims=True) + EPS)
        g = g_ref[...]
        ob_ref[...] = ((o * r) * gn_ref[...] * (g * _sig(g))).astype(BF16)

    grid = (H, n_t)
    in_specs = [q_spec, z_spec, v_spec, g_spec, lb_spec, gn_spec]
    args = [proj, proj, proj, proj, lb, gn]
    out_specs = [act_spec, act_spec, st_spec, act_spec, act_spec, act_spec]
    out_shape = [jax.ShapeDtypeStruct((S, d_conv), BF16), jax.ShapeDtypeStruct((S, d_conv), F32),
                 jax.ShapeDtypeStruct((H, S // BLK, HEAD, HEAD), BF16),
                 jax.ShapeDtypeStruct((S, d_conv), BF16), jax.ShapeDtypeStruct((S, d_conv), BF16),
                 jax.ShapeDtypeStruct((S, d_conv), F32)]
    scratch = [pltpu.VMEM((HEAD, HEAD), F32), pltpu.VMEM((T, HEAD), BF16), pltpu.VMEM((T, HEAD), F32),
               pltpu.VMEM((nb, HEAD, HEAD), F32)]
    aliases, wrap = _host(job, grid, in_specs, args, out_specs, out_shape, scratch)
    return pl.pallas_call(
        wrap(body), name="hgrn_fwd", grid=grid, in_specs=in_specs, out_specs=out_specs, out_shape=out_shape,
        scratch_shapes=scratch, input_output_aliases=aliases,
        compiler_params=_params(("parallel" if job is None else "arbitrary", "arbitrary")),
    )(*args)


def _hgrn_bwd(dob, o_raw, states, qe, ke, dec, proj, lb, gn, d_conv, job=None):
    S = proj.shape[0]
    H = d_conv // HEAD
    T = min(HGRN_BWD_ROWS, S)
    n_t, nb = S // T, T // BLK
    sub = min(HGRN_BWD_SUB, T)
    gate_rows = min(HGRN_GATE_ROWS, T)
    q_spec, z_spec, v_spec, g_spec, lb_spec, gn_spec, act_spec, st_spec = _hgrn_specs(T, d_conv, n_t, True)

    def body(dob_ref, o_ref, st_ref, qe_b, ke_b, dec_s, q_ref, z_ref, v_ref, g_ref, lb_ref, gn_ref,
             dq_ref, dz_ref, dv_ref, dg_ref, dlb_ref, dgn_ref,
             dstate, do_b, v_b, dqe_s, dke_s, dvi_s, ddec_s, do_s, u_s, ds_s):
        @pl.when(pl.program_id(1) == 0)
        def _():
            dstate[...] = jnp.zeros_like(dstate)
            dlb_ref[...] = jnp.zeros_like(dlb_ref)
            dgn_ref[...] = jnp.zeros_like(dgn_ref)

        pos = lax.broadcasted_iota(jnp.int32, (sub, HEAD), 0) % BLK
        lbv, gnv = lb_ref[...], gn_ref[...]

        def before(s, carry):
            r = pl.ds(pl.multiple_of(s * gate_rows, gate_rows), gate_rows)
            g = g_ref[r, :]
            v_b[r, :] = v_ref[r, :].astype(BF16)
            o = o_ref[r, :]
            rs = lax.rsqrt(jnp.mean(o * o, axis=-1, keepdims=True) + EPS)
            on = o * rs
            sgg = _sig(g)
            dobv = dob_ref[r, :]
            dog = dobv * (g * sgg)
            dgn_ref[...] += jnp.sum(dog * on, axis=0, keepdims=True)
            don = dog * gnv
            do = rs * (don - on * jnp.mean(don * on, axis=-1, keepdims=True))
            dg_ref[r, :] = (dobv * (on * gnv) * (sgg * (1.0 + g * (1.0 - sgg)))).astype(BF16)
            do_s[r, :] = do
            do_b[r, :] = do.astype(BF16)
            return carry

        lax.fori_loop(0, T // gate_rows, before, 0)

        for j in range(nb):
            rows = pl.ds(j * BLK, BLK)
            dob_j = do_b[rows, :]
            u_s[j] = lax.dot_general(dob_j, qe_b[rows, :], (((0,), (0,)), ((), ())), preferred_element_type=F32)
            dqe_s[rows, :] = lax.dot_general(dob_j, st_ref[j], (((1,), (0,)), ((), ())), preferred_element_type=F32)
        dst = dstate[...]
        for j in reversed(range(nb)):
            ds_s[j] = dst.astype(BF16)
            ddec = jnp.sum(st_ref[j].astype(F32) * dst, axis=0, keepdims=True)
            ddec_s[pl.ds(j * BLK, BLK), :] = jnp.broadcast_to(ddec, (BLK, HEAD))
            dst = dst * dec_s[pl.ds(j * BLK, 1), :] + u_s[j]
        dstate[...] = dst
        for j in range(nb):
            rows = pl.ds(j * BLK, BLK)
            dke_s[rows, :] = lax.dot_general(v_b[rows, :], ds_s[j], (((1,), (0,)), ((), ())), preferred_element_type=F32)
            dvi_s[rows, :] = lax.dot_general(ke_b[rows, :], ds_s[j], (((1,), (1,)), ((), ())), preferred_element_type=F32)

        def after(s, carry):
            r = pl.ds(pl.multiple_of(s * sub, sub), sub)
            q, v = q_ref[r, :], v_ref[r, :]
            sgq, qs, sg, f, kk, b, b_tot = _hgrn_decays(q, z_ref[r, :], lbv, pos, sub)
            do = do_s[r, :]
            dqs = dqe_s[r, :] * jnp.exp(b)
            dkk = dke_s[r, :] * jnp.exp(b_tot - b)
            d_tot = jnp.where(pos == BLK - 1, _block_cumsum(dkk * kk, pos) + ddec_s[r, :] * jnp.exp(b_tot), 0.0)
            dv = dvi_s[r, :]
            da = jnp.sum(do * v, axis=-1, keepdims=True)
            dv = dv + jnp.sum(qs * kk, axis=-1, keepdims=True) * do
            dqs = dqs + da * kk
            dkk = dkk + da * qs
            for d in range(1, BLK):
                kd = pltpu.roll(kk, d, 0)
                e = jnp.where(pos >= d, jnp.exp(b - pltpu.roll(b, d, 0)), 0.0)
                a = jnp.sum(qs * kd * e, axis=-1, keepdims=True)
                da = jnp.sum(do * pltpu.roll(v, d, 0), axis=-1, keepdims=True)
                gq = da * e
                dqs = dqs + gq * kd
                dkk = dkk + pltpu.roll(gq * qs, sub - d, 0)
                dv = dv + pltpu.roll(a * do, sub - d, 0)
            db = qs * dqs - kk * dkk + d_tot
            dlf = _block_rev_cumsum(db, pos, sub)
            df = dlf / f - dkk
            dlb_ref[...] += jnp.sum(df * (1.0 - sg), axis=0, keepdims=True)
            dz_ref[r, :] = (df * (1.0 - lbv) * sg * (1.0 - sg)).astype(BF16)
            dq_ref[r, :] = (dqs * (sgq * (1.0 + q * (1.0 - sgq)))).astype(BF16)
            dv_ref[r, :] = dv.astype(BF16)
            return carry

        lax.fori_loop(0, T // sub, after, 0)

    tb = lambda dt: pltpu.VMEM((T, HEAD), dt)
    grid = (H, n_t)
    in_specs = [act_spec, act_spec, st_spec, act_spec, act_spec, act_spec, q_spec, z_spec, v_spec, g_spec, lb_spec,
                gn_spec]
    args = [dob, o_raw, states, qe, ke, dec, proj, proj, proj, proj, lb, gn]
    out_specs = [act_spec, act_spec, act_spec, act_spec, lb_spec, pl.BlockSpec((None, 1, HEAD), lambda h, i: (h, 0, 0))]
    out_shape = ([jax.ShapeDtypeStruct((S, d_conv), BF16)] * 4
                 + [jax.ShapeDtypeStruct((1, d_conv), F32), jax.ShapeDtypeStruct((H, 1, HEAD), F32)])
    scratch = [pltpu.VMEM((HEAD, HEAD), F32), tb(BF16), tb(BF16), tb(F32), tb(F32), tb(F32), tb(F32), tb(F32),
               pltpu.VMEM((nb, HEAD, HEAD), F32), pltpu.VMEM((nb, HEAD, HEAD), BF16)]
    aliases, wrap = _host(job, grid, in_specs, args, out_specs, out_shape, scratch)
    return pl.pallas_call(
        wrap(body), name="hgrn_bwd", grid=grid, in_specs=in_specs, out_specs=out_specs, out_shape=out_shape,
        scratch_shapes=scratch, input_output_aliases=aliases,
        compiler_params=_params(("parallel" if job is None else "arbitrary", "arbitrary")),
    )(*args)


def _local_step(x, target, mod, norm_mix_g, lb, gnorm_g, norm_ffn_g, norm_final_g, conv_w, bufs, c_idx, order):
    S, D = x.shape
    d_conv = D // 2
    d_in = 7 * d_conv + 2 * D
    d_ff = N_CHIPS * bufs["w_ffn_down"].shape[1]
    nb_in, nb_co, nb_ff = bufs["w_in"].shape[2], bufs["w_conv_out"].shape[2], bufs["w_ffn_gate"].shape[2]
    rows = lambda g: g.reshape(-1, g.shape[2])
    sh_m, sc_m, gt_m, sh_f, sc_f, gt_f = [mod[:, i * D:(i + 1) * D] for i in range(6)]
    tm = _pick(S, 512, 16)
    tm2 = _pick(S, 1024, 16)
    tn_in = _pick(nb_in, 1408, 128)
    tn_co = _pick(nb_co, 512, 128)
    tn_ff = _pick(nb_ff, 1408, 128)
    tn_d = _pick(D, 512, 128)
    tw = _pick(D, 1024, 128)
    tg = _pick(D, 512, 128)

    h = _norm_mod("norm_mix", x, norm_mix_g, sc_m, sh_m)
    proj, w_in = _proj_gather(h, bufs["w_in"], order)
    ob, o_raw, states, qe, ke, dec, *got = _hgrn_fwd(
        proj, lb, gnorm_g, d_conv,
        job=_GatherJob([bufs[k] for k in ("w_conv_out", "w_hgrn_out", "w_o", "w_ffn_gate")]))
    ya, w_conv_out, w_hgrn_out, w_o, w_ffn_gate, w_ffn_up = _conv_fwd(
        proj, conv_w, d_conv, job=_Jobs([_ForwardJob(got), _GatherJob([bufs["w_ffn_up"]], 0, 2)]))
    w_o = rows(w_o)

    def merge_epi(accs, ex):
        y_a, y_b = accs
        return [y_a, y_b, _sig(ex[0]) * y_a + _sig(ex[1]) * y_b]

    y_a, y_b, merged, w_ffn_up = _matmul(
        "branch_out", [(ya, 'n', w_conv_out, 'n'), (ob, 'n', w_hgrn_out, 'n')], [0, 1], S, D, d_conv, tm2, tn_co, d_conv,
        [(proj, 'tile', 7 * d_conv), (proj, 'tile', 7 * d_conv + D)], [(BF16, None), (BF16, None), (BF16, None)], merge_epi,
        job=_GatherJob([w_ffn_up], 2, 5), rows_outer=True)

    def resid_epi(accs, ex):
        return [accs[0], ex[0] + ex[1] * accs[0]]

    mo, x1, w_ffn_up = _matmul("w_o", [(merged, 'n', w_o, 'n')], [0], S, D, D, tm2, tw, D,
                               [(x, 'tile', 0), (gt_m, 'row', 0)], [(BF16, None), (F32, None)], resid_epi,
                               job=_GatherJob([w_ffn_up], 5, 8))
    h2, w_ffn_up = _norm_mod("norm_ffn", x1, norm_ffn_g, sc_f, sh_f, job=_ForwardJob([w_ffn_up]))

    def swiglu_epi(accs, ex):
        gg, uu = accs
        return [gg, uu, gg * _sig(gg) * uu]

    G, U, act, w_ffn_down = _matmul(
        "ffn_in", [(h2, 'n', w_ffn_gate, 'n'), (h2, 'n', w_ffn_up, 'n')], [0, 1], S, d_ff, D,
        tm2, tn_ff, D, [], [(BF16, None), (BF16, None), (BF16, None)], swiglu_epi, job=_GatherJob([bufs["w_ffn_down"]]))
    w_ffn_down = rows(_forward_halves("forward_down", [w_ffn_down])[0])
    ff, x2 = _matmul("ffn_out", [(act, 'n', w_ffn_down, 'n')], [0], S, D, d_ff, tm2, tn_d, d_ff,
                     [(x1, 'tile', 0), (gt_f, 'row', 0)], [(BF16, None), (F32, None)], resid_epi, rows_outer=True)

    dx2, dffb, loss, dgt_f, dg_final = _loss_head(x2, target, norm_final_g, ff, gt_f)

    def swiglu_bwd_epi(accs, ex):
        dact, gg, uu = accs[0], ex[0], ex[1]
        s = _sig(gg)
        return [dact * uu * (s * (1.0 + gg * (1.0 - s))), dact * (gg * s)]

    dG, dU = _matmul("d_act", [(dffb, 'n', w_ffn_down, 't')], [0], S, d_ff, D, tm2, tn_ff, D,
                     [(G, 'tile', 0), (U, 'tile', 0)], [(BF16, None), (BF16, None)], swiglu_bwd_epi)
    dw_down, = _matmul("dw_ffn_down", [(act, 't', dffb, 'n')], [0], d_ff, D, S, _pick(d_ff, 512, 128),
                       D, S, [], [(BF16, None)], _plain)
    dh2, = _matmul("d_h2", [(dG, 'n', w_ffn_gate, 't'), (dU, 'n', w_ffn_up, 't')], [0, 0], S, D, d_ff,
                   tm2, tw, tn_ff, [], [(F32, None)], _plain, rows_outer=True)
    dw_gate, = _matmul("dw_ffn_gate", [(h2, 't', dG, 'n')], [0], D, d_ff, S, tg, tn_ff, S, [], [(BF16, nb_ff)], _plain)
    dw_up, = _matmul("dw_ffn_up", [(h2, 't', dU, 'n')], [0], D, d_ff, S, tg, tn_ff, S, [], [(BF16, nb_ff)], _plain)

    ck_idx = jnp.stack([c_idx[0], order[0]])

    def pair_sums(names, grads, from_sibling):
        pairs = [_pair_sum("pair_sum_" + k, g, q, ck_idx) for k, g, q in zip(names, grads, from_sibling)]
        return [p[0] for p in pairs], [p[1] for p in pairs]

    ffn_names = ["w_ffn_down", "w_ffn_gate", "w_ffn_up"]
    ffn_grads = [dw_down.reshape(N_CHIPS, -1, D), dw_gate, dw_up]
    dx1, dsh_f, dsc_f, dg_ffn, dmob, dgt_m, *from_sibling = _norm_mod_bwd(
        "norm_ffn_bwd", dh2, x1, norm_ffn_g, sc_f, dx2, (mo, gt_m), job=_SwapJob(ffn_grads))
    parts_f, slots_f = pair_sums(ffn_names, ffn_grads, from_sibling)

    def merge_bwd_epi(accs, ex):
        dm, ga, gb, ya_, yb_ = accs[0], ex[0], ex[1], ex[2], ex[3]
        sa, sb = _sig(ga), _sig(gb)
        return [dm * ya_ * sa * (1.0 - sa), dm * yb_ * sb * (1.0 - sb), dm * sa, dm * sb]

    dga, dgb, dy_a, dy_b = _matmul(
        "d_merged", [(dmob, 'n', w_o, 't')], [0], S, D, D, tm2, tn_co, D,
        [(proj, 'tile', 7 * d_conv), (proj, 'tile', 7 * d_conv + D), (y_a, 'tile', 0), (y_b, 'tile', 0)],
        [(BF16, None)] * 4, merge_bwd_epi, rows_outer=True)
    dw_o, = _matmul("dw_o", [(merged, 't', dmob, 'n')], [0], D, D, S, tg, D, S, [], [(BF16, None)], _plain)
    dya, dob = _matmul("d_branch", [(dy_a, 'n', w_conv_out, 't'), (dy_b, 'n', w_hgrn_out, 't')], [0, 1], S, d_conv, D,
                       tm2, _pick(d_conv, 1024, 128), tn_co, [], [(F32, None), (F32, None)], _plain)
    dw_co, dw_ho = _matmul("dw_branch", [(ya, 't', dy_a, 'n'), (ob, 't', dy_b, 'n')], [0, 1], d_conv, D, S,
                           _pick(d_conv, 512, 128), tn_co, S, [], [(BF16, nb_co), (BF16, nb_co)], _plain)
    branch_names = ["w_conv_out", "w_hgrn_out", "w_o"]
    branch_grads = [dw_co, dw_ho, dw_o.reshape(N_CHIPS, -1, D)]
    dab, dac, dax, dconv_w, *from_sibling = _conv_bwd(dya, proj, conv_w, d_conv, job=_SwapJob(branch_grads))
    parts_b, slots_b = pair_sums(branch_names, branch_grads, from_sibling)
    dq, dz, dv, dgo, dlb, dgn, *arrived_f = _hgrn_bwd(dob, o_raw, states, qe, ke, dec, proj, lb, gnorm_g, d_conv,
                                                      job=_ScatterJob(parts_f, slots_f))
    dproj = jnp.concatenate([dab, dac, dax, dq, dz, dv, dgo, dga, dgb], axis=1)
    halves_f = [_sum_chips("sum_chips_" + k, r, c_idx) for k, r in zip(ffn_names, arrived_f)]
    dw_in, *moved = _matmul("dw_in", [(h, 't', dproj, 'n')], [0], D, d_in, S, tg, tn_in, S, [], [(BF16, nb_in)], _plain,
                            job=_Jobs([_ShareJob(halves_f), _ScatterJob(parts_b, slots_b)]))
    whole_f, arrived_b = moved[:len(ffn_names)], moved[len(ffn_names):]
    parts_i, slots_i = pair_sums(["w_in"], [dw_in], _swap_halves("swap_in", [dw_in]))
    dh, arrived_in = _matmul("d_h", [(dproj, 'n', w_in, 't')], [0], S, D, d_in, tm2, D, tn_in, [], [(F32, None)], _plain,
                             job=_ScatterJob(parts_i, slots_i))
    dx, dsh_m, dsc_m, dg_mix = _norm_mod_bwd("norm_mix_bwd", dh, x, norm_mix_g, sc_m, dx1)
    dmod = jnp.concatenate([dsh_m, dsc_m, dgt_m, dsh_f, dsc_f, dgt_f], axis=1)
    small = dict(dmod=dmod, dg_mix=dg_mix, dg_ffn=dg_ffn, dg_final=dg_final, dlb=dlb,
                 dgn=jnp.sum(dgn, axis=0), dconv_w=dconv_w)
    late_names = branch_names + ["w_in"]
    late = _share_halves([_sum_chips("sum_chips_" + k, r, c_idx) for k, r in zip(late_names, arrived_b + [arrived_in])])
    big = dict(zip(ffn_names + late_names, whole_f + list(late)))
    return loss, dx, small, big


def _adamw_math(w, g, m, v):
    m = ADAM_B1 * m + (1.0 - ADAM_B1) * g
    v = ADAM_B2 * v + (1.0 - ADAM_B2) * (g * g)
    m_hat = m / (1.0 - ADAM_B1 ** ADAM_STEP)
    v_hat = v / (1.0 - ADAM_B2 ** ADAM_STEP)
    delta = -ADAM_LR * (m_hat / (jnp.sqrt(v_hat) + ADAM_EPS) + ADAM_WD * w)
    return delta, m, v


def _adamw(name, w, g, m, v):
    R, C = w.shape
    tr = _pick(R, max(8, (256 * 1024) // C // 8 * 8), 8)
    spec = pl.BlockSpec((tr, C), lambda i: (i, 0))

    def body(w_ref, g_ref, m_ref, v_ref, go_ref, d_ref, mo_ref, vo_ref):
        gv = g_ref[...]
        d, m2, v2 = _adamw_math(w_ref[...], gv, m_ref[...], v_ref[...])
        go_ref[...] = gv
        d_ref[...] = d
        mo_ref[...] = m2
        vo_ref[...] = v2

    return pl.pallas_call(
        body, name=name, grid=(R // tr,), in_specs=[spec] * 4, out_specs=[spec] * 4,
        out_shape=[jax.ShapeDtypeStruct((R, C), F32)] * 4, compiler_params=_params(("parallel",)),
    )(w, g, m, v)


def _ada_update(c_act_t, dmod, w, m, v):
    R, C = w.shape
    B = dmod.shape[0]
    tr = _pick(R, 256, 8)
    tc = _pick(C, 1536, 128)
    spec = pl.BlockSpec((tr, tc), lambda i, j: (i, j))

    def body(ct_ref, dm_ref, w_ref, m_ref, v_ref, g_ref, d_ref, mo_ref, vo_ref):
        ct = ct_ref[...]
        dm = dm_ref[...]
        g = ct[:, 0:1] * dm[0:1, :]
        for b in range(1, B):
            g = g + ct[:, b:b + 1] * dm[b:b + 1, :]
        d, m2, v2 = _adamw_math(w_ref[...], g, m_ref[...], v_ref[...])
        g_ref[...] = g
        d_ref[...] = d
        mo_ref[...] = m2
        vo_ref[...] = v2

    return pl.pallas_call(
        body, name="ada_update", grid=(R // tr, C // tc),
        in_specs=[pl.BlockSpec((tr, B), lambda i, j: (i, 0)), pl.BlockSpec((B, tc), lambda i, j: (0, j)),
                  spec, spec, spec],
        out_specs=[spec] * 4, out_shape=[jax.ShapeDtypeStruct((R, C), F32)] * 4,
        compiler_params=_params(("parallel", "parallel")),
    )(c_act_t, dmod, w, m, v)


def _prep(c_all, lb_param):
    def body(c_ref, p_ref, ca_ref, cab_ref, lb_ref):
        cv = c_ref[...]
        ca = cv * _sig(cv)
        ca_ref[...] = ca
        cab_ref[...] = ca.astype(BF16)
        lb_ref[...] = _sig(p_ref[0:1, :] - p_ref[1:2, :])

    return pl.pallas_call(
        body, name="prep",
        out_shape=[jax.ShapeDtypeStruct(c_all.shape, F32), jax.ShapeDtypeStruct(c_all.shape, BF16),
                   jax.ShapeDtypeStruct((1, lb_param.shape[1]), F32)],
    )(c_all, lb_param)


def _small_reduce(parts):
    W = parts.shape[1]

    def body(p_ref, o_ref):
        acc = p_ref[0:1, :]
        for d in range(1, N_DEV):
            acc = acc + p_ref[d:d + 1, :]
        o_ref[...] = acc

    return pl.pallas_call(body, name="small_reduce", out_shape=jax.ShapeDtypeStruct((1, W), F32))(parts)


def _lb_grad(dlb, lb):
    def body(d_ref, lb_ref, o_ref):
        t = d_ref[...] * lb_ref[...] * (1.0 - lb_ref[...])
        o_ref[0:1, :] = t
        o_ref[1:2, :] = -t

    return pl.pallas_call(body, name="lb_grad", out_shape=jax.ShapeDtypeStruct((2, dlb.shape[1]), F32))(dlb, lb)


def _place():
    x, y, c = lax.axis_index("x"), lax.axis_index("y"), lax.axis_index("c")
    chips = [(1 - x, y), (x, 1 - y), (1 - x, 1 - y)]
    return x, y, c, chips


def _allgather_rows(name, v):
    m_per, n = v.shape

    def body(x_ref, out_ref, send_sems, recv_sems, local_sem):
        x, y, c, chips = _place()
        me, sibling = (x, y, c), (x, y, 1 - c)

        def rows(px, py, pc):
            return out_ref.at[pl.ds((4 * px + 2 * py + pc) * m_per, m_per), :]

        def copy(k, block, to, src=None):
            return pltpu.make_async_remote_copy(
                src_ref=rows(*block) if src is None else src, dst_ref=rows(*block),
                send_sem=send_sems.at[k], recv_sem=recv_sems.at[k], device_id=to, device_id_type=MESH)

        mine = pltpu.make_async_copy(x_ref, rows(*me), local_sem)
        mine.start()
        first = [copy(0, me, sibling, src=x_ref)]
        first += [copy(1 + j, me, (*chip, c), src=x_ref) for j, chip in enumerate(chips)]
        for cp in first:
            cp.start()
        passed = [copy(4 + j, (*chip, c), sibling) for j, chip in enumerate(chips)]
        for j, chip in enumerate(chips):
            copy(1 + j, (*chip, c), me).wait_recv()
            passed[j].start()
        copy(0, sibling, me).wait_recv()
        for j, chip in enumerate(chips):
            copy(4 + j, (*chip, 1 - c), me).wait_recv()
        for cp in first + passed:
            cp.wait_send()
        mine.wait()

    return pl.pallas_call(
        body, name=name, out_shape=jax.ShapeDtypeStruct((N_DEV * m_per, n), v.dtype),
        in_specs=[pl.BlockSpec(memory_space=pltpu.VMEM)], out_specs=pl.BlockSpec(memory_space=pltpu.VMEM),
        scratch_shapes=[pltpu.SemaphoreType.DMA((7,)), pltpu.SemaphoreType.DMA((7,)), pltpu.SemaphoreType.DMA],
    )(v)


def _cast_place(name, w, k_idx):
    R, C = w.shape
    tr = _pick(R, max(16, (512 * 1024) // C // 16 * 16), 16)

    def body(k_ref, w_ref, o_ref):
        o_ref[...] = w_ref[...].astype(BF16)

    return pl.pallas_call(
        body, name=name,
        grid_spec=pltpu.PrefetchScalarGridSpec(
            num_scalar_prefetch=1, grid=(R // tr,),
            in_specs=[pl.BlockSpec((tr, C), lambda i, k_ref: (i, 0))],
            out_specs=pl.BlockSpec((None, tr, C), lambda i, k_ref: (k_ref[0], i, 0))),
        out_shape=jax.ShapeDtypeStruct((N_CHIPS, R, C), BF16),
        compiler_params=_params(("parallel",)),
    )(k_idx, w)


def _chip_copies(src_of, dst_of, got_of, n, sems, receiving):
    x, y, c, chips = _place()
    k_me = 2 * x + y
    send_sems, recv_sems = sems
    out = []
    for a in range(n):
        for j, chip in enumerate(chips):
            k_chip = 2 * chip[0] + chip[1]
            if receiving:
                src = dst = got_of(a, k_chip, c)
                to = (x, y, c)
            else:
                src, dst, to = src_of(a, k_chip, k_me, c), dst_of(a, k_me, c), (*chip, c)
            out.append(pltpu.make_async_remote_copy(
                src_ref=src, dst_ref=dst, send_sem=send_sems.at[3 * a + j], recv_sem=recv_sems.at[3 * a + j],
                device_id=to, device_id_type=MESH))
    return out


class _ChipJob:
    def start(self, j_in, j_out, sems):
        for send in self.copies(j_in, j_out, sems, False):
            send.start()

    def finish(self, j_in, j_out, sems):
        for recv in self.copies(j_in, j_out, sems, True):
            recv.wait_recv()
        for send in self.copies(j_in, j_out, sems, False):
            send.wait_send()


class _GatherJob(_ChipJob):
    def __init__(self, bufs, lo=0, hi=8):
        n = len(bufs)
        self.inputs, self.n_alias = list(bufs), n
        self.sem_shapes = [pltpu.SemaphoreType.DMA((3 * n,)), pltpu.SemaphoreType.DMA((3 * n,))]
        self.half = [b.shape[1] // 2 for b in bufs]
        self.lo, self.hi = lo, hi

    def copies(self, j_in, j_out, sems, receiving):
        def half(a, k, c):
            eighth = self.half[a] // 8
            return j_out[a].at[k, pl.ds(c * self.half[a] + self.lo * eighth, (self.hi - self.lo) * eighth)]

        return _chip_copies(lambda a, k_chip, k_me, c: half(a, k_me, c), half, half, len(self.half), sems, receiving)


class _ScatterJob(_ChipJob):
    def __init__(self, parts, slots):
        n = len(parts)
        self.n = n
        self.inputs, self.n_alias = list(parts) + list(slots), n
        self.sem_shapes = [pltpu.SemaphoreType.DMA((3 * n,)), pltpu.SemaphoreType.DMA((3 * n,))]

    def copies(self, j_in, j_out, sems, receiving):
        return _chip_copies(lambda a, k_chip, k_me, c: j_in[a].at[k_chip], lambda a, k_me, c: j_out[a].at[k_me],
                            lambda a, k_chip, c: j_out[a].at[k_chip], self.n, sems, receiving)


class _SwapJob(_ChipJob):
    def __init__(self, grads):
        n = len(grads)
        self.n = n
        self.half = [g.shape[1] // 2 for g in grads]
        landing = [lax.empty((N_CHIPS, g.shape[1] // 2, g.shape[2]), g.dtype) for g in grads]
        self.inputs, self.n_alias = list(grads) + landing, n
        self.sem_shapes = [pltpu.SemaphoreType.DMA((n,)), pltpu.SemaphoreType.DMA((n,))]

    def copies(self, j_in, j_out, sems, receiving):
        x, y, c, _ = _place()
        out = []
        for a in range(self.n):
            if receiving:
                src, to = j_out[a], (x, y, c)
            else:
                src, to = j_in[a].at[:, pl.ds((1 - c) * self.half[a], self.half[a])], (x, y, 1 - c)
            out.append(pltpu.make_async_remote_copy(src_ref=src, dst_ref=j_out[a], send_sem=sems[0].at[a],
                                                    recv_sem=sems[1].at[a], device_id=to, device_id_type=MESH))
        return out


class _Jobs:
    def __init__(self, jobs):
        self.jobs = jobs
        split = [(j.inputs[:len(j.inputs) - j.n_alias], j.inputs[len(j.inputs) - j.n_alias:]) for j in jobs]
        self.inputs = [a for plain, _ in split for a in plain] + [a for _, aliased in split for a in aliased]
        self.n_alias = sum(j.n_alias for j in jobs)
        self.sem_shapes = [s for j in jobs for s in j.sem_shapes]

    def _each(self, j_in, j_out, sems):
        n_plain = len(j_in) - self.n_alias
        p0 = a0 = s0 = 0
        for j in self.jobs:
            n_p, n_s = len(j.inputs) - j.n_alias, len(j.sem_shapes)
            ins = list(j_in[p0:p0 + n_p]) + list(j_in[n_plain + a0:n_plain + a0 + j.n_alias])
            yield j, ins, j_out[a0:a0 + j.n_alias], sems[s0:s0 + n_s]
            p0, a0, s0 = p0 + n_p, a0 + j.n_alias, s0 + n_s

    def start(self, j_in, j_out, sems):
        for j, ins, outs, s in self._each(j_in, j_out, sems):
            j.start(ins, outs, s)

    def finish(self, j_in, j_out, sems):
        for j, ins, outs, s in self._each(j_in, j_out, sems):
            j.finish(ins, outs, s)


class _ShareJob(_ChipJob):
    def __init__(self, bufs):
        n = len(bufs)
        self.n = n
        self.inputs, self.n_alias = list(bufs), n
        self.sem_shapes = [pltpu.SemaphoreType.DMA((n,)), pltpu.SemaphoreType.DMA((n,))]

    def copies(self, j_in, j_out, sems, receiving):
        x, y, c, _ = _place()
        out = []
        for a in range(self.n):
            hc, to = (1 - c, (x, y, c)) if receiving else (c, (x, y, 1 - c))
            out.append(pltpu.make_async_remote_copy(
                src_ref=j_out[a].at[hc], dst_ref=j_out[a].at[hc], send_sem=sems[0].at[a], recv_sem=sems[1].at[a],
                device_id=to, device_id_type=MESH))
        return out


class _ForwardJob(_ChipJob):
    def __init__(self, bufs):
        n = len(bufs)
        self.n = n
        self.half = [b.shape[1] // 2 for b in bufs]
        self.inputs, self.n_alias = list(bufs), n
        self.sem_shapes = [pltpu.SemaphoreType.DMA((3 * n,)), pltpu.SemaphoreType.DMA((3 * n,))]

    def copies(self, j_in, j_out, sems, receiving):
        x, y, c, chips = _place()
        out = []
        for a in range(self.n):
            for q, chip in enumerate(chips):
                hc, to = (1 - c, (x, y, c)) if receiving else (c, (x, y, 1 - c))
                part = j_out[a].at[2 * chip[0] + chip[1], pl.ds(hc * self.half[a], self.half[a])]
                out.append(pltpu.make_async_remote_copy(
                    src_ref=part, dst_ref=part, send_sem=sems[0].at[3 * a + q], recv_sem=sems[1].at[3 * a + q],
                    device_id=to, device_id_type=MESH))
        return out


def _forward_halves(name, bufs):
    n = len(bufs)

    def body(*refs):
        out_refs = refs[n:2 * n]
        send_sems, recv_sems = refs[2 * n:]
        x, y, c, chips = _place()
        started, waits = [], []
        for a in range(n):
            rh = bufs[a].shape[1] // 2
            for j, chip in enumerate(chips):
                k_chip = 2 * chip[0] + chip[1]
                got = out_refs[a].at[k_chip, pl.ds(c * rh, rh)]
                cp = pltpu.make_async_remote_copy(src_ref=got, dst_ref=got, send_sem=send_sems.at[3 * a + j],
                                                  recv_sem=recv_sems.at[3 * a + j], device_id=(x, y, 1 - c),
                                                  device_id_type=MESH)
                cp.start()
                started.append(cp)
                other = out_refs[a].at[k_chip, pl.ds((1 - c) * rh, rh)]
                waits.append(pltpu.make_async_remote_copy(
                    src_ref=other, dst_ref=other, send_sem=send_sems.at[3 * a + j], recv_sem=recv_sems.at[3 * a + j],
                    device_id=(x, y, c), device_id_type=MESH))
        for cp in waits:
            cp.wait_recv()
        for cp in started:
            cp.wait_send()

    return pl.pallas_call(
        body, name=name, out_shape=[jax.ShapeDtypeStruct(b.shape, b.dtype) for b in bufs],
        in_specs=[_HBM] * n, out_specs=[_HBM] * n, input_output_aliases={a: a for a in range(n)},
        scratch_shapes=[pltpu.SemaphoreType.DMA((3 * n,)), pltpu.SemaphoreType.DMA((3 * n,))],
    )(*bufs)


def _proj_gather(h, buf, order):
    S, D = h.shape
    nb = buf.shape[2]
    tm = _pick(S, 1024, 16)
    tn = _pick(nb, 1408, 128)
    per = nb // tn
    nj, ni = N_CHIPS * per, S // tm
    rh = D // 2
    seq = [(0, t) for t in range(per)] + [(p, t) for t in range(per) for p in (1, 2)] + [(3, t) for t in range(per)]
    tile_chip = jnp.stack([order[p] for p, _ in seq])
    tile_of = jnp.array([t for _, t in seq], jnp.int32)

    def body(chip_ref, t_ref, h_ref, buf_in, proj_ref, buf_ref, wbuf, tile_sems, ici_send, ici_recv, d2d_send, d2d_recv):
        j, i = pl.program_id(0), pl.program_id(1)
        x, y, c, chips = _place()
        k_me = 2 * x + y

        def part(k, hc, t):
            return buf_ref.at[k, pl.ds(hc * rh, rh), pl.ds(t * tn, tn)]

        def ici(q, t, receiving):
            k_chip = 2 * chips[q][0] + chips[q][1]
            src, to = (part(k_chip, c, t), (x, y, c)) if receiving else (part(k_me, c, t), (*chips[q], c))
            return pltpu.make_async_remote_copy(src_ref=src, dst_ref=src, send_sem=ici_send.at[q * per + t],
                                                recv_sem=ici_recv.at[q * per + t], device_id=to, device_id_type=MESH)

        def d2d(q, t, receiving):
            k_chip = 2 * chips[q][0] + chips[q][1]
            src, to = (part(k_chip, 1 - c, t), (x, y, c)) if receiving else (part(k_chip, c, t), (x, y, 1 - c))
            return pltpu.make_async_remote_copy(src_ref=src, dst_ref=src, send_sem=d2d_send.at[q * per + t],
                                                recv_sem=d2d_recv.at[q * per + t], device_id=to, device_id_type=MESH)

        def tile_copy(n):
            col = pl.multiple_of(t_ref[n] * tn, 128)
            return pltpu.make_async_copy(buf_ref.at[chip_ref[n], :, pl.ds(col, tn)], wbuf.at[n % 2], tile_sems.at[n % 2])

        @pl.when(jnp.logical_and(j == 0, i == 0))
        def _():
            for t in range(per):
                ici(0, t, False).start()
                ici(1, t, False).start()
            tile_copy(0).start()

        @pl.when(i == 0)
        def _():
            tile_copy(j).wait()
            for n in range(per, nj):
                p, t = seq[n]

                @pl.when(j + 1 == n)
                def _():
                    ici(p - 1, t, True).wait_recv()
                    d2d(p - 1, t, False).start()
                    if (p, t) == (1, per - 1):
                        for q in range(2):
                            for tt in range(per):
                                ici(q, tt, False).wait_send()
                        for tt in range(per):
                            ici(2, tt, False).start()
                    d2d(p - 1, t, True).wait_recv()

            @pl.when(j + 1 < nj)
            def _():
                tile_copy(j + 1).start()

        proj_ref[...] = jnp.dot(h_ref[...], wbuf[j % 2], preferred_element_type=F32)

        @pl.when(jnp.logical_and(j == nj - 1, i == ni - 1))
        def _():
            for t in range(per):
                ici(2, t, False).wait_send()
                for q in range(3):
                    d2d(q, t, False).wait_send()

    n_sems = 3 * per
    return pl.pallas_call(
        body, name="proj",
        grid_spec=pltpu.PrefetchScalarGridSpec(
            num_scalar_prefetch=2, grid=(nj, ni),
            in_specs=[pl.BlockSpec((tm, D), lambda j, i, kc, kt: (i, 0)), _HBM],
            out_specs=[pl.BlockSpec((tm, tn), lambda j, i, kc, kt: (i, kc[j] * per + kt[j])), _HBM],
            scratch_shapes=[pltpu.VMEM((2, D, tn), BF16), pltpu.SemaphoreType.DMA((2,))]
            + [pltpu.SemaphoreType.DMA((n_sems,))] * 4),
        out_shape=[jax.ShapeDtypeStruct((S, N_CHIPS * nb), F32), jax.ShapeDtypeStruct(buf.shape, buf.dtype)],
        input_output_aliases={3: 1}, compiler_params=_params(("arbitrary", "arbitrary")),
    )(tile_chip, tile_of, h, buf)


def _swap_halves(name, grads):
    n = len(grads)

    def body(*refs):
        in_refs, out_refs = refs[:n], refs[n:2 * n]
        send_sems, recv_sems = refs[2 * n:]
        x, y, c, _ = _place()
        cps = []
        for a in range(n):
            rh = grads[a].shape[1] // 2
            cp = pltpu.make_async_remote_copy(
                src_ref=in_refs[a].at[:, pl.ds((1 - c) * rh, rh)], dst_ref=out_refs[a],
                send_sem=send_sems.at[a], recv_sem=recv_sems.at[a], device_id=(x, y, 1 - c), device_id_type=MESH)
            cp.start()
            cps.append(cp)
        for cp in cps:
            cp.wait()

    return pl.pallas_call(
        body, name=name,
        out_shape=[jax.ShapeDtypeStruct((N_CHIPS, g.shape[1] // 2, g.shape[2]), g.dtype) for g in grads],
        in_specs=[_HBM] * n, out_specs=[_HBM] * n,
        scratch_shapes=[pltpu.SemaphoreType.DMA((n,)), pltpu.SemaphoreType.DMA((n,))],
    )(*grads)


def _pair_sum(name, g, q, ck_idx):
    _, R, C = g.shape
    rh = R // 2
    tr = _pick(rh, max(16, (512 * 1024) // C // 16 * 16), 16)
    nh = rh // tr

    def body(ck_ref, g_ref, q_ref, o_ref, own_ref):
        s = (g_ref[...].astype(F32) + q_ref[...].astype(F32)).astype(BF16)
        o_ref[...] = s

        @pl.when(pl.program_id(1) == ck_ref[1])
        def _():
            own_ref[...] = s

    return pl.pallas_call(
        body, name=name,
        grid_spec=pltpu.PrefetchScalarGridSpec(
            num_scalar_prefetch=1, grid=(nh, N_CHIPS),
            in_specs=[pl.BlockSpec((None, tr, C), lambda i, k, ck: (k, ck[0] * nh + i, 0)),
                      pl.BlockSpec((None, tr, C), lambda i, k, ck: (k, i, 0))],
            out_specs=[pl.BlockSpec((None, tr, C), lambda i, k, ck: (k, i, 0)),
                       pl.BlockSpec((None, tr, C), lambda i, k, ck: (ck[1], i, 0))]),
        out_shape=[jax.ShapeDtypeStruct((N_CHIPS, rh, C), BF16)] * 2,
        compiler_params=_params(("parallel", "arbitrary")),
    )(ck_idx, g, q)


def _sum_chips(name, r, c_idx):
    _, rh, C = r.shape
    tr = _pick(rh, max(16, (256 * 1024) // C // 16 * 16), 16)

    def body(c_ref, r_ref, o_ref):
        acc = r_ref[0].astype(F32)
        for k in range(1, N_CHIPS):
            acc = acc + r_ref[k].astype(F32)
        o_ref[...] = acc

    return pl.pallas_call(
        body, name=name,
        grid_spec=pltpu.PrefetchScalarGridSpec(
            num_scalar_prefetch=1, grid=(rh // tr,),
            in_specs=[pl.BlockSpec((N_CHIPS, tr, C), lambda i, c_ref: (0, i, 0))],
            out_specs=pl.BlockSpec((None, tr, C), lambda i, c_ref: (c_ref[0], i, 0))),
        out_shape=jax.ShapeDtypeStruct((2, rh, C), F32), compiler_params=_params(("parallel",)),
    )(c_idx, r)


def _share_halves(bufs):
    n = len(bufs)

    def body(*refs):
        out_refs = refs[n:2 * n]
        send_sems, recv_sems = refs[2 * n:]
        x, y, c, _ = _place()
        cps = []
        for a in range(n):
            cp = pltpu.make_async_remote_copy(
                src_ref=out_refs[a].at[c], dst_ref=out_refs[a].at[c], send_sem=send_sems.at[a],
                recv_sem=recv_sems.at[a], device_id=(x, y, 1 - c), device_id_type=MESH)
            cp.start()
            cps.append(cp)
        for a, cp in enumerate(cps):
            got = out_refs[a].at[1 - c]
            pltpu.make_async_remote_copy(src_ref=got, dst_ref=got, send_sem=send_sems.at[a], recv_sem=recv_sems.at[a],
                                         device_id=(x, y, c), device_id_type=MESH).wait_recv()
        for cp in cps:
            cp.wait_send()

    return pl.pallas_call(
        body, name="share_halves",
        out_shape=[jax.ShapeDtypeStruct(b.shape, b.dtype) for b in bufs],
        in_specs=[_HBM] * n, out_specs=[_HBM] * n, input_output_aliases={a: a for a in range(n)},
        scratch_shapes=[pltpu.SemaphoreType.DMA((n,)), pltpu.SemaphoreType.DMA((n,))],
    )(*bufs)


_BIG = ("w_in", "w_conv_out", "w_hgrn_out", "w_o", "w_ffn_gate", "w_ffn_up", "w_ffn_down")
_WEIGHTS = ("w_ada", "b_ada", "norm_mix_g", "w_in", "conv_w", "lb_param", "gnorm_g", "w_conv_out", "w_hgrn_out",
            "w_o", "norm_ffn_g", "w_ffn_gate", "w_ffn_up", "w_ffn_down", "norm_final_g")


def _pack_rows(pieces):
    flat = jnp.concatenate([p.reshape(-1) for p in pieces])
    pad = (-flat.shape[0]) % (SUBLANES * LANES)
    return jnp.pad(flat, (0, pad)).reshape(SUBLANES, -1)


def kernel(x, c, w_ada, b_ada, norm_mix_g, w_in, conv_w, lb_param, gnorm_g, w_conv_out, w_hgrn_out, w_o, norm_ffn_g, w_ffn_gate, w_ffn_up, w_ffn_down, norm_final_g, loss_target, m_w_ada, m_b_ada, m_norm_mix_g, m_w_in, m_conv_w, m_lb_param, m_gnorm_g, m_w_conv_out, m_w_hgrn_out, m_w_o, m_norm_ffn_g, m_w_ffn_gate, m_w_ffn_up, m_w_ffn_down, m_norm_final_g, v_w_ada, v_b_ada, v_norm_mix_g, v_w_in, v_conv_w, v_lb_param, v_gnorm_g, v_w_conv_out, v_w_hgrn_out, v_w_o, v_norm_ffn_g, v_w_ffn_gate, v_w_ffn_up, v_w_ffn_down, v_norm_final_g):
    w = dict(w_ada=w_ada, b_ada=b_ada, norm_mix_g=norm_mix_g, w_in=w_in, conv_w=conv_w, lb_param=lb_param,
             gnorm_g=gnorm_g, w_conv_out=w_conv_out, w_hgrn_out=w_hgrn_out, w_o=w_o, norm_ffn_g=norm_ffn_g,
             w_ffn_gate=w_ffn_gate, w_ffn_up=w_ffn_up, w_ffn_down=w_ffn_down, norm_final_g=norm_final_g)
    m = dict(w_ada=m_w_ada, b_ada=m_b_ada, norm_mix_g=m_norm_mix_g, w_in=m_w_in, conv_w=m_conv_w, lb_param=m_lb_param,
             gnorm_g=m_gnorm_g, w_conv_out=m_w_conv_out, w_hgrn_out=m_w_hgrn_out, w_o=m_w_o, norm_ffn_g=m_norm_ffn_g,
             w_ffn_gate=m_w_ffn_gate, w_ffn_up=m_w_ffn_up, w_ffn_down=m_w_ffn_down, norm_final_g=m_norm_final_g)
    v = dict(w_ada=v_w_ada, b_ada=v_b_ada, norm_mix_g=v_norm_mix_g, w_in=v_w_in, conv_w=v_conv_w, lb_param=v_lb_param,
             gnorm_g=v_gnorm_g, w_conv_out=v_w_conv_out, w_hgrn_out=v_w_hgrn_out, w_o=v_w_o, norm_ffn_g=v_norm_ffn_g,
             w_ffn_gate=v_w_ffn_gate, w_ffn_up=v_w_ffn_up, w_ffn_down=v_w_ffn_down, norm_final_g=v_norm_final_g)
    two_d = lambda a: a.reshape((-1, a.shape[-1])) if a.ndim != 2 else a
    shapes = {k: a.shape for k, a in w.items()}
    w, m, v = ({k: two_d(a) for k, a in t.items()} for t in (w, m, v))
    w["lb_param"], m["lb_param"], v["lb_param"] = lb_param, m_lb_param, v_lb_param
    S, D = x.shape[1], x.shape[2]
    d_conv = D // 2
    ix, iy, ic = lax.axis_index("x"), lax.axis_index("y"), lax.axis_index("c")
    k_me = 2 * ix + iy
    dev = 2 * k_me + ic
    cw_shard = w["conv_w"].shape[1]
    ada_cols = w["w_ada"].shape[1]

    got = _allgather_rows("gather_cond", _pack_rows([c, w["conv_w"]])).reshape(N_DEV, -1)
    c_all = got[:, :D]
    conv_full = got[::2, D:D + 3 * cw_shard].reshape(N_CHIPS, 3, cw_shard).transpose(1, 0, 2).reshape(3, -1)
    c_act, c_act_b, lb = _prep(c_all, lb_param)
    b_cols = lax.dynamic_slice(w["b_ada"], (0, k_me * ada_cols), (1, ada_cols))
    mod_cols, = _matmul("ada_fwd", [(c_act_b, 'n', w["w_ada"].astype(BF16), 'n')], [0], N_DEV, ada_cols, D,
                        N_DEV, _pick(ada_cols, 1536, 128), D, [(b_cols, 'row', 0)], [(F32, None)],
                        lambda accs, ex: [accs[0] + ex[0]])
    mod_all = _allgather_rows("gather_mod", mod_cols).reshape(N_CHIPS, 2, N_DEV, ada_cols)[:, 0]
    mod_all = mod_all.transpose(1, 0, 2).reshape(N_DEV, -1)
    mod = lax.dynamic_slice(mod_all, (dev, 0), (1, mod_all.shape[1]))
    k_idx = jnp.reshape(k_me, (1,)).astype(jnp.int32)
    c_idx = jnp.reshape(ic, (1,)).astype(jnp.int32)
    bufs = {k: _cast_place("cast_" + k, w[k], k_idx) for k in _BIG}
    order = jnp.stack([k_me, 2 * (1 - ix) + iy, 2 * ix + (1 - iy), 2 * (1 - ix) + (1 - iy)]).astype(jnp.int32)

    loss, dx, small, arrived = _local_step(
        x[0], loss_target[0], mod, w["norm_mix_g"], lb, w["gnorm_g"], w["norm_ffn_g"], w["norm_final_g"], conv_full,
        bufs, c_idx, order)

    pieces = [small["dmod"], small["dg_mix"], small["dg_ffn"], small["dg_final"], small["dlb"], small["dgn"],
              small["dconv_w"], loss]
    sizes = [p.size for p in pieces]
    parts = _allgather_rows("gather_small", _pack_rows(pieces)).reshape(N_DEV, -1)
    total = _small_reduce(parts)
    offs = [0]
    for s in sizes:
        offs.append(offs[-1] + s)
    tot = [total[:, offs[i]:offs[i + 1]] for i in range(len(sizes))]
    dmod_all = parts[:, :sizes[0]]
    grads = {
        "b_ada": tot[0], "norm_mix_g": tot[1], "norm_ffn_g": tot[2], "norm_final_g": tot[3],
        "lb_param": _lb_grad(tot[4], lb), "gnorm_g": tot[5],
        "conv_w": lax.dynamic_slice(tot[6].reshape(3, -1), (0, k_me * cw_shard), (3, cw_shard)),
    }
    loss_out = tot[7][0, 0]

    for k in _BIG:
        grads[k] = arrived[k].reshape(-1, arrived[k].shape[-1])

    delta, new_m, new_v = {}, {}, {}
    dmod_cols = lax.dynamic_slice(dmod_all, (0, k_me * ada_cols), (N_DEV, ada_cols))
    grads["w_ada"], delta["w_ada"], new_m["w_ada"], new_v["w_ada"] = _ada_update(
        c_act.T, dmod_cols, w["w_ada"], m["w_ada"], v["w_ada"])
    for k in _WEIGHTS:
        if k != "w_ada":
            grads[k], delta[k], new_m[k], new_v[k] = _adamw("adamw_" + k, w[k], grads[k], m[k], v[k])
    outs = [loss_out, dx.reshape(x.shape)]
    for t in (grads, delta, new_m, new_v):
        outs += [t[k].reshape(shapes[k]) for k in _WEIGHTS]
    return tuple(outs)
```

```python
import functools

import jax
import jax.numpy as jnp
from jax import lax
from jax.experimental import pallas as pl
from jax.experimental.pallas import tpu as pltpu

F32 = jnp.float32
BF16 = jnp.bfloat16
MESH = pl.DeviceIdType.MESH

EPS = 1e-6
HEAD = 128
BLK = 16
N_CHIPS = 4
N_DEV = 8
SUBLANES, LANES = 8, 128
VMEM_LIMIT_BYTES = 56 * 1024 * 1024

ADAM_LR = 0.001
ADAM_B1 = 0.9
ADAM_B2 = 0.999
ADAM_EPS = 1e-08
ADAM_WD = 0.01
ADAM_STEP = 10


def _pick(dim, pref, mult):
    if dim <= pref:
        return dim
    t = (pref // mult) * mult
    while t >= mult:
        if dim % t == 0:
            return t
        t -= mult
    return dim


def _sig(x):
    return 1.0 / (1.0 + jnp.exp(-x))


def _params(sem):
    return pltpu.CompilerParams(dimension_semantics=sem, vmem_limit_bytes=VMEM_LIMIT_BYTES)


def _gj(j, i, k):
    return j


def _gk(j, i, k):
    return k


def _wspec(arr, rt, ct, r_of, c_of):
    if arr.ndim == 2:
        return pl.BlockSpec((rt, ct), lambda j, i, k: (r_of(j, i, k), c_of(j, i, k)))
    per = arr.shape[2] // ct
    assert arr.shape[2] % ct == 0
    return pl.BlockSpec((None, rt, ct),
                        lambda j, i, k: (c_of(j, i, k) // per, r_of(j, i, k), c_of(j, i, k) % per))


_HBM = pl.BlockSpec(memory_space=pltpu.HBM)


def _host(job, grid, in_specs, args, out_specs, out_shape, scratch):
    if job is None:
        return {}, (lambda body: body)
    n_in, n_out, n_scr = len(args), len(out_shape), len(scratch)
    n_job, n_alias = len(job.inputs), job.n_alias
    in_specs += [_HBM] * n_job
    args += job.inputs
    out_specs += [_HBM] * n_alias
    out_shape += [jax.ShapeDtypeStruct(a.shape, a.dtype) for a in job.inputs[n_job - n_alias:]]
    scratch += job.sem_shapes
    aliases = {n_in + n_job - n_alias + t: n_out + t for t in range(n_alias)}

    def wrap(body):
        def hosted(*refs):
            ins, refs = refs[:n_in], refs[n_in:]
            j_in, refs = refs[:n_job], refs[n_job:]
            outs, refs = refs[:n_out], refs[n_out:]
            j_out, refs = refs[:n_alias], refs[n_alias:]
            scr, sems = refs[:n_scr], refs[n_scr:]
            first = functools.reduce(jnp.logical_and, [pl.program_id(d) == 0 for d in range(len(grid))])
            last = functools.reduce(jnp.logical_and, [pl.program_id(d) == grid[d] - 1 for d in range(len(grid))])

            @pl.when(first)
            def _():
                job.start(j_in, j_out, sems)

            body(*ins, *outs, *scr)

            @pl.when(last)
            def _():
                job.finish(j_in, j_out, sems)

        return hosted

    return aliases, wrap


EPILOGUE_COLS = 768


def _plain(accs, extras):
    return accs


def _matmul(name, pairs, acc_of, M, N, K, tm, tn, tk, extras, outs, epilogue, job=None, rows_outer=False):
    assert M % tm == 0 and N % tn == 0 and K % tk == 0, (name, M, N, K, tm, tn, tk)
    nj, ni, nk = N // tn, M // tm, K // tk
    n_pairs, n_extra, n_out = len(pairs), len(extras), len(outs)
    n_acc = max(acc_of) + 1
    in_specs, args, dims = [], [], []
    lhs_of, seen = [], {}
    for a, am, b, bm in pairs:
        if (id(a), am) not in seen:
            seen[id(a), am] = len(args)
            args.append(a)
            if am == 'n':
                in_specs.append(pl.BlockSpec((tm, tk), lambda j, i, k: (i, k)))
            else:
                in_specs.append(pl.BlockSpec((tk, tm), lambda j, i, k: (k, i)))
        lhs_of.append(seen[id(a), am])
    n_lhs = len(args)
    for a, am, b, bm in pairs:
        if bm == 'n':
            in_specs.append(_wspec(b, tk, tn, _gk, _gj))
        else:
            in_specs.append(_wspec(b, tn, tk, _gj, _gk))
        dims.append((((1 if am == 'n' else 0,), (0 if bm == 'n' else 1,)), ((), ())))
        args.append(b)
    for e, kind, off in extras:
        assert off % tn == 0, (name, off, tn)
        o = off // tn
        if kind == 'tile':
            in_specs.append(pl.BlockSpec((tm, tn), lambda j, i, k, o=o: (i, j + o)))
        else:
            in_specs.append(pl.BlockSpec((1, tn), lambda j, i, k, o=o: (0, j + o)))
        args.append(e)
    out_shape, out_specs = [], []
    for dt, nb in outs:
        if nb is None:
            out_shape.append(jax.ShapeDtypeStruct((M, N), dt))
            out_specs.append(pl.BlockSpec((tm, tn), lambda j, i, k: (i, j)))
        else:
            assert nb % tn == 0 and N % nb == 0
            per = nb // tn
            out_shape.append(jax.ShapeDtypeStruct((N // nb, M, nb), dt))
            out_specs.append(pl.BlockSpec((None, tm, tn), lambda j, i, k, per=per: (j // per, i, j % per)))

    def body(*refs):
        n_ab = n_lhs + n_pairs
        e_refs = refs[n_ab:n_ab + n_extra]
        o_refs = refs[n_ab + n_extra:n_ab + n_extra + n_out]
        acc_refs = refs[n_ab + n_extra + n_out:]

        def products(cols):
            sums = [None] * n_acc
            for p in range(n_pairs):
                b_ref = refs[n_lhs + p]
                b = b_ref[:, cols] if pairs[p][3] == 'n' else b_ref[cols, :]
                prod = lax.dot_general(refs[lhs_of[p]][...], b, dims[p], preferred_element_type=F32)
                a = acc_of[p]
                sums[a] = prod if sums[a] is None else sums[a] + prod
            return sums

        def finish(accs, cols):
            res = epilogue(accs, [e[:, cols] for e in e_refs])
            for o_ref, r in zip(o_refs, res):
                o_ref[:, cols] = r.astype(o_ref.dtype)

        whole = slice(0, tn)
        if nk == 1:
            step = tn if epilogue is _plain else EPILOGUE_COLS
            for c0 in range(0, tn, step):
                cols = slice(c0, min(c0 + step, tn))
                finish(products(cols), cols)
            return
        sums = products(whole)
        k = pl.program_id(2)
        targets = o_refs if sum_in_outs else acc_refs

        @pl.when(k == 0)
        def _():
            for a in range(n_acc):
                targets[a][...] = sums[a]

        @pl.when(k > 0)
        def _():
            for a in range(n_acc):
                targets[a][...] += sums[a]

        if not sum_in_outs:
            @pl.when(k == nk - 1)
            def _():
                finish([acc_refs[a][...] for a in range(n_acc)], whole)

    sum_in_outs = epilogue is _plain and nk > 1 and n_out == n_acc and all(dt == F32 for dt, _ in outs)
    scratch = [pltpu.VMEM((tm, tn), F32) for _ in range(n_acc)] if nk > 1 and not sum_in_outs else []
    grid = (nj, ni, nk)
    if rows_outer:
        grid = (ni, nj, nk)
        swap = lambda spec: pl.BlockSpec(spec.block_shape, lambda i, j, k, f=spec.index_map: f(j, i, k))
        in_specs, out_specs = [swap(s) for s in in_specs], [swap(s) for s in out_specs]
    aliases, wrap = _host(job, grid, in_specs, args, out_specs, out_shape, scratch)
    sem = ("parallel", "parallel", "arbitrary") if job is None else ("arbitrary",) * 3
    return pl.pallas_call(
        wrap(body), name=name, grid=grid, in_specs=in_specs, out_specs=out_specs, out_shape=out_shape,
        scratch_shapes=scratch, input_output_aliases=aliases, compiler_params=_params(sem),
    )(*args)


def _row_spec(tr, d):
    return pl.BlockSpec((tr, d), lambda i: (i, 0))


def _vec_spec(d):
    return pl.BlockSpec((1, d), lambda i: (0, 0))


def _norm_mod(name, x, g, sc, sh, job=None):
    S, D = x.shape
    tr = _pick(S, 256, 8)

    def body(x_ref, g_ref, sc_ref, sh_ref, h_ref):
        xv = x_ref[...]
        r = lax.rsqrt(jnp.mean(xv * xv, axis=-1, keepdims=True) + EPS)
        h_ref[...] = ((xv * r) * g_ref[...] * (1.0 + sc_ref[...]) + sh_ref[...]).astype(BF16)

    grid = (S // tr,)
    in_specs = [_row_spec(tr, D), _vec_spec(D), _vec_spec(D), _vec_spec(D)]
    args = [x, g, sc, sh]
    out_specs, out_shape, scratch = [_row_spec(tr, D)], [jax.ShapeDtypeStruct((S, D), BF16)], []
    aliases, wrap = _host(job, grid, in_specs, args, out_specs, out_shape, scratch)
    res = pl.pallas_call(
        wrap(body), name=name, grid=grid, in_specs=in_specs, out_specs=out_specs, out_shape=out_shape,
        scratch_shapes=scratch, input_output_aliases=aliases,
        compiler_params=_params(("parallel" if job is None else "arbitrary",)),
    )(*args)
    return res[0] if job is None else res


def _loss_head(x2, target, g, ff, gt):
    S, D = x2.shape
    tr = _pick(S, 256, 8)

    def body(x_ref, t_ref, g_ref, ff_ref, gt_ref, dx_ref, dffb_ref, loss_ref, dgt_ref, dg_ref):
        i = pl.program_id(0)

        @pl.when(i == 0)
        def _():
            loss_ref[...] = jnp.zeros_like(loss_ref)
            dgt_ref[...] = jnp.zeros_like(dgt_ref)
            dg_ref[...] = jnp.zeros_like(dg_ref)

        xv = x_ref[...]
        gv = g_ref[...]
        r = lax.rsqrt(jnp.mean(xv * xv, axis=-1, keepdims=True) + EPS)
        xh = xv * r
        err = xh * gv - t_ref[...]
        loss_ref[...] += 0.5 * jnp.sum(jnp.mean(err * err, axis=-1, keepdims=True))
        dy = err * (1.0 / D)
        dg_ref[...] += jnp.sum(dy * xh, axis=0, keepdims=True)
        dxh = dy * gv
        dx = r * (dxh - xh * jnp.mean(dxh * xh, axis=-1, keepdims=True))
        dx_ref[...] = dx
        dffb_ref[...] = (dx * gt_ref[...]).astype(BF16)
        dgt_ref[...] += jnp.sum(dx * ff_ref[...], axis=0, keepdims=True)

    return pl.pallas_call(
        body, name="loss_head", grid=(S // tr,),
        in_specs=[_row_spec(tr, D), _row_spec(tr, D), _vec_spec(D), _row_spec(tr, D), _vec_spec(D)],
        out_specs=[_row_spec(tr, D), _row_spec(tr, D), pl.BlockSpec((1, 128), lambda i: (0, 0)),
                   _vec_spec(D), _vec_spec(D)],
        out_shape=[jax.ShapeDtypeStruct((S, D), F32), jax.ShapeDtypeStruct((S, D), BF16),
                   jax.ShapeDtypeStruct((1, 128), F32), jax.ShapeDtypeStruct((1, D), F32),
                   jax.ShapeDtypeStruct((1, D), F32)],
        compiler_params=_params(("arbitrary",)),
    )(x2, target, g, ff, gt)


def _norm_mod_bwd(name, dh, x, g, sc, dres, gated=None, job=None):
    S, D = x.shape
    tr = _pick(S, 256, 8)
    n = S // tr
    with_gate = gated is not None

    def body(*refs):
        if with_gate:
            dh_ref, x_ref, g_ref, sc_ref, dres_ref, mo_ref, gt_ref, dx_ref, dsh_ref, dsc_ref, dg_ref, dmob_ref, dgt_ref = refs
        else:
            dh_ref, x_ref, g_ref, sc_ref, dres_ref, dx_ref, dsh_ref, dsc_ref, dg_ref = refs
        i = pl.program_id(0)

        @pl.when(i == 0)
        def _():
            dsh_ref[...] = jnp.zeros_like(dsh_ref)
            dsc_ref[...] = jnp.zeros_like(dsc_ref)
            if with_gate:
                dgt_ref[...] = jnp.zeros_like(dgt_ref)

        xv = x_ref[...]
        dhv = dh_ref[...]
        gv = g_ref[...]
        scale = 1.0 + sc_ref[...]
        r = lax.rsqrt(jnp.mean(xv * xv, axis=-1, keepdims=True) + EPS)
        xh = xv * r
        dsh_ref[...] += jnp.sum(dhv, axis=0, keepdims=True)
        dsc_ref[...] += jnp.sum(dhv * xh, axis=0, keepdims=True)
        dxh = dhv * (gv * scale)
        dx = dres_ref[...] + r * (dxh - xh * jnp.mean(dxh * xh, axis=-1, keepdims=True))
        dx_ref[...] = dx
        if with_gate:
            dmob_ref[...] = (dx * gt_ref[...]).astype(BF16)
            dgt_ref[...] += jnp.sum(dx * mo_ref[...], axis=0, keepdims=True)

        @pl.when(i == n - 1)
        def _():
            t = dsc_ref[...]
            dg_ref[...] = t * scale
            dsc_ref[...] = t * gv

    in_specs = [_row_spec(tr, D), _row_spec(tr, D), _vec_spec(D), _vec_spec(D), _row_spec(tr, D)]
    args = [dh, x, g, sc, dres]
    out_specs = [_row_spec(tr, D), _vec_spec(D), _vec_spec(D), _vec_spec(D)]
    out_shape = [jax.ShapeDtypeStruct((S, D), F32)] + [jax.ShapeDtypeStruct((1, D), F32)] * 3
    if with_gate:
        in_specs += [_row_spec(tr, D), _vec_spec(D)]
        args += list(gated)
        out_specs += [_row_spec(tr, D), _vec_spec(D)]
        out_shape += [jax.ShapeDtypeStruct((S, D), BF16), jax.ShapeDtypeStruct((1, D), F32)]
    scratch = []
    aliases, wrap = _host(job, (n,), in_specs, args, out_specs, out_shape, scratch)
    return pl.pallas_call(
        wrap(body), name=name, grid=(n,), in_specs=in_specs, out_specs=out_specs, out_shape=out_shape,
        scratch_shapes=scratch, input_output_aliases=aliases, compiler_params=_params(("arbitrary",)),
    )(*args)


CONV_ROWS = 256


def _col_spec(S, off_cols):
    o = off_cols // HEAD
    return pl.BlockSpec((S, HEAD), lambda c, o=o: (0, c + o))


def _conv_taps(u, u_prev, rid):
    s1 = jnp.where(rid >= 1, pltpu.roll(u, 1, 0), pltpu.roll(u_prev, 1, 0))
    s2 = jnp.where(rid >= 2, pltpu.roll(u, 2, 0), pltpu.roll(u_prev, 2, 0))
    return s1, s2


def _conv_fwd(proj, conv_w, d_conv, job=None):
    S = proj.shape[0]
    R = min(CONV_ROWS, S)
    nr = S // R

    def body(ab_ref, ac_ref, ax_ref, w_ref, ya_ref):
        w0, w1, w2 = w_ref[0:1, :], w_ref[1:2, :], w_ref[2:3, :]
        rid = lax.broadcasted_iota(jnp.int32, (R, HEAD), 0)

        def step(c, carry):
            rows = pl.ds(pl.multiple_of(c * R, R), R)
            prev = pl.ds(pl.multiple_of(jnp.maximum(c - 1, 0) * R, R), R)
            u = ac_ref[rows, :] * ax_ref[rows, :]
            u_prev = jnp.where(c > 0, ac_ref[prev, :] * ax_ref[prev, :], 0.0)
            s1, s2 = _conv_taps(u, u_prev, rid)
            y = w0 * s2 + w1 * s1 + w2 * u
            ya_ref[rows, :] = (ab_ref[rows, :] * y).astype(BF16)
            return carry

        lax.fori_loop(0, nr, step, 0)

    grid = (d_conv // HEAD,)
    in_specs = [_col_spec(S, 0), _col_spec(S, d_conv), _col_spec(S, 2 * d_conv), pl.BlockSpec((3, HEAD), lambda c: (0, c))]
    args = [proj, proj, proj, conv_w]
    out_specs, out_shape = [pl.BlockSpec((S, HEAD), lambda c: (0, c))], [jax.ShapeDtypeStruct((S, d_conv), BF16)]
    scratch = []
    aliases, wrap = _host(job, grid, in_specs, args, out_specs, out_shape, scratch)
    res = pl.pallas_call(
        wrap(body), name="conv_fwd", grid=grid, in_specs=in_specs, out_specs=out_specs, out_shape=out_shape,
        scratch_shapes=scratch, input_output_aliases=aliases,
        compiler_params=_params(("parallel" if job is None else "arbitrary",)),
    )(*args)
    return res[0] if job is None else res


def _conv_bwd(dya, proj, conv_w, d_conv, job=None):
    S = proj.shape[0]
    R = min(CONV_ROWS, S)
    nr = S // R

    def body(dya_ref, ab_ref, ac_ref, ax_ref, w_ref, dab_ref, dac_ref, dax_ref, dw_ref):
        w0, w1, w2 = w_ref[0:1, :], w_ref[1:2, :], w_ref[2:3, :]
        rid = lax.broadcasted_iota(jnp.int32, (R, HEAD), 0)

        def step(c, carry):
            dw0, dw1, dw2 = carry
            rows = pl.ds(pl.multiple_of(c * R, R), R)
            prev = pl.ds(pl.multiple_of(jnp.maximum(c - 1, 0) * R, R), R)
            nxt = pl.ds(pl.multiple_of(jnp.minimum(c + 1, nr - 1) * R, R), R)
            a_c, a_x, a_b = ac_ref[rows, :], ax_ref[rows, :], ab_ref[rows, :]
            u = a_c * a_x
            u_prev = jnp.where(c > 0, ac_ref[prev, :] * ax_ref[prev, :], 0.0)
            s1, s2 = _conv_taps(u, u_prev, rid)
            y = w0 * s2 + w1 * s1 + w2 * u
            dy = dya_ref[rows, :]
            dab_ref[rows, :] = (dy * y).astype(BF16)
            dyc = dy * a_b
            dyc_next = jnp.where(c < nr - 1, dya_ref[nxt, :] * ab_ref[nxt, :], 0.0)
            t1 = jnp.where(rid < R - 1, pltpu.roll(dyc, R - 1, 0), pltpu.roll(dyc_next, R - 1, 0))
            t2 = jnp.where(rid < R - 2, pltpu.roll(dyc, R - 2, 0), pltpu.roll(dyc_next, R - 2, 0))
            du = w2 * dyc + w1 * t1 + w0 * t2
            dac_ref[rows, :] = (du * a_x).astype(BF16)
            dax_ref[rows, :] = (du * a_c).astype(BF16)
            dw0 = dw0 + jnp.sum(dyc * s2, axis=0, keepdims=True)
            dw1 = dw1 + jnp.sum(dyc * s1, axis=0, keepdims=True)
            dw2 = dw2 + jnp.sum(dyc * u, axis=0, keepdims=True)
            return dw0, dw1, dw2

        z = jnp.zeros((1, HEAD), F32)
        dw0, dw1, dw2 = lax.fori_loop(0, nr, step, (z, z, z))
        dw_ref[0:1, :] = dw0
        dw_ref[1:2, :] = dw1
        dw_ref[2:3, :] = dw2

    col = pl.BlockSpec((S, HEAD), lambda c: (0, c))
    grid = (d_conv // HEAD,)
    in_specs = [col, _col_spec(S, 0), _col_spec(S, d_conv), _col_spec(S, 2 * d_conv),
                pl.BlockSpec((3, HEAD), lambda c: (0, c))]
    args = [dya, proj, proj, proj, conv_w]
    out_specs = [col, col, col, pl.BlockSpec((3, HEAD), lambda c: (0, c))]
    out_shape = [jax.ShapeDtypeStruct((S, d_conv), BF16)] * 3 + [jax.ShapeDtypeStruct((3, d_conv), F32)]
    scratch = []
    aliases, wrap = _host(job, grid, in_specs, args, out_specs, out_shape, scratch)
    return pl.pallas_call(
        wrap(body), name="conv_bwd", grid=grid, in_specs=in_specs, out_specs=out_specs, out_shape=out_shape,
        scratch_shapes=scratch, input_output_aliases=aliases,
        compiler_params=_params(("parallel" if job is None else "arbitrary",)),
    )(*args)


HGRN_FWD_ROWS = 512
HGRN_BWD_ROWS = 1024
HGRN_FWD_SUB = 64
HGRN_BWD_SUB = 32
HGRN_GATE_ROWS = 128


def _block_cumsum(x, pos):
    for s in (1, 2, 4, 8):
        x = x + jnp.where(pos >= s, pltpu.roll(x, s, 0), 0.0)
    return x


def _block_rev_cumsum(x, pos, rows):
    for s in (1, 2, 4, 8):
        x = x + jnp.where(pos < BLK - s, pltpu.roll(x, rows - s, 0), 0.0)
    return x


def _block_last(x, pos, rows):
    m = jnp.where(pos == BLK - 1, x, 0.0)
    for s in (1, 2, 4, 8):
        m = m + pltpu.roll(m, rows - s, 0)
    return m


def _hgrn_gates(q, z, lb):
    sgq = _sig(q)
    qs = q * sgq
    sg = _sig(z)
    f = lb + (1.0 - lb) * sg
    kk = (1.0 - lb) * (1.0 - sg)
    return sgq, qs, sg, f, kk


def _hgrn_decays(q, z, lb, pos, rows):
    sgq, qs, sg, f, kk = _hgrn_gates(q, z, lb)
    b = _block_cumsum(jnp.log(f), pos)
    return sgq, qs, sg, f, kk, b, _block_last(b, pos, rows)


def _hgrn_specs(T, d_conv, n_t, rev):
    def t_of(i):
        return (n_t - 1 - i) if rev else i

    def pcol(off):
        o = off // HEAD
        return pl.BlockSpec((T, HEAD), lambda h, i, o=o: (t_of(i), h + o))

    q_spec, z_spec, v_spec, g_spec = pcol(3 * d_conv), pcol(4 * d_conv), pcol(5 * d_conv), pcol(6 * d_conv)
    lb_spec = pl.BlockSpec((1, HEAD), lambda h, i: (0, h))
    gn_spec = pl.BlockSpec((1, HEAD), lambda h, i: (0, 0))
    act_spec = pl.BlockSpec((T, HEAD), lambda h, i: (t_of(i), h))
    st_spec = pl.BlockSpec((None, T // BLK, HEAD, HEAD), lambda h, i: (h, t_of(i), 0, 0))
    return q_spec, z_spec, v_spec, g_spec, lb_spec, gn_spec, act_spec, st_spec


def _hgrn_fwd(proj, lb, gn, d_conv, job=None):
    S = proj.shape[0]
    H = d_conv // HEAD
    T = min(HGRN_FWD_ROWS, S)
    n_t, nb = S // T, T // BLK
    sub = min(HGRN_FWD_SUB, T)
    q_spec, z_spec, v_spec, g_spec, lb_spec, gn_spec, act_spec, st_spec = _hgrn_specs(T, d_conv, n_t, False)

    def body(q_ref, z_ref, v_ref, g_ref, lb_ref, gn_ref, ob_ref, o_ref, st_ref, qe_s, ke_s, dec_s, state, v_s, o_s, u_s):
        @pl.when(pl.program_id(1) == 0)
        def _():
            state[...] = jnp.zeros_like(state)

        pos = lax.broadcasted_iota(jnp.int32, (sub, HEAD), 0) % BLK
        lbv = lb_ref[...]

        def vector_pass(s, carry):
            r = pl.ds(pl.multiple_of(s * sub, sub), sub)
            v = v_ref[r, :]
            _, qs, _, _, kk, b, b_tot = _hgrn_decays(q_ref[r, :], z_ref[r, :], lbv, pos, sub)
            qe_s[r, :] = (qs * jnp.exp(b)).astype(BF16)
            ke_s[r, :] = (kk * jnp.exp(b_tot - b)).astype(BF16)
            v_s[r, :] = v.astype(BF16)
            dec_s[r, :] = jnp.exp(b_tot)
            o = jnp.sum(qs * kk, axis=-1, keepdims=True) * v
            for d in range(1, BLK):
                e = jnp.exp(b - pltpu.roll(b, d, 0))
                a = jnp.sum(qs * pltpu.roll(kk, d, 0) * e, axis=-1, keepdims=True)
                o = o + jnp.where(pos >= d, a * pltpu.roll(v, d, 0), 0.0)
            o_s[r, :] = o
            return carry

        lax.fori_loop(0, T // sub, vector_pass, 0)

        for j in range(nb):
            rows = pl.ds(j * BLK, BLK)
            u_s[j] = lax.dot_general(v_s[rows, :], ke_s[rows, :], (((0,), (0,)), ((), ())),
                                     preferred_element_type=F32)
        st = state[...]
        for j in range(nb):
            st_ref[j] = st.astype(BF16)
            st = st * dec_s[pl.ds(j * BLK, 1), :] + u_s[j]
        state[...] = st
        for j in range(nb):
            rows = pl.ds(j * BLK, BLK)
            o_s[rows, :] += lax.dot_general(qe_s[rows, :], st_ref[j], (((1,), (1,)), ((), ())),
                                            preferred_element_type=F32)
        o = o_s[...]
        o_ref[...] = o
        r = lax.rsqrt(jnp.mean(o * o, axis=-1, keepdims=True) + EPS)
        g = g_ref[...]
        ob_ref[...] = ((o * r) * gn_ref[...] * (g * _sig(g))).astype(BF16)

    grid = (H, n_t)
    in_specs = [q_spec, z_spec, v_spec, g_spec, lb_spec, gn_spec]
    args = [proj, proj, proj, proj, lb, gn]
    out_specs = [act_spec, act_spec, st_spec, act_spec, act_spec, act_spec]
    out_shape = [jax.ShapeDtypeStruct((S, d_conv), BF16), jax.ShapeDtypeStruct((S, d_conv), F32),
                 jax.ShapeDtypeStruct((H, S // BLK, HEAD, HEAD), BF16),
                 jax.ShapeDtypeStruct((S, d_conv), BF16), jax.ShapeDtypeStruct((S, d_conv), BF16),
                 jax.ShapeDtypeStruct((S, d_conv), F32)]
    scratch = [pltpu.VMEM((HEAD, HEAD), F32), pltpu.VMEM((T, HEAD), BF16), pltpu.VMEM((T, HEAD), F32),
               pltpu.VMEM((nb, HEAD, HEAD), F32)]
    aliases, wrap = _host(job, grid, in_specs, args, out_specs, out_shape, scratch)
    return pl.pallas_call(
        wrap(body), name="hgrn_fwd", grid=grid, in_specs=in_specs, out_specs=out_specs, out_shape=out_shape,
        scratch_shapes=scratch, input_output_aliases=aliases,
        compiler_params=_params(("parallel" if job is None else "arbitrary", "arbitrary")),
    )(*args)


def _hgrn_bwd(dob, o_raw, states, qe, ke, dec, proj, lb, gn, d_conv, job=None):
    S = proj.shape[0]
    H = d_conv // HEAD
    T = min(HGRN_BWD_ROWS, S)
    n_t, nb = S // T, T // BLK
    sub = min(HGRN_BWD_SUB, T)
    gate_rows = min(HGRN_GATE_ROWS, T)
    q_spec, z_spec, v_spec, g_spec, lb_spec, gn_spec, act_spec, st_spec = _hgrn_specs(T, d_conv, n_t, True)

    def body(dob_ref, o_ref, st_ref, qe_b, ke_b, dec_s, q_ref, z_ref, v_ref, g_ref, lb_ref, gn_ref,
             dq_ref, dz_ref, dv_ref, dg_ref, dlb_ref, dgn_ref,
             dstate, do_b, v_b, dqe_s, dke_s, dvi_s, ddec_s, do_s, u_s, ds_s):
        @pl.when(pl.program_id(1) == 0)
        def _():
            dstate[...] = jnp.zeros_like(dstate)
            dlb_ref[...] = jnp.zeros_like(dlb_ref)
            dgn_ref[...] = jnp.zeros_like(dgn_ref)

        pos = lax.broadcasted_iota(jnp.int32, (sub, HEAD), 0) % BLK
        lbv, gnv = lb_ref[...], gn_ref[...]

        def before(s, carry):
            r = pl.ds(pl.multiple_of(s * gate_rows, gate_rows), gate_rows)
            g = g_ref[r, :]
            v_b[r, :] = v_ref[r, :].astype(BF16)
            o = o_ref[r, :]
            rs = lax.rsqrt(jnp.mean(o * o, axis=-1, keepdims=True) + EPS)
            on = o * rs
            sgg = _sig(g)
            dobv = dob_ref[r, :]
            dog = dobv * (g * sgg)
            dgn_ref[...] += jnp.sum(dog * on, axis=0, keepdims=True)
            don = dog * gnv
            do = rs * (don - on * jnp.mean(don * on, axis=-1, keepdims=True))
            dg_ref[r, :] = (dobv * (on * gnv) * (sgg * (1.0 + g * (1.0 - sgg)))).astype(BF16)
            do_s[r, :] = do
            do_b[r, :] = do.astype(BF16)
            return carry

        lax.fori_loop(0, T // gate_rows, before, 0)

        for j in range(nb):
            rows = pl.ds(j * BLK, BLK)
            dob_j = do_b[rows, :]
            u_s[j] = lax.dot_general(dob_j, qe_b[rows, :], (((0,), (0,)), ((), ())), preferred_element_type=F32)
            dqe_s[rows, :] = lax.dot_general(dob_j, st_ref[j], (((1,), (0,)), ((), ())), preferred_element_type=F32)
        dst = dstate[...]
        for j in reversed(range(nb)):
            ds_s[j] = dst.astype(BF16)
            ddec = jnp.sum(st_ref[j].astype(F32) * dst, axis=0, keepdims=True)
            ddec_s[pl.ds(j * BLK, BLK), :] = jnp.broadcast_to(ddec, (BLK, HEAD))
            dst = dst * dec_s[pl.ds(j * BLK, 1), :] + u_s[j]
        dstate[...] = dst
        for j in range(nb):
            rows = pl.ds(j * BLK, BLK)
            dke_s[rows, :] = lax.dot_general(v_b[rows, :], ds_s[j], (((1,), (0,)), ((), ())), preferred_element_type=F32)
            dvi_s[rows, :] = lax.dot_general(ke_b[rows, :], ds_s[j], (((1,), (1,)), ((), ())), preferred_element_type=F32)

        def after(s, carry):
            r = pl.ds(pl.multiple_of(s * sub, sub), sub)
            q, v = q_ref[r, :], v_ref[r, :]
            sgq, qs, sg, f, kk, b, b_tot = _hgrn_decays(q, z_ref[r, :], lbv, pos, sub)
            do = do_s[r, :]
            dqs = dqe_s[r, :] * jnp.exp(b)
            dkk = dke_s[r, :] * jnp.exp(b_tot - b)
            d_tot = jnp.where(pos == BLK - 1, _block_cumsum(dkk * kk, pos) + ddec_s[r, :] * jnp.exp(b_tot), 0.0)
            dv = dvi_s[r, :]
            da = jnp.sum(do * v, axis=-1, keepdims=True)
            dv = dv + jnp.sum(qs * kk, axis=-1, keepdims=True) * do
            dqs = dqs + da * kk
            dkk = dkk + da * qs
            for d in range(1, BLK):
                kd = pltpu.roll(kk, d, 0)
                e = jnp.where(pos >= d, jnp.exp(b - pltpu.roll(b, d, 0)), 0.0)
                a = jnp.sum(qs * kd * e, axis=-1, keepdims=True)
                da = jnp.sum(do * pltpu.roll(v, d, 0), axis=-1, keepdims=True)
                gq = da * e
                dqs = dqs + gq * kd
                dkk = dkk + pltpu.roll(gq * qs, sub - d, 0)
                dv = dv + pltpu.roll(a * do, sub - d, 0)
            db = qs * dqs - kk * dkk + d_tot
            dlf = _block_rev_cumsum(db, pos, sub)
            df = dlf / f - dkk
            dlb_ref[...] += jnp.sum(df * (1.0 - sg), axis=0, keepdims=True)
            dz_ref[r, :] = (df * (1.0 - lbv) * sg * (1.0 - sg)).astype(BF16)
            dq_ref[r, :] = (dqs * (sgq * (1.0 + q * (1.0 - sgq)))).astype(BF16)
            dv_ref[r, :] = dv.astype(BF16)
            return carry

        lax.fori_loop(0, T // sub, after, 0)

    tb = lambda dt: pltpu.VMEM((T, HEAD), dt)
    grid = (H, n_t)
    in_specs = [act_spec, act_spec, st_spec, act_spec, act_spec, act_spec, q_spec, z_spec, v_spec, g_spec, lb_spec,
                gn_spec]
    args = [dob, o_raw, states, qe, ke, dec, proj, proj, proj, proj, lb, gn]
    out_specs = [act_spec, act_spec, act_spec, act_spec, lb_spec, pl.BlockSpec((None, 1, HEAD), lambda h, i: (h, 0, 0))]
    out_shape = ([jax.ShapeDtypeStruct((S, d_conv), BF16)] * 4
                 + [jax.ShapeDtypeStruct((1, d_conv), F32), jax.ShapeDtypeStruct((H, 1, HEAD), F32)])
    scratch = [pltpu.VMEM((HEAD, HEAD), F32), tb(BF16), tb(BF16), tb(F32), tb(F32), tb(F32), tb(F32), tb(F32),
               pltpu.VMEM((nb, HEAD, HEAD), F32), pltpu.VMEM((nb, HEAD, HEAD), BF16)]
    aliases, wrap = _host(job, grid, in_specs, args, out_specs, out_shape, scratch)
    return pl.pallas_call(
        wrap(body), name="hgrn_bwd", grid=grid, in_specs=in_specs, out_specs=out_specs, out_shape=out_shape,
        scratch_shapes=scratch, input_output_aliases=aliases,
        compiler_params=_params(("parallel" if job is None else "arbitrary", "arbitrary")),
    )(*args)


def _local_step(x, target, mod, norm_mix_g, lb, gnorm_g, norm_ffn_g, norm_final_g, conv_w, bufs, c_idx, order):
    S, D = x.shape
    d_conv = D // 2
    d_in = 7 * d_conv + 2 * D
    d_ff = N_CHIPS * bufs["w_ffn_down"].shape[1]
    nb_in, nb_co, nb_ff = bufs["w_in"].shape[2], bufs["w_conv_out"].shape[2], bufs["w_ffn_gate"].shape[2]
    rows = lambda g: g.reshape(-1, g.shape[2])
    sh_m, sc_m, gt_m, sh_f, sc_f, gt_f = [mod[:, i * D:(i + 1) * D] for i in range(6)]
    tm = _pick(S, 512, 16)
    tm2 = _pick(S, 1024, 16)
    tn_in = _pick(nb_in, 1408, 128)
    tn_co = _pick(nb_co, 512, 128)
    tn_ff = _pick(nb_ff, 1408, 128)
    tn_d = _pick(D, 512, 128)
    tw = _pick(D, 1024, 128)
    tg = _pick(D, 512, 128)

    h = _norm_mod("norm_mix", x, norm_mix_g, sc_m, sh_m)
    proj, w_in = _proj_gather(h, bufs["w_in"], order)
    ob, o_raw, states, qe, ke, dec, *got = _hgrn_fwd(
        proj, lb, gnorm_g, d_conv,
        job=_GatherJob([bufs[k] for k in ("w_conv_out", "w_hgrn_out", "w_o", "w_ffn_gate")]))
    ya, w_conv_out, w_hgrn_out, w_o, w_ffn_gate, w_ffn_up = _conv_fwd(
        proj, conv_w, d_conv, job=_Jobs([_ForwardJob(got), _GatherJob([bufs["w_ffn_up"]], 0, 2)]))
    w_o = rows(w_o)

    def merge_epi(accs, ex):
        y_a, y_b = accs
        return [y_a, y_b, _sig(ex[0]) * y_a + _sig(ex[1]) * y_b]

    y_a, y_b, merged, w_ffn_up = _matmul(
        "branch_out", [(ya, 'n', w_conv_out, 'n'), (ob, 'n', w_hgrn_out, 'n')], [0, 1], S, D, d_conv, tm2, tn_co, d_conv,
        [(proj, 'tile', 7 * d_conv), (proj, 'tile', 7 * d_conv + D)], [(BF16, None), (BF16, None), (BF16, None)], merge_epi,
        job=_GatherJob([w_ffn_up], 2, 5), rows_outer=True)

    def resid_epi(accs, ex):
        return [accs[0], ex[0] + ex[1] * accs[0]]

    mo, x1, w_ffn_up = _matmul("w_o", [(merged, 'n', w_o, 'n')], [0], S, D, D, tm2, tw, D,
                               [(x, 'tile', 0), (gt_m, 'row', 0)], [(BF16, None), (F32, None)], resid_epi,
                               job=_GatherJob([w_ffn_up], 5, 8))
    h2, w_ffn_up = _norm_mod("norm_ffn", x1, norm_ffn_g, sc_f, sh_f, job=_ForwardJob([w_ffn_up]))

    def swiglu_epi(accs, ex):
        gg, uu = accs
        return [gg, uu, gg * _sig(gg) * uu]

    G, U, act, w_ffn_down = _matmul(
        "ffn_in", [(h2, 'n', w_ffn_gate, 'n'), (h2, 'n', w_ffn_up, 'n')], [0, 1], S, d_ff, D,
        tm2, tn_ff, D, [], [(BF16, None), (BF16, None), (BF16, None)], swiglu_epi, job=_GatherJob([bufs["w_ffn_down"]]))
    w_ffn_down = rows(_forward_halves("forward_down", [w_ffn_down])[0])
    ff, x2 = _matmul("ffn_out", [(act, 'n', w_ffn_down, 'n')], [0], S, D, d_ff, tm2, tn_d, d_ff,
                     [(x1, 'tile', 0), (gt_f, 'row', 0)], [(BF16, None), (F32, None)], resid_epi, rows_outer=True)

    dx2, dffb, loss, dgt_f, dg_final = _loss_head(x2, target, norm_final_g, ff, gt_f)

    def swiglu_bwd_epi(accs, ex):
        dact, gg, uu = accs[0], ex[0], ex[1]
        s = _sig(gg)
        return [dact * uu * (s * (1.0 + gg * (1.0 - s))), dact * (gg * s)]

    dG, dU = _matmul("d_act", [(dffb, 'n', w_ffn_down, 't')], [0], S, d_ff, D, tm2, tn_ff, D,
                     [(G, 'tile', 0), (U, 'tile', 0)], [(BF16, None), (BF16, None)], swiglu_bwd_epi)
    dw_down, = _matmul("dw_ffn_down", [(act, 't', dffb, 'n')], [0], d_ff, D, S, _pick(d_ff, 512, 128),
                       D, S, [], [(BF16, None)], _plain)
    dh2, = _matmul("d_h2", [(dG, 'n', w_ffn_gate, 't'), (dU, 'n', w_ffn_up, 't')], [0, 0], S, D, d_ff,
                   tm, D, tn_ff, [], [(F32, None)], _plain)
    dw_gate, = _matmul("dw_ffn_gate", [(h2, 't', dG, 'n')], [0], D, d_ff, S, tw, tn_ff, S, [], [(BF16, nb_ff)], _plain)
    dw_up, = _matmul("dw_ffn_up", [(h2, 't', dU, 'n')], [0], D, d_ff, S, tw, tn_ff, S, [], [(BF16, nb_ff)], _plain)

    ck_idx = jnp.stack([c_idx[0], order[0]])

    def pair_sums(names, grads, from_sibling):
        pairs = [_pair_sum("pair_sum_" + k, g, q, ck_idx) for k, g, q in zip(names, grads, from_sibling)]
        return [p[0] for p in pairs], [p[1] for p in pairs]

    ffn_names = ["w_ffn_down", "w_ffn_gate", "w_ffn_up"]
    ffn_grads = [dw_down.reshape(N_CHIPS, -1, D), dw_gate, dw_up]
    dx1, dsh_f, dsc_f, dg_ffn, dmob, dgt_m, *from_sibling = _norm_mod_bwd(
        "norm_ffn_bwd", dh2, x1, norm_ffn_g, sc_f, dx2, (mo, gt_m), job=_SwapJob(ffn_grads))
    parts_f, slots_f = pair_sums(ffn_names, ffn_grads, from_sibling)

    def merge_bwd_epi(accs, ex):
        dm, ga, gb, ya_, yb_ = accs[0], ex[0], ex[1], ex[2], ex[3]
        sa, sb = _sig(ga), _sig(gb)
        return [dm * ya_ * sa * (1.0 - sa), dm * yb_ * sb * (1.0 - sb), dm * sa, dm * sb]

    dga, dgb, dy_a, dy_b = _matmul(
        "d_merged", [(dmob, 'n', w_o, 't')], [0], S, D, D, tm2, tn_co, D,
        [(proj, 'tile', 7 * d_conv), (proj, 'tile', 7 * d_conv + D), (y_a, 'tile', 0), (y_b, 'tile', 0)],
        [(BF16, None)] * 4, merge_bwd_epi, rows_outer=True)
    dw_o, = _matmul("dw_o", [(merged, 't', dmob, 'n')], [0], D, D, S, tg, D, S, [], [(BF16, None)], _plain)
    dya, dob = _matmul("d_branch", [(dy_a, 'n', w_conv_out, 't'), (dy_b, 'n', w_hgrn_out, 't')], [0, 1], S, d_conv, D,
                       tm2, _pick(d_conv, 1024, 128), tn_co, [], [(F32, None), (F32, None)], _plain)
    dw_co, dw_ho = _matmul("dw_branch", [(ya, 't', dy_a, 'n'), (ob, 't', dy_b, 'n')], [0, 1], d_conv, D, S,
                           _pick(d_conv, 512, 128), tn_co, S, [], [(BF16, nb_co), (BF16, nb_co)], _plain)
    branch_names = ["w_conv_out", "w_hgrn_out", "w_o"]
    branch_grads = [dw_co, dw_ho, dw_o.reshape(N_CHIPS, -1, D)]
    dab, dac, dax, dconv_w, *from_sibling = _conv_bwd(dya, proj, conv_w, d_conv, job=_SwapJob(branch_grads))
    parts_b, slots_b = pair_sums(branch_names, branch_grads, from_sibling)
    dq, dz, dv, dgo, dlb, dgn, *arrived_f = _hgrn_bwd(dob, o_raw, states, qe, ke, dec, proj, lb, gnorm_g, d_conv,
                                                      job=_ScatterJob(parts_f, slots_f))
    dproj = jnp.concatenate([dab, dac, dax, dq, dz, dv, dgo, dga, dgb], axis=1)
    halves_f = [_sum_chips("sum_chips_" + k, r, c_idx) for k, r in zip(ffn_names, arrived_f)]
    dw_in, *moved = _matmul("dw_in", [(h, 't', dproj, 'n')], [0], D, d_in, S, tw, tn_in, S, [], [(BF16, nb_in)], _plain,
                            job=_Jobs([_ShareJob(halves_f), _ScatterJob(parts_b, slots_b)]))
    whole_f, arrived_b = moved[:len(ffn_names)], moved[len(ffn_names):]
    parts_i, slots_i = pair_sums(["w_in"], [dw_in], _swap_halves("swap_in", [dw_in]))
    dh, arrived_in = _matmul("d_h", [(dproj, 'n', w_in, 't')], [0], S, D, d_in, tm2, D, tn_in, [], [(F32, None)], _plain,
                             job=_ScatterJob(parts_i, slots_i))
    dx, dsh_m, dsc_m, dg_mix = _norm_mod_bwd("norm_mix_bwd", dh, x, norm_mix_g, sc_m, dx1)
    dmod = jnp.concatenate([dsh_m, dsc_m, dgt_m, dsh_f, dsc_f, dgt_f], axis=1)
    small = dict(dmod=dmod, dg_mix=dg_mix, dg_ffn=dg_ffn, dg_final=dg_final, dlb=dlb,
                 dgn=jnp.sum(dgn, axis=0), dconv_w=dconv_w)
    late_names = branch_names + ["w_in"]
    late = _share_halves([_sum_chips("sum_chips_" + k, r, c_idx) for k, r in zip(late_names, arrived_b + [arrived_in])])
    big = dict(zip(ffn_names + late_names, whole_f + list(late)))
    return loss, dx, small, big


def _adamw_math(w, g, m, v):
    m = ADAM_B1 * m + (1.0 - ADAM_B1) * g
    v = ADAM_B2 * v + (1.0 - ADAM_B2) * (g * g)
    m_hat = m / (1.0 - ADAM_B1 ** ADAM_STEP)
    v_hat = v / (1.0 - ADAM_B2 ** ADAM_STEP)
    delta = -ADAM_LR * (m_hat / (jnp.sqrt(v_hat) + ADAM_EPS) + ADAM_WD * w)
    return delta, m, v


def _adamw(name, w, g, m, v):
    R, C = w.shape
    tr = _pick(R, max(8, (256 * 1024) // C // 8 * 8), 8)
    spec = pl.BlockSpec((tr, C), lambda i: (i, 0))

    def body(w_ref, g_ref, m_ref, v_ref, go_ref, d_ref, mo_ref, vo_ref):
        gv = g_ref[...]
        d, m2, v2 = _adamw_math(w_ref[...], gv, m_ref[...], v_ref[...])
        go_ref[...] = gv
        d_ref[...] = d
        mo_ref[...] = m2
        vo_ref[...] = v2

    return pl.pallas_call(
        body, name=name, grid=(R // tr,), in_specs=[spec] * 4, out_specs=[spec] * 4,
        out_shape=[jax.ShapeDtypeStruct((R, C), F32)] * 4, compiler_params=_params(("parallel",)),
    )(w, g, m, v)


def _ada_update(c_act_t, dmod, w, m, v):
    R, C = w.shape
    B = dmod.shape[0]
    tr = _pick(R, 256, 8)
    tc = _pick(C, 1536, 128)
    spec = pl.BlockSpec((tr, tc), lambda i, j: (i, j))

    def body(ct_ref, dm_ref, w_ref, m_ref, v_ref, g_ref, d_ref, mo_ref, vo_ref):
        ct = ct_ref[...]
        dm = dm_ref[...]
        g = ct[:, 0:1] * dm[0:1, :]
        for b in range(1, B):
            g = g + ct[:, b:b + 1] * dm[b:b + 1, :]
        d, m2, v2 = _adamw_math(w_ref[...], g, m_ref[...], v_ref[...])
        g_ref[...] = g
        d_ref[...] = d
        mo_ref[...] = m2
        vo_ref[...] = v2

    return pl.pallas_call(
        body, name="ada_update", grid=(R // tr, C // tc),
        in_specs=[pl.BlockSpec((tr, B), lambda i, j: (i, 0)), pl.BlockSpec((B, tc), lambda i, j: (0, j)),
                  spec, spec, spec],
        out_specs=[spec] * 4, out_shape=[jax.ShapeDtypeStruct((R, C), F32)] * 4,
        compiler_params=_params(("parallel", "parallel")),
    )(c_act_t, dmod, w, m, v)


def _prep(c_all, lb_param):
    def body(c_ref, p_ref, ca_ref, cab_ref, lb_ref):
        cv = c_ref[...]
        ca = cv * _sig(cv)
        ca_ref[...] = ca
        cab_ref[...] = ca.astype(BF16)
        lb_ref[...] = _sig(p_ref[0:1, :] - p_ref[1:2, :])

    return pl.pallas_call(
        body, name="prep",
        out_shape=[jax.ShapeDtypeStruct(c_all.shape, F32), jax.ShapeDtypeStruct(c_all.shape, BF16),
                   jax.ShapeDtypeStruct((1, lb_param.shape[1]), F32)],
    )(c_all, lb_param)


def _small_reduce(parts):
    W = parts.shape[1]

    def body(p_ref, o_ref):
        acc = p_ref[0:1, :]
        for d in range(1, N_DEV):
            acc = acc + p_ref[d:d + 1, :]
        o_ref[...] = acc

    return pl.pallas_call(body, name="small_reduce", out_shape=jax.ShapeDtypeStruct((1, W), F32))(parts)


def _lb_grad(dlb, lb):
    def body(d_ref, lb_ref, o_ref):
        t = d_ref[...] * lb_ref[...] * (1.0 - lb_ref[...])
        o_ref[0:1, :] = t
        o_ref[1:2, :] = -t

    return pl.pallas_call(body, name="lb_grad", out_shape=jax.ShapeDtypeStruct((2, dlb.shape[1]), F32))(dlb, lb)


def _place():
    x, y, c = lax.axis_index("x"), lax.axis_index("y"), lax.axis_index("c")
    chips = [(1 - x, y), (x, 1 - y), (1 - x, 1 - y)]
    return x, y, c, chips


def _allgather_rows(name, v):
    m_per, n = v.shape

    def body(x_ref, out_ref, send_sems, recv_sems, local_sem):
        x, y, c, chips = _place()
        me, sibling = (x, y, c), (x, y, 1 - c)

        def rows(px, py, pc):
            return out_ref.at[pl.ds((4 * px + 2 * py + pc) * m_per, m_per), :]

        def copy(k, block, to, src=None):
            return pltpu.make_async_remote_copy(
                src_ref=rows(*block) if src is None else src, dst_ref=rows(*block),
                send_sem=send_sems.at[k], recv_sem=recv_sems.at[k], device_id=to, device_id_type=MESH)

        mine = pltpu.make_async_copy(x_ref, rows(*me), local_sem)
        mine.start()
        first = [copy(0, me, sibling, src=x_ref)]
        first += [copy(1 + j, me, (*chip, c), src=x_ref) for j, chip in enumerate(chips)]
        for cp in first:
            cp.start()
        passed = [copy(4 + j, (*chip, c), sibling) for j, chip in enumerate(chips)]
        for j, chip in enumerate(chips):
            copy(1 + j, (*chip, c), me).wait_recv()
            passed[j].start()
        copy(0, sibling, me).wait_recv()
        for j, chip in enumerate(chips):
            copy(4 + j, (*chip, 1 - c), me).wait_recv()
        for cp in first + passed:
            cp.wait_send()
        mine.wait()

    return pl.pallas_call(
        body, name=name, out_shape=jax.ShapeDtypeStruct((N_DEV * m_per, n), v.dtype),
        in_specs=[pl.BlockSpec(memory_space=pltpu.VMEM)], out_specs=pl.BlockSpec(memory_space=pltpu.VMEM),
        scratch_shapes=[pltpu.SemaphoreType.DMA((7,)), pltpu.SemaphoreType.DMA((7,)), pltpu.SemaphoreType.DMA],
    )(v)


def _cast_place(name, w, k_idx):
    R, C = w.shape
    tr = _pick(R, max(16, (512 * 1024) // C // 16 * 16), 16)

    def body(k_ref, w_ref, o_ref):
        o_ref[...] = w_ref[...].astype(BF16)

    return pl.pallas_call(
        body, name=name,
        grid_spec=pltpu.PrefetchScalarGridSpec(
            num_scalar_prefetch=1, grid=(R // tr,),
            in_specs=[pl.BlockSpec((tr, C), lambda i, k_ref: (i, 0))],
            out_specs=pl.BlockSpec((None, tr, C), lambda i, k_ref: (k_ref[0], i, 0))),
        out_shape=jax.ShapeDtypeStruct((N_CHIPS, R, C), BF16),
        compiler_params=_params(("parallel",)),
    )(k_idx, w)


def _chip_copies(src_of, dst_of, got_of, n, sems, receiving):
    x, y, c, chips = _place()
    k_me = 2 * x + y
    send_sems, recv_sems = sems
    out = []
    for a in range(n):
        for j, chip in enumerate(chips):
            k_chip = 2 * chip[0] + chip[1]
            if receiving:
                src = dst = got_of(a, k_chip, c)
                to = (x, y, c)
            else:
                src, dst, to = src_of(a, k_chip, k_me, c), dst_of(a, k_me, c), (*chip, c)
            out.append(pltpu.make_async_remote_copy(
                src_ref=src, dst_ref=dst, send_sem=send_sems.at[3 * a + j], recv_sem=recv_sems.at[3 * a + j],
                device_id=to, device_id_type=MESH))
    return out


class _ChipJob:
    def start(self, j_in, j_out, sems):
        for send in self.copies(j_in, j_out, sems, False):
            send.start()

    def finish(self, j_in, j_out, sems):
        for recv in self.copies(j_in, j_out, sems, True):
            recv.wait_recv()
        for send in self.copies(j_in, j_out, sems, False):
            send.wait_send()


class _GatherJob(_ChipJob):
    def __init__(self, bufs, lo=0, hi=8):
        n = len(bufs)
        self.inputs, self.n_alias = list(bufs), n
        self.sem_shapes = [pltpu.SemaphoreType.DMA((3 * n,)), pltpu.SemaphoreType.DMA((3 * n,))]
        self.half = [b.shape[1] // 2 for b in bufs]
        self.lo, self.hi = lo, hi

    def copies(self, j_in, j_out, sems, receiving):
        def half(a, k, c):
            eighth = self.half[a] // 8
            return j_out[a].at[k, pl.ds(c * self.half[a] + self.lo * eighth, (self.hi - self.lo) * eighth)]

        return _chip_copies(lambda a, k_chip, k_me, c: half(a, k_me, c), half, half, len(self.half), sems, receiving)


class _ScatterJob(_ChipJob):
    def __init__(self, parts, slots):
        n = len(parts)
        self.n = n
        self.inputs, self.n_alias = list(parts) + list(slots), n
        self.sem_shapes = [pltpu.SemaphoreType.DMA((3 * n,)), pltpu.SemaphoreType.DMA((3 * n,))]

    def copies(self, j_in, j_out, sems, receiving):
        return _chip_copies(lambda a, k_chip, k_me, c: j_in[a].at[k_chip], lambda a, k_me, c: j_out[a].at[k_me],
                            lambda a, k_chip, c: j_out[a].at[k_chip], self.n, sems, receiving)


class _SwapJob(_ChipJob):
    def __init__(self, grads):
        n = len(grads)
        self.n = n
        self.half = [g.shape[1] // 2 for g in grads]
        landing = [lax.empty((N_CHIPS, g.shape[1] // 2, g.shape[2]), g.dtype) for g in grads]
        self.inputs, self.n_alias = list(grads) + landing, n
        self.sem_shapes = [pltpu.SemaphoreType.DMA((n,)), pltpu.SemaphoreType.DMA((n,))]

    def copies(self, j_in, j_out, sems, receiving):
        x, y, c, _ = _place()
        out = []
        for a in range(self.n):
            if receiving:
                src, to = j_out[a], (x, y, c)
            else:
                src, to = j_in[a].at[:, pl.ds((1 - c) * self.half[a], self.half[a])], (x, y, 1 - c)
            out.append(pltpu.make_async_remote_copy(src_ref=src, dst_ref=j_out[a], send_sem=sems[0].at[a],
                                                    recv_sem=sems[1].at[a], device_id=to, device_id_type=MESH))
        return out


class _Jobs:
    def __init__(self, jobs):
        self.jobs = jobs
        split = [(j.inputs[:len(j.inputs) - j.n_alias], j.inputs[len(j.inputs) - j.n_alias:]) for j in jobs]
        self.inputs = [a for plain, _ in split for a in plain] + [a for _, aliased in split for a in aliased]
        self.n_alias = sum(j.n_alias for j in jobs)
        self.sem_shapes = [s for j in jobs for s in j.sem_shapes]

    def _each(self, j_in, j_out, sems):
        n_plain = len(j_in) - self.n_alias
        p0 = a0 = s0 = 0
        for j in self.jobs:
            n_p, n_s = len(j.inputs) - j.n_alias, len(j.sem_shapes)
            ins = list(j_in[p0:p0 + n_p]) + list(j_in[n_plain + a0:n_plain + a0 + j.n_alias])
            yield j, ins, j_out[a0:a0 + j.n_alias], sems[s0:s0 + n_s]
            p0, a0, s0 = p0 + n_p, a0 + j.n_alias, s0 + n_s

    def start(self, j_in, j_out, sems):
        for j, ins, outs, s in self._each(j_in, j_out, sems):
            j.start(ins, outs, s)

    def finish(self, j_in, j_out, sems):
        for j, ins, outs, s in self._each(j_in, j_out, sems):
            j.finish(ins, outs, s)


class _ShareJob(_ChipJob):
    def __init__(self, bufs):
        n = len(bufs)
        self.n = n
        self.inputs, self.n_alias = list(bufs), n
        self.sem_shapes = [pltpu.SemaphoreType.DMA((n,)), pltpu.SemaphoreType.DMA((n,))]

    def copies(self, j_in, j_out, sems, receiving):
        x, y, c, _ = _place()
        out = []
        for a in range(self.n):
            hc, to = (1 - c, (x, y, c)) if receiving else (c, (x, y, 1 - c))
            out.append(pltpu.make_async_remote_copy(
                src_ref=j_out[a].at[hc], dst_ref=j_out[a].at[hc], send_sem=sems[0].at[a], recv_sem=sems[1].at[a],
                device_id=to, device_id_type=MESH))
        return out


class _ForwardJob(_ChipJob):
    def __init__(self, bufs):
        n = len(bufs)
        self.n = n
        self.half = [b.shape[1] // 2 for b in bufs]
        self.inputs, self.n_alias = list(bufs), n
        self.sem_shapes = [pltpu.SemaphoreType.DMA((3 * n,)), pltpu.SemaphoreType.DMA((3 * n,))]

    def copies(self, j_in, j_out, sems, receiving):
        x, y, c, chips = _place()
        out = []
        for a in range(self.n):
            for q, chip in enumerate(chips):
                hc, to = (1 - c, (x, y, c)) if receiving else (c, (x, y, 1 - c))
                part = j_out[a].at[2 * chip[0] + chip[1], pl.ds(hc * self.half[a], self.half[a])]
                out.append(pltpu.make_async_remote_copy(
                    src_ref=part, dst_ref=part, send_sem=sems[0].at[3 * a + q], recv_sem=sems[1].at[3 * a + q],
                    device_id=to, device_id_type=MESH))
        return out


def _forward_halves(name, bufs):
    n = len(bufs)

    def body(*refs):
        out_refs = refs[n:2 * n]
        send_sems, recv_sems = refs[2 * n:]
        x, y, c, chips = _place()
        started, waits = [], []
        for a in range(n):
            rh = bufs[a].shape[1] // 2
            for j, chip in enumerate(chips):
                k_chip = 2 * chip[0] + chip[1]
                got = out_refs[a].at[k_chip, pl.ds(c * rh, rh)]
                cp = pltpu.make_async_remote_copy(src_ref=got, dst_ref=got, send_sem=send_sems.at[3 * a + j],
                                                  recv_sem=recv_sems.at[3 * a + j], device_id=(x, y, 1 - c),
                                                  device_id_type=MESH)
                cp.start()
                started.append(cp)
                other = out_refs[a].at[k_chip, pl.ds((1 - c) * rh, rh)]
                waits.append(pltpu.make_async_remote_copy(
                    src_ref=other, dst_ref=other, send_sem=send_sems.at[3 * a + j], recv_sem=recv_sems.at[3 * a + j],
                    device_id=(x, y, c), device_id_type=MESH))
        for cp in waits:
            cp.wait_recv()
        for cp in started:
            cp.wait_send()

    return pl.pallas_call(
        body, name=name, out_shape=[jax.ShapeDtypeStruct(b.shape, b.dtype) for b in bufs],
        in_specs=[_HBM] * n, out_specs=[_HBM] * n, input_output_aliases={a: a for a in range(n)},
        scratch_shapes=[pltpu.SemaphoreType.DMA((3 * n,)), pltpu.SemaphoreType.DMA((3 * n,))],
    )(*bufs)


def _proj_gather(h, buf, order):
    S, D = h.shape
    nb = buf.shape[2]
    tm = _pick(S, 1024, 16)
    tn = _pick(nb, 1408, 128)
    per = nb // tn
    nj, ni = N_CHIPS * per, S // tm
    rh = D // 2
    seq = [(0, t) for t in range(per)] + [(p, t) for t in range(per) for p in (1, 2)] + [(3, t) for t in range(per)]
    tile_chip = jnp.stack([order[p] for p, _ in seq])
    tile_of = jnp.array([t for _, t in seq], jnp.int32)

    def body(chip_ref, t_ref, h_ref, buf_in, proj_ref, buf_ref, wbuf, tile_sems, ici_send, ici_recv, d2d_send, d2d_recv):
        j, i = pl.program_id(0), pl.program_id(1)
        x, y, c, chips = _place()
        k_me = 2 * x + y

        def part(k, hc, t):
            return buf_ref.at[k, pl.ds(hc * rh, rh), pl.ds(t * tn, tn)]

        def ici(q, t, receiving):
            k_chip = 2 * chips[q][0] + chips[q][1]
            src, to = (part(k_chip, c, t), (x, y, c)) if receiving else (part(k_me, c, t), (*chips[q], c))
            return pltpu.make_async_remote_copy(src_ref=src, dst_ref=src, send_sem=ici_send.at[q * per + t],
                                                recv_sem=ici_recv.at[q * per + t], device_id=to, device_id_type=MESH)

        def d2d(q, t, receiving):
            k_chip = 2 * chips[q][0] + chips[q][1]
            src, to = (part(k_chip, 1 - c, t), (x, y, c)) if receiving else (part(k_chip, c, t), (x, y, 1 - c))
            return pltpu.make_async_remote_copy(src_ref=src, dst_ref=src, send_sem=d2d_send.at[q * per + t],
                                                recv_sem=d2d_recv.at[q * per + t], device_id=to, device_id_type=MESH)

        def tile_copy(n):
            col = pl.multiple_of(t_ref[n] * tn, 128)
            return pltpu.make_async_copy(buf_ref.at[chip_ref[n], :, pl.ds(col, tn)], wbuf.at[n % 2], tile_sems.at[n % 2])

        @pl.when(jnp.logical_and(j == 0, i == 0))
        def _():
            for t in range(per):
                ici(0, t, False).start()
                ici(1, t, False).start()
            tile_copy(0).start()

        @pl.when(i == 0)
        def _():
            tile_copy(j).wait()
            for n in range(per, nj):
                p, t = seq[n]

                @pl.when(j + 1 == n)
                def _():
                    ici(p - 1, t, True).wait_recv()
                    d2d(p - 1, t, False).start()
                    if (p, t) == (1, per - 1):
                        for q in range(2):
                            for tt in range(per):
                                ici(q, tt, False).wait_send()
                        for tt in range(per):
                            ici(2, tt, False).start()
                    d2d(p - 1, t, True).wait_recv()

            @pl.when(j + 1 < nj)
            def _():
                tile_copy(j + 1).start()

        proj_ref[...] = jnp.dot(h_ref[...], wbuf[j % 2], preferred_element_type=F32)

        @pl.when(jnp.logical_and(j == nj - 1, i == ni - 1))
        def _():
            for t in range(per):
                ici(2, t, False).wait_send()
                for q in range(3):
                    d2d(q, t, False).wait_send()

    n_sems = 3 * per
    return pl.pallas_call(
        body, name="proj",
        grid_spec=pltpu.PrefetchScalarGridSpec(
            num_scalar_prefetch=2, grid=(nj, ni),
            in_specs=[pl.BlockSpec((tm, D), lambda j, i, kc, kt: (i, 0)), _HBM],
            out_specs=[pl.BlockSpec((tm, tn), lambda j, i, kc, kt: (i, kc[j] * per + kt[j])), _HBM],
            scratch_shapes=[pltpu.VMEM((2, D, tn), BF16), pltpu.SemaphoreType.DMA((2,))]
            + [pltpu.SemaphoreType.DMA((n_sems,))] * 4),
        out_shape=[jax.ShapeDtypeStruct((S, N_CHIPS * nb), F32), jax.ShapeDtypeStruct(buf.shape, buf.dtype)],
        input_output_aliases={3: 1}, compiler_params=_params(("arbitrary", "arbitrary")),
    )(tile_chip, tile_of, h, buf)


def _swap_halves(name, grads):
    n = len(grads)

    def body(*refs):
        in_refs, out_refs = refs[:n], refs[n:2 * n]
        send_sems, recv_sems = refs[2 * n:]
        x, y, c, _ = _place()
        cps = []
        for a in range(n):
            rh = grads[a].shape[1] // 2
            cp = pltpu.make_async_remote_copy(
                src_ref=in_refs[a].at[:, pl.ds((1 - c) * rh, rh)], dst_ref=out_refs[a],
                send_sem=send_sems.at[a], recv_sem=recv_sems.at[a], device_id=(x, y, 1 - c), device_id_type=MESH)
            cp.start()
            cps.append(cp)
        for cp in cps:
            cp.wait()

    return pl.pallas_call(
        body, name=name,
        out_shape=[jax.ShapeDtypeStruct((N_CHIPS, g.shape[1] // 2, g.shape[2]), g.dtype) for g in grads],
        in_specs=[_HBM] * n, out_specs=[_HBM] * n,
        scratch_shapes=[pltpu.SemaphoreType.DMA((n,)), pltpu.SemaphoreType.DMA((n,))],
    )(*grads)


def _pair_sum(name, g, q, ck_idx):
    _, R, C = g.shape
    rh = R // 2
    tr = _pick(rh, max(16, (512 * 1024) // C // 16 * 16), 16)
    nh = rh // tr

    def body(ck_ref, g_ref, q_ref, o_ref, own_ref):
        s = (g_ref[...].astype(F32) + q_ref[...].astype(F32)).astype(BF16)
        o_ref[...] = s

        @pl.when(pl.program_id(1) == ck_ref[1])
        def _():
            own_ref[...] = s

    return pl.pallas_call(
        body, name=name,
        grid_spec=pltpu.PrefetchScalarGridSpec(
            num_scalar_prefetch=1, grid=(nh, N_CHIPS),
            in_specs=[pl.BlockSpec((None, tr, C), lambda i, k, ck: (k, ck[0] * nh + i, 0)),
                      pl.BlockSpec((None, tr, C), lambda i, k, ck: (k, i, 0))],
            out_specs=[pl.BlockSpec((None, tr, C), lambda i, k, ck: (k, i, 0)),
                       pl.BlockSpec((None, tr, C), lambda i, k, ck: (ck[1], i, 0))]),
        out_shape=[jax.ShapeDtypeStruct((N_CHIPS, rh, C), BF16)] * 2,
        compiler_params=_params(("parallel", "arbitrary")),
    )(ck_idx, g, q)


def _sum_chips(name, r, c_idx):
    _, rh, C = r.shape
    tr = _pick(rh, max(16, (256 * 1024) // C // 16 * 16), 16)

    def body(c_ref, r_ref, o_ref):
        acc = r_ref[0].astype(F32)
        for k in range(1, N_CHIPS):
            acc = acc + r_ref[k].astype(F32)
        o_ref[...] = acc

    return pl.pallas_call(
        body, name=name,
        grid_spec=pltpu.PrefetchScalarGridSpec(
            num_scalar_prefetch=1, grid=(rh // tr,),
            in_specs=[pl.BlockSpec((N_CHIPS, tr, C), lambda i, c_ref: (0, i, 0))],
            out_specs=pl.BlockSpec((None, tr, C), lambda i, c_ref: (c_ref[0], i, 0))),
        out_shape=jax.ShapeDtypeStruct((2, rh, C), F32), compiler_params=_params(("parallel",)),
    )(c_idx, r)


def _share_halves(bufs):
    n = len(bufs)

    def body(*refs):
        out_refs = refs[n:2 * n]
        send_sems, recv_sems = refs[2 * n:]
        x, y, c, _ = _place()
        cps = []
        for a in range(n):
            cp = pltpu.make_async_remote_copy(
                src_ref=out_refs[a].at[c], dst_ref=out_refs[a].at[c], send_sem=send_sems.at[a],
                recv_sem=recv_sems.at[a], device_id=(x, y, 1 - c), device_id_type=MESH)
            cp.start()
            cps.append(cp)
        for a, cp in enumerate(cps):
            got = out_refs[a].at[1 - c]
            pltpu.make_async_remote_copy(src_ref=got, dst_ref=got, send_sem=send_sems.at[a], recv_sem=recv_sems.at[a],
                                         device_id=(x, y, c), device_id_type=MESH).wait_recv()
        for cp in cps:
            cp.wait_send()

    return pl.pallas_call(
        body, name="share_halves",
        out_shape=[jax.ShapeDtypeStruct(b.shape, b.dtype) for b in bufs],
        in_specs=[_HBM] * n, out_specs=[_HBM] * n, input_output_aliases={a: a for a in range(n)},
        scratch_shapes=[pltpu.SemaphoreType.DMA((n,)), pltpu.SemaphoreType.DMA((n,))],
    )(*bufs)


_BIG = ("w_in", "w_conv_out", "w_hgrn_out", "w_o", "w_ffn_gate", "w_ffn_up", "w_ffn_down")
_WEIGHTS = ("w_ada", "b_ada", "norm_mix_g", "w_in", "conv_w", "lb_param", "gnorm_g", "w_conv_out", "w_hgrn_out",
            "w_o", "norm_ffn_g", "w_ffn_gate", "w_ffn_up", "w_ffn_down", "norm_final_g")


def _pack_rows(pieces):
    flat = jnp.concatenate([p.reshape(-1) for p in pieces])
    pad = (-flat.shape[0]) % (SUBLANES * LANES)
    return jnp.pad(flat, (0, pad)).reshape(SUBLANES, -1)


def kernel(x, c, w_ada, b_ada, norm_mix_g, w_in, conv_w, lb_param, gnorm_g, w_conv_out, w_hgrn_out, w_o, norm_ffn_g, w_ffn_gate, w_ffn_up, w_ffn_down, norm_final_g, loss_target, m_w_ada, m_b_ada, m_norm_mix_g, m_w_in, m_conv_w, m_lb_param, m_gnorm_g, m_w_conv_out, m_w_hgrn_out, m_w_o, m_norm_ffn_g, m_w_ffn_gate, m_w_ffn_up, m_w_ffn_down, m_norm_final_g, v_w_ada, v_b_ada, v_norm_mix_g, v_w_in, v_conv_w, v_lb_param, v_gnorm_g, v_w_conv_out, v_w_hgrn_out, v_w_o, v_norm_ffn_g, v_w_ffn_gate, v_w_ffn_up, v_w_ffn_down, v_norm_final_g):
    w = dict(w_ada=w_ada, b_ada=b_ada, norm_mix_g=norm_mix_g, w_in=w_in, conv_w=conv_w, lb_param=lb_param,
             gnorm_g=gnorm_g, w_conv_out=w_conv_out, w_hgrn_out=w_hgrn_out, w_o=w_o, norm_ffn_g=norm_ffn_g,
             w_ffn_gate=w_ffn_gate, w_ffn_up=w_ffn_up, w_ffn_down=w_ffn_down, norm_final_g=norm_final_g)
    m = dict(w_ada=m_w_ada, b_ada=m_b_ada, norm_mix_g=m_norm_mix_g, w_in=m_w_in, conv_w=m_conv_w, lb_param=m_lb_param,
             gnorm_g=m_gnorm_g, w_conv_out=m_w_conv_out, w_hgrn_out=m_w_hgrn_out, w_o=m_w_o, norm_ffn_g=m_norm_ffn_g,
             w_ffn_gate=m_w_ffn_gate, w_ffn_up=m_w_ffn_up, w_ffn_down=m_w_ffn_down, norm_final_g=m_norm_final_g)
    v = dict(w_ada=v_w_ada, b_ada=v_b_ada, norm_mix_g=v_norm_mix_g, w_in=v_w_in, conv_w=v_conv_w, lb_param=v_lb_param,
             gnorm_g=v_gnorm_g, w_conv_out=v_w_conv_out, w_hgrn_out=v_w_hgrn_out, w_o=v_w_o, norm_ffn_g=v_norm_ffn_g,
             w_ffn_gate=v_w_ffn_gate, w_ffn_up=v_w_ffn_up, w_ffn_down=v_w_ffn_down, norm_final_g=v_norm_final_g)
    two_d = lambda a: a.reshape((-1, a.shape[-1])) if a.ndim != 2 else a
    shapes = {k: a.shape for k, a in w.items()}
    w, m, v = ({k: two_d(a) for k, a in t.items()} for t in (w, m, v))
    w["lb_param"], m["lb_param"], v["lb_param"] = lb_param, m_lb_param, v_lb_param
    S, D = x.shape[1], x.shape[2]
    d_conv = D // 2
    ix, iy, ic = lax.axis_index("x"), lax.axis_index("y"), lax.axis_index("c")
    k_me = 2 * ix + iy
    dev = 2 * k_me + ic
    cw_shard = w["conv_w"].shape[1]
    ada_cols = w["w_ada"].shape[1]

    got = _allgather_rows("gather_cond", _pack_rows([c, w["conv_w"]])).reshape(N_DEV, -1)
    c_all = got[:, :D]
    conv_full = got[::2, D:D + 3 * cw_shard].reshape(N_CHIPS, 3, cw_shard).transpose(1, 0, 2).reshape(3, -1)
    c_act, c_act_b, lb = _prep(c_all, lb_param)
    b_cols = lax.dynamic_slice(w["b_ada"], (0, k_me * ada_cols), (1, ada_cols))
    mod_cols, = _matmul("ada_fwd", [(c_act_b, 'n', w["w_ada"].astype(BF16), 'n')], [0], N_DEV, ada_cols, D,
                        N_DEV, _pick(ada_cols, 1536, 128), D, [(b_cols, 'row', 0)], [(F32, None)],
                        lambda accs, ex: [accs[0] + ex[0]])
    mod_all = _allgather_rows("gather_mod", mod_cols).reshape(N_CHIPS, 2, N_DEV, ada_cols)[:, 0]
    mod_all = mod_all.transpose(1, 0, 2).reshape(N_DEV, -1)
    mod = lax.dynamic_slice(mod_all, (dev, 0), (1, mod_all.shape[1]))
    k_idx = jnp.reshape(k_me, (1,)).astype(jnp.int32)
    c_idx = jnp.reshape(ic, (1,)).astype(jnp.int32)
    bufs = {k: _cast_place("cast_" + k, w[k], k_idx) for k in _BIG}
    order = jnp.stack([k_me, 2 * (1 - ix) + iy, 2 * ix + (1 - iy), 2 * (1 - ix) + (1 - iy)]).astype(jnp.int32)

    loss, dx, small, arrived = _local_step(
        x[0], loss_target[0], mod, w["norm_mix_g"], lb, w["gnorm_g"], w["norm_ffn_g"], w["norm_final_g"], conv_full,
        bufs, c_idx, order)

    pieces = [small["dmod"], small["dg_mix"], small["dg_ffn"], small["dg_final"], small["dlb"], small["dgn"],
              small["dconv_w"], loss]
    sizes = [p.size for p in pieces]
    parts = _allgather_rows("gather_small", _pack_rows(pieces)).reshape(N_DEV, -1)
    total = _small_reduce(parts)
    offs = [0]
    for s in sizes:
        offs.append(offs[-1] + s)
    tot = [total[:, offs[i]:offs[i + 1]] for i in range(len(sizes))]
    dmod_all = parts[:, :sizes[0]]
    grads = {
        "b_ada": tot[0], "norm_mix_g": tot[1], "norm_ffn_g": tot[2], "norm_final_g": tot[3],
        "lb_param": _lb_grad(tot[4], lb), "gnorm_g": tot[5],
        "conv_w": lax.dynamic_slice(tot[6].reshape(3, -1), (0, k_me * cw_shard), (3, cw_shard)),
    }
    loss_out = tot[7][0, 0]

    for k in _BIG:
        grads[k] = arrived[k].reshape(-1, arrived[k].shape[-1])

    delta, new_m, new_v = {}, {}, {}
    dmod_cols = lax.dynamic_slice(dmod_all, (0, k_me * ada_cols), (N_DEV, ada_cols))
    grads["w_ada"], delta["w_ada"], new_m["w_ada"], new_v["w_ada"] = _ada_update(
        c_act.T, dmod_cols, w["w_ada"], m["w_ada"], v["w_ada"])
    for k in _WEIGHTS:
        if k != "w_ada":
            grads[k], delta[k], new_m[k], new_v[k] = _adamw("adamw_" + k, w[k], grads[k], m[k], v[k])
    outs = [loss_out, dx.reshape(x.shape)]
    for t in (grads, delta, new_m, new_v):
        outs += [t[k].reshape(shapes[k]) for k in _WEIGHTS]
    return tuple(outs)
```

```python
import functools

import jax
import jax.numpy as jnp
from jax import lax
from jax.experimental import pallas as pl
from jax.experimental.pallas import tpu as pltpu

F32 = jnp.float32
BF16 = jnp.bfloat16
MESH = pl.DeviceIdType.MESH

EPS = 1e-6
HEAD = 128
BLK = 16
N_CHIPS = 4
N_DEV = 8
SUBLANES, LANES = 8, 128
VMEM_LIMIT_BYTES = 60 * 1024 * 1024

ADAM_LR = 0.001
ADAM_B1 = 0.9
ADAM_B2 = 0.999
ADAM_EPS = 1e-08
ADAM_WD = 0.01
ADAM_STEP = 10


def _pick(dim, pref, mult):
    if dim <= pref:
        return dim
    t = (pref // mult) * mult
    while t >= mult:
        if dim % t == 0:
            return t
        t -= mult
    return dim


def _sig(x):
    return 1.0 / (1.0 + jnp.exp(-x))


def _params(sem):
    return pltpu.CompilerParams(dimension_semantics=sem, vmem_limit_bytes=VMEM_LIMIT_BYTES)


def _gj(j, i, k):
    return j


def _gk(j, i, k):
    return k


def _wspec(arr, rt, ct, r_of, c_of):
    if arr.ndim == 2:
        return pl.BlockSpec((rt, ct), lambda j, i, k: (r_of(j, i, k), c_of(j, i, k)))
    per = arr.shape[2] // ct
    assert arr.shape[2] % ct == 0
    return pl.BlockSpec((None, rt, ct),
                        lambda j, i, k: (c_of(j, i, k) // per, r_of(j, i, k), c_of(j, i, k) % per))


_HBM = pl.BlockSpec(memory_space=pltpu.HBM)


def _host(job, grid, in_specs, args, out_specs, out_shape, scratch):
    if job is None:
        return {}, (lambda body: body)
    n_in, n_out, n_scr = len(args), len(out_shape), len(scratch)
    n_job, n_alias = len(job.inputs), job.n_alias
    in_specs += [_HBM] * n_job
    args += job.inputs
    out_specs += [_HBM] * n_alias
    out_shape += [jax.ShapeDtypeStruct(a.shape, a.dtype) for a in job.inputs[n_job - n_alias:]]
    scratch += job.sem_shapes
    aliases = {n_in + n_job - n_alias + t: n_out + t for t in range(n_alias)}

    def wrap(body):
        def hosted(*refs):
            ins, refs = refs[:n_in], refs[n_in:]
            j_in, refs = refs[:n_job], refs[n_job:]
            outs, refs = refs[:n_out], refs[n_out:]
            j_out, refs = refs[:n_alias], refs[n_alias:]
            scr, sems = refs[:n_scr], refs[n_scr:]
            first = functools.reduce(jnp.logical_and, [pl.program_id(d) == 0 for d in range(len(grid))])
            last = functools.reduce(jnp.logical_and, [pl.program_id(d) == grid[d] - 1 for d in range(len(grid))])

            @pl.when(first)
            def _():
                job.start(j_in, j_out, sems)

            body(*ins, *outs, *scr)

            @pl.when(last)
            def _():
                job.finish(j_in, j_out, sems)

        return hosted

    return aliases, wrap


EPILOGUE_COLS = 768


def _plain(accs, extras):
    return accs


def _matmul(name, pairs, acc_of, M, N, K, tm, tn, tk, extras, outs, epilogue, job=None, rows_outer=False):
    assert M % tm == 0 and N % tn == 0 and K % tk == 0, (name, M, N, K, tm, tn, tk)
    nj, ni, nk = N // tn, M // tm, K // tk
    n_pairs, n_extra, n_out = len(pairs), len(extras), len(outs)
    n_acc = max(acc_of) + 1
    in_specs, args, dims = [], [], []
    lhs_of, seen = [], {}
    for a, am, b, bm in pairs:
        if (id(a), am) not in seen:
            seen[id(a), am] = len(args)
            args.append(a)
            if am == 'n':
                in_specs.append(pl.BlockSpec((tm, tk), lambda j, i, k: (i, k)))
            else:
                in_specs.append(pl.BlockSpec((tk, tm), lambda j, i, k: (k, i)))
        lhs_of.append(seen[id(a), am])
    n_lhs = len(args)
    for a, am, b, bm in pairs:
        if bm == 'n':
            in_specs.append(_wspec(b, tk, tn, _gk, _gj))
        else:
            in_specs.append(_wspec(b, tn, tk, _gj, _gk))
        dims.append((((1 if am == 'n' else 0,), (0 if bm == 'n' else 1,)), ((), ())))
        args.append(b)
    for e, kind, off in extras:
        assert off % tn == 0, (name, off, tn)
        o = off // tn
        if kind == 'tile':
            in_specs.append(pl.BlockSpec((tm, tn), lambda j, i, k, o=o: (i, j + o)))
        else:
            in_specs.append(pl.BlockSpec((1, tn), lambda j, i, k, o=o: (0, j + o)))
        args.append(e)
    out_shape, out_specs = [], []
    for dt, nb in outs:
        if nb is None:
            out_shape.append(jax.ShapeDtypeStruct((M, N), dt))
            out_specs.append(pl.BlockSpec((tm, tn), lambda j, i, k: (i, j)))
        else:
            assert nb % tn == 0 and N % nb == 0
            per = nb // tn
            out_shape.append(jax.ShapeDtypeStruct((N // nb, M, nb), dt))
            out_specs.append(pl.BlockSpec((None, tm, tn), lambda j, i, k, per=per: (j // per, i, j % per)))

    def body(*refs):
        n_ab = n_lhs + n_pairs
        e_refs = refs[n_ab:n_ab + n_extra]
        o_refs = refs[n_ab + n_extra:n_ab + n_extra + n_out]
        acc_refs = refs[n_ab + n_extra + n_out:]

        def products(cols):
            sums = [None] * n_acc
            for p in range(n_pairs):
                b_ref = refs[n_lhs + p]
                b = b_ref[:, cols] if pairs[p][3] == 'n' else b_ref[cols, :]
                prod = lax.dot_general(refs[lhs_of[p]][...], b, dims[p], preferred_element_type=F32)
                a = acc_of[p]
                sums[a] = prod if sums[a] is None else sums[a] + prod
            return sums

        def finish(accs, cols):
            res = epilogue(accs, [e[:, cols] for e in e_refs])
            for o_ref, r in zip(o_refs, res):
                o_ref[:, cols] = r.astype(o_ref.dtype)

        whole = slice(0, tn)
        if nk == 1:
            step = tn if epilogue is _plain else EPILOGUE_COLS
            for c0 in range(0, tn, step):
                cols = slice(c0, min(c0 + step, tn))
                finish(products(cols), cols)
            return
        sums = products(whole)
        k = pl.program_id(2)
        targets = o_refs if sum_in_outs else acc_refs

        @pl.when(k == 0)
        def _():
            for a in range(n_acc):
                targets[a][...] = sums[a]

        @pl.when(k > 0)
        def _():
            for a in range(n_acc):
                targets[a][...] += sums[a]

        if not sum_in_outs:
            @pl.when(k == nk - 1)
            def _():
                finish([acc_refs[a][...] for a in range(n_acc)], whole)

    sum_in_outs = epilogue is _plain and nk > 1 and n_out == n_acc and all(dt == F32 for dt, _ in outs)
    scratch = [pltpu.VMEM((tm, tn), F32) for _ in range(n_acc)] if nk > 1 and not sum_in_outs else []
    grid = (nj, ni, nk)
    if rows_outer:
        grid = (ni, nj, nk)
        swap = lambda spec: pl.BlockSpec(spec.block_shape, lambda i, j, k, f=spec.index_map: f(j, i, k))
        in_specs, out_specs = [swap(s) for s in in_specs], [swap(s) for s in out_specs]
    aliases, wrap = _host(job, grid, in_specs, args, out_specs, out_shape, scratch)
    sem = ("parallel", "parallel", "arbitrary") if job is None else ("arbitrary",) * 3
    return pl.pallas_call(
        wrap(body), name=name, grid=grid, in_specs=in_specs, out_specs=out_specs, out_shape=out_shape,
        scratch_shapes=scratch, input_output_aliases=aliases, compiler_params=_params(sem),
    )(*args)


def _row_spec(tr, d):
    return pl.BlockSpec((tr, d), lambda i: (i, 0))


def _vec_spec(d):
    return pl.BlockSpec((1, d), lambda i: (0, 0))


def _norm_mod(name, x, g, sc, sh, job=None):
    S, D = x.shape
    tr = _pick(S, 256, 8)

    def body(x_ref, g_ref, sc_ref, sh_ref, h_ref):
        xv = x_ref[...]
        r = lax.rsqrt(jnp.mean(xv * xv, axis=-1, keepdims=True) + EPS)
        h_ref[...] = ((xv * r) * g_ref[...] * (1.0 + sc_ref[...]) + sh_ref[...]).astype(BF16)

    grid = (S // tr,)
    in_specs = [_row_spec(tr, D), _vec_spec(D), _vec_spec(D), _vec_spec(D)]
    args = [x, g, sc, sh]
    out_specs, out_shape, scratch = [_row_spec(tr, D)], [jax.ShapeDtypeStruct((S, D), BF16)], []
    aliases, wrap = _host(job, grid, in_specs, args, out_specs, out_shape, scratch)
    res = pl.pallas_call(
        wrap(body), name=name, grid=grid, in_specs=in_specs, out_specs=out_specs, out_shape=out_shape,
        scratch_shapes=scratch, input_output_aliases=aliases,
        compiler_params=_params(("parallel" if job is None else "arbitrary",)),
    )(*args)
    return res[0] if job is None else res


def _loss_head(x2, target, g, ff, gt):
    S, D = x2.shape
    tr = _pick(S, 256, 8)

    def body(x_ref, t_ref, g_ref, ff_ref, gt_ref, dx_ref, dffb_ref, loss_ref, dgt_ref, dg_ref):
        i = pl.program_id(0)

        @pl.when(i == 0)
        def _():
            loss_ref[...] = jnp.zeros_like(loss_ref)
            dgt_ref[...] = jnp.zeros_like(dgt_ref)
            dg_ref[...] = jnp.zeros_like(dg_ref)

        xv = x_ref[...]
        gv = g_ref[...]
        r = lax.rsqrt(jnp.mean(xv * xv, axis=-1, keepdims=True) + EPS)
        xh = xv * r
        err = xh * gv - t_ref[...]
        loss_ref[...] += 0.5 * jnp.sum(jnp.mean(err * err, axis=-1, keepdims=True))
        dy = err * (1.0 / D)
        dg_ref[...] += jnp.sum(dy * xh, axis=0, keepdims=True)
        dxh = dy * gv
        dx = r * (dxh - xh * jnp.mean(dxh * xh, axis=-1, keepdims=True))
        dx_ref[...] = dx
        dffb_ref[...] = (dx * gt_ref[...]).astype(BF16)
        dgt_ref[...] += jnp.sum(dx * ff_ref[...], axis=0, keepdims=True)

    return pl.pallas_call(
        body, name="loss_head", grid=(S // tr,),
        in_specs=[_row_spec(tr, D), _row_spec(tr, D), _vec_spec(D), _row_spec(tr, D), _vec_spec(D)],
        out_specs=[_row_spec(tr, D), _row_spec(tr, D), pl.BlockSpec((1, 128), lambda i: (0, 0)),
                   _vec_spec(D), _vec_spec(D)],
        out_shape=[jax.ShapeDtypeStruct((S, D), F32), jax.ShapeDtypeStruct((S, D), BF16),
                   jax.ShapeDtypeStruct((1, 128), F32), jax.ShapeDtypeStruct((1, D), F32),
                   jax.ShapeDtypeStruct((1, D), F32)],
        compiler_params=_params(("arbitrary",)),
    )(x2, target, g, ff, gt)


def _norm_mod_bwd(name, dh, x, g, sc, dres, gated=None, job=None):
    S, D = x.shape
    tr = _pick(S, 256, 8)
    n = S // tr
    with_gate = gated is not None

    def body(*refs):
        if with_gate:
            dh_ref, x_ref, g_ref, sc_ref, dres_ref, mo_ref, gt_ref, dx_ref, dsh_ref, dsc_ref, dg_ref, dmob_ref, dgt_ref = refs
        else:
            dh_ref, x_ref, g_ref, sc_ref, dres_ref, dx_ref, dsh_ref, dsc_ref, dg_ref = refs
        i = pl.program_id(0)

        @pl.when(i == 0)
        def _():
            dsh_ref[...] = jnp.zeros_like(dsh_ref)
            dsc_ref[...] = jnp.zeros_like(dsc_ref)
            if with_gate:
                dgt_ref[...] = jnp.zeros_like(dgt_ref)

        xv = x_ref[...]
        dhv = dh_ref[...]
        gv = g_ref[...]
        scale = 1.0 + sc_ref[...]
        r = lax.rsqrt(jnp.mean(xv * xv, axis=-1, keepdims=True) + EPS)
        xh = xv * r
        dsh_ref[...] += jnp.sum(dhv, axis=0, keepdims=True)
        dsc_ref[...] += jnp.sum(dhv * xh, axis=0, keepdims=True)
        dxh = dhv * (gv * scale)
        dx = dres_ref[...] + r * (dxh - xh * jnp.mean(dxh * xh, axis=-1, keepdims=True))
        dx_ref[...] = dx
        if with_gate:
            dmob_ref[...] = (dx * gt_ref[...]).astype(BF16)
            dgt_ref[...] += jnp.sum(dx * mo_ref[...], axis=0, keepdims=True)

        @pl.when(i == n - 1)
        def _():
            t = dsc_ref[...]
            dg_ref[...] = t * scale
            dsc_ref[...] = t * gv

    in_specs = [_row_spec(tr, D), _row_spec(tr, D), _vec_spec(D), _vec_spec(D), _row_spec(tr, D)]
    args = [dh, x, g, sc, dres]
    out_specs = [_row_spec(tr, D), _vec_spec(D), _vec_spec(D), _vec_spec(D)]
    out_shape = [jax.ShapeDtypeStruct((S, D), F32)] + [jax.ShapeDtypeStruct((1, D), F32)] * 3
    if with_gate:
        in_specs += [_row_spec(tr, D), _vec_spec(D)]
        args += list(gated)
        out_specs += [_row_spec(tr, D), _vec_spec(D)]
        out_shape += [jax.ShapeDtypeStruct((S, D), BF16), jax.ShapeDtypeStruct((1, D), F32)]
    scratch = []
    aliases, wrap = _host(job, (n,), in_specs, args, out_specs, out_shape, scratch)
    return pl.pallas_call(
        wrap(body), name=name, grid=(n,), in_specs=in_specs, out_specs=out_specs, out_shape=out_shape,
        scratch_shapes=scratch, input_output_aliases=aliases, compiler_params=_params(("arbitrary",)),
    )(*args)


CONV_ROWS = 256


def _col_spec(S, off_cols):
    o = off_cols // HEAD
    return pl.BlockSpec((S, HEAD), lambda c, o=o: (0, c + o))


def _conv_taps(u, u_prev, rid):
    s1 = jnp.where(rid >= 1, pltpu.roll(u, 1, 0), pltpu.roll(u_prev, 1, 0))
    s2 = jnp.where(rid >= 2, pltpu.roll(u, 2, 0), pltpu.roll(u_prev, 2, 0))
    return s1, s2


def _conv_fwd(proj, conv_w, d_conv, job=None):
    S = proj.shape[0]
    R = min(CONV_ROWS, S)
    nr = S // R

    def body(ab_ref, ac_ref, ax_ref, w_ref, ya_ref):
        w0, w1, w2 = w_ref[0:1, :], w_ref[1:2, :], w_ref[2:3, :]
        rid = lax.broadcasted_iota(jnp.int32, (R, HEAD), 0)

        def step(c, carry):
            rows = pl.ds(pl.multiple_of(c * R, R), R)
            prev = pl.ds(pl.multiple_of(jnp.maximum(c - 1, 0) * R, R), R)
            u = ac_ref[rows, :] * ax_ref[rows, :]
            u_prev = jnp.where(c > 0, ac_ref[prev, :] * ax_ref[prev, :], 0.0)
            s1, s2 = _conv_taps(u, u_prev, rid)
            y = w0 * s2 + w1 * s1 + w2 * u
            ya_ref[rows, :] = (ab_ref[rows, :] * y).astype(BF16)
            return carry

        lax.fori_loop(0, nr, step, 0)

    grid = (d_conv // HEAD,)
    in_specs = [_col_spec(S, 0), _col_spec(S, d_conv), _col_spec(S, 2 * d_conv), pl.BlockSpec((3, HEAD), lambda c: (0, c))]
    args = [proj, proj, proj, conv_w]
    out_specs, out_shape = [pl.BlockSpec((S, HEAD), lambda c: (0, c))], [jax.ShapeDtypeStruct((S, d_conv), BF16)]
    scratch = []
    aliases, wrap = _host(job, grid, in_specs, args, out_specs, out_shape, scratch)
    res = pl.pallas_call(
        wrap(body), name="conv_fwd", grid=grid, in_specs=in_specs, out_specs=out_specs, out_shape=out_shape,
        scratch_shapes=scratch, input_output_aliases=aliases,
        compiler_params=_params(("parallel" if job is None else "arbitrary",)),
    )(*args)
    return res[0] if job is None else res


def _conv_bwd(dya, proj, conv_w, d_conv, job=None):
    S = proj.shape[0]
    R = min(CONV_ROWS, S)
    nr = S // R

    def body(dya_ref, ab_ref, ac_ref, ax_ref, w_ref, dab_ref, dac_ref, dax_ref, dw_ref):
        w0, w1, w2 = w_ref[0:1, :], w_ref[1:2, :], w_ref[2:3, :]
        rid = lax.broadcasted_iota(jnp.int32, (R, HEAD), 0)

        def step(c, carry):
            dw0, dw1, dw2 = carry
            rows = pl.ds(pl.multiple_of(c * R, R), R)
            prev = pl.ds(pl.multiple_of(jnp.maximum(c - 1, 0) * R, R), R)
            nxt = pl.ds(pl.multiple_of(jnp.minimum(c + 1, nr - 1) * R, R), R)
            a_c, a_x, a_b = ac_ref[rows, :], ax_ref[rows, :], ab_ref[rows, :]
            u = a_c * a_x
            u_prev = jnp.where(c > 0, ac_ref[prev, :] * ax_ref[prev, :], 0.0)
            s1, s2 = _conv_taps(u, u_prev, rid)
            y = w0 * s2 + w1 * s1 + w2 * u
            dy = dya_ref[rows, :]
            dab_ref[rows, :] = (dy * y).astype(BF16)
            dyc = dy * a_b
            dyc_next = jnp.where(c < nr - 1, dya_ref[nxt, :] * ab_ref[nxt, :], 0.0)
            t1 = jnp.where(rid < R - 1, pltpu.roll(dyc, R - 1, 0), pltpu.roll(dyc_next, R - 1, 0))
            t2 = jnp.where(rid < R - 2, pltpu.roll(dyc, R - 2, 0), pltpu.roll(dyc_next, R - 2, 0))
            du = w2 * dyc + w1 * t1 + w0 * t2
            dac_ref[rows, :] = (du * a_x).astype(BF16)
            dax_ref[rows, :] = (du * a_c).astype(BF16)
            dw0 = dw0 + jnp.sum(dyc * s2, axis=0, keepdims=True)
            dw1 = dw1 + jnp.sum(dyc * s1, axis=0, keepdims=True)
            dw2 = dw2 + jnp.sum(dyc * u, axis=0, keepdims=True)
            return dw0, dw1, dw2

        z = jnp.zeros((1, HEAD), F32)
        dw0, dw1, dw2 = lax.fori_loop(0, nr, step, (z, z, z))
        dw_ref[0:1, :] = dw0
        dw_ref[1:2, :] = dw1
        dw_ref[2:3, :] = dw2

    col = pl.BlockSpec((S, HEAD), lambda c: (0, c))
    grid = (d_conv // HEAD,)
    in_specs = [col, _col_spec(S, 0), _col_spec(S, d_conv), _col_spec(S, 2 * d_conv),
                pl.BlockSpec((3, HEAD), lambda c: (0, c))]
    args = [dya, proj, proj, proj, conv_w]
    out_specs = [col, col, col, pl.BlockSpec((3, HEAD), lambda c: (0, c))]
    out_shape = [jax.ShapeDtypeStruct((S, d_conv), BF16)] * 3 + [jax.ShapeDtypeStruct((3, d_conv), F32)]
    scratch = []
    aliases, wrap = _host(job, grid, in_specs, args, out_specs, out_shape, scratch)
    return pl.pallas_call(
        wrap(body), name="conv_bwd", grid=grid, in_specs=in_specs, out_specs=out_specs, out_shape=out_shape,
        scratch_shapes=scratch, input_output_aliases=aliases,
        compiler_params=_params(("parallel" if job is None else "arbitrary",)),
    )(*args)


HGRN_FWD_ROWS = 512
HGRN_BWD_ROWS = 1024
HGRN_FWD_SUB = 64
HGRN_BWD_SUB = 32
HGRN_GATE_ROWS = 128


def _block_cumsum(x, pos):
    for s in (1, 2, 4, 8):
        x = x + jnp.where(pos >= s, pltpu.roll(x, s, 0), 0.0)
    return x


def _block_rev_cumsum(x, pos, rows):
    for s in (1, 2, 4, 8):
        x = x + jnp.where(pos < BLK - s, pltpu.roll(x, rows - s, 0), 0.0)
    return x


def _block_last(x, pos, rows):
    m = jnp.where(pos == BLK - 1, x, 0.0)
    for s in (1, 2, 4, 8):
        m = m + pltpu.roll(m, rows - s, 0)
    return m


def _hgrn_gates(q, z, lb):
    sgq = _sig(q)
    qs = q * sgq
    sg = _sig(z)
    f = lb + (1.0 - lb) * sg
    kk = (1.0 - lb) * (1.0 - sg)
    return sgq, qs, sg, f, kk


def _hgrn_decays(q, z, lb, pos, rows):
    sgq, qs, sg, f, kk = _hgrn_gates(q, z, lb)
    b = _block_cumsum(jnp.log(f), pos)
    return sgq, qs, sg, f, kk, b, _block_last(b, pos, rows)


def _hgrn_specs(T, d_conv, n_t, rev):
    def t_of(i):
        return (n_t - 1 - i) if rev else i

    def pcol(off):
        o = off // HEAD
        return pl.BlockSpec((T, HEAD), lambda h, i, o=o: (t_of(i), h + o))

    q_spec, z_spec, v_spec, g_spec = pcol(3 * d_conv), pcol(4 * d_conv), pcol(5 * d_conv), pcol(6 * d_conv)
    lb_spec = pl.BlockSpec((1, HEAD), lambda h, i: (0, h))
    gn_spec = pl.BlockSpec((1, HEAD), lambda h, i: (0, 0))
    act_spec = pl.BlockSpec((T, HEAD), lambda h, i: (t_of(i), h))
    st_spec = pl.BlockSpec((None, T // BLK, HEAD, HEAD), lambda h, i: (h, t_of(i), 0, 0))
    return q_spec, z_spec, v_spec, g_spec, lb_spec, gn_spec, act_spec, st_spec


def _hgrn_fwd(proj, lb, gn, d_conv, job=None):
    S = proj.shape[0]
    H = d_conv // HEAD
    T = min(HGRN_FWD_ROWS, S)
    n_t, nb = S // T, T // BLK
    sub = min(HGRN_FWD_SUB, T)
    q_spec, z_spec, v_spec, g_spec, lb_spec, gn_spec, act_spec, st_spec = _hgrn_specs(T, d_conv, n_t, False)

    def body(q_ref, z_ref, v_ref, g_ref, lb_ref, gn_ref, ob_ref, o_ref, st_ref, qe_s, ke_s, dec_s, state, v_s, o_s, u_s):
        @pl.when(pl.program_id(1) == 0)
        def _():
            state[...] = jnp.zeros_like(state)

        pos = lax.broadcasted_iota(jnp.int32, (sub, HEAD), 0) % BLK
        lbv = lb_ref[...]

        def vector_pass(s, carry):
            r = pl.ds(pl.multiple_of(s * sub, sub), sub)
            v = v_ref[r, :]
            _, qs, _, _, kk, b, b_tot = _hgrn_decays(q_ref[r, :], z_ref[r, :], lbv, pos, sub)
            qe_s[r, :] = (qs * jnp.exp(b)).astype(BF16)
            ke_s[r, :] = (kk * jnp.exp(b_tot - b)).astype(BF16)
            v_s[r, :] = v.astype(BF16)
            dec_s[r, :] = jnp.exp(b_tot)
            o = jnp.sum(qs * kk, axis=-1, keepdims=True) * v
            for d in range(1, BLK):
                e = jnp.exp(b - pltpu.roll(b, d, 0))
                a = jnp.sum(qs * pltpu.roll(kk, d, 0) * e, axis=-1, keepdims=True)
                o = o + jnp.where(pos >= d, a * pltpu.roll(v, d, 0), 0.0)
            o_s[r, :] = o
            return carry

        lax.fori_loop(0, T // sub, vector_pass, 0)

        for j in range(nb):
            rows = pl.ds(j * BLK, BLK)
            u_s[j] = lax.dot_general(v_s[rows, :], ke_s[rows, :], (((0,), (0,)), ((), ())),
                                     preferred_element_type=F32)
        st = state[...]
        for j in range(nb):
            st_ref[j] = st.astype(BF16)
            st = st * dec_s[pl.ds(j * BLK, 1), :] + u_s[j]
        state[...] = st
        for j in range(nb):
            rows = pl.ds(j * BLK, BLK)
            o_s[rows, :] += lax.dot_general(qe_s[rows, :], st_ref[j], (((1,), (1,)), ((), ())),
                                            preferred_element_type=F32)
        o = o_s[...]
        o_ref[...] = o
        r = lax.rsqrt(jnp.mean(o * o, axis=-1, keepdims=True) + EPS)
        g = g_ref[...]
        ob_ref[...] = ((o * r) * gn_ref[...] * (g * _sig(g))).astype(BF16)

    grid = (H, n_t)
    in_specs = [q_spec, z_spec, v_spec, g_spec, lb_spec, gn_spec]
    args = [proj, proj, proj, proj, lb, gn]
    out_specs = [act_spec, act_spec, st_spec, act_spec, act_spec, act_spec]
    out_shape = [jax.ShapeDtypeStruct((S, d_conv), BF16), jax.ShapeDtypeStruct((S, d_conv), F32),
                 jax.ShapeDtypeStruct((H, S // BLK, HEAD, HEAD), BF16),
                 jax.ShapeDtypeStruct((S, d_conv), BF16), jax.ShapeDtypeStruct((S, d_conv), BF16),
                 jax.ShapeDtypeStruct((S, d_conv), F32)]
    scratch = [pltpu.VMEM((HEAD, HEAD), F32), pltpu.VMEM((T, HEAD), BF16), pltpu.VMEM((T, HEAD), F32),
               pltpu.VMEM((nb, HEAD, HEAD), F32)]
    aliases, wrap = _host(job, grid, in_specs, args, out_specs, out_shape, scratch)
    return pl.pallas_call(
        wrap(body), name="hgrn_fwd", grid=grid, in_specs=in_specs, out_specs=out_specs, out_shape=out_shape,
        scratch_shapes=scratch, input_output_aliases=aliases,
        compiler_params=_params(("parallel" if job is None else "arbitrary", "arbitrary")),
    )(*args)


def _hgrn_bwd(dob, o_raw, states, qe, ke, dec, proj, lb, gn, d_conv, job=None):
    S = proj.shape[0]
    H = d_conv // HEAD
    T = min(HGRN_BWD_ROWS, S)
    n_t, nb = S // T, T // BLK
    sub = min(HGRN_BWD_SUB, T)
    gate_rows = min(HGRN_GATE_ROWS, T)
    q_spec, z_spec, v_spec, g_spec, lb_spec, gn_spec, act_spec, st_spec = _hgrn_specs(T, d_conv, n_t, True)

    def body(dob_ref, o_ref, st_ref, qe_b, ke_b, dec_s, q_ref, z_ref, v_ref, g_ref, lb_ref, gn_ref,
             dq_ref, dz_ref, dv_ref, dg_ref, dlb_ref, dgn_ref,
             dstate, do_b, v_b, dqe_s, dke_s, dvi_s, ddec_s, do_s, u_s, ds_s):
        @pl.when(pl.program_id(1) == 0)
        def _():
            dstate[...] = jnp.zeros_like(dstate)
            dlb_ref[...] = jnp.zeros_like(dlb_ref)
            dgn_ref[...] = jnp.zeros_like(dgn_ref)

        pos = lax.broadcasted_iota(jnp.int32, (sub, HEAD), 0) % BLK
        lbv, gnv = lb_ref[...], gn_ref[...]

        def before(s, carry):
            r = pl.ds(pl.multiple_of(s * gate_rows, gate_rows), gate_rows)
            g = g_ref[r, :]
            v_b[r, :] = v_ref[r, :].astype(BF16)
            o = o_ref[r, :]
            rs = lax.rsqrt(jnp.mean(o * o, axis=-1, keepdims=True) + EPS)
            on = o * rs
            sgg = _sig(g)
            dobv = dob_ref[r, :]
            dog = dobv * (g * sgg)
            dgn_ref[...] += jnp.sum(dog * on, axis=0, keepdims=True)
            don = dog * gnv
            do = rs * (don - on * jnp.mean(don * on, axis=-1, keepdims=True))
            dg_ref[r, :] = (dobv * (on * gnv) * (sgg * (1.0 + g * (1.0 - sgg)))).astype(BF16)
            do_s[r, :] = do
            do_b[r, :] = do.astype(BF16)
            return carry

        lax.fori_loop(0, T // gate_rows, before, 0)

        for j in range(nb):
            rows = pl.ds(j * BLK, BLK)
            dob_j = do_b[rows, :]
            u_s[j] = lax.dot_general(dob_j, qe_b[rows, :], (((0,), (0,)), ((), ())), preferred_element_type=F32)
            dqe_s[rows, :] = lax.dot_general(dob_j, st_ref[j], (((1,), (0,)), ((), ())), preferred_element_type=F32)
        dst = dstate[...]
        for j in reversed(range(nb)):
            ds_s[j] = dst.astype(BF16)
            ddec = jnp.sum(st_ref[j].astype(F32) * dst, axis=0, keepdims=True)
            ddec_s[pl.ds(j * BLK, BLK), :] = jnp.broadcast_to(ddec, (BLK, HEAD))
            dst = dst * dec_s[pl.ds(j * BLK, 1), :] + u_s[j]
        dstate[...] = dst
        for j in range(nb):
            rows = pl.ds(j * BLK, BLK)
            dke_s[rows, :] = lax.dot_general(v_b[rows, :], ds_s[j], (((1,), (0,)), ((), ())), preferred_element_type=F32)
            dvi_s[rows, :] = lax.dot_general(ke_b[rows, :], ds_s[j], (((1,), (1,)), ((), ())), preferred_element_type=F32)

        def after(s, carry):
            r = pl.ds(pl.multiple_of(s * sub, sub), sub)
            q, v = q_ref[r, :], v_ref[r, :]
            sgq, qs, sg, f, kk, b, b_tot = _hgrn_decays(q, z_ref[r, :], lbv, pos, sub)
            do = do_s[r, :]
            dqs = dqe_s[r, :] * jnp.exp(b)
            dkk = dke_s[r, :] * jnp.exp(b_tot - b)
            d_tot = jnp.where(pos == BLK - 1, _block_cumsum(dkk * kk, pos) + ddec_s[r, :] * jnp.exp(b_tot), 0.0)
            dv = dvi_s[r, :]
            da = jnp.sum(do * v, axis=-1, keepdims=True)
            dv = dv + jnp.sum(qs * kk, axis=-1, keepdims=True) * do
            dqs = dqs + da * kk
            dkk = dkk + da * qs
            for d in range(1, BLK):
                kd = pltpu.roll(kk, d, 0)
                e = jnp.where(pos >= d, jnp.exp(b - pltpu.roll(b, d, 0)), 0.0)
                a = jnp.sum(qs * kd * e, axis=-1, keepdims=True)
                da = jnp.sum(do * pltpu.roll(v, d, 0), axis=-1, keepdims=True)
                gq = da * e
                dqs = dqs + gq * kd
                dkk = dkk + pltpu.roll(gq * qs, sub - d, 0)
                dv = dv + pltpu.roll(a * do, sub - d, 0)
            db = qs * dqs - kk * dkk + d_tot
            dlf = _block_rev_cumsum(db, pos, sub)
            df = dlf / f - dkk
            dlb_ref[...] += jnp.sum(df * (1.0 - sg), axis=0, keepdims=True)
            dz_ref[r, :] = (df * (1.0 - lbv) * sg * (1.0 - sg)).astype(BF16)
            dq_ref[r, :] = (dqs * (sgq * (1.0 + q * (1.0 - sgq)))).astype(BF16)
            dv_ref[r, :] = dv.astype(BF16)
            return carry

        lax.fori_loop(0, T // sub, after, 0)

    tb = lambda dt: pltpu.VMEM((T, HEAD), dt)
    grid = (H, n_t)
    in_specs = [act_spec, act_spec, st_spec, act_spec, act_spec, act_spec, q_spec, z_spec, v_spec, g_spec, lb_spec,
                gn_spec]
    args = [dob, o_raw, states, qe, ke, dec, proj, proj, proj, proj, lb, gn]
    out_specs = [act_spec, act_spec, act_spec, act_spec, lb_spec, pl.BlockSpec((None, 1, HEAD), lambda h, i: (h, 0, 0))]
    out_shape = ([jax.ShapeDtypeStruct((S, d_conv), BF16)] * 4
                 + [jax.ShapeDtypeStruct((1, d_conv), F32), jax.ShapeDtypeStruct((H, 1, HEAD), F32)])
    scratch = [pltpu.VMEM((HEAD, HEAD), F32), tb(BF16), tb(BF16), tb(F32), tb(F32), tb(F32), tb(F32), tb(F32),
               pltpu.VMEM((nb, HEAD, HEAD), F32), pltpu.VMEM((nb, HEAD, HEAD), BF16)]
    aliases, wrap = _host(job, grid, in_specs, args, out_specs, out_shape, scratch)
    return pl.pallas_call(
        wrap(body), name="hgrn_bwd", grid=grid, in_specs=in_specs, out_specs=out_specs, out_shape=out_shape,
        scratch_shapes=scratch, input_output_aliases=aliases,
        compiler_params=_params(("parallel" if job is None else "arbitrary", "arbitrary")),
    )(*args)


def _local_step(x, target, mod, norm_mix_g, lb, gnorm_g, norm_ffn_g, norm_final_g, conv_w, bufs, c_idx, order):
    S, D = x.shape
    d_conv = D // 2
    d_in = 7 * d_conv + 2 * D
    d_ff = N_CHIPS * bufs["w_ffn_down"].shape[1]
    nb_in, nb_co, nb_ff = bufs["w_in"].shape[2], bufs["w_conv_out"].shape[2], bufs["w_ffn_gate"].shape[2]
    rows = lambda g: g.reshape(-1, g.shape[2])
    sh_m, sc_m, gt_m, sh_f, sc_f, gt_f = [mod[:, i * D:(i + 1) * D] for i in range(6)]
    tm = _pick(S, 512, 16)
    tm2 = _pick(S, 1024, 16)
    tn_in = _pick(nb_in, 1408, 128)
    tn_co = _pick(nb_co, 512, 128)
    tn_ff = _pick(nb_ff, 1408, 128)
    tn_d = _pick(D, 512, 128)
    tw = _pick(D, 1024, 128)
    tg = _pick(D, 512, 128)

    h = _norm_mod("norm_mix", x, norm_mix_g, sc_m, sh_m)
    proj, w_in = _proj_gather(h, bufs["w_in"], order)
    ob, o_raw, states, qe, ke, dec, *got = _hgrn_fwd(
        proj, lb, gnorm_g, d_conv,
        job=_GatherJob([bufs[k] for k in ("w_conv_out", "w_hgrn_out", "w_o", "w_ffn_gate")]))
    ya, w_conv_out, w_hgrn_out, w_o, w_ffn_gate, w_ffn_up = _conv_fwd(
        proj, conv_w, d_conv, job=_Jobs([_ForwardJob(got), _GatherJob([bufs["w_ffn_up"]], 0, 2)]))
    w_o = rows(w_o)

    def merge_epi(accs, ex):
        y_a, y_b = accs
        return [y_a, y_b, _sig(ex[0]) * y_a + _sig(ex[1]) * y_b]

    y_a, y_b, merged, w_ffn_up = _matmul(
        "branch_out", [(ya, 'n', w_conv_out, 'n'), (ob, 'n', w_hgrn_out, 'n')], [0, 1], S, D, d_conv, tm2, tn_co, d_conv,
        [(proj, 'tile', 7 * d_conv), (proj, 'tile', 7 * d_conv + D)], [(BF16, None), (BF16, None), (BF16, None)], merge_epi,
        job=_GatherJob([w_ffn_up], 2, 5), rows_outer=True)

    def resid_epi(accs, ex):
        return [accs[0], ex[0] + ex[1] * accs[0]]

    mo, x1, w_ffn_up = _matmul("w_o", [(merged, 'n', w_o, 'n')], [0], S, D, D, tm2, tw, D,
                               [(x, 'tile', 0), (gt_m, 'row', 0)], [(BF16, None), (F32, None)], resid_epi,
                               job=_GatherJob([w_ffn_up], 5, 8))
    h2, w_ffn_up = _norm_mod("norm_ffn", x1, norm_ffn_g, sc_f, sh_f, job=_ForwardJob([w_ffn_up]))

    def swiglu_epi(accs, ex):
        gg, uu = accs
        return [gg, uu, gg * _sig(gg) * uu]

    G, U, act, w_ffn_down = _matmul(
        "ffn_in", [(h2, 'n', w_ffn_gate, 'n'), (h2, 'n', w_ffn_up, 'n')], [0, 1], S, d_ff, D,
        tm2, tn_ff, D, [], [(BF16, None), (BF16, None), (BF16, None)], swiglu_epi, job=_GatherJob([bufs["w_ffn_down"]]))
    w_ffn_down = rows(_forward_halves("forward_down", [w_ffn_down])[0])
    ff, x2 = _matmul("ffn_out", [(act, 'n', w_ffn_down, 'n')], [0], S, D, d_ff, tm2, tn_d, d_ff,
                     [(x1, 'tile', 0), (gt_f, 'row', 0)], [(BF16, None), (F32, None)], resid_epi, rows_outer=True)

    dx2, dffb, loss, dgt_f, dg_final = _loss_head(x2, target, norm_final_g, ff, gt_f)

    def swiglu_bwd_epi(accs, ex):
        dact, gg, uu = accs[0], ex[0], ex[1]
        s = _sig(gg)
        return [dact * uu * (s * (1.0 + gg * (1.0 - s))), dact * (gg * s)]

    dG, dU = _matmul("d_act", [(dffb, 'n', w_ffn_down, 't')], [0], S, d_ff, D, tm2, tn_ff, D,
                     [(G, 'tile', 0), (U, 'tile', 0)], [(BF16, None), (BF16, None)], swiglu_bwd_epi)
    dw_down, = _matmul("dw_ffn_down", [(act, 't', dffb, 'n')], [0], d_ff, D, S, _pick(d_ff, 512, 128),
                       D, S, [], [(BF16, None)], _plain)
    dh2, = _matmul("d_h2", [(dG, 'n', w_ffn_gate, 't'), (dU, 'n', w_ffn_up, 't')], [0, 0], S, D, d_ff,
                   tm2, D, tn_ff, [], [(F32, None)], _plain)
    dw_gate, = _matmul("dw_ffn_gate", [(h2, 't', dG, 'n')], [0], D, d_ff, S, tw, tn_ff, S, [], [(BF16, nb_ff)], _plain)
    dw_up, = _matmul("dw_ffn_up", [(h2, 't', dU, 'n')], [0], D, d_ff, S, tw, tn_ff, S, [], [(BF16, nb_ff)], _plain)

    ck_idx = jnp.stack([c_idx[0], order[0]])

    def pair_sums(names, grads, from_sibling):
        pairs = [_pair_sum("pair_sum_" + k, g, q, ck_idx) for k, g, q in zip(names, grads, from_sibling)]
        return [p[0] for p in pairs], [p[1] for p in pairs]

    ffn_names = ["w_ffn_down", "w_ffn_gate", "w_ffn_up"]
    ffn_grads = [dw_down.reshape(N_CHIPS, -1, D), dw_gate, dw_up]
    dx1, dsh_f, dsc_f, dg_ffn, dmob, dgt_m, *from_sibling = _norm_mod_bwd(
        "norm_ffn_bwd", dh2, x1, norm_ffn_g, sc_f, dx2, (mo, gt_m), job=_SwapJob(ffn_grads))
    parts_f, slots_f = pair_sums(ffn_names, ffn_grads, from_sibling)

    def merge_bwd_epi(accs, ex):
        dm, ga, gb, ya_, yb_ = accs[0], ex[0], ex[1], ex[2], ex[3]
        sa, sb = _sig(ga), _sig(gb)
        return [dm * ya_ * sa * (1.0 - sa), dm * yb_ * sb * (1.0 - sb), dm * sa, dm * sb]

    dga, dgb, dy_a, dy_b = _matmul(
        "d_merged", [(dmob, 'n', w_o, 't')], [0], S, D, D, tm2, tn_co, D,
        [(proj, 'tile', 7 * d_conv), (proj, 'tile', 7 * d_conv + D), (y_a, 'tile', 0), (y_b, 'tile', 0)],
        [(BF16, None)] * 4, merge_bwd_epi, rows_outer=True)
    dw_o, = _matmul("dw_o", [(merged, 't', dmob, 'n')], [0], D, D, S, tg, D, S, [], [(BF16, None)], _plain)
    dya, dob = _matmul("d_branch", [(dy_a, 'n', w_conv_out, 't'), (dy_b, 'n', w_hgrn_out, 't')], [0, 1], S, d_conv, D,
                       tm2, _pick(d_conv, 1024, 128), tn_co, [], [(F32, None), (F32, None)], _plain)
    dw_co, dw_ho = _matmul("dw_branch", [(ya, 't', dy_a, 'n'), (ob, 't', dy_b, 'n')], [0, 1], d_conv, D, S,
                           _pick(d_conv, 512, 128), tn_co, S, [], [(BF16, nb_co), (BF16, nb_co)], _plain)
    branch_names = ["w_conv_out", "w_hgrn_out", "w_o"]
    branch_grads = [dw_co, dw_ho, dw_o.reshape(N_CHIPS, -1, D)]
    dab, dac, dax, dconv_w, *from_sibling = _conv_bwd(dya, proj, conv_w, d_conv, job=_SwapJob(branch_grads))
    parts_b, slots_b = pair_sums(branch_names, branch_grads, from_sibling)
    dq, dz, dv, dgo, dlb, dgn, *arrived_f = _hgrn_bwd(dob, o_raw, states, qe, ke, dec, proj, lb, gnorm_g, d_conv,
                                                      job=_ScatterJob(parts_f, slots_f))
    dproj = jnp.concatenate([dab, dac, dax, dq, dz, dv, dgo, dga, dgb], axis=1)
    halves_f = [_sum_chips("sum_chips_" + k, r, c_idx) for k, r in zip(ffn_names, arrived_f)]
    dw_in, *moved = _matmul("dw_in", [(h, 't', dproj, 'n')], [0], D, d_in, S, tw, tn_in, S, [], [(BF16, nb_in)], _plain,
                            job=_Jobs([_ShareJob(halves_f), _ScatterJob(parts_b, slots_b)]))
    whole_f, arrived_b = moved[:len(ffn_names)], moved[len(ffn_names):]
    parts_i, slots_i = pair_sums(["w_in"], [dw_in], _swap_halves("swap_in", [dw_in]))
    dh, arrived_in = _matmul("d_h", [(dproj, 'n', w_in, 't')], [0], S, D, d_in, tm2, D, tn_in, [], [(F32, None)], _plain,
                             job=_ScatterJob(parts_i, slots_i))
    dx, dsh_m, dsc_m, dg_mix = _norm_mod_bwd("norm_mix_bwd", dh, x, norm_mix_g, sc_m, dx1)
    dmod = jnp.concatenate([dsh_m, dsc_m, dgt_m, dsh_f, dsc_f, dgt_f], axis=1)
    small = dict(dmod=dmod, dg_mix=dg_mix, dg_ffn=dg_ffn, dg_final=dg_final, dlb=dlb,
                 dgn=jnp.sum(dgn, axis=0), dconv_w=dconv_w)
    late_names = branch_names + ["w_in"]
    late = _share_halves([_sum_chips("sum_chips_" + k, r, c_idx) for k, r in zip(late_names, arrived_b + [arrived_in])])
    big = dict(zip(ffn_names + late_names, whole_f + list(late)))
    return loss, dx, small, big


def _adamw_math(w, g, m, v):
    m = ADAM_B1 * m + (1.0 - ADAM_B1) * g
    v = ADAM_B2 * v + (1.0 - ADAM_B2) * (g * g)
    m_hat = m / (1.0 - ADAM_B1 ** ADAM_STEP)
    v_hat = v / (1.0 - ADAM_B2 ** ADAM_STEP)
    delta = -ADAM_LR * (m_hat / (jnp.sqrt(v_hat) + ADAM_EPS) + ADAM_WD * w)
    return delta, m, v


def _adamw(name, w, g, m, v):
    R, C = w.shape
    tr = _pick(R, max(8, (256 * 1024) // C // 8 * 8), 8)
    spec = pl.BlockSpec((tr, C), lambda i: (i, 0))

    def body(w_ref, g_ref, m_ref, v_ref, go_ref, d_ref, mo_ref, vo_ref):
        gv = g_ref[...]
        d, m2, v2 = _adamw_math(w_ref[...], gv, m_ref[...], v_ref[...])
        go_ref[...] = gv
        d_ref[...] = d
        mo_ref[...] = m2
        vo_ref[...] = v2

    return pl.pallas_call(
        body, name=name, grid=(R // tr,), in_specs=[spec] * 4, out_specs=[spec] * 4,
        out_shape=[jax.ShapeDtypeStruct((R, C), F32)] * 4, compiler_params=_params(("parallel",)),
    )(w, g, m, v)


def _ada_update(c_act_t, dmod, w, m, v):
    R, C = w.shape
    B = dmod.shape[0]
    tr = _pick(R, 256, 8)
    tc = _pick(C, 1536, 128)
    spec = pl.BlockSpec((tr, tc), lambda i, j: (i, j))

    def body(ct_ref, dm_ref, w_ref, m_ref, v_ref, g_ref, d_ref, mo_ref, vo_ref):
        ct = ct_ref[...]
        dm = dm_ref[...]
        g = ct[:, 0:1] * dm[0:1, :]
        for b in range(1, B):
            g = g + ct[:, b:b + 1] * dm[b:b + 1, :]
        d, m2, v2 = _adamw_math(w_ref[...], g, m_ref[...], v_ref[...])
        g_ref[...] = g
        d_ref[...] = d
        mo_ref[...] = m2
        vo_ref[...] = v2

    return pl.pallas_call(
        body, name="ada_update", grid=(R // tr, C // tc),
        in_specs=[pl.BlockSpec((tr, B), lambda i, j: (i, 0)), pl.BlockSpec((B, tc), lambda i, j: (0, j)),
                  spec, spec, spec],
        out_specs=[spec] * 4, out_shape=[jax.ShapeDtypeStruct((R, C), F32)] * 4,
        compiler_params=_params(("parallel", "parallel")),
    )(c_act_t, dmod, w, m, v)


def _prep(c_all, lb_param):
    def body(c_ref, p_ref, ca_ref, cab_ref, lb_ref):
        cv = c_ref[...]
        ca = cv * _sig(cv)
        ca_ref[...] = ca
        cab_ref[...] = ca.astype(BF16)
        lb_ref[...] = _sig(p_ref[0:1, :] - p_ref[1:2, :])

    return pl.pallas_call(
        body, name="prep",
        out_shape=[jax.ShapeDtypeStruct(c_all.shape, F32), jax.ShapeDtypeStruct(c_all.shape, BF16),
                   jax.ShapeDtypeStruct((1, lb_param.shape[1]), F32)],
    )(c_all, lb_param)


def _small_reduce(parts):
    W = parts.shape[1]

    def body(p_ref, o_ref):
        acc = p_ref[0:1, :]
        for d in range(1, N_DEV):
            acc = acc + p_ref[d:d + 1, :]
        o_ref[...] = acc

    return pl.pallas_call(body, name="small_reduce", out_shape=jax.ShapeDtypeStruct((1, W), F32))(parts)


def _lb_grad(dlb, lb):
    def body(d_ref, lb_ref, o_ref):
        t = d_ref[...] * lb_ref[...] * (1.0 - lb_ref[...])
        o_ref[0:1, :] = t
        o_ref[1:2, :] = -t

    return pl.pallas_call(body, name="lb_grad", out_shape=jax.ShapeDtypeStruct((2, dlb.shape[1]), F32))(dlb, lb)


def _place():
    x, y, c = lax.axis_index("x"), lax.axis_index("y"), lax.axis_index("c")
    chips = [(1 - x, y), (x, 1 - y), (1 - x, 1 - y)]
    return x, y, c, chips


def _allgather_rows(name, v):
    m_per, n = v.shape

    def body(x_ref, out_ref, send_sems, recv_sems, local_sem):
        x, y, c, chips = _place()
        me, sibling = (x, y, c), (x, y, 1 - c)

        def rows(px, py, pc):
            return out_ref.at[pl.ds((4 * px + 2 * py + pc) * m_per, m_per), :]

        def copy(k, block, to, src=None):
            return pltpu.make_async_remote_copy(
                src_ref=rows(*block) if src is None else src, dst_ref=rows(*block),
                send_sem=send_sems.at[k], recv_sem=recv_sems.at[k], device_id=to, device_id_type=MESH)

        mine = pltpu.make_async_copy(x_ref, rows(*me), local_sem)
        mine.start()
        first = [copy(0, me, sibling, src=x_ref)]
        first += [copy(1 + j, me, (*chip, c), src=x_ref) for j, chip in enumerate(chips)]
        for cp in first:
            cp.start()
        passed = [copy(4 + j, (*chip, c), sibling) for j, chip in enumerate(chips)]
        for j, chip in enumerate(chips):
            copy(1 + j, (*chip, c), me).wait_recv()
            passed[j].start()
        copy(0, sibling, me).wait_recv()
        for j, chip in enumerate(chips):
            copy(4 + j, (*chip, 1 - c), me).wait_recv()
        for cp in first + passed:
            cp.wait_send()
        mine.wait()

    return pl.pallas_call(
        body, name=name, out_shape=jax.ShapeDtypeStruct((N_DEV * m_per, n), v.dtype),
        in_specs=[pl.BlockSpec(memory_space=pltpu.VMEM)], out_specs=pl.BlockSpec(memory_space=pltpu.VMEM),
        scratch_shapes=[pltpu.SemaphoreType.DMA((7,)), pltpu.SemaphoreType.DMA((7,)), pltpu.SemaphoreType.DMA],
    )(v)


def _cast_place(name, w, k_idx):
    R, C = w.shape
    tr = _pick(R, max(16, (512 * 1024) // C // 16 * 16), 16)

    def body(k_ref, w_ref, o_ref):
        o_ref[...] = w_ref[...].astype(BF16)

    return pl.pallas_call(
        body, name=name,
        grid_spec=pltpu.PrefetchScalarGridSpec(
            num_scalar_prefetch=1, grid=(R // tr,),
            in_specs=[pl.BlockSpec((tr, C), lambda i, k_ref: (i, 0))],
            out_specs=pl.BlockSpec((None, tr, C), lambda i, k_ref: (k_ref[0], i, 0))),
        out_shape=jax.ShapeDtypeStruct((N_CHIPS, R, C), BF16),
        compiler_params=_params(("parallel",)),
    )(k_idx, w)


def _chip_copies(src_of, dst_of, got_of, n, sems, receiving):
    x, y, c, chips = _place()
    k_me = 2 * x + y
    send_sems, recv_sems = sems
    out = []
    for a in range(n):
        for j, chip in enumerate(chips):
            k_chip = 2 * chip[0] + chip[1]
            if receiving:
                src = dst = got_of(a, k_chip, c)
                to = (x, y, c)
            else:
                src, dst, to = src_of(a, k_chip, k_me, c), dst_of(a, k_me, c), (*chip, c)
            out.append(pltpu.make_async_remote_copy(
                src_ref=src, dst_ref=dst, send_sem=send_sems.at[3 * a + j], recv_sem=recv_sems.at[3 * a + j],
                device_id=to, device_id_type=MESH))
    return out


class _ChipJob:
    def start(self, j_in, j_out, sems):
        for send in self.copies(j_in, j_out, sems, False):
            send.start()

    def finish(self, j_in, j_out, sems):
        for recv in self.copies(j_in, j_out, sems, True):
            recv.wait_recv()
        for send in self.copies(j_in, j_out, sems, False):
            send.wait_send()


class _GatherJob(_ChipJob):
    def __init__(self, bufs, lo=0, hi=8):
        n = len(bufs)
        self.inputs, self.n_alias = list(bufs), n
        self.sem_shapes = [pltpu.SemaphoreType.DMA((3 * n,)), pltpu.SemaphoreType.DMA((3 * n,))]
        self.half = [b.shape[1] // 2 for b in bufs]
        self.lo, self.hi = lo, hi

    def copies(self, j_in, j_out, sems, receiving):
        def half(a, k, c):
            eighth = self.half[a] // 8
            return j_out[a].at[k, pl.ds(c * self.half[a] + self.lo * eighth, (self.hi - self.lo) * eighth)]

        return _chip_copies(lambda a, k_chip, k_me, c: half(a, k_me, c), half, half, len(self.half), sems, receiving)


class _ScatterJob(_ChipJob):
    def __init__(self, parts, slots):
        n = len(parts)
        self.n = n
        self.inputs, self.n_alias = list(parts) + list(slots), n
        self.sem_shapes = [pltpu.SemaphoreType.DMA((3 * n,)), pltpu.SemaphoreType.DMA((3 * n,))]

    def copies(self, j_in, j_out, sems, receiving):
        return _chip_copies(lambda a, k_chip, k_me, c: j_in[a].at[k_chip], lambda a, k_me, c: j_out[a].at[k_me],
                            lambda a, k_chip, c: j_out[a].at[k_chip], self.n, sems, receiving)


class _SwapJob(_ChipJob):
    def __init__(self, grads):
        n = len(grads)
        self.n = n
        self.half = [g.shape[1] // 2 for g in grads]
        landing = [lax.empty((N_CHIPS, g.shape[1] // 2, g.shape[2]), g.dtype) for g in grads]
        self.inputs, self.n_alias = list(grads) + landing, n
        self.sem_shapes = [pltpu.SemaphoreType.DMA((n,)), pltpu.SemaphoreType.DMA((n,))]

    def copies(self, j_in, j_out, sems, receiving):
        x, y, c, _ = _place()
        out = []
        for a in range(self.n):
            if receiving:
                src, to = j_out[a], (x, y, c)
            else:
                src, to = j_in[a].at[:, pl.ds((1 - c) * self.half[a], self.half[a])], (x, y, 1 - c)
            out.append(pltpu.make_async_remote_copy(src_ref=src, dst_ref=j_out[a], send_sem=sems[0].at[a],
                                                    recv_sem=sems[1].at[a], device_id=to, device_id_type=MESH))
        return out


class _Jobs:
    def __init__(self, jobs):
        self.jobs = jobs
        split = [(j.inputs[:len(j.inputs) - j.n_alias], j.inputs[len(j.inputs) - j.n_alias:]) for j in jobs]
        self.inputs = [a for plain, _ in split for a in plain] + [a for _, aliased in split for a in aliased]
        self.n_alias = sum(j.n_alias for j in jobs)
        self.sem_shapes = [s for j in jobs for s in j.sem_shapes]

    def _each(self, j_in, j_out, sems):
        n_plain = len(j_in) - self.n_alias
        p0 = a0 = s0 = 0
        for j in self.jobs:
            n_p, n_s = len(j.inputs) - j.n_alias, len(j.sem_shapes)
            ins = list(j_in[p0:p0 + n_p]) + list(j_in[n_plain + a0:n_plain + a0 + j.n_alias])
            yield j, ins, j_out[a0:a0 + j.n_alias], sems[s0:s0 + n_s]
            p0, a0, s0 = p0 + n_p, a0 + j.n_alias, s0 + n_s

    def start(self, j_in, j_out, sems):
        for j, ins, outs, s in self._each(j_in, j_out, sems):
            j.start(ins, outs, s)

    def finish(self, j_in, j_out, sems):
        for j, ins, outs, s in self._each(j_in, j_out, sems):
            j.finish(ins, outs, s)


class _ShareJob(_ChipJob):
    def __init__(self, bufs):
        n = len(bufs)
        self.n = n
        self.inputs, self.n_alias = list(bufs), n
        self.sem_shapes = [pltpu.SemaphoreType.DMA((n,)), pltpu.SemaphoreType.DMA((n,))]

    def copies(self, j_in, j_out, sems, receiving):
        x, y, c, _ = _place()
        out = []
        for a in range(self.n):
            hc, to = (1 - c, (x, y, c)) if receiving else (c, (x, y, 1 - c))
            out.append(pltpu.make_async_remote_copy(
                src_ref=j_out[a].at[hc], dst_ref=j_out[a].at[hc], send_sem=sems[0].at[a], recv_sem=sems[1].at[a],
                device_id=to, device_id_type=MESH))
        return out


class _ForwardJob(_ChipJob):
    def __init__(self, bufs):
        n = len(bufs)
        self.n = n
        self.half = [b.shape[1] // 2 for b in bufs]
        self.inputs, self.n_alias = list(bufs), n
        self.sem_shapes = [pltpu.SemaphoreType.DMA((3 * n,)), pltpu.SemaphoreType.DMA((3 * n,))]

    def copies(self, j_in, j_out, sems, receiving):
        x, y, c, chips = _place()
        out = []
        for a in range(self.n):
            for q, chip in enumerate(chips):
                hc, to = (1 - c, (x, y, c)) if receiving else (c, (x, y, 1 - c))
                part = j_out[a].at[2 * chip[0] + chip[1], pl.ds(hc * self.half[a], self.half[a])]
                out.append(pltpu.make_async_remote_copy(
                    src_ref=part, dst_ref=part, send_sem=sems[0].at[3 * a + q], recv_sem=sems[1].at[3 * a + q],
                    device_id=to, device_id_type=MESH))
        return out


def _forward_halves(name, bufs):
    n = len(bufs)

    def body(*refs):
        out_refs = refs[n:2 * n]
        send_sems, recv_sems = refs[2 * n:]
        x, y, c, chips = _place()
        started, waits = [], []
        for a in range(n):
            rh = bufs[a].shape[1] // 2
            for j, chip in enumerate(chips):
                k_chip = 2 * chip[0] + chip[1]
                got = out_refs[a].at[k_chip, pl.ds(c * rh, rh)]
                cp = pltpu.make_async_remote_copy(src_ref=got, dst_ref=got, send_sem=send_sems.at[3 * a + j],
                                                  recv_sem=recv_sems.at[3 * a + j], device_id=(x, y, 1 - c),
                                                  device_id_type=MESH)
                cp.start()
                started.append(cp)
                other = out_refs[a].at[k_chip, pl.ds((1 - c) * rh, rh)]
                waits.append(pltpu.make_async_remote_copy(
                    src_ref=other, dst_ref=other, send_sem=send_sems.at[3 * a + j], recv_sem=recv_sems.at[3 * a + j],
                    device_id=(x, y, c), device_id_type=MESH))
        for cp in waits:
            cp.wait_recv()
        for cp in started:
            cp.wait_send()

    return pl.pallas_call(
        body, name=name, out_shape=[jax.ShapeDtypeStruct(b.shape, b.dtype) for b in bufs],
        in_specs=[_HBM] * n, out_specs=[_HBM] * n, input_output_aliases={a: a for a in range(n)},
        scratch_shapes=[pltpu.SemaphoreType.DMA((3 * n,)), pltpu.SemaphoreType.DMA((3 * n,))],
    )(*bufs)


def _proj_gather(h, buf, order):
    S, D = h.shape
    nb = buf.shape[2]
    tm = _pick(S, 1024, 16)
    tn = _pick(nb, 1408, 128)
    per = nb // tn
    nj, ni = N_CHIPS * per, S // tm
    rh = D // 2
    seq = [(0, t) for t in range(per)] + [(p, t) for t in range(per) for p in (1, 2)] + [(3, t) for t in range(per)]
    tile_chip = jnp.stack([order[p] for p, _ in seq])
    tile_of = jnp.array([t for _, t in seq], jnp.int32)

    def body(chip_ref, t_ref, h_ref, buf_in, proj_ref, buf_ref, wbuf, tile_sems, ici_send, ici_recv, d2d_send, d2d_recv):
        j, i = pl.program_id(0), pl.program_id(1)
        x, y, c, chips = _place()
        k_me = 2 * x + y

        def part(k, hc, t):
            return buf_ref.at[k, pl.ds(hc * rh, rh), pl.ds(t * tn, tn)]

        def ici(q, t, receiving):
            k_chip = 2 * chips[q][0] + chips[q][1]
            src, to = (part(k_chip, c, t), (x, y, c)) if receiving else (part(k_me, c, t), (*chips[q], c))
            return pltpu.make_async_remote_copy(src_ref=src, dst_ref=src, send_sem=ici_send.at[q * per + t],
                                                recv_sem=ici_recv.at[q * per + t], device_id=to, device_id_type=MESH)

        def d2d(q, t, receiving):
            k_chip = 2 * chips[q][0] + chips[q][1]
            src, to = (part(k_chip, 1 - c, t), (x, y, c)) if receiving else (part(k_chip, c, t), (x, y, 1 - c))
            return pltpu.make_async_remote_copy(src_ref=src, dst_ref=src, send_sem=d2d_send.at[q * per + t],
                                                recv_sem=d2d_recv.at[q * per + t], device_id=to, device_id_type=MESH)

        def tile_copy(n):
            col = pl.multiple_of(t_ref[n] * tn, 128)
            return pltpu.make_async_copy(buf_ref.at[chip_ref[n], :, pl.ds(col, tn)], wbuf.at[n % 2], tile_sems.at[n % 2])

        @pl.when(jnp.logical_and(j == 0, i == 0))
        def _():
            for t in range(per):
                ici(0, t, False).start()
                ici(1, t, False).start()
            tile_copy(0).start()

        @pl.when(i == 0)
        def _():
            tile_copy(j).wait()
            for n in range(per, nj):
                p, t = seq[n]

                @pl.when(j + 1 == n)
                def _():
                    ici(p - 1, t, True).wait_recv()
                    d2d(p - 1, t, False).start()
                    if (p, t) == (1, per - 1):
                        for q in range(2):
                            for tt in range(per):
                                ici(q, tt, False).wait_send()
                        for tt in range(per):
                            ici(2, tt, False).start()
                    d2d(p - 1, t, True).wait_recv()

            @pl.when(j + 1 < nj)
            def _():
                tile_copy(j + 1).start()

        proj_ref[...] = jnp.dot(h_ref[...], wbuf[j % 2], preferred_element_type=F32)

        @pl.when(jnp.logical_and(j == nj - 1, i == ni - 1))
        def _():
            for t in range(per):
                ici(2, t, False).wait_send()
                for q in range(3):
                    d2d(q, t, False).wait_send()

    n_sems = 3 * per
    return pl.pallas_call(
        body, name="proj",
        grid_spec=pltpu.PrefetchScalarGridSpec(
            num_scalar_prefetch=2, grid=(nj, ni),
            in_specs=[pl.BlockSpec((tm, D), lambda j, i, kc, kt: (i, 0)), _HBM],
            out_specs=[pl.BlockSpec((tm, tn), lambda j, i, kc, kt: (i, kc[j] * per + kt[j])), _HBM],
            scratch_shapes=[pltpu.VMEM((2, D, tn), BF16), pltpu.SemaphoreType.DMA((2,))]
            + [pltpu.SemaphoreType.DMA((n_sems,))] * 4),
        out_shape=[jax.ShapeDtypeStruct((S, N_CHIPS * nb), F32), jax.ShapeDtypeStruct(buf.shape, buf.dtype)],
        input_output_aliases={3: 1}, compiler_params=_params(("arbitrary", "arbitrary")),
    )(tile_chip, tile_of, h, buf)


def _swap_halves(name, grads):
    n = len(grads)

    def body(*refs):
        in_refs, out_refs = refs[:n], refs[n:2 * n]
        send_sems, recv_sems = refs[2 * n:]
        x, y, c, _ = _place()
        cps = []
        for a in range(n):
            rh = grads[a].shape[1] // 2
            cp = pltpu.make_async_remote_copy(
                src_ref=in_refs[a].at[:, pl.ds((1 - c) * rh, rh)], dst_ref=out_refs[a],
                send_sem=send_sems.at[a], recv_sem=recv_sems.at[a], device_id=(x, y, 1 - c), device_id_type=MESH)
            cp.start()
            cps.append(cp)
        for cp in cps:
            cp.wait()

    return pl.pallas_call(
        body, name=name,
        out_shape=[jax.ShapeDtypeStruct((N_CHIPS, g.shape[1] // 2, g.shape[2]), g.dtype) for g in grads],
        in_specs=[_HBM] * n, out_specs=[_HBM] * n,
        scratch_shapes=[pltpu.SemaphoreType.DMA((n,)), pltpu.SemaphoreType.DMA((n,))],
    )(*grads)


def _pair_sum(name, g, q, ck_idx):
    _, R, C = g.shape
    rh = R // 2
    tr = _pick(rh, max(16, (512 * 1024) // C // 16 * 16), 16)
    nh = rh // tr

    def body(ck_ref, g_ref, q_ref, o_ref, own_ref):
        s = (g_ref[...].astype(F32) + q_ref[...].astype(F32)).astype(BF16)
        o_ref[...] = s

        @pl.when(pl.program_id(1) == ck_ref[1])
        def _():
            own_ref[...] = s

    return pl.pallas_call(
        body, name=name,
        grid_spec=pltpu.PrefetchScalarGridSpec(
            num_scalar_prefetch=1, grid=(nh, N_CHIPS),
            in_specs=[pl.BlockSpec((None, tr, C), lambda i, k, ck: (k, ck[0] * nh + i, 0)),
                      pl.BlockSpec((None, tr, C), lambda i, k, ck: (k, i, 0))],
            out_specs=[pl.BlockSpec((None, tr, C), lambda i, k, ck: (k, i, 0)),
                       pl.BlockSpec((None, tr, C), lambda i, k, ck: (ck[1], i, 0))]),
        out_shape=[jax.ShapeDtypeStruct((N_CHIPS, rh, C), BF16)] * 2,
        compiler_params=_params(("parallel", "arbitrary")),
    )(ck_idx, g, q)


def _sum_chips(name, r, c_idx):
    _, rh, C = r.shape
    tr = _pick(rh, max(16, (256 * 1024) // C // 16 * 16), 16)

    def body(c_ref, r_ref, o_ref):
        acc = r_ref[0].astype(F32)
        for k in range(1, N_CHIPS):
            acc = acc + r_ref[k].astype(F32)
        o_ref[...] = acc

    return pl.pallas_call(
        body, name=name,
        grid_spec=pltpu.PrefetchScalarGridSpec(
            num_scalar_prefetch=1, grid=(rh // tr,),
            in_specs=[pl.BlockSpec((N_CHIPS, tr, C), lambda i, c_ref: (0, i, 0))],
            out_specs=pl.BlockSpec((None, tr, C), lambda i, c_ref: (c_ref[0], i, 0))),
        out_shape=jax.ShapeDtypeStruct((2, rh, C), F32), compiler_params=_params(("parallel",)),
    )(c_idx, r)


def _share_halves(bufs):
    n = len(bufs)

    def body(*refs):
        out_refs = refs[n:2 * n]
        send_sems, recv_sems = refs[2 * n:]
        x, y, c, _ = _place()
        cps = []
        for a in range(n):
            cp = pltpu.make_async_remote_copy(
                src_ref=out_refs[a].at[c], dst_ref=out_refs[a].at[c], send_sem=send_sems.at[a],
                recv_sem=recv_sems.at[a], device_id=(x, y, 1 - c), device_id_type=MESH)
            cp.start()
            cps.append(cp)
        for a, cp in enumerate(cps):
            got = out_refs[a].at[1 - c]
            pltpu.make_async_remote_copy(src_ref=got, dst_ref=got, send_sem=send_sems.at[a], recv_sem=recv_sems.at[a],
                                         device_id=(x, y, c), device_id_type=MESH).wait_recv()
        for cp in cps:
            cp.wait_send()

    return pl.pallas_call(
        body, name="share_halves",
        out_shape=[jax.ShapeDtypeStruct(b.shape, b.dtype) for b in bufs],
        in_specs=[_HBM] * n, out_specs=[_HBM] * n, input_output_aliases={a: a for a in range(n)},
        scratch_shapes=[pltpu.SemaphoreType.DMA((n,)), pltpu.SemaphoreType.DMA((n,))],
    )(*bufs)


_BIG = ("w_in", "w_conv_out", "w_hgrn_out", "w_o", "w_ffn_gate", "w_ffn_up", "w_ffn_down")
_WEIGHTS = ("w_ada", "b_ada", "norm_mix_g", "w_in", "conv_w", "lb_param", "gnorm_g", "w_conv_out", "w_hgrn_out",
            "w_o", "norm_ffn_g", "w_ffn_gate", "w_ffn_up", "w_ffn_down", "norm_final_g")


def _pack_rows(pieces):
    flat = jnp.concatenate([p.reshape(-1) for p in pieces])
    pad = (-flat.shape[0]) % (SUBLANES * LANES)
    return jnp.pad(flat, (0, pad)).reshape(SUBLANES, -1)


def kernel(x, c, w_ada, b_ada, norm_mix_g, w_in, conv_w, lb_param, gnorm_g, w_conv_out, w_hgrn_out, w_o, norm_ffn_g, w_ffn_gate, w_ffn_up, w_ffn_down, norm_final_g, loss_target, m_w_ada, m_b_ada, m_norm_mix_g, m_w_in, m_conv_w, m_lb_param, m_gnorm_g, m_w_conv_out, m_w_hgrn_out, m_w_o, m_norm_ffn_g, m_w_ffn_gate, m_w_ffn_up, m_w_ffn_down, m_norm_final_g, v_w_ada, v_b_ada, v_norm_mix_g, v_w_in, v_conv_w, v_lb_param, v_gnorm_g, v_w_conv_out, v_w_hgrn_out, v_w_o, v_norm_ffn_g, v_w_ffn_gate, v_w_ffn_up, v_w_ffn_down, v_norm_final_g):
    w = dict(w_ada=w_ada, b_ada=b_ada, norm_mix_g=norm_mix_g, w_in=w_in, conv_w=conv_w, lb_param=lb_param,
             gnorm_g=gnorm_g, w_conv_out=w_conv_out, w_hgrn_out=w_hgrn_out, w_o=w_o, norm_ffn_g=norm_ffn_g,
             w_ffn_gate=w_ffn_gate, w_ffn_up=w_ffn_up, w_ffn_down=w_ffn_down, norm_final_g=norm_final_g)
    m = dict(w_ada=m_w_ada, b_ada=m_b_ada, norm_mix_g=m_norm_mix_g, w_in=m_w_in, conv_w=m_conv_w, lb_param=m_lb_param,
             gnorm_g=m_gnorm_g, w_conv_out=m_w_conv_out, w_hgrn_out=m_w_hgrn_out, w_o=m_w_o, norm_ffn_g=m_norm_ffn_g,
             w_ffn_gate=m_w_ffn_gate, w_ffn_up=m_w_ffn_up, w_ffn_down=m_w_ffn_down, norm_final_g=m_norm_final_g)
    v = dict(w_ada=v_w_ada, b_ada=v_b_ada, norm_mix_g=v_norm_mix_g, w_in=v_w_in, conv_w=v_conv_w, lb_param=v_lb_param,
             gnorm_g=v_gnorm_g, w_conv_out=v_w_conv_out, w_hgrn_out=v_w_hgrn_out, w_o=v_w_o, norm_ffn_g=v_norm_ffn_g,
             w_ffn_gate=v_w_ffn_gate, w_ffn_up=v_w_ffn_up, w_ffn_down=v_w_ffn_down, norm_final_g=v_norm_final_g)
    two_d = lambda a: a.reshape((-1, a.shape[-1])) if a.ndim != 2 else a
    shapes = {k: a.shape for k, a in w.items()}
    w, m, v = ({k: two_d(a) for k, a in t.items()} for t in (w, m, v))
    w["lb_param"], m["lb_param"], v["lb_param"] = lb_param, m_lb_param, v_lb_param
    S, D = x.shape[1], x.shape[2]
    d_conv = D // 2
    ix, iy, ic = lax.axis_index("x"), lax.axis_index("y"), lax.axis_index("c")
    k_me = 2 * ix + iy
    dev = 2 * k_me + ic
    cw_shard = w["conv_w"].shape[1]
    ada_cols = w["w_ada"].shape[1]

    got = _allgather_rows("gather_cond", _pack_rows([c, w["conv_w"]])).reshape(N_DEV, -1)
    c_all = got[:, :D]
    conv_full = got[::2, D:D + 3 * cw_shard].reshape(N_CHIPS, 3, cw_shard).transpose(1, 0, 2).reshape(3, -1)
    c_act, c_act_b, lb = _prep(c_all, lb_param)
    b_cols = lax.dynamic_slice(w["b_ada"], (0, k_me * ada_cols), (1, ada_cols))
    mod_cols, = _matmul("ada_fwd", [(c_act_b, 'n', w["w_ada"].astype(BF16), 'n')], [0], N_DEV, ada_cols, D,
                        N_DEV, _pick(ada_cols, 1536, 128), D, [(b_cols, 'row', 0)], [(F32, None)],
                        lambda accs, ex: [accs[0] + ex[0]])
    mod_all = _allgather_rows("gather_mod", mod_cols).reshape(N_CHIPS, 2, N_DEV, ada_cols)[:, 0]
    mod_all = mod_all.transpose(1, 0, 2).reshape(N_DEV, -1)
    mod = lax.dynamic_slice(mod_all, (dev, 0), (1, mod_all.shape[1]))
    k_idx = jnp.reshape(k_me, (1,)).astype(jnp.int32)
    c_idx = jnp.reshape(ic, (1,)).astype(jnp.int32)
    bufs = {k: _cast_place("cast_" + k, w[k], k_idx) for k in _BIG}
    order = jnp.stack([k_me, 2 * (1 - ix) + iy, 2 * ix + (1 - iy), 2 * (1 - ix) + (1 - iy)]).astype(jnp.int32)

    loss, dx, small, arrived = _local_step(
        x[0], loss_target[0], mod, w["norm_mix_g"], lb, w["gnorm_g"], w["norm_ffn_g"], w["norm_final_g"], conv_full,
        bufs, c_idx, order)

    pieces = [small["dmod"], small["dg_mix"], small["dg_ffn"], small["dg_final"], small["dlb"], small["dgn"],
              small["dconv_w"], loss]
    sizes = [p.size for p in pieces]
    parts = _allgather_rows("gather_small", _pack_rows(pieces)).reshape(N_DEV, -1)
    total = _small_reduce(parts)
    offs = [0]
    for s in sizes:
        offs.append(offs[-1] + s)
    tot = [total[:, offs[i]:offs[i + 1]] for i in range(len(sizes))]
    dmod_all = parts[:, :sizes[0]]
    grads = {
        "b_ada": tot[0], "norm_mix_g": tot[1], "norm_ffn_g": tot[2], "norm_final_g": tot[3],
        "lb_param": _lb_grad(tot[4], lb), "gnorm_g": tot[5],
        "conv_w": lax.dynamic_slice(tot[6].reshape(3, -1), (0, k_me * cw_shard), (3, cw_shard)),
    }
    loss_out = tot[7][0, 0]

    for k in _BIG:
        grads[k] = arrived[k].reshape(-1, arrived[k].shape[-1])

    delta, new_m, new_v = {}, {}, {}
    dmod_cols = lax.dynamic_slice(dmod_all, (0, k_me * ada_cols), (N_DEV, ada_cols))
    grads["w_ada"], delta["w_ada"], new_m["w_ada"], new_v["w_ada"] = _ada_update(
        c_act.T, dmod_cols, w["w_ada"], m["w_ada"], v["w_ada"])
    for k in _WEIGHTS:
        if k != "w_ada":
            grads[k], delta[k], new_m[k], new_v[k] = _adamw("adamw_" + k, w[k], grads[k], m[k], v[k])
    outs = [loss_out, dx.reshape(x.shape)]
    for t in (grads, delta, new_m, new_v):
        outs += [t[k].reshape(shapes[k]) for k in _WEIGHTS]
    return tuple(outs)
```

```python
import functools

import jax
import jax.numpy as jnp
from jax import lax
from jax.experimental import pallas as pl
from jax.experimental.pallas import tpu as pltpu

F32 = jnp.float32
BF16 = jnp.bfloat16
MESH = pl.DeviceIdType.MESH

EPS = 1e-6
HEAD = 128
BLK = 16
N_CHIPS = 4
N_DEV = 8
SUBLANES, LANES = 8, 128
VMEM_LIMIT_BYTES = 56 * 1024 * 1024

ADAM_LR = 0.001
ADAM_B1 = 0.9
ADAM_B2 = 0.999
ADAM_EPS = 1e-08
ADAM_WD = 0.01
ADAM_STEP = 10


def _pick(dim, pref, mult):
    if dim <= pref:
        return dim
    t = (pref // mult) * mult
    while t >= mult:
        if dim % t == 0:
            return t
        t -= mult
    return dim


def _sig(x):
    return 1.0 / (1.0 + jnp.exp(-x))


def _params(sem):
    return pltpu.CompilerParams(dimension_semantics=sem, vmem_limit_bytes=VMEM_LIMIT_BYTES)


def _gj(j, i, k):
    return j


def _gk(j, i, k):
    return k


def _wspec(arr, rt, ct, r_of, c_of):
    if arr.ndim == 2:
        return pl.BlockSpec((rt, ct), lambda j, i, k: (r_of(j, i, k), c_of(j, i, k)))
    per = arr.shape[2] // ct
    assert arr.shape[2] % ct == 0
    return pl.BlockSpec((None, rt, ct),
                        lambda j, i, k: (c_of(j, i, k) // per, r_of(j, i, k), c_of(j, i, k) % per))


_HBM = pl.BlockSpec(memory_space=pltpu.HBM)


def _host(job, grid, in_specs, args, out_specs, out_shape, scratch):
    if job is None:
        return {}, (lambda body: body)
    n_in, n_out, n_scr = len(args), len(out_shape), len(scratch)
    n_job, n_alias = len(job.inputs), job.n_alias
    in_specs += [_HBM] * n_job
    args += job.inputs
    out_specs += [_HBM] * n_alias
    out_shape += [jax.ShapeDtypeStruct(a.shape, a.dtype) for a in job.inputs[n_job - n_alias:]]
    scratch += job.sem_shapes
    aliases = {n_in + n_job - n_alias + t: n_out + t for t in range(n_alias)}

    def wrap(body):
        def hosted(*refs):
            ins, refs = refs[:n_in], refs[n_in:]
            j_in, refs = refs[:n_job], refs[n_job:]
            outs, refs = refs[:n_out], refs[n_out:]
            j_out, refs = refs[:n_alias], refs[n_alias:]
            scr, sems = refs[:n_scr], refs[n_scr:]
            first = functools.reduce(jnp.logical_and, [pl.program_id(d) == 0 for d in range(len(grid))])
            last = functools.reduce(jnp.logical_and, [pl.program_id(d) == grid[d] - 1 for d in range(len(grid))])

            @pl.when(first)
            def _():
                job.start(j_in, j_out, sems)

            body(*ins, *outs, *scr)

            @pl.when(last)
            def _():
                job.finish(j_in, j_out, sems)

        return hosted

    return aliases, wrap


EPILOGUE_COLS = 768


def _plain(accs, extras):
    return accs


def _matmul(name, pairs, acc_of, M, N, K, tm, tn, tk, extras, outs, epilogue, job=None, rows_outer=False):
    assert M % tm == 0 and N % tn == 0 and K % tk == 0, (name, M, N, K, tm, tn, tk)
    nj, ni, nk = N // tn, M // tm, K // tk
    n_pairs, n_extra, n_out = len(pairs), len(extras), len(outs)
    n_acc = max(acc_of) + 1
    in_specs, args, dims = [], [], []
    lhs_of, seen = [], {}
    for a, am, b, bm in pairs:
        if (id(a), am) not in seen:
            seen[id(a), am] = len(args)
            args.append(a)
            if am == 'n':
                in_specs.append(pl.BlockSpec((tm, tk), lambda j, i, k: (i, k)))
            else:
                in_specs.append(pl.BlockSpec((tk, tm), lambda j, i, k: (k, i)))
        lhs_of.append(seen[id(a), am])
    n_lhs = len(args)
    for a, am, b, bm in pairs:
        if bm == 'n':
            in_specs.append(_wspec(b, tk, tn, _gk, _gj))
        else:
            in_specs.append(_wspec(b, tn, tk, _gj, _gk))
        dims.append((((1 if am == 'n' else 0,), (0 if bm == 'n' else 1,)), ((), ())))
        args.append(b)
    for e, kind, off in extras:
        assert off % tn == 0, (name, off, tn)
        o = off // tn
        if kind == 'tile':
            in_specs.append(pl.BlockSpec((tm, tn), lambda j, i, k, o=o: (i, j + o)))
        else:
            in_specs.append(pl.BlockSpec((1, tn), lambda j, i, k, o=o: (0, j + o)))
        args.append(e)
    out_shape, out_specs = [], []
    for dt, nb in outs:
        if nb is None:
            out_shape.append(jax.ShapeDtypeStruct((M, N), dt))
            out_specs.append(pl.BlockSpec((tm, tn), lambda j, i, k: (i, j)))
        else:
            assert nb % tn == 0 and N % nb == 0
            per = nb // tn
            out_shape.append(jax.ShapeDtypeStruct((N // nb, M, nb), dt))
            out_specs.append(pl.BlockSpec((None, tm, tn), lambda j, i, k, per=per: (j // per, i, j % per)))

    def body(*refs):
        n_ab = n_lhs + n_pairs
        e_refs = refs[n_ab:n_ab + n_extra]
        o_refs = refs[n_ab + n_extra:n_ab + n_extra + n_out]
        acc_refs = refs[n_ab + n_extra + n_out:]

        def products(cols):
            sums = [None] * n_acc
            for p in range(n_pairs):
                b_ref = refs[n_lhs + p]
                b = b_ref[:, cols] if pairs[p][3] == 'n' else b_ref[cols, :]
                prod = lax.dot_general(refs[lhs_of[p]][...], b, dims[p], preferred_element_type=F32)
                a = acc_of[p]
                sums[a] = prod if sums[a] is None else sums[a] + prod
            return sums

        def finish(accs, cols):
            res = epilogue(accs, [e[:, cols] for e in e_refs])
            for o_ref, r in zip(o_refs, res):
                o_ref[:, cols] = r.astype(o_ref.dtype)

        whole = slice(0, tn)
        if nk == 1:
            step = tn if epilogue is _plain else EPILOGUE_COLS
            for c0 in range(0, tn, step):
                cols = slice(c0, min(c0 + step, tn))
                finish(products(cols), cols)
            return
        sums = products(whole)
        k = pl.program_id(2)
        targets = o_refs if sum_in_outs else acc_refs

        @pl.when(k == 0)
        def _():
            for a in range(n_acc):
                targets[a][...] = sums[a]

        @pl.when(k > 0)
        def _():
            for a in range(n_acc):
                targets[a][...] += sums[a]

        if not sum_in_outs:
            @pl.when(k == nk - 1)
            def _():
                finish([acc_refs[a][...] for a in range(n_acc)], whole)

    sum_in_outs = epilogue is _plain and nk > 1 and n_out == n_acc and all(dt == F32 for dt, _ in outs)
    scratch = [pltpu.VMEM((tm, tn), F32) for _ in range(n_acc)] if nk > 1 and not sum_in_outs else []
    grid = (nj, ni, nk)
    if rows_outer:
        grid = (ni, nj, nk)
        swap = lambda spec: pl.BlockSpec(spec.block_shape, lambda i, j, k, f=spec.index_map: f(j, i, k))
        in_specs, out_specs = [swap(s) for s in in_specs], [swap(s) for s in out_specs]
    aliases, wrap = _host(job, grid, in_specs, args, out_specs, out_shape, scratch)
    sem = ("parallel", "parallel", "arbitrary") if job is None else ("arbitrary",) * 3
    return pl.pallas_call(
        wrap(body), name=name, grid=grid, in_specs=in_specs, out_specs=out_specs, out_shape=out_shape,
        scratch_shapes=scratch, input_output_aliases=aliases, compiler_params=_params(sem),
    )(*args)


def _row_spec(tr, d):
    return pl.BlockSpec((tr, d), lambda i: (i, 0))


def _vec_spec(d):
    return pl.BlockSpec((1, d), lambda i: (0, 0))


def _norm_mod(name, x, g, sc, sh, job=None):
    S, D = x.shape
    tr = _pick(S, 256, 8)

    def body(x_ref, g_ref, sc_ref, sh_ref, h_ref):
        xv = x_ref[...]
        r = lax.rsqrt(jnp.mean(xv * xv, axis=-1, keepdims=True) + EPS)
        h_ref[...] = ((xv * r) * g_ref[...] * (1.0 + sc_ref[...]) + sh_ref[...]).astype(BF16)

    grid = (S // tr,)
    in_specs = [_row_spec(tr, D), _vec_spec(D), _vec_spec(D), _vec_spec(D)]
    args = [x, g, sc, sh]
    out_specs, out_shape, scratch = [_row_spec(tr, D)], [jax.ShapeDtypeStruct((S, D), BF16)], []
    aliases, wrap = _host(job, grid, in_specs, args, out_specs, out_shape, scratch)
    res = pl.pallas_call(
        wrap(body), name=name, grid=grid, in_specs=in_specs, out_specs=out_specs, out_shape=out_shape,
        scratch_shapes=scratch, input_output_aliases=aliases,
        compiler_params=_params(("parallel" if job is None else "arbitrary",)),
    )(*args)
    return res[0] if job is None else res


def _loss_head(x2, target, g, ff, gt):
    S, D = x2.shape
    tr = _pick(S, 256, 8)

    def body(x_ref, t_ref, g_ref, ff_ref, gt_ref, dx_ref, dffb_ref, loss_ref, dgt_ref, dg_ref):
        i = pl.program_id(0)

        @pl.when(i == 0)
        def _():
            loss_ref[...] = jnp.zeros_like(loss_ref)
            dgt_ref[...] = jnp.zeros_like(dgt_ref)
            dg_ref[...] = jnp.zeros_like(dg_ref)

        xv = x_ref[...]
        gv = g_ref[...]
        r = lax.rsqrt(jnp.mean(xv * xv, axis=-1, keepdims=True) + EPS)
        xh = xv * r
        err = xh * gv - t_ref[...]
        loss_ref[...] += 0.5 * jnp.sum(jnp.mean(err * err, axis=-1, keepdims=True))
        dy = err * (1.0 / D)
        dg_ref[...] += jnp.sum(dy * xh, axis=0, keepdims=True)
        dxh = dy * gv
        dx = r * (dxh - xh * jnp.mean(dxh * xh, axis=-1, keepdims=True))
        dx_ref[...] = dx
        dffb_ref[...] = (dx * gt_ref[...]).astype(BF16)
        dgt_ref[...] += jnp.sum(dx * ff_ref[...], axis=0, keepdims=True)

    return pl.pallas_call(
        body, name="loss_head", grid=(S // tr,),
        in_specs=[_row_spec(tr, D), _row_spec(tr, D), _vec_spec(D), _row_spec(tr, D), _vec_spec(D)],
        out_specs=[_row_spec(tr, D), _row_spec(tr, D), pl.BlockSpec((1, 128), lambda i: (0, 0)),
                   _vec_spec(D), _vec_spec(D)],
        out_shape=[jax.ShapeDtypeStruct((S, D), F32), jax.ShapeDtypeStruct((S, D), BF16),
                   jax.ShapeDtypeStruct((1, 128), F32), jax.ShapeDtypeStruct((1, D), F32),
                   jax.ShapeDtypeStruct((1, D), F32)],
        compiler_params=_params(("arbitrary",)),
    )(x2, target, g, ff, gt)


def _norm_mod_bwd(name, dh, x, g, sc, dres, gated=None, job=None):
    S, D = x.shape
    tr = _pick(S, 256, 8)
    n = S // tr
    with_gate = gated is not None

    def body(*refs):
        if with_gate:
            dh_ref, x_ref, g_ref, sc_ref, dres_ref, mo_ref, gt_ref, dx_ref, dsh_ref, dsc_ref, dg_ref, dmob_ref, dgt_ref = refs
        else:
            dh_ref, x_ref, g_ref, sc_ref, dres_ref, dx_ref, dsh_ref, dsc_ref, dg_ref = refs
        i = pl.program_id(0)

        @pl.when(i == 0)
        def _():
            dsh_ref[...] = jnp.zeros_like(dsh_ref)
            dsc_ref[...] = jnp.zeros_like(dsc_ref)
            if with_gate:
                dgt_ref[...] = jnp.zeros_like(dgt_ref)

        xv = x_ref[...]
        dhv = dh_ref[...]
        gv = g_ref[...]
        scale = 1.0 + sc_ref[...]
        r = lax.rsqrt(jnp.mean(xv * xv, axis=-1, keepdims=True) + EPS)
        xh = xv * r
        dsh_ref[...] += jnp.sum(dhv, axis=0, keepdims=True)
        dsc_ref[...] += jnp.sum(dhv * xh, axis=0, keepdims=True)
        dxh = dhv * (gv * scale)
        dx = dres_ref[...] + r * (dxh - xh * jnp.mean(dxh * xh, axis=-1, keepdims=True))
        dx_ref[...] = dx
        if with_gate:
            dmob_ref[...] = (dx * gt_ref[...]).astype(BF16)
            dgt_ref[...] += jnp.sum(dx * mo_ref[...], axis=0, keepdims=True)

        @pl.when(i == n - 1)
        def _():
            t = dsc_ref[...]
            dg_ref[...] = t * scale
            dsc_ref[...] = t * gv

    in_specs = [_row_spec(tr, D), _row_spec(tr, D), _vec_spec(D), _vec_spec(D), _row_spec(tr, D)]
    args = [dh, x, g, sc, dres]
    out_specs = [_row_spec(tr, D), _vec_spec(D), _vec_spec(D), _vec_spec(D)]
    out_shape = [jax.ShapeDtypeStruct((S, D), F32)] + [jax.ShapeDtypeStruct((1, D), F32)] * 3
    if with_gate:
        in_specs += [_row_spec(tr, D), _vec_spec(D)]
        args += list(gated)
        out_specs += [_row_spec(tr, D), _vec_spec(D)]
        out_shape += [jax.ShapeDtypeStruct((S, D), BF16), jax.ShapeDtypeStruct((1, D), F32)]
    scratch = []
    aliases, wrap = _host(job, (n,), in_specs, args, out_specs, out_shape, scratch)
    return pl.pallas_call(
        wrap(body), name=name, grid=(n,), in_specs=in_specs, out_specs=out_specs, out_shape=out_shape,
        scratch_shapes=scratch, input_output_aliases=aliases, compiler_params=_params(("arbitrary",)),
    )(*args)


CONV_ROWS = 256


def _col_spec(S, off_cols):
    o = off_cols // HEAD
    return pl.BlockSpec((S, HEAD), lambda c, o=o: (0, c + o))


def _conv_taps(u, u_prev, rid):
    s1 = jnp.where(rid >= 1, pltpu.roll(u, 1, 0), pltpu.roll(u_prev, 1, 0))
    s2 = jnp.where(rid >= 2, pltpu.roll(u, 2, 0), pltpu.roll(u_prev, 2, 0))
    return s1, s2


def _conv_fwd(proj, conv_w, d_conv, job=None):
    S = proj.shape[0]
    R = min(CONV_ROWS, S)
    nr = S // R

    def body(ab_ref, ac_ref, ax_ref, w_ref, ya_ref):
        w0, w1, w2 = w_ref[0:1, :], w_ref[1:2, :], w_ref[2:3, :]
        rid = lax.broadcasted_iota(jnp.int32, (R, HEAD), 0)

        def step(c, carry):
            rows = pl.ds(pl.multiple_of(c * R, R), R)
            prev = pl.ds(pl.multiple_of(jnp.maximum(c - 1, 0) * R, R), R)
            u = ac_ref[rows, :] * ax_ref[rows, :]
            u_prev = jnp.where(c > 0, ac_ref[prev, :] * ax_ref[prev, :], 0.0)
            s1, s2 = _conv_taps(u, u_prev, rid)
            y = w0 * s2 + w1 * s1 + w2 * u
            ya_ref[rows, :] = (ab_ref[rows, :] * y).astype(BF16)
            return carry

        lax.fori_loop(0, nr, step, 0)

    grid = (d_conv // HEAD,)
    in_specs = [_col_spec(S, 0), _col_spec(S, d_conv), _col_spec(S, 2 * d_conv), pl.BlockSpec((3, HEAD), lambda c: (0, c))]
    args = [proj, proj, proj, conv_w]
    out_specs, out_shape = [pl.BlockSpec((S, HEAD), lambda c: (0, c))], [jax.ShapeDtypeStruct((S, d_conv), BF16)]
    scratch = []
    aliases, wrap = _host(job, grid, in_specs, args, out_specs, out_shape, scratch)
    res = pl.pallas_call(
        wrap(body), name="conv_fwd", grid=grid, in_specs=in_specs, out_specs=out_specs, out_shape=out_shape,
        scratch_shapes=scratch, input_output_aliases=aliases,
        compiler_params=_params(("parallel" if job is None else "arbitrary",)),
    )(*args)
    return res[0] if job is None else res


def _conv_bwd(dya, proj, conv_w, d_conv, job=None):
    S = proj.shape[0]
    R = min(CONV_ROWS, S)
    nr = S // R

    def body(dya_ref, ab_ref, ac_ref, ax_ref, w_ref, dab_ref, dac_ref, dax_ref, dw_ref):
        w0, w1, w2 = w_ref[0:1, :], w_ref[1:2, :], w_ref[2:3, :]
        rid = lax.broadcasted_iota(jnp.int32, (R, HEAD), 0)

        def step(c, carry):
            dw0, dw1, dw2 = carry
            rows = pl.ds(pl.multiple_of(c * R, R), R)
            prev = pl.ds(pl.multiple_of(jnp.maximum(c - 1, 0) * R, R), R)
            nxt = pl.ds(pl.multiple_of(jnp.minimum(c + 1, nr - 1) * R, R), R)
            a_c, a_x, a_b = ac_ref[rows, :], ax_ref[rows, :], ab_ref[rows, :]
            u = a_c * a_x
            u_prev = jnp.where(c > 0, ac_ref[prev, :] * ax_ref[prev, :], 0.0)
            s1, s2 = _conv_taps(u, u_prev, rid)
            y = w0 * s2 + w1 * s1 + w2 * u
            dy = dya_ref[rows, :]
            dab_ref[rows, :] = (dy * y).astype(BF16)
            dyc = dy * a_b
            dyc_next = jnp.where(c < nr - 1, dya_ref[nxt, :] * ab_ref[nxt, :], 0.0)
            t1 = jnp.where(rid < R - 1, pltpu.roll(dyc, R - 1, 0), pltpu.roll(dyc_next, R - 1, 0))
            t2 = jnp.where(rid < R - 2, pltpu.roll(dyc, R - 2, 0), pltpu.roll(dyc_next, R - 2, 0))
            du = w2 * dyc + w1 * t1 + w0 * t2
            dac_ref[rows, :] = (du * a_x).astype(BF16)
            dax_ref[rows, :] = (du * a_c).astype(BF16)
            dw0 = dw0 + jnp.sum(dyc * s2, axis=0, keepdims=True)
            dw1 = dw1 + jnp.sum(dyc * s1, axis=0, keepdims=True)
            dw2 = dw2 + jnp.sum(dyc * u, axis=0, keepdims=True)
            return dw0, dw1, dw2

        z = jnp.zeros((1, HEAD), F32)
        dw0, dw1, dw2 = lax.fori_loop(0, nr, step, (z, z, z))
        dw_ref[0:1, :] = dw0
        dw_ref[1:2, :] = dw1
        dw_ref[2:3, :] = dw2

    col = pl.BlockSpec((S, HEAD), lambda c: (0, c))
    grid = (d_conv // HEAD,)
    in_specs = [col, _col_spec(S, 0), _col_spec(S, d_conv), _col_spec(S, 2 * d_conv),
                pl.BlockSpec((3, HEAD), lambda c: (0, c))]
    args = [dya, proj, proj, proj, conv_w]
    out_specs = [col, col, col, pl.BlockSpec((3, HEAD), lambda c: (0, c))]
    out_shape = [jax.ShapeDtypeStruct((S, d_conv), BF16)] * 3 + [jax.ShapeDtypeStruct((3, d_conv), F32)]
    scratch = []
    aliases, wrap = _host(job, grid, in_specs, args, out_specs, out_shape, scratch)
    return pl.pallas_call(
        wrap(body), name="conv_bwd", grid=grid, in_specs=in_specs, out_specs=out_specs, out_shape=out_shape,
        scratch_shapes=scratch, input_output_aliases=aliases,
        compiler_params=_params(("parallel" if job is None else "arbitrary",)),
    )(*args)


HGRN_FWD_ROWS = 512
HGRN_BWD_ROWS = 1024
HGRN_FWD_SUB = 64
HGRN_BWD_SUB = 32
HGRN_GATE_ROWS = 128


def _block_cumsum(x, pos):
    for s in (1, 2, 4, 8):
        x = x + jnp.where(pos >= s, pltpu.roll(x, s, 0), 0.0)
    return x


def _block_rev_cumsum(x, pos, rows):
    for s in (1, 2, 4, 8):
        x = x + jnp.where(pos < BLK - s, pltpu.roll(x, rows - s, 0), 0.0)
    return x


def _block_last(x, pos, rows):
    m = jnp.where(pos == BLK - 1, x, 0.0)
    for s in (1, 2, 4, 8):
        m = m + pltpu.roll(m, rows - s, 0)
    return m


def _hgrn_gates(q, z, lb):
    sgq = _sig(q)
    qs = q * sgq
    sg = _sig(z)
    f = lb + (1.0 - lb) * sg
    kk = (1.0 - lb) * (1.0 - sg)
    return sgq, qs, sg, f, kk


def _hgrn_decays(q, z, lb, pos, rows):
    sgq, qs, sg, f, kk = _hgrn_gates(q, z, lb)
    b = _block_cumsum(jnp.log(f), pos)
    return sgq, qs, sg, f, kk, b, _block_last(b, pos, rows)


GROUP = HEAD // BLK


def _block_outer_products(a_ref, b_ref, u_ref, nb):
    rows = GROUP * BLK
    blk = lax.broadcasted_iota(jnp.int32, (rows, HEAD), 0) // BLK
    zero = jnp.zeros((rows, HEAD), BF16)
    for g in range(nb // GROUP):
        r = pl.ds(g * rows, rows)
        b = b_ref[r, :]
        spread = jnp.concatenate([jnp.where(blk == j, b, zero) for j in range(GROUP)], axis=1)
        u = lax.dot_general(a_ref[r, :], spread, (((0,), (0,)), ((), ())), preferred_element_type=F32)
        for j in range(GROUP):
            u_ref[g * GROUP + j] = u[:, j * HEAD:(j + 1) * HEAD]


def _hgrn_specs(T, d_conv, n_t, rev):
    def t_of(i):
        return (n_t - 1 - i) if rev else i

    def pcol(off):
        o = off // HEAD
        return pl.BlockSpec((T, HEAD), lambda h, i, o=o: (t_of(i), h + o))

    q_spec, z_spec, v_spec, g_spec = pcol(3 * d_conv), pcol(4 * d_conv), pcol(5 * d_conv), pcol(6 * d_conv)
    lb_spec = pl.BlockSpec((1, HEAD), lambda h, i: (0, h))
    gn_spec = pl.BlockSpec((1, HEAD), lambda h, i: (0, 0))
    act_spec = pl.BlockSpec((T, HEAD), lambda h, i: (t_of(i), h))
    st_spec = pl.BlockSpec((None, T // BLK, HEAD, HEAD), lambda h, i: (h, t_of(i), 0, 0))
    return q_spec, z_spec, v_spec, g_spec, lb_spec, gn_spec, act_spec, st_spec


def _hgrn_fwd(proj, lb, gn, d_conv, job=None):
    S = proj.shape[0]
    H = d_conv // HEAD
    T = min(HGRN_FWD_ROWS, S)
    n_t, nb = S // T, T // BLK
    sub = min(HGRN_FWD_SUB, T)
    q_spec, z_spec, v_spec, g_spec, lb_spec, gn_spec, act_spec, st_spec = _hgrn_specs(T, d_conv, n_t, False)

    def body(q_ref, z_ref, v_ref, g_ref, lb_ref, gn_ref, ob_ref, o_ref, st_ref, qe_s, ke_s, dec_s, state, v_s, o_s, u_s):
        @pl.when(pl.program_id(1) == 0)
        def _():
            state[...] = jnp.zeros_like(state)

        pos = lax.broadcasted_iota(jnp.int32, (sub, HEAD), 0) % BLK
        lbv = lb_ref[...]

        def vector_pass(s, carry):
            r = pl.ds(pl.multiple_of(s * sub, sub), sub)
            v = v_ref[r, :]
            _, qs, _, _, kk, b, b_tot = _hgrn_decays(q_ref[r, :], z_ref[r, :], lbv, pos, sub)
            qe_s[r, :] = (qs * jnp.exp(b)).astype(BF16)
            ke_s[r, :] = (kk * jnp.exp(b_tot - b)).astype(BF16)
            v_s[r, :] = v.astype(BF16)
            dec_s[r, :] = jnp.exp(b_tot)
            o = jnp.sum(qs * kk, axis=-1, keepdims=True) * v
            for d in range(1, BLK):
                e = jnp.exp(b - pltpu.roll(b, d, 0))
                a = jnp.sum(qs * pltpu.roll(kk, d, 0) * e, axis=-1, keepdims=True)
                o = o + jnp.where(pos >= d, a * pltpu.roll(v, d, 0), 0.0)
            o_s[r, :] = o
            return carry

        lax.fori_loop(0, T // sub, vector_pass, 0)

        _block_outer_products(v_s, ke_s, u_s, nb)
        st = state[...]
        for j in range(nb):
            st_ref[j] = st.astype(BF16)
            st = st * dec_s[pl.ds(j * BLK, 1), :] + u_s[j]
        state[...] = st
        for j in range(nb):
            rows = pl.ds(j * BLK, BLK)
            o_s[rows, :] += lax.dot_general(qe_s[rows, :], st_ref[j], (((1,), (1,)), ((), ())),
                                            preferred_element_type=F32)
        o = o_s[...]
        o_ref[...] = o
        r = lax.rsqrt(jnp.mean(o * o, axis=-1, keepdims=True) + EPS)
        g = g_ref[...]
        ob_ref[...] = ((o * r) * gn_ref[...] * (g * _sig(g))).astype(BF16)

    grid = (H, n_t)
    in_specs = [q_spec, z_spec, v_spec, g_spec, lb_spec, gn_spec]
    args = [proj, proj, proj, proj, lb, gn]
    out_specs = [act_spec, act_spec, st_spec, act_spec, act_spec, act_spec]
    out_shape = [jax.ShapeDtypeStruct((S, d_conv), BF16), jax.ShapeDtypeStruct((S, d_conv), F32),
                 jax.ShapeDtypeStruct((H, S // BLK, HEAD, HEAD), BF16),
                 jax.ShapeDtypeStruct((S, d_conv), BF16), jax.ShapeDtypeStruct((S, d_conv), BF16),
                 jax.ShapeDtypeStruct((S, d_conv), F32)]
    scratch = [pltpu.VMEM((HEAD, HEAD), F32), pltpu.VMEM((T, HEAD), BF16), pltpu.VMEM((T, HEAD), F32),
               pltpu.VMEM((nb, HEAD, HEAD), F32)]
    aliases, wrap = _host(job, grid, in_specs, args, out_specs, out_shape, scratch)
    return pl.pallas_call(
        wrap(body), name="hgrn_fwd", grid=grid, in_specs=in_specs, out_specs=out_specs, out_shape=out_shape,
        scratch_shapes=scratch, input_output_aliases=aliases,
        compiler_params=_params(("parallel" if job is None else "arbitrary", "arbitrary")),
    )(*args)


def _hgrn_bwd(dob, o_raw, states, qe, ke, dec, proj, lb, gn, d_conv, job=None):
    S = proj.shape[0]
    H = d_conv // HEAD
    T = min(HGRN_BWD_ROWS, S)
    n_t, nb = S // T, T // BLK
    sub = min(HGRN_BWD_SUB, T)
    gate_rows = min(HGRN_GATE_ROWS, T)
    q_spec, z_spec, v_spec, g_spec, lb_spec, gn_spec, act_spec, st_spec = _hgrn_specs(T, d_conv, n_t, True)

    def body(dob_ref, o_ref, st_ref, qe_b, ke_b, dec_s, q_ref, z_ref, v_ref, g_ref, lb_ref, gn_ref,
             dq_ref, dz_ref, dv_ref, dg_ref, dlb_ref, dgn_ref,
             dstate, do_b, v_b, dqe_s, dke_s, dvi_s, ddec_s, do_s, u_s, ds_s):
        @pl.when(pl.program_id(1) == 0)
        def _():
            dstate[...] = jnp.zeros_like(dstate)
            dlb_ref[...] = jnp.zeros_like(dlb_ref)
            dgn_ref[...] = jnp.zeros_like(dgn_ref)

        pos = lax.broadcasted_iota(jnp.int32, (sub, HEAD), 0) % BLK
        lbv, gnv = lb_ref[...], gn_ref[...]

        def before(s, carry):
            r = pl.ds(pl.multiple_of(s * gate_rows, gate_rows), gate_rows)
            g = g_ref[r, :]
            v_b[r, :] = v_ref[r, :].astype(BF16)
            o = o_ref[r, :]
            rs = lax.rsqrt(jnp.mean(o * o, axis=-1, keepdims=True) + EPS)
            on = o * rs
            sgg = _sig(g)
            dobv = dob_ref[r, :]
            dog = dobv * (g * sgg)
            dgn_ref[...] += jnp.sum(dog * on, axis=0, keepdims=True)
            don = dog * gnv
            do = rs * (don - on * jnp.mean(don * on, axis=-1, keepdims=True))
            dg_ref[r, :] = (dobv * (on * gnv) * (sgg * (1.0 + g * (1.0 - sgg)))).astype(BF16)
            do_s[r, :] = do
            do_b[r, :] = do.astype(BF16)
            return carry

        lax.fori_loop(0, T // gate_rows, before, 0)

        _block_outer_products(do_b, qe_b, u_s, nb)
        for j in range(nb):
            rows = pl.ds(j * BLK, BLK)
            dqe_s[rows, :] = lax.dot_general(do_b[rows, :], st_ref[j], (((1,), (0,)), ((), ())),
                                             preferred_element_type=F32)
        dst = dstate[...]
        for j in reversed(range(nb)):
            ds_s[j] = dst.astype(BF16)
            ddec = jnp.sum(st_ref[j].astype(F32) * dst, axis=0, keepdims=True)
            ddec_s[pl.ds(j * BLK, BLK), :] = jnp.broadcast_to(ddec, (BLK, HEAD))
            dst = dst * dec_s[pl.ds(j * BLK, 1), :] + u_s[j]
        dstate[...] = dst
        for j in range(nb):
            rows = pl.ds(j * BLK, BLK)
            dke_s[rows, :] = lax.dot_general(v_b[rows, :], ds_s[j], (((1,), (0,)), ((), ())), preferred_element_type=F32)
            dvi_s[rows, :] = lax.dot_general(ke_b[rows, :], ds_s[j], (((1,), (1,)), ((), ())), preferred_element_type=F32)

        def after(s, carry):
            r = pl.ds(pl.multiple_of(s * sub, sub), sub)
            q, v = q_ref[r, :], v_ref[r, :]
            sgq, qs, sg, f, kk, b, b_tot = _hgrn_decays(q, z_ref[r, :], lbv, pos, sub)
            do = do_s[r, :]
            dqs = dqe_s[r, :] * jnp.exp(b)
            dkk = dke_s[r, :] * jnp.exp(b_tot - b)
            d_tot = jnp.where(pos == BLK - 1, _block_cumsum(dkk * kk, pos) + ddec_s[r, :] * jnp.exp(b_tot), 0.0)
            dv = dvi_s[r, :]
            da = jnp.sum(do * v, axis=-1, keepdims=True)
            dv = dv + jnp.sum(qs * kk, axis=-1, keepdims=True) * do
            dqs = dqs + da * kk
            dkk = dkk + da * qs
            for d in range(1, BLK):
                kd = pltpu.roll(kk, d, 0)
                e = jnp.where(pos >= d, jnp.exp(b - pltpu.roll(b, d, 0)), 0.0)
                a = jnp.sum(qs * kd * e, axis=-1, keepdims=True)
                da = jnp.sum(do * pltpu.roll(v, d, 0), axis=-1, keepdims=True)
                gq = da * e
                dqs = dqs + gq * kd
                dkk = dkk + pltpu.roll(gq * qs, sub - d, 0)
                dv = dv + pltpu.roll(a * do, sub - d, 0)
            db = qs * dqs - kk * dkk + d_tot
            dlf = _block_rev_cumsum(db, pos, sub)
            df = dlf / f - dkk
            dlb_ref[...] += jnp.sum(df * (1.0 - sg), axis=0, keepdims=True)
            dz_ref[r, :] = (df * (1.0 - lbv) * sg * (1.0 - sg)).astype(BF16)
            dq_ref[r, :] = (dqs * (sgq * (1.0 + q * (1.0 - sgq)))).astype(BF16)
            dv_ref[r, :] = dv.astype(BF16)
            return carry

        lax.fori_loop(0, T // sub, after, 0)

    tb = lambda dt: pltpu.VMEM((T, HEAD), dt)
    grid = (H, n_t)
    in_specs = [act_spec, act_spec, st_spec, act_spec, act_spec, act_spec, q_spec, z_spec, v_spec, g_spec, lb_spec,
                gn_spec]
    args = [dob, o_raw, states, qe, ke, dec, proj, proj, proj, proj, lb, gn]
    out_specs = [act_spec, act_spec, act_spec, act_spec, lb_spec, pl.BlockSpec((None, 1, HEAD), lambda h, i: (h, 0, 0))]
    out_shape = ([jax.ShapeDtypeStruct((S, d_conv), BF16)] * 4
                 + [jax.ShapeDtypeStruct((1, d_conv), F32), jax.ShapeDtypeStruct((H, 1, HEAD), F32)])
    scratch = [pltpu.VMEM((HEAD, HEAD), F32), tb(BF16), tb(BF16), tb(F32), tb(F32), tb(F32), tb(F32), tb(F32),
               pltpu.VMEM((nb, HEAD, HEAD), F32), pltpu.VMEM((nb, HEAD, HEAD), BF16)]
    aliases, wrap = _host(job, grid, in_specs, args, out_specs, out_shape, scratch)
    return pl.pallas_call(
        wrap(body), name="hgrn_bwd", grid=grid, in_specs=in_specs, out_specs=out_specs, out_shape=out_shape,
        scratch_shapes=scratch, input_output_aliases=aliases,
        compiler_params=_params(("parallel" if job is None else "arbitrary", "arbitrary")),
    )(*args)


def _local_step(x, target, mod, norm_mix_g, lb, gnorm_g, norm_ffn_g, norm_final_g, conv_w, bufs, c_idx, order):
    S, D = x.shape
    d_conv = D // 2
    d_in = 7 * d_conv + 2 * D
    d_ff = N_CHIPS * bufs["w_ffn_down"].shape[1]
    nb_in, nb_co, nb_ff = bufs["w_in"].shape[2], bufs["w_conv_out"].shape[2], bufs["w_ffn_gate"].shape[2]
    rows = lambda g: g.reshape(-1, g.shape[2])
    sh_m, sc_m, gt_m, sh_f, sc_f, gt_f = [mod[:, i * D:(i + 1) * D] for i in range(6)]
    tm = _pick(S, 512, 16)
    tm2 = _pick(S, 1024, 16)
    tn_in = _pick(nb_in, 1408, 128)
    tn_co = _pick(nb_co, 512, 128)
    tn_ff = _pick(nb_ff, 1408, 128)
    tn_d = _pick(D, 512, 128)
    tw = _pick(D, 1024, 128)
    tg = _pick(D, 512, 128)

    h = _norm_mod("norm_mix", x, norm_mix_g, sc_m, sh_m)
    proj, w_in = _proj_gather(h, bufs["w_in"], order)
    ob, o_raw, states, qe, ke, dec, *got = _hgrn_fwd(
        proj, lb, gnorm_g, d_conv,
        job=_GatherJob([bufs[k] for k in ("w_conv_out", "w_hgrn_out", "w_o", "w_ffn_gate")]))
    ya, w_conv_out, w_hgrn_out, w_o, w_ffn_gate, w_ffn_up = _conv_fwd(
        proj, conv_w, d_conv, job=_Jobs([_ForwardJob(got), _GatherJob([bufs["w_ffn_up"]], 0, 2)]))
    w_o = rows(w_o)

    def merge_epi(accs, ex):
        y_a, y_b = accs
        return [y_a, y_b, _sig(ex[0]) * y_a + _sig(ex[1]) * y_b]

    y_a, y_b, merged, w_ffn_up = _matmul(
        "branch_out", [(ya, 'n', w_conv_out, 'n'), (ob, 'n', w_hgrn_out, 'n')], [0, 1], S, D, d_conv, tm2, tn_co, d_conv,
        [(proj, 'tile', 7 * d_conv), (proj, 'tile', 7 * d_conv + D)], [(BF16, None), (BF16, None), (BF16, None)], merge_epi,
        job=_GatherJob([w_ffn_up], 2, 5), rows_outer=True)

    def resid_epi(accs, ex):
        return [accs[0], ex[0] + ex[1] * accs[0]]

    mo, x1, w_ffn_up = _matmul("w_o", [(merged, 'n', w_o, 'n')], [0], S, D, D, tm2, tw, D,
                               [(x, 'tile', 0), (gt_m, 'row', 0)], [(BF16, None), (F32, None)], resid_epi,
                               job=_GatherJob([w_ffn_up], 5, 8))
    h2, w_ffn_up = _norm_mod("norm_ffn", x1, norm_ffn_g, sc_f, sh_f, job=_ForwardJob([w_ffn_up]))

    def swiglu_epi(accs, ex):
        gg, uu = accs
        return [gg, uu, gg * _sig(gg) * uu]

    G, U, act, w_ffn_down = _matmul(
        "ffn_in", [(h2, 'n', w_ffn_gate, 'n'), (h2, 'n', w_ffn_up, 'n')], [0, 1], S, d_ff, D,
        tm2, tn_ff, D, [], [(BF16, None), (BF16, None), (BF16, None)], swiglu_epi, job=_GatherJob([bufs["w_ffn_down"]]))
    w_ffn_down = rows(_forward_halves("forward_down", [w_ffn_down])[0])
    ff, x2 = _matmul("ffn_out", [(act, 'n', w_ffn_down, 'n')], [0], S, D, d_ff, tm2, tn_d, d_ff,
                     [(x1, 'tile', 0), (gt_f, 'row', 0)], [(BF16, None), (F32, None)], resid_epi, rows_outer=True)

    dx2, dffb, loss, dgt_f, dg_final = _loss_head(x2, target, norm_final_g, ff, gt_f)

    def swiglu_bwd_epi(accs, ex):
        dact, gg, uu = accs[0], ex[0], ex[1]
        s = _sig(gg)
        return [dact * uu * (s * (1.0 + gg * (1.0 - s))), dact * (gg * s)]

    dG, dU = _matmul("d_act", [(dffb, 'n', w_ffn_down, 't')], [0], S, d_ff, D, tm2, tn_ff, D,
                     [(G, 'tile', 0), (U, 'tile', 0)], [(BF16, None), (BF16, None)], swiglu_bwd_epi)
    dw_down, = _matmul("dw_ffn_down", [(act, 't', dffb, 'n')], [0], d_ff, D, S, _pick(d_ff, 512, 128),
                       D, S, [], [(BF16, None)], _plain)
    dh2, = _matmul("d_h2", [(dG, 'n', w_ffn_gate, 't'), (dU, 'n', w_ffn_up, 't')], [0, 0], S, D, d_ff,
                   tm, D, tn_ff, [], [(F32, None)], _plain)
    dw_gate, = _matmul("dw_ffn_gate", [(h2, 't', dG, 'n')], [0], D, d_ff, S, tw, tn_ff, S, [], [(BF16, nb_ff)], _plain)
    dw_up, = _matmul("dw_ffn_up", [(h2, 't', dU, 'n')], [0], D, d_ff, S, tw, tn_ff, S, [], [(BF16, nb_ff)], _plain)

    ck_idx = jnp.stack([c_idx[0], order[0]])

    def pair_sums(names, grads, from_sibling):
        pairs = [_pair_sum("pair_sum_" + k, g, q, ck_idx) for k, g, q in zip(names, grads, from_sibling)]
        return [p[0] for p in pairs], [p[1] for p in pairs]

    ffn_names = ["w_ffn_down", "w_ffn_gate", "w_ffn_up"]
    ffn_grads = [dw_down.reshape(N_CHIPS, -1, D), dw_gate, dw_up]
    dx1, dsh_f, dsc_f, dg_ffn, dmob, dgt_m, *from_sibling = _norm_mod_bwd(
        "norm_ffn_bwd", dh2, x1, norm_ffn_g, sc_f, dx2, (mo, gt_m), job=_SwapJob(ffn_grads))
    parts_f, slots_f = pair_sums(ffn_names, ffn_grads, from_sibling)

    def merge_bwd_epi(accs, ex):
        dm, ga, gb, ya_, yb_ = accs[0], ex[0], ex[1], ex[2], ex[3]
        sa, sb = _sig(ga), _sig(gb)
        return [dm * ya_ * sa * (1.0 - sa), dm * yb_ * sb * (1.0 - sb), dm * sa, dm * sb]

    dga, dgb, dy_a, dy_b = _matmul(
        "d_merged", [(dmob, 'n', w_o, 't')], [0], S, D, D, tm2, tn_co, D,
        [(proj, 'tile', 7 * d_conv), (proj, 'tile', 7 * d_conv + D), (y_a, 'tile', 0), (y_b, 'tile', 0)],
        [(BF16, None)] * 4, merge_bwd_epi, rows_outer=True)
    dw_o, = _matmul("dw_o", [(merged, 't', dmob, 'n')], [0], D, D, S, tg, D, S, [], [(BF16, None)], _plain)
    dya, dob = _matmul("d_branch", [(dy_a, 'n', w_conv_out, 't'), (dy_b, 'n', w_hgrn_out, 't')], [0, 1], S, d_conv, D,
                       tm2, _pick(d_conv, 1024, 128), tn_co, [], [(F32, None), (F32, None)], _plain)
    dw_co, dw_ho = _matmul("dw_branch", [(ya, 't', dy_a, 'n'), (ob, 't', dy_b, 'n')], [0, 1], d_conv, D, S,
                           _pick(d_conv, 512, 128), tn_co, S, [], [(BF16, nb_co), (BF16, nb_co)], _plain)
    branch_names = ["w_conv_out", "w_hgrn_out", "w_o"]
    branch_grads = [dw_co, dw_ho, dw_o.reshape(N_CHIPS, -1, D)]
    dab, dac, dax, dconv_w, *from_sibling = _conv_bwd(dya, proj, conv_w, d_conv, job=_SwapJob(branch_grads))
    parts_b, slots_b = pair_sums(branch_names, branch_grads, from_sibling)
    dq, dz, dv, dgo, dlb, dgn, *arrived_f = _hgrn_bwd(dob, o_raw, states, qe, ke, dec, proj, lb, gnorm_g, d_conv,
                                                      job=_ScatterJob(parts_f, slots_f))
    dproj = jnp.concatenate([dab, dac, dax, dq, dz, dv, dgo, dga, dgb], axis=1)
    halves_f = [_sum_chips("sum_chips_" + k, r, c_idx) for k, r in zip(ffn_names, arrived_f)]
    dw_in, *moved = _matmul("dw_in", [(h, 't', dproj, 'n')], [0], D, d_in, S, tw, tn_in, S, [], [(BF16, nb_in)], _plain,
                            job=_Jobs([_ShareJob(halves_f), _ScatterJob(parts_b, slots_b)]))
    whole_f, arrived_b = moved[:len(ffn_names)], moved[len(ffn_names):]
    parts_i, slots_i = pair_sums(["w_in"], [dw_in], _swap_halves("swap_in", [dw_in]))
    dh, arrived_in = _matmul("d_h", [(dproj, 'n', w_in, 't')], [0], S, D, d_in, tm2, D, tn_in, [], [(F32, None)], _plain,
                             job=_ScatterJob(parts_i, slots_i))
    dx, dsh_m, dsc_m, dg_mix = _norm_mod_bwd("norm_mix_bwd", dh, x, norm_mix_g, sc_m, dx1)
    dmod = jnp.concatenate([dsh_m, dsc_m, dgt_m, dsh_f, dsc_f, dgt_f], axis=1)
    small = dict(dmod=dmod, dg_mix=dg_mix, dg_ffn=dg_ffn, dg_final=dg_final, dlb=dlb,
                 dgn=jnp.sum(dgn, axis=0), dconv_w=dconv_w)
    late_names = branch_names + ["w_in"]
    late = _share_halves([_sum_chips("sum_chips_" + k, r, c_idx) for k, r in zip(late_names, arrived_b + [arrived_in])])
    big = dict(zip(ffn_names + late_names, whole_f + list(late)))
    return loss, dx, small, big


def _adamw_math(w, g, m, v):
    m = ADAM_B1 * m + (1.0 - ADAM_B1) * g
    v = ADAM_B2 * v + (1.0 - ADAM_B2) * (g * g)
    m_hat = m / (1.0 - ADAM_B1 ** ADAM_STEP)
    v_hat = v / (1.0 - ADAM_B2 ** ADAM_STEP)
    delta = -ADAM_LR * (m_hat / (jnp.sqrt(v_hat) + ADAM_EPS) + ADAM_WD * w)
    return delta, m, v


def _adamw(name, w, g, m, v):
    R, C = w.shape
    tr = _pick(R, max(8, (256 * 1024) // C // 8 * 8), 8)
    spec = pl.BlockSpec((tr, C), lambda i: (i, 0))

    def body(w_ref, g_ref, m_ref, v_ref, go_ref, d_ref, mo_ref, vo_ref):
        gv = g_ref[...]
        d, m2, v2 = _adamw_math(w_ref[...], gv, m_ref[...], v_ref[...])
        go_ref[...] = gv
        d_ref[...] = d
        mo_ref[...] = m2
        vo_ref[...] = v2

    return pl.pallas_call(
        body, name=name, grid=(R // tr,), in_specs=[spec] * 4, out_specs=[spec] * 4,
        out_shape=[jax.ShapeDtypeStruct((R, C), F32)] * 4, compiler_params=_params(("parallel",)),
    )(w, g, m, v)


def _ada_update(c_act_t, dmod, w, m, v):
    R, C = w.shape
    B = dmod.shape[0]
    tr = _pick(R, 256, 8)
    tc = _pick(C, 1536, 128)
    spec = pl.BlockSpec((tr, tc), lambda i, j: (i, j))

    def body(ct_ref, dm_ref, w_ref, m_ref, v_ref, g_ref, d_ref, mo_ref, vo_ref):
        ct = ct_ref[...]
        dm = dm_ref[...]
        g = ct[:, 0:1] * dm[0:1, :]
        for b in range(1, B):
            g = g + ct[:, b:b + 1] * dm[b:b + 1, :]
        d, m2, v2 = _adamw_math(w_ref[...], g, m_ref[...], v_ref[...])
        g_ref[...] = g
        d_ref[...] = d
        mo_ref[...] = m2
        vo_ref[...] = v2

    return pl.pallas_call(
        body, name="ada_update", grid=(R // tr, C // tc),
        in_specs=[pl.BlockSpec((tr, B), lambda i, j: (i, 0)), pl.BlockSpec((B, tc), lambda i, j: (0, j)),
                  spec, spec, spec],
        out_specs=[spec] * 4, out_shape=[jax.ShapeDtypeStruct((R, C), F32)] * 4,
        compiler_params=_params(("parallel", "parallel")),
    )(c_act_t, dmod, w, m, v)


def _prep(c_all, lb_param):
    def body(c_ref, p_ref, ca_ref, cab_ref, lb_ref):
        cv = c_ref[...]
        ca = cv * _sig(cv)
        ca_ref[...] = ca
        cab_ref[...] = ca.astype(BF16)
        lb_ref[...] = _sig(p_ref[0:1, :] - p_ref[1:2, :])

    return pl.pallas_call(
        body, name="prep",
        out_shape=[jax.ShapeDtypeStruct(c_all.shape, F32), jax.ShapeDtypeStruct(c_all.shape, BF16),
                   jax.ShapeDtypeStruct((1, lb_param.shape[1]), F32)],
    )(c_all, lb_param)


def _small_reduce(parts):
    W = parts.shape[1]

    def body(p_ref, o_ref):
        acc = p_ref[0:1, :]
        for d in range(1, N_DEV):
            acc = acc + p_ref[d:d + 1, :]
        o_ref[...] = acc

    return pl.pallas_call(body, name="small_reduce", out_shape=jax.ShapeDtypeStruct((1, W), F32))(parts)


def _lb_grad(dlb, lb):
    def body(d_ref, lb_ref, o_ref):
        t = d_ref[...] * lb_ref[...] * (1.0 - lb_ref[...])
        o_ref[0:1, :] = t
        o_ref[1:2, :] = -t

    return pl.pallas_call(body, name="lb_grad", out_shape=jax.ShapeDtypeStruct((2, dlb.shape[1]), F32))(dlb, lb)


def _place():
    x, y, c = lax.axis_index("x"), lax.axis_index("y"), lax.axis_index("c")
    chips = [(1 - x, y), (x, 1 - y), (1 - x, 1 - y)]
    return x, y, c, chips


def _allgather_rows(name, v):
    m_per, n = v.shape

    def body(x_ref, out_ref, send_sems, recv_sems, local_sem):
        x, y, c, chips = _place()
        me, sibling = (x, y, c), (x, y, 1 - c)

        def rows(px, py, pc):
            return out_ref.at[pl.ds((4 * px + 2 * py + pc) * m_per, m_per), :]

        def copy(k, block, to, src=None):
            return pltpu.make_async_remote_copy(
                src_ref=rows(*block) if src is None else src, dst_ref=rows(*block),
                send_sem=send_sems.at[k], recv_sem=recv_sems.at[k], device_id=to, device_id_type=MESH)

        mine = pltpu.make_async_copy(x_ref, rows(*me), local_sem)
        mine.start()
        first = [copy(0, me, sibling, src=x_ref)]
        first += [copy(1 + j, me, (*chip, c), src=x_ref) for j, chip in enumerate(chips)]
        for cp in first:
            cp.start()
        passed = [copy(4 + j, (*chip, c), sibling) for j, chip in enumerate(chips)]
        for j, chip in enumerate(chips):
            copy(1 + j, (*chip, c), me).wait_recv()
            passed[j].start()
        copy(0, sibling, me).wait_recv()
        for j, chip in enumerate(chips):
            copy(4 + j, (*chip, 1 - c), me).wait_recv()
        for cp in first + passed:
            cp.wait_send()
        mine.wait()

    return pl.pallas_call(
        body, name=name, out_shape=jax.ShapeDtypeStruct((N_DEV * m_per, n), v.dtype),
        in_specs=[pl.BlockSpec(memory_space=pltpu.VMEM)], out_specs=pl.BlockSpec(memory_space=pltpu.VMEM),
        scratch_shapes=[pltpu.SemaphoreType.DMA((7,)), pltpu.SemaphoreType.DMA((7,)), pltpu.SemaphoreType.DMA],
    )(v)


def _cast_place(name, w, k_idx):
    R, C = w.shape
    tr = _pick(R, max(16, (512 * 1024) // C // 16 * 16), 16)

    def body(k_ref, w_ref, o_ref):
        o_ref[...] = w_ref[...].astype(BF16)

    return pl.pallas_call(
        body, name=name,
        grid_spec=pltpu.PrefetchScalarGridSpec(
            num_scalar_prefetch=1, grid=(R // tr,),
            in_specs=[pl.BlockSpec((tr, C), lambda i, k_ref: (i, 0))],
            out_specs=pl.BlockSpec((None, tr, C), lambda i, k_ref: (k_ref[0], i, 0))),
        out_shape=jax.ShapeDtypeStruct((N_CHIPS, R, C), BF16),
        compiler_params=_params(("parallel",)),
    )(k_idx, w)


def _chip_copies(src_of, dst_of, got_of, n, sems, receiving):
    x, y, c, chips = _place()
    k_me = 2 * x + y
    send_sems, recv_sems = sems
    out = []
    for a in range(n):
        for j, chip in enumerate(chips):
            k_chip = 2 * chip[0] + chip[1]
            if receiving:
                src = dst = got_of(a, k_chip, c)
                to = (x, y, c)
            else:
                src, dst, to = src_of(a, k_chip, k_me, c), dst_of(a, k_me, c), (*chip, c)
            out.append(pltpu.make_async_remote_copy(
                src_ref=src, dst_ref=dst, send_sem=send_sems.at[3 * a + j], recv_sem=recv_sems.at[3 * a + j],
                device_id=to, device_id_type=MESH))
    return out


class _ChipJob:
    def start(self, j_in, j_out, sems):
        for send in self.copies(j_in, j_out, sems, False):
            send.start()

    def finish(self, j_in, j_out, sems):
        for recv in self.copies(j_in, j_out, sems, True):
            recv.wait_recv()
        for send in self.copies(j_in, j_out, sems, False):
            send.wait_send()


class _GatherJob(_ChipJob):
    def __init__(self, bufs, lo=0, hi=8):
        n = len(bufs)
        self.inputs, self.n_alias = list(bufs), n
        self.sem_shapes = [pltpu.SemaphoreType.DMA((3 * n,)), pltpu.SemaphoreType.DMA((3 * n,))]
        self.half = [b.shape[1] // 2 for b in bufs]
        self.lo, self.hi = lo, hi

    def copies(self, j_in, j_out, sems, receiving):
        def half(a, k, c):
            eighth = self.half[a] // 8
            return j_out[a].at[k, pl.ds(c * self.half[a] + self.lo * eighth, (self.hi - self.lo) * eighth)]

        return _chip_copies(lambda a, k_chip, k_me, c: half(a, k_me, c), half, half, len(self.half), sems, receiving)


class _ScatterJob(_ChipJob):
    def __init__(self, parts, slots):
        n = len(parts)
        self.n = n
        self.inputs, self.n_alias = list(parts) + list(slots), n
        self.sem_shapes = [pltpu.SemaphoreType.DMA((3 * n,)), pltpu.SemaphoreType.DMA((3 * n,))]

    def copies(self, j_in, j_out, sems, receiving):
        return _chip_copies(lambda a, k_chip, k_me, c: j_in[a].at[k_chip], lambda a, k_me, c: j_out[a].at[k_me],
                            lambda a, k_chip, c: j_out[a].at[k_chip], self.n, sems, receiving)


class _SwapJob(_ChipJob):
    def __init__(self, grads):
        n = len(grads)
        self.n = n
        self.half = [g.shape[1] // 2 for g in grads]
        landing = [lax.empty((N_CHIPS, g.shape[1] // 2, g.shape[2]), g.dtype) for g in grads]
        self.inputs, self.n_alias = list(grads) + landing, n
        self.sem_shapes = [pltpu.SemaphoreType.DMA((n,)), pltpu.SemaphoreType.DMA((n,))]

    def copies(self, j_in, j_out, sems, receiving):
        x, y, c, _ = _place()
        out = []
        for a in range(self.n):
            if receiving:
                src, to = j_out[a], (x, y, c)
            else:
                src, to = j_in[a].at[:, pl.ds((1 - c) * self.half[a], self.half[a])], (x, y, 1 - c)
            out.append(pltpu.make_async_remote_copy(src_ref=src, dst_ref=j_out[a], send_sem=sems[0].at[a],
                                                    recv_sem=sems[1].at[a], device_id=to, device_id_type=MESH))
        return out


class _Jobs:
    def __init__(self, jobs):
        self.jobs = jobs
        split = [(j.inputs[:len(j.inputs) - j.n_alias], j.inputs[len(j.inputs) - j.n_alias:]) for j in jobs]
        self.inputs = [a for plain, _ in split for a in plain] + [a for _, aliased in split for a in aliased]
        self.n_alias = sum(j.n_alias for j in jobs)
        self.sem_shapes = [s for j in jobs for s in j.sem_shapes]

    def _each(self, j_in, j_out, sems):
        n_plain = len(j_in) - self.n_alias
        p0 = a0 = s0 = 0
        for j in self.jobs:
            n_p, n_s = len(j.inputs) - j.n_alias, len(j.sem_shapes)
            ins = list(j_in[p0:p0 + n_p]) + list(j_in[n_plain + a0:n_plain + a0 + j.n_alias])
            yield j, ins, j_out[a0:a0 + j.n_alias], sems[s0:s0 + n_s]
            p0, a0, s0 = p0 + n_p, a0 + j.n_alias, s0 + n_s

    def start(self, j_in, j_out, sems):
        for j, ins, outs, s in self._each(j_in, j_out, sems):
            j.start(ins, outs, s)

    def finish(self, j_in, j_out, sems):
        for j, ins, outs, s in self._each(j_in, j_out, sems):
            j.finish(ins, outs, s)


class _ShareJob(_ChipJob):
    def __init__(self, bufs):
        n = len(bufs)
        self.n = n
        self.inputs, self.n_alias = list(bufs), n
        self.sem_shapes = [pltpu.SemaphoreType.DMA((n,)), pltpu.SemaphoreType.DMA((n,))]

    def copies(self, j_in, j_out, sems, receiving):
        x, y, c, _ = _place()
        out = []
        for a in range(self.n):
            hc, to = (1 - c, (x, y, c)) if receiving else (c, (x, y, 1 - c))
            out.append(pltpu.make_async_remote_copy(
                src_ref=j_out[a].at[hc], dst_ref=j_out[a].at[hc], send_sem=sems[0].at[a], recv_sem=sems[1].at[a],
                device_id=to, device_id_type=MESH))
        return out


class _ForwardJob(_ChipJob):
    def __init__(self, bufs):
        n = len(bufs)
        self.n = n
        self.half = [b.shape[1] // 2 for b in bufs]
        self.inputs, self.n_alias = list(bufs), n
        self.sem_shapes = [pltpu.SemaphoreType.DMA((3 * n,)), pltpu.SemaphoreType.DMA((3 * n,))]

    def copies(self, j_in, j_out, sems, receiving):
        x, y, c, chips = _place()
        out = []
        for a in range(self.n):
            for q, chip in enumerate(chips):
                hc, to = (1 - c, (x, y, c)) if receiving else (c, (x, y, 1 - c))
                part = j_out[a].at[2 * chip[0] + chip[1], pl.ds(hc * self.half[a], self.half[a])]
                out.append(pltpu.make_async_remote_copy(
                    src_ref=part, dst_ref=part, send_sem=sems[0].at[3 * a + q], recv_sem=sems[1].at[3 * a + q],
                    device_id=to, device_id_type=MESH))
        return out


def _forward_halves(name, bufs):
    n = len(bufs)

    def body(*refs):
        out_refs = refs[n:2 * n]
        send_sems, recv_sems = refs[2 * n:]
        x, y, c, chips = _place()
        started, waits = [], []
        for a in range(n):
            rh = bufs[a].shape[1] // 2
            for j, chip in enumerate(chips):
                k_chip = 2 * chip[0] + chip[1]
                got = out_refs[a].at[k_chip, pl.ds(c * rh, rh)]
                cp = pltpu.make_async_remote_copy(src_ref=got, dst_ref=got, send_sem=send_sems.at[3 * a + j],
                                                  recv_sem=recv_sems.at[3 * a + j], device_id=(x, y, 1 - c),
                                                  device_id_type=MESH)
                cp.start()
                started.append(cp)
                other = out_refs[a].at[k_chip, pl.ds((1 - c) * rh, rh)]
                waits.append(pltpu.make_async_remote_copy(
                    src_ref=other, dst_ref=other, send_sem=send_sems.at[3 * a + j], recv_sem=recv_sems.at[3 * a + j],
                    device_id=(x, y, c), device_id_type=MESH))
        for cp in waits:
            cp.wait_recv()
        for cp in started:
            cp.wait_send()

    return pl.pallas_call(
        body, name=name, out_shape=[jax.ShapeDtypeStruct(b.shape, b.dtype) for b in bufs],
        in_specs=[_HBM] * n, out_specs=[_HBM] * n, input_output_aliases={a: a for a in range(n)},
        scratch_shapes=[pltpu.SemaphoreType.DMA((3 * n,)), pltpu.SemaphoreType.DMA((3 * n,))],
    )(*bufs)


def _proj_gather(h, buf, order):
    S, D = h.shape
    nb = buf.shape[2]
    tm = _pick(S, 1024, 16)
    tn = _pick(nb, 1408, 128)
    per = nb // tn
    nj, ni = N_CHIPS * per, S // tm
    rh = D // 2
    seq = [(0, t) for t in range(per)] + [(p, t) for t in range(per) for p in (1, 2)] + [(3, t) for t in range(per)]
    tile_chip = jnp.stack([order[p] for p, _ in seq])
    tile_of = jnp.array([t for _, t in seq], jnp.int32)

    def body(chip_ref, t_ref, h_ref, buf_in, proj_ref, buf_ref, wbuf, tile_sems, ici_send, ici_recv, d2d_send, d2d_recv):
        j, i = pl.program_id(0), pl.program_id(1)
        x, y, c, chips = _place()
        k_me = 2 * x + y

        def part(k, hc, t):
            return buf_ref.at[k, pl.ds(hc * rh, rh), pl.ds(t * tn, tn)]

        def ici(q, t, receiving):
            k_chip = 2 * chips[q][0] + chips[q][1]
            src, to = (part(k_chip, c, t), (x, y, c)) if receiving else (part(k_me, c, t), (*chips[q], c))
            return pltpu.make_async_remote_copy(src_ref=src, dst_ref=src, send_sem=ici_send.at[q * per + t],
                                                recv_sem=ici_recv.at[q * per + t], device_id=to, device_id_type=MESH)

        def d2d(q, t, receiving):
            k_chip = 2 * chips[q][0] + chips[q][1]
            src, to = (part(k_chip, 1 - c, t), (x, y, c)) if receiving else (part(k_chip, c, t), (x, y, 1 - c))
            return pltpu.make_async_remote_copy(src_ref=src, dst_ref=src, send_sem=d2d_send.at[q * per + t],
                                                recv_sem=d2d_recv.at[q * per + t], device_id=to, device_id_type=MESH)

        def tile_copy(n):
            col = pl.multiple_of(t_ref[n] * tn, 128)
            return pltpu.make_async_copy(buf_ref.at[chip_ref[n], :, pl.ds(col, tn)], wbuf.at[n % 2], tile_sems.at[n % 2])

        @pl.when(jnp.logical_and(j == 0, i == 0))
        def _():
            for t in range(per):
                ici(0, t, False).start()
                ici(1, t, False).start()
            tile_copy(0).start()

        @pl.when(i == 0)
        def _():
            tile_copy(j).wait()
            for n in range(per, nj):
                p, t = seq[n]

                @pl.when(j + 1 == n)
                def _():
                    ici(p - 1, t, True).wait_recv()
                    d2d(p - 1, t, False).start()
                    if (p, t) == (1, per - 1):
                        for q in range(2):
                            for tt in range(per):
                                ici(q, tt, False).wait_send()
                        for tt in range(per):
                            ici(2, tt, False).start()
                    d2d(p - 1, t, True).wait_recv()

            @pl.when(j + 1 < nj)
            def _():
                tile_copy(j + 1).start()

        proj_ref[...] = jnp.dot(h_ref[...], wbuf[j % 2], preferred_element_type=F32)

        @pl.when(jnp.logical_and(j == nj - 1, i == ni - 1))
        def _():
            for t in range(per):
                ici(2, t, False).wait_send()
                for q in range(3):
                    d2d(q, t, False).wait_send()

    n_sems = 3 * per
    return pl.pallas_call(
        body, name="proj",
        grid_spec=pltpu.PrefetchScalarGridSpec(
            num_scalar_prefetch=2, grid=(nj, ni),
            in_specs=[pl.BlockSpec((tm, D), lambda j, i, kc, kt: (i, 0)), _HBM],
            out_specs=[pl.BlockSpec((tm, tn), lambda j, i, kc, kt: (i, kc[j] * per + kt[j])), _HBM],
            scratch_shapes=[pltpu.VMEM((2, D, tn), BF16), pltpu.SemaphoreType.DMA((2,))]
            + [pltpu.SemaphoreType.DMA((n_sems,))] * 4),
        out_shape=[jax.ShapeDtypeStruct((S, N_CHIPS * nb), F32), jax.ShapeDtypeStruct(buf.shape, buf.dtype)],
        input_output_aliases={3: 1}, compiler_params=_params(("arbitrary", "arbitrary")),
    )(tile_chip, tile_of, h, buf)


def _swap_halves(name, grads):
    n = len(grads)

    def body(*refs):
        in_refs, out_refs = refs[:n], refs[n:2 * n]
        send_sems, recv_sems = refs[2 * n:]
        x, y, c, _ = _place()
        cps = []
        for a in range(n):
            rh = grads[a].shape[1] // 2
            cp = pltpu.make_async_remote_copy(
                src_ref=in_refs[a].at[:, pl.ds((1 - c) * rh, rh)], dst_ref=out_refs[a],
                send_sem=send_sems.at[a], recv_sem=recv_sems.at[a], device_id=(x, y, 1 - c), device_id_type=MESH)
            cp.start()
            cps.append(cp)
        for cp in cps:
            cp.wait()

    return pl.pallas_call(
        body, name=name,
        out_shape=[jax.ShapeDtypeStruct((N_CHIPS, g.shape[1] // 2, g.shape[2]), g.dtype) for g in grads],
        in_specs=[_HBM] * n, out_specs=[_HBM] * n,
        scratch_shapes=[pltpu.SemaphoreType.DMA((n,)), pltpu.SemaphoreType.DMA((n,))],
    )(*grads)


def _pair_sum(name, g, q, ck_idx):
    _, R, C = g.shape
    rh = R // 2
    tr = _pick(rh, max(16, (512 * 1024) // C // 16 * 16), 16)
    nh = rh // tr

    def body(ck_ref, g_ref, q_ref, o_ref, own_ref):
        s = (g_ref[...].astype(F32) + q_ref[...].astype(F32)).astype(BF16)
        o_ref[...] = s

        @pl.when(pl.program_id(1) == ck_ref[1])
        def _():
            own_ref[...] = s

    return pl.pallas_call(
        body, name=name,
        grid_spec=pltpu.PrefetchScalarGridSpec(
            num_scalar_prefetch=1, grid=(nh, N_CHIPS),
            in_specs=[pl.BlockSpec((None, tr, C), lambda i, k, ck: (k, ck[0] * nh + i, 0)),
                      pl.BlockSpec((None, tr, C), lambda i, k, ck: (k, i, 0))],
            out_specs=[pl.BlockSpec((None, tr, C), lambda i, k, ck: (k, i, 0)),
                       pl.BlockSpec((None, tr, C), lambda i, k, ck: (ck[1], i, 0))]),
        out_shape=[jax.ShapeDtypeStruct((N_CHIPS, rh, C), BF16)] * 2,
        compiler_params=_params(("parallel", "arbitrary")),
    )(ck_idx, g, q)


def _sum_chips(name, r, c_idx):
    _, rh, C = r.shape
    tr = _pick(rh, max(16, (256 * 1024) // C // 16 * 16), 16)

    def body(c_ref, r_ref, o_ref):
        acc = r_ref[0].astype(F32)
        for k in range(1, N_CHIPS):
            acc = acc + r_ref[k].astype(F32)
        o_ref[...] = acc

    return pl.pallas_call(
        body, name=name,
        grid_spec=pltpu.PrefetchScalarGridSpec(
            num_scalar_prefetch=1, grid=(rh // tr,),
            in_specs=[pl.BlockSpec((N_CHIPS, tr, C), lambda i, c_ref: (0, i, 0))],
            out_specs=pl.BlockSpec((None, tr, C), lambda i, c_ref: (c_ref[0], i, 0))),
        out_shape=jax.ShapeDtypeStruct((2, rh, C), F32), compiler_params=_params(("parallel",)),
    )(c_idx, r)


def _share_halves(bufs):
    n = len(bufs)

    def body(*refs):
        out_refs = refs[n:2 * n]
        send_sems, recv_sems = refs[2 * n:]
        x, y, c, _ = _place()
        cps = []
        for a in range(n):
            cp = pltpu.make_async_remote_copy(
                src_ref=out_refs[a].at[c], dst_ref=out_refs[a].at[c], send_sem=send_sems.at[a],
                recv_sem=recv_sems.at[a], device_id=(x, y, 1 - c), device_id_type=MESH)
            cp.start()
            cps.append(cp)
        for a, cp in enumerate(cps):
            got = out_refs[a].at[1 - c]
            pltpu.make_async_remote_copy(src_ref=got, dst_ref=got, send_sem=send_sems.at[a], recv_sem=recv_sems.at[a],
                                         device_id=(x, y, c), device_id_type=MESH).wait_recv()
        for cp in cps:
            cp.wait_send()

    return pl.pallas_call(
        body, name="share_halves",
        out_shape=[jax.ShapeDtypeStruct(b.shape, b.dtype) for b in bufs],
        in_specs=[_HBM] * n, out_specs=[_HBM] * n, input_output_aliases={a: a for a in range(n)},
        scratch_shapes=[pltpu.SemaphoreType.DMA((n,)), pltpu.SemaphoreType.DMA((n,))],
    )(*bufs)


_BIG = ("w_in", "w_conv_out", "w_hgrn_out", "w_o", "w_ffn_gate", "w_ffn_up", "w_ffn_down")
_WEIGHTS = ("w_ada", "b_ada", "norm_mix_g", "w_in", "conv_w", "lb_param", "gnorm_g", "w_conv_out", "w_hgrn_out",
            "w_o", "norm_ffn_g", "w_ffn_gate", "w_ffn_up", "w_ffn_down", "norm_final_g")


def _pack_rows(pieces):
    flat = jnp.concatenate([p.reshape(-1) for p in pieces])
    pad = (-flat.shape[0]) % (SUBLANES * LANES)
    return jnp.pad(flat, (0, pad)).reshape(SUBLANES, -1)


def kernel(x, c, w_ada, b_ada, norm_mix_g, w_in, conv_w, lb_param, gnorm_g, w_conv_out, w_hgrn_out, w_o, norm_ffn_g, w_ffn_gate, w_ffn_up, w_ffn_down, norm_final_g, loss_target, m_w_ada, m_b_ada, m_norm_mix_g, m_w_in, m_conv_w, m_lb_param, m_gnorm_g, m_w_conv_out, m_w_hgrn_out, m_w_o, m_norm_ffn_g, m_w_ffn_gate, m_w_ffn_up, m_w_ffn_down, m_norm_final_g, v_w_ada, v_b_ada, v_norm_mix_g, v_w_in, v_conv_w, v_lb_param, v_gnorm_g, v_w_conv_out, v_w_hgrn_out, v_w_o, v_norm_ffn_g, v_w_ffn_gate, v_w_ffn_up, v_w_ffn_down, v_norm_final_g):
    w = dict(w_ada=w_ada, b_ada=b_ada, norm_mix_g=norm_mix_g, w_in=w_in, conv_w=conv_w, lb_param=lb_param,
             gnorm_g=gnorm_g, w_conv_out=w_conv_out, w_hgrn_out=w_hgrn_out, w_o=w_o, norm_ffn_g=norm_ffn_g,
             w_ffn_gate=w_ffn_gate, w_ffn_up=w_ffn_up, w_ffn_down=w_ffn_down, norm_final_g=norm_final_g)
    m = dict(w_ada=m_w_ada, b_ada=m_b_ada, norm_mix_g=m_norm_mix_g, w_in=m_w_in, conv_w=m_conv_w, lb_param=m_lb_param,
             gnorm_g=m_gnorm_g, w_conv_out=m_w_conv_out, w_hgrn_out=m_w_hgrn_out, w_o=m_w_o, norm_ffn_g=m_norm_ffn_g,
             w_ffn_gate=m_w_ffn_gate, w_ffn_up=m_w_ffn_up, w_ffn_down=m_w_ffn_down, norm_final_g=m_norm_final_g)
    v = dict(w_ada=v_w_ada, b_ada=v_b_ada, norm_mix_g=v_norm_mix_g, w_in=v_w_in, conv_w=v_conv_w, lb_param=v_lb_param,
             gnorm_g=v_gnorm_g, w_conv_out=v_w_conv_out, w_hgrn_out=v_w_hgrn_out, w_o=v_w_o, norm_ffn_g=v_norm_ffn_g,
             w_ffn_gate=v_w_ffn_gate, w_ffn_up=v_w_ffn_up, w_ffn_down=v_w_ffn_down, norm_final_g=v_norm_final_g)
    two_d = lambda a: a.reshape((-1, a.shape[-1])) if a.ndim != 2 else a
    shapes = {k: a.shape for k, a in w.items()}
    w, m, v = ({k: two_d(a) for k, a in t.items()} for t in (w, m, v))
    w["lb_param"], m["lb_param"], v["lb_param"] = lb_param, m_lb_param, v_lb_param
    S, D = x.shape[1], x.shape[2]
    d_conv = D // 2
    ix, iy, ic = lax.axis_index("x"), lax.axis_index("y"), lax.axis_index("c")
    k_me = 2 * ix + iy
    dev = 2 * k_me + ic
    cw_shard = w["conv_w"].shape[1]
    ada_cols = w["w_ada"].shape[1]

    got = _allgather_rows("gather_cond", _pack_rows([c, w["conv_w"]])).reshape(N_DEV, -1)
    c_all = got[:, :D]
    conv_full = got[::2, D:D + 3 * cw_shard].reshape(N_CHIPS, 3, cw_shard).transpose(1, 0, 2).reshape(3, -1)
    c_act, c_act_b, lb = _prep(c_all, lb_param)
    b_cols = lax.dynamic_slice(w["b_ada"], (0, k_me * ada_cols), (1, ada_cols))
    mod_cols, = _matmul("ada_fwd", [(c_act_b, 'n', w["w_ada"].astype(BF16), 'n')], [0], N_DEV, ada_cols, D,
                        N_DEV, _pick(ada_cols, 1536, 128), D, [(b_cols, 'row', 0)], [(F32, None)],
                        lambda accs, ex: [accs[0] + ex[0]])
    mod_all = _allgather_rows("gather_mod", mod_cols).reshape(N_CHIPS, 2, N_DEV, ada_cols)[:, 0]
    mod_all = mod_all.transpose(1, 0, 2).reshape(N_DEV, -1)
    mod = lax.dynamic_slice(mod_all, (dev, 0), (1, mod_all.shape[1]))
    k_idx = jnp.reshape(k_me, (1,)).astype(jnp.int32)
    c_idx = jnp.reshape(ic, (1,)).astype(jnp.int32)
    bufs = {k: _cast_place("cast_" + k, w[k], k_idx) for k in _BIG}
    order = jnp.stack([k_me, 2 * (1 - ix) + iy, 2 * ix + (1 - iy), 2 * (1 - ix) + (1 - iy)]).astype(jnp.int32)

    loss, dx, small, arrived = _local_step(
        x[0], loss_target[0], mod, w["norm_mix_g"], lb, w["gnorm_g"], w["norm_ffn_g"], w["norm_final_g"], conv_full,
        bufs, c_idx, order)

    pieces = [small["dmod"], small["dg_mix"], small["dg_ffn"], small["dg_final"], small["dlb"], small["dgn"],
              small["dconv_w"], loss]
    sizes = [p.size for p in pieces]
    parts = _allgather_rows("gather_small", _pack_rows(pieces)).reshape(N_DEV, -1)
    total = _small_reduce(parts)
    offs = [0]
    for s in sizes:
        offs.append(offs[-1] + s)
    tot = [total[:, offs[i]:offs[i + 1]] for i in range(len(sizes))]
    dmod_all = parts[:, :sizes[0]]
    grads = {
        "b_ada": tot[0], "norm_mix_g": tot[1], "norm_ffn_g": tot[2], "norm_final_g": tot[3],
        "lb_param": _lb_grad(tot[4], lb), "gnorm_g": tot[5],
        "conv_w": lax.dynamic_slice(tot[6].reshape(3, -1), (0, k_me * cw_shard), (3, cw_shard)),
    }
    loss_out = tot[7][0, 0]

    for k in _BIG:
        grads[k] = arrived[k].reshape(-1, arrived[k].shape[-1])

    delta, new_m, new_v = {}, {}, {}
    dmod_cols = lax.dynamic_slice(dmod_all, (0, k_me * ada_cols), (N_DEV, ada_cols))
    grads["w_ada"], delta["w_ada"], new_m["w_ada"], new_v["w_ada"] = _ada_update(
        c_act.T, dmod_cols, w["w_ada"], m["w_ada"], v["w_ada"])
    for k in _WEIGHTS:
        if k != "w_ada":
            grads[k], delta[k], new_m[k], new_v[k] = _adamw("adamw_" + k, w[k], grads[k], m[k], v[k])
    outs = [loss_out, dx.reshape(x.shape)]
    for t in (grads, delta, new_m, new_v):
        outs += [t[k].reshape(shapes[k]) for k in _WEIGHTS]
    return tuple(outs)
```

```python
import functools

import jax
import jax.numpy as jnp
from jax import lax
from jax.experimental import pallas as pl
from jax.experimental.pallas import tpu as pltpu

F32 = jnp.float32
BF16 = jnp.bfloat16
MESH = pl.DeviceIdType.MESH

EPS = 1e-6
HEAD = 128
BLK = 16
N_CHIPS = 4
N_DEV = 8
SUBLANES, LANES = 8, 128
VMEM_LIMIT_BYTES = 56 * 1024 * 1024

ADAM_LR = 0.001
ADAM_B1 = 0.9
ADAM_B2 = 0.999
ADAM_EPS = 1e-08
ADAM_WD = 0.01
ADAM_STEP = 10


def _pick(dim, pref, mult):
    if dim <= pref:
        return dim
    t = (pref // mult) * mult
    while t >= mult:
        if dim % t == 0:
            return t
        t -= mult
    return dim


def _sig(x):
    return 1.0 / (1.0 + jnp.exp(-x))


def _params(sem):
    return pltpu.CompilerParams(dimension_semantics=sem, vmem_limit_bytes=VMEM_LIMIT_BYTES)


def _gj(j, i, k):
    return j


def _gk(j, i, k):
    return k


def _wspec(arr, rt, ct, r_of, c_of):
    if arr.ndim == 2:
        return pl.BlockSpec((rt, ct), lambda j, i, k: (r_of(j, i, k), c_of(j, i, k)))
    per = arr.shape[2] // ct
    assert arr.shape[2] % ct == 0
    return pl.BlockSpec((None, rt, ct),
                        lambda j, i, k: (c_of(j, i, k) // per, r_of(j, i, k), c_of(j, i, k) % per))


_HBM = pl.BlockSpec(memory_space=pltpu.HBM)


def _host(job, grid, in_specs, args, out_specs, out_shape, scratch):
    if job is None:
        return {}, (lambda body: body)
    n_in, n_out, n_scr = len(args), len(out_shape), len(scratch)
    n_job, n_alias = len(job.inputs), job.n_alias
    in_specs += [_HBM] * n_job
    args += job.inputs
    out_specs += [_HBM] * n_alias
    out_shape += [jax.ShapeDtypeStruct(a.shape, a.dtype) for a in job.inputs[n_job - n_alias:]]
    scratch += job.sem_shapes
    aliases = {n_in + n_job - n_alias + t: n_out + t for t in range(n_alias)}

    def wrap(body):
        def hosted(*refs):
            ins, refs = refs[:n_in], refs[n_in:]
            j_in, refs = refs[:n_job], refs[n_job:]
            outs, refs = refs[:n_out], refs[n_out:]
            j_out, refs = refs[:n_alias], refs[n_alias:]
            scr, sems = refs[:n_scr], refs[n_scr:]
            first = functools.reduce(jnp.logical_and, [pl.program_id(d) == 0 for d in range(len(grid))])
            last = functools.reduce(jnp.logical_and, [pl.program_id(d) == grid[d] - 1 for d in range(len(grid))])

            @pl.when(first)
            def _():
                job.start(j_in, j_out, sems)

            body(*ins, *outs, *scr)

            @pl.when(last)
            def _():
                job.finish(j_in, j_out, sems)

        return hosted

    return aliases, wrap


EPILOGUE_COLS = 768


def _plain(accs, extras):
    return accs


def _matmul(name, pairs, acc_of, M, N, K, tm, tn, tk, extras, outs, epilogue, job=None, rows_outer=False):
    assert M % tm == 0 and N % tn == 0 and K % tk == 0, (name, M, N, K, tm, tn, tk)
    nj, ni, nk = N // tn, M // tm, K // tk
    n_pairs, n_extra, n_out = len(pairs), len(extras), len(outs)
    n_acc = max(acc_of) + 1
    in_specs, args, dims = [], [], []
    lhs_of, seen = [], {}
    for a, am, b, bm in pairs:
        if (id(a), am) not in seen:
            seen[id(a), am] = len(args)
            args.append(a)
            if am == 'n':
                in_specs.append(pl.BlockSpec((tm, tk), lambda j, i, k: (i, k)))
            else:
                in_specs.append(pl.BlockSpec((tk, tm), lambda j, i, k: (k, i)))
        lhs_of.append(seen[id(a), am])
    n_lhs = len(args)
    for a, am, b, bm in pairs:
        if bm == 'n':
            in_specs.append(_wspec(b, tk, tn, _gk, _gj))
        else:
            in_specs.append(_wspec(b, tn, tk, _gj, _gk))
        dims.append((((1 if am == 'n' else 0,), (0 if bm == 'n' else 1,)), ((), ())))
        args.append(b)
    for e, kind, off in extras:
        assert off % tn == 0, (name, off, tn)
        o = off // tn
        if kind == 'tile':
            in_specs.append(pl.BlockSpec((tm, tn), lambda j, i, k, o=o: (i, j + o)))
        else:
            in_specs.append(pl.BlockSpec((1, tn), lambda j, i, k, o=o: (0, j + o)))
        args.append(e)
    out_shape, out_specs = [], []
    for dt, nb in outs:
        if nb is None:
            out_shape.append(jax.ShapeDtypeStruct((M, N), dt))
            out_specs.append(pl.BlockSpec((tm, tn), lambda j, i, k: (i, j)))
        else:
            assert nb % tn == 0 and N % nb == 0
            per = nb // tn
            out_shape.append(jax.ShapeDtypeStruct((N // nb, M, nb), dt))
            out_specs.append(pl.BlockSpec((None, tm, tn), lambda j, i, k, per=per: (j // per, i, j % per)))

    def body(*refs):
        n_ab = n_lhs + n_pairs
        e_refs = refs[n_ab:n_ab + n_extra]
        o_refs = refs[n_ab + n_extra:n_ab + n_extra + n_out]
        acc_refs = refs[n_ab + n_extra + n_out:]

        def products(cols):
            sums = [None] * n_acc
            for p in range(n_pairs):
                b_ref = refs[n_lhs + p]
                b = b_ref[:, cols] if pairs[p][3] == 'n' else b_ref[cols, :]
                prod = lax.dot_general(refs[lhs_of[p]][...], b, dims[p], preferred_element_type=F32)
                a = acc_of[p]
                sums[a] = prod if sums[a] is None else sums[a] + prod
            return sums

        def finish(accs, cols):
            res = epilogue(accs, [e[:, cols] for e in e_refs])
            for o_ref, r in zip(o_refs, res):
                o_ref[:, cols] = r.astype(o_ref.dtype)

        whole = slice(0, tn)
        if nk == 1:
            step = tn if epilogue is _plain else EPILOGUE_COLS
            for c0 in range(0, tn, step):
                cols = slice(c0, min(c0 + step, tn))
                finish(products(cols), cols)
            return
        sums = products(whole)
        k = pl.program_id(2)
        targets = o_refs if sum_in_outs else acc_refs

        @pl.when(k == 0)
        def _():
            for a in range(n_acc):
                targets[a][...] = sums[a]

        @pl.when(k > 0)
        def _():
            for a in range(n_acc):
                targets[a][...] += sums[a]

        if not sum_in_outs:
            @pl.when(k == nk - 1)
            def _():
                finish([acc_refs[a][...] for a in range(n_acc)], whole)

    sum_in_outs = epilogue is _plain and nk > 1 and n_out == n_acc and all(dt == F32 for dt, _ in outs)
    scratch = [pltpu.VMEM((tm, tn), F32) for _ in range(n_acc)] if nk > 1 and not sum_in_outs else []
    grid = (nj, ni, nk)
    if rows_outer:
        grid = (ni, nj, nk)
        swap = lambda spec: pl.BlockSpec(spec.block_shape, lambda i, j, k, f=spec.index_map: f(j, i, k))
        in_specs, out_specs = [swap(s) for s in in_specs], [swap(s) for s in out_specs]
    aliases, wrap = _host(job, grid, in_specs, args, out_specs, out_shape, scratch)
    sem = ("parallel", "parallel", "arbitrary") if job is None else ("arbitrary",) * 3
    return pl.pallas_call(
        wrap(body), name=name, grid=grid, in_specs=in_specs, out_specs=out_specs, out_shape=out_shape,
        scratch_shapes=scratch, input_output_aliases=aliases, compiler_params=_params(sem),
    )(*args)


def _row_spec(tr, d):
    return pl.BlockSpec((tr, d), lambda i: (i, 0))


def _vec_spec(d):
    return pl.BlockSpec((1, d), lambda i: (0, 0))


def _norm_mod(name, x, g, sc, sh, job=None):
    S, D = x.shape
    tr = _pick(S, 256, 8)

    def body(x_ref, g_ref, sc_ref, sh_ref, h_ref):
        xv = x_ref[...]
        r = lax.rsqrt(jnp.mean(xv * xv, axis=-1, keepdims=True) + EPS)
        h_ref[...] = ((xv * r) * g_ref[...] * (1.0 + sc_ref[...]) + sh_ref[...]).astype(BF16)

    grid = (S // tr,)
    in_specs = [_row_spec(tr, D), _vec_spec(D), _vec_spec(D), _vec_spec(D)]
    args = [x, g, sc, sh]
    out_specs, out_shape, scratch = [_row_spec(tr, D)], [jax.ShapeDtypeStruct((S, D), BF16)], []
    aliases, wrap = _host(job, grid, in_specs, args, out_specs, out_shape, scratch)
    res = pl.pallas_call(
        wrap(body), name=name, grid=grid, in_specs=in_specs, out_specs=out_specs, out_shape=out_shape,
        scratch_shapes=scratch, input_output_aliases=aliases,
        compiler_params=_params(("parallel" if job is None else "arbitrary",)),
    )(*args)
    return res[0] if job is None else res


def _loss_head(x2, target, g, ff, gt):
    S, D = x2.shape
    tr = _pick(S, 256, 8)

    def body(x_ref, t_ref, g_ref, ff_ref, gt_ref, dx_ref, dffb_ref, loss_ref, dgt_ref, dg_ref):
        i = pl.program_id(0)

        @pl.when(i == 0)
        def _():
            loss_ref[...] = jnp.zeros_like(loss_ref)
            dgt_ref[...] = jnp.zeros_like(dgt_ref)
            dg_ref[...] = jnp.zeros_like(dg_ref)

        xv = x_ref[...]
        gv = g_ref[...]
        r = lax.rsqrt(jnp.mean(xv * xv, axis=-1, keepdims=True) + EPS)
        xh = xv * r
        err = xh * gv - t_ref[...]
        loss_ref[...] += 0.5 * jnp.sum(jnp.mean(err * err, axis=-1, keepdims=True))
        dy = err * (1.0 / D)
        dg_ref[...] += jnp.sum(dy * xh, axis=0, keepdims=True)
        dxh = dy * gv
        dx = r * (dxh - xh * jnp.mean(dxh * xh, axis=-1, keepdims=True))
        dx_ref[...] = dx
        dffb_ref[...] = (dx * gt_ref[...]).astype(BF16)
        dgt_ref[...] += jnp.sum(dx * ff_ref[...], axis=0, keepdims=True)

    return pl.pallas_call(
        body, name="loss_head", grid=(S // tr,),
        in_specs=[_row_spec(tr, D), _row_spec(tr, D), _vec_spec(D), _row_spec(tr, D), _vec_spec(D)],
        out_specs=[_row_spec(tr, D), _row_spec(tr, D), pl.BlockSpec((1, 128), lambda i: (0, 0)),
                   _vec_spec(D), _vec_spec(D)],
        out_shape=[jax.ShapeDtypeStruct((S, D), F32), jax.ShapeDtypeStruct((S, D), BF16),
                   jax.ShapeDtypeStruct((1, 128), F32), jax.ShapeDtypeStruct((1, D), F32),
                   jax.ShapeDtypeStruct((1, D), F32)],
        compiler_params=_params(("arbitrary",)),
    )(x2, target, g, ff, gt)


def _norm_mod_bwd(name, dh, x, g, sc, dres, gated=None, job=None):
    S, D = x.shape
    tr = _pick(S, 256, 8)
    n = S // tr
    with_gate = gated is not None

    def body(*refs):
        if with_gate:
            dh_ref, x_ref, g_ref, sc_ref, dres_ref, mo_ref, gt_ref, dx_ref, dsh_ref, dsc_ref, dg_ref, dmob_ref, dgt_ref = refs
        else:
            dh_ref, x_ref, g_ref, sc_ref, dres_ref, dx_ref, dsh_ref, dsc_ref, dg_ref = refs
        i = pl.program_id(0)

        @pl.when(i == 0)
        def _():
            dsh_ref[...] = jnp.zeros_like(dsh_ref)
            dsc_ref[...] = jnp.zeros_like(dsc_ref)
            if with_gate:
                dgt_ref[...] = jnp.zeros_like(dgt_ref)

        xv = x_ref[...]
        dhv = dh_ref[...]
        gv = g_ref[...]
        scale = 1.0 + sc_ref[...]
        r = lax.rsqrt(jnp.mean(xv * xv, axis=-1, keepdims=True) + EPS)
        xh = xv * r
        dsh_ref[...] += jnp.sum(dhv, axis=0, keepdims=True)
        dsc_ref[...] += jnp.sum(dhv * xh, axis=0, keepdims=True)
        dxh = dhv * (gv * scale)
        dx = dres_ref[...] + r * (dxh - xh * jnp.mean(dxh * xh, axis=-1, keepdims=True))
        dx_ref[...] = dx
        if with_gate:
            dmob_ref[...] = (dx * gt_ref[...]).astype(BF16)
            dgt_ref[...] += jnp.sum(dx * mo_ref[...], axis=0, keepdims=True)

        @pl.when(i == n - 1)
        def _():
            t = dsc_ref[...]
            dg_ref[...] = t * scale
            dsc_ref[...] = t * gv

    in_specs = [_row_spec(tr, D), _row_spec(tr, D), _vec_spec(D), _vec_spec(D), _row_spec(tr, D)]
    args = [dh, x, g, sc, dres]
    out_specs = [_row_spec(tr, D), _vec_spec(D), _vec_spec(D), _vec_spec(D)]
    out_shape = [jax.ShapeDtypeStruct((S, D), F32)] + [jax.ShapeDtypeStruct((1, D), F32)] * 3
    if with_gate:
        in_specs += [_row_spec(tr, D), _vec_spec(D)]
        args += list(gated)
        out_specs += [_row_spec(tr, D), _vec_spec(D)]
        out_shape += [jax.ShapeDtypeStruct((S, D), BF16), jax.ShapeDtypeStruct((1, D), F32)]
    scratch = []
    aliases, wrap = _host(job, (n,), in_specs, args, out_specs, out_shape, scratch)
    return pl.pallas_call(
        wrap(body), name=name, grid=(n,), in_specs=in_specs, out_specs=out_specs, out_shape=out_shape,
        scratch_shapes=scratch, input_output_aliases=aliases, compiler_params=_params(("arbitrary",)),
    )(*args)


CONV_ROWS = 256


def _col_spec(S, off_cols):
    o = off_cols // HEAD
    return pl.BlockSpec((S, HEAD), lambda c, o=o: (0, c + o))


def _conv_taps(u, u_prev, rid):
    s1 = jnp.where(rid >= 1, pltpu.roll(u, 1, 0), pltpu.roll(u_prev, 1, 0))
    s2 = jnp.where(rid >= 2, pltpu.roll(u, 2, 0), pltpu.roll(u_prev, 2, 0))
    return s1, s2


def _conv_fwd(proj, conv_w, d_conv, job=None):
    S = proj.shape[0]
    R = min(CONV_ROWS, S)
    nr = S // R

    def body(ab_ref, ac_ref, ax_ref, w_ref, ya_ref):
        w0, w1, w2 = w_ref[0:1, :], w_ref[1:2, :], w_ref[2:3, :]
        rid = lax.broadcasted_iota(jnp.int32, (R, HEAD), 0)

        def step(c, carry):
            rows = pl.ds(pl.multiple_of(c * R, R), R)
            prev = pl.ds(pl.multiple_of(jnp.maximum(c - 1, 0) * R, R), R)
            u = ac_ref[rows, :] * ax_ref[rows, :]
            u_prev = jnp.where(c > 0, ac_ref[prev, :] * ax_ref[prev, :], 0.0)
            s1, s2 = _conv_taps(u, u_prev, rid)
            y = w0 * s2 + w1 * s1 + w2 * u
            ya_ref[rows, :] = (ab_ref[rows, :] * y).astype(BF16)
            return carry

        lax.fori_loop(0, nr, step, 0)

    grid = (d_conv // HEAD,)
    in_specs = [_col_spec(S, 0), _col_spec(S, d_conv), _col_spec(S, 2 * d_conv), pl.BlockSpec((3, HEAD), lambda c: (0, c))]
    args = [proj, proj, proj, conv_w]
    out_specs, out_shape = [pl.BlockSpec((S, HEAD), lambda c: (0, c))], [jax.ShapeDtypeStruct((S, d_conv), BF16)]
    scratch = []
    aliases, wrap = _host(job, grid, in_specs, args, out_specs, out_shape, scratch)
    res = pl.pallas_call(
        wrap(body), name="conv_fwd", grid=grid, in_specs=in_specs, out_specs=out_specs, out_shape=out_shape,
        scratch_shapes=scratch, input_output_aliases=aliases,
        compiler_params=_params(("parallel" if job is None else "arbitrary",)),
    )(*args)
    return res[0] if job is None else res


def _conv_bwd(dya, proj, conv_w, d_conv, job=None):
    S = proj.shape[0]
    R = min(CONV_ROWS, S)
    nr = S // R

    def body(dya_ref, ab_ref, ac_ref, ax_ref, w_ref, dab_ref, dac_ref, dax_ref, dw_ref):
        w0, w1, w2 = w_ref[0:1, :], w_ref[1:2, :], w_ref[2:3, :]
        rid = lax.broadcasted_iota(jnp.int32, (R, HEAD), 0)

        def step(c, carry):
            dw0, dw1, dw2 = carry
            rows = pl.ds(pl.multiple_of(c * R, R), R)
            prev = pl.ds(pl.multiple_of(jnp.maximum(c - 1, 0) * R, R), R)
            nxt = pl.ds(pl.multiple_of(jnp.minimum(c + 1, nr - 1) * R, R), R)
            a_c, a_x, a_b = ac_ref[rows, :], ax_ref[rows, :], ab_ref[rows, :]
            u = a_c * a_x
            u_prev = jnp.where(c > 0, ac_ref[prev, :] * ax_ref[prev, :], 0.0)
            s1, s2 = _conv_taps(u, u_prev, rid)
            y = w0 * s2 + w1 * s1 + w2 * u
            dy = dya_ref[rows, :]
            dab_ref[rows, :] = (dy * y).astype(BF16)
            dyc = dy * a_b
            dyc_next = jnp.where(c < nr - 1, dya_ref[nxt, :] * ab_ref[nxt, :], 0.0)
            t1 = jnp.where(rid < R - 1, pltpu.roll(dyc, R - 1, 0), pltpu.roll(dyc_next, R - 1, 0))
            t2 = jnp.where(rid < R - 2, pltpu.roll(dyc, R - 2, 0), pltpu.roll(dyc_next, R - 2, 0))
            du = w2 * dyc + w1 * t1 + w0 * t2
            dac_ref[rows, :] = (du * a_x).astype(BF16)
            dax_ref[rows, :] = (du * a_c).astype(BF16)
            dw0 = dw0 + jnp.sum(dyc * s2, axis=0, keepdims=True)
            dw1 = dw1 + jnp.sum(dyc * s1, axis=0, keepdims=True)
            dw2 = dw2 + jnp.sum(dyc * u, axis=0, keepdims=True)
            return dw0, dw1, dw2

        z = jnp.zeros((1, HEAD), F32)
        dw0, dw1, dw2 = lax.fori_loop(0, nr, step, (z, z, z))
        dw_ref[0:1, :] = dw0
        dw_ref[1:2, :] = dw1
        dw_ref[2:3, :] = dw2

    col = pl.BlockSpec((S, HEAD), lambda c: (0, c))
    grid = (d_conv // HEAD,)
    in_specs = [col, _col_spec(S, 0), _col_spec(S, d_conv), _col_spec(S, 2 * d_conv),
                pl.BlockSpec((3, HEAD), lambda c: (0, c))]
    args = [dya, proj, proj, proj, conv_w]
    out_specs = [col, col, col, pl.BlockSpec((3, HEAD), lambda c: (0, c))]
    out_shape = [jax.ShapeDtypeStruct((S, d_conv), BF16)] * 3 + [jax.ShapeDtypeStruct((3, d_conv), F32)]
    scratch = []
    aliases, wrap = _host(job, grid, in_specs, args, out_specs, out_shape, scratch)
    return pl.pallas_call(
        wrap(body), name="conv_bwd", grid=grid, in_specs=in_specs, out_specs=out_specs, out_shape=out_shape,
        scratch_shapes=scratch, input_output_aliases=aliases,
        compiler_params=_params(("parallel" if job is None else "arbitrary",)),
    )(*args)


HGRN_FWD_ROWS = 512
HGRN_BWD_ROWS = 1024
HGRN_FWD_SUB = 64
HGRN_BWD_SUB = 32
HGRN_GATE_ROWS = 128


def _block_cumsum(x, pos):
    for s in (1, 2, 4, 8):
        x = x + jnp.where(pos >= s, pltpu.roll(x, s, 0), 0.0)
    return x


def _block_rev_cumsum(x, pos, rows):
    for s in (1, 2, 4, 8):
        x = x + jnp.where(pos < BLK - s, pltpu.roll(x, rows - s, 0), 0.0)
    return x


def _block_last(x, pos, rows):
    m = jnp.where(pos == BLK - 1, x, 0.0)
    for s in (1, 2, 4, 8):
        m = m + pltpu.roll(m, rows - s, 0)
    return m


def _hgrn_gates(q, z, lb):
    sgq = _sig(q)
    qs = q * sgq
    sg = _sig(z)
    f = lb + (1.0 - lb) * sg
    kk = (1.0 - lb) * (1.0 - sg)
    return sgq, qs, sg, f, kk


def _hgrn_decays(q, z, lb, pos, rows):
    sgq, qs, sg, f, kk = _hgrn_gates(q, z, lb)
    b = _block_cumsum(jnp.log(f), pos)
    return sgq, qs, sg, f, kk, b, _block_last(b, pos, rows)


GROUP = HEAD // BLK


def _spread(x, blk):
    zero = jnp.zeros_like(x)
    return jnp.concatenate([jnp.where(blk == j, x, zero) for j in range(GROUP)], axis=1)


def _block_products(a_ref, m_ref, out_ref, nb, transposed):
    rows = GROUP * BLK
    blk = lax.broadcasted_iota(jnp.int32, (rows, HEAD), 0) // BLK
    for g in range(nb // GROUP):
        r = pl.ds(g * rows, rows)
        a = _spread(a_ref[r, :], blk)
        if transposed:
            m = jnp.concatenate([m_ref[g * GROUP + j] for j in range(GROUP)], axis=1)
            dims = (((1,), (1,)), ((), ()))
        else:
            m = m_ref[pl.ds(g * GROUP, GROUP)].reshape(GROUP * HEAD, HEAD)
            dims = (((1,), (0,)), ((), ()))
        out_ref[r, :] = lax.dot_general(a, m, dims, preferred_element_type=F32)


def _block_outer_products(a_ref, b_ref, u_ref, nb):
    rows = GROUP * BLK
    blk = lax.broadcasted_iota(jnp.int32, (rows, HEAD), 0) // BLK
    zero = jnp.zeros((rows, HEAD), BF16)
    for g in range(nb // GROUP):
        r = pl.ds(g * rows, rows)
        b = b_ref[r, :]
        spread = jnp.concatenate([jnp.where(blk == j, b, zero) for j in range(GROUP)], axis=1)
        u = lax.dot_general(a_ref[r, :], spread, (((0,), (0,)), ((), ())), preferred_element_type=F32)
        for j in range(GROUP):
            u_ref[g * GROUP + j] = u[:, j * HEAD:(j + 1) * HEAD]


def _hgrn_specs(T, d_conv, n_t, rev):
    def t_of(i):
        return (n_t - 1 - i) if rev else i

    def pcol(off):
        o = off // HEAD
        return pl.BlockSpec((T, HEAD), lambda h, i, o=o: (t_of(i), h + o))

    q_spec, z_spec, v_spec, g_spec = pcol(3 * d_conv), pcol(4 * d_conv), pcol(5 * d_conv), pcol(6 * d_conv)
    lb_spec = pl.BlockSpec((1, HEAD), lambda h, i: (0, h))
    gn_spec = pl.BlockSpec((1, HEAD), lambda h, i: (0, 0))
    act_spec = pl.BlockSpec((T, HEAD), lambda h, i: (t_of(i), h))
    st_spec = pl.BlockSpec((None, T // BLK, HEAD, HEAD), lambda h, i: (h, t_of(i), 0, 0))
    return q_spec, z_spec, v_spec, g_spec, lb_spec, gn_spec, act_spec, st_spec


def _hgrn_fwd(proj, lb, gn, d_conv, job=None):
    S = proj.shape[0]
    H = d_conv // HEAD
    T = min(HGRN_FWD_ROWS, S)
    n_t, nb = S // T, T // BLK
    sub = min(HGRN_FWD_SUB, T)
    q_spec, z_spec, v_spec, g_spec, lb_spec, gn_spec, act_spec, st_spec = _hgrn_specs(T, d_conv, n_t, False)

    def body(q_ref, z_ref, v_ref, g_ref, lb_ref, gn_ref, ob_ref, o_ref, st_ref, qe_s, ke_s, dec_s, state, v_s, o_s, u_s):
        @pl.when(pl.program_id(1) == 0)
        def _():
            state[...] = jnp.zeros_like(state)

        pos = lax.broadcasted_iota(jnp.int32, (sub, HEAD), 0) % BLK
        lbv = lb_ref[...]

        def vector_pass(s, carry):
            r = pl.ds(pl.multiple_of(s * sub, sub), sub)
            v = v_ref[r, :]
            _, qs, _, _, kk, b, b_tot = _hgrn_decays(q_ref[r, :], z_ref[r, :], lbv, pos, sub)
            qe_s[r, :] = (qs * jnp.exp(b)).astype(BF16)
            ke_s[r, :] = (kk * jnp.exp(b_tot - b)).astype(BF16)
            v_s[r, :] = v.astype(BF16)
            dec_s[r, :] = jnp.exp(b_tot)
            o = jnp.sum(qs * kk, axis=-1, keepdims=True) * v
            for d in range(1, BLK):
                e = jnp.exp(b - pltpu.roll(b, d, 0))
                a = jnp.sum(qs * pltpu.roll(kk, d, 0) * e, axis=-1, keepdims=True)
                o = o + jnp.where(pos >= d, a * pltpu.roll(v, d, 0), 0.0)
            o_s[r, :] = o
            return carry

        lax.fori_loop(0, T // sub, vector_pass, 0)

        _block_outer_products(v_s, ke_s, u_s, nb)
        st = state[...]
        for j in range(nb):
            st_ref[j] = st.astype(BF16)
            st = st * dec_s[pl.ds(j * BLK, 1), :] + u_s[j]
        state[...] = st
        for j in range(nb):
            rows = pl.ds(j * BLK, BLK)
            o_s[rows, :] += lax.dot_general(qe_s[rows, :], st_ref[j], (((1,), (1,)), ((), ())),
                                            preferred_element_type=F32)
        o = o_s[...]
        o_ref[...] = o
        r = lax.rsqrt(jnp.mean(o * o, axis=-1, keepdims=True) + EPS)
        g = g_ref[...]
        ob_ref[...] = ((o * r) * gn_ref[...] * (g * _sig(g))).astype(BF16)

    grid = (H, n_t)
    in_specs = [q_spec, z_spec, v_spec, g_spec, lb_spec, gn_spec]
    args = [proj, proj, proj, proj, lb, gn]
    out_specs = [act_spec, act_spec, st_spec, act_spec, act_spec, act_spec]
    out_shape = [jax.ShapeDtypeStruct((S, d_conv), BF16), jax.ShapeDtypeStruct((S, d_conv), F32),
                 jax.ShapeDtypeStruct((H, S // BLK, HEAD, HEAD), BF16),
                 jax.ShapeDtypeStruct((S, d_conv), BF16), jax.ShapeDtypeStruct((S, d_conv), BF16),
                 jax.ShapeDtypeStruct((S, d_conv), F32)]
    scratch = [pltpu.VMEM((HEAD, HEAD), F32), pltpu.VMEM((T, HEAD), BF16), pltpu.VMEM((T, HEAD), F32),
               pltpu.VMEM((nb, HEAD, HEAD), F32)]
    aliases, wrap = _host(job, grid, in_specs, args, out_specs, out_shape, scratch)
    return pl.pallas_call(
        wrap(body), name="hgrn_fwd", grid=grid, in_specs=in_specs, out_specs=out_specs, out_shape=out_shape,
        scratch_shapes=scratch, input_output_aliases=aliases,
        compiler_params=_params(("parallel" if job is None else "arbitrary", "arbitrary")),
    )(*args)


def _hgrn_bwd(dob, o_raw, states, qe, ke, dec, proj, lb, gn, d_conv, job=None):
    S = proj.shape[0]
    H = d_conv // HEAD
    T = min(HGRN_BWD_ROWS, S)
    n_t, nb = S // T, T // BLK
    sub = min(HGRN_BWD_SUB, T)
    gate_rows = min(HGRN_GATE_ROWS, T)
    q_spec, z_spec, v_spec, g_spec, lb_spec, gn_spec, act_spec, st_spec = _hgrn_specs(T, d_conv, n_t, True)

    def body(dob_ref, o_ref, st_ref, qe_b, ke_b, dec_s, q_ref, z_ref, v_ref, g_ref, lb_ref, gn_ref,
             dq_ref, dz_ref, dv_ref, dg_ref, dlb_ref, dgn_ref,
             dstate, do_b, v_b, dqe_s, dke_s, dvi_s, ddec_s, do_s, u_s, ds_s):
        @pl.when(pl.program_id(1) == 0)
        def _():
            dstate[...] = jnp.zeros_like(dstate)
            dlb_ref[...] = jnp.zeros_like(dlb_ref)
            dgn_ref[...] = jnp.zeros_like(dgn_ref)

        pos = lax.broadcasted_iota(jnp.int32, (sub, HEAD), 0) % BLK
        lbv, gnv = lb_ref[...], gn_ref[...]

        def before(s, carry):
            r = pl.ds(pl.multiple_of(s * gate_rows, gate_rows), gate_rows)
            g = g_ref[r, :]
            v_b[r, :] = v_ref[r, :].astype(BF16)
            o = o_ref[r, :]
            rs = lax.rsqrt(jnp.mean(o * o, axis=-1, keepdims=True) + EPS)
            on = o * rs
            sgg = _sig(g)
            dobv = dob_ref[r, :]
            dog = dobv * (g * sgg)
            dgn_ref[...] += jnp.sum(dog * on, axis=0, keepdims=True)
            don = dog * gnv
            do = rs * (don - on * jnp.mean(don * on, axis=-1, keepdims=True))
            dg_ref[r, :] = (dobv * (on * gnv) * (sgg * (1.0 + g * (1.0 - sgg)))).astype(BF16)
            do_s[r, :] = do
            do_b[r, :] = do.astype(BF16)
            return carry

        lax.fori_loop(0, T // gate_rows, before, 0)

        _block_outer_products(do_b, qe_b, u_s, nb)
        _block_products(do_b, st_ref, dqe_s, nb, False)
        dst = dstate[...]
        for j in reversed(range(nb)):
            ds_s[j] = dst.astype(BF16)
            ddec = jnp.sum(st_ref[j].astype(F32) * dst, axis=0, keepdims=True)
            ddec_s[pl.ds(j * BLK, BLK), :] = jnp.broadcast_to(ddec, (BLK, HEAD))
            dst = dst * dec_s[pl.ds(j * BLK, 1), :] + u_s[j]
        dstate[...] = dst
        _block_products(v_b, ds_s, dke_s, nb, False)
        _block_products(ke_b, ds_s, dvi_s, nb, True)

        def after(s, carry):
            r = pl.ds(pl.multiple_of(s * sub, sub), sub)
            q, v = q_ref[r, :], v_ref[r, :]
            sgq, qs, sg, f, kk, b, b_tot = _hgrn_decays(q, z_ref[r, :], lbv, pos, sub)
            do = do_s[r, :]
            dqs = dqe_s[r, :] * jnp.exp(b)
            dkk = dke_s[r, :] * jnp.exp(b_tot - b)
            d_tot = jnp.where(pos == BLK - 1, _block_cumsum(dkk * kk, pos) + ddec_s[r, :] * jnp.exp(b_tot), 0.0)
            dv = dvi_s[r, :]
            da = jnp.sum(do * v, axis=-1, keepdims=True)
            dv = dv + jnp.sum(qs * kk, axis=-1, keepdims=True) * do
            dqs = dqs + da * kk
            dkk = dkk + da * qs
            for d in range(1, BLK):
                kd = pltpu.roll(kk, d, 0)
                e = jnp.where(pos >= d, jnp.exp(b - pltpu.roll(b, d, 0)), 0.0)
                a = jnp.sum(qs * kd * e, axis=-1, keepdims=True)
                da = jnp.sum(do * pltpu.roll(v, d, 0), axis=-1, keepdims=True)
                gq = da * e
                dqs = dqs + gq * kd
                dkk = dkk + pltpu.roll(gq * qs, sub - d, 0)
                dv = dv + pltpu.roll(a * do, sub - d, 0)
            db = qs * dqs - kk * dkk + d_tot
            dlf = _block_rev_cumsum(db, pos, sub)
            df = dlf / f - dkk
            dlb_ref[...] += jnp.sum(df * (1.0 - sg), axis=0, keepdims=True)
            dz_ref[r, :] = (df * (1.0 - lbv) * sg * (1.0 - sg)).astype(BF16)
            dq_ref[r, :] = (dqs * (sgq * (1.0 + q * (1.0 - sgq)))).astype(BF16)
            dv_ref[r, :] = dv.astype(BF16)
            return carry

        lax.fori_loop(0, T // sub, after, 0)

    tb = lambda dt: pltpu.VMEM((T, HEAD), dt)
    grid = (H, n_t)
    in_specs = [act_spec, act_spec, st_spec, act_spec, act_spec, act_spec, q_spec, z_spec, v_spec, g_spec, lb_spec,
                gn_spec]
    args = [dob, o_raw, states, qe, ke, dec, proj, proj, proj, proj, lb, gn]
    out_specs = [act_spec, act_spec, act_spec, act_spec, lb_spec, pl.BlockSpec((None, 1, HEAD), lambda h, i: (h, 0, 0))]
    out_shape = ([jax.ShapeDtypeStruct((S, d_conv), BF16)] * 4
                 + [jax.ShapeDtypeStruct((1, d_conv), F32), jax.ShapeDtypeStruct((H, 1, HEAD), F32)])
    scratch = [pltpu.VMEM((HEAD, HEAD), F32), tb(BF16), tb(BF16), tb(F32), tb(F32), tb(F32), tb(F32), tb(F32),
               pltpu.VMEM((nb, HEAD, HEAD), F32), pltpu.VMEM((nb, HEAD, HEAD), BF16)]
    aliases, wrap = _host(job, grid, in_specs, args, out_specs, out_shape, scratch)
    return pl.pallas_call(
        wrap(body), name="hgrn_bwd", grid=grid, in_specs=in_specs, out_specs=out_specs, out_shape=out_shape,
        scratch_shapes=scratch, input_output_aliases=aliases,
        compiler_params=_params(("parallel" if job is None else "arbitrary", "arbitrary")),
    )(*args)


def _local_step(x, target, mod, norm_mix_g, lb, gnorm_g, norm_ffn_g, norm_final_g, conv_w, bufs, c_idx, order):
    S, D = x.shape
    d_conv = D // 2
    d_in = 7 * d_conv + 2 * D
    d_ff = N_CHIPS * bufs["w_ffn_down"].shape[1]
    nb_in, nb_co, nb_ff = bufs["w_in"].shape[2], bufs["w_conv_out"].shape[2], bufs["w_ffn_gate"].shape[2]
    rows = lambda g: g.reshape(-1, g.shape[2])
    sh_m, sc_m, gt_m, sh_f, sc_f, gt_f = [mod[:, i * D:(i + 1) * D] for i in range(6)]
    tm = _pick(S, 512, 16)
    tm2 = _pick(S, 1024, 16)
    tn_in = _pick(nb_in, 1408, 128)
    tn_co = _pick(nb_co, 512, 128)
    tn_ff = _pick(nb_ff, 1408, 128)
    tn_d = _pick(D, 512, 128)
    tw = _pick(D, 1024, 128)
    tg = _pick(D, 512, 128)

    h = _norm_mod("norm_mix", x, norm_mix_g, sc_m, sh_m)
    proj, w_in = _proj_gather(h, bufs["w_in"], order)
    ob, o_raw, states, qe, ke, dec, *got = _hgrn_fwd(
        proj, lb, gnorm_g, d_conv,
        job=_GatherJob([bufs[k] for k in ("w_conv_out", "w_hgrn_out", "w_o", "w_ffn_gate")]))
    ya, w_conv_out, w_hgrn_out, w_o, w_ffn_gate, w_ffn_up = _conv_fwd(
        proj, conv_w, d_conv, job=_Jobs([_ForwardJob(got), _GatherJob([bufs["w_ffn_up"]], 0, 2)]))
    w_o = rows(w_o)

    def merge_epi(accs, ex):
        y_a, y_b = accs
        return [y_a, y_b, _sig(ex[0]) * y_a + _sig(ex[1]) * y_b]

    y_a, y_b, merged, w_ffn_up = _matmul(
        "branch_out", [(ya, 'n', w_conv_out, 'n'), (ob, 'n', w_hgrn_out, 'n')], [0, 1], S, D, d_conv, tm2, tn_co, d_conv,
        [(proj, 'tile', 7 * d_conv), (proj, 'tile', 7 * d_conv + D)], [(BF16, None), (BF16, None), (BF16, None)], merge_epi,
        job=_GatherJob([w_ffn_up], 2, 5), rows_outer=True)

    def resid_epi(accs, ex):
        return [accs[0], ex[0] + ex[1] * accs[0]]

    mo, x1, w_ffn_up = _matmul("w_o", [(merged, 'n', w_o, 'n')], [0], S, D, D, tm2, tw, D,
                               [(x, 'tile', 0), (gt_m, 'row', 0)], [(BF16, None), (F32, None)], resid_epi,
                               job=_GatherJob([w_ffn_up], 5, 8))
    h2, w_ffn_up = _norm_mod("norm_ffn", x1, norm_ffn_g, sc_f, sh_f, job=_ForwardJob([w_ffn_up]))

    def swiglu_epi(accs, ex):
        gg, uu = accs
        return [gg, uu, gg * _sig(gg) * uu]

    G, U, act, w_ffn_down = _matmul(
        "ffn_in", [(h2, 'n', w_ffn_gate, 'n'), (h2, 'n', w_ffn_up, 'n')], [0, 1], S, d_ff, D,
        tm2, tn_ff, D, [], [(BF16, None), (BF16, None), (BF16, None)], swiglu_epi, job=_GatherJob([bufs["w_ffn_down"]]))
    w_ffn_down = rows(_forward_halves("forward_down", [w_ffn_down])[0])
    ff, x2 = _matmul("ffn_out", [(act, 'n', w_ffn_down, 'n')], [0], S, D, d_ff, tm2, tn_d, d_ff,
                     [(x1, 'tile', 0), (gt_f, 'row', 0)], [(BF16, None), (F32, None)], resid_epi, rows_outer=True)

    dx2, dffb, loss, dgt_f, dg_final = _loss_head(x2, target, norm_final_g, ff, gt_f)

    def swiglu_bwd_epi(accs, ex):
        dact, gg, uu = accs[0], ex[0], ex[1]
        s = _sig(gg)
        return [dact * uu * (s * (1.0 + gg * (1.0 - s))), dact * (gg * s)]

    dG, dU = _matmul("d_act", [(dffb, 'n', w_ffn_down, 't')], [0], S, d_ff, D, tm2, tn_ff, D,
                     [(G, 'tile', 0), (U, 'tile', 0)], [(BF16, None), (BF16, None)], swiglu_bwd_epi)
    dw_down, = _matmul("dw_ffn_down", [(act, 't', dffb, 'n')], [0], d_ff, D, S, _pick(d_ff, 512, 128),
                       D, S, [], [(BF16, None)], _plain)
    dh2, = _matmul("d_h2", [(dG, 'n', w_ffn_gate, 't'), (dU, 'n', w_ffn_up, 't')], [0, 0], S, D, d_ff,
                   tm, D, tn_ff, [], [(F32, None)], _plain)
    dw_gate, = _matmul("dw_ffn_gate", [(h2, 't', dG, 'n')], [0], D, d_ff, S, tw, tn_ff, S, [], [(BF16, nb_ff)], _plain)
    dw_up, = _matmul("dw_ffn_up", [(h2, 't', dU, 'n')], [0], D, d_ff, S, tw, tn_ff, S, [], [(BF16, nb_ff)], _plain)

    ck_idx = jnp.stack([c_idx[0], order[0]])

    def pair_sums(names, grads, from_sibling):
        pairs = [_pair_sum("pair_sum_" + k, g, q, ck_idx) for k, g, q in zip(names, grads, from_sibling)]
        return [p[0] for p in pairs], [p[1] for p in pairs]

    ffn_names = ["w_ffn_down", "w_ffn_gate", "w_ffn_up"]
    ffn_grads = [dw_down.reshape(N_CHIPS, -1, D), dw_gate, dw_up]
    dx1, dsh_f, dsc_f, dg_ffn, dmob, dgt_m, *from_sibling = _norm_mod_bwd(
        "norm_ffn_bwd", dh2, x1, norm_ffn_g, sc_f, dx2, (mo, gt_m), job=_SwapJob(ffn_grads))
    parts_f, slots_f = pair_sums(ffn_names, ffn_grads, from_sibling)

    def merge_bwd_epi(accs, ex):
        dm, ga, gb, ya_, yb_ = accs[0], ex[0], ex[1], ex[2], ex[3]
        sa, sb = _sig(ga), _sig(gb)
        return [dm * ya_ * sa * (1.0 - sa), dm * yb_ * sb * (1.0 - sb), dm * sa, dm * sb]

    dga, dgb, dy_a, dy_b = _matmul(
        "d_merged", [(dmob, 'n', w_o, 't')], [0], S, D, D, tm2, tn_co, D,
        [(proj, 'tile', 7 * d_conv), (proj, 'tile', 7 * d_conv + D), (y_a, 'tile', 0), (y_b, 'tile', 0)],
        [(BF16, None)] * 4, merge_bwd_epi, rows_outer=True)
    dw_o, = _matmul("dw_o", [(merged, 't', dmob, 'n')], [0], D, D, S, tg, D, S, [], [(BF16, None)], _plain)
    dya, dob = _matmul("d_branch", [(dy_a, 'n', w_conv_out, 't'), (dy_b, 'n', w_hgrn_out, 't')], [0, 1], S, d_conv, D,
                       tm2, _pick(d_conv, 1024, 128), tn_co, [], [(F32, None), (F32, None)], _plain)
    dw_co, dw_ho = _matmul("dw_branch", [(ya, 't', dy_a, 'n'), (ob, 't', dy_b, 'n')], [0, 1], d_conv, D, S,
                           _pick(d_conv, 512, 128), tn_co, S, [], [(BF16, nb_co), (BF16, nb_co)], _plain)
    branch_names = ["w_conv_out", "w_hgrn_out", "w_o"]
    branch_grads = [dw_co, dw_ho, dw_o.reshape(N_CHIPS, -1, D)]
    dab, dac, dax, dconv_w, *from_sibling = _conv_bwd(dya, proj, conv_w, d_conv, job=_SwapJob(branch_grads))
    parts_b, slots_b = pair_sums(branch_names, branch_grads, from_sibling)
    dq, dz, dv, dgo, dlb, dgn, *arrived_f = _hgrn_bwd(dob, o_raw, states, qe, ke, dec, proj, lb, gnorm_g, d_conv,
                                                      job=_ScatterJob(parts_f, slots_f))
    dproj = jnp.concatenate([dab, dac, dax, dq, dz, dv, dgo, dga, dgb], axis=1)
    halves_f = [_sum_chips("sum_chips_" + k, r, c_idx) for k, r in zip(ffn_names, arrived_f)]
    dw_in, *moved = _matmul("dw_in", [(h, 't', dproj, 'n')], [0], D, d_in, S, tw, tn_in, S, [], [(BF16, nb_in)], _plain,
                            job=_Jobs([_ShareJob(halves_f), _ScatterJob(parts_b, slots_b)]))
    whole_f, arrived_b = moved[:len(ffn_names)], moved[len(ffn_names):]
    parts_i, slots_i = pair_sums(["w_in"], [dw_in], _swap_halves("swap_in", [dw_in]))
    dh, arrived_in = _matmul("d_h", [(dproj, 'n', w_in, 't')], [0], S, D, d_in, tm2, D, tn_in, [], [(F32, None)], _plain,
                             job=_ScatterJob(parts_i, slots_i))
    dx, dsh_m, dsc_m, dg_mix = _norm_mod_bwd("norm_mix_bwd", dh, x, norm_mix_g, sc_m, dx1)
    dmod = jnp.concatenate([dsh_m, dsc_m, dgt_m, dsh_f, dsc_f, dgt_f], axis=1)
    small = dict(dmod=dmod, dg_mix=dg_mix, dg_ffn=dg_ffn, dg_final=dg_final, dlb=dlb,
                 dgn=jnp.sum(dgn, axis=0), dconv_w=dconv_w)
    late_names = branch_names + ["w_in"]
    late = _share_halves([_sum_chips("sum_chips_" + k, r, c_idx) for k, r in zip(late_names, arrived_b + [arrived_in])])
    big = dict(zip(ffn_names + late_names, whole_f + list(late)))
    return loss, dx, small, big


def _adamw_math(w, g, m, v):
    m = ADAM_B1 * m + (1.0 - ADAM_B1) * g
    v = ADAM_B2 * v + (1.0 - ADAM_B2) * (g * g)
    m_hat = m / (1.0 - ADAM_B1 ** ADAM_STEP)
    v_hat = v / (1.0 - ADAM_B2 ** ADAM_STEP)
    delta = -ADAM_LR * (m_hat / (jnp.sqrt(v_hat) + ADAM_EPS) + ADAM_WD * w)
    return delta, m, v


def _adamw(name, w, g, m, v):
    R, C = w.shape
    tr = _pick(R, max(8, (256 * 1024) // C // 8 * 8), 8)
    spec = pl.BlockSpec((tr, C), lambda i: (i, 0))

    def body(w_ref, g_ref, m_ref, v_ref, go_ref, d_ref, mo_ref, vo_ref):
        gv = g_ref[...]
        d, m2, v2 = _adamw_math(w_ref[...], gv, m_ref[...], v_ref[...])
        go_ref[...] = gv
        d_ref[...] = d
        mo_ref[...] = m2
        vo_ref[...] = v2

    return pl.pallas_call(
        body, name=name, grid=(R // tr,), in_specs=[spec] * 4, out_specs=[spec] * 4,
        out_shape=[jax.ShapeDtypeStruct((R, C), F32)] * 4, compiler_params=_params(("parallel",)),
    )(w, g, m, v)


def _ada_update(c_act_t, dmod, w, m, v):
    R, C = w.shape
    B = dmod.shape[0]
    tr = _pick(R, 256, 8)
    tc = _pick(C, 1536, 128)
    spec = pl.BlockSpec((tr, tc), lambda i, j: (i, j))

    def body(ct_ref, dm_ref, w_ref, m_ref, v_ref, g_ref, d_ref, mo_ref, vo_ref):
        ct = ct_ref[...]
        dm = dm_ref[...]
        g = ct[:, 0:1] * dm[0:1, :]
        for b in range(1, B):
            g = g + ct[:, b:b + 1] * dm[b:b + 1, :]
        d, m2, v2 = _adamw_math(w_ref[...], g, m_ref[...], v_ref[...])
        g_ref[...] = g
        d_ref[...] = d
        mo_ref[...] = m2
        vo_ref[...] = v2

    return pl.pallas_call(
        body, name="ada_update", grid=(R // tr, C // tc),
        in_specs=[pl.BlockSpec((tr, B), lambda i, j: (i, 0)), pl.BlockSpec((B, tc), lambda i, j: (0, j)),
                  spec, spec, spec],
        out_specs=[spec] * 4, out_shape=[jax.ShapeDtypeStruct((R, C), F32)] * 4,
        compiler_params=_params(("parallel", "parallel")),
    )(c_act_t, dmod, w, m, v)


def _prep(c_all, lb_param):
    def body(c_ref, p_ref, ca_ref, cab_ref, lb_ref):
        cv = c_ref[...]
        ca = cv * _sig(cv)
        ca_ref[...] = ca
        cab_ref[...] = ca.astype(BF16)
        lb_ref[...] = _sig(p_ref[0:1, :] - p_ref[1:2, :])

    return pl.pallas_call(
        body, name="prep",
        out_shape=[jax.ShapeDtypeStruct(c_all.shape, F32), jax.ShapeDtypeStruct(c_all.shape, BF16),
                   jax.ShapeDtypeStruct((1, lb_param.shape[1]), F32)],
    )(c_all, lb_param)


def _small_reduce(parts):
    W = parts.shape[1]

    def body(p_ref, o_ref):
        acc = p_ref[0:1, :]
        for d in range(1, N_DEV):
            acc = acc + p_ref[d:d + 1, :]
        o_ref[...] = acc

    return pl.pallas_call(body, name="small_reduce", out_shape=jax.ShapeDtypeStruct((1, W), F32))(parts)


def _lb_grad(dlb, lb):
    def body(d_ref, lb_ref, o_ref):
        t = d_ref[...] * lb_ref[...] * (1.0 - lb_ref[...])
        o_ref[0:1, :] = t
        o_ref[1:2, :] = -t

    return pl.pallas_call(body, name="lb_grad", out_shape=jax.ShapeDtypeStruct((2, dlb.shape[1]), F32))(dlb, lb)


def _place():
    x, y, c = lax.axis_index("x"), lax.axis_index("y"), lax.axis_index("c")
    chips = [(1 - x, y), (x, 1 - y), (1 - x, 1 - y)]
    return x, y, c, chips


def _allgather_rows(name, v):
    m_per, n = v.shape

    def body(x_ref, out_ref, send_sems, recv_sems, local_sem):
        x, y, c, chips = _place()
        me, sibling = (x, y, c), (x, y, 1 - c)

        def rows(px, py, pc):
            return out_ref.at[pl.ds((4 * px + 2 * py + pc) * m_per, m_per), :]

        def copy(k, block, to, src=None):
            return pltpu.make_async_remote_copy(
                src_ref=rows(*block) if src is None else src, dst_ref=rows(*block),
                send_sem=send_sems.at[k], recv_sem=recv_sems.at[k], device_id=to, device_id_type=MESH)

        mine = pltpu.make_async_copy(x_ref, rows(*me), local_sem)
        mine.start()
        first = [copy(0, me, sibling, src=x_ref)]
        first += [copy(1 + j, me, (*chip, c), src=x_ref) for j, chip in enumerate(chips)]
        for cp in first:
            cp.start()
        passed = [copy(4 + j, (*chip, c), sibling) for j, chip in enumerate(chips)]
        for j, chip in enumerate(chips):
            copy(1 + j, (*chip, c), me).wait_recv()
            passed[j].start()
        copy(0, sibling, me).wait_recv()
        for j, chip in enumerate(chips):
            copy(4 + j, (*chip, 1 - c), me).wait_recv()
        for cp in first + passed:
            cp.wait_send()
        mine.wait()

    return pl.pallas_call(
        body, name=name, out_shape=jax.ShapeDtypeStruct((N_DEV * m_per, n), v.dtype),
        in_specs=[pl.BlockSpec(memory_space=pltpu.VMEM)], out_specs=pl.BlockSpec(memory_space=pltpu.VMEM),
        scratch_shapes=[pltpu.SemaphoreType.DMA((7,)), pltpu.SemaphoreType.DMA((7,)), pltpu.SemaphoreType.DMA],
    )(v)


def _cast_place(name, w, k_idx):
    R, C = w.shape
    tr = _pick(R, max(16, (512 * 1024) // C // 16 * 16), 16)

    def body(k_ref, w_ref, o_ref):
        o_ref[...] = w_ref[...].astype(BF16)

    return pl.pallas_call(
        body, name=name,
        grid_spec=pltpu.PrefetchScalarGridSpec(
            num_scalar_prefetch=1, grid=(R // tr,),
            in_specs=[pl.BlockSpec((tr, C), lambda i, k_ref: (i, 0))],
            out_specs=pl.BlockSpec((None, tr, C), lambda i, k_ref: (k_ref[0], i, 0))),
        out_shape=jax.ShapeDtypeStruct((N_CHIPS, R, C), BF16),
        compiler_params=_params(("parallel",)),
    )(k_idx, w)


def _chip_copies(src_of, dst_of, got_of, n, sems, receiving):
    x, y, c, chips = _place()
    k_me = 2 * x + y
    send_sems, recv_sems = sems
    out = []
    for a in range(n):
        for j, chip in enumerate(chips):
            k_chip = 2 * chip[0] + chip[1]
            if receiving:
                src = dst = got_of(a, k_chip, c)
                to = (x, y, c)
            else:
                src, dst, to = src_of(a, k_chip, k_me, c), dst_of(a, k_me, c), (*chip, c)
            out.append(pltpu.make_async_remote_copy(
                src_ref=src, dst_ref=dst, send_sem=send_sems.at[3 * a + j], recv_sem=recv_sems.at[3 * a + j],
                device_id=to, device_id_type=MESH))
    return out


class _ChipJob:
    def start(self, j_in, j_out, sems):
        for send in self.copies(j_in, j_out, sems, False):
            send.start()

    def finish(self, j_in, j_out, sems):
        for recv in self.copies(j_in, j_out, sems, True):
            recv.wait_recv()
        for send in self.copies(j_in, j_out, sems, False):
            send.wait_send()


class _GatherJob(_ChipJob):
    def __init__(self, bufs, lo=0, hi=8):
        n = len(bufs)
        self.inputs, self.n_alias = list(bufs), n
        self.sem_shapes = [pltpu.SemaphoreType.DMA((3 * n,)), pltpu.SemaphoreType.DMA((3 * n,))]
        self.half = [b.shape[1] // 2 for b in bufs]
        self.lo, self.hi = lo, hi

    def copies(self, j_in, j_out, sems, receiving):
        def half(a, k, c):
            eighth = self.half[a] // 8
            return j_out[a].at[k, pl.ds(c * self.half[a] + self.lo * eighth, (self.hi - self.lo) * eighth)]

        return _chip_copies(lambda a, k_chip, k_me, c: half(a, k_me, c), half, half, len(self.half), sems, receiving)


class _ScatterJob(_ChipJob):
    def __init__(self, parts, slots):
        n = len(parts)
        self.n = n
        self.inputs, self.n_alias = list(parts) + list(slots), n
        self.sem_shapes = [pltpu.SemaphoreType.DMA((3 * n,)), pltpu.SemaphoreType.DMA((3 * n,))]

    def copies(self, j_in, j_out, sems, receiving):
        return _chip_copies(lambda a, k_chip, k_me, c: j_in[a].at[k_chip], lambda a, k_me, c: j_out[a].at[k_me],
                            lambda a, k_chip, c: j_out[a].at[k_chip], self.n, sems, receiving)


class _SwapJob(_ChipJob):
    def __init__(self, grads):
        n = len(grads)
        self.n = n
        self.half = [g.shape[1] // 2 for g in grads]
        landing = [lax.empty((N_CHIPS, g.shape[1] // 2, g.shape[2]), g.dtype) for g in grads]
        self.inputs, self.n_alias = list(grads) + landing, n
        self.sem_shapes = [pltpu.SemaphoreType.DMA((n,)), pltpu.SemaphoreType.DMA((n,))]

    def copies(self, j_in, j_out, sems, receiving):
        x, y, c, _ = _place()
        out = []
        for a in range(self.n):
            if receiving:
                src, to = j_out[a], (x, y, c)
            else:
                src, to = j_in[a].at[:, pl.ds((1 - c) * self.half[a], self.half[a])], (x, y, 1 - c)
            out.append(pltpu.make_async_remote_copy(src_ref=src, dst_ref=j_out[a], send_sem=sems[0].at[a],
                                                    recv_sem=sems[1].at[a], device_id=to, device_id_type=MESH))
        return out


class _Jobs:
    def __init__(self, jobs):
        self.jobs = jobs
        split = [(j.inputs[:len(j.inputs) - j.n_alias], j.inputs[len(j.inputs) - j.n_alias:]) for j in jobs]
        self.inputs = [a for plain, _ in split for a in plain] + [a for _, aliased in split for a in aliased]
        self.n_alias = sum(j.n_alias for j in jobs)
        self.sem_shapes = [s for j in jobs for s in j.sem_shapes]

    def _each(self, j_in, j_out, sems):
        n_plain = len(j_in) - self.n_alias
        p0 = a0 = s0 = 0
        for j in self.jobs:
            n_p, n_s = len(j.inputs) - j.n_alias, len(j.sem_shapes)
            ins = list(j_in[p0:p0 + n_p]) + list(j_in[n_plain + a0:n_plain + a0 + j.n_alias])
            yield j, ins, j_out[a0:a0 + j.n_alias], sems[s0:s0 + n_s]
            p0, a0, s0 = p0 + n_p, a0 + j.n_alias, s0 + n_s

    def start(self, j_in, j_out, sems):
        for j, ins, outs, s in self._each(j_in, j_out, sems):
            j.start(ins, outs, s)

    def finish(self, j_in, j_out, sems):
        for j, ins, outs, s in self._each(j_in, j_out, sems):
            j.finish(ins, outs, s)


class _ShareJob(_ChipJob):
    def __init__(self, bufs):
        n = len(bufs)
        self.n = n
        self.inputs, self.n_alias = list(bufs), n
        self.sem_shapes = [pltpu.SemaphoreType.DMA((n,)), pltpu.SemaphoreType.DMA((n,))]

    def copies(self, j_in, j_out, sems, receiving):
        x, y, c, _ = _place()
        out = []
        for a in range(self.n):
            hc, to = (1 - c, (x, y, c)) if receiving else (c, (x, y, 1 - c))
            out.append(pltpu.make_async_remote_copy(
                src_ref=j_out[a].at[hc], dst_ref=j_out[a].at[hc], send_sem=sems[0].at[a], recv_sem=sems[1].at[a],
                device_id=to, device_id_type=MESH))
        return out


class _ForwardJob(_ChipJob):
    def __init__(self, bufs):
        n = len(bufs)
        self.n = n
        self.half = [b.shape[1] // 2 for b in bufs]
        self.inputs, self.n_alias = list(bufs), n
        self.sem_shapes = [pltpu.SemaphoreType.DMA((3 * n,)), pltpu.SemaphoreType.DMA((3 * n,))]

    def copies(self, j_in, j_out, sems, receiving):
        x, y, c, chips = _place()
        out = []
        for a in range(self.n):
            for q, chip in enumerate(chips):
                hc, to = (1 - c, (x, y, c)) if receiving else (c, (x, y, 1 - c))
                part = j_out[a].at[2 * chip[0] + chip[1], pl.ds(hc * self.half[a], self.half[a])]
                out.append(pltpu.make_async_remote_copy(
                    src_ref=part, dst_ref=part, send_sem=sems[0].at[3 * a + q], recv_sem=sems[1].at[3 * a + q],
                    device_id=to, device_id_type=MESH))
        return out


def _forward_halves(name, bufs):
    n = len(bufs)

    def body(*refs):
        out_refs = refs[n:2 * n]
        send_sems, recv_sems = refs[2 * n:]
        x, y, c, chips = _place()
        started, waits = [], []
        for a in range(n):
            rh = bufs[a].shape[1] // 2
            for j, chip in enumerate(chips):
                k_chip = 2 * chip[0] + chip[1]
                got = out_refs[a].at[k_chip, pl.ds(c * rh, rh)]
                cp = pltpu.make_async_remote_copy(src_ref=got, dst_ref=got, send_sem=send_sems.at[3 * a + j],
                                                  recv_sem=recv_sems.at[3 * a + j], device_id=(x, y, 1 - c),
                                                  device_id_type=MESH)
                cp.start()
                started.append(cp)
                other = out_refs[a].at[k_chip, pl.ds((1 - c) * rh, rh)]
                waits.append(pltpu.make_async_remote_copy(
                    src_ref=other, dst_ref=other, send_sem=send_sems.at[3 * a + j], recv_sem=recv_sems.at[3 * a + j],
                    device_id=(x, y, c), device_id_type=MESH))
        for cp in waits:
            cp.wait_recv()
        for cp in started:
            cp.wait_send()

    return pl.pallas_call(
        body, name=name, out_shape=[jax.ShapeDtypeStruct(b.shape, b.dtype) for b in bufs],
        in_specs=[_HBM] * n, out_specs=[_HBM] * n, input_output_aliases={a: a for a in range(n)},
        scratch_shapes=[pltpu.SemaphoreType.DMA((3 * n,)), pltpu.SemaphoreType.DMA((3 * n,))],
    )(*bufs)


def _proj_gather(h, buf, order):
    S, D = h.shape
    nb = buf.shape[2]
    tm = _pick(S, 1024, 16)
    tn = _pick(nb, 1408, 128)
    per = nb // tn
    nj, ni = N_CHIPS * per, S // tm
    rh = D // 2
    seq = [(0, t) for t in range(per)] + [(p, t) for t in range(per) for p in (1, 2)] + [(3, t) for t in range(per)]
    tile_chip = jnp.stack([order[p] for p, _ in seq])
    tile_of = jnp.array([t for _, t in seq], jnp.int32)

    def body(chip_ref, t_ref, h_ref, buf_in, proj_ref, buf_ref, wbuf, tile_sems, ici_send, ici_recv, d2d_send, d2d_recv):
        j, i = pl.program_id(0), pl.program_id(1)
        x, y, c, chips = _place()
        k_me = 2 * x + y

        def part(k, hc, t):
            return buf_ref.at[k, pl.ds(hc * rh, rh), pl.ds(t * tn, tn)]

        def ici(q, t, receiving):
            k_chip = 2 * chips[q][0] + chips[q][1]
            src, to = (part(k_chip, c, t), (x, y, c)) if receiving else (part(k_me, c, t), (*chips[q], c))
            return pltpu.make_async_remote_copy(src_ref=src, dst_ref=src, send_sem=ici_send.at[q * per + t],
                                                recv_sem=ici_recv.at[q * per + t], device_id=to, device_id_type=MESH)

        def d2d(q, t, receiving):
            k_chip = 2 * chips[q][0] + chips[q][1]
            src, to = (part(k_chip, 1 - c, t), (x, y, c)) if receiving else (part(k_chip, c, t), (x, y, 1 - c))
            return pltpu.make_async_remote_copy(src_ref=src, dst_ref=src, send_sem=d2d_send.at[q * per + t],
                                                recv_sem=d2d_recv.at[q * per + t], device_id=to, device_id_type=MESH)

        def tile_copy(n):
            col = pl.multiple_of(t_ref[n] * tn, 128)
            return pltpu.make_async_copy(buf_ref.at[chip_ref[n], :, pl.ds(col, tn)], wbuf.at[n % 2], tile_sems.at[n % 2])

        @pl.when(jnp.logical_and(j == 0, i == 0))
        def _():
            for t in range(per):
                ici(0, t, False).start()
                ici(1, t, False).start()
            tile_copy(0).start()

        @pl.when(i == 0)
        def _():
            tile_copy(j).wait()
            for n in range(per, nj):
                p, t = seq[n]

                @pl.when(j + 1 == n)
                def _():
                    ici(p - 1, t, True).wait_recv()
                    d2d(p - 1, t, False).start()
                    if (p, t) == (1, per - 1):
                        for q in range(2):
                            for tt in range(per):
                                ici(q, tt, False).wait_send()
                        for tt in range(per):
                            ici(2, tt, False).start()
                    d2d(p - 1, t, True).wait_recv()

            @pl.when(j + 1 < nj)
            def _():
                tile_copy(j + 1).start()

        proj_ref[...] = jnp.dot(h_ref[...], wbuf[j % 2], preferred_element_type=F32)

        @pl.when(jnp.logical_and(j == nj - 1, i == ni - 1))
        def _():
            for t in range(per):
                ici(2, t, False).wait_send()
                for q in range(3):
                    d2d(q, t, False).wait_send()

    n_sems = 3 * per
    return pl.pallas_call(
        body, name="proj",
        grid_spec=pltpu.PrefetchScalarGridSpec(
            num_scalar_prefetch=2, grid=(nj, ni),
            in_specs=[pl.BlockSpec((tm, D), lambda j, i, kc, kt: (i, 0)), _HBM],
            out_specs=[pl.BlockSpec((tm, tn), lambda j, i, kc, kt: (i, kc[j] * per + kt[j])), _HBM],
            scratch_shapes=[pltpu.VMEM((2, D, tn), BF16), pltpu.SemaphoreType.DMA((2,))]
            + [pltpu.SemaphoreType.DMA((n_sems,))] * 4),
        out_shape=[jax.ShapeDtypeStruct((S, N_CHIPS * nb), F32), jax.ShapeDtypeStruct(buf.shape, buf.dtype)],
        input_output_aliases={3: 1}, compiler_params=_params(("arbitrary", "arbitrary")),
    )(tile_chip, tile_of, h, buf)


def _swap_halves(name, grads):
    n = len(grads)

    def body(*refs):
        in_refs, out_refs = refs[:n], refs[n:2 * n]
        send_sems, recv_sems = refs[2 * n:]
        x, y, c, _ = _place()
        cps = []
        for a in range(n):
            rh = grads[a].shape[1] // 2
            cp = pltpu.make_async_remote_copy(
                src_ref=in_refs[a].at[:, pl.ds((1 - c) * rh, rh)], dst_ref=out_refs[a],
                send_sem=send_sems.at[a], recv_sem=recv_sems.at[a], device_id=(x, y, 1 - c), device_id_type=MESH)
            cp.start()
            cps.append(cp)
        for cp in cps:
            cp.wait()

    return pl.pallas_call(
        body, name=name,
        out_shape=[jax.ShapeDtypeStruct((N_CHIPS, g.shape[1] // 2, g.shape[2]), g.dtype) for g in grads],
        in_specs=[_HBM] * n, out_specs=[_HBM] * n,
        scratch_shapes=[pltpu.SemaphoreType.DMA((n,)), pltpu.SemaphoreType.DMA((n,))],
    )(*grads)


def _pair_sum(name, g, q, ck_idx):
    _, R, C = g.shape
    rh = R // 2
    tr = _pick(rh, max(16, (512 * 1024) // C // 16 * 16), 16)
    nh = rh // tr

    def body(ck_ref, g_ref, q_ref, o_ref, own_ref):
        s = (g_ref[...].astype(F32) + q_ref[...].astype(F32)).astype(BF16)
        o_ref[...] = s

        @pl.when(pl.program_id(1) == ck_ref[1])
        def _():
            own_ref[...] = s

    return pl.pallas_call(
        body, name=name,
        grid_spec=pltpu.PrefetchScalarGridSpec(
            num_scalar_prefetch=1, grid=(nh, N_CHIPS),
            in_specs=[pl.BlockSpec((None, tr, C), lambda i, k, ck: (k, ck[0] * nh + i, 0)),
                      pl.BlockSpec((None, tr, C), lambda i, k, ck: (k, i, 0))],
            out_specs=[pl.BlockSpec((None, tr, C), lambda i, k, ck: (k, i, 0)),
                       pl.BlockSpec((None, tr, C), lambda i, k, ck: (ck[1], i, 0))]),
        out_shape=[jax.ShapeDtypeStruct((N_CHIPS, rh, C), BF16)] * 2,
        compiler_params=_params(("parallel", "arbitrary")),
    )(ck_idx, g, q)


def _sum_chips(name, r, c_idx):
    _, rh, C = r.shape
    tr = _pick(rh, max(16, (256 * 1024) // C // 16 * 16), 16)

    def body(c_ref, r_ref, o_ref):
        acc = r_ref[0].astype(F32)
        for k in range(1, N_CHIPS):
            acc = acc + r_ref[k].astype(F32)
        o_ref[...] = acc

    return pl.pallas_call(
        body, name=name,
        grid_spec=pltpu.PrefetchScalarGridSpec(
            num_scalar_prefetch=1, grid=(rh // tr,),
            in_specs=[pl.BlockSpec((N_CHIPS, tr, C), lambda i, c_ref: (0, i, 0))],
            out_specs=pl.BlockSpec((None, tr, C), lambda i, c_ref: (c_ref[0], i, 0))),
        out_shape=jax.ShapeDtypeStruct((2, rh, C), F32), compiler_params=_params(("parallel",)),
    )(c_idx, r)


def _share_halves(bufs):
    n = len(bufs)

    def body(*refs):
        out_refs = refs[n:2 * n]
        send_sems, recv_sems = refs[2 * n:]
        x, y, c, _ = _place()
        cps = []
        for a in range(n):
            cp = pltpu.make_async_remote_copy(
                src_ref=out_refs[a].at[c], dst_ref=out_refs[a].at[c], send_sem=send_sems.at[a],
                recv_sem=recv_sems.at[a], device_id=(x, y, 1 - c), device_id_type=MESH)
            cp.start()
            cps.append(cp)
        for a, cp in enumerate(cps):
            got = out_refs[a].at[1 - c]
            pltpu.make_async_remote_copy(src_ref=got, dst_ref=got, send_sem=send_sems.at[a], recv_sem=recv_sems.at[a],
                                         device_id=(x, y, c), device_id_type=MESH).wait_recv()
        for cp in cps:
            cp.wait_send()

    return pl.pallas_call(
        body, name="share_halves",
        out_shape=[jax.ShapeDtypeStruct(b.shape, b.dtype) for b in bufs],
        in_specs=[_HBM] * n, out_specs=[_HBM] * n, input_output_aliases={a: a for a in range(n)},
        scratch_shapes=[pltpu.SemaphoreType.DMA((n,)), pltpu.SemaphoreType.DMA((n,))],
    )(*bufs)


_BIG = ("w_in", "w_conv_out", "w_hgrn_out", "w_o", "w_ffn_gate", "w_ffn_up", "w_ffn_down")
_WEIGHTS = ("w_ada", "b_ada", "norm_mix_g", "w_in", "conv_w", "lb_param", "gnorm_g", "w_conv_out", "w_hgrn_out",
            "w_o", "norm_ffn_g", "w_ffn_gate", "w_ffn_up", "w_ffn_down", "norm_final_g")


def _pack_rows(pieces):
    flat = jnp.concatenate([p.reshape(-1) for p in pieces])
    pad = (-flat.shape[0]) % (SUBLANES * LANES)
    return jnp.pad(flat, (0, pad)).reshape(SUBLANES, -1)


def kernel(x, c, w_ada, b_ada, norm_mix_g, w_in, conv_w, lb_param, gnorm_g, w_conv_out, w_hgrn_out, w_o, norm_ffn_g, w_ffn_gate, w_ffn_up, w_ffn_down, norm_final_g, loss_target, m_w_ada, m_b_ada, m_norm_mix_g, m_w_in, m_conv_w, m_lb_param, m_gnorm_g, m_w_conv_out, m_w_hgrn_out, m_w_o, m_norm_ffn_g, m_w_ffn_gate, m_w_ffn_up, m_w_ffn_down, m_norm_final_g, v_w_ada, v_b_ada, v_norm_mix_g, v_w_in, v_conv_w, v_lb_param, v_gnorm_g, v_w_conv_out, v_w_hgrn_out, v_w_o, v_norm_ffn_g, v_w_ffn_gate, v_w_ffn_up, v_w_ffn_down, v_norm_final_g):
    w = dict(w_ada=w_ada, b_ada=b_ada, norm_mix_g=norm_mix_g, w_in=w_in, conv_w=conv_w, lb_param=lb_param,
             gnorm_g=gnorm_g, w_conv_out=w_conv_out, w_hgrn_out=w_hgrn_out, w_o=w_o, norm_ffn_g=norm_ffn_g,
             w_ffn_gate=w_ffn_gate, w_ffn_up=w_ffn_up, w_ffn_down=w_ffn_down, norm_final_g=norm_final_g)
    m = dict(w_ada=m_w_ada, b_ada=m_b_ada, norm_mix_g=m_norm_mix_g, w_in=m_w_in, conv_w=m_conv_w, lb_param=m_lb_param,
             gnorm_g=m_gnorm_g, w_conv_out=m_w_conv_out, w_hgrn_out=m_w_hgrn_out, w_o=m_w_o, norm_ffn_g=m_norm_ffn_g,
             w_ffn_gate=m_w_ffn_gate, w_ffn_up=m_w_ffn_up, w_ffn_down=m_w_ffn_down, norm_final_g=m_norm_final_g)
    v = dict(w_ada=v_w_ada, b_ada=v_b_ada, norm_mix_g=v_norm_mix_g, w_in=v_w_in, conv_w=v_conv_w, lb_param=v_lb_param,
             gnorm_g=v_gnorm_g, w_conv_out=v_w_conv_out, w_hgrn_out=v_w_hgrn_out, w_o=v_w_o, norm_ffn_g=v_norm_ffn_g,
             w_ffn_gate=v_w_ffn_gate, w_ffn_up=v_w_ffn_up, w_ffn_down=v_w_ffn_down, norm_final_g=v_norm_final_g)
    two_d = lambda a: a.reshape((-1, a.shape[-1])) if a.ndim != 2 else a
    shapes = {k: a.shape for k, a in w.items()}
    w, m, v = ({k: two_d(a) for k, a in t.items()} for t in (w, m, v))
    w["lb_param"], m["lb_param"], v["lb_param"] = lb_param, m_lb_param, v_lb_param
    S, D = x.shape[1], x.shape[2]
    d_conv = D // 2
    ix, iy, ic = lax.axis_index("x"), lax.axis_index("y"), lax.axis_index("c")
    k_me = 2 * ix + iy
    dev = 2 * k_me + ic
    cw_shard = w["conv_w"].shape[1]
    ada_cols = w["w_ada"].shape[1]

    got = _allgather_rows("gather_cond", _pack_rows([c, w["conv_w"]])).reshape(N_DEV, -1)
    c_all = got[:, :D]
    conv_full = got[::2, D:D + 3 * cw_shard].reshape(N_CHIPS, 3, cw_shard).transpose(1, 0, 2).reshape(3, -1)
    c_act, c_act_b, lb = _prep(c_all, lb_param)
    b_cols = lax.dynamic_slice(w["b_ada"], (0, k_me * ada_cols), (1, ada_cols))
    mod_cols, = _matmul("ada_fwd", [(c_act_b, 'n', w["w_ada"].astype(BF16), 'n')], [0], N_DEV, ada_cols, D,
                        N_DEV, _pick(ada_cols, 1536, 128), D, [(b_cols, 'row', 0)], [(F32, None)],
                        lambda accs, ex: [accs[0] + ex[0]])
    mod_all = _allgather_rows("gather_mod", mod_cols).reshape(N_CHIPS, 2, N_DEV, ada_cols)[:, 0]
    mod_all = mod_all.transpose(1, 0, 2).reshape(N_DEV, -1)
    mod = lax.dynamic_slice(mod_all, (dev, 0), (1, mod_all.shape[1]))
    k_idx = jnp.reshape(k_me, (1,)).astype(jnp.int32)
    c_idx = jnp.reshape(ic, (1,)).astype(jnp.int32)
    bufs = {k: _cast_place("cast_" + k, w[k], k_idx) for k in _BIG}
    order = jnp.stack([k_me, 2 * (1 - ix) + iy, 2 * ix + (1 - iy), 2 * (1 - ix) + (1 - iy)]).astype(jnp.int32)

    loss, dx, small, arrived = _local_step(
        x[0], loss_target[0], mod, w["norm_mix_g"], lb, w["gnorm_g"], w["norm_ffn_g"], w["norm_final_g"], conv_full,
        bufs, c_idx, order)

    pieces = [small["dmod"], small["dg_mix"], small["dg_ffn"], small["dg_final"], small["dlb"], small["dgn"],
              small["dconv_w"], loss]
    sizes = [p.size for p in pieces]
    parts = _allgather_rows("gather_small", _pack_rows(pieces)).reshape(N_DEV, -1)
    total = _small_reduce(parts)
    offs = [0]
    for s in sizes:
        offs.append(offs[-1] + s)
    tot = [total[:, offs[i]:offs[i + 1]] for i in range(len(sizes))]
    dmod_all = parts[:, :sizes[0]]
    grads = {
        "b_ada": tot[0], "norm_mix_g": tot[1], "norm_ffn_g": tot[2], "norm_final_g": tot[3],
        "lb_param": _lb_grad(tot[4], lb), "gnorm_g": tot[5],
        "conv_w": lax.dynamic_slice(tot[6].reshape(3, -1), (0, k_me * cw_shard), (3, cw_shard)),
    }
    loss_out = tot[7][0, 0]

    for k in _BIG:
        grads[k] = arrived[k].reshape(-1, arrived[k].shape[-1])

    delta, new_m, new_v = {}, {}, {}
    dmod_cols = lax.dynamic_slice(dmod_all, (0, k_me * ada_cols), (N_DEV, ada_cols))
    grads["w_ada"], delta["w_ada"], new_m["w_ada"], new_v["w_ada"] = _ada_update(
        c_act.T, dmod_cols, w["w_ada"], m["w_ada"], v["w_ada"])
    for k in _WEIGHTS:
        if k != "w_ada":
            grads[k], delta[k], new_m[k], new_v[k] = _adamw("adamw_" + k, w[k], grads[k], m[k], v[k])
    outs = [loss_out, dx.reshape(x.shape)]
    for t in (grads, delta, new_m, new_v):
        outs += [t[k].reshape(shapes[k]) for k in _WEIGHTS]
    return tuple(outs)
```

```python
import functools

import jax
import jax.numpy as jnp
from jax import lax
from jax.experimental import pallas as pl
from jax.experimental.pallas import tpu as pltpu

F32 = jnp.float32
BF16 = jnp.bfloat16
MESH = pl.DeviceIdType.MESH

EPS = 1e-6
HEAD = 128
BLK = 16
N_CHIPS = 4
N_DEV = 8
SUBLANES, LANES = 8, 128
VMEM_LIMIT_BYTES = 56 * 1024 * 1024

ADAM_LR = 0.001
ADAM_B1 = 0.9
ADAM_B2 = 0.999
ADAM_EPS = 1e-08
ADAM_WD = 0.01
ADAM_STEP = 10


def _pick(dim, pref, mult):
    if dim <= pref:
        return dim
    t = (pref // mult) * mult
    while t >= mult:
        if dim % t == 0:
            return t
        t -= mult
    return dim


def _sig(x):
    return 1.0 / (1.0 + jnp.exp(-x))


def _params(sem):
    return pltpu.CompilerParams(dimension_semantics=sem, vmem_limit_bytes=VMEM_LIMIT_BYTES)


def _gj(j, i, k):
    return j


def _gk(j, i, k):
    return k


def _wspec(arr, rt, ct, r_of, c_of):
    if arr.ndim == 2:
        return pl.BlockSpec((rt, ct), lambda j, i, k: (r_of(j, i, k), c_of(j, i, k)))
    per = arr.shape[2] // ct
    assert arr.shape[2] % ct == 0
    return pl.BlockSpec((None, rt, ct),
                        lambda j, i, k: (c_of(j, i, k) // per, r_of(j, i, k), c_of(j, i, k) % per))


_HBM = pl.BlockSpec(memory_space=pltpu.HBM)


def _host(job, grid, in_specs, args, out_specs, out_shape, scratch):
    if job is None:
        return {}, (lambda body: body)
    n_in, n_out, n_scr = len(args), len(out_shape), len(scratch)
    n_job, n_alias = len(job.inputs), job.n_alias
    in_specs += [_HBM] * n_job
    args += job.inputs
    out_specs += [_HBM] * n_alias
    out_shape += [jax.ShapeDtypeStruct(a.shape, a.dtype) for a in job.inputs[n_job - n_alias:]]
    scratch += job.sem_shapes
    aliases = {n_in + n_job - n_alias + t: n_out + t for t in range(n_alias)}

    def wrap(body):
        def hosted(*refs):
            ins, refs = refs[:n_in], refs[n_in:]
            j_in, refs = refs[:n_job], refs[n_job:]
            outs, refs = refs[:n_out], refs[n_out:]
            j_out, refs = refs[:n_alias], refs[n_alias:]
            scr, sems = refs[:n_scr], refs[n_scr:]
            first = functools.reduce(jnp.logical_and, [pl.program_id(d) == 0 for d in range(len(grid))])
            last = functools.reduce(jnp.logical_and, [pl.program_id(d) == grid[d] - 1 for d in range(len(grid))])

            @pl.when(first)
            def _():
                job.start(j_in, j_out, sems)

            body(*ins, *outs, *scr)

            @pl.when(last)
            def _():
                job.finish(j_in, j_out, sems)

        return hosted

    return aliases, wrap


EPILOGUE_COLS = 768


def _plain(accs, extras):
    return accs


def _matmul(name, pairs, acc_of, M, N, K, tm, tn, tk, extras, outs, epilogue, job=None, rows_outer=False):
    assert M % tm == 0 and N % tn == 0 and K % tk == 0, (name, M, N, K, tm, tn, tk)
    nj, ni, nk = N // tn, M // tm, K // tk
    n_pairs, n_extra, n_out = len(pairs), len(extras), len(outs)
    n_acc = max(acc_of) + 1
    in_specs, args, dims = [], [], []
    lhs_of, seen = [], {}
    for a, am, b, bm in pairs:
        if (id(a), am) not in seen:
            seen[id(a), am] = len(args)
            args.append(a)
            if am == 'n':
                in_specs.append(pl.BlockSpec((tm, tk), lambda j, i, k: (i, k)))
            else:
                in_specs.append(pl.BlockSpec((tk, tm), lambda j, i, k: (k, i)))
        lhs_of.append(seen[id(a), am])
    n_lhs = len(args)
    for a, am, b, bm in pairs:
        if bm == 'n':
            in_specs.append(_wspec(b, tk, tn, _gk, _gj))
        else:
            in_specs.append(_wspec(b, tn, tk, _gj, _gk))
        dims.append((((1 if am == 'n' else 0,), (0 if bm == 'n' else 1,)), ((), ())))
        args.append(b)
    for e, kind, off in extras:
        assert off % tn == 0, (name, off, tn)
        o = off // tn
        if kind == 'tile':
            in_specs.append(pl.BlockSpec((tm, tn), lambda j, i, k, o=o: (i, j + o)))
        else:
            in_specs.append(pl.BlockSpec((1, tn), lambda j, i, k, o=o: (0, j + o)))
        args.append(e)
    out_shape, out_specs = [], []
    for dt, nb in outs:
        if nb is None:
            out_shape.append(jax.ShapeDtypeStruct((M, N), dt))
            out_specs.append(pl.BlockSpec((tm, tn), lambda j, i, k: (i, j)))
        else:
            assert nb % tn == 0 and N % nb == 0
            per = nb // tn
            out_shape.append(jax.ShapeDtypeStruct((N // nb, M, nb), dt))
            out_specs.append(pl.BlockSpec((None, tm, tn), lambda j, i, k, per=per: (j // per, i, j % per)))

    def body(*refs):
        n_ab = n_lhs + n_pairs
        e_refs = refs[n_ab:n_ab + n_extra]
        o_refs = refs[n_ab + n_extra:n_ab + n_extra + n_out]
        acc_refs = refs[n_ab + n_extra + n_out:]

        def products(cols):
            sums = [None] * n_acc
            for p in range(n_pairs):
                b_ref = refs[n_lhs + p]
                b = b_ref[:, cols] if pairs[p][3] == 'n' else b_ref[cols, :]
                prod = lax.dot_general(refs[lhs_of[p]][...], b, dims[p], preferred_element_type=F32)
                a = acc_of[p]
                sums[a] = prod if sums[a] is None else sums[a] + prod
            return sums

        def finish(accs, cols):
            res = epilogue(accs, [e[:, cols] for e in e_refs])
            for o_ref, r in zip(o_refs, res):
                o_ref[:, cols] = r.astype(o_ref.dtype)

        whole = slice(0, tn)
        if nk == 1:
            step = tn if epilogue is _plain else EPILOGUE_COLS
            for c0 in range(0, tn, step):
                cols = slice(c0, min(c0 + step, tn))
                finish(products(cols), cols)
            return
        sums = products(whole)
        k = pl.program_id(2)
        targets = o_refs if sum_in_outs else acc_refs

        @pl.when(k == 0)
        def _():
            for a in range(n_acc):
                targets[a][...] = sums[a]

        @pl.when(k > 0)
        def _():
            for a in range(n_acc):
                targets[a][...] += sums[a]

        if not sum_in_outs:
            @pl.when(k == nk - 1)
            def _():
                finish([acc_refs[a][...] for a in range(n_acc)], whole)

    sum_in_outs = epilogue is _plain and nk > 1 and n_out == n_acc and all(dt == F32 for dt, _ in outs)
    scratch = [pltpu.VMEM((tm, tn), F32) for _ in range(n_acc)] if nk > 1 and not sum_in_outs else []
    grid = (nj, ni, nk)
    if rows_outer:
        grid = (ni, nj, nk)
        swap = lambda spec: pl.BlockSpec(spec.block_shape, lambda i, j, k, f=spec.index_map: f(j, i, k))
        in_specs, out_specs = [swap(s) for s in in_specs], [swap(s) for s in out_specs]
    aliases, wrap = _host(job, grid, in_specs, args, out_specs, out_shape, scratch)
    sem = ("parallel", "parallel", "arbitrary") if job is None else ("arbitrary",) * 3
    return pl.pallas_call(
        wrap(body), name=name, grid=grid, in_specs=in_specs, out_specs=out_specs, out_shape=out_shape,
        scratch_shapes=scratch, input_output_aliases=aliases, compiler_params=_params(sem),
    )(*args)


def _row_spec(tr, d):
    return pl.BlockSpec((tr, d), lambda i: (i, 0))


def _vec_spec(d):
    return pl.BlockSpec((1, d), lambda i: (0, 0))


def _norm_mod(name, x, g, sc, sh, job=None):
    S, D = x.shape
    tr = _pick(S, 256, 8)

    def body(x_ref, g_ref, sc_ref, sh_ref, h_ref):
        xv = x_ref[...]
        r = lax.rsqrt(jnp.mean(xv * xv, axis=-1, keepdims=True) + EPS)
        h_ref[...] = ((xv * r) * g_ref[...] * (1.0 + sc_ref[...]) + sh_ref[...]).astype(BF16)

    grid = (S // tr,)
    in_specs = [_row_spec(tr, D), _vec_spec(D), _vec_spec(D), _vec_spec(D)]
    args = [x, g, sc, sh]
    out_specs, out_shape, scratch = [_row_spec(tr, D)], [jax.ShapeDtypeStruct((S, D), BF16)], []
    aliases, wrap = _host(job, grid, in_specs, args, out_specs, out_shape, scratch)
    res = pl.pallas_call(
        wrap(body), name=name, grid=grid, in_specs=in_specs, out_specs=out_specs, out_shape=out_shape,
        scratch_shapes=scratch, input_output_aliases=aliases,
        compiler_params=_params(("parallel" if job is None else "arbitrary",)),
    )(*args)
    return res[0] if job is None else res


def _loss_head(x2, target, g, ff, gt):
    S, D = x2.shape
    tr = _pick(S, 256, 8)

    def body(x_ref, t_ref, g_ref, ff_ref, gt_ref, dx_ref, dffb_ref, loss_ref, dgt_ref, dg_ref):
        i = pl.program_id(0)

        @pl.when(i == 0)
        def _():
            loss_ref[...] = jnp.zeros_like(loss_ref)
            dgt_ref[...] = jnp.zeros_like(dgt_ref)
            dg_ref[...] = jnp.zeros_like(dg_ref)

        xv = x_ref[...]
        gv = g_ref[...]
        r = lax.rsqrt(jnp.mean(xv * xv, axis=-1, keepdims=True) + EPS)
        xh = xv * r
        err = xh * gv - t_ref[...]
        loss_ref[...] += 0.5 * jnp.sum(jnp.mean(err * err, axis=-1, keepdims=True))
        dy = err * (1.0 / D)
        dg_ref[...] += jnp.sum(dy * xh, axis=0, keepdims=True)
        dxh = dy * gv
        dx = r * (dxh - xh * jnp.mean(dxh * xh, axis=-1, keepdims=True))
        dx_ref[...] = dx
        dffb_ref[...] = (dx * gt_ref[...]).astype(BF16)
        dgt_ref[...] += jnp.sum(dx * ff_ref[...], axis=0, keepdims=True)

    return pl.pallas_call(
        body, name="loss_head", grid=(S // tr,),
        in_specs=[_row_spec(tr, D), _row_spec(tr, D), _vec_spec(D), _row_spec(tr, D), _vec_spec(D)],
        out_specs=[_row_spec(tr, D), _row_spec(tr, D), pl.BlockSpec((1, 128), lambda i: (0, 0)),
                   _vec_spec(D), _vec_spec(D)],
        out_shape=[jax.ShapeDtypeStruct((S, D), F32), jax.ShapeDtypeStruct((S, D), BF16),
                   jax.ShapeDtypeStruct((1, 128), F32), jax.ShapeDtypeStruct((1, D), F32),
                   jax.ShapeDtypeStruct((1, D), F32)],
        compiler_params=_params(("arbitrary",)),
    )(x2, target, g, ff, gt)


def _norm_mod_bwd(name, dh, x, g, sc, dres, gated=None, job=None):
    S, D = x.shape
    tr = _pick(S, 256, 8)
    n = S // tr
    with_gate = gated is not None

    def body(*refs):
        if with_gate:
            dh_ref, x_ref, g_ref, sc_ref, dres_ref, mo_ref, gt_ref, dx_ref, dsh_ref, dsc_ref, dg_ref, dmob_ref, dgt_ref = refs
        else:
            dh_ref, x_ref, g_ref, sc_ref, dres_ref, dx_ref, dsh_ref, dsc_ref, dg_ref = refs
        i = pl.program_id(0)

        @pl.when(i == 0)
        def _():
            dsh_ref[...] = jnp.zeros_like(dsh_ref)
            dsc_ref[...] = jnp.zeros_like(dsc_ref)
            if with_gate:
                dgt_ref[...] = jnp.zeros_like(dgt_ref)

        xv = x_ref[...]
        dhv = dh_ref[...]
        gv = g_ref[...]
        scale = 1.0 + sc_ref[...]
        r = lax.rsqrt(jnp.mean(xv * xv, axis=-1, keepdims=True) + EPS)
        xh = xv * r
        dsh_ref[...] += jnp.sum(dhv, axis=0, keepdims=True)
        dsc_ref[...] += jnp.sum(dhv * xh, axis=0, keepdims=True)
        dxh = dhv * (gv * scale)
        dx = dres_ref[...] + r * (dxh - xh * jnp.mean(dxh * xh, axis=-1, keepdims=True))
        dx_ref[...] = dx
        if with_gate:
            dmob_ref[...] = (dx * gt_ref[...]).astype(BF16)
            dgt_ref[...] += jnp.sum(dx * mo_ref[...], axis=0, keepdims=True)

        @pl.when(i == n - 1)
        def _():
            t = dsc_ref[...]
            dg_ref[...] = t * scale
            dsc_ref[...] = t * gv

    in_specs = [_row_spec(tr, D), _row_spec(tr, D), _vec_spec(D), _vec_spec(D), _row_spec(tr, D)]
    args = [dh, x, g, sc, dres]
    out_specs = [_row_spec(tr, D), _vec_spec(D), _vec_spec(D), _vec_spec(D)]
    out_shape = [jax.ShapeDtypeStruct((S, D), F32)] + [jax.ShapeDtypeStruct((1, D), F32)] * 3
    if with_gate:
        in_specs += [_row_spec(tr, D), _vec_spec(D)]
        args += list(gated)
        out_specs += [_row_spec(tr, D), _vec_spec(D)]
        out_shape += [jax.ShapeDtypeStruct((S, D), BF16), jax.ShapeDtypeStruct((1, D), F32)]
    scratch = []
    aliases, wrap = _host(job, (n,), in_specs, args, out_specs, out_shape, scratch)
    return pl.pallas_call(
        wrap(body), name=name, grid=(n,), in_specs=in_specs, out_specs=out_specs, out_shape=out_shape,
        scratch_shapes=scratch, input_output_aliases=aliases, compiler_params=_params(("arbitrary",)),
    )(*args)


CONV_ROWS = 256


def _col_spec(S, off_cols):
    o = off_cols // HEAD
    return pl.BlockSpec((S, HEAD), lambda c, o=o: (0, c + o))


def _conv_taps(u, u_prev, rid):
    s1 = jnp.where(rid >= 1, pltpu.roll(u, 1, 0), pltpu.roll(u_prev, 1, 0))
    s2 = jnp.where(rid >= 2, pltpu.roll(u, 2, 0), pltpu.roll(u_prev, 2, 0))
    return s1, s2


def _conv_fwd(proj, conv_w, d_conv, job=None):
    S = proj.shape[0]
    R = min(CONV_ROWS, S)
    nr = S // R

    def body(ab_ref, ac_ref, ax_ref, w_ref, ya_ref):
        w0, w1, w2 = w_ref[0:1, :], w_ref[1:2, :], w_ref[2:3, :]
        rid = lax.broadcasted_iota(jnp.int32, (R, HEAD), 0)

        def step(c, carry):
            rows = pl.ds(pl.multiple_of(c * R, R), R)
            prev = pl.ds(pl.multiple_of(jnp.maximum(c - 1, 0) * R, R), R)
            u = ac_ref[rows, :] * ax_ref[rows, :]
            u_prev = jnp.where(c > 0, ac_ref[prev, :] * ax_ref[prev, :], 0.0)
            s1, s2 = _conv_taps(u, u_prev, rid)
            y = w0 * s2 + w1 * s1 + w2 * u
            ya_ref[rows, :] = (ab_ref[rows, :] * y).astype(BF16)
            return carry

        lax.fori_loop(0, nr, step, 0)

    grid = (d_conv // HEAD,)
    in_specs = [_col_spec(S, 0), _col_spec(S, d_conv), _col_spec(S, 2 * d_conv), pl.BlockSpec((3, HEAD), lambda c: (0, c))]
    args = [proj, proj, proj, conv_w]
    out_specs, out_shape = [pl.BlockSpec((S, HEAD), lambda c: (0, c))], [jax.ShapeDtypeStruct((S, d_conv), BF16)]
    scratch = []
    aliases, wrap = _host(job, grid, in_specs, args, out_specs, out_shape, scratch)
    res = pl.pallas_call(
        wrap(body), name="conv_fwd", grid=grid, in_specs=in_specs, out_specs=out_specs, out_shape=out_shape,
        scratch_shapes=scratch, input_output_aliases=aliases,
        compiler_params=_params(("parallel" if job is None else "arbitrary",)),
    )(*args)
    return res[0] if job is None else res


def _conv_bwd(dya, proj, conv_w, d_conv, job=None):
    S = proj.shape[0]
    R = min(CONV_ROWS, S)
    nr = S // R

    def body(dya_ref, ab_ref, ac_ref, ax_ref, w_ref, dab_ref, dac_ref, dax_ref, dw_ref):
        w0, w1, w2 = w_ref[0:1, :], w_ref[1:2, :], w_ref[2:3, :]
        rid = lax.broadcasted_iota(jnp.int32, (R, HEAD), 0)

        def step(c, carry):
            dw0, dw1, dw2 = carry
            rows = pl.ds(pl.multiple_of(c * R, R), R)
            prev = pl.ds(pl.multiple_of(jnp.maximum(c - 1, 0) * R, R), R)
            nxt = pl.ds(pl.multiple_of(jnp.minimum(c + 1, nr - 1) * R, R), R)
            a_c, a_x, a_b = ac_ref[rows, :], ax_ref[rows, :], ab_ref[rows, :]
            u = a_c * a_x
            u_prev = jnp.where(c > 0, ac_ref[prev, :] * ax_ref[prev, :], 0.0)
            s1, s2 = _conv_taps(u, u_prev, rid)
            y = w0 * s2 + w1 * s1 + w2 * u
            dy = dya_ref[rows, :]
            dab_ref[rows, :] = (dy * y).astype(BF16)
            dyc = dy * a_b
            dyc_next = jnp.where(c < nr - 1, dya_ref[nxt, :] * ab_ref[nxt, :], 0.0)
            t1 = jnp.where(rid < R - 1, pltpu.roll(dyc, R - 1, 0), pltpu.roll(dyc_next, R - 1, 0))
            t2 = jnp.where(rid < R - 2, pltpu.roll(dyc, R - 2, 0), pltpu.roll(dyc_next, R - 2, 0))
            du = w2 * dyc + w1 * t1 + w0 * t2
            dac_ref[rows, :] = (du * a_x).astype(BF16)
            dax_ref[rows, :] = (du * a_c).astype(BF16)
            dw0 = dw0 + jnp.sum(dyc * s2, axis=0, keepdims=True)
            dw1 = dw1 + jnp.sum(dyc * s1, axis=0, keepdims=True)
            dw2 = dw2 + jnp.sum(dyc * u, axis=0, keepdims=True)
            return dw0, dw1, dw2

        z = jnp.zeros((1, HEAD), F32)
        dw0, dw1, dw2 = lax.fori_loop(0, nr, step, (z, z, z))
        dw_ref[0:1, :] = dw0
        dw_ref[1:2, :] = dw1
        dw_ref[2:3, :] = dw2

    col = pl.BlockSpec((S, HEAD), lambda c: (0, c))
    grid = (d_conv // HEAD,)
    in_specs = [col, _col_spec(S, 0), _col_spec(S, d_conv), _col_spec(S, 2 * d_conv),
                pl.BlockSpec((3, HEAD), lambda c: (0, c))]
    args = [dya, proj, proj, proj, conv_w]
    out_specs = [col, col, col, pl.BlockSpec((3, HEAD), lambda c: (0, c))]
    out_shape = [jax.ShapeDtypeStruct((S, d_conv), BF16)] * 3 + [jax.ShapeDtypeStruct((3, d_conv), F32)]
    scratch = []
    aliases, wrap = _host(job, grid, in_specs, args, out_specs, out_shape, scratch)
    return pl.pallas_call(
        wrap(body), name="conv_bwd", grid=grid, in_specs=in_specs, out_specs=out_specs, out_shape=out_shape,
        scratch_shapes=scratch, input_output_aliases=aliases,
        compiler_params=_params(("parallel" if job is None else "arbitrary",)),
    )(*args)


HGRN_FWD_ROWS = 512
HGRN_BWD_ROWS = 1024
HGRN_FWD_SUB = 64
HGRN_BWD_SUB = 32
HGRN_GATE_ROWS = 128


def _block_cumsum(x, pos):
    for s in (1, 2, 4, 8):
        x = x + jnp.where(pos >= s, pltpu.roll(x, s, 0), 0.0)
    return x


def _block_rev_cumsum(x, pos, rows):
    for s in (1, 2, 4, 8):
        x = x + jnp.where(pos < BLK - s, pltpu.roll(x, rows - s, 0), 0.0)
    return x


def _block_last(x, pos, rows):
    m = jnp.where(pos == BLK - 1, x, 0.0)
    for s in (1, 2, 4, 8):
        m = m + pltpu.roll(m, rows - s, 0)
    return m


def _hgrn_gates(q, z, lb):
    sgq = _sig(q)
    qs = q * sgq
    sg = _sig(z)
    f = lb + (1.0 - lb) * sg
    kk = (1.0 - lb) * (1.0 - sg)
    return sgq, qs, sg, f, kk


def _hgrn_decays(q, z, lb, pos, rows):
    sgq, qs, sg, f, kk = _hgrn_gates(q, z, lb)
    b = _block_cumsum(jnp.log(f), pos)
    return sgq, qs, sg, f, kk, b, _block_last(b, pos, rows)


GROUP = HEAD // BLK


def _spread(x, blk):
    zero = jnp.zeros_like(x)
    return jnp.concatenate([jnp.where(blk == j, x, zero) for j in range(GROUP)], axis=1)


def _block_products(a_ref, m_ref, out_ref, nb, transposed):
    rows = GROUP * BLK
    blk = lax.broadcasted_iota(jnp.int32, (rows, HEAD), 0) // BLK
    for g in range(nb // GROUP):
        r = pl.ds(g * rows, rows)
        a = _spread(a_ref[r, :], blk)
        if transposed:
            m = jnp.concatenate([m_ref[g * GROUP + j] for j in range(GROUP)], axis=1)
            dims = (((1,), (1,)), ((), ()))
        else:
            m = m_ref[pl.ds(g * GROUP, GROUP)].reshape(GROUP * HEAD, HEAD)
            dims = (((1,), (0,)), ((), ()))
        out_ref[r, :] = lax.dot_general(a, m, dims, preferred_element_type=F32)


def _block_outer_products(a_ref, b_ref, u_ref, nb):
    rows = GROUP * BLK
    blk = lax.broadcasted_iota(jnp.int32, (rows, HEAD), 0) // BLK
    zero = jnp.zeros((rows, HEAD), BF16)
    for g in range(nb // GROUP):
        r = pl.ds(g * rows, rows)
        b = b_ref[r, :]
        spread = jnp.concatenate([jnp.where(blk == j, b, zero) for j in range(GROUP)], axis=1)
        u = lax.dot_general(a_ref[r, :], spread, (((0,), (0,)), ((), ())), preferred_element_type=F32)
        for j in range(GROUP):
            u_ref[g * GROUP + j] = u[:, j * HEAD:(j + 1) * HEAD]


def _hgrn_specs(T, d_conv, n_t, rev):
    def t_of(i):
        return (n_t - 1 - i) if rev else i

    def pcol(off):
        o = off // HEAD
        return pl.BlockSpec((T, HEAD), lambda h, i, o=o: (t_of(i), h + o))

    q_spec, z_spec, v_spec, g_spec = pcol(3 * d_conv), pcol(4 * d_conv), pcol(5 * d_conv), pcol(6 * d_conv)
    lb_spec = pl.BlockSpec((1, HEAD), lambda h, i: (0, h))
    gn_spec = pl.BlockSpec((1, HEAD), lambda h, i: (0, 0))
    act_spec = pl.BlockSpec((T, HEAD), lambda h, i: (t_of(i), h))
    st_spec = pl.BlockSpec((None, T // BLK, HEAD, HEAD), lambda h, i: (h, t_of(i), 0, 0))
    return q_spec, z_spec, v_spec, g_spec, lb_spec, gn_spec, act_spec, st_spec


def _hgrn_fwd(proj, lb, gn, d_conv, job=None):
    S = proj.shape[0]
    H = d_conv // HEAD
    T = min(HGRN_FWD_ROWS, S)
    n_t, nb = S // T, T // BLK
    sub = min(HGRN_FWD_SUB, T)
    q_spec, z_spec, v_spec, g_spec, lb_spec, gn_spec, act_spec, st_spec = _hgrn_specs(T, d_conv, n_t, False)

    def body(q_ref, z_ref, v_ref, g_ref, lb_ref, gn_ref, ob_ref, o_ref, st_ref, qe_s, ke_s, dec_s, state, v_s, o_s, u_s):
        @pl.when(pl.program_id(1) == 0)
        def _():
            state[...] = jnp.zeros_like(state)

        pos = lax.broadcasted_iota(jnp.int32, (sub, HEAD), 0) % BLK
        lbv = lb_ref[...]

        def vector_pass(s, carry):
            r = pl.ds(pl.multiple_of(s * sub, sub), sub)
            v = v_ref[r, :]
            _, qs, _, _, kk, b, b_tot = _hgrn_decays(q_ref[r, :], z_ref[r, :], lbv, pos, sub)
            qe_s[r, :] = (qs * jnp.exp(b)).astype(BF16)
            ke_s[r, :] = (kk * jnp.exp(b_tot - b)).astype(BF16)
            v_s[r, :] = v.astype(BF16)
            dec_s[r, :] = jnp.exp(b_tot)
            o = jnp.sum(qs * kk, axis=-1, keepdims=True) * v
            for d in range(1, BLK):
                e = jnp.exp(b - pltpu.roll(b, d, 0))
                a = jnp.sum(qs * pltpu.roll(kk, d, 0) * e, axis=-1, keepdims=True)
                o = o + jnp.where(pos >= d, a * pltpu.roll(v, d, 0), 0.0)
            o_s[r, :] = o
            return carry

        lax.fori_loop(0, T // sub, vector_pass, 0)

        _block_outer_products(v_s, ke_s, u_s, nb)
        st = state[...]
        for j in range(nb):
            st_ref[j] = st.astype(BF16)
            st = st * dec_s[pl.ds(j * BLK, 1), :] + u_s[j]
        state[...] = st
        for j in range(nb):
            rows = pl.ds(j * BLK, BLK)
            o_s[rows, :] += lax.dot_general(qe_s[rows, :], st_ref[j], (((1,), (1,)), ((), ())),
                                            preferred_element_type=F32)
        o = o_s[...]
        o_ref[...] = o
        r = lax.rsqrt(jnp.mean(o * o, axis=-1, keepdims=True) + EPS)
        g = g_ref[...]
        ob_ref[...] = ((o * r) * gn_ref[...] * (g * _sig(g))).astype(BF16)

    grid = (H, n_t)
    in_specs = [q_spec, z_spec, v_spec, g_spec, lb_spec, gn_spec]
    args = [proj, proj, proj, proj, lb, gn]
    out_specs = [act_spec, act_spec, st_spec, act_spec, act_spec, act_spec]
    out_shape = [jax.ShapeDtypeStruct((S, d_conv), BF16), jax.ShapeDtypeStruct((S, d_conv), F32),
                 jax.ShapeDtypeStruct((H, S // BLK, HEAD, HEAD), BF16),
                 jax.ShapeDtypeStruct((S, d_conv), BF16), jax.ShapeDtypeStruct((S, d_conv), BF16),
                 jax.ShapeDtypeStruct((S, d_conv), F32)]
    scratch = [pltpu.VMEM((HEAD, HEAD), F32), pltpu.VMEM((T, HEAD), BF16), pltpu.VMEM((T, HEAD), F32),
               pltpu.VMEM((nb, HEAD, HEAD), F32)]
    aliases, wrap = _host(job, grid, in_specs, args, out_specs, out_shape, scratch)
    return pl.pallas_call(
        wrap(body), name="hgrn_fwd", grid=grid, in_specs=in_specs, out_specs=out_specs, out_shape=out_shape,
        scratch_shapes=scratch, input_output_aliases=aliases,
        compiler_params=_params(("parallel" if job is None else "arbitrary", "arbitrary")),
    )(*args)


def _hgrn_bwd(dob, o_raw, states, qe, ke, dec, proj, lb, gn, d_conv, job=None):
    S = proj.shape[0]
    H = d_conv // HEAD
    T = min(HGRN_BWD_ROWS, S)
    n_t, nb = S // T, T // BLK
    sub = min(HGRN_BWD_SUB, T)
    gate_rows = min(HGRN_GATE_ROWS, T)
    q_spec, z_spec, v_spec, g_spec, lb_spec, gn_spec, act_spec, st_spec = _hgrn_specs(T, d_conv, n_t, True)

    def body(dob_ref, o_ref, st_ref, qe_b, ke_b, dec_s, q_ref, z_ref, v_ref, g_ref, lb_ref, gn_ref,
             dq_ref, dz_ref, dv_ref, dg_ref, dlb_ref, dgn_ref,
             dstate, do_b, v_b, dqe_s, dke_s, dvi_s, ddec_s, do_s, u_s, ds_s):
        @pl.when(pl.program_id(1) == 0)
        def _():
            dstate[...] = jnp.zeros_like(dstate)
            dlb_ref[...] = jnp.zeros_like(dlb_ref)
            dgn_ref[...] = jnp.zeros_like(dgn_ref)

        pos = lax.broadcasted_iota(jnp.int32, (sub, HEAD), 0) % BLK
        lbv, gnv = lb_ref[...], gn_ref[...]

        def before(s, carry):
            r = pl.ds(pl.multiple_of(s * gate_rows, gate_rows), gate_rows)
            g = g_ref[r, :]
            v_b[r, :] = v_ref[r, :].astype(BF16)
            o = o_ref[r, :]
            rs = lax.rsqrt(jnp.mean(o * o, axis=-1, keepdims=True) + EPS)
            on = o * rs
            sgg = _sig(g)
            dobv = dob_ref[r, :]
            dog = dobv * (g * sgg)
            dgn_ref[...] += jnp.sum(dog * on, axis=0, keepdims=True)
            don = dog * gnv
            do = rs * (don - on * jnp.mean(don * on, axis=-1, keepdims=True))
            dg_ref[r, :] = (dobv * (on * gnv) * (sgg * (1.0 + g * (1.0 - sgg)))).astype(BF16)
            do_s[r, :] = do
            do_b[r, :] = do.astype(BF16)
            return carry

        lax.fori_loop(0, T // gate_rows, before, 0)

        _block_outer_products(do_b, qe_b, u_s, nb)
        _block_products(do_b, st_ref, dqe_s, nb, False)
        dst = dstate[...]
        for j in reversed(range(nb)):
            ds_s[j] = dst.astype(BF16)
            ddec = jnp.sum(st_ref[j].astype(F32) * dst, axis=0, keepdims=True)
            ddec_s[pl.ds(j * BLK, BLK), :] = jnp.broadcast_to(ddec, (BLK, HEAD))
            dst = dst * dec_s[pl.ds(j * BLK, 1), :] + u_s[j]
        dstate[...] = dst
        _block_products(v_b, ds_s, dke_s, nb, False)
        _block_products(ke_b, ds_s, dvi_s, nb, True)

        def after(s, carry):
            r = pl.ds(pl.multiple_of(s * sub, sub), sub)
            q, v = q_ref[r, :], v_ref[r, :]
            sgq, qs, sg, f, kk, b, b_tot = _hgrn_decays(q, z_ref[r, :], lbv, pos, sub)
            do = do_s[r, :]
            dqs = dqe_s[r, :] * jnp.exp(b)
            dkk = dke_s[r, :] * jnp.exp(b_tot - b)
            d_tot = jnp.where(pos == BLK - 1, _block_cumsum(dkk * kk, pos) + ddec_s[r, :] * jnp.exp(b_tot), 0.0)
            dv = dvi_s[r, :]
            da = jnp.sum(do * v, axis=-1, keepdims=True)
            dv = dv + jnp.sum(qs * kk, axis=-1, keepdims=True) * do
            dqs = dqs + da * kk
            dkk = dkk + da * qs
            for d in range(1, BLK):
                kd = pltpu.roll(kk, d, 0)
                e = jnp.where(pos >= d, jnp.exp(b - pltpu.roll(b, d, 0)), 0.0)
                a = jnp.sum(qs * kd * e, axis=-1, keepdims=True)
                da = jnp.sum(do * pltpu.roll(v, d, 0), axis=-1, keepdims=True)
                gq = da * e
                dqs = dqs + gq * kd
                dkk = dkk + pltpu.roll(gq * qs, sub - d, 0)
                dv = dv + pltpu.roll(a * do, sub - d, 0)
            db = qs * dqs - kk * dkk + d_tot
            dlf = _block_rev_cumsum(db, pos, sub)
            df = dlf / f - dkk
            dlb_ref[...] += jnp.sum(df * (1.0 - sg), axis=0, keepdims=True)
            dz_ref[r, :] = (df * (1.0 - lbv) * sg * (1.0 - sg)).astype(BF16)
            dq_ref[r, :] = (dqs * (sgq * (1.0 + q * (1.0 - sgq)))).astype(BF16)
            dv_ref[r, :] = dv.astype(BF16)
            return carry

        lax.fori_loop(0, T // sub, after, 0)

    tb = lambda dt: pltpu.VMEM((T, HEAD), dt)
    grid = (H, n_t)
    in_specs = [act_spec, act_spec, st_spec, act_spec, act_spec, act_spec, q_spec, z_spec, v_spec, g_spec, lb_spec,
                gn_spec]
    args = [dob, o_raw, states, qe, ke, dec, proj, proj, proj, proj, lb, gn]
    out_specs = [act_spec, act_spec, act_spec, act_spec, lb_spec, pl.BlockSpec((None, 1, HEAD), lambda h, i: (h, 0, 0))]
    out_shape = ([jax.ShapeDtypeStruct((S, d_conv), BF16)] * 4
                 + [jax.ShapeDtypeStruct((1, d_conv), F32), jax.ShapeDtypeStruct((H, 1, HEAD), F32)])
    scratch = [pltpu.VMEM((HEAD, HEAD), F32), tb(BF16), tb(BF16), tb(F32), tb(F32), tb(F32), tb(F32), tb(F32),
               pltpu.VMEM((nb, HEAD, HEAD), F32), pltpu.VMEM((nb, HEAD, HEAD), BF16)]
    aliases, wrap = _host(job, grid, in_specs, args, out_specs, out_shape, scratch)
    return pl.pallas_call(
        wrap(body), name="hgrn_bwd", grid=grid, in_specs=in_specs, out_specs=out_specs, out_shape=out_shape,
        scratch_shapes=scratch, input_output_aliases=aliases,
        compiler_params=_params(("parallel" if job is None else "arbitrary", "arbitrary")),
    )(*args)


def _local_step(x, target, mod, norm_mix_g, lb, gnorm_g, norm_ffn_g, norm_final_g, conv_w, bufs, c_idx, order):
    S, D = x.shape
    d_conv = D // 2
    d_in = 7 * d_conv + 2 * D
    d_ff = N_CHIPS * bufs["w_ffn_down"].shape[1]
    nb_in, nb_co, nb_ff = bufs["w_in"].shape[2], bufs["w_conv_out"].shape[2], bufs["w_ffn_gate"].shape[2]
    rows = lambda g: g.reshape(-1, g.shape[2])
    sh_m, sc_m, gt_m, sh_f, sc_f, gt_f = [mod[:, i * D:(i + 1) * D] for i in range(6)]
    tm = _pick(S, 512, 16)
    tm2 = _pick(S, 1024, 16)
    tn_in = _pick(nb_in, 1408, 128)
    tn_co = _pick(nb_co, 512, 128)
    tn_ff = _pick(nb_ff, 1408, 128)
    tn_d = _pick(D, 512, 128)
    tw = _pick(D, 1024, 128)
    tg = _pick(D, 512, 128)

    h = _norm_mod("norm_mix", x, norm_mix_g, sc_m, sh_m)
    proj, w_in = _proj_gather(h, bufs["w_in"], order)
    ob, o_raw, states, qe, ke, dec, *got = _hgrn_fwd(
        proj, lb, gnorm_g, d_conv,
        job=_GatherJob([bufs[k] for k in ("w_conv_out", "w_hgrn_out", "w_o", "w_ffn_gate")]))
    ya, w_conv_out, w_hgrn_out, w_o, w_ffn_gate, w_ffn_up = _conv_fwd(
        proj, conv_w, d_conv, job=_Jobs([_ForwardJob(got), _GatherJob([bufs["w_ffn_up"]], 0, 2)]))
    w_o = rows(w_o)

    def merge_epi(accs, ex):
        y_a, y_b = accs
        return [y_a, y_b, _sig(ex[0]) * y_a + _sig(ex[1]) * y_b]

    y_a, y_b, merged, w_ffn_up = _matmul(
        "branch_out", [(ya, 'n', w_conv_out, 'n'), (ob, 'n', w_hgrn_out, 'n')], [0, 1], S, D, d_conv, tm2, tn_co, d_conv,
        [(proj, 'tile', 7 * d_conv), (proj, 'tile', 7 * d_conv + D)], [(BF16, None), (BF16, None), (BF16, None)], merge_epi,
        job=_GatherJob([w_ffn_up], 2, 5), rows_outer=True)

    def resid_epi(accs, ex):
        return [accs[0], ex[0] + ex[1] * accs[0]]

    mo, x1, w_ffn_up = _matmul("w_o", [(merged, 'n', w_o, 'n')], [0], S, D, D, tm2, tw, D,
                               [(x, 'tile', 0), (gt_m, 'row', 0)], [(BF16, None), (F32, None)], resid_epi,
                               job=_GatherJob([w_ffn_up], 5, 8))
    h2, w_ffn_up = _norm_mod("norm_ffn", x1, norm_ffn_g, sc_f, sh_f, job=_ForwardJob([w_ffn_up]))

    def swiglu_epi(accs, ex):
        gg, uu = accs
        return [gg, uu, gg * _sig(gg) * uu]

    G, U, act, w_ffn_down = _matmul(
        "ffn_in", [(h2, 'n', w_ffn_gate, 'n'), (h2, 'n', w_ffn_up, 'n')], [0, 1], S, d_ff, D,
        tm2, tn_ff, D, [], [(BF16, None), (BF16, None), (BF16, None)], swiglu_epi, job=_GatherJob([bufs["w_ffn_down"]]))
    w_ffn_down = rows(_forward_halves("forward_down", [w_ffn_down])[0])
    ff, x2 = _matmul("ffn_out", [(act, 'n', w_ffn_down, 'n')], [0], S, D, d_ff, tm2, tn_d, d_ff,
                     [(x1, 'tile', 0), (gt_f, 'row', 0)], [(BF16, None), (F32, None)], resid_epi, rows_outer=True)

    dx2, dffb, loss, dgt_f, dg_final = _loss_head(x2, target, norm_final_g, ff, gt_f)

    def swiglu_bwd_epi(accs, ex):
        dact, gg, uu = accs[0], ex[0], ex[1]
        s = _sig(gg)
        return [dact * uu * (s * (1.0 + gg * (1.0 - s))), dact * (gg * s)]

    dG, dU = _matmul("d_act", [(dffb, 'n', w_ffn_down, 't')], [0], S, d_ff, D, tm2, tn_ff, D,
                     [(G, 'tile', 0), (U, 'tile', 0)], [(BF16, None), (BF16, None)], swiglu_bwd_epi)
    dw_down, = _matmul("dw_ffn_down", [(act, 't', dffb, 'n')], [0], d_ff, D, S, _pick(d_ff, 512, 128),
                       D, S, [], [(BF16, None)], _plain)
    dh2, = _matmul("d_h2", [(dG, 'n', w_ffn_gate, 't'), (dU, 'n', w_ffn_up, 't')], [0, 0], S, D, d_ff,
                   tm, D, tn_ff, [], [(F32, None)], _plain)
    dw_gate, = _matmul("dw_ffn_gate", [(h2, 't', dG, 'n')], [0], D, d_ff, S, tw, tn_ff, S, [], [(BF16, nb_ff)], _plain)
    dw_up, = _matmul("dw_ffn_up", [(h2, 't', dU, 'n')], [0], D, d_ff, S, tw, tn_ff, S, [], [(BF16, nb_ff)], _plain)

    ck_idx = jnp.stack([c_idx[0], order[0]])

    def pair_sums(names, grads, from_sibling):
        pairs = [_pair_sum("pair_sum_" + k, g, q, ck_idx) for k, g, q in zip(names, grads, from_sibling)]
        return [p[0] for p in pairs], [p[1] for p in pairs]

    ffn_names = ["w_ffn_down", "w_ffn_gate", "w_ffn_up"]
    ffn_grads = [dw_down.reshape(N_CHIPS, -1, D), dw_gate, dw_up]
    dx1, dsh_f, dsc_f, dg_ffn, dmob, dgt_m, *from_sibling = _norm_mod_bwd(
        "norm_ffn_bwd", dh2, x1, norm_ffn_g, sc_f, dx2, (mo, gt_m), job=_SwapJob(ffn_grads))
    parts_f, slots_f = pair_sums(ffn_names, ffn_grads, from_sibling)

    def merge_bwd_epi(accs, ex):
        dm, ga, gb, ya_, yb_ = accs[0], ex[0], ex[1], ex[2], ex[3]
        sa, sb = _sig(ga), _sig(gb)
        return [dm * ya_ * sa * (1.0 - sa), dm * yb_ * sb * (1.0 - sb), dm * sa, dm * sb]

    dga, dgb, dy_a, dy_b = _matmul(
        "d_merged", [(dmob, 'n', w_o, 't')], [0], S, D, D, tm2, tn_co, D,
        [(proj, 'tile', 7 * d_conv), (proj, 'tile', 7 * d_conv + D), (y_a, 'tile', 0), (y_b, 'tile', 0)],
        [(BF16, None)] * 4, merge_bwd_epi, rows_outer=True)
    dw_o, = _matmul("dw_o", [(merged, 't', dmob, 'n')], [0], D, D, S, tg, D, S, [], [(BF16, None)], _plain)
    dya, dob = _matmul("d_branch", [(dy_a, 'n', w_conv_out, 't'), (dy_b, 'n', w_hgrn_out, 't')], [0, 1], S, d_conv, D,
                       tm2, _pick(d_conv, 1024, 128), tn_co, [], [(F32, None), (F32, None)], _plain)
    dw_co, dw_ho = _matmul("dw_branch", [(ya, 't', dy_a, 'n'), (ob, 't', dy_b, 'n')], [0, 1], d_conv, D, S,
                           _pick(d_conv, 512, 128), tn_co, S, [], [(BF16, nb_co), (BF16, nb_co)], _plain)
    branch_names = ["w_conv_out", "w_hgrn_out", "w_o"]
    branch_grads = [dw_co, dw_ho, dw_o.reshape(N_CHIPS, -1, D)]
    dab, dac, dax, dconv_w, *from_sibling = _conv_bwd(dya, proj, conv_w, d_conv, job=_SwapJob(branch_grads))
    parts_b, slots_b = pair_sums(branch_names, branch_grads, from_sibling)
    early_names, mid_names = ffn_names[1:], ffn_names[:1] + branch_names
    dq, dz, dv, dgo, dlb, dgn, *arrived_e = _hgrn_bwd(dob, o_raw, states, qe, ke, dec, proj, lb, gnorm_g, d_conv,
                                                      job=_ScatterJob(parts_f[1:], slots_f[1:]))
    dproj = jnp.concatenate([dab, dac, dax, dq, dz, dv, dgo, dga, dgb], axis=1)
    halves_e = [_sum_chips("sum_chips_" + k, r, c_idx) for k, r in zip(early_names, arrived_e)]
    dw_in, *moved = _matmul("dw_in", [(h, 't', dproj, 'n')], [0], D, d_in, S, tw, tn_in, S, [], [(BF16, nb_in)], _plain,
                            job=_Jobs([_ShareJob(halves_e), _ScatterJob(parts_f[:1] + parts_b, slots_f[:1] + slots_b)]))
    whole_e, arrived_m = moved[:len(early_names)], moved[len(early_names):]
    parts_i, slots_i = pair_sums(["w_in"], [dw_in], _swap_halves("swap_in", [dw_in]))
    dh, arrived_in = _matmul("d_h", [(dproj, 'n', w_in, 't')], [0], S, D, d_in, tm2, D, tn_in, [], [(F32, None)], _plain,
                             job=_ScatterJob(parts_i, slots_i))
    dx, dsh_m, dsc_m, dg_mix = _norm_mod_bwd("norm_mix_bwd", dh, x, norm_mix_g, sc_m, dx1)
    dmod = jnp.concatenate([dsh_m, dsc_m, dgt_m, dsh_f, dsc_f, dgt_f], axis=1)
    small = dict(dmod=dmod, dg_mix=dg_mix, dg_ffn=dg_ffn, dg_final=dg_final, dlb=dlb,
                 dgn=jnp.sum(dgn, axis=0), dconv_w=dconv_w)
    late_names = mid_names + ["w_in"]
    late = _share_halves([_sum_chips("sum_chips_" + k, r, c_idx) for k, r in zip(late_names, arrived_m + [arrived_in])])
    big = dict(zip(early_names + late_names, whole_e + list(late)))
    return loss, dx, small, big


def _adamw_math(w, g, m, v):
    m = ADAM_B1 * m + (1.0 - ADAM_B1) * g
    v = ADAM_B2 * v + (1.0 - ADAM_B2) * (g * g)
    m_hat = m / (1.0 - ADAM_B1 ** ADAM_STEP)
    v_hat = v / (1.0 - ADAM_B2 ** ADAM_STEP)
    delta = -ADAM_LR * (m_hat / (jnp.sqrt(v_hat) + ADAM_EPS) + ADAM_WD * w)
    return delta, m, v


def _adamw(name, w, g, m, v):
    R, C = w.shape
    tr = _pick(R, max(8, (256 * 1024) // C // 8 * 8), 8)
    spec = pl.BlockSpec((tr, C), lambda i: (i, 0))

    def body(w_ref, g_ref, m_ref, v_ref, go_ref, d_ref, mo_ref, vo_ref):
        gv = g_ref[...]
        d, m2, v2 = _adamw_math(w_ref[...], gv, m_ref[...], v_ref[...])
        go_ref[...] = gv
        d_ref[...] = d
        mo_ref[...] = m2
        vo_ref[...] = v2

    return pl.pallas_call(
        body, name=name, grid=(R // tr,), in_specs=[spec] * 4, out_specs=[spec] * 4,
        out_shape=[jax.ShapeDtypeStruct((R, C), F32)] * 4, compiler_params=_params(("parallel",)),
    )(w, g, m, v)


def _ada_update(c_act_t, dmod, w, m, v):
    R, C = w.shape
    B = dmod.shape[0]
    tr = _pick(R, 256, 8)
    tc = _pick(C, 1536, 128)
    spec = pl.BlockSpec((tr, tc), lambda i, j: (i, j))

    def body(ct_ref, dm_ref, w_ref, m_ref, v_ref, g_ref, d_ref, mo_ref, vo_ref):
        ct = ct_ref[...]
        dm = dm_ref[...]
        g = ct[:, 0:1] * dm[0:1, :]
        for b in range(1, B):
            g = g + ct[:, b:b + 1] * dm[b:b + 1, :]
        d, m2, v2 = _adamw_math(w_ref[...], g, m_ref[...], v_ref[...])
        g_ref[...] = g
        d_ref[...] = d
        mo_ref[...] = m2
        vo_ref[...] = v2

    return pl.pallas_call(
        body, name="ada_update", grid=(R // tr, C // tc),
        in_specs=[pl.BlockSpec((tr, B), lambda i, j: (i, 0)), pl.BlockSpec((B, tc), lambda i, j: (0, j)),
                  spec, spec, spec],
        out_specs=[spec] * 4, out_shape=[jax.ShapeDtypeStruct((R, C), F32)] * 4,
        compiler_params=_params(("parallel", "parallel")),
    )(c_act_t, dmod, w, m, v)


def _prep(c_all, lb_param):
    def body(c_ref, p_ref, ca_ref, cab_ref, lb_ref):
        cv = c_ref[...]
        ca = cv * _sig(cv)
        ca_ref[...] = ca
        cab_ref[...] = ca.astype(BF16)
        lb_ref[...] = _sig(p_ref[0:1, :] - p_ref[1:2, :])

    return pl.pallas_call(
        body, name="prep",
        out_shape=[jax.ShapeDtypeStruct(c_all.shape, F32), jax.ShapeDtypeStruct(c_all.shape, BF16),
                   jax.ShapeDtypeStruct((1, lb_param.shape[1]), F32)],
    )(c_all, lb_param)


def _small_reduce(parts):
    W = parts.shape[1]

    def body(p_ref, o_ref):
        acc = p_ref[0:1, :]
        for d in range(1, N_DEV):
            acc = acc + p_ref[d:d + 1, :]
        o_ref[...] = acc

    return pl.pallas_call(body, name="small_reduce", out_shape=jax.ShapeDtypeStruct((1, W), F32))(parts)


def _lb_grad(dlb, lb):
    def body(d_ref, lb_ref, o_ref):
        t = d_ref[...] * lb_ref[...] * (1.0 - lb_ref[...])
        o_ref[0:1, :] = t
        o_ref[1:2, :] = -t

    return pl.pallas_call(body, name="lb_grad", out_shape=jax.ShapeDtypeStruct((2, dlb.shape[1]), F32))(dlb, lb)


def _place():
    x, y, c = lax.axis_index("x"), lax.axis_index("y"), lax.axis_index("c")
    chips = [(1 - x, y), (x, 1 - y), (1 - x, 1 - y)]
    return x, y, c, chips


def _allgather_rows(name, v):
    m_per, n = v.shape

    def body(x_ref, out_ref, send_sems, recv_sems, local_sem):
        x, y, c, chips = _place()
        me, sibling = (x, y, c), (x, y, 1 - c)

        def rows(px, py, pc):
            return out_ref.at[pl.ds((4 * px + 2 * py + pc) * m_per, m_per), :]

        def copy(k, block, to, src=None):
            return pltpu.make_async_remote_copy(
                src_ref=rows(*block) if src is None else src, dst_ref=rows(*block),
                send_sem=send_sems.at[k], recv_sem=recv_sems.at[k], device_id=to, device_id_type=MESH)

        mine = pltpu.make_async_copy(x_ref, rows(*me), local_sem)
        mine.start()
        first = [copy(0, me, sibling, src=x_ref)]
        first += [copy(1 + j, me, (*chip, c), src=x_ref) for j, chip in enumerate(chips)]
        for cp in first:
            cp.start()
        passed = [copy(4 + j, (*chip, c), sibling) for j, chip in enumerate(chips)]
        for j, chip in enumerate(chips):
            copy(1 + j, (*chip, c), me).wait_recv()
            passed[j].start()
        copy(0, sibling, me).wait_recv()
        for j, chip in enumerate(chips):
            copy(4 + j, (*chip, 1 - c), me).wait_recv()
        for cp in first + passed:
            cp.wait_send()
        mine.wait()

    return pl.pallas_call(
        body, name=name, out_shape=jax.ShapeDtypeStruct((N_DEV * m_per, n), v.dtype),
        in_specs=[pl.BlockSpec(memory_space=pltpu.VMEM)], out_specs=pl.BlockSpec(memory_space=pltpu.VMEM),
        scratch_shapes=[pltpu.SemaphoreType.DMA((7,)), pltpu.SemaphoreType.DMA((7,)), pltpu.SemaphoreType.DMA],
    )(v)


def _cast_place(name, w, k_idx):
    R, C = w.shape
    tr = _pick(R, max(16, (512 * 1024) // C // 16 * 16), 16)

    def body(k_ref, w_ref, o_ref):
        o_ref[...] = w_ref[...].astype(BF16)

    return pl.pallas_call(
        body, name=name,
        grid_spec=pltpu.PrefetchScalarGridSpec(
            num_scalar_prefetch=1, grid=(R // tr,),
            in_specs=[pl.BlockSpec((tr, C), lambda i, k_ref: (i, 0))],
            out_specs=pl.BlockSpec((None, tr, C), lambda i, k_ref: (k_ref[0], i, 0))),
        out_shape=jax.ShapeDtypeStruct((N_CHIPS, R, C), BF16),
        compiler_params=_params(("parallel",)),
    )(k_idx, w)


def _chip_copies(src_of, dst_of, got_of, n, sems, receiving):
    x, y, c, chips = _place()
    k_me = 2 * x + y
    send_sems, recv_sems = sems
    out = []
    for a in range(n):
        for j, chip in enumerate(chips):
            k_chip = 2 * chip[0] + chip[1]
            if receiving:
                src = dst = got_of(a, k_chip, c)
                to = (x, y, c)
            else:
                src, dst, to = src_of(a, k_chip, k_me, c), dst_of(a, k_me, c), (*chip, c)
            out.append(pltpu.make_async_remote_copy(
                src_ref=src, dst_ref=dst, send_sem=send_sems.at[3 * a + j], recv_sem=recv_sems.at[3 * a + j],
                device_id=to, device_id_type=MESH))
    return out


class _ChipJob:
    def start(self, j_in, j_out, sems):
        for send in self.copies(j_in, j_out, sems, False):
            send.start()

    def finish(self, j_in, j_out, sems):
        for recv in self.copies(j_in, j_out, sems, True):
            recv.wait_recv()
        for send in self.copies(j_in, j_out, sems, False):
            send.wait_send()


class _GatherJob(_ChipJob):
    def __init__(self, bufs, lo=0, hi=8):
        n = len(bufs)
        self.inputs, self.n_alias = list(bufs), n
        self.sem_shapes = [pltpu.SemaphoreType.DMA((3 * n,)), pltpu.SemaphoreType.DMA((3 * n,))]
        self.half = [b.shape[1] // 2 for b in bufs]
        self.lo, self.hi = lo, hi

    def copies(self, j_in, j_out, sems, receiving):
        def half(a, k, c):
            eighth = self.half[a] // 8
            return j_out[a].at[k, pl.ds(c * self.half[a] + self.lo * eighth, (self.hi - self.lo) * eighth)]

        return _chip_copies(lambda a, k_chip, k_me, c: half(a, k_me, c), half, half, len(self.half), sems, receiving)


class _ScatterJob(_ChipJob):
    def __init__(self, parts, slots):
        n = len(parts)
        self.n = n
        self.inputs, self.n_alias = list(parts) + list(slots), n
        self.sem_shapes = [pltpu.SemaphoreType.DMA((3 * n,)), pltpu.SemaphoreType.DMA((3 * n,))]

    def copies(self, j_in, j_out, sems, receiving):
        return _chip_copies(lambda a, k_chip, k_me, c: j_in[a].at[k_chip], lambda a, k_me, c: j_out[a].at[k_me],
                            lambda a, k_chip, c: j_out[a].at[k_chip], self.n, sems, receiving)


class _SwapJob(_ChipJob):
    def __init__(self, grads):
        n = len(grads)
        self.n = n
        self.half = [g.shape[1] // 2 for g in grads]
        landing = [lax.empty((N_CHIPS, g.shape[1] // 2, g.shape[2]), g.dtype) for g in grads]
        self.inputs, self.n_alias = list(grads) + landing, n
        self.sem_shapes = [pltpu.SemaphoreType.DMA((n,)), pltpu.SemaphoreType.DMA((n,))]

    def copies(self, j_in, j_out, sems, receiving):
        x, y, c, _ = _place()
        out = []
        for a in range(self.n):
            if receiving:
                src, to = j_out[a], (x, y, c)
            else:
                src, to = j_in[a].at[:, pl.ds((1 - c) * self.half[a], self.half[a])], (x, y, 1 - c)
            out.append(pltpu.make_async_remote_copy(src_ref=src, dst_ref=j_out[a], send_sem=sems[0].at[a],
                                                    recv_sem=sems[1].at[a], device_id=to, device_id_type=MESH))
        return out


class _Jobs:
    def __init__(self, jobs):
        self.jobs = jobs
        split = [(j.inputs[:len(j.inputs) - j.n_alias], j.inputs[len(j.inputs) - j.n_alias:]) for j in jobs]
        self.inputs = [a for plain, _ in split for a in plain] + [a for _, aliased in split for a in aliased]
        self.n_alias = sum(j.n_alias for j in jobs)
        self.sem_shapes = [s for j in jobs for s in j.sem_shapes]

    def _each(self, j_in, j_out, sems):
        n_plain = len(j_in) - self.n_alias
        p0 = a0 = s0 = 0
        for j in self.jobs:
            n_p, n_s = len(j.inputs) - j.n_alias, len(j.sem_shapes)
            ins = list(j_in[p0:p0 + n_p]) + list(j_in[n_plain + a0:n_plain + a0 + j.n_alias])
            yield j, ins, j_out[a0:a0 + j.n_alias], sems[s0:s0 + n_s]
            p0, a0, s0 = p0 + n_p, a0 + j.n_alias, s0 + n_s

    def start(self, j_in, j_out, sems):
        for j, ins, outs, s in self._each(j_in, j_out, sems):
            j.start(ins, outs, s)

    def finish(self, j_in, j_out, sems):
        for j, ins, outs, s in self._each(j_in, j_out, sems):
            j.finish(ins, outs, s)


class _ShareJob(_ChipJob):
    def __init__(self, bufs):
        n = len(bufs)
        self.n = n
        self.inputs, self.n_alias = list(bufs), n
        self.sem_shapes = [pltpu.SemaphoreType.DMA((n,)), pltpu.SemaphoreType.DMA((n,))]

    def copies(self, j_in, j_out, sems, receiving):
        x, y, c, _ = _place()
        out = []
        for a in range(self.n):
            hc, to = (1 - c, (x, y, c)) if receiving else (c, (x, y, 1 - c))
            out.append(pltpu.make_async_remote_copy(
                src_ref=j_out[a].at[hc], dst_ref=j_out[a].at[hc], send_sem=sems[0].at[a], recv_sem=sems[1].at[a],
                device_id=to, device_id_type=MESH))
        return out


class _ForwardJob(_ChipJob):
    def __init__(self, bufs):
        n = len(bufs)
        self.n = n
        self.half = [b.shape[1] // 2 for b in bufs]
        self.inputs, self.n_alias = list(bufs), n
        self.sem_shapes = [pltpu.SemaphoreType.DMA((3 * n,)), pltpu.SemaphoreType.DMA((3 * n,))]

    def copies(self, j_in, j_out, sems, receiving):
        x, y, c, chips = _place()
        out = []
        for a in range(self.n):
            for q, chip in enumerate(chips):
                hc, to = (1 - c, (x, y, c)) if receiving else (c, (x, y, 1 - c))
                part = j_out[a].at[2 * chip[0] + chip[1], pl.ds(hc * self.half[a], self.half[a])]
                out.append(pltpu.make_async_remote_copy(
                    src_ref=part, dst_ref=part, send_sem=sems[0].at[3 * a + q], recv_sem=sems[1].at[3 * a + q],
                    device_id=to, device_id_type=MESH))
        return out


def _forward_halves(name, bufs):
    n = len(bufs)

    def body(*refs):
        out_refs = refs[n:2 * n]
        send_sems, recv_sems = refs[2 * n:]
        x, y, c, chips = _place()
        started, waits = [], []
        for a in range(n):
            rh = bufs[a].shape[1] // 2
            for j, chip in enumerate(chips):
                k_chip = 2 * chip[0] + chip[1]
                got = out_refs[a].at[k_chip, pl.ds(c * rh, rh)]
                cp = pltpu.make_async_remote_copy(src_ref=got, dst_ref=got, send_sem=send_sems.at[3 * a + j],
                                                  recv_sem=recv_sems.at[3 * a + j], device_id=(x, y, 1 - c),
                                                  device_id_type=MESH)
                cp.start()
                started.append(cp)
                other = out_refs[a].at[k_chip, pl.ds((1 - c) * rh, rh)]
                waits.append(pltpu.make_async_remote_copy(
                    src_ref=other, dst_ref=other, send_sem=send_sems.at[3 * a + j], recv_sem=recv_sems.at[3 * a + j],
                    device_id=(x, y, c), device_id_type=MESH))
        for cp in waits:
            cp.wait_recv()
        for cp in started:
            cp.wait_send()

    return pl.pallas_call(
        body, name=name, out_shape=[jax.ShapeDtypeStruct(b.shape, b.dtype) for b in bufs],
        in_specs=[_HBM] * n, out_specs=[_HBM] * n, input_output_aliases={a: a for a in range(n)},
        scratch_shapes=[pltpu.SemaphoreType.DMA((3 * n,)), pltpu.SemaphoreType.DMA((3 * n,))],
    )(*bufs)


def _proj_gather(h, buf, order):
    S, D = h.shape
    nb = buf.shape[2]
    tm = _pick(S, 1024, 16)
    tn = _pick(nb, 1408, 128)
    per = nb // tn
    nj, ni = N_CHIPS * per, S // tm
    rh = D // 2
    seq = [(0, t) for t in range(per)] + [(p, t) for t in range(per) for p in (1, 2)] + [(3, t) for t in range(per)]
    tile_chip = jnp.stack([order[p] for p, _ in seq])
    tile_of = jnp.array([t for _, t in seq], jnp.int32)

    def body(chip_ref, t_ref, h_ref, buf_in, proj_ref, buf_ref, wbuf, tile_sems, ici_send, ici_recv, d2d_send, d2d_recv):
        j, i = pl.program_id(0), pl.program_id(1)
        x, y, c, chips = _place()
        k_me = 2 * x + y

        def part(k, hc, t):
            return buf_ref.at[k, pl.ds(hc * rh, rh), pl.ds(t * tn, tn)]

        def ici(q, t, receiving):
            k_chip = 2 * chips[q][0] + chips[q][1]
            src, to = (part(k_chip, c, t), (x, y, c)) if receiving else (part(k_me, c, t), (*chips[q], c))
            return pltpu.make_async_remote_copy(src_ref=src, dst_ref=src, send_sem=ici_send.at[q * per + t],
                                                recv_sem=ici_recv.at[q * per + t], device_id=to, device_id_type=MESH)

        def d2d(q, t, receiving):
            k_chip = 2 * chips[q][0] + chips[q][1]
            src, to = (part(k_chip, 1 - c, t), (x, y, c)) if receiving else (part(k_chip, c, t), (x, y, 1 - c))
            return pltpu.make_async_remote_copy(src_ref=src, dst_ref=src, send_sem=d2d_send.at[q * per + t],
                                                recv_sem=d2d_recv.at[q * per + t], device_id=to, device_id_type=MESH)

        def tile_copy(n):
            col = pl.multiple_of(t_ref[n] * tn, 128)
            return pltpu.make_async_copy(buf_ref.at[chip_ref[n], :, pl.ds(col, tn)], wbuf.at[n % 2], tile_sems.at[n % 2])

        @pl.when(jnp.logical_and(j == 0, i == 0))
        def _():
            for t in range(per):
                ici(0, t, False).start()
                ici(1, t, False).start()
            tile_copy(0).start()

        @pl.when(i == 0)
        def _():
            tile_copy(j).wait()
            for n in range(per, nj):
                p, t = seq[n]

                @pl.when(j + 1 == n)
                def _():
                    ici(p - 1, t, True).wait_recv()
                    d2d(p - 1, t, False).start()
                    if (p, t) == (1, per - 1):
                        for q in range(2):
                            for tt in range(per):
                                ici(q, tt, False).wait_send()
                        for tt in range(per):
                            ici(2, tt, False).start()
                    d2d(p - 1, t, True).wait_recv()

            @pl.when(j + 1 < nj)
            def _():
                tile_copy(j + 1).start()

        proj_ref[...] = jnp.dot(h_ref[...], wbuf[j % 2], preferred_element_type=F32)

        @pl.when(jnp.logical_and(j == nj - 1, i == ni - 1))
        def _():
            for t in range(per):
                ici(2, t, False).wait_send()
                for q in range(3):
                    d2d(q, t, False).wait_send()

    n_sems = 3 * per
    return pl.pallas_call(
        body, name="proj",
        grid_spec=pltpu.PrefetchScalarGridSpec(
            num_scalar_prefetch=2, grid=(nj, ni),
            in_specs=[pl.BlockSpec((tm, D), lambda j, i, kc, kt: (i, 0)), _HBM],
            out_specs=[pl.BlockSpec((tm, tn), lambda j, i, kc, kt: (i, kc[j] * per + kt[j])), _HBM],
            scratch_shapes=[pltpu.VMEM((2, D, tn), BF16), pltpu.SemaphoreType.DMA((2,))]
            + [pltpu.SemaphoreType.DMA((n_sems,))] * 4),
        out_shape=[jax.ShapeDtypeStruct((S, N_CHIPS * nb), F32), jax.ShapeDtypeStruct(buf.shape, buf.dtype)],
        input_output_aliases={3: 1}, compiler_params=_params(("arbitrary", "arbitrary")),
    )(tile_chip, tile_of, h, buf)


def _swap_halves(name, grads):
    n = len(grads)

    def body(*refs):
        in_refs, out_refs = refs[:n], refs[n:2 * n]
        send_sems, recv_sems = refs[2 * n:]
        x, y, c, _ = _place()
        cps = []
        for a in range(n):
            rh = grads[a].shape[1] // 2
            cp = pltpu.make_async_remote_copy(
                src_ref=in_refs[a].at[:, pl.ds((1 - c) * rh, rh)], dst_ref=out_refs[a],
                send_sem=send_sems.at[a], recv_sem=recv_sems.at[a], device_id=(x, y, 1 - c), device_id_type=MESH)
            cp.start()
            cps.append(cp)
        for cp in cps:
            cp.wait()

    return pl.pallas_call(
        body, name=name,
        out_shape=[jax.ShapeDtypeStruct((N_CHIPS, g.shape[1] // 2, g.shape[2]), g.dtype) for g in grads],
        in_specs=[_HBM] * n, out_specs=[_HBM] * n,
        scratch_shapes=[pltpu.SemaphoreType.DMA((n,)), pltpu.SemaphoreType.DMA((n,))],
    )(*grads)


def _pair_sum(name, g, q, ck_idx):
    _, R, C = g.shape
    rh = R // 2
    tr = _pick(rh, max(16, (512 * 1024) // C // 16 * 16), 16)
    nh = rh // tr

    def body(ck_ref, g_ref, q_ref, o_ref, own_ref):
        s = (g_ref[...].astype(F32) + q_ref[...].astype(F32)).astype(BF16)
        o_ref[...] = s

        @pl.when(pl.program_id(1) == ck_ref[1])
        def _():
            own_ref[...] = s

    return pl.pallas_call(
        body, name=name,
        grid_spec=pltpu.PrefetchScalarGridSpec(
            num_scalar_prefetch=1, grid=(nh, N_CHIPS),
            in_specs=[pl.BlockSpec((None, tr, C), lambda i, k, ck: (k, ck[0] * nh + i, 0)),
                      pl.BlockSpec((None, tr, C), lambda i, k, ck: (k, i, 0))],
            out_specs=[pl.BlockSpec((None, tr, C), lambda i, k, ck: (k, i, 0)),
                       pl.BlockSpec((None, tr, C), lambda i, k, ck: (ck[1], i, 0))]),
        out_shape=[jax.ShapeDtypeStruct((N_CHIPS, rh, C), BF16)] * 2,
        compiler_params=_params(("parallel", "arbitrary")),
    )(ck_idx, g, q)


def _sum_chips(name, r, c_idx):
    _, rh, C = r.shape
    tr = _pick(rh, max(16, (256 * 1024) // C // 16 * 16), 16)

    def body(c_ref, r_ref, o_ref):
        acc = r_ref[0].astype(F32)
        for k in range(1, N_CHIPS):
            acc = acc + r_ref[k].astype(F32)
        o_ref[...] = acc

    return pl.pallas_call(
        body, name=name,
        grid_spec=pltpu.PrefetchScalarGridSpec(
            num_scalar_prefetch=1, grid=(rh // tr,),
            in_specs=[pl.BlockSpec((N_CHIPS, tr, C), lambda i, c_ref: (0, i, 0))],
            out_specs=pl.BlockSpec((None, tr, C), lambda i, c_ref: (c_ref[0], i, 0))),
        out_shape=jax.ShapeDtypeStruct((2, rh, C), F32), compiler_params=_params(("parallel",)),
    )(c_idx, r)


def _share_halves(bufs):
    n = len(bufs)

    def body(*refs):
        out_refs = refs[n:2 * n]
        send_sems, recv_sems = refs[2 * n:]
        x, y, c, _ = _place()
        cps = []
        for a in range(n):
            cp = pltpu.make_async_remote_copy(
                src_ref=out_refs[a].at[c], dst_ref=out_refs[a].at[c], send_sem=send_sems.at[a],
                recv_sem=recv_sems.at[a], device_id=(x, y, 1 - c), device_id_type=MESH)
            cp.start()
            cps.append(cp)
        for a, cp in enumerate(cps):
            got = out_refs[a].at[1 - c]
            pltpu.make_async_remote_copy(src_ref=got, dst_ref=got, send_sem=send_sems.at[a], recv_sem=recv_sems.at[a],
                                         device_id=(x, y, c), device_id_type=MESH).wait_recv()
        for cp in cps:
            cp.wait_send()

    return pl.pallas_call(
        body, name="share_halves",
        out_shape=[jax.ShapeDtypeStruct(b.shape, b.dtype) for b in bufs],
        in_specs=[_HBM] * n, out_specs=[_HBM] * n, input_output_aliases={a: a for a in range(n)},
        scratch_shapes=[pltpu.SemaphoreType.DMA((n,)), pltpu.SemaphoreType.DMA((n,))],
    )(*bufs)


_BIG = ("w_in", "w_conv_out", "w_hgrn_out", "w_o", "w_ffn_gate", "w_ffn_up", "w_ffn_down")
_WEIGHTS = ("w_ada", "b_ada", "norm_mix_g", "w_in", "conv_w", "lb_param", "gnorm_g", "w_conv_out", "w_hgrn_out",
            "w_o", "norm_ffn_g", "w_ffn_gate", "w_ffn_up", "w_ffn_down", "norm_final_g")


def _pack_rows(pieces):
    flat = jnp.concatenate([p.reshape(-1) for p in pieces])
    pad = (-flat.shape[0]) % (SUBLANES * LANES)
    return jnp.pad(flat, (0, pad)).reshape(SUBLANES, -1)


def kernel(x, c, w_ada, b_ada, norm_mix_g, w_in, conv_w, lb_param, gnorm_g, w_conv_out, w_hgrn_out, w_o, norm_ffn_g, w_ffn_gate, w_ffn_up, w_ffn_down, norm_final_g, loss_target, m_w_ada, m_b_ada, m_norm_mix_g, m_w_in, m_conv_w, m_lb_param, m_gnorm_g, m_w_conv_out, m_w_hgrn_out, m_w_o, m_norm_ffn_g, m_w_ffn_gate, m_w_ffn_up, m_w_ffn_down, m_norm_final_g, v_w_ada, v_b_ada, v_norm_mix_g, v_w_in, v_conv_w, v_lb_param, v_gnorm_g, v_w_conv_out, v_w_hgrn_out, v_w_o, v_norm_ffn_g, v_w_ffn_gate, v_w_ffn_up, v_w_ffn_down, v_norm_final_g):
    w = dict(w_ada=w_ada, b_ada=b_ada, norm_mix_g=norm_mix_g, w_in=w_in, conv_w=conv_w, lb_param=lb_param,
             gnorm_g=gnorm_g, w_conv_out=w_conv_out, w_hgrn_out=w_hgrn_out, w_o=w_o, norm_ffn_g=norm_ffn_g,
             w_ffn_gate=w_ffn_gate, w_ffn_up=w_ffn_up, w_ffn_down=w_ffn_down, norm_final_g=norm_final_g)
    m = dict(w_ada=m_w_ada, b_ada=m_b_ada, norm_mix_g=m_norm_mix_g, w_in=m_w_in, conv_w=m_conv_w, lb_param=m_lb_param,
             gnorm_g=m_gnorm_g, w_conv_out=m_w_conv_out, w_hgrn_out=m_w_hgrn_out, w_o=m_w_o, norm_ffn_g=m_norm_ffn_g,
             w_ffn_gate=m_w_ffn_gate, w_ffn_up=m_w_ffn_up, w_ffn_down=m_w_ffn_down, norm_final_g=m_norm_final_g)
    v = dict(w_ada=v_w_ada, b_ada=v_b_ada, norm_mix_g=v_norm_mix_g, w_in=v_w_in, conv_w=v_conv_w, lb_param=v_lb_param,
             gnorm_g=v_gnorm_g, w_conv_out=v_w_conv_out, w_hgrn_out=v_w_hgrn_out, w_o=v_w_o, norm_ffn_g=v_norm_ffn_g,
             w_ffn_gate=v_w_ffn_gate, w_ffn_up=v_w_ffn_up, w_ffn_down=v_w_ffn_down, norm_final_g=v_norm_final_g)
    two_d = lambda a: a.reshape((-1, a.shape[-1])) if a.ndim != 2 else a
    shapes = {k: a.shape for k, a in w.items()}
    w, m, v = ({k: two_d(a) for k, a in t.items()} for t in (w, m, v))
    w["lb_param"], m["lb_param"], v["lb_param"] = lb_param, m_lb_param, v_lb_param
    S, D = x.shape[1], x.shape[2]
    d_conv = D // 2
    ix, iy, ic = lax.axis_index("x"), lax.axis_index("y"), lax.axis_index("c")
    k_me = 2 * ix + iy
    dev = 2 * k_me + ic
    cw_shard = w["conv_w"].shape[1]
    ada_cols = w["w_ada"].shape[1]

    got = _allgather_rows("gather_cond", _pack_rows([c, w["conv_w"]])).reshape(N_DEV, -1)
    c_all = got[:, :D]
    conv_full = got[::2, D:D + 3 * cw_shard].reshape(N_CHIPS, 3, cw_shard).transpose(1, 0, 2).reshape(3, -1)
    c_act, c_act_b, lb = _prep(c_all, lb_param)
    b_cols = lax.dynamic_slice(w["b_ada"], (0, k_me * ada_cols), (1, ada_cols))
    mod_cols, = _matmul("ada_fwd", [(c_act_b, 'n', w["w_ada"].astype(BF16), 'n')], [0], N_DEV, ada_cols, D,
                        N_DEV, _pick(ada_cols, 1536, 128), D, [(b_cols, 'row', 0)], [(F32, None)],
                        lambda accs, ex: [accs[0] + ex[0]])
    mod_all = _allgather_rows("gather_mod", mod_cols).reshape(N_CHIPS, 2, N_DEV, ada_cols)[:, 0]
    mod_all = mod_all.transpose(1, 0, 2).reshape(N_DEV, -1)
    mod = lax.dynamic_slice(mod_all, (dev, 0), (1, mod_all.shape[1]))
    k_idx = jnp.reshape(k_me, (1,)).astype(jnp.int32)
    c_idx = jnp.reshape(ic, (1,)).astype(jnp.int32)
    bufs = {k: _cast_place("cast_" + k, w[k], k_idx) for k in _BIG}
    order = jnp.stack([k_me, 2 * (1 - ix) + iy, 2 * ix + (1 - iy), 2 * (1 - ix) + (1 - iy)]).astype(jnp.int32)

    loss, dx, small, arrived = _local_step(
        x[0], loss_target[0], mod, w["norm_mix_g"], lb, w["gnorm_g"], w["norm_ffn_g"], w["norm_final_g"], conv_full,
        bufs, c_idx, order)

    pieces = [small["dmod"], small["dg_mix"], small["dg_ffn"], small["dg_final"], small["dlb"], small["dgn"],
              small["dconv_w"], loss]
    sizes = [p.size for p in pieces]
    parts = _allgather_rows("gather_small", _pack_rows(pieces)).reshape(N_DEV, -1)
    total = _small_reduce(parts)
    offs = [0]
    for s in sizes:
        offs.append(offs[-1] + s)
    tot = [total[:, offs[i]:offs[i + 1]] for i in range(len(sizes))]
    dmod_all = parts[:, :sizes[0]]
    grads = {
        "b_ada": tot[0], "norm_mix_g": tot[1], "norm_ffn_g": tot[2], "norm_final_g": tot[3],
        "lb_param": _lb_grad(tot[4], lb), "gnorm_g": tot[5],
        "conv_w": lax.dynamic_slice(tot[6].reshape(3, -1), (0, k_me * cw_shard), (3, cw_shard)),
    }
    loss_out = tot[7][0, 0]

    for k in _BIG:
        grads[k] = arrived[k].reshape(-1, arrived[k].shape[-1])

    delta, new_m, new_v = {}, {}, {}
    dmod_cols = lax.dynamic_slice(dmod_all, (0, k_me * ada_cols), (N_DEV, ada_cols))
    grads["w_ada"], delta["w_ada"], new_m["w_ada"], new_v["w_ada"] = _ada_update(
        c_act.T, dmod_cols, w["w_ada"], m["w_ada"], v["w_ada"])
    for k in _WEIGHTS:
        if k != "w_ada":
            grads[k], delta[k], new_m[k], new_v[k] = _adamw("adamw_" + k, w[k], grads[k], m[k], v[k])
    outs = [loss_out, dx.reshape(x.shape)]
    for t in (grads, delta, new_m, new_v):
        outs += [t[k].reshape(shapes[k]) for k in _WEIGHTS]
    return tuple(outs)
```

```python
import functools

import jax
import jax.numpy as jnp
from jax import lax
from jax.experimental import pallas as pl
from jax.experimental.pallas import tpu as pltpu

F32 = jnp.float32
BF16 = jnp.bfloat16
MESH = pl.DeviceIdType.MESH

EPS = 1e-6
HEAD = 128
BLK = 16
N_CHIPS = 4
N_DEV = 8
SUBLANES, LANES = 8, 128
VMEM_LIMIT_BYTES = 56 * 1024 * 1024

ADAM_LR = 0.001
ADAM_B1 = 0.9
ADAM_B2 = 0.999
ADAM_EPS = 1e-08
ADAM_WD = 0.01
ADAM_STEP = 10


def _pick(dim, pref, mult):
    if dim <= pref:
        return dim
    t = (pref // mult) * mult
    while t >= mult:
        if dim % t == 0:
            return t
        t -= mult
    return dim


def _sig(x):
    return 1.0 / (1.0 + jnp.exp(-x))


def _params(sem):
    return pltpu.CompilerParams(dimension_semantics=sem, vmem_limit_bytes=VMEM_LIMIT_BYTES)


def _gj(j, i, k):
    return j


def _gk(j, i, k):
    return k


def _wspec(arr, rt, ct, r_of, c_of):
    if arr.ndim == 2:
        return pl.BlockSpec((rt, ct), lambda j, i, k: (r_of(j, i, k), c_of(j, i, k)))
    per = arr.shape[2] // ct
    assert arr.shape[2] % ct == 0
    return pl.BlockSpec((None, rt, ct),
                        lambda j, i, k: (c_of(j, i, k) // per, r_of(j, i, k), c_of(j, i, k) % per))


_HBM = pl.BlockSpec(memory_space=pltpu.HBM)


def _host(job, grid, in_specs, args, out_specs, out_shape, scratch):
    if job is None:
        return {}, (lambda body: body)
    n_in, n_out, n_scr = len(args), len(out_shape), len(scratch)
    n_job, n_alias = len(job.inputs), job.n_alias
    in_specs += [_HBM] * n_job
    args += job.inputs
    out_specs += [_HBM] * n_alias
    out_shape += [jax.ShapeDtypeStruct(a.shape, a.dtype) for a in job.inputs[n_job - n_alias:]]
    scratch += job.sem_shapes
    aliases = {n_in + n_job - n_alias + t: n_out + t for t in range(n_alias)}

    def wrap(body):
        def hosted(*refs):
            ins, refs = refs[:n_in], refs[n_in:]
            j_in, refs = refs[:n_job], refs[n_job:]
            outs, refs = refs[:n_out], refs[n_out:]
            j_out, refs = refs[:n_alias], refs[n_alias:]
            scr, sems = refs[:n_scr], refs[n_scr:]
            first = functools.reduce(jnp.logical_and, [pl.program_id(d) == 0 for d in range(len(grid))])
            last = functools.reduce(jnp.logical_and, [pl.program_id(d) == grid[d] - 1 for d in range(len(grid))])

            @pl.when(first)
            def _():
                job.start(j_in, j_out, sems)

            body(*ins, *outs, *scr)

            @pl.when(last)
            def _():
                job.finish(j_in, j_out, sems)

        return hosted

    return aliases, wrap


EPILOGUE_COLS = 768


def _plain(accs, extras):
    return accs


def _matmul(name, pairs, acc_of, M, N, K, tm, tn, tk, extras, outs, epilogue, job=None, rows_outer=False):
    assert M % tm == 0 and N % tn == 0 and K % tk == 0, (name, M, N, K, tm, tn, tk)
    nj, ni, nk = N // tn, M // tm, K // tk
    n_pairs, n_extra, n_out = len(pairs), len(extras), len(outs)
    n_acc = max(acc_of) + 1
    in_specs, args, dims = [], [], []
    lhs_of, seen = [], {}
    for a, am, b, bm in pairs:
        if (id(a), am) not in seen:
            seen[id(a), am] = len(args)
            args.append(a)
            if am == 'n':
                in_specs.append(pl.BlockSpec((tm, tk), lambda j, i, k: (i, k)))
            else:
                in_specs.append(pl.BlockSpec((tk, tm), lambda j, i, k: (k, i)))
        lhs_of.append(seen[id(a), am])
    n_lhs = len(args)
    for a, am, b, bm in pairs:
        if bm == 'n':
            in_specs.append(_wspec(b, tk, tn, _gk, _gj))
        else:
            in_specs.append(_wspec(b, tn, tk, _gj, _gk))
        dims.append((((1 if am == 'n' else 0,), (0 if bm == 'n' else 1,)), ((), ())))
        args.append(b)
    for e, kind, off in extras:
        assert off % tn == 0, (name, off, tn)
        o = off // tn
        if kind == 'tile':
            in_specs.append(pl.BlockSpec((tm, tn), lambda j, i, k, o=o: (i, j + o)))
        else:
            in_specs.append(pl.BlockSpec((1, tn), lambda j, i, k, o=o: (0, j + o)))
        args.append(e)
    out_shape, out_specs = [], []
    for dt, nb in outs:
        if nb is None:
            out_shape.append(jax.ShapeDtypeStruct((M, N), dt))
            out_specs.append(pl.BlockSpec((tm, tn), lambda j, i, k: (i, j)))
        else:
            assert nb % tn == 0 and N % nb == 0
            per = nb // tn
            out_shape.append(jax.ShapeDtypeStruct((N // nb, M, nb), dt))
            out_specs.append(pl.BlockSpec((None, tm, tn), lambda j, i, k, per=per: (j // per, i, j % per)))

    def body(*refs):
        n_ab = n_lhs + n_pairs
        e_refs = refs[n_ab:n_ab + n_extra]
        o_refs = refs[n_ab + n_extra:n_ab + n_extra + n_out]
        acc_refs = refs[n_ab + n_extra + n_out:]

        def products(cols):
            sums = [None] * n_acc
            for p in range(n_pairs):
                b_ref = refs[n_lhs + p]
                b = b_ref[:, cols] if pairs[p][3] == 'n' else b_ref[cols, :]
                prod = lax.dot_general(refs[lhs_of[p]][...], b, dims[p], preferred_element_type=F32)
                a = acc_of[p]
                sums[a] = prod if sums[a] is None else sums[a] + prod
            return sums

        def finish(accs, cols):
            res = epilogue(accs, [e[:, cols] for e in e_refs])
            for o_ref, r in zip(o_refs, res):
                o_ref[:, cols] = r.astype(o_ref.dtype)

        whole = slice(0, tn)
        if nk == 1:
            step = tn if epilogue is _plain else EPILOGUE_COLS
            for c0 in range(0, tn, step):
                cols = slice(c0, min(c0 + step, tn))
                finish(products(cols), cols)
            return
        sums = products(whole)
        k = pl.program_id(2)
        targets = o_refs if sum_in_outs else acc_refs

        @pl.when(k == 0)
        def _():
            for a in range(n_acc):
                targets[a][...] = sums[a]

        @pl.when(k > 0)
        def _():
            for a in range(n_acc):
                targets[a][...] += sums[a]

        if not sum_in_outs:
            @pl.when(k == nk - 1)
            def _():
                finish([acc_refs[a][...] for a in range(n_acc)], whole)

    sum_in_outs = epilogue is _plain and nk > 1 and n_out == n_acc and all(dt == F32 for dt, _ in outs)
    scratch = [pltpu.VMEM((tm, tn), F32) for _ in range(n_acc)] if nk > 1 and not sum_in_outs else []
    grid = (nj, ni, nk)
    if rows_outer:
        grid = (ni, nj, nk)
        swap = lambda spec: pl.BlockSpec(spec.block_shape, lambda i, j, k, f=spec.index_map: f(j, i, k))
        in_specs, out_specs = [swap(s) for s in in_specs], [swap(s) for s in out_specs]
    aliases, wrap = _host(job, grid, in_specs, args, out_specs, out_shape, scratch)
    sem = ("parallel", "parallel", "arbitrary") if job is None else ("arbitrary",) * 3
    return pl.pallas_call(
        wrap(body), name=name, grid=grid, in_specs=in_specs, out_specs=out_specs, out_shape=out_shape,
        scratch_shapes=scratch, input_output_aliases=aliases, compiler_params=_params(sem),
    )(*args)


def _row_spec(tr, d):
    return pl.BlockSpec((tr, d), lambda i: (i, 0))


def _vec_spec(d):
    return pl.BlockSpec((1, d), lambda i: (0, 0))


def _norm_mod(name, x, g, sc, sh, job=None):
    S, D = x.shape
    tr = _pick(S, 256, 8)

    def body(x_ref, g_ref, sc_ref, sh_ref, h_ref):
        xv = x_ref[...]
        r = lax.rsqrt(jnp.mean(xv * xv, axis=-1, keepdims=True) + EPS)
        h_ref[...] = ((xv * r) * g_ref[...] * (1.0 + sc_ref[...]) + sh_ref[...]).astype(BF16)

    grid = (S // tr,)
    in_specs = [_row_spec(tr, D), _vec_spec(D), _vec_spec(D), _vec_spec(D)]
    args = [x, g, sc, sh]
    out_specs, out_shape, scratch = [_row_spec(tr, D)], [jax.ShapeDtypeStruct((S, D), BF16)], []
    aliases, wrap = _host(job, grid, in_specs, args, out_specs, out_shape, scratch)
    res = pl.pallas_call(
        wrap(body), name=name, grid=grid, in_specs=in_specs, out_specs=out_specs, out_shape=out_shape,
        scratch_shapes=scratch, input_output_aliases=aliases,
        compiler_params=_params(("parallel" if job is None else "arbitrary",)),
    )(*args)
    return res[0] if job is None else res


def _loss_head(x2, target, g, ff, gt):
    S, D = x2.shape
    tr = _pick(S, 256, 8)

    def body(x_ref, t_ref, g_ref, ff_ref, gt_ref, dx_ref, dffb_ref, loss_ref, dgt_ref, dg_ref):
        i = pl.program_id(0)

        @pl.when(i == 0)
        def _():
            loss_ref[...] = jnp.zeros_like(loss_ref)
            dgt_ref[...] = jnp.zeros_like(dgt_ref)
            dg_ref[...] = jnp.zeros_like(dg_ref)

        xv = x_ref[...]
        gv = g_ref[...]
        r = lax.rsqrt(jnp.mean(xv * xv, axis=-1, keepdims=True) + EPS)
        xh = xv * r
        err = xh * gv - t_ref[...]
        loss_ref[...] += 0.5 * jnp.sum(jnp.mean(err * err, axis=-1, keepdims=True))
        dy = err * (1.0 / D)
        dg_ref[...] += jnp.sum(dy * xh, axis=0, keepdims=True)
        dxh = dy * gv
        dx = r * (dxh - xh * jnp.mean(dxh * xh, axis=-1, keepdims=True))
        dx_ref[...] = dx
        dffb_ref[...] = (dx * gt_ref[...]).astype(BF16)
        dgt_ref[...] += jnp.sum(dx * ff_ref[...], axis=0, keepdims=True)

    return pl.pallas_call(
        body, name="loss_head", grid=(S // tr,),
        in_specs=[_row_spec(tr, D), _row_spec(tr, D), _vec_spec(D), _row_spec(tr, D), _vec_spec(D)],
        out_specs=[_row_spec(tr, D), _row_spec(tr, D), pl.BlockSpec((1, 128), lambda i: (0, 0)),
                   _vec_spec(D), _vec_spec(D)],
        out_shape=[jax.ShapeDtypeStruct((S, D), F32), jax.ShapeDtypeStruct((S, D), BF16),
                   jax.ShapeDtypeStruct((1, 128), F32), jax.ShapeDtypeStruct((1, D), F32),
                   jax.ShapeDtypeStruct((1, D), F32)],
        compiler_params=_params(("arbitrary",)),
    )(x2, target, g, ff, gt)


def _norm_mod_bwd(name, dh, x, g, sc, dres, gated=None, job=None):
    S, D = x.shape
    tr = _pick(S, 256, 8)
    n = S // tr
    with_gate = gated is not None

    def body(*refs):
        if with_gate:
            dh_ref, x_ref, g_ref, sc_ref, dres_ref, mo_ref, gt_ref, dx_ref, dsh_ref, dsc_ref, dg_ref, dmob_ref, dgt_ref = refs
        else:
            dh_ref, x_ref, g_ref, sc_ref, dres_ref, dx_ref, dsh_ref, dsc_ref, dg_ref = refs
        i = pl.program_id(0)

        @pl.when(i == 0)
        def _():
            dsh_ref[...] = jnp.zeros_like(dsh_ref)
            dsc_ref[...] = jnp.zeros_like(dsc_ref)
            if with_gate:
                dgt_ref[...] = jnp.zeros_like(dgt_ref)

        xv = x_ref[...]
        dhv = dh_ref[...]
        gv = g_ref[...]
        scale = 1.0 + sc_ref[...]
        r = lax.rsqrt(jnp.mean(xv * xv, axis=-1, keepdims=True) + EPS)
        xh = xv * r
        dsh_ref[...] += jnp.sum(dhv, axis=0, keepdims=True)
        dsc_ref[...] += jnp.sum(dhv * xh, axis=0, keepdims=True)
        dxh = dhv * (gv * scale)
        dx = dres_ref[...] + r * (dxh - xh * jnp.mean(dxh * xh, axis=-1, keepdims=True))
        dx_ref[...] = dx
        if with_gate:
            dmob_ref[...] = (dx * gt_ref[...]).astype(BF16)
            dgt_ref[...] += jnp.sum(dx * mo_ref[...], axis=0, keepdims=True)

        @pl.when(i == n - 1)
        def _():
            t = dsc_ref[...]
            dg_ref[...] = t * scale
            dsc_ref[...] = t * gv

    in_specs = [_row_spec(tr, D), _row_spec(tr, D), _vec_spec(D), _vec_spec(D), _row_spec(tr, D)]
    args = [dh, x, g, sc, dres]
    out_specs = [_row_spec(tr, D), _vec_spec(D), _vec_spec(D), _vec_spec(D)]
    out_shape = [jax.ShapeDtypeStruct((S, D), F32)] + [jax.ShapeDtypeStruct((1, D), F32)] * 3
    if with_gate:
        in_specs += [_row_spec(tr, D), _vec_spec(D)]
        args += list(gated)
        out_specs += [_row_spec(tr, D), _vec_spec(D)]
        out_shape += [jax.ShapeDtypeStruct((S, D), BF16), jax.ShapeDtypeStruct((1, D), F32)]
    scratch = []
    aliases, wrap = _host(job, (n,), in_specs, args, out_specs, out_shape, scratch)
    return pl.pallas_call(
        wrap(body), name=name, grid=(n,), in_specs=in_specs, out_specs=out_specs, out_shape=out_shape,
        scratch_shapes=scratch, input_output_aliases=aliases, compiler_params=_params(("arbitrary",)),
    )(*args)


CONV_ROWS = 256


def _col_spec(S, off_cols):
    o = off_cols // HEAD
    return pl.BlockSpec((S, HEAD), lambda c, o=o: (0, c + o))


def _conv_taps(u, u_prev, rid):
    s1 = jnp.where(rid >= 1, pltpu.roll(u, 1, 0), pltpu.roll(u_prev, 1, 0))
    s2 = jnp.where(rid >= 2, pltpu.roll(u, 2, 0), pltpu.roll(u_prev, 2, 0))
    return s1, s2


def _conv_fwd(proj, conv_w, d_conv, job=None):
    S = proj.shape[0]
    R = min(CONV_ROWS, S)
    nr = S // R

    def body(ab_ref, ac_ref, ax_ref, w_ref, ya_ref):
        w0, w1, w2 = w_ref[0:1, :], w_ref[1:2, :], w_ref[2:3, :]
        rid = lax.broadcasted_iota(jnp.int32, (R, HEAD), 0)

        def step(c, carry):
            rows = pl.ds(pl.multiple_of(c * R, R), R)
            prev = pl.ds(pl.multiple_of(jnp.maximum(c - 1, 0) * R, R), R)
            u = ac_ref[rows, :] * ax_ref[rows, :]
            u_prev = jnp.where(c > 0, ac_ref[prev, :] * ax_ref[prev, :], 0.0)
            s1, s2 = _conv_taps(u, u_prev, rid)
            y = w0 * s2 + w1 * s1 + w2 * u
            ya_ref[rows, :] = (ab_ref[rows, :] * y).astype(BF16)
            return carry

        lax.fori_loop(0, nr, step, 0)

    grid = (d_conv // HEAD,)
    in_specs = [_col_spec(S, 0), _col_spec(S, d_conv), _col_spec(S, 2 * d_conv), pl.BlockSpec((3, HEAD), lambda c: (0, c))]
    args = [proj, proj, proj, conv_w]
    out_specs, out_shape = [pl.BlockSpec((S, HEAD), lambda c: (0, c))], [jax.ShapeDtypeStruct((S, d_conv), BF16)]
    scratch = []
    aliases, wrap = _host(job, grid, in_specs, args, out_specs, out_shape, scratch)
    res = pl.pallas_call(
        wrap(body), name="conv_fwd", grid=grid, in_specs=in_specs, out_specs=out_specs, out_shape=out_shape,
        scratch_shapes=scratch, input_output_aliases=aliases,
        compiler_params=_params(("parallel" if job is None else "arbitrary",)),
    )(*args)
    return res[0] if job is None else res


def _conv_bwd(dya, proj, conv_w, d_conv, job=None):
    S = proj.shape[0]
    R = min(CONV_ROWS, S)
    nr = S // R

    def body(dya_ref, ab_ref, ac_ref, ax_ref, w_ref, dab_ref, dac_ref, dax_ref, dw_ref):
        w0, w1, w2 = w_ref[0:1, :], w_ref[1:2, :], w_ref[2:3, :]
        rid = lax.broadcasted_iota(jnp.int32, (R, HEAD), 0)

        def step(c, carry):
            dw0, dw1, dw2 = carry
            rows = pl.ds(pl.multiple_of(c * R, R), R)
            prev = pl.ds(pl.multiple_of(jnp.maximum(c - 1, 0) * R, R), R)
            nxt = pl.ds(pl.multiple_of(jnp.minimum(c + 1, nr - 1) * R, R), R)
            a_c, a_x, a_b = ac_ref[rows, :], ax_ref[rows, :], ab_ref[rows, :]
            u = a_c * a_x
            u_prev = jnp.where(c > 0, ac_ref[prev, :] * ax_ref[prev, :], 0.0)
            s1, s2 = _conv_taps(u, u_prev, rid)
            y = w0 * s2 + w1 * s1 + w2 * u
            dy = dya_ref[rows, :]
            dab_ref[rows, :] = (dy * y).astype(BF16)
            dyc = dy * a_b
            dyc_next = jnp.where(c < nr - 1, dya_ref[nxt, :] * ab_ref[nxt, :], 0.0)
            t1 = jnp.where(rid < R - 1, pltpu.roll(dyc, R - 1, 0), pltpu.roll(dyc_next, R - 1, 0))
            t2 = jnp.where(rid < R - 2, pltpu.roll(dyc, R - 2, 0), pltpu.roll(dyc_next, R - 2, 0))
            du = w2 * dyc + w1 * t1 + w0 * t2
            dac_ref[rows, :] = (du * a_x).astype(BF16)
            dax_ref[rows, :] = (du * a_c).astype(BF16)
            dw0 = dw0 + jnp.sum(dyc * s2, axis=0, keepdims=True)
            dw1 = dw1 + jnp.sum(dyc * s1, axis=0, keepdims=True)
            dw2 = dw2 + jnp.sum(dyc * u, axis=0, keepdims=True)
            return dw0, dw1, dw2

        z = jnp.zeros((1, HEAD), F32)
        dw0, dw1, dw2 = lax.fori_loop(0, nr, step, (z, z, z))
        dw_ref[0:1, :] = dw0
        dw_ref[1:2, :] = dw1
        dw_ref[2:3, :] = dw2

    col = pl.BlockSpec((S, HEAD), lambda c: (0, c))
    grid = (d_conv // HEAD,)
    in_specs = [col, _col_spec(S, 0), _col_spec(S, d_conv), _col_spec(S, 2 * d_conv),
                pl.BlockSpec((3, HEAD), lambda c: (0, c))]
    args = [dya, proj, proj, proj, conv_w]
    out_specs = [col, col, col, pl.BlockSpec((3, HEAD), lambda c: (0, c))]
    out_shape = [jax.ShapeDtypeStruct((S, d_conv), BF16)] * 3 + [jax.ShapeDtypeStruct((3, d_conv), F32)]
    scratch = []
    aliases, wrap = _host(job, grid, in_specs, args, out_specs, out_shape, scratch)
    return pl.pallas_call(
        wrap(body), name="conv_bwd", grid=grid, in_specs=in_specs, out_specs=out_specs, out_shape=out_shape,
        scratch_shapes=scratch, input_output_aliases=aliases,
        compiler_params=_params(("parallel" if job is None else "arbitrary",)),
    )(*args)


HGRN_FWD_ROWS = 512
HGRN_BWD_ROWS = 1024
HGRN_FWD_SUB = 64
HGRN_BWD_SUB = 32
HGRN_GATE_ROWS = 128


def _block_cumsum(x, pos):
    for s in (1, 2, 4, 8):
        x = x + jnp.where(pos >= s, pltpu.roll(x, s, 0), 0.0)
    return x


def _block_rev_cumsum(x, pos, rows):
    for s in (1, 2, 4, 8):
        x = x + jnp.where(pos < BLK - s, pltpu.roll(x, rows - s, 0), 0.0)
    return x


def _block_last(x, pos, rows):
    m = jnp.where(pos == BLK - 1, x, 0.0)
    for s in (1, 2, 4, 8):
        m = m + pltpu.roll(m, rows - s, 0)
    return m


def _hgrn_gates(q, z, lb):
    sgq = _sig(q)
    qs = q * sgq
    sg = _sig(z)
    f = lb + (1.0 - lb) * sg
    kk = (1.0 - lb) * (1.0 - sg)
    return sgq, qs, sg, f, kk


def _hgrn_decays(q, z, lb, pos, rows):
    sgq, qs, sg, f, kk = _hgrn_gates(q, z, lb)
    b = _block_cumsum(jnp.log(f), pos)
    return sgq, qs, sg, f, kk, b, _block_last(b, pos, rows)


GROUP = HEAD // BLK


def _spread(x, blk):
    zero = jnp.zeros_like(x)
    return jnp.concatenate([jnp.where(blk == j, x, zero) for j in range(GROUP)], axis=1)


def _block_products(a_ref, m_ref, out_ref, nb, transposed, accumulate=False):
    rows = GROUP * BLK
    blk = lax.broadcasted_iota(jnp.int32, (rows, HEAD), 0) // BLK
    for g in range(nb // GROUP):
        r = pl.ds(g * rows, rows)
        a = _spread(a_ref[r, :], blk)
        if transposed:
            m = jnp.concatenate([m_ref[g * GROUP + j] for j in range(GROUP)], axis=1)
            dims = (((1,), (1,)), ((), ()))
        else:
            m = m_ref[pl.ds(g * GROUP, GROUP)].reshape(GROUP * HEAD, HEAD)
            dims = (((1,), (0,)), ((), ()))
        prod = lax.dot_general(a, m, dims, preferred_element_type=F32)
        out_ref[r, :] = out_ref[r, :] + prod if accumulate else prod


def _block_outer_products(a_ref, b_ref, u_ref, nb):
    rows = GROUP * BLK
    blk = lax.broadcasted_iota(jnp.int32, (rows, HEAD), 0) // BLK
    zero = jnp.zeros((rows, HEAD), BF16)
    for g in range(nb // GROUP):
        r = pl.ds(g * rows, rows)
        b = b_ref[r, :]
        spread = jnp.concatenate([jnp.where(blk == j, b, zero) for j in range(GROUP)], axis=1)
        u = lax.dot_general(a_ref[r, :], spread, (((0,), (0,)), ((), ())), preferred_element_type=F32)
        for j in range(GROUP):
            u_ref[g * GROUP + j] = u[:, j * HEAD:(j + 1) * HEAD]


def _hgrn_specs(T, d_conv, n_t, rev):
    def t_of(i):
        return (n_t - 1 - i) if rev else i

    def pcol(off):
        o = off // HEAD
        return pl.BlockSpec((T, HEAD), lambda h, i, o=o: (t_of(i), h + o))

    q_spec, z_spec, v_spec, g_spec = pcol(3 * d_conv), pcol(4 * d_conv), pcol(5 * d_conv), pcol(6 * d_conv)
    lb_spec = pl.BlockSpec((1, HEAD), lambda h, i: (0, h))
    gn_spec = pl.BlockSpec((1, HEAD), lambda h, i: (0, 0))
    act_spec = pl.BlockSpec((T, HEAD), lambda h, i: (t_of(i), h))
    st_spec = pl.BlockSpec((None, T // BLK, HEAD, HEAD), lambda h, i: (h, t_of(i), 0, 0))
    return q_spec, z_spec, v_spec, g_spec, lb_spec, gn_spec, act_spec, st_spec


def _hgrn_fwd(proj, lb, gn, d_conv, job=None):
    S = proj.shape[0]
    H = d_conv // HEAD
    T = min(HGRN_FWD_ROWS, S)
    n_t, nb = S // T, T // BLK
    sub = min(HGRN_FWD_SUB, T)
    q_spec, z_spec, v_spec, g_spec, lb_spec, gn_spec, act_spec, st_spec = _hgrn_specs(T, d_conv, n_t, False)

    def body(q_ref, z_ref, v_ref, g_ref, lb_ref, gn_ref, ob_ref, o_ref, st_ref, qe_s, ke_s, dec_s, state, v_s, o_s, u_s):
        @pl.when(pl.program_id(1) == 0)
        def _():
            state[...] = jnp.zeros_like(state)

        pos = lax.broadcasted_iota(jnp.int32, (sub, HEAD), 0) % BLK
        lbv = lb_ref[...]

        def vector_pass(s, carry):
            r = pl.ds(pl.multiple_of(s * sub, sub), sub)
            v = v_ref[r, :]
            _, qs, _, _, kk, b, b_tot = _hgrn_decays(q_ref[r, :], z_ref[r, :], lbv, pos, sub)
            qe_s[r, :] = (qs * jnp.exp(b)).astype(BF16)
            ke_s[r, :] = (kk * jnp.exp(b_tot - b)).astype(BF16)
            v_s[r, :] = v.astype(BF16)
            dec_s[r, :] = jnp.exp(b_tot)
            o = jnp.sum(qs * kk, axis=-1, keepdims=True) * v
            for d in range(1, BLK):
                e = jnp.exp(b - pltpu.roll(b, d, 0))
                a = jnp.sum(qs * pltpu.roll(kk, d, 0) * e, axis=-1, keepdims=True)
                o = o + jnp.where(pos >= d, a * pltpu.roll(v, d, 0), 0.0)
            o_s[r, :] = o
            return carry

        lax.fori_loop(0, T // sub, vector_pass, 0)

        _block_outer_products(v_s, ke_s, u_s, nb)
        st = state[...]
        for j in range(nb):
            st_ref[j] = st.astype(BF16)
            st = st * dec_s[pl.ds(j * BLK, 1), :] + u_s[j]
        state[...] = st
        _block_products(qe_s, st_ref, o_s, nb, True, accumulate=True)
        o = o_s[...]
        o_ref[...] = o
        r = lax.rsqrt(jnp.mean(o * o, axis=-1, keepdims=True) + EPS)
        g = g_ref[...]
        ob_ref[...] = ((o * r) * gn_ref[...] * (g * _sig(g))).astype(BF16)

    grid = (H, n_t)
    in_specs = [q_spec, z_spec, v_spec, g_spec, lb_spec, gn_spec]
    args = [proj, proj, proj, proj, lb, gn]
    out_specs = [act_spec, act_spec, st_spec, act_spec, act_spec, act_spec]
    out_shape = [jax.ShapeDtypeStruct((S, d_conv), BF16), jax.ShapeDtypeStruct((S, d_conv), F32),
                 jax.ShapeDtypeStruct((H, S // BLK, HEAD, HEAD), BF16),
                 jax.ShapeDtypeStruct((S, d_conv), BF16), jax.ShapeDtypeStruct((S, d_conv), BF16),
                 jax.ShapeDtypeStruct((S, d_conv), F32)]
    scratch = [pltpu.VMEM((HEAD, HEAD), F32), pltpu.VMEM((T, HEAD), BF16), pltpu.VMEM((T, HEAD), F32),
               pltpu.VMEM((nb, HEAD, HEAD), F32)]
    aliases, wrap = _host(job, grid, in_specs, args, out_specs, out_shape, scratch)
    return pl.pallas_call(
        wrap(body), name="hgrn_fwd", grid=grid, in_specs=in_specs, out_specs=out_specs, out_shape=out_shape,
        scratch_shapes=scratch, input_output_aliases=aliases,
        compiler_params=_params(("parallel" if job is None else "arbitrary", "arbitrary")),
    )(*args)


def _hgrn_bwd(dob, o_raw, states, qe, ke, dec, proj, lb, gn, d_conv, job=None):
    S = proj.shape[0]
    H = d_conv // HEAD
    T = min(HGRN_BWD_ROWS, S)
    n_t, nb = S // T, T // BLK
    sub = min(HGRN_BWD_SUB, T)
    gate_rows = min(HGRN_GATE_ROWS, T)
    q_spec, z_spec, v_spec, g_spec, lb_spec, gn_spec, act_spec, st_spec = _hgrn_specs(T, d_conv, n_t, True)

    def body(dob_ref, o_ref, st_ref, qe_b, ke_b, dec_s, q_ref, z_ref, v_ref, g_ref, lb_ref, gn_ref,
             dq_ref, dz_ref, dv_ref, dg_ref, dlb_ref, dgn_ref,
             dstate, do_b, v_b, dqe_s, dke_s, dvi_s, ddec_s, do_s, u_s, ds_s):
        @pl.when(pl.program_id(1) == 0)
        def _():
            dstate[...] = jnp.zeros_like(dstate)
            dlb_ref[...] = jnp.zeros_like(dlb_ref)
            dgn_ref[...] = jnp.zeros_like(dgn_ref)

        pos = lax.broadcasted_iota(jnp.int32, (sub, HEAD), 0) % BLK
        lbv, gnv = lb_ref[...], gn_ref[...]

        def before(s, carry):
            r = pl.ds(pl.multiple_of(s * gate_rows, gate_rows), gate_rows)
            g = g_ref[r, :]
            v_b[r, :] = v_ref[r, :].astype(BF16)
            o = o_ref[r, :]
            rs = lax.rsqrt(jnp.mean(o * o, axis=-1, keepdims=True) + EPS)
            on = o * rs
            sgg = _sig(g)
            dobv = dob_ref[r, :]
            dog = dobv * (g * sgg)
            dgn_ref[...] += jnp.sum(dog * on, axis=0, keepdims=True)
            don = dog * gnv
            do = rs * (don - on * jnp.mean(don * on, axis=-1, keepdims=True))
            dg_ref[r, :] = (dobv * (on * gnv) * (sgg * (1.0 + g * (1.0 - sgg)))).astype(BF16)
            do_s[r, :] = do
            do_b[r, :] = do.astype(BF16)
            return carry

        lax.fori_loop(0, T // gate_rows, before, 0)

        _block_outer_products(do_b, qe_b, u_s, nb)
        _block_products(do_b, st_ref, dqe_s, nb, False)
        dst = dstate[...]
        for j in reversed(range(nb)):
            ds_s[j] = dst.astype(BF16)
            ddec = jnp.sum(st_ref[j].astype(F32) * dst, axis=0, keepdims=True)
            ddec_s[pl.ds(j * BLK, BLK), :] = jnp.broadcast_to(ddec, (BLK, HEAD))
            dst = dst * dec_s[pl.ds(j * BLK, 1), :] + u_s[j]
        dstate[...] = dst
        _block_products(v_b, ds_s, dke_s, nb, False)
        _block_products(ke_b, ds_s, dvi_s, nb, True)

        def after(s, carry):
            r = pl.ds(pl.multiple_of(s * sub, sub), sub)
            q, v = q_ref[r, :], v_ref[r, :]
            sgq, qs, sg, f, kk, b, b_tot = _hgrn_decays(q, z_ref[r, :], lbv, pos, sub)
            do = do_s[r, :]
            dqs = dqe_s[r, :] * jnp.exp(b)
            dkk = dke_s[r, :] * jnp.exp(b_tot - b)
            d_tot = jnp.where(pos == BLK - 1, _block_cumsum(dkk * kk, pos) + ddec_s[r, :] * jnp.exp(b_tot), 0.0)
            dv = dvi_s[r, :]
            da = jnp.sum(do * v, axis=-1, keepdims=True)
            dv = dv + jnp.sum(qs * kk, axis=-1, keepdims=True) * do
            dqs = dqs + da * kk
            dkk = dkk + da * qs
            for d in range(1, BLK):
                kd = pltpu.roll(kk, d, 0)
                e = jnp.where(pos >= d, jnp.exp(b - pltpu.roll(b, d, 0)), 0.0)
                a = jnp.sum(qs * kd * e, axis=-1, keepdims=True)
                da = jnp.sum(do * pltpu.roll(v, d, 0), axis=-1, keepdims=True)
                gq = da * e
                dqs = dqs + gq * kd
                dkk = dkk + pltpu.roll(gq * qs, sub - d, 0)
                dv = dv + pltpu.roll(a * do, sub - d, 0)
            db = qs * dqs - kk * dkk + d_tot
            dlf = _block_rev_cumsum(db, pos, sub)
            df = dlf / f - dkk
            dlb_ref[...] += jnp.sum(df * (1.0 - sg), axis=0, keepdims=True)
            dz_ref[r, :] = (df * (1.0 - lbv) * sg * (1.0 - sg)).astype(BF16)
            dq_ref[r, :] = (dqs * (sgq * (1.0 + q * (1.0 - sgq)))).astype(BF16)
            dv_ref[r, :] = dv.astype(BF16)
            return carry

        lax.fori_loop(0, T // sub, after, 0)

    tb = lambda dt: pltpu.VMEM((T, HEAD), dt)
    grid = (H, n_t)
    in_specs = [act_spec, act_spec, st_spec, act_spec, act_spec, act_spec, q_spec, z_spec, v_spec, g_spec, lb_spec,
                gn_spec]
    args = [dob, o_raw, states, qe, ke, dec, proj, proj, proj, proj, lb, gn]
    out_specs = [act_spec, act_spec, act_spec, act_spec, lb_spec, pl.BlockSpec((None, 1, HEAD), lambda h, i: (h, 0, 0))]
    out_shape = ([jax.ShapeDtypeStruct((S, d_conv), BF16)] * 4
                 + [jax.ShapeDtypeStruct((1, d_conv), F32), jax.ShapeDtypeStruct((H, 1, HEAD), F32)])
    scratch = [pltpu.VMEM((HEAD, HEAD), F32), tb(BF16), tb(BF16), tb(F32), tb(F32), tb(F32), tb(F32), tb(F32),
               pltpu.VMEM((nb, HEAD, HEAD), F32), pltpu.VMEM((nb, HEAD, HEAD), BF16)]
    aliases, wrap = _host(job, grid, in_specs, args, out_specs, out_shape, scratch)
    return pl.pallas_call(
        wrap(body), name="hgrn_bwd", grid=grid, in_specs=in_specs, out_specs=out_specs, out_shape=out_shape,
        scratch_shapes=scratch, input_output_aliases=aliases,
        compiler_params=_params(("parallel" if job is None else "arbitrary", "arbitrary")),
    )(*args)


def _local_step(x, target, mod, norm_mix_g, lb, gnorm_g, norm_ffn_g, norm_final_g, conv_w, bufs, c_idx, order):
    S, D = x.shape
    d_conv = D // 2
    d_in = 7 * d_conv + 2 * D
    d_ff = N_CHIPS * bufs["w_ffn_down"].shape[1]
    nb_in, nb_co, nb_ff = bufs["w_in"].shape[2], bufs["w_conv_out"].shape[2], bufs["w_ffn_gate"].shape[2]
    rows = lambda g: g.reshape(-1, g.shape[2])
    sh_m, sc_m, gt_m, sh_f, sc_f, gt_f = [mod[:, i * D:(i + 1) * D] for i in range(6)]
    tm = _pick(S, 512, 16)
    tm2 = _pick(S, 1024, 16)
    tn_in = _pick(nb_in, 1408, 128)
    tn_co = _pick(nb_co, 512, 128)
    tn_ff = _pick(nb_ff, 1408, 128)
    tn_d = _pick(D, 512, 128)
    tw = _pick(D, 1024, 128)
    tg = _pick(D, 512, 128)

    h = _norm_mod("norm_mix", x, norm_mix_g, sc_m, sh_m)
    proj, w_in = _proj_gather(h, bufs["w_in"], order)
    ob, o_raw, states, qe, ke, dec, *got = _hgrn_fwd(
        proj, lb, gnorm_g, d_conv,
        job=_GatherJob([bufs[k] for k in ("w_conv_out", "w_hgrn_out", "w_o", "w_ffn_gate")]))
    ya, w_conv_out, w_hgrn_out, w_o, w_ffn_gate, w_ffn_up = _conv_fwd(
        proj, conv_w, d_conv, job=_Jobs([_ForwardJob(got), _GatherJob([bufs["w_ffn_up"]], 0, 2)]))
    w_o = rows(w_o)

    def merge_epi(accs, ex):
        y_a, y_b = accs
        return [y_a, y_b, _sig(ex[0]) * y_a + _sig(ex[1]) * y_b]

    y_a, y_b, merged, w_ffn_up = _matmul(
        "branch_out", [(ya, 'n', w_conv_out, 'n'), (ob, 'n', w_hgrn_out, 'n')], [0, 1], S, D, d_conv, tm2, tn_co, d_conv,
        [(proj, 'tile', 7 * d_conv), (proj, 'tile', 7 * d_conv + D)], [(BF16, None), (BF16, None), (BF16, None)], merge_epi,
        job=_GatherJob([w_ffn_up], 2, 5), rows_outer=True)

    def resid_epi(accs, ex):
        return [accs[0], ex[0] + ex[1] * accs[0]]

    mo, x1, w_ffn_up = _matmul("w_o", [(merged, 'n', w_o, 'n')], [0], S, D, D, tm2, tw, D,
                               [(x, 'tile', 0), (gt_m, 'row', 0)], [(BF16, None), (F32, None)], resid_epi,
                               job=_GatherJob([w_ffn_up], 5, 8))
    h2, w_ffn_up = _norm_mod("norm_ffn", x1, norm_ffn_g, sc_f, sh_f, job=_ForwardJob([w_ffn_up]))

    def swiglu_epi(accs, ex):
        gg, uu = accs
        return [gg, uu, gg * _sig(gg) * uu]

    G, U, act, w_ffn_down = _matmul(
        "ffn_in", [(h2, 'n', w_ffn_gate, 'n'), (h2, 'n', w_ffn_up, 'n')], [0, 1], S, d_ff, D,
        tm2, tn_ff, D, [], [(BF16, None), (BF16, None), (BF16, None)], swiglu_epi, job=_GatherJob([bufs["w_ffn_down"]]))
    w_ffn_down = rows(_forward_halves("forward_down", [w_ffn_down])[0])
    ff, x2 = _matmul("ffn_out", [(act, 'n', w_ffn_down, 'n')], [0], S, D, d_ff, tm2, tn_d, d_ff,
                     [(x1, 'tile', 0), (gt_f, 'row', 0)], [(BF16, None), (F32, None)], resid_epi, rows_outer=True)

    dx2, dffb, loss, dgt_f, dg_final = _loss_head(x2, target, norm_final_g, ff, gt_f)

    def swiglu_bwd_epi(accs, ex):
        dact, gg, uu = accs[0], ex[0], ex[1]
        s = _sig(gg)
        return [dact * uu * (s * (1.0 + gg * (1.0 - s))), dact * (gg * s)]

    dG, dU = _matmul("d_act", [(dffb, 'n', w_ffn_down, 't')], [0], S, d_ff, D, tm2, tn_ff, D,
                     [(G, 'tile', 0), (U, 'tile', 0)], [(BF16, None), (BF16, None)], swiglu_bwd_epi)
    dw_down, = _matmul("dw_ffn_down", [(act, 't', dffb, 'n')], [0], d_ff, D, S, _pick(d_ff, 512, 128),
                       D, S, [], [(BF16, None)], _plain)
    dh2, = _matmul("d_h2", [(dG, 'n', w_ffn_gate, 't'), (dU, 'n', w_ffn_up, 't')], [0, 0], S, D, d_ff,
                   tm, D, tn_ff, [], [(F32, None)], _plain)
    dw_gate, = _matmul("dw_ffn_gate", [(h2, 't', dG, 'n')], [0], D, d_ff, S, tw, tn_ff, S, [], [(BF16, nb_ff)], _plain)
    dw_up, = _matmul("dw_ffn_up", [(h2, 't', dU, 'n')], [0], D, d_ff, S, tw, tn_ff, S, [], [(BF16, nb_ff)], _plain)

    ck_idx = jnp.stack([c_idx[0], order[0]])

    def pair_sums(names, grads, from_sibling):
        pairs = [_pair_sum("pair_sum_" + k, g, q, ck_idx) for k, g, q in zip(names, grads, from_sibling)]
        return [p[0] for p in pairs], [p[1] for p in pairs]

    ffn_names = ["w_ffn_down", "w_ffn_gate", "w_ffn_up"]
    ffn_grads = [dw_down.reshape(N_CHIPS, -1, D), dw_gate, dw_up]
    dx1, dsh_f, dsc_f, dg_ffn, dmob, dgt_m, *from_sibling = _norm_mod_bwd(
        "norm_ffn_bwd", dh2, x1, norm_ffn_g, sc_f, dx2, (mo, gt_m), job=_SwapJob(ffn_grads))
    parts_f, slots_f = pair_sums(ffn_names, ffn_grads, from_sibling)

    def merge_bwd_epi(accs, ex):
        dm, ga, gb, ya_, yb_ = accs[0], ex[0], ex[1], ex[2], ex[3]
        sa, sb = _sig(ga), _sig(gb)
        return [dm * ya_ * sa * (1.0 - sa), dm * yb_ * sb * (1.0 - sb), dm * sa, dm * sb]

    dga, dgb, dy_a, dy_b = _matmul(
        "d_merged", [(dmob, 'n', w_o, 't')], [0], S, D, D, tm2, tn_co, D,
        [(proj, 'tile', 7 * d_conv), (proj, 'tile', 7 * d_conv + D), (y_a, 'tile', 0), (y_b, 'tile', 0)],
        [(BF16, None)] * 4, merge_bwd_epi, rows_outer=True)
    dw_o, = _matmul("dw_o", [(merged, 't', dmob, 'n')], [0], D, D, S, tg, D, S, [], [(BF16, None)], _plain)
    dya, dob = _matmul("d_branch", [(dy_a, 'n', w_conv_out, 't'), (dy_b, 'n', w_hgrn_out, 't')], [0, 1], S, d_conv, D,
                       tm2, _pick(d_conv, 1024, 128), tn_co, [], [(F32, None), (F32, None)], _plain)
    dw_co, dw_ho = _matmul("dw_branch", [(ya, 't', dy_a, 'n'), (ob, 't', dy_b, 'n')], [0, 1], d_conv, D, S,
                           _pick(d_conv, 512, 128), tn_co, S, [], [(BF16, nb_co), (BF16, nb_co)], _plain)
    branch_names = ["w_conv_out", "w_hgrn_out", "w_o"]
    branch_grads = [dw_co, dw_ho, dw_o.reshape(N_CHIPS, -1, D)]
    dab, dac, dax, dconv_w, *from_sibling = _conv_bwd(dya, proj, conv_w, d_conv, job=_SwapJob(branch_grads))
    parts_b, slots_b = pair_sums(branch_names, branch_grads, from_sibling)
    dq, dz, dv, dgo, dlb, dgn, *arrived_f = _hgrn_bwd(dob, o_raw, states, qe, ke, dec, proj, lb, gnorm_g, d_conv,
                                                      job=_ScatterJob(parts_f, slots_f))
    dproj = jnp.concatenate([dab, dac, dax, dq, dz, dv, dgo, dga, dgb], axis=1)
    halves_f = [_sum_chips("sum_chips_" + k, r, c_idx) for k, r in zip(ffn_names, arrived_f)]
    dw_in, *moved = _matmul("dw_in", [(h, 't', dproj, 'n')], [0], D, d_in, S, tw, tn_in, S, [], [(BF16, nb_in)], _plain,
                            job=_Jobs([_ShareJob(halves_f), _ScatterJob(parts_b, slots_b)]))
    whole_f, arrived_b = moved[:len(ffn_names)], moved[len(ffn_names):]
    parts_i, slots_i = pair_sums(["w_in"], [dw_in], _swap_halves("swap_in", [dw_in]))
    dh, arrived_in = _matmul("d_h", [(dproj, 'n', w_in, 't')], [0], S, D, d_in, tm2, D, tn_in, [], [(F32, None)], _plain,
                             job=_ScatterJob(parts_i, slots_i))
    dx, dsh_m, dsc_m, dg_mix = _norm_mod_bwd("norm_mix_bwd", dh, x, norm_mix_g, sc_m, dx1)
    dmod = jnp.concatenate([dsh_m, dsc_m, dgt_m, dsh_f, dsc_f, dgt_f], axis=1)
    small = dict(dmod=dmod, dg_mix=dg_mix, dg_ffn=dg_ffn, dg_final=dg_final, dlb=dlb,
                 dgn=jnp.sum(dgn, axis=0), dconv_w=dconv_w)
    late_names = branch_names + ["w_in"]
    late = _share_halves([_sum_chips("sum_chips_" + k, r, c_idx) for k, r in zip(late_names, arrived_b + [arrived_in])])
    big = dict(zip(ffn_names + late_names, whole_f + list(late)))
    return loss, dx, small, big


def _adamw_math(w, g, m, v):
    m = ADAM_B1 * m + (1.0 - ADAM_B1) * g
    v = ADAM_B2 * v + (1.0 - ADAM_B2) * (g * g)
    m_hat = m / (1.0 - ADAM_B1 ** ADAM_STEP)
    v_hat = v / (1.0 - ADAM_B2 ** ADAM_STEP)
    delta = -ADAM_LR * (m_hat / (jnp.sqrt(v_hat) + ADAM_EPS) + ADAM_WD * w)
    return delta, m, v


def _adamw(name, w, g, m, v):
    R, C = w.shape
    tr = _pick(R, max(8, (256 * 1024) // C // 8 * 8), 8)
    spec = pl.BlockSpec((tr, C), lambda i: (i, 0))

    def body(w_ref, g_ref, m_ref, v_ref, go_ref, d_ref, mo_ref, vo_ref):
        gv = g_ref[...]
        d, m2, v2 = _adamw_math(w_ref[...], gv, m_ref[...], v_ref[...])
        go_ref[...] = gv
        d_ref[...] = d
        mo_ref[...] = m2
        vo_ref[...] = v2

    return pl.pallas_call(
        body, name=name, grid=(R // tr,), in_specs=[spec] * 4, out_specs=[spec] * 4,
        out_shape=[jax.ShapeDtypeStruct((R, C), F32)] * 4, compiler_params=_params(("parallel",)),
    )(w, g, m, v)


def _ada_update(c_act_t, dmod, w, m, v):
    R, C = w.shape
    B = dmod.shape[0]
    tr = _pick(R, 256, 8)
    tc = _pick(C, 1536, 128)
    spec = pl.BlockSpec((tr, tc), lambda i, j: (i, j))

    def body(ct_ref, dm_ref, w_ref, m_ref, v_ref, g_ref, d_ref, mo_ref, vo_ref):
        ct = ct_ref[...]
        dm = dm_ref[...]
        g = ct[:, 0:1] * dm[0:1, :]
        for b in range(1, B):
            g = g + ct[:, b:b + 1] * dm[b:b + 1, :]
        d, m2, v2 = _adamw_math(w_ref[...], g, m_ref[...], v_ref[...])
        g_ref[...] = g
        d_ref[...] = d
        mo_ref[...] = m2
        vo_ref[...] = v2

    return pl.pallas_call(
        body, name="ada_update", grid=(R // tr, C // tc),
        in_specs=[pl.BlockSpec((tr, B), lambda i, j: (i, 0)), pl.BlockSpec((B, tc), lambda i, j: (0, j)),
                  spec, spec, spec],
        out_specs=[spec] * 4, out_shape=[jax.ShapeDtypeStruct((R, C), F32)] * 4,
        compiler_params=_params(("parallel", "parallel")),
    )(c_act_t, dmod, w, m, v)


def _prep(c_all, lb_param):
    def body(c_ref, p_ref, ca_ref, cab_ref, lb_ref):
        cv = c_ref[...]
        ca = cv * _sig(cv)
        ca_ref[...] = ca
        cab_ref[...] = ca.astype(BF16)
        lb_ref[...] = _sig(p_ref[0:1, :] - p_ref[1:2, :])

    return pl.pallas_call(
        body, name="prep",
        out_shape=[jax.ShapeDtypeStruct(c_all.shape, F32), jax.ShapeDtypeStruct(c_all.shape, BF16),
                   jax.ShapeDtypeStruct((1, lb_param.shape[1]), F32)],
    )(c_all, lb_param)


def _small_reduce(parts):
    W = parts.shape[1]

    def body(p_ref, o_ref):
        acc = p_ref[0:1, :]
        for d in range(1, N_DEV):
            acc = acc + p_ref[d:d + 1, :]
        o_ref[...] = acc

    return pl.pallas_call(body, name="small_reduce", out_shape=jax.ShapeDtypeStruct((1, W), F32))(parts)


def _lb_grad(dlb, lb):
    def body(d_ref, lb_ref, o_ref):
        t = d_ref[...] * lb_ref[...] * (1.0 - lb_ref[...])
        o_ref[0:1, :] = t
        o_ref[1:2, :] = -t

    return pl.pallas_call(body, name="lb_grad", out_shape=jax.ShapeDtypeStruct((2, dlb.shape[1]), F32))(dlb, lb)


def _place():
    x, y, c = lax.axis_index("x"), lax.axis_index("y"), lax.axis_index("c")
    chips = [(1 - x, y), (x, 1 - y), (1 - x, 1 - y)]
    return x, y, c, chips


def _allgather_rows(name, v):
    m_per, n = v.shape

    def body(x_ref, out_ref, send_sems, recv_sems, local_sem):
        x, y, c, chips = _place()
        me, sibling = (x, y, c), (x, y, 1 - c)

        def rows(px, py, pc):
            return out_ref.at[pl.ds((4 * px + 2 * py + pc) * m_per, m_per), :]

        def copy(k, block, to, src=None):
            return pltpu.make_async_remote_copy(
                src_ref=rows(*block) if src is None else src, dst_ref=rows(*block),
                send_sem=send_sems.at[k], recv_sem=recv_sems.at[k], device_id=to, device_id_type=MESH)

        mine = pltpu.make_async_copy(x_ref, rows(*me), local_sem)
        mine.start()
        first = [copy(0, me, sibling, src=x_ref)]
        first += [copy(1 + j, me, (*chip, c), src=x_ref) for j, chip in enumerate(chips)]
        for cp in first:
            cp.start()
        passed = [copy(4 + j, (*chip, c), sibling) for j, chip in enumerate(chips)]
        for j, chip in enumerate(chips):
            copy(1 + j, (*chip, c), me).wait_recv()
            passed[j].start()
        copy(0, sibling, me).wait_recv()
        for j, chip in enumerate(chips):
            copy(4 + j, (*chip, 1 - c), me).wait_recv()
        for cp in first + passed:
            cp.wait_send()
        mine.wait()

    return pl.pallas_call(
        body, name=name, out_shape=jax.ShapeDtypeStruct((N_DEV * m_per, n), v.dtype),
        in_specs=[pl.BlockSpec(memory_space=pltpu.VMEM)], out_specs=pl.BlockSpec(memory_space=pltpu.VMEM),
        scratch_shapes=[pltpu.SemaphoreType.DMA((7,)), pltpu.SemaphoreType.DMA((7,)), pltpu.SemaphoreType.DMA],
    )(v)


def _cast_place(name, w, k_idx):
    R, C = w.shape
    tr = _pick(R, max(16, (512 * 1024) // C // 16 * 16), 16)

    def body(k_ref, w_ref, o_ref):
        o_ref[...] = w_ref[...].astype(BF16)

    return pl.pallas_call(
        body, name=name,
        grid_spec=pltpu.PrefetchScalarGridSpec(
            num_scalar_prefetch=1, grid=(R // tr,),
            in_specs=[pl.BlockSpec((tr, C), lambda i, k_ref: (i, 0))],
            out_specs=pl.BlockSpec((None, tr, C), lambda i, k_ref: (k_ref[0], i, 0))),
        out_shape=jax.ShapeDtypeStruct((N_CHIPS, R, C), BF16),
        compiler_params=_params(("parallel",)),
    )(k_idx, w)


def _chip_copies(src_of, dst_of, got_of, n, sems, receiving):
    x, y, c, chips = _place()
    k_me = 2 * x + y
    send_sems, recv_sems = sems
    out = []
    for a in range(n):
        for j, chip in enumerate(chips):
            k_chip = 2 * chip[0] + chip[1]
            if receiving:
                src = dst = got_of(a, k_chip, c)
                to = (x, y, c)
            else:
                src, dst, to = src_of(a, k_chip, k_me, c), dst_of(a, k_me, c), (*chip, c)
            out.append(pltpu.make_async_remote_copy(
                src_ref=src, dst_ref=dst, send_sem=send_sems.at[3 * a + j], recv_sem=recv_sems.at[3 * a + j],
                device_id=to, device_id_type=MESH))
    return out


class _ChipJob:
    def start(self, j_in, j_out, sems):
        for send in self.copies(j_in, j_out, sems, False):
            send.start()

    def finish(self, j_in, j_out, sems):
        for recv in self.copies(j_in, j_out, sems, True):
            recv.wait_recv()
        for send in self.copies(j_in, j_out, sems, False):
            send.wait_send()


class _GatherJob(_ChipJob):
    def __init__(self, bufs, lo=0, hi=8):
        n = len(bufs)
        self.inputs, self.n_alias = list(bufs), n
        self.sem_shapes = [pltpu.SemaphoreType.DMA((3 * n,)), pltpu.SemaphoreType.DMA((3 * n,))]
        self.half = [b.shape[1] // 2 for b in bufs]
        self.lo, self.hi = lo, hi

    def copies(self, j_in, j_out, sems, receiving):
        def half(a, k, c):
            eighth = self.half[a] // 8
            return j_out[a].at[k, pl.ds(c * self.half[a] + self.lo * eighth, (self.hi - self.lo) * eighth)]

        return _chip_copies(lambda a, k_chip, k_me, c: half(a, k_me, c), half, half, len(self.half), sems, receiving)


class _ScatterJob(_ChipJob):
    def __init__(self, parts, slots):
        n = len(parts)
        self.n = n
        self.inputs, self.n_alias = list(parts) + list(slots), n
        self.sem_shapes = [pltpu.SemaphoreType.DMA((3 * n,)), pltpu.SemaphoreType.DMA((3 * n,))]

    def copies(self, j_in, j_out, sems, receiving):
        return _chip_copies(lambda a, k_chip, k_me, c: j_in[a].at[k_chip], lambda a, k_me, c: j_out[a].at[k_me],
                            lambda a, k_chip, c: j_out[a].at[k_chip], self.n, sems, receiving)


class _SwapJob(_ChipJob):
    def __init__(self, grads):
        n = len(grads)
        self.n = n
        self.half = [g.shape[1] // 2 for g in grads]
        landing = [lax.empty((N_CHIPS, g.shape[1] // 2, g.shape[2]), g.dtype) for g in grads]
        self.inputs, self.n_alias = list(grads) + landing, n
        self.sem_shapes = [pltpu.SemaphoreType.DMA((n,)), pltpu.SemaphoreType.DMA((n,))]

    def copies(self, j_in, j_out, sems, receiving):
        x, y, c, _ = _place()
        out = []
        for a in range(self.n):
            if receiving:
                src, to = j_out[a], (x, y, c)
            else:
                src, to = j_in[a].at[:, pl.ds((1 - c) * self.half[a], self.half[a])], (x, y, 1 - c)
            out.append(pltpu.make_async_remote_copy(src_ref=src, dst_ref=j_out[a], send_sem=sems[0].at[a],
                                                    recv_sem=sems[1].at[a], device_id=to, device_id_type=MESH))
        return out


class _Jobs:
    def __init__(self, jobs):
        self.jobs = jobs
        split = [(j.inputs[:len(j.inputs) - j.n_alias], j.inputs[len(j.inputs) - j.n_alias:]) for j in jobs]
        self.inputs = [a for plain, _ in split for a in plain] + [a for _, aliased in split for a in aliased]
        self.n_alias = sum(j.n_alias for j in jobs)
        self.sem_shapes = [s for j in jobs for s in j.sem_shapes]

    def _each(self, j_in, j_out, sems):
        n_plain = len(j_in) - self.n_alias
        p0 = a0 = s0 = 0
        for j in self.jobs:
            n_p, n_s = len(j.inputs) - j.n_alias, len(j.sem_shapes)
            ins = list(j_in[p0:p0 + n_p]) + list(j_in[n_plain + a0:n_plain + a0 + j.n_alias])
            yield j, ins, j_out[a0:a0 + j.n_alias], sems[s0:s0 + n_s]
            p0, a0, s0 = p0 + n_p, a0 + j.n_alias, s0 + n_s

    def start(self, j_in, j_out, sems):
        for j, ins, outs, s in self._each(j_in, j_out, sems):
            j.start(ins, outs, s)

    def finish(self, j_in, j_out, sems):
        for j, ins, outs, s in self._each(j_in, j_out, sems):
            j.finish(ins, outs, s)


class _ShareJob(_ChipJob):
    def __init__(self, bufs):
        n = len(bufs)
        self.n = n
        self.inputs, self.n_alias = list(bufs), n
        self.sem_shapes = [pltpu.SemaphoreType.DMA((n,)), pltpu.SemaphoreType.DMA((n,))]

    def copies(self, j_in, j_out, sems, receiving):
        x, y, c, _ = _place()
        out = []
        for a in range(self.n):
            hc, to = (1 - c, (x, y, c)) if receiving else (c, (x, y, 1 - c))
            out.append(pltpu.make_async_remote_copy(
                src_ref=j_out[a].at[hc], dst_ref=j_out[a].at[hc], send_sem=sems[0].at[a], recv_sem=sems[1].at[a],
                device_id=to, device_id_type=MESH))
        return out


class _ForwardJob(_ChipJob):
    def __init__(self, bufs):
        n = len(bufs)
        self.n = n
        self.half = [b.shape[1] // 2 for b in bufs]
        self.inputs, self.n_alias = list(bufs), n
        self.sem_shapes = [pltpu.SemaphoreType.DMA((3 * n,)), pltpu.SemaphoreType.DMA((3 * n,))]

    def copies(self, j_in, j_out, sems, receiving):
        x, y, c, chips = _place()
        out = []
        for a in range(self.n):
            for q, chip in enumerate(chips):
                hc, to = (1 - c, (x, y, c)) if receiving else (c, (x, y, 1 - c))
                part = j_out[a].at[2 * chip[0] + chip[1], pl.ds(hc * self.half[a], self.half[a])]
                out.append(pltpu.make_async_remote_copy(
                    src_ref=part, dst_ref=part, send_sem=sems[0].at[3 * a + q], recv_sem=sems[1].at[3 * a + q],
                    device_id=to, device_id_type=MESH))
        return out


def _forward_halves(name, bufs):
    n = len(bufs)

    def body(*refs):
        out_refs = refs[n:2 * n]
        send_sems, recv_sems = refs[2 * n:]
        x, y, c, chips = _place()
        started, waits = [], []
        for a in range(n):
            rh = bufs[a].shape[1] // 2
            for j, chip in enumerate(chips):
                k_chip = 2 * chip[0] + chip[1]
                got = out_refs[a].at[k_chip, pl.ds(c * rh, rh)]
                cp = pltpu.make_async_remote_copy(src_ref=got, dst_ref=got, send_sem=send_sems.at[3 * a + j],
                                                  recv_sem=recv_sems.at[3 * a + j], device_id=(x, y, 1 - c),
                                                  device_id_type=MESH)
                cp.start()
                started.append(cp)
                other = out_refs[a].at[k_chip, pl.ds((1 - c) * rh, rh)]
                waits.append(pltpu.make_async_remote_copy(
                    src_ref=other, dst_ref=other, send_sem=send_sems.at[3 * a + j], recv_sem=recv_sems.at[3 * a + j],
                    device_id=(x, y, c), device_id_type=MESH))
        for cp in waits:
            cp.wait_recv()
        for cp in started:
            cp.wait_send()

    return pl.pallas_call(
        body, name=name, out_shape=[jax.ShapeDtypeStruct(b.shape, b.dtype) for b in bufs],
        in_specs=[_HBM] * n, out_specs=[_HBM] * n, input_output_aliases={a: a for a in range(n)},
        scratch_shapes=[pltpu.SemaphoreType.DMA((3 * n,)), pltpu.SemaphoreType.DMA((3 * n,))],
    )(*bufs)


def _proj_gather(h, buf, order):
    S, D = h.shape
    nb = buf.shape[2]
    tm = _pick(S, 1024, 16)
    tn = _pick(nb, 1408, 128)
    per = nb // tn
    nj, ni = N_CHIPS * per, S // tm
    rh = D // 2
    seq = [(0, t) for t in range(per)] + [(p, t) for t in range(per) for p in (1, 2)] + [(3, t) for t in range(per)]
    tile_chip = jnp.stack([order[p] for p, _ in seq])
    tile_of = jnp.array([t for _, t in seq], jnp.int32)

    def body(chip_ref, t_ref, h_ref, buf_in, proj_ref, buf_ref, wbuf, tile_sems, ici_send, ici_recv, d2d_send, d2d_recv):
        j, i = pl.program_id(0), pl.program_id(1)
        x, y, c, chips = _place()
        k_me = 2 * x + y

        def part(k, hc, t):
            return buf_ref.at[k, pl.ds(hc * rh, rh), pl.ds(t * tn, tn)]

        def ici(q, t, receiving):
            k_chip = 2 * chips[q][0] + chips[q][1]
            src, to = (part(k_chip, c, t), (x, y, c)) if receiving else (part(k_me, c, t), (*chips[q], c))
            return pltpu.make_async_remote_copy(src_ref=src, dst_ref=src, send_sem=ici_send.at[q * per + t],
                                                recv_sem=ici_recv.at[q * per + t], device_id=to, device_id_type=MESH)

        def d2d(q, t, receiving):
            k_chip = 2 * chips[q][0] + chips[q][1]
            src, to = (part(k_chip, 1 - c, t), (x, y, c)) if receiving else (part(k_chip, c, t), (x, y, 1 - c))
            return pltpu.make_async_remote_copy(src_ref=src, dst_ref=src, send_sem=d2d_send.at[q * per + t],
                                                recv_sem=d2d_recv.at[q * per + t], device_id=to, device_id_type=MESH)

        def tile_copy(n):
            col = pl.multiple_of(t_ref[n] * tn, 128)
            return pltpu.make_async_copy(buf_ref.at[chip_ref[n], :, pl.ds(col, tn)], wbuf.at[n % 2], tile_sems.at[n % 2])

        @pl.when(jnp.logical_and(j == 0, i == 0))
        def _():
            for t in range(per):
                ici(0, t, False).start()
                ici(1, t, False).start()
            tile_copy(0).start()

        @pl.when(i == 0)
        def _():
            tile_copy(j).wait()
            for n in range(per, nj):
                p, t = seq[n]

                @pl.when(j + 1 == n)
                def _():
                    ici(p - 1, t, True).wait_recv()
                    d2d(p - 1, t, False).start()
                    if (p, t) == (1, per - 1):
                        for q in range(2):
                            for tt in range(per):
                                ici(q, tt, False).wait_send()
                        for tt in range(per):
                            ici(2, tt, False).start()
                    d2d(p - 1, t, True).wait_recv()

            @pl.when(j + 1 < nj)
            def _():
                tile_copy(j + 1).start()

        proj_ref[...] = jnp.dot(h_ref[...], wbuf[j % 2], preferred_element_type=F32)

        @pl.when(jnp.logical_and(j == nj - 1, i == ni - 1))
        def _():
            for t in range(per):
                ici(2, t, False).wait_send()
                for q in range(3):
                    d2d(q, t, False).wait_send()

    n_sems = 3 * per
    return pl.pallas_call(
        body, name="proj",
        grid_spec=pltpu.PrefetchScalarGridSpec(
            num_scalar_prefetch=2, grid=(nj, ni),
            in_specs=[pl.BlockSpec((tm, D), lambda j, i, kc, kt: (i, 0)), _HBM],
            out_specs=[pl.BlockSpec((tm, tn), lambda j, i, kc, kt: (i, kc[j] * per + kt[j])), _HBM],
            scratch_shapes=[pltpu.VMEM((2, D, tn), BF16), pltpu.SemaphoreType.DMA((2,))]
            + [pltpu.SemaphoreType.DMA((n_sems,))] * 4),
        out_shape=[jax.ShapeDtypeStruct((S, N_CHIPS * nb), F32), jax.ShapeDtypeStruct(buf.shape, buf.dtype)],
        input_output_aliases={3: 1}, compiler_params=_params(("arbitrary", "arbitrary")),
    )(tile_chip, tile_of, h, buf)


def _swap_halves(name, grads):
    n = len(grads)

    def body(*refs):
        in_refs, out_refs = refs[:n], refs[n:2 * n]
        send_sems, recv_sems = refs[2 * n:]
        x, y, c, _ = _place()
        cps = []
        for a in range(n):
            rh = grads[a].shape[1] // 2
            cp = pltpu.make_async_remote_copy(
                src_ref=in_refs[a].at[:, pl.ds((1 - c) * rh, rh)], dst_ref=out_refs[a],
                send_sem=send_sems.at[a], recv_sem=recv_sems.at[a], device_id=(x, y, 1 - c), device_id_type=MESH)
            cp.start()
            cps.append(cp)
        for cp in cps:
            cp.wait()

    return pl.pallas_call(
        body, name=name,
        out_shape=[jax.ShapeDtypeStruct((N_CHIPS, g.shape[1] // 2, g.shape[2]), g.dtype) for g in grads],
        in_specs=[_HBM] * n, out_specs=[_HBM] * n,
        scratch_shapes=[pltpu.SemaphoreType.DMA((n,)), pltpu.SemaphoreType.DMA((n,))],
    )(*grads)


def _pair_sum(name, g, q, ck_idx):
    _, R, C = g.shape
    rh = R // 2
    tr = _pick(rh, max(16, (512 * 1024) // C // 16 * 16), 16)
    nh = rh // tr

    def body(ck_ref, g_ref, q_ref, o_ref, own_ref):
        s = (g_ref[...].astype(F32) + q_ref[...].astype(F32)).astype(BF16)
        o_ref[...] = s

        @pl.when(pl.program_id(1) == ck_ref[1])
        def _():
            own_ref[...] = s

    return pl.pallas_call(
        body, name=name,
        grid_spec=pltpu.PrefetchScalarGridSpec(
            num_scalar_prefetch=1, grid=(nh, N_CHIPS),
            in_specs=[pl.BlockSpec((None, tr, C), lambda i, k, ck: (k, ck[0] * nh + i, 0)),
                      pl.BlockSpec((None, tr, C), lambda i, k, ck: (k, i, 0))],
            out_specs=[pl.BlockSpec((None, tr, C), lambda i, k, ck: (k, i, 0)),
                       pl.BlockSpec((None, tr, C), lambda i, k, ck: (ck[1], i, 0))]),
        out_shape=[jax.ShapeDtypeStruct((N_CHIPS, rh, C), BF16)] * 2,
        compiler_params=_params(("parallel", "arbitrary")),
    )(ck_idx, g, q)


def _sum_chips(name, r, c_idx):
    _, rh, C = r.shape
    tr = _pick(rh, max(16, (256 * 1024) // C // 16 * 16), 16)

    def body(c_ref, r_ref, o_ref):
        acc = r_ref[0].astype(F32)
        for k in range(1, N_CHIPS):
            acc = acc + r_ref[k].astype(F32)
        o_ref[...] = acc

    return pl.pallas_call(
        body, name=name,
        grid_spec=pltpu.PrefetchScalarGridSpec(
            num_scalar_prefetch=1, grid=(rh // tr,),
            in_specs=[pl.BlockSpec((N_CHIPS, tr, C), lambda i, c_ref: (0, i, 0))],
            out_specs=pl.BlockSpec((None, tr, C), lambda i, c_ref: (c_ref[0], i, 0))),
        out_shape=jax.ShapeDtypeStruct((2, rh, C), F32), compiler_params=_params(("parallel",)),
    )(c_idx, r)


def _share_halves(bufs):
    n = len(bufs)

    def body(*refs):
        out_refs = refs[n:2 * n]
        send_sems, recv_sems = refs[2 * n:]
        x, y, c, _ = _place()
        cps = []
        for a in range(n):
            cp = pltpu.make_async_remote_copy(
                src_ref=out_refs[a].at[c], dst_ref=out_refs[a].at[c], send_sem=send_sems.at[a],
                recv_sem=recv_sems.at[a], device_id=(x, y, 1 - c), device_id_type=MESH)
            cp.start()
            cps.append(cp)
        for a, cp in enumerate(cps):
            got = out_refs[a].at[1 - c]
            pltpu.make_async_remote_copy(src_ref=got, dst_ref=got, send_sem=send_sems.at[a], recv_sem=recv_sems.at[a],
                                         device_id=(x, y, c), device_id_type=MESH).wait_recv()
        for cp in cps:
            cp.wait_send()

    return pl.pallas_call(
        body, name="share_halves",
        out_shape=[jax.ShapeDtypeStruct(b.shape, b.dtype) for b in bufs],
        in_specs=[_HBM] * n, out_specs=[_HBM] * n, input_output_aliases={a: a for a in range(n)},
        scratch_shapes=[pltpu.SemaphoreType.DMA((n,)), pltpu.SemaphoreType.DMA((n,))],
    )(*bufs)


_BIG = ("w_in", "w_conv_out", "w_hgrn_out", "w_o", "w_ffn_gate", "w_ffn_up", "w_ffn_down")
_WEIGHTS = ("w_ada", "b_ada", "norm_mix_g", "w_in", "conv_w", "lb_param", "gnorm_g", "w_conv_out", "w_hgrn_out",
            "w_o", "norm_ffn_g", "w_ffn_gate", "w_ffn_up", "w_ffn_down", "norm_final_g")


def _pack_rows(pieces):
    flat = jnp.concatenate([p.reshape(-1) for p in pieces])
    pad = (-flat.shape[0]) % (SUBLANES * LANES)
    return jnp.pad(flat, (0, pad)).reshape(SUBLANES, -1)


def kernel(x, c, w_ada, b_ada, norm_mix_g, w_in, conv_w, lb_param, gnorm_g, w_conv_out, w_hgrn_out, w_o, norm_ffn_g, w_ffn_gate, w_ffn_up, w_ffn_down, norm_final_g, loss_target, m_w_ada, m_b_ada, m_norm_mix_g, m_w_in, m_conv_w, m_lb_param, m_gnorm_g, m_w_conv_out, m_w_hgrn_out, m_w_o, m_norm_ffn_g, m_w_ffn_gate, m_w_ffn_up, m_w_ffn_down, m_norm_final_g, v_w_ada, v_b_ada, v_norm_mix_g, v_w_in, v_conv_w, v_lb_param, v_gnorm_g, v_w_conv_out, v_w_hgrn_out, v_w_o, v_norm_ffn_g, v_w_ffn_gate, v_w_ffn_up, v_w_ffn_down, v_norm_final_g):
    w = dict(w_ada=w_ada, b_ada=b_ada, norm_mix_g=norm_mix_g, w_in=w_in, conv_w=conv_w, lb_param=lb_param,
             gnorm_g=gnorm_g, w_conv_out=w_conv_out, w_hgrn_out=w_hgrn_out, w_o=w_o, norm_ffn_g=norm_ffn_g,
             w_ffn_gate=w_ffn_gate, w_ffn_up=w_ffn_up, w_ffn_down=w_ffn_down, norm_final_g=norm_final_g)
    m = dict(w_ada=m_w_ada, b_ada=m_b_ada, norm_mix_g=m_norm_mix_g, w_in=m_w_in, conv_w=m_conv_w, lb_param=m_lb_param,
             gnorm_g=m_gnorm_g, w_conv_out=m_w_conv_out, w_hgrn_out=m_w_hgrn_out, w_o=m_w_o, norm_ffn_g=m_norm_ffn_g,
             w_ffn_gate=m_w_ffn_gate, w_ffn_up=m_w_ffn_up, w_ffn_down=m_w_ffn_down, norm_final_g=m_norm_final_g)
    v = dict(w_ada=v_w_ada, b_ada=v_b_ada, norm_mix_g=v_norm_mix_g, w_in=v_w_in, conv_w=v_conv_w, lb_param=v_lb_param,
             gnorm_g=v_gnorm_g, w_conv_out=v_w_conv_out, w_hgrn_out=v_w_hgrn_out, w_o=v_w_o, norm_ffn_g=v_norm_ffn_g,
             w_ffn_gate=v_w_ffn_gate, w_ffn_up=v_w_ffn_up, w_ffn_down=v_w_ffn_down, norm_final_g=v_norm_final_g)
    two_d = lambda a: a.reshape((-1, a.shape[-1])) if a.ndim != 2 else a
    shapes = {k: a.shape for k, a in w.items()}
    w, m, v = ({k: two_d(a) for k, a in t.items()} for t in (w, m, v))
    w["lb_param"], m["lb_param"], v["lb_param"] = lb_param, m_lb_param, v_lb_param
    S, D = x.shape[1], x.shape[2]
    d_conv = D // 2
    ix, iy, ic = lax.axis_index("x"), lax.axis_index("y"), lax.axis_index("c")
    k_me = 2 * ix + iy
    dev = 2 * k_me + ic
    cw_shard = w["conv_w"].shape[1]
    ada_cols = w["w_ada"].shape[1]

    got = _allgather_rows("gather_cond", _pack_rows([c, w["conv_w"]])).reshape(N_DEV, -1)
    c_all = got[:, :D]
    conv_full = got[::2, D:D + 3 * cw_shard].reshape(N_CHIPS, 3, cw_shard).transpose(1, 0, 2).reshape(3, -1)
    c_act, c_act_b, lb = _prep(c_all, lb_param)
    b_cols = lax.dynamic_slice(w["b_ada"], (0, k_me * ada_cols), (1, ada_cols))
    mod_cols, = _matmul("ada_fwd", [(c_act_b, 'n', w["w_ada"].astype(BF16), 'n')], [0], N_DEV, ada_cols, D,
                        N_DEV, _pick(ada_cols, 1536, 128), D, [(b_cols, 'row', 0)], [(F32, None)],
                        lambda accs, ex: [accs[0] + ex[0]])
    mod_all = _allgather_rows("gather_mod", mod_cols).reshape(N_CHIPS, 2, N_DEV, ada_cols)[:, 0]
    mod_all = mod_all.transpose(1, 0, 2).reshape(N_DEV, -1)
    mod = lax.dynamic_slice(mod_all, (dev, 0), (1, mod_all.shape[1]))
    k_idx = jnp.reshape(k_me, (1,)).astype(jnp.int32)
    c_idx = jnp.reshape(ic, (1,)).astype(jnp.int32)
    bufs = {k: _cast_place("cast_" + k, w[k], k_idx) for k in _BIG}
    order = jnp.stack([k_me, 2 * (1 - ix) + iy, 2 * ix + (1 - iy), 2 * (1 - ix) + (1 - iy)]).astype(jnp.int32)

    loss, dx, small, arrived = _local_step(
        x[0], loss_target[0], mod, w["norm_mix_g"], lb, w["gnorm_g"], w["norm_ffn_g"], w["norm_final_g"], conv_full,
        bufs, c_idx, order)

    pieces = [small["dmod"], small["dg_mix"], small["dg_ffn"], small["dg_final"], small["dlb"], small["dgn"],
              small["dconv_w"], loss]
    sizes = [p.size for p in pieces]
    parts = _allgather_rows("gather_small", _pack_rows(pieces)).reshape(N_DEV, -1)
    total = _small_reduce(parts)
    offs = [0]
    for s in sizes:
        offs.append(offs[-1] + s)
    tot = [total[:, offs[i]:offs[i + 1]] for i in range(len(sizes))]
    dmod_all = parts[:, :sizes[0]]
    grads = {
        "b_ada": tot[0], "norm_mix_g": tot[1], "norm_ffn_g": tot[2], "norm_final_g": tot[3],
        "lb_param": _lb_grad(tot[4], lb), "gnorm_g": tot[5],
        "conv_w": lax.dynamic_slice(tot[6].reshape(3, -1), (0, k_me * cw_shard), (3, cw_shard)),
    }
    loss_out = tot[7][0, 0]

    for k in _BIG:
        grads[k] = arrived[k].reshape(-1, arrived[k].shape[-1])

    delta, new_m, new_v = {}, {}, {}
    dmod_cols = lax.dynamic_slice(dmod_all, (0, k_me * ada_cols), (N_DEV, ada_cols))
    grads["w_ada"], delta["w_ada"], new_m["w_ada"], new_v["w_ada"] = _ada_update(
        c_act.T, dmod_cols, w["w_ada"], m["w_ada"], v["w_ada"])
    for k in _WEIGHTS:
        if k != "w_ada":
            grads[k], delta[k], new_m[k], new_v[k] = _adamw("adamw_" + k, w[k], grads[k], m[k], v[k])
    outs = [loss_out, dx.reshape(x.shape)]
    for t in (grads, delta, new_m, new_v):
        outs += [t[k].reshape(shapes[k]) for k in _WEIGHTS]
    return tuple(outs)
```

```python
import functools

import jax
import jax.numpy as jnp
from jax import lax
from jax.experimental import pallas as pl
from jax.experimental.pallas import tpu as pltpu

F32 = jnp.float32
BF16 = jnp.bfloat16
MESH = pl.DeviceIdType.MESH

EPS = 1e-6
HEAD = 128
BLK = 16
N_CHIPS = 4
N_DEV = 8
SUBLANES, LANES = 8, 128
VMEM_LIMIT_BYTES = 56 * 1024 * 1024

ADAM_LR = 0.001
ADAM_B1 = 0.9
ADAM_B2 = 0.999
ADAM_EPS = 1e-08
ADAM_WD = 0.01
ADAM_STEP = 10


def _pick(dim, pref, mult):
    if dim <= pref:
        return dim
    t = (pref // mult) * mult
    while t >= mult:
        if dim % t == 0:
            return t
        t -= mult
    return dim


def _sig(x):
    return 1.0 / (1.0 + jnp.exp(-x))


def _params(sem):
    return pltpu.CompilerParams(dimension_semantics=sem, vmem_limit_bytes=VMEM_LIMIT_BYTES)


def _gj(j, i, k):
    return j


def _gk(j, i, k):
    return k


def _wspec(arr, rt, ct, r_of, c_of):
    if arr.ndim == 2:
        return pl.BlockSpec((rt, ct), lambda j, i, k: (r_of(j, i, k), c_of(j, i, k)))
    per = arr.shape[2] // ct
    assert arr.shape[2] % ct == 0
    return pl.BlockSpec((None, rt, ct),
                        lambda j, i, k: (c_of(j, i, k) // per, r_of(j, i, k), c_of(j, i, k) % per))


_HBM = pl.BlockSpec(memory_space=pltpu.HBM)


def _host(job, grid, in_specs, args, out_specs, out_shape, scratch):
    if job is None:
        return {}, (lambda body: body)
    n_in, n_out, n_scr = len(args), len(out_shape), len(scratch)
    n_job, n_alias = len(job.inputs), job.n_alias
    in_specs += [_HBM] * n_job
    args += job.inputs
    out_specs += [_HBM] * n_alias
    out_shape += [jax.ShapeDtypeStruct(a.shape, a.dtype) for a in job.inputs[n_job - n_alias:]]
    scratch += job.sem_shapes
    aliases = {n_in + n_job - n_alias + t: n_out + t for t in range(n_alias)}

    def wrap(body):
        def hosted(*refs):
            ins, refs = refs[:n_in], refs[n_in:]
            j_in, refs = refs[:n_job], refs[n_job:]
            outs, refs = refs[:n_out], refs[n_out:]
            j_out, refs = refs[:n_alias], refs[n_alias:]
            scr, sems = refs[:n_scr], refs[n_scr:]
            first = functools.reduce(jnp.logical_and, [pl.program_id(d) == 0 for d in range(len(grid))])
            last = functools.reduce(jnp.logical_and, [pl.program_id(d) == grid[d] - 1 for d in range(len(grid))])

            @pl.when(first)
            def _():
                job.start(j_in, j_out, sems)

            body(*ins, *outs, *scr)

            @pl.when(last)
            def _():
                job.finish(j_in, j_out, sems)

        return hosted

    return aliases, wrap


EPILOGUE_COLS = 768


def _plain(accs, extras):
    return accs


def _matmul(name, pairs, acc_of, M, N, K, tm, tn, tk, extras, outs, epilogue, job=None, rows_outer=False):
    assert M % tm == 0 and N % tn == 0 and K % tk == 0, (name, M, N, K, tm, tn, tk)
    nj, ni, nk = N // tn, M // tm, K // tk
    n_pairs, n_extra, n_out = len(pairs), len(extras), len(outs)
    n_acc = max(acc_of) + 1
    in_specs, args, dims = [], [], []
    lhs_of, seen = [], {}
    for a, am, b, bm in pairs:
        if (id(a), am) not in seen:
            seen[id(a), am] = len(args)
            args.append(a)
            if am == 'n':
                in_specs.append(pl.BlockSpec((tm, tk), lambda j, i, k: (i, k)))
            else:
                in_specs.append(pl.BlockSpec((tk, tm), lambda j, i, k: (k, i)))
        lhs_of.append(seen[id(a), am])
    n_lhs = len(args)
    for a, am, b, bm in pairs:
        if bm == 'n':
            in_specs.append(_wspec(b, tk, tn, _gk, _gj))
        else:
            in_specs.append(_wspec(b, tn, tk, _gj, _gk))
        dims.append((((1 if am == 'n' else 0,), (0 if bm == 'n' else 1,)), ((), ())))
        args.append(b)
    for e, kind, off in extras:
        assert off % tn == 0, (name, off, tn)
        o = off // tn
        if kind == 'tile':
            in_specs.append(pl.BlockSpec((tm, tn), lambda j, i, k, o=o: (i, j + o)))
        else:
            in_specs.append(pl.BlockSpec((1, tn), lambda j, i, k, o=o: (0, j + o)))
        args.append(e)
    out_shape, out_specs = [], []
    for dt, nb in outs:
        if nb is None:
            out_shape.append(jax.ShapeDtypeStruct((M, N), dt))
            out_specs.append(pl.BlockSpec((tm, tn), lambda j, i, k: (i, j)))
        else:
            assert nb % tn == 0 and N % nb == 0
            per = nb // tn
            out_shape.append(jax.ShapeDtypeStruct((N // nb, M, nb), dt))
            out_specs.append(pl.BlockSpec((None, tm, tn), lambda j, i, k, per=per: (j // per, i, j % per)))

    def body(*refs):
        n_ab = n_lhs + n_pairs
        e_refs = refs[n_ab:n_ab + n_extra]
        o_refs = refs[n_ab + n_extra:n_ab + n_extra + n_out]
        acc_refs = refs[n_ab + n_extra + n_out:]

        def products(cols):
            sums = [None] * n_acc
            for p in range(n_pairs):
                b_ref = refs[n_lhs + p]
                b = b_ref[:, cols] if pairs[p][3] == 'n' else b_ref[cols, :]
                prod = lax.dot_general(refs[lhs_of[p]][...], b, dims[p], preferred_element_type=F32)
                a = acc_of[p]
                sums[a] = prod if sums[a] is None else sums[a] + prod
            return sums

        def finish(accs, cols):
            res = epilogue(accs, [e[:, cols] for e in e_refs])
            for o_ref, r in zip(o_refs, res):
                o_ref[:, cols] = r.astype(o_ref.dtype)

        whole = slice(0, tn)
        if nk == 1:
            step = tn if epilogue is _plain else EPILOGUE_COLS
            for c0 in range(0, tn, step):
                cols = slice(c0, min(c0 + step, tn))
                finish(products(cols), cols)
            return
        sums = products(whole)
        k = pl.program_id(2)
        targets = o_refs if sum_in_outs else acc_refs

        @pl.when(k == 0)
        def _():
            for a in range(n_acc):
                targets[a][...] = sums[a]

        @pl.when(k > 0)
        def _():
            for a in range(n_acc):
                targets[a][...] += sums[a]

        if not sum_in_outs:
            @pl.when(k == nk - 1)
            def _():
                finish([acc_refs[a][...] for a in range(n_acc)], whole)

    sum_in_outs = epilogue is _plain and nk > 1 and n_out == n_acc and all(dt == F32 for dt, _ in outs)
    scratch = [pltpu.VMEM((tm, tn), F32) for _ in range(n_acc)] if nk > 1 and not sum_in_outs else []
    grid = (nj, ni, nk)
    if rows_outer:
        grid = (ni, nj, nk)
        swap = lambda spec: pl.BlockSpec(spec.block_shape, lambda i, j, k, f=spec.index_map: f(j, i, k))
        in_specs, out_specs = [swap(s) for s in in_specs], [swap(s) for s in out_specs]
    aliases, wrap = _host(job, grid, in_specs, args, out_specs, out_shape, scratch)
    sem = ("parallel", "parallel", "arbitrary") if job is None else ("arbitrary",) * 3
    return pl.pallas_call(
        wrap(body), name=name, grid=grid, in_specs=in_specs, out_specs=out_specs, out_shape=out_shape,
        scratch_shapes=scratch, input_output_aliases=aliases, compiler_params=_params(sem),
    )(*args)


ROW_TILE = 512


def _row_spec(tr, d):
    return pl.BlockSpec((tr, d), lambda i: (i, 0))


def _vec_spec(d):
    return pl.BlockSpec((1, d), lambda i: (0, 0))


def _norm_mod(name, x, g, sc, sh, job=None):
    S, D = x.shape
    tr = _pick(S, ROW_TILE, 16)

    def body(x_ref, g_ref, sc_ref, sh_ref, h_ref):
        xv = x_ref[...]
        r = lax.rsqrt(jnp.mean(xv * xv, axis=-1, keepdims=True) + EPS)
        h_ref[...] = ((xv * r) * g_ref[...] * (1.0 + sc_ref[...]) + sh_ref[...]).astype(BF16)

    grid = (S // tr,)
    in_specs = [_row_spec(tr, D), _vec_spec(D), _vec_spec(D), _vec_spec(D)]
    args = [x, g, sc, sh]
    out_specs, out_shape, scratch = [_row_spec(tr, D)], [jax.ShapeDtypeStruct((S, D), BF16)], []
    aliases, wrap = _host(job, grid, in_specs, args, out_specs, out_shape, scratch)
    res = pl.pallas_call(
        wrap(body), name=name, grid=grid, in_specs=in_specs, out_specs=out_specs, out_shape=out_shape,
        scratch_shapes=scratch, input_output_aliases=aliases,
        compiler_params=_params(("parallel" if job is None else "arbitrary",)),
    )(*args)
    return res[0] if job is None else res


def _loss_head(x2, target, g, ff, gt):
    S, D = x2.shape
    tr = _pick(S, ROW_TILE, 16)

    def body(x_ref, t_ref, g_ref, ff_ref, gt_ref, dx_ref, dffb_ref, loss_ref, dgt_ref, dg_ref):
        i = pl.program_id(0)

        @pl.when(i == 0)
        def _():
            loss_ref[...] = jnp.zeros_like(loss_ref)
            dgt_ref[...] = jnp.zeros_like(dgt_ref)
            dg_ref[...] = jnp.zeros_like(dg_ref)

        xv = x_ref[...]
        gv = g_ref[...]
        r = lax.rsqrt(jnp.mean(xv * xv, axis=-1, keepdims=True) + EPS)
        xh = xv * r
        err = xh * gv - t_ref[...]
        loss_ref[...] += 0.5 * jnp.sum(jnp.mean(err * err, axis=-1, keepdims=True))
        dy = err * (1.0 / D)
        dg_ref[...] += jnp.sum(dy * xh, axis=0, keepdims=True)
        dxh = dy * gv
        dx = r * (dxh - xh * jnp.mean(dxh * xh, axis=-1, keepdims=True))
        dx_ref[...] = dx
        dffb_ref[...] = (dx * gt_ref[...]).astype(BF16)
        dgt_ref[...] += jnp.sum(dx * ff_ref[...], axis=0, keepdims=True)

    return pl.pallas_call(
        body, name="loss_head", grid=(S // tr,),
        in_specs=[_row_spec(tr, D), _row_spec(tr, D), _vec_spec(D), _row_spec(tr, D), _vec_spec(D)],
        out_specs=[_row_spec(tr, D), _row_spec(tr, D), pl.BlockSpec((1, 128), lambda i: (0, 0)),
                   _vec_spec(D), _vec_spec(D)],
        out_shape=[jax.ShapeDtypeStruct((S, D), F32), jax.ShapeDtypeStruct((S, D), BF16),
                   jax.ShapeDtypeStruct((1, 128), F32), jax.ShapeDtypeStruct((1, D), F32),
                   jax.ShapeDtypeStruct((1, D), F32)],
        compiler_params=_params(("arbitrary",)),
    )(x2, target, g, ff, gt)


def _norm_mod_bwd(name, dh, x, g, sc, dres, gated=None, job=None):
    S, D = x.shape
    tr = _pick(S, ROW_TILE, 16)
    n = S // tr
    with_gate = gated is not None

    def body(*refs):
        if with_gate:
            dh_ref, x_ref, g_ref, sc_ref, dres_ref, mo_ref, gt_ref, dx_ref, dsh_ref, dsc_ref, dg_ref, dmob_ref, dgt_ref = refs
        else:
            dh_ref, x_ref, g_ref, sc_ref, dres_ref, dx_ref, dsh_ref, dsc_ref, dg_ref = refs
        i = pl.program_id(0)

        @pl.when(i == 0)
        def _():
            dsh_ref[...] = jnp.zeros_like(dsh_ref)
            dsc_ref[...] = jnp.zeros_like(dsc_ref)
            if with_gate:
                dgt_ref[...] = jnp.zeros_like(dgt_ref)

        xv = x_ref[...]
        dhv = dh_ref[...]
        gv = g_ref[...]
        scale = 1.0 + sc_ref[...]
        r = lax.rsqrt(jnp.mean(xv * xv, axis=-1, keepdims=True) + EPS)
        xh = xv * r
        dsh_ref[...] += jnp.sum(dhv, axis=0, keepdims=True)
        dsc_ref[...] += jnp.sum(dhv * xh, axis=0, keepdims=True)
        dxh = dhv * (gv * scale)
        dx = dres_ref[...] + r * (dxh - xh * jnp.mean(dxh * xh, axis=-1, keepdims=True))
        dx_ref[...] = dx
        if with_gate:
            dmob_ref[...] = (dx * gt_ref[...]).astype(BF16)
            dgt_ref[...] += jnp.sum(dx * mo_ref[...], axis=0, keepdims=True)

        @pl.when(i == n - 1)
        def _():
            t = dsc_ref[...]
            dg_ref[...] = t * scale
            dsc_ref[...] = t * gv

    in_specs = [_row_spec(tr, D), _row_spec(tr, D), _vec_spec(D), _vec_spec(D), _row_spec(tr, D)]
    args = [dh, x, g, sc, dres]
    out_specs = [_row_spec(tr, D), _vec_spec(D), _vec_spec(D), _vec_spec(D)]
    out_shape = [jax.ShapeDtypeStruct((S, D), F32)] + [jax.ShapeDtypeStruct((1, D), F32)] * 3
    if with_gate:
        in_specs += [_row_spec(tr, D), _vec_spec(D)]
        args += list(gated)
        out_specs += [_row_spec(tr, D), _vec_spec(D)]
        out_shape += [jax.ShapeDtypeStruct((S, D), BF16), jax.ShapeDtypeStruct((1, D), F32)]
    scratch = []
    aliases, wrap = _host(job, (n,), in_specs, args, out_specs, out_shape, scratch)
    return pl.pallas_call(
        wrap(body), name=name, grid=(n,), in_specs=in_specs, out_specs=out_specs, out_shape=out_shape,
        scratch_shapes=scratch, input_output_aliases=aliases, compiler_params=_params(("arbitrary",)),
    )(*args)


CONV_ROWS = 256


def _col_spec(S, off_cols):
    o = off_cols // HEAD
    return pl.BlockSpec((S, HEAD), lambda c, o=o: (0, c + o))


def _conv_taps(u, u_prev, rid):
    s1 = jnp.where(rid >= 1, pltpu.roll(u, 1, 0), pltpu.roll(u_prev, 1, 0))
    s2 = jnp.where(rid >= 2, pltpu.roll(u, 2, 0), pltpu.roll(u_prev, 2, 0))
    return s1, s2


def _conv_fwd(proj, conv_w, d_conv, job=None):
    S = proj.shape[0]
    R = min(CONV_ROWS, S)
    nr = S // R

    def body(ab_ref, ac_ref, ax_ref, w_ref, ya_ref):
        w0, w1, w2 = w_ref[0:1, :], w_ref[1:2, :], w_ref[2:3, :]
        rid = lax.broadcasted_iota(jnp.int32, (R, HEAD), 0)

        def step(c, carry):
            rows = pl.ds(pl.multiple_of(c * R, R), R)
            prev = pl.ds(pl.multiple_of(jnp.maximum(c - 1, 0) * R, R), R)
            u = ac_ref[rows, :] * ax_ref[rows, :]
            u_prev = jnp.where(c > 0, ac_ref[prev, :] * ax_ref[prev, :], 0.0)
            s1, s2 = _conv_taps(u, u_prev, rid)
            y = w0 * s2 + w1 * s1 + w2 * u
            ya_ref[rows, :] = (ab_ref[rows, :] * y).astype(BF16)
            return carry

        lax.fori_loop(0, nr, step, 0)

    grid = (d_conv // HEAD,)
    in_specs = [_col_spec(S, 0), _col_spec(S, d_conv), _col_spec(S, 2 * d_conv), pl.BlockSpec((3, HEAD), lambda c: (0, c))]
    args = [proj, proj, proj, conv_w]
    out_specs, out_shape = [pl.BlockSpec((S, HEAD), lambda c: (0, c))], [jax.ShapeDtypeStruct((S, d_conv), BF16)]
    scratch = []
    aliases, wrap = _host(job, grid, in_specs, args, out_specs, out_shape, scratch)
    res = pl.pallas_call(
        wrap(body), name="conv_fwd", grid=grid, in_specs=in_specs, out_specs=out_specs, out_shape=out_shape,
        scratch_shapes=scratch, input_output_aliases=aliases,
        compiler_params=_params(("parallel" if job is None else "arbitrary",)),
    )(*args)
    return res[0] if job is None else res


def _conv_bwd(dya, proj, conv_w, d_conv, job=None):
    S = proj.shape[0]
    R = min(CONV_ROWS, S)
    nr = S // R

    def body(dya_ref, ab_ref, ac_ref, ax_ref, w_ref, dab_ref, dac_ref, dax_ref, dw_ref):
        w0, w1, w2 = w_ref[0:1, :], w_ref[1:2, :], w_ref[2:3, :]
        rid = lax.broadcasted_iota(jnp.int32, (R, HEAD), 0)

        def step(c, carry):
            dw0, dw1, dw2 = carry
            rows = pl.ds(pl.multiple_of(c * R, R), R)
            prev = pl.ds(pl.multiple_of(jnp.maximum(c - 1, 0) * R, R), R)
            nxt = pl.ds(pl.multiple_of(jnp.minimum(c + 1, nr - 1) * R, R), R)
            a_c, a_x, a_b = ac_ref[rows, :], ax_ref[rows, :], ab_ref[rows, :]
            u = a_c * a_x
            u_prev = jnp.where(c > 0, ac_ref[prev, :] * ax_ref[prev, :], 0.0)
            s1, s2 = _conv_taps(u, u_prev, rid)
            y = w0 * s2 + w1 * s1 + w2 * u
            dy = dya_ref[rows, :]
            dab_ref[rows, :] = (dy * y).astype(BF16)
            dyc = dy * a_b
            dyc_next = jnp.where(c < nr - 1, dya_ref[nxt, :] * ab_ref[nxt, :], 0.0)
            t1 = jnp.where(rid < R - 1, pltpu.roll(dyc, R - 1, 0), pltpu.roll(dyc_next, R - 1, 0))
            t2 = jnp.where(rid < R - 2, pltpu.roll(dyc, R - 2, 0), pltpu.roll(dyc_next, R - 2, 0))
            du = w2 * dyc + w1 * t1 + w0 * t2
            dac_ref[rows, :] = (du * a_x).astype(BF16)
            dax_ref[rows, :] = (du * a_c).astype(BF16)
            dw0 = dw0 + jnp.sum(dyc * s2, axis=0, keepdims=True)
            dw1 = dw1 + jnp.sum(dyc * s1, axis=0, keepdims=True)
            dw2 = dw2 + jnp.sum(dyc * u, axis=0, keepdims=True)
            return dw0, dw1, dw2

        z = jnp.zeros((1, HEAD), F32)
        dw0, dw1, dw2 = lax.fori_loop(0, nr, step, (z, z, z))
        dw_ref[0:1, :] = dw0
        dw_ref[1:2, :] = dw1
        dw_ref[2:3, :] = dw2

    col = pl.BlockSpec((S, HEAD), lambda c: (0, c))
    grid = (d_conv // HEAD,)
    in_specs = [col, _col_spec(S, 0), _col_spec(S, d_conv), _col_spec(S, 2 * d_conv),
                pl.BlockSpec((3, HEAD), lambda c: (0, c))]
    args = [dya, proj, proj, proj, conv_w]
    out_specs = [col, col, col, pl.BlockSpec((3, HEAD), lambda c: (0, c))]
    out_shape = [jax.ShapeDtypeStruct((S, d_conv), BF16)] * 3 + [jax.ShapeDtypeStruct((3, d_conv), F32)]
    scratch = []
    aliases, wrap = _host(job, grid, in_specs, args, out_specs, out_shape, scratch)
    return pl.pallas_call(
        wrap(body), name="conv_bwd", grid=grid, in_specs=in_specs, out_specs=out_specs, out_shape=out_shape,
        scratch_shapes=scratch, input_output_aliases=aliases,
        compiler_params=_params(("parallel" if job is None else "arbitrary",)),
    )(*args)


HGRN_FWD_ROWS = 512
HGRN_BWD_ROWS = 1024
HGRN_FWD_SUB = 64
HGRN_BWD_SUB = 32
HGRN_GATE_ROWS = 128


def _block_cumsum(x, pos):
    for s in (1, 2, 4, 8):
        x = x + jnp.where(pos >= s, pltpu.roll(x, s, 0), 0.0)
    return x


def _block_rev_cumsum(x, pos, rows):
    for s in (1, 2, 4, 8):
        x = x + jnp.where(pos < BLK - s, pltpu.roll(x, rows - s, 0), 0.0)
    return x


def _block_last(x, pos, rows):
    m = jnp.where(pos == BLK - 1, x, 0.0)
    for s in (1, 2, 4, 8):
        m = m + pltpu.roll(m, rows - s, 0)
    return m


def _hgrn_gates(q, z, lb):
    sgq = _sig(q)
    qs = q * sgq
    sg = _sig(z)
    f = lb + (1.0 - lb) * sg
    kk = (1.0 - lb) * (1.0 - sg)
    return sgq, qs, sg, f, kk


def _hgrn_decays(q, z, lb, pos, rows):
    sgq, qs, sg, f, kk = _hgrn_gates(q, z, lb)
    b = _block_cumsum(jnp.log(f), pos)
    return sgq, qs, sg, f, kk, b, _block_last(b, pos, rows)


GROUP = HEAD // BLK


def _spread(x, blk):
    zero = jnp.zeros_like(x)
    return jnp.concatenate([jnp.where(blk == j, x, zero) for j in range(GROUP)], axis=1)


def _block_products(a_ref, m_ref, out_ref, nb, transposed, accumulate=False):
    rows = GROUP * BLK
    blk = lax.broadcasted_iota(jnp.int32, (rows, HEAD), 0) // BLK
    for g in range(nb // GROUP):
        r = pl.ds(g * rows, rows)
        a = _spread(a_ref[r, :], blk)
        if transposed:
            m = jnp.concatenate([m_ref[g * GROUP + j] for j in range(GROUP)], axis=1)
            dims = (((1,), (1,)), ((), ()))
        else:
            m = m_ref[pl.ds(g * GROUP, GROUP)].reshape(GROUP * HEAD, HEAD)
            dims = (((1,), (0,)), ((), ()))
        prod = lax.dot_general(a, m, dims, preferred_element_type=F32)
        out_ref[r, :] = out_ref[r, :] + prod if accumulate else prod


def _block_outer_products(a_ref, b_ref, u_ref, nb):
    rows = GROUP * BLK
    blk = lax.broadcasted_iota(jnp.int32, (rows, HEAD), 0) // BLK
    zero = jnp.zeros((rows, HEAD), BF16)
    for g in range(nb // GROUP):
        r = pl.ds(g * rows, rows)
        b = b_ref[r, :]
        spread = jnp.concatenate([jnp.where(blk == j, b, zero) for j in range(GROUP)], axis=1)
        u = lax.dot_general(a_ref[r, :], spread, (((0,), (0,)), ((), ())), preferred_element_type=F32)
        for j in range(GROUP):
            u_ref[g * GROUP + j] = u[:, j * HEAD:(j + 1) * HEAD]


def _hgrn_specs(T, d_conv, n_t, rev):
    def t_of(i):
        return (n_t - 1 - i) if rev else i

    def pcol(off):
        o = off // HEAD
        return pl.BlockSpec((T, HEAD), lambda h, i, o=o: (t_of(i), h + o))

    q_spec, z_spec, v_spec, g_spec = pcol(3 * d_conv), pcol(4 * d_conv), pcol(5 * d_conv), pcol(6 * d_conv)
    lb_spec = pl.BlockSpec((1, HEAD), lambda h, i: (0, h))
    gn_spec = pl.BlockSpec((1, HEAD), lambda h, i: (0, 0))
    act_spec = pl.BlockSpec((T, HEAD), lambda h, i: (t_of(i), h))
    st_spec = pl.BlockSpec((None, T // BLK, HEAD, HEAD), lambda h, i: (h, t_of(i), 0, 0))
    return q_spec, z_spec, v_spec, g_spec, lb_spec, gn_spec, act_spec, st_spec


def _hgrn_fwd(proj, lb, gn, d_conv, job=None):
    S = proj.shape[0]
    H = d_conv // HEAD
    T = min(HGRN_FWD_ROWS, S)
    n_t, nb = S // T, T // BLK
    sub = min(HGRN_FWD_SUB, T)
    q_spec, z_spec, v_spec, g_spec, lb_spec, gn_spec, act_spec, st_spec = _hgrn_specs(T, d_conv, n_t, False)

    def body(q_ref, z_ref, v_ref, g_ref, lb_ref, gn_ref, ob_ref, o_ref, st_ref, qe_s, ke_s, dec_s, state, v_s, o_s, u_s):
        @pl.when(pl.program_id(1) == 0)
        def _():
            state[...] = jnp.zeros_like(state)

        pos = lax.broadcasted_iota(jnp.int32, (sub, HEAD), 0) % BLK
        lbv = lb_ref[...]

        def vector_pass(s, carry):
            r = pl.ds(pl.multiple_of(s * sub, sub), sub)
            v = v_ref[r, :]
            _, qs, _, _, kk, b, b_tot = _hgrn_decays(q_ref[r, :], z_ref[r, :], lbv, pos, sub)
            qe_s[r, :] = (qs * jnp.exp(b)).astype(BF16)
            ke_s[r, :] = (kk * jnp.exp(b_tot - b)).astype(BF16)
            v_s[r, :] = v.astype(BF16)
            dec_s[r, :] = jnp.exp(b_tot)
            o = jnp.sum(qs * kk, axis=-1, keepdims=True) * v
            for d in range(1, BLK):
                e = jnp.exp(b - pltpu.roll(b, d, 0))
                a = jnp.sum(qs * pltpu.roll(kk, d, 0) * e, axis=-1, keepdims=True)
                o = o + jnp.where(pos >= d, a * pltpu.roll(v, d, 0), 0.0)
            o_s[r, :] = o
            return carry

        lax.fori_loop(0, T // sub, vector_pass, 0)

        _block_outer_products(v_s, ke_s, u_s, nb)
        st = state[...]
        for j in range(nb):
            st_ref[j] = st.astype(BF16)
            st = st * dec_s[pl.ds(j * BLK, 1), :] + u_s[j]
        state[...] = st
        _block_products(qe_s, st_ref, o_s, nb, True, accumulate=True)
        o = o_s[...]
        o_ref[...] = o
        r = lax.rsqrt(jnp.mean(o * o, axis=-1, keepdims=True) + EPS)
        g = g_ref[...]
        ob_ref[...] = ((o * r) * gn_ref[...] * (g * _sig(g))).astype(BF16)

    grid = (H, n_t)
    in_specs = [q_spec, z_spec, v_spec, g_spec, lb_spec, gn_spec]
    args = [proj, proj, proj, proj, lb, gn]
    out_specs = [act_spec, act_spec, st_spec, act_spec, act_spec, act_spec]
    out_shape = [jax.ShapeDtypeStruct((S, d_conv), BF16), jax.ShapeDtypeStruct((S, d_conv), F32),
                 jax.ShapeDtypeStruct((H, S // BLK, HEAD, HEAD), BF16),
                 jax.ShapeDtypeStruct((S, d_conv), BF16), jax.ShapeDtypeStruct((S, d_conv), BF16),
                 jax.ShapeDtypeStruct((S, d_conv), F32)]
    scratch = [pltpu.VMEM((HEAD, HEAD), F32), pltpu.VMEM((T, HEAD), BF16), pltpu.VMEM((T, HEAD), F32),
               pltpu.VMEM((nb, HEAD, HEAD), F32)]
    aliases, wrap = _host(job, grid, in_specs, args, out_specs, out_shape, scratch)
    return pl.pallas_call(
        wrap(body), name="hgrn_fwd", grid=grid, in_specs=in_specs, out_specs=out_specs, out_shape=out_shape,
        scratch_shapes=scratch, input_output_aliases=aliases,
        compiler_params=_params(("parallel" if job is None else "arbitrary", "arbitrary")),
    )(*args)


def _hgrn_bwd(dob, o_raw, states, qe, ke, dec, proj, lb, gn, d_conv, job=None):
    S = proj.shape[0]
    H = d_conv // HEAD
    T = min(HGRN_BWD_ROWS, S)
    n_t, nb = S // T, T // BLK
    sub = min(HGRN_BWD_SUB, T)
    gate_rows = min(HGRN_GATE_ROWS, T)
    q_spec, z_spec, v_spec, g_spec, lb_spec, gn_spec, act_spec, st_spec = _hgrn_specs(T, d_conv, n_t, True)

    def body(dob_ref, o_ref, st_ref, qe_b, ke_b, dec_s, q_ref, z_ref, v_ref, g_ref, lb_ref, gn_ref,
             dq_ref, dz_ref, dv_ref, dg_ref, dlb_ref, dgn_ref,
             dstate, do_b, v_b, dqe_s, dke_s, dvi_s, ddec_s, do_s, u_s, ds_s):
        @pl.when(pl.program_id(1) == 0)
        def _():
            dstate[...] = jnp.zeros_like(dstate)
            dlb_ref[...] = jnp.zeros_like(dlb_ref)
            dgn_ref[...] = jnp.zeros_like(dgn_ref)

        pos = lax.broadcasted_iota(jnp.int32, (sub, HEAD), 0) % BLK
        lbv, gnv = lb_ref[...], gn_ref[...]

        def before(s, carry):
            r = pl.ds(pl.multiple_of(s * gate_rows, gate_rows), gate_rows)
            g = g_ref[r, :]
            v_b[r, :] = v_ref[r, :].astype(BF16)
            o = o_ref[r, :]
            rs = lax.rsqrt(jnp.mean(o * o, axis=-1, keepdims=True) + EPS)
            on = o * rs
            sgg = _sig(g)
            dobv = dob_ref[r, :]
            dog = dobv * (g * sgg)
            dgn_ref[...] += jnp.sum(dog * on, axis=0, keepdims=True)
            don = dog * gnv
            do = rs * (don - on * jnp.mean(don * on, axis=-1, keepdims=True))
            dg_ref[r, :] = (dobv * (on * gnv) * (sgg * (1.0 + g * (1.0 - sgg)))).astype(BF16)
            do_s[r, :] = do
            do_b[r, :] = do.astype(BF16)
            return carry

        lax.fori_loop(0, T // gate_rows, before, 0)

        _block_outer_products(do_b, qe_b, u_s, nb)
        _block_products(do_b, st_ref, dqe_s, nb, False)
        dst = dstate[...]
        for j in reversed(range(nb)):
            ds_s[j] = dst.astype(BF16)
            ddec = jnp.sum(st_ref[j].astype(F32) * dst, axis=0, keepdims=True)
            ddec_s[pl.ds(j * BLK, BLK), :] = jnp.broadcast_to(ddec, (BLK, HEAD))
            dst = dst * dec_s[pl.ds(j * BLK, 1), :] + u_s[j]
        dstate[...] = dst
        _block_products(v_b, ds_s, dke_s, nb, False)
        _block_products(ke_b, ds_s, dvi_s, nb, True)

        def after(s, carry):
            r = pl.ds(pl.multiple_of(s * sub, sub), sub)
            q, v = q_ref[r, :], v_ref[r, :]
            sgq, qs, sg, f, kk, b, b_tot = _hgrn_decays(q, z_ref[r, :], lbv, pos, sub)
            do = do_s[r, :]
            dqs = dqe_s[r, :] * jnp.exp(b)
            dkk = dke_s[r, :] * jnp.exp(b_tot - b)
            d_tot = jnp.where(pos == BLK - 1, _block_cumsum(dkk * kk, pos) + ddec_s[r, :] * jnp.exp(b_tot), 0.0)
            dv = dvi_s[r, :]
            da = jnp.sum(do * v, axis=-1, keepdims=True)
            dv = dv + jnp.sum(qs * kk, axis=-1, keepdims=True) * do
            dqs = dqs + da * kk
            dkk = dkk + da * qs
            for d in range(1, BLK):
                kd = pltpu.roll(kk, d, 0)
                e = jnp.where(pos >= d, jnp.exp(b - pltpu.roll(b, d, 0)), 0.0)
                a = jnp.sum(qs * kd * e, axis=-1, keepdims=True)
                da = jnp.sum(do * pltpu.roll(v, d, 0), axis=-1, keepdims=True)
                gq = da * e
                dqs = dqs + gq * kd
                dkk = dkk + pltpu.roll(gq * qs, sub - d, 0)
                dv = dv + pltpu.roll(a * do, sub - d, 0)
            db = qs * dqs - kk * dkk + d_tot
            dlf = _block_rev_cumsum(db, pos, sub)
            df = dlf / f - dkk
            dlb_ref[...] += jnp.sum(df * (1.0 - sg), axis=0, keepdims=True)
            dz_ref[r, :] = (df * (1.0 - lbv) * sg * (1.0 - sg)).astype(BF16)
            dq_ref[r, :] = (dqs * (sgq * (1.0 + q * (1.0 - sgq)))).astype(BF16)
            dv_ref[r, :] = dv.astype(BF16)
            return carry

        lax.fori_loop(0, T // sub, after, 0)

    tb = lambda dt: pltpu.VMEM((T, HEAD), dt)
    grid = (H, n_t)
    in_specs = [act_spec, act_spec, st_spec, act_spec, act_spec, act_spec, q_spec, z_spec, v_spec, g_spec, lb_spec,
                gn_spec]
    args = [dob, o_raw, states, qe, ke, dec, proj, proj, proj, proj, lb, gn]
    out_specs = [act_spec, act_spec, act_spec, act_spec, lb_spec, pl.BlockSpec((None, 1, HEAD), lambda h, i: (h, 0, 0))]
    out_shape = ([jax.ShapeDtypeStruct((S, d_conv), BF16)] * 4
                 + [jax.ShapeDtypeStruct((1, d_conv), F32), jax.ShapeDtypeStruct((H, 1, HEAD), F32)])
    scratch = [pltpu.VMEM((HEAD, HEAD), F32), tb(BF16), tb(BF16), tb(F32), tb(F32), tb(F32), tb(F32), tb(F32),
               pltpu.VMEM((nb, HEAD, HEAD), F32), pltpu.VMEM((nb, HEAD, HEAD), BF16)]
    aliases, wrap = _host(job, grid, in_specs, args, out_specs, out_shape, scratch)
    return pl.pallas_call(
        wrap(body), name="hgrn_bwd", grid=grid, in_specs=in_specs, out_specs=out_specs, out_shape=out_shape,
        scratch_shapes=scratch, input_output_aliases=aliases,
        compiler_params=_params(("parallel" if job is None else "arbitrary", "arbitrary")),
    )(*args)


def _local_step(x, target, mod, norm_mix_g, lb, gnorm_g, norm_ffn_g, norm_final_g, conv_w, bufs, c_idx, order):
    S, D = x.shape
    d_conv = D // 2
    d_in = 7 * d_conv + 2 * D
    d_ff = N_CHIPS * bufs["w_ffn_down"].shape[1]
    nb_in, nb_co, nb_ff = bufs["w_in"].shape[2], bufs["w_conv_out"].shape[2], bufs["w_ffn_gate"].shape[2]
    rows = lambda g: g.reshape(-1, g.shape[2])
    sh_m, sc_m, gt_m, sh_f, sc_f, gt_f = [mod[:, i * D:(i + 1) * D] for i in range(6)]
    tm = _pick(S, 512, 16)
    tm2 = _pick(S, 1024, 16)
    tn_in = _pick(nb_in, 1408, 128)
    tn_co = _pick(nb_co, 512, 128)
    tn_ff = _pick(nb_ff, 1408, 128)
    tn_d = _pick(D, 512, 128)
    tw = _pick(D, 1024, 128)
    tg = _pick(D, 512, 128)

    h = _norm_mod("norm_mix", x, norm_mix_g, sc_m, sh_m)
    proj, w_in = _proj_gather(h, bufs["w_in"], order)
    ob, o_raw, states, qe, ke, dec, *got = _hgrn_fwd(
        proj, lb, gnorm_g, d_conv,
        job=_GatherJob([bufs[k] for k in ("w_conv_out", "w_hgrn_out", "w_o", "w_ffn_gate")]))
    ya, w_conv_out, w_hgrn_out, w_o, w_ffn_gate, w_ffn_up = _conv_fwd(
        proj, conv_w, d_conv, job=_Jobs([_ForwardJob(got), _GatherJob([bufs["w_ffn_up"]], 0, 2)]))
    w_o = rows(w_o)

    def merge_epi(accs, ex):
        y_a, y_b = accs
        return [y_a, y_b, _sig(ex[0]) * y_a + _sig(ex[1]) * y_b]

    y_a, y_b, merged, w_ffn_up = _matmul(
        "branch_out", [(ya, 'n', w_conv_out, 'n'), (ob, 'n', w_hgrn_out, 'n')], [0, 1], S, D, d_conv, tm2, tn_co, d_conv,
        [(proj, 'tile', 7 * d_conv), (proj, 'tile', 7 * d_conv + D)], [(BF16, None), (BF16, None), (BF16, None)], merge_epi,
        job=_GatherJob([w_ffn_up], 2, 5), rows_outer=True)

    def resid_epi(accs, ex):
        return [accs[0], ex[0] + ex[1] * accs[0]]

    mo, x1, w_ffn_up = _matmul("w_o", [(merged, 'n', w_o, 'n')], [0], S, D, D, tm2, tw, D,
                               [(x, 'tile', 0), (gt_m, 'row', 0)], [(BF16, None), (F32, None)], resid_epi,
                               job=_GatherJob([w_ffn_up], 5, 8))
    h2, w_ffn_up = _norm_mod("norm_ffn", x1, norm_ffn_g, sc_f, sh_f, job=_ForwardJob([w_ffn_up]))

    def swiglu_epi(accs, ex):
        gg, uu = accs
        return [gg, uu, gg * _sig(gg) * uu]

    G, U, act, w_ffn_down = _matmul(
        "ffn_in", [(h2, 'n', w_ffn_gate, 'n'), (h2, 'n', w_ffn_up, 'n')], [0, 1], S, d_ff, D,
        tm2, tn_ff, D, [], [(BF16, None), (BF16, None), (BF16, None)], swiglu_epi, job=_GatherJob([bufs["w_ffn_down"]]))
    w_ffn_down = rows(_forward_halves("forward_down", [w_ffn_down])[0])
    ff, x2 = _matmul("ffn_out", [(act, 'n', w_ffn_down, 'n')], [0], S, D, d_ff, tm2, tn_d, d_ff,
                     [(x1, 'tile', 0), (gt_f, 'row', 0)], [(BF16, None), (F32, None)], resid_epi, rows_outer=True)

    dx2, dffb, loss, dgt_f, dg_final = _loss_head(x2, target, norm_final_g, ff, gt_f)

    def swiglu_bwd_epi(accs, ex):
        dact, gg, uu = accs[0], ex[0], ex[1]
        s = _sig(gg)
        return [dact * uu * (s * (1.0 + gg * (1.0 - s))), dact * (gg * s)]

    dG, dU = _matmul("d_act", [(dffb, 'n', w_ffn_down, 't')], [0], S, d_ff, D, tm2, tn_ff, D,
                     [(G, 'tile', 0), (U, 'tile', 0)], [(BF16, None), (BF16, None)], swiglu_bwd_epi)
    dw_down, = _matmul("dw_ffn_down", [(act, 't', dffb, 'n')], [0], d_ff, D, S, _pick(d_ff, 512, 128),
                       D, S, [], [(BF16, None)], _plain)
    dh2, = _matmul("d_h2", [(dG, 'n', w_ffn_gate, 't'), (dU, 'n', w_ffn_up, 't')], [0, 0], S, D, d_ff,
                   tm, D, tn_ff, [], [(F32, None)], _plain)
    dw_gate, = _matmul("dw_ffn_gate", [(h2, 't', dG, 'n')], [0], D, d_ff, S, tw, tn_ff, S, [], [(BF16, nb_ff)], _plain)
    dw_up, = _matmul("dw_ffn_up", [(h2, 't', dU, 'n')], [0], D, d_ff, S, tw, tn_ff, S, [], [(BF16, nb_ff)], _plain)

    ck_idx = jnp.stack([c_idx[0], order[0]])

    def pair_sums(names, grads, from_sibling):
        pairs = [_pair_sum("pair_sum_" + k, g, q, ck_idx) for k, g, q in zip(names, grads, from_sibling)]
        return [p[0] for p in pairs], [p[1] for p in pairs]

    ffn_names = ["w_ffn_down", "w_ffn_gate", "w_ffn_up"]
    ffn_grads = [dw_down.reshape(N_CHIPS, -1, D), dw_gate, dw_up]
    dx1, dsh_f, dsc_f, dg_ffn, dmob, dgt_m, *from_sibling = _norm_mod_bwd(
        "norm_ffn_bwd", dh2, x1, norm_ffn_g, sc_f, dx2, (mo, gt_m), job=_SwapJob(ffn_grads))
    parts_f, slots_f = pair_sums(ffn_names, ffn_grads, from_sibling)

    def merge_bwd_epi(accs, ex):
        dm, ga, gb, ya_, yb_ = accs[0], ex[0], ex[1], ex[2], ex[3]
        sa, sb = _sig(ga), _sig(gb)
        return [dm * ya_ * sa * (1.0 - sa), dm * yb_ * sb * (1.0 - sb), dm * sa, dm * sb]

    dga, dgb, dy_a, dy_b = _matmul(
        "d_merged", [(dmob, 'n', w_o, 't')], [0], S, D, D, tm2, tn_co, D,
        [(proj, 'tile', 7 * d_conv), (proj, 'tile', 7 * d_conv + D), (y_a, 'tile', 0), (y_b, 'tile', 0)],
        [(BF16, None)] * 4, merge_bwd_epi, rows_outer=True)
    dw_o, = _matmul("dw_o", [(merged, 't', dmob, 'n')], [0], D, D, S, tg, D, S, [], [(BF16, None)], _plain)
    dya, dob = _matmul("d_branch", [(dy_a, 'n', w_conv_out, 't'), (dy_b, 'n', w_hgrn_out, 't')], [0, 1], S, d_conv, D,
                       tm2, _pick(d_conv, 1024, 128), tn_co, [], [(F32, None), (F32, None)], _plain)
    dw_co, dw_ho = _matmul("dw_branch", [(ya, 't', dy_a, 'n'), (ob, 't', dy_b, 'n')], [0, 1], d_conv, D, S,
                           _pick(d_conv, 512, 128), tn_co, S, [], [(BF16, nb_co), (BF16, nb_co)], _plain)
    branch_names = ["w_conv_out", "w_hgrn_out", "w_o"]
    branch_grads = [dw_co, dw_ho, dw_o.reshape(N_CHIPS, -1, D)]
    dab, dac, dax, dconv_w, *from_sibling = _conv_bwd(dya, proj, conv_w, d_conv, job=_SwapJob(branch_grads))
    parts_b, slots_b = pair_sums(branch_names, branch_grads, from_sibling)
    dq, dz, dv, dgo, dlb, dgn, *arrived_f = _hgrn_bwd(dob, o_raw, states, qe, ke, dec, proj, lb, gnorm_g, d_conv,
                                                      job=_ScatterJob(parts_f, slots_f))
    dproj = jnp.concatenate([dab, dac, dax, dq, dz, dv, dgo, dga, dgb], axis=1)
    halves_f = [_sum_chips("sum_chips_" + k, r, c_idx) for k, r in zip(ffn_names, arrived_f)]
    dw_in, *moved = _matmul("dw_in", [(h, 't', dproj, 'n')], [0], D, d_in, S, tw, tn_in, S, [], [(BF16, nb_in)], _plain,
                            job=_Jobs([_ShareJob(halves_f), _ScatterJob(parts_b, slots_b)]))
    whole_f, arrived_b = moved[:len(ffn_names)], moved[len(ffn_names):]
    parts_i, slots_i = pair_sums(["w_in"], [dw_in], _swap_halves("swap_in", [dw_in]))
    dh, arrived_in = _matmul("d_h", [(dproj, 'n', w_in, 't')], [0], S, D, d_in, tm2, D, tn_in, [], [(F32, None)], _plain,
                             job=_ScatterJob(parts_i, slots_i))
    dx, dsh_m, dsc_m, dg_mix = _norm_mod_bwd("norm_mix_bwd", dh, x, norm_mix_g, sc_m, dx1)
    dmod = jnp.concatenate([dsh_m, dsc_m, dgt_m, dsh_f, dsc_f, dgt_f], axis=1)
    small = dict(dmod=dmod, dg_mix=dg_mix, dg_ffn=dg_ffn, dg_final=dg_final, dlb=dlb,
                 dgn=jnp.sum(dgn, axis=0), dconv_w=dconv_w)
    late_names = branch_names + ["w_in"]
    late = _share_halves([_sum_chips("sum_chips_" + k, r, c_idx) for k, r in zip(late_names, arrived_b + [arrived_in])])
    big = dict(zip(ffn_names + late_names, whole_f + list(late)))
    return loss, dx, small, big


def _adamw_math(w, g, m, v):
    m = ADAM_B1 * m + (1.0 - ADAM_B1) * g
    v = ADAM_B2 * v + (1.0 - ADAM_B2) * (g * g)
    m_hat = m / (1.0 - ADAM_B1 ** ADAM_STEP)
    v_hat = v / (1.0 - ADAM_B2 ** ADAM_STEP)
    delta = -ADAM_LR * (m_hat / (jnp.sqrt(v_hat) + ADAM_EPS) + ADAM_WD * w)
    return delta, m, v


def _adamw(name, w, g, m, v):
    R, C = w.shape
    tr = _pick(R, max(8, (256 * 1024) // C // 8 * 8), 8)
    spec = pl.BlockSpec((tr, C), lambda i: (i, 0))

    def body(w_ref, g_ref, m_ref, v_ref, go_ref, d_ref, mo_ref, vo_ref):
        gv = g_ref[...]
        d, m2, v2 = _adamw_math(w_ref[...], gv, m_ref[...], v_ref[...])
        go_ref[...] = gv
        d_ref[...] = d
        mo_ref[...] = m2
        vo_ref[...] = v2

    return pl.pallas_call(
        body, name=name, grid=(R // tr,), in_specs=[spec] * 4, out_specs=[spec] * 4,
        out_shape=[jax.ShapeDtypeStruct((R, C), F32)] * 4, compiler_params=_params(("parallel",)),
    )(w, g, m, v)


def _ada_update(c_act_t, dmod, w, m, v):
    R, C = w.shape
    B = dmod.shape[0]
    tr = _pick(R, 256, 8)
    tc = _pick(C, 1536, 128)
    spec = pl.BlockSpec((tr, tc), lambda i, j: (i, j))

    def body(ct_ref, dm_ref, w_ref, m_ref, v_ref, g_ref, d_ref, mo_ref, vo_ref):
        ct = ct_ref[...]
        dm = dm_ref[...]
        g = ct[:, 0:1] * dm[0:1, :]
        for b in range(1, B):
            g = g + ct[:, b:b + 1] * dm[b:b + 1, :]
        d, m2, v2 = _adamw_math(w_ref[...], g, m_ref[...], v_ref[...])
        g_ref[...] = g
        d_ref[...] = d
        mo_ref[...] = m2
        vo_ref[...] = v2

    return pl.pallas_call(
        body, name="ada_update", grid=(R // tr, C // tc),
        in_specs=[pl.BlockSpec((tr, B), lambda i, j: (i, 0)), pl.BlockSpec((B, tc), lambda i, j: (0, j)),
                  spec, spec, spec],
        out_specs=[spec] * 4, out_shape=[jax.ShapeDtypeStruct((R, C), F32)] * 4,
        compiler_params=_params(("parallel", "parallel")),
    )(c_act_t, dmod, w, m, v)


def _prep(c_all, lb_param):
    def body(c_ref, p_ref, ca_ref, cab_ref, lb_ref):
        cv = c_ref[...]
        ca = cv * _sig(cv)
        ca_ref[...] = ca
        cab_ref[...] = ca.astype(BF16)
        lb_ref[...] = _sig(p_ref[0:1, :] - p_ref[1:2, :])

    return pl.pallas_call(
        body, name="prep",
        out_shape=[jax.ShapeDtypeStruct(c_all.shape, F32), jax.ShapeDtypeStruct(c_all.shape, BF16),
                   jax.ShapeDtypeStruct((1, lb_param.shape[1]), F32)],
    )(c_all, lb_param)


def _small_reduce(parts):
    W = parts.shape[1]

    def body(p_ref, o_ref):
        acc = p_ref[0:1, :]
        for d in range(1, N_DEV):
            acc = acc + p_ref[d:d + 1, :]
        o_ref[...] = acc

    return pl.pallas_call(body, name="small_reduce", out_shape=jax.ShapeDtypeStruct((1, W), F32))(parts)


def _lb_grad(dlb, lb):
    def body(d_ref, lb_ref, o_ref):
        t = d_ref[...] * lb_ref[...] * (1.0 - lb_ref[...])
        o_ref[0:1, :] = t
        o_ref[1:2, :] = -t

    return pl.pallas_call(body, name="lb_grad", out_shape=jax.ShapeDtypeStruct((2, dlb.shape[1]), F32))(dlb, lb)


def _place():
    x, y, c = lax.axis_index("x"), lax.axis_index("y"), lax.axis_index("c")
    chips = [(1 - x, y), (x, 1 - y), (1 - x, 1 - y)]
    return x, y, c, chips


def _allgather_rows(name, v):
    m_per, n = v.shape

    def body(x_ref, out_ref, send_sems, recv_sems, local_sem):
        x, y, c, chips = _place()
        me, sibling = (x, y, c), (x, y, 1 - c)

        def rows(px, py, pc):
            return out_ref.at[pl.ds((4 * px + 2 * py + pc) * m_per, m_per), :]

        def copy(k, block, to, src=None):
            return pltpu.make_async_remote_copy(
                src_ref=rows(*block) if src is None else src, dst_ref=rows(*block),
                send_sem=send_sems.at[k], recv_sem=recv_sems.at[k], device_id=to, device_id_type=MESH)

        mine = pltpu.make_async_copy(x_ref, rows(*me), local_sem)
        mine.start()
        first = [copy(0, me, sibling, src=x_ref)]
        first += [copy(1 + j, me, (*chip, c), src=x_ref) for j, chip in enumerate(chips)]
        for cp in first:
            cp.start()
        passed = [copy(4 + j, (*chip, c), sibling) for j, chip in enumerate(chips)]
        for j, chip in enumerate(chips):
            copy(1 + j, (*chip, c), me).wait_recv()
            passed[j].start()
        copy(0, sibling, me).wait_recv()
        for j, chip in enumerate(chips):
            copy(4 + j, (*chip, 1 - c), me).wait_recv()
        for cp in first + passed:
            cp.wait_send()
        mine.wait()

    return pl.pallas_call(
        body, name=name, out_shape=jax.ShapeDtypeStruct((N_DEV * m_per, n), v.dtype),
        in_specs=[pl.BlockSpec(memory_space=pltpu.VMEM)], out_specs=pl.BlockSpec(memory_space=pltpu.VMEM),
        scratch_shapes=[pltpu.SemaphoreType.DMA((7,)), pltpu.SemaphoreType.DMA((7,)), pltpu.SemaphoreType.DMA],
    )(v)


def _cast_place(name, w, k_idx):
    R, C = w.shape
    tr = _pick(R, max(16, (512 * 1024) // C // 16 * 16), 16)

    def body(k_ref, w_ref, o_ref):
        o_ref[...] = w_ref[...].astype(BF16)

    return pl.pallas_call(
        body, name=name,
        grid_spec=pltpu.PrefetchScalarGridSpec(
            num_scalar_prefetch=1, grid=(R // tr,),
            in_specs=[pl.BlockSpec((tr, C), lambda i, k_ref: (i, 0))],
            out_specs=pl.BlockSpec((None, tr, C), lambda i, k_ref: (k_ref[0], i, 0))),
        out_shape=jax.ShapeDtypeStruct((N_CHIPS, R, C), BF16),
        compiler_params=_params(("parallel",)),
    )(k_idx, w)


def _chip_copies(src_of, dst_of, got_of, n, sems, receiving):
    x, y, c, chips = _place()
    k_me = 2 * x + y
    send_sems, recv_sems = sems
    out = []
    for a in range(n):
        for j, chip in enumerate(chips):
            k_chip = 2 * chip[0] + chip[1]
            if receiving:
                src = dst = got_of(a, k_chip, c)
                to = (x, y, c)
            else:
                src, dst, to = src_of(a, k_chip, k_me, c), dst_of(a, k_me, c), (*chip, c)
            out.append(pltpu.make_async_remote_copy(
                src_ref=src, dst_ref=dst, send_sem=send_sems.at[3 * a + j], recv_sem=recv_sems.at[3 * a + j],
                device_id=to, device_id_type=MESH))
    return out


class _ChipJob:
    def start(self, j_in, j_out, sems):
        for send in self.copies(j_in, j_out, sems, False):
            send.start()

    def finish(self, j_in, j_out, sems):
        for recv in self.copies(j_in, j_out, sems, True):
            recv.wait_recv()
        for send in self.copies(j_in, j_out, sems, False):
            send.wait_send()


class _GatherJob(_ChipJob):
    def __init__(self, bufs, lo=0, hi=8):
        n = len(bufs)
        self.inputs, self.n_alias = list(bufs), n
        self.sem_shapes = [pltpu.SemaphoreType.DMA((3 * n,)), pltpu.SemaphoreType.DMA((3 * n,))]
        self.half = [b.shape[1] // 2 for b in bufs]
        self.lo, self.hi = lo, hi

    def copies(self, j_in, j_out, sems, receiving):
        def half(a, k, c):
            eighth = self.half[a] // 8
            return j_out[a].at[k, pl.ds(c * self.half[a] + self.lo * eighth, (self.hi - self.lo) * eighth)]

        return _chip_copies(lambda a, k_chip, k_me, c: half(a, k_me, c), half, half, len(self.half), sems, receiving)


class _ScatterJob(_ChipJob):
    def __init__(self, parts, slots):
        n = len(parts)
        self.n = n
        self.inputs, self.n_alias = list(parts) + list(slots), n
        self.sem_shapes = [pltpu.SemaphoreType.DMA((3 * n,)), pltpu.SemaphoreType.DMA((3 * n,))]

    def copies(self, j_in, j_out, sems, receiving):
        return _chip_copies(lambda a, k_chip, k_me, c: j_in[a].at[k_chip], lambda a, k_me, c: j_out[a].at[k_me],
                            lambda a, k_chip, c: j_out[a].at[k_chip], self.n, sems, receiving)


class _SwapJob(_ChipJob):
    def __init__(self, grads):
        n = len(grads)
        self.n = n
        self.half = [g.shape[1] // 2 for g in grads]
        landing = [lax.empty((N_CHIPS, g.shape[1] // 2, g.shape[2]), g.dtype) for g in grads]
        self.inputs, self.n_alias = list(grads) + landing, n
        self.sem_shapes = [pltpu.SemaphoreType.DMA((n,)), pltpu.SemaphoreType.DMA((n,))]

    def copies(self, j_in, j_out, sems, receiving):
        x, y, c, _ = _place()
        out = []
        for a in range(self.n):
            if receiving:
                src, to = j_out[a], (x, y, c)
            else:
                src, to = j_in[a].at[:, pl.ds((1 - c) * self.half[a], self.half[a])], (x, y, 1 - c)
            out.append(pltpu.make_async_remote_copy(src_ref=src, dst_ref=j_out[a], send_sem=sems[0].at[a],
                                                    recv_sem=sems[1].at[a], device_id=to, device_id_type=MESH))
        return out


class _Jobs:
    def __init__(self, jobs):
        self.jobs = jobs
        split = [(j.inputs[:len(j.inputs) - j.n_alias], j.inputs[len(j.inputs) - j.n_alias:]) for j in jobs]
        self.inputs = [a for plain, _ in split for a in plain] + [a for _, aliased in split for a in aliased]
        self.n_alias = sum(j.n_alias for j in jobs)
        self.sem_shapes = [s for j in jobs for s in j.sem_shapes]

    def _each(self, j_in, j_out, sems):
        n_plain = len(j_in) - self.n_alias
        p0 = a0 = s0 = 0
        for j in self.jobs:
            n_p, n_s = len(j.inputs) - j.n_alias, len(j.sem_shapes)
            ins = list(j_in[p0:p0 + n_p]) + list(j_in[n_plain + a0:n_plain + a0 + j.n_alias])
            yield j, ins, j_out[a0:a0 + j.n_alias], sems[s0:s0 + n_s]
            p0, a0, s0 = p0 + n_p, a0 + j.n_alias, s0 + n_s

    def start(self, j_in, j_out, sems):
        for j, ins, outs, s in self._each(j_in, j_out, sems):
            j.start(ins, outs, s)

    def finish(self, j_in, j_out, sems):
        for j, ins, outs, s in self._each(j_in, j_out, sems):
            j.finish(ins, outs, s)


class _ShareJob(_ChipJob):
    def __init__(self, bufs):
        n = len(bufs)
        self.n = n
        self.inputs, self.n_alias = list(bufs), n
        self.sem_shapes = [pltpu.SemaphoreType.DMA((n,)), pltpu.SemaphoreType.DMA((n,))]

    def copies(self, j_in, j_out, sems, receiving):
        x, y, c, _ = _place()
        out = []
        for a in range(self.n):
            hc, to = (1 - c, (x, y, c)) if receiving else (c, (x, y, 1 - c))
            out.append(pltpu.make_async_remote_copy(
                src_ref=j_out[a].at[hc], dst_ref=j_out[a].at[hc], send_sem=sems[0].at[a], recv_sem=sems[1].at[a],
                device_id=to, device_id_type=MESH))
        return out


class _ForwardJob(_ChipJob):
    def __init__(self, bufs):
        n = len(bufs)
        self.n = n
        self.half = [b.shape[1] // 2 for b in bufs]
        self.inputs, self.n_alias = list(bufs), n
        self.sem_shapes = [pltpu.SemaphoreType.DMA((3 * n,)), pltpu.SemaphoreType.DMA((3 * n,))]

    def copies(self, j_in, j_out, sems, receiving):
        x, y, c, chips = _place()
        out = []
        for a in range(self.n):
            for q, chip in enumerate(chips):
                hc, to = (1 - c, (x, y, c)) if receiving else (c, (x, y, 1 - c))
                part = j_out[a].at[2 * chip[0] + chip[1], pl.ds(hc * self.half[a], self.half[a])]
                out.append(pltpu.make_async_remote_copy(
                    src_ref=part, dst_ref=part, send_sem=sems[0].at[3 * a + q], recv_sem=sems[1].at[3 * a + q],
                    device_id=to, device_id_type=MESH))
        return out


def _forward_halves(name, bufs):
    n = len(bufs)

    def body(*refs):
        out_refs = refs[n:2 * n]
        send_sems, recv_sems = refs[2 * n:]
        x, y, c, chips = _place()
        started, waits = [], []
        for a in range(n):
            rh = bufs[a].shape[1] // 2
            for j, chip in enumerate(chips):
                k_chip = 2 * chip[0] + chip[1]
                got = out_refs[a].at[k_chip, pl.ds(c * rh, rh)]
                cp = pltpu.make_async_remote_copy(src_ref=got, dst_ref=got, send_sem=send_sems.at[3 * a + j],
                                                  recv_sem=recv_sems.at[3 * a + j], device_id=(x, y, 1 - c),
                                                  device_id_type=MESH)
                cp.start()
                started.append(cp)
                other = out_refs[a].at[k_chip, pl.ds((1 - c) * rh, rh)]
                waits.append(pltpu.make_async_remote_copy(
                    src_ref=other, dst_ref=other, send_sem=send_sems.at[3 * a + j], recv_sem=recv_sems.at[3 * a + j],
                    device_id=(x, y, c), device_id_type=MESH))
        for cp in waits:
            cp.wait_recv()
        for cp in started:
            cp.wait_send()

    return pl.pallas_call(
        body, name=name, out_shape=[jax.ShapeDtypeStruct(b.shape, b.dtype) for b in bufs],
        in_specs=[_HBM] * n, out_specs=[_HBM] * n, input_output_aliases={a: a for a in range(n)},
        scratch_shapes=[pltpu.SemaphoreType.DMA((3 * n,)), pltpu.SemaphoreType.DMA((3 * n,))],
    )(*bufs)


def _proj_gather(h, buf, order):
    S, D = h.shape
    nb = buf.shape[2]
    tm = _pick(S, 1024, 16)
    tn = _pick(nb, 1408, 128)
    per = nb // tn
    nj, ni = N_CHIPS * per, S // tm
    rh = D // 2
    seq = [(0, t) for t in range(per)] + [(p, t) for t in range(per) for p in (1, 2)] + [(3, t) for t in range(per)]
    tile_chip = jnp.stack([order[p] for p, _ in seq])
    tile_of = jnp.array([t for _, t in seq], jnp.int32)

    def body(chip_ref, t_ref, h_ref, buf_in, proj_ref, buf_ref, wbuf, tile_sems, ici_send, ici_recv, d2d_send, d2d_recv):
        j, i = pl.program_id(0), pl.program_id(1)
        x, y, c, chips = _place()
        k_me = 2 * x + y

        def part(k, hc, t):
            return buf_ref.at[k, pl.ds(hc * rh, rh), pl.ds(t * tn, tn)]

        def ici(q, t, receiving):
            k_chip = 2 * chips[q][0] + chips[q][1]
            src, to = (part(k_chip, c, t), (x, y, c)) if receiving else (part(k_me, c, t), (*chips[q], c))
            return pltpu.make_async_remote_copy(src_ref=src, dst_ref=src, send_sem=ici_send.at[q * per + t],
                                                recv_sem=ici_recv.at[q * per + t], device_id=to, device_id_type=MESH)

        def d2d(q, t, receiving):
            k_chip = 2 * chips[q][0] + chips[q][1]
            src, to = (part(k_chip, 1 - c, t), (x, y, c)) if receiving else (part(k_chip, c, t), (x, y, 1 - c))
            return pltpu.make_async_remote_copy(src_ref=src, dst_ref=src, send_sem=d2d_send.at[q * per + t],
                                                recv_sem=d2d_recv.at[q * per + t], device_id=to, device_id_type=MESH)

        def tile_copy(n):
            col = pl.multiple_of(t_ref[n] * tn, 128)
            return pltpu.make_async_copy(buf_ref.at[chip_ref[n], :, pl.ds(col, tn)], wbuf.at[n % 2], tile_sems.at[n % 2])

        @pl.when(jnp.logical_and(j == 0, i == 0))
        def _():
            for t in range(per):
                ici(0, t, False).start()
                ici(1, t, False).start()
            tile_copy(0).start()

        @pl.when(i == 0)
        def _():
            tile_copy(j).wait()
            for n in range(per, nj):
                p, t = seq[n]

                @pl.when(j + 1 == n)
                def _():
                    ici(p - 1, t, True).wait_recv()
                    d2d(p - 1, t, False).start()
                    if (p, t) == (1, per - 1):
                        for q in range(2):
                            for tt in range(per):
                                ici(q, tt, False).wait_send()
                        for tt in range(per):
                            ici(2, tt, False).start()
                    d2d(p - 1, t, True).wait_recv()

            @pl.when(j + 1 < nj)
            def _():
                tile_copy(j + 1).start()

        proj_ref[...] = jnp.dot(h_ref[...], wbuf[j % 2], preferred_element_type=F32)

        @pl.when(jnp.logical_and(j == nj - 1, i == ni - 1))
        def _():
            for t in range(per):
                ici(2, t, False).wait_send()
                for q in range(3):
                    d2d(q, t, False).wait_send()

    n_sems = 3 * per
    return pl.pallas_call(
        body, name="proj",
        grid_spec=pltpu.PrefetchScalarGridSpec(
            num_scalar_prefetch=2, grid=(nj, ni),
            in_specs=[pl.BlockSpec((tm, D), lambda j, i, kc, kt: (i, 0)), _HBM],
            out_specs=[pl.BlockSpec((tm, tn), lambda j, i, kc, kt: (i, kc[j] * per + kt[j])), _HBM],
            scratch_shapes=[pltpu.VMEM((2, D, tn), BF16), pltpu.SemaphoreType.DMA((2,))]
            + [pltpu.SemaphoreType.DMA((n_sems,))] * 4),
        out_shape=[jax.ShapeDtypeStruct((S, N_CHIPS * nb), F32), jax.ShapeDtypeStruct(buf.shape, buf.dtype)],
        input_output_aliases={3: 1}, compiler_params=_params(("arbitrary", "arbitrary")),
    )(tile_chip, tile_of, h, buf)


def _swap_halves(name, grads):
    n = len(grads)

    def body(*refs):
        in_refs, out_refs = refs[:n], refs[n:2 * n]
        send_sems, recv_sems = refs[2 * n:]
        x, y, c, _ = _place()
        cps = []
        for a in range(n):
            rh = grads[a].shape[1] // 2
            cp = pltpu.make_async_remote_copy(
                src_ref=in_refs[a].at[:, pl.ds((1 - c) * rh, rh)], dst_ref=out_refs[a],
                send_sem=send_sems.at[a], recv_sem=recv_sems.at[a], device_id=(x, y, 1 - c), device_id_type=MESH)
            cp.start()
            cps.append(cp)
        for cp in cps:
            cp.wait()

    return pl.pallas_call(
        body, name=name,
        out_shape=[jax.ShapeDtypeStruct((N_CHIPS, g.shape[1] // 2, g.shape[2]), g.dtype) for g in grads],
        in_specs=[_HBM] * n, out_specs=[_HBM] * n,
        scratch_shapes=[pltpu.SemaphoreType.DMA((n,)), pltpu.SemaphoreType.DMA((n,))],
    )(*grads)


def _pair_sum(name, g, q, ck_idx):
    _, R, C = g.shape
    rh = R // 2
    tr = _pick(rh, max(16, (512 * 1024) // C // 16 * 16), 16)
    nh = rh // tr

    def body(ck_ref, g_ref, q_ref, o_ref, own_ref):
        s = (g_ref[...].astype(F32) + q_ref[...].astype(F32)).astype(BF16)
        o_ref[...] = s

        @pl.when(pl.program_id(1) == ck_ref[1])
        def _():
            own_ref[...] = s

    return pl.pallas_call(
        body, name=name,
        grid_spec=pltpu.PrefetchScalarGridSpec(
            num_scalar_prefetch=1, grid=(nh, N_CHIPS),
            in_specs=[pl.BlockSpec((None, tr, C), lambda i, k, ck: (k, ck[0] * nh + i, 0)),
                      pl.BlockSpec((None, tr, C), lambda i, k, ck: (k, i, 0))],
            out_specs=[pl.BlockSpec((None, tr, C), lambda i, k, ck: (k, i, 0)),
                       pl.BlockSpec((None, tr, C), lambda i, k, ck: (ck[1], i, 0))]),
        out_shape=[jax.ShapeDtypeStruct((N_CHIPS, rh, C), BF16)] * 2,
        compiler_params=_params(("parallel", "arbitrary")),
    )(ck_idx, g, q)


def _sum_chips(name, r, c_idx):
    _, rh, C = r.shape
    tr = _pick(rh, max(16, (256 * 1024) // C // 16 * 16), 16)

    def body(c_ref, r_ref, o_ref):
        acc = r_ref[0].astype(F32)
        for k in range(1, N_CHIPS):
            acc = acc + r_ref[k].astype(F32)
        o_ref[...] = acc

    return pl.pallas_call(
        body, name=name,
        grid_spec=pltpu.PrefetchScalarGridSpec(
            num_scalar_prefetch=1, grid=(rh // tr,),
            in_specs=[pl.BlockSpec((N_CHIPS, tr, C), lambda i, c_ref: (0, i, 0))],
            out_specs=pl.BlockSpec((None, tr, C), lambda i, c_ref: (c_ref[0], i, 0))),
        out_shape=jax.ShapeDtypeStruct((2, rh, C), F32), compiler_params=_params(("parallel",)),
    )(c_idx, r)


def _share_halves(bufs):
    n = len(bufs)

    def body(*refs):
        out_refs = refs[n:2 * n]
        send_sems, recv_sems = refs[2 * n:]
        x, y, c, _ = _place()
        cps = []
        for a in range(n):
            cp = pltpu.make_async_remote_copy(
                src_ref=out_refs[a].at[c], dst_ref=out_refs[a].at[c], send_sem=send_sems.at[a],
                recv_sem=recv_sems.at[a], device_id=(x, y, 1 - c), device_id_type=MESH)
            cp.start()
            cps.append(cp)
        for a, cp in enumerate(cps):
            got = out_refs[a].at[1 - c]
            pltpu.make_async_remote_copy(src_ref=got, dst_ref=got, send_sem=send_sems.at[a], recv_sem=recv_sems.at[a],
                                         device_id=(x, y, c), device_id_type=MESH).wait_recv()
        for cp in cps:
            cp.wait_send()

    return pl.pallas_call(
        body, name="share_halves",
        out_shape=[jax.ShapeDtypeStruct(b.shape, b.dtype) for b in bufs],
        in_specs=[_HBM] * n, out_specs=[_HBM] * n, input_output_aliases={a: a for a in range(n)},
        scratch_shapes=[pltpu.SemaphoreType.DMA((n,)), pltpu.SemaphoreType.DMA((n,))],
    )(*bufs)


_BIG = ("w_in", "w_conv_out", "w_hgrn_out", "w_o", "w_ffn_gate", "w_ffn_up", "w_ffn_down")
_WEIGHTS = ("w_ada", "b_ada", "norm_mix_g", "w_in", "conv_w", "lb_param", "gnorm_g", "w_conv_out", "w_hgrn_out",
            "w_o", "norm_ffn_g", "w_ffn_gate", "w_ffn_up", "w_ffn_down", "norm_final_g")


def _pack_rows(pieces):
    flat = jnp.concatenate([p.reshape(-1) for p in pieces])
    pad = (-flat.shape[0]) % (SUBLANES * LANES)
    return jnp.pad(flat, (0, pad)).reshape(SUBLANES, -1)


def kernel(x, c, w_ada, b_ada, norm_mix_g, w_in, conv_w, lb_param, gnorm_g, w_conv_out, w_hgrn_out, w_o, norm_ffn_g, w_ffn_gate, w_ffn_up, w_ffn_down, norm_final_g, loss_target, m_w_ada, m_b_ada, m_norm_mix_g, m_w_in, m_conv_w, m_lb_param, m_gnorm_g, m_w_conv_out, m_w_hgrn_out, m_w_o, m_norm_ffn_g, m_w_ffn_gate, m_w_ffn_up, m_w_ffn_down, m_norm_final_g, v_w_ada, v_b_ada, v_norm_mix_g, v_w_in, v_conv_w, v_lb_param, v_gnorm_g, v_w_conv_out, v_w_hgrn_out, v_w_o, v_norm_ffn_g, v_w_ffn_gate, v_w_ffn_up, v_w_ffn_down, v_norm_final_g):
    w = dict(w_ada=w_ada, b_ada=b_ada, norm_mix_g=norm_mix_g, w_in=w_in, conv_w=conv_w, lb_param=lb_param,
             gnorm_g=gnorm_g, w_conv_out=w_conv_out, w_hgrn_out=w_hgrn_out, w_o=w_o, norm_ffn_g=norm_ffn_g,
             w_ffn_gate=w_ffn_gate, w_ffn_up=w_ffn_up, w_ffn_down=w_ffn_down, norm_final_g=norm_final_g)
    m = dict(w_ada=m_w_ada, b_ada=m_b_ada, norm_mix_g=m_norm_mix_g, w_in=m_w_in, conv_w=m_conv_w, lb_param=m_lb_param,
             gnorm_g=m_gnorm_g, w_conv_out=m_w_conv_out, w_hgrn_out=m_w_hgrn_out, w_o=m_w_o, norm_ffn_g=m_norm_ffn_g,
             w_ffn_gate=m_w_ffn_gate, w_ffn_up=m_w_ffn_up, w_ffn_down=m_w_ffn_down, norm_final_g=m_norm_final_g)
    v = dict(w_ada=v_w_ada, b_ada=v_b_ada, norm_mix_g=v_norm_mix_g, w_in=v_w_in, conv_w=v_conv_w, lb_param=v_lb_param,
             gnorm_g=v_gnorm_g, w_conv_out=v_w_conv_out, w_hgrn_out=v_w_hgrn_out, w_o=v_w_o, norm_ffn_g=v_norm_ffn_g,
             w_ffn_gate=v_w_ffn_gate, w_ffn_up=v_w_ffn_up, w_ffn_down=v_w_ffn_down, norm_final_g=v_norm_final_g)
    two_d = lambda a: a.reshape((-1, a.shape[-1])) if a.ndim != 2 else a
    shapes = {k: a.shape for k, a in w.items()}
    w, m, v = ({k: two_d(a) for k, a in t.items()} for t in (w, m, v))
    w["lb_param"], m["lb_param"], v["lb_param"] = lb_param, m_lb_param, v_lb_param
    S, D = x.shape[1], x.shape[2]
    d_conv = D // 2
    ix, iy, ic = lax.axis_index("x"), lax.axis_index("y"), lax.axis_index("c")
    k_me = 2 * ix + iy
    dev = 2 * k_me + ic
    cw_shard = w["conv_w"].shape[1]
    ada_cols = w["w_ada"].shape[1]

    got = _allgather_rows("gather_cond", _pack_rows([c, w["conv_w"]])).reshape(N_DEV, -1)
    c_all = got[:, :D]
    conv_full = got[::2, D:D + 3 * cw_shard].reshape(N_CHIPS, 3, cw_shard).transpose(1, 0, 2).reshape(3, -1)
    c_act, c_act_b, lb = _prep(c_all, lb_param)
    b_cols = lax.dynamic_slice(w["b_ada"], (0, k_me * ada_cols), (1, ada_cols))
    mod_cols, = _matmul("ada_fwd", [(c_act_b, 'n', w["w_ada"].astype(BF16), 'n')], [0], N_DEV, ada_cols, D,
                        N_DEV, _pick(ada_cols, 1536, 128), D, [(b_cols, 'row', 0)], [(F32, None)],
                        lambda accs, ex: [accs[0] + ex[0]])
    mod_all = _allgather_rows("gather_mod", mod_cols).reshape(N_CHIPS, 2, N_DEV, ada_cols)[:, 0]
    mod_all = mod_all.transpose(1, 0, 2).reshape(N_DEV, -1)
    mod = lax.dynamic_slice(mod_all, (dev, 0), (1, mod_all.shape[1]))
    k_idx = jnp.reshape(k_me, (1,)).astype(jnp.int32)
    c_idx = jnp.reshape(ic, (1,)).astype(jnp.int32)
    bufs = {k: _cast_place("cast_" + k, w[k], k_idx) for k in _BIG}
    order = jnp.stack([k_me, 2 * (1 - ix) + iy, 2 * ix + (1 - iy), 2 * (1 - ix) + (1 - iy)]).astype(jnp.int32)

    loss, dx, small, arrived = _local_step(
        x[0], loss_target[0], mod, w["norm_mix_g"], lb, w["gnorm_g"], w["norm_ffn_g"], w["norm_final_g"], conv_full,
        bufs, c_idx, order)

    pieces = [small["dmod"], small["dg_mix"], small["dg_ffn"], small["dg_final"], small["dlb"], small["dgn"],
              small["dconv_w"], loss]
    sizes = [p.size for p in pieces]
    parts = _allgather_rows("gather_small", _pack_rows(pieces)).reshape(N_DEV, -1)
    total = _small_reduce(parts)
    offs = [0]
    for s in sizes:
        offs.append(offs[-1] + s)
    tot = [total[:, offs[i]:offs[i + 1]] for i in range(len(sizes))]
    dmod_all = parts[:, :sizes[0]]
    grads = {
        "b_ada": tot[0], "norm_mix_g": tot[1], "norm_ffn_g": tot[2], "norm_final_g": tot[3],
        "lb_param": _lb_grad(tot[4], lb), "gnorm_g": tot[5],
        "conv_w": lax.dynamic_slice(tot[6].reshape(3, -1), (0, k_me * cw_shard), (3, cw_shard)),
    }
    loss_out = tot[7][0, 0]

    for k in _BIG:
        grads[k] = arrived[k].reshape(-1, arrived[k].shape[-1])

    delta, new_m, new_v = {}, {}, {}
    dmod_cols = lax.dynamic_slice(dmod_all, (0, k_me * ada_cols), (N_DEV, ada_cols))
    grads["w_ada"], delta["w_ada"], new_m["w_ada"], new_v["w_ada"] = _ada_update(
        c_act.T, dmod_cols, w["w_ada"], m["w_ada"], v["w_ada"])
    for k in _WEIGHTS:
        if k != "w_ada":
            grads[k], delta[k], new_m[k], new_v[k] = _adamw("adamw_" + k, w[k], grads[k], m[k], v[k])
    outs = [loss_out, dx.reshape(x.shape)]
    for t in (grads, delta, new_m, new_v):
        outs += [t[k].reshape(shapes[k]) for k in _WEIGHTS]
    return tuple(outs)
```
